```python
import jax, jax.numpy as jnp
from jax import lax
import numpy as np

D_MODEL = 1024
BATCH = 16
SEQ = 2048
DEPTH = 1

HEAD_DIM = 64
N_HEADS_DIL = 8
N_HEADS_SB = 8
DIL_WIDTH = N_HEADS_DIL * HEAD_DIM
SB_WIDTH = N_HEADS_SB * HEAD_DIM
DIL_PATTERNS = ((128, 1), (512, 4), (2048, 16))
BLOCK = 128
ROPE_THETA = 500000.0
ROPE_DIM = HEAD_DIM // 4
N_MEM = 256
N_HEADS_MEM = 4
MEM_HEAD_DIM = 128
MEM_WIDTH = N_HEADS_MEM * MEM_HEAD_DIM
D_FF = -(-(8 * D_MODEL) // (3 * 256)) * 256
IN_COLS = 3 * DIL_WIDTH + 3 * SB_WIDTH + 2 * D_MODEL
RMS_EPS = 1e-6
MAX_POS_OFFSET = 1024

kernel_name = "hybrid_dilated_stickbreak_gated_block"


def _rmsnorm(x, g):
    xf = x.astype(jnp.float32)
    y = xf * lax.rsqrt(jnp.mean(xf * xf, axis=-1, keepdims=True) + RMS_EPS)
    return (y * g.astype(jnp.float32)).astype(x.dtype)


def _partial_rope(x, positions):
    half = ROPE_DIM // 2
    inv_freq = ROPE_THETA ** (-jnp.arange(half, dtype=jnp.float32) / half)
    ang = positions.astype(jnp.float32)[:, None, :, None] * inv_freq
    cos, sin = jnp.cos(ang), jnp.sin(ang)
    xf = x.astype(jnp.float32)
    x1, x2 = xf[..., :half], xf[..., half:ROPE_DIM]
    out = jnp.concatenate([x1 * cos - x2 * sin, x2 * cos + x1 * sin, xf[..., ROPE_DIM:]], axis=-1)
    return out.astype(x.dtype)


def _dilated_pattern(q, k, v, window, dilation):
    B, H, S, hd = q.shape
    L = S // dilation
    n_back = window // dilation
    C = BLOCK
    nb = -(-L // C)
    Lp = nb * C

    def to_stream(t):
        t = t.reshape(B, H, L, dilation, hd).transpose(0, 1, 3, 2, 4)
        t = jnp.pad(t, ((0, 0), (0, 0), (0, 0), (0, Lp - L), (0, 0)))
        return t.reshape(B, H, dilation, nb, C, hd)

    def with_prev(t):
        prev = jnp.pad(t, ((0, 0), (0, 0), (0, 0), (1, 0), (0, 0), (0, 0)))[:, :, :, :-1]
        return jnp.concatenate([prev, t], axis=4)

    qs = to_stream(q)
    kb = with_prev(to_stream(k))
    vb = with_prev(to_stream(v)).astype(jnp.float32)
    s = jnp.einsum('bhrnqd,bhrnkd->bhrnqk', qs, kb,
                   preferred_element_type=jnp.float32) * (hd ** -0.5)
    i = jnp.arange(C)[:, None]
    j = jnp.arange(2 * C)[None, :]
    dist = C + i - j
    band = (dist >= 0) & (dist <= n_back)
    in_cur = j >= C
    blk = jnp.arange(nb)[:, None, None]
    valid = band[None] & ((blk > 0) | in_cur[None])
    s = jnp.where(valid, s, -jnp.inf)
    m = jnp.max(s, axis=-1, keepdims=True)
    p = jnp.exp(s - m)
    den = jnp.sum(p, axis=-1)
    o = jnp.einsum('bhrnqk,bhrnkd->bhrnqd', p, vb) / den[..., None]
    lse = m[..., 0] + jnp.log(den)

    def from_stream(t):
        rest = t.shape[5:]
        t = t.reshape((B, H, dilation, Lp) + rest)[:, :, :, :L]
        t = jnp.moveaxis(t, 2, 3)
        return t.reshape((B, H, S) + rest)

    return from_stream(o), from_stream(lse)


def _dilated_attention(q, k, v):
    outs, lses = [], []
    for window, dilation in DIL_PATTERNS:
        o, lse = _dilated_pattern(q, k, v, window, dilation)
        outs.append(o)
        lses.append(lse)
    alpha = jax.nn.softmax(jnp.stack(lses, axis=0), axis=0)
    return jnp.einsum('gbhs,gbhsd->bhsd', alpha, jnp.stack(outs, axis=0))


def _stick_breaking(q, k, v):
    B, H, S, hd = q.shape
    nb = S // BLOCK
    qb = q.reshape(B, H, nb, BLOCK, hd).transpose(2, 0, 1, 3, 4)
    vf = v.astype(jnp.float32)
    kpos = jnp.arange(S)

    def block(args):
        qi, bi = args
        z = jnp.einsum('bhqd,bhkd->bhqk', qi, k,
                       preferred_element_type=jnp.float32) * (hd ** -0.5)
        qpos = bi * BLOCK + jnp.arange(BLOCK)
        strict = kpos[None, :] < qpos[:, None]
        log1m = jnp.where(strict, jax.nn.log_sigmoid(-z), 0.0)
        between = lax.cumsum(log1m, axis=3, reverse=True) - log1m
        a = jnp.where(strict, jnp.exp(jax.nn.log_sigmoid(z) + between), 0.0)
        return jnp.einsum('bhqk,bhkd->bhqd', a, vf)

    o = lax.map(block, (qb, jnp.arange(nb)))
    return o.transpose(1, 2, 0, 3, 4).reshape(B, H, S, hd)


def _memory_attention(hn, mem_n, w_q, w_kv, w_o):
    B, S, _ = hn.shape
    q = (hn @ w_q).reshape(B, S, N_HEADS_MEM, MEM_HEAD_DIM)
    kv = mem_n @ w_kv
    k = kv[..., :MEM_WIDTH].reshape(B, N_MEM, N_HEADS_MEM, MEM_HEAD_DIM)
    v = kv[..., MEM_WIDTH:].reshape(B, N_MEM, N_HEADS_MEM, MEM_HEAD_DIM)
    s = jnp.einsum('bshd,bmhd->bhsm', q, k,
                   preferred_element_type=jnp.float32) * (MEM_HEAD_DIM ** -0.5)
    p = jax.nn.softmax(s, axis=-1)
    o = jnp.einsum('bhsm,bmhd->bshd', p, v.astype(jnp.float32))
    return o.reshape(B, S, MEM_WIDTH).astype(hn.dtype) @ w_o


def _swiglu(n, w_gate, w_up, w_down):
    return (jax.nn.silu(n @ w_gate) * (n @ w_up)) @ w_down


def _fwd_setup_inputs(seed: int = 0) -> dict:
    key = jax.random.key(seed)
    ks = jax.random.split(key, 20)

    def w(k, shape):
        return jax.random.normal(k, shape, jnp.float32) * (shape[-2] ** -0.5)

    def gain(k, shape):
        return 1.0 + 0.01 * jax.random.normal(k, shape, jnp.float32)

    x = jax.random.normal(ks[0], (BATCH, SEQ, D_MODEL), jnp.float32)
    mem = jax.random.normal(ks[1], (BATCH, N_MEM, D_MODEL), jnp.float32)
    offs = jax.random.randint(ks[2], (BATCH, 1), 0, MAX_POS_OFFSET, dtype=jnp.int32)
    positions = offs + jnp.arange(SEQ, dtype=jnp.int32)[None, :]
    return {
        "x": x,
        "mem": mem,
        "positions": positions,
        "g_mix": gain(ks[3], (DEPTH, D_MODEL)),
        "w_in": w(ks[4], (DEPTH, D_MODEL, IN_COLS)),
        "w_up_a": w(ks[5], (DEPTH, DIL_WIDTH, D_MODEL)),
        "w_up_b": w(ks[6], (DEPTH, SB_WIDTH, D_MODEL)),
        "w_out": w(ks[7], (DEPTH, D_MODEL, D_MODEL)),
        "g_mem_q": gain(ks[8], (DEPTH, D_MODEL)),
        "g_mem_kv": gain(ks[9], (DEPTH, D_MODEL)),
        "w_q_mem": w(ks[10], (DEPTH, D_MODEL, MEM_WIDTH)),
        "w_kv_mem": w(ks[11], (DEPTH, D_MODEL, 2 * MEM_WIDTH)),
        "w_o_mem": w(ks[12], (DEPTH, MEM_WIDTH, D_MODEL)),
        "g_ffn": gain(ks[13], (DEPTH, D_MODEL)),
        "w_ffn_gate": w(ks[14], (DEPTH, D_MODEL, D_FF)),
        "w_ffn_up": w(ks[15], (DEPTH, D_MODEL, D_FF)),
        "w_ffn_down": w(ks[16], (DEPTH, D_FF, D_MODEL)),
        "g_final": gain(ks[17], (D_MODEL,)),
    }


def _fwd_reference(x, mem, positions, g_mix, w_in, w_up_a, w_up_b, w_out, g_mem_q, g_mem_kv,
              w_q_mem, w_kv_mem, w_o_mem, g_ffn, w_ffn_gate, w_ffn_up, w_ffn_down, g_final):
    B, S, _ = x.shape
    split_at = list(np.cumsum([DIL_WIDTH, DIL_WIDTH, DIL_WIDTH,
                               SB_WIDTH, SB_WIDTH, SB_WIDTH, D_MODEL]))

    def heads(t, nh):
        return t.reshape(B, S, nh, HEAD_DIM).transpose(0, 2, 1, 3)

    def merge(t):
        return t.transpose(0, 2, 1, 3).reshape(B, S, -1).astype(x.dtype)

    h = x
    for l in range(DEPTH):
        n = _rmsnorm(h, g_mix[l])
        proj = n @ w_in[l]
        qa, ka, va, qb, kb, vb, gate_a, gate_b = jnp.split(proj, split_at, axis=-1)
        qa = _partial_rope(heads(qa, N_HEADS_DIL), positions)
        ka = _partial_rope(heads(ka, N_HEADS_DIL), positions)
        o_a = merge(_dilated_attention(qa, ka, heads(va, N_HEADS_DIL)))
        o_b = merge(_stick_breaking(heads(qb, N_HEADS_SB), heads(kb, N_HEADS_SB),
                                    heads(vb, N_HEADS_SB)))
        mixed = (jax.nn.sigmoid(gate_a) * (o_a @ w_up_a[l])
                 + jax.nn.sigmoid(gate_b) * (o_b @ w_up_b[l]))
        h = h + mixed @ w_out[l]
        h = h + _memory_attention(_rmsnorm(h, g_mem_q[l]), _rmsnorm(mem, g_mem_kv[l]),
                                  w_q_mem[l], w_kv_mem[l], w_o_mem[l])
        h = h + _swiglu(_rmsnorm(h, g_ffn[l]), w_ffn_gate[l], w_ffn_up[l], w_ffn_down[l])
    return _rmsnorm(h, g_final)


import jax as _jax
import jax.numpy as _jnp

TWIN_FORMAT = 'train_step'
FWD_PARAMS = ['x', 'mem', 'positions', 'g_mix', 'w_in', 'w_up_a', 'w_up_b', 'w_out', 'g_mem_q', 'g_mem_kv', 'w_q_mem', 'w_kv_mem', 'w_o_mem', 'g_ffn', 'w_ffn_gate', 'w_ffn_up', 'w_ffn_down', 'g_final']
TWIN_WEIGHTS = ['g_mix', 'w_in', 'w_up_a', 'w_up_b', 'w_out', 'g_mem_q', 'g_mem_kv', 'w_q_mem', 'w_kv_mem', 'w_o_mem', 'g_ffn', 'w_ffn_gate', 'w_ffn_up', 'w_ffn_down', 'g_final']
TWIN_DIFF_INPUT = 'x'
TWIN_INPUTS = ['x', 'mem', 'positions', 'g_mix', 'w_in', 'w_up_a', 'w_up_b', 'w_out', 'g_mem_q', 'g_mem_kv', 'w_q_mem', 'w_kv_mem', 'w_o_mem', 'g_ffn', 'w_ffn_gate', 'w_ffn_up', 'w_ffn_down', 'g_final', 'loss_target', 'm_g_mix', 'm_w_in', 'm_w_up_a', 'm_w_up_b', 'm_w_out', 'm_g_mem_q', 'm_g_mem_kv', 'm_w_q_mem', 'm_w_kv_mem', 'm_w_o_mem', 'm_g_ffn', 'm_w_ffn_gate', 'm_w_ffn_up', 'm_w_ffn_down', 'm_g_final', 'v_g_mix', 'v_w_in', 'v_w_up_a', 'v_w_up_b', 'v_w_out', 'v_g_mem_q', 'v_g_mem_kv', 'v_w_q_mem', 'v_w_kv_mem', 'v_w_o_mem', 'v_g_ffn', 'v_w_ffn_gate', 'v_w_ffn_up', 'v_w_ffn_down', 'v_g_final']
TWIN_OUTPUTS = ['loss', 'grad_x', 'grad_g_mix', 'grad_w_in', 'grad_w_up_a', 'grad_w_up_b', 'grad_w_out', 'grad_g_mem_q', 'grad_g_mem_kv', 'grad_w_q_mem', 'grad_w_kv_mem', 'grad_w_o_mem', 'grad_g_ffn', 'grad_w_ffn_gate', 'grad_w_ffn_up', 'grad_w_ffn_down', 'grad_g_final', 'delta_g_mix', 'delta_w_in', 'delta_w_up_a', 'delta_w_up_b', 'delta_w_out', 'delta_g_mem_q', 'delta_g_mem_kv', 'delta_w_q_mem', 'delta_w_kv_mem', 'delta_w_o_mem', 'delta_g_ffn', 'delta_w_ffn_gate', 'delta_w_ffn_up', 'delta_w_ffn_down', 'delta_g_final', 'new_m_g_mix', 'new_m_w_in', 'new_m_w_up_a', 'new_m_w_up_b', 'new_m_w_out', 'new_m_g_mem_q', 'new_m_g_mem_kv', 'new_m_w_q_mem', 'new_m_w_kv_mem', 'new_m_w_o_mem', 'new_m_g_ffn', 'new_m_w_ffn_gate', 'new_m_w_ffn_up', 'new_m_w_ffn_down', 'new_m_g_final', 'new_v_g_mix', 'new_v_w_in', 'new_v_w_up_a', 'new_v_w_up_b', 'new_v_w_out', 'new_v_g_mem_q', 'new_v_g_mem_kv', 'new_v_w_q_mem', 'new_v_w_kv_mem', 'new_v_w_o_mem', 'new_v_g_ffn', 'new_v_w_ffn_gate', 'new_v_w_ffn_up', 'new_v_w_ffn_down', 'new_v_g_final']
TWIN_LEAF_KINDS = {'loss': 'loss', 'grad_x': 'grad_x', 'grad_g_mix': 'grad_w', 'grad_w_in': 'grad_w', 'grad_w_up_a': 'grad_w', 'grad_w_up_b': 'grad_w', 'grad_w_out': 'grad_w', 'grad_g_mem_q': 'grad_w', 'grad_g_mem_kv': 'grad_w', 'grad_w_q_mem': 'grad_w', 'grad_w_kv_mem': 'grad_w', 'grad_w_o_mem': 'grad_w', 'grad_g_ffn': 'grad_w', 'grad_w_ffn_gate': 'grad_w', 'grad_w_ffn_up': 'grad_w', 'grad_w_ffn_down': 'grad_w', 'grad_g_final': 'grad_w', 'delta_g_mix': 'delta_w', 'delta_w_in': 'delta_w', 'delta_w_up_a': 'delta_w', 'delta_w_up_b': 'delta_w', 'delta_w_out': 'delta_w', 'delta_g_mem_q': 'delta_w', 'delta_g_mem_kv': 'delta_w', 'delta_w_q_mem': 'delta_w', 'delta_w_kv_mem': 'delta_w', 'delta_w_o_mem': 'delta_w', 'delta_g_ffn': 'delta_w', 'delta_w_ffn_gate': 'delta_w', 'delta_w_ffn_up': 'delta_w', 'delta_w_ffn_down': 'delta_w', 'delta_g_final': 'delta_w', 'new_m_g_mix': 'new_m', 'new_m_w_in': 'new_m', 'new_m_w_up_a': 'new_m', 'new_m_w_up_b': 'new_m', 'new_m_w_out': 'new_m', 'new_m_g_mem_q': 'new_m', 'new_m_g_mem_kv': 'new_m', 'new_m_w_q_mem': 'new_m', 'new_m_w_kv_mem': 'new_m', 'new_m_w_o_mem': 'new_m', 'new_m_g_ffn': 'new_m', 'new_m_w_ffn_gate': 'new_m', 'new_m_w_ffn_up': 'new_m', 'new_m_w_ffn_down': 'new_m', 'new_m_g_final': 'new_m', 'new_v_g_mix': 'new_v', 'new_v_w_in': 'new_v', 'new_v_w_up_a': 'new_v', 'new_v_w_up_b': 'new_v', 'new_v_w_out': 'new_v', 'new_v_g_mem_q': 'new_v', 'new_v_g_mem_kv': 'new_v', 'new_v_w_q_mem': 'new_v', 'new_v_w_kv_mem': 'new_v', 'new_v_w_o_mem': 'new_v', 'new_v_g_ffn': 'new_v', 'new_v_w_ffn_gate': 'new_v', 'new_v_w_ffn_up': 'new_v', 'new_v_w_ffn_down': 'new_v', 'new_v_g_final': 'new_v'}


def _forward(args):
    return _fwd_reference(*[args[k] for k in FWD_PARAMS])


def _output_shape():
    out = _jax.eval_shape(lambda: _forward(_fwd_setup_inputs(0)))
    return out.shape, out.dtype

N_MICROBATCH = 1
ADAM_LR = 0.001
ADAM_B1 = 0.9
ADAM_B2 = 0.999
ADAM_EPS = 1e-08
ADAM_WD = 0.01
ADAM_STEP = 10
PER_EXAMPLE_BATCH_AXIS = {'x': 0, 'mem': 0, 'positions': 0, 'loss_target': 0}
SHARED_INPUTS = []
_WEIGHT_DTYPES = {'g_mix': _jnp.float32, 'w_in': _jnp.float32, 'w_up_a': _jnp.float32, 'w_up_b': _jnp.float32, 'w_out': _jnp.float32, 'g_mem_q': _jnp.float32, 'g_mem_kv': _jnp.float32, 'w_q_mem': _jnp.float32, 'w_kv_mem': _jnp.float32, 'w_o_mem': _jnp.float32, 'g_ffn': _jnp.float32, 'w_ffn_gate': _jnp.float32, 'w_ffn_up': _jnp.float32, 'w_ffn_down': _jnp.float32, 'g_final': _jnp.float32}
MOMENT_SCALE = {'g_mix': 8.988610e-02, 'w_in': 3.996910e-02, 'w_up_a': 2.311503e-02, 'w_up_b': 6.581687e-02, 'w_out': 6.836285e-02, 'g_mem_q': 1.988819e-02, 'g_mem_kv': 2.869595e-02, 'w_q_mem': 2.737226e-02, 'w_kv_mem': 2.769239e-02, 'w_o_mem': 1.970274e-02, 'g_ffn': 1.214906e-01, 'w_ffn_gate': 5.395738e-02, 'w_ffn_up': 5.217183e-02, 'w_ffn_down': 8.664970e-02, 'g_final': 3.199419e+01}


def _to_microbatches(a, axis):
    t = _jnp.moveaxis(a, axis, 0)
    t = t.reshape((N_MICROBATCH, t.shape[0] // N_MICROBATCH) + t.shape[1:])
    return _jnp.moveaxis(t, 1, axis + 1)


def setup_inputs(seed: int = 0) -> dict:
    inp = _fwd_setup_inputs(seed)
    key = _jax.random.fold_in(_jax.random.key(seed), 7919)
    shape, _ = _output_shape()
    out = dict(inp)
    out["loss_target"] = _jax.random.normal(_jax.random.fold_in(key, 0), shape, _jnp.float32)
    for i, name in enumerate(TWIN_WEIGHTS):
        w = inp[name].astype(_jnp.float32)
        if MOMENT_SCALE is None:
            s = _jnp.sqrt(_jnp.mean(_jnp.square(w)) + 1e-30)
        else:
            s = MOMENT_SCALE[name]
        km, kv = _jax.random.split(_jax.random.fold_in(key, i + 1))
        out[name] = w
        out["m_" + name] = s * _jax.random.normal(km, w.shape, _jnp.float32)
        out["v_" + name] = (s * s) * _jax.random.uniform(kv, w.shape, _jnp.float32, 0.5, 1.5)
    if N_MICROBATCH > 1:
        for name, axis in PER_EXAMPLE_BATCH_AXIS.items():
            out[name] = _to_microbatches(out[name], axis)
    return {'x': out['x'], 'mem': out['mem'], 'positions': out['positions'], 'g_mix': out['g_mix'], 'w_in': out['w_in'], 'w_up_a': out['w_up_a'], 'w_up_b': out['w_up_b'], 'w_out': out['w_out'], 'g_mem_q': out['g_mem_q'], 'g_mem_kv': out['g_mem_kv'], 'w_q_mem': out['w_q_mem'], 'w_kv_mem': out['w_kv_mem'], 'w_o_mem': out['w_o_mem'], 'g_ffn': out['g_ffn'], 'w_ffn_gate': out['w_ffn_gate'], 'w_ffn_up': out['w_ffn_up'], 'w_ffn_down': out['w_ffn_down'], 'g_final': out['g_final'], 'loss_target': out['loss_target'], 'm_g_mix': out['m_g_mix'], 'm_w_in': out['m_w_in'], 'm_w_up_a': out['m_w_up_a'], 'm_w_up_b': out['m_w_up_b'], 'm_w_out': out['m_w_out'], 'm_g_mem_q': out['m_g_mem_q'], 'm_g_mem_kv': out['m_g_mem_kv'], 'm_w_q_mem': out['m_w_q_mem'], 'm_w_kv_mem': out['m_w_kv_mem'], 'm_w_o_mem': out['m_w_o_mem'], 'm_g_ffn': out['m_g_ffn'], 'm_w_ffn_gate': out['m_w_ffn_gate'], 'm_w_ffn_up': out['m_w_ffn_up'], 'm_w_ffn_down': out['m_w_ffn_down'], 'm_g_final': out['m_g_final'], 'v_g_mix': out['v_g_mix'], 'v_w_in': out['v_w_in'], 'v_w_up_a': out['v_w_up_a'], 'v_w_up_b': out['v_w_up_b'], 'v_w_out': out['v_w_out'], 'v_g_mem_q': out['v_g_mem_q'], 'v_g_mem_kv': out['v_g_mem_kv'], 'v_w_q_mem': out['v_w_q_mem'], 'v_w_kv_mem': out['v_w_kv_mem'], 'v_w_o_mem': out['v_w_o_mem'], 'v_g_ffn': out['v_g_ffn'], 'v_w_ffn_gate': out['v_w_ffn_gate'], 'v_w_ffn_up': out['v_w_ffn_up'], 'v_w_ffn_down': out['v_w_ffn_down'], 'v_g_final': out['v_g_final']}


def _loss(weights, diff, rest, loss_target):
    with _jax.named_scope("forward"):
        args = {**rest, TWIN_DIFF_INPUT: diff, **{k: w.astype(_WEIGHT_DTYPES[k]) for k, w in weights.items()}}
        y = _forward(args)
    with _jax.named_scope("loss_head"):
        err = _jnp.square(y.astype(_jnp.float32) - loss_target)
        return 0.5 * _jnp.sum(_jnp.mean(err, axis=-1)) if err.ndim else 0.5 * err


def _adamw(w, g, m, v):
    m = ADAM_B1 * m + (1.0 - ADAM_B1) * g
    v = ADAM_B2 * v + (1.0 - ADAM_B2) * _jnp.square(g)
    m_hat = m / (1.0 - ADAM_B1 ** ADAM_STEP)
    v_hat = v / (1.0 - ADAM_B2 ** ADAM_STEP)
    delta = -ADAM_LR * (m_hat / (_jnp.sqrt(v_hat) + ADAM_EPS) + ADAM_WD * w)
    return delta, m, v


def reference(x, mem, positions, g_mix, w_in, w_up_a, w_up_b, w_out, g_mem_q, g_mem_kv, w_q_mem, w_kv_mem, w_o_mem, g_ffn, w_ffn_gate, w_ffn_up, w_ffn_down, g_final, loss_target, m_g_mix, m_w_in, m_w_up_a, m_w_up_b, m_w_out, m_g_mem_q, m_g_mem_kv, m_w_q_mem, m_w_kv_mem, m_w_o_mem, m_g_ffn, m_w_ffn_gate, m_w_ffn_up, m_w_ffn_down, m_g_final, v_g_mix, v_w_in, v_w_up_a, v_w_up_b, v_w_out, v_g_mem_q, v_g_mem_kv, v_w_q_mem, v_w_kv_mem, v_w_o_mem, v_g_ffn, v_w_ffn_gate, v_w_ffn_up, v_w_ffn_down, v_g_final):
    given = dict(x=x, mem=mem, positions=positions, g_mix=g_mix, w_in=w_in, w_up_a=w_up_a, w_up_b=w_up_b, w_out=w_out, g_mem_q=g_mem_q, g_mem_kv=g_mem_kv, w_q_mem=w_q_mem, w_kv_mem=w_kv_mem, w_o_mem=w_o_mem, g_ffn=g_ffn, w_ffn_gate=w_ffn_gate, w_ffn_up=w_ffn_up, w_ffn_down=w_ffn_down, g_final=g_final, loss_target=loss_target, m_g_mix=m_g_mix, m_w_in=m_w_in, m_w_up_a=m_w_up_a, m_w_up_b=m_w_up_b, m_w_out=m_w_out, m_g_mem_q=m_g_mem_q, m_g_mem_kv=m_g_mem_kv, m_w_q_mem=m_w_q_mem, m_w_kv_mem=m_w_kv_mem, m_w_o_mem=m_w_o_mem, m_g_ffn=m_g_ffn, m_w_ffn_gate=m_w_ffn_gate, m_w_ffn_up=m_w_ffn_up, m_w_ffn_down=m_w_ffn_down, m_g_final=m_g_final, v_g_mix=v_g_mix, v_w_in=v_w_in, v_w_up_a=v_w_up_a, v_w_up_b=v_w_up_b, v_w_out=v_w_out, v_g_mem_q=v_g_mem_q, v_g_mem_kv=v_g_mem_kv, v_w_q_mem=v_w_q_mem, v_w_kv_mem=v_w_kv_mem, v_w_o_mem=v_w_o_mem, v_g_ffn=v_g_ffn, v_w_ffn_gate=v_w_ffn_gate, v_w_ffn_up=v_w_ffn_up, v_w_ffn_down=v_w_ffn_down, v_g_final=v_g_final)
    weights = {n: given[n] for n in TWIN_WEIGHTS}
    shared = {n: given[n] for n in SHARED_INPUTS}
    per_example = {n: given[n] for n in ['x', 'mem', 'positions']}
    grad_fn = _jax.value_and_grad(_loss, argnums=(0, 1))

    def one_microbatch(ex, loss_target):
        ex = dict(ex)
        diff = ex.pop(TWIN_DIFF_INPUT)
        return grad_fn(weights, diff, {**shared, **ex}, loss_target)

    if N_MICROBATCH == 1:
        loss, (grad_w, grad_x) = one_microbatch(per_example, given["loss_target"])
    else:
        def body(carry, xs):
            loss_sum, grad_sum = carry
            l_k, (gw_k, gx_k) = one_microbatch(xs[0], xs[1])
            with _jax.named_scope("update"):
                return (loss_sum + l_k, _jax.tree.map(_jnp.add, grad_sum, gw_k)), gx_k

        init = (_jnp.zeros((), _jnp.float32), _jax.tree.map(_jnp.zeros_like, weights))
        (loss, grad_w), grad_x = _jax.lax.scan(body, init, (per_example, given["loss_target"]))
    with _jax.named_scope("update"):
        delta_w, new_m, new_v = {}, {}, {}
        for n in TWIN_WEIGHTS:
            delta_w[n], new_m[n], new_v[n] = _adamw(weights[n], grad_w[n], given["m_" + n], given["v_" + n])
    return (loss, grad_x, *[grad_w[n] for n in TWIN_WEIGHTS], *[delta_w[n] for n in TWIN_WEIGHTS],
            *[new_m[n] for n in TWIN_WEIGHTS], *[new_v[n] for n in TWIN_WEIGHTS])
```

```python
import functools
import math

import jax
import jax.numpy as jnp
from jax import lax
from jax.experimental import pallas as pl
from jax.experimental.pallas import tpu as pltpu

F32 = jnp.float32
BF16 = jnp.bfloat16

N_DEV = 8
HEAD_DIM = 64
MEM_HEAD_DIM = 128
N_HEADS_MEM = 4
BLOCK = 128
DIL_PATTERNS = ((128, 1), (512, 4), (2048, 16))
ROPE_THETA = 500000.0
ROPE_HALF = 8
RMS_EPS = 1e-6
ADAM_LR, ADAM_B1, ADAM_B2, ADAM_EPS, ADAM_WD, ADAM_STEP = 0.001, 0.9, 0.999, 1e-08, 0.01, 10
NEG = -1e30
ROW_TILE = 512
LANES = 128

ANY = pl.BlockSpec(memory_space=pl.ANY)
VMEM = pl.BlockSpec(memory_space=pltpu.VMEM)
NN = (((1,), (0,)), ((), ()))
NT = (((1,), (1,)), ((), ()))
TN = (((0,), (0,)), ((), ()))


def _params(sem):
    return pltpu.CompilerParams(dimension_semantics=sem)


def _mm(name, a, b, *, grid, a_spec, b_spec, o_shape, o_spec, dims, out_dtype, nk=1, res=None, res_spec=None):
    has_res = res is not None

    def body(*refs):
        a_ref, b_ref = refs[0], refs[1]
        r_ref = refs[2] if has_res else None
        o_ref = refs[3] if has_res else refs[2]
        p = lax.dot_general(a_ref[...], b_ref[...], dims, preferred_element_type=F32)
        if nk == 1:
            if has_res:
                p = p + r_ref[...].astype(F32)
            o_ref[...] = p.astype(out_dtype)
            return
        acc_ref = refs[-1]
        k = pl.program_id(len(grid) - 1)

        @pl.when(k == 0)
        def _():
            acc_ref[...] = p

        @pl.when(k > 0)
        def _():
            acc_ref[...] += p

        @pl.when(k == nk - 1)
        def _():
            t = acc_ref[...]
            if has_res:
                t = t + r_ref[...].astype(F32)
            o_ref[...] = t.astype(out_dtype)

    o_block = tuple(d for d in o_spec.block_shape if d is not None)
    sem = ("parallel",) * (len(grid) - 1) + (("arbitrary",) if nk > 1 else ("parallel",))
    return pl.pallas_call(
        body, name=name, grid=grid,
        in_specs=[a_spec, b_spec] + ([res_spec] if has_res else []),
        out_specs=o_spec, out_shape=jax.ShapeDtypeStruct(o_shape, out_dtype),
        scratch_shapes=[pltpu.VMEM(o_block, F32)] if nk > 1 else [],
        compiler_params=_params(sem),
    )(*([a, b] + ([res] if has_res else [])))


def _rms_fwd(name, x, g):
    t, d = x.shape
    tm = min(ROW_TILE, t)

    def body(x_ref, g_ref, o_ref):
        xf = x_ref[...]
        r = lax.rsqrt(jnp.mean(xf * xf, axis=-1, keepdims=True) + RMS_EPS)
        o_ref[...] = (xf * r * g_ref[...]).astype(BF16)

    return pl.pallas_call(
        body, name=name, grid=(t // tm,),
        in_specs=[pl.BlockSpec((tm, d), lambda i: (i, 0)), pl.BlockSpec((1, d), lambda i: (0, 0))],
        out_specs=pl.BlockSpec((tm, d), lambda i: (i, 0)), out_shape=jax.ShapeDtypeStruct((t, d), BF16),
        compiler_params=_params(("parallel",)),
    )(x, g)


def _rms_bwd(name, dn, x, g, dres):
    t, d = x.shape
    tm = min(ROW_TILE, t)
    has_res = dres is not None

    def body(*refs):
        dn_ref, x_ref, g_ref = refs[0], refs[1], refs[2]
        r_ref = refs[3] if has_res else None
        dx_ref, dg_ref = refs[-2], refs[-1]
        xf = x_ref[...]
        r = lax.rsqrt(jnp.mean(xf * xf, axis=-1, keepdims=True) + RMS_EPS)
        xh = xf * r
        dnf = dn_ref[...].astype(F32)
        dxh = dnf * g_ref[...]
        dx = r * (dxh - xh * jnp.mean(dxh * xh, axis=-1, keepdims=True))
        if has_res:
            dx = dx + r_ref[...]
        dx_ref[...] = dx

        @pl.when(pl.program_id(0) == 0)
        def _():
            dg_ref[...] = jnp.zeros_like(dg_ref)

        dg_ref[...] += jnp.sum(dnf * xh, axis=0, keepdims=True)

    row = pl.BlockSpec((tm, d), lambda i: (i, 0))
    vec = pl.BlockSpec((1, d), lambda i: (0, 0))
    return pl.pallas_call(
        body, name=name, grid=(t // tm,),
        in_specs=[row, row, vec] + ([row] if has_res else []),
        out_specs=[row, vec],
        out_shape=[jax.ShapeDtypeStruct((t, d), F32), jax.ShapeDtypeStruct((1, d), F32)],
        compiler_params=_params(("arbitrary",)),
    )(*([dn, x, g] + ([dres] if has_res else [])))


def _loss_head(h, tgt, g):
    t, d = h.shape
    tm = min(ROW_TILE, t)

    def body(h_ref, t_ref, g_ref, loss_ref, dh_ref, dg_ref):
        xf = h_ref[...]
        gv = g_ref[...]
        r = lax.rsqrt(jnp.mean(xf * xf, axis=-1, keepdims=True) + RMS_EPS)
        xh = xf * r
        e = xh * gv - t_ref[...]
        dy = e * (1.0 / d)
        dxh = dy * gv
        dh_ref[...] = r * (dxh - xh * jnp.mean(dxh * xh, axis=-1, keepdims=True))

        @pl.when(pl.program_id(0) == 0)
        def _():
            dg_ref[...] = jnp.zeros_like(dg_ref)
            loss_ref[...] = jnp.zeros_like(loss_ref)

        dg_ref[...] += jnp.sum(dy * xh, axis=0, keepdims=True)
        part = jnp.sum(jnp.sum(e * e, axis=1, keepdims=True), axis=0, keepdims=True) * (0.5 / d)
        loss_ref[...] += jnp.broadcast_to(part, loss_ref.shape)

    row = pl.BlockSpec((tm, d), lambda i: (i, 0))
    vec = pl.BlockSpec((1, d), lambda i: (0, 0))
    return pl.pallas_call(
        body, name="loss_head", grid=(t // tm,),
        in_specs=[row, row, vec],
        out_specs=[pl.BlockSpec((8, LANES), lambda i: (0, 0)), row, vec],
        out_shape=[jax.ShapeDtypeStruct((8, LANES), F32), jax.ShapeDtypeStruct((t, d), F32),
                   jax.ShapeDtypeStruct((1, d), F32)],
        compiler_params=_params(("arbitrary",)),
    )(h, tgt, g)


def _rope_tables(pos, inv_freq, sel_lo, sel_hi):
    t = pos.shape[0]
    tm = min(ROW_TILE, t)

    def body(p_ref, f_ref, lo_ref, hi_ref, c_ref, sa_ref, sb_ref):
        ang = p_ref[...].astype(F32) * f_ref[...]
        rot = lo_ref[...] + hi_ref[...]
        cs, sn = jnp.cos(ang), jnp.sin(ang)
        c_ref[...] = cs * rot + (1.0 - rot)
        sa_ref[...] = -sn * lo_ref[...]
        sb_ref[...] = sn * hi_ref[...]

    vec = pl.BlockSpec((1, LANES), lambda i: (0, 0))
    row = pl.BlockSpec((tm, LANES), lambda i: (i, 0))
    return pl.pallas_call(
        body, name="rope_tables", grid=(t // tm,),
        in_specs=[pl.BlockSpec((tm, 1), lambda i: (i, 0)), vec, vec, vec],
        out_specs=[row, row, row], out_shape=[jax.ShapeDtypeStruct((t, LANES), F32)] * 3,
        compiler_params=_params(("parallel",)),
    )(pos, inv_freq, sel_lo, sel_hi)


def _rope_apply(name, src, col0, n_cols, cos_t, sin_a, sin_b, sign):
    t = src.shape[0]
    tm = min(ROW_TILE, t)

    def body(x_ref, c_ref, sa_ref, sb_ref, o_ref):
        xf = x_ref[...].astype(F32)
        up = pltpu.roll(xf, LANES - ROPE_HALF, 1)
        dn = pltpu.roll(xf, ROPE_HALF, 1)
        o_ref[...] = (xf * c_ref[...] + sign * (up * sa_ref[...] + dn * sb_ref[...])).astype(BF16)

    tab = pl.BlockSpec((tm, LANES), lambda i, c: (i, 0))
    return pl.pallas_call(
        body, name=name, grid=(t // tm, n_cols),
        in_specs=[pl.BlockSpec((tm, LANES), lambda i, c: (i, col0 + c)), tab, tab, tab],
        out_specs=pl.BlockSpec((tm, LANES), lambda i, c: (i, c)),
        out_shape=jax.ShapeDtypeStruct((t, n_cols * LANES), BF16),
        compiler_params=_params(("parallel", "parallel")),
    )(src, cos_t, sin_a, sin_b)


def _dilated_bias(s):
    nq = s // BLOCK
    delta = jnp.arange(nq, dtype=jnp.int32)[:, None, None] * BLOCK
    dist = delta + jnp.arange(BLOCK, dtype=jnp.int32)[None, :, None] - jnp.arange(BLOCK, dtype=jnp.int32)[None, None, :]
    cnt = jnp.zeros(dist.shape, F32)
    for window, dil in DIL_PATTERNS:
        cnt = cnt + ((dist >= 0) & (dist % dil == 0) & (dist <= window)).astype(F32)
    return jnp.where(cnt > 0, jnp.log(jnp.maximum(cnt, 1.0)), NEG)


def _lane_lo():
    return lax.broadcasted_iota(jnp.int32, (BLOCK, LANES), 1) < HEAD_DIM


def _attn_a_fwd(qk, proj, v_col0, bias, batch, s):
    t = qk.shape[0]
    nq = s // BLOCK
    n_pairs = 4
    scale = HEAD_DIM ** -0.5

    def body(q_ref, k_ref, v_ref, b_ref, o_ref, lse_ref):
        i = pl.program_id(2)
        lo = _lane_lo()
        q = q_ref[...]
        zero = jnp.zeros_like(q)
        q0, q1 = jnp.where(lo, q, zero), jnp.where(lo, zero, q)

        def step(dlt, carry):
            m0, l0, m1, l1, acc = carry
            j = i - dlt
            rows = pl.ds(pl.multiple_of(j * BLOCK, BLOCK), BLOCK)
            k = k_ref[rows, :]
            v = v_ref[rows, :]
            bias_t = b_ref[dlt]
            s0 = lax.dot_general(q0, k, NT, preferred_element_type=F32) * scale + bias_t
            s1 = lax.dot_general(q1, k, NT, preferred_element_type=F32) * scale + bias_t
            n0 = jnp.maximum(m0, jnp.max(s0, axis=1, keepdims=True))
            n1 = jnp.maximum(m1, jnp.max(s1, axis=1, keepdims=True))
            p0, p1 = jnp.exp(s0 - n0), jnp.exp(s1 - n1)
            a0, a1 = jnp.exp(m0 - n0), jnp.exp(m1 - n1)
            l0 = a0 * l0 + jnp.sum(p0, axis=1, keepdims=True)
            l1 = a1 * l1 + jnp.sum(p1, axis=1, keepdims=True)
            vz = jnp.zeros_like(v)
            pv = (lax.dot_general(p0.astype(BF16), jnp.where(lo, v, vz), NN, preferred_element_type=F32)
                  + lax.dot_general(p1.astype(BF16), jnp.where(lo, vz, v), NN, preferred_element_type=F32))
            acc = acc * jnp.where(lo, a0, a1) + pv
            return n0, l0, n1, l1, acc

        col = jnp.full((BLOCK, 1), NEG, F32)
        zc = jnp.zeros((BLOCK, 1), F32)
        m0, l0, m1, l1, acc = lax.fori_loop(0, i + 1, step, (col, zc, col, zc, jnp.zeros((BLOCK, LANES), F32)))
        o_ref[...] = (acc / jnp.where(lo, l0, l1)).astype(BF16)
        lse_ref[...] = jnp.where(lo, m0 + jnp.log(l0), m1 + jnp.log(l1))

    blk = pl.BlockSpec((BLOCK, LANES), lambda b, h, i: (b * nq + i, h))
    return pl.pallas_call(
        body, name="attn_a_fwd", grid=(batch, n_pairs, nq),
        in_specs=[blk,
                  pl.BlockSpec((s, LANES), lambda b, h, i: (b, n_pairs + h)),
                  pl.BlockSpec((s, LANES), lambda b, h, i: (b, v_col0 + h)),
                  pl.BlockSpec((nq, BLOCK, BLOCK), lambda b, h, i: (0, 0, 0))],
        out_specs=[blk, blk],
        out_shape=[jax.ShapeDtypeStruct((t, n_pairs * LANES), BF16), jax.ShapeDtypeStruct((t, n_pairs * LANES), F32)],
        compiler_params=_params(("parallel", "parallel", "arbitrary")),
    )(qk, qk, proj, bias)


def _attn_a_bwd(qk, proj, v_col0, bias, o, lse, do, batch, s):
    t = qk.shape[0]
    nq = s // BLOCK
    n_pairs = 4
    scale = HEAD_DIM ** -0.5

    def body(q_ref, k_ref, v_ref, b_ref, o_ref, lse_ref, do_ref, dq_ref, dk_ref, dv_ref, dk_acc, dv_acc):
        i = pl.program_id(2)
        lo = _lane_lo()

        @pl.when(i == 0)
        def _():
            dk_acc[...] = jnp.zeros_like(dk_acc)
            dv_acc[...] = jnp.zeros_like(dv_acc)

        q = q_ref[...]
        do_ = do_ref[...]
        zero = jnp.zeros_like(q)
        q0, q1 = jnp.where(lo, q, zero), jnp.where(lo, zero, q)
        do0, do1 = jnp.where(lo, do_, zero), jnp.where(lo, zero, do_)
        dd = do_.astype(F32) * o_ref[...].astype(F32)
        fz = jnp.zeros_like(dd)
        dl0 = jnp.sum(jnp.where(lo, dd, fz), axis=1, keepdims=True)
        dl1 = jnp.sum(jnp.where(lo, fz, dd), axis=1, keepdims=True)
        lse_t = lse_ref[...]
        ls0 = lse_t[:, 0:1]
        ls1 = lse_t[:, HEAD_DIM:HEAD_DIM + 1]

        def step(dlt, dq):
            j = i - dlt
            rows = pl.ds(pl.multiple_of(j * BLOCK, BLOCK), BLOCK)
            k = k_ref[rows, :]
            v = v_ref[rows, :]
            bias_t = b_ref[dlt]
            kz = jnp.zeros_like(k)
            k0, k1 = jnp.where(lo, k, kz), jnp.where(lo, kz, k)
            p0 = jnp.exp(lax.dot_general(q0, k, NT, preferred_element_type=F32) * scale + bias_t - ls0)
            p1 = jnp.exp(lax.dot_general(q1, k, NT, preferred_element_type=F32) * scale + bias_t - ls1)
            dp0 = lax.dot_general(do0, v, NT, preferred_element_type=F32)
            dp1 = lax.dot_general(do1, v, NT, preferred_element_type=F32)
            ds0 = (p0 * (dp0 - dl0) * scale).astype(BF16)
            ds1 = (p1 * (dp1 - dl1) * scale).astype(BF16)
            dq = dq + (lax.dot_general(ds0, k0, NN, preferred_element_type=F32)
                       + lax.dot_general(ds1, k1, NN, preferred_element_type=F32))
            dk_acc[rows, :] += (lax.dot_general(ds0, q0, TN, preferred_element_type=F32)
                                + lax.dot_general(ds1, q1, TN, preferred_element_type=F32))
            dv_acc[rows, :] += (lax.dot_general(p0.astype(BF16), do0, TN, preferred_element_type=F32)
                                + lax.dot_general(p1.astype(BF16), do1, TN, preferred_element_type=F32))
            return dq

        dq = lax.fori_loop(0, i + 1, step, jnp.zeros((BLOCK, LANES), F32))
        dq_ref[...] = dq.astype(BF16)

        @pl.when(i == nq - 1)
        def _():
            dk_ref[...] = dk_acc[...].astype(BF16)
            dv_ref[...] = dv_acc[...].astype(BF16)

    blk = pl.BlockSpec((BLOCK, LANES), lambda b, h, i: (b * nq + i, h))
    seq = pl.BlockSpec((s, LANES), lambda b, h, i: (b, h))
    out = jax.ShapeDtypeStruct((t, n_pairs * LANES), BF16)
    return pl.pallas_call(
        body, name="attn_a_bwd", grid=(batch, n_pairs, nq),
        in_specs=[blk,
                  pl.BlockSpec((s, LANES), lambda b, h, i: (b, n_pairs + h)),
                  pl.BlockSpec((s, LANES), lambda b, h, i: (b, v_col0 + h)),
                  pl.BlockSpec((nq, BLOCK, BLOCK), lambda b, h, i: (0, 0, 0)),
                  blk, blk, blk],
        out_specs=[blk, seq, seq], out_shape=[out, out, out],
        scratch_shapes=[pltpu.VMEM((s, LANES), F32), pltpu.VMEM((s, LANES), F32)],
        compiler_params=_params(("parallel", "parallel", "arbitrary")),
    )(qk, qk, proj, bias, o, lse, do)


def _tri(after):
    r = lax.broadcasted_iota(jnp.int32, (BLOCK, BLOCK), 0)
    c = lax.broadcasted_iota(jnp.int32, (BLOCK, BLOCK), 1)
    return ((r > c) if after else (r < c)).astype(F32)


def _sb_logs(qm, k, scale, valid):
    z = lax.dot_general(qm, k, NT, preferred_element_type=F32) * scale
    lsig = jnp.minimum(z, 0.0) - jnp.log(1.0 + jnp.exp(-jnp.abs(z)))
    lneg = jnp.where(valid, lsig - z, 0.0)
    return lsig, lneg


def _attn_b_fwd(proj, q_col0, k_col0, v_col0, batch, s):
    t = proj.shape[0]
    nq = s // BLOCK
    n_pairs = 4
    scale = HEAD_DIM ** -0.5

    def body(q_ref, k_ref, v_ref, o_ref, run_ref):
        i = pl.program_id(2)
        lo = _lane_lo()
        lane = lax.broadcasted_iota(jnp.int32, (BLOCK, LANES), 1)
        strict = lane < lax.broadcasted_iota(jnp.int32, (BLOCK, LANES), 0)
        upper = _tri(True)
        q = q_ref[...]
        zero = jnp.zeros_like(q)
        qs = (jnp.where(lo, q, zero), jnp.where(lo, zero, q))

        def step(dlt, carry):
            acc, r0, r1, sv0, sv1 = carry
            j = i - dlt
            rows = pl.ds(pl.multiple_of(j * BLOCK, BLOCK), BLOCK)
            k = k_ref[rows, :]
            v = v_ref[rows, :]
            vz = jnp.zeros_like(v)
            vs = (jnp.where(lo, v, vz), jnp.where(lo, vz, v))
            valid = jnp.logical_or(dlt > 0, strict)
            runs, saves = [], []
            for e, (run, save) in enumerate(((r0, sv0), (r1, sv1))):
                lsig, lneg = _sb_logs(qs[e], k, scale, valid)
                inside = lax.dot_general(lneg, upper, NN, preferred_element_type=F32, precision=lax.Precision.HIGHEST)
                a = jnp.where(valid, jnp.exp(lsig + run + inside), 0.0)
                acc = acc + lax.dot_general(a.astype(BF16), vs[e], NN, preferred_element_type=F32)
                saves.append(jnp.where(lane == j, run, save))
                runs.append(run + jnp.sum(lneg, axis=1, keepdims=True))
            return acc, runs[0], runs[1], saves[0], saves[1]

        zc = jnp.zeros((BLOCK, 1), F32)
        zt = jnp.zeros((BLOCK, LANES), F32)
        zs = lax.broadcasted_iota(jnp.int32, (BLOCK, LANES), 0).astype(F32) * 0.0
        acc, _, _, sv0, sv1 = lax.fori_loop(0, i + 1, step, (zt, zc, zc, zs, zs))
        o_ref[...] = acc.astype(BF16)
        run_ref[:, 0:LANES] = sv0
        run_ref[:, LANES:2 * LANES] = sv1

    def seq(col0):
        return pl.BlockSpec((s, LANES), lambda b, h, i: (b, col0 + h))

    return pl.pallas_call(
        body, name="attn_b_fwd", grid=(batch, n_pairs, nq),
        in_specs=[pl.BlockSpec((BLOCK, LANES), lambda b, h, i: (b * nq + i, q_col0 + h)), seq(k_col0), seq(v_col0)],
        out_specs=[pl.BlockSpec((BLOCK, LANES), lambda b, h, i: (b * nq + i, h)),
                   pl.BlockSpec((BLOCK, 2 * LANES), lambda b, h, i: (b * nq + i, h))],
        out_shape=[jax.ShapeDtypeStruct((t, n_pairs * LANES), BF16),
                   jax.ShapeDtypeStruct((t, n_pairs * 2 * LANES), F32)],
        compiler_params=_params(("parallel", "parallel", "arbitrary")),
    )(proj, proj, proj)


def _attn_b_bwd(proj, q_col0, k_col0, v_col0, runs, do, batch, s):
    t = proj.shape[0]
    nq = s // BLOCK
    n_pairs = 4
    scale = HEAD_DIM ** -0.5

    def body(q_ref, k_ref, v_ref, run_ref, do_ref, dq_ref, dk_ref, dv_ref, dk_acc, dv_acc):
        i = pl.program_id(2)
        lo = _lane_lo()
        lane = lax.broadcasted_iota(jnp.int32, (BLOCK, LANES), 1)
        strict = lane < lax.broadcasted_iota(jnp.int32, (BLOCK, LANES), 0)
        upper = _tri(True)
        lower = _tri(False)

        @pl.when(i == 0)
        def _():
            dk_acc[...] = jnp.zeros_like(dk_acc)
            dv_acc[...] = jnp.zeros_like(dv_acc)

        q = q_ref[...]
        do_ = do_ref[...]
        zero = jnp.zeros_like(q)
        qs = (jnp.where(lo, q, zero), jnp.where(lo, zero, q))
        dos = (jnp.where(lo, do_, zero), jnp.where(lo, zero, do_))
        saved = (run_ref[:, 0:LANES], run_ref[:, LANES:2 * LANES])

        def step(j, carry):
            dq, c0, c1 = carry
            rows = pl.ds(pl.multiple_of(j * BLOCK, BLOCK), BLOCK)
            k = k_ref[rows, :]
            v = v_ref[rows, :]
            kz = jnp.zeros_like(k)
            ks = (jnp.where(lo, k, kz), jnp.where(lo, kz, k))
            valid = jnp.logical_or(j < i, strict)
            before = []
            dk_t = jnp.zeros((BLOCK, LANES), F32)
            dv_t = jnp.zeros((BLOCK, LANES), F32)
            for e, cb in enumerate((c0, c1)):
                lsig, lneg = _sb_logs(qs[e], k, scale, valid)
                inside = lax.dot_general(lneg, upper, NN, preferred_element_type=F32, precision=lax.Precision.HIGHEST)
                run = jnp.sum(jnp.where(lane == j, saved[e], 0.0), axis=1, keepdims=True)
                a = jnp.where(valid, jnp.exp(lsig + run + inside), 0.0)
                g = a * lax.dot_general(dos[e], v, NT, preferred_element_type=F32)
                pre = cb + lax.dot_general(g, lower, NN, preferred_element_type=F32, precision=lax.Precision.HIGHEST)
                sig = jnp.exp(lsig)
                dz = (jnp.where(valid, g * (1.0 - sig) - pre * sig, 0.0) * scale).astype(BF16)
                dq = dq + lax.dot_general(dz, ks[e], NN, preferred_element_type=F32)
                dk_t = dk_t + lax.dot_general(dz, qs[e], TN, preferred_element_type=F32)
                dv_t = dv_t + lax.dot_general(a.astype(BF16), dos[e], TN, preferred_element_type=F32)
                before.append(cb + jnp.sum(g, axis=1, keepdims=True))
            dk_acc[rows, :] += dk_t
            dv_acc[rows, :] += dv_t
            return dq, before[0], before[1]

        zc = jnp.zeros((BLOCK, 1), F32)
        dq, _, _ = lax.fori_loop(0, i + 1, step, (jnp.zeros((BLOCK, LANES), F32), zc, zc))
        dq_ref[...] = dq.astype(BF16)

        @pl.when(i == nq - 1)
        def _():
            dk_ref[...] = dk_acc[...].astype(BF16)
            dv_ref[...] = dv_acc[...].astype(BF16)

    def seq_in(col0):
        return pl.BlockSpec((s, LANES), lambda b, h, i: (b, col0 + h))

    blk = pl.BlockSpec((BLOCK, LANES), lambda b, h, i: (b * nq + i, h))
    seq = pl.BlockSpec((s, LANES), lambda b, h, i: (b, h))
    out = jax.ShapeDtypeStruct((t, n_pairs * LANES), BF16)
    return pl.pallas_call(
        body, name="attn_b_bwd", grid=(batch, n_pairs, nq),
        in_specs=[pl.BlockSpec((BLOCK, LANES), lambda b, h, i: (b * nq + i, q_col0 + h)), seq_in(k_col0), seq_in(v_col0),
                  pl.BlockSpec((BLOCK, 2 * LANES), lambda b, h, i: (b * nq + i, h)), blk],
        out_specs=[blk, seq, seq], out_shape=[out, out, out],
        scratch_shapes=[pltpu.VMEM((s, LANES), F32), pltpu.VMEM((s, LANES), F32)],
        compiler_params=_params(("parallel", "parallel", "arbitrary")),
    )(proj, proj, proj, runs, do)


MEM_Q_TILE = 256


def _mem_fwd(q, kv, batch, s, n_mem):
    t, width = q.shape
    tq = min(MEM_Q_TILE, s)
    nq = s // tq
    scale = MEM_HEAD_DIM ** -0.5

    def body(q_ref, kv_ref, o_ref):
        for h in range(N_HEADS_MEM):
            cols = slice(h * MEM_HEAD_DIM, (h + 1) * MEM_HEAD_DIM)
            k = kv_ref[:, cols]
            v = kv_ref[:, width + h * MEM_HEAD_DIM: width + (h + 1) * MEM_HEAD_DIM]
            sc = lax.dot_general(q_ref[:, cols], k, NT, preferred_element_type=F32) * scale
            p = jnp.exp(sc - jnp.max(sc, axis=1, keepdims=True))
            p = p / jnp.sum(p, axis=1, keepdims=True)
            o_ref[:, cols] = lax.dot_general(p.astype(BF16), v, NN, preferred_element_type=F32).astype(BF16)

    return pl.pallas_call(
        body, name="mem_attn_fwd", grid=(batch, nq),
        in_specs=[pl.BlockSpec((tq, width), lambda b, i: (b * nq + i, 0)),
                  pl.BlockSpec((n_mem, 2 * width), lambda b, i: (b, 0))],
        out_specs=pl.BlockSpec((tq, width), lambda b, i: (b * nq + i, 0)),
        out_shape=jax.ShapeDtypeStruct((t, width), BF16),
        compiler_params=_params(("parallel", "parallel")),
    )(q, kv)


def _mem_bwd(q, kv, do, batch, s, n_mem):
    t, width = q.shape
    tq = min(MEM_Q_TILE, s)
    nq = s // tq
    scale = MEM_HEAD_DIM ** -0.5

    def body(q_ref, kv_ref, do_ref, dq_ref, dkv_ref, acc):
        i = pl.program_id(1)

        @pl.when(i == 0)
        def _():
            acc[...] = jnp.zeros_like(acc)

        for h in range(N_HEADS_MEM):
            cols = slice(h * MEM_HEAD_DIM, (h + 1) * MEM_HEAD_DIM)
            vcols = slice(width + h * MEM_HEAD_DIM, width + (h + 1) * MEM_HEAD_DIM)
            qh, k, v, doh = q_ref[:, cols], kv_ref[:, cols], kv_ref[:, vcols], do_ref[:, cols]
            sc = lax.dot_general(qh, k, NT, preferred_element_type=F32) * scale
            p = jnp.exp(sc - jnp.max(sc, axis=1, keepdims=True))
            p = p / jnp.sum(p, axis=1, keepdims=True)
            dp = lax.dot_general(doh, v, NT, preferred_element_type=F32)
            ds = (p * (dp - jnp.sum(p * dp, axis=1, keepdims=True)) * scale).astype(BF16)
            dq_ref[:, cols] = lax.dot_general(ds, k, NN, preferred_element_type=F32).astype(BF16)
            acc[:, cols] += lax.dot_general(ds, qh, TN, preferred_element_type=F32)
            acc[:, vcols] += lax.dot_general(p.astype(BF16), doh, TN, preferred_element_type=F32)

        @pl.when(i == nq - 1)
        def _():
            dkv_ref[...] = acc[...].astype(BF16)

    row = pl.BlockSpec((tq, width), lambda b, i: (b * nq + i, 0))
    kvs = pl.BlockSpec((n_mem, 2 * width), lambda b, i: (b, 0))
    return pl.pallas_call(
        body, name="mem_attn_bwd", grid=(batch, nq),
        in_specs=[row, kvs, row], out_specs=[row, kvs],
        out_shape=[jax.ShapeDtypeStruct((t, width), BF16), jax.ShapeDtypeStruct((batch * n_mem, 2 * width), BF16)],
        scratch_shapes=[pltpu.VMEM((n_mem, 2 * width), F32)],
        compiler_params=_params(("parallel", "arbitrary")),
    )(q, kv, do)


def _mixer_fwd(o_a, o_b, w_a, w_b, proj, gate_col0):
    t, width = o_a.shape
    tm = min(ROW_TILE, t)
    n_sh, _, cs = w_a.shape

    def body(oa_ref, ob_ref, wa_ref, wb_ref, ga_ref, gb_ref, ua_ref, ub_ref, mix_ref):
        ua = lax.dot_general(oa_ref[...], wa_ref[...], NN, preferred_element_type=F32)
        ub = lax.dot_general(ob_ref[...], wb_ref[...], NN, preferred_element_type=F32)
        ua_ref[...] = ua.astype(BF16)
        ub_ref[...] = ub.astype(BF16)
        mix_ref[...] = (jax.nn.sigmoid(ga_ref[...].astype(F32)) * ua
                        + jax.nn.sigmoid(gb_ref[...].astype(F32)) * ub).astype(BF16)

    row = pl.BlockSpec((tm, width), lambda i, j: (i, 0))
    wsp = pl.BlockSpec((None, width, cs), lambda i, j: (j, 0, 0))
    out = pl.BlockSpec((tm, cs), lambda i, j: (i, j))
    osh = jax.ShapeDtypeStruct((t, n_sh * cs), BF16)
    return pl.pallas_call(
        body, name="mixer_fwd", grid=(t // tm, n_sh),
        in_specs=[row, row, wsp, wsp,
                  pl.BlockSpec((tm, cs), lambda i, j: (i, gate_col0 + j)),
                  pl.BlockSpec((tm, cs), lambda i, j: (i, gate_col0 + n_sh + j))],
        out_specs=[out, out, out], out_shape=[osh, osh, osh],
        compiler_params=_params(("parallel", "parallel")),
    )(o_a, o_b, w_a, w_b, proj, proj)


def _mixer_bwd(dmix, ua, ub, proj, gate_col0):
    t, d = dmix.shape
    tm = min(ROW_TILE, t)
    nc = d // LANES

    def body(dm_ref, ua_ref, ub_ref, ga_ref, gb_ref, dua_ref, dub_ref, dg_ref):
        dm = dm_ref[...].astype(F32)
        sa = jax.nn.sigmoid(ga_ref[...].astype(F32))
        sb = jax.nn.sigmoid(gb_ref[...].astype(F32))
        dua_ref[...] = (dm * sa).astype(BF16)
        dub_ref[...] = (dm * sb).astype(BF16)
        dg_ref[:, 0:d] = (dm * ua_ref[...].astype(F32) * sa * (1.0 - sa)).astype(BF16)
        dg_ref[:, d:2 * d] = (dm * ub_ref[...].astype(F32) * sb * (1.0 - sb)).astype(BF16)

    row = pl.BlockSpec((tm, d), lambda i: (i, 0))
    return pl.pallas_call(
        body, name="mixer_bwd", grid=(t // tm,),
        in_specs=[row, row, row,
                  pl.BlockSpec((tm, d), lambda i: (i, gate_col0 // nc)),
                  pl.BlockSpec((tm, d), lambda i: (i, gate_col0 // nc + 1))],
        out_specs=[row, row, pl.BlockSpec((tm, 2 * d), lambda i: (i, 0))],
        out_shape=[jax.ShapeDtypeStruct((t, d), BF16), jax.ShapeDtypeStruct((t, d), BF16),
                   jax.ShapeDtypeStruct((t, 2 * d), BF16)],
        compiler_params=_params(("parallel",)),
    )(dmix, ua, ub, proj, proj)


def _ffn_up(n, w_gate, w_up):
    t, d = n.shape
    tm = min(ROW_TILE, t)
    n_sh, _, cs = w_gate.shape

    def body(n_ref, wg_ref, wu_ref, hg_ref, hu_ref, act_ref):
        hg = lax.dot_general(n_ref[...], wg_ref[...], NN, preferred_element_type=F32)
        hu = lax.dot_general(n_ref[...], wu_ref[...], NN, preferred_element_type=F32)
        hg_ref[...] = hg.astype(BF16)
        hu_ref[...] = hu.astype(BF16)
        act_ref[...] = (hg * jax.nn.sigmoid(hg) * hu).astype(BF16)

    wsp = pl.BlockSpec((None, d, cs), lambda j, i: (j, 0, 0))
    out = pl.BlockSpec((None, tm, cs), lambda j, i: (j, i, 0))
    osh = jax.ShapeDtypeStruct((n_sh, t, cs), BF16)
    return pl.pallas_call(
        body, name="ffn_up", grid=(n_sh, t // tm),
        in_specs=[pl.BlockSpec((tm, d), lambda j, i: (i, 0)), wsp, wsp],
        out_specs=[out, out, out], out_shape=[osh, osh, osh],
        compiler_params=_params(("parallel", "parallel")),
    )(n, w_gate, w_up)


def _ffn_bwd_act(dh, w_down, hg, hu):
    t, d = dh.shape
    tm = min(ROW_TILE, t)
    n_sh, cs, _ = w_down.shape

    def body(dh_ref, wd_ref, hg_ref, hu_ref, dhg_ref, dhu_ref):
        dact = lax.dot_general(dh_ref[...], wd_ref[...], NT, preferred_element_type=F32)
        hg = hg_ref[...].astype(F32)
        sg = jax.nn.sigmoid(hg)
        dhu_ref[...] = (dact * hg * sg).astype(BF16)
        dhg_ref[...] = (dact * hu_ref[...].astype(F32) * sg * (1.0 + hg * (1.0 - sg))).astype(BF16)

    hid = pl.BlockSpec((None, tm, cs), lambda j, i: (j, i, 0))
    osh = jax.ShapeDtypeStruct((n_sh, t, cs), BF16)
    return pl.pallas_call(
        body, name="ffn_bwd_act", grid=(n_sh, t // tm),
        in_specs=[pl.BlockSpec((tm, d), lambda j, i: (i, 0)), pl.BlockSpec((None, cs, d), lambda j, i: (j, 0, 0)), hid, hid],
        out_specs=[hid, hid], out_shape=[osh, osh],
        compiler_params=_params(("parallel", "parallel")),
    )(dh, w_down, hg, hu)


def _mm_cols(name, a, w, out_dtype=BF16):
    t, k = a.shape
    n_sh, _, cs = w.shape
    tm = min(ROW_TILE, t)
    return _mm(name, a, w, grid=(n_sh, t // tm),
               a_spec=pl.BlockSpec((tm, k), lambda j, i: (i, 0)), b_spec=pl.BlockSpec((None, k, cs), lambda j, i: (j, 0, 0)),
               o_shape=(t, n_sh * cs), o_spec=pl.BlockSpec((tm, cs), lambda j, i: (i, j)), dims=NN, out_dtype=out_dtype)


def _mm_cols_t(name, a, w, out_dtype, res=None):
    t = a.shape[0]
    n_sh, k, cs = w.shape
    tm = min(ROW_TILE, t)
    o_spec = pl.BlockSpec((tm, k), lambda i, j: (i, 0))
    return _mm(name, a, w, grid=(t // tm, n_sh),
               a_spec=pl.BlockSpec((tm, cs), lambda i, j: (i, j)), b_spec=pl.BlockSpec((None, k, cs), lambda i, j: (j, 0, 0)),
               o_shape=(t, k), o_spec=o_spec, dims=NT, out_dtype=out_dtype, nk=n_sh, res=res,
               res_spec=o_spec if res is not None else None)


def _mm_hid_t(name, a, w, out_dtype, res=None):
    n_sh, t, cs = a.shape
    k = w.shape[1]
    tm = min(ROW_TILE, t)
    o_spec = pl.BlockSpec((tm, k), lambda i, j: (i, 0))
    return _mm(name, a, w, grid=(t // tm, n_sh),
               a_spec=pl.BlockSpec((None, tm, cs), lambda i, j: (j, i, 0)),
               b_spec=pl.BlockSpec((None, k, cs), lambda i, j: (j, 0, 0)),
               o_shape=(t, k), o_spec=o_spec, dims=NT, out_dtype=out_dtype, nk=n_sh, res=res,
               res_spec=o_spec if res is not None else None)


def _mm_hid(name, a, w, res):
    n_sh, t, cs = a.shape
    n = w.shape[2]
    tm = min(ROW_TILE, t)
    o_spec = pl.BlockSpec((tm, n), lambda i, j: (i, 0))
    return _mm(name, a, w, grid=(t // tm, n_sh),
               a_spec=pl.BlockSpec((None, tm, cs), lambda i, j: (j, i, 0)),
               b_spec=pl.BlockSpec((None, cs, n), lambda i, j: (j, 0, 0)),
               o_shape=(t, n), o_spec=o_spec, dims=NN, out_dtype=F32, nk=n_sh, res=res, res_spec=o_spec)


def _mm_full(name, a, w, out_dtype, dims=NN, res=None):
    t, k = a.shape
    n = w.shape[1] if dims == NN else w.shape[0]
    tm = min(ROW_TILE, t)
    o_spec = pl.BlockSpec((tm, n), lambda i: (i, 0))
    return _mm(name, a, w, grid=(t // tm,),
               a_spec=pl.BlockSpec((tm, k), lambda i: (i, 0)), b_spec=pl.BlockSpec(w.shape, lambda i: (0, 0)),
               o_shape=(t, n), o_spec=o_spec, dims=dims, out_dtype=out_dtype, res=res,
               res_spec=o_spec if res is not None else None)


def _wgrad_cols(name, a, g, n_sh):
    t, k = a.shape
    cs = g.shape[1] // n_sh
    tm = min(ROW_TILE, t)
    return _mm(name, a, g, grid=(n_sh, t // tm),
               a_spec=pl.BlockSpec((tm, k), lambda j, r: (r, 0)), b_spec=pl.BlockSpec((tm, cs), lambda j, r: (r, j)),
               o_shape=(n_sh, k, cs), o_spec=pl.BlockSpec((None, k, cs), lambda j, r: (j, 0, 0)), dims=TN,
               out_dtype=BF16, nk=t // tm)


def _wgrad_hid_cols(name, a, g):
    t, k = a.shape
    n_sh, _, cs = g.shape
    tm = min(ROW_TILE, t)
    return _mm(name, a, g, grid=(n_sh, t // tm),
               a_spec=pl.BlockSpec((tm, k), lambda j, r: (r, 0)), b_spec=pl.BlockSpec((None, tm, cs), lambda j, r: (j, r, 0)),
               o_shape=(n_sh, k, cs), o_spec=pl.BlockSpec((None, k, cs), lambda j, r: (j, 0, 0)), dims=TN,
               out_dtype=BF16, nk=t // tm)


def _wgrad_hid_rows(name, a, g):
    n_sh, t, cs = a.shape
    n = g.shape[1]
    tm = min(ROW_TILE, t)
    return _mm(name, a, g, grid=(n_sh, t // tm),
               a_spec=pl.BlockSpec((None, tm, cs), lambda j, r: (j, r, 0)), b_spec=pl.BlockSpec((tm, n), lambda j, r: (r, 0)),
               o_shape=(n_sh, cs, n), o_spec=pl.BlockSpec((None, cs, n), lambda j, r: (j, 0, 0)), dims=TN,
               out_dtype=BF16, nk=t // tm)


def _wgrad_rows(name, a, g, n_sh):
    t, k = a.shape
    n = g.shape[1]
    rs = k // n_sh
    tm = min(ROW_TILE, t)
    return _mm(name, a, g, grid=(n_sh, t // tm),
               a_spec=pl.BlockSpec((tm, rs), lambda j, r: (r, j)), b_spec=pl.BlockSpec((tm, n), lambda j, r: (r, 0)),
               o_shape=(n_sh, rs, n), o_spec=pl.BlockSpec((None, rs, n), lambda j, r: (j, 0, 0)), dims=TN,
               out_dtype=BF16, nk=t // tm)


def _peers():
    x, y, c = lax.axis_index("x"), lax.axis_index("y"), lax.axis_index("c")
    me = 4 * x + 2 * y + c
    out = []
    for k in range(1, N_DEV):
        kx, ky, kc = (k >> 2) & 1, (k >> 1) & 1, k & 1
        px = 1 - x if kx else x
        py = 1 - y if ky else y
        pc = 1 - c if kc else c
        out.append(((px, py, pc), 4 * px + 2 * py + pc))
    return me, out


def _cast_weights(ws):
    def body(*refs):
        n = len(refs) // 2
        for i_ref, o_ref in zip(refs[:n], refs[n:]):
            o_ref[...] = i_ref[...].astype(BF16)

    return pl.pallas_call(
        body, name="cast_weights", in_specs=[VMEM] * len(ws), out_specs=[VMEM] * len(ws),
        out_shape=[jax.ShapeDtypeStruct(w.shape, BF16) for w in ws],
    )(*ws)


def _exchange(name, arrs, gather):
    n = len(arrs)
    n_peer = N_DEV - 1

    def body(*refs):
        ins, outs = refs[:n], refs[n:2 * n]
        send_sems, recv_sems, loc_sems = refs[2 * n:]
        me, peers = _peers()
        started = []
        for w in range(n):
            src_me = ins[w] if gather else ins[w].at[me]
            loc = pltpu.make_async_copy(src_me, outs[w].at[me], loc_sems.at[w])
            loc.start()
            started.append(loc)
        for k, (dev, idx) in enumerate(peers):
            for w in range(n):
                cp = pltpu.make_async_remote_copy(
                    src_ref=ins[w] if gather else ins[w].at[idx], dst_ref=outs[w].at[me],
                    send_sem=send_sems.at[w * n_peer + k], recv_sem=recv_sems.at[w * n_peer + k],
                    device_id=dev, device_id_type=pl.DeviceIdType.MESH)
                cp.start()
        for loc in started:
            loc.wait()
        for k, (dev, idx) in enumerate(peers):
            for w in range(n):
                cp = pltpu.make_async_remote_copy(
                    src_ref=ins[w] if gather else ins[w].at[idx], dst_ref=outs[w].at[idx],
                    send_sem=send_sems.at[w * n_peer + k], recv_sem=recv_sems.at[w * n_peer + k],
                    device_id=dev, device_id_type=pl.DeviceIdType.MESH)
                cp.wait_send()
                cp.wait_recv()

    out_shape = [jax.ShapeDtypeStruct(((N_DEV,) + a.shape) if gather else a.shape, a.dtype) for a in arrs]
    return pl.pallas_call(
        body, name=name, in_specs=[ANY] * n, out_specs=[ANY] * n, out_shape=out_shape,
        scratch_shapes=[pltpu.SemaphoreType.DMA((n * n_peer,)), pltpu.SemaphoreType.DMA((n * n_peer,)),
                        pltpu.SemaphoreType.DMA((n,))],
    )(*arrs)


def _allreduce_small(v):
    def body(v_ref, o_ref, all_ref, send_sems, recv_sems):
        me, peers = _peers()
        all_ref[me] = v_ref[...]
        for k, (dev, idx) in enumerate(peers):
            pltpu.make_async_remote_copy(src_ref=v_ref, dst_ref=all_ref.at[me], send_sem=send_sems.at[k],
                                         recv_sem=recv_sems.at[k], device_id=dev,
                                         device_id_type=pl.DeviceIdType.MESH).start()
        for k, (dev, idx) in enumerate(peers):
            cp = pltpu.make_async_remote_copy(src_ref=v_ref, dst_ref=all_ref.at[idx], send_sem=send_sems.at[k],
                                              recv_sem=recv_sems.at[k], device_id=dev,
                                              device_id_type=pl.DeviceIdType.MESH)
            cp.wait_send()
            cp.wait_recv()
        tot = all_ref[0]
        for dvc in range(1, N_DEV):
            tot = tot + all_ref[dvc]
        o_ref[...] = tot

    return pl.pallas_call(
        body, name="allreduce_small", in_specs=[VMEM], out_specs=VMEM,
        out_shape=jax.ShapeDtypeStruct(v.shape, F32),
        scratch_shapes=[pltpu.VMEM((N_DEV,) + v.shape, F32), pltpu.SemaphoreType.DMA((N_DEV - 1,)),
                        pltpu.SemaphoreType.DMA((N_DEV - 1,))],
    )(v)


def _adam_math(g, w, m, v):
    m_new = ADAM_B1 * m + (1.0 - ADAM_B1) * g
    v_new = ADAM_B2 * v + (1.0 - ADAM_B2) * (g * g)
    m_hat = m_new / (1.0 - ADAM_B1 ** ADAM_STEP)
    v_hat = v_new / (1.0 - ADAM_B2 ** ADAM_STEP)
    delta = -ADAM_LR * (m_hat / (jnp.sqrt(v_hat) + ADAM_EPS) + ADAM_WD * w)
    return delta, m_new, v_new


def _adam(name, pieces, w, m, v):
    r, c = w.shape
    tr = r
    for cand in (256, 176, 128, 64):
        if r % cand == 0 and r > cand:
            tr = cand
            break

    def body(p_ref, w_ref, m_ref, v_ref, g_ref, d_ref, mo_ref, vo_ref):
        g = p_ref[0].astype(F32)
        for dvc in range(1, N_DEV):
            g = g + p_ref[dvc].astype(F32)
        delta, m_new, v_new = _adam_math(g, w_ref[...], m_ref[...], v_ref[...])
        g_ref[...] = g
        d_ref[...] = delta
        mo_ref[...] = m_new
        vo_ref[...] = v_new

    blk = pl.BlockSpec((tr, c), lambda i: (i, 0))
    osh = jax.ShapeDtypeStruct((r, c), F32)
    return pl.pallas_call(
        body, name=name, grid=(r // tr,),
        in_specs=[pl.BlockSpec((N_DEV, tr, c), lambda i: (0, i, 0)), blk, blk, blk],
        out_specs=[blk, blk, blk, blk], out_shape=[osh, osh, osh, osh],
        compiler_params=_params(("parallel",)),
    )(pieces, w, m, v)


def _adam_small(g, w, m, v):
    def body(g_ref, w_ref, m_ref, v_ref, d_ref, mo_ref, vo_ref):
        delta, m_new, v_new = _adam_math(g_ref[...], w_ref[...], m_ref[...], v_ref[...])
        d_ref[...] = delta
        mo_ref[...] = m_new
        vo_ref[...] = v_new

    osh = jax.ShapeDtypeStruct(g.shape, F32)
    return pl.pallas_call(body, name="adam_small", in_specs=[VMEM] * 4, out_specs=[VMEM] * 3,
                          out_shape=[osh, osh, osh])(g, w, m, v)


def _local_step(x, mem, pos, tgt, gains, wts, batch):
    g_mix, g_mem_q, g_mem_kv, g_ffn, g_final = gains
    w_in, w_up_a, w_up_b, w_out, w_q, w_kv, w_o, w_fg, w_fu, w_fd = wts
    t, d = x.shape
    s = t // batch
    n_mem = mem.shape[0] // batch
    n_sh = N_DEV
    width = w_up_a.shape[1]
    nb = width // LANES
    w_out_m = w_out.reshape(d, d)
    w_q_m = w_q.reshape(d, -1)
    w_kv_m = w_kv.reshape(d, -1)

    lane = jnp.arange(LANES, dtype=jnp.int32) % HEAD_DIM
    sel_lo = (lane < ROPE_HALF).astype(F32)[None, :]
    sel_hi = ((lane >= ROPE_HALF) & (lane < 2 * ROPE_HALF)).astype(F32)[None, :]
    freqs = ROPE_THETA ** (-jnp.arange(ROPE_HALF, dtype=F32) / ROPE_HALF)
    inv_freq = jnp.where(lane < 2 * ROPE_HALF, freqs[lane % ROPE_HALF], 0.0)[None, :]
    cos_t, sin_a, sin_b = _rope_tables(pos, inv_freq, sel_lo, sel_hi)
    bias = _dilated_bias(s)

    n1 = _rms_fwd("norm_mix", x, g_mix)
    proj = _mm_cols("proj_in", n1, w_in)
    qk_a = _rope_apply("rope_fwd", proj, 0, 2 * nb, cos_t, sin_a, sin_b, 1.0)
    o_a, lse_a = _attn_a_fwd(qk_a, proj, 2 * nb, bias, batch, s)
    o_b, runs_b = _attn_b_fwd(proj, 3 * nb, 4 * nb, 5 * nb, batch, s)
    ua, ub, mixed = _mixer_fwd(o_a, o_b, w_up_a, w_up_b, proj, 6 * nb)
    h1 = _mm_full("mix_out", mixed, w_out_m, F32, res=x)
    n2 = _rms_fwd("norm_mem_q", h1, g_mem_q)
    mem_n = _rms_fwd("norm_mem_kv", mem, g_mem_kv)
    q_m = _mm_full("mem_q", n2, w_q_m, BF16)
    kv_m = _mm_full("mem_kv", mem_n, w_kv_m, BF16)
    o_m = _mem_fwd(q_m, kv_m, batch, s, n_mem)
    h2 = _mm_cols_like_res("mem_out", o_m, w_o, h1)
    n3 = _rms_fwd("norm_ffn", h2, g_ffn)
    hg, hu, act = _ffn_up(n3, w_fg, w_fu)
    h3 = _mm_hid("ffn_down", act, w_fd, h2)
    loss_part, dh3, dg_final = _loss_head(h3, tgt, g_final.reshape(1, d))

    dh3_b = _to_bf16("dh3_bf16", dh3)
    dhg, dhu = _ffn_bwd_act(dh3_b, w_fd, hg, hu)
    gw_fd = _wgrad_hid_rows("gw_ffn_down", act, dh3_b)
    gw_fg = _wgrad_hid_cols("gw_ffn_gate", n3, dhg)
    gw_fu = _wgrad_hid_cols("gw_ffn_up", n3, dhu)
    dn3 = _mm_hid_t("dn_ffn_gate", dhg, w_fg, F32)
    dn3 = _mm_hid_t("dn_ffn_up", dhu, w_fu, F32, res=dn3)
    dh2, dg_ffn = _rms_bwd("norm_ffn_bwd", dn3, h2, g_ffn, dh3)

    dh2_b = _to_bf16("dh2_bf16", dh2)
    do_m = _mm_cols_t("mem_out_bwd", dh2_b, w_o, BF16)
    gw_o = _wgrad_cols("gw_mem_o", o_m, dh2_b, n_sh)
    dq_m, dkv_m = _mem_bwd(q_m, kv_m, do_m, batch, s, n_mem)
    gw_q = _wgrad_rows("gw_mem_q", n2, dq_m, n_sh)
    gw_kv = _wgrad_rows("gw_mem_kv", mem_n, dkv_m, n_sh)
    dn2 = _mm_full("mem_q_bwd", dq_m, w_q_m, F32, dims=NT)
    dmem_n = _mm_full("mem_kv_bwd", dkv_m, w_kv_m, F32, dims=NT)
    _, dg_mem_kv = _rms_bwd("norm_mem_kv_bwd", dmem_n, mem, g_mem_kv, None)
    dh1, dg_mem_q = _rms_bwd("norm_mem_q_bwd", dn2, h1, g_mem_q, dh2)

    dh1_b = _to_bf16("dh1_bf16", dh1)
    dmix = _mm_full("mix_out_bwd", dh1_b, w_out_m, BF16, dims=NT)
    gw_out = _wgrad_rows("gw_out", mixed, dh1_b, n_sh)
    dua, dub, dgates = _mixer_bwd(dmix, ua, ub, proj, 6 * nb)
    do_a = _mm_cols_t("up_a_bwd", dua, w_up_a, BF16)
    do_b = _mm_cols_t("up_b_bwd", dub, w_up_b, BF16)
    gw_ua = _wgrad_cols("gw_up_a", o_a, dua, n_sh)
    gw_ub = _wgrad_cols("gw_up_b", o_b, dub, n_sh)
    dq_ar, dk_ar, dv_a = _attn_a_bwd(qk_a, proj, 2 * nb, bias, o_a, lse_a, do_a, batch, s)
    dqk_a = _rope_apply("rope_bwd", jnp.concatenate([dq_ar, dk_ar], axis=1), 0, 2 * nb, cos_t, sin_a, sin_b, -1.0)
    dq_b, dk_b, dv_b = _attn_b_bwd(proj, 3 * nb, 4 * nb, 5 * nb, runs_b, do_b, batch, s)
    dproj = jnp.concatenate([dqk_a, dv_a, dq_b, dk_b, dv_b, dgates], axis=1)
    gw_in = _wgrad_cols("gw_in", n1, dproj, n_sh)
    dn1 = _mm_cols_t("proj_in_bwd", dproj, w_in, F32)
    grad_x, dg_mix = _rms_bwd("norm_mix_bwd", dn1, x, g_mix, dh1)

    grads = (gw_in, gw_ua, gw_ub, gw_out.reshape(w_out.shape), gw_q.reshape(w_q.shape), gw_kv.reshape(w_kv.shape), gw_o,
             gw_fg, gw_fu, gw_fd)
    return loss_part, grad_x, grads, (dg_mix, dg_mem_q, dg_mem_kv, dg_ffn, dg_final)


def _mm_cols_like_res(name, a, w, res):
    t, k = a.shape
    n_sh, _, cs = w.shape
    tm = min(ROW_TILE, t)
    o_spec = pl.BlockSpec((tm, cs), lambda j, i: (i, j))
    return _mm(name, a, w, grid=(n_sh, t // tm),
               a_spec=pl.BlockSpec((tm, k), lambda j, i: (i, 0)), b_spec=pl.BlockSpec((None, k, cs), lambda j, i: (j, 0, 0)),
               o_shape=(t, n_sh * cs), o_spec=o_spec, dims=NN, out_dtype=F32, res=res, res_spec=o_spec)


def _to_bf16(name, a):
    t, d = a.shape
    tm = min(ROW_TILE, t)

    def body(a_ref, o_ref):
        o_ref[...] = a_ref[...].astype(BF16)

    row = pl.BlockSpec((tm, d), lambda i: (i, 0))
    return pl.pallas_call(body, name=name, grid=(t // tm,), in_specs=[row], out_specs=row,
                          out_shape=jax.ShapeDtypeStruct((t, d), BF16), compiler_params=_params(("parallel",)))(a)


WEIGHTS = ("w_in", "w_up_a", "w_up_b", "w_out", "w_q_mem", "w_kv_mem", "w_o_mem", "w_ffn_gate", "w_ffn_up", "w_ffn_down")
GAINS = ("g_mix", "g_mem_q", "g_mem_kv", "g_ffn", "g_final")
ORDER = ("g_mix", "w_in", "w_up_a", "w_up_b", "w_out", "g_mem_q", "g_mem_kv", "w_q_mem", "w_kv_mem", "w_o_mem", "g_ffn",
         "w_ffn_gate", "w_ffn_up", "w_ffn_down", "g_final")


def kernel(x, mem, positions, g_mix, w_in, w_up_a, w_up_b, w_out, g_mem_q, g_mem_kv, w_q_mem, w_kv_mem, w_o_mem, g_ffn, w_ffn_gate, w_ffn_up, w_ffn_down, g_final, loss_target, m_g_mix, m_w_in, m_w_up_a, m_w_up_b, m_w_out, m_g_mem_q, m_g_mem_kv, m_w_q_mem, m_w_kv_mem, m_w_o_mem, m_g_ffn, m_w_ffn_gate, m_w_ffn_up, m_w_ffn_down, m_g_final, v_g_mix, v_w_in, v_w_up_a, v_w_up_b, v_w_out, v_g_mem_q, v_g_mem_kv, v_w_q_mem, v_w_kv_mem, v_w_o_mem, v_g_ffn, v_w_ffn_gate, v_w_ffn_up, v_w_ffn_down, v_g_final):
    given = dict(locals())
    batch, s, d = x.shape
    t = batch * s
    shard = {n: given[n].reshape(given[n].shape[-2:]) for n in WEIGHTS}
    gains = [given[n].reshape(1, d) for n in GAINS]

    cast = _cast_weights([shard[n] for n in WEIGHTS])
    stacked = _exchange("gather_weights", cast, True)
    loss_part, grad_x, grads, dgains = _local_step(
        x.reshape(t, d), mem.reshape(-1, d), positions.reshape(t, 1), loss_target.reshape(t, d), gains, stacked, batch)
    pieces = _exchange("scatter_grads", list(grads), False)

    grad, delta, new_m, new_v = {}, {}, {}, {}
    for n, p in zip(WEIGHTS, pieces):
        m2, v2 = given["m_" + n].reshape(shard[n].shape), given["v_" + n].reshape(shard[n].shape)
        outs = _adam("adam_" + n, p, shard[n], m2, v2)
        grad[n], delta[n], new_m[n], new_v[n] = [o.reshape(given[n].shape) for o in outs]

    rows = jnp.concatenate(list(dgains) + [jnp.zeros((N_DEV - len(GAINS), d), F32)], axis=0)
    g_all = _allreduce_small(rows)
    w_all = jnp.concatenate(gains + [jnp.zeros((N_DEV - len(GAINS), d), F32)], axis=0)
    m_all = jnp.concatenate([given["m_" + n].reshape(1, d) for n in GAINS] + [jnp.zeros((N_DEV - len(GAINS), d), F32)], axis=0)
    v_all = jnp.concatenate([given["v_" + n].reshape(1, d) for n in GAINS] + [jnp.ones((N_DEV - len(GAINS), d), F32)], axis=0)
    d_all, mo_all, vo_all = _adam_small(g_all, w_all, m_all, v_all)
    for i, n in enumerate(GAINS):
        grad[n] = g_all[i].reshape(given[n].shape)
        delta[n] = d_all[i].reshape(given[n].shape)
        new_m[n] = mo_all[i].reshape(given[n].shape)
        new_v[n] = vo_all[i].reshape(given[n].shape)

    loss = lax.psum(loss_part[0, 0], ("x", "y", "c"))
    return (loss, grad_x.reshape(x.shape), *[grad[n] for n in ORDER], *[delta[n] for n in ORDER],
            *[new_m[n] for n in ORDER], *[new_v[n] for n in ORDER])
```

```python
import functools
import math

import jax
import jax.numpy as jnp
from jax import lax
from jax.experimental import pallas as pl
from jax.experimental.pallas import tpu as pltpu

F32 = jnp.float32
BF16 = jnp.bfloat16

N_DEV = 8
HEAD_DIM = 64
MEM_HEAD_DIM = 128
N_HEADS_MEM = 4
BLOCK = 128
DIL_PATTERNS = ((128, 1), (512, 4), (2048, 16))
ROPE_THETA = 500000.0
ROPE_HALF = 8
RMS_EPS = 1e-6
ADAM_LR, ADAM_B1, ADAM_B2, ADAM_EPS, ADAM_WD, ADAM_STEP = 0.001, 0.9, 0.999, 1e-08, 0.01, 10
NEG = -1e30
ROW_TILE = 512
LANES = 128

ANY = pl.BlockSpec(memory_space=pl.ANY)
VMEM = pl.BlockSpec(memory_space=pltpu.VMEM)
NN = (((1,), (0,)), ((), ()))
NT = (((1,), (1,)), ((), ()))
TN = (((0,), (0,)), ((), ()))


def _params(sem):
    return pltpu.CompilerParams(dimension_semantics=sem)


def _mm(name, a, b, *, grid, a_spec, b_spec, o_shape, o_spec, dims, out_dtype, nk=1, res=None, res_spec=None):
    has_res = res is not None

    def body(*refs):
        a_ref, b_ref = refs[0], refs[1]
        r_ref = refs[2] if has_res else None
        o_ref = refs[3] if has_res else refs[2]
        p = lax.dot_general(a_ref[...], b_ref[...], dims, preferred_element_type=F32)
        if nk == 1:
            if has_res:
                p = p + r_ref[...].astype(F32)
            o_ref[...] = p.astype(out_dtype)
            return
        acc_ref = refs[-1]
        k = pl.program_id(len(grid) - 1)

        @pl.when(k == 0)
        def _():
            acc_ref[...] = p

        @pl.when(k > 0)
        def _():
            acc_ref[...] += p

        @pl.when(k == nk - 1)
        def _():
            t = acc_ref[...]
            if has_res:
                t = t + r_ref[...].astype(F32)
            o_ref[...] = t.astype(out_dtype)

    o_block = tuple(d for d in o_spec.block_shape if d is not None)
    sem = ("parallel",) * (len(grid) - 1) + (("arbitrary",) if nk > 1 else ("parallel",))
    return pl.pallas_call(
        body, name=name, grid=grid,
        in_specs=[a_spec, b_spec] + ([res_spec] if has_res else []),
        out_specs=o_spec, out_shape=jax.ShapeDtypeStruct(o_shape, out_dtype),
        scratch_shapes=[pltpu.VMEM(o_block, F32)] if nk > 1 else [],
        compiler_params=_params(sem),
    )(*([a, b] + ([res] if has_res else [])))


def _rms_fwd(name, x, g):
    t, d = x.shape
    tm = min(ROW_TILE, t)

    def body(x_ref, g_ref, o_ref):
        xf = x_ref[...]
        r = lax.rsqrt(jnp.mean(xf * xf, axis=-1, keepdims=True) + RMS_EPS)
        o_ref[...] = (xf * r * g_ref[...]).astype(BF16)

    return pl.pallas_call(
        body, name=name, grid=(t // tm,),
        in_specs=[pl.BlockSpec((tm, d), lambda i: (i, 0)), pl.BlockSpec((1, d), lambda i: (0, 0))],
        out_specs=pl.BlockSpec((tm, d), lambda i: (i, 0)), out_shape=jax.ShapeDtypeStruct((t, d), BF16),
        compiler_params=_params(("parallel",)),
    )(x, g)


def _rms_bwd(name, dn, x, g, dres):
    t, d = x.shape
    tm = min(ROW_TILE, t)
    has_res = dres is not None

    def body(*refs):
        dn_ref, x_ref, g_ref = refs[0], refs[1], refs[2]
        r_ref = refs[3] if has_res else None
        dx_ref, dg_ref = refs[-2], refs[-1]
        xf = x_ref[...]
        r = lax.rsqrt(jnp.mean(xf * xf, axis=-1, keepdims=True) + RMS_EPS)
        xh = xf * r
        dnf = dn_ref[...].astype(F32)
        dxh = dnf * g_ref[...]
        dx = r * (dxh - xh * jnp.mean(dxh * xh, axis=-1, keepdims=True))
        if has_res:
            dx = dx + r_ref[...]
        dx_ref[...] = dx

        @pl.when(pl.program_id(0) == 0)
        def _():
            dg_ref[...] = jnp.zeros_like(dg_ref)

        dg_ref[...] += jnp.sum(dnf * xh, axis=0, keepdims=True)

    row = pl.BlockSpec((tm, d), lambda i: (i, 0))
    vec = pl.BlockSpec((1, d), lambda i: (0, 0))
    return pl.pallas_call(
        body, name=name, grid=(t // tm,),
        in_specs=[row, row, vec] + ([row] if has_res else []),
        out_specs=[row, vec],
        out_shape=[jax.ShapeDtypeStruct((t, d), F32), jax.ShapeDtypeStruct((1, d), F32)],
        compiler_params=_params(("arbitrary",)),
    )(*([dn, x, g] + ([dres] if has_res else [])))


def _loss_head(h, tgt, g):
    t, d = h.shape
    tm = min(ROW_TILE, t)

    def body(h_ref, t_ref, g_ref, loss_ref, dh_ref, dg_ref):
        xf = h_ref[...]
        gv = g_ref[...]
        r = lax.rsqrt(jnp.mean(xf * xf, axis=-1, keepdims=True) + RMS_EPS)
        xh = xf * r
        e = xh * gv - t_ref[...]
        dy = e * (1.0 / d)
        dxh = dy * gv
        dh_ref[...] = r * (dxh - xh * jnp.mean(dxh * xh, axis=-1, keepdims=True))

        @pl.when(pl.program_id(0) == 0)
        def _():
            dg_ref[...] = jnp.zeros_like(dg_ref)
            loss_ref[...] = jnp.zeros_like(loss_ref)

        dg_ref[...] += jnp.sum(dy * xh, axis=0, keepdims=True)
        part = jnp.sum(jnp.sum(e * e, axis=1, keepdims=True), axis=0, keepdims=True) * (0.5 / d)
        loss_ref[...] += jnp.broadcast_to(part, loss_ref.shape)

    row = pl.BlockSpec((tm, d), lambda i: (i, 0))
    vec = pl.BlockSpec((1, d), lambda i: (0, 0))
    return pl.pallas_call(
        body, name="loss_head", grid=(t // tm,),
        in_specs=[row, row, vec],
        out_specs=[pl.BlockSpec((8, LANES), lambda i: (0, 0)), row, vec],
        out_shape=[jax.ShapeDtypeStruct((8, LANES), F32), jax.ShapeDtypeStruct((t, d), F32),
                   jax.ShapeDtypeStruct((1, d), F32)],
        compiler_params=_params(("arbitrary",)),
    )(h, tgt, g)


def _rope_tables(pos, inv_freq, sel_lo, sel_hi):
    t = pos.shape[0]
    tm = min(ROW_TILE, t)

    def body(p_ref, f_ref, lo_ref, hi_ref, c_ref, sa_ref, sb_ref):
        ang = p_ref[...].astype(F32) * f_ref[...]
        rot = lo_ref[...] + hi_ref[...]
        cs, sn = jnp.cos(ang), jnp.sin(ang)
        c_ref[...] = cs * rot + (1.0 - rot)
        sa_ref[...] = -sn * lo_ref[...]
        sb_ref[...] = sn * hi_ref[...]

    vec = pl.BlockSpec((1, LANES), lambda i: (0, 0))
    row = pl.BlockSpec((tm, LANES), lambda i: (i, 0))
    return pl.pallas_call(
        body, name="rope_tables", grid=(t // tm,),
        in_specs=[pl.BlockSpec((tm, 1), lambda i: (i, 0)), vec, vec, vec],
        out_specs=[row, row, row], out_shape=[jax.ShapeDtypeStruct((t, LANES), F32)] * 3,
        compiler_params=_params(("parallel",)),
    )(pos, inv_freq, sel_lo, sel_hi)


def _rope_apply(name, src, col0, n_cols, cos_t, sin_a, sin_b, sign):
    t = src.shape[0]
    tm = min(ROW_TILE, t)

    def body(x_ref, c_ref, sa_ref, sb_ref, o_ref):
        xf = x_ref[...].astype(F32)
        up = pltpu.roll(xf, LANES - ROPE_HALF, 1)
        dn = pltpu.roll(xf, ROPE_HALF, 1)
        o_ref[...] = (xf * c_ref[...] + sign * (up * sa_ref[...] + dn * sb_ref[...])).astype(BF16)

    tab = pl.BlockSpec((tm, LANES), lambda i, c: (i, 0))
    return pl.pallas_call(
        body, name=name, grid=(t // tm, n_cols),
        in_specs=[pl.BlockSpec((tm, LANES), lambda i, c: (i, col0 + c)), tab, tab, tab],
        out_specs=pl.BlockSpec((tm, LANES), lambda i, c: (i, c)),
        out_shape=jax.ShapeDtypeStruct((t, n_cols * LANES), BF16),
        compiler_params=_params(("parallel", "parallel")),
    )(src, cos_t, sin_a, sin_b)


DA_T = 256


def _lane_lo():
    return lax.broadcasted_iota(jnp.int32, (BLOCK, LANES), 1) < HEAD_DIM


def _dilated_bias_tiles(s):
    n = s // DA_T
    dist = (jnp.arange(n, dtype=jnp.int32)[:, None, None] * DA_T
            + jnp.arange(DA_T, dtype=jnp.int32)[None, :, None] - jnp.arange(DA_T, dtype=jnp.int32)[None, None, :])
    cnt = jnp.zeros(dist.shape, F32)
    for window, dil in DIL_PATTERNS:
        cnt = cnt + ((dist >= 0) & (dist % dil == 0) & (dist <= window)).astype(F32)
    return jnp.where(cnt > 0, jnp.log(jnp.maximum(cnt, 1.0)), NEG)


def _stack_heads(x, lo):
    zero = jnp.zeros_like(x)
    return jnp.concatenate([jnp.where(lo, x, zero), jnp.where(lo, zero, x)], axis=0)


def _da_fwd(qk, proj, v_col0, bias, batch, s):
    t = qk.shape[0]
    nq = s // DA_T
    n_pairs = 4
    scale = HEAD_DIM ** -0.5

    def body(q_ref, k_ref, v_ref, b_ref, o_ref, lse_ref, acc_ref, m_ref, l_ref):
        i = pl.program_id(2)
        lo = lax.broadcasted_iota(jnp.int32, (DA_T, LANES), 1) < HEAD_DIM
        qq = _stack_heads(q_ref[...] * scale, lo)
        ones = jnp.ones((DA_T, LANES), BF16)
        acc_ref[...] = jnp.zeros_like(acc_ref)
        m_ref[...] = jnp.full(m_ref.shape, NEG, F32)
        l_ref[...] = jnp.zeros_like(l_ref)

        def unit(dlt, carry):
            rows = pl.ds(pl.multiple_of((i - dlt) * DA_T, DA_T), DA_T)
            k = k_ref[rows, :]
            v = v_ref[rows, :]
            vz = jnp.zeros_like(v)
            bias_t = b_ref[dlt]
            sc = lax.dot_general(qq, k, NT, preferred_element_type=F32) + jnp.concatenate([bias_t, bias_t], axis=0)
            m_old = m_ref[...]
            m_new = jnp.maximum(m_old, jnp.max(sc, axis=1, keepdims=True))
            p = jnp.exp(sc - m_new).astype(BF16)
            alpha = jnp.exp(m_old - m_new)
            m_ref[...] = m_new
            l_ref[...] = alpha * l_ref[...] + lax.dot_general(p, ones, NN, preferred_element_type=F32)
            pv = (lax.dot_general(p[:DA_T], jnp.where(lo, v, vz), NN, preferred_element_type=F32)
                  + lax.dot_general(p[DA_T:], jnp.where(lo, vz, v), NN, preferred_element_type=F32))
            acc_ref[...] = acc_ref[...] * jnp.where(lo, alpha[:DA_T], alpha[DA_T:]) + pv
            return carry

        lax.fori_loop(0, i + 1, unit, 0)
        l_t = l_ref[...]
        o_ref[...] = (acc_ref[...] / jnp.where(lo, l_t[:DA_T], l_t[DA_T:])).astype(BF16)
        lse = m_ref[...] + jnp.log(l_t)
        lse_ref[...] = jnp.where(lo, lse[:DA_T], lse[DA_T:])

    blk = pl.BlockSpec((DA_T, LANES), lambda b, h, i: (b * nq + i, h))
    return pl.pallas_call(
        body, name="attn_a_fwd", grid=(batch, n_pairs, nq),
        in_specs=[blk,
                  pl.BlockSpec((s, LANES), lambda b, h, i: (b, n_pairs + h)),
                  pl.BlockSpec((s, LANES), lambda b, h, i: (b, v_col0 + h)),
                  pl.BlockSpec((nq, DA_T, DA_T), lambda b, h, i: (0, 0, 0))],
        out_specs=[blk, blk],
        out_shape=[jax.ShapeDtypeStruct((t, n_pairs * LANES), BF16), jax.ShapeDtypeStruct((t, n_pairs * LANES), F32)],
        scratch_shapes=[pltpu.VMEM((DA_T, LANES), F32), pltpu.VMEM((2 * DA_T, 1), F32), pltpu.VMEM((2 * DA_T, LANES), F32)],
        compiler_params=_params(("parallel", "parallel", "arbitrary")),
    )(qk, qk, proj, bias)


def _da_bwd(qk, proj, v_col0, bias, o, lse, do, batch, s):
    t = qk.shape[0]
    nq = s // DA_T
    n_pairs = 4
    scale = HEAD_DIM ** -0.5

    def body(q_ref, k_ref, v_ref, b_ref, o_ref, lse_ref, do_ref, dq_ref, dk_ref, dv_ref, dk_acc, dv_acc, dq_acc):
        i = pl.program_id(2)
        lo = lax.broadcasted_iota(jnp.int32, (DA_T, LANES), 1) < HEAD_DIM

        @pl.when(i == 0)
        def _():
            dk_acc[...] = jnp.zeros_like(dk_acc)
            dv_acc[...] = jnp.zeros_like(dv_acc)

        do_ = do_ref[...]
        qq = _stack_heads(q_ref[...] * scale, lo)
        dd = _stack_heads(do_, lo)
        prod = do_.astype(F32) * o_ref[...].astype(F32)
        fz = jnp.zeros_like(prod)
        delta = jnp.concatenate([jnp.sum(jnp.where(lo, prod, fz), axis=1, keepdims=True),
                                 jnp.sum(jnp.where(lo, fz, prod), axis=1, keepdims=True)], axis=0)
        lse_t = lse_ref[...]
        lse2 = jnp.concatenate([lse_t[:, 0:1], lse_t[:, HEAD_DIM:HEAD_DIM + 1]], axis=0)
        dq_acc[...] = jnp.zeros_like(dq_acc)

        def unit(dlt, carry):
            rows = pl.ds(pl.multiple_of((i - dlt) * DA_T, DA_T), DA_T)
            k = k_ref[rows, :]
            v = v_ref[rows, :]
            kz = jnp.zeros_like(k)
            bias_t = b_ref[dlt]
            sc = lax.dot_general(qq, k, NT, preferred_element_type=F32) + jnp.concatenate([bias_t, bias_t], axis=0)
            p = jnp.exp(sc - lse2)
            dp = lax.dot_general(dd, v, NT, preferred_element_type=F32)
            ds = (p * (dp - delta)).astype(BF16)
            dq_acc[...] += (lax.dot_general(ds[:DA_T], jnp.where(lo, k, kz), NN, preferred_element_type=F32)
                            + lax.dot_general(ds[DA_T:], jnp.where(lo, kz, k), NN, preferred_element_type=F32))
            dk_acc[rows, :] += lax.dot_general(ds, qq, TN, preferred_element_type=F32)
            dv_acc[rows, :] += lax.dot_general(p.astype(BF16), dd, TN, preferred_element_type=F32)
            return carry

        lax.fori_loop(0, i + 1, unit, 0)
        dq_ref[...] = (dq_acc[...] * scale).astype(BF16)

        @pl.when(i == nq - 1)
        def _():
            dk_ref[...] = dk_acc[...].astype(BF16)
            dv_ref[...] = dv_acc[...].astype(BF16)

    blk = pl.BlockSpec((DA_T, LANES), lambda b, h, i: (b * nq + i, h))
    seq = pl.BlockSpec((s, LANES), lambda b, h, i: (b, h))
    out = jax.ShapeDtypeStruct((t, n_pairs * LANES), BF16)
    return pl.pallas_call(
        body, name="attn_a_bwd", grid=(batch, n_pairs, nq),
        in_specs=[blk,
                  pl.BlockSpec((s, LANES), lambda b, h, i: (b, n_pairs + h)),
                  pl.BlockSpec((s, LANES), lambda b, h, i: (b, v_col0 + h)),
                  pl.BlockSpec((nq, DA_T, DA_T), lambda b, h, i: (0, 0, 0)),
                  blk, blk, blk],
        out_specs=[blk, seq, seq], out_shape=[out, out, out],
        scratch_shapes=[pltpu.VMEM((s, LANES), F32), pltpu.VMEM((s, LANES), F32), pltpu.VMEM((DA_T, LANES), F32)],
        compiler_params=_params(("parallel", "parallel", "arbitrary")),
    )(qk, qk, proj, bias, o, lse, do)


SB_Q = 256


def _sb_consts(after):
    r = lax.broadcasted_iota(jnp.int32, (2 * BLOCK, 2 * BLOCK), 0) % BLOCK
    c = lax.broadcasted_iota(jnp.int32, (2 * BLOCK, 2 * BLOCK), 1)
    tri = (r > c) if after else (r < c)
    return jnp.logical_or(c >= BLOCK, tri).astype(BF16)


def _split_dot(x, mat):
    hi = x.astype(BF16)
    lo = (x - hi.astype(F32)).astype(BF16)
    return lax.dot_general(jnp.concatenate([hi, lo], axis=1), mat, NN, preferred_element_type=F32)


def _sb_scores(qq, k):
    z = lax.dot_general(qq, k, NT, preferred_element_type=F32)
    lsig = jnp.minimum(z, 0.0) - jnp.log(1.0 + jnp.exp(-jnp.abs(z)))
    return lsig, lsig - z


def _sb_fwd(proj, q_col0, k_col0, v_col0, batch, s):
    t = proj.shape[0]
    nq = s // SB_Q
    n_pairs = 4
    scale = HEAD_DIM ** -0.5

    def body(q_ref, k_ref, v_ref, o_ref, tot_ref, acc_ref, run_ref):
        i = pl.program_id(2)
        lo_q = lax.broadcasted_iota(jnp.int32, (SB_Q, LANES), 1) < HEAD_DIM
        lo_k = _lane_lo()
        qq = _stack_heads(q_ref[...] * scale, lo_q)
        mat = _sb_consts(True)
        row = lax.broadcasted_iota(jnp.int32, (2 * SB_Q, LANES), 0) % SB_Q
        ahead = row - lax.broadcasted_iota(jnp.int32, (2 * SB_Q, LANES), 1)
        acc_ref[...] = jnp.zeros_like(acc_ref)
        run_ref[...] = jnp.zeros_like(run_ref)

        def unit(j, off):
            rows = pl.ds(pl.multiple_of(j * BLOCK, BLOCK), BLOCK)
            k = k_ref[rows, :]
            v = v_ref[rows, :]
            vz = jnp.zeros_like(v)
            lsig, lneg = _sb_scores(qq, k)
            if off is not None:
                valid = ahead > off
                lneg = jnp.where(valid, lneg, 0.0)
            sums = _split_dot(lneg, mat)
            run = run_ref[...]
            a = jnp.exp(lsig + run + sums[:, :BLOCK])
            if off is not None:
                a = jnp.where(valid, a, 0.0)
            run_ref[...] = run + sums[:, BLOCK:]
            ab = a.astype(BF16)
            acc_ref[...] += (lax.dot_general(ab[:SB_Q], jnp.where(lo_k, v, vz), NN, preferred_element_type=F32)
                             + lax.dot_general(ab[SB_Q:], jnp.where(lo_k, vz, v), NN, preferred_element_type=F32))

        unit(2 * i + 1, BLOCK)
        unit(2 * i, 0)

        def pair(p, carry):
            jp = i - 1 - p
            unit(2 * jp + 1, None)
            unit(2 * jp, None)
            return carry

        lax.fori_loop(0, i, pair, 0)
        o_ref[...] = acc_ref[...].astype(BF16)
        tot_ref[...] = jnp.where(lo_q, run_ref[0:SB_Q, :], run_ref[SB_Q:2 * SB_Q, :])

    def seq(col0):
        return pl.BlockSpec((s, LANES), lambda b, h, i: (b, col0 + h))

    blk = pl.BlockSpec((SB_Q, LANES), lambda b, h, i: (b * nq + i, h))
    return pl.pallas_call(
        body, name="attn_b_fwd", grid=(batch, n_pairs, nq),
        in_specs=[pl.BlockSpec((SB_Q, LANES), lambda b, h, i: (b * nq + i, q_col0 + h)), seq(k_col0), seq(v_col0)],
        out_specs=[blk, blk],
        out_shape=[jax.ShapeDtypeStruct((t, n_pairs * LANES), BF16), jax.ShapeDtypeStruct((t, n_pairs * LANES), F32)],
        scratch_shapes=[pltpu.VMEM((SB_Q, LANES), F32), pltpu.VMEM((2 * SB_Q, LANES), F32)],
        compiler_params=_params(("parallel", "parallel", "arbitrary")),
    )(proj, proj, proj)


def _sb_bwd(proj, q_col0, k_col0, v_col0, tot, do, batch, s):
    t = proj.shape[0]
    nq = s // SB_Q
    n_pairs = 4
    scale = HEAD_DIM ** -0.5

    def body(q_ref, k_ref, v_ref, tot_ref, do_ref, dq_ref, dk_ref, dv_ref, dk_acc, dv_acc, dq_acc, seen_ref, gsum_ref):
        i = pl.program_id(2)
        lo_q = lax.broadcasted_iota(jnp.int32, (SB_Q, LANES), 1) < HEAD_DIM
        lo_k = _lane_lo()

        @pl.when(i == 0)
        def _():
            dk_acc[...] = jnp.zeros_like(dk_acc)
            dv_acc[...] = jnp.zeros_like(dv_acc)

        qq = _stack_heads(q_ref[...] * scale, lo_q)
        dd = _stack_heads(do_ref[...], lo_q)
        tot_t = tot_ref[...]
        total = jnp.concatenate([jnp.broadcast_to(tot_t[:, 0:1], (SB_Q, LANES)),
                                 jnp.broadcast_to(tot_t[:, HEAD_DIM:HEAD_DIM + 1], (SB_Q, LANES))], axis=0)
        mat_after = _sb_consts(True)
        mat_before = _sb_consts(False)
        row = lax.broadcasted_iota(jnp.int32, (2 * SB_Q, LANES), 0) % SB_Q
        ahead = row - lax.broadcasted_iota(jnp.int32, (2 * SB_Q, LANES), 1)
        dq_acc[...] = jnp.zeros_like(dq_acc)
        seen_ref[...] = jnp.zeros_like(seen_ref)
        gsum_ref[...] = jnp.zeros_like(gsum_ref)

        def unit(j, off):
            rows = pl.ds(pl.multiple_of(j * BLOCK, BLOCK), BLOCK)
            k = k_ref[rows, :]
            v = v_ref[rows, :]
            kz = jnp.zeros_like(k)
            lsig, lneg = _sb_scores(qq, k)
            if off is not None:
                valid = ahead > off
                lneg = jnp.where(valid, lneg, 0.0)
            sums = _split_dot(lneg, mat_after)
            seen = seen_ref[...]
            a = jnp.exp(lsig + (total - seen - sums[:, BLOCK:]) + sums[:, :BLOCK])
            if off is not None:
                a = jnp.where(valid, a, 0.0)
            seen_ref[...] = seen + sums[:, BLOCK:]
            g = a * lax.dot_general(dd, v, NT, preferred_element_type=F32)
            gs = _split_dot(g, mat_before)
            gsum = gsum_ref[...]
            dz = g - jnp.exp(lsig) * (g + gsum + gs[:, :BLOCK])
            if off is not None:
                dz = jnp.where(valid, dz, 0.0)
            gsum_ref[...] = gsum + gs[:, BLOCK:]
            dzb = dz.astype(BF16)
            dq_acc[...] += (lax.dot_general(dzb[:SB_Q], jnp.where(lo_k, k, kz), NN, preferred_element_type=F32)
                            + lax.dot_general(dzb[SB_Q:], jnp.where(lo_k, kz, k), NN, preferred_element_type=F32))
            dk_acc[rows, :] += lax.dot_general(dzb, qq, TN, preferred_element_type=F32)
            dv_acc[rows, :] += lax.dot_general(a.astype(BF16), dd, TN, preferred_element_type=F32)

        def pair(p, carry):
            unit(2 * p, None)
            unit(2 * p + 1, None)
            return carry

        lax.fori_loop(0, i, pair, 0)
        unit(2 * i, 0)
        unit(2 * i + 1, BLOCK)
        dq_ref[...] = (dq_acc[...] * scale).astype(BF16)

        @pl.when(i == nq - 1)
        def _():
            dk_ref[...] = dk_acc[...].astype(BF16)
            dv_ref[...] = dv_acc[...].astype(BF16)

    def seq_in(col0):
        return pl.BlockSpec((s, LANES), lambda b, h, i: (b, col0 + h))

    blk = pl.BlockSpec((SB_Q, LANES), lambda b, h, i: (b * nq + i, h))
    seq = pl.BlockSpec((s, LANES), lambda b, h, i: (b, h))
    out = jax.ShapeDtypeStruct((t, n_pairs * LANES), BF16)
    return pl.pallas_call(
        body, name="attn_b_bwd", grid=(batch, n_pairs, nq),
        in_specs=[pl.BlockSpec((SB_Q, LANES), lambda b, h, i: (b * nq + i, q_col0 + h)), seq_in(k_col0), seq_in(v_col0),
                  blk, blk],
        out_specs=[blk, seq, seq], out_shape=[out, out, out],
        scratch_shapes=[pltpu.VMEM((s, LANES), F32), pltpu.VMEM((s, LANES), F32), pltpu.VMEM((SB_Q, LANES), F32),
                        pltpu.VMEM((2 * SB_Q, LANES), F32), pltpu.VMEM((2 * SB_Q, LANES), F32)],
        compiler_params=_params(("parallel", "parallel", "arbitrary")),
    )(proj, proj, proj, tot, do)


MEM_Q_TILE = 256


def _mem_fwd(q, kv, batch, s, n_mem):
    t, width = q.shape
    tq = min(MEM_Q_TILE, s)
    nq = s // tq
    scale = MEM_HEAD_DIM ** -0.5

    def body(q_ref, kv_ref, o_ref):
        for h in range(N_HEADS_MEM):
            cols = slice(h * MEM_HEAD_DIM, (h + 1) * MEM_HEAD_DIM)
            k = kv_ref[:, cols]
            v = kv_ref[:, width + h * MEM_HEAD_DIM: width + (h + 1) * MEM_HEAD_DIM]
            sc = lax.dot_general(q_ref[:, cols], k, NT, preferred_element_type=F32) * scale
            p = jnp.exp(sc - jnp.max(sc, axis=1, keepdims=True))
            p = p / jnp.sum(p, axis=1, keepdims=True)
            o_ref[:, cols] = lax.dot_general(p.astype(BF16), v, NN, preferred_element_type=F32).astype(BF16)

    return pl.pallas_call(
        body, name="mem_attn_fwd", grid=(batch, nq),
        in_specs=[pl.BlockSpec((tq, width), lambda b, i: (b * nq + i, 0)),
                  pl.BlockSpec((n_mem, 2 * width), lambda b, i: (b, 0))],
        out_specs=pl.BlockSpec((tq, width), lambda b, i: (b * nq + i, 0)),
        out_shape=jax.ShapeDtypeStruct((t, width), BF16),
        compiler_params=_params(("parallel", "parallel")),
    )(q, kv)


def _mem_bwd(q, kv, do, batch, s, n_mem):
    t, width = q.shape
    tq = min(MEM_Q_TILE, s)
    nq = s // tq
    scale = MEM_HEAD_DIM ** -0.5

    def body(q_ref, kv_ref, do_ref, dq_ref, dkv_ref, acc):
        i = pl.program_id(1)

        @pl.when(i == 0)
        def _():
            acc[...] = jnp.zeros_like(acc)

        for h in range(N_HEADS_MEM):
            cols = slice(h * MEM_HEAD_DIM, (h + 1) * MEM_HEAD_DIM)
            vcols = slice(width + h * MEM_HEAD_DIM, width + (h + 1) * MEM_HEAD_DIM)
            qh, k, v, doh = q_ref[:, cols], kv_ref[:, cols], kv_ref[:, vcols], do_ref[:, cols]
            sc = lax.dot_general(qh, k, NT, preferred_element_type=F32) * scale
            p = jnp.exp(sc - jnp.max(sc, axis=1, keepdims=True))
            p = p / jnp.sum(p, axis=1, keepdims=True)
            dp = lax.dot_general(doh, v, NT, preferred_element_type=F32)
            ds = (p * (dp - jnp.sum(p * dp, axis=1, keepdims=True)) * scale).astype(BF16)
            dq_ref[:, cols] = lax.dot_general(ds, k, NN, preferred_element_type=F32).astype(BF16)
            acc[:, cols] += lax.dot_general(ds, qh, TN, preferred_element_type=F32)
            acc[:, vcols] += lax.dot_general(p.astype(BF16), doh, TN, preferred_element_type=F32)

        @pl.when(i == nq - 1)
        def _():
            dkv_ref[...] = acc[...].astype(BF16)

    row = pl.BlockSpec((tq, width), lambda b, i: (b * nq + i, 0))
    kvs = pl.BlockSpec((n_mem, 2 * width), lambda b, i: (b, 0))
    return pl.pallas_call(
        body, name="mem_attn_bwd", grid=(batch, nq),
        in_specs=[row, kvs, row], out_specs=[row, kvs],
        out_shape=[jax.ShapeDtypeStruct((t, width), BF16), jax.ShapeDtypeStruct((batch * n_mem, 2 * width), BF16)],
        scratch_shapes=[pltpu.VMEM((n_mem, 2 * width), F32)],
        compiler_params=_params(("parallel", "arbitrary")),
    )(q, kv, do)


def _mixer_fwd(o_a, o_b, w_a, w_b, proj, gate_col0):
    t, width = o_a.shape
    tm = min(ROW_TILE, t)
    n_sh, _, cs = w_a.shape

    def body(oa_ref, ob_ref, wa_ref, wb_ref, ga_ref, gb_ref, ua_ref, ub_ref, mix_ref):
        ua = lax.dot_general(oa_ref[...], wa_ref[...], NN, preferred_element_type=F32)
        ub = lax.dot_general(ob_ref[...], wb_ref[...], NN, preferred_element_type=F32)
        ua_ref[...] = ua.astype(BF16)
        ub_ref[...] = ub.astype(BF16)
        mix_ref[...] = (jax.nn.sigmoid(ga_ref[...].astype(F32)) * ua
                        + jax.nn.sigmoid(gb_ref[...].astype(F32)) * ub).astype(BF16)

    row = pl.BlockSpec((tm, width), lambda i, j: (i, 0))
    wsp = pl.BlockSpec((None, width, cs), lambda i, j: (j, 0, 0))
    out = pl.BlockSpec((tm, cs), lambda i, j: (i, j))
    osh = jax.ShapeDtypeStruct((t, n_sh * cs), BF16)
    return pl.pallas_call(
        body, name="mixer_fwd", grid=(t // tm, n_sh),
        in_specs=[row, row, wsp, wsp,
                  pl.BlockSpec((tm, cs), lambda i, j: (i, gate_col0 + j)),
                  pl.BlockSpec((tm, cs), lambda i, j: (i, gate_col0 + n_sh + j))],
        out_specs=[out, out, out], out_shape=[osh, osh, osh],
        compiler_params=_params(("parallel", "parallel")),
    )(o_a, o_b, w_a, w_b, proj, proj)


def _mixer_bwd(dmix, ua, ub, proj, gate_col0):
    t, d = dmix.shape
    tm = min(ROW_TILE, t)
    nc = d // LANES

    def body(dm_ref, ua_ref, ub_ref, ga_ref, gb_ref, dua_ref, dub_ref, dg_ref):
        dm = dm_ref[...].astype(F32)
        sa = jax.nn.sigmoid(ga_ref[...].astype(F32))
        sb = jax.nn.sigmoid(gb_ref[...].astype(F32))
        dua_ref[...] = (dm * sa).astype(BF16)
        dub_ref[...] = (dm * sb).astype(BF16)
        dg_ref[:, 0:d] = (dm * ua_ref[...].astype(F32) * sa * (1.0 - sa)).astype(BF16)
        dg_ref[:, d:2 * d] = (dm * ub_ref[...].astype(F32) * sb * (1.0 - sb)).astype(BF16)

    row = pl.BlockSpec((tm, d), lambda i: (i, 0))
    return pl.pallas_call(
        body, name="mixer_bwd", grid=(t // tm,),
        in_specs=[row, row, row,
                  pl.BlockSpec((tm, d), lambda i: (i, gate_col0 // nc)),
                  pl.BlockSpec((tm, d), lambda i: (i, gate_col0 // nc + 1))],
        out_specs=[row, row, pl.BlockSpec((tm, 2 * d), lambda i: (i, 0))],
        out_shape=[jax.ShapeDtypeStruct((t, d), BF16), jax.ShapeDtypeStruct((t, d), BF16),
                   jax.ShapeDtypeStruct((t, 2 * d), BF16)],
        compiler_params=_params(("parallel",)),
    )(dmix, ua, ub, proj, proj)


def _ffn_up(n, w_gate, w_up):
    t, d = n.shape
    tm = min(ROW_TILE, t)
    n_sh, _, cs = w_gate.shape

    def body(n_ref, wg_ref, wu_ref, hg_ref, hu_ref, act_ref):
        hg = lax.dot_general(n_ref[...], wg_ref[...], NN, preferred_element_type=F32)
        hu = lax.dot_general(n_ref[...], wu_ref[...], NN, preferred_element_type=F32)
        hg_ref[...] = hg.astype(BF16)
        hu_ref[...] = hu.astype(BF16)
        act_ref[...] = (hg * jax.nn.sigmoid(hg) * hu).astype(BF16)

    wsp = pl.BlockSpec((None, d, cs), lambda j, i: (j, 0, 0))
    out = pl.BlockSpec((None, tm, cs), lambda j, i: (j, i, 0))
    osh = jax.ShapeDtypeStruct((n_sh, t, cs), BF16)
    return pl.pallas_call(
        body, name="ffn_up", grid=(n_sh, t // tm),
        in_specs=[pl.BlockSpec((tm, d), lambda j, i: (i, 0)), wsp, wsp],
        out_specs=[out, out, out], out_shape=[osh, osh, osh],
        compiler_params=_params(("parallel", "parallel")),
    )(n, w_gate, w_up)


def _ffn_bwd_act(dh, w_down, hg, hu):
    t, d = dh.shape
    tm = min(ROW_TILE, t)
    n_sh, cs, _ = w_down.shape

    def body(dh_ref, wd_ref, hg_ref, hu_ref, dhg_ref, dhu_ref):
        dact = lax.dot_general(dh_ref[...], wd_ref[...], NT, preferred_element_type=F32)
        hg = hg_ref[...].astype(F32)
        sg = jax.nn.sigmoid(hg)
        dhu_ref[...] = (dact * hg * sg).astype(BF16)
        dhg_ref[...] = (dact * hu_ref[...].astype(F32) * sg * (1.0 + hg * (1.0 - sg))).astype(BF16)

    hid = pl.BlockSpec((None, tm, cs), lambda j, i: (j, i, 0))
    osh = jax.ShapeDtypeStruct((n_sh, t, cs), BF16)
    return pl.pallas_call(
        body, name="ffn_bwd_act", grid=(n_sh, t // tm),
        in_specs=[pl.BlockSpec((tm, d), lambda j, i: (i, 0)), pl.BlockSpec((None, cs, d), lambda j, i: (j, 0, 0)), hid, hid],
        out_specs=[hid, hid], out_shape=[osh, osh],
        compiler_params=_params(("parallel", "parallel")),
    )(dh, w_down, hg, hu)


def _mm_cols(name, a, w, out_dtype=BF16):
    t, k = a.shape
    n_sh, _, cs = w.shape
    tm = min(ROW_TILE, t)
    return _mm(name, a, w, grid=(n_sh, t // tm),
               a_spec=pl.BlockSpec((tm, k), lambda j, i: (i, 0)), b_spec=pl.BlockSpec((None, k, cs), lambda j, i: (j, 0, 0)),
               o_shape=(t, n_sh * cs), o_spec=pl.BlockSpec((tm, cs), lambda j, i: (i, j)), dims=NN, out_dtype=out_dtype)


def _mm_cols_t(name, a, w, out_dtype, res=None):
    t = a.shape[0]
    n_sh, k, cs = w.shape
    tm = min(ROW_TILE, t)
    o_spec = pl.BlockSpec((tm, k), lambda i, j: (i, 0))
    return _mm(name, a, w, grid=(t // tm, n_sh),
               a_spec=pl.BlockSpec((tm, cs), lambda i, j: (i, j)), b_spec=pl.BlockSpec((None, k, cs), lambda i, j: (j, 0, 0)),
               o_shape=(t, k), o_spec=o_spec, dims=NT, out_dtype=out_dtype, nk=n_sh, res=res,
               res_spec=o_spec if res is not None else None)


def _mm_hid_t(name, a, w, out_dtype, res=None):
    n_sh, t, cs = a.shape
    k = w.shape[1]
    tm = min(ROW_TILE, t)
    o_spec = pl.BlockSpec((tm, k), lambda i, j: (i, 0))
    return _mm(name, a, w, grid=(t // tm, n_sh),
               a_spec=pl.BlockSpec((None, tm, cs), lambda i, j: (j, i, 0)),
               b_spec=pl.BlockSpec((None, k, cs), lambda i, j: (j, 0, 0)),
               o_shape=(t, k), o_spec=o_spec, dims=NT, out_dtype=out_dtype, nk=n_sh, res=res,
               res_spec=o_spec if res is not None else None)


def _mm_hid(name, a, w, res):
    n_sh, t, cs = a.shape
    n = w.shape[2]
    tm = min(ROW_TILE, t)
    o_spec = pl.BlockSpec((tm, n), lambda i, j: (i, 0))
    return _mm(name, a, w, grid=(t // tm, n_sh),
               a_spec=pl.BlockSpec((None, tm, cs), lambda i, j: (j, i, 0)),
               b_spec=pl.BlockSpec((None, cs, n), lambda i, j: (j, 0, 0)),
               o_shape=(t, n), o_spec=o_spec, dims=NN, out_dtype=F32, nk=n_sh, res=res, res_spec=o_spec)


def _mm_full(name, a, w, out_dtype, dims=NN, res=None):
    t, k = a.shape
    n = w.shape[1] if dims == NN else w.shape[0]
    tm = min(ROW_TILE, t)
    o_spec = pl.BlockSpec((tm, n), lambda i: (i, 0))
    return _mm(name, a, w, grid=(t // tm,),
               a_spec=pl.BlockSpec((tm, k), lambda i: (i, 0)), b_spec=pl.BlockSpec(w.shape, lambda i: (0, 0)),
               o_shape=(t, n), o_spec=o_spec, dims=dims, out_dtype=out_dtype, res=res,
               res_spec=o_spec if res is not None else None)


def _wgrad_cols(name, a, g, n_sh):
    t, k = a.shape
    cs = g.shape[1] // n_sh
    tm = min(ROW_TILE, t)
    return _mm(name, a, g, grid=(n_sh, t // tm),
               a_spec=pl.BlockSpec((tm, k), lambda j, r: (r, 0)), b_spec=pl.BlockSpec((tm, cs), lambda j, r: (r, j)),
               o_shape=(n_sh, k, cs), o_spec=pl.BlockSpec((None, k, cs), lambda j, r: (j, 0, 0)), dims=TN,
               out_dtype=BF16, nk=t // tm)


def _wgrad_hid_cols(name, a, g):
    t, k = a.shape
    n_sh, _, cs = g.shape
    tm = min(ROW_TILE, t)
    return _mm(name, a, g, grid=(n_sh, t // tm),
               a_spec=pl.BlockSpec((tm, k), lambda j, r: (r, 0)), b_spec=pl.BlockSpec((None, tm, cs), lambda j, r: (j, r, 0)),
               o_shape=(n_sh, k, cs), o_spec=pl.BlockSpec((None, k, cs), lambda j, r: (j, 0, 0)), dims=TN,
               out_dtype=BF16, nk=t // tm)


def _wgrad_hid_rows(name, a, g):
    n_sh, t, cs = a.shape
    n = g.shape[1]
    tm = min(ROW_TILE, t)
    return _mm(name, a, g, grid=(n_sh, t // tm),
               a_spec=pl.BlockSpec((None, tm, cs), lambda j, r: (j, r, 0)), b_spec=pl.BlockSpec((tm, n), lambda j, r: (r, 0)),
               o_shape=(n_sh, cs, n), o_spec=pl.BlockSpec((None, cs, n), lambda j, r: (j, 0, 0)), dims=TN,
               out_dtype=BF16, nk=t // tm)


def _wgrad_rows(name, a, g, n_sh):
    t, k = a.shape
    n = g.shape[1]
    rs = k // n_sh
    tm = min(ROW_TILE, t)
    return _mm(name, a, g, grid=(n_sh, t // tm),
               a_spec=pl.BlockSpec((tm, rs), lambda j, r: (r, j)), b_spec=pl.BlockSpec((tm, n), lambda j, r: (r, 0)),
               o_shape=(n_sh, rs, n), o_spec=pl.BlockSpec((None, rs, n), lambda j, r: (j, 0, 0)), dims=TN,
               out_dtype=BF16, nk=t // tm)


def _peers():
    x, y, c = lax.axis_index("x"), lax.axis_index("y"), lax.axis_index("c")
    me = 4 * x + 2 * y + c
    out = []
    for k in range(1, N_DEV):
        kx, ky, kc = (k >> 2) & 1, (k >> 1) & 1, k & 1
        px = 1 - x if kx else x
        py = 1 - y if ky else y
        pc = 1 - c if kc else c
        out.append(((px, py, pc), 4 * px + 2 * py + pc))
    return me, out


def _cast_weights(ws):
    def body(*refs):
        n = len(refs) // 2
        for i_ref, o_ref in zip(refs[:n], refs[n:]):
            o_ref[...] = i_ref[...].astype(BF16)

    return pl.pallas_call(
        body, name="cast_weights", in_specs=[VMEM] * len(ws), out_specs=[VMEM] * len(ws),
        out_shape=[jax.ShapeDtypeStruct(w.shape, BF16) for w in ws],
    )(*ws)


def _exchange(name, arrs, gather):
    n = len(arrs)
    n_peer = N_DEV - 1

    def body(*refs):
        ins, outs = refs[:n], refs[n:2 * n]
        send_sems, recv_sems, loc_sems = refs[2 * n:]
        me, peers = _peers()
        started = []
        for w in range(n):
            src_me = ins[w] if gather else ins[w].at[me]
            loc = pltpu.make_async_copy(src_me, outs[w].at[me], loc_sems.at[w])
            loc.start()
            started.append(loc)
        for k, (dev, idx) in enumerate(peers):
            for w in range(n):
                cp = pltpu.make_async_remote_copy(
                    src_ref=ins[w] if gather else ins[w].at[idx], dst_ref=outs[w].at[me],
                    send_sem=send_sems.at[w * n_peer + k], recv_sem=recv_sems.at[w * n_peer + k],
                    device_id=dev, device_id_type=pl.DeviceIdType.MESH)
                cp.start()
        for loc in started:
            loc.wait()
        for k, (dev, idx) in enumerate(peers):
            for w in range(n):
                cp = pltpu.make_async_remote_copy(
                    src_ref=ins[w] if gather else ins[w].at[idx], dst_ref=outs[w].at[idx],
                    send_sem=send_sems.at[w * n_peer + k], recv_sem=recv_sems.at[w * n_peer + k],
                    device_id=dev, device_id_type=pl.DeviceIdType.MESH)
                cp.wait_send()
                cp.wait_recv()

    out_shape = [jax.ShapeDtypeStruct(((N_DEV,) + a.shape) if gather else a.shape, a.dtype) for a in arrs]
    return pl.pallas_call(
        body, name=name, in_specs=[ANY] * n, out_specs=[ANY] * n, out_shape=out_shape,
        scratch_shapes=[pltpu.SemaphoreType.DMA((n * n_peer,)), pltpu.SemaphoreType.DMA((n * n_peer,)),
                        pltpu.SemaphoreType.DMA((n,))],
    )(*arrs)


def _allreduce_small(v):
    def body(v_ref, o_ref, all_ref, send_sems, recv_sems):
        me, peers = _peers()
        all_ref[me] = v_ref[...]
        for k, (dev, idx) in enumerate(peers):
            pltpu.make_async_remote_copy(src_ref=v_ref, dst_ref=all_ref.at[me], send_sem=send_sems.at[k],
                                         recv_sem=recv_sems.at[k], device_id=dev,
                                         device_id_type=pl.DeviceIdType.MESH).start()
        for k, (dev, idx) in enumerate(peers):
            cp = pltpu.make_async_remote_copy(src_ref=v_ref, dst_ref=all_ref.at[idx], send_sem=send_sems.at[k],
                                              recv_sem=recv_sems.at[k], device_id=dev,
                                              device_id_type=pl.DeviceIdType.MESH)
            cp.wait_send()
            cp.wait_recv()
        tot = all_ref[0]
        for dvc in range(1, N_DEV):
            tot = tot + all_ref[dvc]
        o_ref[...] = tot

    return pl.pallas_call(
        body, name="allreduce_small", in_specs=[VMEM], out_specs=VMEM,
        out_shape=jax.ShapeDtypeStruct(v.shape, F32),
        scratch_shapes=[pltpu.VMEM((N_DEV,) + v.shape, F32), pltpu.SemaphoreType.DMA((N_DEV - 1,)),
                        pltpu.SemaphoreType.DMA((N_DEV - 1,))],
    )(v)


def _adam_math(g, w, m, v):
    m_new = ADAM_B1 * m + (1.0 - ADAM_B1) * g
    v_new = ADAM_B2 * v + (1.0 - ADAM_B2) * (g * g)
    m_hat = m_new / (1.0 - ADAM_B1 ** ADAM_STEP)
    v_hat = v_new / (1.0 - ADAM_B2 ** ADAM_STEP)
    delta = -ADAM_LR * (m_hat / (jnp.sqrt(v_hat) + ADAM_EPS) + ADAM_WD * w)
    return delta, m_new, v_new


def _adam(name, pieces, w, m, v):
    r, c = w.shape
    tr = r
    for cand in (256, 176, 128, 64):
        if r % cand == 0 and r > cand:
            tr = cand
            break

    def body(p_ref, w_ref, m_ref, v_ref, g_ref, d_ref, mo_ref, vo_ref):
        g = p_ref[0].astype(F32)
        for dvc in range(1, N_DEV):
            g = g + p_ref[dvc].astype(F32)
        delta, m_new, v_new = _adam_math(g, w_ref[...], m_ref[...], v_ref[...])
        g_ref[...] = g
        d_ref[...] = delta
        mo_ref[...] = m_new
        vo_ref[...] = v_new

    blk = pl.BlockSpec((tr, c), lambda i: (i, 0))
    osh = jax.ShapeDtypeStruct((r, c), F32)
    return pl.pallas_call(
        body, name=name, grid=(r // tr,),
        in_specs=[pl.BlockSpec((N_DEV, tr, c), lambda i: (0, i, 0)), blk, blk, blk],
        out_specs=[blk, blk, blk, blk], out_shape=[osh, osh, osh, osh],
        compiler_params=_params(("parallel",)),
    )(pieces, w, m, v)


def _adam_small(g, w, m, v):
    def body(g_ref, w_ref, m_ref, v_ref, d_ref, mo_ref, vo_ref):
        delta, m_new, v_new = _adam_math(g_ref[...], w_ref[...], m_ref[...], v_ref[...])
        d_ref[...] = delta
        mo_ref[...] = m_new
        vo_ref[...] = v_new

    osh = jax.ShapeDtypeStruct(g.shape, F32)
    return pl.pallas_call(body, name="adam_small", in_specs=[VMEM] * 4, out_specs=[VMEM] * 3,
                          out_shape=[osh, osh, osh])(g, w, m, v)


def _local_step(x, mem, pos, tgt, gains, wts, batch):
    g_mix, g_mem_q, g_mem_kv, g_ffn, g_final = gains
    w_in, w_up_a, w_up_b, w_out, w_q, w_kv, w_o, w_fg, w_fu, w_fd = wts
    t, d = x.shape
    s = t // batch
    n_mem = mem.shape[0] // batch
    n_sh = N_DEV
    width = w_up_a.shape[1]
    nb = width // LANES
    w_out_m = w_out.reshape(d, d)
    w_q_m = w_q.reshape(d, -1)
    w_kv_m = w_kv.reshape(d, -1)

    lane = jnp.arange(LANES, dtype=jnp.int32) % HEAD_DIM
    sel_lo = (lane < ROPE_HALF).astype(F32)[None, :]
    sel_hi = ((lane >= ROPE_HALF) & (lane < 2 * ROPE_HALF)).astype(F32)[None, :]
    freqs = ROPE_THETA ** (-jnp.arange(ROPE_HALF, dtype=F32) / ROPE_HALF)
    inv_freq = jnp.where(lane < 2 * ROPE_HALF, freqs[lane % ROPE_HALF], 0.0)[None, :]
    cos_t, sin_a, sin_b = _rope_tables(pos, inv_freq, sel_lo, sel_hi)
    bias = _dilated_bias_tiles(s)

    n1 = _rms_fwd("norm_mix", x, g_mix)
    proj = _mm_cols("proj_in", n1, w_in)
    qk_a = _rope_apply("rope_fwd", proj, 0, 2 * nb, cos_t, sin_a, sin_b, 1.0)
    o_a, lse_a = _da_fwd(qk_a, proj, 2 * nb, bias, batch, s)
    o_b, tot_b = _sb_fwd(proj, 3 * nb, 4 * nb, 5 * nb, batch, s)
    ua, ub, mixed = _mixer_fwd(o_a, o_b, w_up_a, w_up_b, proj, 6 * nb)
    h1 = _mm_full("mix_out", mixed, w_out_m, F32, res=x)
    n2 = _rms_fwd("norm_mem_q", h1, g_mem_q)
    mem_n = _rms_fwd("norm_mem_kv", mem, g_mem_kv)
    q_m = _mm_full("mem_q", n2, w_q_m, BF16)
    kv_m = _mm_full("mem_kv", mem_n, w_kv_m, BF16)
    o_m = _mem_fwd(q_m, kv_m, batch, s, n_mem)
    h2 = _mm_cols_like_res("mem_out", o_m, w_o, h1)
    n3 = _rms_fwd("norm_ffn", h2, g_ffn)
    hg, hu, act = _ffn_up(n3, w_fg, w_fu)
    h3 = _mm_hid("ffn_down", act, w_fd, h2)
    loss_part, dh3, dg_final = _loss_head(h3, tgt, g_final.reshape(1, d))

    dh3_b = _to_bf16("dh3_bf16", dh3)
    dhg, dhu = _ffn_bwd_act(dh3_b, w_fd, hg, hu)
    gw_fd = _wgrad_hid_rows("gw_ffn_down", act, dh3_b)
    gw_fg = _wgrad_hid_cols("gw_ffn_gate", n3, dhg)
    gw_fu = _wgrad_hid_cols("gw_ffn_up", n3, dhu)
    dn3 = _mm_hid_t("dn_ffn_gate", dhg, w_fg, F32)
    dn3 = _mm_hid_t("dn_ffn_up", dhu, w_fu, F32, res=dn3)
    dh2, dg_ffn = _rms_bwd("norm_ffn_bwd", dn3, h2, g_ffn, dh3)

    dh2_b = _to_bf16("dh2_bf16", dh2)
    do_m = _mm_cols_t("mem_out_bwd", dh2_b, w_o, BF16)
    gw_o = _wgrad_cols("gw_mem_o", o_m, dh2_b, n_sh)
    dq_m, dkv_m = _mem_bwd(q_m, kv_m, do_m, batch, s, n_mem)
    gw_q = _wgrad_rows("gw_mem_q", n2, dq_m, n_sh)
    gw_kv = _wgrad_rows("gw_mem_kv", mem_n, dkv_m, n_sh)
    dn2 = _mm_full("mem_q_bwd", dq_m, w_q_m, F32, dims=NT)
    dmem_n = _mm_full("mem_kv_bwd", dkv_m, w_kv_m, F32, dims=NT)
    _, dg_mem_kv = _rms_bwd("norm_mem_kv_bwd", dmem_n, mem, g_mem_kv, None)
    dh1, dg_mem_q = _rms_bwd("norm_mem_q_bwd", dn2, h1, g_mem_q, dh2)

    dh1_b = _to_bf16("dh1_bf16", dh1)
    dmix = _mm_full("mix_out_bwd", dh1_b, w_out_m, BF16, dims=NT)
    gw_out = _wgrad_rows("gw_out", mixed, dh1_b, n_sh)
    dua, dub, dgates = _mixer_bwd(dmix, ua, ub, proj, 6 * nb)
    do_a = _mm_cols_t("up_a_bwd", dua, w_up_a, BF16)
    do_b = _mm_cols_t("up_b_bwd", dub, w_up_b, BF16)
    gw_ua = _wgrad_cols("gw_up_a", o_a, dua, n_sh)
    gw_ub = _wgrad_cols("gw_up_b", o_b, dub, n_sh)
    dq_ar, dk_ar, dv_a = _da_bwd(qk_a, proj, 2 * nb, bias, o_a, lse_a, do_a, batch, s)
    dqk_a = _rope_apply("rope_bwd", jnp.concatenate([dq_ar, dk_ar], axis=1), 0, 2 * nb, cos_t, sin_a, sin_b, -1.0)
    dq_b, dk_b, dv_b = _sb_bwd(proj, 3 * nb, 4 * nb, 5 * nb, tot_b, do_b, batch, s)
    dproj = jnp.concatenate([dqk_a, dv_a, dq_b, dk_b, dv_b, dgates], axis=1)
    gw_in = _wgrad_cols("gw_in", n1, dproj, n_sh)
    dn1 = _mm_cols_t("proj_in_bwd", dproj, w_in, F32)
    grad_x, dg_mix = _rms_bwd("norm_mix_bwd", dn1, x, g_mix, dh1)

    grads = (gw_in, gw_ua, gw_ub, gw_out.reshape(w_out.shape), gw_q.reshape(w_q.shape), gw_kv.reshape(w_kv.shape), gw_o,
             gw_fg, gw_fu, gw_fd)
    return loss_part, grad_x, grads, (dg_mix, dg_mem_q, dg_mem_kv, dg_ffn, dg_final)


def _mm_cols_like_res(name, a, w, res):
    t, k = a.shape
    n_sh, _, cs = w.shape
    tm = min(ROW_TILE, t)
    o_spec = pl.BlockSpec((tm, cs), lambda j, i: (i, j))
    return _mm(name, a, w, grid=(n_sh, t // tm),
               a_spec=pl.BlockSpec((tm, k), lambda j, i: (i, 0)), b_spec=pl.BlockSpec((None, k, cs), lambda j, i: (j, 0, 0)),
               o_shape=(t, n_sh * cs), o_spec=o_spec, dims=NN, out_dtype=F32, res=res, res_spec=o_spec)


def _to_bf16(name, a):
    t, d = a.shape
    tm = min(ROW_TILE, t)

    def body(a_ref, o_ref):
        o_ref[...] = a_ref[...].astype(BF16)

    row = pl.BlockSpec((tm, d), lambda i: (i, 0))
    return pl.pallas_call(body, name=name, grid=(t // tm,), in_specs=[row], out_specs=row,
                          out_shape=jax.ShapeDtypeStruct((t, d), BF16), compiler_params=_params(("parallel",)))(a)


WEIGHTS = ("w_in", "w_up_a", "w_up_b", "w_out", "w_q_mem", "w_kv_mem", "w_o_mem", "w_ffn_gate", "w_ffn_up", "w_ffn_down")
GAINS = ("g_mix", "g_mem_q", "g_mem_kv", "g_ffn", "g_final")
ORDER = ("g_mix", "w_in", "w_up_a", "w_up_b", "w_out", "g_mem_q", "g_mem_kv", "w_q_mem", "w_kv_mem", "w_o_mem", "g_ffn",
         "w_ffn_gate", "w_ffn_up", "w_ffn_down", "g_final")


def kernel(x, mem, positions, g_mix, w_in, w_up_a, w_up_b, w_out, g_mem_q, g_mem_kv, w_q_mem, w_kv_mem, w_o_mem, g_ffn, w_ffn_gate, w_ffn_up, w_ffn_down, g_final, loss_target, m_g_mix, m_w_in, m_w_up_a, m_w_up_b, m_w_out, m_g_mem_q, m_g_mem_kv, m_w_q_mem, m_w_kv_mem, m_w_o_mem, m_g_ffn, m_w_ffn_gate, m_w_ffn_up, m_w_ffn_down, m_g_final, v_g_mix, v_w_in, v_w_up_a, v_w_up_b, v_w_out, v_g_mem_q, v_g_mem_kv, v_w_q_mem, v_w_kv_mem, v_w_o_mem, v_g_ffn, v_w_ffn_gate, v_w_ffn_up, v_w_ffn_down, v_g_final):
    given = dict(locals())
    batch, s, d = x.shape
    t = batch * s
    shard = {n: given[n].reshape(given[n].shape[-2:]) for n in WEIGHTS}
    gains = [given[n].reshape(1, d) for n in GAINS]

    cast = _cast_weights([shard[n] for n in WEIGHTS])
    stacked = _exchange("gather_weights", cast, True)
    loss_part, grad_x, grads, dgains = _local_step(
        x.reshape(t, d), mem.reshape(-1, d), positions.reshape(t, 1), loss_target.reshape(t, d), gains, stacked, batch)
    pieces = _exchange("scatter_grads", list(grads), False)

    grad, delta, new_m, new_v = {}, {}, {}, {}
    for n, p in zip(WEIGHTS, pieces):
        m2, v2 = given["m_" + n].reshape(shard[n].shape), given["v_" + n].reshape(shard[n].shape)
        outs = _adam("adam_" + n, p, shard[n], m2, v2)
        grad[n], delta[n], new_m[n], new_v[n] = [o.reshape(given[n].shape) for o in outs]

    rows = jnp.concatenate(list(dgains) + [jnp.zeros((N_DEV - len(GAINS), d), F32)], axis=0)
    g_all = _allreduce_small(rows)
    w_all = jnp.concatenate(gains + [jnp.zeros((N_DEV - len(GAINS), d), F32)], axis=0)
    m_all = jnp.concatenate([given["m_" + n].reshape(1, d) for n in GAINS] + [jnp.zeros((N_DEV - len(GAINS), d), F32)], axis=0)
    v_all = jnp.concatenate([given["v_" + n].reshape(1, d) for n in GAINS] + [jnp.ones((N_DEV - len(GAINS), d), F32)], axis=0)
    d_all, mo_all, vo_all = _adam_small(g_all, w_all, m_all, v_all)
    for i, n in enumerate(GAINS):
        grad[n] = g_all[i].reshape(given[n].shape)
        delta[n] = d_all[i].reshape(given[n].shape)
        new_m[n] = mo_all[i].reshape(given[n].shape)
        new_v[n] = vo_all[i].reshape(given[n].shape)

    loss = lax.psum(loss_part[0, 0], ("x", "y", "c"))
    return (loss, grad_x.reshape(x.shape), *[grad[n] for n in ORDER], *[delta[n] for n in ORDER],
            *[new_m[n] for n in ORDER], *[new_v[n] for n in ORDER])
```

```python
import functools
import math

import jax
import jax.numpy as jnp
from jax import lax
from jax.experimental import pallas as pl
from jax.experimental.pallas import tpu as pltpu

F32 = jnp.float32
BF16 = jnp.bfloat16

N_DEV = 8
HEAD_DIM = 64
MEM_HEAD_DIM = 128
N_HEADS_MEM = 4
BLOCK = 128
DIL_PATTERNS = ((128, 1), (512, 4), (2048, 16))
ROPE_THETA = 500000.0
ROPE_HALF = 8
RMS_EPS = 1e-6
ADAM_LR, ADAM_B1, ADAM_B2, ADAM_EPS, ADAM_WD, ADAM_STEP = 0.001, 0.9, 0.999, 1e-08, 0.01, 10
NEG = -1e30
ROW_TILE = 512
LANES = 128

ANY = pl.BlockSpec(memory_space=pl.ANY)
VMEM = pl.BlockSpec(memory_space=pltpu.VMEM)
NN = (((1,), (0,)), ((), ()))
NT = (((1,), (1,)), ((), ()))
TN = (((0,), (0,)), ((), ()))


def _params(sem):
    return pltpu.CompilerParams(dimension_semantics=sem)


def _mm(name, a, b, *, grid, a_spec, b_spec, o_shape, o_spec, dims, out_dtype, nk=1, res=None, res_spec=None):
    has_res = res is not None

    def body(*refs):
        a_ref, b_ref = refs[0], refs[1]
        r_ref = refs[2] if has_res else None
        o_ref = refs[3] if has_res else refs[2]
        p = lax.dot_general(a_ref[...], b_ref[...], dims, preferred_element_type=F32)
        if nk == 1:
            if has_res:
                p = p + r_ref[...].astype(F32)
            o_ref[...] = p.astype(out_dtype)
            return
        acc_ref = refs[-1]
        k = pl.program_id(len(grid) - 1)

        @pl.when(k == 0)
        def _():
            acc_ref[...] = p

        @pl.when(k > 0)
        def _():
            acc_ref[...] += p

        @pl.when(k == nk - 1)
        def _():
            t = acc_ref[...]
            if has_res:
                t = t + r_ref[...].astype(F32)
            o_ref[...] = t.astype(out_dtype)

    o_block = tuple(d for d in o_spec.block_shape if d is not None)
    sem = ("parallel",) * (len(grid) - 1) + (("arbitrary",) if nk > 1 else ("parallel",))
    return pl.pallas_call(
        body, name=name, grid=grid,
        in_specs=[a_spec, b_spec] + ([res_spec] if has_res else []),
        out_specs=o_spec, out_shape=jax.ShapeDtypeStruct(o_shape, out_dtype),
        scratch_shapes=[pltpu.VMEM(o_block, F32)] if nk > 1 else [],
        compiler_params=_params(sem),
    )(*([a, b] + ([res] if has_res else [])))


def _rms_fwd(name, x, g):
    t, d = x.shape
    tm = min(ROW_TILE, t)

    def body(x_ref, g_ref, o_ref):
        xf = x_ref[...]
        r = lax.rsqrt(jnp.mean(xf * xf, axis=-1, keepdims=True) + RMS_EPS)
        o_ref[...] = (xf * r * g_ref[...]).astype(BF16)

    return pl.pallas_call(
        body, name=name, grid=(t // tm,),
        in_specs=[pl.BlockSpec((tm, d), lambda i: (i, 0)), pl.BlockSpec((1, d), lambda i: (0, 0))],
        out_specs=pl.BlockSpec((tm, d), lambda i: (i, 0)), out_shape=jax.ShapeDtypeStruct((t, d), BF16),
        compiler_params=_params(("parallel",)),
    )(x, g)


def _rms_bwd(name, dn, x, g, dres):
    t, d = x.shape
    tm = min(ROW_TILE, t)
    has_res = dres is not None

    def body(*refs):
        dn_ref, x_ref, g_ref = refs[0], refs[1], refs[2]
        r_ref = refs[3] if has_res else None
        dx_ref, dg_ref = refs[-2], refs[-1]
        xf = x_ref[...]
        r = lax.rsqrt(jnp.mean(xf * xf, axis=-1, keepdims=True) + RMS_EPS)
        xh = xf * r
        dnf = dn_ref[...].astype(F32)
        dxh = dnf * g_ref[...]
        dx = r * (dxh - xh * jnp.mean(dxh * xh, axis=-1, keepdims=True))
        if has_res:
            dx = dx + r_ref[...]
        dx_ref[...] = dx

        @pl.when(pl.program_id(0) == 0)
        def _():
            dg_ref[...] = jnp.zeros_like(dg_ref)

        dg_ref[...] += jnp.sum(dnf * xh, axis=0, keepdims=True)

    row = pl.BlockSpec((tm, d), lambda i: (i, 0))
    vec = pl.BlockSpec((1, d), lambda i: (0, 0))
    return pl.pallas_call(
        body, name=name, grid=(t // tm,),
        in_specs=[row, row, vec] + ([row] if has_res else []),
        out_specs=[row, vec],
        out_shape=[jax.ShapeDtypeStruct((t, d), F32), jax.ShapeDtypeStruct((1, d), F32)],
        compiler_params=_params(("arbitrary",)),
    )(*([dn, x, g] + ([dres] if has_res else [])))


def _loss_head(h, tgt, g):
    t, d = h.shape
    tm = min(ROW_TILE, t)

    def body(h_ref, t_ref, g_ref, loss_ref, dh_ref, dg_ref):
        xf = h_ref[...]
        gv = g_ref[...]
        r = lax.rsqrt(jnp.mean(xf * xf, axis=-1, keepdims=True) + RMS_EPS)
        xh = xf * r
        e = xh * gv - t_ref[...]
        dy = e * (1.0 / d)
        dxh = dy * gv
        dh_ref[...] = r * (dxh - xh * jnp.mean(dxh * xh, axis=-1, keepdims=True))

        @pl.when(pl.program_id(0) == 0)
        def _():
            dg_ref[...] = jnp.zeros_like(dg_ref)
            loss_ref[...] = jnp.zeros_like(loss_ref)

        dg_ref[...] += jnp.sum(dy * xh, axis=0, keepdims=True)
        part = jnp.sum(jnp.sum(e * e, axis=1, keepdims=True), axis=0, keepdims=True) * (0.5 / d)
        loss_ref[...] += jnp.broadcast_to(part, loss_ref.shape)

    row = pl.BlockSpec((tm, d), lambda i: (i, 0))
    vec = pl.BlockSpec((1, d), lambda i: (0, 0))
    return pl.pallas_call(
        body, name="loss_head", grid=(t // tm,),
        in_specs=[row, row, vec],
        out_specs=[pl.BlockSpec((8, LANES), lambda i: (0, 0)), row, vec],
        out_shape=[jax.ShapeDtypeStruct((8, LANES), F32), jax.ShapeDtypeStruct((t, d), F32),
                   jax.ShapeDtypeStruct((1, d), F32)],
        compiler_params=_params(("arbitrary",)),
    )(h, tgt, g)


def _rope_tables(pos, inv_freq, sel_lo, sel_hi):
    t = pos.shape[0]
    tm = min(ROW_TILE, t)

    def body(p_ref, f_ref, lo_ref, hi_ref, c_ref, sa_ref, sb_ref):
        ang = p_ref[...].astype(F32) * f_ref[...]
        rot = lo_ref[...] + hi_ref[...]
        cs, sn = jnp.cos(ang), jnp.sin(ang)
        c_ref[...] = cs * rot + (1.0 - rot)
        sa_ref[...] = -sn * lo_ref[...]
        sb_ref[...] = sn * hi_ref[...]

    vec = pl.BlockSpec((1, LANES), lambda i: (0, 0))
    row = pl.BlockSpec((tm, LANES), lambda i: (i, 0))
    return pl.pallas_call(
        body, name="rope_tables", grid=(t // tm,),
        in_specs=[pl.BlockSpec((tm, 1), lambda i: (i, 0)), vec, vec, vec],
        out_specs=[row, row, row], out_shape=[jax.ShapeDtypeStruct((t, LANES), F32)] * 3,
        compiler_params=_params(("parallel",)),
    )(pos, inv_freq, sel_lo, sel_hi)


def _rope_apply(name, src, col0, n_cols, cos_t, sin_a, sin_b, sign):
    t = src.shape[0]
    tm = min(ROW_TILE, t)

    def body(x_ref, c_ref, sa_ref, sb_ref, o_ref):
        xf = x_ref[...].astype(F32)
        up = pltpu.roll(xf, LANES - ROPE_HALF, 1)
        dn = pltpu.roll(xf, ROPE_HALF, 1)
        o_ref[...] = (xf * c_ref[...] + sign * (up * sa_ref[...] + dn * sb_ref[...])).astype(BF16)

    tab = pl.BlockSpec((tm, LANES), lambda i, c: (i, 0))
    return pl.pallas_call(
        body, name=name, grid=(t // tm, n_cols),
        in_specs=[pl.BlockSpec((tm, LANES), lambda i, c: (i, col0 + c)), tab, tab, tab],
        out_specs=pl.BlockSpec((tm, LANES), lambda i, c: (i, c)),
        out_shape=jax.ShapeDtypeStruct((t, n_cols * LANES), BF16),
        compiler_params=_params(("parallel", "parallel")),
    )(src, cos_t, sin_a, sin_b)


DA_T = 256


def _lane_lo():
    return lax.broadcasted_iota(jnp.int32, (BLOCK, LANES), 1) < HEAD_DIM


def _dilated_bias_tiles(s):
    n = s // DA_T
    dist = (jnp.arange(n, dtype=jnp.int32)[:, None, None] * DA_T
            + jnp.arange(DA_T, dtype=jnp.int32)[None, :, None] - jnp.arange(DA_T, dtype=jnp.int32)[None, None, :])
    cnt = jnp.zeros(dist.shape, F32)
    for window, dil in DIL_PATTERNS:
        cnt = cnt + ((dist >= 0) & (dist % dil == 0) & (dist <= window)).astype(F32)
    return jnp.where(cnt > 0, jnp.log(jnp.maximum(cnt, 1.0)), NEG)


def _stack_heads(x, lo):
    zero = jnp.zeros_like(x)
    return jnp.concatenate([jnp.where(lo, x, zero), jnp.where(lo, zero, x)], axis=0)


def _da_fwd(qk, proj, v_col0, bias, batch, s, ride=None):
    t = qk.shape[0]
    nq = s // DA_T
    n_pairs = 4
    scale = HEAD_DIM ** -0.5

    def body(q_ref, k_ref, v_ref, b_ref, o_ref, lse_ref, acc_ref, m_ref, l_ref):
        i = pl.program_id(2)
        lo = lax.broadcasted_iota(jnp.int32, (DA_T, LANES), 1) < HEAD_DIM
        qq = _stack_heads(q_ref[...] * scale, lo)
        ones = jnp.ones((DA_T, LANES), BF16)
        acc_ref[...] = jnp.zeros_like(acc_ref)
        m_ref[...] = jnp.full(m_ref.shape, NEG, F32)
        l_ref[...] = jnp.zeros_like(l_ref)

        def unit(dlt, carry):
            rows = pl.ds(pl.multiple_of((i - dlt) * DA_T, DA_T), DA_T)
            k = k_ref[rows, :]
            v = v_ref[rows, :]
            vz = jnp.zeros_like(v)
            bias_t = b_ref[dlt]
            sc = lax.dot_general(qq, k, NT, preferred_element_type=F32) + jnp.concatenate([bias_t, bias_t], axis=0)
            m_old = m_ref[...]
            m_new = jnp.maximum(m_old, jnp.max(sc, axis=1, keepdims=True))
            p = jnp.exp(sc - m_new).astype(BF16)
            alpha = jnp.exp(m_old - m_new)
            m_ref[...] = m_new
            l_ref[...] = alpha * l_ref[...] + lax.dot_general(p, ones, NN, preferred_element_type=F32)
            pv = (lax.dot_general(p[:DA_T], jnp.where(lo, v, vz), NN, preferred_element_type=F32)
                  + lax.dot_general(p[DA_T:], jnp.where(lo, vz, v), NN, preferred_element_type=F32))
            acc_ref[...] = acc_ref[...] * jnp.where(lo, alpha[:DA_T], alpha[DA_T:]) + pv
            return carry

        lax.fori_loop(0, i + 1, unit, 0)
        l_t = l_ref[...]
        o_ref[...] = (acc_ref[...] / jnp.where(lo, l_t[:DA_T], l_t[DA_T:])).astype(BF16)
        lse = m_ref[...] + jnp.log(l_t)
        lse_ref[...] = jnp.where(lo, lse[:DA_T], lse[DA_T:])

    blk = pl.BlockSpec((DA_T, LANES), lambda b, h, i: (b * nq + i, h))
    return _call(
        body, name="attn_a_fwd", grid=(batch, n_pairs, nq),
        in_specs=[blk,
                  pl.BlockSpec((s, LANES), lambda b, h, i: (b, n_pairs + h)),
                  pl.BlockSpec((s, LANES), lambda b, h, i: (b, v_col0 + h)),
                  pl.BlockSpec((nq, DA_T, DA_T), lambda b, h, i: (0, 0, 0))],
        out_specs=[blk, blk],
        out_shape=[jax.ShapeDtypeStruct((t, n_pairs * LANES), BF16), jax.ShapeDtypeStruct((t, n_pairs * LANES), F32)],
        scratch=[pltpu.VMEM((DA_T, LANES), F32), pltpu.VMEM((2 * DA_T, 1), F32), pltpu.VMEM((2 * DA_T, LANES), F32)],
        sem=("parallel", "parallel", "arbitrary"), args=(qk, qk, proj, bias), ride=ride)


def _da_bwd(qk, proj, v_col0, bias, o, lse, do, batch, s, ride=None):
    t = qk.shape[0]
    nq = s // DA_T
    n_pairs = 4
    scale = HEAD_DIM ** -0.5

    def body(q_ref, k_ref, v_ref, b_ref, o_ref, lse_ref, do_ref, dq_ref, dk_ref, dv_ref, dk_acc, dv_acc, dq_acc):
        i = pl.program_id(2)
        lo = lax.broadcasted_iota(jnp.int32, (DA_T, LANES), 1) < HEAD_DIM

        @pl.when(i == 0)
        def _():
            dk_acc[...] = jnp.zeros_like(dk_acc)
            dv_acc[...] = jnp.zeros_like(dv_acc)

        do_ = do_ref[...]
        qq = _stack_heads(q_ref[...] * scale, lo)
        dd = _stack_heads(do_, lo)
        prod = do_.astype(F32) * o_ref[...].astype(F32)
        fz = jnp.zeros_like(prod)
        delta = jnp.concatenate([jnp.sum(jnp.where(lo, prod, fz), axis=1, keepdims=True),
                                 jnp.sum(jnp.where(lo, fz, prod), axis=1, keepdims=True)], axis=0)
        lse_t = lse_ref[...]
        lse2 = jnp.concatenate([lse_t[:, 0:1], lse_t[:, HEAD_DIM:HEAD_DIM + 1]], axis=0)
        dq_acc[...] = jnp.zeros_like(dq_acc)

        def unit(dlt, carry):
            rows = pl.ds(pl.multiple_of((i - dlt) * DA_T, DA_T), DA_T)
            k = k_ref[rows, :]
            v = v_ref[rows, :]
            kz = jnp.zeros_like(k)
            bias_t = b_ref[dlt]
            sc = lax.dot_general(qq, k, NT, preferred_element_type=F32) + jnp.concatenate([bias_t, bias_t], axis=0)
            p = jnp.exp(sc - lse2)
            dp = lax.dot_general(dd, v, NT, preferred_element_type=F32)
            ds = (p * (dp - delta)).astype(BF16)
            dq_acc[...] += (lax.dot_general(ds[:DA_T], jnp.where(lo, k, kz), NN, preferred_element_type=F32)
                            + lax.dot_general(ds[DA_T:], jnp.where(lo, kz, k), NN, preferred_element_type=F32))
            dk_acc[rows, :] += lax.dot_general(ds, qq, TN, preferred_element_type=F32)
            dv_acc[rows, :] += lax.dot_general(p.astype(BF16), dd, TN, preferred_element_type=F32)
            return carry

        lax.fori_loop(0, i + 1, unit, 0)
        dq_ref[...] = (dq_acc[...] * scale).astype(BF16)

        @pl.when(i == nq - 1)
        def _():
            dk_ref[...] = dk_acc[...].astype(BF16)
            dv_ref[...] = dv_acc[...].astype(BF16)

    blk = pl.BlockSpec((DA_T, LANES), lambda b, h, i: (b * nq + i, h))
    seq = pl.BlockSpec((s, LANES), lambda b, h, i: (b, h))
    out = jax.ShapeDtypeStruct((t, n_pairs * LANES), BF16)
    return _call(
        body, name="attn_a_bwd", grid=(batch, n_pairs, nq),
        in_specs=[blk,
                  pl.BlockSpec((s, LANES), lambda b, h, i: (b, n_pairs + h)),
                  pl.BlockSpec((s, LANES), lambda b, h, i: (b, v_col0 + h)),
                  pl.BlockSpec((nq, DA_T, DA_T), lambda b, h, i: (0, 0, 0)),
                  blk, blk, blk],
        out_specs=[blk, seq, seq], out_shape=[out, out, out],
        scratch=[pltpu.VMEM((s, LANES), F32), pltpu.VMEM((s, LANES), F32), pltpu.VMEM((DA_T, LANES), F32)],
        sem=("parallel", "parallel", "arbitrary"), args=(qk, qk, proj, bias, o, lse, do), ride=ride)


SB_Q = 256


def _sb_consts(after):
    r = lax.broadcasted_iota(jnp.int32, (2 * BLOCK, 2 * BLOCK), 0) % BLOCK
    c = lax.broadcasted_iota(jnp.int32, (2 * BLOCK, 2 * BLOCK), 1)
    tri = (r > c) if after else (r < c)
    return jnp.logical_or(c >= BLOCK, tri).astype(BF16)


def _split_dot(x, mat):
    hi = x.astype(BF16)
    lo = (x - hi.astype(F32)).astype(BF16)
    return lax.dot_general(jnp.concatenate([hi, lo], axis=1), mat, NN, preferred_element_type=F32)


def _sb_scores(qq, k):
    z = lax.dot_general(qq, k, NT, preferred_element_type=F32)
    lsig = jnp.minimum(z, 0.0) - jnp.log(1.0 + jnp.exp(-jnp.abs(z)))
    return lsig, lsig - z


def _sb_fwd(proj, q_col0, k_col0, v_col0, batch, s, ride=None):
    t = proj.shape[0]
    nq = s // SB_Q
    n_pairs = 4
    scale = HEAD_DIM ** -0.5

    def body(q_ref, k_ref, v_ref, o_ref, tot_ref, acc_ref, run_ref):
        i = pl.program_id(2)
        lo_q = lax.broadcasted_iota(jnp.int32, (SB_Q, LANES), 1) < HEAD_DIM
        lo_k = _lane_lo()
        qq = _stack_heads(q_ref[...] * scale, lo_q)
        mat = _sb_consts(True)
        row = lax.broadcasted_iota(jnp.int32, (2 * SB_Q, LANES), 0) % SB_Q
        ahead = row - lax.broadcasted_iota(jnp.int32, (2 * SB_Q, LANES), 1)
        acc_ref[...] = jnp.zeros_like(acc_ref)
        run_ref[...] = jnp.zeros_like(run_ref)

        def unit(j, off):
            rows = pl.ds(pl.multiple_of(j * BLOCK, BLOCK), BLOCK)
            k = k_ref[rows, :]
            v = v_ref[rows, :]
            vz = jnp.zeros_like(v)
            lsig, lneg = _sb_scores(qq, k)
            if off is not None:
                valid = ahead > off
                lneg = jnp.where(valid, lneg, 0.0)
            sums = _split_dot(lneg, mat)
            run = run_ref[...]
            a = jnp.exp(lsig + run + sums[:, :BLOCK])
            if off is not None:
                a = jnp.where(valid, a, 0.0)
            run_ref[...] = run + sums[:, BLOCK:]
            ab = a.astype(BF16)
            acc_ref[...] += (lax.dot_general(ab[:SB_Q], jnp.where(lo_k, v, vz), NN, preferred_element_type=F32)
                             + lax.dot_general(ab[SB_Q:], jnp.where(lo_k, vz, v), NN, preferred_element_type=F32))

        unit(2 * i + 1, BLOCK)
        unit(2 * i, 0)

        def pair(p, carry):
            jp = i - 1 - p
            unit(2 * jp + 1, None)
            unit(2 * jp, None)
            return carry

        lax.fori_loop(0, i, pair, 0)
        o_ref[...] = acc_ref[...].astype(BF16)
        tot_ref[...] = jnp.where(lo_q, run_ref[0:SB_Q, :], run_ref[SB_Q:2 * SB_Q, :])

    def seq(col0):
        return pl.BlockSpec((s, LANES), lambda b, h, i: (b, col0 + h))

    blk = pl.BlockSpec((SB_Q, LANES), lambda b, h, i: (b * nq + i, h))
    return _call(
        body, name="attn_b_fwd", grid=(batch, n_pairs, nq),
        in_specs=[pl.BlockSpec((SB_Q, LANES), lambda b, h, i: (b * nq + i, q_col0 + h)), seq(k_col0), seq(v_col0)],
        out_specs=[blk, blk],
        out_shape=[jax.ShapeDtypeStruct((t, n_pairs * LANES), BF16), jax.ShapeDtypeStruct((t, n_pairs * LANES), F32)],
        scratch=[pltpu.VMEM((SB_Q, LANES), F32), pltpu.VMEM((2 * SB_Q, LANES), F32)],
        sem=("parallel", "parallel", "arbitrary"), args=(proj, proj, proj), ride=ride)


def _sb_bwd(proj, q_col0, k_col0, v_col0, tot, do, batch, s, ride=None):
    t = proj.shape[0]
    nq = s // SB_Q
    n_pairs = 4
    scale = HEAD_DIM ** -0.5

    def body(q_ref, k_ref, v_ref, tot_ref, do_ref, dq_ref, dk_ref, dv_ref, dk_acc, dv_acc, dq_acc, seen_ref, gsum_ref):
        i = pl.program_id(2)
        lo_q = lax.broadcasted_iota(jnp.int32, (SB_Q, LANES), 1) < HEAD_DIM
        lo_k = _lane_lo()

        @pl.when(i == 0)
        def _():
            dk_acc[...] = jnp.zeros_like(dk_acc)
            dv_acc[...] = jnp.zeros_like(dv_acc)

        qq = _stack_heads(q_ref[...] * scale, lo_q)
        dd = _stack_heads(do_ref[...], lo_q)
        tot_t = tot_ref[...]
        total = jnp.concatenate([jnp.broadcast_to(tot_t[:, 0:1], (SB_Q, LANES)),
                                 jnp.broadcast_to(tot_t[:, HEAD_DIM:HEAD_DIM + 1], (SB_Q, LANES))], axis=0)
        mat_after = _sb_consts(True)
        mat_before = _sb_consts(False)
        row = lax.broadcasted_iota(jnp.int32, (2 * SB_Q, LANES), 0) % SB_Q
        ahead = row - lax.broadcasted_iota(jnp.int32, (2 * SB_Q, LANES), 1)
        dq_acc[...] = jnp.zeros_like(dq_acc)
        seen_ref[...] = jnp.zeros_like(seen_ref)
        gsum_ref[...] = jnp.zeros_like(gsum_ref)

        def unit(j, off):
            rows = pl.ds(pl.multiple_of(j * BLOCK, BLOCK), BLOCK)
            k = k_ref[rows, :]
            v = v_ref[rows, :]
            kz = jnp.zeros_like(k)
            lsig, lneg = _sb_scores(qq, k)
            if off is not None:
                valid = ahead > off
                lneg = jnp.where(valid, lneg, 0.0)
            sums = _split_dot(lneg, mat_after)
            seen = seen_ref[...]
            a = jnp.exp(lsig + (total - seen - sums[:, BLOCK:]) + sums[:, :BLOCK])
            if off is not None:
                a = jnp.where(valid, a, 0.0)
            seen_ref[...] = seen + sums[:, BLOCK:]
            g = a * lax.dot_general(dd, v, NT, preferred_element_type=F32)
            gs = _split_dot(g, mat_before)
            gsum = gsum_ref[...]
            dz = g - jnp.exp(lsig) * (g + gsum + gs[:, :BLOCK])
            if off is not None:
                dz = jnp.where(valid, dz, 0.0)
            gsum_ref[...] = gsum + gs[:, BLOCK:]
            dzb = dz.astype(BF16)
            dq_acc[...] += (lax.dot_general(dzb[:SB_Q], jnp.where(lo_k, k, kz), NN, preferred_element_type=F32)
                            + lax.dot_general(dzb[SB_Q:], jnp.where(lo_k, kz, k), NN, preferred_element_type=F32))
            dk_acc[rows, :] += lax.dot_general(dzb, qq, TN, preferred_element_type=F32)
            dv_acc[rows, :] += lax.dot_general(a.astype(BF16), dd, TN, preferred_element_type=F32)

        def pair(p, carry):
            unit(2 * p, None)
            unit(2 * p + 1, None)
            return carry

        lax.fori_loop(0, i, pair, 0)
        unit(2 * i, 0)
        unit(2 * i + 1, BLOCK)
        dq_ref[...] = (dq_acc[...] * scale).astype(BF16)

        @pl.when(i == nq - 1)
        def _():
            dk_ref[...] = dk_acc[...].astype(BF16)
            dv_ref[...] = dv_acc[...].astype(BF16)

    def seq_in(col0):
        return pl.BlockSpec((s, LANES), lambda b, h, i: (b, col0 + h))

    blk = pl.BlockSpec((SB_Q, LANES), lambda b, h, i: (b * nq + i, h))
    seq = pl.BlockSpec((s, LANES), lambda b, h, i: (b, h))
    out = jax.ShapeDtypeStruct((t, n_pairs * LANES), BF16)
    return _call(
        body, name="attn_b_bwd", grid=(batch, n_pairs, nq),
        in_specs=[pl.BlockSpec((SB_Q, LANES), lambda b, h, i: (b * nq + i, q_col0 + h)), seq_in(k_col0), seq_in(v_col0),
                  blk, blk],
        out_specs=[blk, seq, seq], out_shape=[out, out, out],
        scratch=[pltpu.VMEM((s, LANES), F32), pltpu.VMEM((s, LANES), F32), pltpu.VMEM((SB_Q, LANES), F32),
                 pltpu.VMEM((2 * SB_Q, LANES), F32), pltpu.VMEM((2 * SB_Q, LANES), F32)],
        sem=("parallel", "parallel", "arbitrary"), args=(proj, proj, proj, tot, do), ride=ride)


MEM_Q_TILE = 256


def _mem_fwd(q, kv, batch, s, n_mem):
    t, width = q.shape
    tq = min(MEM_Q_TILE, s)
    nq = s // tq
    scale = MEM_HEAD_DIM ** -0.5

    def body(q_ref, kv_ref, o_ref):
        for h in range(N_HEADS_MEM):
            cols = slice(h * MEM_HEAD_DIM, (h + 1) * MEM_HEAD_DIM)
            k = kv_ref[:, cols]
            v = kv_ref[:, width + h * MEM_HEAD_DIM: width + (h + 1) * MEM_HEAD_DIM]
            sc = lax.dot_general(q_ref[:, cols], k, NT, preferred_element_type=F32) * scale
            p = jnp.exp(sc - jnp.max(sc, axis=1, keepdims=True))
            p = p / jnp.sum(p, axis=1, keepdims=True)
            o_ref[:, cols] = lax.dot_general(p.astype(BF16), v, NN, preferred_element_type=F32).astype(BF16)

    return pl.pallas_call(
        body, name="mem_attn_fwd", grid=(batch, nq),
        in_specs=[pl.BlockSpec((tq, width), lambda b, i: (b * nq + i, 0)),
                  pl.BlockSpec((n_mem, 2 * width), lambda b, i: (b, 0))],
        out_specs=pl.BlockSpec((tq, width), lambda b, i: (b * nq + i, 0)),
        out_shape=jax.ShapeDtypeStruct((t, width), BF16),
        compiler_params=_params(("parallel", "parallel")),
    )(q, kv)


def _mem_bwd(q, kv, do, batch, s, n_mem):
    t, width = q.shape
    tq = min(MEM_Q_TILE, s)
    nq = s // tq
    scale = MEM_HEAD_DIM ** -0.5

    def body(q_ref, kv_ref, do_ref, dq_ref, dkv_ref, acc):
        i = pl.program_id(1)

        @pl.when(i == 0)
        def _():
            acc[...] = jnp.zeros_like(acc)

        for h in range(N_HEADS_MEM):
            cols = slice(h * MEM_HEAD_DIM, (h + 1) * MEM_HEAD_DIM)
            vcols = slice(width + h * MEM_HEAD_DIM, width + (h + 1) * MEM_HEAD_DIM)
            qh, k, v, doh = q_ref[:, cols], kv_ref[:, cols], kv_ref[:, vcols], do_ref[:, cols]
            sc = lax.dot_general(qh, k, NT, preferred_element_type=F32) * scale
            p = jnp.exp(sc - jnp.max(sc, axis=1, keepdims=True))
            p = p / jnp.sum(p, axis=1, keepdims=True)
            dp = lax.dot_general(doh, v, NT, preferred_element_type=F32)
            ds = (p * (dp - jnp.sum(p * dp, axis=1, keepdims=True)) * scale).astype(BF16)
            dq_ref[:, cols] = lax.dot_general(ds, k, NN, preferred_element_type=F32).astype(BF16)
            acc[:, cols] += lax.dot_general(ds, qh, TN, preferred_element_type=F32)
            acc[:, vcols] += lax.dot_general(p.astype(BF16), doh, TN, preferred_element_type=F32)

        @pl.when(i == nq - 1)
        def _():
            dkv_ref[...] = acc[...].astype(BF16)

    row = pl.BlockSpec((tq, width), lambda b, i: (b * nq + i, 0))
    kvs = pl.BlockSpec((n_mem, 2 * width), lambda b, i: (b, 0))
    return pl.pallas_call(
        body, name="mem_attn_bwd", grid=(batch, nq),
        in_specs=[row, kvs, row], out_specs=[row, kvs],
        out_shape=[jax.ShapeDtypeStruct((t, width), BF16), jax.ShapeDtypeStruct((batch * n_mem, 2 * width), BF16)],
        scratch_shapes=[pltpu.VMEM((n_mem, 2 * width), F32)],
        compiler_params=_params(("parallel", "arbitrary")),
    )(q, kv, do)


def _mixer_fwd(o_a, o_b, w_a, w_b, proj, gate_col0):
    t, width = o_a.shape
    tm = min(ROW_TILE, t)
    n_sh, _, cs = w_a.shape

    def body(oa_ref, ob_ref, wa_ref, wb_ref, ga_ref, gb_ref, ua_ref, ub_ref, mix_ref):
        ua = lax.dot_general(oa_ref[...], wa_ref[...], NN, preferred_element_type=F32)
        ub = lax.dot_general(ob_ref[...], wb_ref[...], NN, preferred_element_type=F32)
        ua_ref[...] = ua.astype(BF16)
        ub_ref[...] = ub.astype(BF16)
        mix_ref[...] = (jax.nn.sigmoid(ga_ref[...].astype(F32)) * ua
                        + jax.nn.sigmoid(gb_ref[...].astype(F32)) * ub).astype(BF16)

    row = pl.BlockSpec((tm, width), lambda i, j: (i, 0))
    wsp = pl.BlockSpec((None, width, cs), lambda i, j: (j, 0, 0))
    out = pl.BlockSpec((tm, cs), lambda i, j: (i, j))
    osh = jax.ShapeDtypeStruct((t, n_sh * cs), BF16)
    return pl.pallas_call(
        body, name="mixer_fwd", grid=(t // tm, n_sh),
        in_specs=[row, row, wsp, wsp,
                  pl.BlockSpec((tm, cs), lambda i, j: (i, gate_col0 + j)),
                  pl.BlockSpec((tm, cs), lambda i, j: (i, gate_col0 + n_sh + j))],
        out_specs=[out, out, out], out_shape=[osh, osh, osh],
        compiler_params=_params(("parallel", "parallel")),
    )(o_a, o_b, w_a, w_b, proj, proj)


def _mixer_bwd(dmix, ua, ub, proj, gate_col0):
    t, d = dmix.shape
    tm = min(ROW_TILE, t)
    nc = d // LANES

    def body(dm_ref, ua_ref, ub_ref, ga_ref, gb_ref, dua_ref, dub_ref, dg_ref):
        dm = dm_ref[...].astype(F32)
        sa = jax.nn.sigmoid(ga_ref[...].astype(F32))
        sb = jax.nn.sigmoid(gb_ref[...].astype(F32))
        dua_ref[...] = (dm * sa).astype(BF16)
        dub_ref[...] = (dm * sb).astype(BF16)
        dg_ref[:, 0:d] = (dm * ua_ref[...].astype(F32) * sa * (1.0 - sa)).astype(BF16)
        dg_ref[:, d:2 * d] = (dm * ub_ref[...].astype(F32) * sb * (1.0 - sb)).astype(BF16)

    row = pl.BlockSpec((tm, d), lambda i: (i, 0))
    return pl.pallas_call(
        body, name="mixer_bwd", grid=(t // tm,),
        in_specs=[row, row, row,
                  pl.BlockSpec((tm, d), lambda i: (i, gate_col0 // nc)),
                  pl.BlockSpec((tm, d), lambda i: (i, gate_col0 // nc + 1))],
        out_specs=[row, row, pl.BlockSpec((tm, 2 * d), lambda i: (i, 0))],
        out_shape=[jax.ShapeDtypeStruct((t, d), BF16), jax.ShapeDtypeStruct((t, d), BF16),
                   jax.ShapeDtypeStruct((t, 2 * d), BF16)],
        compiler_params=_params(("parallel",)),
    )(dmix, ua, ub, proj, proj)


def _ffn_up(n, w_gate, w_up):
    t, d = n.shape
    tm = min(ROW_TILE, t)
    n_sh, _, cs = w_gate.shape

    def body(n_ref, wg_ref, wu_ref, hg_ref, hu_ref, act_ref):
        hg = lax.dot_general(n_ref[...], wg_ref[...], NN, preferred_element_type=F32)
        hu = lax.dot_general(n_ref[...], wu_ref[...], NN, preferred_element_type=F32)
        hg_ref[...] = hg.astype(BF16)
        hu_ref[...] = hu.astype(BF16)
        act_ref[...] = (hg * jax.nn.sigmoid(hg) * hu).astype(BF16)

    wsp = pl.BlockSpec((None, d, cs), lambda j, i: (j, 0, 0))
    out = pl.BlockSpec((None, tm, cs), lambda j, i: (j, i, 0))
    osh = jax.ShapeDtypeStruct((n_sh, t, cs), BF16)
    return pl.pallas_call(
        body, name="ffn_up", grid=(n_sh, t // tm),
        in_specs=[pl.BlockSpec((tm, d), lambda j, i: (i, 0)), wsp, wsp],
        out_specs=[out, out, out], out_shape=[osh, osh, osh],
        compiler_params=_params(("parallel", "parallel")),
    )(n, w_gate, w_up)


def _ffn_bwd_act(dh, w_down, hg, hu):
    t, d = dh.shape
    tm = min(ROW_TILE, t)
    n_sh, cs, _ = w_down.shape

    def body(dh_ref, wd_ref, hg_ref, hu_ref, dhg_ref, dhu_ref):
        dact = lax.dot_general(dh_ref[...], wd_ref[...], NT, preferred_element_type=F32)
        hg = hg_ref[...].astype(F32)
        sg = jax.nn.sigmoid(hg)
        dhu_ref[...] = (dact * hg * sg).astype(BF16)
        dhg_ref[...] = (dact * hu_ref[...].astype(F32) * sg * (1.0 + hg * (1.0 - sg))).astype(BF16)

    hid = pl.BlockSpec((None, tm, cs), lambda j, i: (j, i, 0))
    osh = jax.ShapeDtypeStruct((n_sh, t, cs), BF16)
    return pl.pallas_call(
        body, name="ffn_bwd_act", grid=(n_sh, t // tm),
        in_specs=[pl.BlockSpec((tm, d), lambda j, i: (i, 0)), pl.BlockSpec((None, cs, d), lambda j, i: (j, 0, 0)), hid, hid],
        out_specs=[hid, hid], out_shape=[osh, osh],
        compiler_params=_params(("parallel", "parallel")),
    )(dh, w_down, hg, hu)


def _mm_cols(name, a, w, out_dtype=BF16):
    t, k = a.shape
    n_sh, _, cs = w.shape
    tm = min(ROW_TILE, t)
    return _mm(name, a, w, grid=(n_sh, t // tm),
               a_spec=pl.BlockSpec((tm, k), lambda j, i: (i, 0)), b_spec=pl.BlockSpec((None, k, cs), lambda j, i: (j, 0, 0)),
               o_shape=(t, n_sh * cs), o_spec=pl.BlockSpec((tm, cs), lambda j, i: (i, j)), dims=NN, out_dtype=out_dtype)


def _mm_cols_t(name, a, w, out_dtype, res=None):
    t = a.shape[0]
    n_sh, k, cs = w.shape
    tm = min(ROW_TILE, t)
    o_spec = pl.BlockSpec((tm, k), lambda i, j: (i, 0))
    return _mm(name, a, w, grid=(t // tm, n_sh),
               a_spec=pl.BlockSpec((tm, cs), lambda i, j: (i, j)), b_spec=pl.BlockSpec((None, k, cs), lambda i, j: (j, 0, 0)),
               o_shape=(t, k), o_spec=o_spec, dims=NT, out_dtype=out_dtype, nk=n_sh, res=res,
               res_spec=o_spec if res is not None else None)


def _mm_hid_t(name, a, w, out_dtype, res=None):
    n_sh, t, cs = a.shape
    k = w.shape[1]
    tm = min(ROW_TILE, t)
    o_spec = pl.BlockSpec((tm, k), lambda i, j: (i, 0))
    return _mm(name, a, w, grid=(t // tm, n_sh),
               a_spec=pl.BlockSpec((None, tm, cs), lambda i, j: (j, i, 0)),
               b_spec=pl.BlockSpec((None, k, cs), lambda i, j: (j, 0, 0)),
               o_shape=(t, k), o_spec=o_spec, dims=NT, out_dtype=out_dtype, nk=n_sh, res=res,
               res_spec=o_spec if res is not None else None)


def _mm_hid(name, a, w, res):
    n_sh, t, cs = a.shape
    n = w.shape[2]
    tm = min(ROW_TILE, t)
    o_spec = pl.BlockSpec((tm, n), lambda i, j: (i, 0))
    return _mm(name, a, w, grid=(t // tm, n_sh),
               a_spec=pl.BlockSpec((None, tm, cs), lambda i, j: (j, i, 0)),
               b_spec=pl.BlockSpec((None, cs, n), lambda i, j: (j, 0, 0)),
               o_shape=(t, n), o_spec=o_spec, dims=NN, out_dtype=F32, nk=n_sh, res=res, res_spec=o_spec)


def _mm_full(name, a, w, out_dtype, dims=NN, res=None):
    t, k = a.shape
    n = w.shape[1] if dims == NN else w.shape[0]
    tm = min(ROW_TILE, t)
    o_spec = pl.BlockSpec((tm, n), lambda i: (i, 0))
    return _mm(name, a, w, grid=(t // tm,),
               a_spec=pl.BlockSpec((tm, k), lambda i: (i, 0)), b_spec=pl.BlockSpec(w.shape, lambda i: (0, 0)),
               o_shape=(t, n), o_spec=o_spec, dims=dims, out_dtype=out_dtype, res=res,
               res_spec=o_spec if res is not None else None)


def _wgrad_cols(name, a, g, n_sh):
    t, k = a.shape
    cs = g.shape[1] // n_sh
    tm = min(ROW_TILE, t)
    return _mm(name, a, g, grid=(n_sh, t // tm),
               a_spec=pl.BlockSpec((tm, k), lambda j, r: (r, 0)), b_spec=pl.BlockSpec((tm, cs), lambda j, r: (r, j)),
               o_shape=(n_sh, k, cs), o_spec=pl.BlockSpec((None, k, cs), lambda j, r: (j, 0, 0)), dims=TN,
               out_dtype=BF16, nk=t // tm)


def _wgrad_hid_cols(name, a, g):
    t, k = a.shape
    n_sh, _, cs = g.shape
    tm = min(ROW_TILE, t)
    return _mm(name, a, g, grid=(n_sh, t // tm),
               a_spec=pl.BlockSpec((tm, k), lambda j, r: (r, 0)), b_spec=pl.BlockSpec((None, tm, cs), lambda j, r: (j, r, 0)),
               o_shape=(n_sh, k, cs), o_spec=pl.BlockSpec((None, k, cs), lambda j, r: (j, 0, 0)), dims=TN,
               out_dtype=BF16, nk=t // tm)


def _wgrad_hid_rows(name, a, g):
    n_sh, t, cs = a.shape
    n = g.shape[1]
    tm = min(ROW_TILE, t)
    return _mm(name, a, g, grid=(n_sh, t // tm),
               a_spec=pl.BlockSpec((None, tm, cs), lambda j, r: (j, r, 0)), b_spec=pl.BlockSpec((tm, n), lambda j, r: (r, 0)),
               o_shape=(n_sh, cs, n), o_spec=pl.BlockSpec((None, cs, n), lambda j, r: (j, 0, 0)), dims=TN,
               out_dtype=BF16, nk=t // tm)


def _wgrad_rows(name, a, g, n_sh):
    t, k = a.shape
    n = g.shape[1]
    rs = k // n_sh
    tm = min(ROW_TILE, t)
    return _mm(name, a, g, grid=(n_sh, t // tm),
               a_spec=pl.BlockSpec((tm, rs), lambda j, r: (r, j)), b_spec=pl.BlockSpec((tm, n), lambda j, r: (r, 0)),
               o_shape=(n_sh, rs, n), o_spec=pl.BlockSpec((None, rs, n), lambda j, r: (j, 0, 0)), dims=TN,
               out_dtype=BF16, nk=t // tm)


def _peers():
    x, y, c = lax.axis_index("x"), lax.axis_index("y"), lax.axis_index("c")
    me = 4 * x + 2 * y + c
    out = []
    for k in range(1, N_DEV):
        kx, ky, kc = (k >> 2) & 1, (k >> 1) & 1, k & 1
        px = 1 - x if kx else x
        py = 1 - y if ky else y
        pc = 1 - c if kc else c
        out.append(((px, py, pc), 4 * px + 2 * py + pc))
    return me, out


def _cast_weights(ws):
    def body(*refs):
        n = len(refs) // 2
        for i_ref, o_ref in zip(refs[:n], refs[n:]):
            o_ref[...] = i_ref[...].astype(BF16)

    return pl.pallas_call(
        body, name="cast_weights", in_specs=[VMEM] * len(ws), out_specs=[VMEM] * len(ws),
        out_shape=[jax.ShapeDtypeStruct(w.shape, BF16) for w in ws],
    )(*ws)


def _exchange_copies(ins, outs, sems, gather, landed):
    send_sems, recv_sems, loc_sems = sems
    n_peer = N_DEV - 1
    me, peers = _peers()
    local = [pltpu.make_async_copy(ins[w] if gather else ins[w].at[me], outs[w].at[me], loc_sems.at[w])
             for w in range(len(ins))]
    remote = [pltpu.make_async_remote_copy(
        src_ref=ins[w] if gather else ins[w].at[idx], dst_ref=outs[w].at[idx if landed else me],
        send_sem=send_sems.at[w * n_peer + k], recv_sem=recv_sems.at[w * n_peer + k],
        device_id=dev, device_id_type=pl.DeviceIdType.MESH)
        for k, (dev, idx) in enumerate(peers) for w in range(len(ins))]
    return local, remote


def _exchange_start(ins, outs, sems, gather):
    local, remote = _exchange_copies(ins, outs, sems, gather, False)
    for cp in local + remote:
        cp.start()


def _exchange_wait(ins, outs, sems, gather):
    local, remote = _exchange_copies(ins, outs, sems, gather, True)
    for cp in local:
        cp.wait()
    for cp in remote:
        cp.wait_send()
        cp.wait_recv()


def _exchange_shapes(arrs, gather):
    n = len(arrs)
    out_shape = [jax.ShapeDtypeStruct(((N_DEV,) + a.shape) if gather else a.shape, a.dtype) for a in arrs]
    sems = [pltpu.SemaphoreType.DMA((n * (N_DEV - 1),)), pltpu.SemaphoreType.DMA((n * (N_DEV - 1),)),
            pltpu.SemaphoreType.DMA((n,))]
    return out_shape, sems


def _exchange(name, arrs, gather):
    n = len(arrs)

    def body(*refs):
        ins, outs, sems = refs[:n], refs[n:2 * n], refs[2 * n:]
        _exchange_start(ins, outs, sems, gather)
        _exchange_wait(ins, outs, sems, gather)

    out_shape, sems = _exchange_shapes(arrs, gather)
    return pl.pallas_call(body, name=name, in_specs=[ANY] * n, out_specs=[ANY] * n, out_shape=out_shape,
                          scratch_shapes=sems)(*arrs)


def _call(body, *, name, grid, in_specs, out_specs, out_shape, scratch, sem, args, ride=None):
    if ride is None:
        outs = pl.pallas_call(body, name=name, grid=grid, in_specs=in_specs, out_specs=out_specs, out_shape=out_shape,
                              scratch_shapes=scratch, compiler_params=_params(sem))(*args)
        return outs, None
    arrs, gather = ride
    n, n_in, n_out, n_scr = len(arrs), len(in_specs), len(out_specs), len(scratch)
    x_shape, x_sems = _exchange_shapes(arrs, gather)

    def riding(*refs):
        ins, x_ins = refs[:n_in], refs[n_in:n_in + n]
        outs = refs[n_in + n:n_in + n + n_out]
        x_outs = refs[n_in + n + n_out:n_in + 2 * n + n_out]
        scr = refs[n_in + 2 * n + n_out:n_in + 2 * n + n_out + n_scr]
        sems = refs[n_in + 2 * n + n_out + n_scr:]
        first = functools.reduce(jnp.logical_and, [pl.program_id(a) == 0 for a in range(len(grid))])
        last = functools.reduce(jnp.logical_and, [pl.program_id(a) == g - 1 for a, g in enumerate(grid)])

        @pl.when(first)
        def _():
            _exchange_start(x_ins, x_outs, sems, gather)

        body(*ins, *outs, *scr)

        @pl.when(last)
        def _():
            _exchange_wait(x_ins, x_outs, sems, gather)

    res = pl.pallas_call(
        riding, name=name, grid=grid, in_specs=list(in_specs) + [ANY] * n, out_specs=list(out_specs) + [ANY] * n,
        out_shape=list(out_shape) + x_shape, scratch_shapes=list(scratch) + x_sems,
        compiler_params=_params(("arbitrary",) * len(grid)))(*args, *arrs)
    return res[:n_out], res[n_out:]


def _allreduce_small(v):
    def body(v_ref, o_ref, all_ref, send_sems, recv_sems):
        me, peers = _peers()
        all_ref[me] = v_ref[...]
        for k, (dev, idx) in enumerate(peers):
            pltpu.make_async_remote_copy(src_ref=v_ref, dst_ref=all_ref.at[me], send_sem=send_sems.at[k],
                                         recv_sem=recv_sems.at[k], device_id=dev,
                                         device_id_type=pl.DeviceIdType.MESH).start()
        for k, (dev, idx) in enumerate(peers):
            cp = pltpu.make_async_remote_copy(src_ref=v_ref, dst_ref=all_ref.at[idx], send_sem=send_sems.at[k],
                                              recv_sem=recv_sems.at[k], device_id=dev,
                                              device_id_type=pl.DeviceIdType.MESH)
            cp.wait_send()
            cp.wait_recv()
        tot = all_ref[0]
        for dvc in range(1, N_DEV):
            tot = tot + all_ref[dvc]
        o_ref[...] = tot

    return pl.pallas_call(
        body, name="allreduce_small", in_specs=[VMEM], out_specs=VMEM,
        out_shape=jax.ShapeDtypeStruct(v.shape, F32),
        scratch_shapes=[pltpu.VMEM((N_DEV,) + v.shape, F32), pltpu.SemaphoreType.DMA((N_DEV - 1,)),
                        pltpu.SemaphoreType.DMA((N_DEV - 1,))],
    )(v)


def _adam_math(g, w, m, v):
    m_new = ADAM_B1 * m + (1.0 - ADAM_B1) * g
    v_new = ADAM_B2 * v + (1.0 - ADAM_B2) * (g * g)
    m_hat = m_new / (1.0 - ADAM_B1 ** ADAM_STEP)
    v_hat = v_new / (1.0 - ADAM_B2 ** ADAM_STEP)
    delta = -ADAM_LR * (m_hat / (jnp.sqrt(v_hat) + ADAM_EPS) + ADAM_WD * w)
    return delta, m_new, v_new


def _adam(name, pieces, w, m, v):
    r, c = w.shape
    tr = r
    for cand in (256, 176, 128, 64):
        if r % cand == 0 and r > cand:
            tr = cand
            break

    def body(p_ref, w_ref, m_ref, v_ref, g_ref, d_ref, mo_ref, vo_ref):
        g = p_ref[0].astype(F32)
        for dvc in range(1, N_DEV):
            g = g + p_ref[dvc].astype(F32)
        delta, m_new, v_new = _adam_math(g, w_ref[...], m_ref[...], v_ref[...])
        g_ref[...] = g
        d_ref[...] = delta
        mo_ref[...] = m_new
        vo_ref[...] = v_new

    blk = pl.BlockSpec((tr, c), lambda i: (i, 0))
    osh = jax.ShapeDtypeStruct((r, c), F32)
    return pl.pallas_call(
        body, name=name, grid=(r // tr,),
        in_specs=[pl.BlockSpec((N_DEV, tr, c), lambda i: (0, i, 0)), blk, blk, blk],
        out_specs=[blk, blk, blk, blk], out_shape=[osh, osh, osh, osh],
        compiler_params=_params(("parallel",)),
    )(pieces, w, m, v)


def _adam_small(g, w, m, v):
    def body(g_ref, w_ref, m_ref, v_ref, d_ref, mo_ref, vo_ref):
        delta, m_new, v_new = _adam_math(g_ref[...], w_ref[...], m_ref[...], v_ref[...])
        d_ref[...] = delta
        mo_ref[...] = m_new
        vo_ref[...] = v_new

    osh = jax.ShapeDtypeStruct(g.shape, F32)
    return pl.pallas_call(body, name="adam_small", in_specs=[VMEM] * 4, out_specs=[VMEM] * 3,
                          out_shape=[osh, osh, osh])(g, w, m, v)


def _local_step(x, mem, pos, tgt, gains, w_in, shards, batch):
    g_mix, g_mem_q, g_mem_kv, g_ffn, g_final = gains
    t, d = x.shape
    s = t // batch
    n_mem = mem.shape[0] // batch
    n_sh = N_DEV
    width = shards[0].shape[0]
    nb = width // LANES

    lane = jnp.arange(LANES, dtype=jnp.int32) % HEAD_DIM
    sel_lo = (lane < ROPE_HALF).astype(F32)[None, :]
    sel_hi = ((lane >= ROPE_HALF) & (lane < 2 * ROPE_HALF)).astype(F32)[None, :]
    freqs = ROPE_THETA ** (-jnp.arange(ROPE_HALF, dtype=F32) / ROPE_HALF)
    inv_freq = jnp.where(lane < 2 * ROPE_HALF, freqs[lane % ROPE_HALF], 0.0)[None, :]
    cos_t, sin_a, sin_b = _rope_tables(pos, inv_freq, sel_lo, sel_hi)
    bias = _dilated_bias_tiles(s)

    n1 = _rms_fwd("norm_mix", x, g_mix)
    proj = _mm_cols("proj_in", n1, w_in)
    qk_a = _rope_apply("rope_fwd", proj, 0, 2 * nb, cos_t, sin_a, sin_b, 1.0)
    (o_a, lse_a), (w_up_a, w_up_b, w_out, w_q, w_kv, w_o) = _da_fwd(qk_a, proj, 2 * nb, bias, batch, s,
                                                                    ride=(shards[:6], True))
    (o_b, tot_b), (w_fg, w_fu, w_fd) = _sb_fwd(proj, 3 * nb, 4 * nb, 5 * nb, batch, s, ride=(shards[6:], True))
    w_out_m = w_out.reshape(d, d)
    w_q_m = w_q.reshape(d, -1)
    w_kv_m = w_kv.reshape(d, -1)
    ua, ub, mixed = _mixer_fwd(o_a, o_b, w_up_a, w_up_b, proj, 6 * nb)
    h1 = _mm_full("mix_out", mixed, w_out_m, F32, res=x)
    n2 = _rms_fwd("norm_mem_q", h1, g_mem_q)
    mem_n = _rms_fwd("norm_mem_kv", mem, g_mem_kv)
    q_m = _mm_full("mem_q", n2, w_q_m, BF16)
    kv_m = _mm_full("mem_kv", mem_n, w_kv_m, BF16)
    o_m = _mem_fwd(q_m, kv_m, batch, s, n_mem)
    h2 = _mm_cols_like_res("mem_out", o_m, w_o, h1)
    n3 = _rms_fwd("norm_ffn", h2, g_ffn)
    hg, hu, act = _ffn_up(n3, w_fg, w_fu)
    h3 = _mm_hid("ffn_down", act, w_fd, h2)
    loss_part, dh3, dg_final = _loss_head(h3, tgt, g_final.reshape(1, d))

    dh3_b = _to_bf16("dh3_bf16", dh3)
    dhg, dhu = _ffn_bwd_act(dh3_b, w_fd, hg, hu)
    gw_fd = _wgrad_hid_rows("gw_ffn_down", act, dh3_b)
    gw_fg = _wgrad_hid_cols("gw_ffn_gate", n3, dhg)
    gw_fu = _wgrad_hid_cols("gw_ffn_up", n3, dhu)
    dn3 = _mm_hid_t("dn_ffn_gate", dhg, w_fg, F32)
    dn3 = _mm_hid_t("dn_ffn_up", dhu, w_fu, F32, res=dn3)
    dh2, dg_ffn = _rms_bwd("norm_ffn_bwd", dn3, h2, g_ffn, dh3)

    dh2_b = _to_bf16("dh2_bf16", dh2)
    do_m = _mm_cols_t("mem_out_bwd", dh2_b, w_o, BF16)
    gw_o = _wgrad_cols("gw_mem_o", o_m, dh2_b, n_sh)
    dq_m, dkv_m = _mem_bwd(q_m, kv_m, do_m, batch, s, n_mem)
    gw_q = _wgrad_rows("gw_mem_q", n2, dq_m, n_sh)
    gw_kv = _wgrad_rows("gw_mem_kv", mem_n, dkv_m, n_sh)
    dn2 = _mm_full("mem_q_bwd", dq_m, w_q_m, F32, dims=NT)
    dmem_n = _mm_full("mem_kv_bwd", dkv_m, w_kv_m, F32, dims=NT)
    _, dg_mem_kv = _rms_bwd("norm_mem_kv_bwd", dmem_n, mem, g_mem_kv, None)
    dh1, dg_mem_q = _rms_bwd("norm_mem_q_bwd", dn2, h1, g_mem_q, dh2)

    dh1_b = _to_bf16("dh1_bf16", dh1)
    dmix = _mm_full("mix_out_bwd", dh1_b, w_out_m, BF16, dims=NT)
    gw_out = _wgrad_rows("gw_out", mixed, dh1_b, n_sh)
    dua, dub, dgates = _mixer_bwd(dmix, ua, ub, proj, 6 * nb)
    do_a = _mm_cols_t("up_a_bwd", dua, w_up_a, BF16)
    do_b = _mm_cols_t("up_b_bwd", dub, w_up_b, BF16)
    gw_ua = _wgrad_cols("gw_up_a", o_a, dua, n_sh)
    gw_ub = _wgrad_cols("gw_up_b", o_b, dub, n_sh)
    (dq_ar, dk_ar, dv_a), p_ffn = _da_bwd(qk_a, proj, 2 * nb, bias, o_a, lse_a, do_a, batch, s,
                                          ride=([gw_fg, gw_fu, gw_fd], False))
    dqk_a = _rope_apply("rope_bwd", jnp.concatenate([dq_ar, dk_ar], axis=1), 0, 2 * nb, cos_t, sin_a, sin_b, -1.0)
    mid = [gw_ua, gw_ub, gw_out.reshape(w_out.shape), gw_q.reshape(w_q.shape), gw_kv.reshape(w_kv.shape), gw_o]
    (dq_b, dk_b, dv_b), p_mid = _sb_bwd(proj, 3 * nb, 4 * nb, 5 * nb, tot_b, do_b, batch, s, ride=(mid, False))
    dproj = jnp.concatenate([dqk_a, dv_a, dq_b, dk_b, dv_b, dgates], axis=1)
    gw_in = _wgrad_cols("gw_in", n1, dproj, n_sh)
    dn1 = _mm_cols_t("proj_in_bwd", dproj, w_in, F32)
    grad_x, dg_mix = _rms_bwd("norm_mix_bwd", dn1, x, g_mix, dh1)
    return loss_part, grad_x, gw_in, list(p_mid) + list(p_ffn), (dg_mix, dg_mem_q, dg_mem_kv, dg_ffn, dg_final)


def _mm_cols_like_res(name, a, w, res):
    t, k = a.shape
    n_sh, _, cs = w.shape
    tm = min(ROW_TILE, t)
    o_spec = pl.BlockSpec((tm, cs), lambda j, i: (i, j))
    return _mm(name, a, w, grid=(n_sh, t // tm),
               a_spec=pl.BlockSpec((tm, k), lambda j, i: (i, 0)), b_spec=pl.BlockSpec((None, k, cs), lambda j, i: (j, 0, 0)),
               o_shape=(t, n_sh * cs), o_spec=o_spec, dims=NN, out_dtype=F32, res=res, res_spec=o_spec)


def _to_bf16(name, a):
    t, d = a.shape
    tm = min(ROW_TILE, t)

    def body(a_ref, o_ref):
        o_ref[...] = a_ref[...].astype(BF16)

    row = pl.BlockSpec((tm, d), lambda i: (i, 0))
    return pl.pallas_call(body, name=name, grid=(t // tm,), in_specs=[row], out_specs=row,
                          out_shape=jax.ShapeDtypeStruct((t, d), BF16), compiler_params=_params(("parallel",)))(a)


WEIGHTS = ("w_in", "w_up_a", "w_up_b", "w_out", "w_q_mem", "w_kv_mem", "w_o_mem", "w_ffn_gate", "w_ffn_up", "w_ffn_down")
GAINS = ("g_mix", "g_mem_q", "g_mem_kv", "g_ffn", "g_final")
ORDER = ("g_mix", "w_in", "w_up_a", "w_up_b", "w_out", "g_mem_q", "g_mem_kv", "w_q_mem", "w_kv_mem", "w_o_mem", "g_ffn",
         "w_ffn_gate", "w_ffn_up", "w_ffn_down", "g_final")


def kernel(x, mem, positions, g_mix, w_in, w_up_a, w_up_b, w_out, g_mem_q, g_mem_kv, w_q_mem, w_kv_mem, w_o_mem, g_ffn, w_ffn_gate, w_ffn_up, w_ffn_down, g_final, loss_target, m_g_mix, m_w_in, m_w_up_a, m_w_up_b, m_w_out, m_g_mem_q, m_g_mem_kv, m_w_q_mem, m_w_kv_mem, m_w_o_mem, m_g_ffn, m_w_ffn_gate, m_w_ffn_up, m_w_ffn_down, m_g_final, v_g_mix, v_w_in, v_w_up_a, v_w_up_b, v_w_out, v_g_mem_q, v_g_mem_kv, v_w_q_mem, v_w_kv_mem, v_w_o_mem, v_g_ffn, v_w_ffn_gate, v_w_ffn_up, v_w_ffn_down, v_g_final):
    given = dict(locals())
    batch, s, d = x.shape
    t = batch * s
    shard = {n: given[n].reshape(given[n].shape[-2:]) for n in WEIGHTS}
    gains = [given[n].reshape(1, d) for n in GAINS]

    cast = _cast_weights([shard[n] for n in WEIGHTS])
    (w_in_all,) = _exchange("gather_w_in", cast[:1], True)
    loss_part, grad_x, gw_in, pieces, dgains = _local_step(
        x.reshape(t, d), mem.reshape(-1, d), positions.reshape(t, 1), loss_target.reshape(t, d), gains, w_in_all,
        cast[1:], batch)
    pieces = list(_exchange("scatter_gw_in", [gw_in], False)) + pieces

    grad, delta, new_m, new_v = {}, {}, {}, {}
    for n, p in zip(WEIGHTS, pieces):
        m2, v2 = given["m_" + n].reshape(shard[n].shape), given["v_" + n].reshape(shard[n].shape)
        outs = _adam("adam_" + n, p, shard[n], m2, v2)
        grad[n], delta[n], new_m[n], new_v[n] = [o.reshape(given[n].shape) for o in outs]

    rows = jnp.concatenate(list(dgains) + [jnp.zeros((N_DEV - len(GAINS), d), F32)], axis=0)
    g_all = _allreduce_small(rows)
    w_all = jnp.concatenate(gains + [jnp.zeros((N_DEV - len(GAINS), d), F32)], axis=0)
    m_all = jnp.concatenate([given["m_" + n].reshape(1, d) for n in GAINS] + [jnp.zeros((N_DEV - len(GAINS), d), F32)], axis=0)
    v_all = jnp.concatenate([given["v_" + n].reshape(1, d) for n in GAINS] + [jnp.ones((N_DEV - len(GAINS), d), F32)], axis=0)
    d_all, mo_all, vo_all = _adam_small(g_all, w_all, m_all, v_all)
    for i, n in enumerate(GAINS):
        grad[n] = g_all[i].reshape(given[n].shape)
        delta[n] = d_all[i].reshape(given[n].shape)
        new_m[n] = mo_all[i].reshape(given[n].shape)
        new_v[n] = vo_all[i].reshape(given[n].shape)

    loss = lax.psum(loss_part[0, 0], ("x", "y", "c"))
    return (loss, grad_x.reshape(x.shape), *[grad[n] for n in ORDER], *[delta[n] for n in ORDER],
            *[new_m[n] for n in ORDER], *[new_v[n] for n in ORDER])
```

```python
import functools
import math

import jax
import jax.numpy as jnp
from jax import lax
from jax.experimental import pallas as pl
from jax.experimental.pallas import tpu as pltpu

F32 = jnp.float32
BF16 = jnp.bfloat16

N_DEV = 8
HEAD_DIM = 64
MEM_HEAD_DIM = 128
N_HEADS_MEM = 4
BLOCK = 128
DIL_PATTERNS = ((128, 1), (512, 4), (2048, 16))
ROPE_THETA = 500000.0
ROPE_HALF = 8
RMS_EPS = 1e-6
ADAM_LR, ADAM_B1, ADAM_B2, ADAM_EPS, ADAM_WD, ADAM_STEP = 0.001, 0.9, 0.999, 1e-08, 0.01, 10
NEG = -1e30
ROW_TILE = 512
LANES = 128

ANY = pl.BlockSpec(memory_space=pl.ANY)
VMEM = pl.BlockSpec(memory_space=pltpu.VMEM)
NN = (((1,), (0,)), ((), ()))
NT = (((1,), (1,)), ((), ()))
TN = (((0,), (0,)), ((), ()))


def _params(sem):
    return pltpu.CompilerParams(dimension_semantics=sem)


def _mm(name, a, b, *, grid, a_spec, b_spec, o_shape, o_spec, dims, out_dtype, nk=1, res=None, res_spec=None):
    has_res = res is not None

    def body(*refs):
        a_ref, b_ref = refs[0], refs[1]
        r_ref = refs[2] if has_res else None
        o_ref = refs[3] if has_res else refs[2]
        p = lax.dot_general(a_ref[...], b_ref[...], dims, preferred_element_type=F32)
        if nk == 1:
            if has_res:
                p = p + r_ref[...].astype(F32)
            o_ref[...] = p.astype(out_dtype)
            return
        acc_ref = refs[-1]
        k = pl.program_id(len(grid) - 1)

        @pl.when(k == 0)
        def _():
            acc_ref[...] = p

        @pl.when(k > 0)
        def _():
            acc_ref[...] += p

        @pl.when(k == nk - 1)
        def _():
            t = acc_ref[...]
            if has_res:
                t = t + r_ref[...].astype(F32)
            o_ref[...] = t.astype(out_dtype)

    o_block = tuple(d for d in o_spec.block_shape if d is not None)
    sem = ("parallel",) * (len(grid) - 1) + (("arbitrary",) if nk > 1 else ("parallel",))
    return pl.pallas_call(
        body, name=name, grid=grid,
        in_specs=[a_spec, b_spec] + ([res_spec] if has_res else []),
        out_specs=o_spec, out_shape=jax.ShapeDtypeStruct(o_shape, out_dtype),
        scratch_shapes=[pltpu.VMEM(o_block, F32)] if nk > 1 else [],
        compiler_params=_params(sem),
    )(*([a, b] + ([res] if has_res else [])))


def _rms_fwd(name, x, g):
    t, d = x.shape
    tm = min(ROW_TILE, t)

    def body(x_ref, g_ref, o_ref):
        xf = x_ref[...]
        r = lax.rsqrt(jnp.mean(xf * xf, axis=-1, keepdims=True) + RMS_EPS)
        o_ref[...] = (xf * r * g_ref[...]).astype(BF16)

    return pl.pallas_call(
        body, name=name, grid=(t // tm,),
        in_specs=[pl.BlockSpec((tm, d), lambda i: (i, 0)), pl.BlockSpec((1, d), lambda i: (0, 0))],
        out_specs=pl.BlockSpec((tm, d), lambda i: (i, 0)), out_shape=jax.ShapeDtypeStruct((t, d), BF16),
        compiler_params=_params(("parallel",)),
    )(x, g)


def _rms_bwd(name, dn, x, g, dres, want):
    t, d = x.shape
    tm = min(ROW_TILE, t)
    has_res = dres is not None

    def body(*refs):
        dn_ref, x_ref, g_ref = refs[0], refs[1], refs[2]
        r_ref = refs[3] if has_res else None
        dx_refs, dg_ref = refs[-1 - len(want):-1], refs[-1]
        xf = x_ref[...]
        r = lax.rsqrt(jnp.mean(xf * xf, axis=-1, keepdims=True) + RMS_EPS)
        xh = xf * r
        dnf = dn_ref[...].astype(F32)
        if want:
            dxh = dnf * g_ref[...]
            dx = r * (dxh - xh * jnp.mean(dxh * xh, axis=-1, keepdims=True))
            if has_res:
                dx = dx + r_ref[...]
            for kind, dx_ref in zip(want, dx_refs):
                dx_ref[...] = dx.astype(F32 if kind == "f32" else BF16)

        @pl.when(pl.program_id(0) == 0)
        def _():
            dg_ref[...] = jnp.zeros_like(dg_ref)

        dg_ref[...] += jnp.sum(dnf * xh, axis=0, keepdims=True)

    row = pl.BlockSpec((tm, d), lambda i: (i, 0))
    vec = pl.BlockSpec((1, d), lambda i: (0, 0))
    return pl.pallas_call(
        body, name=name, grid=(t // tm,),
        in_specs=[row, row, vec] + ([row] if has_res else []),
        out_specs=[row] * len(want) + [vec],
        out_shape=[jax.ShapeDtypeStruct((t, d), F32 if kind == "f32" else BF16) for kind in want]
        + [jax.ShapeDtypeStruct((1, d), F32)],
        compiler_params=_params(("arbitrary",)),
    )(*([dn, x, g] + ([dres] if has_res else [])))


def _loss_head(h, tgt, g):
    t, d = h.shape
    tm = min(ROW_TILE, t)

    def body(h_ref, t_ref, g_ref, loss_ref, dh_ref, dhb_ref, dg_ref):
        xf = h_ref[...]
        gv = g_ref[...]
        r = lax.rsqrt(jnp.mean(xf * xf, axis=-1, keepdims=True) + RMS_EPS)
        xh = xf * r
        e = xh * gv - t_ref[...]
        dy = e * (1.0 / d)
        dxh = dy * gv
        dh = r * (dxh - xh * jnp.mean(dxh * xh, axis=-1, keepdims=True))
        dh_ref[...] = dh
        dhb_ref[...] = dh.astype(BF16)

        @pl.when(pl.program_id(0) == 0)
        def _():
            dg_ref[...] = jnp.zeros_like(dg_ref)
            loss_ref[...] = jnp.zeros_like(loss_ref)

        dg_ref[...] += jnp.sum(dy * xh, axis=0, keepdims=True)
        part = jnp.sum(jnp.sum(e * e, axis=1, keepdims=True), axis=0, keepdims=True) * (0.5 / d)
        loss_ref[...] += jnp.broadcast_to(part, loss_ref.shape)

    row = pl.BlockSpec((tm, d), lambda i: (i, 0))
    vec = pl.BlockSpec((1, d), lambda i: (0, 0))
    return pl.pallas_call(
        body, name="loss_head", grid=(t // tm,),
        in_specs=[row, row, vec],
        out_specs=[pl.BlockSpec((8, LANES), lambda i: (0, 0)), row, row, vec],
        out_shape=[jax.ShapeDtypeStruct((8, LANES), F32), jax.ShapeDtypeStruct((t, d), F32),
                   jax.ShapeDtypeStruct((t, d), BF16), jax.ShapeDtypeStruct((1, d), F32)],
        compiler_params=_params(("arbitrary",)),
    )(h, tgt, g)


def _rope_tables(pos, inv_freq, sel_lo, sel_hi):
    t = pos.shape[0]
    tm = min(ROW_TILE, t)

    def body(p_ref, f_ref, lo_ref, hi_ref, c_ref, sa_ref, sb_ref):
        ang = p_ref[...].astype(F32) * f_ref[...]
        rot = lo_ref[...] + hi_ref[...]
        cs, sn = jnp.cos(ang), jnp.sin(ang)
        c_ref[...] = cs * rot + (1.0 - rot)
        sa_ref[...] = -sn * lo_ref[...]
        sb_ref[...] = sn * hi_ref[...]

    vec = pl.BlockSpec((1, LANES), lambda i: (0, 0))
    row = pl.BlockSpec((tm, LANES), lambda i: (i, 0))
    return pl.pallas_call(
        body, name="rope_tables", grid=(t // tm,),
        in_specs=[pl.BlockSpec((tm, 1), lambda i: (i, 0)), vec, vec, vec],
        out_specs=[row, row, row], out_shape=[jax.ShapeDtypeStruct((t, LANES), F32)] * 3,
        compiler_params=_params(("parallel",)),
    )(pos, inv_freq, sel_lo, sel_hi)


def _rope_apply(name, src, col0, n_cols, cos_t, sin_a, sin_b, sign):
    t = src.shape[0]
    tm = min(ROW_TILE, t)

    def body(x_ref, c_ref, sa_ref, sb_ref, o_ref):
        cs, sa, sb = c_ref[...], sign * sa_ref[...], sign * sb_ref[...]
        for c in range(n_cols):
            cols = slice(c * LANES, (c + 1) * LANES)
            xf = x_ref[:, cols].astype(F32)
            up = pltpu.roll(xf, LANES - ROPE_HALF, 1)
            dn = pltpu.roll(xf, ROPE_HALF, 1)
            o_ref[:, cols] = (xf * cs + up * sa + dn * sb).astype(BF16)

    wide = n_cols * LANES
    tab = pl.BlockSpec((tm, LANES), lambda i: (i, 0))
    return pl.pallas_call(
        body, name=name, grid=(t // tm,),
        in_specs=[pl.BlockSpec((tm, wide), lambda i: (i, col0 // n_cols)), tab, tab, tab],
        out_specs=pl.BlockSpec((tm, wide), lambda i: (i, 0)),
        out_shape=jax.ShapeDtypeStruct((t, wide), BF16),
        compiler_params=_params(("parallel",)),
    )(src, cos_t, sin_a, sin_b)


DA_T = 256


def _lane_lo():
    return lax.broadcasted_iota(jnp.int32, (BLOCK, LANES), 1) < HEAD_DIM


def _dilated_bias_tiles(s):
    n = s // DA_T
    dist = (jnp.arange(n, dtype=jnp.int32)[:, None, None] * DA_T
            + jnp.arange(DA_T, dtype=jnp.int32)[None, :, None] - jnp.arange(DA_T, dtype=jnp.int32)[None, None, :])
    cnt = jnp.zeros(dist.shape, F32)
    for window, dil in DIL_PATTERNS:
        cnt = cnt + ((dist >= 0) & (dist % dil == 0) & (dist <= window)).astype(F32)
    return jnp.where(cnt > 0, jnp.log(jnp.maximum(cnt, 1.0)), NEG)


def _stack_heads(x, lo):
    zero = jnp.zeros_like(x)
    return jnp.concatenate([jnp.where(lo, x, zero), jnp.where(lo, zero, x)], axis=0)


def _da_fwd(qk, proj, v_col0, bias, batch, s, ride=None):
    t = qk.shape[0]
    nq = s // DA_T
    n_pairs = 4
    scale = HEAD_DIM ** -0.5

    def body(q_ref, k_ref, v_ref, b_ref, o_ref, lse_ref, acc_ref, m_ref, l_ref):
        i = pl.program_id(2)
        lo = lax.broadcasted_iota(jnp.int32, (DA_T, LANES), 1) < HEAD_DIM
        qq = _stack_heads(q_ref[...] * scale, lo)
        ones = jnp.ones((DA_T, LANES), BF16)
        acc_ref[...] = jnp.zeros_like(acc_ref)
        m_ref[...] = jnp.full(m_ref.shape, NEG, F32)
        l_ref[...] = jnp.zeros_like(l_ref)

        def unit(dlt, carry):
            rows = pl.ds(pl.multiple_of((i - dlt) * DA_T, DA_T), DA_T)
            k = k_ref[rows, :]
            v = v_ref[rows, :]
            vz = jnp.zeros_like(v)
            bias_t = b_ref[dlt]
            sc = lax.dot_general(qq, k, NT, preferred_element_type=F32) + jnp.concatenate([bias_t, bias_t], axis=0)
            m_old = m_ref[...]
            m_new = jnp.maximum(m_old, jnp.max(sc, axis=1, keepdims=True))
            p = jnp.exp(sc - m_new).astype(BF16)
            alpha = jnp.exp(m_old - m_new)
            m_ref[...] = m_new
            l_ref[...] = alpha * l_ref[...] + lax.dot_general(p, ones, NN, preferred_element_type=F32)
            pv = (lax.dot_general(p[:DA_T], jnp.where(lo, v, vz), NN, preferred_element_type=F32)
                  + lax.dot_general(p[DA_T:], jnp.where(lo, vz, v), NN, preferred_element_type=F32))
            acc_ref[...] = acc_ref[...] * jnp.where(lo, alpha[:DA_T], alpha[DA_T:]) + pv
            return carry

        lax.fori_loop(0, i + 1, unit, 0)
        l_t = l_ref[...]
        o_ref[...] = (acc_ref[...] / jnp.where(lo, l_t[:DA_T], l_t[DA_T:])).astype(BF16)
        lse = m_ref[...] + jnp.log(l_t)
        lse_ref[...] = jnp.where(lo, lse[:DA_T], lse[DA_T:])

    blk = pl.BlockSpec((DA_T, LANES), lambda b, h, i: (b * nq + i, h))
    return _call(
        body, name="attn_a_fwd", grid=(batch, n_pairs, nq),
        in_specs=[blk,
                  pl.BlockSpec((s, LANES), lambda b, h, i: (b, n_pairs + h)),
                  pl.BlockSpec((s, LANES), lambda b, h, i: (b, v_col0 + h)),
                  pl.BlockSpec((nq, DA_T, DA_T), lambda b, h, i: (0, 0, 0))],
        out_specs=[blk, blk],
        out_shape=[jax.ShapeDtypeStruct((t, n_pairs * LANES), BF16), jax.ShapeDtypeStruct((t, n_pairs * LANES), F32)],
        scratch=[pltpu.VMEM((DA_T, LANES), F32), pltpu.VMEM((2 * DA_T, 1), F32), pltpu.VMEM((2 * DA_T, LANES), F32)],
        sem=("parallel", "parallel", "arbitrary"), args=(qk, qk, proj, bias), ride=ride)


def _da_bwd(qk, proj, v_col0, bias, o, lse, do, batch, s, ride=None):
    t = qk.shape[0]
    nq = s // DA_T
    n_pairs = 4
    scale = HEAD_DIM ** -0.5

    def body(q_ref, k_ref, v_ref, b_ref, o_ref, lse_ref, do_ref, dq_ref, dk_ref, dv_ref, dk_acc, dv_acc, dq_acc):
        i = pl.program_id(2)
        lo = lax.broadcasted_iota(jnp.int32, (DA_T, LANES), 1) < HEAD_DIM

        @pl.when(i == 0)
        def _():
            dk_acc[...] = jnp.zeros_like(dk_acc)
            dv_acc[...] = jnp.zeros_like(dv_acc)

        do_ = do_ref[...]
        qq = _stack_heads(q_ref[...] * scale, lo)
        dd = _stack_heads(do_, lo)
        prod = do_.astype(F32) * o_ref[...].astype(F32)
        fz = jnp.zeros_like(prod)
        delta = jnp.concatenate([jnp.sum(jnp.where(lo, prod, fz), axis=1, keepdims=True),
                                 jnp.sum(jnp.where(lo, fz, prod), axis=1, keepdims=True)], axis=0)
        lse_t = lse_ref[...]
        lse2 = jnp.concatenate([lse_t[:, 0:1], lse_t[:, HEAD_DIM:HEAD_DIM + 1]], axis=0)
        dq_acc[...] = jnp.zeros_like(dq_acc)

        def unit(dlt, carry):
            rows = pl.ds(pl.multiple_of((i - dlt) * DA_T, DA_T), DA_T)
            k = k_ref[rows, :]
            v = v_ref[rows, :]
            kz = jnp.zeros_like(k)
            bias_t = b_ref[dlt]
            sc = lax.dot_general(qq, k, NT, preferred_element_type=F32) + jnp.concatenate([bias_t, bias_t], axis=0)
            p = jnp.exp(sc - lse2)
            dp = lax.dot_general(dd, v, NT, preferred_element_type=F32)
            ds = (p * (dp - delta)).astype(BF16)
            dq_acc[...] += (lax.dot_general(ds[:DA_T], jnp.where(lo, k, kz), NN, preferred_element_type=F32)
                            + lax.dot_general(ds[DA_T:], jnp.where(lo, kz, k), NN, preferred_element_type=F32))
            dk_acc[rows, :] += lax.dot_general(ds, qq, TN, preferred_element_type=F32)
            dv_acc[rows, :] += lax.dot_general(p.astype(BF16), dd, TN, preferred_element_type=F32)
            return carry

        lax.fori_loop(0, i + 1, unit, 0)
        dq_ref[...] = (dq_acc[...] * scale).astype(BF16)

        @pl.when(i == nq - 1)
        def _():
            dk_ref[...] = dk_acc[...].astype(BF16)
            dv_ref[...] = dv_acc[...].astype(BF16)

    blk = pl.BlockSpec((DA_T, LANES), lambda b, h, i: (b * nq + i, h))
    seq = pl.BlockSpec((s, LANES), lambda b, h, i: (b, h))
    out = jax.ShapeDtypeStruct((t, n_pairs * LANES), BF16)
    return _call(
        body, name="attn_a_bwd", grid=(batch, n_pairs, nq),
        in_specs=[blk,
                  pl.BlockSpec((s, LANES), lambda b, h, i: (b, n_pairs + h)),
                  pl.BlockSpec((s, LANES), lambda b, h, i: (b, v_col0 + h)),
                  pl.BlockSpec((nq, DA_T, DA_T), lambda b, h, i: (0, 0, 0)),
                  blk, blk, blk],
        out_specs=[blk, seq, seq], out_shape=[out, out, out],
        scratch=[pltpu.VMEM((s, LANES), F32), pltpu.VMEM((s, LANES), F32), pltpu.VMEM((DA_T, LANES), F32)],
        sem=("parallel", "parallel", "arbitrary"), args=(qk, qk, proj, bias, o, lse, do), ride=ride)


SB_Q = 256


def _sb_consts(after):
    r = lax.broadcasted_iota(jnp.int32, (2 * BLOCK, 2 * BLOCK), 0) % BLOCK
    c = lax.broadcasted_iota(jnp.int32, (2 * BLOCK, 2 * BLOCK), 1)
    tri = (r > c) if after else (r < c)
    return jnp.logical_or(c >= BLOCK, tri).astype(BF16)


def _split_dot(x, mat):
    hi = x.astype(BF16)
    lo = (x - hi.astype(F32)).astype(BF16)
    return lax.dot_general(jnp.concatenate([hi, lo], axis=1), mat, NN, preferred_element_type=F32)


def _sb_scores(qq, k):
    z = lax.dot_general(qq, k, NT, preferred_element_type=F32)
    lsig = jnp.minimum(z, 0.0) - jnp.log(1.0 + jnp.exp(-jnp.abs(z)))
    return lsig, lsig - z


def _sb_fwd(proj, q_col0, k_col0, v_col0, batch, s, ride=None):
    t = proj.shape[0]
    nq = s // SB_Q
    n_pairs = 4
    scale = HEAD_DIM ** -0.5

    def body(q_ref, k_ref, v_ref, o_ref, tot_ref, acc_ref, run_ref):
        i = pl.program_id(2)
        lo_q = lax.broadcasted_iota(jnp.int32, (SB_Q, LANES), 1) < HEAD_DIM
        lo_k = _lane_lo()
        qq = _stack_heads(q_ref[...] * scale, lo_q)
        mat = _sb_consts(True)
        row = lax.broadcasted_iota(jnp.int32, (2 * SB_Q, LANES), 0) % SB_Q
        ahead = row - lax.broadcasted_iota(jnp.int32, (2 * SB_Q, LANES), 1)
        acc_ref[...] = jnp.zeros_like(acc_ref)
        run_ref[...] = jnp.zeros_like(run_ref)

        def unit(j, off):
            rows = pl.ds(pl.multiple_of(j * BLOCK, BLOCK), BLOCK)
            k = k_ref[rows, :]
            v = v_ref[rows, :]
            vz = jnp.zeros_like(v)
            lsig, lneg = _sb_scores(qq, k)
            if off is not None:
                valid = ahead > off
                lneg = jnp.where(valid, lneg, 0.0)
            sums = _split_dot(lneg, mat)
            run = run_ref[...]
            a = jnp.exp(lsig + run + sums[:, :BLOCK])
            if off is not None:
                a = jnp.where(valid, a, 0.0)
            run_ref[...] = run + sums[:, BLOCK:]
            ab = a.astype(BF16)
            acc_ref[...] += (lax.dot_general(ab[:SB_Q], jnp.where(lo_k, v, vz), NN, preferred_element_type=F32)
                             + lax.dot_general(ab[SB_Q:], jnp.where(lo_k, vz, v), NN, preferred_element_type=F32))

        unit(2 * i + 1, BLOCK)
        unit(2 * i, 0)

        def pair(p, carry):
            jp = i - 1 - p
            unit(2 * jp + 1, None)
            unit(2 * jp, None)
            return carry

        lax.fori_loop(0, i, pair, 0)
        o_ref[...] = acc_ref[...].astype(BF16)
        tot_ref[...] = jnp.where(lo_q, run_ref[0:SB_Q, :], run_ref[SB_Q:2 * SB_Q, :])

    def seq(col0):
        return pl.BlockSpec((s, LANES), lambda b, h, i: (b, col0 + h))

    blk = pl.BlockSpec((SB_Q, LANES), lambda b, h, i: (b * nq + i, h))
    return _call(
        body, name="attn_b_fwd", grid=(batch, n_pairs, nq),
        in_specs=[pl.BlockSpec((SB_Q, LANES), lambda b, h, i: (b * nq + i, q_col0 + h)), seq(k_col0), seq(v_col0)],
        out_specs=[blk, blk],
        out_shape=[jax.ShapeDtypeStruct((t, n_pairs * LANES), BF16), jax.ShapeDtypeStruct((t, n_pairs * LANES), F32)],
        scratch=[pltpu.VMEM((SB_Q, LANES), F32), pltpu.VMEM((2 * SB_Q, LANES), F32)],
        sem=("parallel", "parallel", "arbitrary"), args=(proj, proj, proj), ride=ride)


def _sb_bwd(proj, q_col0, k_col0, v_col0, tot, do, batch, s, ride=None):
    t = proj.shape[0]
    nq = s // SB_Q
    n_pairs = 4
    scale = HEAD_DIM ** -0.5

    def body(q_ref, k_ref, v_ref, tot_ref, do_ref, dq_ref, dk_ref, dv_ref, dk_acc, dv_acc, dq_acc, seen_ref, gsum_ref):
        i = pl.program_id(2)
        lo_q = lax.broadcasted_iota(jnp.int32, (SB_Q, LANES), 1) < HEAD_DIM
        lo_k = _lane_lo()

        @pl.when(i == 0)
        def _():
            dk_acc[...] = jnp.zeros_like(dk_acc)
            dv_acc[...] = jnp.zeros_like(dv_acc)

        qq = _stack_heads(q_ref[...] * scale, lo_q)
        dd = _stack_heads(do_ref[...], lo_q)
        tot_t = tot_ref[...]
        total = jnp.concatenate([jnp.broadcast_to(tot_t[:, 0:1], (SB_Q, LANES)),
                                 jnp.broadcast_to(tot_t[:, HEAD_DIM:HEAD_DIM + 1], (SB_Q, LANES))], axis=0)
        mat_after = _sb_consts(True)
        mat_before = _sb_consts(False)
        row = lax.broadcasted_iota(jnp.int32, (2 * SB_Q, LANES), 0) % SB_Q
        ahead = row - lax.broadcasted_iota(jnp.int32, (2 * SB_Q, LANES), 1)
        dq_acc[...] = jnp.zeros_like(dq_acc)
        seen_ref[...] = jnp.zeros_like(seen_ref)
        gsum_ref[...] = jnp.zeros_like(gsum_ref)

        def unit(j, off):
            rows = pl.ds(pl.multiple_of(j * BLOCK, BLOCK), BLOCK)
            k = k_ref[rows, :]
            v = v_ref[rows, :]
            kz = jnp.zeros_like(k)
            lsig, lneg = _sb_scores(qq, k)
            if off is not None:
                valid = ahead > off
                lneg = jnp.where(valid, lneg, 0.0)
            sums = _split_dot(lneg, mat_after)
            seen = seen_ref[...]
            a = jnp.exp(lsig + (total - seen - sums[:, BLOCK:]) + sums[:, :BLOCK])
            if off is not None:
                a = jnp.where(valid, a, 0.0)
            seen_ref[...] = seen + sums[:, BLOCK:]
            g = a * lax.dot_general(dd, v, NT, preferred_element_type=F32)
            gs = _split_dot(g, mat_before)
            gsum = gsum_ref[...]
            dz = g - jnp.exp(lsig) * (g + gsum + gs[:, :BLOCK])
            if off is not None:
                dz = jnp.where(valid, dz, 0.0)
            gsum_ref[...] = gsum + gs[:, BLOCK:]
            dzb = dz.astype(BF16)
            dq_acc[...] += (lax.dot_general(dzb[:SB_Q], jnp.where(lo_k, k, kz), NN, preferred_element_type=F32)
                            + lax.dot_general(dzb[SB_Q:], jnp.where(lo_k, kz, k), NN, preferred_element_type=F32))
            dk_acc[rows, :] += lax.dot_general(dzb, qq, TN, preferred_element_type=F32)
            dv_acc[rows, :] += lax.dot_general(a.astype(BF16), dd, TN, preferred_element_type=F32)

        def pair(p, carry):
            unit(2 * p, None)
            unit(2 * p + 1, None)
            return carry

        lax.fori_loop(0, i, pair, 0)
        unit(2 * i, 0)
        unit(2 * i + 1, BLOCK)
        dq_ref[...] = (dq_acc[...] * scale).astype(BF16)

        @pl.when(i == nq - 1)
        def _():
            dk_ref[...] = dk_acc[...].astype(BF16)
            dv_ref[...] = dv_acc[...].astype(BF16)

    def seq_in(col0):
        return pl.BlockSpec((s, LANES), lambda b, h, i: (b, col0 + h))

    blk = pl.BlockSpec((SB_Q, LANES), lambda b, h, i: (b * nq + i, h))
    seq = pl.BlockSpec((s, LANES), lambda b, h, i: (b, h))
    out = jax.ShapeDtypeStruct((t, n_pairs * LANES), BF16)
    return _call(
        body, name="attn_b_bwd", grid=(batch, n_pairs, nq),
        in_specs=[pl.BlockSpec((SB_Q, LANES), lambda b, h, i: (b * nq + i, q_col0 + h)), seq_in(k_col0), seq_in(v_col0),
                  blk, blk],
        out_specs=[blk, seq, seq], out_shape=[out, out, out],
        scratch=[pltpu.VMEM((s, LANES), F32), pltpu.VMEM((s, LANES), F32), pltpu.VMEM((SB_Q, LANES), F32),
                 pltpu.VMEM((2 * SB_Q, LANES), F32), pltpu.VMEM((2 * SB_Q, LANES), F32)],
        sem=("parallel", "parallel", "arbitrary"), args=(proj, proj, proj, tot, do), ride=ride)


MEM_Q_TILE = 256


def _mem_fwd(q, kv, batch, s, n_mem):
    t, width = q.shape
    tq = min(MEM_Q_TILE, s)
    nq = s // tq
    scale = MEM_HEAD_DIM ** -0.5

    def body(q_ref, kv_ref, o_ref):
        for h in range(N_HEADS_MEM):
            cols = slice(h * MEM_HEAD_DIM, (h + 1) * MEM_HEAD_DIM)
            k = kv_ref[:, cols]
            v = kv_ref[:, width + h * MEM_HEAD_DIM: width + (h + 1) * MEM_HEAD_DIM]
            sc = lax.dot_general(q_ref[:, cols], k, NT, preferred_element_type=F32) * scale
            p = jnp.exp(sc - jnp.max(sc, axis=1, keepdims=True))
            p = p / jnp.sum(p, axis=1, keepdims=True)
            o_ref[:, cols] = lax.dot_general(p.astype(BF16), v, NN, preferred_element_type=F32).astype(BF16)

    return pl.pallas_call(
        body, name="mem_attn_fwd", grid=(batch, nq),
        in_specs=[pl.BlockSpec((tq, width), lambda b, i: (b * nq + i, 0)),
                  pl.BlockSpec((n_mem, 2 * width), lambda b, i: (b, 0))],
        out_specs=pl.BlockSpec((tq, width), lambda b, i: (b * nq + i, 0)),
        out_shape=jax.ShapeDtypeStruct((t, width), BF16),
        compiler_params=_params(("parallel", "parallel")),
    )(q, kv)


def _mem_bwd(q, kv, do, batch, s, n_mem):
    t, width = q.shape
    tq = min(MEM_Q_TILE, s)
    nq = s // tq
    scale = MEM_HEAD_DIM ** -0.5

    def body(q_ref, kv_ref, do_ref, dq_ref, dkv_ref, acc):
        i = pl.program_id(1)

        @pl.when(i == 0)
        def _():
            acc[...] = jnp.zeros_like(acc)

        for h in range(N_HEADS_MEM):
            cols = slice(h * MEM_HEAD_DIM, (h + 1) * MEM_HEAD_DIM)
            vcols = slice(width + h * MEM_HEAD_DIM, width + (h + 1) * MEM_HEAD_DIM)
            qh, k, v, doh = q_ref[:, cols], kv_ref[:, cols], kv_ref[:, vcols], do_ref[:, cols]
            sc = lax.dot_general(qh, k, NT, preferred_element_type=F32) * scale
            p = jnp.exp(sc - jnp.max(sc, axis=1, keepdims=True))
            p = p / jnp.sum(p, axis=1, keepdims=True)
            dp = lax.dot_general(doh, v, NT, preferred_element_type=F32)
            ds = (p * (dp - jnp.sum(p * dp, axis=1, keepdims=True)) * scale).astype(BF16)
            dq_ref[:, cols] = lax.dot_general(ds, k, NN, preferred_element_type=F32).astype(BF16)
            acc[:, cols] += lax.dot_general(ds, qh, TN, preferred_element_type=F32)
            acc[:, vcols] += lax.dot_general(p.astype(BF16), doh, TN, preferred_element_type=F32)

        @pl.when(i == nq - 1)
        def _():
            dkv_ref[...] = acc[...].astype(BF16)

    row = pl.BlockSpec((tq, width), lambda b, i: (b * nq + i, 0))
    kvs = pl.BlockSpec((n_mem, 2 * width), lambda b, i: (b, 0))
    return pl.pallas_call(
        body, name="mem_attn_bwd", grid=(batch, nq),
        in_specs=[row, kvs, row], out_specs=[row, kvs],
        out_shape=[jax.ShapeDtypeStruct((t, width), BF16), jax.ShapeDtypeStruct((batch * n_mem, 2 * width), BF16)],
        scratch_shapes=[pltpu.VMEM((n_mem, 2 * width), F32)],
        compiler_params=_params(("parallel", "arbitrary")),
    )(q, kv, do)


def _mixer_fwd(o_a, o_b, w_a, w_b, proj, gate_col0):
    t, width = o_a.shape
    d = w_a.shape[1]
    tm = min(ROW_TILE, t)
    gb0 = gate_col0 * LANES // d

    def body(oa_ref, ob_ref, wa_ref, wb_ref, ga_ref, gb_ref, ua_ref, ub_ref, mix_ref):
        ua = lax.dot_general(oa_ref[...], wa_ref[...], NN, preferred_element_type=F32)
        ub = lax.dot_general(ob_ref[...], wb_ref[...], NN, preferred_element_type=F32)
        ua_ref[...] = ua.astype(BF16)
        ub_ref[...] = ub.astype(BF16)
        mix_ref[...] = (jax.nn.sigmoid(ga_ref[...].astype(F32)) * ua
                        + jax.nn.sigmoid(gb_ref[...].astype(F32)) * ub).astype(BF16)

    row = pl.BlockSpec((tm, width), lambda i: (i, 0))
    wsp = pl.BlockSpec((width, d), lambda i: (0, 0))
    out = pl.BlockSpec((tm, d), lambda i: (i, 0))
    osh = jax.ShapeDtypeStruct((t, d), BF16)
    return pl.pallas_call(
        body, name="mixer_fwd", grid=(t // tm,),
        in_specs=[row, row, wsp, wsp,
                  pl.BlockSpec((tm, d), lambda i: (i, gb0)), pl.BlockSpec((tm, d), lambda i: (i, gb0 + 1))],
        out_specs=[out, out, out], out_shape=[osh, osh, osh],
        compiler_params=_params(("parallel",)),
    )(o_a, o_b, w_a, w_b, proj, proj)


def _mixer_bwd(dmix, ua, ub, proj, gate_col0):
    t, d = dmix.shape
    tm = min(ROW_TILE, t)
    nc = d // LANES

    def body(dm_ref, ua_ref, ub_ref, ga_ref, gb_ref, dua_ref, dub_ref, dg_ref):
        dm = dm_ref[...].astype(F32)
        sa = jax.nn.sigmoid(ga_ref[...].astype(F32))
        sb = jax.nn.sigmoid(gb_ref[...].astype(F32))
        dua_ref[...] = (dm * sa).astype(BF16)
        dub_ref[...] = (dm * sb).astype(BF16)
        dg_ref[:, 0:d] = (dm * ua_ref[...].astype(F32) * sa * (1.0 - sa)).astype(BF16)
        dg_ref[:, d:2 * d] = (dm * ub_ref[...].astype(F32) * sb * (1.0 - sb)).astype(BF16)

    row = pl.BlockSpec((tm, d), lambda i: (i, 0))
    return pl.pallas_call(
        body, name="mixer_bwd", grid=(t // tm,),
        in_specs=[row, row, row,
                  pl.BlockSpec((tm, d), lambda i: (i, gate_col0 // nc)),
                  pl.BlockSpec((tm, d), lambda i: (i, gate_col0 // nc + 1))],
        out_specs=[row, row, pl.BlockSpec((tm, 2 * d), lambda i: (i, 0))],
        out_shape=[jax.ShapeDtypeStruct((t, d), BF16), jax.ShapeDtypeStruct((t, d), BF16),
                   jax.ShapeDtypeStruct((t, 2 * d), BF16)],
        compiler_params=_params(("parallel",)),
    )(dmix, ua, ub, proj, proj)


FFN_COLS = 1024


def _ffn_up(n, w_gate, w_up):
    t, d = n.shape
    hidden = w_gate.shape[1]
    tm = min(ROW_TILE, t)
    tn = min(FFN_COLS, hidden)

    def body(n_ref, wg_ref, wu_ref, hg_ref, hu_ref, act_ref):
        hg = lax.dot_general(n_ref[...], wg_ref[...], NN, preferred_element_type=F32)
        hu = lax.dot_general(n_ref[...], wu_ref[...], NN, preferred_element_type=F32)
        hg_ref[...] = hg.astype(BF16)
        hu_ref[...] = hu.astype(BF16)
        act_ref[...] = (hg * jax.nn.sigmoid(hg) * hu).astype(BF16)

    wsp = pl.BlockSpec((d, tn), lambda j, i: (0, j))
    out = pl.BlockSpec((tm, tn), lambda j, i: (i, j))
    osh = jax.ShapeDtypeStruct((t, hidden), BF16)
    return pl.pallas_call(
        body, name="ffn_up", grid=(hidden // tn, t // tm),
        in_specs=[pl.BlockSpec((tm, d), lambda j, i: (i, 0)), wsp, wsp],
        out_specs=[out, out, out], out_shape=[osh, osh, osh],
        compiler_params=_params(("parallel", "parallel")),
    )(n, w_gate, w_up)


def _ffn_bwd_act(dh, w_down, hg, hu):
    t, d = dh.shape
    hidden = w_down.shape[0]
    tm = min(ROW_TILE, t)
    tn = min(FFN_COLS, hidden)

    def body(dh_ref, wd_ref, hg_ref, hu_ref, dhg_ref, dhu_ref):
        dact = lax.dot_general(dh_ref[...], wd_ref[...], NT, preferred_element_type=F32)
        hg = hg_ref[...].astype(F32)
        sg = jax.nn.sigmoid(hg)
        dhu_ref[...] = (dact * hg * sg).astype(BF16)
        dhg_ref[...] = (dact * hu_ref[...].astype(F32) * sg * (1.0 + hg * (1.0 - sg))).astype(BF16)

    hid = pl.BlockSpec((tm, tn), lambda j, i: (i, j))
    osh = jax.ShapeDtypeStruct((t, hidden), BF16)
    return pl.pallas_call(
        body, name="ffn_bwd_act", grid=(hidden // tn, t // tm),
        in_specs=[pl.BlockSpec((tm, d), lambda j, i: (i, 0)), pl.BlockSpec((tn, d), lambda j, i: (j, 0)), hid, hid],
        out_specs=[hid, hid], out_shape=[osh, osh],
        compiler_params=_params(("parallel", "parallel")),
    )(dh, w_down, hg, hu)


MM_ROWS = 1024


def _mm_cols(name, a, w, out_dtype=BF16):
    t, k = a.shape
    n_sh, _, cs = w.shape
    tm = min(MM_ROWS, t)
    return _mm(name, a, w, grid=(n_sh, t // tm),
               a_spec=pl.BlockSpec((tm, k), lambda j, i: (i, 0)), b_spec=pl.BlockSpec((None, k, cs), lambda j, i: (j, 0, 0)),
               o_shape=(t, n_sh * cs), o_spec=pl.BlockSpec((tm, cs), lambda j, i: (i, j)), dims=NN, out_dtype=out_dtype)


def _mm_cols_t(name, a, w, out_dtype, res=None):
    t = a.shape[0]
    n_sh, k, cs = w.shape
    tm = min(MM_ROWS, t)
    o_spec = pl.BlockSpec((tm, k), lambda i, j: (i, 0))
    return _mm(name, a, w, grid=(t // tm, n_sh),
               a_spec=pl.BlockSpec((tm, cs), lambda i, j: (i, j)), b_spec=pl.BlockSpec((None, k, cs), lambda i, j: (j, 0, 0)),
               o_shape=(t, k), o_spec=o_spec, dims=NT, out_dtype=out_dtype, nk=n_sh, res=res,
               res_spec=o_spec if res is not None else None)


def _wgrad_cols(name, a, g, n_sh):
    t, k = a.shape
    cs = g.shape[1] // n_sh
    tm = min(MM_ROWS, t)
    return _mm(name, a, g, grid=(n_sh, t // tm),
               a_spec=pl.BlockSpec((tm, k), lambda j, r: (r, 0)), b_spec=pl.BlockSpec((tm, cs), lambda j, r: (r, j)),
               o_shape=(n_sh, k, cs), o_spec=pl.BlockSpec((None, k, cs), lambda j, r: (j, 0, 0)), dims=TN,
               out_dtype=BF16, nk=t // tm)


def _mm_w(name, a, w, out_dtype, dims=NN, res=None, tm=MM_ROWS, tn=1024):
    t, k = a.shape
    n = w.shape[1] if dims == NN else w.shape[0]
    tm, tn = min(tm, t), min(tn, n)
    o_spec = pl.BlockSpec((tm, tn), lambda j, i: (i, j))
    b_spec = pl.BlockSpec((k, tn), lambda j, i: (0, j)) if dims == NN else pl.BlockSpec((tn, k), lambda j, i: (j, 0))
    return _mm(name, a, w, grid=(n // tn, t // tm), a_spec=pl.BlockSpec((tm, k), lambda j, i: (i, 0)), b_spec=b_spec,
               o_shape=(t, n), o_spec=o_spec, dims=dims, out_dtype=out_dtype, res=res,
               res_spec=o_spec if res is not None else None)


def _wgrad(name, a, g, tk=1024, tn=1024):
    t, k = a.shape
    n = g.shape[1]
    tm, tk, tn = min(MM_ROWS, t), min(tk, k), min(tn, n)
    return _mm(name, a, g, grid=(k // tk, n // tn, t // tm),
               a_spec=pl.BlockSpec((tm, tk), lambda p, q, r: (r, p)), b_spec=pl.BlockSpec((tm, tn), lambda p, q, r: (r, q)),
               o_shape=(k, n), o_spec=pl.BlockSpec((tk, tn), lambda p, q, r: (p, q)), dims=TN, out_dtype=BF16, nk=t // tm)


def _peers():
    x, y, c = lax.axis_index("x"), lax.axis_index("y"), lax.axis_index("c")
    me = 4 * x + 2 * y + c
    out = []
    for k in range(1, N_DEV):
        kx, ky, kc = (k >> 2) & 1, (k >> 1) & 1, k & 1
        px = 1 - x if kx else x
        py = 1 - y if ky else y
        pc = 1 - c if kc else c
        out.append(((px, py, pc), 4 * px + 2 * py + pc))
    return me, out


def _cast_weights(ws, pads):
    def body(*refs):
        n = len(refs) // 2
        for i_ref, o_ref, (pr, pc) in zip(refs[:n], refs[n:], pads):
            r, c = i_ref.shape
            o_ref[0:r, 0:c] = i_ref[...].astype(BF16)
            if pr:
                o_ref[r:r + pr, :] = jnp.zeros((pr, c), BF16)
            if pc:
                o_ref[:, c:c + pc] = jnp.zeros((r, pc), BF16)

    return pl.pallas_call(
        body, name="cast_weights", in_specs=[VMEM] * len(ws), out_specs=[VMEM] * len(ws),
        out_shape=[jax.ShapeDtypeStruct((w.shape[0] + pr, w.shape[1] + pc), BF16) for w, (pr, pc) in zip(ws, pads)],
    )(*ws)


def _exchange_copies(ins, outs, sems, gather, cols, landed):
    send_sems, recv_sems, loc_sems = sems
    n_peer = N_DEV - 1
    me, peers = _peers()

    def win(ref, j, c):
        return ref.at[:, pl.ds(pl.multiple_of(j * c, LANES), c)]

    def src(w, j):
        if gather:
            return ins[w]
        return win(ins[w], j, cols[w]) if cols[w] else ins[w].at[j]

    def dst(w, j):
        return win(outs[w], j, cols[w]) if gather and cols[w] else outs[w].at[j]

    local = [pltpu.make_async_copy(src(w, me), dst(w, me), loc_sems.at[w]) for w in range(len(ins))]
    remote = [pltpu.make_async_remote_copy(
        src_ref=src(w, idx), dst_ref=dst(w, idx if landed else me),
        send_sem=send_sems.at[w * n_peer + k], recv_sem=recv_sems.at[w * n_peer + k],
        device_id=dev, device_id_type=pl.DeviceIdType.MESH)
        for k, (dev, idx) in enumerate(peers) for w in range(len(ins))]
    return local, remote


def _exchange_start(ins, outs, sems, gather, cols):
    local, remote = _exchange_copies(ins, outs, sems, gather, cols, False)
    for cp in local + remote:
        cp.start()


def _exchange_wait(ins, outs, sems, gather, cols):
    local, remote = _exchange_copies(ins, outs, sems, gather, cols, True)
    for cp in local:
        cp.wait()
    for cp in remote:
        cp.wait_send()
        cp.wait_recv()


def _exchange_shapes(arrs, gather, cols):
    n = len(arrs)
    out_shape = []
    for a, c in zip(arrs, cols):
        if gather:
            shape = (a.shape[0], N_DEV * c) if c else (N_DEV,) + a.shape
        else:
            shape = (N_DEV, a.shape[0], c) if c else a.shape
        out_shape.append(jax.ShapeDtypeStruct(shape, a.dtype))
    sems = [pltpu.SemaphoreType.DMA((n * (N_DEV - 1),)), pltpu.SemaphoreType.DMA((n * (N_DEV - 1),)),
            pltpu.SemaphoreType.DMA((n,))]
    return out_shape, sems


def _exchange(name, arrs, gather, cols):
    n = len(arrs)

    def body(*refs):
        ins, outs, sems = refs[:n], refs[n:2 * n], refs[2 * n:]
        _exchange_start(ins, outs, sems, gather, cols)
        _exchange_wait(ins, outs, sems, gather, cols)

    out_shape, sems = _exchange_shapes(arrs, gather, cols)
    return pl.pallas_call(body, name=name, in_specs=[ANY] * n, out_specs=[ANY] * n, out_shape=out_shape,
                          scratch_shapes=sems)(*arrs)


def _call(body, *, name, grid, in_specs, out_specs, out_shape, scratch, sem, args, ride=None):
    if ride is None:
        outs = pl.pallas_call(body, name=name, grid=grid, in_specs=in_specs, out_specs=out_specs, out_shape=out_shape,
                              scratch_shapes=scratch, compiler_params=_params(sem))(*args)
        return outs, None
    arrs, gather, cols = ride
    n, n_in, n_out, n_scr = len(arrs), len(in_specs), len(out_specs), len(scratch)
    x_shape, x_sems = _exchange_shapes(arrs, gather, cols)

    def riding(*refs):
        ins, x_ins = refs[:n_in], refs[n_in:n_in + n]
        outs = refs[n_in + n:n_in + n + n_out]
        x_outs = refs[n_in + n + n_out:n_in + 2 * n + n_out]
        scr = refs[n_in + 2 * n + n_out:n_in + 2 * n + n_out + n_scr]
        sems = refs[n_in + 2 * n + n_out + n_scr:]
        first = functools.reduce(jnp.logical_and, [pl.program_id(a) == 0 for a in range(len(grid))])
        last = functools.reduce(jnp.logical_and, [pl.program_id(a) == g - 1 for a, g in enumerate(grid)])

        @pl.when(first)
        def _():
            _exchange_start(x_ins, x_outs, sems, gather, cols)

        body(*ins, *outs, *scr)

        @pl.when(last)
        def _():
            _exchange_wait(x_ins, x_outs, sems, gather, cols)

    res = pl.pallas_call(
        riding, name=name, grid=grid, in_specs=list(in_specs) + [ANY] * n, out_specs=list(out_specs) + [ANY] * n,
        out_shape=list(out_shape) + x_shape, scratch_shapes=list(scratch) + x_sems,
        compiler_params=_params(("arbitrary",) * len(grid)))(*args, *arrs)
    return res[:n_out], res[n_out:]


def _allreduce_small(v):
    def body(v_ref, o_ref, all_ref, send_sems, recv_sems):
        me, peers = _peers()
        all_ref[me] = v_ref[...]
        for k, (dev, idx) in enumerate(peers):
            pltpu.make_async_remote_copy(src_ref=v_ref, dst_ref=all_ref.at[me], send_sem=send_sems.at[k],
                                         recv_sem=recv_sems.at[k], device_id=dev,
                                         device_id_type=pl.DeviceIdType.MESH).start()
        for k, (dev, idx) in enumerate(peers):
            cp = pltpu.make_async_remote_copy(src_ref=v_ref, dst_ref=all_ref.at[idx], send_sem=send_sems.at[k],
                                              recv_sem=recv_sems.at[k], device_id=dev,
                                              device_id_type=pl.DeviceIdType.MESH)
            cp.wait_send()
            cp.wait_recv()
        tot = all_ref[0]
        for dvc in range(1, N_DEV):
            tot = tot + all_ref[dvc]
        o_ref[...] = tot

    return pl.pallas_call(
        body, name="allreduce_small", in_specs=[VMEM], out_specs=VMEM,
        out_shape=jax.ShapeDtypeStruct(v.shape, F32),
        scratch_shapes=[pltpu.VMEM((N_DEV,) + v.shape, F32), pltpu.SemaphoreType.DMA((N_DEV - 1,)),
                        pltpu.SemaphoreType.DMA((N_DEV - 1,))],
    )(v)


def _adam_math(g, w, m, v):
    m_new = ADAM_B1 * m + (1.0 - ADAM_B1) * g
    v_new = ADAM_B2 * v + (1.0 - ADAM_B2) * (g * g)
    m_hat = m_new / (1.0 - ADAM_B1 ** ADAM_STEP)
    v_hat = v_new / (1.0 - ADAM_B2 ** ADAM_STEP)
    delta = -ADAM_LR * (m_hat / (jnp.sqrt(v_hat) + ADAM_EPS) + ADAM_WD * w)
    return delta, m_new, v_new


def _adam(name, pieces, w, m, v):
    r, c = w.shape
    cp = pieces.shape[2]
    tr = r
    for cand in (256, 176, 128, 64):
        if r % cand == 0 and r > cand:
            tr = cand
            break

    def body(p_ref, w_ref, m_ref, v_ref, g_ref, d_ref, mo_ref, vo_ref):
        g = p_ref[0, :, 0:c].astype(F32)
        for dvc in range(1, N_DEV):
            g = g + p_ref[dvc, :, 0:c].astype(F32)
        delta, m_new, v_new = _adam_math(g, w_ref[...], m_ref[...], v_ref[...])
        g_ref[...] = g
        d_ref[...] = delta
        mo_ref[...] = m_new
        vo_ref[...] = v_new

    blk = pl.BlockSpec((tr, c), lambda i: (i, 0))
    osh = jax.ShapeDtypeStruct((r, c), F32)
    return pl.pallas_call(
        body, name=name, grid=(r // tr,),
        in_specs=[pl.BlockSpec((N_DEV, tr, cp), lambda i: (0, i, 0)), blk, blk, blk],
        out_specs=[blk, blk, blk, blk], out_shape=[osh, osh, osh, osh],
        compiler_params=_params(("parallel",)),
    )(pieces, w, m, v)


def _adam_small(g, w, m, v):
    def body(g_ref, w_ref, m_ref, v_ref, d_ref, mo_ref, vo_ref):
        delta, m_new, v_new = _adam_math(g_ref[...], w_ref[...], m_ref[...], v_ref[...])
        d_ref[...] = delta
        mo_ref[...] = m_new
        vo_ref[...] = v_new

    osh = jax.ShapeDtypeStruct(g.shape, F32)
    return pl.pallas_call(body, name="adam_small", in_specs=[VMEM] * 4, out_specs=[VMEM] * 3,
                          out_shape=[osh, osh, osh])(g, w, m, v)


def _local_step(x, mem, pos, tgt, gains, w_in, shards, batch):
    g_mix, g_mem_q, g_mem_kv, g_ffn, g_final = gains
    t, d = x.shape
    s = t // batch
    n_mem = mem.shape[0] // batch
    n_sh = N_DEV
    width = shards[0].shape[0]
    nb = width // LANES

    lane = jnp.arange(LANES, dtype=jnp.int32) % HEAD_DIM
    sel_lo = (lane < ROPE_HALF).astype(F32)[None, :]
    sel_hi = ((lane >= ROPE_HALF) & (lane < 2 * ROPE_HALF)).astype(F32)[None, :]
    freqs = ROPE_THETA ** (-jnp.arange(ROPE_HALF, dtype=F32) / ROPE_HALF)
    inv_freq = jnp.where(lane < 2 * ROPE_HALF, freqs[lane % ROPE_HALF], 0.0)[None, :]
    cos_t, sin_a, sin_b = _rope_tables(pos, inv_freq, sel_lo, sel_hi)
    bias = _dilated_bias_tiles(s)

    n1 = _rms_fwd("norm_mix", x, g_mix)
    proj = _mm_cols("proj_in", n1, w_in)
    qk_a = _rope_apply("rope_fwd", proj, 0, 2 * nb, cos_t, sin_a, sin_b, 1.0)
    cs_up, cs_ffn = shards[0].shape[1], shards[6].shape[1]
    (o_a, lse_a), (w_up_a, w_up_b, w_out, w_q, w_kv, w_o) = _da_fwd(
        qk_a, proj, 2 * nb, bias, batch, s, ride=(shards[:6], True, (cs_up, cs_up, 0, 0, 0, cs_up)))
    (o_b, tot_b), (w_fg, w_fu, w_fd) = _sb_fwd(proj, 3 * nb, 4 * nb, 5 * nb, batch, s,
                                               ride=(shards[6:], True, (cs_ffn, cs_ffn, 0)))
    w_out = w_out.reshape(d, d)
    w_q = w_q.reshape(d, -1)
    w_kv = w_kv.reshape(d, -1)
    w_fd = w_fd.reshape(-1, d)
    ua, ub, mixed = _mixer_fwd(o_a, o_b, w_up_a, w_up_b, proj, 6 * nb)
    h1 = _mm_w("mix_out", mixed, w_out, F32, res=x)
    n2 = _rms_fwd("norm_mem_q", h1, g_mem_q)
    mem_n = _rms_fwd("norm_mem_kv", mem, g_mem_kv)
    q_m = _mm_w("mem_q", n2, w_q, BF16)
    kv_m = _mm_w("mem_kv", mem_n, w_kv, BF16)
    o_m = _mem_fwd(q_m, kv_m, batch, s, n_mem)
    h2 = _mm_w("mem_out", o_m, w_o, F32, res=h1)
    n3 = _rms_fwd("norm_ffn", h2, g_ffn)
    hg, hu, act = _ffn_up(n3, w_fg, w_fu)
    h3 = _mm_w("ffn_down", act, w_fd, F32, res=h2, tm=ROW_TILE)
    loss_part, dh3, dh3_b, dg_final = _loss_head(h3, tgt, g_final.reshape(1, d))

    dhg, dhu = _ffn_bwd_act(dh3_b, w_fd, hg, hu)
    gw_fd = _wgrad("gw_ffn_down", act, dh3_b)
    gw_fg = _wgrad("gw_ffn_gate", n3, dhg)
    gw_fu = _wgrad("gw_ffn_up", n3, dhu)
    dn3 = _mm_w("dn_ffn_gate", dhg, w_fg, F32, dims=NT, tm=ROW_TILE)
    dn3 = _mm_w("dn_ffn_up", dhu, w_fu, F32, dims=NT, res=dn3, tm=ROW_TILE)
    dh2, dh2_b, dg_ffn = _rms_bwd("norm_ffn_bwd", dn3, h2, g_ffn, dh3, ("f32", "bf16"))

    do_m = _mm_w("mem_out_bwd", dh2_b, w_o, BF16, dims=NT)
    gw_o = _wgrad("gw_mem_o", o_m, dh2_b)
    dq_m, dkv_m = _mem_bwd(q_m, kv_m, do_m, batch, s, n_mem)
    gw_q = _wgrad("gw_mem_q", n2, dq_m)
    gw_kv = _wgrad("gw_mem_kv", mem_n, dkv_m)
    dn2 = _mm_w("mem_q_bwd", dq_m, w_q, F32, dims=NT)
    dmem_n = _mm_w("mem_kv_bwd", dkv_m, w_kv, F32, dims=NT)
    (dg_mem_kv,) = _rms_bwd("norm_mem_kv_bwd", dmem_n, mem, g_mem_kv, None, ())
    dh1, dh1_b, dg_mem_q = _rms_bwd("norm_mem_q_bwd", dn2, h1, g_mem_q, dh2, ("f32", "bf16"))

    dmix = _mm_w("mix_out_bwd", dh1_b, w_out, BF16, dims=NT)
    gw_out = _wgrad("gw_out", mixed, dh1_b)
    dua, dub, dgates = _mixer_bwd(dmix, ua, ub, proj, 6 * nb)
    do_a = _mm_w("up_a_bwd", dua, w_up_a, BF16, dims=NT)
    do_b = _mm_w("up_b_bwd", dub, w_up_b, BF16, dims=NT)
    gw_ua = _wgrad("gw_up_a", o_a, dua)
    gw_ub = _wgrad("gw_up_b", o_b, dub)
    (dq_ar, dk_ar, dv_a), p_ffn = _da_bwd(
        qk_a, proj, 2 * nb, bias, o_a, lse_a, do_a, batch, s,
        ride=([gw_fg, gw_fu, gw_fd.reshape(n_sh, -1, d)], False, (cs_ffn, cs_ffn, 0)))
    dqk_a = _rope_apply("rope_bwd", jnp.concatenate([dq_ar, dk_ar], axis=1), 0, 2 * nb, cos_t, sin_a, sin_b, -1.0)
    mid = [gw_ua, gw_ub, gw_out.reshape(n_sh, -1, d), gw_q.reshape(n_sh, -1, gw_q.shape[1]),
           gw_kv.reshape(n_sh, -1, gw_kv.shape[1]), gw_o]
    (dq_b, dk_b, dv_b), p_mid = _sb_bwd(proj, 3 * nb, 4 * nb, 5 * nb, tot_b, do_b, batch, s,
                                        ride=(mid, False, (cs_up, cs_up, 0, 0, 0, cs_up)))
    dproj = jnp.concatenate([dqk_a, dv_a, dq_b, dk_b, dv_b, dgates], axis=1)
    gw_in = _wgrad_cols("gw_in", n1, dproj, n_sh)
    dn1 = _mm_cols_t("proj_in_bwd", dproj, w_in, F32)
    grad_x, dg_mix = _rms_bwd("norm_mix_bwd", dn1, x, g_mix, dh1, ("f32",))
    return loss_part, grad_x, gw_in, list(p_mid) + list(p_ffn), (dg_mix, dg_mem_q, dg_mem_kv, dg_ffn, dg_final)


WEIGHTS =("w_in", "w_up_a", "w_up_b", "w_out", "w_q_mem", "w_kv_mem", "w_o_mem", "w_ffn_gate", "w_ffn_up", "w_ffn_down")
GAINS = ("g_mix", "g_mem_q", "g_mem_kv", "g_ffn", "g_final")
ORDER = ("g_mix", "w_in", "w_up_a", "w_up_b", "w_out", "g_mem_q", "g_mem_kv", "w_q_mem", "w_kv_mem", "w_o_mem", "g_ffn",
         "w_ffn_gate", "w_ffn_up", "w_ffn_down", "g_final")


def kernel(x, mem, positions, g_mix, w_in, w_up_a, w_up_b, w_out, g_mem_q, g_mem_kv, w_q_mem, w_kv_mem, w_o_mem, g_ffn, w_ffn_gate, w_ffn_up, w_ffn_down, g_final, loss_target, m_g_mix, m_w_in, m_w_up_a, m_w_up_b, m_w_out, m_g_mem_q, m_g_mem_kv, m_w_q_mem, m_w_kv_mem, m_w_o_mem, m_g_ffn, m_w_ffn_gate, m_w_ffn_up, m_w_ffn_down, m_g_final, v_g_mix, v_w_in, v_w_up_a, v_w_up_b, v_w_out, v_g_mem_q, v_g_mem_kv, v_w_q_mem, v_w_kv_mem, v_w_o_mem, v_g_ffn, v_w_ffn_gate, v_w_ffn_up, v_w_ffn_down, v_g_final):
    given = dict(locals())
    batch, s, d = x.shape
    t = batch * s
    shard = {n: given[n].reshape(given[n].shape[-2:]) for n in WEIGHTS}
    gains = [given[n].reshape(1, d) for n in GAINS]

    pad = (-shard["w_ffn_down"].shape[0]) % LANES
    pads = {"w_ffn_gate": (0, pad), "w_ffn_up": (0, pad), "w_ffn_down": (pad, 0)}
    cast = _cast_weights([shard[n] for n in WEIGHTS], [pads.get(n, (0, 0)) for n in WEIGHTS])
    (w_in_all,) = _exchange("gather_w_in", cast[:1], True, (0,))
    loss_part, grad_x, gw_in, pieces, dgains = _local_step(
        x.reshape(t, d), mem.reshape(-1, d), positions.reshape(t, 1), loss_target.reshape(t, d), gains, w_in_all,
        cast[1:], batch)
    pieces = list(_exchange("scatter_gw_in", [gw_in], False, (0,))) + pieces

    grad, delta, new_m, new_v = {}, {}, {}, {}
    for n, p in zip(WEIGHTS, pieces):
        m2, v2 = given["m_" + n].reshape(shard[n].shape), given["v_" + n].reshape(shard[n].shape)
        outs = _adam("adam_" + n, p, shard[n], m2, v2)
        grad[n], delta[n], new_m[n], new_v[n] = [o.reshape(given[n].shape) for o in outs]

    rows = jnp.concatenate(list(dgains) + [jnp.zeros((N_DEV - len(GAINS), d), F32)], axis=0)
    g_all = _allreduce_small(rows)
    w_all = jnp.concatenate(gains + [jnp.zeros((N_DEV - len(GAINS), d), F32)], axis=0)
    m_all = jnp.concatenate([given["m_" + n].reshape(1, d) for n in GAINS] + [jnp.zeros((N_DEV - len(GAINS), d), F32)], axis=0)
    v_all = jnp.concatenate([given["v_" + n].reshape(1, d) for n in GAINS] + [jnp.ones((N_DEV - len(GAINS), d), F32)], axis=0)
    d_all, mo_all, vo_all = _adam_small(g_all, w_all, m_all, v_all)
    for i, n in enumerate(GAINS):
        grad[n] = g_all[i].reshape(given[n].shape)
        delta[n] = d_all[i].reshape(given[n].shape)
        new_m[n] = mo_all[i].reshape(given[n].shape)
        new_v[n] = vo_all[i].reshape(given[n].shape)

    loss = lax.psum(loss_part[0, 0], ("x", "y", "c"))
    return (loss, grad_x.reshape(x.shape), *[grad[n] for n in ORDER], *[delta[n] for n in ORDER],
            *[new_m[n] for n in ORDER], *[new_v[n] for n in ORDER])
```

```python
import functools
import math

import jax
import jax.numpy as jnp
from jax import lax
from jax.experimental import pallas as pl
from jax.experimental.pallas import tpu as pltpu

F32 = jnp.float32
BF16 = jnp.bfloat16

N_DEV = 8
HEAD_DIM = 64
MEM_HEAD_DIM = 128
N_HEADS_MEM = 4
BLOCK = 128
DIL_PATTERNS = ((128, 1), (512, 4), (2048, 16))
ROPE_THETA = 500000.0
ROPE_HALF = 8
RMS_EPS = 1e-6
ADAM_LR, ADAM_B1, ADAM_B2, ADAM_EPS, ADAM_WD, ADAM_STEP = 0.001, 0.9, 0.999, 1e-08, 0.01, 10
NEG = -1e30
ROW_TILE = 512
LANES = 128

ANY = pl.BlockSpec(memory_space=pl.ANY)
VMEM = pl.BlockSpec(memory_space=pltpu.VMEM)
NN = (((1,), (0,)), ((), ()))
NT = (((1,), (1,)), ((), ()))
TN = (((0,), (0,)), ((), ()))


def _params(sem):
    return pltpu.CompilerParams(dimension_semantics=sem)


def _mm(name, a, b, *, grid, a_spec, b_spec, o_shape, o_spec, dims, out_dtype, nk=1, res=None, res_spec=None):
    has_res = res is not None

    def body(*refs):
        a_ref, b_ref = refs[0], refs[1]
        r_ref = refs[2] if has_res else None
        o_ref = refs[3] if has_res else refs[2]
        p = lax.dot_general(a_ref[...], b_ref[...], dims, preferred_element_type=F32)
        if nk == 1:
            if has_res:
                p = p + r_ref[...].astype(F32)
            o_ref[...] = p.astype(out_dtype)
            return
        acc_ref = refs[-1]
        k = pl.program_id(len(grid) - 1)

        @pl.when(k == 0)
        def _():
            acc_ref[...] = p

        @pl.when(k > 0)
        def _():
            acc_ref[...] += p

        @pl.when(k == nk - 1)
        def _():
            t = acc_ref[...]
            if has_res:
                t = t + r_ref[...].astype(F32)
            o_ref[...] = t.astype(out_dtype)

    o_block = tuple(d for d in o_spec.block_shape if d is not None)
    sem = ("parallel",) * (len(grid) - 1) + (("arbitrary",) if nk > 1 else ("parallel",))
    return pl.pallas_call(
        body, name=name, grid=grid,
        in_specs=[a_spec, b_spec] + ([res_spec] if has_res else []),
        out_specs=o_spec, out_shape=jax.ShapeDtypeStruct(o_shape, out_dtype),
        scratch_shapes=[pltpu.VMEM(o_block, F32)] if nk > 1 else [],
        compiler_params=_params(sem),
    )(*([a, b] + ([res] if has_res else [])))


def _rms_fwd(name, x, g):
    t, d = x.shape
    tm = min(ROW_TILE, t)

    def body(x_ref, g_ref, o_ref):
        xf = x_ref[...]
        r = lax.rsqrt(jnp.mean(xf * xf, axis=-1, keepdims=True) + RMS_EPS)
        o_ref[...] = (xf * r * g_ref[...]).astype(BF16)

    return pl.pallas_call(
        body, name=name, grid=(t // tm,),
        in_specs=[pl.BlockSpec((tm, d), lambda i: (i, 0)), pl.BlockSpec((1, d), lambda i: (0, 0))],
        out_specs=pl.BlockSpec((tm, d), lambda i: (i, 0)), out_shape=jax.ShapeDtypeStruct((t, d), BF16),
        compiler_params=_params(("parallel",)),
    )(x, g)


def _rms_bwd(name, dn, x, g, dres, want):
    t, d = x.shape
    tm = min(ROW_TILE, t)
    has_res = dres is not None

    def body(*refs):
        dn_ref, x_ref, g_ref = refs[0], refs[1], refs[2]
        r_ref = refs[3] if has_res else None
        dx_refs, dg_ref = refs[-1 - len(want):-1], refs[-1]
        xf = x_ref[...]
        r = lax.rsqrt(jnp.mean(xf * xf, axis=-1, keepdims=True) + RMS_EPS)
        xh = xf * r
        dnf = dn_ref[...].astype(F32)
        if want:
            dxh = dnf * g_ref[...]
            dx = r * (dxh - xh * jnp.mean(dxh * xh, axis=-1, keepdims=True))
            if has_res:
                dx = dx + r_ref[...]
            for kind, dx_ref in zip(want, dx_refs):
                dx_ref[...] = dx.astype(F32 if kind == "f32" else BF16)

        @pl.when(pl.program_id(0) == 0)
        def _():
            dg_ref[...] = jnp.zeros_like(dg_ref)

        dg_ref[...] += jnp.sum(dnf * xh, axis=0, keepdims=True)

    row = pl.BlockSpec((tm, d), lambda i: (i, 0))
    vec = pl.BlockSpec((1, d), lambda i: (0, 0))
    return pl.pallas_call(
        body, name=name, grid=(t // tm,),
        in_specs=[row, row, vec] + ([row] if has_res else []),
        out_specs=[row] * len(want) + [vec],
        out_shape=[jax.ShapeDtypeStruct((t, d), F32 if kind == "f32" else BF16) for kind in want]
        + [jax.ShapeDtypeStruct((1, d), F32)],
        compiler_params=_params(("arbitrary",)),
    )(*([dn, x, g] + ([dres] if has_res else [])))


def _loss_head(h, tgt, g):
    t, d = h.shape
    tm = min(ROW_TILE, t)

    def body(h_ref, t_ref, g_ref, loss_ref, dh_ref, dhb_ref, dg_ref):
        xf = h_ref[...]
        gv = g_ref[...]
        r = lax.rsqrt(jnp.mean(xf * xf, axis=-1, keepdims=True) + RMS_EPS)
        xh = xf * r
        e = xh * gv - t_ref[...]
        dy = e * (1.0 / d)
        dxh = dy * gv
        dh = r * (dxh - xh * jnp.mean(dxh * xh, axis=-1, keepdims=True))
        dh_ref[...] = dh
        dhb_ref[...] = dh.astype(BF16)

        @pl.when(pl.program_id(0) == 0)
        def _():
            dg_ref[...] = jnp.zeros_like(dg_ref)
            loss_ref[...] = jnp.zeros_like(loss_ref)

        dg_ref[...] += jnp.sum(dy * xh, axis=0, keepdims=True)
        part = jnp.sum(jnp.sum(e * e, axis=1, keepdims=True), axis=0, keepdims=True) * (0.5 / d)
        loss_ref[...] += jnp.broadcast_to(part, loss_ref.shape)

    row = pl.BlockSpec((tm, d), lambda i: (i, 0))
    vec = pl.BlockSpec((1, d), lambda i: (0, 0))
    return pl.pallas_call(
        body, name="loss_head", grid=(t // tm,),
        in_specs=[row, row, vec],
        out_specs=[pl.BlockSpec((8, LANES), lambda i: (0, 0)), row, row, vec],
        out_shape=[jax.ShapeDtypeStruct((8, LANES), F32), jax.ShapeDtypeStruct((t, d), F32),
                   jax.ShapeDtypeStruct((t, d), BF16), jax.ShapeDtypeStruct((1, d), F32)],
        compiler_params=_params(("arbitrary",)),
    )(h, tgt, g)


def _rope_tables(pos, inv_freq, sel_lo, sel_hi):
    t = pos.shape[0]
    tm = min(ROW_TILE, t)

    def body(p_ref, f_ref, lo_ref, hi_ref, c_ref, sa_ref, sb_ref):
        ang = p_ref[...].astype(F32) * f_ref[...]
        rot = lo_ref[...] + hi_ref[...]
        cs, sn = jnp.cos(ang), jnp.sin(ang)
        c_ref[...] = cs * rot + (1.0 - rot)
        sa_ref[...] = -sn * lo_ref[...]
        sb_ref[...] = sn * hi_ref[...]

    vec = pl.BlockSpec((1, LANES), lambda i: (0, 0))
    row = pl.BlockSpec((tm, LANES), lambda i: (i, 0))
    return pl.pallas_call(
        body, name="rope_tables", grid=(t // tm,),
        in_specs=[pl.BlockSpec((tm, 1), lambda i: (i, 0)), vec, vec, vec],
        out_specs=[row, row, row], out_shape=[jax.ShapeDtypeStruct((t, LANES), F32)] * 3,
        compiler_params=_params(("parallel",)),
    )(pos, inv_freq, sel_lo, sel_hi)


def _rope_apply(name, src, col0, n_cols, cos_t, sin_a, sin_b, sign):
    t = src.shape[0]
    tm = min(ROW_TILE, t)

    def body(x_ref, c_ref, sa_ref, sb_ref, o_ref):
        cs, sa, sb = c_ref[...], sign * sa_ref[...], sign * sb_ref[...]
        for c in range(n_cols):
            cols = slice(c * LANES, (c + 1) * LANES)
            xf = x_ref[:, cols].astype(F32)
            up = pltpu.roll(xf, LANES - ROPE_HALF, 1)
            dn = pltpu.roll(xf, ROPE_HALF, 1)
            o_ref[:, cols] = (xf * cs + up * sa + dn * sb).astype(BF16)

    wide = n_cols * LANES
    tab = pl.BlockSpec((tm, LANES), lambda i: (i, 0))
    return pl.pallas_call(
        body, name=name, grid=(t // tm,),
        in_specs=[pl.BlockSpec((tm, wide), lambda i: (i, col0 // n_cols)), tab, tab, tab],
        out_specs=pl.BlockSpec((tm, wide), lambda i: (i, 0)),
        out_shape=jax.ShapeDtypeStruct((t, wide), BF16),
        compiler_params=_params(("parallel",)),
    )(src, cos_t, sin_a, sin_b)


DA_T = 256
FWD_STREAMS = 4
BWD_STREAMS = 2


def _lane_lo():
    return lax.broadcasted_iota(jnp.int32, (BLOCK, LANES), 1) < HEAD_DIM


def _dilated_bias_tiles(s):
    n = s // DA_T
    dist = (jnp.arange(n, dtype=jnp.int32)[:, None, None] * DA_T
            + jnp.arange(DA_T, dtype=jnp.int32)[None, :, None] - jnp.arange(DA_T, dtype=jnp.int32)[None, None, :])
    cnt = jnp.zeros(dist.shape, F32)
    for window, dil in DIL_PATTERNS:
        cnt = cnt + ((dist >= 0) & (dist % dil == 0) & (dist <= window)).astype(F32)
    return jnp.where(cnt > 0, jnp.log(jnp.maximum(cnt, 1.0)), NEG)


def _stack_heads(x, lo):
    zero = jnp.zeros_like(x)
    return jnp.concatenate([jnp.where(lo, x, zero), jnp.where(lo, zero, x)], axis=0)


def _da_fwd(qk, proj, v_col0, bias, batch, s, ride=None, streams=FWD_STREAMS):
    t = qk.shape[0]
    nq = s // DA_T
    n_pairs = 4
    ns = streams
    wide = ns * LANES
    scale = HEAD_DIM ** -0.5

    def body(q_ref, k_ref, v_ref, b_ref, o_ref, lse_ref, acc_ref, m_ref, l_ref):
        i = pl.program_id(2)
        lo = lax.broadcasted_iota(jnp.int32, (DA_T, LANES), 1) < HEAD_DIM
        ones = jnp.ones((DA_T, LANES), BF16)
        acc_ref[...] = jnp.zeros_like(acc_ref)
        m_ref[...] = jnp.full(m_ref.shape, NEG, F32)
        l_ref[...] = jnp.zeros_like(l_ref)
        qqs = [_stack_heads(q_ref[:, st * LANES:(st + 1) * LANES] * scale, lo) for st in range(ns)]

        def scores(st, rows, bias2):
            k = k_ref[rows, st * LANES:(st + 1) * LANES]
            return lax.dot_general(qqs[st], k, NT, preferred_element_type=F32) + bias2

        def softmax(st, sc):
            m_old = m_ref[st]
            m_new = jnp.maximum(m_old, jnp.max(sc, axis=1, keepdims=True))
            m_ref[st] = m_new
            return jnp.exp(sc - m_new).astype(BF16), jnp.exp(m_old - m_new)

        def values(st, rows, p, alpha):
            v = v_ref[rows, st * LANES:(st + 1) * LANES]
            vz = jnp.zeros_like(v)
            l_ref[st] = alpha * l_ref[st] + lax.dot_general(p, ones, NN, preferred_element_type=F32)
            pv = (lax.dot_general(p[:DA_T], jnp.where(lo, v, vz), NN, preferred_element_type=F32)
                  + lax.dot_general(p[DA_T:], jnp.where(lo, vz, v), NN, preferred_element_type=F32))
            acc_ref[st] = acc_ref[st] * jnp.where(lo, alpha[:DA_T], alpha[DA_T:]) + pv

        def trip(dlt, carry):
            rows = pl.ds(pl.multiple_of((i - dlt) * DA_T, DA_T), DA_T)
            bias_t = b_ref[dlt]
            bias2 = jnp.concatenate([bias_t, bias_t], axis=0)
            scs = [scores(st, rows, bias2) for st in range(ns)]
            pas = [softmax(st, scs[st]) for st in range(ns)]
            for st in range(ns):
                values(st, rows, *pas[st])
            return carry

        lax.fori_loop(0, i + 1, trip, 0)
        for st in range(ns):
            cols = slice(st * LANES, (st + 1) * LANES)
            l_t = l_ref[st]
            o_ref[:, cols] = (acc_ref[st] / jnp.where(lo, l_t[:DA_T], l_t[DA_T:])).astype(BF16)
            lse = m_ref[st] + jnp.log(l_t)
            lse_ref[:, cols] = jnp.where(lo, lse[:DA_T], lse[DA_T:])

    blk = pl.BlockSpec((DA_T, wide), lambda b, h, i: (b * nq + i, h))
    return _call(
        body, name="attn_a_fwd", grid=(batch, n_pairs // ns, nq),
        in_specs=[blk,
                  pl.BlockSpec((s, wide), lambda b, h, i: (b, n_pairs // ns + h)),
                  pl.BlockSpec((s, wide), lambda b, h, i: (b, v_col0 // ns + h)),
                  pl.BlockSpec((nq, DA_T, DA_T), lambda b, h, i: (0, 0, 0))],
        out_specs=[blk, blk],
        out_shape=[jax.ShapeDtypeStruct((t, n_pairs * LANES), BF16), jax.ShapeDtypeStruct((t, n_pairs * LANES), F32)],
        scratch=[pltpu.VMEM((ns, DA_T, LANES), F32), pltpu.VMEM((ns, 2 * DA_T, 1), F32),
                 pltpu.VMEM((ns, 2 * DA_T, LANES), F32)],
        sem=("parallel", "parallel", "arbitrary"), args=(qk, qk, proj, bias), ride=ride)


def _da_bwd(qk, proj, v_col0, bias, o, lse, do, batch, s, ride=None, streams=BWD_STREAMS):
    t = qk.shape[0]
    nq = s // DA_T
    n_pairs = 4
    ns = streams
    wide = ns * LANES
    scale = HEAD_DIM ** -0.5

    def body(q_ref, k_ref, v_ref, b_ref, o_ref, lse_ref, do_ref, dq_ref, dk_ref, dv_ref, dk_acc, dv_acc, dq_acc):
        i = pl.program_id(2)
        lo = lax.broadcasted_iota(jnp.int32, (DA_T, LANES), 1) < HEAD_DIM

        @pl.when(i == 0)
        def _():
            dk_acc[...] = jnp.zeros_like(dk_acc)
            dv_acc[...] = jnp.zeros_like(dv_acc)

        dq_acc[...] = jnp.zeros_like(dq_acc)
        qqs, dds, deltas, lses = [], [], [], []
        for st in range(ns):
            cols = slice(st * LANES, (st + 1) * LANES)
            do_ = do_ref[:, cols]
            qqs.append(_stack_heads(q_ref[:, cols] * scale, lo))
            dds.append(_stack_heads(do_, lo))
            prod = do_.astype(F32) * o_ref[:, cols].astype(F32)
            fz = jnp.zeros_like(prod)
            deltas.append(jnp.concatenate([jnp.sum(jnp.where(lo, prod, fz), axis=1, keepdims=True),
                                           jnp.sum(jnp.where(lo, fz, prod), axis=1, keepdims=True)], axis=0))
            lse_t = lse_ref[:, cols]
            lses.append(jnp.concatenate([lse_t[:, 0:1], lse_t[:, HEAD_DIM:HEAD_DIM + 1]], axis=0))

        def products(st, rows, bias2):
            cols = slice(st * LANES, (st + 1) * LANES)
            sc = lax.dot_general(qqs[st], k_ref[rows, cols], NT, preferred_element_type=F32) + bias2
            return sc, lax.dot_general(dds[st], v_ref[rows, cols], NT, preferred_element_type=F32)

        def weights(st, sc, dp):
            p = jnp.exp(sc - lses[st])
            return (p * (dp - deltas[st])).astype(BF16), p.astype(BF16)

        def gradients(st, rows, ds, p):
            cols = slice(st * LANES, (st + 1) * LANES)
            k = k_ref[rows, cols]
            kz = jnp.zeros_like(k)
            dq_acc[st] += (lax.dot_general(ds[:DA_T], jnp.where(lo, k, kz), NN, preferred_element_type=F32)
                           + lax.dot_general(ds[DA_T:], jnp.where(lo, kz, k), NN, preferred_element_type=F32))
            dk_acc[rows, cols] += lax.dot_general(ds, qqs[st], TN, preferred_element_type=F32)
            dv_acc[rows, cols] += lax.dot_general(p, dds[st], TN, preferred_element_type=F32)

        def trip(dlt, carry):
            rows = pl.ds(pl.multiple_of((i - dlt) * DA_T, DA_T), DA_T)
            bias_t = b_ref[dlt]
            bias2 = jnp.concatenate([bias_t, bias_t], axis=0)
            prods = [products(st, rows, bias2) for st in range(ns)]
            wts = [weights(st, *prods[st]) for st in range(ns)]
            for st in range(ns):
                gradients(st, rows, *wts[st])
            return carry

        lax.fori_loop(0, i + 1, trip, 0)
        for st in range(ns):
            dq_ref[:, st * LANES:(st + 1) * LANES] = (dq_acc[st] * scale).astype(BF16)

        @pl.when(i == nq - 1)
        def _():
            dk_ref[...] = dk_acc[...].astype(BF16)
            dv_ref[...] = dv_acc[...].astype(BF16)

    blk = pl.BlockSpec((DA_T, wide), lambda b, h, i: (b * nq + i, h))
    seq = pl.BlockSpec((s, wide), lambda b, h, i: (b, h))
    out = jax.ShapeDtypeStruct((t, n_pairs * LANES), BF16)
    return _call(
        body, name="attn_a_bwd", grid=(batch, n_pairs // ns, nq),
        in_specs=[blk,
                  pl.BlockSpec((s, wide), lambda b, h, i: (b, n_pairs // ns + h)),
                  pl.BlockSpec((s, wide), lambda b, h, i: (b, v_col0 // ns + h)),
                  pl.BlockSpec((nq, DA_T, DA_T), lambda b, h, i: (0, 0, 0)),
                  blk, blk, blk],
        out_specs=[blk, seq, seq], out_shape=[out, out, out],
        scratch=[pltpu.VMEM((s, wide), F32), pltpu.VMEM((s, wide), F32), pltpu.VMEM((ns, DA_T, LANES), F32)],
        sem=("parallel", "parallel", "arbitrary"), args=(qk, qk, proj, bias, o, lse, do), ride=ride)


SB_Q = 256


def _sb_consts(after):
    r = lax.broadcasted_iota(jnp.int32, (2 * BLOCK, 2 * BLOCK), 0) % BLOCK
    c = lax.broadcasted_iota(jnp.int32, (2 * BLOCK, 2 * BLOCK), 1)
    tri = (r > c) if after else (r < c)
    return jnp.logical_or(c >= BLOCK, tri).astype(BF16)


def _split(x):
    hi = x.astype(BF16)
    lo = (x - hi.astype(F32)).astype(BF16)
    return jnp.concatenate([hi, lo], axis=1)


def _sb_fwd(proj, q_col0, k_col0, v_col0, batch, s, ride=None, streams=FWD_STREAMS):
    t = proj.shape[0]
    nq = s // SB_Q
    n_pairs = 4
    ns = streams
    wide = ns * LANES
    scale = HEAD_DIM ** -0.5

    def body(q_ref, k_ref, v_ref, o_ref, tot_ref, acc_ref, run_ref):
        i = pl.program_id(2)
        lo_q = lax.broadcasted_iota(jnp.int32, (SB_Q, LANES), 1) < HEAD_DIM
        lo_k = _lane_lo()
        mat = _sb_consts(True)
        row = lax.broadcasted_iota(jnp.int32, (2 * SB_Q, LANES), 0) % SB_Q
        ahead = row - lax.broadcasted_iota(jnp.int32, (2 * SB_Q, LANES), 1)
        acc_ref[...] = jnp.zeros_like(acc_ref)
        run_ref[...] = jnp.zeros_like(run_ref)
        qqs = [_stack_heads(q_ref[:, st * LANES:(st + 1) * LANES] * scale, lo_q) for st in range(ns)]

        def units(todo):
            def rows(j):
                return pl.ds(pl.multiple_of(j * BLOCK, BLOCK), BLOCK)

            zs = [lax.dot_general(qqs[st], k_ref[rows(j), st * LANES:(st + 1) * LANES], NT, preferred_element_type=F32)
                  for st, j, _ in todo]
            logs = []
            for z, (_, _, off) in zip(zs, todo):
                lsig = jnp.minimum(z, 0.0) - jnp.log(1.0 + jnp.exp(-jnp.abs(z)))
                lneg = lsig - z
                if off is not None:
                    lneg = jnp.where(ahead > off, lneg, 0.0)
                logs.append((lsig, _split(lneg)))
            sums = [lax.dot_general(cat, mat, NN, preferred_element_type=F32) for _, cat in logs]
            probs = []
            for (lsig, _), sm, (st, _, off) in zip(logs, sums, todo):
                run = run_ref[st]
                a = jnp.exp(lsig + run + sm[:, :BLOCK])
                if off is not None:
                    a = jnp.where(ahead > off, a, 0.0)
                run_ref[st] = run + sm[:, BLOCK:]
                probs.append(a.astype(BF16))
            for ab, (st, j, _) in zip(probs, todo):
                v = v_ref[rows(j), st * LANES:(st + 1) * LANES]
                vz = jnp.zeros_like(v)
                acc_ref[st] += (lax.dot_general(ab[:SB_Q], jnp.where(lo_k, v, vz), NN, preferred_element_type=F32)
                                + lax.dot_general(ab[SB_Q:], jnp.where(lo_k, vz, v), NN, preferred_element_type=F32))

        units([(st, 2 * i + 1, BLOCK) for st in range(ns)] + [(st, 2 * i, 0) for st in range(ns)])

        def pair(p, carry):
            jp = i - 1 - p
            units([(st, 2 * jp + 1, None) for st in range(ns)] + [(st, 2 * jp, None) for st in range(ns)])
            return carry

        lax.fori_loop(0, i, pair, 0)
        for st in range(ns):
            cols = slice(st * LANES, (st + 1) * LANES)
            o_ref[:, cols] = acc_ref[st].astype(BF16)
            tot_ref[:, cols] = jnp.where(lo_q, run_ref[st, 0:SB_Q, :], run_ref[st, SB_Q:2 * SB_Q, :])

    def seq(col0):
        return pl.BlockSpec((s, wide), lambda b, h, i: (b, col0 // ns + h))

    blk = pl.BlockSpec((SB_Q, wide), lambda b, h, i: (b * nq + i, h))
    return _call(
        body, name="attn_b_fwd", grid=(batch, n_pairs // ns, nq),
        in_specs=[pl.BlockSpec((SB_Q, wide), lambda b, h, i: (b * nq + i, q_col0 // ns + h)), seq(k_col0), seq(v_col0)],
        out_specs=[blk, blk],
        out_shape=[jax.ShapeDtypeStruct((t, n_pairs * LANES), BF16), jax.ShapeDtypeStruct((t, n_pairs * LANES), F32)],
        scratch=[pltpu.VMEM((ns, SB_Q, LANES), F32), pltpu.VMEM((ns, 2 * SB_Q, LANES), F32)],
        sem=("parallel", "parallel", "arbitrary"), args=(proj, proj, proj), ride=ride)


def _sb_bwd(proj, q_col0, k_col0, v_col0, tot, do, batch, s, ride=None, streams=BWD_STREAMS):
    t = proj.shape[0]
    nq = s // SB_Q
    n_pairs = 4
    ns = streams
    wide = ns * LANES
    scale = HEAD_DIM ** -0.5

    def body(q_ref, k_ref, v_ref, tot_ref, do_ref, dq_ref, dk_ref, dv_ref, dk_acc, dv_acc, dq_acc, seen_ref, gsum_ref):
        i = pl.program_id(2)
        lo_q = lax.broadcasted_iota(jnp.int32, (SB_Q, LANES), 1) < HEAD_DIM
        lo_k = _lane_lo()

        @pl.when(i == 0)
        def _():
            dk_acc[...] = jnp.zeros_like(dk_acc)
            dv_acc[...] = jnp.zeros_like(dv_acc)

        mat_after = _sb_consts(True)
        mat_before = _sb_consts(False)
        row = lax.broadcasted_iota(jnp.int32, (2 * SB_Q, LANES), 0) % SB_Q
        ahead = row - lax.broadcasted_iota(jnp.int32, (2 * SB_Q, LANES), 1)
        dq_acc[...] = jnp.zeros_like(dq_acc)
        seen_ref[...] = jnp.zeros_like(seen_ref)
        gsum_ref[...] = jnp.zeros_like(gsum_ref)
        qqs, dds, totals = [], [], []
        for st in range(ns):
            cols = slice(st * LANES, (st + 1) * LANES)
            qqs.append(_stack_heads(q_ref[:, cols] * scale, lo_q))
            dds.append(_stack_heads(do_ref[:, cols], lo_q))
            tot_t = tot_ref[:, cols]
            totals.append(jnp.concatenate([jnp.broadcast_to(tot_t[:, 0:1], (SB_Q, LANES)),
                                           jnp.broadcast_to(tot_t[:, HEAD_DIM:HEAD_DIM + 1], (SB_Q, LANES))], axis=0))

        def units(todo):
            def rows(j):
                return pl.ds(pl.multiple_of(j * BLOCK, BLOCK), BLOCK)

            def cols(st):
                return slice(st * LANES, (st + 1) * LANES)

            prods = [(lax.dot_general(qqs[st], k_ref[rows(j), cols(st)], NT, preferred_element_type=F32),
                      lax.dot_general(dds[st], v_ref[rows(j), cols(st)], NT, preferred_element_type=F32))
                     for st, j, _ in todo]
            logs = []
            for (z, _), (_, _, off) in zip(prods, todo):
                lsig = jnp.minimum(z, 0.0) - jnp.log(1.0 + jnp.exp(-jnp.abs(z)))
                lneg = lsig - z
                if off is not None:
                    lneg = jnp.where(ahead > off, lneg, 0.0)
                logs.append((lsig, _split(lneg)))
            sums = [lax.dot_general(cat, mat_after, NN, preferred_element_type=F32) for _, cat in logs]
            gates = []
            for (lsig, _), sm, (_, da), (st, _, off) in zip(logs, sums, prods, todo):
                seen = seen_ref[st]
                a = jnp.exp(lsig + (totals[st] - seen - sm[:, BLOCK:]) + sm[:, :BLOCK])
                if off is not None:
                    a = jnp.where(ahead > off, a, 0.0)
                seen_ref[st] = seen + sm[:, BLOCK:]
                g = a * da
                gates.append((a.astype(BF16), g, _split(g)))
            gsums = [lax.dot_general(cat, mat_before, NN, preferred_element_type=F32) for _, _, cat in gates]
            outs = []
            for (lsig, _), (ab, g, _), gs, (st, _, off) in zip(logs, gates, gsums, todo):
                gsum = gsum_ref[st]
                dz = g - jnp.exp(lsig) * (g + gsum + gs[:, :BLOCK])
                if off is not None:
                    dz = jnp.where(ahead > off, dz, 0.0)
                gsum_ref[st] = gsum + gs[:, BLOCK:]
                outs.append((dz.astype(BF16), ab))
            for (dzb, ab), (st, j, _) in zip(outs, todo):
                k = k_ref[rows(j), cols(st)]
                kz = jnp.zeros_like(k)
                dq_acc[st] += (lax.dot_general(dzb[:SB_Q], jnp.where(lo_k, k, kz), NN, preferred_element_type=F32)
                               + lax.dot_general(dzb[SB_Q:], jnp.where(lo_k, kz, k), NN, preferred_element_type=F32))
                dk_acc[rows(j), cols(st)] += lax.dot_general(dzb, qqs[st], TN, preferred_element_type=F32)
                dv_acc[rows(j), cols(st)] += lax.dot_general(ab, dds[st], TN, preferred_element_type=F32)

        def pair(p, carry):
            units([(st, 2 * p, None) for st in range(ns)] + [(st, 2 * p + 1, None) for st in range(ns)])
            return carry

        lax.fori_loop(0, i, pair, 0)
        units([(st, 2 * i, 0) for st in range(ns)] + [(st, 2 * i + 1, BLOCK) for st in range(ns)])
        for st in range(ns):
            dq_ref[:, st * LANES:(st + 1) * LANES] = (dq_acc[st] * scale).astype(BF16)

        @pl.when(i == nq - 1)
        def _():
            dk_ref[...] = dk_acc[...].astype(BF16)
            dv_ref[...] = dv_acc[...].astype(BF16)

    def seq_in(col0):
        return pl.BlockSpec((s, wide), lambda b, h, i: (b, col0 // ns + h))

    blk = pl.BlockSpec((SB_Q, wide), lambda b, h, i: (b * nq + i, h))
    seq = pl.BlockSpec((s, wide), lambda b, h, i: (b, h))
    out = jax.ShapeDtypeStruct((t, n_pairs * LANES), BF16)
    return _call(
        body, name="attn_b_bwd", grid=(batch, n_pairs // ns, nq),
        in_specs=[pl.BlockSpec((SB_Q, wide), lambda b, h, i: (b * nq + i, q_col0 // ns + h)), seq_in(k_col0),
                  seq_in(v_col0), blk, blk],
        out_specs=[blk, seq, seq], out_shape=[out, out, out],
        scratch=[pltpu.VMEM((s, wide), F32), pltpu.VMEM((s, wide), F32), pltpu.VMEM((ns, SB_Q, LANES), F32),
                 pltpu.VMEM((ns, 2 * SB_Q, LANES), F32), pltpu.VMEM((ns, 2 * SB_Q, LANES), F32)],
        sem=("parallel", "parallel", "arbitrary"), args=(proj, proj, proj, tot, do), ride=ride)


MEM_Q_TILE = 256


def _mem_fwd(q, kv, batch, s, n_mem):
    t, width = q.shape
    tq = min(MEM_Q_TILE, s)
    nq = s // tq
    scale = MEM_HEAD_DIM ** -0.5

    def body(q_ref, kv_ref, o_ref):
        for h in range(N_HEADS_MEM):
            cols = slice(h * MEM_HEAD_DIM, (h + 1) * MEM_HEAD_DIM)
            k = kv_ref[:, cols]
            v = kv_ref[:, width + h * MEM_HEAD_DIM: width + (h + 1) * MEM_HEAD_DIM]
            sc = lax.dot_general(q_ref[:, cols], k, NT, preferred_element_type=F32) * scale
            p = jnp.exp(sc - jnp.max(sc, axis=1, keepdims=True))
            p = p / jnp.sum(p, axis=1, keepdims=True)
            o_ref[:, cols] = lax.dot_general(p.astype(BF16), v, NN, preferred_element_type=F32).astype(BF16)

    return pl.pallas_call(
        body, name="mem_attn_fwd", grid=(batch, nq),
        in_specs=[pl.BlockSpec((tq, width), lambda b, i: (b * nq + i, 0)),
                  pl.BlockSpec((n_mem, 2 * width), lambda b, i: (b, 0))],
        out_specs=pl.BlockSpec((tq, width), lambda b, i: (b * nq + i, 0)),
        out_shape=jax.ShapeDtypeStruct((t, width), BF16),
        compiler_params=_params(("parallel", "parallel")),
    )(q, kv)


def _mem_bwd(q, kv, do, batch, s, n_mem):
    t, width = q.shape
    tq = min(MEM_Q_TILE, s)
    nq = s // tq
    scale = MEM_HEAD_DIM ** -0.5

    def body(q_ref, kv_ref, do_ref, dq_ref, dkv_ref, acc):
        i = pl.program_id(1)

        @pl.when(i == 0)
        def _():
            acc[...] = jnp.zeros_like(acc)

        for h in range(N_HEADS_MEM):
            cols = slice(h * MEM_HEAD_DIM, (h + 1) * MEM_HEAD_DIM)
            vcols = slice(width + h * MEM_HEAD_DIM, width + (h + 1) * MEM_HEAD_DIM)
            qh, k, v, doh = q_ref[:, cols], kv_ref[:, cols], kv_ref[:, vcols], do_ref[:, cols]
            sc = lax.dot_general(qh, k, NT, preferred_element_type=F32) * scale
            p = jnp.exp(sc - jnp.max(sc, axis=1, keepdims=True))
            p = p / jnp.sum(p, axis=1, keepdims=True)
            dp = lax.dot_general(doh, v, NT, preferred_element_type=F32)
            ds = (p * (dp - jnp.sum(p * dp, axis=1, keepdims=True)) * scale).astype(BF16)
            dq_ref[:, cols] = lax.dot_general(ds, k, NN, preferred_element_type=F32).astype(BF16)
            acc[:, cols] += lax.dot_general(ds, qh, TN, preferred_element_type=F32)
            acc[:, vcols] += lax.dot_general(p.astype(BF16), doh, TN, preferred_element_type=F32)

        @pl.when(i == nq - 1)
        def _():
            dkv_ref[...] = acc[...].astype(BF16)

    row = pl.BlockSpec((tq, width), lambda b, i: (b * nq + i, 0))
    kvs = pl.BlockSpec((n_mem, 2 * width), lambda b, i: (b, 0))
    return pl.pallas_call(
        body, name="mem_attn_bwd", grid=(batch, nq),
        in_specs=[row, kvs, row], out_specs=[row, kvs],
        out_shape=[jax.ShapeDtypeStruct((t, width), BF16), jax.ShapeDtypeStruct((batch * n_mem, 2 * width), BF16)],
        scratch_shapes=[pltpu.VMEM((n_mem, 2 * width), F32)],
        compiler_params=_params(("parallel", "arbitrary")),
    )(q, kv, do)


def _mixer_fwd(o_a, o_b, w_a, w_b, proj, gate_col0):
    t, width = o_a.shape
    d = w_a.shape[1]
    tm = min(ROW_TILE, t)
    gb0 = gate_col0 * LANES // d

    def body(oa_ref, ob_ref, wa_ref, wb_ref, ga_ref, gb_ref, ua_ref, ub_ref, mix_ref):
        ua = lax.dot_general(oa_ref[...], wa_ref[...], NN, preferred_element_type=F32)
        ub = lax.dot_general(ob_ref[...], wb_ref[...], NN, preferred_element_type=F32)
        ua_ref[...] = ua.astype(BF16)
        ub_ref[...] = ub.astype(BF16)
        mix_ref[...] = (jax.nn.sigmoid(ga_ref[...].astype(F32)) * ua
                        + jax.nn.sigmoid(gb_ref[...].astype(F32)) * ub).astype(BF16)

    row = pl.BlockSpec((tm, width), lambda i: (i, 0))
    wsp = pl.BlockSpec((width, d), lambda i: (0, 0))
    out = pl.BlockSpec((tm, d), lambda i: (i, 0))
    osh = jax.ShapeDtypeStruct((t, d), BF16)
    return pl.pallas_call(
        body, name="mixer_fwd", grid=(t // tm,),
        in_specs=[row, row, wsp, wsp,
                  pl.BlockSpec((tm, d), lambda i: (i, gb0)), pl.BlockSpec((tm, d), lambda i: (i, gb0 + 1))],
        out_specs=[out, out, out], out_shape=[osh, osh, osh],
        compiler_params=_params(("parallel",)),
    )(o_a, o_b, w_a, w_b, proj, proj)


def _mixer_bwd(dmix, ua, ub, proj, gate_col0):
    t, d = dmix.shape
    tm = min(ROW_TILE, t)
    nc = d // LANES

    def body(dm_ref, ua_ref, ub_ref, ga_ref, gb_ref, dua_ref, dub_ref, dg_ref):
        dm = dm_ref[...].astype(F32)
        sa = jax.nn.sigmoid(ga_ref[...].astype(F32))
        sb = jax.nn.sigmoid(gb_ref[...].astype(F32))
        dua_ref[...] = (dm * sa).astype(BF16)
        dub_ref[...] = (dm * sb).astype(BF16)
        dg_ref[:, 0:d] = (dm * ua_ref[...].astype(F32) * sa * (1.0 - sa)).astype(BF16)
        dg_ref[:, d:2 * d] = (dm * ub_ref[...].astype(F32) * sb * (1.0 - sb)).astype(BF16)

    row = pl.BlockSpec((tm, d), lambda i: (i, 0))
    return pl.pallas_call(
        body, name="mixer_bwd", grid=(t // tm,),
        in_specs=[row, row, row,
                  pl.BlockSpec((tm, d), lambda i: (i, gate_col0 // nc)),
                  pl.BlockSpec((tm, d), lambda i: (i, gate_col0 // nc + 1))],
        out_specs=[row, row, pl.BlockSpec((tm, 2 * d), lambda i: (i, 0))],
        out_shape=[jax.ShapeDtypeStruct((t, d), BF16), jax.ShapeDtypeStruct((t, d), BF16),
                   jax.ShapeDtypeStruct((t, 2 * d), BF16)],
        compiler_params=_params(("parallel",)),
    )(dmix, ua, ub, proj, proj)


FFN_COLS = 1024


def _ffn_up(n, w_gate, w_up):
    t, d = n.shape
    hidden = w_gate.shape[1]
    tm = min(ROW_TILE, t)
    tn = min(FFN_COLS, hidden)

    def body(n_ref, wg_ref, wu_ref, hg_ref, hu_ref, act_ref):
        hg = lax.dot_general(n_ref[...], wg_ref[...], NN, preferred_element_type=F32)
        hu = lax.dot_general(n_ref[...], wu_ref[...], NN, preferred_element_type=F32)
        hg_ref[...] = hg.astype(BF16)
        hu_ref[...] = hu.astype(BF16)
        act_ref[...] = (hg * jax.nn.sigmoid(hg) * hu).astype(BF16)

    wsp = pl.BlockSpec((d, tn), lambda j, i: (0, j))
    out = pl.BlockSpec((tm, tn), lambda j, i: (i, j))
    osh = jax.ShapeDtypeStruct((t, hidden), BF16)
    return pl.pallas_call(
        body, name="ffn_up", grid=(hidden // tn, t // tm),
        in_specs=[pl.BlockSpec((tm, d), lambda j, i: (i, 0)), wsp, wsp],
        out_specs=[out, out, out], out_shape=[osh, osh, osh],
        compiler_params=_params(("parallel", "parallel")),
    )(n, w_gate, w_up)


def _ffn_bwd_act(dh, w_down, hg, hu):
    t, d = dh.shape
    hidden = w_down.shape[0]
    tm = min(ROW_TILE, t)
    tn = min(FFN_COLS, hidden)

    def body(dh_ref, wd_ref, hg_ref, hu_ref, dhg_ref, dhu_ref):
        dact = lax.dot_general(dh_ref[...], wd_ref[...], NT, preferred_element_type=F32)
        hg = hg_ref[...].astype(F32)
        sg = jax.nn.sigmoid(hg)
        dhu_ref[...] = (dact * hg * sg).astype(BF16)
        dhg_ref[...] = (dact * hu_ref[...].astype(F32) * sg * (1.0 + hg * (1.0 - sg))).astype(BF16)

    hid = pl.BlockSpec((tm, tn), lambda j, i: (i, j))
    osh = jax.ShapeDtypeStruct((t, hidden), BF16)
    return pl.pallas_call(
        body, name="ffn_bwd_act", grid=(hidden // tn, t // tm),
        in_specs=[pl.BlockSpec((tm, d), lambda j, i: (i, 0)), pl.BlockSpec((tn, d), lambda j, i: (j, 0)), hid, hid],
        out_specs=[hid, hid], out_shape=[osh, osh],
        compiler_params=_params(("parallel", "parallel")),
    )(dh, w_down, hg, hu)


MM_ROWS = 1024


def _mm_cols(name, a, w, out_dtype=BF16):
    t, k = a.shape
    n_sh, _, cs = w.shape
    tm = min(MM_ROWS, t)
    return _mm(name, a, w, grid=(n_sh, t // tm),
               a_spec=pl.BlockSpec((tm, k), lambda j, i: (i, 0)), b_spec=pl.BlockSpec((None, k, cs), lambda j, i: (j, 0, 0)),
               o_shape=(t, n_sh * cs), o_spec=pl.BlockSpec((tm, cs), lambda j, i: (i, j)), dims=NN, out_dtype=out_dtype)


def _mm_cols_t(name, a, w, out_dtype, res=None):
    t = a.shape[0]
    n_sh, k, cs = w.shape
    tm = min(MM_ROWS, t)
    o_spec = pl.BlockSpec((tm, k), lambda i, j: (i, 0))
    return _mm(name, a, w, grid=(t // tm, n_sh),
               a_spec=pl.BlockSpec((tm, cs), lambda i, j: (i, j)), b_spec=pl.BlockSpec((None, k, cs), lambda i, j: (j, 0, 0)),
               o_shape=(t, k), o_spec=o_spec, dims=NT, out_dtype=out_dtype, nk=n_sh, res=res,
               res_spec=o_spec if res is not None else None)


def _wgrad_cols(name, a, g, n_sh):
    t, k = a.shape
    cs = g.shape[1] // n_sh
    tm = min(MM_ROWS, t)
    return _mm(name, a, g, grid=(n_sh, t // tm),
               a_spec=pl.BlockSpec((tm, k), lambda j, r: (r, 0)), b_spec=pl.BlockSpec((tm, cs), lambda j, r: (r, j)),
               o_shape=(n_sh, k, cs), o_spec=pl.BlockSpec((None, k, cs), lambda j, r: (j, 0, 0)), dims=TN,
               out_dtype=BF16, nk=t // tm)


def _mm_w(name, a, w, out_dtype, dims=NN, res=None, tm=MM_ROWS, tn=1024):
    t, k = a.shape
    n = w.shape[1] if dims == NN else w.shape[0]
    tm, tn = min(tm, t), min(tn, n)
    o_spec = pl.BlockSpec((tm, tn), lambda j, i: (i, j))
    b_spec = pl.BlockSpec((k, tn), lambda j, i: (0, j)) if dims == NN else pl.BlockSpec((tn, k), lambda j, i: (j, 0))
    return _mm(name, a, w, grid=(n // tn, t // tm), a_spec=pl.BlockSpec((tm, k), lambda j, i: (i, 0)), b_spec=b_spec,
               o_shape=(t, n), o_spec=o_spec, dims=dims, out_dtype=out_dtype, res=res,
               res_spec=o_spec if res is not None else None)


def _wgrad(name, a, g, tk=1024, tn=1024):
    t, k = a.shape
    n = g.shape[1]
    tm, tk, tn = min(MM_ROWS, t), min(tk, k), min(tn, n)
    return _mm(name, a, g, grid=(k // tk, n // tn, t // tm),
               a_spec=pl.BlockSpec((tm, tk), lambda p, q, r: (r, p)), b_spec=pl.BlockSpec((tm, tn), lambda p, q, r: (r, q)),
               o_shape=(k, n), o_spec=pl.BlockSpec((tk, tn), lambda p, q, r: (p, q)), dims=TN, out_dtype=BF16, nk=t // tm)


def _peers():
    x, y, c = lax.axis_index("x"), lax.axis_index("y"), lax.axis_index("c")
    me = 4 * x + 2 * y + c
    out = []
    for k in range(1, N_DEV):
        kx, ky, kc = (k >> 2) & 1, (k >> 1) & 1, k & 1
        px = 1 - x if kx else x
        py = 1 - y if ky else y
        pc = 1 - c if kc else c
        out.append(((px, py, pc), 4 * px + 2 * py + pc))
    return me, out


def _cast_weights(ws, pads):
    def body(*refs):
        n = len(refs) // 2
        for i_ref, o_ref, (pr, pc) in zip(refs[:n], refs[n:], pads):
            r, c = i_ref.shape
            o_ref[0:r, 0:c] = i_ref[...].astype(BF16)
            if pr:
                o_ref[r:r + pr, :] = jnp.zeros((pr, c), BF16)
            if pc:
                o_ref[:, c:c + pc] = jnp.zeros((r, pc), BF16)

    return pl.pallas_call(
        body, name="cast_weights", in_specs=[VMEM] * len(ws), out_specs=[VMEM] * len(ws),
        out_shape=[jax.ShapeDtypeStruct((w.shape[0] + pr, w.shape[1] + pc), BF16) for w, (pr, pc) in zip(ws, pads)],
    )(*ws)


def _exchange_copies(ins, outs, sems, gather, cols, landed):
    send_sems, recv_sems, loc_sems = sems
    n_peer = N_DEV - 1
    me, peers = _peers()

    def win(ref, j, c):
        return ref.at[:, pl.ds(pl.multiple_of(j * c, LANES), c)]

    def src(w, j):
        if gather:
            return ins[w]
        return win(ins[w], j, cols[w]) if cols[w] else ins[w].at[j]

    def dst(w, j):
        return win(outs[w], j, cols[w]) if gather and cols[w] else outs[w].at[j]

    local = [pltpu.make_async_copy(src(w, me), dst(w, me), loc_sems.at[w]) for w in range(len(ins))]
    remote = [pltpu.make_async_remote_copy(
        src_ref=src(w, idx), dst_ref=dst(w, idx if landed else me),
        send_sem=send_sems.at[w * n_peer + k], recv_sem=recv_sems.at[w * n_peer + k],
        device_id=dev, device_id_type=pl.DeviceIdType.MESH)
        for k, (dev, idx) in enumerate(peers) for w in range(len(ins))]
    return local, remote


def _exchange_start(ins, outs, sems, gather, cols):
    local, remote = _exchange_copies(ins, outs, sems, gather, cols, False)
    for cp in local + remote:
        cp.start()


def _exchange_wait(ins, outs, sems, gather, cols):
    local, remote = _exchange_copies(ins, outs, sems, gather, cols, True)
    for cp in local:
        cp.wait()
    for cp in remote:
        cp.wait_send()
        cp.wait_recv()


def _exchange_shapes(arrs, gather, cols):
    n = len(arrs)
    out_shape = []
    for a, c in zip(arrs, cols):
        if gather:
            shape = (a.shape[0], N_DEV * c) if c else (N_DEV,) + a.shape
        else:
            shape = (N_DEV, a.shape[0], c) if c else a.shape
        out_shape.append(jax.ShapeDtypeStruct(shape, a.dtype))
    sems = [pltpu.SemaphoreType.DMA((n * (N_DEV - 1),)), pltpu.SemaphoreType.DMA((n * (N_DEV - 1),)),
            pltpu.SemaphoreType.DMA((n,))]
    return out_shape, sems


def _exchange(name, arrs, gather, cols):
    n = len(arrs)

    def body(*refs):
        ins, outs, sems = refs[:n], refs[n:2 * n], refs[2 * n:]
        _exchange_start(ins, outs, sems, gather, cols)
        _exchange_wait(ins, outs, sems, gather, cols)

    out_shape, sems = _exchange_shapes(arrs, gather, cols)
    return pl.pallas_call(body, name=name, in_specs=[ANY] * n, out_specs=[ANY] * n, out_shape=out_shape,
                          scratch_shapes=sems)(*arrs)


def _call(body, *, name, grid, in_specs, out_specs, out_shape, scratch, sem, args, ride=None):
    if ride is None:
        outs = pl.pallas_call(body, name=name, grid=grid, in_specs=in_specs, out_specs=out_specs, out_shape=out_shape,
                              scratch_shapes=scratch, compiler_params=_params(sem))(*args)
        return outs, None
    arrs, gather, cols = ride
    n, n_in, n_out, n_scr = len(arrs), len(in_specs), len(out_specs), len(scratch)
    x_shape, x_sems = _exchange_shapes(arrs, gather, cols)

    def riding(*refs):
        ins, x_ins = refs[:n_in], refs[n_in:n_in + n]
        outs = refs[n_in + n:n_in + n + n_out]
        x_outs = refs[n_in + n + n_out:n_in + 2 * n + n_out]
        scr = refs[n_in + 2 * n + n_out:n_in + 2 * n + n_out + n_scr]
        sems = refs[n_in + 2 * n + n_out + n_scr:]
        first = functools.reduce(jnp.logical_and, [pl.program_id(a) == 0 for a in range(len(grid))])
        last = functools.reduce(jnp.logical_and, [pl.program_id(a) == g - 1 for a, g in enumerate(grid)])

        @pl.when(first)
        def _():
            _exchange_start(x_ins, x_outs, sems, gather, cols)

        body(*ins, *outs, *scr)

        @pl.when(last)
        def _():
            _exchange_wait(x_ins, x_outs, sems, gather, cols)

    res = pl.pallas_call(
        riding, name=name, grid=grid, in_specs=list(in_specs) + [ANY] * n, out_specs=list(out_specs) + [ANY] * n,
        out_shape=list(out_shape) + x_shape, scratch_shapes=list(scratch) + x_sems,
        compiler_params=_params(("arbitrary",) * len(grid)))(*args, *arrs)
    return res[:n_out], res[n_out:]


def _allreduce_small(v):
    def body(v_ref, o_ref, all_ref, send_sems, recv_sems):
        me, peers = _peers()
        all_ref[me] = v_ref[...]
        for k, (dev, idx) in enumerate(peers):
            pltpu.make_async_remote_copy(src_ref=v_ref, dst_ref=all_ref.at[me], send_sem=send_sems.at[k],
                                         recv_sem=recv_sems.at[k], device_id=dev,
                                         device_id_type=pl.DeviceIdType.MESH).start()
        for k, (dev, idx) in enumerate(peers):
            cp = pltpu.make_async_remote_copy(src_ref=v_ref, dst_ref=all_ref.at[idx], send_sem=send_sems.at[k],
                                              recv_sem=recv_sems.at[k], device_id=dev,
                                              device_id_type=pl.DeviceIdType.MESH)
            cp.wait_send()
            cp.wait_recv()
        tot = all_ref[0]
        for dvc in range(1, N_DEV):
            tot = tot + all_ref[dvc]
        o_ref[...] = tot

    return pl.pallas_call(
        body, name="allreduce_small", in_specs=[VMEM], out_specs=VMEM,
        out_shape=jax.ShapeDtypeStruct(v.shape, F32),
        scratch_shapes=[pltpu.VMEM((N_DEV,) + v.shape, F32), pltpu.SemaphoreType.DMA((N_DEV - 1,)),
                        pltpu.SemaphoreType.DMA((N_DEV - 1,))],
    )(v)


def _adam_math(g, w, m, v):
    m_new = ADAM_B1 * m + (1.0 - ADAM_B1) * g
    v_new = ADAM_B2 * v + (1.0 - ADAM_B2) * (g * g)
    m_hat = m_new / (1.0 - ADAM_B1 ** ADAM_STEP)
    v_hat = v_new / (1.0 - ADAM_B2 ** ADAM_STEP)
    delta = -ADAM_LR * (m_hat / (jnp.sqrt(v_hat) + ADAM_EPS) + ADAM_WD * w)
    return delta, m_new, v_new


def _adam(name, pieces, w, m, v):
    r, c = w.shape
    cp = pieces.shape[2]
    tr = r
    for cand in (256, 176, 128, 64):
        if r % cand == 0 and r > cand:
            tr = cand
            break

    def body(p_ref, w_ref, m_ref, v_ref, g_ref, d_ref, mo_ref, vo_ref):
        g = p_ref[0, :, 0:c].astype(F32)
        for dvc in range(1, N_DEV):
            g = g + p_ref[dvc, :, 0:c].astype(F32)
        delta, m_new, v_new = _adam_math(g, w_ref[...], m_ref[...], v_ref[...])
        g_ref[...] = g
        d_ref[...] = delta
        mo_ref[...] = m_new
        vo_ref[...] = v_new

    blk = pl.BlockSpec((tr, c), lambda i: (i, 0))
    osh = jax.ShapeDtypeStruct((r, c), F32)
    return pl.pallas_call(
        body, name=name, grid=(r // tr,),
        in_specs=[pl.BlockSpec((N_DEV, tr, cp), lambda i: (0, i, 0)), blk, blk, blk],
        out_specs=[blk, blk, blk, blk], out_shape=[osh, osh, osh, osh],
        compiler_params=_params(("parallel",)),
    )(pieces, w, m, v)


def _adam_small(g, w, m, v):
    def body(g_ref, w_ref, m_ref, v_ref, d_ref, mo_ref, vo_ref):
        delta, m_new, v_new = _adam_math(g_ref[...], w_ref[...], m_ref[...], v_ref[...])
        d_ref[...] = delta
        mo_ref[...] = m_new
        vo_ref[...] = v_new

    osh = jax.ShapeDtypeStruct(g.shape, F32)
    return pl.pallas_call(body, name="adam_small", in_specs=[VMEM] * 4, out_specs=[VMEM] * 3,
                          out_shape=[osh, osh, osh])(g, w, m, v)


def _local_step(x, mem, pos, tgt, gains, w_in, shards, batch):
    g_mix, g_mem_q, g_mem_kv, g_ffn, g_final = gains
    t, d = x.shape
    s = t // batch
    n_mem = mem.shape[0] // batch
    n_sh = N_DEV
    width = shards[0].shape[0]
    nb = width // LANES

    lane = jnp.arange(LANES, dtype=jnp.int32) % HEAD_DIM
    sel_lo = (lane < ROPE_HALF).astype(F32)[None, :]
    sel_hi = ((lane >= ROPE_HALF) & (lane < 2 * ROPE_HALF)).astype(F32)[None, :]
    freqs = ROPE_THETA ** (-jnp.arange(ROPE_HALF, dtype=F32) / ROPE_HALF)
    inv_freq = jnp.where(lane < 2 * ROPE_HALF, freqs[lane % ROPE_HALF], 0.0)[None, :]
    cos_t, sin_a, sin_b = _rope_tables(pos, inv_freq, sel_lo, sel_hi)
    bias = _dilated_bias_tiles(s)

    n1 = _rms_fwd("norm_mix", x, g_mix)
    proj = _mm_cols("proj_in", n1, w_in)
    qk_a = _rope_apply("rope_fwd", proj, 0, 2 * nb, cos_t, sin_a, sin_b, 1.0)
    cs_up, cs_ffn = shards[0].shape[1], shards[6].shape[1]
    (o_a, lse_a), (w_up_a, w_up_b, w_out, w_q, w_kv, w_o) = _da_fwd(
        qk_a, proj, 2 * nb, bias, batch, s, ride=(shards[:6], True, (cs_up, cs_up, 0, 0, 0, cs_up)))
    (o_b, tot_b), (w_fg, w_fu, w_fd) = _sb_fwd(proj, 3 * nb, 4 * nb, 5 * nb, batch, s,
                                               ride=(shards[6:], True, (cs_ffn, cs_ffn, 0)))
    w_out = w_out.reshape(d, d)
    w_q = w_q.reshape(d, -1)
    w_kv = w_kv.reshape(d, -1)
    w_fd = w_fd.reshape(-1, d)
    ua, ub, mixed = _mixer_fwd(o_a, o_b, w_up_a, w_up_b, proj, 6 * nb)
    h1 = _mm_w("mix_out", mixed, w_out, F32, res=x)
    n2 = _rms_fwd("norm_mem_q", h1, g_mem_q)
    mem_n = _rms_fwd("norm_mem_kv", mem, g_mem_kv)
    q_m = _mm_w("mem_q", n2, w_q, BF16)
    kv_m = _mm_w("mem_kv", mem_n, w_kv, BF16)
    o_m = _mem_fwd(q_m, kv_m, batch, s, n_mem)
    h2 = _mm_w("mem_out", o_m, w_o, F32, res=h1)
    n3 = _rms_fwd("norm_ffn", h2, g_ffn)
    hg, hu, act = _ffn_up(n3, w_fg, w_fu)
    h3 = _mm_w("ffn_down", act, w_fd, F32, res=h2, tm=ROW_TILE)
    loss_part, dh3, dh3_b, dg_final = _loss_head(h3, tgt, g_final.reshape(1, d))

    dhg, dhu = _ffn_bwd_act(dh3_b, w_fd, hg, hu)
    gw_fd = _wgrad("gw_ffn_down", act, dh3_b)
    gw_fg = _wgrad("gw_ffn_gate", n3, dhg)
    gw_fu = _wgrad("gw_ffn_up", n3, dhu)
    dn3 = _mm_w("dn_ffn_gate", dhg, w_fg, F32, dims=NT, tm=ROW_TILE)
    dn3 = _mm_w("dn_ffn_up", dhu, w_fu, F32, dims=NT, res=dn3, tm=ROW_TILE)
    dh2, dh2_b, dg_ffn = _rms_bwd("norm_ffn_bwd", dn3, h2, g_ffn, dh3, ("f32", "bf16"))

    do_m = _mm_w("mem_out_bwd", dh2_b, w_o, BF16, dims=NT)
    gw_o = _wgrad("gw_mem_o", o_m, dh2_b)
    dq_m, dkv_m = _mem_bwd(q_m, kv_m, do_m, batch, s, n_mem)
    gw_q = _wgrad("gw_mem_q", n2, dq_m)
    gw_kv = _wgrad("gw_mem_kv", mem_n, dkv_m)
    dn2 = _mm_w("mem_q_bwd", dq_m, w_q, F32, dims=NT)
    dmem_n = _mm_w("mem_kv_bwd", dkv_m, w_kv, F32, dims=NT)
    (dg_mem_kv,) = _rms_bwd("norm_mem_kv_bwd", dmem_n, mem, g_mem_kv, None, ())
    dh1, dh1_b, dg_mem_q = _rms_bwd("norm_mem_q_bwd", dn2, h1, g_mem_q, dh2, ("f32", "bf16"))

    dmix = _mm_w("mix_out_bwd", dh1_b, w_out, BF16, dims=NT)
    gw_out = _wgrad("gw_out", mixed, dh1_b)
    dua, dub, dgates = _mixer_bwd(dmix, ua, ub, proj, 6 * nb)
    do_a = _mm_w("up_a_bwd", dua, w_up_a, BF16, dims=NT)
    do_b = _mm_w("up_b_bwd", dub, w_up_b, BF16, dims=NT)
    gw_ua = _wgrad("gw_up_a", o_a, dua)
    gw_ub = _wgrad("gw_up_b", o_b, dub)
    (dq_ar, dk_ar, dv_a), p_ffn = _da_bwd(
        qk_a, proj, 2 * nb, bias, o_a, lse_a, do_a, batch, s,
        ride=([gw_fg, gw_fu, gw_fd.reshape(n_sh, -1, d)], False, (cs_ffn, cs_ffn, 0)))
    dqk_a = _rope_apply("rope_bwd", jnp.concatenate([dq_ar, dk_ar], axis=1), 0, 2 * nb, cos_t, sin_a, sin_b, -1.0)
    mid = [gw_ua, gw_ub, gw_out.reshape(n_sh, -1, d), gw_q.reshape(n_sh, -1, gw_q.shape[1]),
           gw_kv.reshape(n_sh, -1, gw_kv.shape[1]), gw_o]
    (dq_b, dk_b, dv_b), p_mid = _sb_bwd(proj, 3 * nb, 4 * nb, 5 * nb, tot_b, do_b, batch, s,
                                        ride=(mid, False, (cs_up, cs_up, 0, 0, 0, cs_up)))
    dproj = jnp.concatenate([dqk_a, dv_a, dq_b, dk_b, dv_b, dgates], axis=1)
    gw_in = _wgrad_cols("gw_in", n1, dproj, n_sh)
    dn1 = _mm_cols_t("proj_in_bwd", dproj, w_in, F32)
    grad_x, dg_mix = _rms_bwd("norm_mix_bwd", dn1, x, g_mix, dh1, ("f32",))
    return loss_part, grad_x, gw_in, list(p_mid) + list(p_ffn), (dg_mix, dg_mem_q, dg_mem_kv, dg_ffn, dg_final)


WEIGHTS =("w_in", "w_up_a", "w_up_b", "w_out", "w_q_mem", "w_kv_mem", "w_o_mem", "w_ffn_gate", "w_ffn_up", "w_ffn_down")
GAINS = ("g_mix", "g_mem_q", "g_mem_kv", "g_ffn", "g_final")
ORDER = ("g_mix", "w_in", "w_up_a", "w_up_b", "w_out", "g_mem_q", "g_mem_kv", "w_q_mem", "w_kv_mem", "w_o_mem", "g_ffn",
         "w_ffn_gate", "w_ffn_up", "w_ffn_down", "g_final")


def kernel(x, mem, positions, g_mix, w_in, w_up_a, w_up_b, w_out, g_mem_q, g_mem_kv, w_q_mem, w_kv_mem, w_o_mem, g_ffn, w_ffn_gate, w_ffn_up, w_ffn_down, g_final, loss_target, m_g_mix, m_w_in, m_w_up_a, m_w_up_b, m_w_out, m_g_mem_q, m_g_mem_kv, m_w_q_mem, m_w_kv_mem, m_w_o_mem, m_g_ffn, m_w_ffn_gate, m_w_ffn_up, m_w_ffn_down, m_g_final, v_g_mix, v_w_in, v_w_up_a, v_w_up_b, v_w_out, v_g_mem_q, v_g_mem_kv, v_w_q_mem, v_w_kv_mem, v_w_o_mem, v_g_ffn, v_w_ffn_gate, v_w_ffn_up, v_w_ffn_down, v_g_final):
    given = dict(locals())
    batch, s, d = x.shape
    t = batch * s
    shard = {n: given[n].reshape(given[n].shape[-2:]) for n in WEIGHTS}
    gains = [given[n].reshape(1, d) for n in GAINS]

    pad = (-shard["w_ffn_down"].shape[0]) % LANES
    pads = {"w_ffn_gate": (0, pad), "w_ffn_up": (0, pad), "w_ffn_down": (pad, 0)}
    cast = _cast_weights([shard[n] for n in WEIGHTS], [pads.get(n, (0, 0)) for n in WEIGHTS])
    (w_in_all,) = _exchange("gather_w_in", cast[:1], True, (0,))
    loss_part, grad_x, gw_in, pieces, dgains = _local_step(
        x.reshape(t, d), mem.reshape(-1, d), positions.reshape(t, 1), loss_target.reshape(t, d), gains, w_in_all,
        cast[1:], batch)
    pieces = list(_exchange("scatter_gw_in", [gw_in], False, (0,))) + pieces

    grad, delta, new_m, new_v = {}, {}, {}, {}
    for n, p in zip(WEIGHTS, pieces):
        m2, v2 = given["m_" + n].reshape(shard[n].shape), given["v_" + n].reshape(shard[n].shape)
        outs = _adam("adam_" + n, p, shard[n], m2, v2)
        grad[n], delta[n], new_m[n], new_v[n] = [o.reshape(given[n].shape) for o in outs]

    rows = jnp.concatenate(list(dgains) + [jnp.zeros((N_DEV - len(GAINS), d), F32)], axis=0)
    g_all = _allreduce_small(rows)
    w_all = jnp.concatenate(gains + [jnp.zeros((N_DEV - len(GAINS), d), F32)], axis=0)
    m_all = jnp.concatenate([given["m_" + n].reshape(1, d) for n in GAINS] + [jnp.zeros((N_DEV - len(GAINS), d), F32)], axis=0)
    v_all = jnp.concatenate([given["v_" + n].reshape(1, d) for n in GAINS] + [jnp.ones((N_DEV - len(GAINS), d), F32)], axis=0)
    d_all, mo_all, vo_all = _adam_small(g_all, w_all, m_all, v_all)
    for i, n in enumerate(GAINS):
        grad[n] = g_all[i].reshape(given[n].shape)
        delta[n] = d_all[i].reshape(given[n].shape)
        new_m[n] = mo_all[i].reshape(given[n].shape)
        new_v[n] = vo_all[i].reshape(given[n].shape)

    loss = lax.psum(loss_part[0, 0], ("x", "y", "c"))
    return (loss, grad_x.reshape(x.shape), *[grad[n] for n in ORDER], *[delta[n] for n in ORDER],
            *[new_m[n] for n in ORDER], *[new_v[n] for n in ORDER])
```

```python
import functools
import math

import jax
import jax.numpy as jnp
from jax import lax
from jax.experimental import pallas as pl
from jax.experimental.pallas import tpu as pltpu

F32 = jnp.float32
BF16 = jnp.bfloat16

N_DEV = 8
HEAD_DIM = 64
MEM_HEAD_DIM = 128
N_HEADS_MEM = 4
BLOCK = 128
DIL_PATTERNS = ((128, 1), (512, 4), (2048, 16))
ROPE_THETA = 500000.0
ROPE_HALF = 8
RMS_EPS = 1e-6
ADAM_LR, ADAM_B1, ADAM_B2, ADAM_EPS, ADAM_WD, ADAM_STEP = 0.001, 0.9, 0.999, 1e-08, 0.01, 10
NEG = -1e30
ROW_TILE = 512
LANES = 128

ANY = pl.BlockSpec(memory_space=pl.ANY)
VMEM = pl.BlockSpec(memory_space=pltpu.VMEM)
NN = (((1,), (0,)), ((), ()))
NT = (((1,), (1,)), ((), ()))
TN = (((0,), (0,)), ((), ()))


def _params(sem):
    return pltpu.CompilerParams(dimension_semantics=sem)


def _mm(name, a, b, *, grid, a_spec, b_spec, o_shape, o_spec, dims, out_dtype, nk=1, res=None, res_spec=None):
    has_res = res is not None

    def body(*refs):
        a_ref, b_ref = refs[0], refs[1]
        r_ref = refs[2] if has_res else None
        o_ref = refs[3] if has_res else refs[2]
        p = lax.dot_general(a_ref[...], b_ref[...], dims, preferred_element_type=F32)
        if nk == 1:
            if has_res:
                p = p + r_ref[...].astype(F32)
            o_ref[...] = p.astype(out_dtype)
            return
        acc_ref = refs[-1]
        k = pl.program_id(len(grid) - 1)

        @pl.when(k == 0)
        def _():
            acc_ref[...] = p

        @pl.when(k > 0)
        def _():
            acc_ref[...] += p

        @pl.when(k == nk - 1)
        def _():
            t = acc_ref[...]
            if has_res:
                t = t + r_ref[...].astype(F32)
            o_ref[...] = t.astype(out_dtype)

    o_block = tuple(d for d in o_spec.block_shape if d is not None)
    sem = ("parallel",) * (len(grid) - 1) + (("arbitrary",) if nk > 1 else ("parallel",))
    return pl.pallas_call(
        body, name=name, grid=grid,
        in_specs=[a_spec, b_spec] + ([res_spec] if has_res else []),
        out_specs=o_spec, out_shape=jax.ShapeDtypeStruct(o_shape, out_dtype),
        scratch_shapes=[pltpu.VMEM(o_block, F32)] if nk > 1 else [],
        compiler_params=_params(sem),
    )(*([a, b] + ([res] if has_res else [])))


def _rms_fwd(name, x, g):
    t, d = x.shape
    tm = min(ROW_TILE, t)

    def body(x_ref, g_ref, o_ref):
        xf = x_ref[...]
        r = lax.rsqrt(jnp.mean(xf * xf, axis=-1, keepdims=True) + RMS_EPS)
        o_ref[...] = (xf * r * g_ref[...]).astype(BF16)

    return pl.pallas_call(
        body, name=name, grid=(t // tm,),
        in_specs=[pl.BlockSpec((tm, d), lambda i: (i, 0)), pl.BlockSpec((1, d), lambda i: (0, 0))],
        out_specs=pl.BlockSpec((tm, d), lambda i: (i, 0)), out_shape=jax.ShapeDtypeStruct((t, d), BF16),
        compiler_params=_params(("parallel",)),
    )(x, g)


def _rms_bwd(name, dn, x, g, dres, want):
    t, d = x.shape
    tm = min(ROW_TILE, t)
    has_res = dres is not None

    def body(*refs):
        dn_ref, x_ref, g_ref = refs[0], refs[1], refs[2]
        r_ref = refs[3] if has_res else None
        dx_refs, dg_ref = refs[-1 - len(want):-1], refs[-1]
        xf = x_ref[...]
        r = lax.rsqrt(jnp.mean(xf * xf, axis=-1, keepdims=True) + RMS_EPS)
        xh = xf * r
        dnf = dn_ref[...].astype(F32)
        if want:
            dxh = dnf * g_ref[...]
            dx = r * (dxh - xh * jnp.mean(dxh * xh, axis=-1, keepdims=True))
            if has_res:
                dx = dx + r_ref[...]
            for kind, dx_ref in zip(want, dx_refs):
                dx_ref[...] = dx.astype(F32 if kind == "f32" else BF16)

        @pl.when(pl.program_id(0) == 0)
        def _():
            dg_ref[...] = jnp.zeros_like(dg_ref)

        dg_ref[...] += jnp.sum(dnf * xh, axis=0, keepdims=True)

    row = pl.BlockSpec((tm, d), lambda i: (i, 0))
    vec = pl.BlockSpec((1, d), lambda i: (0, 0))
    return pl.pallas_call(
        body, name=name, grid=(t // tm,),
        in_specs=[row, row, vec] + ([row] if has_res else []),
        out_specs=[row] * len(want) + [vec],
        out_shape=[jax.ShapeDtypeStruct((t, d), F32 if kind == "f32" else BF16) for kind in want]
        + [jax.ShapeDtypeStruct((1, d), F32)],
        compiler_params=_params(("arbitrary",)),
    )(*([dn, x, g] + ([dres] if has_res else [])))


def _loss_head(h, tgt, g):
    t, d = h.shape
    tm = min(ROW_TILE, t)

    def body(h_ref, t_ref, g_ref, loss_ref, dh_ref, dhb_ref, dg_ref):
        xf = h_ref[...]
        gv = g_ref[...]
        r = lax.rsqrt(jnp.mean(xf * xf, axis=-1, keepdims=True) + RMS_EPS)
        xh = xf * r
        e = xh * gv - t_ref[...]
        dy = e * (1.0 / d)
        dxh = dy * gv
        dh = r * (dxh - xh * jnp.mean(dxh * xh, axis=-1, keepdims=True))
        dh_ref[...] = dh
        dhb_ref[...] = dh.astype(BF16)

        @pl.when(pl.program_id(0) == 0)
        def _():
            dg_ref[...] = jnp.zeros_like(dg_ref)
            loss_ref[...] = jnp.zeros_like(loss_ref)

        dg_ref[...] += jnp.sum(dy * xh, axis=0, keepdims=True)
        part = jnp.sum(jnp.sum(e * e, axis=1, keepdims=True), axis=0, keepdims=True) * (0.5 / d)
        loss_ref[...] += jnp.broadcast_to(part, loss_ref.shape)

    row = pl.BlockSpec((tm, d), lambda i: (i, 0))
    vec = pl.BlockSpec((1, d), lambda i: (0, 0))
    return pl.pallas_call(
        body, name="loss_head", grid=(t // tm,),
        in_specs=[row, row, vec],
        out_specs=[pl.BlockSpec((8, LANES), lambda i: (0, 0)), row, row, vec],
        out_shape=[jax.ShapeDtypeStruct((8, LANES), F32), jax.ShapeDtypeStruct((t, d), F32),
                   jax.ShapeDtypeStruct((t, d), BF16), jax.ShapeDtypeStruct((1, d), F32)],
        compiler_params=_params(("arbitrary",)),
    )(h, tgt, g)


def _rope_tables(pos, inv_freq, sel_lo, sel_hi):
    t = pos.shape[0]
    tm = min(ROW_TILE, t)

    def body(p_ref, f_ref, lo_ref, hi_ref, c_ref, sa_ref, sb_ref):
        ang = p_ref[...].astype(F32) * f_ref[...]
        rot = lo_ref[...] + hi_ref[...]
        cs, sn = jnp.cos(ang), jnp.sin(ang)
        c_ref[...] = cs * rot + (1.0 - rot)
        sa_ref[...] = -sn * lo_ref[...]
        sb_ref[...] = sn * hi_ref[...]

    vec = pl.BlockSpec((1, LANES), lambda i: (0, 0))
    row = pl.BlockSpec((tm, LANES), lambda i: (i, 0))
    return pl.pallas_call(
        body, name="rope_tables", grid=(t // tm,),
        in_specs=[pl.BlockSpec((tm, 1), lambda i: (i, 0)), vec, vec, vec],
        out_specs=[row, row, row], out_shape=[jax.ShapeDtypeStruct((t, LANES), F32)] * 3,
        compiler_params=_params(("parallel",)),
    )(pos, inv_freq, sel_lo, sel_hi)


def _rope_apply(name, src, col0, n_cols, cos_t, sin_a, sin_b, sign):
    t = src.shape[0]
    tm = min(ROW_TILE, t)

    def body(x_ref, c_ref, sa_ref, sb_ref, o_ref):
        cs, sa, sb = c_ref[...], sign * sa_ref[...], sign * sb_ref[...]
        for c in range(n_cols):
            cols = slice(c * LANES, (c + 1) * LANES)
            xf = x_ref[:, cols].astype(F32)
            up = pltpu.roll(xf, LANES - ROPE_HALF, 1)
            dn = pltpu.roll(xf, ROPE_HALF, 1)
            o_ref[:, cols] = (xf * cs + up * sa + dn * sb).astype(BF16)

    wide = n_cols * LANES
    tab = pl.BlockSpec((tm, LANES), lambda i: (i, 0))
    return pl.pallas_call(
        body, name=name, grid=(t // tm,),
        in_specs=[pl.BlockSpec((tm, wide), lambda i: (i, col0 // n_cols)), tab, tab, tab],
        out_specs=pl.BlockSpec((tm, wide), lambda i: (i, 0)),
        out_shape=jax.ShapeDtypeStruct((t, wide), BF16),
        compiler_params=_params(("parallel",)),
    )(src, cos_t, sin_a, sin_b)


DA_T = 256
FWD_STREAMS = 4
BWD_STREAMS = 2


def _lane_lo():
    return lax.broadcasted_iota(jnp.int32, (BLOCK, LANES), 1) < HEAD_DIM


def _dilated_bias_tiles(s):
    n = s // DA_T
    dist = (jnp.arange(n, dtype=jnp.int32)[:, None, None] * DA_T
            + jnp.arange(DA_T, dtype=jnp.int32)[None, :, None] - jnp.arange(DA_T, dtype=jnp.int32)[None, None, :])
    cnt = jnp.zeros(dist.shape, F32)
    for window, dil in DIL_PATTERNS:
        cnt = cnt + ((dist >= 0) & (dist % dil == 0) & (dist <= window)).astype(F32)
    return jnp.where(cnt > 0, jnp.log(jnp.maximum(cnt, 1.0)), NEG)


def _stack_heads(x, lo):
    zero = jnp.zeros_like(x)
    return jnp.concatenate([jnp.where(lo, x, zero), jnp.where(lo, zero, x)], axis=0)


def _da_fwd(qk, proj, v_col0, bias, batch, s, ride=None, streams=FWD_STREAMS):
    t = qk.shape[0]
    nq = s // DA_T
    n_pairs = 4
    ns = streams
    wide = ns * LANES
    scale = HEAD_DIM ** -0.5

    def body(q_ref, k_ref, v_ref, b_ref, o_ref, lse_ref, acc_ref, m_ref, l_ref):
        i = pl.program_id(2)
        lo = lax.broadcasted_iota(jnp.int32, (DA_T, LANES), 1) < HEAD_DIM
        ones = jnp.ones((DA_T, LANES), BF16)
        acc_ref[...] = jnp.zeros_like(acc_ref)
        m_ref[...] = jnp.full(m_ref.shape, NEG, F32)
        l_ref[...] = jnp.zeros_like(l_ref)
        qqs = [_stack_heads(q_ref[:, st * LANES:(st + 1) * LANES] * scale, lo) for st in range(ns)]

        def scores(st, rows, bias2):
            k = k_ref[rows, st * LANES:(st + 1) * LANES]
            return lax.dot_general(qqs[st], k, NT, preferred_element_type=F32) + bias2

        def softmax(st, sc):
            m_old = m_ref[st]
            m_new = jnp.maximum(m_old, jnp.max(sc, axis=1, keepdims=True))
            m_ref[st] = m_new
            return jnp.exp(sc - m_new).astype(BF16), jnp.exp(m_old - m_new)

        def values(st, rows, p, alpha):
            v = v_ref[rows, st * LANES:(st + 1) * LANES]
            vz = jnp.zeros_like(v)
            l_ref[st] = alpha * l_ref[st] + lax.dot_general(p, ones, NN, preferred_element_type=F32)
            pv = (lax.dot_general(p[:DA_T], jnp.where(lo, v, vz), NN, preferred_element_type=F32)
                  + lax.dot_general(p[DA_T:], jnp.where(lo, vz, v), NN, preferred_element_type=F32))
            acc_ref[st] = acc_ref[st] * jnp.where(lo, alpha[:DA_T], alpha[DA_T:]) + pv

        def trip(dlt, carry):
            rows = pl.ds(pl.multiple_of((i - dlt) * DA_T, DA_T), DA_T)
            bias_t = b_ref[dlt]
            bias2 = jnp.concatenate([bias_t, bias_t], axis=0)
            scs = [scores(st, rows, bias2) for st in range(ns)]
            pas = [softmax(st, scs[st]) for st in range(ns)]
            for st in range(ns):
                values(st, rows, *pas[st])
            return carry

        lax.fori_loop(0, i + 1, trip, 0)
        for st in range(ns):
            cols = slice(st * LANES, (st + 1) * LANES)
            l_t = l_ref[st]
            o_ref[:, cols] = (acc_ref[st] / jnp.where(lo, l_t[:DA_T], l_t[DA_T:])).astype(BF16)
            lse = m_ref[st] + jnp.log(l_t)
            lse_ref[:, cols] = jnp.where(lo, lse[:DA_T], lse[DA_T:])

    blk = pl.BlockSpec((DA_T, wide), lambda b, h, i: (b * nq + i, h))
    return _call(
        body, name="attn_a_fwd", grid=(batch, n_pairs // ns, nq),
        in_specs=[blk,
                  pl.BlockSpec((s, wide), lambda b, h, i: (b, n_pairs // ns + h)),
                  pl.BlockSpec((s, wide), lambda b, h, i: (b, v_col0 // ns + h)),
                  pl.BlockSpec((nq, DA_T, DA_T), lambda b, h, i: (0, 0, 0))],
        out_specs=[blk, blk],
        out_shape=[jax.ShapeDtypeStruct((t, n_pairs * LANES), BF16), jax.ShapeDtypeStruct((t, n_pairs * LANES), F32)],
        scratch=[pltpu.VMEM((ns, DA_T, LANES), F32), pltpu.VMEM((ns, 2 * DA_T, 1), F32),
                 pltpu.VMEM((ns, 2 * DA_T, LANES), F32)],
        sem=("parallel", "parallel", "arbitrary"), args=(qk, qk, proj, bias), ride=ride)


def _da_bwd(qk, proj, v_col0, bias, o, lse, do, batch, s, ride=None, streams=BWD_STREAMS):
    t = qk.shape[0]
    nq = s // DA_T
    n_pairs = 4
    ns = streams
    wide = ns * LANES
    scale = HEAD_DIM ** -0.5

    def body(q_ref, k_ref, v_ref, b_ref, o_ref, lse_ref, do_ref, dq_ref, dk_ref, dv_ref, dk_acc, dv_acc, dq_acc):
        i = pl.program_id(2)
        lo = lax.broadcasted_iota(jnp.int32, (DA_T, LANES), 1) < HEAD_DIM

        @pl.when(i == 0)
        def _():
            dk_acc[...] = jnp.zeros_like(dk_acc)
            dv_acc[...] = jnp.zeros_like(dv_acc)

        dq_acc[...] = jnp.zeros_like(dq_acc)
        qqs, dds, deltas, lses = [], [], [], []
        for st in range(ns):
            cols = slice(st * LANES, (st + 1) * LANES)
            do_ = do_ref[:, cols]
            qqs.append(_stack_heads(q_ref[:, cols] * scale, lo))
            dds.append(_stack_heads(do_, lo))
            prod = do_.astype(F32) * o_ref[:, cols].astype(F32)
            fz = jnp.zeros_like(prod)
            deltas.append(jnp.concatenate([jnp.sum(jnp.where(lo, prod, fz), axis=1, keepdims=True),
                                           jnp.sum(jnp.where(lo, fz, prod), axis=1, keepdims=True)], axis=0))
            lse_t = lse_ref[:, cols]
            lses.append(jnp.concatenate([lse_t[:, 0:1], lse_t[:, HEAD_DIM:HEAD_DIM + 1]], axis=0))

        def products(st, rows, bias2):
            cols = slice(st * LANES, (st + 1) * LANES)
            sc = lax.dot_general(qqs[st], k_ref[rows, cols], NT, preferred_element_type=F32) + bias2
            return sc, lax.dot_general(dds[st], v_ref[rows, cols], NT, preferred_element_type=F32)

        def weights(st, sc, dp):
            p = jnp.exp(sc - lses[st])
            return (p * (dp - deltas[st])).astype(BF16), p.astype(BF16)

        def gradients(st, rows, ds, p):
            cols = slice(st * LANES, (st + 1) * LANES)
            k = k_ref[rows, cols]
            kz = jnp.zeros_like(k)
            dq_acc[st] += (lax.dot_general(ds[:DA_T], jnp.where(lo, k, kz), NN, preferred_element_type=F32)
                           + lax.dot_general(ds[DA_T:], jnp.where(lo, kz, k), NN, preferred_element_type=F32))
            dk_acc[rows, cols] += lax.dot_general(ds, qqs[st], TN, preferred_element_type=F32)
            dv_acc[rows, cols] += lax.dot_general(p, dds[st], TN, preferred_element_type=F32)

        def trip(dlt, carry):
            rows = pl.ds(pl.multiple_of((i - dlt) * DA_T, DA_T), DA_T)
            bias_t = b_ref[dlt]
            bias2 = jnp.concatenate([bias_t, bias_t], axis=0)
            prods = [products(st, rows, bias2) for st in range(ns)]
            wts = [weights(st, *prods[st]) for st in range(ns)]
            for st in range(ns):
                gradients(st, rows, *wts[st])
            return carry

        lax.fori_loop(0, i + 1, trip, 0)
        for st in range(ns):
            dq_ref[:, st * LANES:(st + 1) * LANES] = (dq_acc[st] * scale).astype(BF16)

        @pl.when(i == nq - 1)
        def _():
            dk_ref[...] = dk_acc[...].astype(BF16)
            dv_ref[...] = dv_acc[...].astype(BF16)

    blk = pl.BlockSpec((DA_T, wide), lambda b, h, i: (b * nq + i, h))
    seq = pl.BlockSpec((s, wide), lambda b, h, i: (b, h))
    out = jax.ShapeDtypeStruct((t, n_pairs * LANES), BF16)
    return _call(
        body, name="attn_a_bwd", grid=(batch, n_pairs // ns, nq),
        in_specs=[blk,
                  pl.BlockSpec((s, wide), lambda b, h, i: (b, n_pairs // ns + h)),
                  pl.BlockSpec((s, wide), lambda b, h, i: (b, v_col0 // ns + h)),
                  pl.BlockSpec((nq, DA_T, DA_T), lambda b, h, i: (0, 0, 0)),
                  blk, blk, blk],
        out_specs=[blk, seq, seq], out_shape=[out, out, out],
        scratch=[pltpu.VMEM((s, wide), F32), pltpu.VMEM((s, wide), F32), pltpu.VMEM((ns, DA_T, LANES), F32)],
        sem=("parallel", "parallel", "arbitrary"), args=(qk, qk, proj, bias, o, lse, do), ride=ride)


SB_Q = 256


def _sb_consts(after):
    r = lax.broadcasted_iota(jnp.int32, (2 * BLOCK, 2 * BLOCK), 0) % BLOCK
    c = lax.broadcasted_iota(jnp.int32, (2 * BLOCK, 2 * BLOCK), 1)
    tri = (r > c) if after else (r < c)
    return jnp.logical_or(c >= BLOCK, tri).astype(BF16)


def _split(x):
    hi = x.astype(BF16)
    lo = (x - hi.astype(F32)).astype(BF16)
    return jnp.concatenate([hi, lo], axis=1)


def _sb_fwd(proj, q_col0, k_col0, v_col0, batch, s, ride=None, streams=FWD_STREAMS):
    t = proj.shape[0]
    nq = s // SB_Q
    n_pairs = 4
    ns = streams
    wide = ns * LANES
    scale = HEAD_DIM ** -0.5

    def body(q_ref, k_ref, v_ref, o_ref, tot_ref, acc_ref, run_ref):
        i = pl.program_id(2)
        lo_q = lax.broadcasted_iota(jnp.int32, (SB_Q, LANES), 1) < HEAD_DIM
        lo_k = _lane_lo()
        mat = _sb_consts(True)
        row = lax.broadcasted_iota(jnp.int32, (2 * SB_Q, LANES), 0) % SB_Q
        ahead = row - lax.broadcasted_iota(jnp.int32, (2 * SB_Q, LANES), 1)
        acc_ref[...] = jnp.zeros_like(acc_ref)
        run_ref[...] = jnp.zeros_like(run_ref)
        qqs = [_stack_heads(q_ref[:, st * LANES:(st + 1) * LANES] * scale, lo_q) for st in range(ns)]

        def units(todo):
            def rows(j):
                return pl.ds(pl.multiple_of(j * BLOCK, BLOCK), BLOCK)

            zs = [lax.dot_general(qqs[st], k_ref[rows(j), st * LANES:(st + 1) * LANES], NT, preferred_element_type=F32)
                  for st, j, _ in todo]
            logs = []
            for z, (_, _, off) in zip(zs, todo):
                lsig = jnp.minimum(z, 0.0) - jnp.log(1.0 + jnp.exp(-jnp.abs(z)))
                lneg = lsig - z
                if off is not None:
                    lneg = jnp.where(ahead > off, lneg, 0.0)
                logs.append((lsig, _split(lneg)))
            sums = [lax.dot_general(cat, mat, NN, preferred_element_type=F32) for _, cat in logs]
            probs = []
            for (lsig, _), sm, (st, _, off) in zip(logs, sums, todo):
                run = run_ref[st]
                a = jnp.exp(lsig + run + sm[:, :BLOCK])
                if off is not None:
                    a = jnp.where(ahead > off, a, 0.0)
                run_ref[st] = run + sm[:, BLOCK:]
                probs.append(a.astype(BF16))
            for ab, (st, j, _) in zip(probs, todo):
                v = v_ref[rows(j), st * LANES:(st + 1) * LANES]
                vz = jnp.zeros_like(v)
                acc_ref[st] += (lax.dot_general(ab[:SB_Q], jnp.where(lo_k, v, vz), NN, preferred_element_type=F32)
                                + lax.dot_general(ab[SB_Q:], jnp.where(lo_k, vz, v), NN, preferred_element_type=F32))

        units([(st, 2 * i + 1, BLOCK) for st in range(ns)] + [(st, 2 * i, 0) for st in range(ns)])

        def pair(p, carry):
            jp = i - 1 - p
            units([(st, 2 * jp + 1, None) for st in range(ns)] + [(st, 2 * jp, None) for st in range(ns)])
            return carry

        lax.fori_loop(0, i, pair, 0)
        for st in range(ns):
            cols = slice(st * LANES, (st + 1) * LANES)
            o_ref[:, cols] = acc_ref[st].astype(BF16)
            tot_ref[:, cols] = jnp.where(lo_q, run_ref[st, 0:SB_Q, :], run_ref[st, SB_Q:2 * SB_Q, :])

    def seq(col0):
        return pl.BlockSpec((s, wide), lambda b, h, i: (b, col0 // ns + h))

    blk = pl.BlockSpec((SB_Q, wide), lambda b, h, i: (b * nq + i, h))
    return _call(
        body, name="attn_b_fwd", grid=(batch, n_pairs // ns, nq),
        in_specs=[pl.BlockSpec((SB_Q, wide), lambda b, h, i: (b * nq + i, q_col0 // ns + h)), seq(k_col0), seq(v_col0)],
        out_specs=[blk, blk],
        out_shape=[jax.ShapeDtypeStruct((t, n_pairs * LANES), BF16), jax.ShapeDtypeStruct((t, n_pairs * LANES), F32)],
        scratch=[pltpu.VMEM((ns, SB_Q, LANES), F32), pltpu.VMEM((ns, 2 * SB_Q, LANES), F32)],
        sem=("parallel", "parallel", "arbitrary"), args=(proj, proj, proj), ride=ride)


def _sb_bwd(proj, q_col0, k_col0, v_col0, tot, do, batch, s, ride=None, streams=BWD_STREAMS):
    t = proj.shape[0]
    nq = s // SB_Q
    n_pairs = 4
    ns = streams
    wide = ns * LANES
    scale = HEAD_DIM ** -0.5

    def body(q_ref, k_ref, v_ref, tot_ref, do_ref, dq_ref, dk_ref, dv_ref, dk_acc, dv_acc, dq_acc, seen_ref, gsum_ref):
        i = pl.program_id(2)
        lo_q = lax.broadcasted_iota(jnp.int32, (SB_Q, LANES), 1) < HEAD_DIM
        lo_k = _lane_lo()

        @pl.when(i == 0)
        def _():
            dk_acc[...] = jnp.zeros_like(dk_acc)
            dv_acc[...] = jnp.zeros_like(dv_acc)

        mat_after = _sb_consts(True)
        mat_before = _sb_consts(False)
        row = lax.broadcasted_iota(jnp.int32, (2 * SB_Q, LANES), 0) % SB_Q
        ahead = row - lax.broadcasted_iota(jnp.int32, (2 * SB_Q, LANES), 1)
        dq_acc[...] = jnp.zeros_like(dq_acc)
        seen_ref[...] = jnp.zeros_like(seen_ref)
        gsum_ref[...] = jnp.zeros_like(gsum_ref)
        qqs, dds, totals = [], [], []
        for st in range(ns):
            cols = slice(st * LANES, (st + 1) * LANES)
            qqs.append(_stack_heads(q_ref[:, cols] * scale, lo_q))
            dds.append(_stack_heads(do_ref[:, cols], lo_q))
            tot_t = tot_ref[:, cols]
            totals.append(jnp.concatenate([jnp.broadcast_to(tot_t[:, 0:1], (SB_Q, LANES)),
                                           jnp.broadcast_to(tot_t[:, HEAD_DIM:HEAD_DIM + 1], (SB_Q, LANES))], axis=0))

        def units(todo):
            def rows(j):
                return pl.ds(pl.multiple_of(j * BLOCK, BLOCK), BLOCK)

            def cols(st):
                return slice(st * LANES, (st + 1) * LANES)

            prods = [(lax.dot_general(qqs[st], k_ref[rows(j), cols(st)], NT, preferred_element_type=F32),
                      lax.dot_general(dds[st], v_ref[rows(j), cols(st)], NT, preferred_element_type=F32))
                     for st, j, _ in todo]
            logs = []
            for (z, _), (_, _, off) in zip(prods, todo):
                lsig = jnp.minimum(z, 0.0) - jnp.log(1.0 + jnp.exp(-jnp.abs(z)))
                lneg = lsig - z
                if off is not None:
                    lneg = jnp.where(ahead > off, lneg, 0.0)
                logs.append((lsig, _split(lneg)))
            sums = [lax.dot_general(cat, mat_after, NN, preferred_element_type=F32) for _, cat in logs]
            gates = []
            for (lsig, _), sm, (_, da), (st, _, off) in zip(logs, sums, prods, todo):
                seen = seen_ref[st]
                a = jnp.exp(lsig + (totals[st] - seen - sm[:, BLOCK:]) + sm[:, :BLOCK])
                if off is not None:
                    a = jnp.where(ahead > off, a, 0.0)
                seen_ref[st] = seen + sm[:, BLOCK:]
                g = a * da
                gates.append((a.astype(BF16), g, _split(g)))
            gsums = [lax.dot_general(cat, mat_before, NN, preferred_element_type=F32) for _, _, cat in gates]
            outs = []
            for (lsig, _), (ab, g, _), gs, (st, _, off) in zip(logs, gates, gsums, todo):
                gsum = gsum_ref[st]
                dz = g - jnp.exp(lsig) * (g + gsum + gs[:, :BLOCK])
                if off is not None:
                    dz = jnp.where(ahead > off, dz, 0.0)
                gsum_ref[st] = gsum + gs[:, BLOCK:]
                outs.append((dz.astype(BF16), ab))
            for (dzb, ab), (st, j, _) in zip(outs, todo):
                k = k_ref[rows(j), cols(st)]
                kz = jnp.zeros_like(k)
                dq_acc[st] += (lax.dot_general(dzb[:SB_Q], jnp.where(lo_k, k, kz), NN, preferred_element_type=F32)
                               + lax.dot_general(dzb[SB_Q:], jnp.where(lo_k, kz, k), NN, preferred_element_type=F32))
                dk_acc[rows(j), cols(st)] += lax.dot_general(dzb, qqs[st], TN, preferred_element_type=F32)
                dv_acc[rows(j), cols(st)] += lax.dot_general(ab, dds[st], TN, preferred_element_type=F32)

        def pair(p, carry):
            units([(st, 2 * p, None) for st in range(ns)] + [(st, 2 * p + 1, None) for st in range(ns)])
            return carry

        lax.fori_loop(0, i, pair, 0)
        units([(st, 2 * i, 0) for st in range(ns)] + [(st, 2 * i + 1, BLOCK) for st in range(ns)])
        for st in range(ns):
            dq_ref[:, st * LANES:(st + 1) * LANES] = (dq_acc[st] * scale).astype(BF16)

        @pl.when(i == nq - 1)
        def _():
            dk_ref[...] = dk_acc[...].astype(BF16)
            dv_ref[...] = dv_acc[...].astype(BF16)

    def seq_in(col0):
        return pl.BlockSpec((s, wide), lambda b, h, i: (b, col0 // ns + h))

    blk = pl.BlockSpec((SB_Q, wide), lambda b, h, i: (b * nq + i, h))
    seq = pl.BlockSpec((s, wide), lambda b, h, i: (b, h))
    out = jax.ShapeDtypeStruct((t, n_pairs * LANES), BF16)
    return _call(
        body, name="attn_b_bwd", grid=(batch, n_pairs // ns, nq),
        in_specs=[pl.BlockSpec((SB_Q, wide), lambda b, h, i: (b * nq + i, q_col0 // ns + h)), seq_in(k_col0),
                  seq_in(v_col0), blk, blk],
        out_specs=[blk, seq, seq], out_shape=[out, out, out],
        scratch=[pltpu.VMEM((s, wide), F32), pltpu.VMEM((s, wide), F32), pltpu.VMEM((ns, SB_Q, LANES), F32),
                 pltpu.VMEM((ns, 2 * SB_Q, LANES), F32), pltpu.VMEM((ns, 2 * SB_Q, LANES), F32)],
        sem=("parallel", "parallel", "arbitrary"), args=(proj, proj, proj, tot, do), ride=ride)


MEM_Q_TILE = 256


def _mem_fwd(q, kv, batch, s, n_mem):
    t, width = q.shape
    tq = min(MEM_Q_TILE, s)
    nq = s // tq
    scale = MEM_HEAD_DIM ** -0.5

    def body(q_ref, kv_ref, o_ref):
        for h in range(N_HEADS_MEM):
            cols = slice(h * MEM_HEAD_DIM, (h + 1) * MEM_HEAD_DIM)
            k = kv_ref[:, cols]
            v = kv_ref[:, width + h * MEM_HEAD_DIM: width + (h + 1) * MEM_HEAD_DIM]
            sc = lax.dot_general(q_ref[:, cols], k, NT, preferred_element_type=F32) * scale
            p = jnp.exp(sc - jnp.max(sc, axis=1, keepdims=True))
            p = p / jnp.sum(p, axis=1, keepdims=True)
            o_ref[:, cols] = lax.dot_general(p.astype(BF16), v, NN, preferred_element_type=F32).astype(BF16)

    return pl.pallas_call(
        body, name="mem_attn_fwd", grid=(batch, nq),
        in_specs=[pl.BlockSpec((tq, width), lambda b, i: (b * nq + i, 0)),
                  pl.BlockSpec((n_mem, 2 * width), lambda b, i: (b, 0))],
        out_specs=pl.BlockSpec((tq, width), lambda b, i: (b * nq + i, 0)),
        out_shape=jax.ShapeDtypeStruct((t, width), BF16),
        compiler_params=_params(("parallel", "parallel")),
    )(q, kv)


def _mem_bwd(q, kv, do, batch, s, n_mem):
    t, width = q.shape
    tq = min(MEM_Q_TILE, s)
    nq = s // tq
    scale = MEM_HEAD_DIM ** -0.5

    def body(q_ref, kv_ref, do_ref, dq_ref, dkv_ref, acc):
        i = pl.program_id(1)

        @pl.when(i == 0)
        def _():
            acc[...] = jnp.zeros_like(acc)

        for h in range(N_HEADS_MEM):
            cols = slice(h * MEM_HEAD_DIM, (h + 1) * MEM_HEAD_DIM)
            vcols = slice(width + h * MEM_HEAD_DIM, width + (h + 1) * MEM_HEAD_DIM)
            qh, k, v, doh = q_ref[:, cols], kv_ref[:, cols], kv_ref[:, vcols], do_ref[:, cols]
            sc = lax.dot_general(qh, k, NT, preferred_element_type=F32) * scale
            p = jnp.exp(sc - jnp.max(sc, axis=1, keepdims=True))
            p = p / jnp.sum(p, axis=1, keepdims=True)
            dp = lax.dot_general(doh, v, NT, preferred_element_type=F32)
            ds = (p * (dp - jnp.sum(p * dp, axis=1, keepdims=True)) * scale).astype(BF16)
            dq_ref[:, cols] = lax.dot_general(ds, k, NN, preferred_element_type=F32).astype(BF16)
            acc[:, cols] += lax.dot_general(ds, qh, TN, preferred_element_type=F32)
            acc[:, vcols] += lax.dot_general(p.astype(BF16), doh, TN, preferred_element_type=F32)

        @pl.when(i == nq - 1)
        def _():
            dkv_ref[...] = acc[...].astype(BF16)

    row = pl.BlockSpec((tq, width), lambda b, i: (b * nq + i, 0))
    kvs = pl.BlockSpec((n_mem, 2 * width), lambda b, i: (b, 0))
    return pl.pallas_call(
        body, name="mem_attn_bwd", grid=(batch, nq),
        in_specs=[row, kvs, row], out_specs=[row, kvs],
        out_shape=[jax.ShapeDtypeStruct((t, width), BF16), jax.ShapeDtypeStruct((batch * n_mem, 2 * width), BF16)],
        scratch_shapes=[pltpu.VMEM((n_mem, 2 * width), F32)],
        compiler_params=_params(("parallel", "arbitrary")),
    )(q, kv, do)


def _mixer_fwd(o_a, o_b, w_a, w_b, proj, gate_col0):
    t, width = o_a.shape
    d = w_a.shape[1]
    tm = min(ROW_TILE, t)
    gb0 = gate_col0 * LANES // d

    def body(oa_ref, ob_ref, wa_ref, wb_ref, ga_ref, gb_ref, ua_ref, ub_ref, mix_ref):
        ua = lax.dot_general(oa_ref[...], wa_ref[...], NN, preferred_element_type=F32)
        ub = lax.dot_general(ob_ref[...], wb_ref[...], NN, preferred_element_type=F32)
        ua_ref[...] = ua.astype(BF16)
        ub_ref[...] = ub.astype(BF16)
        mix_ref[...] = (jax.nn.sigmoid(ga_ref[...].astype(F32)) * ua
                        + jax.nn.sigmoid(gb_ref[...].astype(F32)) * ub).astype(BF16)

    row = pl.BlockSpec((tm, width), lambda i: (i, 0))
    wsp = pl.BlockSpec((width, d), lambda i: (0, 0))
    out = pl.BlockSpec((tm, d), lambda i: (i, 0))
    osh = jax.ShapeDtypeStruct((t, d), BF16)
    return pl.pallas_call(
        body, name="mixer_fwd", grid=(t // tm,),
        in_specs=[row, row, wsp, wsp,
                  pl.BlockSpec((tm, d), lambda i: (i, gb0)), pl.BlockSpec((tm, d), lambda i: (i, gb0 + 1))],
        out_specs=[out, out, out], out_shape=[osh, osh, osh],
        compiler_params=_params(("parallel",)),
    )(o_a, o_b, w_a, w_b, proj, proj)


def _mixer_bwd(dmix, ua, ub, proj, gate_col0):
    t, d = dmix.shape
    tm = min(ROW_TILE, t)
    nc = d // LANES

    def body(dm_ref, ua_ref, ub_ref, ga_ref, gb_ref, dua_ref, dub_ref, dg_ref):
        dm = dm_ref[...].astype(F32)
        sa = jax.nn.sigmoid(ga_ref[...].astype(F32))
        sb = jax.nn.sigmoid(gb_ref[...].astype(F32))
        dua_ref[...] = (dm * sa).astype(BF16)
        dub_ref[...] = (dm * sb).astype(BF16)
        dg_ref[:, 0:d] = (dm * ua_ref[...].astype(F32) * sa * (1.0 - sa)).astype(BF16)
        dg_ref[:, d:2 * d] = (dm * ub_ref[...].astype(F32) * sb * (1.0 - sb)).astype(BF16)

    row = pl.BlockSpec((tm, d), lambda i: (i, 0))
    return pl.pallas_call(
        body, name="mixer_bwd", grid=(t // tm,),
        in_specs=[row, row, row,
                  pl.BlockSpec((tm, d), lambda i: (i, gate_col0 // nc)),
                  pl.BlockSpec((tm, d), lambda i: (i, gate_col0 // nc + 1))],
        out_specs=[row, row, pl.BlockSpec((tm, 2 * d), lambda i: (i, 0))],
        out_shape=[jax.ShapeDtypeStruct((t, d), BF16), jax.ShapeDtypeStruct((t, d), BF16),
                   jax.ShapeDtypeStruct((t, 2 * d), BF16)],
        compiler_params=_params(("parallel",)),
    )(dmix, ua, ub, proj, proj)


FFN_COLS = 1024


def _ffn_up(n, w_gate, w_up):
    t, d = n.shape
    hidden = w_gate.shape[1]
    tm = min(ROW_TILE, t)
    tn = min(FFN_COLS, hidden)

    def body(n_ref, wg_ref, wu_ref, hg_ref, hu_ref, act_ref):
        hg = lax.dot_general(n_ref[...], wg_ref[...], NN, preferred_element_type=F32)
        hu = lax.dot_general(n_ref[...], wu_ref[...], NN, preferred_element_type=F32)
        hg_ref[...] = hg.astype(BF16)
        hu_ref[...] = hu.astype(BF16)
        act_ref[...] = (hg * jax.nn.sigmoid(hg) * hu).astype(BF16)

    wsp = pl.BlockSpec((d, tn), lambda j, i: (0, j))
    out = pl.BlockSpec((tm, tn), lambda j, i: (i, j))
    osh = jax.ShapeDtypeStruct((t, hidden), BF16)
    return pl.pallas_call(
        body, name="ffn_up", grid=(hidden // tn, t // tm),
        in_specs=[pl.BlockSpec((tm, d), lambda j, i: (i, 0)), wsp, wsp],
        out_specs=[out, out, out], out_shape=[osh, osh, osh],
        compiler_params=_params(("parallel", "parallel")),
    )(n, w_gate, w_up)


def _ffn_bwd_act(dh, w_down, hg, hu):
    t, d = dh.shape
    hidden = w_down.shape[0]
    tm = min(ROW_TILE, t)
    tn = min(FFN_COLS, hidden)

    def body(dh_ref, wd_ref, hg_ref, hu_ref, dhg_ref, dhu_ref):
        dact = lax.dot_general(dh_ref[...], wd_ref[...], NT, preferred_element_type=F32)
        hg = hg_ref[...].astype(F32)
        sg = jax.nn.sigmoid(hg)
        dhu_ref[...] = (dact * hg * sg).astype(BF16)
        dhg_ref[...] = (dact * hu_ref[...].astype(F32) * sg * (1.0 + hg * (1.0 - sg))).astype(BF16)

    hid = pl.BlockSpec((tm, tn), lambda j, i: (i, j))
    osh = jax.ShapeDtypeStruct((t, hidden), BF16)
    return pl.pallas_call(
        body, name="ffn_bwd_act", grid=(hidden // tn, t // tm),
        in_specs=[pl.BlockSpec((tm, d), lambda j, i: (i, 0)), pl.BlockSpec((tn, d), lambda j, i: (j, 0)), hid, hid],
        out_specs=[hid, hid], out_shape=[osh, osh],
        compiler_params=_params(("parallel", "parallel")),
    )(dh, w_down, hg, hu)


MM_ROWS = 1024


def _mm_cols(name, a, w, out_dtype=BF16):
    t, k = a.shape
    n_sh, _, cs = w.shape
    tm = min(MM_ROWS, t)
    return _mm(name, a, w, grid=(n_sh, t // tm),
               a_spec=pl.BlockSpec((tm, k), lambda j, i: (i, 0)), b_spec=pl.BlockSpec((None, k, cs), lambda j, i: (j, 0, 0)),
               o_shape=(t, n_sh * cs), o_spec=pl.BlockSpec((tm, cs), lambda j, i: (i, j)), dims=NN, out_dtype=out_dtype)


def _mm_cols_t(name, a, w, out_dtype, res=None):
    t = a.shape[0]
    n_sh, k, cs = w.shape
    tm = min(MM_ROWS, t)
    o_spec = pl.BlockSpec((tm, k), lambda i, j: (i, 0))
    return _mm(name, a, w, grid=(t // tm, n_sh),
               a_spec=pl.BlockSpec((tm, cs), lambda i, j: (i, j)), b_spec=pl.BlockSpec((None, k, cs), lambda i, j: (j, 0, 0)),
               o_shape=(t, k), o_spec=o_spec, dims=NT, out_dtype=out_dtype, nk=n_sh, res=res,
               res_spec=o_spec if res is not None else None)


def _wgrad_cols(name, a, g, n_sh):
    t, k = a.shape
    cs = g.shape[1] // n_sh
    tm = min(MM_ROWS, t)
    return _mm(name, a, g, grid=(n_sh, t // tm),
               a_spec=pl.BlockSpec((tm, k), lambda j, r: (r, 0)), b_spec=pl.BlockSpec((tm, cs), lambda j, r: (r, j)),
               o_shape=(n_sh, k, cs), o_spec=pl.BlockSpec((None, k, cs), lambda j, r: (j, 0, 0)), dims=TN,
               out_dtype=BF16, nk=t // tm)


def _mm_w(name, a, w, out_dtype, dims=NN, res=None, tm=MM_ROWS, tn=1024):
    t, k = a.shape
    n = w.shape[1] if dims == NN else w.shape[0]
    tm, tn = min(tm, t), min(tn, n)
    o_spec = pl.BlockSpec((tm, tn), lambda j, i: (i, j))
    b_spec = pl.BlockSpec((k, tn), lambda j, i: (0, j)) if dims == NN else pl.BlockSpec((tn, k), lambda j, i: (j, 0))
    return _mm(name, a, w, grid=(n // tn, t // tm), a_spec=pl.BlockSpec((tm, k), lambda j, i: (i, 0)), b_spec=b_spec,
               o_shape=(t, n), o_spec=o_spec, dims=dims, out_dtype=out_dtype, res=res,
               res_spec=o_spec if res is not None else None)


def _wgrad(name, a, g, tk=1024, tn=1024):
    t, k = a.shape
    n = g.shape[1]
    tm, tk, tn = min(MM_ROWS, t), min(tk, k), min(tn, n)
    return _mm(name, a, g, grid=(k // tk, n // tn, t // tm),
               a_spec=pl.BlockSpec((tm, tk), lambda p, q, r: (r, p)), b_spec=pl.BlockSpec((tm, tn), lambda p, q, r: (r, q)),
               o_shape=(k, n), o_spec=pl.BlockSpec((tk, tn), lambda p, q, r: (p, q)), dims=TN, out_dtype=BF16, nk=t // tm)


def _peers():
    x, y, c = lax.axis_index("x"), lax.axis_index("y"), lax.axis_index("c")
    me = 4 * x + 2 * y + c
    out = []
    for k in range(1, N_DEV):
        kx, ky, kc = (k >> 2) & 1, (k >> 1) & 1, k & 1
        px = 1 - x if kx else x
        py = 1 - y if ky else y
        pc = 1 - c if kc else c
        out.append(((px, py, pc), 4 * px + 2 * py + pc))
    return me, out


def _cast_weights(ws, pads):
    def body(*refs):
        n = len(refs) // 2
        for i_ref, o_ref, (pr, pc) in zip(refs[:n], refs[n:], pads):
            r, c = i_ref.shape
            o_ref[0:r, 0:c] = i_ref[...].astype(BF16)
            if pr:
                o_ref[r:r + pr, :] = jnp.zeros((pr, c), BF16)
            if pc:
                o_ref[:, c:c + pc] = jnp.zeros((r, pc), BF16)

    return pl.pallas_call(
        body, name="cast_weights", in_specs=[VMEM] * len(ws), out_specs=[VMEM] * len(ws),
        out_shape=[jax.ShapeDtypeStruct((w.shape[0] + pr, w.shape[1] + pc), BF16) for w, (pr, pc) in zip(ws, pads)],
    )(*ws)


def _exchange_copies(ins, outs, sems, gather, cols, landed):
    send_sems, recv_sems, loc_sems = sems
    n_peer = N_DEV - 1
    me, peers = _peers()

    def win(ref, j, c):
        return ref.at[:, pl.ds(pl.multiple_of(j * c, LANES), c)]

    def src(w, j):
        if gather:
            return ins[w]
        return win(ins[w], j, cols[w]) if cols[w] else ins[w].at[j]

    def dst(w, j):
        return win(outs[w], j, cols[w]) if gather and cols[w] else outs[w].at[j]

    local = [pltpu.make_async_copy(src(w, me), dst(w, me), loc_sems.at[w]) for w in range(len(ins))]
    remote = [pltpu.make_async_remote_copy(
        src_ref=src(w, idx), dst_ref=dst(w, idx if landed else me),
        send_sem=send_sems.at[w * n_peer + k], recv_sem=recv_sems.at[w * n_peer + k],
        device_id=dev, device_id_type=pl.DeviceIdType.MESH)
        for k, (dev, idx) in enumerate(peers) for w in range(len(ins))]
    return local, remote


def _exchange_start(ins, outs, sems, gather, cols):
    local, remote = _exchange_copies(ins, outs, sems, gather, cols, False)
    for cp in local + remote:
        cp.start()


def _exchange_wait(ins, outs, sems, gather, cols):
    local, remote = _exchange_copies(ins, outs, sems, gather, cols, True)
    for cp in local:
        cp.wait()
    for cp in remote:
        cp.wait_send()
        cp.wait_recv()


def _exchange_shapes(arrs, gather, cols):
    n = len(arrs)
    out_shape = []
    for a, c in zip(arrs, cols):
        if gather:
            shape = (a.shape[0], N_DEV * c) if c else (N_DEV,) + a.shape
        else:
            shape = (N_DEV, a.shape[0], c) if c else a.shape
        out_shape.append(jax.ShapeDtypeStruct(shape, a.dtype))
    sems = [pltpu.SemaphoreType.DMA((n * (N_DEV - 1),)), pltpu.SemaphoreType.DMA((n * (N_DEV - 1),)),
            pltpu.SemaphoreType.DMA((n,))]
    return out_shape, sems


def _exchange(name, arrs, gather, cols):
    n = len(arrs)

    def body(*refs):
        ins, outs, sems = refs[:n], refs[n:2 * n], refs[2 * n:]
        _exchange_start(ins, outs, sems, gather, cols)
        _exchange_wait(ins, outs, sems, gather, cols)

    out_shape, sems = _exchange_shapes(arrs, gather, cols)
    return pl.pallas_call(body, name=name, in_specs=[ANY] * n, out_specs=[ANY] * n, out_shape=out_shape,
                          scratch_shapes=sems)(*arrs)


def _call(body, *, name, grid, in_specs, out_specs, out_shape, scratch, sem, args, ride=None):
    if ride is None:
        outs = pl.pallas_call(body, name=name, grid=grid, in_specs=in_specs, out_specs=out_specs, out_shape=out_shape,
                              scratch_shapes=scratch, compiler_params=_params(sem))(*args)
        return outs, None
    arrs, gather, cols = ride
    n, n_in, n_out, n_scr = len(arrs), len(in_specs), len(out_specs), len(scratch)
    x_shape, x_sems = _exchange_shapes(arrs, gather, cols)

    def riding(*refs):
        ins, x_ins = refs[:n_in], refs[n_in:n_in + n]
        outs = refs[n_in + n:n_in + n + n_out]
        x_outs = refs[n_in + n + n_out:n_in + 2 * n + n_out]
        scr = refs[n_in + 2 * n + n_out:n_in + 2 * n + n_out + n_scr]
        sems = refs[n_in + 2 * n + n_out + n_scr:]
        first = functools.reduce(jnp.logical_and, [pl.program_id(a) == 0 for a in range(len(grid))])
        last = functools.reduce(jnp.logical_and, [pl.program_id(a) == g - 1 for a, g in enumerate(grid)])

        @pl.when(first)
        def _():
            _exchange_start(x_ins, x_outs, sems, gather, cols)

        body(*ins, *outs, *scr)

        @pl.when(last)
        def _():
            _exchange_wait(x_ins, x_outs, sems, gather, cols)

    res = pl.pallas_call(
        riding, name=name, grid=grid, in_specs=list(in_specs) + [ANY] * n, out_specs=list(out_specs) + [ANY] * n,
        out_shape=list(out_shape) + x_shape, scratch_shapes=list(scratch) + x_sems,
        compiler_params=_params(("arbitrary",) * len(grid)))(*args, *arrs)
    return res[:n_out], res[n_out:]


def _allreduce_small(v):
    def body(v_ref, o_ref, all_ref, send_sems, recv_sems):
        me, peers = _peers()
        all_ref[me] = v_ref[...]
        for k, (dev, idx) in enumerate(peers):
            pltpu.make_async_remote_copy(src_ref=v_ref, dst_ref=all_ref.at[me], send_sem=send_sems.at[k],
                                         recv_sem=recv_sems.at[k], device_id=dev,
                                         device_id_type=pl.DeviceIdType.MESH).start()
        for k, (dev, idx) in enumerate(peers):
            cp = pltpu.make_async_remote_copy(src_ref=v_ref, dst_ref=all_ref.at[idx], send_sem=send_sems.at[k],
                                              recv_sem=recv_sems.at[k], device_id=dev,
                                              device_id_type=pl.DeviceIdType.MESH)
            cp.wait_send()
            cp.wait_recv()
        tot = all_ref[0]
        for dvc in range(1, N_DEV):
            tot = tot + all_ref[dvc]
        o_ref[...] = tot

    return pl.pallas_call(
        body, name="allreduce_small", in_specs=[VMEM], out_specs=VMEM,
        out_shape=jax.ShapeDtypeStruct(v.shape, F32),
        scratch_shapes=[pltpu.VMEM((N_DEV,) + v.shape, F32), pltpu.SemaphoreType.DMA((N_DEV - 1,)),
                        pltpu.SemaphoreType.DMA((N_DEV - 1,))],
    )(v)


def _adam_math(g, w, m, v):
    m_new = ADAM_B1 * m + (1.0 - ADAM_B1) * g
    v_new = ADAM_B2 * v + (1.0 - ADAM_B2) * (g * g)
    m_hat = m_new / (1.0 - ADAM_B1 ** ADAM_STEP)
    v_hat = v_new / (1.0 - ADAM_B2 ** ADAM_STEP)
    delta = -ADAM_LR * (m_hat / (jnp.sqrt(v_hat) + ADAM_EPS) + ADAM_WD * w)
    return delta, m_new, v_new


def _adam(name, pieces, w, m, v):
    r, c = w.shape
    cp = pieces.shape[2]
    tr = r
    for cand in (256, 176, 128, 64):
        if r % cand == 0 and r > cand:
            tr = cand
            break

    def body(p_ref, w_ref, m_ref, v_ref, g_ref, d_ref, mo_ref, vo_ref):
        g = p_ref[0, :, 0:c].astype(F32)
        for dvc in range(1, N_DEV):
            g = g + p_ref[dvc, :, 0:c].astype(F32)
        delta, m_new, v_new = _adam_math(g, w_ref[...], m_ref[...], v_ref[...])
        g_ref[...] = g
        d_ref[...] = delta
        mo_ref[...] = m_new
        vo_ref[...] = v_new

    blk = pl.BlockSpec((tr, c), lambda i: (i, 0))
    osh = jax.ShapeDtypeStruct((r, c), F32)
    return pl.pallas_call(
        body, name=name, grid=(r // tr,),
        in_specs=[pl.BlockSpec((N_DEV, tr, cp), lambda i: (0, i, 0)), blk, blk, blk],
        out_specs=[blk, blk, blk, blk], out_shape=[osh, osh, osh, osh],
        compiler_params=_params(("parallel",)),
    )(pieces, w, m, v)


def _adam_small(g, w, m, v):
    def body(g_ref, w_ref, m_ref, v_ref, d_ref, mo_ref, vo_ref):
        delta, m_new, v_new = _adam_math(g_ref[...], w_ref[...], m_ref[...], v_ref[...])
        d_ref[...] = delta
        mo_ref[...] = m_new
        vo_ref[...] = v_new

    osh = jax.ShapeDtypeStruct(g.shape, F32)
    return pl.pallas_call(body, name="adam_small", in_specs=[VMEM] * 4, out_specs=[VMEM] * 3,
                          out_shape=[osh, osh, osh])(g, w, m, v)


def _local_step(x, mem, pos, tgt, gains, w_in, shards, batch):
    g_mix, g_mem_q, g_mem_kv, g_ffn, g_final = gains
    t, d = x.shape
    s = t // batch
    n_mem = mem.shape[0] // batch
    n_sh = N_DEV
    width = shards[0].shape[0]
    nb = width // LANES

    lane = jnp.arange(LANES, dtype=jnp.int32) % HEAD_DIM
    sel_lo = (lane < ROPE_HALF).astype(F32)[None, :]
    sel_hi = ((lane >= ROPE_HALF) & (lane < 2 * ROPE_HALF)).astype(F32)[None, :]
    freqs = ROPE_THETA ** (-jnp.arange(ROPE_HALF, dtype=F32) / ROPE_HALF)
    inv_freq = jnp.where(lane < 2 * ROPE_HALF, freqs[lane % ROPE_HALF], 0.0)[None, :]
    cos_t, sin_a, sin_b = _rope_tables(pos, inv_freq, sel_lo, sel_hi)
    bias = _dilated_bias_tiles(s)

    n1 = _rms_fwd("norm_mix", x, g_mix)
    proj = _mm_cols("proj_in", n1, w_in)
    qk_a = _rope_apply("rope_fwd", proj, 0, 2 * nb, cos_t, sin_a, sin_b, 1.0)
    cs_up, cs_ffn = shards[0].shape[1], shards[6].shape[1]
    (o_a, lse_a), (w_up_a, w_up_b, w_out, w_q, w_kv, w_o, w_fd) = _da_fwd(
        qk_a, proj, 2 * nb, bias, batch, s,
        ride=(shards[:6] + shards[8:], True, (cs_up, cs_up, 0, 0, 0, cs_up, 0)))
    (o_b, tot_b), (w_fg, w_fu) = _sb_fwd(proj, 3 * nb, 4 * nb, 5 * nb, batch, s,
                                         ride=(shards[6:8], True, (cs_ffn, cs_ffn)))
    w_out = w_out.reshape(d, d)
    w_q = w_q.reshape(d, -1)
    w_kv = w_kv.reshape(d, -1)
    w_fd = w_fd.reshape(-1, d)
    ua, ub, mixed = _mixer_fwd(o_a, o_b, w_up_a, w_up_b, proj, 6 * nb)
    h1 = _mm_w("mix_out", mixed, w_out, F32, res=x)
    n2 = _rms_fwd("norm_mem_q", h1, g_mem_q)
    mem_n = _rms_fwd("norm_mem_kv", mem, g_mem_kv)
    q_m = _mm_w("mem_q", n2, w_q, BF16)
    kv_m = _mm_w("mem_kv", mem_n, w_kv, BF16)
    o_m = _mem_fwd(q_m, kv_m, batch, s, n_mem)
    h2 = _mm_w("mem_out", o_m, w_o, F32, res=h1)
    n3 = _rms_fwd("norm_ffn", h2, g_ffn)
    hg, hu, act = _ffn_up(n3, w_fg, w_fu)
    h3 = _mm_w("ffn_down", act, w_fd, F32, res=h2, tm=ROW_TILE)
    loss_part, dh3, dh3_b, dg_final = _loss_head(h3, tgt, g_final.reshape(1, d))

    dhg, dhu = _ffn_bwd_act(dh3_b, w_fd, hg, hu)
    gw_fd = _wgrad("gw_ffn_down", act, dh3_b)
    gw_fg = _wgrad("gw_ffn_gate", n3, dhg)
    gw_fu = _wgrad("gw_ffn_up", n3, dhu)
    dn3 = _mm_w("dn_ffn_gate", dhg, w_fg, F32, dims=NT, tm=ROW_TILE)
    dn3 = _mm_w("dn_ffn_up", dhu, w_fu, F32, dims=NT, res=dn3, tm=ROW_TILE)
    dh2, dh2_b, dg_ffn = _rms_bwd("norm_ffn_bwd", dn3, h2, g_ffn, dh3, ("f32", "bf16"))

    do_m = _mm_w("mem_out_bwd", dh2_b, w_o, BF16, dims=NT)
    gw_o = _wgrad("gw_mem_o", o_m, dh2_b)
    dq_m, dkv_m = _mem_bwd(q_m, kv_m, do_m, batch, s, n_mem)
    gw_q = _wgrad("gw_mem_q", n2, dq_m)
    gw_kv = _wgrad("gw_mem_kv", mem_n, dkv_m)
    dn2 = _mm_w("mem_q_bwd", dq_m, w_q, F32, dims=NT)
    dmem_n = _mm_w("mem_kv_bwd", dkv_m, w_kv, F32, dims=NT)
    (dg_mem_kv,) = _rms_bwd("norm_mem_kv_bwd", dmem_n, mem, g_mem_kv, None, ())
    dh1, dh1_b, dg_mem_q = _rms_bwd("norm_mem_q_bwd", dn2, h1, g_mem_q, dh2, ("f32", "bf16"))

    dmix = _mm_w("mix_out_bwd", dh1_b, w_out, BF16, dims=NT)
    gw_out = _wgrad("gw_out", mixed, dh1_b)
    dua, dub, dgates = _mixer_bwd(dmix, ua, ub, proj, 6 * nb)
    do_a = _mm_w("up_a_bwd", dua, w_up_a, BF16, dims=NT)
    do_b = _mm_w("up_b_bwd", dub, w_up_b, BF16, dims=NT)
    gw_ua = _wgrad("gw_up_a", o_a, dua)
    gw_ub = _wgrad("gw_up_b", o_b, dub)
    (dq_ar, dk_ar, dv_a), (p_fg, p_fd) = _da_bwd(
        qk_a, proj, 2 * nb, bias, o_a, lse_a, do_a, batch, s,
        ride=([gw_fg, gw_fd.reshape(n_sh, -1, d)], False, (cs_ffn, 0)))
    dqk_a = _rope_apply("rope_bwd", jnp.concatenate([dq_ar, dk_ar], axis=1), 0, 2 * nb, cos_t, sin_a, sin_b, -1.0)
    mid = [gw_ua, gw_ub, gw_out.reshape(n_sh, -1, d), gw_q.reshape(n_sh, -1, gw_q.shape[1]),
           gw_kv.reshape(n_sh, -1, gw_kv.shape[1]), gw_o, gw_fu]
    (dq_b, dk_b, dv_b), (*p_mid, p_fu) = _sb_bwd(proj, 3 * nb, 4 * nb, 5 * nb, tot_b, do_b, batch, s,
                                                 ride=(mid, False, (cs_up, cs_up, 0, 0, 0, cs_up, cs_ffn)))
    p_ffn = [p_fg, p_fu, p_fd]
    dproj = jnp.concatenate([dqk_a, dv_a, dq_b, dk_b, dv_b, dgates], axis=1)
    gw_in = _wgrad_cols("gw_in", n1, dproj, n_sh)
    dn1 = _mm_cols_t("proj_in_bwd", dproj, w_in, F32)
    grad_x, dg_mix = _rms_bwd("norm_mix_bwd", dn1, x, g_mix, dh1, ("f32",))
    return loss_part, grad_x, gw_in, list(p_mid) + list(p_ffn), (dg_mix, dg_mem_q, dg_mem_kv, dg_ffn, dg_final)


WEIGHTS =("w_in", "w_up_a", "w_up_b", "w_out", "w_q_mem", "w_kv_mem", "w_o_mem", "w_ffn_gate", "w_ffn_up", "w_ffn_down")
GAINS = ("g_mix", "g_mem_q", "g_mem_kv", "g_ffn", "g_final")
ORDER = ("g_mix", "w_in", "w_up_a", "w_up_b", "w_out", "g_mem_q", "g_mem_kv", "w_q_mem", "w_kv_mem", "w_o_mem", "g_ffn",
         "w_ffn_gate", "w_ffn_up", "w_ffn_down", "g_final")


def kernel(x, mem, positions, g_mix, w_in, w_up_a, w_up_b, w_out, g_mem_q, g_mem_kv, w_q_mem, w_kv_mem, w_o_mem, g_ffn, w_ffn_gate, w_ffn_up, w_ffn_down, g_final, loss_target, m_g_mix, m_w_in, m_w_up_a, m_w_up_b, m_w_out, m_g_mem_q, m_g_mem_kv, m_w_q_mem, m_w_kv_mem, m_w_o_mem, m_g_ffn, m_w_ffn_gate, m_w_ffn_up, m_w_ffn_down, m_g_final, v_g_mix, v_w_in, v_w_up_a, v_w_up_b, v_w_out, v_g_mem_q, v_g_mem_kv, v_w_q_mem, v_w_kv_mem, v_w_o_mem, v_g_ffn, v_w_ffn_gate, v_w_ffn_up, v_w_ffn_down, v_g_final):
    given = dict(locals())
    batch, s, d = x.shape
    t = batch * s
    shard = {n: given[n].reshape(given[n].shape[-2:]) for n in WEIGHTS}
    gains = [given[n].reshape(1, d) for n in GAINS]

    pad = (-shard["w_ffn_down"].shape[0]) % LANES
    pads = {"w_ffn_gate": (0, pad), "w_ffn_up": (0, pad), "w_ffn_down": (pad, 0)}
    cast = _cast_weights([shard[n] for n in WEIGHTS], [pads.get(n, (0, 0)) for n in WEIGHTS])
    (w_in_all,) = _exchange("gather_w_in", cast[:1], True, (0,))
    loss_part, grad_x, gw_in, pieces, dgains = _local_step(
        x.reshape(t, d), mem.reshape(-1, d), positions.reshape(t, 1), loss_target.reshape(t, d), gains, w_in_all,
        cast[1:], batch)
    pieces = list(_exchange("scatter_gw_in", [gw_in], False, (0,))) + pieces

    grad, delta, new_m, new_v = {}, {}, {}, {}
    for n, p in zip(WEIGHTS, pieces):
        m2, v2 = given["m_" + n].reshape(shard[n].shape), given["v_" + n].reshape(shard[n].shape)
        outs = _adam("adam_" + n, p, shard[n], m2, v2)
        grad[n], delta[n], new_m[n], new_v[n] = [o.reshape(given[n].shape) for o in outs]

    rows = jnp.concatenate(list(dgains) + [jnp.zeros((N_DEV - len(GAINS), d), F32)], axis=0)
    g_all = _allreduce_small(rows)
    w_all = jnp.concatenate(gains + [jnp.zeros((N_DEV - len(GAINS), d), F32)], axis=0)
    m_all = jnp.concatenate([given["m_" + n].reshape(1, d) for n in GAINS] + [jnp.zeros((N_DEV - len(GAINS), d), F32)], axis=0)
    v_all = jnp.concatenate([given["v_" + n].reshape(1, d) for n in GAINS] + [jnp.ones((N_DEV - len(GAINS), d), F32)], axis=0)
    d_all, mo_all, vo_all = _adam_small(g_all, w_all, m_all, v_all)
    for i, n in enumerate(GAINS):
        grad[n] = g_all[i].reshape(given[n].shape)
        delta[n] = d_all[i].reshape(given[n].shape)
        new_m[n] = mo_all[i].reshape(given[n].shape)
        new_v[n] = vo_all[i].reshape(given[n].shape)

    loss = lax.psum(loss_part[0, 0], ("x", "y", "c"))
    return (loss, grad_x.reshape(x.shape), *[grad[n] for n in ORDER], *[delta[n] for n in ORDER],
            *[new_m[n] for n in ORDER], *[new_v[n] for n in ORDER])
```

```python
import functools
import math

import jax
import jax.numpy as jnp
from jax import lax
from jax.experimental import pallas as pl
from jax.experimental.pallas import tpu as pltpu

F32 = jnp.float32
BF16 = jnp.bfloat16

N_DEV = 8
HEAD_DIM = 64
MEM_HEAD_DIM = 128
N_HEADS_MEM = 4
BLOCK = 128
DIL_PATTERNS = ((128, 1), (512, 4), (2048, 16))
ROPE_THETA = 500000.0
ROPE_HALF = 8
RMS_EPS = 1e-6
ADAM_LR, ADAM_B1, ADAM_B2, ADAM_EPS, ADAM_WD, ADAM_STEP = 0.001, 0.9, 0.999, 1e-08, 0.01, 10
NEG = -1e30
ROW_TILE = 512
LANES = 128

ANY = pl.BlockSpec(memory_space=pl.ANY)
VMEM = pl.BlockSpec(memory_space=pltpu.VMEM)
NN = (((1,), (0,)), ((), ()))
NT = (((1,), (1,)), ((), ()))
TN = (((0,), (0,)), ((), ()))


def _params(sem):
    return pltpu.CompilerParams(dimension_semantics=sem)


def _mm(name, a, b, *, grid, a_spec, b_spec, o_shape, o_spec, dims, out_dtype, nk=1, res=None, res_spec=None):
    has_res = res is not None

    def body(*refs):
        a_ref, b_ref = refs[0], refs[1]
        r_ref = refs[2] if has_res else None
        o_ref = refs[3] if has_res else refs[2]
        p = lax.dot_general(a_ref[...], b_ref[...], dims, preferred_element_type=F32)
        if nk == 1:
            if has_res:
                p = p + r_ref[...].astype(F32)
            o_ref[...] = p.astype(out_dtype)
            return
        acc_ref = refs[-1]
        k = pl.program_id(len(grid) - 1)

        @pl.when(k == 0)
        def _():
            acc_ref[...] = p

        @pl.when(k > 0)
        def _():
            acc_ref[...] += p

        @pl.when(k == nk - 1)
        def _():
            t = acc_ref[...]
            if has_res:
                t = t + r_ref[...].astype(F32)
            o_ref[...] = t.astype(out_dtype)

    o_block = tuple(d for d in o_spec.block_shape if d is not None)
    sem = ("parallel",) * (len(grid) - 1) + (("arbitrary",) if nk > 1 else ("parallel",))
    return pl.pallas_call(
        body, name=name, grid=grid,
        in_specs=[a_spec, b_spec] + ([res_spec] if has_res else []),
        out_specs=o_spec, out_shape=jax.ShapeDtypeStruct(o_shape, out_dtype),
        scratch_shapes=[pltpu.VMEM(o_block, F32)] if nk > 1 else [],
        compiler_params=_params(sem),
    )(*([a, b] + ([res] if has_res else [])))


def _rms_fwd(name, x, g):
    t, d = x.shape
    tm = min(ROW_TILE, t)

    def body(x_ref, g_ref, o_ref):
        xf = x_ref[...]
        r = lax.rsqrt(jnp.mean(xf * xf, axis=-1, keepdims=True) + RMS_EPS)
        o_ref[...] = (xf * r * g_ref[...]).astype(BF16)

    return pl.pallas_call(
        body, name=name, grid=(t // tm,),
        in_specs=[pl.BlockSpec((tm, d), lambda i: (i, 0)), pl.BlockSpec((1, d), lambda i: (0, 0))],
        out_specs=pl.BlockSpec((tm, d), lambda i: (i, 0)), out_shape=jax.ShapeDtypeStruct((t, d), BF16),
        compiler_params=_params(("parallel",)),
    )(x, g)


def _rms_bwd(name, dn, x, g, dres, want):
    t, d = x.shape
    tm = min(ROW_TILE, t)
    has_res = dres is not None

    def body(*refs):
        dn_ref, x_ref, g_ref = refs[0], refs[1], refs[2]
        r_ref = refs[3] if has_res else None
        dx_refs, dg_ref = refs[-1 - len(want):-1], refs[-1]
        xf = x_ref[...]
        r = lax.rsqrt(jnp.mean(xf * xf, axis=-1, keepdims=True) + RMS_EPS)
        xh = xf * r
        dnf = dn_ref[...].astype(F32)
        if want:
            dxh = dnf * g_ref[...]
            dx = r * (dxh - xh * jnp.mean(dxh * xh, axis=-1, keepdims=True))
            if has_res:
                dx = dx + r_ref[...]
            for kind, dx_ref in zip(want, dx_refs):
                dx_ref[...] = dx.astype(F32 if kind == "f32" else BF16)

        @pl.when(pl.program_id(0) == 0)
        def _():
            dg_ref[...] = jnp.zeros_like(dg_ref)

        dg_ref[...] += jnp.sum(dnf * xh, axis=0, keepdims=True)

    row = pl.BlockSpec((tm, d), lambda i: (i, 0))
    vec = pl.BlockSpec((1, d), lambda i: (0, 0))
    return pl.pallas_call(
        body, name=name, grid=(t // tm,),
        in_specs=[row, row, vec] + ([row] if has_res else []),
        out_specs=[row] * len(want) + [vec],
        out_shape=[jax.ShapeDtypeStruct((t, d), F32 if kind == "f32" else BF16) for kind in want]
        + [jax.ShapeDtypeStruct((1, d), F32)],
        compiler_params=_params(("arbitrary",)),
    )(*([dn, x, g] + ([dres] if has_res else [])))


def _loss_head(h, tgt, g):
    t, d = h.shape
    tm = min(ROW_TILE, t)

    def body(h_ref, t_ref, g_ref, loss_ref, dh_ref, dhb_ref, dg_ref):
        xf = h_ref[...]
        gv = g_ref[...]
        r = lax.rsqrt(jnp.mean(xf * xf, axis=-1, keepdims=True) + RMS_EPS)
        xh = xf * r
        e = xh * gv - t_ref[...]
        dy = e * (1.0 / d)
        dxh = dy * gv
        dh = r * (dxh - xh * jnp.mean(dxh * xh, axis=-1, keepdims=True))
        dh_ref[...] = dh
        dhb_ref[...] = dh.astype(BF16)

        @pl.when(pl.program_id(0) == 0)
        def _():
            dg_ref[...] = jnp.zeros_like(dg_ref)
            loss_ref[...] = jnp.zeros_like(loss_ref)

        dg_ref[...] += jnp.sum(dy * xh, axis=0, keepdims=True)
        part = jnp.sum(jnp.sum(e * e, axis=1, keepdims=True), axis=0, keepdims=True) * (0.5 / d)
        loss_ref[...] += jnp.broadcast_to(part, loss_ref.shape)

    row = pl.BlockSpec((tm, d), lambda i: (i, 0))
    vec = pl.BlockSpec((1, d), lambda i: (0, 0))
    return pl.pallas_call(
        body, name="loss_head", grid=(t // tm,),
        in_specs=[row, row, vec],
        out_specs=[pl.BlockSpec((8, LANES), lambda i: (0, 0)), row, row, vec],
        out_shape=[jax.ShapeDtypeStruct((8, LANES), F32), jax.ShapeDtypeStruct((t, d), F32),
                   jax.ShapeDtypeStruct((t, d), BF16), jax.ShapeDtypeStruct((1, d), F32)],
        compiler_params=_params(("arbitrary",)),
    )(h, tgt, g)


def _rope_tables(pos, inv_freq, sel_lo, sel_hi):
    t = pos.shape[0]
    tm = min(ROW_TILE, t)

    def body(p_ref, f_ref, lo_ref, hi_ref, c_ref, sa_ref, sb_ref):
        ang = p_ref[...].astype(F32) * f_ref[...]
        rot = lo_ref[...] + hi_ref[...]
        cs, sn = jnp.cos(ang), jnp.sin(ang)
        c_ref[...] = cs * rot + (1.0 - rot)
        sa_ref[...] = -sn * lo_ref[...]
        sb_ref[...] = sn * hi_ref[...]

    vec = pl.BlockSpec((1, LANES), lambda i: (0, 0))
    row = pl.BlockSpec((tm, LANES), lambda i: (i, 0))
    return pl.pallas_call(
        body, name="rope_tables", grid=(t // tm,),
        in_specs=[pl.BlockSpec((tm, 1), lambda i: (i, 0)), vec, vec, vec],
        out_specs=[row, row, row], out_shape=[jax.ShapeDtypeStruct((t, LANES), F32)] * 3,
        compiler_params=_params(("parallel",)),
    )(pos, inv_freq, sel_lo, sel_hi)


def _rope_apply(name, src, col0, n_cols, cos_t, sin_a, sin_b, sign):
    t = src.shape[0]
    tm = min(ROW_TILE, t)

    def body(x_ref, c_ref, sa_ref, sb_ref, o_ref):
        cs, sa, sb = c_ref[...], sign * sa_ref[...], sign * sb_ref[...]
        for c in range(n_cols):
            cols = slice(c * LANES, (c + 1) * LANES)
            xf = x_ref[:, cols].astype(F32)
            up = pltpu.roll(xf, LANES - ROPE_HALF, 1)
            dn = pltpu.roll(xf, ROPE_HALF, 1)
            o_ref[:, cols] = (xf * cs + up * sa + dn * sb).astype(BF16)

    wide = n_cols * LANES
    tab = pl.BlockSpec((tm, LANES), lambda i: (i, 0))
    return pl.pallas_call(
        body, name=name, grid=(t // tm,),
        in_specs=[pl.BlockSpec((tm, wide), lambda i: (i, col0 // n_cols)), tab, tab, tab],
        out_specs=pl.BlockSpec((tm, wide), lambda i: (i, 0)),
        out_shape=jax.ShapeDtypeStruct((t, wide), BF16),
        compiler_params=_params(("parallel",)),
    )(src, cos_t, sin_a, sin_b)


DA_T = 256
FWD_STREAMS = 4
BWD_STREAMS = 2


def _lane_lo():
    return lax.broadcasted_iota(jnp.int32, (BLOCK, LANES), 1) < HEAD_DIM


def _dilated_bias_tiles(s):
    n = s // DA_T
    dist = (jnp.arange(n, dtype=jnp.int32)[:, None, None] * DA_T
            + jnp.arange(DA_T, dtype=jnp.int32)[None, :, None] - jnp.arange(DA_T, dtype=jnp.int32)[None, None, :])
    cnt = jnp.zeros(dist.shape, F32)
    for window, dil in DIL_PATTERNS:
        cnt = cnt + ((dist >= 0) & (dist % dil == 0) & (dist <= window)).astype(F32)
    return jnp.where(cnt > 0, jnp.log(jnp.maximum(cnt, 1.0)), NEG)


def _stack_heads(x, lo):
    zero = jnp.zeros_like(x)
    return jnp.concatenate([jnp.where(lo, x, zero), jnp.where(lo, zero, x)], axis=0)


def _da_fwd(qk, proj, v_col0, bias, batch, s, ride=None, streams=FWD_STREAMS):
    t = qk.shape[0]
    nq = s // DA_T
    n_pairs = 4
    ns = streams
    wide = ns * LANES
    scale = HEAD_DIM ** -0.5

    def body(q_ref, k_ref, v_ref, b_ref, o_ref, lse_ref, acc_ref, m_ref, l_ref):
        i = pl.program_id(2)
        lo = lax.broadcasted_iota(jnp.int32, (DA_T, LANES), 1) < HEAD_DIM
        ones = jnp.ones((DA_T, LANES), BF16)
        acc_ref[...] = jnp.zeros_like(acc_ref)
        m_ref[...] = jnp.full(m_ref.shape, NEG, F32)
        l_ref[...] = jnp.zeros_like(l_ref)
        qqs = [_stack_heads(q_ref[:, st * LANES:(st + 1) * LANES] * scale, lo) for st in range(ns)]

        def scores(st, rows, bias2):
            k = k_ref[rows, st * LANES:(st + 1) * LANES]
            return lax.dot_general(qqs[st], k, NT, preferred_element_type=F32) + bias2

        def softmax(st, sc):
            m_old = m_ref[st]
            m_new = jnp.maximum(m_old, jnp.max(sc, axis=1, keepdims=True))
            m_ref[st] = m_new
            return jnp.exp(sc - m_new).astype(BF16), jnp.exp(m_old - m_new)

        def values(st, rows, p, alpha):
            v = v_ref[rows, st * LANES:(st + 1) * LANES]
            vz = jnp.zeros_like(v)
            l_ref[st] = alpha * l_ref[st] + lax.dot_general(p, ones, NN, preferred_element_type=F32)
            pv = (lax.dot_general(p[:DA_T], jnp.where(lo, v, vz), NN, preferred_element_type=F32)
                  + lax.dot_general(p[DA_T:], jnp.where(lo, vz, v), NN, preferred_element_type=F32))
            acc_ref[st] = acc_ref[st] * jnp.where(lo, alpha[:DA_T], alpha[DA_T:]) + pv

        def trip(dlt, carry):
            rows = pl.ds(pl.multiple_of((i - dlt) * DA_T, DA_T), DA_T)
            bias_t = b_ref[dlt]
            bias2 = jnp.concatenate([bias_t, bias_t], axis=0)
            scs = [scores(st, rows, bias2) for st in range(ns)]
            pas = [softmax(st, scs[st]) for st in range(ns)]
            for st in range(ns):
                values(st, rows, *pas[st])
            return carry

        lax.fori_loop(0, i + 1, trip, 0)
        for st in range(ns):
            cols = slice(st * LANES, (st + 1) * LANES)
            l_t = l_ref[st]
            o_ref[:, cols] = (acc_ref[st] / jnp.where(lo, l_t[:DA_T], l_t[DA_T:])).astype(BF16)
            lse = m_ref[st] + jnp.log(l_t)
            lse_ref[:, cols] = jnp.where(lo, lse[:DA_T], lse[DA_T:])

    blk = pl.BlockSpec((DA_T, wide), lambda b, h, i: (b * nq + i, h))
    return _call(
        body, name="attn_a_fwd", grid=(batch, n_pairs // ns, nq),
        in_specs=[blk,
                  pl.BlockSpec((s, wide), lambda b, h, i: (b, n_pairs // ns + h)),
                  pl.BlockSpec((s, wide), lambda b, h, i: (b, v_col0 // ns + h)),
                  pl.BlockSpec((nq, DA_T, DA_T), lambda b, h, i: (0, 0, 0))],
        out_specs=[blk, blk],
        out_shape=[jax.ShapeDtypeStruct((t, n_pairs * LANES), BF16), jax.ShapeDtypeStruct((t, n_pairs * LANES), F32)],
        scratch=[pltpu.VMEM((ns, DA_T, LANES), F32), pltpu.VMEM((ns, 2 * DA_T, 1), F32),
                 pltpu.VMEM((ns, 2 * DA_T, LANES), F32)],
        sem=("parallel", "parallel", "arbitrary"), args=(qk, qk, proj, bias), ride=ride)


def _da_bwd(qk, proj, v_col0, bias, o, lse, do, batch, s, ride=None, streams=BWD_STREAMS):
    t = qk.shape[0]
    nq = s // DA_T
    n_pairs = 4
    ns = streams
    wide = ns * LANES
    scale = HEAD_DIM ** -0.5

    def body(q_ref, k_ref, v_ref, b_ref, o_ref, lse_ref, do_ref, dq_ref, dk_ref, dv_ref, dk_acc, dv_acc, dq_acc):
        i = pl.program_id(2)
        lo = lax.broadcasted_iota(jnp.int32, (DA_T, LANES), 1) < HEAD_DIM

        @pl.when(i == 0)
        def _():
            dk_acc[...] = jnp.zeros_like(dk_acc)
            dv_acc[...] = jnp.zeros_like(dv_acc)

        dq_acc[...] = jnp.zeros_like(dq_acc)
        qqs, dds, deltas, lses = [], [], [], []
        for st in range(ns):
            cols = slice(st * LANES, (st + 1) * LANES)
            do_ = do_ref[:, cols]
            qqs.append(_stack_heads(q_ref[:, cols] * scale, lo))
            dds.append(_stack_heads(do_, lo))
            prod = do_.astype(F32) * o_ref[:, cols].astype(F32)
            fz = jnp.zeros_like(prod)
            deltas.append(jnp.concatenate([jnp.sum(jnp.where(lo, prod, fz), axis=1, keepdims=True),
                                           jnp.sum(jnp.where(lo, fz, prod), axis=1, keepdims=True)], axis=0))
            lse_t = lse_ref[:, cols]
            lses.append(jnp.concatenate([lse_t[:, 0:1], lse_t[:, HEAD_DIM:HEAD_DIM + 1]], axis=0))

        def products(st, rows, bias2):
            cols = slice(st * LANES, (st + 1) * LANES)
            sc = lax.dot_general(qqs[st], k_ref[rows, cols], NT, preferred_element_type=F32) + bias2
            return sc, lax.dot_general(dds[st], v_ref[rows, cols], NT, preferred_element_type=F32)

        def weights(st, sc, dp):
            p = jnp.exp(sc - lses[st])
            return (p * (dp - deltas[st])).astype(BF16), p.astype(BF16)

        def gradients(st, rows, ds, p):
            cols = slice(st * LANES, (st + 1) * LANES)
            k = k_ref[rows, cols]
            kz = jnp.zeros_like(k)
            dq_acc[st] += (lax.dot_general(ds[:DA_T], jnp.where(lo, k, kz), NN, preferred_element_type=F32)
                           + lax.dot_general(ds[DA_T:], jnp.where(lo, kz, k), NN, preferred_element_type=F32))
            dk_acc[rows, cols] += lax.dot_general(ds, qqs[st], TN, preferred_element_type=F32)
            dv_acc[rows, cols] += lax.dot_general(p, dds[st], TN, preferred_element_type=F32)

        def trip(dlt, carry):
            rows = pl.ds(pl.multiple_of((i - dlt) * DA_T, DA_T), DA_T)
            bias_t = b_ref[dlt]
            bias2 = jnp.concatenate([bias_t, bias_t], axis=0)
            prods = [products(st, rows, bias2) for st in range(ns)]
            wts = [weights(st, *prods[st]) for st in range(ns)]
            for st in range(ns):
                gradients(st, rows, *wts[st])
            return carry

        lax.fori_loop(0, i + 1, trip, 0)
        for st in range(ns):
            dq_ref[:, st * LANES:(st + 1) * LANES] = (dq_acc[st] * scale).astype(BF16)

        @pl.when(i == nq - 1)
        def _():
            dk_ref[...] = dk_acc[...].astype(BF16)
            dv_ref[...] = dv_acc[...].astype(BF16)

    blk = pl.BlockSpec((DA_T, wide), lambda b, h, i: (b * nq + i, h))
    seq = pl.BlockSpec((s, wide), lambda b, h, i: (b, h))
    out = jax.ShapeDtypeStruct((t, n_pairs * LANES), BF16)
    return _call(
        body, name="attn_a_bwd", grid=(batch, n_pairs // ns, nq),
        in_specs=[blk,
                  pl.BlockSpec((s, wide), lambda b, h, i: (b, n_pairs // ns + h)),
                  pl.BlockSpec((s, wide), lambda b, h, i: (b, v_col0 // ns + h)),
                  pl.BlockSpec((nq, DA_T, DA_T), lambda b, h, i: (0, 0, 0)),
                  blk, blk, blk],
        out_specs=[blk, seq, seq], out_shape=[out, out, out],
        scratch=[pltpu.VMEM((s, wide), F32), pltpu.VMEM((s, wide), F32), pltpu.VMEM((ns, DA_T, LANES), F32)],
        sem=("parallel", "parallel", "arbitrary"), args=(qk, qk, proj, bias, o, lse, do), ride=ride)


SB_Q = 256


def _sb_consts(after):
    r = lax.broadcasted_iota(jnp.int32, (2 * BLOCK, 2 * BLOCK), 0) % BLOCK
    c = lax.broadcasted_iota(jnp.int32, (2 * BLOCK, 2 * BLOCK), 1)
    tri = (r > c) if after else (r < c)
    return jnp.logical_or(c >= BLOCK, tri).astype(BF16)


def _split(x):
    hi = x.astype(BF16)
    lo = (x - hi.astype(F32)).astype(BF16)
    return jnp.concatenate([hi, lo], axis=1)


def _sb_fwd(proj, q_col0, k_col0, v_col0, batch, s, ride=None, streams=FWD_STREAMS):
    t = proj.shape[0]
    nq = s // SB_Q
    n_pairs = 4
    ns = streams
    wide = ns * LANES
    scale = HEAD_DIM ** -0.5

    def body(q_ref, k_ref, v_ref, o_ref, tot_ref, acc_ref, run_ref):
        i = pl.program_id(2)
        lo_q = lax.broadcasted_iota(jnp.int32, (SB_Q, LANES), 1) < HEAD_DIM
        lo_k = _lane_lo()
        mat = _sb_consts(True)
        row = lax.broadcasted_iota(jnp.int32, (2 * SB_Q, LANES), 0) % SB_Q
        ahead = row - lax.broadcasted_iota(jnp.int32, (2 * SB_Q, LANES), 1)
        acc_ref[...] = jnp.zeros_like(acc_ref)
        run_ref[...] = jnp.zeros_like(run_ref)
        qqs = [_stack_heads(q_ref[:, st * LANES:(st + 1) * LANES] * scale, lo_q) for st in range(ns)]

        def units(todo):
            def rows(j):
                return pl.ds(pl.multiple_of(j * BLOCK, BLOCK), BLOCK)

            zs = [lax.dot_general(qqs[st], k_ref[rows(j), st * LANES:(st + 1) * LANES], NT, preferred_element_type=F32)
                  for st, j, _ in todo]
            logs = []
            for z, (_, _, off) in zip(zs, todo):
                lsig = jnp.minimum(z, 0.0) - jnp.log(1.0 + jnp.exp(-jnp.abs(z)))
                lneg = lsig - z
                if off is not None:
                    lneg = jnp.where(ahead > off, lneg, 0.0)
                logs.append((lsig, _split(lneg)))
            sums = [lax.dot_general(cat, mat, NN, preferred_element_type=F32) for _, cat in logs]
            probs = []
            for (lsig, _), sm, (st, _, off) in zip(logs, sums, todo):
                run = run_ref[st]
                a = jnp.exp(lsig + run + sm[:, :BLOCK])
                if off is not None:
                    a = jnp.where(ahead > off, a, 0.0)
                run_ref[st] = run + sm[:, BLOCK:]
                probs.append(a.astype(BF16))
            for ab, (st, j, _) in zip(probs, todo):
                v = v_ref[rows(j), st * LANES:(st + 1) * LANES]
                vz = jnp.zeros_like(v)
                acc_ref[st] += (lax.dot_general(ab[:SB_Q], jnp.where(lo_k, v, vz), NN, preferred_element_type=F32)
                                + lax.dot_general(ab[SB_Q:], jnp.where(lo_k, vz, v), NN, preferred_element_type=F32))

        units([(st, 2 * i + 1, BLOCK) for st in range(ns)] + [(st, 2 * i, 0) for st in range(ns)])

        def pair(p, carry):
            jp = i - 1 - p
            units([(st, 2 * jp + 1, None) for st in range(ns)] + [(st, 2 * jp, None) for st in range(ns)])
            return carry

        lax.fori_loop(0, i, pair, 0)
        for st in range(ns):
            cols = slice(st * LANES, (st + 1) * LANES)
            o_ref[:, cols] = acc_ref[st].astype(BF16)
            tot_ref[:, cols] = jnp.where(lo_q, run_ref[st, 0:SB_Q, :], run_ref[st, SB_Q:2 * SB_Q, :])

    def seq(col0):
        return pl.BlockSpec((s, wide), lambda b, h, i: (b, col0 // ns + h))

    blk = pl.BlockSpec((SB_Q, wide), lambda b, h, i: (b * nq + i, h))
    return _call(
        body, name="attn_b_fwd", grid=(batch, n_pairs // ns, nq),
        in_specs=[pl.BlockSpec((SB_Q, wide), lambda b, h, i: (b * nq + i, q_col0 // ns + h)), seq(k_col0), seq(v_col0)],
        out_specs=[blk, blk],
        out_shape=[jax.ShapeDtypeStruct((t, n_pairs * LANES), BF16), jax.ShapeDtypeStruct((t, n_pairs * LANES), F32)],
        scratch=[pltpu.VMEM((ns, SB_Q, LANES), F32), pltpu.VMEM((ns, 2 * SB_Q, LANES), F32)],
        sem=("parallel", "parallel", "arbitrary"), args=(proj, proj, proj), ride=ride)


def _sb_bwd(proj, q_col0, k_col0, v_col0, tot, do, batch, s, ride=None, streams=BWD_STREAMS):
    t = proj.shape[0]
    nq = s // SB_Q
    n_pairs = 4
    ns = streams
    wide = ns * LANES
    scale = HEAD_DIM ** -0.5

    def body(q_ref, k_ref, v_ref, tot_ref, do_ref, dq_ref, dk_ref, dv_ref, dk_acc, dv_acc, dq_acc, seen_ref, gsum_ref):
        i = pl.program_id(2)
        lo_q = lax.broadcasted_iota(jnp.int32, (SB_Q, LANES), 1) < HEAD_DIM
        lo_k = _lane_lo()

        @pl.when(i == 0)
        def _():
            dk_acc[...] = jnp.zeros_like(dk_acc)
            dv_acc[...] = jnp.zeros_like(dv_acc)

        mat_after = _sb_consts(True)
        mat_before = _sb_consts(False)
        row = lax.broadcasted_iota(jnp.int32, (2 * SB_Q, LANES), 0) % SB_Q
        ahead = row - lax.broadcasted_iota(jnp.int32, (2 * SB_Q, LANES), 1)
        dq_acc[...] = jnp.zeros_like(dq_acc)
        seen_ref[...] = jnp.zeros_like(seen_ref)
        gsum_ref[...] = jnp.zeros_like(gsum_ref)
        qqs, dds, totals = [], [], []
        for st in range(ns):
            cols = slice(st * LANES, (st + 1) * LANES)
            qqs.append(_stack_heads(q_ref[:, cols] * scale, lo_q))
            dds.append(_stack_heads(do_ref[:, cols], lo_q))
            tot_t = tot_ref[:, cols]
            totals.append(jnp.concatenate([jnp.broadcast_to(tot_t[:, 0:1], (SB_Q, LANES)),
                                           jnp.broadcast_to(tot_t[:, HEAD_DIM:HEAD_DIM + 1], (SB_Q, LANES))], axis=0))

        def units(todo):
            def rows(j):
                return pl.ds(pl.multiple_of(j * BLOCK, BLOCK), BLOCK)

            def cols(st):
                return slice(st * LANES, (st + 1) * LANES)

            prods = [(lax.dot_general(qqs[st], k_ref[rows(j), cols(st)], NT, preferred_element_type=F32),
                      lax.dot_general(dds[st], v_ref[rows(j), cols(st)], NT, preferred_element_type=F32))
                     for st, j, _ in todo]
            logs = []
            for (z, _), (_, _, off) in zip(prods, todo):
                lsig = jnp.minimum(z, 0.0) - jnp.log(1.0 + jnp.exp(-jnp.abs(z)))
                lneg = lsig - z
                if off is not None:
                    lneg = jnp.where(ahead > off, lneg, 0.0)
                logs.append((lsig, _split(lneg)))
            sums = [lax.dot_general(cat, mat_after, NN, preferred_element_type=F32) for _, cat in logs]
            gates = []
            for (lsig, _), sm, (_, da), (st, _, off) in zip(logs, sums, prods, todo):
                seen = seen_ref[st]
                a = jnp.exp(lsig + (totals[st] - seen - sm[:, BLOCK:]) + sm[:, :BLOCK])
                if off is not None:
                    a = jnp.where(ahead > off, a, 0.0)
                seen_ref[st] = seen + sm[:, BLOCK:]
                g = a * da
                gates.append((a.astype(BF16), g, _split(g)))
            gsums = [lax.dot_general(cat, mat_before, NN, preferred_element_type=F32) for _, _, cat in gates]
            outs = []
            for (lsig, _), (ab, g, _), gs, (st, _, off) in zip(logs, gates, gsums, todo):
                gsum = gsum_ref[st]
                dz = g - jnp.exp(lsig) * (g + gsum + gs[:, :BLOCK])
                if off is not None:
                    dz = jnp.where(ahead > off, dz, 0.0)
                gsum_ref[st] = gsum + gs[:, BLOCK:]
                outs.append((dz.astype(BF16), ab))
            for (dzb, ab), (st, j, _) in zip(outs, todo):
                k = k_ref[rows(j), cols(st)]
                kz = jnp.zeros_like(k)
                dq_acc[st] += (lax.dot_general(dzb[:SB_Q], jnp.where(lo_k, k, kz), NN, preferred_element_type=F32)
                               + lax.dot_general(dzb[SB_Q:], jnp.where(lo_k, kz, k), NN, preferred_element_type=F32))
                dk_acc[rows(j), cols(st)] += lax.dot_general(dzb, qqs[st], TN, preferred_element_type=F32)
                dv_acc[rows(j), cols(st)] += lax.dot_general(ab, dds[st], TN, preferred_element_type=F32)

        def pair(p, carry):
            units([(st, 2 * p, None) for st in range(ns)] + [(st, 2 * p + 1, None) for st in range(ns)])
            return carry

        lax.fori_loop(0, i, pair, 0)
        units([(st, 2 * i, 0) for st in range(ns)] + [(st, 2 * i + 1, BLOCK) for st in range(ns)])
        for st in range(ns):
            dq_ref[:, st * LANES:(st + 1) * LANES] = (dq_acc[st] * scale).astype(BF16)

        @pl.when(i == nq - 1)
        def _():
            dk_ref[...] = dk_acc[...].astype(BF16)
            dv_ref[...] = dv_acc[...].astype(BF16)

    def seq_in(col0):
        return pl.BlockSpec((s, wide), lambda b, h, i: (b, col0 // ns + h))

    blk = pl.BlockSpec((SB_Q, wide), lambda b, h, i: (b * nq + i, h))
    seq = pl.BlockSpec((s, wide), lambda b, h, i: (b, h))
    out = jax.ShapeDtypeStruct((t, n_pairs * LANES), BF16)
    return _call(
        body, name="attn_b_bwd", grid=(batch, n_pairs // ns, nq),
        in_specs=[pl.BlockSpec((SB_Q, wide), lambda b, h, i: (b * nq + i, q_col0 // ns + h)), seq_in(k_col0),
                  seq_in(v_col0), blk, blk],
        out_specs=[blk, seq, seq], out_shape=[out, out, out],
        scratch=[pltpu.VMEM((s, wide), F32), pltpu.VMEM((s, wide), F32), pltpu.VMEM((ns, SB_Q, LANES), F32),
                 pltpu.VMEM((ns, 2 * SB_Q, LANES), F32), pltpu.VMEM((ns, 2 * SB_Q, LANES), F32)],
        sem=("parallel", "parallel", "arbitrary"), args=(proj, proj, proj, tot, do), ride=ride)


MEM_Q_TILE = 256


def _mem_fwd(q, kv, batch, s, n_mem):
    t, width = q.shape
    tq = min(MEM_Q_TILE, s)
    nq = s // tq
    scale = MEM_HEAD_DIM ** -0.5

    def body(q_ref, kv_ref, o_ref):
        for h in range(N_HEADS_MEM):
            cols = slice(h * MEM_HEAD_DIM, (h + 1) * MEM_HEAD_DIM)
            k = kv_ref[:, cols]
            v = kv_ref[:, width + h * MEM_HEAD_DIM: width + (h + 1) * MEM_HEAD_DIM]
            sc = lax.dot_general(q_ref[:, cols], k, NT, preferred_element_type=F32) * scale
            p = jnp.exp(sc - jnp.max(sc, axis=1, keepdims=True))
            p = p / jnp.sum(p, axis=1, keepdims=True)
            o_ref[:, cols] = lax.dot_general(p.astype(BF16), v, NN, preferred_element_type=F32).astype(BF16)

    return pl.pallas_call(
        body, name="mem_attn_fwd", grid=(batch, nq),
        in_specs=[pl.BlockSpec((tq, width), lambda b, i: (b * nq + i, 0)),
                  pl.BlockSpec((n_mem, 2 * width), lambda b, i: (b, 0))],
        out_specs=pl.BlockSpec((tq, width), lambda b, i: (b * nq + i, 0)),
        out_shape=jax.ShapeDtypeStruct((t, width), BF16),
        compiler_params=_params(("parallel", "parallel")),
    )(q, kv)


def _mem_bwd(q, kv, do, batch, s, n_mem):
    t, width = q.shape
    tq = min(MEM_Q_TILE, s)
    nq = s // tq
    scale = MEM_HEAD_DIM ** -0.5

    def body(q_ref, kv_ref, do_ref, dq_ref, dkv_ref, acc):
        i = pl.program_id(1)

        @pl.when(i == 0)
        def _():
            acc[...] = jnp.zeros_like(acc)

        for h in range(N_HEADS_MEM):
            cols = slice(h * MEM_HEAD_DIM, (h + 1) * MEM_HEAD_DIM)
            vcols = slice(width + h * MEM_HEAD_DIM, width + (h + 1) * MEM_HEAD_DIM)
            qh, k, v, doh = q_ref[:, cols], kv_ref[:, cols], kv_ref[:, vcols], do_ref[:, cols]
            sc = lax.dot_general(qh, k, NT, preferred_element_type=F32) * scale
            p = jnp.exp(sc - jnp.max(sc, axis=1, keepdims=True))
            p = p / jnp.sum(p, axis=1, keepdims=True)
            dp = lax.dot_general(doh, v, NT, preferred_element_type=F32)
            ds = (p * (dp - jnp.sum(p * dp, axis=1, keepdims=True)) * scale).astype(BF16)
            dq_ref[:, cols] = lax.dot_general(ds, k, NN, preferred_element_type=F32).astype(BF16)
            acc[:, cols] += lax.dot_general(ds, qh, TN, preferred_element_type=F32)
            acc[:, vcols] += lax.dot_general(p.astype(BF16), doh, TN, preferred_element_type=F32)

        @pl.when(i == nq - 1)
        def _():
            dkv_ref[...] = acc[...].astype(BF16)

    row = pl.BlockSpec((tq, width), lambda b, i: (b * nq + i, 0))
    kvs = pl.BlockSpec((n_mem, 2 * width), lambda b, i: (b, 0))
    return pl.pallas_call(
        body, name="mem_attn_bwd", grid=(batch, nq),
        in_specs=[row, kvs, row], out_specs=[row, kvs],
        out_shape=[jax.ShapeDtypeStruct((t, width), BF16), jax.ShapeDtypeStruct((batch * n_mem, 2 * width), BF16)],
        scratch_shapes=[pltpu.VMEM((n_mem, 2 * width), F32)],
        compiler_params=_params(("parallel", "arbitrary")),
    )(q, kv, do)


def _mixer_fwd(o_a, o_b, w_a, w_b, proj, gate_col0):
    t, width = o_a.shape
    d = w_a.shape[1]
    tm = min(ROW_TILE, t)
    gb0 = gate_col0 * LANES // d

    def body(oa_ref, ob_ref, wa_ref, wb_ref, ga_ref, gb_ref, ua_ref, ub_ref, mix_ref):
        ua = lax.dot_general(oa_ref[...], wa_ref[...], NN, preferred_element_type=F32)
        ub = lax.dot_general(ob_ref[...], wb_ref[...], NN, preferred_element_type=F32)
        ua_ref[...] = ua.astype(BF16)
        ub_ref[...] = ub.astype(BF16)
        mix_ref[...] = (jax.nn.sigmoid(ga_ref[...].astype(F32)) * ua
                        + jax.nn.sigmoid(gb_ref[...].astype(F32)) * ub).astype(BF16)

    row = pl.BlockSpec((tm, width), lambda i: (i, 0))
    wsp = pl.BlockSpec((width, d), lambda i: (0, 0))
    out = pl.BlockSpec((tm, d), lambda i: (i, 0))
    osh = jax.ShapeDtypeStruct((t, d), BF16)
    return pl.pallas_call(
        body, name="mixer_fwd", grid=(t // tm,),
        in_specs=[row, row, wsp, wsp,
                  pl.BlockSpec((tm, d), lambda i: (i, gb0)), pl.BlockSpec((tm, d), lambda i: (i, gb0 + 1))],
        out_specs=[out, out, out], out_shape=[osh, osh, osh],
        compiler_params=_params(("parallel",)),
    )(o_a, o_b, w_a, w_b, proj, proj)


def _mixer_bwd(dmix, ua, ub, proj, gate_col0):
    t, d = dmix.shape
    tm = min(ROW_TILE, t)
    nc = d // LANES

    def body(dm_ref, ua_ref, ub_ref, ga_ref, gb_ref, dua_ref, dub_ref, dg_ref):
        dm = dm_ref[...].astype(F32)
        sa = jax.nn.sigmoid(ga_ref[...].astype(F32))
        sb = jax.nn.sigmoid(gb_ref[...].astype(F32))
        dua_ref[...] = (dm * sa).astype(BF16)
        dub_ref[...] = (dm * sb).astype(BF16)
        dg_ref[:, 0:d] = (dm * ua_ref[...].astype(F32) * sa * (1.0 - sa)).astype(BF16)
        dg_ref[:, d:2 * d] = (dm * ub_ref[...].astype(F32) * sb * (1.0 - sb)).astype(BF16)

    row = pl.BlockSpec((tm, d), lambda i: (i, 0))
    return pl.pallas_call(
        body, name="mixer_bwd", grid=(t // tm,),
        in_specs=[row, row, row,
                  pl.BlockSpec((tm, d), lambda i: (i, gate_col0 // nc)),
                  pl.BlockSpec((tm, d), lambda i: (i, gate_col0 // nc + 1))],
        out_specs=[row, row, pl.BlockSpec((tm, 2 * d), lambda i: (i, 0))],
        out_shape=[jax.ShapeDtypeStruct((t, d), BF16), jax.ShapeDtypeStruct((t, d), BF16),
                   jax.ShapeDtypeStruct((t, 2 * d), BF16)],
        compiler_params=_params(("parallel",)),
    )(dmix, ua, ub, proj, proj)


FFN_COLS = 1024


def _ffn_up(n, w_gate, w_up):
    t, d = n.shape
    hidden = w_gate.shape[1]
    tm = min(ROW_TILE, t)
    tn = min(FFN_COLS, hidden)

    def body(n_ref, wg_ref, wu_ref, hg_ref, hu_ref, act_ref):
        hg = lax.dot_general(n_ref[...], wg_ref[...], NN, preferred_element_type=F32)
        hu = lax.dot_general(n_ref[...], wu_ref[...], NN, preferred_element_type=F32)
        hg_ref[...] = hg.astype(BF16)
        hu_ref[...] = hu.astype(BF16)
        act_ref[...] = (hg * jax.nn.sigmoid(hg) * hu).astype(BF16)

    wsp = pl.BlockSpec((d, tn), lambda j, i: (0, j))
    out = pl.BlockSpec((tm, tn), lambda j, i: (i, j))
    osh = jax.ShapeDtypeStruct((t, hidden), BF16)
    return pl.pallas_call(
        body, name="ffn_up", grid=(hidden // tn, t // tm),
        in_specs=[pl.BlockSpec((tm, d), lambda j, i: (i, 0)), wsp, wsp],
        out_specs=[out, out, out], out_shape=[osh, osh, osh],
        compiler_params=_params(("parallel", "parallel")),
    )(n, w_gate, w_up)


def _ffn_bwd_act(dh, w_down, hg, hu):
    t, d = dh.shape
    hidden = w_down.shape[0]
    tm = min(ROW_TILE, t)
    tn = min(FFN_COLS, hidden)

    def body(dh_ref, wd_ref, hg_ref, hu_ref, dhg_ref, dhu_ref):
        dact = lax.dot_general(dh_ref[...], wd_ref[...], NT, preferred_element_type=F32)
        hg = hg_ref[...].astype(F32)
        sg = jax.nn.sigmoid(hg)
        dhu_ref[...] = (dact * hg * sg).astype(BF16)
        dhg_ref[...] = (dact * hu_ref[...].astype(F32) * sg * (1.0 + hg * (1.0 - sg))).astype(BF16)

    hid = pl.BlockSpec((tm, tn), lambda j, i: (i, j))
    osh = jax.ShapeDtypeStruct((t, hidden), BF16)
    return pl.pallas_call(
        body, name="ffn_bwd_act", grid=(hidden // tn, t // tm),
        in_specs=[pl.BlockSpec((tm, d), lambda j, i: (i, 0)), pl.BlockSpec((tn, d), lambda j, i: (j, 0)), hid, hid],
        out_specs=[hid, hid], out_shape=[osh, osh],
        compiler_params=_params(("parallel", "parallel")),
    )(dh, w_down, hg, hu)


MM_ROWS = 1024


def _mm_cols_t(name, a, w, out_dtype, res=None):
    t = a.shape[0]
    n_sh, k, cs = w.shape
    tm = min(MM_ROWS, t)
    o_spec = pl.BlockSpec((tm, k), lambda i, j: (i, 0))
    return _mm(name, a, w, grid=(t // tm, n_sh),
               a_spec=pl.BlockSpec((tm, cs), lambda i, j: (i, j)), b_spec=pl.BlockSpec((None, k, cs), lambda i, j: (j, 0, 0)),
               o_shape=(t, k), o_spec=o_spec, dims=NT, out_dtype=out_dtype, nk=n_sh, res=res,
               res_spec=o_spec if res is not None else None)


def _mm_w(name, a, w, out_dtype, dims=NN, res=None, tm=MM_ROWS, tn=1024):
    t, k = a.shape
    n = w.shape[1] if dims == NN else w.shape[0]
    tm, tn = min(tm, t), min(tn, n)
    o_spec = pl.BlockSpec((tm, tn), lambda j, i: (i, j))
    b_spec = pl.BlockSpec((k, tn), lambda j, i: (0, j)) if dims == NN else pl.BlockSpec((tn, k), lambda j, i: (j, 0))
    return _mm(name, a, w, grid=(n // tn, t // tm), a_spec=pl.BlockSpec((tm, k), lambda j, i: (i, 0)), b_spec=b_spec,
               o_shape=(t, n), o_spec=o_spec, dims=dims, out_dtype=out_dtype, res=res,
               res_spec=o_spec if res is not None else None)


def _wgrad(name, a, g, tk=1024, tn=1024):
    t, k = a.shape
    n = g.shape[1]
    tm, tk, tn = min(MM_ROWS, t), min(tk, k), min(tn, n)
    return _mm(name, a, g, grid=(k // tk, n // tn, t // tm),
               a_spec=pl.BlockSpec((tm, tk), lambda p, q, r: (r, p)), b_spec=pl.BlockSpec((tm, tn), lambda p, q, r: (r, q)),
               o_shape=(k, n), o_spec=pl.BlockSpec((tk, tn), lambda p, q, r: (p, q)), dims=TN, out_dtype=BF16, nk=t // tm)


def _peers():
    x, y, c = lax.axis_index("x"), lax.axis_index("y"), lax.axis_index("c")
    me = 4 * x + 2 * y + c
    out = []
    for k in range(1, N_DEV):
        kx, ky, kc = (k >> 2) & 1, (k >> 1) & 1, k & 1
        px = 1 - x if kx else x
        py = 1 - y if ky else y
        pc = 1 - c if kc else c
        out.append(((px, py, pc), 4 * px + 2 * py + pc))
    return me, out


def _cast_weights(ws, pads):
    def body(*refs):
        n = len(refs) // 2
        for i_ref, o_ref, (pr, pc) in zip(refs[:n], refs[n:], pads):
            r, c = i_ref.shape
            o_ref[0:r, 0:c] = i_ref[...].astype(BF16)
            if pr:
                o_ref[r:r + pr, :] = jnp.zeros((pr, c), BF16)
            if pc:
                o_ref[:, c:c + pc] = jnp.zeros((r, pc), BF16)

    return pl.pallas_call(
        body, name="cast_weights", in_specs=[VMEM] * len(ws), out_specs=[VMEM] * len(ws),
        out_shape=[jax.ShapeDtypeStruct((w.shape[0] + pr, w.shape[1] + pc), BF16) for w, (pr, pc) in zip(ws, pads)],
    )(*ws)


def _exchange_copies(ins, outs, sems, gather, cols, landed):
    send_sems, recv_sems, loc_sems = sems
    n_peer = N_DEV - 1
    me, peers = _peers()

    def win(ref, j, c):
        return ref.at[:, pl.ds(pl.multiple_of(j * c, LANES), c)]

    def src(w, j):
        if gather:
            return ins[w]
        return win(ins[w], j, cols[w]) if cols[w] else ins[w].at[j]

    def dst(w, j):
        return win(outs[w], j, cols[w]) if gather and cols[w] else outs[w].at[j]

    local = [pltpu.make_async_copy(src(w, me), dst(w, me), loc_sems.at[w]) for w in range(len(ins))]
    remote = [pltpu.make_async_remote_copy(
        src_ref=src(w, idx), dst_ref=dst(w, idx if landed else me),
        send_sem=send_sems.at[w * n_peer + k], recv_sem=recv_sems.at[w * n_peer + k],
        device_id=dev, device_id_type=pl.DeviceIdType.MESH)
        for k, (dev, idx) in enumerate(peers) for w in range(len(ins))]
    return local, remote


def _exchange_start(ins, outs, sems, gather, cols):
    local, remote = _exchange_copies(ins, outs, sems, gather, cols, False)
    for cp in local + remote:
        cp.start()


def _exchange_wait(ins, outs, sems, gather, cols):
    local, remote = _exchange_copies(ins, outs, sems, gather, cols, True)
    for cp in local:
        cp.wait()
    for cp in remote:
        cp.wait_send()
        cp.wait_recv()


def _exchange_shapes(arrs, gather, cols):
    n = len(arrs)
    out_shape = []
    for a, c in zip(arrs, cols):
        if gather:
            shape = (a.shape[0], N_DEV * c) if c else (N_DEV,) + a.shape
        else:
            shape = (N_DEV, a.shape[0], c) if c else a.shape
        out_shape.append(jax.ShapeDtypeStruct(shape, a.dtype))
    sems = [pltpu.SemaphoreType.DMA((n * (N_DEV - 1),)), pltpu.SemaphoreType.DMA((n * (N_DEV - 1),)),
            pltpu.SemaphoreType.DMA((n,))]
    return out_shape, sems


def _call(body, *, name, grid, in_specs, out_specs, out_shape, scratch, sem, args, ride=None):
    if ride is None:
        outs = pl.pallas_call(body, name=name, grid=grid, in_specs=in_specs, out_specs=out_specs, out_shape=out_shape,
                              scratch_shapes=scratch, compiler_params=_params(sem))(*args)
        return outs, None
    arrs, gather, cols = ride
    n, n_in, n_out, n_scr = len(arrs), len(in_specs), len(out_specs), len(scratch)
    x_shape, x_sems = _exchange_shapes(arrs, gather, cols)

    def riding(*refs):
        ins, x_ins = refs[:n_in], refs[n_in:n_in + n]
        outs = refs[n_in + n:n_in + n + n_out]
        x_outs = refs[n_in + n + n_out:n_in + 2 * n + n_out]
        scr = refs[n_in + 2 * n + n_out:n_in + 2 * n + n_out + n_scr]
        sems = refs[n_in + 2 * n + n_out + n_scr:]
        first = functools.reduce(jnp.logical_and, [pl.program_id(a) == 0 for a in range(len(grid))])
        last = functools.reduce(jnp.logical_and, [pl.program_id(a) == g - 1 for a, g in enumerate(grid)])

        @pl.when(first)
        def _():
            _exchange_start(x_ins, x_outs, sems, gather, cols)

        body(*ins, *outs, *scr)

        @pl.when(last)
        def _():
            _exchange_wait(x_ins, x_outs, sems, gather, cols)

    res = pl.pallas_call(
        riding, name=name, grid=grid, in_specs=list(in_specs) + [ANY] * n, out_specs=list(out_specs) + [ANY] * n,
        out_shape=list(out_shape) + x_shape, scratch_shapes=list(scratch) + x_sems,
        compiler_params=_params(("arbitrary",) * len(grid)))(*args, *arrs)
    return res[:n_out], res[n_out:]


def _my_block():
    return (4 * lax.axis_index("x") + 2 * lax.axis_index("y") + lax.axis_index("c")).astype(jnp.int32).reshape(1)


def _proj_in_gather(n, w_shard):
    t, k = n.shape
    cs = w_shard.shape[1]
    tm = min(MM_ROWS, t)
    ni = t // tm

    def body(me_ref, n_ref, w_hbm, o_ref, all_hbm, w_vmem, send_sems, recv_sems, loc_sem, load_sem):
        s, i = pl.program_id(0), pl.program_id(1)
        me, peers = _peers()
        mine = (lax.axis_index("x"), lax.axis_index("y"), lax.axis_index("c"))
        local = pltpu.make_async_copy(w_hbm, all_hbm.at[me], loc_sem)
        sends = [pltpu.make_async_remote_copy(src_ref=w_hbm, dst_ref=all_hbm.at[me], send_sem=send_sems.at[kk],
                                              recv_sem=recv_sems.at[kk], device_id=dev,
                                              device_id_type=pl.DeviceIdType.MESH)
                 for kk, (dev, _) in enumerate(peers)]

        @pl.when(jnp.logical_and(s == 0, i == 0))
        def _():
            local.start()
            for cp in sends:
                cp.start()

        @pl.when(i == 0)
        def _():
            src = jnp.bitwise_xor(me, s)

            @pl.when(s == 0)
            def _():
                local.wait()

            @pl.when(s > 0)
            def _():
                pltpu.make_async_remote_copy(src_ref=w_hbm, dst_ref=all_hbm.at[src], send_sem=send_sems.at[s - 1],
                                             recv_sem=recv_sems.at[s - 1], device_id=mine,
                                             device_id_type=pl.DeviceIdType.MESH).wait_recv()

            load = pltpu.make_async_copy(all_hbm.at[src], w_vmem, load_sem)
            load.start()
            load.wait()

        o_ref[...] = lax.dot_general(n_ref[...], w_vmem[...], NN, preferred_element_type=F32).astype(BF16)

        @pl.when(jnp.logical_and(s == N_DEV - 1, i == ni - 1))
        def _():
            for cp in sends:
                cp.wait_send()

    return pl.pallas_call(
        body, name="proj_in",
        grid_spec=pltpu.PrefetchScalarGridSpec(
            num_scalar_prefetch=1, grid=(N_DEV, ni),
            in_specs=[pl.BlockSpec((tm, k), lambda s, i, me: (i, 0)), ANY],
            out_specs=[pl.BlockSpec((tm, cs), lambda s, i, me: (i, jnp.bitwise_xor(me[0], s))), ANY],
            scratch_shapes=[pltpu.VMEM((k, cs), BF16), pltpu.SemaphoreType.DMA((N_DEV - 1,)),
                            pltpu.SemaphoreType.DMA((N_DEV - 1,)), pltpu.SemaphoreType.DMA, pltpu.SemaphoreType.DMA]),
        out_shape=[jax.ShapeDtypeStruct((t, N_DEV * cs), BF16), jax.ShapeDtypeStruct((N_DEV, k, cs), BF16)],
        compiler_params=_params(("arbitrary", "arbitrary")),
    )(_my_block(), n, w_shard)


def _gw_in_scatter(a, g):
    t, k = a.shape
    cs = g.shape[1] // N_DEV
    tm = min(MM_ROWS, t)
    nr = t // tm

    def body(me_ref, a_ref, g_ref, out_hbm, acc, stage, send_sems, recv_sems, loc_sem):
        s, r = pl.program_id(0), pl.program_id(1)
        me, peers = _peers()
        part = lax.dot_general(a_ref[...], g_ref[...], TN, preferred_element_type=F32)

        def send(kk):
            dev, _ = peers[kk]
            return pltpu.make_async_remote_copy(src_ref=stage.at[kk % 2], dst_ref=out_hbm.at[me],
                                                send_sem=send_sems.at[kk], recv_sem=recv_sems.at[kk], device_id=dev,
                                                device_id_type=pl.DeviceIdType.MESH)

        local = pltpu.make_async_copy(stage.at[(N_DEV - 1) % 2], out_hbm.at[me], loc_sem)

        @pl.when(r == 0)
        def _():
            acc[...] = part

        @pl.when(r > 0)
        def _():
            acc[...] += part

        for kk in range(N_DEV):
            @pl.when(jnp.logical_and(s == kk, r == nr - 1))
            def _(kk=kk):
                if kk >= 2:
                    send(kk - 2).wait_send()
                stage[kk % 2] = acc[...].astype(BF16)
                if kk < N_DEV - 1:
                    send(kk).start()
                else:
                    local.start()
                    send(kk - 1).wait_send()
                    local.wait()
                    for j, (dev, idx) in enumerate(peers):
                        pltpu.make_async_remote_copy(src_ref=stage.at[j % 2], dst_ref=out_hbm.at[idx],
                                                     send_sem=send_sems.at[j], recv_sem=recv_sems.at[j], device_id=dev,
                                                     device_id_type=pl.DeviceIdType.MESH).wait_recv()

    return pl.pallas_call(
        body, name="gw_in",
        grid_spec=pltpu.PrefetchScalarGridSpec(
            num_scalar_prefetch=1, grid=(N_DEV, nr),
            in_specs=[pl.BlockSpec((tm, k), lambda s, r, me: (r, 0)),
                      pl.BlockSpec((tm, cs), lambda s, r, me: (r, jnp.bitwise_xor(me[0], (s + 1) % N_DEV)))],
            out_specs=ANY,
            scratch_shapes=[pltpu.VMEM((k, cs), F32), pltpu.VMEM((2, k, cs), BF16),
                            pltpu.SemaphoreType.DMA((N_DEV - 1,)), pltpu.SemaphoreType.DMA((N_DEV - 1,)),
                            pltpu.SemaphoreType.DMA]),
        out_shape=jax.ShapeDtypeStruct((N_DEV, k, cs), BF16),
        compiler_params=_params(("arbitrary", "arbitrary")),
    )(_my_block(), a, g)


def _allreduce_small(v):
    def body(v_ref, o_ref, all_ref, send_sems, recv_sems):
        me, peers = _peers()
        all_ref[me] = v_ref[...]
        for k, (dev, idx) in enumerate(peers):
            pltpu.make_async_remote_copy(src_ref=v_ref, dst_ref=all_ref.at[me], send_sem=send_sems.at[k],
                                         recv_sem=recv_sems.at[k], device_id=dev,
                                         device_id_type=pl.DeviceIdType.MESH).start()
        for k, (dev, idx) in enumerate(peers):
            cp = pltpu.make_async_remote_copy(src_ref=v_ref, dst_ref=all_ref.at[idx], send_sem=send_sems.at[k],
                                              recv_sem=recv_sems.at[k], device_id=dev,
                                              device_id_type=pl.DeviceIdType.MESH)
            cp.wait_send()
            cp.wait_recv()
        tot = all_ref[0]
        for dvc in range(1, N_DEV):
            tot = tot + all_ref[dvc]
        o_ref[...] = tot

    return pl.pallas_call(
        body, name="allreduce_small", in_specs=[VMEM], out_specs=VMEM,
        out_shape=jax.ShapeDtypeStruct(v.shape, F32),
        scratch_shapes=[pltpu.VMEM((N_DEV,) + v.shape, F32), pltpu.SemaphoreType.DMA((N_DEV - 1,)),
                        pltpu.SemaphoreType.DMA((N_DEV - 1,))],
    )(v)


def _adam_math(g, w, m, v):
    m_new = ADAM_B1 * m + (1.0 - ADAM_B1) * g
    v_new = ADAM_B2 * v + (1.0 - ADAM_B2) * (g * g)
    m_hat = m_new / (1.0 - ADAM_B1 ** ADAM_STEP)
    v_hat = v_new / (1.0 - ADAM_B2 ** ADAM_STEP)
    delta = -ADAM_LR * (m_hat / (jnp.sqrt(v_hat) + ADAM_EPS) + ADAM_WD * w)
    return delta, m_new, v_new


def _adam(name, pieces, w, m, v):
    r, c = w.shape
    cp = pieces.shape[2]
    tr = r
    for cand in (256, 176, 128, 64):
        if r % cand == 0 and r > cand:
            tr = cand
            break

    def body(p_ref, w_ref, m_ref, v_ref, g_ref, d_ref, mo_ref, vo_ref):
        g = p_ref[0, :, 0:c].astype(F32)
        for dvc in range(1, N_DEV):
            g = g + p_ref[dvc, :, 0:c].astype(F32)
        delta, m_new, v_new = _adam_math(g, w_ref[...], m_ref[...], v_ref[...])
        g_ref[...] = g
        d_ref[...] = delta
        mo_ref[...] = m_new
        vo_ref[...] = v_new

    blk = pl.BlockSpec((tr, c), lambda i: (i, 0))
    osh = jax.ShapeDtypeStruct((r, c), F32)
    return pl.pallas_call(
        body, name=name, grid=(r // tr,),
        in_specs=[pl.BlockSpec((N_DEV, tr, cp), lambda i: (0, i, 0)), blk, blk, blk],
        out_specs=[blk, blk, blk, blk], out_shape=[osh, osh, osh, osh],
        compiler_params=_params(("parallel",)),
    )(pieces, w, m, v)


def _adam_small(g, w, m, v):
    def body(g_ref, w_ref, m_ref, v_ref, d_ref, mo_ref, vo_ref):
        delta, m_new, v_new = _adam_math(g_ref[...], w_ref[...], m_ref[...], v_ref[...])
        d_ref[...] = delta
        mo_ref[...] = m_new
        vo_ref[...] = v_new

    osh = jax.ShapeDtypeStruct(g.shape, F32)
    return pl.pallas_call(body, name="adam_small", in_specs=[VMEM] * 4, out_specs=[VMEM] * 3,
                          out_shape=[osh, osh, osh])(g, w, m, v)


def _local_step(x, mem, pos, tgt, gains, w_in_shard, shards, batch):
    g_mix, g_mem_q, g_mem_kv, g_ffn, g_final = gains
    t, d = x.shape
    s = t // batch
    n_mem = mem.shape[0] // batch
    n_sh = N_DEV
    width = shards[0].shape[0]
    nb = width // LANES

    lane = jnp.arange(LANES, dtype=jnp.int32) % HEAD_DIM
    sel_lo = (lane < ROPE_HALF).astype(F32)[None, :]
    sel_hi = ((lane >= ROPE_HALF) & (lane < 2 * ROPE_HALF)).astype(F32)[None, :]
    freqs = ROPE_THETA ** (-jnp.arange(ROPE_HALF, dtype=F32) / ROPE_HALF)
    inv_freq = jnp.where(lane < 2 * ROPE_HALF, freqs[lane % ROPE_HALF], 0.0)[None, :]
    cos_t, sin_a, sin_b = _rope_tables(pos, inv_freq, sel_lo, sel_hi)
    bias = _dilated_bias_tiles(s)

    n1 = _rms_fwd("norm_mix", x, g_mix)
    proj, w_in = _proj_in_gather(n1, w_in_shard)
    qk_a = _rope_apply("rope_fwd", proj, 0, 2 * nb, cos_t, sin_a, sin_b, 1.0)
    cs_up, cs_ffn = shards[0].shape[1], shards[6].shape[1]
    (o_a, lse_a), (w_up_a, w_up_b, w_out, w_q, w_kv, w_o, w_fd) = _da_fwd(
        qk_a, proj, 2 * nb, bias, batch, s,
        ride=(shards[:6] + shards[8:], True, (cs_up, cs_up, 0, 0, 0, cs_up, 0)))
    (o_b, tot_b), (w_fg, w_fu) = _sb_fwd(proj, 3 * nb, 4 * nb, 5 * nb, batch, s,
                                         ride=(shards[6:8], True, (cs_ffn, cs_ffn)))
    w_out = w_out.reshape(d, d)
    w_q = w_q.reshape(d, -1)
    w_kv = w_kv.reshape(d, -1)
    w_fd = w_fd.reshape(-1, d)
    ua, ub, mixed = _mixer_fwd(o_a, o_b, w_up_a, w_up_b, proj, 6 * nb)
    h1 = _mm_w("mix_out", mixed, w_out, F32, res=x)
    n2 = _rms_fwd("norm_mem_q", h1, g_mem_q)
    mem_n = _rms_fwd("norm_mem_kv", mem, g_mem_kv)
    q_m = _mm_w("mem_q", n2, w_q, BF16)
    kv_m = _mm_w("mem_kv", mem_n, w_kv, BF16)
    o_m = _mem_fwd(q_m, kv_m, batch, s, n_mem)
    h2 = _mm_w("mem_out", o_m, w_o, F32, res=h1)
    n3 = _rms_fwd("norm_ffn", h2, g_ffn)
    hg, hu, act = _ffn_up(n3, w_fg, w_fu)
    h3 = _mm_w("ffn_down", act, w_fd, F32, res=h2, tm=ROW_TILE)
    loss_part, dh3, dh3_b, dg_final = _loss_head(h3, tgt, g_final.reshape(1, d))

    dhg, dhu = _ffn_bwd_act(dh3_b, w_fd, hg, hu)
    gw_fd = _wgrad("gw_ffn_down", act, dh3_b)
    gw_fg = _wgrad("gw_ffn_gate", n3, dhg)
    gw_fu = _wgrad("gw_ffn_up", n3, dhu)
    dn3 = _mm_w("dn_ffn_gate", dhg, w_fg, F32, dims=NT, tm=ROW_TILE)
    dn3 = _mm_w("dn_ffn_up", dhu, w_fu, F32, dims=NT, res=dn3, tm=ROW_TILE)
    dh2, dh2_b, dg_ffn = _rms_bwd("norm_ffn_bwd", dn3, h2, g_ffn, dh3, ("f32", "bf16"))

    do_m = _mm_w("mem_out_bwd", dh2_b, w_o, BF16, dims=NT)
    gw_o = _wgrad("gw_mem_o", o_m, dh2_b)
    dq_m, dkv_m = _mem_bwd(q_m, kv_m, do_m, batch, s, n_mem)
    gw_q = _wgrad("gw_mem_q", n2, dq_m)
    gw_kv = _wgrad("gw_mem_kv", mem_n, dkv_m)
    dn2 = _mm_w("mem_q_bwd", dq_m, w_q, F32, dims=NT)
    dmem_n = _mm_w("mem_kv_bwd", dkv_m, w_kv, F32, dims=NT)
    (dg_mem_kv,) = _rms_bwd("norm_mem_kv_bwd", dmem_n, mem, g_mem_kv, None, ())
    dh1, dh1_b, dg_mem_q = _rms_bwd("norm_mem_q_bwd", dn2, h1, g_mem_q, dh2, ("f32", "bf16"))

    dmix = _mm_w("mix_out_bwd", dh1_b, w_out, BF16, dims=NT)
    gw_out = _wgrad("gw_out", mixed, dh1_b)
    dua, dub, dgates = _mixer_bwd(dmix, ua, ub, proj, 6 * nb)
    do_a = _mm_w("up_a_bwd", dua, w_up_a, BF16, dims=NT)
    do_b = _mm_w("up_b_bwd", dub, w_up_b, BF16, dims=NT)
    gw_ua = _wgrad("gw_up_a", o_a, dua)
    gw_ub = _wgrad("gw_up_b", o_b, dub)
    (dq_ar, dk_ar, dv_a), (p_fg, p_fd) = _da_bwd(
        qk_a, proj, 2 * nb, bias, o_a, lse_a, do_a, batch, s,
        ride=([gw_fg, gw_fd.reshape(n_sh, -1, d)], False, (cs_ffn, 0)))
    dqk_a = _rope_apply("rope_bwd", jnp.concatenate([dq_ar, dk_ar], axis=1), 0, 2 * nb, cos_t, sin_a, sin_b, -1.0)
    mid = [gw_ua, gw_ub, gw_out.reshape(n_sh, -1, d), gw_q.reshape(n_sh, -1, gw_q.shape[1]),
           gw_kv.reshape(n_sh, -1, gw_kv.shape[1]), gw_o, gw_fu]
    (dq_b, dk_b, dv_b), (*p_mid, p_fu) = _sb_bwd(proj, 3 * nb, 4 * nb, 5 * nb, tot_b, do_b, batch, s,
                                                 ride=(mid, False, (cs_up, cs_up, 0, 0, 0, cs_up, cs_ffn)))
    p_ffn = [p_fg, p_fu, p_fd]
    dproj = jnp.concatenate([dqk_a, dv_a, dq_b, dk_b, dv_b, dgates], axis=1)
    dn1 = _mm_cols_t("proj_in_bwd", dproj, w_in, F32)
    p_in = _gw_in_scatter(n1, dproj)
    grad_x, dg_mix = _rms_bwd("norm_mix_bwd", dn1, x, g_mix, dh1, ("f32",))
    return loss_part, grad_x, [p_in] + list(p_mid) + p_ffn, (dg_mix, dg_mem_q, dg_mem_kv, dg_ffn, dg_final)


WEIGHTS =("w_in", "w_up_a", "w_up_b", "w_out", "w_q_mem", "w_kv_mem", "w_o_mem", "w_ffn_gate", "w_ffn_up", "w_ffn_down")
GAINS = ("g_mix", "g_mem_q", "g_mem_kv", "g_ffn", "g_final")
ORDER = ("g_mix", "w_in", "w_up_a", "w_up_b", "w_out", "g_mem_q", "g_mem_kv", "w_q_mem", "w_kv_mem", "w_o_mem", "g_ffn",
         "w_ffn_gate", "w_ffn_up", "w_ffn_down", "g_final")


def kernel(x, mem, positions, g_mix, w_in, w_up_a, w_up_b, w_out, g_mem_q, g_mem_kv, w_q_mem, w_kv_mem, w_o_mem, g_ffn, w_ffn_gate, w_ffn_up, w_ffn_down, g_final, loss_target, m_g_mix, m_w_in, m_w_up_a, m_w_up_b, m_w_out, m_g_mem_q, m_g_mem_kv, m_w_q_mem, m_w_kv_mem, m_w_o_mem, m_g_ffn, m_w_ffn_gate, m_w_ffn_up, m_w_ffn_down, m_g_final, v_g_mix, v_w_in, v_w_up_a, v_w_up_b, v_w_out, v_g_mem_q, v_g_mem_kv, v_w_q_mem, v_w_kv_mem, v_w_o_mem, v_g_ffn, v_w_ffn_gate, v_w_ffn_up, v_w_ffn_down, v_g_final):
    given = dict(locals())
    batch, s, d = x.shape
    t = batch * s
    shard = {n: given[n].reshape(given[n].shape[-2:]) for n in WEIGHTS}
    gains = [given[n].reshape(1, d) for n in GAINS]

    pad = (-shard["w_ffn_down"].shape[0]) % LANES
    pads = {"w_ffn_gate": (0, pad), "w_ffn_up": (0, pad), "w_ffn_down": (pad, 0)}
    cast = _cast_weights([shard[n] for n in WEIGHTS], [pads.get(n, (0, 0)) for n in WEIGHTS])
    loss_part, grad_x, pieces, dgains = _local_step(
        x.reshape(t, d), mem.reshape(-1, d), positions.reshape(t, 1), loss_target.reshape(t, d), gains, cast[0],
        cast[1:], batch)

    grad, delta, new_m, new_v = {}, {}, {}, {}
    for n, p in zip(WEIGHTS, pieces):
        m2, v2 = given["m_" + n].reshape(shard[n].shape), given["v_" + n].reshape(shard[n].shape)
        outs = _adam("adam_" + n, p, shard[n], m2, v2)
        grad[n], delta[n], new_m[n], new_v[n] = [o.reshape(given[n].shape) for o in outs]

    rows = jnp.concatenate(list(dgains) + [jnp.zeros((N_DEV - len(GAINS), d), F32)], axis=0)
    g_all = _allreduce_small(rows)
    w_all = jnp.concatenate(gains + [jnp.zeros((N_DEV - len(GAINS), d), F32)], axis=0)
    m_all = jnp.concatenate([given["m_" + n].reshape(1, d) for n in GAINS] + [jnp.zeros((N_DEV - len(GAINS), d), F32)], axis=0)
    v_all = jnp.concatenate([given["v_" + n].reshape(1, d) for n in GAINS] + [jnp.ones((N_DEV - len(GAINS), d), F32)], axis=0)
    d_all, mo_all, vo_all = _adam_small(g_all, w_all, m_all, v_all)
    for i, n in enumerate(GAINS):
        grad[n] = g_all[i].reshape(given[n].shape)
        delta[n] = d_all[i].reshape(given[n].shape)
        new_m[n] = mo_all[i].reshape(given[n].shape)
        new_v[n] = vo_all[i].reshape(given[n].shape)

    loss = lax.psum(loss_part[0, 0], ("x", "y", "c"))
    return (loss, grad_x.reshape(x.shape), *[grad[n] for n in ORDER], *[delta[n] for n in ORDER],
            *[new_m[n] for n in ORDER], *[new_v[n] for n in ORDER])
```

```python
import functools
import math

import jax
import jax.numpy as jnp
import numpy as np
from jax import lax
from jax.experimental import pallas as pl
from jax.experimental.pallas import tpu as pltpu

F32 = jnp.float32
BF16 = jnp.bfloat16

N_DEV = 8
HEAD_DIM = 64
MEM_HEAD_DIM = 128
N_HEADS_MEM = 4
BLOCK = 128
DIL_PATTERNS = ((128, 1), (512, 4), (2048, 16))
ROPE_THETA = 500000.0
ROPE_HALF = 8
RMS_EPS = 1e-6
ADAM_LR, ADAM_B1, ADAM_B2, ADAM_EPS, ADAM_WD, ADAM_STEP = 0.001, 0.9, 0.999, 1e-08, 0.01, 10
NEG = -1e30
ROW_TILE = 512
LANES = 128

ANY = pl.BlockSpec(memory_space=pl.ANY)
VMEM = pl.BlockSpec(memory_space=pltpu.VMEM)
NN = (((1,), (0,)), ((), ()))
NT = (((1,), (1,)), ((), ()))
TN = (((0,), (0,)), ((), ()))


def _params(sem):
    return pltpu.CompilerParams(dimension_semantics=sem)


def _mm(name, a, b, *, grid, a_spec, b_spec, o_shape, o_spec, dims, out_dtype, nk=1, res=None, res_spec=None):
    has_res = res is not None

    def body(*refs):
        a_ref, b_ref = refs[0], refs[1]
        r_ref = refs[2] if has_res else None
        o_ref = refs[3] if has_res else refs[2]
        p = lax.dot_general(a_ref[...], b_ref[...], dims, preferred_element_type=F32)
        if nk == 1:
            if has_res:
                p = p + r_ref[...].astype(F32)
            o_ref[...] = p.astype(out_dtype)
            return
        acc_ref = refs[-1]
        k = pl.program_id(len(grid) - 1)

        @pl.when(k == 0)
        def _():
            acc_ref[...] = p

        @pl.when(k > 0)
        def _():
            acc_ref[...] += p

        @pl.when(k == nk - 1)
        def _():
            t = acc_ref[...]
            if has_res:
                t = t + r_ref[...].astype(F32)
            o_ref[...] = t.astype(out_dtype)

    o_block = tuple(d for d in o_spec.block_shape if d is not None)
    sem = ("parallel",) * (len(grid) - 1) + (("arbitrary",) if nk > 1 else ("parallel",))
    return pl.pallas_call(
        body, name=name, grid=grid,
        in_specs=[a_spec, b_spec] + ([res_spec] if has_res else []),
        out_specs=o_spec, out_shape=jax.ShapeDtypeStruct(o_shape, out_dtype),
        scratch_shapes=[pltpu.VMEM(o_block, F32)] if nk > 1 else [],
        compiler_params=_params(sem),
    )(*([a, b] + ([res] if has_res else [])))


def _rms_fwd(name, x, g):
    t, d = x.shape
    tm = min(ROW_TILE, t)

    def body(x_ref, g_ref, o_ref):
        xf = x_ref[...]
        r = lax.rsqrt(jnp.mean(xf * xf, axis=-1, keepdims=True) + RMS_EPS)
        o_ref[...] = (xf * r * g_ref[...]).astype(BF16)

    return pl.pallas_call(
        body, name=name, grid=(t // tm,),
        in_specs=[pl.BlockSpec((tm, d), lambda i: (i, 0)), pl.BlockSpec((1, d), lambda i: (0, 0))],
        out_specs=pl.BlockSpec((tm, d), lambda i: (i, 0)), out_shape=jax.ShapeDtypeStruct((t, d), BF16),
        compiler_params=_params(("parallel",)),
    )(x, g)


def _rms_bwd(name, dn, x, g, dres, want):
    t, d = x.shape
    tm = min(ROW_TILE, t)
    has_res = dres is not None

    def body(*refs):
        dn_ref, x_ref, g_ref = refs[0], refs[1], refs[2]
        r_ref = refs[3] if has_res else None
        dx_refs, dg_ref = refs[-1 - len(want):-1], refs[-1]
        xf = x_ref[...]
        r = lax.rsqrt(jnp.mean(xf * xf, axis=-1, keepdims=True) + RMS_EPS)
        xh = xf * r
        dnf = dn_ref[...].astype(F32)
        if want:
            dxh = dnf * g_ref[...]
            dx = r * (dxh - xh * jnp.mean(dxh * xh, axis=-1, keepdims=True))
            if has_res:
                dx = dx + r_ref[...]
            for kind, dx_ref in zip(want, dx_refs):
                dx_ref[...] = dx.astype(F32 if kind == "f32" else BF16)

        @pl.when(pl.program_id(0) == 0)
        def _():
            dg_ref[...] = jnp.zeros_like(dg_ref)

        dg_ref[...] += jnp.sum(dnf * xh, axis=0, keepdims=True)

    row = pl.BlockSpec((tm, d), lambda i: (i, 0))
    vec = pl.BlockSpec((1, d), lambda i: (0, 0))
    return pl.pallas_call(
        body, name=name, grid=(t // tm,),
        in_specs=[row, row, vec] + ([row] if has_res else []),
        out_specs=[row] * len(want) + [vec],
        out_shape=[jax.ShapeDtypeStruct((t, d), F32 if kind == "f32" else BF16) for kind in want]
        + [jax.ShapeDtypeStruct((1, d), F32)],
        compiler_params=_params(("arbitrary",)),
    )(*([dn, x, g] + ([dres] if has_res else [])))


def _loss_head(h, tgt, g):
    t, d = h.shape
    tm = min(ROW_TILE, t)

    def body(h_ref, t_ref, g_ref, loss_ref, dh_ref, dhb_ref, dg_ref):
        xf = h_ref[...]
        gv = g_ref[...]
        r = lax.rsqrt(jnp.mean(xf * xf, axis=-1, keepdims=True) + RMS_EPS)
        xh = xf * r
        e = xh * gv - t_ref[...]
        dy = e * (1.0 / d)
        dxh = dy * gv
        dh = r * (dxh - xh * jnp.mean(dxh * xh, axis=-1, keepdims=True))
        dh_ref[...] = dh
        dhb_ref[...] = dh.astype(BF16)

        @pl.when(pl.program_id(0) == 0)
        def _():
            dg_ref[...] = jnp.zeros_like(dg_ref)
            loss_ref[...] = jnp.zeros_like(loss_ref)

        dg_ref[...] += jnp.sum(dy * xh, axis=0, keepdims=True)
        part = jnp.sum(jnp.sum(e * e, axis=1, keepdims=True), axis=0, keepdims=True) * (0.5 / d)
        loss_ref[...] += jnp.broadcast_to(part, loss_ref.shape)

    row = pl.BlockSpec((tm, d), lambda i: (i, 0))
    vec = pl.BlockSpec((1, d), lambda i: (0, 0))
    return pl.pallas_call(
        body, name="loss_head", grid=(t // tm,),
        in_specs=[row, row, vec],
        out_specs=[pl.BlockSpec((8, LANES), lambda i: (0, 0)), row, row, vec],
        out_shape=[jax.ShapeDtypeStruct((8, LANES), F32), jax.ShapeDtypeStruct((t, d), F32),
                   jax.ShapeDtypeStruct((t, d), BF16), jax.ShapeDtypeStruct((1, d), F32)],
        compiler_params=_params(("arbitrary",)),
    )(h, tgt, g)


def _rope_tables(pos, inv_freq, sel_lo, sel_hi):
    t = pos.shape[0]
    tm = min(ROW_TILE, t)

    def body(p_ref, f_ref, lo_ref, hi_ref, c_ref, sa_ref, sb_ref):
        ang = p_ref[...].astype(F32) * f_ref[...]
        rot = lo_ref[...] + hi_ref[...]
        cs, sn = jnp.cos(ang), jnp.sin(ang)
        c_ref[...] = cs * rot + (1.0 - rot)
        sa_ref[...] = -sn * lo_ref[...]
        sb_ref[...] = sn * hi_ref[...]

    vec = pl.BlockSpec((1, LANES), lambda i: (0, 0))
    row = pl.BlockSpec((tm, LANES), lambda i: (i, 0))
    return pl.pallas_call(
        body, name="rope_tables", grid=(t // tm,),
        in_specs=[pl.BlockSpec((tm, 1), lambda i: (i, 0)), vec, vec, vec],
        out_specs=[row, row, row], out_shape=[jax.ShapeDtypeStruct((t, LANES), F32)] * 3,
        compiler_params=_params(("parallel",)),
    )(pos, inv_freq, sel_lo, sel_hi)


def _rope_apply(name, src, col0, n_cols, cos_t, sin_a, sin_b, sign):
    t = src.shape[0]
    tm = min(ROW_TILE, t)

    def body(x_ref, c_ref, sa_ref, sb_ref, o_ref):
        cs, sa, sb = c_ref[...], sign * sa_ref[...], sign * sb_ref[...]
        for c in range(n_cols):
            cols = slice(c * LANES, (c + 1) * LANES)
            xf = x_ref[:, cols].astype(F32)
            up = pltpu.roll(xf, LANES - ROPE_HALF, 1)
            dn = pltpu.roll(xf, ROPE_HALF, 1)
            o_ref[:, cols] = (xf * cs + up * sa + dn * sb).astype(BF16)

    wide = n_cols * LANES
    tab = pl.BlockSpec((tm, LANES), lambda i: (i, 0))
    return pl.pallas_call(
        body, name=name, grid=(t // tm,),
        in_specs=[pl.BlockSpec((tm, wide), lambda i: (i, col0 // n_cols)), tab, tab, tab],
        out_specs=pl.BlockSpec((tm, wide), lambda i: (i, 0)),
        out_shape=jax.ShapeDtypeStruct((t, wide), BF16),
        compiler_params=_params(("parallel",)),
    )(src, cos_t, sin_a, sin_b)


DA_T = 256
FWD_STREAMS = 4
BWD_STREAMS = 2


def _lane_lo():
    return lax.broadcasted_iota(jnp.int32, (BLOCK, LANES), 1) < HEAD_DIM


def _dilated_bias_tiles(s):
    n = s // DA_T
    dist = (np.arange(n)[:, None, None] * DA_T + np.arange(DA_T)[None, :, None] - np.arange(DA_T)[None, None, :])
    cnt = np.zeros(dist.shape, np.float32)
    for window, dil in DIL_PATTERNS:
        cnt += ((dist >= 0) & (dist % dil == 0) & (dist <= window)).astype(np.float32)
    return jnp.asarray(np.where(cnt > 0, np.log(np.maximum(cnt, 1.0)), NEG).astype(np.float32))


def _stack_heads(x, lo):
    zero = jnp.zeros_like(x)
    return jnp.concatenate([jnp.where(lo, x, zero), jnp.where(lo, zero, x)], axis=0)


def _da_fwd(qk, proj, v_col0, bias, batch, s, ride=None, streams=FWD_STREAMS):
    t = qk.shape[0]
    nq = s // DA_T
    n_pairs = 4
    ns = streams
    wide = ns * LANES
    scale = HEAD_DIM ** -0.5

    def body(q_ref, k_ref, v_ref, b_ref, o_ref, lse_ref, acc_ref, m_ref, l_ref):
        i = pl.program_id(2)
        lo = lax.broadcasted_iota(jnp.int32, (DA_T, LANES), 1) < HEAD_DIM
        ones = jnp.ones((DA_T, LANES), BF16)
        acc_ref[...] = jnp.zeros_like(acc_ref)
        m_ref[...] = jnp.full(m_ref.shape, NEG, F32)
        l_ref[...] = jnp.zeros_like(l_ref)
        qqs = [_stack_heads(q_ref[:, st * LANES:(st + 1) * LANES] * scale, lo) for st in range(ns)]

        def scores(st, rows, bias2):
            k = k_ref[rows, st * LANES:(st + 1) * LANES]
            return lax.dot_general(qqs[st], k, NT, preferred_element_type=F32) + bias2

        def softmax(st, sc):
            m_old = m_ref[st]
            m_new = jnp.maximum(m_old, jnp.max(sc, axis=1, keepdims=True))
            m_ref[st] = m_new
            return jnp.exp(sc - m_new).astype(BF16), jnp.exp(m_old - m_new)

        def values(st, rows, p, alpha):
            v = v_ref[rows, st * LANES:(st + 1) * LANES]
            vz = jnp.zeros_like(v)
            l_ref[st] = alpha * l_ref[st] + lax.dot_general(p, ones, NN, preferred_element_type=F32)
            pv = (lax.dot_general(p[:DA_T], jnp.where(lo, v, vz), NN, preferred_element_type=F32)
                  + lax.dot_general(p[DA_T:], jnp.where(lo, vz, v), NN, preferred_element_type=F32))
            acc_ref[st] = acc_ref[st] * jnp.where(lo, alpha[:DA_T], alpha[DA_T:]) + pv

        def trip(dlt, carry):
            rows = pl.ds(pl.multiple_of((i - dlt) * DA_T, DA_T), DA_T)
            bias_t = b_ref[dlt]
            bias2 = jnp.concatenate([bias_t, bias_t], axis=0)
            scs = [scores(st, rows, bias2) for st in range(ns)]
            pas = [softmax(st, scs[st]) for st in range(ns)]
            for st in range(ns):
                values(st, rows, *pas[st])
            return carry

        lax.fori_loop(0, i + 1, trip, 0)
        for st in range(ns):
            cols = slice(st * LANES, (st + 1) * LANES)
            l_t = l_ref[st]
            o_ref[:, cols] = (acc_ref[st] / jnp.where(lo, l_t[:DA_T], l_t[DA_T:])).astype(BF16)
            lse = m_ref[st] + jnp.log(l_t)
            lse_ref[:, cols] = jnp.where(lo, lse[:DA_T], lse[DA_T:])

    blk = pl.BlockSpec((DA_T, wide), lambda b, h, i: (b * nq + i, h))
    return _call(
        body, name="attn_a_fwd", grid=(batch, n_pairs // ns, nq),
        in_specs=[blk,
                  pl.BlockSpec((s, wide), lambda b, h, i: (b, n_pairs // ns + h)),
                  pl.BlockSpec((s, wide), lambda b, h, i: (b, v_col0 // ns + h)),
                  pl.BlockSpec((nq, DA_T, DA_T), lambda b, h, i: (0, 0, 0))],
        out_specs=[blk, blk],
        out_shape=[jax.ShapeDtypeStruct((t, n_pairs * LANES), BF16), jax.ShapeDtypeStruct((t, n_pairs * LANES), F32)],
        scratch=[pltpu.VMEM((ns, DA_T, LANES), F32), pltpu.VMEM((ns, 2 * DA_T, 1), F32),
                 pltpu.VMEM((ns, 2 * DA_T, LANES), F32)],
        sem=("parallel", "parallel", "arbitrary"), args=(qk, qk, proj, bias), ride=ride)


def _da_bwd(qk, proj, v_col0, bias, o, lse, do, batch, s, ride=None, streams=BWD_STREAMS):
    t = qk.shape[0]
    nq = s // DA_T
    n_pairs = 4
    ns = streams
    wide = ns * LANES
    scale = HEAD_DIM ** -0.5

    def body(q_ref, k_ref, v_ref, b_ref, o_ref, lse_ref, do_ref, dq_ref, dk_ref, dv_ref, dk_acc, dv_acc, dq_acc):
        i = pl.program_id(2)
        lo = lax.broadcasted_iota(jnp.int32, (DA_T, LANES), 1) < HEAD_DIM

        @pl.when(i == 0)
        def _():
            dk_acc[...] = jnp.zeros_like(dk_acc)
            dv_acc[...] = jnp.zeros_like(dv_acc)

        dq_acc[...] = jnp.zeros_like(dq_acc)
        qqs, dds, deltas, lses = [], [], [], []
        for st in range(ns):
            cols = slice(st * LANES, (st + 1) * LANES)
            do_ = do_ref[:, cols]
            qqs.append(_stack_heads(q_ref[:, cols] * scale, lo))
            dds.append(_stack_heads(do_, lo))
            prod = do_.astype(F32) * o_ref[:, cols].astype(F32)
            fz = jnp.zeros_like(prod)
            deltas.append(jnp.concatenate([jnp.sum(jnp.where(lo, prod, fz), axis=1, keepdims=True),
                                           jnp.sum(jnp.where(lo, fz, prod), axis=1, keepdims=True)], axis=0))
            lse_t = lse_ref[:, cols]
            lses.append(jnp.concatenate([lse_t[:, 0:1], lse_t[:, HEAD_DIM:HEAD_DIM + 1]], axis=0))

        def products(st, rows, bias2):
            cols = slice(st * LANES, (st + 1) * LANES)
            sc = lax.dot_general(qqs[st], k_ref[rows, cols], NT, preferred_element_type=F32) + bias2
            return sc, lax.dot_general(dds[st], v_ref[rows, cols], NT, preferred_element_type=F32)

        def weights(st, sc, dp):
            p = jnp.exp(sc - lses[st])
            return (p * (dp - deltas[st])).astype(BF16), p.astype(BF16)

        def gradients(st, rows, ds, p):
            cols = slice(st * LANES, (st + 1) * LANES)
            k = k_ref[rows, cols]
            kz = jnp.zeros_like(k)
            dq_acc[st] += (lax.dot_general(ds[:DA_T], jnp.where(lo, k, kz), NN, preferred_element_type=F32)
                           + lax.dot_general(ds[DA_T:], jnp.where(lo, kz, k), NN, preferred_element_type=F32))
            dk_acc[rows, cols] += lax.dot_general(ds, qqs[st], TN, preferred_element_type=F32)
            dv_acc[rows, cols] += lax.dot_general(p, dds[st], TN, preferred_element_type=F32)

        def trip(dlt, carry):
            rows = pl.ds(pl.multiple_of((i - dlt) * DA_T, DA_T), DA_T)
            bias_t = b_ref[dlt]
            bias2 = jnp.concatenate([bias_t, bias_t], axis=0)
            prods = [products(st, rows, bias2) for st in range(ns)]
            wts = [weights(st, *prods[st]) for st in range(ns)]
            for st in range(ns):
                gradients(st, rows, *wts[st])
            return carry

        lax.fori_loop(0, i + 1, trip, 0)
        for st in range(ns):
            dq_ref[:, st * LANES:(st + 1) * LANES] = (dq_acc[st] * scale).astype(BF16)

        @pl.when(i == nq - 1)
        def _():
            dk_ref[...] = dk_acc[...].astype(BF16)
            dv_ref[...] = dv_acc[...].astype(BF16)

    blk = pl.BlockSpec((DA_T, wide), lambda b, h, i: (b * nq + i, h))
    seq = pl.BlockSpec((s, wide), lambda b, h, i: (b, h))
    out = jax.ShapeDtypeStruct((t, n_pairs * LANES), BF16)
    return _call(
        body, name="attn_a_bwd", grid=(batch, n_pairs // ns, nq),
        in_specs=[blk,
                  pl.BlockSpec((s, wide), lambda b, h, i: (b, n_pairs // ns + h)),
                  pl.BlockSpec((s, wide), lambda b, h, i: (b, v_col0 // ns + h)),
                  pl.BlockSpec((nq, DA_T, DA_T), lambda b, h, i: (0, 0, 0)),
                  blk, blk, blk],
        out_specs=[blk, seq, seq], out_shape=[out, out, out],
        scratch=[pltpu.VMEM((s, wide), F32), pltpu.VMEM((s, wide), F32), pltpu.VMEM((ns, DA_T, LANES), F32)],
        sem=("parallel", "parallel", "arbitrary"), args=(qk, qk, proj, bias, o, lse, do), ride=ride)


SB_Q = 256


def _sb_consts(after):
    r = lax.broadcasted_iota(jnp.int32, (2 * BLOCK, 2 * BLOCK), 0) % BLOCK
    c = lax.broadcasted_iota(jnp.int32, (2 * BLOCK, 2 * BLOCK), 1)
    tri = (r > c) if after else (r < c)
    return jnp.logical_or(c >= BLOCK, tri).astype(BF16)


def _split(x):
    hi = x.astype(BF16)
    lo = (x - hi.astype(F32)).astype(BF16)
    return jnp.concatenate([hi, lo], axis=1)


def _sb_fwd(proj, q_col0, k_col0, v_col0, batch, s, ride=None, streams=FWD_STREAMS):
    t = proj.shape[0]
    nq = s // SB_Q
    n_pairs = 4
    ns = streams
    wide = ns * LANES
    scale = HEAD_DIM ** -0.5

    def body(q_ref, k_ref, v_ref, o_ref, tot_ref, acc_ref, run_ref):
        i = pl.program_id(2)
        lo_q = lax.broadcasted_iota(jnp.int32, (SB_Q, LANES), 1) < HEAD_DIM
        lo_k = _lane_lo()
        mat = _sb_consts(True)
        row = lax.broadcasted_iota(jnp.int32, (2 * SB_Q, LANES), 0) % SB_Q
        ahead = row - lax.broadcasted_iota(jnp.int32, (2 * SB_Q, LANES), 1)
        acc_ref[...] = jnp.zeros_like(acc_ref)
        run_ref[...] = jnp.zeros_like(run_ref)
        qqs = [_stack_heads(q_ref[:, st * LANES:(st + 1) * LANES] * scale, lo_q) for st in range(ns)]

        def units(todo):
            def rows(j):
                return pl.ds(pl.multiple_of(j * BLOCK, BLOCK), BLOCK)

            zs = [lax.dot_general(qqs[st], k_ref[rows(j), st * LANES:(st + 1) * LANES], NT, preferred_element_type=F32)
                  for st, j, _ in todo]
            logs = []
            for z, (_, _, off) in zip(zs, todo):
                lsig = jnp.minimum(z, 0.0) - jnp.log(1.0 + jnp.exp(-jnp.abs(z)))
                lneg = lsig - z
                if off is not None:
                    lneg = jnp.where(ahead > off, lneg, 0.0)
                logs.append((lsig, _split(lneg)))
            sums = [lax.dot_general(cat, mat, NN, preferred_element_type=F32) for _, cat in logs]
            probs = []
            for (lsig, _), sm, (st, _, off) in zip(logs, sums, todo):
                run = run_ref[st]
                a = jnp.exp(lsig + run + sm[:, :BLOCK])
                if off is not None:
                    a = jnp.where(ahead > off, a, 0.0)
                run_ref[st] = run + sm[:, BLOCK:]
                probs.append(a.astype(BF16))
            for ab, (st, j, _) in zip(probs, todo):
                v = v_ref[rows(j), st * LANES:(st + 1) * LANES]
                vz = jnp.zeros_like(v)
                acc_ref[st] += (lax.dot_general(ab[:SB_Q], jnp.where(lo_k, v, vz), NN, preferred_element_type=F32)
                                + lax.dot_general(ab[SB_Q:], jnp.where(lo_k, vz, v), NN, preferred_element_type=F32))

        units([(st, 2 * i + 1, BLOCK) for st in range(ns)] + [(st, 2 * i, 0) for st in range(ns)])

        def pair(p, carry):
            jp = i - 1 - p
            units([(st, 2 * jp + 1, None) for st in range(ns)] + [(st, 2 * jp, None) for st in range(ns)])
            return carry

        lax.fori_loop(0, i, pair, 0)
        for st in range(ns):
            cols = slice(st * LANES, (st + 1) * LANES)
            o_ref[:, cols] = acc_ref[st].astype(BF16)
            tot_ref[:, cols] = jnp.where(lo_q, run_ref[st, 0:SB_Q, :], run_ref[st, SB_Q:2 * SB_Q, :])

    def seq(col0):
        return pl.BlockSpec((s, wide), lambda b, h, i: (b, col0 // ns + h))

    blk = pl.BlockSpec((SB_Q, wide), lambda b, h, i: (b * nq + i, h))
    return _call(
        body, name="attn_b_fwd", grid=(batch, n_pairs // ns, nq),
        in_specs=[pl.BlockSpec((SB_Q, wide), lambda b, h, i: (b * nq + i, q_col0 // ns + h)), seq(k_col0), seq(v_col0)],
        out_specs=[blk, blk],
        out_shape=[jax.ShapeDtypeStruct((t, n_pairs * LANES), BF16), jax.ShapeDtypeStruct((t, n_pairs * LANES), F32)],
        scratch=[pltpu.VMEM((ns, SB_Q, LANES), F32), pltpu.VMEM((ns, 2 * SB_Q, LANES), F32)],
        sem=("parallel", "parallel", "arbitrary"), args=(proj, proj, proj), ride=ride)


def _sb_bwd(proj, q_col0, k_col0, v_col0, tot, do, batch, s, ride=None, streams=BWD_STREAMS):
    t = proj.shape[0]
    nq = s // SB_Q
    n_pairs = 4
    ns = streams
    wide = ns * LANES
    scale = HEAD_DIM ** -0.5

    def body(q_ref, k_ref, v_ref, tot_ref, do_ref, dq_ref, dk_ref, dv_ref, dk_acc, dv_acc, dq_acc, seen_ref, gsum_ref):
        i = pl.program_id(2)
        lo_q = lax.broadcasted_iota(jnp.int32, (SB_Q, LANES), 1) < HEAD_DIM
        lo_k = _lane_lo()

        @pl.when(i == 0)
        def _():
            dk_acc[...] = jnp.zeros_like(dk_acc)
            dv_acc[...] = jnp.zeros_like(dv_acc)

        mat_after = _sb_consts(True)
        mat_before = _sb_consts(False)
        row = lax.broadcasted_iota(jnp.int32, (2 * SB_Q, LANES), 0) % SB_Q
        ahead = row - lax.broadcasted_iota(jnp.int32, (2 * SB_Q, LANES), 1)
        dq_acc[...] = jnp.zeros_like(dq_acc)
        seen_ref[...] = jnp.zeros_like(seen_ref)
        gsum_ref[...] = jnp.zeros_like(gsum_ref)
        qqs, dds, totals = [], [], []
        for st in range(ns):
            cols = slice(st * LANES, (st + 1) * LANES)
            qqs.append(_stack_heads(q_ref[:, cols] * scale, lo_q))
            dds.append(_stack_heads(do_ref[:, cols], lo_q))
            tot_t = tot_ref[:, cols]
            totals.append(jnp.concatenate([jnp.broadcast_to(tot_t[:, 0:1], (SB_Q, LANES)),
                                           jnp.broadcast_to(tot_t[:, HEAD_DIM:HEAD_DIM + 1], (SB_Q, LANES))], axis=0))

        def units(todo):
            def rows(j):
                return pl.ds(pl.multiple_of(j * BLOCK, BLOCK), BLOCK)

            def cols(st):
                return slice(st * LANES, (st + 1) * LANES)

            prods = [(lax.dot_general(qqs[st], k_ref[rows(j), cols(st)], NT, preferred_element_type=F32),
                      lax.dot_general(dds[st], v_ref[rows(j), cols(st)], NT, preferred_element_type=F32))
                     for st, j, _ in todo]
            logs = []
            for (z, _), (_, _, off) in zip(prods, todo):
                lsig = jnp.minimum(z, 0.0) - jnp.log(1.0 + jnp.exp(-jnp.abs(z)))
                lneg = lsig - z
                if off is not None:
                    lneg = jnp.where(ahead > off, lneg, 0.0)
                logs.append((lsig, _split(lneg)))
            sums = [lax.dot_general(cat, mat_after, NN, preferred_element_type=F32) for _, cat in logs]
            gates = []
            for (lsig, _), sm, (_, da), (st, _, off) in zip(logs, sums, prods, todo):
                seen = seen_ref[st]
                a = jnp.exp(lsig + (totals[st] - seen - sm[:, BLOCK:]) + sm[:, :BLOCK])
                if off is not None:
                    a = jnp.where(ahead > off, a, 0.0)
                seen_ref[st] = seen + sm[:, BLOCK:]
                g = a * da
                gates.append((a.astype(BF16), g, _split(g)))
            gsums = [lax.dot_general(cat, mat_before, NN, preferred_element_type=F32) for _, _, cat in gates]
            outs = []
            for (lsig, _), (ab, g, _), gs, (st, _, off) in zip(logs, gates, gsums, todo):
                gsum = gsum_ref[st]
                dz = g - jnp.exp(lsig) * (g + gsum + gs[:, :BLOCK])
                if off is not None:
                    dz = jnp.where(ahead > off, dz, 0.0)
                gsum_ref[st] = gsum + gs[:, BLOCK:]
                outs.append((dz.astype(BF16), ab))
            for (dzb, ab), (st, j, _) in zip(outs, todo):
                k = k_ref[rows(j), cols(st)]
                kz = jnp.zeros_like(k)
                dq_acc[st] += (lax.dot_general(dzb[:SB_Q], jnp.where(lo_k, k, kz), NN, preferred_element_type=F32)
                               + lax.dot_general(dzb[SB_Q:], jnp.where(lo_k, kz, k), NN, preferred_element_type=F32))
                dk_acc[rows(j), cols(st)] += lax.dot_general(dzb, qqs[st], TN, preferred_element_type=F32)
                dv_acc[rows(j), cols(st)] += lax.dot_general(ab, dds[st], TN, preferred_element_type=F32)

        def pair(p, carry):
            units([(st, 2 * p, None) for st in range(ns)] + [(st, 2 * p + 1, None) for st in range(ns)])
            return carry

        lax.fori_loop(0, i, pair, 0)
        units([(st, 2 * i, 0) for st in range(ns)] + [(st, 2 * i + 1, BLOCK) for st in range(ns)])
        for st in range(ns):
            dq_ref[:, st * LANES:(st + 1) * LANES] = (dq_acc[st] * scale).astype(BF16)

        @pl.when(i == nq - 1)
        def _():
            dk_ref[...] = dk_acc[...].astype(BF16)
            dv_ref[...] = dv_acc[...].astype(BF16)

    def seq_in(col0):
        return pl.BlockSpec((s, wide), lambda b, h, i: (b, col0 // ns + h))

    blk = pl.BlockSpec((SB_Q, wide), lambda b, h, i: (b * nq + i, h))
    seq = pl.BlockSpec((s, wide), lambda b, h, i: (b, h))
    out = jax.ShapeDtypeStruct((t, n_pairs * LANES), BF16)
    return _call(
        body, name="attn_b_bwd", grid=(batch, n_pairs // ns, nq),
        in_specs=[pl.BlockSpec((SB_Q, wide), lambda b, h, i: (b * nq + i, q_col0 // ns + h)), seq_in(k_col0),
                  seq_in(v_col0), blk, blk],
        out_specs=[blk, seq, seq], out_shape=[out, out, out],
        scratch=[pltpu.VMEM((s, wide), F32), pltpu.VMEM((s, wide), F32), pltpu.VMEM((ns, SB_Q, LANES), F32),
                 pltpu.VMEM((ns, 2 * SB_Q, LANES), F32), pltpu.VMEM((ns, 2 * SB_Q, LANES), F32)],
        sem=("parallel", "parallel", "arbitrary"), args=(proj, proj, proj, tot, do), ride=ride)


MEM_Q_TILE = 256


def _mem_fwd(q, kv, batch, s, n_mem):
    t, width = q.shape
    tq = min(MEM_Q_TILE, s)
    nq = s // tq
    scale = MEM_HEAD_DIM ** -0.5

    def body(q_ref, kv_ref, o_ref):
        for h in range(N_HEADS_MEM):
            cols = slice(h * MEM_HEAD_DIM, (h + 1) * MEM_HEAD_DIM)
            k = kv_ref[:, cols]
            v = kv_ref[:, width + h * MEM_HEAD_DIM: width + (h + 1) * MEM_HEAD_DIM]
            sc = lax.dot_general(q_ref[:, cols], k, NT, preferred_element_type=F32) * scale
            p = jnp.exp(sc - jnp.max(sc, axis=1, keepdims=True))
            p = p / jnp.sum(p, axis=1, keepdims=True)
            o_ref[:, cols] = lax.dot_general(p.astype(BF16), v, NN, preferred_element_type=F32).astype(BF16)

    return pl.pallas_call(
        body, name="mem_attn_fwd", grid=(batch, nq),
        in_specs=[pl.BlockSpec((tq, width), lambda b, i: (b * nq + i, 0)),
                  pl.BlockSpec((n_mem, 2 * width), lambda b, i: (b, 0))],
        out_specs=pl.BlockSpec((tq, width), lambda b, i: (b * nq + i, 0)),
        out_shape=jax.ShapeDtypeStruct((t, width), BF16),
        compiler_params=_params(("parallel", "parallel")),
    )(q, kv)


def _mem_bwd(q, kv, do, batch, s, n_mem):
    t, width = q.shape
    tq = min(MEM_Q_TILE, s)
    nq = s // tq
    scale = MEM_HEAD_DIM ** -0.5

    def body(q_ref, kv_ref, do_ref, dq_ref, dkv_ref, acc):
        i = pl.program_id(1)

        @pl.when(i == 0)
        def _():
            acc[...] = jnp.zeros_like(acc)

        for h in range(N_HEADS_MEM):
            cols = slice(h * MEM_HEAD_DIM, (h + 1) * MEM_HEAD_DIM)
            vcols = slice(width + h * MEM_HEAD_DIM, width + (h + 1) * MEM_HEAD_DIM)
            qh, k, v, doh = q_ref[:, cols], kv_ref[:, cols], kv_ref[:, vcols], do_ref[:, cols]
            sc = lax.dot_general(qh, k, NT, preferred_element_type=F32) * scale
            p = jnp.exp(sc - jnp.max(sc, axis=1, keepdims=True))
            p = p / jnp.sum(p, axis=1, keepdims=True)
            dp = lax.dot_general(doh, v, NT, preferred_element_type=F32)
            ds = (p * (dp - jnp.sum(p * dp, axis=1, keepdims=True)) * scale).astype(BF16)
            dq_ref[:, cols] = lax.dot_general(ds, k, NN, preferred_element_type=F32).astype(BF16)
            acc[:, cols] += lax.dot_general(ds, qh, TN, preferred_element_type=F32)
            acc[:, vcols] += lax.dot_general(p.astype(BF16), doh, TN, preferred_element_type=F32)

        @pl.when(i == nq - 1)
        def _():
            dkv_ref[...] = acc[...].astype(BF16)

    row = pl.BlockSpec((tq, width), lambda b, i: (b * nq + i, 0))
    kvs = pl.BlockSpec((n_mem, 2 * width), lambda b, i: (b, 0))
    return pl.pallas_call(
        body, name="mem_attn_bwd", grid=(batch, nq),
        in_specs=[row, kvs, row], out_specs=[row, kvs],
        out_shape=[jax.ShapeDtypeStruct((t, width), BF16), jax.ShapeDtypeStruct((batch * n_mem, 2 * width), BF16)],
        scratch_shapes=[pltpu.VMEM((n_mem, 2 * width), F32)],
        compiler_params=_params(("parallel", "arbitrary")),
    )(q, kv, do)


def _mixer_fwd(o_a, o_b, w_a, w_b, proj, gate_col0):
    t, width = o_a.shape
    d = w_a.shape[1]
    tm = min(ROW_TILE, t)
    gb0 = gate_col0 * LANES // d

    def body(oa_ref, ob_ref, wa_ref, wb_ref, ga_ref, gb_ref, ua_ref, ub_ref, mix_ref):
        ua = lax.dot_general(oa_ref[...], wa_ref[...], NN, preferred_element_type=F32)
        ub = lax.dot_general(ob_ref[...], wb_ref[...], NN, preferred_element_type=F32)
        ua_ref[...] = ua.astype(BF16)
        ub_ref[...] = ub.astype(BF16)
        mix_ref[...] = (jax.nn.sigmoid(ga_ref[...].astype(F32)) * ua
                        + jax.nn.sigmoid(gb_ref[...].astype(F32)) * ub).astype(BF16)

    row = pl.BlockSpec((tm, width), lambda i: (i, 0))
    wsp = pl.BlockSpec((width, d), lambda i: (0, 0))
    out = pl.BlockSpec((tm, d), lambda i: (i, 0))
    osh = jax.ShapeDtypeStruct((t, d), BF16)
    return pl.pallas_call(
        body, name="mixer_fwd", grid=(t // tm,),
        in_specs=[row, row, wsp, wsp,
                  pl.BlockSpec((tm, d), lambda i: (i, gb0)), pl.BlockSpec((tm, d), lambda i: (i, gb0 + 1))],
        out_specs=[out, out, out], out_shape=[osh, osh, osh],
        compiler_params=_params(("parallel",)),
    )(o_a, o_b, w_a, w_b, proj, proj)


def _mixer_bwd(dmix, ua, ub, proj, gate_col0):
    t, d = dmix.shape
    tm = min(ROW_TILE, t)
    nc = d // LANES

    def body(dm_ref, ua_ref, ub_ref, ga_ref, gb_ref, dua_ref, dub_ref, dg_ref):
        dm = dm_ref[...].astype(F32)
        sa = jax.nn.sigmoid(ga_ref[...].astype(F32))
        sb = jax.nn.sigmoid(gb_ref[...].astype(F32))
        dua_ref[...] = (dm * sa).astype(BF16)
        dub_ref[...] = (dm * sb).astype(BF16)
        dg_ref[:, 0:d] = (dm * ua_ref[...].astype(F32) * sa * (1.0 - sa)).astype(BF16)
        dg_ref[:, d:2 * d] = (dm * ub_ref[...].astype(F32) * sb * (1.0 - sb)).astype(BF16)

    row = pl.BlockSpec((tm, d), lambda i: (i, 0))
    return pl.pallas_call(
        body, name="mixer_bwd", grid=(t // tm,),
        in_specs=[row, row, row,
                  pl.BlockSpec((tm, d), lambda i: (i, gate_col0 // nc)),
                  pl.BlockSpec((tm, d), lambda i: (i, gate_col0 // nc + 1))],
        out_specs=[row, row, pl.BlockSpec((tm, 2 * d), lambda i: (i, 0))],
        out_shape=[jax.ShapeDtypeStruct((t, d), BF16), jax.ShapeDtypeStruct((t, d), BF16),
                   jax.ShapeDtypeStruct((t, 2 * d), BF16)],
        compiler_params=_params(("parallel",)),
    )(dmix, ua, ub, proj, proj)


FFN_COLS = 1024


def _ffn_up(n, w_gate, w_up):
    t, d = n.shape
    hidden = w_gate.shape[1]
    tm = min(ROW_TILE, t)
    tn = min(FFN_COLS, hidden)

    def body(n_ref, wg_ref, wu_ref, hg_ref, hu_ref, act_ref):
        hg = lax.dot_general(n_ref[...], wg_ref[...], NN, preferred_element_type=F32)
        hu = lax.dot_general(n_ref[...], wu_ref[...], NN, preferred_element_type=F32)
        hg_ref[...] = hg.astype(BF16)
        hu_ref[...] = hu.astype(BF16)
        act_ref[...] = (hg * jax.nn.sigmoid(hg) * hu).astype(BF16)

    wsp = pl.BlockSpec((d, tn), lambda j, i: (0, j))
    out = pl.BlockSpec((tm, tn), lambda j, i: (i, j))
    osh = jax.ShapeDtypeStruct((t, hidden), BF16)
    return pl.pallas_call(
        body, name="ffn_up", grid=(hidden // tn, t // tm),
        in_specs=[pl.BlockSpec((tm, d), lambda j, i: (i, 0)), wsp, wsp],
        out_specs=[out, out, out], out_shape=[osh, osh, osh],
        compiler_params=_params(("parallel", "parallel")),
    )(n, w_gate, w_up)


def _ffn_bwd_act(dh, w_down, hg, hu):
    t, d = dh.shape
    hidden = w_down.shape[0]
    tm = min(ROW_TILE, t)
    tn = min(FFN_COLS, hidden)

    def body(dh_ref, wd_ref, hg_ref, hu_ref, dhg_ref, dhu_ref):
        dact = lax.dot_general(dh_ref[...], wd_ref[...], NT, preferred_element_type=F32)
        hg = hg_ref[...].astype(F32)
        sg = jax.nn.sigmoid(hg)
        dhu_ref[...] = (dact * hg * sg).astype(BF16)
        dhg_ref[...] = (dact * hu_ref[...].astype(F32) * sg * (1.0 + hg * (1.0 - sg))).astype(BF16)

    hid = pl.BlockSpec((tm, tn), lambda j, i: (i, j))
    osh = jax.ShapeDtypeStruct((t, hidden), BF16)
    return pl.pallas_call(
        body, name="ffn_bwd_act", grid=(hidden // tn, t // tm),
        in_specs=[pl.BlockSpec((tm, d), lambda j, i: (i, 0)), pl.BlockSpec((tn, d), lambda j, i: (j, 0)), hid, hid],
        out_specs=[hid, hid], out_shape=[osh, osh],
        compiler_params=_params(("parallel", "parallel")),
    )(dh, w_down, hg, hu)


MM_ROWS = 1024


def _mm_cols_t(name, a, w, out_dtype, res=None):
    t = a.shape[0]
    n_sh, k, cs = w.shape
    tm = min(MM_ROWS, t)
    o_spec = pl.BlockSpec((tm, k), lambda i, j: (i, 0))
    return _mm(name, a, w, grid=(t // tm, n_sh),
               a_spec=pl.BlockSpec((tm, cs), lambda i, j: (i, j)), b_spec=pl.BlockSpec((None, k, cs), lambda i, j: (j, 0, 0)),
               o_shape=(t, k), o_spec=o_spec, dims=NT, out_dtype=out_dtype, nk=n_sh, res=res,
               res_spec=o_spec if res is not None else None)


def _mm_w(name, a, w, out_dtype, dims=NN, res=None, tm=MM_ROWS, tn=1024):
    t, k = a.shape
    n = w.shape[1] if dims == NN else w.shape[0]
    tm, tn = min(tm, t), min(tn, n)
    o_spec = pl.BlockSpec((tm, tn), lambda j, i: (i, j))
    b_spec = pl.BlockSpec((k, tn), lambda j, i: (0, j)) if dims == NN else pl.BlockSpec((tn, k), lambda j, i: (j, 0))
    return _mm(name, a, w, grid=(n // tn, t // tm), a_spec=pl.BlockSpec((tm, k), lambda j, i: (i, 0)), b_spec=b_spec,
               o_shape=(t, n), o_spec=o_spec, dims=dims, out_dtype=out_dtype, res=res,
               res_spec=o_spec if res is not None else None)


def _wgrad(name, a, g, tk=1024, tn=1024):
    t, k = a.shape
    n = g.shape[1]
    tm, tk, tn = min(MM_ROWS, t), min(tk, k), min(tn, n)
    return _mm(name, a, g, grid=(k // tk, n // tn, t // tm),
               a_spec=pl.BlockSpec((tm, tk), lambda p, q, r: (r, p)), b_spec=pl.BlockSpec((tm, tn), lambda p, q, r: (r, q)),
               o_shape=(k, n), o_spec=pl.BlockSpec((tk, tn), lambda p, q, r: (p, q)), dims=TN, out_dtype=BF16, nk=t // tm)


def _peers():
    x, y, c = lax.axis_index("x"), lax.axis_index("y"), lax.axis_index("c")
    me = 4 * x + 2 * y + c
    out = []
    for k in range(1, N_DEV):
        kx, ky, kc = (k >> 2) & 1, (k >> 1) & 1, k & 1
        px = 1 - x if kx else x
        py = 1 - y if ky else y
        pc = 1 - c if kc else c
        out.append(((px, py, pc), 4 * px + 2 * py + pc))
    return me, out


def _cast_weights(ws, pads):
    def body(*refs):
        n = len(refs) // 2
        for i_ref, o_ref, (pr, pc) in zip(refs[:n], refs[n:], pads):
            r, c = i_ref.shape
            o_ref[0:r, 0:c] = i_ref[...].astype(BF16)
            if pr:
                o_ref[r:r + pr, :] = jnp.zeros((pr, c), BF16)
            if pc:
                o_ref[:, c:c + pc] = jnp.zeros((r, pc), BF16)

    return pl.pallas_call(
        body, name="cast_weights", in_specs=[VMEM] * len(ws), out_specs=[VMEM] * len(ws),
        out_shape=[jax.ShapeDtypeStruct((w.shape[0] + pr, w.shape[1] + pc), BF16) for w, (pr, pc) in zip(ws, pads)],
    )(*ws)


def _window(ref, j, c):
    return ref.at[:, pl.ds(pl.multiple_of(j * c, LANES), c)]


def _scatter_copies(ins, outs, sems, cols, landed):
    send_sems, recv_sems, loc_sems = sems
    n_peer = N_DEV - 1
    me, peers = _peers()

    def src(w, j):
        return _window(ins[w], j, cols[w]) if cols[w] else ins[w].at[j]

    local = [pltpu.make_async_copy(src(w, me), outs[w].at[me], loc_sems.at[w]) for w in range(len(ins))]
    remote = [pltpu.make_async_remote_copy(
        src_ref=src(w, idx), dst_ref=outs[w].at[idx if landed else me],
        send_sem=send_sems.at[w * n_peer + k], recv_sem=recv_sems.at[w * n_peer + k],
        device_id=dev, device_id_type=pl.DeviceIdType.MESH)
        for k, (dev, idx) in reversed(list(enumerate(peers))) for w in range(len(ins))]
    return local, remote


OTHER_CHIPS = (2, 4, 6)


def _gather_copies(ins, outs, sems, cols):
    send_sems, recv_sems, loc_sems = sems
    x, y, c = lax.axis_index("x"), lax.axis_index("y"), lax.axis_index("c")
    me = 4 * x + 2 * y + c
    n_pair = N_DEV - 1

    def dev(mask):
        return (1 - x if mask & 4 else x, 1 - y if mask & 2 else y, 1 - c if mask & 1 else c)

    def slot(w, mask):
        j = jnp.bitwise_xor(me, mask)
        return _window(outs[w], j, cols[w]) if cols[w] else outs[w].at[j]

    def remote(w, pair, src, to_slot, target):
        return pltpu.make_async_remote_copy(src_ref=src, dst_ref=slot(w, to_slot), send_sem=send_sems.at[w * n_pair + pair],
                                            recv_sem=recv_sems.at[w * n_pair + pair], device_id=dev(target),
                                            device_id_type=pl.DeviceIdType.MESH)

    ws = range(len(ins))
    return dict(
        local=[pltpu.make_async_copy(ins[w], slot(w, 0), loc_sems.at[w]) for w in ws],
        to_chips=[remote(w, 1 + t, ins[w], 0, m) for t, m in enumerate(OTHER_CHIPS) for w in ws],
        to_core=[remote(w, 0, ins[w], 0, 1) for w in ws],
        from_chips=[remote(w, 1 + t, ins[w], m, 0) for t, m in enumerate(OTHER_CHIPS) for w in ws],
        pass_on=[remote(w, 4 + t, slot(w, m), m, 1) for t, m in enumerate(OTHER_CHIPS) for w in ws],
        from_core=[remote(w, 0, ins[w], 1, 0) for w in ws]
        + [remote(w, 4 + t, ins[w], m + 1, 0) for t, m in enumerate(OTHER_CHIPS) for w in ws])


def _exchange_start(ins, outs, sems, gather, cols):
    if gather:
        cps = _gather_copies(ins, outs, sems, cols)
        for cp in cps["local"] + cps["to_chips"] + cps["to_core"]:
            cp.start()
    else:
        local, remote = _scatter_copies(ins, outs, sems, cols, False)
        for cp in local + remote:
            cp.start()


def _exchange_pass_on(ins, outs, sems, gather, cols):
    if gather:
        cps = _gather_copies(ins, outs, sems, cols)
        for arrived, onward in zip(cps["from_chips"], cps["pass_on"]):
            arrived.wait_recv()
            onward.start()


def _exchange_wait(ins, outs, sems, gather, cols):
    if gather:
        cps = _gather_copies(ins, outs, sems, cols)
        for cp in cps["local"]:
            cp.wait()
        for cp in cps["to_chips"] + cps["to_core"] + cps["pass_on"]:
            cp.wait_send()
        for cp in cps["from_core"]:
            cp.wait_recv()
    else:
        local, remote = _scatter_copies(ins, outs, sems, cols, True)
        for cp in local:
            cp.wait()
        for cp in remote:
            cp.wait_send()
            cp.wait_recv()


def _exchange_shapes(arrs, gather, cols):
    n = len(arrs)
    out_shape = []
    for a, c in zip(arrs, cols):
        if gather:
            shape = (a.shape[0], N_DEV * c) if c else (N_DEV,) + a.shape
        else:
            shape = (N_DEV, a.shape[0], c) if c else a.shape
        out_shape.append(jax.ShapeDtypeStruct(shape, a.dtype))
    sems = [pltpu.SemaphoreType.DMA((n * (N_DEV - 1),)), pltpu.SemaphoreType.DMA((n * (N_DEV - 1),)),
            pltpu.SemaphoreType.DMA((n,))]
    return out_shape, sems


def _call(body, *, name, grid, in_specs, out_specs, out_shape, scratch, sem, args, ride=None):
    if ride is None:
        outs = pl.pallas_call(body, name=name, grid=grid, in_specs=in_specs, out_specs=out_specs, out_shape=out_shape,
                              scratch_shapes=scratch, compiler_params=_params(sem))(*args)
        return outs, None
    arrs, gather, cols = ride
    n, n_in, n_out, n_scr = len(arrs), len(in_specs), len(out_specs), len(scratch)
    x_shape, x_sems = _exchange_shapes(arrs, gather, cols)

    def riding(*refs):
        ins, x_ins = refs[:n_in], refs[n_in:n_in + n]
        outs = refs[n_in + n:n_in + n + n_out]
        x_outs = refs[n_in + n + n_out:n_in + 2 * n + n_out]
        scr = refs[n_in + 2 * n + n_out:n_in + 2 * n + n_out + n_scr]
        sems = refs[n_in + 2 * n + n_out + n_scr:]
        def at(step):
            return functools.reduce(jnp.logical_and, [pl.program_id(a) == v for a, v in enumerate(step)])

        @pl.when(at((0,) * len(grid)))
        def _():
            _exchange_start(x_ins, x_outs, sems, gather, cols)

        @pl.when(at((grid[0] // 2,) + (0,) * (len(grid) - 1)))
        def _():
            _exchange_pass_on(x_ins, x_outs, sems, gather, cols)

        body(*ins, *outs, *scr)

        @pl.when(at(tuple(g - 1 for g in grid)))
        def _():
            _exchange_wait(x_ins, x_outs, sems, gather, cols)

    res = pl.pallas_call(
        riding, name=name, grid=grid, in_specs=list(in_specs) + [ANY] * n, out_specs=list(out_specs) + [ANY] * n,
        out_shape=list(out_shape) + x_shape, scratch_shapes=list(scratch) + x_sems,
        compiler_params=_params(("arbitrary",) * len(grid)))(*args, *arrs)
    return res[:n_out], res[n_out:]


def _my_block():
    return (4 * lax.axis_index("x") + 2 * lax.axis_index("y") + lax.axis_index("c")).astype(jnp.int32).reshape(1)


def _proj_in_gather(n, w_shard):
    t, k = n.shape
    cs = w_shard.shape[1]
    tm = min(MM_ROWS, t)
    ni = t // tm
    arrival = (0, 1) + OTHER_CHIPS + tuple(m + 1 for m in OTHER_CHIPS)

    def mask_at(s):
        return jnp.where(s < 2, s, jnp.where(s < 5, 2 * (s - 1), 2 * (s - 4) + 1))

    def body(me_ref, n_ref, w_hbm, o_ref, all_hbm, w_vmem, send_sems, recv_sems, loc_sems, load_sem):
        s, i = pl.program_id(0), pl.program_id(1)
        cps = _gather_copies([w_hbm], [all_hbm], (send_sems, recv_sems, loc_sems), (0,))
        arrived = cps["local"] + cps["from_core"][:1] + cps["from_chips"] + cps["from_core"][1:]

        @pl.when(jnp.logical_and(s == 0, i == 0))
        def _():
            for cp in cps["local"] + cps["to_chips"] + cps["to_core"]:
                cp.start()

        for step, mask in enumerate(arrival):
            @pl.when(jnp.logical_and(s == step, i == 0))
            def _(step=step, mask=mask):
                if step == 0:
                    arrived[0].wait()
                else:
                    arrived[step].wait_recv()
                if mask in OTHER_CHIPS:
                    cps["pass_on"][OTHER_CHIPS.index(mask)].start()
                load = pltpu.make_async_copy(all_hbm.at[jnp.bitwise_xor(me_ref[0], mask)], w_vmem, load_sem)
                load.start()
                load.wait()

        o_ref[...] = lax.dot_general(n_ref[...], w_vmem[...], NN, preferred_element_type=F32).astype(BF16)

        @pl.when(jnp.logical_and(s == N_DEV - 1, i == ni - 1))
        def _():
            for cp in cps["to_chips"] + cps["to_core"] + cps["pass_on"]:
                cp.wait_send()

    return pl.pallas_call(
        body, name="proj_in",
        grid_spec=pltpu.PrefetchScalarGridSpec(
            num_scalar_prefetch=1, grid=(N_DEV, ni),
            in_specs=[pl.BlockSpec((tm, k), lambda s, i, me: (i, 0)), ANY],
            out_specs=[pl.BlockSpec((tm, cs), lambda s, i, me: (i, jnp.bitwise_xor(me[0], mask_at(s)))), ANY],
            scratch_shapes=[pltpu.VMEM((k, cs), BF16), pltpu.SemaphoreType.DMA((N_DEV - 1,)),
                            pltpu.SemaphoreType.DMA((N_DEV - 1,)), pltpu.SemaphoreType.DMA((1,)),
                            pltpu.SemaphoreType.DMA]),
        out_shape=[jax.ShapeDtypeStruct((t, N_DEV * cs), BF16), jax.ShapeDtypeStruct((N_DEV, k, cs), BF16)],
        compiler_params=_params(("arbitrary", "arbitrary")),
    )(_my_block(), n, w_shard)


def _gw_in_scatter(a, g):
    t, k = a.shape
    cs = g.shape[1] // N_DEV
    tm = min(MM_ROWS, t)
    nr = t // tm
    last = N_DEV - 1

    def body(me_ref, a_ref, g_ref, out_hbm, acc, stage, send_sems, recv_sems, loc_sem):
        s, r = pl.program_id(0), pl.program_id(1)
        me, peers = _peers()
        part = lax.dot_general(a_ref[...], g_ref[...], TN, preferred_element_type=F32)

        def send(step):
            j = last - 1 - step
            return pltpu.make_async_remote_copy(src_ref=stage.at[step % 2], dst_ref=out_hbm.at[me],
                                                send_sem=send_sems.at[j], recv_sem=recv_sems.at[j], device_id=peers[j][0],
                                                device_id_type=pl.DeviceIdType.MESH)

        local = pltpu.make_async_copy(stage.at[last % 2], out_hbm.at[me], loc_sem)

        @pl.when(r == 0)
        def _():
            acc[...] = part

        @pl.when(r > 0)
        def _():
            acc[...] += part

        for step in range(N_DEV):
            @pl.when(jnp.logical_and(s == step, r == nr - 1))
            def _(step=step):
                if step >= 2:
                    send(step - 2).wait_send()
                stage[step % 2] = acc[...].astype(BF16)
                if step < last:
                    send(step).start()
                else:
                    local.start()
                    send(step - 1).wait_send()
                    local.wait()
                    for j, (dev, idx) in enumerate(peers):
                        pltpu.make_async_remote_copy(src_ref=stage.at[0], dst_ref=out_hbm.at[idx],
                                                     send_sem=send_sems.at[j], recv_sem=recv_sems.at[j], device_id=dev,
                                                     device_id_type=pl.DeviceIdType.MESH).wait_recv()

    return pl.pallas_call(
        body, name="gw_in",
        grid_spec=pltpu.PrefetchScalarGridSpec(
            num_scalar_prefetch=1, grid=(N_DEV, nr),
            in_specs=[pl.BlockSpec((tm, k), lambda s, r, me: (r, 0)),
                      pl.BlockSpec((tm, cs), lambda s, r, me: (r, jnp.bitwise_xor(me[0], N_DEV - 1 - s)))],
            out_specs=ANY,
            scratch_shapes=[pltpu.VMEM((k, cs), F32), pltpu.VMEM((2, k, cs), BF16),
                            pltpu.SemaphoreType.DMA((N_DEV - 1,)), pltpu.SemaphoreType.DMA((N_DEV - 1,)),
                            pltpu.SemaphoreType.DMA]),
        out_shape=jax.ShapeDtypeStruct((N_DEV, k, cs), BF16),
        compiler_params=_params(("arbitrary", "arbitrary")),
    )(_my_block(), a, g)


SMALL_ROWS = 8


def _allreduce_small(parts, loss_part):
    n, d = len(parts), parts[0].shape[1]

    def body(*refs):
        part_refs, loss_ref, o_ref = refs[:n], refs[n], refs[n + 1]
        mine_ref, all_ref, send_sems, recv_sems = refs[n + 2:]
        me, peers = _peers()
        mine_ref[...] = jnp.zeros_like(mine_ref)
        for i, p_ref in enumerate(part_refs):
            mine_ref[i:i + 1, :] = p_ref[...]
        mine_ref[SMALL_ROWS - 1:SMALL_ROWS, 0:LANES] = loss_ref[0:1, :]
        all_ref[me] = mine_ref[...]
        for k, (dev, idx) in enumerate(peers):
            pltpu.make_async_remote_copy(src_ref=mine_ref, dst_ref=all_ref.at[me], send_sem=send_sems.at[k],
                                         recv_sem=recv_sems.at[k], device_id=dev,
                                         device_id_type=pl.DeviceIdType.MESH).start()
        for k, (dev, idx) in enumerate(peers):
            cp = pltpu.make_async_remote_copy(src_ref=mine_ref, dst_ref=all_ref.at[idx], send_sem=send_sems.at[k],
                                              recv_sem=recv_sems.at[k], device_id=dev,
                                              device_id_type=pl.DeviceIdType.MESH)
            cp.wait_send()
            cp.wait_recv()
        tot = all_ref[0]
        for dvc in range(1, N_DEV):
            tot = tot + all_ref[dvc]
        o_ref[...] = tot

    return pl.pallas_call(
        body, name="allreduce_small", in_specs=[VMEM] * (n + 1), out_specs=VMEM,
        out_shape=jax.ShapeDtypeStruct((SMALL_ROWS, d), F32),
        scratch_shapes=[pltpu.VMEM((SMALL_ROWS, d), F32), pltpu.VMEM((N_DEV, SMALL_ROWS, d), F32),
                        pltpu.SemaphoreType.DMA((N_DEV - 1,)), pltpu.SemaphoreType.DMA((N_DEV - 1,))],
    )(*parts, loss_part)


def _adam_math(g, w, m, v):
    m_new = ADAM_B1 * m + (1.0 - ADAM_B1) * g
    v_new = ADAM_B2 * v + (1.0 - ADAM_B2) * (g * g)
    m_hat = m_new / (1.0 - ADAM_B1 ** ADAM_STEP)
    v_hat = v_new / (1.0 - ADAM_B2 ** ADAM_STEP)
    delta = -ADAM_LR * (m_hat / (jnp.sqrt(v_hat) + ADAM_EPS) + ADAM_WD * w)
    return delta, m_new, v_new


def _adam(name, pieces, w, m, v):
    r, c = w.shape
    cp = pieces.shape[2]
    tr = r
    for cand in (256, 176, 128, 64):
        if r % cand == 0 and r > cand:
            tr = cand
            break

    def body(p_ref, w_ref, m_ref, v_ref, g_ref, d_ref, mo_ref, vo_ref):
        g = p_ref[0, :, 0:c].astype(F32)
        for dvc in range(1, N_DEV):
            g = g + p_ref[dvc, :, 0:c].astype(F32)
        delta, m_new, v_new = _adam_math(g, w_ref[...], m_ref[...], v_ref[...])
        g_ref[...] = g
        d_ref[...] = delta
        mo_ref[...] = m_new
        vo_ref[...] = v_new

    blk = pl.BlockSpec((tr, c), lambda i: (i, 0))
    osh = jax.ShapeDtypeStruct((r, c), F32)
    return pl.pallas_call(
        body, name=name, grid=(r // tr,),
        in_specs=[pl.BlockSpec((N_DEV, tr, cp), lambda i: (0, i, 0)), blk, blk, blk],
        out_specs=[blk, blk, blk, blk], out_shape=[osh, osh, osh, osh],
        compiler_params=_params(("parallel",)),
    )(pieces, w, m, v)


def _adam_small(g_all, ws, ms, vs):
    n = len(ws)

    def body(*refs):
        g_ref, ins, outs = refs[0], refs[1:1 + 3 * n], refs[1 + 3 * n:]
        for i in range(n):
            g = g_ref[i:i + 1, :]
            delta, m_new, v_new = _adam_math(g, ins[i][...], ins[n + i][...], ins[2 * n + i][...])
            for kind, val in enumerate((g, delta, m_new, v_new)):
                outs[kind * n + i][...] = val

    osh = jax.ShapeDtypeStruct(ws[0].shape, F32)
    res = pl.pallas_call(body, name="adam_small", in_specs=[VMEM] * (1 + 3 * n), out_specs=[VMEM] * (4 * n),
                         out_shape=[osh] * (4 * n))(g_all, *ws, *ms, *vs)
    return res[:n], res[n:2 * n], res[2 * n:3 * n], res[3 * n:]


def _local_step(x, mem, pos, tgt, gains, w_in_shard, shards, batch):
    g_mix, g_mem_q, g_mem_kv, g_ffn, g_final = gains
    t, d = x.shape
    s = t // batch
    n_mem = mem.shape[0] // batch
    n_sh = N_DEV
    width = shards[0].shape[0]
    nb = width // LANES

    lane = jnp.arange(LANES, dtype=jnp.int32) % HEAD_DIM
    sel_lo = (lane < ROPE_HALF).astype(F32)[None, :]
    sel_hi = ((lane >= ROPE_HALF) & (lane < 2 * ROPE_HALF)).astype(F32)[None, :]
    freqs = ROPE_THETA ** (-jnp.arange(ROPE_HALF, dtype=F32) / ROPE_HALF)
    inv_freq = jnp.where(lane < 2 * ROPE_HALF, freqs[lane % ROPE_HALF], 0.0)[None, :]
    cos_t, sin_a, sin_b = _rope_tables(pos, inv_freq, sel_lo, sel_hi)
    bias = _dilated_bias_tiles(s)

    n1 = _rms_fwd("norm_mix", x, g_mix)
    proj, w_in = _proj_in_gather(n1, w_in_shard)
    qk_a = _rope_apply("rope_fwd", proj, 0, 2 * nb, cos_t, sin_a, sin_b, 1.0)
    cs_up, cs_ffn = shards[0].shape[1], shards[6].shape[1]
    (o_a, lse_a), (w_up_a, w_up_b, w_out, w_q, w_kv, w_o, w_fd) = _da_fwd(
        qk_a, proj, 2 * nb, bias, batch, s,
        ride=(shards[:6] + shards[8:], True, (cs_up, cs_up, 0, 0, 0, cs_up, 0)))
    (o_b, tot_b), (w_fg, w_fu) = _sb_fwd(proj, 3 * nb, 4 * nb, 5 * nb, batch, s,
                                         ride=(shards[6:8], True, (cs_ffn, cs_ffn)))
    w_out = w_out.reshape(d, d)
    w_q = w_q.reshape(d, -1)
    w_kv = w_kv.reshape(d, -1)
    w_fd = w_fd.reshape(-1, d)
    ua, ub, mixed = _mixer_fwd(o_a, o_b, w_up_a, w_up_b, proj, 6 * nb)
    h1 = _mm_w("mix_out", mixed, w_out, F32, res=x)
    n2 = _rms_fwd("norm_mem_q", h1, g_mem_q)
    mem_n = _rms_fwd("norm_mem_kv", mem, g_mem_kv)
    q_m = _mm_w("mem_q", n2, w_q, BF16)
    kv_m = _mm_w("mem_kv", mem_n, w_kv, BF16)
    o_m = _mem_fwd(q_m, kv_m, batch, s, n_mem)
    h2 = _mm_w("mem_out", o_m, w_o, F32, res=h1)
    n3 = _rms_fwd("norm_ffn", h2, g_ffn)
    hg, hu, act = _ffn_up(n3, w_fg, w_fu)
    h3 = _mm_w("ffn_down", act, w_fd, F32, res=h2, tm=ROW_TILE)
    loss_part, dh3, dh3_b, dg_final = _loss_head(h3, tgt, g_final.reshape(1, d))

    dhg, dhu = _ffn_bwd_act(dh3_b, w_fd, hg, hu)
    gw_fd = _wgrad("gw_ffn_down", act, dh3_b)
    gw_fg = _wgrad("gw_ffn_gate", n3, dhg)
    gw_fu = _wgrad("gw_ffn_up", n3, dhu)
    dn3 = _mm_w("dn_ffn_gate", dhg, w_fg, F32, dims=NT, tm=ROW_TILE)
    dn3 = _mm_w("dn_ffn_up", dhu, w_fu, F32, dims=NT, res=dn3, tm=ROW_TILE)
    dh2, dh2_b, dg_ffn = _rms_bwd("norm_ffn_bwd", dn3, h2, g_ffn, dh3, ("f32", "bf16"))

    do_m = _mm_w("mem_out_bwd", dh2_b, w_o, BF16, dims=NT)
    gw_o = _wgrad("gw_mem_o", o_m, dh2_b)
    dq_m, dkv_m = _mem_bwd(q_m, kv_m, do_m, batch, s, n_mem)
    gw_q = _wgrad("gw_mem_q", n2, dq_m)
    gw_kv = _wgrad("gw_mem_kv", mem_n, dkv_m)
    dn2 = _mm_w("mem_q_bwd", dq_m, w_q, F32, dims=NT)
    dmem_n = _mm_w("mem_kv_bwd", dkv_m, w_kv, F32, dims=NT)
    (dg_mem_kv,) = _rms_bwd("norm_mem_kv_bwd", dmem_n, mem, g_mem_kv, None, ())
    dh1, dh1_b, dg_mem_q = _rms_bwd("norm_mem_q_bwd", dn2, h1, g_mem_q, dh2, ("f32", "bf16"))

    dmix = _mm_w("mix_out_bwd", dh1_b, w_out, BF16, dims=NT)
    gw_out = _wgrad("gw_out", mixed, dh1_b)
    dua, dub, dgates = _mixer_bwd(dmix, ua, ub, proj, 6 * nb)
    do_a = _mm_w("up_a_bwd", dua, w_up_a, BF16, dims=NT)
    do_b = _mm_w("up_b_bwd", dub, w_up_b, BF16, dims=NT)
    gw_ua = _wgrad("gw_up_a", o_a, dua)
    gw_ub = _wgrad("gw_up_b", o_b, dub)
    (dq_ar, dk_ar, dv_a), (p_fg, p_fd) = _da_bwd(
        qk_a, proj, 2 * nb, bias, o_a, lse_a, do_a, batch, s,
        ride=([gw_fg, gw_fd.reshape(n_sh, -1, d)], False, (cs_ffn, 0)))
    dqk_a = _rope_apply("rope_bwd", jnp.concatenate([dq_ar, dk_ar], axis=1), 0, 2 * nb, cos_t, sin_a, sin_b, -1.0)
    mid = [gw_ua, gw_ub, gw_out.reshape(n_sh, -1, d), gw_q.reshape(n_sh, -1, gw_q.shape[1]),
           gw_kv.reshape(n_sh, -1, gw_kv.shape[1]), gw_o, gw_fu]
    (dq_b, dk_b, dv_b), (*p_mid, p_fu) = _sb_bwd(proj, 3 * nb, 4 * nb, 5 * nb, tot_b, do_b, batch, s,
                                                 ride=(mid, False, (cs_up, cs_up, 0, 0, 0, cs_up, cs_ffn)))
    p_ffn = [p_fg, p_fu, p_fd]
    dproj = jnp.concatenate([dqk_a, dv_a, dq_b, dk_b, dv_b, dgates], axis=1)
    dn1 = _mm_cols_t("proj_in_bwd", dproj, w_in, F32)
    p_in = _gw_in_scatter(n1, dproj)
    grad_x, dg_mix = _rms_bwd("norm_mix_bwd", dn1, x, g_mix, dh1, ("f32",))
    return loss_part, grad_x, [p_in] + list(p_mid) + p_ffn, (dg_mix, dg_mem_q, dg_mem_kv, dg_ffn, dg_final)


WEIGHTS =("w_in", "w_up_a", "w_up_b", "w_out", "w_q_mem", "w_kv_mem", "w_o_mem", "w_ffn_gate", "w_ffn_up", "w_ffn_down")
GAINS = ("g_mix", "g_mem_q", "g_mem_kv", "g_ffn", "g_final")
ORDER = ("g_mix", "w_in", "w_up_a", "w_up_b", "w_out", "g_mem_q", "g_mem_kv", "w_q_mem", "w_kv_mem", "w_o_mem", "g_ffn",
         "w_ffn_gate", "w_ffn_up", "w_ffn_down", "g_final")


def kernel(x, mem, positions, g_mix, w_in, w_up_a, w_up_b, w_out, g_mem_q, g_mem_kv, w_q_mem, w_kv_mem, w_o_mem, g_ffn, w_ffn_gate, w_ffn_up, w_ffn_down, g_final, loss_target, m_g_mix, m_w_in, m_w_up_a, m_w_up_b, m_w_out, m_g_mem_q, m_g_mem_kv, m_w_q_mem, m_w_kv_mem, m_w_o_mem, m_g_ffn, m_w_ffn_gate, m_w_ffn_up, m_w_ffn_down, m_g_final, v_g_mix, v_w_in, v_w_up_a, v_w_up_b, v_w_out, v_g_mem_q, v_g_mem_kv, v_w_q_mem, v_w_kv_mem, v_w_o_mem, v_g_ffn, v_w_ffn_gate, v_w_ffn_up, v_w_ffn_down, v_g_final):
    given = dict(locals())
    batch, s, d = x.shape
    t = batch * s
    shard = {n: given[n].reshape(given[n].shape[-2:]) for n in WEIGHTS}
    gains = [given[n].reshape(1, d) for n in GAINS]

    pad = (-shard["w_ffn_down"].shape[0]) % LANES
    pads = {"w_ffn_gate": (0, pad), "w_ffn_up": (0, pad), "w_ffn_down": (pad, 0)}
    cast = _cast_weights([shard[n] for n in WEIGHTS], [pads.get(n, (0, 0)) for n in WEIGHTS])
    loss_part, grad_x, pieces, dgains = _local_step(
        x.reshape(t, d), mem.reshape(-1, d), positions.reshape(t, 1), loss_target.reshape(t, d), gains, cast[0],
        cast[1:], batch)

    grad, delta, new_m, new_v = {}, {}, {}, {}
    for n, p in zip(WEIGHTS, pieces):
        m2, v2 = given["m_" + n].reshape(shard[n].shape), given["v_" + n].reshape(shard[n].shape)
        outs = _adam("adam_" + n, p, shard[n], m2, v2)
        grad[n], delta[n], new_m[n], new_v[n] = [o.reshape(given[n].shape) for o in outs]

    g_all = _allreduce_small(list(dgains), loss_part)
    small = _adam_small(g_all, gains, [given["m_" + n].reshape(1, d) for n in GAINS],
                        [given["v_" + n].reshape(1, d) for n in GAINS])
    for out, vals in zip((grad, delta, new_m, new_v), small):
        for n, val in zip(GAINS, vals):
            out[n] = val.reshape(given[n].shape)

    loss = g_all[SMALL_ROWS - 1, 0]
    return (loss, grad_x.reshape(x.shape), *[grad[n] for n in ORDER], *[delta[n] for n in ORDER],
            *[new_m[n] for n in ORDER], *[new_v[n] for n in ORDER])
```

```python
import functools
import math

import jax
import jax.numpy as jnp
import numpy as np
from jax import lax
from jax.experimental import pallas as pl
from jax.experimental.pallas import tpu as pltpu

F32 = jnp.float32
BF16 = jnp.bfloat16

N_DEV = 8
HEAD_DIM = 64
MEM_HEAD_DIM = 128
N_HEADS_MEM = 4
BLOCK = 128
DIL_PATTERNS = ((128, 1), (512, 4), (2048, 16))
ROPE_THETA = 500000.0
ROPE_HALF = 8
RMS_EPS = 1e-6
ADAM_LR, ADAM_B1, ADAM_B2, ADAM_EPS, ADAM_WD, ADAM_STEP = 0.001, 0.9, 0.999, 1e-08, 0.01, 10
NEG = -1e30
ROW_TILE = 512
LANES = 128

ANY = pl.BlockSpec(memory_space=pl.ANY)
VMEM = pl.BlockSpec(memory_space=pltpu.VMEM)
NN = (((1,), (0,)), ((), ()))
NT = (((1,), (1,)), ((), ()))
TN = (((0,), (0,)), ((), ()))


def _params(sem):
    return pltpu.CompilerParams(dimension_semantics=sem)


def _mm(name, a, b, *, grid, a_spec, b_spec, o_shape, o_spec, dims, out_dtype, nk=1, res=None, res_spec=None):
    has_res = res is not None

    def body(*refs):
        a_ref, b_ref = refs[0], refs[1]
        r_ref = refs[2] if has_res else None
        o_ref = refs[3] if has_res else refs[2]
        p = lax.dot_general(a_ref[...], b_ref[...], dims, preferred_element_type=F32)
        if nk == 1:
            if has_res:
                p = p + r_ref[...].astype(F32)
            o_ref[...] = p.astype(out_dtype)
            return
        acc_ref = refs[-1]
        k = pl.program_id(len(grid) - 1)

        @pl.when(k == 0)
        def _():
            acc_ref[...] = p

        @pl.when(k > 0)
        def _():
            acc_ref[...] += p

        @pl.when(k == nk - 1)
        def _():
            t = acc_ref[...]
            if has_res:
                t = t + r_ref[...].astype(F32)
            o_ref[...] = t.astype(out_dtype)

    o_block = tuple(d for d in o_spec.block_shape if d is not None)
    sem = ("parallel",) * (len(grid) - 1) + (("arbitrary",) if nk > 1 else ("parallel",))
    return pl.pallas_call(
        body, name=name, grid=grid,
        in_specs=[a_spec, b_spec] + ([res_spec] if has_res else []),
        out_specs=o_spec, out_shape=jax.ShapeDtypeStruct(o_shape, out_dtype),
        scratch_shapes=[pltpu.VMEM(o_block, F32)] if nk > 1 else [],
        compiler_params=_params(sem),
    )(*([a, b] + ([res] if has_res else [])))


def _rms_fwd(name, x, g):
    t, d = x.shape
    tm = min(ROW_TILE, t)

    def body(x_ref, g_ref, o_ref):
        xf = x_ref[...]
        r = lax.rsqrt(jnp.mean(xf * xf, axis=-1, keepdims=True) + RMS_EPS)
        o_ref[...] = (xf * r * g_ref[...]).astype(BF16)

    return pl.pallas_call(
        body, name=name, grid=(t // tm,),
        in_specs=[pl.BlockSpec((tm, d), lambda i: (i, 0)), pl.BlockSpec((1, d), lambda i: (0, 0))],
        out_specs=pl.BlockSpec((tm, d), lambda i: (i, 0)), out_shape=jax.ShapeDtypeStruct((t, d), BF16),
        compiler_params=_params(("parallel",)),
    )(x, g)


def _rms_bwd(name, dn, x, g, dres, want):
    t, d = x.shape
    tm = min(ROW_TILE, t)
    has_res = dres is not None

    def body(*refs):
        dn_ref, x_ref, g_ref = refs[0], refs[1], refs[2]
        r_ref = refs[3] if has_res else None
        dx_refs, dg_ref = refs[-1 - len(want):-1], refs[-1]
        xf = x_ref[...]
        r = lax.rsqrt(jnp.mean(xf * xf, axis=-1, keepdims=True) + RMS_EPS)
        xh = xf * r
        dnf = dn_ref[...].astype(F32)
        if want:
            dxh = dnf * g_ref[...]
            dx = r * (dxh - xh * jnp.mean(dxh * xh, axis=-1, keepdims=True))
            if has_res:
                dx = dx + r_ref[...]
            for kind, dx_ref in zip(want, dx_refs):
                dx_ref[...] = dx.astype(F32 if kind == "f32" else BF16)

        @pl.when(pl.program_id(0) == 0)
        def _():
            dg_ref[...] = jnp.zeros_like(dg_ref)

        dg_ref[...] += jnp.sum(dnf * xh, axis=0, keepdims=True)

    row = pl.BlockSpec((tm, d), lambda i: (i, 0))
    vec = pl.BlockSpec((1, d), lambda i: (0, 0))
    return pl.pallas_call(
        body, name=name, grid=(t // tm,),
        in_specs=[row, row, vec] + ([row] if has_res else []),
        out_specs=[row] * len(want) + [vec],
        out_shape=[jax.ShapeDtypeStruct((t, d), F32 if kind == "f32" else BF16) for kind in want]
        + [jax.ShapeDtypeStruct((1, d), F32)],
        compiler_params=_params(("arbitrary",)),
    )(*([dn, x, g] + ([dres] if has_res else [])))


def _loss_head(h, tgt, g):
    t, d = h.shape
    tm = min(ROW_TILE, t)

    def body(h_ref, t_ref, g_ref, loss_ref, dh_ref, dhb_ref, dg_ref):
        xf = h_ref[...]
        gv = g_ref[...]
        r = lax.rsqrt(jnp.mean(xf * xf, axis=-1, keepdims=True) + RMS_EPS)
        xh = xf * r
        e = xh * gv - t_ref[...]
        dy = e * (1.0 / d)
        dxh = dy * gv
        dh = r * (dxh - xh * jnp.mean(dxh * xh, axis=-1, keepdims=True))
        dh_ref[...] = dh
        dhb_ref[...] = dh.astype(BF16)

        @pl.when(pl.program_id(0) == 0)
        def _():
            dg_ref[...] = jnp.zeros_like(dg_ref)
            loss_ref[...] = jnp.zeros_like(loss_ref)

        dg_ref[...] += jnp.sum(dy * xh, axis=0, keepdims=True)
        part = jnp.sum(jnp.sum(e * e, axis=1, keepdims=True), axis=0, keepdims=True) * (0.5 / d)
        loss_ref[...] += jnp.broadcast_to(part, loss_ref.shape)

    row = pl.BlockSpec((tm, d), lambda i: (i, 0))
    vec = pl.BlockSpec((1, d), lambda i: (0, 0))
    return pl.pallas_call(
        body, name="loss_head", grid=(t // tm,),
        in_specs=[row, row, vec],
        out_specs=[pl.BlockSpec((8, LANES), lambda i: (0, 0)), row, row, vec],
        out_shape=[jax.ShapeDtypeStruct((8, LANES), F32), jax.ShapeDtypeStruct((t, d), F32),
                   jax.ShapeDtypeStruct((t, d), BF16), jax.ShapeDtypeStruct((1, d), F32)],
        compiler_params=_params(("arbitrary",)),
    )(h, tgt, g)


def _rope_tables(pos, inv_freq, sel_lo, sel_hi):
    t = pos.shape[0]
    tm = min(ROW_TILE, t)

    def body(p_ref, f_ref, lo_ref, hi_ref, c_ref, sa_ref, sb_ref):
        ang = p_ref[...].astype(F32) * f_ref[...]
        rot = lo_ref[...] + hi_ref[...]
        cs, sn = jnp.cos(ang), jnp.sin(ang)
        c_ref[...] = cs * rot + (1.0 - rot)
        sa_ref[...] = -sn * lo_ref[...]
        sb_ref[...] = sn * hi_ref[...]

    vec = pl.BlockSpec((1, LANES), lambda i: (0, 0))
    row = pl.BlockSpec((tm, LANES), lambda i: (i, 0))
    return pl.pallas_call(
        body, name="rope_tables", grid=(t // tm,),
        in_specs=[pl.BlockSpec((tm, 1), lambda i: (i, 0)), vec, vec, vec],
        out_specs=[row, row, row], out_shape=[jax.ShapeDtypeStruct((t, LANES), F32)] * 3,
        compiler_params=_params(("parallel",)),
    )(pos, inv_freq, sel_lo, sel_hi)


def _rope_apply(name, src, col0, n_cols, cos_t, sin_a, sin_b, sign):
    t = src.shape[0]
    tm = min(ROW_TILE, t)

    def body(x_ref, c_ref, sa_ref, sb_ref, o_ref):
        cs, sa, sb = c_ref[...], sign * sa_ref[...], sign * sb_ref[...]
        for c in range(n_cols):
            cols = slice(c * LANES, (c + 1) * LANES)
            xf = x_ref[:, cols].astype(F32)
            up = pltpu.roll(xf, LANES - ROPE_HALF, 1)
            dn = pltpu.roll(xf, ROPE_HALF, 1)
            o_ref[:, cols] = (xf * cs + up * sa + dn * sb).astype(BF16)

    wide = n_cols * LANES
    tab = pl.BlockSpec((tm, LANES), lambda i: (i, 0))
    return pl.pallas_call(
        body, name=name, grid=(t // tm,),
        in_specs=[pl.BlockSpec((tm, wide), lambda i: (i, col0 // n_cols)), tab, tab, tab],
        out_specs=pl.BlockSpec((tm, wide), lambda i: (i, 0)),
        out_shape=jax.ShapeDtypeStruct((t, wide), BF16),
        compiler_params=_params(("parallel",)),
    )(src, cos_t, sin_a, sin_b)


DA_T = 256
FWD_STREAMS = 4
BWD_STREAMS = 2


def _lane_lo():
    return lax.broadcasted_iota(jnp.int32, (BLOCK, LANES), 1) < HEAD_DIM


def _dilated_bias_tiles(s):
    n = s // DA_T
    dist = (np.arange(n)[:, None, None] * DA_T + np.arange(DA_T)[None, :, None] - np.arange(DA_T)[None, None, :])
    cnt = np.zeros(dist.shape, np.float32)
    for window, dil in DIL_PATTERNS:
        cnt += ((dist >= 0) & (dist % dil == 0) & (dist <= window)).astype(np.float32)
    return jnp.asarray(np.where(cnt > 0, np.log(np.maximum(cnt, 1.0)), NEG).astype(np.float32))


def _stack_heads(x, lo):
    zero = jnp.zeros_like(x)
    return jnp.concatenate([jnp.where(lo, x, zero), jnp.where(lo, zero, x)], axis=0)


def _da_fwd(qk, proj, v_col0, bias, batch, s, ride=None, streams=FWD_STREAMS):
    t = qk.shape[0]
    nq = s // DA_T
    n_pairs = 4
    ns = streams
    wide = ns * LANES
    scale = HEAD_DIM ** -0.5

    def body(q_ref, k_ref, v_ref, b_ref, o_ref, lse_ref, acc_ref, m_ref, l_ref):
        i = pl.program_id(2)
        lo = lax.broadcasted_iota(jnp.int32, (DA_T, LANES), 1) < HEAD_DIM
        ones = jnp.ones((DA_T, LANES), BF16)
        acc_ref[...] = jnp.zeros_like(acc_ref)
        m_ref[...] = jnp.full(m_ref.shape, NEG, F32)
        l_ref[...] = jnp.zeros_like(l_ref)
        qqs = [_stack_heads(q_ref[:, st * LANES:(st + 1) * LANES] * scale, lo) for st in range(ns)]

        def scores(st, rows, bias2):
            k = k_ref[rows, st * LANES:(st + 1) * LANES]
            return lax.dot_general(qqs[st], k, NT, preferred_element_type=F32) + bias2

        def softmax(st, sc):
            m_old = m_ref[st]
            m_new = jnp.maximum(m_old, jnp.max(sc, axis=1, keepdims=True))
            m_ref[st] = m_new
            return jnp.exp(sc - m_new).astype(BF16), jnp.exp(m_old - m_new)

        def values(st, rows, p, alpha):
            v = v_ref[rows, st * LANES:(st + 1) * LANES]
            vz = jnp.zeros_like(v)
            l_ref[st] = alpha * l_ref[st] + lax.dot_general(p, ones, NN, preferred_element_type=F32)
            pv = (lax.dot_general(p[:DA_T], jnp.where(lo, v, vz), NN, preferred_element_type=F32)
                  + lax.dot_general(p[DA_T:], jnp.where(lo, vz, v), NN, preferred_element_type=F32))
            acc_ref[st] = acc_ref[st] * jnp.where(lo, alpha[:DA_T], alpha[DA_T:]) + pv

        def trip(dlt, carry):
            rows = pl.ds(pl.multiple_of((i - dlt) * DA_T, DA_T), DA_T)
            bias_t = b_ref[dlt]
            bias2 = jnp.concatenate([bias_t, bias_t], axis=0)
            scs = [scores(st, rows, bias2) for st in range(ns)]
            pas = [softmax(st, scs[st]) for st in range(ns)]
            for st in range(ns):
                values(st, rows, *pas[st])
            return carry

        lax.fori_loop(0, i + 1, trip, 0)
        for st in range(ns):
            cols = slice(st * LANES, (st + 1) * LANES)
            l_t = l_ref[st]
            o_ref[:, cols] = (acc_ref[st] / jnp.where(lo, l_t[:DA_T], l_t[DA_T:])).astype(BF16)
            lse = m_ref[st] + jnp.log(l_t)
            lse_ref[:, cols] = jnp.where(lo, lse[:DA_T], lse[DA_T:])

    blk = pl.BlockSpec((DA_T, wide), lambda b, h, i: (b * nq + i, h))
    return _call(
        body, name="attn_a_fwd", grid=(batch, n_pairs // ns, nq),
        in_specs=[blk,
                  pl.BlockSpec((s, wide), lambda b, h, i: (b, n_pairs // ns + h)),
                  pl.BlockSpec((s, wide), lambda b, h, i: (b, v_col0 // ns + h)),
                  pl.BlockSpec((nq, DA_T, DA_T), lambda b, h, i: (0, 0, 0))],
        out_specs=[blk, blk],
        out_shape=[jax.ShapeDtypeStruct((t, n_pairs * LANES), BF16), jax.ShapeDtypeStruct((t, n_pairs * LANES), F32)],
        scratch=[pltpu.VMEM((ns, DA_T, LANES), F32), pltpu.VMEM((ns, 2 * DA_T, 1), F32),
                 pltpu.VMEM((ns, 2 * DA_T, LANES), F32)],
        sem=("parallel", "parallel", "arbitrary"), args=(qk, qk, proj, bias), ride=ride)


def _da_bwd(qk, proj, v_col0, bias, o, lse, do, batch, s, ride=None, streams=BWD_STREAMS):
    t = qk.shape[0]
    nq = s // DA_T
    n_pairs = 4
    ns = streams
    wide = ns * LANES
    scale = HEAD_DIM ** -0.5

    def body(q_ref, k_ref, v_ref, b_ref, o_ref, lse_ref, do_ref, dq_ref, dk_ref, dv_ref, dk_acc, dv_acc, dq_acc):
        i = pl.program_id(2)
        lo = lax.broadcasted_iota(jnp.int32, (DA_T, LANES), 1) < HEAD_DIM

        @pl.when(i == 0)
        def _():
            dk_acc[...] = jnp.zeros_like(dk_acc)
            dv_acc[...] = jnp.zeros_like(dv_acc)

        dq_acc[...] = jnp.zeros_like(dq_acc)
        qqs, dds, deltas, lses = [], [], [], []
        for st in range(ns):
            cols = slice(st * LANES, (st + 1) * LANES)
            do_ = do_ref[:, cols]
            qqs.append(_stack_heads(q_ref[:, cols] * scale, lo))
            dds.append(_stack_heads(do_, lo))
            prod = do_.astype(F32) * o_ref[:, cols].astype(F32)
            fz = jnp.zeros_like(prod)
            deltas.append(jnp.concatenate([jnp.sum(jnp.where(lo, prod, fz), axis=1, keepdims=True),
                                           jnp.sum(jnp.where(lo, fz, prod), axis=1, keepdims=True)], axis=0))
            lse_t = lse_ref[:, cols]
            lses.append(jnp.concatenate([lse_t[:, 0:1], lse_t[:, HEAD_DIM:HEAD_DIM + 1]], axis=0))

        def products(st, rows, bias2):
            cols = slice(st * LANES, (st + 1) * LANES)
            sc = lax.dot_general(qqs[st], k_ref[rows, cols], NT, preferred_element_type=F32) + bias2
            return sc, lax.dot_general(dds[st], v_ref[rows, cols], NT, preferred_element_type=F32)

        def weights(st, sc, dp):
            p = jnp.exp(sc - lses[st])
            return (p * (dp - deltas[st])).astype(BF16), p.astype(BF16)

        def gradients(st, rows, ds, p):
            cols = slice(st * LANES, (st + 1) * LANES)
            k = k_ref[rows, cols]
            kz = jnp.zeros_like(k)
            dq_acc[st] += (lax.dot_general(ds[:DA_T], jnp.where(lo, k, kz), NN, preferred_element_type=F32)
                           + lax.dot_general(ds[DA_T:], jnp.where(lo, kz, k), NN, preferred_element_type=F32))
            dk_acc[rows, cols] += lax.dot_general(ds, qqs[st], TN, preferred_element_type=F32)
            dv_acc[rows, cols] += lax.dot_general(p, dds[st], TN, preferred_element_type=F32)

        def trip(dlt, carry):
            rows = pl.ds(pl.multiple_of((i - dlt) * DA_T, DA_T), DA_T)
            bias_t = b_ref[dlt]
            bias2 = jnp.concatenate([bias_t, bias_t], axis=0)
            prods = [products(st, rows, bias2) for st in range(ns)]
            wts = [weights(st, *prods[st]) for st in range(ns)]
            for st in range(ns):
                gradients(st, rows, *wts[st])
            return carry

        lax.fori_loop(0, i + 1, trip, 0)
        for st in range(ns):
            dq_ref[:, st * LANES:(st + 1) * LANES] = (dq_acc[st] * scale).astype(BF16)

        @pl.when(i == nq - 1)
        def _():
            dk_ref[...] = dk_acc[...].astype(BF16)
            dv_ref[...] = dv_acc[...].astype(BF16)

    blk = pl.BlockSpec((DA_T, wide), lambda b, h, i: (b * nq + i, h))
    seq = pl.BlockSpec((s, wide), lambda b, h, i: (b, h))
    out = jax.ShapeDtypeStruct((t, n_pairs * LANES), BF16)
    return _call(
        body, name="attn_a_bwd", grid=(batch, n_pairs // ns, nq),
        in_specs=[blk,
                  pl.BlockSpec((s, wide), lambda b, h, i: (b, n_pairs // ns + h)),
                  pl.BlockSpec((s, wide), lambda b, h, i: (b, v_col0 // ns + h)),
                  pl.BlockSpec((nq, DA_T, DA_T), lambda b, h, i: (0, 0, 0)),
                  blk, blk, blk],
        out_specs=[blk, seq, seq], out_shape=[out, out, out],
        scratch=[pltpu.VMEM((s, wide), F32), pltpu.VMEM((s, wide), F32), pltpu.VMEM((ns, DA_T, LANES), F32)],
        sem=("parallel", "parallel", "arbitrary"), args=(qk, qk, proj, bias, o, lse, do), ride=ride)


SB_Q = 256


def _sb_consts(after):
    r = lax.broadcasted_iota(jnp.int32, (2 * BLOCK, 2 * BLOCK), 0) % BLOCK
    c = lax.broadcasted_iota(jnp.int32, (2 * BLOCK, 2 * BLOCK), 1)
    tri = (r > c) if after else (r < c)
    return jnp.logical_or(c >= BLOCK, tri).astype(BF16)


def _split(x):
    hi = x.astype(BF16)
    lo = (x - hi.astype(F32)).astype(BF16)
    return jnp.concatenate([hi, lo], axis=1)


def _sb_fwd(proj, q_col0, k_col0, v_col0, batch, s, ride=None, streams=FWD_STREAMS):
    t = proj.shape[0]
    nq = s // SB_Q
    n_pairs = 4
    ns = streams
    wide = ns * LANES
    scale = HEAD_DIM ** -0.5

    def body(q_ref, k_ref, v_ref, o_ref, tot_ref, acc_ref, run_ref):
        i = pl.program_id(2)
        lo_q = lax.broadcasted_iota(jnp.int32, (SB_Q, LANES), 1) < HEAD_DIM
        lo_k = _lane_lo()
        mat = _sb_consts(True)
        row = lax.broadcasted_iota(jnp.int32, (2 * SB_Q, LANES), 0) % SB_Q
        ahead = row - lax.broadcasted_iota(jnp.int32, (2 * SB_Q, LANES), 1)
        acc_ref[...] = jnp.zeros_like(acc_ref)
        run_ref[...] = jnp.zeros_like(run_ref)
        qqs = [_stack_heads(q_ref[:, st * LANES:(st + 1) * LANES] * scale, lo_q) for st in range(ns)]

        def units(todo):
            def rows(j):
                return pl.ds(pl.multiple_of(j * BLOCK, BLOCK), BLOCK)

            zs = [lax.dot_general(qqs[st], k_ref[rows(j), st * LANES:(st + 1) * LANES], NT, preferred_element_type=F32)
                  for st, j, _ in todo]
            logs = []
            for z, (_, _, off) in zip(zs, todo):
                lsig = jnp.minimum(z, 0.0) - jnp.log(1.0 + jnp.exp(-jnp.abs(z)))
                lneg = lsig - z
                if off is not None:
                    lneg = jnp.where(ahead > off, lneg, 0.0)
                logs.append((lsig, _split(lneg)))
            sums = [lax.dot_general(cat, mat, NN, preferred_element_type=F32) for _, cat in logs]
            probs = []
            for (lsig, _), sm, (st, _, off) in zip(logs, sums, todo):
                run = run_ref[st]
                a = jnp.exp(lsig + run + sm[:, :BLOCK])
                if off is not None:
                    a = jnp.where(ahead > off, a, 0.0)
                run_ref[st] = run + sm[:, BLOCK:]
                probs.append(a.astype(BF16))
            for ab, (st, j, _) in zip(probs, todo):
                v = v_ref[rows(j), st * LANES:(st + 1) * LANES]
                vz = jnp.zeros_like(v)
                acc_ref[st] += (lax.dot_general(ab[:SB_Q], jnp.where(lo_k, v, vz), NN, preferred_element_type=F32)
                                + lax.dot_general(ab[SB_Q:], jnp.where(lo_k, vz, v), NN, preferred_element_type=F32))

        units([(st, 2 * i + 1, BLOCK) for st in range(ns)] + [(st, 2 * i, 0) for st in range(ns)])

        def pair(p, carry):
            jp = i - 1 - p
            units([(st, 2 * jp + 1, None) for st in range(ns)] + [(st, 2 * jp, None) for st in range(ns)])
            return carry

        lax.fori_loop(0, i, pair, 0)
        for st in range(ns):
            cols = slice(st * LANES, (st + 1) * LANES)
            o_ref[:, cols] = acc_ref[st].astype(BF16)
            tot_ref[:, cols] = jnp.where(lo_q, run_ref[st, 0:SB_Q, :], run_ref[st, SB_Q:2 * SB_Q, :])

    def seq(col0):
        return pl.BlockSpec((s, wide), lambda b, h, i: (b, col0 // ns + h))

    blk = pl.BlockSpec((SB_Q, wide), lambda b, h, i: (b * nq + i, h))
    return _call(
        body, name="attn_b_fwd", grid=(batch, n_pairs // ns, nq),
        in_specs=[pl.BlockSpec((SB_Q, wide), lambda b, h, i: (b * nq + i, q_col0 // ns + h)), seq(k_col0), seq(v_col0)],
        out_specs=[blk, blk],
        out_shape=[jax.ShapeDtypeStruct((t, n_pairs * LANES), BF16), jax.ShapeDtypeStruct((t, n_pairs * LANES), F32)],
        scratch=[pltpu.VMEM((ns, SB_Q, LANES), F32), pltpu.VMEM((ns, 2 * SB_Q, LANES), F32)],
        sem=("parallel", "parallel", "arbitrary"), args=(proj, proj, proj), ride=ride)


def _sb_bwd(proj, q_col0, k_col0, v_col0, tot, do, batch, s, ride=None, streams=BWD_STREAMS):
    t = proj.shape[0]
    nq = s // SB_Q
    n_pairs = 4
    ns = streams
    wide = ns * LANES
    scale = HEAD_DIM ** -0.5

    def body(q_ref, k_ref, v_ref, tot_ref, do_ref, dq_ref, dk_ref, dv_ref, dk_acc, dv_acc, dq_acc, seen_ref, gsum_ref):
        i = pl.program_id(2)
        lo_q = lax.broadcasted_iota(jnp.int32, (SB_Q, LANES), 1) < HEAD_DIM
        lo_k = _lane_lo()

        @pl.when(i == 0)
        def _():
            dk_acc[...] = jnp.zeros_like(dk_acc)
            dv_acc[...] = jnp.zeros_like(dv_acc)

        mat_after = _sb_consts(True)
        mat_before = _sb_consts(False)
        row = lax.broadcasted_iota(jnp.int32, (2 * SB_Q, LANES), 0) % SB_Q
        ahead = row - lax.broadcasted_iota(jnp.int32, (2 * SB_Q, LANES), 1)
        dq_acc[...] = jnp.zeros_like(dq_acc)
        seen_ref[...] = jnp.zeros_like(seen_ref)
        gsum_ref[...] = jnp.zeros_like(gsum_ref)
        qqs, dds, totals = [], [], []
        for st in range(ns):
            cols = slice(st * LANES, (st + 1) * LANES)
            qqs.append(_stack_heads(q_ref[:, cols] * scale, lo_q))
            dds.append(_stack_heads(do_ref[:, cols], lo_q))
            tot_t = tot_ref[:, cols]
            totals.append(jnp.concatenate([jnp.broadcast_to(tot_t[:, 0:1], (SB_Q, LANES)),
                                           jnp.broadcast_to(tot_t[:, HEAD_DIM:HEAD_DIM + 1], (SB_Q, LANES))], axis=0))

        def units(todo):
            def rows(j):
                return pl.ds(pl.multiple_of(j * BLOCK, BLOCK), BLOCK)

            def cols(st):
                return slice(st * LANES, (st + 1) * LANES)

            prods = [(lax.dot_general(qqs[st], k_ref[rows(j), cols(st)], NT, preferred_element_type=F32),
                      lax.dot_general(dds[st], v_ref[rows(j), cols(st)], NT, preferred_element_type=F32))
                     for st, j, _ in todo]
            logs = []
            for (z, _), (_, _, off) in zip(prods, todo):
                lsig = jnp.minimum(z, 0.0) - jnp.log(1.0 + jnp.exp(-jnp.abs(z)))
                lneg = lsig - z
                if off is not None:
                    lneg = jnp.where(ahead > off, lneg, 0.0)
                logs.append((lsig, _split(lneg)))
            sums = [lax.dot_general(cat, mat_after, NN, preferred_element_type=F32) for _, cat in logs]
            gates = []
            for (lsig, _), sm, (_, da), (st, _, off) in zip(logs, sums, prods, todo):
                seen = seen_ref[st]
                a = jnp.exp(lsig + (totals[st] - seen - sm[:, BLOCK:]) + sm[:, :BLOCK])
                if off is not None:
                    a = jnp.where(ahead > off, a, 0.0)
                seen_ref[st] = seen + sm[:, BLOCK:]
                g = a * da
                gates.append((a.astype(BF16), g, _split(g)))
            gsums = [lax.dot_general(cat, mat_before, NN, preferred_element_type=F32) for _, _, cat in gates]
            outs = []
            for (lsig, _), (ab, g, _), gs, (st, _, off) in zip(logs, gates, gsums, todo):
                gsum = gsum_ref[st]
                dz = g - jnp.exp(lsig) * (g + gsum + gs[:, :BLOCK])
                if off is not None:
                    dz = jnp.where(ahead > off, dz, 0.0)
                gsum_ref[st] = gsum + gs[:, BLOCK:]
                outs.append((dz.astype(BF16), ab))
            for (dzb, ab), (st, j, _) in zip(outs, todo):
                k = k_ref[rows(j), cols(st)]
                kz = jnp.zeros_like(k)
                dq_acc[st] += (lax.dot_general(dzb[:SB_Q], jnp.where(lo_k, k, kz), NN, preferred_element_type=F32)
                               + lax.dot_general(dzb[SB_Q:], jnp.where(lo_k, kz, k), NN, preferred_element_type=F32))
                dk_acc[rows(j), cols(st)] += lax.dot_general(dzb, qqs[st], TN, preferred_element_type=F32)
                dv_acc[rows(j), cols(st)] += lax.dot_general(ab, dds[st], TN, preferred_element_type=F32)

        def pair(p, carry):
            units([(st, 2 * p, None) for st in range(ns)] + [(st, 2 * p + 1, None) for st in range(ns)])
            return carry

        lax.fori_loop(0, i, pair, 0)
        units([(st, 2 * i, 0) for st in range(ns)] + [(st, 2 * i + 1, BLOCK) for st in range(ns)])
        for st in range(ns):
            dq_ref[:, st * LANES:(st + 1) * LANES] = (dq_acc[st] * scale).astype(BF16)

        @pl.when(i == nq - 1)
        def _():
            dk_ref[...] = dk_acc[...].astype(BF16)
            dv_ref[...] = dv_acc[...].astype(BF16)

    def seq_in(col0):
        return pl.BlockSpec((s, wide), lambda b, h, i: (b, col0 // ns + h))

    blk = pl.BlockSpec((SB_Q, wide), lambda b, h, i: (b * nq + i, h))
    seq = pl.BlockSpec((s, wide), lambda b, h, i: (b, h))
    out = jax.ShapeDtypeStruct((t, n_pairs * LANES), BF16)
    return _call(
        body, name="attn_b_bwd", grid=(batch, n_pairs // ns, nq),
        in_specs=[pl.BlockSpec((SB_Q, wide), lambda b, h, i: (b * nq + i, q_col0 // ns + h)), seq_in(k_col0),
                  seq_in(v_col0), blk, blk],
        out_specs=[blk, seq, seq], out_shape=[out, out, out],
        scratch=[pltpu.VMEM((s, wide), F32), pltpu.VMEM((s, wide), F32), pltpu.VMEM((ns, SB_Q, LANES), F32),
                 pltpu.VMEM((ns, 2 * SB_Q, LANES), F32), pltpu.VMEM((ns, 2 * SB_Q, LANES), F32)],
        sem=("parallel", "parallel", "arbitrary"), args=(proj, proj, proj, tot, do), ride=ride)


MEM_Q_TILE = 256


def _mem_fwd(q, kv, batch, s, n_mem):
    t, width = q.shape
    tq = min(MEM_Q_TILE, s)
    nq = s // tq
    scale = MEM_HEAD_DIM ** -0.5

    def body(q_ref, kv_ref, o_ref):
        for h in range(N_HEADS_MEM):
            cols = slice(h * MEM_HEAD_DIM, (h + 1) * MEM_HEAD_DIM)
            k = kv_ref[:, cols]
            v = kv_ref[:, width + h * MEM_HEAD_DIM: width + (h + 1) * MEM_HEAD_DIM]
            sc = lax.dot_general(q_ref[:, cols], k, NT, preferred_element_type=F32) * scale
            p = jnp.exp(sc - jnp.max(sc, axis=1, keepdims=True))
            p = p / jnp.sum(p, axis=1, keepdims=True)
            o_ref[:, cols] = lax.dot_general(p.astype(BF16), v, NN, preferred_element_type=F32).astype(BF16)

    return pl.pallas_call(
        body, name="mem_attn_fwd", grid=(batch, nq),
        in_specs=[pl.BlockSpec((tq, width), lambda b, i: (b * nq + i, 0)),
                  pl.BlockSpec((n_mem, 2 * width), lambda b, i: (b, 0))],
        out_specs=pl.BlockSpec((tq, width), lambda b, i: (b * nq + i, 0)),
        out_shape=jax.ShapeDtypeStruct((t, width), BF16),
        compiler_params=_params(("parallel", "parallel")),
    )(q, kv)


def _mem_bwd(q, kv, do, batch, s, n_mem):
    t, width = q.shape
    tq = min(MEM_Q_TILE, s)
    nq = s // tq
    scale = MEM_HEAD_DIM ** -0.5

    def body(q_ref, kv_ref, do_ref, dq_ref, dkv_ref, acc):
        i = pl.program_id(1)

        @pl.when(i == 0)
        def _():
            acc[...] = jnp.zeros_like(acc)

        for h in range(N_HEADS_MEM):
            cols = slice(h * MEM_HEAD_DIM, (h + 1) * MEM_HEAD_DIM)
            vcols = slice(width + h * MEM_HEAD_DIM, width + (h + 1) * MEM_HEAD_DIM)
            qh, k, v, doh = q_ref[:, cols], kv_ref[:, cols], kv_ref[:, vcols], do_ref[:, cols]
            sc = lax.dot_general(qh, k, NT, preferred_element_type=F32) * scale
            p = jnp.exp(sc - jnp.max(sc, axis=1, keepdims=True))
            p = p / jnp.sum(p, axis=1, keepdims=True)
            dp = lax.dot_general(doh, v, NT, preferred_element_type=F32)
            ds = (p * (dp - jnp.sum(p * dp, axis=1, keepdims=True)) * scale).astype(BF16)
            dq_ref[:, cols] = lax.dot_general(ds, k, NN, preferred_element_type=F32).astype(BF16)
            acc[:, cols] += lax.dot_general(ds, qh, TN, preferred_element_type=F32)
            acc[:, vcols] += lax.dot_general(p.astype(BF16), doh, TN, preferred_element_type=F32)

        @pl.when(i == nq - 1)
        def _():
            dkv_ref[...] = acc[...].astype(BF16)

    row = pl.BlockSpec((tq, width), lambda b, i: (b * nq + i, 0))
    kvs = pl.BlockSpec((n_mem, 2 * width), lambda b, i: (b, 0))
    return pl.pallas_call(
        body, name="mem_attn_bwd", grid=(batch, nq),
        in_specs=[row, kvs, row], out_specs=[row, kvs],
        out_shape=[jax.ShapeDtypeStruct((t, width), BF16), jax.ShapeDtypeStruct((batch * n_mem, 2 * width), BF16)],
        scratch_shapes=[pltpu.VMEM((n_mem, 2 * width), F32)],
        compiler_params=_params(("parallel", "arbitrary")),
    )(q, kv, do)


def _mixer_fwd(o_a, o_b, w_a, w_b, proj, gate_col0):
    t, width = o_a.shape
    d = w_a.shape[1]
    tm = min(ROW_TILE, t)
    gb0 = gate_col0 * LANES // d

    def body(oa_ref, ob_ref, wa_ref, wb_ref, ga_ref, gb_ref, ua_ref, ub_ref, mix_ref):
        ua = lax.dot_general(oa_ref[...], wa_ref[...], NN, preferred_element_type=F32)
        ub = lax.dot_general(ob_ref[...], wb_ref[...], NN, preferred_element_type=F32)
        ua_ref[...] = ua.astype(BF16)
        ub_ref[...] = ub.astype(BF16)
        mix_ref[...] = (jax.nn.sigmoid(ga_ref[...].astype(F32)) * ua
                        + jax.nn.sigmoid(gb_ref[...].astype(F32)) * ub).astype(BF16)

    row = pl.BlockSpec((tm, width), lambda i: (i, 0))
    wsp = pl.BlockSpec((width, d), lambda i: (0, 0))
    out = pl.BlockSpec((tm, d), lambda i: (i, 0))
    osh = jax.ShapeDtypeStruct((t, d), BF16)
    return pl.pallas_call(
        body, name="mixer_fwd", grid=(t // tm,),
        in_specs=[row, row, wsp, wsp,
                  pl.BlockSpec((tm, d), lambda i: (i, gb0)), pl.BlockSpec((tm, d), lambda i: (i, gb0 + 1))],
        out_specs=[out, out, out], out_shape=[osh, osh, osh],
        compiler_params=_params(("parallel",)),
    )(o_a, o_b, w_a, w_b, proj, proj)


def _mixer_bwd(dmix, ua, ub, proj, gate_col0):
    t, d = dmix.shape
    tm = min(ROW_TILE, t)
    nc = d // LANES

    def body(dm_ref, ua_ref, ub_ref, ga_ref, gb_ref, dua_ref, dub_ref, dg_ref):
        dm = dm_ref[...].astype(F32)
        sa = jax.nn.sigmoid(ga_ref[...].astype(F32))
        sb = jax.nn.sigmoid(gb_ref[...].astype(F32))
        dua_ref[...] = (dm * sa).astype(BF16)
        dub_ref[...] = (dm * sb).astype(BF16)
        dg_ref[:, 0:d] = (dm * ua_ref[...].astype(F32) * sa * (1.0 - sa)).astype(BF16)
        dg_ref[:, d:2 * d] = (dm * ub_ref[...].astype(F32) * sb * (1.0 - sb)).astype(BF16)

    row = pl.BlockSpec((tm, d), lambda i: (i, 0))
    return pl.pallas_call(
        body, name="mixer_bwd", grid=(t // tm,),
        in_specs=[row, row, row,
                  pl.BlockSpec((tm, d), lambda i: (i, gate_col0 // nc)),
                  pl.BlockSpec((tm, d), lambda i: (i, gate_col0 // nc + 1))],
        out_specs=[row, row, pl.BlockSpec((tm, 2 * d), lambda i: (i, 0))],
        out_shape=[jax.ShapeDtypeStruct((t, d), BF16), jax.ShapeDtypeStruct((t, d), BF16),
                   jax.ShapeDtypeStruct((t, 2 * d), BF16)],
        compiler_params=_params(("parallel",)),
    )(dmix, ua, ub, proj, proj)


FFN_COLS = 1024


def _ffn_up(n, w_gate, w_up):
    t, d = n.shape
    hidden = w_gate.shape[1]
    tm = min(ROW_TILE, t)
    tn = min(FFN_COLS, hidden)

    def body(n_ref, wg_ref, wu_ref, hg_ref, hu_ref, act_ref):
        hg = lax.dot_general(n_ref[...], wg_ref[...], NN, preferred_element_type=F32)
        hu = lax.dot_general(n_ref[...], wu_ref[...], NN, preferred_element_type=F32)
        hg_ref[...] = hg.astype(BF16)
        hu_ref[...] = hu.astype(BF16)
        act_ref[...] = (hg * jax.nn.sigmoid(hg) * hu).astype(BF16)

    wsp = pl.BlockSpec((d, tn), lambda j, i: (0, j))
    out = pl.BlockSpec((tm, tn), lambda j, i: (i, j))
    osh = jax.ShapeDtypeStruct((t, hidden), BF16)
    return pl.pallas_call(
        body, name="ffn_up", grid=(hidden // tn, t // tm),
        in_specs=[pl.BlockSpec((tm, d), lambda j, i: (i, 0)), wsp, wsp],
        out_specs=[out, out, out], out_shape=[osh, osh, osh],
        compiler_params=_params(("parallel", "parallel")),
    )(n, w_gate, w_up)


def _ffn_bwd_act(dh, w_down, hg, hu):
    t, d = dh.shape
    hidden = w_down.shape[0]
    tm = min(ROW_TILE, t)
    tn = min(FFN_COLS, hidden)

    def body(dh_ref, wd_ref, hg_ref, hu_ref, dhg_ref, dhu_ref):
        dact = lax.dot_general(dh_ref[...], wd_ref[...], NT, preferred_element_type=F32)
        hg = hg_ref[...].astype(F32)
        sg = jax.nn.sigmoid(hg)
        dhu_ref[...] = (dact * hg * sg).astype(BF16)
        dhg_ref[...] = (dact * hu_ref[...].astype(F32) * sg * (1.0 + hg * (1.0 - sg))).astype(BF16)

    hid = pl.BlockSpec((tm, tn), lambda j, i: (i, j))
    osh = jax.ShapeDtypeStruct((t, hidden), BF16)
    return pl.pallas_call(
        body, name="ffn_bwd_act", grid=(hidden // tn, t // tm),
        in_specs=[pl.BlockSpec((tm, d), lambda j, i: (i, 0)), pl.BlockSpec((tn, d), lambda j, i: (j, 0)), hid, hid],
        out_specs=[hid, hid], out_shape=[osh, osh],
        compiler_params=_params(("parallel", "parallel")),
    )(dh, w_down, hg, hu)


MM_ROWS = 1024


def _mm_cols_t(name, a, w, out_dtype, res=None):
    t = a.shape[0]
    n_sh, k, cs = w.shape
    tm = min(MM_ROWS, t)
    o_spec = pl.BlockSpec((tm, k), lambda i, j: (i, 0))
    return _mm(name, a, w, grid=(t // tm, n_sh),
               a_spec=pl.BlockSpec((tm, cs), lambda i, j: (i, j)), b_spec=pl.BlockSpec((None, k, cs), lambda i, j: (j, 0, 0)),
               o_shape=(t, k), o_spec=o_spec, dims=NT, out_dtype=out_dtype, nk=n_sh, res=res,
               res_spec=o_spec if res is not None else None)


def _mm_w(name, a, w, out_dtype, dims=NN, res=None, tm=MM_ROWS, tn=1024):
    t, k = a.shape
    n = w.shape[1] if dims == NN else w.shape[0]
    tm, tn = min(tm, t), min(tn, n)
    o_spec = pl.BlockSpec((tm, tn), lambda j, i: (i, j))
    b_spec = pl.BlockSpec((k, tn), lambda j, i: (0, j)) if dims == NN else pl.BlockSpec((tn, k), lambda j, i: (j, 0))
    return _mm(name, a, w, grid=(n // tn, t // tm), a_spec=pl.BlockSpec((tm, k), lambda j, i: (i, 0)), b_spec=b_spec,
               o_shape=(t, n), o_spec=o_spec, dims=dims, out_dtype=out_dtype, res=res,
               res_spec=o_spec if res is not None else None)


def _wgrad(name, a, g, tk=1024, tn=1024):
    t, k = a.shape
    n = g.shape[1]
    tm, tk, tn = min(MM_ROWS, t), min(tk, k), min(tn, n)
    return _mm(name, a, g, grid=(k // tk, n // tn, t // tm),
               a_spec=pl.BlockSpec((tm, tk), lambda p, q, r: (r, p)), b_spec=pl.BlockSpec((tm, tn), lambda p, q, r: (r, q)),
               o_shape=(k, n), o_spec=pl.BlockSpec((tk, tn), lambda p, q, r: (p, q)), dims=TN, out_dtype=BF16, nk=t // tm)


def _peers():
    x, y, c = lax.axis_index("x"), lax.axis_index("y"), lax.axis_index("c")
    me = 4 * x + 2 * y + c
    out = []
    for k in range(1, N_DEV):
        kx, ky, kc = (k >> 2) & 1, (k >> 1) & 1, k & 1
        px = 1 - x if kx else x
        py = 1 - y if ky else y
        pc = 1 - c if kc else c
        out.append(((px, py, pc), 4 * px + 2 * py + pc))
    return me, out


def _cast_weights(ws, pads):
    def body(*refs):
        n = len(refs) // 2
        for i_ref, o_ref, (pr, pc) in zip(refs[:n], refs[n:], pads):
            r, c = i_ref.shape
            o_ref[0:r, 0:c] = i_ref[...].astype(BF16)
            if pr:
                o_ref[r:r + pr, :] = jnp.zeros((pr, c), BF16)
            if pc:
                o_ref[:, c:c + pc] = jnp.zeros((r, pc), BF16)

    return pl.pallas_call(
        body, name="cast_weights", in_specs=[VMEM] * len(ws), out_specs=[VMEM] * len(ws),
        out_shape=[jax.ShapeDtypeStruct((w.shape[0] + pr, w.shape[1] + pc), BF16) for w, (pr, pc) in zip(ws, pads)],
    )(*ws)


def _window(ref, j, c):
    return ref.at[:, pl.ds(pl.multiple_of(j * c, LANES), c)]


def _scatter_copies(ins, outs, sems, cols, landed):
    send_sems, recv_sems, loc_sems = sems
    n_peer = N_DEV - 1
    me, peers = _peers()

    def src(w, j):
        return _window(ins[w], j, cols[w]) if cols[w] else ins[w].at[j]

    local = [pltpu.make_async_copy(src(w, me), outs[w].at[me], loc_sems.at[w]) for w in range(len(ins))]
    remote = [pltpu.make_async_remote_copy(
        src_ref=src(w, idx), dst_ref=outs[w].at[idx if landed else me],
        send_sem=send_sems.at[w * n_peer + k], recv_sem=recv_sems.at[w * n_peer + k],
        device_id=dev, device_id_type=pl.DeviceIdType.MESH)
        for k, (dev, idx) in reversed(list(enumerate(peers))) for w in range(len(ins))]
    return local, remote


OTHER_CHIPS = (2, 4, 6)


def _gather_copies(ins, outs, sems, cols):
    send_sems, recv_sems, loc_sems = sems
    x, y, c = lax.axis_index("x"), lax.axis_index("y"), lax.axis_index("c")
    me = 4 * x + 2 * y + c
    n_pair = N_DEV - 1

    def dev(mask):
        return (1 - x if mask & 4 else x, 1 - y if mask & 2 else y, 1 - c if mask & 1 else c)

    def slot(w, mask):
        j = jnp.bitwise_xor(me, mask)
        return _window(outs[w], j, cols[w]) if cols[w] else outs[w].at[j]

    def remote(w, pair, src, to_slot, target):
        return pltpu.make_async_remote_copy(src_ref=src, dst_ref=slot(w, to_slot), send_sem=send_sems.at[w * n_pair + pair],
                                            recv_sem=recv_sems.at[w * n_pair + pair], device_id=dev(target),
                                            device_id_type=pl.DeviceIdType.MESH)

    ws = range(len(ins))
    return dict(
        local=[pltpu.make_async_copy(ins[w], slot(w, 0), loc_sems.at[w]) for w in ws],
        to_chips=[remote(w, 1 + t, ins[w], 0, m) for t, m in enumerate(OTHER_CHIPS) for w in ws],
        to_core=[remote(w, 0, ins[w], 0, 1) for w in ws],
        from_chips=[remote(w, 1 + t, ins[w], m, 0) for t, m in enumerate(OTHER_CHIPS) for w in ws],
        pass_on=[remote(w, 4 + t, slot(w, m), m, 1) for t, m in enumerate(OTHER_CHIPS) for w in ws],
        from_core=[remote(w, 0, ins[w], 1, 0) for w in ws]
        + [remote(w, 4 + t, ins[w], m + 1, 0) for t, m in enumerate(OTHER_CHIPS) for w in ws])


def _exchange_start(ins, outs, sems, gather, cols):
    if gather:
        cps = _gather_copies(ins, outs, sems, cols)
        for cp in cps["local"] + cps["to_chips"] + cps["to_core"]:
            cp.start()
    else:
        local, remote = _scatter_copies(ins, outs, sems, cols, False)
        for cp in local + remote:
            cp.start()


def _exchange_pass_on(ins, outs, sems, gather, cols):
    if gather:
        cps = _gather_copies(ins, outs, sems, cols)
        for arrived, onward in zip(cps["from_chips"], cps["pass_on"]):
            arrived.wait_recv()
            onward.start()


def _exchange_wait(ins, outs, sems, gather, cols):
    if gather:
        cps = _gather_copies(ins, outs, sems, cols)
        for cp in cps["local"]:
            cp.wait()
        for cp in cps["to_chips"] + cps["to_core"] + cps["pass_on"]:
            cp.wait_send()
        for cp in cps["from_core"]:
            cp.wait_recv()
    else:
        local, remote = _scatter_copies(ins, outs, sems, cols, True)
        for cp in local:
            cp.wait()
        for cp in remote:
            cp.wait_send()
            cp.wait_recv()


def _exchange_shapes(arrs, gather, cols):
    n = len(arrs)
    out_shape = []
    for a, c in zip(arrs, cols):
        if gather:
            shape = (a.shape[0], N_DEV * c) if c else (N_DEV,) + a.shape
        else:
            shape = (N_DEV, a.shape[0], c) if c else a.shape
        out_shape.append(jax.ShapeDtypeStruct(shape, a.dtype))
    sems = [pltpu.SemaphoreType.DMA((n * (N_DEV - 1),)), pltpu.SemaphoreType.DMA((n * (N_DEV - 1),)),
            pltpu.SemaphoreType.DMA((n,))]
    return out_shape, sems


def _call(body, *, name, grid, in_specs, out_specs, out_shape, scratch, sem, args, ride=None):
    if ride is None:
        outs = pl.pallas_call(body, name=name, grid=grid, in_specs=in_specs, out_specs=out_specs, out_shape=out_shape,
                              scratch_shapes=scratch, compiler_params=_params(sem))(*args)
        return outs, None
    arrs, gather, cols = ride
    n, n_in, n_out, n_scr = len(arrs), len(in_specs), len(out_specs), len(scratch)
    x_shape, x_sems = _exchange_shapes(arrs, gather, cols)

    def riding(*refs):
        ins, x_ins = refs[:n_in], refs[n_in:n_in + n]
        outs = refs[n_in + n:n_in + n + n_out]
        x_outs = refs[n_in + n + n_out:n_in + 2 * n + n_out]
        scr = refs[n_in + 2 * n + n_out:n_in + 2 * n + n_out + n_scr]
        sems = refs[n_in + 2 * n + n_out + n_scr:]
        def at(step):
            return functools.reduce(jnp.logical_and, [pl.program_id(a) == v for a, v in enumerate(step)])

        @pl.when(at((0,) * len(grid)))
        def _():
            _exchange_start(x_ins, x_outs, sems, gather, cols)

        @pl.when(at((grid[0] // 2,) + (0,) * (len(grid) - 1)))
        def _():
            _exchange_pass_on(x_ins, x_outs, sems, gather, cols)

        body(*ins, *outs, *scr)

        @pl.when(at(tuple(g - 1 for g in grid)))
        def _():
            _exchange_wait(x_ins, x_outs, sems, gather, cols)

    res = pl.pallas_call(
        riding, name=name, grid=grid, in_specs=list(in_specs) + [ANY] * n, out_specs=list(out_specs) + [ANY] * n,
        out_shape=list(out_shape) + x_shape, scratch_shapes=list(scratch) + x_sems,
        compiler_params=_params(("arbitrary",) * len(grid)))(*args, *arrs)
    return res[:n_out], res[n_out:]


def _my_block():
    return (4 * lax.axis_index("x") + 2 * lax.axis_index("y") + lax.axis_index("c")).astype(jnp.int32).reshape(1)


def _proj_in_gather(n, w_shard):
    t, k = n.shape
    cs = w_shard.shape[1]
    tm = min(MM_ROWS, t)
    ni = t // tm
    arrival = (0, 1) + OTHER_CHIPS + tuple(m + 1 for m in OTHER_CHIPS)

    def mask_at(s):
        return jnp.where(s < 2, s, jnp.where(s < 5, 2 * (s - 1), 2 * (s - 4) + 1))

    def body(me_ref, n_ref, w_hbm, o_ref, all_hbm, w_vmem, send_sems, recv_sems, loc_sems, load_sems):
        s, i = pl.program_id(0), pl.program_id(1)
        cps = _gather_copies([w_hbm], [all_hbm], (send_sems, recv_sems, loc_sems), (0,))
        arrived = cps["local"] + cps["from_core"][:1] + cps["from_chips"] + cps["from_core"][1:]

        def load(step):
            src = w_hbm if step == 0 else all_hbm.at[jnp.bitwise_xor(me_ref[0], arrival[step])]
            return pltpu.make_async_copy(src, w_vmem.at[step % 2], load_sems.at[step % 2])

        @pl.when(jnp.logical_and(s == 0, i == 0))
        def _():
            for cp in cps["local"] + cps["to_chips"] + cps["to_core"]:
                cp.start()
            load(0).start()

        for step, mask in enumerate(arrival):
            @pl.when(jnp.logical_and(s == step, i == 0))
            def _(step=step):
                load(step).wait()

            if step + 1 < N_DEV:
                @pl.when(jnp.logical_and(s == step, i == min(1, ni - 1)))
                def _(step=step):
                    arrived[step + 1].wait_recv()
                    if arrival[step + 1] in OTHER_CHIPS:
                        cps["pass_on"][OTHER_CHIPS.index(arrival[step + 1])].start()
                    load(step + 1).start()

        o_ref[...] = lax.dot_general(n_ref[...], w_vmem[s % 2], NN, preferred_element_type=F32).astype(BF16)

        @pl.when(jnp.logical_and(s == N_DEV - 1, i == ni - 1))
        def _():
            cps["local"][0].wait()
            for cp in cps["to_chips"] + cps["to_core"] + cps["pass_on"]:
                cp.wait_send()

    return pl.pallas_call(
        body, name="proj_in",
        grid_spec=pltpu.PrefetchScalarGridSpec(
            num_scalar_prefetch=1, grid=(N_DEV, ni),
            in_specs=[pl.BlockSpec((tm, k), lambda s, i, me: (i, 0)), ANY],
            out_specs=[pl.BlockSpec((tm, cs), lambda s, i, me: (i, jnp.bitwise_xor(me[0], mask_at(s)))), ANY],
            scratch_shapes=[pltpu.VMEM((2, k, cs), BF16), pltpu.SemaphoreType.DMA((N_DEV - 1,)),
                            pltpu.SemaphoreType.DMA((N_DEV - 1,)), pltpu.SemaphoreType.DMA((1,)),
                            pltpu.SemaphoreType.DMA((2,))]),
        out_shape=[jax.ShapeDtypeStruct((t, N_DEV * cs), BF16), jax.ShapeDtypeStruct((N_DEV, k, cs), BF16)],
        compiler_params=_params(("arbitrary", "arbitrary")),
    )(_my_block(), n, w_shard)


def _gw_in_scatter(a, g):
    t, k = a.shape
    cs = g.shape[1] // N_DEV
    tm = min(MM_ROWS, t)
    nr = t // tm
    n_chip = N_DEV // 2
    chips = (6, 4, 2, 0)

    def body(me_ref, a_ref, g_ref, out_hbm, acc, stage, other, core_send, core_recv, chip_send, chip_recv, loc_sem):
        s, r = pl.program_id(0), pl.program_id(1)
        x, y, c = lax.axis_index("x"), lax.axis_index("y"), lax.axis_index("c")
        my_chip = 2 * x + y
        part = lax.dot_general(a_ref[...], g_ref[...], TN, preferred_element_type=F32)

        def to_core(m):
            return pltpu.make_async_remote_copy(src_ref=stage.at[0], dst_ref=other.at[m], send_sem=core_send.at[m],
                                                recv_sem=core_recv.at[m], device_id=(x, y, 1 - c),
                                                device_id_type=pl.DeviceIdType.MESH)

        def to_chip(m, landed):
            mask = chips[m]
            there = (1 - x if mask & 4 else x, 1 - y if mask & 2 else y, c)
            slot = (2 * there[0] + there[1]) if landed else my_chip
            return pltpu.make_async_remote_copy(src_ref=stage.at[1], dst_ref=out_hbm.at[slot], send_sem=chip_send.at[m],
                                                recv_sem=chip_recv.at[m], device_id=there,
                                                device_id_type=pl.DeviceIdType.MESH)

        local = pltpu.make_async_copy(stage.at[1], out_hbm.at[my_chip], loc_sem)

        @pl.when(r == 0)
        def _():
            acc[...] = part

        @pl.when(r > 0)
        def _():
            acc[...] += part

        for step in range(N_DEV):
            m = step // 2

            @pl.when(jnp.logical_and(s == step, r == nr - 1))
            def _(step=step, m=m):
                if step % 2 == 0:
                    if m > 0:
                        to_core(m - 1).wait_send()
                    stage[0] = acc[...].astype(BF16)
                    to_core(m).start()
                else:
                    if m > 0:
                        to_chip(m - 1, False).wait_send()
                    to_core(m).wait_recv()
                    stage[1] = (acc[...] + other[m].astype(F32)).astype(BF16)
                    if m < n_chip - 1:
                        to_chip(m, False).start()
                    else:
                        local.start()
                        to_core(m).wait_send()
                        local.wait()
                        for mm in range(n_chip - 1):
                            to_chip(mm, True).wait_recv()

    return pl.pallas_call(
        body, name="gw_in",
        grid_spec=pltpu.PrefetchScalarGridSpec(
            num_scalar_prefetch=1, grid=(N_DEV, nr),
            in_specs=[pl.BlockSpec((tm, k), lambda s, r, me: (r, 0)),
                      pl.BlockSpec((tm, cs), lambda s, r, me: (r, jnp.bitwise_xor(me[0], N_DEV - 1 - s)))],
            out_specs=ANY,
            scratch_shapes=[pltpu.VMEM((k, cs), F32), pltpu.VMEM((2, k, cs), BF16), pltpu.VMEM((n_chip, k, cs), BF16),
                            pltpu.SemaphoreType.DMA((n_chip,)), pltpu.SemaphoreType.DMA((n_chip,)),
                            pltpu.SemaphoreType.DMA((n_chip - 1,)), pltpu.SemaphoreType.DMA((n_chip - 1,)),
                            pltpu.SemaphoreType.DMA]),
        out_shape=jax.ShapeDtypeStruct((n_chip, k, cs), BF16),
        compiler_params=_params(("arbitrary", "arbitrary")),
    )(_my_block(), a, g)


SMALL_ROWS = 8


def _allreduce_small(parts, loss_part):
    n, d = len(parts), parts[0].shape[1]

    def body(*refs):
        part_refs, loss_ref, o_ref = refs[:n], refs[n], refs[n + 1]
        mine_ref, all_ref, send_sems, recv_sems = refs[n + 2:]
        me, peers = _peers()
        mine_ref[...] = jnp.zeros_like(mine_ref)
        for i, p_ref in enumerate(part_refs):
            mine_ref[i:i + 1, :] = p_ref[...]
        mine_ref[SMALL_ROWS - 1:SMALL_ROWS, 0:LANES] = loss_ref[0:1, :]
        all_ref[me] = mine_ref[...]
        for k, (dev, idx) in enumerate(peers):
            pltpu.make_async_remote_copy(src_ref=mine_ref, dst_ref=all_ref.at[me], send_sem=send_sems.at[k],
                                         recv_sem=recv_sems.at[k], device_id=dev,
                                         device_id_type=pl.DeviceIdType.MESH).start()
        for k, (dev, idx) in enumerate(peers):
            cp = pltpu.make_async_remote_copy(src_ref=mine_ref, dst_ref=all_ref.at[idx], send_sem=send_sems.at[k],
                                              recv_sem=recv_sems.at[k], device_id=dev,
                                              device_id_type=pl.DeviceIdType.MESH)
            cp.wait_send()
            cp.wait_recv()
        tot = all_ref[0]
        for dvc in range(1, N_DEV):
            tot = tot + all_ref[dvc]
        o_ref[...] = tot

    return pl.pallas_call(
        body, name="allreduce_small", in_specs=[VMEM] * (n + 1), out_specs=VMEM,
        out_shape=jax.ShapeDtypeStruct((SMALL_ROWS, d), F32),
        scratch_shapes=[pltpu.VMEM((SMALL_ROWS, d), F32), pltpu.VMEM((N_DEV, SMALL_ROWS, d), F32),
                        pltpu.SemaphoreType.DMA((N_DEV - 1,)), pltpu.SemaphoreType.DMA((N_DEV - 1,))],
    )(*parts, loss_part)


def _adam_math(g, w, m, v):
    m_new = ADAM_B1 * m + (1.0 - ADAM_B1) * g
    v_new = ADAM_B2 * v + (1.0 - ADAM_B2) * (g * g)
    m_hat = m_new / (1.0 - ADAM_B1 ** ADAM_STEP)
    v_hat = v_new / (1.0 - ADAM_B2 ** ADAM_STEP)
    delta = -ADAM_LR * (m_hat / (jnp.sqrt(v_hat) + ADAM_EPS) + ADAM_WD * w)
    return delta, m_new, v_new


def _adam(name, pieces, w, m, v):
    r, c = w.shape
    n_piece, _, cp = pieces.shape
    tr = r
    for cand in (256, 176, 128, 64):
        if r % cand == 0 and r > cand:
            tr = cand
            break

    def body(p_ref, w_ref, m_ref, v_ref, g_ref, d_ref, mo_ref, vo_ref):
        g = p_ref[0, :, 0:c].astype(F32)
        for j in range(1, n_piece):
            g = g + p_ref[j, :, 0:c].astype(F32)
        delta, m_new, v_new = _adam_math(g, w_ref[...], m_ref[...], v_ref[...])
        g_ref[...] = g
        d_ref[...] = delta
        mo_ref[...] = m_new
        vo_ref[...] = v_new

    blk = pl.BlockSpec((tr, c), lambda i: (i, 0))
    osh = jax.ShapeDtypeStruct((r, c), F32)
    return pl.pallas_call(
        body, name=name, grid=(r // tr,),
        in_specs=[pl.BlockSpec((n_piece, tr, cp), lambda i: (0, i, 0)), blk, blk, blk],
        out_specs=[blk, blk, blk, blk], out_shape=[osh, osh, osh, osh],
        compiler_params=_params(("parallel",)),
    )(pieces, w, m, v)


def _adam_small(g_all, ws, ms, vs):
    n = len(ws)

    def body(*refs):
        g_ref, ins, outs = refs[0], refs[1:1 + 3 * n], refs[1 + 3 * n:]
        for i in range(n):
            g = g_ref[i:i + 1, :]
            delta, m_new, v_new = _adam_math(g, ins[i][...], ins[n + i][...], ins[2 * n + i][...])
            for kind, val in enumerate((g, delta, m_new, v_new)):
                outs[kind * n + i][...] = val

    osh = jax.ShapeDtypeStruct(ws[0].shape, F32)
    res = pl.pallas_call(body, name="adam_small", in_specs=[VMEM] * (1 + 3 * n), out_specs=[VMEM] * (4 * n),
                         out_shape=[osh] * (4 * n))(g_all, *ws, *ms, *vs)
    return res[:n], res[n:2 * n], res[2 * n:3 * n], res[3 * n:]


def _local_step(x, mem, pos, tgt, gains, w_in_shard, shards, batch):
    g_mix, g_mem_q, g_mem_kv, g_ffn, g_final = gains
    t, d = x.shape
    s = t // batch
    n_mem = mem.shape[0] // batch
    n_sh = N_DEV
    width = shards[0].shape[0]
    nb = width // LANES

    lane = jnp.arange(LANES, dtype=jnp.int32) % HEAD_DIM
    sel_lo = (lane < ROPE_HALF).astype(F32)[None, :]
    sel_hi = ((lane >= ROPE_HALF) & (lane < 2 * ROPE_HALF)).astype(F32)[None, :]
    freqs = ROPE_THETA ** (-jnp.arange(ROPE_HALF, dtype=F32) / ROPE_HALF)
    inv_freq = jnp.where(lane < 2 * ROPE_HALF, freqs[lane % ROPE_HALF], 0.0)[None, :]
    cos_t, sin_a, sin_b = _rope_tables(pos, inv_freq, sel_lo, sel_hi)
    bias = _dilated_bias_tiles(s)

    n1 = _rms_fwd("norm_mix", x, g_mix)
    proj, w_in = _proj_in_gather(n1, w_in_shard)
    qk_a = _rope_apply("rope_fwd", proj, 0, 2 * nb, cos_t, sin_a, sin_b, 1.0)
    cs_up, cs_ffn = shards[0].shape[1], shards[6].shape[1]
    (o_a, lse_a), (w_up_a, w_up_b, w_out, w_q, w_kv, w_o, w_fd) = _da_fwd(
        qk_a, proj, 2 * nb, bias, batch, s,
        ride=(shards[:6] + shards[8:], True, (cs_up, cs_up, 0, 0, 0, cs_up, 0)))
    (o_b, tot_b), (w_fg, w_fu) = _sb_fwd(proj, 3 * nb, 4 * nb, 5 * nb, batch, s,
                                         ride=(shards[6:8], True, (cs_ffn, cs_ffn)))
    w_out = w_out.reshape(d, d)
    w_q = w_q.reshape(d, -1)
    w_kv = w_kv.reshape(d, -1)
    w_fd = w_fd.reshape(-1, d)
    ua, ub, mixed = _mixer_fwd(o_a, o_b, w_up_a, w_up_b, proj, 6 * nb)
    h1 = _mm_w("mix_out", mixed, w_out, F32, res=x)
    n2 = _rms_fwd("norm_mem_q", h1, g_mem_q)
    mem_n = _rms_fwd("norm_mem_kv", mem, g_mem_kv)
    q_m = _mm_w("mem_q", n2, w_q, BF16)
    kv_m = _mm_w("mem_kv", mem_n, w_kv, BF16)
    o_m = _mem_fwd(q_m, kv_m, batch, s, n_mem)
    h2 = _mm_w("mem_out", o_m, w_o, F32, res=h1)
    n3 = _rms_fwd("norm_ffn", h2, g_ffn)
    hg, hu, act = _ffn_up(n3, w_fg, w_fu)
    h3 = _mm_w("ffn_down", act, w_fd, F32, res=h2, tm=ROW_TILE)
    loss_part, dh3, dh3_b, dg_final = _loss_head(h3, tgt, g_final.reshape(1, d))

    dhg, dhu = _ffn_bwd_act(dh3_b, w_fd, hg, hu)
    gw_fd = _wgrad("gw_ffn_down", act, dh3_b)
    gw_fg = _wgrad("gw_ffn_gate", n3, dhg)
    gw_fu = _wgrad("gw_ffn_up", n3, dhu)
    dn3 = _mm_w("dn_ffn_gate", dhg, w_fg, F32, dims=NT, tm=ROW_TILE)
    dn3 = _mm_w("dn_ffn_up", dhu, w_fu, F32, dims=NT, res=dn3, tm=ROW_TILE)
    dh2, dh2_b, dg_ffn = _rms_bwd("norm_ffn_bwd", dn3, h2, g_ffn, dh3, ("f32", "bf16"))

    do_m = _mm_w("mem_out_bwd", dh2_b, w_o, BF16, dims=NT)
    gw_o = _wgrad("gw_mem_o", o_m, dh2_b)
    dq_m, dkv_m = _mem_bwd(q_m, kv_m, do_m, batch, s, n_mem)
    gw_q = _wgrad("gw_mem_q", n2, dq_m)
    gw_kv = _wgrad("gw_mem_kv", mem_n, dkv_m)
    dn2 = _mm_w("mem_q_bwd", dq_m, w_q, F32, dims=NT)
    dmem_n = _mm_w("mem_kv_bwd", dkv_m, w_kv, F32, dims=NT)
    (dg_mem_kv,) = _rms_bwd("norm_mem_kv_bwd", dmem_n, mem, g_mem_kv, None, ())
    dh1, dh1_b, dg_mem_q = _rms_bwd("norm_mem_q_bwd", dn2, h1, g_mem_q, dh2, ("f32", "bf16"))

    dmix = _mm_w("mix_out_bwd", dh1_b, w_out, BF16, dims=NT)
    gw_out = _wgrad("gw_out", mixed, dh1_b)
    dua, dub, dgates = _mixer_bwd(dmix, ua, ub, proj, 6 * nb)
    do_a = _mm_w("up_a_bwd", dua, w_up_a, BF16, dims=NT)
    do_b = _mm_w("up_b_bwd", dub, w_up_b, BF16, dims=NT)
    gw_ua = _wgrad("gw_up_a", o_a, dua)
    gw_ub = _wgrad("gw_up_b", o_b, dub)
    (dq_ar, dk_ar, dv_a), (p_fg, p_fd) = _da_bwd(
        qk_a, proj, 2 * nb, bias, o_a, lse_a, do_a, batch, s,
        ride=([gw_fg, gw_fd.reshape(n_sh, -1, d)], False, (cs_ffn, 0)))
    dqk_a = _rope_apply("rope_bwd", jnp.concatenate([dq_ar, dk_ar], axis=1), 0, 2 * nb, cos_t, sin_a, sin_b, -1.0)
    mid = [gw_ua, gw_ub, gw_out.reshape(n_sh, -1, d), gw_q.reshape(n_sh, -1, gw_q.shape[1]),
           gw_kv.reshape(n_sh, -1, gw_kv.shape[1]), gw_o, gw_fu]
    (dq_b, dk_b, dv_b), (*p_mid, p_fu) = _sb_bwd(proj, 3 * nb, 4 * nb, 5 * nb, tot_b, do_b, batch, s,
                                                 ride=(mid, False, (cs_up, cs_up, 0, 0, 0, cs_up, cs_ffn)))
    p_ffn = [p_fg, p_fu, p_fd]
    dproj = jnp.concatenate([dqk_a, dv_a, dq_b, dk_b, dv_b, dgates], axis=1)
    dn1 = _mm_cols_t("proj_in_bwd", dproj, w_in, F32)
    p_in = _gw_in_scatter(n1, dproj)
    grad_x, dg_mix = _rms_bwd("norm_mix_bwd", dn1, x, g_mix, dh1, ("f32",))
    return loss_part, grad_x, [p_in] + list(p_mid) + p_ffn, (dg_mix, dg_mem_q, dg_mem_kv, dg_ffn, dg_final)


WEIGHTS =("w_in", "w_up_a", "w_up_b", "w_out", "w_q_mem", "w_kv_mem", "w_o_mem", "w_ffn_gate", "w_ffn_up", "w_ffn_down")
GAINS = ("g_mix", "g_mem_q", "g_mem_kv", "g_ffn", "g_final")
ORDER = ("g_mix", "w_in", "w_up_a", "w_up_b", "w_out", "g_mem_q", "g_mem_kv", "w_q_mem", "w_kv_mem", "w_o_mem", "g_ffn",
         "w_ffn_gate", "w_ffn_up", "w_ffn_down", "g_final")


def kernel(x, mem, positions, g_mix, w_in, w_up_a, w_up_b, w_out, g_mem_q, g_mem_kv, w_q_mem, w_kv_mem, w_o_mem, g_ffn, w_ffn_gate, w_ffn_up, w_ffn_down, g_final, loss_target, m_g_mix, m_w_in, m_w_up_a, m_w_up_b, m_w_out, m_g_mem_q, m_g_mem_kv, m_w_q_mem, m_w_kv_mem, m_w_o_mem, m_g_ffn, m_w_ffn_gate, m_w_ffn_up, m_w_ffn_down, m_g_final, v_g_mix, v_w_in, v_w_up_a, v_w_up_b, v_w_out, v_g_mem_q, v_g_mem_kv, v_w_q_mem, v_w_kv_mem, v_w_o_mem, v_g_ffn, v_w_ffn_gate, v_w_ffn_up, v_w_ffn_down, v_g_final):
    given = dict(locals())
    batch, s, d = x.shape
    t = batch * s
    shard = {n: given[n].reshape(given[n].shape[-2:]) for n in WEIGHTS}
    gains = [given[n].reshape(1, d) for n in GAINS]

    pad = (-shard["w_ffn_down"].shape[0]) % LANES
    pads = {"w_ffn_gate": (0, pad), "w_ffn_up": (0, pad), "w_ffn_down": (pad, 0)}
    cast = _cast_weights([shard[n] for n in WEIGHTS], [pads.get(n, (0, 0)) for n in WEIGHTS])
    loss_part, grad_x, pieces, dgains = _local_step(
        x.reshape(t, d), mem.reshape(-1, d), positions.reshape(t, 1), loss_target.reshape(t, d), gains, cast[0],
        cast[1:], batch)

    grad, delta, new_m, new_v = {}, {}, {}, {}
    for n, p in zip(WEIGHTS, pieces):
        m2, v2 = given["m_" + n].reshape(shard[n].shape), given["v_" + n].reshape(shard[n].shape)
        outs = _adam("adam_" + n, p, shard[n], m2, v2)
        grad[n], delta[n], new_m[n], new_v[n] = [o.reshape(given[n].shape) for o in outs]

    g_all = _allreduce_small(list(dgains), loss_part)
    small = _adam_small(g_all, gains, [given["m_" + n].reshape(1, d) for n in GAINS],
                        [given["v_" + n].reshape(1, d) for n in GAINS])
    for out, vals in zip((grad, delta, new_m, new_v), small):
        for n, val in zip(GAINS, vals):
            out[n] = val.reshape(given[n].shape)

    loss = g_all[SMALL_ROWS - 1, 0]
    return (loss, grad_x.reshape(x.shape), *[grad[n] for n in ORDER], *[delta[n] for n in ORDER],
            *[new_m[n] for n in ORDER], *[new_v[n] for n in ORDER])
```

```python
import functools
import math

import jax
import jax.numpy as jnp
import numpy as np
from jax import lax
from jax.experimental import pallas as pl
from jax.experimental.pallas import tpu as pltpu

F32 = jnp.float32
BF16 = jnp.bfloat16

N_DEV = 8
HEAD_DIM = 64
MEM_HEAD_DIM = 128
N_HEADS_MEM = 4
BLOCK = 128
DIL_PATTERNS = ((128, 1), (512, 4), (2048, 16))
ROPE_THETA = 500000.0
ROPE_HALF = 8
RMS_EPS = 1e-6
ADAM_LR, ADAM_B1, ADAM_B2, ADAM_EPS, ADAM_WD, ADAM_STEP = 0.001, 0.9, 0.999, 1e-08, 0.01, 10
NEG = -1e30
ROW_TILE = 512
LANES = 128

ANY = pl.BlockSpec(memory_space=pl.ANY)
VMEM = pl.BlockSpec(memory_space=pltpu.VMEM)
NN = (((1,), (0,)), ((), ()))
NT = (((1,), (1,)), ((), ()))
TN = (((0,), (0,)), ((), ()))


def _params(sem):
    return pltpu.CompilerParams(dimension_semantics=sem)


def _mm(name, a, b, *, grid, a_spec, b_spec, o_shape, o_spec, dims, out_dtype, nk=1, res=None, res_spec=None):
    has_res = res is not None

    def body(*refs):
        a_ref, b_ref = refs[0], refs[1]
        r_ref = refs[2] if has_res else None
        o_ref = refs[3] if has_res else refs[2]
        p = lax.dot_general(a_ref[...], b_ref[...], dims, preferred_element_type=F32)
        if nk == 1:
            if has_res:
                p = p + r_ref[...].astype(F32)
            o_ref[...] = p.astype(out_dtype)
            return
        acc_ref = refs[-1]
        k = pl.program_id(len(grid) - 1)

        @pl.when(k == 0)
        def _():
            acc_ref[...] = p

        @pl.when(k > 0)
        def _():
            acc_ref[...] += p

        @pl.when(k == nk - 1)
        def _():
            t = acc_ref[...]
            if has_res:
                t = t + r_ref[...].astype(F32)
            o_ref[...] = t.astype(out_dtype)

    o_block = tuple(d for d in o_spec.block_shape if d is not None)
    sem = ("parallel",) * (len(grid) - 1) + (("arbitrary",) if nk > 1 else ("parallel",))
    return pl.pallas_call(
        body, name=name, grid=grid,
        in_specs=[a_spec, b_spec] + ([res_spec] if has_res else []),
        out_specs=o_spec, out_shape=jax.ShapeDtypeStruct(o_shape, out_dtype),
        scratch_shapes=[pltpu.VMEM(o_block, F32)] if nk > 1 else [],
        compiler_params=_params(sem),
    )(*([a, b] + ([res] if has_res else [])))


def _rms_fwd(name, x, g):
    t, d = x.shape
    tm = min(ROW_TILE, t)

    def body(x_ref, g_ref, o_ref):
        xf = x_ref[...]
        r = lax.rsqrt(jnp.mean(xf * xf, axis=-1, keepdims=True) + RMS_EPS)
        o_ref[...] = (xf * r * g_ref[...]).astype(BF16)

    return pl.pallas_call(
        body, name=name, grid=(t // tm,),
        in_specs=[pl.BlockSpec((tm, d), lambda i: (i, 0)), pl.BlockSpec((1, d), lambda i: (0, 0))],
        out_specs=pl.BlockSpec((tm, d), lambda i: (i, 0)), out_shape=jax.ShapeDtypeStruct((t, d), BF16),
        compiler_params=_params(("parallel",)),
    )(x, g)


def _rms_bwd(name, dn, x, g, dres, want):
    t, d = x.shape
    tm = min(ROW_TILE, t)
    has_res = dres is not None

    def body(*refs):
        dn_ref, x_ref, g_ref = refs[0], refs[1], refs[2]
        r_ref = refs[3] if has_res else None
        dx_refs, dg_ref = refs[-1 - len(want):-1], refs[-1]
        xf = x_ref[...]
        r = lax.rsqrt(jnp.mean(xf * xf, axis=-1, keepdims=True) + RMS_EPS)
        xh = xf * r
        dnf = dn_ref[...].astype(F32)
        if want:
            dxh = dnf * g_ref[...]
            dx = r * (dxh - xh * jnp.mean(dxh * xh, axis=-1, keepdims=True))
            if has_res:
                dx = dx + r_ref[...]
            for kind, dx_ref in zip(want, dx_refs):
                dx_ref[...] = dx.astype(F32 if kind == "f32" else BF16)

        @pl.when(pl.program_id(0) == 0)
        def _():
            dg_ref[...] = jnp.zeros_like(dg_ref)

        dg_ref[...] += jnp.sum(dnf * xh, axis=0, keepdims=True)

    row = pl.BlockSpec((tm, d), lambda i: (i, 0))
    vec = pl.BlockSpec((1, d), lambda i: (0, 0))
    return pl.pallas_call(
        body, name=name, grid=(t // tm,),
        in_specs=[row, row, vec] + ([row] if has_res else []),
        out_specs=[row] * len(want) + [vec],
        out_shape=[jax.ShapeDtypeStruct((t, d), F32 if kind == "f32" else BF16) for kind in want]
        + [jax.ShapeDtypeStruct((1, d), F32)],
        compiler_params=_params(("arbitrary",)),
    )(*([dn, x, g] + ([dres] if has_res else [])))


def _loss_head(h, tgt, g):
    t, d = h.shape
    tm = min(ROW_TILE, t)

    def body(h_ref, t_ref, g_ref, loss_ref, dh_ref, dhb_ref, dg_ref):
        xf = h_ref[...]
        gv = g_ref[...]
        r = lax.rsqrt(jnp.mean(xf * xf, axis=-1, keepdims=True) + RMS_EPS)
        xh = xf * r
        e = xh * gv - t_ref[...]
        dy = e * (1.0 / d)
        dxh = dy * gv
        dh = r * (dxh - xh * jnp.mean(dxh * xh, axis=-1, keepdims=True))
        dh_ref[...] = dh
        dhb_ref[...] = dh.astype(BF16)

        @pl.when(pl.program_id(0) == 0)
        def _():
            dg_ref[...] = jnp.zeros_like(dg_ref)
            loss_ref[...] = jnp.zeros_like(loss_ref)

        dg_ref[...] += jnp.sum(dy * xh, axis=0, keepdims=True)
        part = jnp.sum(jnp.sum(e * e, axis=1, keepdims=True), axis=0, keepdims=True) * (0.5 / d)
        loss_ref[...] += jnp.broadcast_to(part, loss_ref.shape)

    row = pl.BlockSpec((tm, d), lambda i: (i, 0))
    vec = pl.BlockSpec((1, d), lambda i: (0, 0))
    return pl.pallas_call(
        body, name="loss_head", grid=(t // tm,),
        in_specs=[row, row, vec],
        out_specs=[pl.BlockSpec((8, LANES), lambda i: (0, 0)), row, row, vec],
        out_shape=[jax.ShapeDtypeStruct((8, LANES), F32), jax.ShapeDtypeStruct((t, d), F32),
                   jax.ShapeDtypeStruct((t, d), BF16), jax.ShapeDtypeStruct((1, d), F32)],
        compiler_params=_params(("arbitrary",)),
    )(h, tgt, g)


def _rope_tables(pos, inv_freq, sel_lo, sel_hi):
    t = pos.shape[0]
    tm = min(ROW_TILE, t)

    def body(p_ref, f_ref, lo_ref, hi_ref, c_ref, sa_ref, sb_ref):
        ang = p_ref[...].astype(F32) * f_ref[...]
        rot = lo_ref[...] + hi_ref[...]
        cs, sn = jnp.cos(ang), jnp.sin(ang)
        c_ref[...] = cs * rot + (1.0 - rot)
        sa_ref[...] = -sn * lo_ref[...]
        sb_ref[...] = sn * hi_ref[...]

    vec = pl.BlockSpec((1, LANES), lambda i: (0, 0))
    row = pl.BlockSpec((tm, LANES), lambda i: (i, 0))
    return pl.pallas_call(
        body, name="rope_tables", grid=(t // tm,),
        in_specs=[pl.BlockSpec((tm, 1), lambda i: (i, 0)), vec, vec, vec],
        out_specs=[row, row, row], out_shape=[jax.ShapeDtypeStruct((t, LANES), F32)] * 3,
        compiler_params=_params(("parallel",)),
    )(pos, inv_freq, sel_lo, sel_hi)


def _rope_apply(name, src, col0, n_cols, cos_t, sin_a, sin_b, sign):
    t = src.shape[0]
    tm = min(ROW_TILE, t)

    def body(x_ref, c_ref, sa_ref, sb_ref, o_ref):
        cs, sa, sb = c_ref[...], sign * sa_ref[...], sign * sb_ref[...]
        for c in range(n_cols):
            cols = slice(c * LANES, (c + 1) * LANES)
            xf = x_ref[:, cols].astype(F32)
            up = pltpu.roll(xf, LANES - ROPE_HALF, 1)
            dn = pltpu.roll(xf, ROPE_HALF, 1)
            o_ref[:, cols] = (xf * cs + up * sa + dn * sb).astype(BF16)

    wide = n_cols * LANES
    tab = pl.BlockSpec((tm, LANES), lambda i: (i, 0))
    return pl.pallas_call(
        body, name=name, grid=(t // tm,),
        in_specs=[pl.BlockSpec((tm, wide), lambda i: (i, col0 // n_cols)), tab, tab, tab],
        out_specs=pl.BlockSpec((tm, wide), lambda i: (i, 0)),
        out_shape=jax.ShapeDtypeStruct((t, wide), BF16),
        compiler_params=_params(("parallel",)),
    )(src, cos_t, sin_a, sin_b)


DA_T = 256
FWD_STREAMS = 4
BWD_STREAMS = 2


def _lane_lo():
    return lax.broadcasted_iota(jnp.int32, (BLOCK, LANES), 1) < HEAD_DIM


def _dilated_bias_tiles(s):
    n = s // DA_T
    dist = (np.arange(n)[:, None, None] * DA_T + np.arange(DA_T)[None, :, None] - np.arange(DA_T)[None, None, :])
    cnt = np.zeros(dist.shape, np.float32)
    for window, dil in DIL_PATTERNS:
        cnt += ((dist >= 0) & (dist % dil == 0) & (dist <= window)).astype(np.float32)
    return jnp.asarray(np.where(cnt > 0, np.log(np.maximum(cnt, 1.0)), NEG).astype(np.float32))


def _stack_heads(x, lo):
    zero = jnp.zeros_like(x)
    return jnp.concatenate([jnp.where(lo, x, zero), jnp.where(lo, zero, x)], axis=0)


def _da_fwd(qk, proj, v_col0, bias, batch, s, ride=None, streams=FWD_STREAMS):
    t = qk.shape[0]
    nq = s // DA_T
    n_pairs = 4
    ns = streams
    wide = ns * LANES
    scale = HEAD_DIM ** -0.5

    def body(q_ref, k_ref, v_ref, b_ref, o_ref, lse_ref, acc_ref, m_ref, l_ref):
        i = pl.program_id(2)
        lo = lax.broadcasted_iota(jnp.int32, (DA_T, LANES), 1) < HEAD_DIM
        ones = jnp.ones((DA_T, LANES), BF16)
        acc_ref[...] = jnp.zeros_like(acc_ref)
        m_ref[...] = jnp.full(m_ref.shape, NEG, F32)
        l_ref[...] = jnp.zeros_like(l_ref)
        qqs = [_stack_heads(q_ref[:, st * LANES:(st + 1) * LANES] * scale, lo) for st in range(ns)]

        def scores(st, rows, bias2):
            k = k_ref[rows, st * LANES:(st + 1) * LANES]
            return lax.dot_general(qqs[st], k, NT, preferred_element_type=F32) + bias2

        def softmax(st, sc):
            m_old = m_ref[st]
            m_new = jnp.maximum(m_old, jnp.max(sc, axis=1, keepdims=True))
            m_ref[st] = m_new
            return jnp.exp(sc - m_new).astype(BF16), jnp.exp(m_old - m_new)

        def values(st, rows, p, alpha):
            v = v_ref[rows, st * LANES:(st + 1) * LANES]
            vz = jnp.zeros_like(v)
            l_ref[st] = alpha * l_ref[st] + lax.dot_general(p, ones, NN, preferred_element_type=F32)
            pv = (lax.dot_general(p[:DA_T], jnp.where(lo, v, vz), NN, preferred_element_type=F32)
                  + lax.dot_general(p[DA_T:], jnp.where(lo, vz, v), NN, preferred_element_type=F32))
            acc_ref[st] = acc_ref[st] * jnp.where(lo, alpha[:DA_T], alpha[DA_T:]) + pv

        def trip(dlt, carry):
            rows = pl.ds(pl.multiple_of((i - dlt) * DA_T, DA_T), DA_T)
            bias_t = b_ref[dlt]
            bias2 = jnp.concatenate([bias_t, bias_t], axis=0)
            scs = [scores(st, rows, bias2) for st in range(ns)]
            pas = [softmax(st, scs[st]) for st in range(ns)]
            for st in range(ns):
                values(st, rows, *pas[st])
            return carry

        lax.fori_loop(0, i + 1, trip, 0)
        for st in range(ns):
            cols = slice(st * LANES, (st + 1) * LANES)
            l_t = l_ref[st]
            o_ref[:, cols] = (acc_ref[st] / jnp.where(lo, l_t[:DA_T], l_t[DA_T:])).astype(BF16)
            lse = m_ref[st] + jnp.log(l_t)
            lse_ref[:, cols] = jnp.where(lo, lse[:DA_T], lse[DA_T:])

    blk = pl.BlockSpec((DA_T, wide), lambda b, h, i: (b * nq + i, h))
    return _call(
        body, name="attn_a_fwd", grid=(batch, n_pairs // ns, nq),
        in_specs=[blk,
                  pl.BlockSpec((s, wide), lambda b, h, i: (b, n_pairs // ns + h)),
                  pl.BlockSpec((s, wide), lambda b, h, i: (b, v_col0 // ns + h)),
                  pl.BlockSpec((nq, DA_T, DA_T), lambda b, h, i: (0, 0, 0))],
        out_specs=[blk, blk],
        out_shape=[jax.ShapeDtypeStruct((t, n_pairs * LANES), BF16), jax.ShapeDtypeStruct((t, n_pairs * LANES), F32)],
        scratch=[pltpu.VMEM((ns, DA_T, LANES), F32), pltpu.VMEM((ns, 2 * DA_T, 1), F32),
                 pltpu.VMEM((ns, 2 * DA_T, LANES), F32)],
        sem=("parallel", "parallel", "arbitrary"), args=(qk, qk, proj, bias), ride=ride)


def _da_bwd(qk, proj, v_col0, bias, o, lse, do, batch, s, ride=None, streams=BWD_STREAMS):
    t = qk.shape[0]
    nq = s // DA_T
    n_pairs = 4
    ns = streams
    wide = ns * LANES
    scale = HEAD_DIM ** -0.5

    def body(q_ref, k_ref, v_ref, b_ref, o_ref, lse_ref, do_ref, dq_ref, dk_ref, dv_ref, dk_acc, dv_acc, dq_acc):
        i = pl.program_id(2)
        lo = lax.broadcasted_iota(jnp.int32, (DA_T, LANES), 1) < HEAD_DIM

        @pl.when(i == 0)
        def _():
            dk_acc[...] = jnp.zeros_like(dk_acc)
            dv_acc[...] = jnp.zeros_like(dv_acc)

        dq_acc[...] = jnp.zeros_like(dq_acc)
        qqs, dds, deltas, lses = [], [], [], []
        for st in range(ns):
            cols = slice(st * LANES, (st + 1) * LANES)
            do_ = do_ref[:, cols]
            qqs.append(_stack_heads(q_ref[:, cols] * scale, lo))
            dds.append(_stack_heads(do_, lo))
            prod = do_.astype(F32) * o_ref[:, cols].astype(F32)
            fz = jnp.zeros_like(prod)
            deltas.append(jnp.concatenate([jnp.sum(jnp.where(lo, prod, fz), axis=1, keepdims=True),
                                           jnp.sum(jnp.where(lo, fz, prod), axis=1, keepdims=True)], axis=0))
            lse_t = lse_ref[:, cols]
            lses.append(jnp.concatenate([lse_t[:, 0:1], lse_t[:, HEAD_DIM:HEAD_DIM + 1]], axis=0))

        def products(st, rows, bias2):
            cols = slice(st * LANES, (st + 1) * LANES)
            sc = lax.dot_general(qqs[st], k_ref[rows, cols], NT, preferred_element_type=F32) + bias2
            return sc, lax.dot_general(dds[st], v_ref[rows, cols], NT, preferred_element_type=F32)

        def weights(st, sc, dp):
            p = jnp.exp(sc - lses[st])
            return (p * (dp - deltas[st])).astype(BF16), p.astype(BF16)

        def gradients(st, rows, ds, p):
            cols = slice(st * LANES, (st + 1) * LANES)
            k = k_ref[rows, cols]
            kz = jnp.zeros_like(k)
            dq_acc[st] += (lax.dot_general(ds[:DA_T], jnp.where(lo, k, kz), NN, preferred_element_type=F32)
                           + lax.dot_general(ds[DA_T:], jnp.where(lo, kz, k), NN, preferred_element_type=F32))
            dk_acc[rows, cols] += lax.dot_general(ds, qqs[st], TN, preferred_element_type=F32)
            dv_acc[rows, cols] += lax.dot_general(p, dds[st], TN, preferred_element_type=F32)

        def trip(dlt, carry):
            rows = pl.ds(pl.multiple_of((i - dlt) * DA_T, DA_T), DA_T)
            bias_t = b_ref[dlt]
            bias2 = jnp.concatenate([bias_t, bias_t], axis=0)
            prods = [products(st, rows, bias2) for st in range(ns)]
            wts = [weights(st, *prods[st]) for st in range(ns)]
            for st in range(ns):
                gradients(st, rows, *wts[st])
            return carry

        lax.fori_loop(0, i + 1, trip, 0)
        for st in range(ns):
            dq_ref[:, st * LANES:(st + 1) * LANES] = (dq_acc[st] * scale).astype(BF16)

        @pl.when(i == nq - 1)
        def _():
            dk_ref[...] = dk_acc[...].astype(BF16)
            dv_ref[...] = dv_acc[...].astype(BF16)

    blk = pl.BlockSpec((DA_T, wide), lambda b, h, i: (b * nq + i, h))
    seq = pl.BlockSpec((s, wide), lambda b, h, i: (b, h))
    out = jax.ShapeDtypeStruct((t, n_pairs * LANES), BF16)
    return _call(
        body, name="attn_a_bwd", grid=(batch, n_pairs // ns, nq),
        in_specs=[blk,
                  pl.BlockSpec((s, wide), lambda b, h, i: (b, n_pairs // ns + h)),
                  pl.BlockSpec((s, wide), lambda b, h, i: (b, v_col0 // ns + h)),
                  pl.BlockSpec((nq, DA_T, DA_T), lambda b, h, i: (0, 0, 0)),
                  blk, blk, blk],
        out_specs=[blk, seq, seq], out_shape=[out, out, out],
        scratch=[pltpu.VMEM((s, wide), F32), pltpu.VMEM((s, wide), F32), pltpu.VMEM((ns, DA_T, LANES), F32)],
        sem=("parallel", "parallel", "arbitrary"), args=(qk, qk, proj, bias, o, lse, do), ride=ride)


SB_Q = 256


def _sb_consts(after):
    r = lax.broadcasted_iota(jnp.int32, (2 * BLOCK, 2 * BLOCK), 0) % BLOCK
    c = lax.broadcasted_iota(jnp.int32, (2 * BLOCK, 2 * BLOCK), 1)
    tri = (r > c) if after else (r < c)
    return jnp.logical_or(c >= BLOCK, tri).astype(BF16)


def _split(x):
    hi = x.astype(BF16)
    lo = (x - hi.astype(F32)).astype(BF16)
    return jnp.concatenate([hi, lo], axis=1)


def _sb_fwd(proj, q_col0, k_col0, v_col0, batch, s, ride=None, streams=FWD_STREAMS):
    t = proj.shape[0]
    nq = s // SB_Q
    n_pairs = 4
    ns = streams
    wide = ns * LANES
    scale = HEAD_DIM ** -0.5

    def body(q_ref, k_ref, v_ref, o_ref, tot_ref, acc_ref, run_ref):
        i = pl.program_id(2)
        lo_q = lax.broadcasted_iota(jnp.int32, (SB_Q, LANES), 1) < HEAD_DIM
        lo_k = _lane_lo()
        mat = _sb_consts(True)
        row = lax.broadcasted_iota(jnp.int32, (2 * SB_Q, LANES), 0) % SB_Q
        ahead = row - lax.broadcasted_iota(jnp.int32, (2 * SB_Q, LANES), 1)
        acc_ref[...] = jnp.zeros_like(acc_ref)
        run_ref[...] = jnp.zeros_like(run_ref)
        qqs = [_stack_heads(q_ref[:, st * LANES:(st + 1) * LANES] * scale, lo_q) for st in range(ns)]

        def units(todo):
            def rows(j):
                return pl.ds(pl.multiple_of(j * BLOCK, BLOCK), BLOCK)

            zs = [lax.dot_general(qqs[st], k_ref[rows(j), st * LANES:(st + 1) * LANES], NT, preferred_element_type=F32)
                  for st, j, _ in todo]
            logs = []
            for z, (_, _, off) in zip(zs, todo):
                lsig = jnp.minimum(z, 0.0) - jnp.log(1.0 + jnp.exp(-jnp.abs(z)))
                lneg = lsig - z
                if off is not None:
                    lneg = jnp.where(ahead > off, lneg, 0.0)
                logs.append((lsig, _split(lneg)))
            sums = [lax.dot_general(cat, mat, NN, preferred_element_type=F32) for _, cat in logs]
            probs = []
            for (lsig, _), sm, (st, _, off) in zip(logs, sums, todo):
                run = run_ref[st]
                a = jnp.exp(lsig + run + sm[:, :BLOCK])
                if off is not None:
                    a = jnp.where(ahead > off, a, 0.0)
                run_ref[st] = run + sm[:, BLOCK:]
                probs.append(a.astype(BF16))
            for ab, (st, j, _) in zip(probs, todo):
                v = v_ref[rows(j), st * LANES:(st + 1) * LANES]
                vz = jnp.zeros_like(v)
                acc_ref[st] += (lax.dot_general(ab[:SB_Q], jnp.where(lo_k, v, vz), NN, preferred_element_type=F32)
                                + lax.dot_general(ab[SB_Q:], jnp.where(lo_k, vz, v), NN, preferred_element_type=F32))

        units([(st, 2 * i + 1, BLOCK) for st in range(ns)] + [(st, 2 * i, 0) for st in range(ns)])

        def pair(p, carry):
            jp = i - 1 - p
            units([(st, 2 * jp + 1, None) for st in range(ns)] + [(st, 2 * jp, None) for st in range(ns)])
            return carry

        lax.fori_loop(0, i, pair, 0)
        for st in range(ns):
            cols = slice(st * LANES, (st + 1) * LANES)
            o_ref[:, cols] = acc_ref[st].astype(BF16)
            tot_ref[:, cols] = jnp.where(lo_q, run_ref[st, 0:SB_Q, :], run_ref[st, SB_Q:2 * SB_Q, :])

    def seq(col0):
        return pl.BlockSpec((s, wide), lambda b, h, i: (b, col0 // ns + h))

    blk = pl.BlockSpec((SB_Q, wide), lambda b, h, i: (b * nq + i, h))
    return _call(
        body, name="attn_b_fwd", grid=(batch, n_pairs // ns, nq),
        in_specs=[pl.BlockSpec((SB_Q, wide), lambda b, h, i: (b * nq + i, q_col0 // ns + h)), seq(k_col0), seq(v_col0)],
        out_specs=[blk, blk],
        out_shape=[jax.ShapeDtypeStruct((t, n_pairs * LANES), BF16), jax.ShapeDtypeStruct((t, n_pairs * LANES), F32)],
        scratch=[pltpu.VMEM((ns, SB_Q, LANES), F32), pltpu.VMEM((ns, 2 * SB_Q, LANES), F32)],
        sem=("parallel", "parallel", "arbitrary"), args=(proj, proj, proj), ride=ride)


def _sb_bwd(proj, q_col0, k_col0, v_col0, tot, do, batch, s, ride=None, streams=BWD_STREAMS):
    t = proj.shape[0]
    nq = s // SB_Q
    n_pairs = 4
    ns = streams
    wide = ns * LANES
    scale = HEAD_DIM ** -0.5

    def body(q_ref, k_ref, v_ref, tot_ref, do_ref, dq_ref, dk_ref, dv_ref, dk_acc, dv_acc, dq_acc, seen_ref, gsum_ref):
        i = pl.program_id(2)
        lo_q = lax.broadcasted_iota(jnp.int32, (SB_Q, LANES), 1) < HEAD_DIM
        lo_k = _lane_lo()

        @pl.when(i == 0)
        def _():
            dk_acc[...] = jnp.zeros_like(dk_acc)
            dv_acc[...] = jnp.zeros_like(dv_acc)

        mat_after = _sb_consts(True)
        mat_before = _sb_consts(False)
        row = lax.broadcasted_iota(jnp.int32, (2 * SB_Q, LANES), 0) % SB_Q
        ahead = row - lax.broadcasted_iota(jnp.int32, (2 * SB_Q, LANES), 1)
        dq_acc[...] = jnp.zeros_like(dq_acc)
        seen_ref[...] = jnp.zeros_like(seen_ref)
        gsum_ref[...] = jnp.zeros_like(gsum_ref)
        qqs, dds, totals = [], [], []
        for st in range(ns):
            cols = slice(st * LANES, (st + 1) * LANES)
            qqs.append(_stack_heads(q_ref[:, cols] * scale, lo_q))
            dds.append(_stack_heads(do_ref[:, cols], lo_q))
            tot_t = tot_ref[:, cols]
            totals.append(jnp.concatenate([jnp.broadcast_to(tot_t[:, 0:1], (SB_Q, LANES)),
                                           jnp.broadcast_to(tot_t[:, HEAD_DIM:HEAD_DIM + 1], (SB_Q, LANES))], axis=0))

        def units(todo):
            def rows(j):
                return pl.ds(pl.multiple_of(j * BLOCK, BLOCK), BLOCK)

            def cols(st):
                return slice(st * LANES, (st + 1) * LANES)

            prods = [(lax.dot_general(qqs[st], k_ref[rows(j), cols(st)], NT, preferred_element_type=F32),
                      lax.dot_general(dds[st], v_ref[rows(j), cols(st)], NT, preferred_element_type=F32))
                     for st, j, _ in todo]
            logs = []
            for (z, _), (_, _, off) in zip(prods, todo):
                lsig = jnp.minimum(z, 0.0) - jnp.log(1.0 + jnp.exp(-jnp.abs(z)))
                lneg = lsig - z
                if off is not None:
                    lneg = jnp.where(ahead > off, lneg, 0.0)
                logs.append((lsig, _split(lneg)))
            sums = [lax.dot_general(cat, mat_after, NN, preferred_element_type=F32) for _, cat in logs]
            gates = []
            for (lsig, _), sm, (_, da), (st, _, off) in zip(logs, sums, prods, todo):
                seen = seen_ref[st]
                a = jnp.exp(lsig + (totals[st] - seen - sm[:, BLOCK:]) + sm[:, :BLOCK])
                if off is not None:
                    a = jnp.where(ahead > off, a, 0.0)
                seen_ref[st] = seen + sm[:, BLOCK:]
                g = a * da
                gates.append((a.astype(BF16), g, _split(g)))
            gsums = [lax.dot_general(cat, mat_before, NN, preferred_element_type=F32) for _, _, cat in gates]
            outs = []
            for (lsig, _), (ab, g, _), gs, (st, _, off) in zip(logs, gates, gsums, todo):
                gsum = gsum_ref[st]
                dz = g - jnp.exp(lsig) * (g + gsum + gs[:, :BLOCK])
                if off is not None:
                    dz = jnp.where(ahead > off, dz, 0.0)
                gsum_ref[st] = gsum + gs[:, BLOCK:]
                outs.append((dz.astype(BF16), ab))
            for (dzb, ab), (st, j, _) in zip(outs, todo):
                k = k_ref[rows(j), cols(st)]
                kz = jnp.zeros_like(k)
                dq_acc[st] += (lax.dot_general(dzb[:SB_Q], jnp.where(lo_k, k, kz), NN, preferred_element_type=F32)
                               + lax.dot_general(dzb[SB_Q:], jnp.where(lo_k, kz, k), NN, preferred_element_type=F32))
                dk_acc[rows(j), cols(st)] += lax.dot_general(dzb, qqs[st], TN, preferred_element_type=F32)
                dv_acc[rows(j), cols(st)] += lax.dot_general(ab, dds[st], TN, preferred_element_type=F32)

        def pair(p, carry):
            units([(st, 2 * p, None) for st in range(ns)] + [(st, 2 * p + 1, None) for st in range(ns)])
            return carry

        lax.fori_loop(0, i, pair, 0)
        units([(st, 2 * i, 0) for st in range(ns)] + [(st, 2 * i + 1, BLOCK) for st in range(ns)])
        for st in range(ns):
            dq_ref[:, st * LANES:(st + 1) * LANES] = (dq_acc[st] * scale).astype(BF16)

        @pl.when(i == nq - 1)
        def _():
            dk_ref[...] = dk_acc[...].astype(BF16)
            dv_ref[...] = dv_acc[...].astype(BF16)

    def seq_in(col0):
        return pl.BlockSpec((s, wide), lambda b, h, i: (b, col0 // ns + h))

    blk = pl.BlockSpec((SB_Q, wide), lambda b, h, i: (b * nq + i, h))
    seq = pl.BlockSpec((s, wide), lambda b, h, i: (b, h))
    out = jax.ShapeDtypeStruct((t, n_pairs * LANES), BF16)
    return _call(
        body, name="attn_b_bwd", grid=(batch, n_pairs // ns, nq),
        in_specs=[pl.BlockSpec((SB_Q, wide), lambda b, h, i: (b * nq + i, q_col0 // ns + h)), seq_in(k_col0),
                  seq_in(v_col0), blk, blk],
        out_specs=[blk, seq, seq], out_shape=[out, out, out],
        scratch=[pltpu.VMEM((s, wide), F32), pltpu.VMEM((s, wide), F32), pltpu.VMEM((ns, SB_Q, LANES), F32),
                 pltpu.VMEM((ns, 2 * SB_Q, LANES), F32), pltpu.VMEM((ns, 2 * SB_Q, LANES), F32)],
        sem=("parallel", "parallel", "arbitrary"), args=(proj, proj, proj, tot, do), ride=ride)


MEM_Q_TILE = 256


def _mem_fwd(q, kv, batch, s, n_mem):
    t, width = q.shape
    tq = min(MEM_Q_TILE, s)
    nq = s // tq
    scale = MEM_HEAD_DIM ** -0.5

    def body(q_ref, kv_ref, o_ref):
        for h in range(N_HEADS_MEM):
            cols = slice(h * MEM_HEAD_DIM, (h + 1) * MEM_HEAD_DIM)
            k = kv_ref[:, cols]
            v = kv_ref[:, width + h * MEM_HEAD_DIM: width + (h + 1) * MEM_HEAD_DIM]
            sc = lax.dot_general(q_ref[:, cols], k, NT, preferred_element_type=F32) * scale
            p = jnp.exp(sc - jnp.max(sc, axis=1, keepdims=True))
            p = p / jnp.sum(p, axis=1, keepdims=True)
            o_ref[:, cols] = lax.dot_general(p.astype(BF16), v, NN, preferred_element_type=F32).astype(BF16)

    return pl.pallas_call(
        body, name="mem_attn_fwd", grid=(batch, nq),
        in_specs=[pl.BlockSpec((tq, width), lambda b, i: (b * nq + i, 0)),
                  pl.BlockSpec((n_mem, 2 * width), lambda b, i: (b, 0))],
        out_specs=pl.BlockSpec((tq, width), lambda b, i: (b * nq + i, 0)),
        out_shape=jax.ShapeDtypeStruct((t, width), BF16),
        compiler_params=_params(("parallel", "parallel")),
    )(q, kv)


def _mem_bwd(q, kv, do, batch, s, n_mem):
    t, width = q.shape
    tq = min(MEM_Q_TILE, s)
    nq = s // tq
    scale = MEM_HEAD_DIM ** -0.5

    def body(q_ref, kv_ref, do_ref, dq_ref, dkv_ref, acc):
        i = pl.program_id(1)

        @pl.when(i == 0)
        def _():
            acc[...] = jnp.zeros_like(acc)

        for h in range(N_HEADS_MEM):
            cols = slice(h * MEM_HEAD_DIM, (h + 1) * MEM_HEAD_DIM)
            vcols = slice(width + h * MEM_HEAD_DIM, width + (h + 1) * MEM_HEAD_DIM)
            qh, k, v, doh = q_ref[:, cols], kv_ref[:, cols], kv_ref[:, vcols], do_ref[:, cols]
            sc = lax.dot_general(qh, k, NT, preferred_element_type=F32) * scale
            p = jnp.exp(sc - jnp.max(sc, axis=1, keepdims=True))
            p = p / jnp.sum(p, axis=1, keepdims=True)
            dp = lax.dot_general(doh, v, NT, preferred_element_type=F32)
            ds = (p * (dp - jnp.sum(p * dp, axis=1, keepdims=True)) * scale).astype(BF16)
            dq_ref[:, cols] = lax.dot_general(ds, k, NN, preferred_element_type=F32).astype(BF16)
            acc[:, cols] += lax.dot_general(ds, qh, TN, preferred_element_type=F32)
            acc[:, vcols] += lax.dot_general(p.astype(BF16), doh, TN, preferred_element_type=F32)

        @pl.when(i == nq - 1)
        def _():
            dkv_ref[...] = acc[...].astype(BF16)

    row = pl.BlockSpec((tq, width), lambda b, i: (b * nq + i, 0))
    kvs = pl.BlockSpec((n_mem, 2 * width), lambda b, i: (b, 0))
    return pl.pallas_call(
        body, name="mem_attn_bwd", grid=(batch, nq),
        in_specs=[row, kvs, row], out_specs=[row, kvs],
        out_shape=[jax.ShapeDtypeStruct((t, width), BF16), jax.ShapeDtypeStruct((batch * n_mem, 2 * width), BF16)],
        scratch_shapes=[pltpu.VMEM((n_mem, 2 * width), F32)],
        compiler_params=_params(("parallel", "arbitrary")),
    )(q, kv, do)


def _mixer_fwd(o_a, o_b, w_a, w_b, proj, gate_col0):
    t, width = o_a.shape
    d = w_a.shape[1]
    tm = min(ROW_TILE, t)
    gb0 = gate_col0 * LANES // d

    def body(oa_ref, ob_ref, wa_ref, wb_ref, ga_ref, gb_ref, ua_ref, ub_ref, mix_ref):
        ua = lax.dot_general(oa_ref[...], wa_ref[...], NN, preferred_element_type=F32)
        ub = lax.dot_general(ob_ref[...], wb_ref[...], NN, preferred_element_type=F32)
        ua_ref[...] = ua.astype(BF16)
        ub_ref[...] = ub.astype(BF16)
        mix_ref[...] = (jax.nn.sigmoid(ga_ref[...].astype(F32)) * ua
                        + jax.nn.sigmoid(gb_ref[...].astype(F32)) * ub).astype(BF16)

    row = pl.BlockSpec((tm, width), lambda i: (i, 0))
    wsp = pl.BlockSpec((width, d), lambda i: (0, 0))
    out = pl.BlockSpec((tm, d), lambda i: (i, 0))
    osh = jax.ShapeDtypeStruct((t, d), BF16)
    return pl.pallas_call(
        body, name="mixer_fwd", grid=(t // tm,),
        in_specs=[row, row, wsp, wsp,
                  pl.BlockSpec((tm, d), lambda i: (i, gb0)), pl.BlockSpec((tm, d), lambda i: (i, gb0 + 1))],
        out_specs=[out, out, out], out_shape=[osh, osh, osh],
        compiler_params=_params(("parallel",)),
    )(o_a, o_b, w_a, w_b, proj, proj)


def _mixer_bwd(dmix, ua, ub, proj, gate_col0):
    t, d = dmix.shape
    tm = min(ROW_TILE, t)
    nc = d // LANES

    def body(dm_ref, ua_ref, ub_ref, ga_ref, gb_ref, dua_ref, dub_ref, dg_ref):
        dm = dm_ref[...].astype(F32)
        sa = jax.nn.sigmoid(ga_ref[...].astype(F32))
        sb = jax.nn.sigmoid(gb_ref[...].astype(F32))
        dua_ref[...] = (dm * sa).astype(BF16)
        dub_ref[...] = (dm * sb).astype(BF16)
        dg_ref[:, 0:d] = (dm * ua_ref[...].astype(F32) * sa * (1.0 - sa)).astype(BF16)
        dg_ref[:, d:2 * d] = (dm * ub_ref[...].astype(F32) * sb * (1.0 - sb)).astype(BF16)

    row = pl.BlockSpec((tm, d), lambda i: (i, 0))
    return pl.pallas_call(
        body, name="mixer_bwd", grid=(t // tm,),
        in_specs=[row, row, row,
                  pl.BlockSpec((tm, d), lambda i: (i, gate_col0 // nc)),
                  pl.BlockSpec((tm, d), lambda i: (i, gate_col0 // nc + 1))],
        out_specs=[row, row, pl.BlockSpec((tm, 2 * d), lambda i: (i, 0))],
        out_shape=[jax.ShapeDtypeStruct((t, d), BF16), jax.ShapeDtypeStruct((t, d), BF16),
                   jax.ShapeDtypeStruct((t, 2 * d), BF16)],
        compiler_params=_params(("parallel",)),
    )(dmix, ua, ub, proj, proj)


FFN_COLS = 1024


def _ffn_up(n, w_gate, w_up):
    t, d = n.shape
    hidden = w_gate.shape[1]
    tm = min(ROW_TILE, t)
    tn = min(FFN_COLS, hidden)

    def body(n_ref, wg_ref, wu_ref, hg_ref, hu_ref, act_ref):
        hg = lax.dot_general(n_ref[...], wg_ref[...], NN, preferred_element_type=F32)
        hu = lax.dot_general(n_ref[...], wu_ref[...], NN, preferred_element_type=F32)
        hg_ref[...] = hg.astype(BF16)
        hu_ref[...] = hu.astype(BF16)
        act_ref[...] = (hg * jax.nn.sigmoid(hg) * hu).astype(BF16)

    wsp = pl.BlockSpec((d, tn), lambda j, i: (0, j))
    out = pl.BlockSpec((tm, tn), lambda j, i: (i, j))
    osh = jax.ShapeDtypeStruct((t, hidden), BF16)
    return pl.pallas_call(
        body, name="ffn_up", grid=(hidden // tn, t // tm),
        in_specs=[pl.BlockSpec((tm, d), lambda j, i: (i, 0)), wsp, wsp],
        out_specs=[out, out, out], out_shape=[osh, osh, osh],
        compiler_params=_params(("parallel", "parallel")),
    )(n, w_gate, w_up)


def _ffn_bwd_act(dh, w_down, hg, hu):
    t, d = dh.shape
    hidden = w_down.shape[0]
    tm = min(ROW_TILE, t)
    tn = min(FFN_COLS, hidden)

    def body(dh_ref, wd_ref, hg_ref, hu_ref, dhg_ref, dhu_ref):
        dact = lax.dot_general(dh_ref[...], wd_ref[...], NT, preferred_element_type=F32)
        hg = hg_ref[...].astype(F32)
        sg = jax.nn.sigmoid(hg)
        dhu_ref[...] = (dact * hg * sg).astype(BF16)
        dhg_ref[...] = (dact * hu_ref[...].astype(F32) * sg * (1.0 + hg * (1.0 - sg))).astype(BF16)

    hid = pl.BlockSpec((tm, tn), lambda j, i: (i, j))
    osh = jax.ShapeDtypeStruct((t, hidden), BF16)
    return pl.pallas_call(
        body, name="ffn_bwd_act", grid=(hidden // tn, t // tm),
        in_specs=[pl.BlockSpec((tm, d), lambda j, i: (i, 0)), pl.BlockSpec((tn, d), lambda j, i: (j, 0)), hid, hid],
        out_specs=[hid, hid], out_shape=[osh, osh],
        compiler_params=_params(("parallel", "parallel")),
    )(dh, w_down, hg, hu)


MM_ROWS = 1024


def _mm_cols_t(name, a, w, out_dtype, res=None):
    t = a.shape[0]
    n_sh, k, cs = w.shape
    tm = min(MM_ROWS, t)
    o_spec = pl.BlockSpec((tm, k), lambda i, j: (i, 0))
    return _mm(name, a, w, grid=(t // tm, n_sh),
               a_spec=pl.BlockSpec((tm, cs), lambda i, j: (i, j)), b_spec=pl.BlockSpec((None, k, cs), lambda i, j: (j, 0, 0)),
               o_shape=(t, k), o_spec=o_spec, dims=NT, out_dtype=out_dtype, nk=n_sh, res=res,
               res_spec=o_spec if res is not None else None)


def _mm_w(name, a, w, out_dtype, dims=NN, res=None, tm=MM_ROWS, tn=1024):
    t, k = a.shape
    n = w.shape[1] if dims == NN else w.shape[0]
    tm, tn = min(tm, t), min(tn, n)
    o_spec = pl.BlockSpec((tm, tn), lambda j, i: (i, j))
    b_spec = pl.BlockSpec((k, tn), lambda j, i: (0, j)) if dims == NN else pl.BlockSpec((tn, k), lambda j, i: (j, 0))
    return _mm(name, a, w, grid=(n // tn, t // tm), a_spec=pl.BlockSpec((tm, k), lambda j, i: (i, 0)), b_spec=b_spec,
               o_shape=(t, n), o_spec=o_spec, dims=dims, out_dtype=out_dtype, res=res,
               res_spec=o_spec if res is not None else None)


def _wgrad(name, a, g, tk=1024, tn=1024):
    t, k = a.shape
    n = g.shape[1]
    tm, tk, tn = min(MM_ROWS, t), min(tk, k), min(tn, n)
    return _mm(name, a, g, grid=(k // tk, n // tn, t // tm),
               a_spec=pl.BlockSpec((tm, tk), lambda p, q, r: (r, p)), b_spec=pl.BlockSpec((tm, tn), lambda p, q, r: (r, q)),
               o_shape=(k, n), o_spec=pl.BlockSpec((tk, tn), lambda p, q, r: (p, q)), dims=TN, out_dtype=BF16, nk=t // tm)


def _peers():
    x, y, c = lax.axis_index("x"), lax.axis_index("y"), lax.axis_index("c")
    me = 4 * x + 2 * y + c
    out = []
    for k in range(1, N_DEV):
        kx, ky, kc = (k >> 2) & 1, (k >> 1) & 1, k & 1
        px = 1 - x if kx else x
        py = 1 - y if ky else y
        pc = 1 - c if kc else c
        out.append(((px, py, pc), 4 * px + 2 * py + pc))
    return me, out


def _cast_weights(ws, pads):
    def body(*refs):
        n = len(refs) // 2
        for i_ref, o_ref, (pr, pc) in zip(refs[:n], refs[n:], pads):
            r, c = i_ref.shape
            o_ref[0:r, 0:c] = i_ref[...].astype(BF16)
            if pr:
                o_ref[r:r + pr, :] = jnp.zeros((pr, c), BF16)
            if pc:
                o_ref[:, c:c + pc] = jnp.zeros((r, pc), BF16)

    return pl.pallas_call(
        body, name="cast_weights", in_specs=[VMEM] * len(ws), out_specs=[VMEM] * len(ws),
        out_shape=[jax.ShapeDtypeStruct((w.shape[0] + pr, w.shape[1] + pc), BF16) for w, (pr, pc) in zip(ws, pads)],
    )(*ws)


def _window(ref, j, c):
    return ref.at[:, pl.ds(pl.multiple_of(j * c, LANES), c)]


def _scatter_copies(ins, outs, sems, cols, landed):
    send_sems, recv_sems, loc_sems = sems
    n_peer = N_DEV - 1
    me, peers = _peers()

    def src(w, j):
        return _window(ins[w], j, cols[w]) if cols[w] else ins[w].at[j]

    local = [pltpu.make_async_copy(src(w, me), outs[w].at[me], loc_sems.at[w]) for w in range(len(ins))]
    remote = [pltpu.make_async_remote_copy(
        src_ref=src(w, idx), dst_ref=outs[w].at[idx if landed else me],
        send_sem=send_sems.at[w * n_peer + k], recv_sem=recv_sems.at[w * n_peer + k],
        device_id=dev, device_id_type=pl.DeviceIdType.MESH)
        for k, (dev, idx) in reversed(list(enumerate(peers))) for w in range(len(ins))]
    return local, remote


OTHER_CHIPS = (2, 4, 6)


def _gather_copies(ins, outs, sems, cols):
    send_sems, recv_sems, loc_sems = sems
    x, y, c = lax.axis_index("x"), lax.axis_index("y"), lax.axis_index("c")
    me = 4 * x + 2 * y + c
    n_pair = N_DEV - 1

    def dev(mask):
        return (1 - x if mask & 4 else x, 1 - y if mask & 2 else y, 1 - c if mask & 1 else c)

    def slot(w, mask):
        j = jnp.bitwise_xor(me, mask)
        return _window(outs[w], j, cols[w]) if cols[w] else outs[w].at[j]

    def remote(w, pair, src, to_slot, target):
        return pltpu.make_async_remote_copy(src_ref=src, dst_ref=slot(w, to_slot), send_sem=send_sems.at[w * n_pair + pair],
                                            recv_sem=recv_sems.at[w * n_pair + pair], device_id=dev(target),
                                            device_id_type=pl.DeviceIdType.MESH)

    ws = range(len(ins))
    return dict(
        local=[pltpu.make_async_copy(ins[w], slot(w, 0), loc_sems.at[w]) for w in ws],
        to_chips=[remote(w, 1 + t, ins[w], 0, m) for t, m in enumerate(OTHER_CHIPS) for w in ws],
        to_core=[remote(w, 0, ins[w], 0, 1) for w in ws],
        from_chips=[remote(w, 1 + t, ins[w], m, 0) for t, m in enumerate(OTHER_CHIPS) for w in ws],
        pass_on=[remote(w, 4 + t, slot(w, m), m, 1) for t, m in enumerate(OTHER_CHIPS) for w in ws],
        from_core=[remote(w, 0, ins[w], 1, 0) for w in ws]
        + [remote(w, 4 + t, ins[w], m + 1, 0) for t, m in enumerate(OTHER_CHIPS) for w in ws])


def _exchange_start(ins, outs, sems, gather, cols):
    if gather:
        cps = _gather_copies(ins, outs, sems, cols)
        for cp in cps["local"] + cps["to_chips"] + cps["to_core"]:
            cp.start()
    else:
        local, remote = _scatter_copies(ins, outs, sems, cols, False)
        for cp in local + remote:
            cp.start()


def _exchange_pass_on(ins, outs, sems, gather, cols, chips):
    if gather:
        cps = _gather_copies(ins, outs, sems, cols)
        n = len(ins)
        for t in chips:
            for arrived, onward in zip(cps["from_chips"][t * n:(t + 1) * n], cps["pass_on"][t * n:(t + 1) * n]):
                arrived.wait_recv()
                onward.start()


def _exchange_wait(ins, outs, sems, gather, cols):
    if gather:
        cps = _gather_copies(ins, outs, sems, cols)
        for cp in cps["local"]:
            cp.wait()
        for cp in cps["to_chips"] + cps["to_core"] + cps["pass_on"]:
            cp.wait_send()
        for cp in cps["from_core"]:
            cp.wait_recv()
    else:
        local, remote = _scatter_copies(ins, outs, sems, cols, True)
        for cp in local:
            cp.wait()
        for cp in remote:
            cp.wait_send()
            cp.wait_recv()


def _exchange_shapes(arrs, gather, cols):
    n = len(arrs)
    out_shape = []
    for a, c in zip(arrs, cols):
        if gather:
            shape = (a.shape[0], N_DEV * c) if c else (N_DEV,) + a.shape
        else:
            shape = (N_DEV, a.shape[0], c) if c else a.shape
        out_shape.append(jax.ShapeDtypeStruct(shape, a.dtype))
    sems = [pltpu.SemaphoreType.DMA((n * (N_DEV - 1),)), pltpu.SemaphoreType.DMA((n * (N_DEV - 1),)),
            pltpu.SemaphoreType.DMA((n,))]
    return out_shape, sems


def _call(body, *, name, grid, in_specs, out_specs, out_shape, scratch, sem, args, ride=None):
    if ride is None:
        outs = pl.pallas_call(body, name=name, grid=grid, in_specs=in_specs, out_specs=out_specs, out_shape=out_shape,
                              scratch_shapes=scratch, compiler_params=_params(sem))(*args)
        return outs, None
    arrs, gather, cols = ride
    n, n_in, n_out, n_scr = len(arrs), len(in_specs), len(out_specs), len(scratch)
    x_shape, x_sems = _exchange_shapes(arrs, gather, cols)

    def riding(*refs):
        ins, x_ins = refs[:n_in], refs[n_in:n_in + n]
        outs = refs[n_in + n:n_in + n + n_out]
        x_outs = refs[n_in + n + n_out:n_in + 2 * n + n_out]
        scr = refs[n_in + 2 * n + n_out:n_in + 2 * n + n_out + n_scr]
        sems = refs[n_in + 2 * n + n_out + n_scr:]
        def at(step):
            return functools.reduce(jnp.logical_and, [pl.program_id(a) == v for a, v in enumerate(step)])

        @pl.when(at((0,) * len(grid)))
        def _():
            _exchange_start(x_ins, x_outs, sems, gather, cols)

        @pl.when(at((grid[0] // 2,) + (0,) * (len(grid) - 1)))
        def _():
            _exchange_pass_on(x_ins, x_outs, sems, gather, cols, (0, 1))

        @pl.when(at((grid[0] // 2,) + (0,) * (len(grid) - 2) + (5 * grid[-1] // 8,)))
        def _():
            _exchange_pass_on(x_ins, x_outs, sems, gather, cols, (2,))

        body(*ins, *outs, *scr)

        @pl.when(at(tuple(g - 1 for g in grid)))
        def _():
            _exchange_wait(x_ins, x_outs, sems, gather, cols)

    res = pl.pallas_call(
        riding, name=name, grid=grid, in_specs=list(in_specs) + [ANY] * n, out_specs=list(out_specs) + [ANY] * n,
        out_shape=list(out_shape) + x_shape, scratch_shapes=list(scratch) + x_sems,
        compiler_params=_params(("arbitrary",) * len(grid)))(*args, *arrs)
    return res[:n_out], res[n_out:]


def _my_block():
    return (4 * lax.axis_index("x") + 2 * lax.axis_index("y") + lax.axis_index("c")).astype(jnp.int32).reshape(1)


def _proj_in_gather(n, w_shard):
    t, k = n.shape
    cs = w_shard.shape[1]
    tm = min(MM_ROWS, t)
    ni = t // tm
    arrival = (0, 1) + OTHER_CHIPS + tuple(m + 1 for m in OTHER_CHIPS)

    def mask_at(s):
        return jnp.where(s < 2, s, jnp.where(s < 5, 2 * (s - 1), 2 * (s - 4) + 1))

    def body(me_ref, n_ref, w_hbm, o_ref, all_hbm, w_vmem, send_sems, recv_sems, loc_sems, load_sems):
        s, i = pl.program_id(0), pl.program_id(1)
        cps = _gather_copies([w_hbm], [all_hbm], (send_sems, recv_sems, loc_sems), (0,))
        arrived = cps["local"] + cps["from_core"][:1] + cps["from_chips"] + cps["from_core"][1:]

        def load(step):
            src = w_hbm if step == 0 else all_hbm.at[jnp.bitwise_xor(me_ref[0], arrival[step])]
            return pltpu.make_async_copy(src, w_vmem.at[step % 2], load_sems.at[step % 2])

        @pl.when(jnp.logical_and(s == 0, i == 0))
        def _():
            for cp in cps["local"] + cps["to_chips"] + cps["to_core"]:
                cp.start()
            load(0).start()

        for step, mask in enumerate(arrival):
            @pl.when(jnp.logical_and(s == step, i == 0))
            def _(step=step):
                load(step).wait()

            if step + 1 < N_DEV:
                @pl.when(jnp.logical_and(s == step, i == min(1, ni - 1)))
                def _(step=step):
                    arrived[step + 1].wait_recv()
                    if arrival[step + 1] in OTHER_CHIPS:
                        cps["pass_on"][OTHER_CHIPS.index(arrival[step + 1])].start()
                    load(step + 1).start()

        o_ref[...] = lax.dot_general(n_ref[...], w_vmem[s % 2], NN, preferred_element_type=F32).astype(BF16)

        @pl.when(jnp.logical_and(s == N_DEV - 1, i == ni - 1))
        def _():
            cps["local"][0].wait()
            for cp in cps["to_chips"] + cps["to_core"] + cps["pass_on"]:
                cp.wait_send()

    return pl.pallas_call(
        body, name="proj_in",
        grid_spec=pltpu.PrefetchScalarGridSpec(
            num_scalar_prefetch=1, grid=(N_DEV, ni),
            in_specs=[pl.BlockSpec((tm, k), lambda s, i, me: (i, 0)), ANY],
            out_specs=[pl.BlockSpec((tm, cs), lambda s, i, me: (i, jnp.bitwise_xor(me[0], mask_at(s)))), ANY],
            scratch_shapes=[pltpu.VMEM((2, k, cs), BF16), pltpu.SemaphoreType.DMA((N_DEV - 1,)),
                            pltpu.SemaphoreType.DMA((N_DEV - 1,)), pltpu.SemaphoreType.DMA((1,)),
                            pltpu.SemaphoreType.DMA((2,))]),
        out_shape=[jax.ShapeDtypeStruct((t, N_DEV * cs), BF16), jax.ShapeDtypeStruct((N_DEV, k, cs), BF16)],
        compiler_params=_params(("arbitrary", "arbitrary")),
    )(_my_block(), n, w_shard)


def _gw_in_scatter(a, g):
    t, k = a.shape
    cs = g.shape[1] // N_DEV
    tm = min(MM_ROWS, t)
    nr = t // tm
    n_chip = N_DEV // 2
    chips = (6, 4, 2, 0)

    def body(me_ref, a_ref, g_ref, out_hbm, acc, stage, other, core_send, core_recv, chip_send, chip_recv, loc_sem):
        s, r = pl.program_id(0), pl.program_id(1)
        x, y, c = lax.axis_index("x"), lax.axis_index("y"), lax.axis_index("c")
        my_chip = 2 * x + y
        part = lax.dot_general(a_ref[...], g_ref[...], TN, preferred_element_type=F32)

        def to_core(m):
            return pltpu.make_async_remote_copy(src_ref=stage.at[0], dst_ref=other.at[m], send_sem=core_send.at[m],
                                                recv_sem=core_recv.at[m], device_id=(x, y, 1 - c),
                                                device_id_type=pl.DeviceIdType.MESH)

        def to_chip(m, landed):
            mask = chips[m]
            there = (1 - x if mask & 4 else x, 1 - y if mask & 2 else y, c)
            slot = (2 * there[0] + there[1]) if landed else my_chip
            return pltpu.make_async_remote_copy(src_ref=stage.at[1], dst_ref=out_hbm.at[slot], send_sem=chip_send.at[m],
                                                recv_sem=chip_recv.at[m], device_id=there,
                                                device_id_type=pl.DeviceIdType.MESH)

        local = pltpu.make_async_copy(stage.at[1], out_hbm.at[my_chip], loc_sem)

        @pl.when(r == 0)
        def _():
            acc[...] = part

        @pl.when(r > 0)
        def _():
            acc[...] += part

        for step in range(N_DEV):
            m = step // 2

            @pl.when(jnp.logical_and(s == step, r == nr - 1))
            def _(step=step, m=m):
                if step % 2 == 0:
                    if m > 0:
                        to_core(m - 1).wait_send()
                    stage[0] = acc[...].astype(BF16)
                    to_core(m).start()
                else:
                    if m > 0:
                        to_chip(m - 1, False).wait_send()
                    to_core(m).wait_recv()
                    stage[1] = (acc[...] + other[m].astype(F32)).astype(BF16)
                    if m < n_chip - 1:
                        to_chip(m, False).start()
                    else:
                        local.start()
                        to_core(m).wait_send()
                        local.wait()
                        for mm in range(n_chip - 1):
                            to_chip(mm, True).wait_recv()

    return pl.pallas_call(
        body, name="gw_in",
        grid_spec=pltpu.PrefetchScalarGridSpec(
            num_scalar_prefetch=1, grid=(N_DEV, nr),
            in_specs=[pl.BlockSpec((tm, k), lambda s, r, me: (r, 0)),
                      pl.BlockSpec((tm, cs), lambda s, r, me: (r, jnp.bitwise_xor(me[0], N_DEV - 1 - s)))],
            out_specs=ANY,
            scratch_shapes=[pltpu.VMEM((k, cs), F32), pltpu.VMEM((2, k, cs), BF16), pltpu.VMEM((n_chip, k, cs), BF16),
                            pltpu.SemaphoreType.DMA((n_chip,)), pltpu.SemaphoreType.DMA((n_chip,)),
                            pltpu.SemaphoreType.DMA((n_chip - 1,)), pltpu.SemaphoreType.DMA((n_chip - 1,)),
                            pltpu.SemaphoreType.DMA]),
        out_shape=jax.ShapeDtypeStruct((n_chip, k, cs), BF16),
        compiler_params=_params(("arbitrary", "arbitrary")),
    )(_my_block(), a, g)


SMALL_ROWS = 8


def _allreduce_small(parts, loss_part):
    n, d = len(parts), parts[0].shape[1]

    def body(*refs):
        part_refs, loss_ref, o_ref = refs[:n], refs[n], refs[n + 1]
        mine_ref, all_ref, send_sems, recv_sems = refs[n + 2:]
        me, peers = _peers()
        mine_ref[...] = jnp.zeros_like(mine_ref)
        for i, p_ref in enumerate(part_refs):
            mine_ref[i:i + 1, :] = p_ref[...]
        mine_ref[SMALL_ROWS - 1:SMALL_ROWS, 0:LANES] = loss_ref[0:1, :]
        all_ref[me] = mine_ref[...]
        for k, (dev, idx) in enumerate(peers):
            pltpu.make_async_remote_copy(src_ref=mine_ref, dst_ref=all_ref.at[me], send_sem=send_sems.at[k],
                                         recv_sem=recv_sems.at[k], device_id=dev,
                                         device_id_type=pl.DeviceIdType.MESH).start()
        for k, (dev, idx) in enumerate(peers):
            cp = pltpu.make_async_remote_copy(src_ref=mine_ref, dst_ref=all_ref.at[idx], send_sem=send_sems.at[k],
                                              recv_sem=recv_sems.at[k], device_id=dev,
                                              device_id_type=pl.DeviceIdType.MESH)
            cp.wait_send()
            cp.wait_recv()
        tot = all_ref[0]
        for dvc in range(1, N_DEV):
            tot = tot + all_ref[dvc]
        o_ref[...] = tot

    return pl.pallas_call(
        body, name="allreduce_small", in_specs=[VMEM] * (n + 1), out_specs=VMEM,
        out_shape=jax.ShapeDtypeStruct((SMALL_ROWS, d), F32),
        scratch_shapes=[pltpu.VMEM((SMALL_ROWS, d), F32), pltpu.VMEM((N_DEV, SMALL_ROWS, d), F32),
                        pltpu.SemaphoreType.DMA((N_DEV - 1,)), pltpu.SemaphoreType.DMA((N_DEV - 1,))],
    )(*parts, loss_part)


def _adam_math(g, w, m, v):
    m_new = ADAM_B1 * m + (1.0 - ADAM_B1) * g
    v_new = ADAM_B2 * v + (1.0 - ADAM_B2) * (g * g)
    m_hat = m_new / (1.0 - ADAM_B1 ** ADAM_STEP)
    v_hat = v_new / (1.0 - ADAM_B2 ** ADAM_STEP)
    delta = -ADAM_LR * (m_hat / (jnp.sqrt(v_hat) + ADAM_EPS) + ADAM_WD * w)
    return delta, m_new, v_new


def _adam(name, pieces, w, m, v):
    r, c = w.shape
    n_piece, _, cp = pieces.shape
    tr = r
    for cand in (256, 176, 128, 64):
        if r % cand == 0 and r > cand:
            tr = cand
            break

    def body(p_ref, w_ref, m_ref, v_ref, g_ref, d_ref, mo_ref, vo_ref):
        g = p_ref[0, :, 0:c].astype(F32)
        for j in range(1, n_piece):
            g = g + p_ref[j, :, 0:c].astype(F32)
        delta, m_new, v_new = _adam_math(g, w_ref[...], m_ref[...], v_ref[...])
        g_ref[...] = g
        d_ref[...] = delta
        mo_ref[...] = m_new
        vo_ref[...] = v_new

    blk = pl.BlockSpec((tr, c), lambda i: (i, 0))
    osh = jax.ShapeDtypeStruct((r, c), F32)
    return pl.pallas_call(
        body, name=name, grid=(r // tr,),
        in_specs=[pl.BlockSpec((n_piece, tr, cp), lambda i: (0, i, 0)), blk, blk, blk],
        out_specs=[blk, blk, blk, blk], out_shape=[osh, osh, osh, osh],
        compiler_params=_params(("parallel",)),
    )(pieces, w, m, v)


def _adam_small(g_all, ws, ms, vs):
    n = len(ws)

    def body(*refs):
        g_ref, ins, outs = refs[0], refs[1:1 + 3 * n], refs[1 + 3 * n:]
        for i in range(n):
            g = g_ref[i:i + 1, :]
            delta, m_new, v_new = _adam_math(g, ins[i][...], ins[n + i][...], ins[2 * n + i][...])
            for kind, val in enumerate((g, delta, m_new, v_new)):
                outs[kind * n + i][...] = val

    osh = jax.ShapeDtypeStruct(ws[0].shape, F32)
    res = pl.pallas_call(body, name="adam_small", in_specs=[VMEM] * (1 + 3 * n), out_specs=[VMEM] * (4 * n),
                         out_shape=[osh] * (4 * n))(g_all, *ws, *ms, *vs)
    return res[:n], res[n:2 * n], res[2 * n:3 * n], res[3 * n:]


def _local_step(x, mem, pos, tgt, gains, w_in_shard, shards, batch):
    g_mix, g_mem_q, g_mem_kv, g_ffn, g_final = gains
    t, d = x.shape
    s = t // batch
    n_mem = mem.shape[0] // batch
    n_sh = N_DEV
    width = shards[0].shape[0]
    nb = width // LANES

    lane = np.arange(LANES) % HEAD_DIM
    sel_lo = (lane < ROPE_HALF).astype(np.float32)[None, :]
    sel_hi = ((lane >= ROPE_HALF) & (lane < 2 * ROPE_HALF)).astype(np.float32)[None, :]
    freqs = np.float32(ROPE_THETA) ** (-np.arange(ROPE_HALF, dtype=np.float32) / np.float32(ROPE_HALF))
    inv_freq = np.where(lane < 2 * ROPE_HALF, freqs[lane % ROPE_HALF], 0.0).astype(np.float32)[None, :]
    cos_t, sin_a, sin_b = _rope_tables(pos, jnp.asarray(inv_freq), jnp.asarray(sel_lo), jnp.asarray(sel_hi))
    bias = _dilated_bias_tiles(s)

    n1 = _rms_fwd("norm_mix", x, g_mix)
    proj, w_in = _proj_in_gather(n1, w_in_shard)
    qk_a = _rope_apply("rope_fwd", proj, 0, 2 * nb, cos_t, sin_a, sin_b, 1.0)
    cs_up, cs_ffn = shards[0].shape[1], shards[6].shape[1]
    (o_a, lse_a), (w_up_a, w_up_b, w_out, w_q, w_kv, w_o, w_fd) = _da_fwd(
        qk_a, proj, 2 * nb, bias, batch, s,
        ride=(shards[:6] + shards[8:], True, (cs_up, cs_up, 0, 0, 0, cs_up, 0)))
    (o_b, tot_b), (w_fg, w_fu) = _sb_fwd(proj, 3 * nb, 4 * nb, 5 * nb, batch, s,
                                         ride=(shards[6:8], True, (cs_ffn, cs_ffn)))
    w_out = w_out.reshape(d, d)
    w_q = w_q.reshape(d, -1)
    w_kv = w_kv.reshape(d, -1)
    w_fd = w_fd.reshape(-1, d)
    ua, ub, mixed = _mixer_fwd(o_a, o_b, w_up_a, w_up_b, proj, 6 * nb)
    h1 = _mm_w("mix_out", mixed, w_out, F32, res=x)
    n2 = _rms_fwd("norm_mem_q", h1, g_mem_q)
    mem_n = _rms_fwd("norm_mem_kv", mem, g_mem_kv)
    q_m = _mm_w("mem_q", n2, w_q, BF16)
    kv_m = _mm_w("mem_kv", mem_n, w_kv, BF16)
    o_m = _mem_fwd(q_m, kv_m, batch, s, n_mem)
    h2 = _mm_w("mem_out", o_m, w_o, F32, res=h1)
    n3 = _rms_fwd("norm_ffn", h2, g_ffn)
    hg, hu, act = _ffn_up(n3, w_fg, w_fu)
    h3 = _mm_w("ffn_down", act, w_fd, F32, res=h2, tm=ROW_TILE)
    loss_part, dh3, dh3_b, dg_final = _loss_head(h3, tgt, g_final.reshape(1, d))

    dhg, dhu = _ffn_bwd_act(dh3_b, w_fd, hg, hu)
    gw_fd = _wgrad("gw_ffn_down", act, dh3_b)
    gw_fg = _wgrad("gw_ffn_gate", n3, dhg)
    gw_fu = _wgrad("gw_ffn_up", n3, dhu)
    dn3 = _mm_w("dn_ffn_gate", dhg, w_fg, F32, dims=NT, tm=ROW_TILE)
    dn3 = _mm_w("dn_ffn_up", dhu, w_fu, F32, dims=NT, res=dn3, tm=ROW_TILE)
    dh2, dh2_b, dg_ffn = _rms_bwd("norm_ffn_bwd", dn3, h2, g_ffn, dh3, ("f32", "bf16"))

    do_m = _mm_w("mem_out_bwd", dh2_b, w_o, BF16, dims=NT)
    gw_o = _wgrad("gw_mem_o", o_m, dh2_b)
    dq_m, dkv_m = _mem_bwd(q_m, kv_m, do_m, batch, s, n_mem)
    gw_q = _wgrad("gw_mem_q", n2, dq_m)
    gw_kv = _wgrad("gw_mem_kv", mem_n, dkv_m)
    dn2 = _mm_w("mem_q_bwd", dq_m, w_q, F32, dims=NT)
    dmem_n = _mm_w("mem_kv_bwd", dkv_m, w_kv, F32, dims=NT)
    (dg_mem_kv,) = _rms_bwd("norm_mem_kv_bwd", dmem_n, mem, g_mem_kv, None, ())
    dh1, dh1_b, dg_mem_q = _rms_bwd("norm_mem_q_bwd", dn2, h1, g_mem_q, dh2, ("f32", "bf16"))

    dmix = _mm_w("mix_out_bwd", dh1_b, w_out, BF16, dims=NT)
    gw_out = _wgrad("gw_out", mixed, dh1_b)
    dua, dub, dgates = _mixer_bwd(dmix, ua, ub, proj, 6 * nb)
    do_a = _mm_w("up_a_bwd", dua, w_up_a, BF16, dims=NT)
    do_b = _mm_w("up_b_bwd", dub, w_up_b, BF16, dims=NT)
    gw_ua = _wgrad("gw_up_a", o_a, dua)
    gw_ub = _wgrad("gw_up_b", o_b, dub)
    (dq_ar, dk_ar, dv_a), (p_fg, p_fd) = _da_bwd(
        qk_a, proj, 2 * nb, bias, o_a, lse_a, do_a, batch, s,
        ride=([gw_fg, gw_fd.reshape(n_sh, -1, d)], False, (cs_ffn, 0)))
    dqk_a = _rope_apply("rope_bwd", jnp.concatenate([dq_ar, dk_ar], axis=1), 0, 2 * nb, cos_t, sin_a, sin_b, -1.0)
    mid = [gw_ua, gw_ub, gw_out.reshape(n_sh, -1, d), gw_q.reshape(n_sh, -1, gw_q.shape[1]),
           gw_kv.reshape(n_sh, -1, gw_kv.shape[1]), gw_o, gw_fu]
    (dq_b, dk_b, dv_b), (*p_mid, p_fu) = _sb_bwd(proj, 3 * nb, 4 * nb, 5 * nb, tot_b, do_b, batch, s,
                                                 ride=(mid, False, (cs_up, cs_up, 0, 0, 0, cs_up, cs_ffn)))
    p_ffn = [p_fg, p_fu, p_fd]
    dproj = jnp.concatenate([dqk_a, dv_a, dq_b, dk_b, dv_b, dgates], axis=1)
    dn1 = _mm_cols_t("proj_in_bwd", dproj, w_in, F32)
    p_in = _gw_in_scatter(n1, dproj)
    grad_x, dg_mix = _rms_bwd("norm_mix_bwd", dn1, x, g_mix, dh1, ("f32",))
    return loss_part, grad_x, [p_in] + list(p_mid) + p_ffn, (dg_mix, dg_mem_q, dg_mem_kv, dg_ffn, dg_final)


WEIGHTS =("w_in", "w_up_a", "w_up_b", "w_out", "w_q_mem", "w_kv_mem", "w_o_mem", "w_ffn_gate", "w_ffn_up", "w_ffn_down")
GAINS = ("g_mix", "g_mem_q", "g_mem_kv", "g_ffn", "g_final")
ORDER = ("g_mix", "w_in", "w_up_a", "w_up_b", "w_out", "g_mem_q", "g_mem_kv", "w_q_mem", "w_kv_mem", "w_o_mem", "g_ffn",
         "w_ffn_gate", "w_ffn_up", "w_ffn_down", "g_final")


def kernel(x, mem, positions, g_mix, w_in, w_up_a, w_up_b, w_out, g_mem_q, g_mem_kv, w_q_mem, w_kv_mem, w_o_mem, g_ffn, w_ffn_gate, w_ffn_up, w_ffn_down, g_final, loss_target, m_g_mix, m_w_in, m_w_up_a, m_w_up_b, m_w_out, m_g_mem_q, m_g_mem_kv, m_w_q_mem, m_w_kv_mem, m_w_o_mem, m_g_ffn, m_w_ffn_gate, m_w_ffn_up, m_w_ffn_down, m_g_final, v_g_mix, v_w_in, v_w_up_a, v_w_up_b, v_w_out, v_g_mem_q, v_g_mem_kv, v_w_q_mem, v_w_kv_mem, v_w_o_mem, v_g_ffn, v_w_ffn_gate, v_w_ffn_up, v_w_ffn_down, v_g_final):
    given = dict(locals())
    batch, s, d = x.shape
    t = batch * s
    shard = {n: given[n].reshape(given[n].shape[-2:]) for n in WEIGHTS}
    gains = [given[n].reshape(1, d) for n in GAINS]

    pad = (-shard["w_ffn_down"].shape[0]) % LANES
    pads = {"w_ffn_gate": (0, pad), "w_ffn_up": (0, pad), "w_ffn_down": (pad, 0)}
    cast = _cast_weights([shard[n] for n in WEIGHTS], [pads.get(n, (0, 0)) for n in WEIGHTS])
    loss_part, grad_x, pieces, dgains = _local_step(
        x.reshape(t, d), mem.reshape(-1, d), positions.reshape(t, 1), loss_target.reshape(t, d), gains, cast[0],
        cast[1:], batch)

    grad, delta, new_m, new_v = {}, {}, {}, {}
    for n, p in zip(WEIGHTS, pieces):
        m2, v2 = given["m_" + n].reshape(shard[n].shape), given["v_" + n].reshape(shard[n].shape)
        outs = _adam("adam_" + n, p, shard[n], m2, v2)
        grad[n], delta[n], new_m[n], new_v[n] = [o.reshape(given[n].shape) for o in outs]

    g_all = _allreduce_small(list(dgains), loss_part)
    small = _adam_small(g_all, gains, [given["m_" + n].reshape(1, d) for n in GAINS],
                        [given["v_" + n].reshape(1, d) for n in GAINS])
    for out, vals in zip((grad, delta, new_m, new_v), small):
        for n, val in zip(GAINS, vals):
            out[n] = val.reshape(given[n].shape)

    loss = g_all[SMALL_ROWS - 1, 0]
    return (loss, grad_x.reshape(x.shape), *[grad[n] for n in ORDER], *[delta[n] for n in ORDER],
            *[new_m[n] for n in ORDER], *[new_v[n] for n in ORDER])
```

```python
import functools
import math

import jax
import jax.numpy as jnp
import numpy as np
from jax import lax
from jax.experimental import pallas as pl
from jax.experimental.pallas import tpu as pltpu

F32 = jnp.float32
BF16 = jnp.bfloat16

N_DEV = 8
HEAD_DIM = 64
MEM_HEAD_DIM = 128
N_HEADS_MEM = 4
BLOCK = 128
DIL_PATTERNS = ((128, 1), (512, 4), (2048, 16))
ROPE_THETA = 500000.0
ROPE_HALF = 8
RMS_EPS = 1e-6
ADAM_LR, ADAM_B1, ADAM_B2, ADAM_EPS, ADAM_WD, ADAM_STEP = 0.001, 0.9, 0.999, 1e-08, 0.01, 10
NEG = -1e30
ROW_TILE = 512
LANES = 128

ANY = pl.BlockSpec(memory_space=pl.ANY)
VMEM = pl.BlockSpec(memory_space=pltpu.VMEM)
NN = (((1,), (0,)), ((), ()))
NT = (((1,), (1,)), ((), ()))
TN = (((0,), (0,)), ((), ()))


def _params(sem):
    return pltpu.CompilerParams(dimension_semantics=sem)


def _mm(name, a, b, *, grid, a_spec, b_spec, o_shape, o_spec, dims, out_dtype, nk=1, res=None, res_spec=None):
    has_res = res is not None

    def body(*refs):
        a_ref, b_ref = refs[0], refs[1]
        r_ref = refs[2] if has_res else None
        o_ref = refs[3] if has_res else refs[2]
        p = lax.dot_general(a_ref[...], b_ref[...], dims, preferred_element_type=F32)
        if nk == 1:
            if has_res:
                p = p + r_ref[...].astype(F32)
            o_ref[...] = p.astype(out_dtype)
            return
        acc_ref = refs[-1]
        k = pl.program_id(len(grid) - 1)

        @pl.when(k == 0)
        def _():
            acc_ref[...] = p

        @pl.when(k > 0)
        def _():
            acc_ref[...] += p

        @pl.when(k == nk - 1)
        def _():
            t = acc_ref[...]
            if has_res:
                t = t + r_ref[...].astype(F32)
            o_ref[...] = t.astype(out_dtype)

    o_block = tuple(d for d in o_spec.block_shape if d is not None)
    sem = ("parallel",) * (len(grid) - 1) + (("arbitrary",) if nk > 1 else ("parallel",))
    return pl.pallas_call(
        body, name=name, grid=grid,
        in_specs=[a_spec, b_spec] + ([res_spec] if has_res else []),
        out_specs=o_spec, out_shape=jax.ShapeDtypeStruct(o_shape, out_dtype),
        scratch_shapes=[pltpu.VMEM(o_block, F32)] if nk > 1 else [],
        compiler_params=_params(sem),
    )(*([a, b] + ([res] if has_res else [])))


def _rms_fwd(name, x, g):
    t, d = x.shape
    tm = min(ROW_TILE, t)

    def body(x_ref, g_ref, o_ref):
        xf = x_ref[...]
        r = lax.rsqrt(jnp.mean(xf * xf, axis=-1, keepdims=True) + RMS_EPS)
        o_ref[...] = (xf * r * g_ref[...]).astype(BF16)

    return pl.pallas_call(
        body, name=name, grid=(t // tm,),
        in_specs=[pl.BlockSpec((tm, d), lambda i: (i, 0)), pl.BlockSpec((1, d), lambda i: (0, 0))],
        out_specs=pl.BlockSpec((tm, d), lambda i: (i, 0)), out_shape=jax.ShapeDtypeStruct((t, d), BF16),
        compiler_params=_params(("parallel",)),
    )(x, g)


def _rms_bwd(name, dn, x, g, dres, want):
    t, d = x.shape
    tm = min(ROW_TILE, t)
    has_res = dres is not None

    def body(*refs):
        dn_ref, x_ref, g_ref = refs[0], refs[1], refs[2]
        r_ref = refs[3] if has_res else None
        dx_refs, dg_ref = refs[-1 - len(want):-1], refs[-1]
        xf = x_ref[...]
        r = lax.rsqrt(jnp.mean(xf * xf, axis=-1, keepdims=True) + RMS_EPS)
        xh = xf * r
        dnf = dn_ref[...].astype(F32)
        if want:
            dxh = dnf * g_ref[...]
            dx = r * (dxh - xh * jnp.mean(dxh * xh, axis=-1, keepdims=True))
            if has_res:
                dx = dx + r_ref[...]
            for kind, dx_ref in zip(want, dx_refs):
                dx_ref[...] = dx.astype(F32 if kind == "f32" else BF16)

        @pl.when(pl.program_id(0) == 0)
        def _():
            dg_ref[...] = jnp.zeros_like(dg_ref)

        dg_ref[...] += jnp.sum(dnf * xh, axis=0, keepdims=True)

    row = pl.BlockSpec((tm, d), lambda i: (i, 0))
    vec = pl.BlockSpec((1, d), lambda i: (0, 0))
    return pl.pallas_call(
        body, name=name, grid=(t // tm,),
        in_specs=[row, row, vec] + ([row] if has_res else []),
        out_specs=[row] * len(want) + [vec],
        out_shape=[jax.ShapeDtypeStruct((t, d), F32 if kind == "f32" else BF16) for kind in want]
        + [jax.ShapeDtypeStruct((1, d), F32)],
        compiler_params=_params(("arbitrary",)),
    )(*([dn, x, g] + ([dres] if has_res else [])))


def _loss_head(h, tgt, g):
    t, d = h.shape
    tm = min(ROW_TILE, t)

    def body(h_ref, t_ref, g_ref, loss_ref, dh_ref, dhb_ref, dg_ref):
        xf = h_ref[...]
        gv = g_ref[...]
        r = lax.rsqrt(jnp.mean(xf * xf, axis=-1, keepdims=True) + RMS_EPS)
        xh = xf * r
        e = xh * gv - t_ref[...]
        dy = e * (1.0 / d)
        dxh = dy * gv
        dh = r * (dxh - xh * jnp.mean(dxh * xh, axis=-1, keepdims=True))
        dh_ref[...] = dh
        dhb_ref[...] = dh.astype(BF16)

        @pl.when(pl.program_id(0) == 0)
        def _():
            dg_ref[...] = jnp.zeros_like(dg_ref)
            loss_ref[...] = jnp.zeros_like(loss_ref)

        dg_ref[...] += jnp.sum(dy * xh, axis=0, keepdims=True)
        part = jnp.sum(jnp.sum(e * e, axis=1, keepdims=True), axis=0, keepdims=True) * (0.5 / d)
        loss_ref[...] += jnp.broadcast_to(part, loss_ref.shape)

    row = pl.BlockSpec((tm, d), lambda i: (i, 0))
    vec = pl.BlockSpec((1, d), lambda i: (0, 0))
    return pl.pallas_call(
        body, name="loss_head", grid=(t // tm,),
        in_specs=[row, row, vec],
        out_specs=[pl.BlockSpec((8, LANES), lambda i: (0, 0)), row, row, vec],
        out_shape=[jax.ShapeDtypeStruct((8, LANES), F32), jax.ShapeDtypeStruct((t, d), F32),
                   jax.ShapeDtypeStruct((t, d), BF16), jax.ShapeDtypeStruct((1, d), F32)],
        compiler_params=_params(("arbitrary",)),
    )(h, tgt, g)


def _rope_tables(pos, inv_freq, sel_lo, sel_hi):
    t = pos.shape[0]
    tm = min(ROW_TILE, t)

    def body(p_ref, f_ref, lo_ref, hi_ref, c_ref, sa_ref, sb_ref):
        ang = p_ref[...].astype(F32) * f_ref[...]
        rot = lo_ref[...] + hi_ref[...]
        cs, sn = jnp.cos(ang), jnp.sin(ang)
        c_ref[...] = cs * rot + (1.0 - rot)
        sa_ref[...] = -sn * lo_ref[...]
        sb_ref[...] = sn * hi_ref[...]

    vec = pl.BlockSpec((1, LANES), lambda i: (0, 0))
    row = pl.BlockSpec((tm, LANES), lambda i: (i, 0))
    return pl.pallas_call(
        body, name="rope_tables", grid=(t // tm,),
        in_specs=[pl.BlockSpec((tm, 1), lambda i: (i, 0)), vec, vec, vec],
        out_specs=[row, row, row], out_shape=[jax.ShapeDtypeStruct((t, LANES), F32)] * 3,
        compiler_params=_params(("parallel",)),
    )(pos, inv_freq, sel_lo, sel_hi)


def _rope_apply(name, src, col0, n_cols, cos_t, sin_a, sin_b, sign):
    t = src.shape[0]
    tm = min(ROW_TILE, t)

    def body(x_ref, c_ref, sa_ref, sb_ref, o_ref):
        cs, sa, sb = c_ref[...], sign * sa_ref[...], sign * sb_ref[...]
        for c in range(n_cols):
            cols = slice(c * LANES, (c + 1) * LANES)
            xf = x_ref[:, cols].astype(F32)
            up = pltpu.roll(xf, LANES - ROPE_HALF, 1)
            dn = pltpu.roll(xf, ROPE_HALF, 1)
            o_ref[:, cols] = (xf * cs + up * sa + dn * sb).astype(BF16)

    wide = n_cols * LANES
    tab = pl.BlockSpec((tm, LANES), lambda i: (i, 0))
    return pl.pallas_call(
        body, name=name, grid=(t // tm,),
        in_specs=[pl.BlockSpec((tm, wide), lambda i: (i, col0 // n_cols)), tab, tab, tab],
        out_specs=pl.BlockSpec((tm, wide), lambda i: (i, 0)),
        out_shape=jax.ShapeDtypeStruct((t, wide), BF16),
        compiler_params=_params(("parallel",)),
    )(src, cos_t, sin_a, sin_b)


DA_T = 256
FWD_STREAMS = 4
BWD_STREAMS = 2


def _lane_lo():
    return lax.broadcasted_iota(jnp.int32, (BLOCK, LANES), 1) < HEAD_DIM


def _dilated_bias_tiles(s):
    n = s // DA_T
    dist = (np.arange(n)[:, None, None] * DA_T + np.arange(DA_T)[None, :, None] - np.arange(DA_T)[None, None, :])
    cnt = np.zeros(dist.shape, np.float32)
    for window, dil in DIL_PATTERNS:
        cnt += ((dist >= 0) & (dist % dil == 0) & (dist <= window)).astype(np.float32)
    return jnp.asarray(np.where(cnt > 0, np.log(np.maximum(cnt, 1.0)), NEG).astype(np.float32))


def _stack_heads(x, lo):
    zero = jnp.zeros_like(x)
    return jnp.concatenate([jnp.where(lo, x, zero), jnp.where(lo, zero, x)], axis=0)


def _da_fwd(qk, proj, v_col0, bias, batch, s, ride=None, streams=FWD_STREAMS):
    t = qk.shape[0]
    nq = s // DA_T
    n_pairs = 4
    ns = streams
    wide = ns * LANES
    scale = HEAD_DIM ** -0.5

    def body(q_ref, k_ref, v_ref, b_ref, o_ref, lse_ref, acc_ref, m_ref, l_ref):
        i = pl.program_id(2)
        lo = lax.broadcasted_iota(jnp.int32, (DA_T, LANES), 1) < HEAD_DIM
        ones = jnp.ones((DA_T, LANES), BF16)
        acc_ref[...] = jnp.zeros_like(acc_ref)
        m_ref[...] = jnp.full(m_ref.shape, NEG, F32)
        l_ref[...] = jnp.zeros_like(l_ref)
        qqs = [_stack_heads(q_ref[:, st * LANES:(st + 1) * LANES] * scale, lo) for st in range(ns)]

        def scores(st, rows, bias2):
            k = k_ref[rows, st * LANES:(st + 1) * LANES]
            return lax.dot_general(qqs[st], k, NT, preferred_element_type=F32) + bias2

        def softmax(st, sc):
            m_old = m_ref[st]
            m_new = jnp.maximum(m_old, jnp.max(sc, axis=1, keepdims=True))
            m_ref[st] = m_new
            return jnp.exp(sc - m_new).astype(BF16), jnp.exp(m_old - m_new)

        def values(st, rows, p, alpha):
            v = v_ref[rows, st * LANES:(st + 1) * LANES]
            vz = jnp.zeros_like(v)
            l_ref[st] = alpha * l_ref[st] + lax.dot_general(p, ones, NN, preferred_element_type=F32)
            pv = (lax.dot_general(p[:DA_T], jnp.where(lo, v, vz), NN, preferred_element_type=F32)
                  + lax.dot_general(p[DA_T:], jnp.where(lo, vz, v), NN, preferred_element_type=F32))
            acc_ref[st] = acc_ref[st] * jnp.where(lo, alpha[:DA_T], alpha[DA_T:]) + pv

        def trip(dlt, carry):
            rows = pl.ds(pl.multiple_of((i - dlt) * DA_T, DA_T), DA_T)
            bias_t = b_ref[dlt]
            bias2 = jnp.concatenate([bias_t, bias_t], axis=0)
            scs = [scores(st, rows, bias2) for st in range(ns)]
            pas = [softmax(st, scs[st]) for st in range(ns)]
            for st in range(ns):
                values(st, rows, *pas[st])
            return carry

        lax.fori_loop(0, i + 1, trip, 0)
        for st in range(ns):
            cols = slice(st * LANES, (st + 1) * LANES)
            l_t = l_ref[st]
            o_ref[:, cols] = (acc_ref[st] / jnp.where(lo, l_t[:DA_T], l_t[DA_T:])).astype(BF16)
            lse = m_ref[st] + jnp.log(l_t)
            lse_ref[:, cols] = jnp.where(lo, lse[:DA_T], lse[DA_T:])

    blk = pl.BlockSpec((DA_T, wide), lambda b, h, i: (b * nq + i, h))
    return _call(
        body, name="attn_a_fwd", grid=(batch, n_pairs // ns, nq),
        in_specs=[blk,
                  pl.BlockSpec((s, wide), lambda b, h, i: (b, n_pairs // ns + h)),
                  pl.BlockSpec((s, wide), lambda b, h, i: (b, v_col0 // ns + h)),
                  pl.BlockSpec((nq, DA_T, DA_T), lambda b, h, i: (0, 0, 0))],
        out_specs=[blk, blk],
        out_shape=[jax.ShapeDtypeStruct((t, n_pairs * LANES), BF16), jax.ShapeDtypeStruct((t, n_pairs * LANES), F32)],
        scratch=[pltpu.VMEM((ns, DA_T, LANES), F32), pltpu.VMEM((ns, 2 * DA_T, 1), F32),
                 pltpu.VMEM((ns, 2 * DA_T, LANES), F32)],
        sem=("parallel", "parallel", "arbitrary"), args=(qk, qk, proj, bias), ride=ride)


def _da_bwd(qk, proj, v_col0, bias, o, lse, do, batch, s, ride=None, streams=BWD_STREAMS):
    t = qk.shape[0]
    nq = s // DA_T
    n_pairs = 4
    ns = streams
    wide = ns * LANES
    scale = HEAD_DIM ** -0.5

    def body(q_ref, k_ref, v_ref, b_ref, o_ref, lse_ref, do_ref, dq_ref, dk_ref, dv_ref, dk_acc, dv_acc, dq_acc):
        i = pl.program_id(2)
        lo = lax.broadcasted_iota(jnp.int32, (DA_T, LANES), 1) < HEAD_DIM

        @pl.when(i == 0)
        def _():
            dk_acc[...] = jnp.zeros_like(dk_acc)
            dv_acc[...] = jnp.zeros_like(dv_acc)

        dq_acc[...] = jnp.zeros_like(dq_acc)
        qqs, dds, deltas, lses = [], [], [], []
        for st in range(ns):
            cols = slice(st * LANES, (st + 1) * LANES)
            do_ = do_ref[:, cols]
            qqs.append(_stack_heads(q_ref[:, cols] * scale, lo))
            dds.append(_stack_heads(do_, lo))
            prod = do_.astype(F32) * o_ref[:, cols].astype(F32)
            fz = jnp.zeros_like(prod)
            deltas.append(jnp.concatenate([jnp.sum(jnp.where(lo, prod, fz), axis=1, keepdims=True),
                                           jnp.sum(jnp.where(lo, fz, prod), axis=1, keepdims=True)], axis=0))
            lse_t = lse_ref[:, cols]
            lses.append(jnp.concatenate([lse_t[:, 0:1], lse_t[:, HEAD_DIM:HEAD_DIM + 1]], axis=0))

        def products(st, rows, bias2):
            cols = slice(st * LANES, (st + 1) * LANES)
            sc = lax.dot_general(qqs[st], k_ref[rows, cols], NT, preferred_element_type=F32) + bias2
            return sc, lax.dot_general(dds[st], v_ref[rows, cols], NT, preferred_element_type=F32)

        def weights(st, sc, dp):
            p = jnp.exp(sc - lses[st])
            return (p * (dp - deltas[st])).astype(BF16), p.astype(BF16)

        def gradients(st, rows, ds, p):
            cols = slice(st * LANES, (st + 1) * LANES)
            k = k_ref[rows, cols]
            kz = jnp.zeros_like(k)
            dq_acc[st] += (lax.dot_general(ds[:DA_T], jnp.where(lo, k, kz), NN, preferred_element_type=F32)
                           + lax.dot_general(ds[DA_T:], jnp.where(lo, kz, k), NN, preferred_element_type=F32))
            dk_acc[rows, cols] += lax.dot_general(ds, qqs[st], TN, preferred_element_type=F32)
            dv_acc[rows, cols] += lax.dot_general(p, dds[st], TN, preferred_element_type=F32)

        def trip(dlt, carry):
            rows = pl.ds(pl.multiple_of((i - dlt) * DA_T, DA_T), DA_T)
            bias_t = b_ref[dlt]
            bias2 = jnp.concatenate([bias_t, bias_t], axis=0)
            prods = [products(st, rows, bias2) for st in range(ns)]
            wts = [weights(st, *prods[st]) for st in range(ns)]
            for st in range(ns):
                gradients(st, rows, *wts[st])
            return carry

        lax.fori_loop(0, i + 1, trip, 0)
        for st in range(ns):
            dq_ref[:, st * LANES:(st + 1) * LANES] = (dq_acc[st] * scale).astype(BF16)

        @pl.when(i == nq - 1)
        def _():
            dk_ref[...] = dk_acc[...].astype(BF16)
            dv_ref[...] = dv_acc[...].astype(BF16)

    blk = pl.BlockSpec((DA_T, wide), lambda b, h, i: (b * nq + i, h))
    seq = pl.BlockSpec((s, wide), lambda b, h, i: (b, h))
    out = jax.ShapeDtypeStruct((t, n_pairs * LANES), BF16)
    return _call(
        body, name="attn_a_bwd", grid=(batch, n_pairs // ns, nq),
        in_specs=[blk,
                  pl.BlockSpec((s, wide), lambda b, h, i: (b, n_pairs // ns + h)),
                  pl.BlockSpec((s, wide), lambda b, h, i: (b, v_col0 // ns + h)),
                  pl.BlockSpec((nq, DA_T, DA_T), lambda b, h, i: (0, 0, 0)),
                  blk, blk, blk],
        out_specs=[blk, seq, seq], out_shape=[out, out, out],
        scratch=[pltpu.VMEM((s, wide), F32), pltpu.VMEM((s, wide), F32), pltpu.VMEM((ns, DA_T, LANES), F32)],
        sem=("parallel", "parallel", "arbitrary"), args=(qk, qk, proj, bias, o, lse, do), ride=ride)


SB_Q = 256


def _sb_consts(after):
    r = lax.broadcasted_iota(jnp.int32, (2 * BLOCK, 2 * BLOCK), 0) % BLOCK
    c = lax.broadcasted_iota(jnp.int32, (2 * BLOCK, 2 * BLOCK), 1)
    tri = (r > c) if after else (r < c)
    return jnp.logical_or(c >= BLOCK, tri).astype(BF16)


def _split(x):
    hi = x.astype(BF16)
    lo = (x - hi.astype(F32)).astype(BF16)
    return jnp.concatenate([hi, lo], axis=1)


def _sb_fwd(proj, q_col0, k_col0, v_col0, batch, s, ride=None, streams=FWD_STREAMS):
    t = proj.shape[0]
    nq = s // SB_Q
    n_pairs = 4
    ns = streams
    wide = ns * LANES
    scale = HEAD_DIM ** -0.5

    def body(q_ref, k_ref, v_ref, o_ref, tot_ref, acc_ref, run_ref):
        i = pl.program_id(2)
        lo_q = lax.broadcasted_iota(jnp.int32, (SB_Q, LANES), 1) < HEAD_DIM
        lo_k = _lane_lo()
        mat = _sb_consts(True)
        row = lax.broadcasted_iota(jnp.int32, (2 * SB_Q, LANES), 0) % SB_Q
        ahead = row - lax.broadcasted_iota(jnp.int32, (2 * SB_Q, LANES), 1)
        acc_ref[...] = jnp.zeros_like(acc_ref)
        run_ref[...] = jnp.zeros_like(run_ref)
        qqs = [_stack_heads(q_ref[:, st * LANES:(st + 1) * LANES] * scale, lo_q) for st in range(ns)]

        def units(todo):
            def rows(j):
                return pl.ds(pl.multiple_of(j * BLOCK, BLOCK), BLOCK)

            zs = [lax.dot_general(qqs[st], k_ref[rows(j), st * LANES:(st + 1) * LANES], NT, preferred_element_type=F32)
                  for st, j, _ in todo]
            logs = []
            for z, (_, _, off) in zip(zs, todo):
                lsig = jnp.minimum(z, 0.0) - jnp.log(1.0 + jnp.exp(-jnp.abs(z)))
                lneg = lsig - z
                if off is not None:
                    lneg = jnp.where(ahead > off, lneg, 0.0)
                logs.append((lsig, _split(lneg)))
            sums = [lax.dot_general(cat, mat, NN, preferred_element_type=F32) for _, cat in logs]
            probs = []
            for (lsig, _), sm, (st, _, off) in zip(logs, sums, todo):
                run = run_ref[st]
                a = jnp.exp(lsig + run + sm[:, :BLOCK])
                if off is not None:
                    a = jnp.where(ahead > off, a, 0.0)
                run_ref[st] = run + sm[:, BLOCK:]
                probs.append(a.astype(BF16))
            for ab, (st, j, _) in zip(probs, todo):
                v = v_ref[rows(j), st * LANES:(st + 1) * LANES]
                vz = jnp.zeros_like(v)
                acc_ref[st] += (lax.dot_general(ab[:SB_Q], jnp.where(lo_k, v, vz), NN, preferred_element_type=F32)
                                + lax.dot_general(ab[SB_Q:], jnp.where(lo_k, vz, v), NN, preferred_element_type=F32))

        units([(st, 2 * i + 1, BLOCK) for st in range(ns)] + [(st, 2 * i, 0) for st in range(ns)])

        def pair(p, carry):
            jp = i - 1 - p
            units([(st, 2 * jp + 1, None) for st in range(ns)] + [(st, 2 * jp, None) for st in range(ns)])
            return carry

        lax.fori_loop(0, i, pair, 0)
        for st in range(ns):
            cols = slice(st * LANES, (st + 1) * LANES)
            o_ref[:, cols] = acc_ref[st].astype(BF16)
            tot_ref[:, cols] = jnp.where(lo_q, run_ref[st, 0:SB_Q, :], run_ref[st, SB_Q:2 * SB_Q, :])

    def seq(col0):
        return pl.BlockSpec((s, wide), lambda b, h, i: (b, col0 // ns + h))

    blk = pl.BlockSpec((SB_Q, wide), lambda b, h, i: (b * nq + i, h))
    return _call(
        body, name="attn_b_fwd", grid=(batch, n_pairs // ns, nq),
        in_specs=[pl.BlockSpec((SB_Q, wide), lambda b, h, i: (b * nq + i, q_col0 // ns + h)), seq(k_col0), seq(v_col0)],
        out_specs=[blk, blk],
        out_shape=[jax.ShapeDtypeStruct((t, n_pairs * LANES), BF16), jax.ShapeDtypeStruct((t, n_pairs * LANES), F32)],
        scratch=[pltpu.VMEM((ns, SB_Q, LANES), F32), pltpu.VMEM((ns, 2 * SB_Q, LANES), F32)],
        sem=("parallel", "parallel", "arbitrary"), args=(proj, proj, proj), ride=ride)


def _sb_bwd(proj, q_col0, k_col0, v_col0, tot, do, batch, s, ride=None, streams=BWD_STREAMS):
    t = proj.shape[0]
    nq = s // SB_Q
    n_pairs = 4
    ns = streams
    wide = ns * LANES
    scale = HEAD_DIM ** -0.5

    def body(q_ref, k_ref, v_ref, tot_ref, do_ref, dq_ref, dk_ref, dv_ref, dk_acc, dv_acc, dq_acc, seen_ref, gsum_ref):
        i = pl.program_id(2)
        lo_q = lax.broadcasted_iota(jnp.int32, (SB_Q, LANES), 1) < HEAD_DIM
        lo_k = _lane_lo()

        @pl.when(i == 0)
        def _():
            dk_acc[...] = jnp.zeros_like(dk_acc)
            dv_acc[...] = jnp.zeros_like(dv_acc)

        mat_after = _sb_consts(True)
        mat_before = _sb_consts(False)
        row = lax.broadcasted_iota(jnp.int32, (2 * SB_Q, LANES), 0) % SB_Q
        ahead = row - lax.broadcasted_iota(jnp.int32, (2 * SB_Q, LANES), 1)
        dq_acc[...] = jnp.zeros_like(dq_acc)
        seen_ref[...] = jnp.zeros_like(seen_ref)
        gsum_ref[...] = jnp.zeros_like(gsum_ref)
        qqs, dds, totals = [], [], []
        for st in range(ns):
            cols = slice(st * LANES, (st + 1) * LANES)
            qqs.append(_stack_heads(q_ref[:, cols] * scale, lo_q))
            dds.append(_stack_heads(do_ref[:, cols], lo_q))
            tot_t = tot_ref[:, cols]
            totals.append(jnp.concatenate([jnp.broadcast_to(tot_t[:, 0:1], (SB_Q, LANES)),
                                           jnp.broadcast_to(tot_t[:, HEAD_DIM:HEAD_DIM + 1], (SB_Q, LANES))], axis=0))

        def units(todo):
            def rows(j):
                return pl.ds(pl.multiple_of(j * BLOCK, BLOCK), BLOCK)

            def cols(st):
                return slice(st * LANES, (st + 1) * LANES)

            prods = [(lax.dot_general(qqs[st], k_ref[rows(j), cols(st)], NT, preferred_element_type=F32),
                      lax.dot_general(dds[st], v_ref[rows(j), cols(st)], NT, preferred_element_type=F32))
                     for st, j, _ in todo]
            logs = []
            for (z, _), (_, _, off) in zip(prods, todo):
                lsig = jnp.minimum(z, 0.0) - jnp.log(1.0 + jnp.exp(-jnp.abs(z)))
                lneg = lsig - z
                if off is not None:
                    lneg = jnp.where(ahead > off, lneg, 0.0)
                logs.append((lsig, _split(lneg)))
            sums = [lax.dot_general(cat, mat_after, NN, preferred_element_type=F32) for _, cat in logs]
            gates = []
            for (lsig, _), sm, (_, da), (st, _, off) in zip(logs, sums, prods, todo):
                seen = seen_ref[st]
                a = jnp.exp(lsig + (totals[st] - seen - sm[:, BLOCK:]) + sm[:, :BLOCK])
                if off is not None:
                    a = jnp.where(ahead > off, a, 0.0)
                seen_ref[st] = seen + sm[:, BLOCK:]
                g = a * da
                gates.append((a.astype(BF16), g, _split(g)))
            gsums = [lax.dot_general(cat, mat_before, NN, preferred_element_type=F32) for _, _, cat in gates]
            outs = []
            for (lsig, _), (ab, g, _), gs, (st, _, off) in zip(logs, gates, gsums, todo):
                gsum = gsum_ref[st]
                dz = g - jnp.exp(lsig) * (g + gsum + gs[:, :BLOCK])
                if off is not None:
                    dz = jnp.where(ahead > off, dz, 0.0)
                gsum_ref[st] = gsum + gs[:, BLOCK:]
                outs.append((dz.astype(BF16), ab))
            for (dzb, ab), (st, j, _) in zip(outs, todo):
                k = k_ref[rows(j), cols(st)]
                kz = jnp.zeros_like(k)
                dq_acc[st] += (lax.dot_general(dzb[:SB_Q], jnp.where(lo_k, k, kz), NN, preferred_element_type=F32)
                               + lax.dot_general(dzb[SB_Q:], jnp.where(lo_k, kz, k), NN, preferred_element_type=F32))
                dk_acc[rows(j), cols(st)] += lax.dot_general(dzb, qqs[st], TN, preferred_element_type=F32)
                dv_acc[rows(j), cols(st)] += lax.dot_general(ab, dds[st], TN, preferred_element_type=F32)

        def pair(p, carry):
            units([(st, 2 * p, None) for st in range(ns)] + [(st, 2 * p + 1, None) for st in range(ns)])
            return carry

        lax.fori_loop(0, i, pair, 0)
        units([(st, 2 * i, 0) for st in range(ns)] + [(st, 2 * i + 1, BLOCK) for st in range(ns)])
        for st in range(ns):
            dq_ref[:, st * LANES:(st + 1) * LANES] = (dq_acc[st] * scale).astype(BF16)

        @pl.when(i == nq - 1)
        def _():
            dk_ref[...] = dk_acc[...].astype(BF16)
            dv_ref[...] = dv_acc[...].astype(BF16)

    def seq_in(col0):
        return pl.BlockSpec((s, wide), lambda b, h, i: (b, col0 // ns + h))

    blk = pl.BlockSpec((SB_Q, wide), lambda b, h, i: (b * nq + i, h))
    seq = pl.BlockSpec((s, wide), lambda b, h, i: (b, h))
    out = jax.ShapeDtypeStruct((t, n_pairs * LANES), BF16)
    return _call(
        body, name="attn_b_bwd", grid=(batch, n_pairs // ns, nq),
        in_specs=[pl.BlockSpec((SB_Q, wide), lambda b, h, i: (b * nq + i, q_col0 // ns + h)), seq_in(k_col0),
                  seq_in(v_col0), blk, blk],
        out_specs=[blk, seq, seq], out_shape=[out, out, out],
        scratch=[pltpu.VMEM((s, wide), F32), pltpu.VMEM((s, wide), F32), pltpu.VMEM((ns, SB_Q, LANES), F32),
                 pltpu.VMEM((ns, 2 * SB_Q, LANES), F32), pltpu.VMEM((ns, 2 * SB_Q, LANES), F32)],
        sem=("parallel", "parallel", "arbitrary"), args=(proj, proj, proj, tot, do), ride=ride)


MEM_Q_TILE = 512


def _mem_fwd(q, kv, batch, s, n_mem):
    t, width = q.shape
    tq = min(MEM_Q_TILE, s)
    nq = s // tq
    scale = MEM_HEAD_DIM ** -0.5

    def body(q_ref, kv_ref, o_ref):
        for h in range(N_HEADS_MEM):
            cols = slice(h * MEM_HEAD_DIM, (h + 1) * MEM_HEAD_DIM)
            k = kv_ref[:, cols]
            v = kv_ref[:, width + h * MEM_HEAD_DIM: width + (h + 1) * MEM_HEAD_DIM]
            sc = lax.dot_general(q_ref[:, cols], k, NT, preferred_element_type=F32) * scale
            p = jnp.exp(sc - jnp.max(sc, axis=1, keepdims=True))
            p = p / jnp.sum(p, axis=1, keepdims=True)
            o_ref[:, cols] = lax.dot_general(p.astype(BF16), v, NN, preferred_element_type=F32).astype(BF16)

    return pl.pallas_call(
        body, name="mem_attn_fwd", grid=(batch, nq),
        in_specs=[pl.BlockSpec((tq, width), lambda b, i: (b * nq + i, 0)),
                  pl.BlockSpec((n_mem, 2 * width), lambda b, i: (b, 0))],
        out_specs=pl.BlockSpec((tq, width), lambda b, i: (b * nq + i, 0)),
        out_shape=jax.ShapeDtypeStruct((t, width), BF16),
        compiler_params=_params(("parallel", "parallel")),
    )(q, kv)


def _mem_bwd(q, kv, do, batch, s, n_mem):
    t, width = q.shape
    tq = min(MEM_Q_TILE, s)
    nq = s // tq
    scale = MEM_HEAD_DIM ** -0.5

    def body(q_ref, kv_ref, do_ref, dq_ref, dkv_ref, acc):
        i = pl.program_id(1)

        @pl.when(i == 0)
        def _():
            acc[...] = jnp.zeros_like(acc)

        for h in range(N_HEADS_MEM):
            cols = slice(h * MEM_HEAD_DIM, (h + 1) * MEM_HEAD_DIM)
            vcols = slice(width + h * MEM_HEAD_DIM, width + (h + 1) * MEM_HEAD_DIM)
            qh, k, v, doh = q_ref[:, cols], kv_ref[:, cols], kv_ref[:, vcols], do_ref[:, cols]
            sc = lax.dot_general(qh, k, NT, preferred_element_type=F32) * scale
            p = jnp.exp(sc - jnp.max(sc, axis=1, keepdims=True))
            p = p / jnp.sum(p, axis=1, keepdims=True)
            dp = lax.dot_general(doh, v, NT, preferred_element_type=F32)
            ds = (p * (dp - jnp.sum(p * dp, axis=1, keepdims=True)) * scale).astype(BF16)
            dq_ref[:, cols] = lax.dot_general(ds, k, NN, preferred_element_type=F32).astype(BF16)
            acc[:, cols] += lax.dot_general(ds, qh, TN, preferred_element_type=F32)
            acc[:, vcols] += lax.dot_general(p.astype(BF16), doh, TN, preferred_element_type=F32)

        @pl.when(i == nq - 1)
        def _():
            dkv_ref[...] = acc[...].astype(BF16)

    row = pl.BlockSpec((tq, width), lambda b, i: (b * nq + i, 0))
    kvs = pl.BlockSpec((n_mem, 2 * width), lambda b, i: (b, 0))
    return pl.pallas_call(
        body, name="mem_attn_bwd", grid=(batch, nq),
        in_specs=[row, kvs, row], out_specs=[row, kvs],
        out_shape=[jax.ShapeDtypeStruct((t, width), BF16), jax.ShapeDtypeStruct((batch * n_mem, 2 * width), BF16)],
        scratch_shapes=[pltpu.VMEM((n_mem, 2 * width), F32)],
        compiler_params=_params(("parallel", "arbitrary")),
    )(q, kv, do)


def _mixer_fwd(o_a, o_b, w_a, w_b, proj, gate_col0):
    t, width = o_a.shape
    d = w_a.shape[1]
    tm = min(ROW_TILE, t)
    gb0 = gate_col0 * LANES // d

    def body(oa_ref, ob_ref, wa_ref, wb_ref, ga_ref, gb_ref, ua_ref, ub_ref, mix_ref):
        ua = lax.dot_general(oa_ref[...], wa_ref[...], NN, preferred_element_type=F32)
        ub = lax.dot_general(ob_ref[...], wb_ref[...], NN, preferred_element_type=F32)
        ua_ref[...] = ua.astype(BF16)
        ub_ref[...] = ub.astype(BF16)
        mix_ref[...] = (jax.nn.sigmoid(ga_ref[...].astype(F32)) * ua
                        + jax.nn.sigmoid(gb_ref[...].astype(F32)) * ub).astype(BF16)

    row = pl.BlockSpec((tm, width), lambda i: (i, 0))
    wsp = pl.BlockSpec((width, d), lambda i: (0, 0))
    out = pl.BlockSpec((tm, d), lambda i: (i, 0))
    osh = jax.ShapeDtypeStruct((t, d), BF16)
    return pl.pallas_call(
        body, name="mixer_fwd", grid=(t // tm,),
        in_specs=[row, row, wsp, wsp,
                  pl.BlockSpec((tm, d), lambda i: (i, gb0)), pl.BlockSpec((tm, d), lambda i: (i, gb0 + 1))],
        out_specs=[out, out, out], out_shape=[osh, osh, osh],
        compiler_params=_params(("parallel",)),
    )(o_a, o_b, w_a, w_b, proj, proj)


def _mixer_bwd(dmix, ua, ub, proj, gate_col0):
    t, d = dmix.shape
    tm = min(ROW_TILE, t)
    nc = d // LANES

    def body(dm_ref, ua_ref, ub_ref, ga_ref, gb_ref, dua_ref, dub_ref, dg_ref):
        dm = dm_ref[...].astype(F32)
        sa = jax.nn.sigmoid(ga_ref[...].astype(F32))
        sb = jax.nn.sigmoid(gb_ref[...].astype(F32))
        dua_ref[...] = (dm * sa).astype(BF16)
        dub_ref[...] = (dm * sb).astype(BF16)
        dg_ref[:, 0:d] = (dm * ua_ref[...].astype(F32) * sa * (1.0 - sa)).astype(BF16)
        dg_ref[:, d:2 * d] = (dm * ub_ref[...].astype(F32) * sb * (1.0 - sb)).astype(BF16)

    row = pl.BlockSpec((tm, d), lambda i: (i, 0))
    return pl.pallas_call(
        body, name="mixer_bwd", grid=(t // tm,),
        in_specs=[row, row, row,
                  pl.BlockSpec((tm, d), lambda i: (i, gate_col0 // nc)),
                  pl.BlockSpec((tm, d), lambda i: (i, gate_col0 // nc + 1))],
        out_specs=[row, row, pl.BlockSpec((tm, 2 * d), lambda i: (i, 0))],
        out_shape=[jax.ShapeDtypeStruct((t, d), BF16), jax.ShapeDtypeStruct((t, d), BF16),
                   jax.ShapeDtypeStruct((t, 2 * d), BF16)],
        compiler_params=_params(("parallel",)),
    )(dmix, ua, ub, proj, proj)


FFN_COLS = 1024


def _ffn_up(n, w_gate, w_up):
    t, d = n.shape
    hidden = w_gate.shape[1]
    tm = min(ROW_TILE, t)
    tn = min(FFN_COLS, hidden)

    def body(n_ref, wg_ref, wu_ref, hg_ref, hu_ref, act_ref):
        hg = lax.dot_general(n_ref[...], wg_ref[...], NN, preferred_element_type=F32)
        hu = lax.dot_general(n_ref[...], wu_ref[...], NN, preferred_element_type=F32)
        hg_ref[...] = hg.astype(BF16)
        hu_ref[...] = hu.astype(BF16)
        act_ref[...] = (hg * jax.nn.sigmoid(hg) * hu).astype(BF16)

    wsp = pl.BlockSpec((d, tn), lambda j, i: (0, j))
    out = pl.BlockSpec((tm, tn), lambda j, i: (i, j))
    osh = jax.ShapeDtypeStruct((t, hidden), BF16)
    return pl.pallas_call(
        body, name="ffn_up", grid=(hidden // tn, t // tm),
        in_specs=[pl.BlockSpec((tm, d), lambda j, i: (i, 0)), wsp, wsp],
        out_specs=[out, out, out], out_shape=[osh, osh, osh],
        compiler_params=_params(("parallel", "parallel")),
    )(n, w_gate, w_up)


def _ffn_bwd_act(dh, w_down, hg, hu):
    t, d = dh.shape
    hidden = w_down.shape[0]
    tm = min(ROW_TILE, t)
    tn = min(FFN_COLS, hidden)

    def body(dh_ref, wd_ref, hg_ref, hu_ref, dhg_ref, dhu_ref):
        dact = lax.dot_general(dh_ref[...], wd_ref[...], NT, preferred_element_type=F32)
        hg = hg_ref[...].astype(F32)
        sg = jax.nn.sigmoid(hg)
        dhu_ref[...] = (dact * hg * sg).astype(BF16)
        dhg_ref[...] = (dact * hu_ref[...].astype(F32) * sg * (1.0 + hg * (1.0 - sg))).astype(BF16)

    hid = pl.BlockSpec((tm, tn), lambda j, i: (i, j))
    osh = jax.ShapeDtypeStruct((t, hidden), BF16)
    return pl.pallas_call(
        body, name="ffn_bwd_act", grid=(hidden // tn, t // tm),
        in_specs=[pl.BlockSpec((tm, d), lambda j, i: (i, 0)), pl.BlockSpec((tn, d), lambda j, i: (j, 0)), hid, hid],
        out_specs=[hid, hid], out_shape=[osh, osh],
        compiler_params=_params(("parallel", "parallel")),
    )(dh, w_down, hg, hu)


MM_ROWS = 1024


def _mm_w(name, a, w, out_dtype, dims=NN, res=None, tm=MM_ROWS, tn=1024):
    t, k = a.shape
    n = w.shape[1] if dims == NN else w.shape[0]
    tm, tn = min(tm, t), min(tn, n)
    o_spec = pl.BlockSpec((tm, tn), lambda j, i: (i, j))
    b_spec = pl.BlockSpec((k, tn), lambda j, i: (0, j)) if dims == NN else pl.BlockSpec((tn, k), lambda j, i: (j, 0))
    return _mm(name, a, w, grid=(n // tn, t // tm), a_spec=pl.BlockSpec((tm, k), lambda j, i: (i, 0)), b_spec=b_spec,
               o_shape=(t, n), o_spec=o_spec, dims=dims, out_dtype=out_dtype, res=res,
               res_spec=o_spec if res is not None else None)


def _wgrad(name, a, g, tk=1024, tn=1024):
    t, k = a.shape
    n = g.shape[1]
    tm, tk, tn = min(2 * MM_ROWS, t), min(tk, k), min(tn, n)
    return _mm(name, a, g, grid=(k // tk, n // tn, t // tm),
               a_spec=pl.BlockSpec((tm, tk), lambda p, q, r: (r, p)), b_spec=pl.BlockSpec((tm, tn), lambda p, q, r: (r, q)),
               o_shape=(k, n), o_spec=pl.BlockSpec((tk, tn), lambda p, q, r: (p, q)), dims=TN, out_dtype=BF16, nk=t // tm)


def _peers():
    x, y, c = lax.axis_index("x"), lax.axis_index("y"), lax.axis_index("c")
    me = 4 * x + 2 * y + c
    out = []
    for k in range(1, N_DEV):
        kx, ky, kc = (k >> 2) & 1, (k >> 1) & 1, k & 1
        px = 1 - x if kx else x
        py = 1 - y if ky else y
        pc = 1 - c if kc else c
        out.append(((px, py, pc), 4 * px + 2 * py + pc))
    return me, out


def _cast_weights(ws, pads):
    def body(*refs):
        n = len(refs) // 2
        for i_ref, o_ref, (pr, pc) in zip(refs[:n], refs[n:], pads):
            r, c = i_ref.shape
            o_ref[0:r, 0:c] = i_ref[...].astype(BF16)
            if pr:
                o_ref[r:r + pr, :] = jnp.zeros((pr, c), BF16)
            if pc:
                o_ref[:, c:c + pc] = jnp.zeros((r, pc), BF16)

    return pl.pallas_call(
        body, name="cast_weights", in_specs=[VMEM] * len(ws), out_specs=[VMEM] * len(ws),
        out_shape=[jax.ShapeDtypeStruct((w.shape[0] + pr, w.shape[1] + pc), BF16) for w, (pr, pc) in zip(ws, pads)],
    )(*ws)


def _window(ref, j, c):
    return ref.at[:, pl.ds(pl.multiple_of(j * c, LANES), c)]


def _scatter_copies(ins, outs, sems, cols, landed):
    send_sems, recv_sems, loc_sems = sems
    n_peer = N_DEV - 1
    me, peers = _peers()

    def src(w, j):
        return _window(ins[w], j, cols[w]) if cols[w] else ins[w].at[j]

    local = [pltpu.make_async_copy(src(w, me), outs[w].at[me], loc_sems.at[w]) for w in range(len(ins))]
    remote = [pltpu.make_async_remote_copy(
        src_ref=src(w, idx), dst_ref=outs[w].at[idx if landed else me],
        send_sem=send_sems.at[w * n_peer + k], recv_sem=recv_sems.at[w * n_peer + k],
        device_id=dev, device_id_type=pl.DeviceIdType.MESH)
        for k, (dev, idx) in reversed(list(enumerate(peers))) for w in range(len(ins))]
    return local, remote


OTHER_CHIPS = (2, 4, 6)


def _gather_copies(ins, outs, sems, cols):
    send_sems, recv_sems, loc_sems = sems
    x, y, c = lax.axis_index("x"), lax.axis_index("y"), lax.axis_index("c")
    me = 4 * x + 2 * y + c
    n_pair = N_DEV - 1

    def dev(mask):
        return (1 - x if mask & 4 else x, 1 - y if mask & 2 else y, 1 - c if mask & 1 else c)

    def slot(w, mask):
        j = jnp.bitwise_xor(me, mask)
        return _window(outs[w], j, cols[w]) if cols[w] else outs[w].at[j]

    def remote(w, pair, src, to_slot, target):
        return pltpu.make_async_remote_copy(src_ref=src, dst_ref=slot(w, to_slot), send_sem=send_sems.at[w * n_pair + pair],
                                            recv_sem=recv_sems.at[w * n_pair + pair], device_id=dev(target),
                                            device_id_type=pl.DeviceIdType.MESH)

    ws = range(len(ins))
    return dict(
        local=[pltpu.make_async_copy(ins[w], slot(w, 0), loc_sems.at[w]) for w in ws],
        to_chips=[remote(w, 1 + t, ins[w], 0, m) for t, m in enumerate(OTHER_CHIPS) for w in ws],
        to_core=[remote(w, 0, ins[w], 0, 1) for w in ws],
        from_chips=[remote(w, 1 + t, ins[w], m, 0) for t, m in enumerate(OTHER_CHIPS) for w in ws],
        pass_on=[remote(w, 4 + t, slot(w, m), m, 1) for t, m in enumerate(OTHER_CHIPS) for w in ws],
        from_core=[remote(w, 0, ins[w], 1, 0) for w in ws]
        + [remote(w, 4 + t, ins[w], m + 1, 0) for t, m in enumerate(OTHER_CHIPS) for w in ws])


def _exchange_start(ins, outs, sems, gather, cols):
    if gather:
        cps = _gather_copies(ins, outs, sems, cols)
        for cp in cps["local"] + cps["to_chips"] + cps["to_core"]:
            cp.start()
    else:
        local, remote = _scatter_copies(ins, outs, sems, cols, False)
        for cp in local + remote:
            cp.start()


def _exchange_pass_on(ins, outs, sems, gather, cols, chips):
    if gather:
        cps = _gather_copies(ins, outs, sems, cols)
        n = len(ins)
        for t in chips:
            for arrived, onward in zip(cps["from_chips"][t * n:(t + 1) * n], cps["pass_on"][t * n:(t + 1) * n]):
                arrived.wait_recv()
                onward.start()


def _exchange_wait(ins, outs, sems, gather, cols):
    if gather:
        cps = _gather_copies(ins, outs, sems, cols)
        for cp in cps["local"]:
            cp.wait()
        for cp in cps["to_chips"] + cps["to_core"] + cps["pass_on"]:
            cp.wait_send()
        for cp in cps["from_core"]:
            cp.wait_recv()
    else:
        local, remote = _scatter_copies(ins, outs, sems, cols, True)
        for cp in local:
            cp.wait()
        for cp in remote:
            cp.wait_send()
            cp.wait_recv()


def _exchange_shapes(arrs, gather, cols):
    n = len(arrs)
    out_shape = []
    for a, c in zip(arrs, cols):
        if gather:
            shape = (a.shape[0], N_DEV * c) if c else (N_DEV,) + a.shape
        else:
            shape = (N_DEV, a.shape[0], c) if c else a.shape
        out_shape.append(jax.ShapeDtypeStruct(shape, a.dtype))
    sems = [pltpu.SemaphoreType.DMA((n * (N_DEV - 1),)), pltpu.SemaphoreType.DMA((n * (N_DEV - 1),)),
            pltpu.SemaphoreType.DMA((n,))]
    return out_shape, sems


def _call(body, *, name, grid, in_specs, out_specs, out_shape, scratch, sem, args, ride=None):
    if ride is None:
        outs = pl.pallas_call(body, name=name, grid=grid, in_specs=in_specs, out_specs=out_specs, out_shape=out_shape,
                              scratch_shapes=scratch, compiler_params=_params(sem))(*args)
        return outs, None
    arrs, gather, cols = ride
    n, n_in, n_out, n_scr = len(arrs), len(in_specs), len(out_specs), len(scratch)
    x_shape, x_sems = _exchange_shapes(arrs, gather, cols)

    def riding(*refs):
        ins, x_ins = refs[:n_in], refs[n_in:n_in + n]
        outs = refs[n_in + n:n_in + n + n_out]
        x_outs = refs[n_in + n + n_out:n_in + 2 * n + n_out]
        scr = refs[n_in + 2 * n + n_out:n_in + 2 * n + n_out + n_scr]
        sems = refs[n_in + 2 * n + n_out + n_scr:]
        def at(step):
            return functools.reduce(jnp.logical_and, [pl.program_id(a) == v for a, v in enumerate(step)])

        @pl.when(at((0,) * len(grid)))
        def _():
            _exchange_start(x_ins, x_outs, sems, gather, cols)

        @pl.when(at((grid[0] // 2,) + (0,) * (len(grid) - 1)))
        def _():
            _exchange_pass_on(x_ins, x_outs, sems, gather, cols, (0, 1))

        @pl.when(at((grid[0] // 2,) + (0,) * (len(grid) - 2) + (5 * grid[-1] // 8,)))
        def _():
            _exchange_pass_on(x_ins, x_outs, sems, gather, cols, (2,))

        body(*ins, *outs, *scr)

        @pl.when(at(tuple(g - 1 for g in grid)))
        def _():
            _exchange_wait(x_ins, x_outs, sems, gather, cols)

    res = pl.pallas_call(
        riding, name=name, grid=grid, in_specs=list(in_specs) + [ANY] * n, out_specs=list(out_specs) + [ANY] * n,
        out_shape=list(out_shape) + x_shape, scratch_shapes=list(scratch) + x_sems,
        compiler_params=_params(("arbitrary",) * len(grid)))(*args, *arrs)
    return res[:n_out], res[n_out:]


def _my_block():
    return (4 * lax.axis_index("x") + 2 * lax.axis_index("y") + lax.axis_index("c")).astype(jnp.int32).reshape(1)


def _proj_in_gather(n, w_shard):
    t, k = n.shape
    cs = w_shard.shape[1]
    tm = min(MM_ROWS, t)
    ni = t // tm
    arrival = (0, 1) + OTHER_CHIPS + tuple(m + 1 for m in OTHER_CHIPS)

    def mask_at(s):
        return jnp.where(s < 2, s, jnp.where(s < 5, 2 * (s - 1), 2 * (s - 4) + 1))

    def body(me_ref, n_ref, w_hbm, o_ref, all_hbm, w_vmem, send_sems, recv_sems, loc_sems, load_sems):
        s, i = pl.program_id(0), pl.program_id(1)
        cps = _gather_copies([w_hbm], [all_hbm], (send_sems, recv_sems, loc_sems), (cs,))
        arrived = cps["local"] + cps["from_core"][:1] + cps["from_chips"] + cps["from_core"][1:]

        def load(step):
            src = w_hbm if step == 0 else _window(all_hbm, jnp.bitwise_xor(me_ref[0], arrival[step]), cs)
            return pltpu.make_async_copy(src, w_vmem.at[step % 2], load_sems.at[step % 2])

        @pl.when(jnp.logical_and(s == 0, i == 0))
        def _():
            for cp in cps["local"] + cps["to_chips"] + cps["to_core"]:
                cp.start()
            load(0).start()

        for step, mask in enumerate(arrival):
            @pl.when(jnp.logical_and(s == step, i == 0))
            def _(step=step):
                load(step).wait()

            if step + 1 < N_DEV:
                @pl.when(jnp.logical_and(s == step, i == min(1, ni - 1)))
                def _(step=step):
                    arrived[step + 1].wait_recv()
                    if arrival[step + 1] in OTHER_CHIPS:
                        cps["pass_on"][OTHER_CHIPS.index(arrival[step + 1])].start()
                    load(step + 1).start()

        o_ref[...] = lax.dot_general(n_ref[...], w_vmem[s % 2], NN, preferred_element_type=F32).astype(BF16)

        @pl.when(jnp.logical_and(s == N_DEV - 1, i == ni - 1))
        def _():
            cps["local"][0].wait()
            for cp in cps["to_chips"] + cps["to_core"] + cps["pass_on"]:
                cp.wait_send()

    return pl.pallas_call(
        body, name="proj_in",
        grid_spec=pltpu.PrefetchScalarGridSpec(
            num_scalar_prefetch=1, grid=(N_DEV, ni),
            in_specs=[pl.BlockSpec((tm, k), lambda s, i, me: (i, 0)), ANY],
            out_specs=[pl.BlockSpec((tm, cs), lambda s, i, me: (i, jnp.bitwise_xor(me[0], mask_at(s)))), ANY],
            scratch_shapes=[pltpu.VMEM((2, k, cs), BF16), pltpu.SemaphoreType.DMA((N_DEV - 1,)),
                            pltpu.SemaphoreType.DMA((N_DEV - 1,)), pltpu.SemaphoreType.DMA((1,)),
                            pltpu.SemaphoreType.DMA((2,))]),
        out_shape=[jax.ShapeDtypeStruct((t, N_DEV * cs), BF16), jax.ShapeDtypeStruct((k, N_DEV * cs), BF16)],
        compiler_params=_params(("arbitrary", "arbitrary")),
    )(_my_block(), n, w_shard)


def _gw_in_scatter(a, g):
    t, k = a.shape
    cs = g.shape[1] // N_DEV
    tm = min(MM_ROWS, t)
    nr = t // tm
    n_chip = N_DEV // 2
    chips = (6, 4, 2, 0)

    def body(me_ref, a_ref, g_ref, out_hbm, acc, stage, other, core_send, core_recv, chip_send, chip_recv, loc_sem):
        s, r = pl.program_id(0), pl.program_id(1)
        x, y, c = lax.axis_index("x"), lax.axis_index("y"), lax.axis_index("c")
        my_chip = 2 * x + y
        part = lax.dot_general(a_ref[...], g_ref[...], TN, preferred_element_type=F32)

        def to_core(m):
            return pltpu.make_async_remote_copy(src_ref=stage.at[0], dst_ref=other.at[m], send_sem=core_send.at[m],
                                                recv_sem=core_recv.at[m], device_id=(x, y, 1 - c),
                                                device_id_type=pl.DeviceIdType.MESH)

        def to_chip(m, landed):
            mask = chips[m]
            there = (1 - x if mask & 4 else x, 1 - y if mask & 2 else y, c)
            slot = (2 * there[0] + there[1]) if landed else my_chip
            return pltpu.make_async_remote_copy(src_ref=stage.at[1], dst_ref=out_hbm.at[slot], send_sem=chip_send.at[m],
                                                recv_sem=chip_recv.at[m], device_id=there,
                                                device_id_type=pl.DeviceIdType.MESH)

        local = pltpu.make_async_copy(stage.at[1], out_hbm.at[my_chip], loc_sem)

        @pl.when(r == 0)
        def _():
            acc[...] = part

        @pl.when(r > 0)
        def _():
            acc[...] += part

        for step in range(N_DEV):
            m = step // 2

            @pl.when(jnp.logical_and(s == step, r == nr - 1))
            def _(step=step, m=m):
                if step % 2 == 0:
                    if m > 0:
                        to_core(m - 1).wait_send()
                    stage[0] = acc[...].astype(BF16)
                    to_core(m).start()
                else:
                    if m > 0:
                        to_chip(m - 1, False).wait_send()
                    to_core(m).wait_recv()
                    stage[1] = (acc[...] + other[m].astype(F32)).astype(BF16)
                    if m < n_chip - 1:
                        to_chip(m, False).start()
                    else:
                        local.start()
                        to_core(m).wait_send()
                        local.wait()
                        for mm in range(n_chip - 1):
                            to_chip(mm, True).wait_recv()

    return pl.pallas_call(
        body, name="gw_in",
        grid_spec=pltpu.PrefetchScalarGridSpec(
            num_scalar_prefetch=1, grid=(N_DEV, nr),
            in_specs=[pl.BlockSpec((tm, k), lambda s, r, me: (r, 0)),
                      pl.BlockSpec((tm, cs), lambda s, r, me: (r, jnp.bitwise_xor(me[0], N_DEV - 1 - s)))],
            out_specs=ANY,
            scratch_shapes=[pltpu.VMEM((k, cs), F32), pltpu.VMEM((2, k, cs), BF16), pltpu.VMEM((n_chip, k, cs), BF16),
                            pltpu.SemaphoreType.DMA((n_chip,)), pltpu.SemaphoreType.DMA((n_chip,)),
                            pltpu.SemaphoreType.DMA((n_chip - 1,)), pltpu.SemaphoreType.DMA((n_chip - 1,)),
                            pltpu.SemaphoreType.DMA]),
        out_shape=jax.ShapeDtypeStruct((n_chip, k, cs), BF16),
        compiler_params=_params(("arbitrary", "arbitrary")),
    )(_my_block(), a, g)


SMALL_ROWS = 8


def _allreduce_small(parts, loss_part):
    n, d = len(parts), parts[0].shape[1]

    def body(*refs):
        part_refs, loss_ref, o_ref = refs[:n], refs[n], refs[n + 1]
        mine_ref, all_ref, send_sems, recv_sems = refs[n + 2:]
        me, peers = _peers()
        mine_ref[...] = jnp.zeros_like(mine_ref)
        for i, p_ref in enumerate(part_refs):
            mine_ref[i:i + 1, :] = p_ref[...]
        mine_ref[SMALL_ROWS - 1:SMALL_ROWS, 0:LANES] = loss_ref[0:1, :]
        all_ref[me] = mine_ref[...]
        for k, (dev, idx) in enumerate(peers):
            pltpu.make_async_remote_copy(src_ref=mine_ref, dst_ref=all_ref.at[me], send_sem=send_sems.at[k],
                                         recv_sem=recv_sems.at[k], device_id=dev,
                                         device_id_type=pl.DeviceIdType.MESH).start()
        for k, (dev, idx) in enumerate(peers):
            cp = pltpu.make_async_remote_copy(src_ref=mine_ref, dst_ref=all_ref.at[idx], send_sem=send_sems.at[k],
                                              recv_sem=recv_sems.at[k], device_id=dev,
                                              device_id_type=pl.DeviceIdType.MESH)
            cp.wait_send()
            cp.wait_recv()
        tot = all_ref[0]
        for dvc in range(1, N_DEV):
            tot = tot + all_ref[dvc]
        o_ref[...] = tot

    return pl.pallas_call(
        body, name="allreduce_small", in_specs=[VMEM] * (n + 1), out_specs=VMEM,
        out_shape=jax.ShapeDtypeStruct((SMALL_ROWS, d), F32),
        scratch_shapes=[pltpu.VMEM((SMALL_ROWS, d), F32), pltpu.VMEM((N_DEV, SMALL_ROWS, d), F32),
                        pltpu.SemaphoreType.DMA((N_DEV - 1,)), pltpu.SemaphoreType.DMA((N_DEV - 1,))],
    )(*parts, loss_part)


def _adam_math(g, w, m, v):
    m_new = ADAM_B1 * m + (1.0 - ADAM_B1) * g
    v_new = ADAM_B2 * v + (1.0 - ADAM_B2) * (g * g)
    m_hat = m_new / (1.0 - ADAM_B1 ** ADAM_STEP)
    v_hat = v_new / (1.0 - ADAM_B2 ** ADAM_STEP)
    delta = -ADAM_LR * (m_hat / (jnp.sqrt(v_hat) + ADAM_EPS) + ADAM_WD * w)
    return delta, m_new, v_new


def _adam(name, pieces, w, m, v):
    r, c = w.shape
    n_piece, _, cp = pieces.shape
    tr = r
    for cand in (256, 176, 128, 64):
        if r % cand == 0 and r > cand:
            tr = cand
            break

    def body(p_ref, w_ref, m_ref, v_ref, g_ref, d_ref, mo_ref, vo_ref):
        g = p_ref[0, :, 0:c].astype(F32)
        for j in range(1, n_piece):
            g = g + p_ref[j, :, 0:c].astype(F32)
        delta, m_new, v_new = _adam_math(g, w_ref[...], m_ref[...], v_ref[...])
        g_ref[...] = g
        d_ref[...] = delta
        mo_ref[...] = m_new
        vo_ref[...] = v_new

    blk = pl.BlockSpec((tr, c), lambda i: (i, 0))
    osh = jax.ShapeDtypeStruct((r, c), F32)
    return pl.pallas_call(
        body, name=name, grid=(r // tr,),
        in_specs=[pl.BlockSpec((n_piece, tr, cp), lambda i: (0, i, 0)), blk, blk, blk],
        out_specs=[blk, blk, blk, blk], out_shape=[osh, osh, osh, osh],
        compiler_params=_params(("parallel",)),
    )(pieces, w, m, v)


def _adam_small(g_all, ws, ms, vs):
    n = len(ws)

    def body(*refs):
        g_ref, ins, outs = refs[0], refs[1:1 + 3 * n], refs[1 + 3 * n:]
        for i in range(n):
            g = g_ref[i:i + 1, :]
            delta, m_new, v_new = _adam_math(g, ins[i][...], ins[n + i][...], ins[2 * n + i][...])
            for kind, val in enumerate((g, delta, m_new, v_new)):
                outs[kind * n + i][...] = val

    osh = jax.ShapeDtypeStruct(ws[0].shape, F32)
    res = pl.pallas_call(body, name="adam_small", in_specs=[VMEM] * (1 + 3 * n), out_specs=[VMEM] * (4 * n),
                         out_shape=[osh] * (4 * n))(g_all, *ws, *ms, *vs)
    return res[:n], res[n:2 * n], res[2 * n:3 * n], res[3 * n:]


def _local_step(x, mem, pos, tgt, gains, w_in_shard, shards, batch):
    g_mix, g_mem_q, g_mem_kv, g_ffn, g_final = gains
    t, d = x.shape
    s = t // batch
    n_mem = mem.shape[0] // batch
    n_sh = N_DEV
    width = shards[0].shape[0]
    nb = width // LANES

    lane = np.arange(LANES) % HEAD_DIM
    sel_lo = (lane < ROPE_HALF).astype(np.float32)[None, :]
    sel_hi = ((lane >= ROPE_HALF) & (lane < 2 * ROPE_HALF)).astype(np.float32)[None, :]
    freqs = np.float32(ROPE_THETA) ** (-np.arange(ROPE_HALF, dtype=np.float32) / np.float32(ROPE_HALF))
    inv_freq = np.where(lane < 2 * ROPE_HALF, freqs[lane % ROPE_HALF], 0.0).astype(np.float32)[None, :]
    cos_t, sin_a, sin_b = _rope_tables(pos, jnp.asarray(inv_freq), jnp.asarray(sel_lo), jnp.asarray(sel_hi))
    bias = _dilated_bias_tiles(s)

    n1 = _rms_fwd("norm_mix", x, g_mix)
    proj, w_in = _proj_in_gather(n1, w_in_shard)
    qk_a = _rope_apply("rope_fwd", proj, 0, 2 * nb, cos_t, sin_a, sin_b, 1.0)
    cs_up, cs_ffn = shards[0].shape[1], shards[6].shape[1]
    (o_a, lse_a), (w_up_a, w_up_b, w_out, w_q, w_kv, w_o, w_fd) = _da_fwd(
        qk_a, proj, 2 * nb, bias, batch, s,
        ride=(shards[:6] + shards[8:], True, (cs_up, cs_up, 0, 0, 0, cs_up, 0)))
    (o_b, tot_b), (w_fg, w_fu) = _sb_fwd(proj, 3 * nb, 4 * nb, 5 * nb, batch, s,
                                         ride=(shards[6:8], True, (cs_ffn, cs_ffn)))
    w_out = w_out.reshape(d, d)
    w_q = w_q.reshape(d, -1)
    w_kv = w_kv.reshape(d, -1)
    w_fd = w_fd.reshape(-1, d)
    ua, ub, mixed = _mixer_fwd(o_a, o_b, w_up_a, w_up_b, proj, 6 * nb)
    h1 = _mm_w("mix_out", mixed, w_out, F32, res=x)
    n2 = _rms_fwd("norm_mem_q", h1, g_mem_q)
    mem_n = _rms_fwd("norm_mem_kv", mem, g_mem_kv)
    q_m = _mm_w("mem_q", n2, w_q, BF16)
    kv_m = _mm_w("mem_kv", mem_n, w_kv, BF16)
    o_m = _mem_fwd(q_m, kv_m, batch, s, n_mem)
    h2 = _mm_w("mem_out", o_m, w_o, F32, res=h1)
    n3 = _rms_fwd("norm_ffn", h2, g_ffn)
    hg, hu, act = _ffn_up(n3, w_fg, w_fu)
    h3 = _mm_w("ffn_down", act, w_fd, F32, res=h2, tm=ROW_TILE)
    loss_part, dh3, dh3_b, dg_final = _loss_head(h3, tgt, g_final.reshape(1, d))

    dhg, dhu = _ffn_bwd_act(dh3_b, w_fd, hg, hu)
    gw_fd = _wgrad("gw_ffn_down", act, dh3_b)
    gw_fg = _wgrad("gw_ffn_gate", n3, dhg)
    gw_fu = _wgrad("gw_ffn_up", n3, dhu)
    dn3 = _mm_w("dn_ffn_gate", dhg, w_fg, F32, dims=NT, tm=ROW_TILE)
    dn3 = _mm_w("dn_ffn_up", dhu, w_fu, F32, dims=NT, res=dn3, tm=ROW_TILE)
    dh2, dh2_b, dg_ffn = _rms_bwd("norm_ffn_bwd", dn3, h2, g_ffn, dh3, ("f32", "bf16"))

    do_m = _mm_w("mem_out_bwd", dh2_b, w_o, BF16, dims=NT)
    gw_o = _wgrad("gw_mem_o", o_m, dh2_b)
    dq_m, dkv_m = _mem_bwd(q_m, kv_m, do_m, batch, s, n_mem)
    gw_q = _wgrad("gw_mem_q", n2, dq_m)
    gw_kv = _wgrad("gw_mem_kv", mem_n, dkv_m)
    dn2 = _mm_w("mem_q_bwd", dq_m, w_q, F32, dims=NT)
    dmem_n = _mm_w("mem_kv_bwd", dkv_m, w_kv, F32, dims=NT)
    (dg_mem_kv,) = _rms_bwd("norm_mem_kv_bwd", dmem_n, mem, g_mem_kv, None, ())
    dh1, dh1_b, dg_mem_q = _rms_bwd("norm_mem_q_bwd", dn2, h1, g_mem_q, dh2, ("f32", "bf16"))

    dmix = _mm_w("mix_out_bwd", dh1_b, w_out, BF16, dims=NT)
    gw_out = _wgrad("gw_out", mixed, dh1_b)
    dua, dub, dgates = _mixer_bwd(dmix, ua, ub, proj, 6 * nb)
    do_a = _mm_w("up_a_bwd", dua, w_up_a, BF16, dims=NT)
    do_b = _mm_w("up_b_bwd", dub, w_up_b, BF16, dims=NT)
    gw_ua = _wgrad("gw_up_a", o_a, dua)
    gw_ub = _wgrad("gw_up_b", o_b, dub)
    (dq_ar, dk_ar, dv_a), (p_fg, p_fd) = _da_bwd(
        qk_a, proj, 2 * nb, bias, o_a, lse_a, do_a, batch, s,
        ride=([gw_fg, gw_fd.reshape(n_sh, -1, d)], False, (cs_ffn, 0)))
    dqk_a = _rope_apply("rope_bwd", jnp.concatenate([dq_ar, dk_ar], axis=1), 0, 2 * nb, cos_t, sin_a, sin_b, -1.0)
    mid = [gw_ua, gw_ub, gw_out.reshape(n_sh, -1, d), gw_q.reshape(n_sh, -1, gw_q.shape[1]),
           gw_kv.reshape(n_sh, -1, gw_kv.shape[1]), gw_o, gw_fu]
    (dq_b, dk_b, dv_b), (*p_mid, p_fu) = _sb_bwd(proj, 3 * nb, 4 * nb, 5 * nb, tot_b, do_b, batch, s,
                                                 ride=(mid, False, (cs_up, cs_up, 0, 0, 0, cs_up, cs_ffn)))
    p_ffn = [p_fg, p_fu, p_fd]
    dproj = jnp.concatenate([dqk_a, dv_a, dq_b, dk_b, dv_b, dgates], axis=1)
    dn1 = _mm_w("proj_in_bwd", dproj, w_in, F32, dims=NT, tm=ROW_TILE)
    p_in = _gw_in_scatter(n1, dproj)
    grad_x, dg_mix = _rms_bwd("norm_mix_bwd", dn1, x, g_mix, dh1, ("f32",))
    return loss_part, grad_x, [p_in] + list(p_mid) + p_ffn, (dg_mix, dg_mem_q, dg_mem_kv, dg_ffn, dg_final)


WEIGHTS =("w_in", "w_up_a", "w_up_b", "w_out", "w_q_mem", "w_kv_mem", "w_o_mem", "w_ffn_gate", "w_ffn_up", "w_ffn_down")
GAINS = ("g_mix", "g_mem_q", "g_mem_kv", "g_ffn", "g_final")
ORDER = ("g_mix", "w_in", "w_up_a", "w_up_b", "w_out", "g_mem_q", "g_mem_kv", "w_q_mem", "w_kv_mem", "w_o_mem", "g_ffn",
         "w_ffn_gate", "w_ffn_up", "w_ffn_down", "g_final")


def kernel(x, mem, positions, g_mix, w_in, w_up_a, w_up_b, w_out, g_mem_q, g_mem_kv, w_q_mem, w_kv_mem, w_o_mem, g_ffn, w_ffn_gate, w_ffn_up, w_ffn_down, g_final, loss_target, m_g_mix, m_w_in, m_w_up_a, m_w_up_b, m_w_out, m_g_mem_q, m_g_mem_kv, m_w_q_mem, m_w_kv_mem, m_w_o_mem, m_g_ffn, m_w_ffn_gate, m_w_ffn_up, m_w_ffn_down, m_g_final, v_g_mix, v_w_in, v_w_up_a, v_w_up_b, v_w_out, v_g_mem_q, v_g_mem_kv, v_w_q_mem, v_w_kv_mem, v_w_o_mem, v_g_ffn, v_w_ffn_gate, v_w_ffn_up, v_w_ffn_down, v_g_final):
    given = dict(locals())
    batch, s, d = x.shape
    t = batch * s
    shard = {n: given[n].reshape(given[n].shape[-2:]) for n in WEIGHTS}
    gains = [given[n].reshape(1, d) for n in GAINS]

    pad = (-shard["w_ffn_down"].shape[0]) % LANES
    pads = {"w_ffn_gate": (0, pad), "w_ffn_up": (0, pad), "w_ffn_down": (pad, 0)}
    cast = _cast_weights([shard[n] for n in WEIGHTS], [pads.get(n, (0, 0)) for n in WEIGHTS])
    loss_part, grad_x, pieces, dgains = _local_step(
        x.reshape(t, d), mem.reshape(-1, d), positions.reshape(t, 1), loss_target.reshape(t, d), gains, cast[0],
        cast[1:], batch)

    grad, delta, new_m, new_v = {}, {}, {}, {}
    for n, p in zip(WEIGHTS, pieces):
        m2, v2 = given["m_" + n].reshape(shard[n].shape), given["v_" + n].reshape(shard[n].shape)
        outs = _adam("adam_" + n, p, shard[n], m2, v2)
        grad[n], delta[n], new_m[n], new_v[n] = [o.reshape(given[n].shape) for o in outs]

    g_all = _allreduce_small(list(dgains), loss_part)
    small = _adam_small(g_all, gains, [given["m_" + n].reshape(1, d) for n in GAINS],
                        [given["v_" + n].reshape(1, d) for n in GAINS])
    for out, vals in zip((grad, delta, new_m, new_v), small):
        for n, val in zip(GAINS, vals):
            out[n] = val.reshape(given[n].shape)

    loss = g_all[SMALL_ROWS - 1, 0]
    return (loss, grad_x.reshape(x.shape), *[grad[n] for n in ORDER], *[delta[n] for n in ORDER],
            *[new_m[n] for n in ORDER], *[new_v[n] for n in ORDER])
```

```python
import functools
import math

import jax
import jax.numpy as jnp
import numpy as np
from jax import lax
from jax.experimental import pallas as pl
from jax.experimental.pallas import tpu as pltpu

F32 = jnp.float32
BF16 = jnp.bfloat16

N_DEV = 8
HEAD_DIM = 64
MEM_HEAD_DIM = 128
N_HEADS_MEM = 4
BLOCK = 128
DIL_PATTERNS = ((128, 1), (512, 4), (2048, 16))
ROPE_THETA = 500000.0
ROPE_HALF = 8
RMS_EPS = 1e-6
ADAM_LR, ADAM_B1, ADAM_B2, ADAM_EPS, ADAM_WD, ADAM_STEP = 0.001, 0.9, 0.999, 1e-08, 0.01, 10
NEG = -1e30
ROW_TILE = 512
LANES = 128

ANY = pl.BlockSpec(memory_space=pl.ANY)
VMEM = pl.BlockSpec(memory_space=pltpu.VMEM)
NN = (((1,), (0,)), ((), ()))
NT = (((1,), (1,)), ((), ()))
TN = (((0,), (0,)), ((), ()))


def _params(sem):
    return pltpu.CompilerParams(dimension_semantics=sem)


def _mm(name, a, b, *, grid, a_spec, b_spec, o_shape, o_spec, dims, out_dtype, nk=1, res=None, res_spec=None):
    has_res = res is not None

    def body(*refs):
        a_ref, b_ref = refs[0], refs[1]
        r_ref = refs[2] if has_res else None
        o_ref = refs[3] if has_res else refs[2]
        p = lax.dot_general(a_ref[...], b_ref[...], dims, preferred_element_type=F32)
        if nk == 1:
            if has_res:
                p = p + r_ref[...].astype(F32)
            o_ref[...] = p.astype(out_dtype)
            return
        acc_ref = refs[-1]
        k = pl.program_id(len(grid) - 1)

        @pl.when(k == 0)
        def _():
            acc_ref[...] = p

        @pl.when(k > 0)
        def _():
            acc_ref[...] += p

        @pl.when(k == nk - 1)
        def _():
            t = acc_ref[...]
            if has_res:
                t = t + r_ref[...].astype(F32)
            o_ref[...] = t.astype(out_dtype)

    o_block = tuple(d for d in o_spec.block_shape if d is not None)
    sem = ("parallel",) * (len(grid) - 1) + (("arbitrary",) if nk > 1 else ("parallel",))
    return pl.pallas_call(
        body, name=name, grid=grid,
        in_specs=[a_spec, b_spec] + ([res_spec] if has_res else []),
        out_specs=o_spec, out_shape=jax.ShapeDtypeStruct(o_shape, out_dtype),
        scratch_shapes=[pltpu.VMEM(o_block, F32)] if nk > 1 else [],
        compiler_params=_params(sem),
    )(*([a, b] + ([res] if has_res else [])))


def _rms_fwd(name, x, g):
    t, d = x.shape
    tm = min(ROW_TILE, t)

    def body(x_ref, g_ref, o_ref):
        xf = x_ref[...]
        r = lax.rsqrt(jnp.mean(xf * xf, axis=-1, keepdims=True) + RMS_EPS)
        o_ref[...] = (xf * r * g_ref[...]).astype(BF16)

    return pl.pallas_call(
        body, name=name, grid=(t // tm,),
        in_specs=[pl.BlockSpec((tm, d), lambda i: (i, 0)), pl.BlockSpec((1, d), lambda i: (0, 0))],
        out_specs=pl.BlockSpec((tm, d), lambda i: (i, 0)), out_shape=jax.ShapeDtypeStruct((t, d), BF16),
        compiler_params=_params(("parallel",)),
    )(x, g)


def _rms_bwd(name, dn, x, g, dres, want):
    t, d = x.shape
    tm = min(ROW_TILE, t)
    has_res = dres is not None

    def body(*refs):
        dn_ref, x_ref, g_ref = refs[0], refs[1], refs[2]
        r_ref = refs[3] if has_res else None
        dx_refs, dg_ref = refs[-1 - len(want):-1], refs[-1]
        xf = x_ref[...]
        r = lax.rsqrt(jnp.mean(xf * xf, axis=-1, keepdims=True) + RMS_EPS)
        xh = xf * r
        dnf = dn_ref[...].astype(F32)
        if want:
            dxh = dnf * g_ref[...]
            dx = r * (dxh - xh * jnp.mean(dxh * xh, axis=-1, keepdims=True))
            if has_res:
                dx = dx + r_ref[...]
            for kind, dx_ref in zip(want, dx_refs):
                dx_ref[...] = dx.astype(F32 if kind == "f32" else BF16)

        @pl.when(pl.program_id(0) == 0)
        def _():
            dg_ref[...] = jnp.zeros_like(dg_ref)

        dg_ref[...] += jnp.sum(dnf * xh, axis=0, keepdims=True)

    row = pl.BlockSpec((tm, d), lambda i: (i, 0))
    vec = pl.BlockSpec((1, d), lambda i: (0, 0))
    return pl.pallas_call(
        body, name=name, grid=(t // tm,),
        in_specs=[row, row, vec] + ([row] if has_res else []),
        out_specs=[row] * len(want) + [vec],
        out_shape=[jax.ShapeDtypeStruct((t, d), F32 if kind == "f32" else BF16) for kind in want]
        + [jax.ShapeDtypeStruct((1, d), F32)],
        compiler_params=_params(("arbitrary",)),
    )(*([dn, x, g] + ([dres] if has_res else [])))


def _loss_head(h, tgt, g):
    t, d = h.shape
    tm = min(ROW_TILE, t)

    def body(h_ref, t_ref, g_ref, loss_ref, dh_ref, dhb_ref, dg_ref):
        xf = h_ref[...]
        gv = g_ref[...]
        r = lax.rsqrt(jnp.mean(xf * xf, axis=-1, keepdims=True) + RMS_EPS)
        xh = xf * r
        e = xh * gv - t_ref[...]
        dy = e * (1.0 / d)
        dxh = dy * gv
        dh = r * (dxh - xh * jnp.mean(dxh * xh, axis=-1, keepdims=True))
        dh_ref[...] = dh
        dhb_ref[...] = dh.astype(BF16)

        @pl.when(pl.program_id(0) == 0)
        def _():
            dg_ref[...] = jnp.zeros_like(dg_ref)
            loss_ref[...] = jnp.zeros_like(loss_ref)

        dg_ref[...] += jnp.sum(dy * xh, axis=0, keepdims=True)
        part = jnp.sum(jnp.sum(e * e, axis=1, keepdims=True), axis=0, keepdims=True) * (0.5 / d)
        loss_ref[...] += jnp.broadcast_to(part, loss_ref.shape)

    row = pl.BlockSpec((tm, d), lambda i: (i, 0))
    vec = pl.BlockSpec((1, d), lambda i: (0, 0))
    return pl.pallas_call(
        body, name="loss_head", grid=(t // tm,),
        in_specs=[row, row, vec],
        out_specs=[pl.BlockSpec((8, LANES), lambda i: (0, 0)), row, row, vec],
        out_shape=[jax.ShapeDtypeStruct((8, LANES), F32), jax.ShapeDtypeStruct((t, d), F32),
                   jax.ShapeDtypeStruct((t, d), BF16), jax.ShapeDtypeStruct((1, d), F32)],
        compiler_params=_params(("arbitrary",)),
    )(h, tgt, g)


def _rope_tables(pos, inv_freq, sel_lo, sel_hi):
    t = pos.shape[0]
    tm = min(ROW_TILE, t)

    def body(p_ref, f_ref, lo_ref, hi_ref, c_ref, sa_ref, sb_ref):
        ang = p_ref[...].astype(F32) * f_ref[...]
        rot = lo_ref[...] + hi_ref[...]
        cs, sn = jnp.cos(ang), jnp.sin(ang)
        c_ref[...] = cs * rot + (1.0 - rot)
        sa_ref[...] = -sn * lo_ref[...]
        sb_ref[...] = sn * hi_ref[...]

    vec = pl.BlockSpec((1, LANES), lambda i: (0, 0))
    row = pl.BlockSpec((tm, LANES), lambda i: (i, 0))
    return pl.pallas_call(
        body, name="rope_tables", grid=(t // tm,),
        in_specs=[pl.BlockSpec((tm, 1), lambda i: (i, 0)), vec, vec, vec],
        out_specs=[row, row, row], out_shape=[jax.ShapeDtypeStruct((t, LANES), F32)] * 3,
        compiler_params=_params(("parallel",)),
    )(pos, inv_freq, sel_lo, sel_hi)


def _rope_apply(name, src, col0, n_cols, cos_t, sin_a, sin_b, sign):
    t = src.shape[0]
    tm = min(ROW_TILE, t)

    def body(x_ref, c_ref, sa_ref, sb_ref, o_ref):
        cs, sa, sb = c_ref[...], sign * sa_ref[...], sign * sb_ref[...]
        for c in range(n_cols):
            cols = slice(c * LANES, (c + 1) * LANES)
            xf = x_ref[:, cols].astype(F32)
            up = pltpu.roll(xf, LANES - ROPE_HALF, 1)
            dn = pltpu.roll(xf, ROPE_HALF, 1)
            o_ref[:, cols] = (xf * cs + up * sa + dn * sb).astype(BF16)

    wide = n_cols * LANES
    tab = pl.BlockSpec((tm, LANES), lambda i: (i, 0))
    return pl.pallas_call(
        body, name=name, grid=(t // tm,),
        in_specs=[pl.BlockSpec((tm, wide), lambda i: (i, col0 // n_cols)), tab, tab, tab],
        out_specs=pl.BlockSpec((tm, wide), lambda i: (i, 0)),
        out_shape=jax.ShapeDtypeStruct((t, wide), BF16),
        compiler_params=_params(("parallel",)),
    )(src, cos_t, sin_a, sin_b)


DA_T = 256
FWD_STREAMS = 4
BWD_STREAMS = 2


def _lane_lo():
    return lax.broadcasted_iota(jnp.int32, (BLOCK, LANES), 1) < HEAD_DIM


def _dilated_bias_tiles(s):
    n = s // DA_T
    dist = (np.arange(n)[:, None, None] * DA_T + np.arange(DA_T)[None, :, None] - np.arange(DA_T)[None, None, :])
    cnt = np.zeros(dist.shape, np.float32)
    for window, dil in DIL_PATTERNS:
        cnt += ((dist >= 0) & (dist % dil == 0) & (dist <= window)).astype(np.float32)
    return jnp.asarray(np.where(cnt > 0, np.log(np.maximum(cnt, 1.0)), NEG).astype(np.float32))


def _stack_heads(x, lo):
    zero = jnp.zeros_like(x)
    return jnp.concatenate([jnp.where(lo, x, zero), jnp.where(lo, zero, x)], axis=0)


def _da_fwd(qk, proj, v_col0, bias, batch, s, ride=None, streams=FWD_STREAMS):
    t = qk.shape[0]
    nq = s // DA_T
    n_pairs = 4
    ns = streams
    wide = ns * LANES
    scale = HEAD_DIM ** -0.5

    def body(q_ref, k_ref, v_ref, b_ref, o_ref, lse_ref, acc_ref, m_ref, l_ref):
        i = pl.program_id(2)
        lo = lax.broadcasted_iota(jnp.int32, (DA_T, LANES), 1) < HEAD_DIM
        ones = jnp.ones((DA_T, LANES), BF16)
        acc_ref[...] = jnp.zeros_like(acc_ref)
        m_ref[...] = jnp.full(m_ref.shape, NEG, F32)
        l_ref[...] = jnp.zeros_like(l_ref)
        qqs = [_stack_heads(q_ref[:, st * LANES:(st + 1) * LANES] * scale, lo) for st in range(ns)]

        def scores(st, rows, bias2):
            k = k_ref[rows, st * LANES:(st + 1) * LANES]
            return lax.dot_general(qqs[st], k, NT, preferred_element_type=F32) + bias2

        def softmax(st, sc):
            m_old = m_ref[st]
            m_new = jnp.maximum(m_old, jnp.max(sc, axis=1, keepdims=True))
            m_ref[st] = m_new
            return jnp.exp(sc - m_new).astype(BF16), jnp.exp(m_old - m_new)

        def values(st, rows, p, alpha):
            v = v_ref[rows, st * LANES:(st + 1) * LANES]
            vz = jnp.zeros_like(v)
            l_ref[st] = alpha * l_ref[st] + lax.dot_general(p, ones, NN, preferred_element_type=F32)
            pv = (lax.dot_general(p[:DA_T], jnp.where(lo, v, vz), NN, preferred_element_type=F32)
                  + lax.dot_general(p[DA_T:], jnp.where(lo, vz, v), NN, preferred_element_type=F32))
            acc_ref[st] = acc_ref[st] * jnp.where(lo, alpha[:DA_T], alpha[DA_T:]) + pv

        def trip(dlt, carry):
            rows = pl.ds(pl.multiple_of((i - dlt) * DA_T, DA_T), DA_T)
            bias_t = b_ref[dlt]
            bias2 = jnp.concatenate([bias_t, bias_t], axis=0)
            scs = [scores(st, rows, bias2) for st in range(ns)]
            pas = [softmax(st, scs[st]) for st in range(ns)]
            for st in range(ns):
                values(st, rows, *pas[st])
            return carry

        lax.fori_loop(0, i + 1, trip, 0)
        for st in range(ns):
            cols = slice(st * LANES, (st + 1) * LANES)
            l_t = l_ref[st]
            o_ref[:, cols] = (acc_ref[st] / jnp.where(lo, l_t[:DA_T], l_t[DA_T:])).astype(BF16)
            lse = m_ref[st] + jnp.log(l_t)
            lse_ref[:, cols] = jnp.where(lo, lse[:DA_T], lse[DA_T:])

    blk = pl.BlockSpec((DA_T, wide), lambda b, h, i: (b * nq + i, h))
    return _call(
        body, name="attn_a_fwd", grid=(batch, n_pairs // ns, nq),
        in_specs=[blk,
                  pl.BlockSpec((s, wide), lambda b, h, i: (b, n_pairs // ns + h)),
                  pl.BlockSpec((s, wide), lambda b, h, i: (b, v_col0 // ns + h)),
                  pl.BlockSpec((nq, DA_T, DA_T), lambda b, h, i: (0, 0, 0))],
        out_specs=[blk, blk],
        out_shape=[jax.ShapeDtypeStruct((t, n_pairs * LANES), BF16), jax.ShapeDtypeStruct((t, n_pairs * LANES), F32)],
        scratch=[pltpu.VMEM((ns, DA_T, LANES), F32), pltpu.VMEM((ns, 2 * DA_T, 1), F32),
                 pltpu.VMEM((ns, 2 * DA_T, LANES), F32)],
        sem=("parallel", "parallel", "arbitrary"), args=(qk, qk, proj, bias), ride=ride)


def _da_bwd(qk, proj, v_col0, bias, o, lse, do, batch, s, ride=None, streams=BWD_STREAMS):
    t = qk.shape[0]
    nq = s // DA_T
    n_pairs = 4
    ns = streams
    wide = ns * LANES
    scale = HEAD_DIM ** -0.5

    def body(q_ref, k_ref, v_ref, b_ref, o_ref, lse_ref, do_ref, dq_ref, dk_ref, dv_ref, dk_acc, dv_acc, dq_acc):
        i = pl.program_id(2)
        lo = lax.broadcasted_iota(jnp.int32, (DA_T, LANES), 1) < HEAD_DIM

        @pl.when(i == 0)
        def _():
            dk_acc[...] = jnp.zeros_like(dk_acc)
            dv_acc[...] = jnp.zeros_like(dv_acc)

        dq_acc[...] = jnp.zeros_like(dq_acc)
        qqs, dds, deltas, lses = [], [], [], []
        for st in range(ns):
            cols = slice(st * LANES, (st + 1) * LANES)
            do_ = do_ref[:, cols]
            qqs.append(_stack_heads(q_ref[:, cols] * scale, lo))
            dds.append(_stack_heads(do_, lo))
            prod = do_.astype(F32) * o_ref[:, cols].astype(F32)
            fz = jnp.zeros_like(prod)
            deltas.append(jnp.concatenate([jnp.sum(jnp.where(lo, prod, fz), axis=1, keepdims=True),
                                           jnp.sum(jnp.where(lo, fz, prod), axis=1, keepdims=True)], axis=0))
            lse_t = lse_ref[:, cols]
            lses.append(jnp.concatenate([lse_t[:, 0:1], lse_t[:, HEAD_DIM:HEAD_DIM + 1]], axis=0))

        def products(st, rows, bias2):
            cols = slice(st * LANES, (st + 1) * LANES)
            sc = lax.dot_general(qqs[st], k_ref[rows, cols], NT, preferred_element_type=F32) + bias2
            return sc, lax.dot_general(dds[st], v_ref[rows, cols], NT, preferred_element_type=F32)

        def weights(st, sc, dp):
            p = jnp.exp(sc - lses[st])
            return (p * (dp - deltas[st])).astype(BF16), p.astype(BF16)

        def gradients(st, rows, ds, p):
            cols = slice(st * LANES, (st + 1) * LANES)
            k = k_ref[rows, cols]
            kz = jnp.zeros_like(k)
            dq_acc[st] += (lax.dot_general(ds[:DA_T], jnp.where(lo, k, kz), NN, preferred_element_type=F32)
                           + lax.dot_general(ds[DA_T:], jnp.where(lo, kz, k), NN, preferred_element_type=F32))
            dk_acc[rows, cols] += lax.dot_general(ds, qqs[st], TN, preferred_element_type=F32)
            dv_acc[rows, cols] += lax.dot_general(p, dds[st], TN, preferred_element_type=F32)

        def trip(dlt, carry):
            rows = pl.ds(pl.multiple_of((i - dlt) * DA_T, DA_T), DA_T)
            bias_t = b_ref[dlt]
            bias2 = jnp.concatenate([bias_t, bias_t], axis=0)
            prods = [products(st, rows, bias2) for st in range(ns)]
            wts = [weights(st, *prods[st]) for st in range(ns)]
            for st in range(ns):
                gradients(st, rows, *wts[st])
            return carry

        lax.fori_loop(0, i + 1, trip, 0)
        for st in range(ns):
            dq_ref[:, st * LANES:(st + 1) * LANES] = (dq_acc[st] * scale).astype(BF16)

        @pl.when(i == nq - 1)
        def _():
            dk_ref[...] = dk_acc[...].astype(BF16)
            dv_ref[...] = dv_acc[...].astype(BF16)

    blk = pl.BlockSpec((DA_T, wide), lambda b, h, i: (b * nq + i, h))
    seq = pl.BlockSpec((s, wide), lambda b, h, i: (b, h))
    out = jax.ShapeDtypeStruct((t, n_pairs * LANES), BF16)
    return _call(
        body, name="attn_a_bwd", grid=(batch, n_pairs // ns, nq),
        in_specs=[blk,
                  pl.BlockSpec((s, wide), lambda b, h, i: (b, n_pairs // ns + h)),
                  pl.BlockSpec((s, wide), lambda b, h, i: (b, v_col0 // ns + h)),
                  pl.BlockSpec((nq, DA_T, DA_T), lambda b, h, i: (0, 0, 0)),
                  blk, blk, blk],
        out_specs=[blk, seq, seq], out_shape=[out, out, out],
        scratch=[pltpu.VMEM((s, wide), F32), pltpu.VMEM((s, wide), F32), pltpu.VMEM((ns, DA_T, LANES), F32)],
        sem=("parallel", "parallel", "arbitrary"), args=(qk, qk, proj, bias, o, lse, do), ride=ride)


SB_Q = 256


def _sb_consts(after):
    r = lax.broadcasted_iota(jnp.int32, (2 * BLOCK, 2 * BLOCK), 0) % BLOCK
    c = lax.broadcasted_iota(jnp.int32, (2 * BLOCK, 2 * BLOCK), 1)
    tri = (r > c) if after else (r < c)
    return jnp.logical_or(c >= BLOCK, tri).astype(BF16)


def _split(x):
    hi = x.astype(BF16)
    lo = (x - hi.astype(F32)).astype(BF16)
    return jnp.concatenate([hi, lo], axis=1)


def _sb_fwd(proj, q_col0, k_col0, v_col0, batch, s, ride=None, streams=FWD_STREAMS):
    t = proj.shape[0]
    nq = s // SB_Q
    n_pairs = 4
    ns = streams
    wide = ns * LANES
    scale = HEAD_DIM ** -0.5

    def body(q_ref, k_ref, v_ref, o_ref, tot_ref, acc_ref, run_ref):
        i = pl.program_id(2)
        lo_q = lax.broadcasted_iota(jnp.int32, (SB_Q, LANES), 1) < HEAD_DIM
        lo_k = _lane_lo()
        mat = _sb_consts(True)
        row = lax.broadcasted_iota(jnp.int32, (2 * SB_Q, LANES), 0) % SB_Q
        ahead = row - lax.broadcasted_iota(jnp.int32, (2 * SB_Q, LANES), 1)
        acc_ref[...] = jnp.zeros_like(acc_ref)
        run_ref[...] = jnp.zeros_like(run_ref)
        qqs = [_stack_heads(q_ref[:, st * LANES:(st + 1) * LANES] * scale, lo_q) for st in range(ns)]

        def units(todo):
            def rows(j):
                return pl.ds(pl.multiple_of(j * BLOCK, BLOCK), BLOCK)

            zs = [lax.dot_general(qqs[st], k_ref[rows(j), st * LANES:(st + 1) * LANES], NT, preferred_element_type=F32)
                  for st, j, _ in todo]
            logs = []
            for z, (_, _, off) in zip(zs, todo):
                lsig = jnp.minimum(z, 0.0) - jnp.log(1.0 + jnp.exp(-jnp.abs(z)))
                lneg = lsig - z
                if off is not None:
                    lneg = jnp.where(ahead > off, lneg, 0.0)
                logs.append((lsig, _split(lneg)))
            sums = [lax.dot_general(cat, mat, NN, preferred_element_type=F32) for _, cat in logs]
            probs = []
            for (lsig, _), sm, (st, _, off) in zip(logs, sums, todo):
                run = run_ref[st]
                a = jnp.exp(lsig + run + sm[:, :BLOCK])
                if off is not None:
                    a = jnp.where(ahead > off, a, 0.0)
                run_ref[st] = run + sm[:, BLOCK:]
                probs.append(a.astype(BF16))
            for ab, (st, j, _) in zip(probs, todo):
                v = v_ref[rows(j), st * LANES:(st + 1) * LANES]
                vz = jnp.zeros_like(v)
                acc_ref[st] += (lax.dot_general(ab[:SB_Q], jnp.where(lo_k, v, vz), NN, preferred_element_type=F32)
                                + lax.dot_general(ab[SB_Q:], jnp.where(lo_k, vz, v), NN, preferred_element_type=F32))

        units([(st, 2 * i + 1, BLOCK) for st in range(ns)] + [(st, 2 * i, 0) for st in range(ns)])

        def pair(p, carry):
            jp = i - 1 - p
            units([(st, 2 * jp + 1, None) for st in range(ns)] + [(st, 2 * jp, None) for st in range(ns)])
            return carry

        lax.fori_loop(0, i, pair, 0)
        for st in range(ns):
            cols = slice(st * LANES, (st + 1) * LANES)
            o_ref[:, cols] = acc_ref[st].astype(BF16)
            tot_ref[:, cols] = jnp.where(lo_q, run_ref[st, 0:SB_Q, :], run_ref[st, SB_Q:2 * SB_Q, :])

    def seq(col0):
        return pl.BlockSpec((s, wide), lambda b, h, i: (b, col0 // ns + h))

    blk = pl.BlockSpec((SB_Q, wide), lambda b, h, i: (b * nq + i, h))
    return _call(
        body, name="attn_b_fwd", grid=(batch, n_pairs // ns, nq),
        in_specs=[pl.BlockSpec((SB_Q, wide), lambda b, h, i: (b * nq + i, q_col0 // ns + h)), seq(k_col0), seq(v_col0)],
        out_specs=[blk, blk],
        out_shape=[jax.ShapeDtypeStruct((t, n_pairs * LANES), BF16), jax.ShapeDtypeStruct((t, n_pairs * LANES), F32)],
        scratch=[pltpu.VMEM((ns, SB_Q, LANES), F32), pltpu.VMEM((ns, 2 * SB_Q, LANES), F32)],
        sem=("parallel", "parallel", "arbitrary"), args=(proj, proj, proj), ride=ride)


def _sb_bwd(proj, q_col0, k_col0, v_col0, tot, do, batch, s, ride=None, streams=BWD_STREAMS):
    t = proj.shape[0]
    nq = s // SB_Q
    n_pairs = 4
    ns = streams
    wide = ns * LANES
    scale = HEAD_DIM ** -0.5

    def body(q_ref, k_ref, v_ref, tot_ref, do_ref, dq_ref, dk_ref, dv_ref, dk_acc, dv_acc, dq_acc, seen_ref, gsum_ref):
        i = pl.program_id(2)
        lo_q = lax.broadcasted_iota(jnp.int32, (SB_Q, LANES), 1) < HEAD_DIM
        lo_k = _lane_lo()

        @pl.when(i == 0)
        def _():
            dk_acc[...] = jnp.zeros_like(dk_acc)
            dv_acc[...] = jnp.zeros_like(dv_acc)

        mat_after = _sb_consts(True)
        mat_before = _sb_consts(False)
        row = lax.broadcasted_iota(jnp.int32, (2 * SB_Q, LANES), 0) % SB_Q
        ahead = row - lax.broadcasted_iota(jnp.int32, (2 * SB_Q, LANES), 1)
        dq_acc[...] = jnp.zeros_like(dq_acc)
        seen_ref[...] = jnp.zeros_like(seen_ref)
        gsum_ref[...] = jnp.zeros_like(gsum_ref)
        qqs, dds, totals = [], [], []
        for st in range(ns):
            cols = slice(st * LANES, (st + 1) * LANES)
            qqs.append(_stack_heads(q_ref[:, cols] * scale, lo_q))
            dds.append(_stack_heads(do_ref[:, cols], lo_q))
            tot_t = tot_ref[:, cols]
            totals.append(jnp.concatenate([jnp.broadcast_to(tot_t[:, 0:1], (SB_Q, LANES)),
                                           jnp.broadcast_to(tot_t[:, HEAD_DIM:HEAD_DIM + 1], (SB_Q, LANES))], axis=0))

        def units(todo):
            def rows(j):
                return pl.ds(pl.multiple_of(j * BLOCK, BLOCK), BLOCK)

            def cols(st):
                return slice(st * LANES, (st + 1) * LANES)

            prods = [(lax.dot_general(qqs[st], k_ref[rows(j), cols(st)], NT, preferred_element_type=F32),
                      lax.dot_general(dds[st], v_ref[rows(j), cols(st)], NT, preferred_element_type=F32))
                     for st, j, _ in todo]
            logs = []
            for (z, _), (_, _, off) in zip(prods, todo):
                lsig = jnp.minimum(z, 0.0) - jnp.log(1.0 + jnp.exp(-jnp.abs(z)))
                lneg = lsig - z
                if off is not None:
                    lneg = jnp.where(ahead > off, lneg, 0.0)
                logs.append((lsig, _split(lneg)))
            sums = [lax.dot_general(cat, mat_after, NN, preferred_element_type=F32) for _, cat in logs]
            gates = []
            for (lsig, _), sm, (_, da), (st, _, off) in zip(logs, sums, prods, todo):
                seen = seen_ref[st]
                a = jnp.exp(lsig + (totals[st] - seen - sm[:, BLOCK:]) + sm[:, :BLOCK])
                if off is not None:
                    a = jnp.where(ahead > off, a, 0.0)
                seen_ref[st] = seen + sm[:, BLOCK:]
                g = a * da
                gates.append((a.astype(BF16), g, _split(g)))
            gsums = [lax.dot_general(cat, mat_before, NN, preferred_element_type=F32) for _, _, cat in gates]
            outs = []
            for (lsig, _), (ab, g, _), gs, (st, _, off) in zip(logs, gates, gsums, todo):
                gsum = gsum_ref[st]
                dz = g - jnp.exp(lsig) * (g + gsum + gs[:, :BLOCK])
                if off is not None:
                    dz = jnp.where(ahead > off, dz, 0.0)
                gsum_ref[st] = gsum + gs[:, BLOCK:]
                outs.append((dz.astype(BF16), ab))
            for (dzb, ab), (st, j, _) in zip(outs, todo):
                k = k_ref[rows(j), cols(st)]
                kz = jnp.zeros_like(k)
                dq_acc[st] += (lax.dot_general(dzb[:SB_Q], jnp.where(lo_k, k, kz), NN, preferred_element_type=F32)
                               + lax.dot_general(dzb[SB_Q:], jnp.where(lo_k, kz, k), NN, preferred_element_type=F32))
                dk_acc[rows(j), cols(st)] += lax.dot_general(dzb, qqs[st], TN, preferred_element_type=F32)
                dv_acc[rows(j), cols(st)] += lax.dot_general(ab, dds[st], TN, preferred_element_type=F32)

        def pair(p, carry):
            units([(st, 2 * p, None) for st in range(ns)] + [(st, 2 * p + 1, None) for st in range(ns)])
            return carry

        lax.fori_loop(0, i, pair, 0)
        units([(st, 2 * i, 0) for st in range(ns)] + [(st, 2 * i + 1, BLOCK) for st in range(ns)])
        for st in range(ns):
            dq_ref[:, st * LANES:(st + 1) * LANES] = (dq_acc[st] * scale).astype(BF16)

        @pl.when(i == nq - 1)
        def _():
            dk_ref[...] = dk_acc[...].astype(BF16)
            dv_ref[...] = dv_acc[...].astype(BF16)

    def seq_in(col0):
        return pl.BlockSpec((s, wide), lambda b, h, i: (b, col0 // ns + h))

    blk = pl.BlockSpec((SB_Q, wide), lambda b, h, i: (b * nq + i, h))
    seq = pl.BlockSpec((s, wide), lambda b, h, i: (b, h))
    out = jax.ShapeDtypeStruct((t, n_pairs * LANES), BF16)
    return _call(
        body, name="attn_b_bwd", grid=(batch, n_pairs // ns, nq),
        in_specs=[pl.BlockSpec((SB_Q, wide), lambda b, h, i: (b * nq + i, q_col0 // ns + h)), seq_in(k_col0),
                  seq_in(v_col0), blk, blk],
        out_specs=[blk, seq, seq], out_shape=[out, out, out],
        scratch=[pltpu.VMEM((s, wide), F32), pltpu.VMEM((s, wide), F32), pltpu.VMEM((ns, SB_Q, LANES), F32),
                 pltpu.VMEM((ns, 2 * SB_Q, LANES), F32), pltpu.VMEM((ns, 2 * SB_Q, LANES), F32)],
        sem=("parallel", "parallel", "arbitrary"), args=(proj, proj, proj, tot, do), ride=ride)


MEM_Q_TILE = 512


def _mem_fwd(q, kv, batch, s, n_mem):
    t, width = q.shape
    tq = min(MEM_Q_TILE, s)
    nq = s // tq
    scale = MEM_HEAD_DIM ** -0.5

    def body(q_ref, kv_ref, o_ref):
        for h in range(N_HEADS_MEM):
            cols = slice(h * MEM_HEAD_DIM, (h + 1) * MEM_HEAD_DIM)
            k = kv_ref[:, cols]
            v = kv_ref[:, width + h * MEM_HEAD_DIM: width + (h + 1) * MEM_HEAD_DIM]
            sc = lax.dot_general(q_ref[:, cols], k, NT, preferred_element_type=F32) * scale
            p = jnp.exp(sc - jnp.max(sc, axis=1, keepdims=True))
            p = p / jnp.sum(p, axis=1, keepdims=True)
            o_ref[:, cols] = lax.dot_general(p.astype(BF16), v, NN, preferred_element_type=F32).astype(BF16)

    return pl.pallas_call(
        body, name="mem_attn_fwd", grid=(batch, nq),
        in_specs=[pl.BlockSpec((tq, width), lambda b, i: (b * nq + i, 0)),
                  pl.BlockSpec((n_mem, 2 * width), lambda b, i: (b, 0))],
        out_specs=pl.BlockSpec((tq, width), lambda b, i: (b * nq + i, 0)),
        out_shape=jax.ShapeDtypeStruct((t, width), BF16),
        compiler_params=_params(("parallel", "parallel")),
    )(q, kv)


def _mem_bwd(q, kv, do, batch, s, n_mem):
    t, width = q.shape
    tq = min(MEM_Q_TILE, s)
    nq = s // tq
    scale = MEM_HEAD_DIM ** -0.5

    def body(q_ref, kv_ref, do_ref, dq_ref, dkv_ref, acc):
        i = pl.program_id(1)

        @pl.when(i == 0)
        def _():
            acc[...] = jnp.zeros_like(acc)

        for h in range(N_HEADS_MEM):
            cols = slice(h * MEM_HEAD_DIM, (h + 1) * MEM_HEAD_DIM)
            vcols = slice(width + h * MEM_HEAD_DIM, width + (h + 1) * MEM_HEAD_DIM)
            qh, k, v, doh = q_ref[:, cols], kv_ref[:, cols], kv_ref[:, vcols], do_ref[:, cols]
            sc = lax.dot_general(qh, k, NT, preferred_element_type=F32) * scale
            p = jnp.exp(sc - jnp.max(sc, axis=1, keepdims=True))
            p = p / jnp.sum(p, axis=1, keepdims=True)
            dp = lax.dot_general(doh, v, NT, preferred_element_type=F32)
            ds = (p * (dp - jnp.sum(p * dp, axis=1, keepdims=True)) * scale).astype(BF16)
            dq_ref[:, cols] = lax.dot_general(ds, k, NN, preferred_element_type=F32).astype(BF16)
            acc[:, cols] += lax.dot_general(ds, qh, TN, preferred_element_type=F32)
            acc[:, vcols] += lax.dot_general(p.astype(BF16), doh, TN, preferred_element_type=F32)

        @pl.when(i == nq - 1)
        def _():
            dkv_ref[...] = acc[...].astype(BF16)

    row = pl.BlockSpec((tq, width), lambda b, i: (b * nq + i, 0))
    kvs = pl.BlockSpec((n_mem, 2 * width), lambda b, i: (b, 0))
    return pl.pallas_call(
        body, name="mem_attn_bwd", grid=(batch, nq),
        in_specs=[row, kvs, row], out_specs=[row, kvs],
        out_shape=[jax.ShapeDtypeStruct((t, width), BF16), jax.ShapeDtypeStruct((batch * n_mem, 2 * width), BF16)],
        scratch_shapes=[pltpu.VMEM((n_mem, 2 * width), F32)],
        compiler_params=_params(("parallel", "arbitrary")),
    )(q, kv, do)


def _mixer_fwd(o_a, o_b, w_a, w_b, proj, gate_col0):
    t, width = o_a.shape
    d = w_a.shape[1]
    tm = min(ROW_TILE, t)
    gb0 = gate_col0 * LANES // d

    def body(oa_ref, ob_ref, wa_ref, wb_ref, ga_ref, gb_ref, ua_ref, ub_ref, mix_ref):
        ua = lax.dot_general(oa_ref[...], wa_ref[...], NN, preferred_element_type=F32)
        ub = lax.dot_general(ob_ref[...], wb_ref[...], NN, preferred_element_type=F32)
        ua_ref[...] = ua.astype(BF16)
        ub_ref[...] = ub.astype(BF16)
        mix_ref[...] = (jax.nn.sigmoid(ga_ref[...].astype(F32)) * ua
                        + jax.nn.sigmoid(gb_ref[...].astype(F32)) * ub).astype(BF16)

    row = pl.BlockSpec((tm, width), lambda i: (i, 0))
    wsp = pl.BlockSpec((width, d), lambda i: (0, 0))
    out = pl.BlockSpec((tm, d), lambda i: (i, 0))
    osh = jax.ShapeDtypeStruct((t, d), BF16)
    return pl.pallas_call(
        body, name="mixer_fwd", grid=(t // tm,),
        in_specs=[row, row, wsp, wsp,
                  pl.BlockSpec((tm, d), lambda i: (i, gb0)), pl.BlockSpec((tm, d), lambda i: (i, gb0 + 1))],
        out_specs=[out, out, out], out_shape=[osh, osh, osh],
        compiler_params=_params(("parallel",)),
    )(o_a, o_b, w_a, w_b, proj, proj)


def _mixer_bwd(dmix, ua, ub, proj, gate_col0):
    t, d = dmix.shape
    tm = min(ROW_TILE, t)
    nc = d // LANES

    def body(dm_ref, ua_ref, ub_ref, ga_ref, gb_ref, dua_ref, dub_ref, dg_ref):
        dm = dm_ref[...].astype(F32)
        sa = jax.nn.sigmoid(ga_ref[...].astype(F32))
        sb = jax.nn.sigmoid(gb_ref[...].astype(F32))
        dua_ref[...] = (dm * sa).astype(BF16)
        dub_ref[...] = (dm * sb).astype(BF16)
        dg_ref[:, 0:d] = (dm * ua_ref[...].astype(F32) * sa * (1.0 - sa)).astype(BF16)
        dg_ref[:, d:2 * d] = (dm * ub_ref[...].astype(F32) * sb * (1.0 - sb)).astype(BF16)

    row = pl.BlockSpec((tm, d), lambda i: (i, 0))
    return pl.pallas_call(
        body, name="mixer_bwd", grid=(t // tm,),
        in_specs=[row, row, row,
                  pl.BlockSpec((tm, d), lambda i: (i, gate_col0 // nc)),
                  pl.BlockSpec((tm, d), lambda i: (i, gate_col0 // nc + 1))],
        out_specs=[row, row, pl.BlockSpec((tm, 2 * d), lambda i: (i, 0))],
        out_shape=[jax.ShapeDtypeStruct((t, d), BF16), jax.ShapeDtypeStruct((t, d), BF16),
                   jax.ShapeDtypeStruct((t, 2 * d), BF16)],
        compiler_params=_params(("parallel",)),
    )(dmix, ua, ub, proj, proj)


FFN_COLS = 1024


def _ffn_up(n, w_gate, w_up):
    t, d = n.shape
    hidden = w_gate.shape[0]
    tm = min(ROW_TILE, t)
    tn = min(FFN_COLS, hidden)

    def body(n_ref, wg_ref, wu_ref, hg_ref, hu_ref, act_ref):
        hg = lax.dot_general(n_ref[...], wg_ref[...], NT, preferred_element_type=F32)
        hu = lax.dot_general(n_ref[...], wu_ref[...], NT, preferred_element_type=F32)
        hg_ref[...] = hg.astype(BF16)
        hu_ref[...] = hu.astype(BF16)
        act_ref[...] = (hg * jax.nn.sigmoid(hg) * hu).astype(BF16)

    wsp = pl.BlockSpec((tn, d), lambda j, i: (j, 0))
    out = pl.BlockSpec((tm, tn), lambda j, i: (i, j))
    osh = jax.ShapeDtypeStruct((t, hidden), BF16)
    return pl.pallas_call(
        body, name="ffn_up", grid=(hidden // tn, t // tm),
        in_specs=[pl.BlockSpec((tm, d), lambda j, i: (i, 0)), wsp, wsp],
        out_specs=[out, out, out], out_shape=[osh, osh, osh],
        compiler_params=_params(("parallel", "parallel")),
    )(n, w_gate, w_up)


def _ffn_bwd_act(dh, w_down, hg, hu):
    t, d = dh.shape
    hidden = w_down.shape[0]
    tm = min(ROW_TILE, t)
    tn = min(FFN_COLS, hidden)

    def body(dh_ref, wd_ref, hg_ref, hu_ref, dhg_ref, dhu_ref):
        dact = lax.dot_general(dh_ref[...], wd_ref[...], NT, preferred_element_type=F32)
        hg = hg_ref[...].astype(F32)
        sg = jax.nn.sigmoid(hg)
        dhu_ref[...] = (dact * hg * sg).astype(BF16)
        dhg_ref[...] = (dact * hu_ref[...].astype(F32) * sg * (1.0 + hg * (1.0 - sg))).astype(BF16)

    hid = pl.BlockSpec((tm, tn), lambda j, i: (i, j))
    osh = jax.ShapeDtypeStruct((t, hidden), BF16)
    return pl.pallas_call(
        body, name="ffn_bwd_act", grid=(hidden // tn, t // tm),
        in_specs=[pl.BlockSpec((tm, d), lambda j, i: (i, 0)), pl.BlockSpec((tn, d), lambda j, i: (j, 0)), hid, hid],
        out_specs=[hid, hid], out_shape=[osh, osh],
        compiler_params=_params(("parallel", "parallel")),
    )(dh, w_down, hg, hu)


MM_ROWS = 1024


def _mm_w(name, a, w, out_dtype, dims=NN, res=None, tm=MM_ROWS, tn=1024):
    t, k = a.shape
    n = w.shape[1] if dims == NN else w.shape[0]
    tm, tn = min(tm, t), min(tn, n)
    o_spec = pl.BlockSpec((tm, tn), lambda j, i: (i, j))
    b_spec = pl.BlockSpec((k, tn), lambda j, i: (0, j)) if dims == NN else pl.BlockSpec((tn, k), lambda j, i: (j, 0))
    return _mm(name, a, w, grid=(n // tn, t // tm), a_spec=pl.BlockSpec((tm, k), lambda j, i: (i, 0)), b_spec=b_spec,
               o_shape=(t, n), o_spec=o_spec, dims=dims, out_dtype=out_dtype, res=res,
               res_spec=o_spec if res is not None else None)


def _wgrad(name, a, g, tk=1024, tn=1024):
    t, k = a.shape
    n = g.shape[1]
    tm, tk, tn = min(2 * MM_ROWS, t), min(tk, k), min(tn, n)
    return _mm(name, a, g, grid=(k // tk, n // tn, t // tm),
               a_spec=pl.BlockSpec((tm, tk), lambda p, q, r: (r, p)), b_spec=pl.BlockSpec((tm, tn), lambda p, q, r: (r, q)),
               o_shape=(k, n), o_spec=pl.BlockSpec((tk, tn), lambda p, q, r: (p, q)), dims=TN, out_dtype=BF16, nk=t // tm)


def _peers():
    x, y, c = lax.axis_index("x"), lax.axis_index("y"), lax.axis_index("c")
    me = 4 * x + 2 * y + c
    out = []
    for k in range(1, N_DEV):
        kx, ky, kc = (k >> 2) & 1, (k >> 1) & 1, k & 1
        px = 1 - x if kx else x
        py = 1 - y if ky else y
        pc = 1 - c if kc else c
        out.append(((px, py, pc), 4 * px + 2 * py + pc))
    return me, out


def _cast_weights(ws, pad_rows):
    def body(*refs):
        n = len(refs) // 2
        for i_ref, o_ref, pr in zip(refs[:n], refs[n:], pad_rows):
            r, c = i_ref.shape
            o_ref[0:r, :] = i_ref[...].astype(BF16)
            if pr:
                o_ref[r:r + pr, :] = jnp.zeros((pr, c), BF16)

    return pl.pallas_call(
        body, name="cast_weights", in_specs=[VMEM] * len(ws), out_specs=[VMEM] * len(ws),
        out_shape=[jax.ShapeDtypeStruct((w.shape[0] + pr, w.shape[1]), BF16) for w, pr in zip(ws, pad_rows)],
    )(*ws)


def _window(ref, j, c):
    return ref.at[:, pl.ds(pl.multiple_of(j * c, LANES), c)]


def _scatter_copies(ins, outs, sems, cols, landed):
    send_sems, recv_sems, loc_sems = sems
    n_peer = N_DEV - 1
    me, peers = _peers()

    def src(w, j):
        return _window(ins[w], j, cols[w]) if cols[w] else ins[w].at[j]

    local = [pltpu.make_async_copy(src(w, me), outs[w].at[me], loc_sems.at[w]) for w in range(len(ins))]
    remote = [pltpu.make_async_remote_copy(
        src_ref=src(w, idx), dst_ref=outs[w].at[idx if landed else me],
        send_sem=send_sems.at[w * n_peer + k], recv_sem=recv_sems.at[w * n_peer + k],
        device_id=dev, device_id_type=pl.DeviceIdType.MESH)
        for k, (dev, idx) in reversed(list(enumerate(peers))) for w in range(len(ins))]
    return local, remote


OTHER_CHIPS = (2, 4, 6)


def _gather_copies(ins, outs, sems, cols):
    send_sems, recv_sems, loc_sems = sems
    x, y, c = lax.axis_index("x"), lax.axis_index("y"), lax.axis_index("c")
    me = 4 * x + 2 * y + c
    n_pair = N_DEV - 1

    def dev(mask):
        return (1 - x if mask & 4 else x, 1 - y if mask & 2 else y, 1 - c if mask & 1 else c)

    def slot(w, mask):
        j = jnp.bitwise_xor(me, mask)
        return _window(outs[w], j, cols[w]) if cols[w] else outs[w].at[j]

    def remote(w, pair, src, to_slot, target):
        return pltpu.make_async_remote_copy(src_ref=src, dst_ref=slot(w, to_slot), send_sem=send_sems.at[w * n_pair + pair],
                                            recv_sem=recv_sems.at[w * n_pair + pair], device_id=dev(target),
                                            device_id_type=pl.DeviceIdType.MESH)

    ws = range(len(ins))
    return dict(
        local=[pltpu.make_async_copy(ins[w], slot(w, 0), loc_sems.at[w]) for w in ws],
        to_chips=[remote(w, 1 + t, ins[w], 0, m) for t, m in enumerate(OTHER_CHIPS) for w in ws],
        to_core=[remote(w, 0, ins[w], 0, 1) for w in ws],
        from_chips=[remote(w, 1 + t, ins[w], m, 0) for t, m in enumerate(OTHER_CHIPS) for w in ws],
        pass_on=[remote(w, 4 + t, slot(w, m), m, 1) for t, m in enumerate(OTHER_CHIPS) for w in ws],
        from_core=[remote(w, 0, ins[w], 1, 0) for w in ws]
        + [remote(w, 4 + t, ins[w], m + 1, 0) for t, m in enumerate(OTHER_CHIPS) for w in ws])


def _exchange_start(ins, outs, sems, gather, cols):
    if gather:
        cps = _gather_copies(ins, outs, sems, cols)
        for cp in cps["local"] + cps["to_chips"] + cps["to_core"]:
            cp.start()
    else:
        local, remote = _scatter_copies(ins, outs, sems, cols, False)
        for cp in local + remote:
            cp.start()


def _exchange_pass_on(ins, outs, sems, gather, cols, chips):
    if gather:
        cps = _gather_copies(ins, outs, sems, cols)
        n = len(ins)
        for t in chips:
            for arrived, onward in zip(cps["from_chips"][t * n:(t + 1) * n], cps["pass_on"][t * n:(t + 1) * n]):
                arrived.wait_recv()
                onward.start()


def _exchange_wait(ins, outs, sems, gather, cols):
    if gather:
        cps = _gather_copies(ins, outs, sems, cols)
        for cp in cps["local"]:
            cp.wait()
        for cp in cps["to_chips"] + cps["to_core"] + cps["pass_on"]:
            cp.wait_send()
        for cp in cps["from_core"]:
            cp.wait_recv()
    else:
        local, remote = _scatter_copies(ins, outs, sems, cols, True)
        for cp in local:
            cp.wait()
        for cp in remote:
            cp.wait_send()
            cp.wait_recv()


def _exchange_shapes(arrs, gather, cols):
    n = len(arrs)
    out_shape = []
    for a, c in zip(arrs, cols):
        if gather:
            shape = (a.shape[0], N_DEV * c) if c else (N_DEV,) + a.shape
        else:
            shape = (N_DEV, a.shape[0], c) if c else a.shape
        out_shape.append(jax.ShapeDtypeStruct(shape, a.dtype))
    sems = [pltpu.SemaphoreType.DMA((n * (N_DEV - 1),)), pltpu.SemaphoreType.DMA((n * (N_DEV - 1),)),
            pltpu.SemaphoreType.DMA((n,))]
    return out_shape, sems


def _call(body, *, name, grid, in_specs, out_specs, out_shape, scratch, sem, args, ride=None):
    if ride is None:
        outs = pl.pallas_call(body, name=name, grid=grid, in_specs=in_specs, out_specs=out_specs, out_shape=out_shape,
                              scratch_shapes=scratch, compiler_params=_params(sem))(*args)
        return outs, None
    arrs, gather, cols = ride
    n, n_in, n_out, n_scr = len(arrs), len(in_specs), len(out_specs), len(scratch)
    x_shape, x_sems = _exchange_shapes(arrs, gather, cols)

    def riding(*refs):
        ins, x_ins = refs[:n_in], refs[n_in:n_in + n]
        outs = refs[n_in + n:n_in + n + n_out]
        x_outs = refs[n_in + n + n_out:n_in + 2 * n + n_out]
        scr = refs[n_in + 2 * n + n_out:n_in + 2 * n + n_out + n_scr]
        sems = refs[n_in + 2 * n + n_out + n_scr:]
        def at(step):
            return functools.reduce(jnp.logical_and, [pl.program_id(a) == v for a, v in enumerate(step)])

        @pl.when(at((0,) * len(grid)))
        def _():
            _exchange_start(x_ins, x_outs, sems, gather, cols)

        @pl.when(at((grid[0] // 2,) + (0,) * (len(grid) - 1)))
        def _():
            _exchange_pass_on(x_ins, x_outs, sems, gather, cols, (0, 1))

        @pl.when(at((grid[0] // 2,) + (0,) * (len(grid) - 2) + (5 * grid[-1] // 8,)))
        def _():
            _exchange_pass_on(x_ins, x_outs, sems, gather, cols, (2,))

        body(*ins, *outs, *scr)

        @pl.when(at(tuple(g - 1 for g in grid)))
        def _():
            _exchange_wait(x_ins, x_outs, sems, gather, cols)

    res = pl.pallas_call(
        riding, name=name, grid=grid, in_specs=list(in_specs) + [ANY] * n, out_specs=list(out_specs) + [ANY] * n,
        out_shape=list(out_shape) + x_shape, scratch_shapes=list(scratch) + x_sems,
        compiler_params=_params(("arbitrary",) * len(grid)))(*args, *arrs)
    return res[:n_out], res[n_out:]


def _my_block():
    return (4 * lax.axis_index("x") + 2 * lax.axis_index("y") + lax.axis_index("c")).astype(jnp.int32).reshape(1)


def _proj_in_gather(n, w_shard):
    t, k = n.shape
    cs = w_shard.shape[1]
    tm = min(MM_ROWS, t)
    ni = t // tm
    arrival = (0, 1) + OTHER_CHIPS + tuple(m + 1 for m in OTHER_CHIPS)

    def mask_at(s):
        return jnp.where(s < 2, s, jnp.where(s < 5, 2 * (s - 1), 2 * (s - 4) + 1))

    def body(me_ref, n_ref, w_hbm, o_ref, all_hbm, w_vmem, send_sems, recv_sems, loc_sems, load_sems):
        s, i = pl.program_id(0), pl.program_id(1)
        cps = _gather_copies([w_hbm], [all_hbm], (send_sems, recv_sems, loc_sems), (cs,))
        arrived = cps["local"] + cps["from_core"][:1] + cps["from_chips"] + cps["from_core"][1:]

        def load(step):
            src = w_hbm if step == 0 else _window(all_hbm, jnp.bitwise_xor(me_ref[0], arrival[step]), cs)
            return pltpu.make_async_copy(src, w_vmem.at[step % 2], load_sems.at[step % 2])

        @pl.when(jnp.logical_and(s == 0, i == 0))
        def _():
            for cp in cps["local"] + cps["to_chips"] + cps["to_core"]:
                cp.start()
            load(0).start()

        for step, mask in enumerate(arrival):
            @pl.when(jnp.logical_and(s == step, i == 0))
            def _(step=step):
                load(step).wait()

            if step + 1 < N_DEV:
                @pl.when(jnp.logical_and(s == step, i == min(1, ni - 1)))
                def _(step=step):
                    arrived[step + 1].wait_recv()
                    if arrival[step + 1] in OTHER_CHIPS:
                        cps["pass_on"][OTHER_CHIPS.index(arrival[step + 1])].start()
                    load(step + 1).start()

        o_ref[...] = lax.dot_general(n_ref[...], w_vmem[s % 2], NN, preferred_element_type=F32).astype(BF16)

        @pl.when(jnp.logical_and(s == N_DEV - 1, i == ni - 1))
        def _():
            cps["local"][0].wait()
            for cp in cps["to_chips"] + cps["to_core"] + cps["pass_on"]:
                cp.wait_send()

    return pl.pallas_call(
        body, name="proj_in",
        grid_spec=pltpu.PrefetchScalarGridSpec(
            num_scalar_prefetch=1, grid=(N_DEV, ni),
            in_specs=[pl.BlockSpec((tm, k), lambda s, i, me: (i, 0)), ANY],
            out_specs=[pl.BlockSpec((tm, cs), lambda s, i, me: (i, jnp.bitwise_xor(me[0], mask_at(s)))), ANY],
            scratch_shapes=[pltpu.VMEM((2, k, cs), BF16), pltpu.SemaphoreType.DMA((N_DEV - 1,)),
                            pltpu.SemaphoreType.DMA((N_DEV - 1,)), pltpu.SemaphoreType.DMA((1,)),
                            pltpu.SemaphoreType.DMA((2,))]),
        out_shape=[jax.ShapeDtypeStruct((t, N_DEV * cs), BF16), jax.ShapeDtypeStruct((k, N_DEV * cs), BF16)],
        compiler_params=_params(("arbitrary", "arbitrary")),
    )(_my_block(), n, w_shard)


def _gw_in_scatter(a, g):
    t, k = a.shape
    cs = g.shape[1] // N_DEV
    tm = min(MM_ROWS, t)
    nr = t // tm
    n_chip = N_DEV // 2
    chips = (6, 4, 2, 0)

    def body(me_ref, a_ref, g_ref, out_hbm, acc, stage, other, core_send, core_recv, chip_send, chip_recv, loc_sem):
        s, r = pl.program_id(0), pl.program_id(1)
        x, y, c = lax.axis_index("x"), lax.axis_index("y"), lax.axis_index("c")
        my_chip = 2 * x + y
        part = lax.dot_general(a_ref[...], g_ref[...], TN, preferred_element_type=F32)

        def to_core(m):
            return pltpu.make_async_remote_copy(src_ref=stage.at[0], dst_ref=other.at[m], send_sem=core_send.at[m],
                                                recv_sem=core_recv.at[m], device_id=(x, y, 1 - c),
                                                device_id_type=pl.DeviceIdType.MESH)

        def to_chip(m, landed):
            mask = chips[m]
            there = (1 - x if mask & 4 else x, 1 - y if mask & 2 else y, c)
            slot = (2 * there[0] + there[1]) if landed else my_chip
            return pltpu.make_async_remote_copy(src_ref=stage.at[1], dst_ref=out_hbm.at[slot], send_sem=chip_send.at[m],
                                                recv_sem=chip_recv.at[m], device_id=there,
                                                device_id_type=pl.DeviceIdType.MESH)

        local = pltpu.make_async_copy(stage.at[1], out_hbm.at[my_chip], loc_sem)

        @pl.when(r == 0)
        def _():
            acc[...] = part

        @pl.when(r > 0)
        def _():
            acc[...] += part

        for step in range(N_DEV):
            m = step // 2

            @pl.when(jnp.logical_and(s == step, r == nr - 1))
            def _(step=step, m=m):
                if step % 2 == 0:
                    if m > 0:
                        to_core(m - 1).wait_send()
                    stage[0] = acc[...].astype(BF16)
                    to_core(m).start()
                else:
                    if m > 0:
                        to_chip(m - 1, False).wait_send()
                    to_core(m).wait_recv()
                    stage[1] = (acc[...] + other[m].astype(F32)).astype(BF16)
                    if m < n_chip - 1:
                        to_chip(m, False).start()
                    else:
                        local.start()
                        to_core(m).wait_send()
                        local.wait()
                        for mm in range(n_chip - 1):
                            to_chip(mm, True).wait_recv()

    return pl.pallas_call(
        body, name="gw_in",
        grid_spec=pltpu.PrefetchScalarGridSpec(
            num_scalar_prefetch=1, grid=(N_DEV, nr),
            in_specs=[pl.BlockSpec((tm, k), lambda s, r, me: (r, 0)),
                      pl.BlockSpec((tm, cs), lambda s, r, me: (r, jnp.bitwise_xor(me[0], N_DEV - 1 - s)))],
            out_specs=ANY,
            scratch_shapes=[pltpu.VMEM((k, cs), F32), pltpu.VMEM((2, k, cs), BF16), pltpu.VMEM((n_chip, k, cs), BF16),
                            pltpu.SemaphoreType.DMA((n_chip,)), pltpu.SemaphoreType.DMA((n_chip,)),
                            pltpu.SemaphoreType.DMA((n_chip - 1,)), pltpu.SemaphoreType.DMA((n_chip - 1,)),
                            pltpu.SemaphoreType.DMA]),
        out_shape=jax.ShapeDtypeStruct((n_chip, k, cs), BF16),
        compiler_params=_params(("arbitrary", "arbitrary")),
    )(_my_block(), a, g)


SMALL_ROWS = 8


def _allreduce_small(parts, loss_part):
    n, d = len(parts), parts[0].shape[1]

    def body(*refs):
        part_refs, loss_ref, o_ref = refs[:n], refs[n], refs[n + 1]
        mine_ref, all_ref, send_sems, recv_sems = refs[n + 2:]
        me, peers = _peers()
        mine_ref[...] = jnp.zeros_like(mine_ref)
        for i, p_ref in enumerate(part_refs):
            mine_ref[i:i + 1, :] = p_ref[...]
        mine_ref[SMALL_ROWS - 1:SMALL_ROWS, 0:LANES] = loss_ref[0:1, :]
        all_ref[me] = mine_ref[...]
        for k, (dev, idx) in enumerate(peers):
            pltpu.make_async_remote_copy(src_ref=mine_ref, dst_ref=all_ref.at[me], send_sem=send_sems.at[k],
                                         recv_sem=recv_sems.at[k], device_id=dev,
                                         device_id_type=pl.DeviceIdType.MESH).start()
        for k, (dev, idx) in enumerate(peers):
            cp = pltpu.make_async_remote_copy(src_ref=mine_ref, dst_ref=all_ref.at[idx], send_sem=send_sems.at[k],
                                              recv_sem=recv_sems.at[k], device_id=dev,
                                              device_id_type=pl.DeviceIdType.MESH)
            cp.wait_send()
            cp.wait_recv()
        tot = all_ref[0]
        for dvc in range(1, N_DEV):
            tot = tot + all_ref[dvc]
        o_ref[...] = tot

    return pl.pallas_call(
        body, name="allreduce_small", in_specs=[VMEM] * (n + 1), out_specs=VMEM,
        out_shape=jax.ShapeDtypeStruct((SMALL_ROWS, d), F32),
        scratch_shapes=[pltpu.VMEM((SMALL_ROWS, d), F32), pltpu.VMEM((N_DEV, SMALL_ROWS, d), F32),
                        pltpu.SemaphoreType.DMA((N_DEV - 1,)), pltpu.SemaphoreType.DMA((N_DEV - 1,))],
    )(*parts, loss_part)


def _adam_math(g, w, m, v):
    m_new = ADAM_B1 * m + (1.0 - ADAM_B1) * g
    v_new = ADAM_B2 * v + (1.0 - ADAM_B2) * (g * g)
    m_hat = m_new / (1.0 - ADAM_B1 ** ADAM_STEP)
    v_hat = v_new / (1.0 - ADAM_B2 ** ADAM_STEP)
    delta = -ADAM_LR * (m_hat / (jnp.sqrt(v_hat) + ADAM_EPS) + ADAM_WD * w)
    return delta, m_new, v_new


def _adam(name, pieces, w, m, v):
    r, c = w.shape
    n_piece, _, cp = pieces.shape
    tr = r
    for cand in (256, 176, 128, 64):
        if r % cand == 0 and r > cand:
            tr = cand
            break

    def body(p_ref, w_ref, m_ref, v_ref, g_ref, d_ref, mo_ref, vo_ref):
        g = p_ref[0, :, 0:c].astype(F32)
        for j in range(1, n_piece):
            g = g + p_ref[j, :, 0:c].astype(F32)
        delta, m_new, v_new = _adam_math(g, w_ref[...], m_ref[...], v_ref[...])
        g_ref[...] = g
        d_ref[...] = delta
        mo_ref[...] = m_new
        vo_ref[...] = v_new

    blk = pl.BlockSpec((tr, c), lambda i: (i, 0))
    osh = jax.ShapeDtypeStruct((r, c), F32)
    return pl.pallas_call(
        body, name=name, grid=(r // tr,),
        in_specs=[pl.BlockSpec((n_piece, tr, cp), lambda i: (0, i, 0)), blk, blk, blk],
        out_specs=[blk, blk, blk, blk], out_shape=[osh, osh, osh, osh],
        compiler_params=_params(("parallel",)),
    )(pieces, w, m, v)


def _adam_small(g_all, ws, ms, vs):
    n = len(ws)

    def body(*refs):
        g_ref, ins, outs = refs[0], refs[1:1 + 3 * n], refs[1 + 3 * n:]
        for i in range(n):
            g = g_ref[i:i + 1, :]
            delta, m_new, v_new = _adam_math(g, ins[i][...], ins[n + i][...], ins[2 * n + i][...])
            for kind, val in enumerate((g, delta, m_new, v_new)):
                outs[kind * n + i][...] = val

    osh = jax.ShapeDtypeStruct(ws[0].shape, F32)
    res = pl.pallas_call(body, name="adam_small", in_specs=[VMEM] * (1 + 3 * n), out_specs=[VMEM] * (4 * n),
                         out_shape=[osh] * (4 * n))(g_all, *ws, *ms, *vs)
    return res[:n], res[n:2 * n], res[2 * n:3 * n], res[3 * n:]


def _local_step(x, mem, pos, tgt, gains, w_in_shard, shards, batch):
    g_mix, g_mem_q, g_mem_kv, g_ffn, g_final = gains
    t, d = x.shape
    s = t // batch
    n_mem = mem.shape[0] // batch
    n_sh = N_DEV
    width = shards[0].shape[0]
    nb = width // LANES

    lane = np.arange(LANES) % HEAD_DIM
    sel_lo = (lane < ROPE_HALF).astype(np.float32)[None, :]
    sel_hi = ((lane >= ROPE_HALF) & (lane < 2 * ROPE_HALF)).astype(np.float32)[None, :]
    freqs = np.float32(ROPE_THETA) ** (-np.arange(ROPE_HALF, dtype=np.float32) / np.float32(ROPE_HALF))
    inv_freq = np.where(lane < 2 * ROPE_HALF, freqs[lane % ROPE_HALF], 0.0).astype(np.float32)[None, :]
    cos_t, sin_a, sin_b = _rope_tables(pos, jnp.asarray(inv_freq), jnp.asarray(sel_lo), jnp.asarray(sel_hi))
    bias = _dilated_bias_tiles(s)

    n1 = _rms_fwd("norm_mix", x, g_mix)
    proj, w_in = _proj_in_gather(n1, w_in_shard)
    qk_a = _rope_apply("rope_fwd", proj, 0, 2 * nb, cos_t, sin_a, sin_b, 1.0)
    cs_up = shards[0].shape[1]
    (o_a, lse_a), (w_up_a, w_up_b, w_out, w_q, w_kv, w_o, w_fd) = _da_fwd(
        qk_a, proj, 2 * nb, bias, batch, s,
        ride=(shards[:6] + shards[8:], True, (cs_up, cs_up, 0, 0, 0, cs_up, 0)))
    (o_b, tot_b), (w_fg, w_fu) = _sb_fwd(proj, 3 * nb, 4 * nb, 5 * nb, batch, s, ride=(shards[6:8], True, (0, 0)))
    w_out = w_out.reshape(d, d)
    w_q = w_q.reshape(d, -1)
    w_kv = w_kv.reshape(d, -1)
    w_fd = w_fd.reshape(-1, d)
    w_fg = w_fg.reshape(-1, d)
    w_fu = w_fu.reshape(-1, d)
    ua, ub, mixed = _mixer_fwd(o_a, o_b, w_up_a, w_up_b, proj, 6 * nb)
    h1 = _mm_w("mix_out", mixed, w_out, F32, res=x)
    n2 = _rms_fwd("norm_mem_q", h1, g_mem_q)
    mem_n = _rms_fwd("norm_mem_kv", mem, g_mem_kv)
    q_m = _mm_w("mem_q", n2, w_q, BF16)
    kv_m = _mm_w("mem_kv", mem_n, w_kv, BF16)
    o_m = _mem_fwd(q_m, kv_m, batch, s, n_mem)
    h2 = _mm_w("mem_out", o_m, w_o, F32, res=h1)
    n3 = _rms_fwd("norm_ffn", h2, g_ffn)
    hg, hu, act = _ffn_up(n3, w_fg, w_fu)
    h3 = _mm_w("ffn_down", act, w_fd, F32, res=h2, tm=ROW_TILE)
    loss_part, dh3, dh3_b, dg_final = _loss_head(h3, tgt, g_final.reshape(1, d))

    dhg, dhu = _ffn_bwd_act(dh3_b, w_fd, hg, hu)
    gw_fd = _wgrad("gw_ffn_down", act, dh3_b)
    gw_fg = _wgrad("gw_ffn_gate", dhg, n3)
    gw_fu = _wgrad("gw_ffn_up", dhu, n3)
    dn3 = _mm_w("dn_ffn_gate", dhg, w_fg, F32, tm=ROW_TILE)
    dn3 = _mm_w("dn_ffn_up", dhu, w_fu, F32, res=dn3, tm=ROW_TILE)
    dh2, dh2_b, dg_ffn = _rms_bwd("norm_ffn_bwd", dn3, h2, g_ffn, dh3, ("f32", "bf16"))

    do_m = _mm_w("mem_out_bwd", dh2_b, w_o, BF16, dims=NT)
    gw_o = _wgrad("gw_mem_o", o_m, dh2_b)
    dq_m, dkv_m = _mem_bwd(q_m, kv_m, do_m, batch, s, n_mem)
    gw_q = _wgrad("gw_mem_q", n2, dq_m)
    gw_kv = _wgrad("gw_mem_kv", mem_n, dkv_m)
    dn2 = _mm_w("mem_q_bwd", dq_m, w_q, F32, dims=NT)
    dmem_n = _mm_w("mem_kv_bwd", dkv_m, w_kv, F32, dims=NT)
    (dg_mem_kv,) = _rms_bwd("norm_mem_kv_bwd", dmem_n, mem, g_mem_kv, None, ())
    dh1, dh1_b, dg_mem_q = _rms_bwd("norm_mem_q_bwd", dn2, h1, g_mem_q, dh2, ("f32", "bf16"))

    dmix = _mm_w("mix_out_bwd", dh1_b, w_out, BF16, dims=NT)
    gw_out = _wgrad("gw_out", mixed, dh1_b)
    dua, dub, dgates = _mixer_bwd(dmix, ua, ub, proj, 6 * nb)
    do_a = _mm_w("up_a_bwd", dua, w_up_a, BF16, dims=NT)
    do_b = _mm_w("up_b_bwd", dub, w_up_b, BF16, dims=NT)
    gw_ua = _wgrad("gw_up_a", o_a, dua)
    gw_ub = _wgrad("gw_up_b", o_b, dub)
    (dq_ar, dk_ar, dv_a), (p_fg, p_fd) = _da_bwd(
        qk_a, proj, 2 * nb, bias, o_a, lse_a, do_a, batch, s,
        ride=([gw_fg.reshape(n_sh, -1, d), gw_fd.reshape(n_sh, -1, d)], False, (0, 0)))
    dqk_a = _rope_apply("rope_bwd", jnp.concatenate([dq_ar, dk_ar], axis=1), 0, 2 * nb, cos_t, sin_a, sin_b, -1.0)
    mid = [gw_ua, gw_ub, gw_out.reshape(n_sh, -1, d), gw_q.reshape(n_sh, -1, gw_q.shape[1]),
           gw_kv.reshape(n_sh, -1, gw_kv.shape[1]), gw_o, gw_fu.reshape(n_sh, -1, d)]
    (dq_b, dk_b, dv_b), (*p_mid, p_fu) = _sb_bwd(proj, 3 * nb, 4 * nb, 5 * nb, tot_b, do_b, batch, s,
                                                 ride=(mid, False, (cs_up, cs_up, 0, 0, 0, cs_up, 0)))
    p_ffn = [p_fg, p_fu, p_fd]
    dproj = jnp.concatenate([dqk_a, dv_a, dq_b, dk_b, dv_b, dgates], axis=1)
    dn1 = _mm_w("proj_in_bwd", dproj, w_in, F32, dims=NT, tm=ROW_TILE)
    p_in = _gw_in_scatter(n1, dproj)
    grad_x, dg_mix = _rms_bwd("norm_mix_bwd", dn1, x, g_mix, dh1, ("f32",))
    return loss_part, grad_x, [p_in] + list(p_mid) + p_ffn, (dg_mix, dg_mem_q, dg_mem_kv, dg_ffn, dg_final)


WEIGHTS =("w_in", "w_up_a", "w_up_b", "w_out", "w_q_mem", "w_kv_mem", "w_o_mem", "w_ffn_gate", "w_ffn_up", "w_ffn_down")
GAINS = ("g_mix", "g_mem_q", "g_mem_kv", "g_ffn", "g_final")
ORDER = ("g_mix", "w_in", "w_up_a", "w_up_b", "w_out", "g_mem_q", "g_mem_kv", "w_q_mem", "w_kv_mem", "w_o_mem", "g_ffn",
         "w_ffn_gate", "w_ffn_up", "w_ffn_down", "g_final")


def kernel(x, mem, positions, g_mix, w_in, w_up_a, w_up_b, w_out, g_mem_q, g_mem_kv, w_q_mem, w_kv_mem, w_o_mem, g_ffn, w_ffn_gate, w_ffn_up, w_ffn_down, g_final, loss_target, m_g_mix, m_w_in, m_w_up_a, m_w_up_b, m_w_out, m_g_mem_q, m_g_mem_kv, m_w_q_mem, m_w_kv_mem, m_w_o_mem, m_g_ffn, m_w_ffn_gate, m_w_ffn_up, m_w_ffn_down, m_g_final, v_g_mix, v_w_in, v_w_up_a, v_w_up_b, v_w_out, v_g_mem_q, v_g_mem_kv, v_w_q_mem, v_w_kv_mem, v_w_o_mem, v_g_ffn, v_w_ffn_gate, v_w_ffn_up, v_w_ffn_down, v_g_final):
    given = dict(locals())
    batch, s, d = x.shape
    t = batch * s
    flipped = ("w_ffn_gate", "w_ffn_up")

    def view(a, n):
        a = a.reshape(a.shape[-2:])
        return a.T if n in flipped else a

    def unview(a, n):
        return (a.T if n in flipped else a).reshape(given[n].shape)

    shard = {n: view(given[n], n) for n in WEIGHTS}
    gains = [given[n].reshape(1, d) for n in GAINS]

    pad = (-shard["w_ffn_down"].shape[0]) % LANES
    cast = _cast_weights([shard[n] for n in WEIGHTS], [pad if n in flipped + ("w_ffn_down",) else 0 for n in WEIGHTS])
    loss_part, grad_x, pieces, dgains = _local_step(
        x.reshape(t, d), mem.reshape(-1, d), positions.reshape(t, 1), loss_target.reshape(t, d), gains, cast[0],
        cast[1:], batch)

    grad, delta, new_m, new_v = {}, {}, {}, {}
    for n, p in zip(WEIGHTS, pieces):
        outs = _adam("adam_" + n, p, shard[n], view(given["m_" + n], n), view(given["v_" + n], n))
        grad[n], delta[n], new_m[n], new_v[n] = [unview(o, n) for o in outs]

    g_all = _allreduce_small(list(dgains), loss_part)
    small = _adam_small(g_all, gains, [given["m_" + n].reshape(1, d) for n in GAINS],
                        [given["v_" + n].reshape(1, d) for n in GAINS])
    for out, vals in zip((grad, delta, new_m, new_v), small):
        for n, val in zip(GAINS, vals):
            out[n] = val.reshape(given[n].shape)

    loss = g_all[SMALL_ROWS - 1, 0]
    return (loss, grad_x.reshape(x.shape), *[grad[n] for n in ORDER], *[delta[n] for n in ORDER],
            *[new_m[n] for n in ORDER], *[new_v[n] for n in ORDER])
```

```python
import functools
import math

import jax
import jax.numpy as jnp
import numpy as np
from jax import lax
from jax.experimental import pallas as pl
from jax.experimental.pallas import tpu as pltpu

F32 = jnp.float32
BF16 = jnp.bfloat16

N_DEV = 8
HEAD_DIM = 64
MEM_HEAD_DIM = 128
N_HEADS_MEM = 4
BLOCK = 128
DIL_PATTERNS = ((128, 1), (512, 4), (2048, 16))
ROPE_THETA = 500000.0
ROPE_HALF = 8
RMS_EPS = 1e-6
ADAM_LR, ADAM_B1, ADAM_B2, ADAM_EPS, ADAM_WD, ADAM_STEP = 0.001, 0.9, 0.999, 1e-08, 0.01, 10
NEG = -1e30
ROW_TILE = 512
LANES = 128

ANY = pl.BlockSpec(memory_space=pl.ANY)
VMEM = pl.BlockSpec(memory_space=pltpu.VMEM)
NN = (((1,), (0,)), ((), ()))
NT = (((1,), (1,)), ((), ()))
TN = (((0,), (0,)), ((), ()))


def _params(sem):
    return pltpu.CompilerParams(dimension_semantics=sem)


def _mm(name, a, b, *, grid, a_spec, b_spec, o_shape, o_spec, dims, out_dtype, nk=1, res=None, res_spec=None):
    has_res = res is not None

    def body(*refs):
        a_ref, b_ref = refs[0], refs[1]
        r_ref = refs[2] if has_res else None
        o_ref = refs[3] if has_res else refs[2]
        p = lax.dot_general(a_ref[...], b_ref[...], dims, preferred_element_type=F32)
        if nk == 1:
            if has_res:
                p = p + r_ref[...].astype(F32)
            o_ref[...] = p.astype(out_dtype)
            return
        acc_ref = refs[-1]
        k = pl.program_id(len(grid) - 1)

        @pl.when(k == 0)
        def _():
            acc_ref[...] = p

        @pl.when(k > 0)
        def _():
            acc_ref[...] += p

        @pl.when(k == nk - 1)
        def _():
            t = acc_ref[...]
            if has_res:
                t = t + r_ref[...].astype(F32)
            o_ref[...] = t.astype(out_dtype)

    o_block = tuple(d for d in o_spec.block_shape if d is not None)
    sem = ("parallel",) * (len(grid) - 1) + (("arbitrary",) if nk > 1 else ("parallel",))
    return pl.pallas_call(
        body, name=name, grid=grid,
        in_specs=[a_spec, b_spec] + ([res_spec] if has_res else []),
        out_specs=o_spec, out_shape=jax.ShapeDtypeStruct(o_shape, out_dtype),
        scratch_shapes=[pltpu.VMEM(o_block, F32)] if nk > 1 else [],
        compiler_params=_params(sem),
    )(*([a, b] + ([res] if has_res else [])))


def _rms_fwd(name, x, g):
    t, d = x.shape
    tm = min(ROW_TILE, t)

    def body(x_ref, g_ref, o_ref):
        xf = x_ref[...]
        r = lax.rsqrt(jnp.mean(xf * xf, axis=-1, keepdims=True) + RMS_EPS)
        o_ref[...] = (xf * r * g_ref[...]).astype(BF16)

    return pl.pallas_call(
        body, name=name, grid=(t // tm,),
        in_specs=[pl.BlockSpec((tm, d), lambda i: (i, 0)), pl.BlockSpec((1, d), lambda i: (0, 0))],
        out_specs=pl.BlockSpec((tm, d), lambda i: (i, 0)), out_shape=jax.ShapeDtypeStruct((t, d), BF16),
        compiler_params=_params(("parallel",)),
    )(x, g)


def _rms_bwd(name, dn, x, g, dres, want):
    t, d = x.shape
    tm = min(ROW_TILE, t)
    has_res = dres is not None

    def body(*refs):
        dn_ref, x_ref, g_ref = refs[0], refs[1], refs[2]
        r_ref = refs[3] if has_res else None
        dx_refs, dg_ref = refs[-1 - len(want):-1], refs[-1]
        xf = x_ref[...]
        r = lax.rsqrt(jnp.mean(xf * xf, axis=-1, keepdims=True) + RMS_EPS)
        xh = xf * r
        dnf = dn_ref[...].astype(F32)
        if want:
            dxh = dnf * g_ref[...]
            dx = r * (dxh - xh * jnp.mean(dxh * xh, axis=-1, keepdims=True))
            if has_res:
                dx = dx + r_ref[...]
            for kind, dx_ref in zip(want, dx_refs):
                dx_ref[...] = dx.astype(F32 if kind == "f32" else BF16)

        @pl.when(pl.program_id(0) == 0)
        def _():
            dg_ref[...] = jnp.zeros_like(dg_ref)

        dg_ref[...] += jnp.sum(dnf * xh, axis=0, keepdims=True)

    row = pl.BlockSpec((tm, d), lambda i: (i, 0))
    vec = pl.BlockSpec((1, d), lambda i: (0, 0))
    return pl.pallas_call(
        body, name=name, grid=(t // tm,),
        in_specs=[row, row, vec] + ([row] if has_res else []),
        out_specs=[row] * len(want) + [vec],
        out_shape=[jax.ShapeDtypeStruct((t, d), F32 if kind == "f32" else BF16) for kind in want]
        + [jax.ShapeDtypeStruct((1, d), F32)],
        compiler_params=_params(("arbitrary",)),
    )(*([dn, x, g] + ([dres] if has_res else [])))


def _loss_head(h, tgt, g):
    t, d = h.shape
    tm = min(ROW_TILE, t)

    def body(h_ref, t_ref, g_ref, loss_ref, dh_ref, dhb_ref, dg_ref):
        xf = h_ref[...]
        gv = g_ref[...]
        r = lax.rsqrt(jnp.mean(xf * xf, axis=-1, keepdims=True) + RMS_EPS)
        xh = xf * r
        e = xh * gv - t_ref[...]
        dy = e * (1.0 / d)
        dxh = dy * gv
        dh = r * (dxh - xh * jnp.mean(dxh * xh, axis=-1, keepdims=True))
        dh_ref[...] = dh
        dhb_ref[...] = dh.astype(BF16)

        @pl.when(pl.program_id(0) == 0)
        def _():
            dg_ref[...] = jnp.zeros_like(dg_ref)
            loss_ref[...] = jnp.zeros_like(loss_ref)

        dg_ref[...] += jnp.sum(dy * xh, axis=0, keepdims=True)
        part = jnp.sum(jnp.sum(e * e, axis=1, keepdims=True), axis=0, keepdims=True) * (0.5 / d)
        loss_ref[...] += jnp.broadcast_to(part, loss_ref.shape)

    row = pl.BlockSpec((tm, d), lambda i: (i, 0))
    vec = pl.BlockSpec((1, d), lambda i: (0, 0))
    return pl.pallas_call(
        body, name="loss_head", grid=(t // tm,),
        in_specs=[row, row, vec],
        out_specs=[pl.BlockSpec((8, LANES), lambda i: (0, 0)), row, row, vec],
        out_shape=[jax.ShapeDtypeStruct((8, LANES), F32), jax.ShapeDtypeStruct((t, d), F32),
                   jax.ShapeDtypeStruct((t, d), BF16), jax.ShapeDtypeStruct((1, d), F32)],
        compiler_params=_params(("arbitrary",)),
    )(h, tgt, g)


def _rope_tables(pos, inv_freq, sel_lo, sel_hi):
    t = pos.shape[0]
    tm = min(ROW_TILE, t)

    def body(p_ref, f_ref, lo_ref, hi_ref, c_ref, sa_ref, sb_ref):
        ang = p_ref[...].astype(F32) * f_ref[...]
        rot = lo_ref[...] + hi_ref[...]
        cs, sn = jnp.cos(ang), jnp.sin(ang)
        c_ref[...] = cs * rot + (1.0 - rot)
        sa_ref[...] = -sn * lo_ref[...]
        sb_ref[...] = sn * hi_ref[...]

    vec = pl.BlockSpec((1, LANES), lambda i: (0, 0))
    row = pl.BlockSpec((tm, LANES), lambda i: (i, 0))
    return pl.pallas_call(
        body, name="rope_tables", grid=(t // tm,),
        in_specs=[pl.BlockSpec((tm, 1), lambda i: (i, 0)), vec, vec, vec],
        out_specs=[row, row, row], out_shape=[jax.ShapeDtypeStruct((t, LANES), F32)] * 3,
        compiler_params=_params(("parallel",)),
    )(pos, inv_freq, sel_lo, sel_hi)


def _rope_apply(name, srcs, width, cos_t, sin_a, sin_b, sign):
    t = srcs[0].shape[0]
    tm = min(ROW_TILE, t)
    n_cols = width // LANES

    def body(*refs):
        x_refs, (c_ref, sa_ref, sb_ref, o_ref) = refs[:len(srcs)], refs[len(srcs):]
        cs, sa, sb = c_ref[...], sign * sa_ref[...], sign * sb_ref[...]
        for a, x_ref in enumerate(x_refs):
            for c in range(n_cols):
                xf = x_ref[:, c * LANES:(c + 1) * LANES].astype(F32)
                up = pltpu.roll(xf, LANES - ROPE_HALF, 1)
                dn = pltpu.roll(xf, ROPE_HALF, 1)
                o_ref[:, a * width + c * LANES:a * width + (c + 1) * LANES] = (xf * cs + up * sa + dn * sb).astype(BF16)

    wide = len(srcs) * width
    tab = pl.BlockSpec((tm, LANES), lambda i: (i, 0))
    return pl.pallas_call(
        body, name=name, grid=(t // tm,),
        in_specs=[pl.BlockSpec((tm, width), lambda i: (i, 0))] * len(srcs) + [tab, tab, tab],
        out_specs=pl.BlockSpec((tm, wide), lambda i: (i, 0)),
        out_shape=jax.ShapeDtypeStruct((t, wide), BF16),
        compiler_params=_params(("parallel",)),
    )(*srcs, cos_t, sin_a, sin_b)


DA_T = 256
FWD_STREAMS = 4
BWD_STREAMS = 2


def _lane_lo():
    return lax.broadcasted_iota(jnp.int32, (BLOCK, LANES), 1) < HEAD_DIM


def _dilated_bias_tiles(s):
    n = s // DA_T
    dist = (np.arange(n)[:, None, None] * DA_T + np.arange(DA_T)[None, :, None] - np.arange(DA_T)[None, None, :])
    cnt = np.zeros(dist.shape, np.float32)
    for window, dil in DIL_PATTERNS:
        cnt += ((dist >= 0) & (dist % dil == 0) & (dist <= window)).astype(np.float32)
    return jnp.asarray(np.where(cnt > 0, np.log(np.maximum(cnt, 1.0)), NEG).astype(np.float32))


def _stack_heads(x, lo):
    zero = jnp.zeros_like(x)
    return jnp.concatenate([jnp.where(lo, x, zero), jnp.where(lo, zero, x)], axis=0)


def _da_fwd(qk, proj, v_col0, bias, batch, s, ride=None, streams=FWD_STREAMS):
    t = qk.shape[0]
    nq = s // DA_T
    n_pairs = 4
    ns = streams
    wide = ns * LANES
    scale = HEAD_DIM ** -0.5

    def body(q_ref, k_ref, v_ref, b_ref, o_ref, lse_ref, acc_ref, m_ref, l_ref):
        i = pl.program_id(2)
        lo = lax.broadcasted_iota(jnp.int32, (DA_T, LANES), 1) < HEAD_DIM
        ones = jnp.ones((DA_T, LANES), BF16)
        acc_ref[...] = jnp.zeros_like(acc_ref)
        m_ref[...] = jnp.full(m_ref.shape, NEG, F32)
        l_ref[...] = jnp.zeros_like(l_ref)
        qqs = [_stack_heads(q_ref[:, st * LANES:(st + 1) * LANES] * scale, lo) for st in range(ns)]

        def scores(st, rows, bias2):
            k = k_ref[rows, st * LANES:(st + 1) * LANES]
            return lax.dot_general(qqs[st], k, NT, preferred_element_type=F32) + bias2

        def softmax(st, sc):
            m_old = m_ref[st]
            m_new = jnp.maximum(m_old, jnp.max(sc, axis=1, keepdims=True))
            m_ref[st] = m_new
            return jnp.exp(sc - m_new).astype(BF16), jnp.exp(m_old - m_new)

        def values(st, rows, p, alpha):
            v = v_ref[rows, st * LANES:(st + 1) * LANES]
            vz = jnp.zeros_like(v)
            l_ref[st] = alpha * l_ref[st] + lax.dot_general(p, ones, NN, preferred_element_type=F32)
            pv = (lax.dot_general(p[:DA_T], jnp.where(lo, v, vz), NN, preferred_element_type=F32)
                  + lax.dot_general(p[DA_T:], jnp.where(lo, vz, v), NN, preferred_element_type=F32))
            acc_ref[st] = acc_ref[st] * jnp.where(lo, alpha[:DA_T], alpha[DA_T:]) + pv

        def trip(dlt, carry):
            rows = pl.ds(pl.multiple_of((i - dlt) * DA_T, DA_T), DA_T)
            bias_t = b_ref[dlt]
            bias2 = jnp.concatenate([bias_t, bias_t], axis=0)
            scs = [scores(st, rows, bias2) for st in range(ns)]
            pas = [softmax(st, scs[st]) for st in range(ns)]
            for st in range(ns):
                values(st, rows, *pas[st])
            return carry

        lax.fori_loop(0, i + 1, trip, 0)
        for st in range(ns):
            cols = slice(st * LANES, (st + 1) * LANES)
            l_t = l_ref[st]
            o_ref[:, cols] = (acc_ref[st] / jnp.where(lo, l_t[:DA_T], l_t[DA_T:])).astype(BF16)
            lse = m_ref[st] + jnp.log(l_t)
            lse_ref[:, cols] = jnp.where(lo, lse[:DA_T], lse[DA_T:])

    blk = pl.BlockSpec((DA_T, wide), lambda b, h, i: (b * nq + i, h))
    return _call(
        body, name="attn_a_fwd", grid=(batch, n_pairs // ns, nq),
        in_specs=[blk,
                  pl.BlockSpec((s, wide), lambda b, h, i: (b, n_pairs // ns + h)),
                  pl.BlockSpec((s, wide), lambda b, h, i: (b, v_col0 // ns + h)),
                  pl.BlockSpec((nq, DA_T, DA_T), lambda b, h, i: (0, 0, 0))],
        out_specs=[blk, blk],
        out_shape=[jax.ShapeDtypeStruct((t, n_pairs * LANES), BF16), jax.ShapeDtypeStruct((t, n_pairs * LANES), F32)],
        scratch=[pltpu.VMEM((ns, DA_T, LANES), F32), pltpu.VMEM((ns, 2 * DA_T, 1), F32),
                 pltpu.VMEM((ns, 2 * DA_T, LANES), F32)],
        sem=("parallel", "parallel", "arbitrary"), args=(qk, qk, proj, bias), ride=ride)


def _da_bwd(qk, proj, v_col0, bias, o, lse, do, batch, s, ride=None, streams=BWD_STREAMS):
    t = qk.shape[0]
    nq = s // DA_T
    n_pairs = 4
    ns = streams
    wide = ns * LANES
    scale = HEAD_DIM ** -0.5

    def body(q_ref, k_ref, v_ref, b_ref, o_ref, lse_ref, do_ref, dq_ref, dk_ref, dv_ref, dk_acc, dv_acc, dq_acc):
        i = pl.program_id(2)
        lo = lax.broadcasted_iota(jnp.int32, (DA_T, LANES), 1) < HEAD_DIM

        @pl.when(i == 0)
        def _():
            dk_acc[...] = jnp.zeros_like(dk_acc)
            dv_acc[...] = jnp.zeros_like(dv_acc)

        dq_acc[...] = jnp.zeros_like(dq_acc)
        qqs, dds, deltas, lses = [], [], [], []
        for st in range(ns):
            cols = slice(st * LANES, (st + 1) * LANES)
            do_ = do_ref[:, cols]
            qqs.append(_stack_heads(q_ref[:, cols] * scale, lo))
            dds.append(_stack_heads(do_, lo))
            prod = do_.astype(F32) * o_ref[:, cols].astype(F32)
            fz = jnp.zeros_like(prod)
            deltas.append(jnp.concatenate([jnp.sum(jnp.where(lo, prod, fz), axis=1, keepdims=True),
                                           jnp.sum(jnp.where(lo, fz, prod), axis=1, keepdims=True)], axis=0))
            lse_t = lse_ref[:, cols]
            lses.append(jnp.concatenate([lse_t[:, 0:1], lse_t[:, HEAD_DIM:HEAD_DIM + 1]], axis=0))

        def products(st, rows, bias2):
            cols = slice(st * LANES, (st + 1) * LANES)
            sc = lax.dot_general(qqs[st], k_ref[rows, cols], NT, preferred_element_type=F32) + bias2
            return sc, lax.dot_general(dds[st], v_ref[rows, cols], NT, preferred_element_type=F32)

        def weights(st, sc, dp):
            p = jnp.exp(sc - lses[st])
            return (p * (dp - deltas[st])).astype(BF16), p.astype(BF16)

        def gradients(st, rows, ds, p):
            cols = slice(st * LANES, (st + 1) * LANES)
            k = k_ref[rows, cols]
            kz = jnp.zeros_like(k)
            dq_acc[st] += (lax.dot_general(ds[:DA_T], jnp.where(lo, k, kz), NN, preferred_element_type=F32)
                           + lax.dot_general(ds[DA_T:], jnp.where(lo, kz, k), NN, preferred_element_type=F32))
            dk_acc[rows, cols] += lax.dot_general(ds, qqs[st], TN, preferred_element_type=F32)
            dv_acc[rows, cols] += lax.dot_general(p, dds[st], TN, preferred_element_type=F32)

        def trip(dlt, carry):
            rows = pl.ds(pl.multiple_of((i - dlt) * DA_T, DA_T), DA_T)
            bias_t = b_ref[dlt]
            bias2 = jnp.concatenate([bias_t, bias_t], axis=0)
            prods = [products(st, rows, bias2) for st in range(ns)]
            wts = [weights(st, *prods[st]) for st in range(ns)]
            for st in range(ns):
                gradients(st, rows, *wts[st])
            return carry

        lax.fori_loop(0, i + 1, trip, 0)
        for st in range(ns):
            dq_ref[:, st * LANES:(st + 1) * LANES] = (dq_acc[st] * scale).astype(BF16)

        @pl.when(i == nq - 1)
        def _():
            dk_ref[...] = dk_acc[...].astype(BF16)
            dv_ref[...] = dv_acc[...].astype(BF16)

    blk = pl.BlockSpec((DA_T, wide), lambda b, h, i: (b * nq + i, h))
    seq = pl.BlockSpec((s, wide), lambda b, h, i: (b, h))
    out = jax.ShapeDtypeStruct((t, n_pairs * LANES), BF16)
    return _call(
        body, name="attn_a_bwd", grid=(batch, n_pairs // ns, nq),
        in_specs=[blk,
                  pl.BlockSpec((s, wide), lambda b, h, i: (b, n_pairs // ns + h)),
                  pl.BlockSpec((s, wide), lambda b, h, i: (b, v_col0 // ns + h)),
                  pl.BlockSpec((nq, DA_T, DA_T), lambda b, h, i: (0, 0, 0)),
                  blk, blk, blk],
        out_specs=[blk, seq, seq], out_shape=[out, out, out],
        scratch=[pltpu.VMEM((s, wide), F32), pltpu.VMEM((s, wide), F32), pltpu.VMEM((ns, DA_T, LANES), F32)],
        sem=("parallel", "parallel", "arbitrary"), args=(qk, qk, proj, bias, o, lse, do), ride=ride)


SB_Q = 256


def _sb_consts(after):
    r = lax.broadcasted_iota(jnp.int32, (2 * BLOCK, 2 * BLOCK), 0) % BLOCK
    c = lax.broadcasted_iota(jnp.int32, (2 * BLOCK, 2 * BLOCK), 1)
    tri = (r > c) if after else (r < c)
    return jnp.logical_or(c >= BLOCK, tri).astype(BF16)


def _split(x):
    hi = x.astype(BF16)
    lo = (x - hi.astype(F32)).astype(BF16)
    return jnp.concatenate([hi, lo], axis=1)


def _sb_fwd(proj, q_col0, k_col0, v_col0, batch, s, ride=None, streams=FWD_STREAMS):
    t = proj.shape[0]
    nq = s // SB_Q
    n_pairs = 4
    ns = streams
    wide = ns * LANES
    scale = HEAD_DIM ** -0.5

    def body(q_ref, k_ref, v_ref, o_ref, tot_ref, acc_ref, run_ref):
        i = pl.program_id(2)
        lo_q = lax.broadcasted_iota(jnp.int32, (SB_Q, LANES), 1) < HEAD_DIM
        lo_k = _lane_lo()
        mat = _sb_consts(True)
        row = lax.broadcasted_iota(jnp.int32, (2 * SB_Q, LANES), 0) % SB_Q
        ahead = row - lax.broadcasted_iota(jnp.int32, (2 * SB_Q, LANES), 1)
        acc_ref[...] = jnp.zeros_like(acc_ref)
        run_ref[...] = jnp.zeros_like(run_ref)
        qqs = [_stack_heads(q_ref[:, st * LANES:(st + 1) * LANES] * scale, lo_q) for st in range(ns)]

        def units(todo):
            def rows(j):
                return pl.ds(pl.multiple_of(j * BLOCK, BLOCK), BLOCK)

            zs = [lax.dot_general(qqs[st], k_ref[rows(j), st * LANES:(st + 1) * LANES], NT, preferred_element_type=F32)
                  for st, j, _ in todo]
            logs = []
            for z, (_, _, off) in zip(zs, todo):
                lsig = jnp.minimum(z, 0.0) - jnp.log(1.0 + jnp.exp(-jnp.abs(z)))
                lneg = lsig - z
                if off is not None:
                    lneg = jnp.where(ahead > off, lneg, 0.0)
                logs.append((lsig, _split(lneg)))
            sums = [lax.dot_general(cat, mat, NN, preferred_element_type=F32) for _, cat in logs]
            probs = []
            for (lsig, _), sm, (st, _, off) in zip(logs, sums, todo):
                run = run_ref[st]
                a = jnp.exp(lsig + run + sm[:, :BLOCK])
                if off is not None:
                    a = jnp.where(ahead > off, a, 0.0)
                run_ref[st] = run + sm[:, BLOCK:]
                probs.append(a.astype(BF16))
            for ab, (st, j, _) in zip(probs, todo):
                v = v_ref[rows(j), st * LANES:(st + 1) * LANES]
                vz = jnp.zeros_like(v)
                acc_ref[st] += (lax.dot_general(ab[:SB_Q], jnp.where(lo_k, v, vz), NN, preferred_element_type=F32)
                                + lax.dot_general(ab[SB_Q:], jnp.where(lo_k, vz, v), NN, preferred_element_type=F32))

        units([(st, 2 * i + 1, BLOCK) for st in range(ns)] + [(st, 2 * i, 0) for st in range(ns)])

        def pair(p, carry):
            jp = i - 1 - p
            units([(st, 2 * jp + 1, None) for st in range(ns)] + [(st, 2 * jp, None) for st in range(ns)])
            return carry

        lax.fori_loop(0, i, pair, 0)
        for st in range(ns):
            cols = slice(st * LANES, (st + 1) * LANES)
            o_ref[:, cols] = acc_ref[st].astype(BF16)
            tot_ref[:, cols] = jnp.where(lo_q, run_ref[st, 0:SB_Q, :], run_ref[st, SB_Q:2 * SB_Q, :])

    def seq(col0):
        return pl.BlockSpec((s, wide), lambda b, h, i: (b, col0 // ns + h))

    blk = pl.BlockSpec((SB_Q, wide), lambda b, h, i: (b * nq + i, h))
    return _call(
        body, name="attn_b_fwd", grid=(batch, n_pairs // ns, nq),
        in_specs=[pl.BlockSpec((SB_Q, wide), lambda b, h, i: (b * nq + i, q_col0 // ns + h)), seq(k_col0), seq(v_col0)],
        out_specs=[blk, blk],
        out_shape=[jax.ShapeDtypeStruct((t, n_pairs * LANES), BF16), jax.ShapeDtypeStruct((t, n_pairs * LANES), F32)],
        scratch=[pltpu.VMEM((ns, SB_Q, LANES), F32), pltpu.VMEM((ns, 2 * SB_Q, LANES), F32)],
        sem=("parallel", "parallel", "arbitrary"), args=(proj, proj, proj), ride=ride)


def _sb_bwd(proj, q_col0, k_col0, v_col0, tot, do, batch, s, ride=None, streams=BWD_STREAMS):
    t = proj.shape[0]
    nq = s // SB_Q
    n_pairs = 4
    ns = streams
    wide = ns * LANES
    scale = HEAD_DIM ** -0.5

    def body(q_ref, k_ref, v_ref, tot_ref, do_ref, dq_ref, dk_ref, dv_ref, dk_acc, dv_acc, dq_acc, seen_ref, gsum_ref):
        i = pl.program_id(2)
        lo_q = lax.broadcasted_iota(jnp.int32, (SB_Q, LANES), 1) < HEAD_DIM
        lo_k = _lane_lo()

        @pl.when(i == 0)
        def _():
            dk_acc[...] = jnp.zeros_like(dk_acc)
            dv_acc[...] = jnp.zeros_like(dv_acc)

        mat_after = _sb_consts(True)
        mat_before = _sb_consts(False)
        row = lax.broadcasted_iota(jnp.int32, (2 * SB_Q, LANES), 0) % SB_Q
        ahead = row - lax.broadcasted_iota(jnp.int32, (2 * SB_Q, LANES), 1)
        dq_acc[...] = jnp.zeros_like(dq_acc)
        seen_ref[...] = jnp.zeros_like(seen_ref)
        gsum_ref[...] = jnp.zeros_like(gsum_ref)
        qqs, dds, totals = [], [], []
        for st in range(ns):
            cols = slice(st * LANES, (st + 1) * LANES)
            qqs.append(_stack_heads(q_ref[:, cols] * scale, lo_q))
            dds.append(_stack_heads(do_ref[:, cols], lo_q))
            tot_t = tot_ref[:, cols]
            totals.append(jnp.concatenate([jnp.broadcast_to(tot_t[:, 0:1], (SB_Q, LANES)),
                                           jnp.broadcast_to(tot_t[:, HEAD_DIM:HEAD_DIM + 1], (SB_Q, LANES))], axis=0))

        def units(todo):
            def rows(j):
                return pl.ds(pl.multiple_of(j * BLOCK, BLOCK), BLOCK)

            def cols(st):
                return slice(st * LANES, (st + 1) * LANES)

            prods = [(lax.dot_general(qqs[st], k_ref[rows(j), cols(st)], NT, preferred_element_type=F32),
                      lax.dot_general(dds[st], v_ref[rows(j), cols(st)], NT, preferred_element_type=F32))
                     for st, j, _ in todo]
            logs = []
            for (z, _), (_, _, off) in zip(prods, todo):
                lsig = jnp.minimum(z, 0.0) - jnp.log(1.0 + jnp.exp(-jnp.abs(z)))
                lneg = lsig - z
                if off is not None:
                    lneg = jnp.where(ahead > off, lneg, 0.0)
                logs.append((lsig, _split(lneg)))
            sums = [lax.dot_general(cat, mat_after, NN, preferred_element_type=F32) for _, cat in logs]
            gates = []
            for (lsig, _), sm, (_, da), (st, _, off) in zip(logs, sums, prods, todo):
                seen = seen_ref[st]
                a = jnp.exp(lsig + (totals[st] - seen - sm[:, BLOCK:]) + sm[:, :BLOCK])
                if off is not None:
                    a = jnp.where(ahead > off, a, 0.0)
                seen_ref[st] = seen + sm[:, BLOCK:]
                g = a * da
                gates.append((a.astype(BF16), g, _split(g)))
            gsums = [lax.dot_general(cat, mat_before, NN, preferred_element_type=F32) for _, _, cat in gates]
            outs = []
            for (lsig, _), (ab, g, _), gs, (st, _, off) in zip(logs, gates, gsums, todo):
                gsum = gsum_ref[st]
                dz = g - jnp.exp(lsig) * (g + gsum + gs[:, :BLOCK])
                if off is not None:
                    dz = jnp.where(ahead > off, dz, 0.0)
                gsum_ref[st] = gsum + gs[:, BLOCK:]
                outs.append((dz.astype(BF16), ab))
            for (dzb, ab), (st, j, _) in zip(outs, todo):
                k = k_ref[rows(j), cols(st)]
                kz = jnp.zeros_like(k)
                dq_acc[st] += (lax.dot_general(dzb[:SB_Q], jnp.where(lo_k, k, kz), NN, preferred_element_type=F32)
                               + lax.dot_general(dzb[SB_Q:], jnp.where(lo_k, kz, k), NN, preferred_element_type=F32))
                dk_acc[rows(j), cols(st)] += lax.dot_general(dzb, qqs[st], TN, preferred_element_type=F32)
                dv_acc[rows(j), cols(st)] += lax.dot_general(ab, dds[st], TN, preferred_element_type=F32)

        def pair(p, carry):
            units([(st, 2 * p, None) for st in range(ns)] + [(st, 2 * p + 1, None) for st in range(ns)])
            return carry

        lax.fori_loop(0, i, pair, 0)
        units([(st, 2 * i, 0) for st in range(ns)] + [(st, 2 * i + 1, BLOCK) for st in range(ns)])
        for st in range(ns):
            dq_ref[:, st * LANES:(st + 1) * LANES] = (dq_acc[st] * scale).astype(BF16)

        @pl.when(i == nq - 1)
        def _():
            dk_ref[...] = dk_acc[...].astype(BF16)
            dv_ref[...] = dv_acc[...].astype(BF16)

    def seq_in(col0):
        return pl.BlockSpec((s, wide), lambda b, h, i: (b, col0 // ns + h))

    blk = pl.BlockSpec((SB_Q, wide), lambda b, h, i: (b * nq + i, h))
    seq = pl.BlockSpec((s, wide), lambda b, h, i: (b, h))
    out = jax.ShapeDtypeStruct((t, n_pairs * LANES), BF16)
    return _call(
        body, name="attn_b_bwd", grid=(batch, n_pairs // ns, nq),
        in_specs=[pl.BlockSpec((SB_Q, wide), lambda b, h, i: (b * nq + i, q_col0 // ns + h)), seq_in(k_col0),
                  seq_in(v_col0), blk, blk],
        out_specs=[blk, seq, seq], out_shape=[out, out, out],
        scratch=[pltpu.VMEM((s, wide), F32), pltpu.VMEM((s, wide), F32), pltpu.VMEM((ns, SB_Q, LANES), F32),
                 pltpu.VMEM((ns, 2 * SB_Q, LANES), F32), pltpu.VMEM((ns, 2 * SB_Q, LANES), F32)],
        sem=("parallel", "parallel", "arbitrary"), args=(proj, proj, proj, tot, do), ride=ride)


MEM_Q_TILE = 512


def _mem_fwd(q, kv, batch, s, n_mem):
    t, width = q.shape
    tq = min(MEM_Q_TILE, s)
    nq = s // tq
    scale = MEM_HEAD_DIM ** -0.5

    def body(q_ref, kv_ref, o_ref):
        for h in range(N_HEADS_MEM):
            cols = slice(h * MEM_HEAD_DIM, (h + 1) * MEM_HEAD_DIM)
            k = kv_ref[:, cols]
            v = kv_ref[:, width + h * MEM_HEAD_DIM: width + (h + 1) * MEM_HEAD_DIM]
            sc = lax.dot_general(q_ref[:, cols], k, NT, preferred_element_type=F32) * scale
            p = jnp.exp(sc - jnp.max(sc, axis=1, keepdims=True))
            p = p / jnp.sum(p, axis=1, keepdims=True)
            o_ref[:, cols] = lax.dot_general(p.astype(BF16), v, NN, preferred_element_type=F32).astype(BF16)

    return pl.pallas_call(
        body, name="mem_attn_fwd", grid=(batch, nq),
        in_specs=[pl.BlockSpec((tq, width), lambda b, i: (b * nq + i, 0)),
                  pl.BlockSpec((n_mem, 2 * width), lambda b, i: (b, 0))],
        out_specs=pl.BlockSpec((tq, width), lambda b, i: (b * nq + i, 0)),
        out_shape=jax.ShapeDtypeStruct((t, width), BF16),
        compiler_params=_params(("parallel", "parallel")),
    )(q, kv)


def _mem_bwd(q, kv, do, batch, s, n_mem):
    t, width = q.shape
    tq = min(MEM_Q_TILE, s)
    nq = s // tq
    scale = MEM_HEAD_DIM ** -0.5

    def body(q_ref, kv_ref, do_ref, dq_ref, dkv_ref, acc):
        i = pl.program_id(1)

        @pl.when(i == 0)
        def _():
            acc[...] = jnp.zeros_like(acc)

        for h in range(N_HEADS_MEM):
            cols = slice(h * MEM_HEAD_DIM, (h + 1) * MEM_HEAD_DIM)
            vcols = slice(width + h * MEM_HEAD_DIM, width + (h + 1) * MEM_HEAD_DIM)
            qh, k, v, doh = q_ref[:, cols], kv_ref[:, cols], kv_ref[:, vcols], do_ref[:, cols]
            sc = lax.dot_general(qh, k, NT, preferred_element_type=F32) * scale
            p = jnp.exp(sc - jnp.max(sc, axis=1, keepdims=True))
            p = p / jnp.sum(p, axis=1, keepdims=True)
            dp = lax.dot_general(doh, v, NT, preferred_element_type=F32)
            ds = (p * (dp - jnp.sum(p * dp, axis=1, keepdims=True)) * scale).astype(BF16)
            dq_ref[:, cols] = lax.dot_general(ds, k, NN, preferred_element_type=F32).astype(BF16)
            acc[:, cols] += lax.dot_general(ds, qh, TN, preferred_element_type=F32)
            acc[:, vcols] += lax.dot_general(p.astype(BF16), doh, TN, preferred_element_type=F32)

        @pl.when(i == nq - 1)
        def _():
            dkv_ref[...] = acc[...].astype(BF16)

    row = pl.BlockSpec((tq, width), lambda b, i: (b * nq + i, 0))
    kvs = pl.BlockSpec((n_mem, 2 * width), lambda b, i: (b, 0))
    return pl.pallas_call(
        body, name="mem_attn_bwd", grid=(batch, nq),
        in_specs=[row, kvs, row], out_specs=[row, kvs],
        out_shape=[jax.ShapeDtypeStruct((t, width), BF16), jax.ShapeDtypeStruct((batch * n_mem, 2 * width), BF16)],
        scratch_shapes=[pltpu.VMEM((n_mem, 2 * width), F32)],
        compiler_params=_params(("parallel", "arbitrary")),
    )(q, kv, do)


def _mixer_fwd(o_a, o_b, w_a, w_b, proj, gate_col0):
    t, width = o_a.shape
    d = w_a.shape[1]
    tm = min(ROW_TILE, t)
    gb0 = gate_col0 * LANES // d

    def body(oa_ref, ob_ref, wa_ref, wb_ref, ga_ref, gb_ref, ua_ref, ub_ref, mix_ref):
        ua = lax.dot_general(oa_ref[...], wa_ref[...], NN, preferred_element_type=F32)
        ub = lax.dot_general(ob_ref[...], wb_ref[...], NN, preferred_element_type=F32)
        ua_ref[...] = ua.astype(BF16)
        ub_ref[...] = ub.astype(BF16)
        mix_ref[...] = (jax.nn.sigmoid(ga_ref[...].astype(F32)) * ua
                        + jax.nn.sigmoid(gb_ref[...].astype(F32)) * ub).astype(BF16)

    row = pl.BlockSpec((tm, width), lambda i: (i, 0))
    wsp = pl.BlockSpec((width, d), lambda i: (0, 0))
    out = pl.BlockSpec((tm, d), lambda i: (i, 0))
    osh = jax.ShapeDtypeStruct((t, d), BF16)
    return pl.pallas_call(
        body, name="mixer_fwd", grid=(t // tm,),
        in_specs=[row, row, wsp, wsp,
                  pl.BlockSpec((tm, d), lambda i: (i, gb0)), pl.BlockSpec((tm, d), lambda i: (i, gb0 + 1))],
        out_specs=[out, out, out], out_shape=[osh, osh, osh],
        compiler_params=_params(("parallel",)),
    )(o_a, o_b, w_a, w_b, proj, proj)


def _mixer_bwd(dh, w_out, ua, ub, proj, gate_col0):
    t, d = dh.shape
    tm = min(ROW_TILE, t)
    nc = d // LANES

    def body(dh_ref, w_ref, ua_ref, ub_ref, ga_ref, gb_ref, dua_ref, dub_ref, dg_ref):
        dm = lax.dot_general(dh_ref[...], w_ref[...], NT, preferred_element_type=F32)
        sa = jax.nn.sigmoid(ga_ref[...].astype(F32))
        sb = jax.nn.sigmoid(gb_ref[...].astype(F32))
        dua_ref[...] = (dm * sa).astype(BF16)
        dub_ref[...] = (dm * sb).astype(BF16)
        dg_ref[:, 0:d] = (dm * ua_ref[...].astype(F32) * sa * (1.0 - sa)).astype(BF16)
        dg_ref[:, d:2 * d] = (dm * ub_ref[...].astype(F32) * sb * (1.0 - sb)).astype(BF16)

    row = pl.BlockSpec((tm, d), lambda i: (i, 0))
    return pl.pallas_call(
        body, name="mixer_bwd", grid=(t // tm,),
        in_specs=[row, pl.BlockSpec((d, d), lambda i: (0, 0)), row, row,
                  pl.BlockSpec((tm, d), lambda i: (i, gate_col0 // nc)),
                  pl.BlockSpec((tm, d), lambda i: (i, gate_col0 // nc + 1))],
        out_specs=[row, row, pl.BlockSpec((tm, 2 * d), lambda i: (i, 0))],
        out_shape=[jax.ShapeDtypeStruct((t, d), BF16), jax.ShapeDtypeStruct((t, d), BF16),
                   jax.ShapeDtypeStruct((t, 2 * d), BF16)],
        compiler_params=_params(("parallel",)),
    )(dh, w_out, ua, ub, proj, proj)


FFN_COLS = 1024


def _ffn_up(n, w_gate, w_up):
    t, d = n.shape
    hidden = w_gate.shape[0]
    tm = min(ROW_TILE, t)
    tn = min(FFN_COLS, hidden)

    def body(n_ref, wg_ref, wu_ref, hg_ref, hu_ref, act_ref):
        hg = lax.dot_general(n_ref[...], wg_ref[...], NT, preferred_element_type=F32)
        hu = lax.dot_general(n_ref[...], wu_ref[...], NT, preferred_element_type=F32)
        hg_ref[...] = hg.astype(BF16)
        hu_ref[...] = hu.astype(BF16)
        act_ref[...] = (hg * jax.nn.sigmoid(hg) * hu).astype(BF16)

    wsp = pl.BlockSpec((tn, d), lambda j, i: (j, 0))
    out = pl.BlockSpec((tm, tn), lambda j, i: (i, j))
    osh = jax.ShapeDtypeStruct((t, hidden), BF16)
    return pl.pallas_call(
        body, name="ffn_up", grid=(hidden // tn, t // tm),
        in_specs=[pl.BlockSpec((tm, d), lambda j, i: (i, 0)), wsp, wsp],
        out_specs=[out, out, out], out_shape=[osh, osh, osh],
        compiler_params=_params(("parallel", "parallel")),
    )(n, w_gate, w_up)


def _ffn_bwd(dh, w_down, w_gate, w_up, hg, hu):
    t, d = dh.shape
    hidden = w_down.shape[0]
    tm = min(ROW_TILE, t)
    tn = min(FFN_COLS, hidden)
    nj = hidden // tn

    def body(dh_ref, wd_ref, wg_ref, wu_ref, hg_ref, hu_ref, dhg_ref, dhu_ref, dn_ref, acc):
        j, i = pl.program_id(0), pl.program_id(1)
        dact = lax.dot_general(dh_ref[...], wd_ref[...], NT, preferred_element_type=F32)
        hg = hg_ref[...].astype(F32)
        sg = jax.nn.sigmoid(hg)
        dhu = (dact * hg * sg).astype(BF16)
        dhg = (dact * hu_ref[...].astype(F32) * sg * (1.0 + hg * (1.0 - sg))).astype(BF16)
        dhu_ref[...] = dhu
        dhg_ref[...] = dhg
        part = (lax.dot_general(dhg, wg_ref[...], NN, preferred_element_type=F32)
                + lax.dot_general(dhu, wu_ref[...], NN, preferred_element_type=F32))

        @pl.when(j == 0)
        def _():
            acc[i] = part

        @pl.when(j > 0)
        def _():
            acc[i] += part

        @pl.when(j == nj - 1)
        def _():
            dn_ref[...] = acc[i]

    hid = pl.BlockSpec((tm, tn), lambda j, i: (i, j))
    wsp = pl.BlockSpec((tn, d), lambda j, i: (j, 0))
    osh = jax.ShapeDtypeStruct((t, hidden), BF16)
    return pl.pallas_call(
        body, name="ffn_bwd", grid=(nj, t // tm),
        in_specs=[pl.BlockSpec((tm, d), lambda j, i: (i, 0)), wsp, wsp, wsp, hid, hid],
        out_specs=[hid, hid, pl.BlockSpec((tm, d), lambda j, i: (jnp.where(j == nj - 1, i, 0), 0))],
        out_shape=[osh, osh, jax.ShapeDtypeStruct((t, d), F32)],
        scratch_shapes=[pltpu.VMEM((t // tm, tm, d), F32)],
        compiler_params=_params(("arbitrary", "arbitrary")),
    )(dh, w_down, w_gate, w_up, hg, hu)


MM_ROWS = 1024


def _mm_w(name, a, w, out_dtype, dims=NN, res=None, tm=MM_ROWS, tn=1024):
    t, k = a.shape
    n = w.shape[1] if dims == NN else w.shape[0]
    tm, tn = min(tm, t), min(tn, n)
    o_spec = pl.BlockSpec((tm, tn), lambda j, i: (i, j))
    b_spec = pl.BlockSpec((k, tn), lambda j, i: (0, j)) if dims == NN else pl.BlockSpec((tn, k), lambda j, i: (j, 0))
    return _mm(name, a, w, grid=(n // tn, t // tm), a_spec=pl.BlockSpec((tm, k), lambda j, i: (i, 0)), b_spec=b_spec,
               o_shape=(t, n), o_spec=o_spec, dims=dims, out_dtype=out_dtype, res=res,
               res_spec=o_spec if res is not None else None)


def _mm_res_norm(name, a, w, res, g):
    t, k = a.shape
    d = w.shape[1]
    tm = min(ROW_TILE, t)

    def body(a_ref, w_ref, r_ref, g_ref, h_ref, n_ref):
        h = lax.dot_general(a_ref[...], w_ref[...], NN, preferred_element_type=F32) + r_ref[...]
        h_ref[...] = h
        r = lax.rsqrt(jnp.mean(h * h, axis=-1, keepdims=True) + RMS_EPS)
        n_ref[...] = (h * r * g_ref[...]).astype(BF16)

    row = pl.BlockSpec((tm, d), lambda i: (i, 0))
    return pl.pallas_call(
        body, name=name, grid=(t // tm,),
        in_specs=[pl.BlockSpec((tm, k), lambda i: (i, 0)), pl.BlockSpec((k, d), lambda i: (0, 0)), row,
                  pl.BlockSpec((1, d), lambda i: (0, 0))],
        out_specs=[row, row], out_shape=[jax.ShapeDtypeStruct((t, d), F32), jax.ShapeDtypeStruct((t, d), BF16)],
        compiler_params=_params(("parallel",)),
    )(a, w, res, g)


def _wgrad(name, a, g, tk=1024, tn=1024):
    t, k = a.shape
    n = g.shape[1]
    tm, tk, tn = min(2 * MM_ROWS, t), min(tk, k), min(tn, n)
    return _mm(name, a, g, grid=(k // tk, n // tn, t // tm),
               a_spec=pl.BlockSpec((tm, tk), lambda p, q, r: (r, p)), b_spec=pl.BlockSpec((tm, tn), lambda p, q, r: (r, q)),
               o_shape=(k, n), o_spec=pl.BlockSpec((tk, tn), lambda p, q, r: (p, q)), dims=TN, out_dtype=BF16, nk=t // tm)


def _peers():
    x, y, c = lax.axis_index("x"), lax.axis_index("y"), lax.axis_index("c")
    me = 4 * x + 2 * y + c
    out = []
    for k in range(1, N_DEV):
        kx, ky, kc = (k >> 2) & 1, (k >> 1) & 1, k & 1
        px = 1 - x if kx else x
        py = 1 - y if ky else y
        pc = 1 - c if kc else c
        out.append(((px, py, pc), 4 * px + 2 * py + pc))
    return me, out


def _cast_weights(ws, pad_rows):
    def body(*refs):
        n = len(refs) // 2
        for i_ref, o_ref, pr in zip(refs[:n], refs[n:], pad_rows):
            r, c = i_ref.shape
            o_ref[0:r, :] = i_ref[...].astype(BF16)
            if pr:
                o_ref[r:r + pr, :] = jnp.zeros((pr, c), BF16)

    return pl.pallas_call(
        body, name="cast_weights", in_specs=[VMEM] * len(ws), out_specs=[VMEM] * len(ws),
        out_shape=[jax.ShapeDtypeStruct((w.shape[0] + pr, w.shape[1]), BF16) for w, pr in zip(ws, pad_rows)],
    )(*ws)


def _window(ref, j, c):
    return ref.at[:, pl.ds(pl.multiple_of(j * c, LANES), c)]


def _scatter_copies(ins, outs, sems, cols, landed):
    send_sems, recv_sems, loc_sems = sems
    n_peer = N_DEV - 1
    me, peers = _peers()

    def src(w, j):
        return _window(ins[w], j, cols[w]) if cols[w] else ins[w].at[j]

    local = [pltpu.make_async_copy(src(w, me), outs[w].at[me], loc_sems.at[w]) for w in range(len(ins))]
    remote = [pltpu.make_async_remote_copy(
        src_ref=src(w, idx), dst_ref=outs[w].at[idx if landed else me],
        send_sem=send_sems.at[w * n_peer + k], recv_sem=recv_sems.at[w * n_peer + k],
        device_id=dev, device_id_type=pl.DeviceIdType.MESH)
        for k, (dev, idx) in reversed(list(enumerate(peers))) for w in range(len(ins))]
    return local, remote


OTHER_CHIPS = (2, 4, 6)


def _gather_copies(ins, outs, sems, cols):
    send_sems, recv_sems, loc_sems = sems
    x, y, c = lax.axis_index("x"), lax.axis_index("y"), lax.axis_index("c")
    me = 4 * x + 2 * y + c
    n_pair = N_DEV - 1

    def dev(mask):
        return (1 - x if mask & 4 else x, 1 - y if mask & 2 else y, 1 - c if mask & 1 else c)

    def slot(w, mask):
        j = jnp.bitwise_xor(me, mask)
        return _window(outs[w], j, cols[w]) if cols[w] else outs[w].at[j]

    def remote(w, pair, src, to_slot, target):
        return pltpu.make_async_remote_copy(src_ref=src, dst_ref=slot(w, to_slot), send_sem=send_sems.at[w * n_pair + pair],
                                            recv_sem=recv_sems.at[w * n_pair + pair], device_id=dev(target),
                                            device_id_type=pl.DeviceIdType.MESH)

    ws = range(len(ins))
    return dict(
        local=[pltpu.make_async_copy(ins[w], slot(w, 0), loc_sems.at[w]) for w in ws],
        to_chips=[remote(w, 1 + t, ins[w], 0, m) for t, m in enumerate(OTHER_CHIPS) for w in ws],
        to_core=[remote(w, 0, ins[w], 0, 1) for w in ws],
        from_chips=[remote(w, 1 + t, ins[w], m, 0) for t, m in enumerate(OTHER_CHIPS) for w in ws],
        pass_on=[remote(w, 4 + t, slot(w, m), m, 1) for t, m in enumerate(OTHER_CHIPS) for w in ws],
        from_core=[remote(w, 0, ins[w], 1, 0) for w in ws]
        + [remote(w, 4 + t, ins[w], m + 1, 0) for t, m in enumerate(OTHER_CHIPS) for w in ws])


def _exchange_start(ins, outs, sems, gather, cols):
    if gather:
        cps = _gather_copies(ins, outs, sems, cols)
        for cp in cps["local"] + cps["to_chips"] + cps["to_core"]:
            cp.start()
    else:
        local, remote = _scatter_copies(ins, outs, sems, cols, False)
        for cp in local + remote:
            cp.start()


def _exchange_pass_on(ins, outs, sems, gather, cols, chips):
    if gather:
        cps = _gather_copies(ins, outs, sems, cols)
        n = len(ins)
        for t in chips:
            for arrived, onward in zip(cps["from_chips"][t * n:(t + 1) * n], cps["pass_on"][t * n:(t + 1) * n]):
                arrived.wait_recv()
                onward.start()


def _exchange_wait(ins, outs, sems, gather, cols):
    if gather:
        cps = _gather_copies(ins, outs, sems, cols)
        for cp in cps["local"]:
            cp.wait()
        for cp in cps["to_chips"] + cps["to_core"] + cps["pass_on"]:
            cp.wait_send()
        for cp in cps["from_core"]:
            cp.wait_recv()
    else:
        local, remote = _scatter_copies(ins, outs, sems, cols, True)
        for cp in local:
            cp.wait()
        for cp in remote:
            cp.wait_send()
            cp.wait_recv()


def _exchange_shapes(arrs, gather, cols):
    n = len(arrs)
    out_shape = []
    for a, c in zip(arrs, cols):
        if gather:
            shape = (a.shape[0], N_DEV * c) if c else (N_DEV,) + a.shape
        else:
            shape = (N_DEV, a.shape[0], c) if c else a.shape
        out_shape.append(jax.ShapeDtypeStruct(shape, a.dtype))
    sems = [pltpu.SemaphoreType.DMA((n * (N_DEV - 1),)), pltpu.SemaphoreType.DMA((n * (N_DEV - 1),)),
            pltpu.SemaphoreType.DMA((n,))]
    return out_shape, sems


def _call(body, *, name, grid, in_specs, out_specs, out_shape, scratch, sem, args, ride=None):
    if ride is None:
        outs = pl.pallas_call(body, name=name, grid=grid, in_specs=in_specs, out_specs=out_specs, out_shape=out_shape,
                              scratch_shapes=scratch, compiler_params=_params(sem))(*args)
        return outs, None
    arrs, gather, cols = ride
    n, n_in, n_out, n_scr = len(arrs), len(in_specs), len(out_specs), len(scratch)
    x_shape, x_sems = _exchange_shapes(arrs, gather, cols)

    def riding(*refs):
        ins, x_ins = refs[:n_in], refs[n_in:n_in + n]
        outs = refs[n_in + n:n_in + n + n_out]
        x_outs = refs[n_in + n + n_out:n_in + 2 * n + n_out]
        scr = refs[n_in + 2 * n + n_out:n_in + 2 * n + n_out + n_scr]
        sems = refs[n_in + 2 * n + n_out + n_scr:]
        def at(step):
            return functools.reduce(jnp.logical_and, [pl.program_id(a) == v for a, v in enumerate(step)])

        @pl.when(at((0,) * len(grid)))
        def _():
            _exchange_start(x_ins, x_outs, sems, gather, cols)

        @pl.when(at((grid[0] // 2,) + (0,) * (len(grid) - 1)))
        def _():
            _exchange_pass_on(x_ins, x_outs, sems, gather, cols, (0, 1))

        @pl.when(at((grid[0] // 2,) + (0,) * (len(grid) - 2) + (5 * grid[-1] // 8,)))
        def _():
            _exchange_pass_on(x_ins, x_outs, sems, gather, cols, (2,))

        body(*ins, *outs, *scr)

        @pl.when(at(tuple(g - 1 for g in grid)))
        def _():
            _exchange_wait(x_ins, x_outs, sems, gather, cols)

    res = pl.pallas_call(
        riding, name=name, grid=grid, in_specs=list(in_specs) + [ANY] * n, out_specs=list(out_specs) + [ANY] * n,
        out_shape=list(out_shape) + x_shape, scratch_shapes=list(scratch) + x_sems,
        compiler_params=_params(("arbitrary",) * len(grid)))(*args, *arrs)
    return res[:n_out], res[n_out:]


def _my_block():
    return (4 * lax.axis_index("x") + 2 * lax.axis_index("y") + lax.axis_index("c")).astype(jnp.int32).reshape(1)


def _proj_in_gather(n, w_shard):
    t, k = n.shape
    cs = w_shard.shape[1]
    tm = min(MM_ROWS, t)
    ni = t // tm
    arrival = (0, 1) + OTHER_CHIPS + tuple(m + 1 for m in OTHER_CHIPS)

    def mask_at(s):
        return jnp.where(s < 2, s, jnp.where(s < 5, 2 * (s - 1), 2 * (s - 4) + 1))

    def body(me_ref, n_ref, w_hbm, o_ref, all_hbm, w_vmem, send_sems, recv_sems, loc_sems, load_sems):
        s, i = pl.program_id(0), pl.program_id(1)
        cps = _gather_copies([w_hbm], [all_hbm], (send_sems, recv_sems, loc_sems), (cs,))
        arrived = cps["local"] + cps["from_core"][:1] + cps["from_chips"] + cps["from_core"][1:]

        def load(step):
            src = w_hbm if step == 0 else _window(all_hbm, jnp.bitwise_xor(me_ref[0], arrival[step]), cs)
            return pltpu.make_async_copy(src, w_vmem.at[step % 2], load_sems.at[step % 2])

        @pl.when(jnp.logical_and(s == 0, i == 0))
        def _():
            for cp in cps["local"] + cps["to_chips"] + cps["to_core"]:
                cp.start()
            load(0).start()

        for step, mask in enumerate(arrival):
            @pl.when(jnp.logical_and(s == step, i == 0))
            def _(step=step):
                load(step).wait()

            if step + 1 < N_DEV:
                @pl.when(jnp.logical_and(s == step, i == min(1, ni - 1)))
                def _(step=step):
                    arrived[step + 1].wait_recv()
                    if arrival[step + 1] in OTHER_CHIPS:
                        cps["pass_on"][OTHER_CHIPS.index(arrival[step + 1])].start()
                    load(step + 1).start()

        o_ref[...] = lax.dot_general(n_ref[...], w_vmem[s % 2], NN, preferred_element_type=F32).astype(BF16)

        @pl.when(jnp.logical_and(s == N_DEV - 1, i == ni - 1))
        def _():
            cps["local"][0].wait()
            for cp in cps["to_chips"] + cps["to_core"] + cps["pass_on"]:
                cp.wait_send()

    return pl.pallas_call(
        body, name="proj_in",
        grid_spec=pltpu.PrefetchScalarGridSpec(
            num_scalar_prefetch=1, grid=(N_DEV, ni),
            in_specs=[pl.BlockSpec((tm, k), lambda s, i, me: (i, 0)), ANY],
            out_specs=[pl.BlockSpec((tm, cs), lambda s, i, me: (i, jnp.bitwise_xor(me[0], mask_at(s)))), ANY],
            scratch_shapes=[pltpu.VMEM((2, k, cs), BF16), pltpu.SemaphoreType.DMA((N_DEV - 1,)),
                            pltpu.SemaphoreType.DMA((N_DEV - 1,)), pltpu.SemaphoreType.DMA((1,)),
                            pltpu.SemaphoreType.DMA((2,))]),
        out_shape=[jax.ShapeDtypeStruct((t, N_DEV * cs), BF16), jax.ShapeDtypeStruct((k, N_DEV * cs), BF16)],
        compiler_params=_params(("arbitrary", "arbitrary")),
    )(_my_block(), n, w_shard)


def _gw_in_scatter(a, g):
    t, k = a.shape
    cs = g.shape[1] // N_DEV
    tm = min(MM_ROWS, t)
    nr = t // tm
    n_chip = N_DEV // 2
    chips = (6, 4, 2, 0)

    def body(me_ref, a_ref, g_ref, out_hbm, acc, stage, other, core_send, core_recv, chip_send, chip_recv, loc_sem):
        s, r = pl.program_id(0), pl.program_id(1)
        x, y, c = lax.axis_index("x"), lax.axis_index("y"), lax.axis_index("c")
        my_chip = 2 * x + y
        part = lax.dot_general(a_ref[...], g_ref[...], TN, preferred_element_type=F32)

        def to_core(m):
            return pltpu.make_async_remote_copy(src_ref=stage.at[0], dst_ref=other.at[m], send_sem=core_send.at[m],
                                                recv_sem=core_recv.at[m], device_id=(x, y, 1 - c),
                                                device_id_type=pl.DeviceIdType.MESH)

        def to_chip(m, landed):
            mask = chips[m]
            there = (1 - x if mask & 4 else x, 1 - y if mask & 2 else y, c)
            slot = (2 * there[0] + there[1]) if landed else my_chip
            return pltpu.make_async_remote_copy(src_ref=stage.at[1], dst_ref=out_hbm.at[slot], send_sem=chip_send.at[m],
                                                recv_sem=chip_recv.at[m], device_id=there,
                                                device_id_type=pl.DeviceIdType.MESH)

        local = pltpu.make_async_copy(stage.at[1], out_hbm.at[my_chip], loc_sem)

        @pl.when(r == 0)
        def _():
            acc[...] = part

        @pl.when(r > 0)
        def _():
            acc[...] += part

        for step in range(N_DEV):
            m = step // 2

            @pl.when(jnp.logical_and(s == step, r == nr - 1))
            def _(step=step, m=m):
                if step % 2 == 0:
                    if m > 0:
                        to_core(m - 1).wait_send()
                    stage[0] = acc[...].astype(BF16)
                    to_core(m).start()
                else:
                    if m > 0:
                        to_chip(m - 1, False).wait_send()
                    to_core(m).wait_recv()
                    stage[1] = (acc[...] + other[m].astype(F32)).astype(BF16)
                    if m < n_chip - 1:
                        to_chip(m, False).start()
                    else:
                        local.start()
                        to_core(m).wait_send()
                        local.wait()
                        for mm in range(n_chip - 1):
                            to_chip(mm, True).wait_recv()

    return pl.pallas_call(
        body, name="gw_in",
        grid_spec=pltpu.PrefetchScalarGridSpec(
            num_scalar_prefetch=1, grid=(N_DEV, nr),
            in_specs=[pl.BlockSpec((tm, k), lambda s, r, me: (r, 0)),
                      pl.BlockSpec((tm, cs), lambda s, r, me: (r, jnp.bitwise_xor(me[0], N_DEV - 1 - s)))],
            out_specs=ANY,
            scratch_shapes=[pltpu.VMEM((k, cs), F32), pltpu.VMEM((2, k, cs), BF16), pltpu.VMEM((n_chip, k, cs), BF16),
                            pltpu.SemaphoreType.DMA((n_chip,)), pltpu.SemaphoreType.DMA((n_chip,)),
                            pltpu.SemaphoreType.DMA((n_chip - 1,)), pltpu.SemaphoreType.DMA((n_chip - 1,)),
                            pltpu.SemaphoreType.DMA]),
        out_shape=jax.ShapeDtypeStruct((n_chip, k, cs), BF16),
        compiler_params=_params(("arbitrary", "arbitrary")),
    )(_my_block(), a, g)


SMALL_ROWS = 8


def _allreduce_small(parts, loss_part):
    n, d = len(parts), parts[0].shape[1]

    def body(*refs):
        part_refs, loss_ref, o_ref = refs[:n], refs[n], refs[n + 1]
        mine_ref, all_ref, send_sems, recv_sems = refs[n + 2:]
        me, peers = _peers()
        mine_ref[...] = jnp.zeros_like(mine_ref)
        for i, p_ref in enumerate(part_refs):
            mine_ref[i:i + 1, :] = p_ref[...]
        mine_ref[SMALL_ROWS - 1:SMALL_ROWS, 0:LANES] = loss_ref[0:1, :]
        all_ref[me] = mine_ref[...]
        for k, (dev, idx) in enumerate(peers):
            pltpu.make_async_remote_copy(src_ref=mine_ref, dst_ref=all_ref.at[me], send_sem=send_sems.at[k],
                                         recv_sem=recv_sems.at[k], device_id=dev,
                                         device_id_type=pl.DeviceIdType.MESH).start()
        for k, (dev, idx) in enumerate(peers):
            cp = pltpu.make_async_remote_copy(src_ref=mine_ref, dst_ref=all_ref.at[idx], send_sem=send_sems.at[k],
                                              recv_sem=recv_sems.at[k], device_id=dev,
                                              device_id_type=pl.DeviceIdType.MESH)
            cp.wait_send()
            cp.wait_recv()
        tot = all_ref[0]
        for dvc in range(1, N_DEV):
            tot = tot + all_ref[dvc]
        o_ref[...] = tot

    return pl.pallas_call(
        body, name="allreduce_small", in_specs=[VMEM] * (n + 1), out_specs=VMEM,
        out_shape=jax.ShapeDtypeStruct((SMALL_ROWS, d), F32),
        scratch_shapes=[pltpu.VMEM((SMALL_ROWS, d), F32), pltpu.VMEM((N_DEV, SMALL_ROWS, d), F32),
                        pltpu.SemaphoreType.DMA((N_DEV - 1,)), pltpu.SemaphoreType.DMA((N_DEV - 1,))],
    )(*parts, loss_part)


def _adam_math(g, w, m, v):
    m_new = ADAM_B1 * m + (1.0 - ADAM_B1) * g
    v_new = ADAM_B2 * v + (1.0 - ADAM_B2) * (g * g)
    m_hat = m_new / (1.0 - ADAM_B1 ** ADAM_STEP)
    v_hat = v_new / (1.0 - ADAM_B2 ** ADAM_STEP)
    delta = -ADAM_LR * (m_hat / (jnp.sqrt(v_hat) + ADAM_EPS) + ADAM_WD * w)
    return delta, m_new, v_new


def _adam(name, pieces, w, m, v):
    r, c = w.shape
    n_piece, _, cp = pieces.shape
    tr = r
    for cand in (256, 176, 128, 64):
        if r % cand == 0 and r > cand:
            tr = cand
            break

    def body(p_ref, w_ref, m_ref, v_ref, g_ref, d_ref, mo_ref, vo_ref):
        g = p_ref[0, :, 0:c].astype(F32)
        for j in range(1, n_piece):
            g = g + p_ref[j, :, 0:c].astype(F32)
        delta, m_new, v_new = _adam_math(g, w_ref[...], m_ref[...], v_ref[...])
        g_ref[...] = g
        d_ref[...] = delta
        mo_ref[...] = m_new
        vo_ref[...] = v_new

    blk = pl.BlockSpec((tr, c), lambda i: (i, 0))
    osh = jax.ShapeDtypeStruct((r, c), F32)
    return pl.pallas_call(
        body, name=name, grid=(r // tr,),
        in_specs=[pl.BlockSpec((n_piece, tr, cp), lambda i: (0, i, 0)), blk, blk, blk],
        out_specs=[blk, blk, blk, blk], out_shape=[osh, osh, osh, osh],
        compiler_params=_params(("parallel",)),
    )(pieces, w, m, v)


def _adam_small(g_all, ws, ms, vs):
    n = len(ws)

    def body(*refs):
        g_ref, ins, outs = refs[0], refs[1:1 + 3 * n], refs[1 + 3 * n:]
        for i in range(n):
            g = g_ref[i:i + 1, :]
            delta, m_new, v_new = _adam_math(g, ins[i][...], ins[n + i][...], ins[2 * n + i][...])
            for kind, val in enumerate((g, delta, m_new, v_new)):
                outs[kind * n + i][...] = val

    osh = jax.ShapeDtypeStruct(ws[0].shape, F32)
    res = pl.pallas_call(body, name="adam_small", in_specs=[VMEM] * (1 + 3 * n), out_specs=[VMEM] * (4 * n),
                         out_shape=[osh] * (4 * n))(g_all, *ws, *ms, *vs)
    return res[:n], res[n:2 * n], res[2 * n:3 * n], res[3 * n:]


def _local_step(x, mem, pos, tgt, gains, w_in_shard, shards, batch):
    g_mix, g_mem_q, g_mem_kv, g_ffn, g_final = gains
    t, d = x.shape
    s = t // batch
    n_mem = mem.shape[0] // batch
    n_sh = N_DEV
    width = shards[0].shape[0]
    nb = width // LANES

    lane = np.arange(LANES) % HEAD_DIM
    sel_lo = (lane < ROPE_HALF).astype(np.float32)[None, :]
    sel_hi = ((lane >= ROPE_HALF) & (lane < 2 * ROPE_HALF)).astype(np.float32)[None, :]
    freqs = np.float32(ROPE_THETA) ** (-np.arange(ROPE_HALF, dtype=np.float32) / np.float32(ROPE_HALF))
    inv_freq = np.where(lane < 2 * ROPE_HALF, freqs[lane % ROPE_HALF], 0.0).astype(np.float32)[None, :]
    cos_t, sin_a, sin_b = _rope_tables(pos, jnp.asarray(inv_freq), jnp.asarray(sel_lo), jnp.asarray(sel_hi))
    bias = _dilated_bias_tiles(s)

    n1 = _rms_fwd("norm_mix", x, g_mix)
    proj, w_in = _proj_in_gather(n1, w_in_shard)
    qk_a = _rope_apply("rope_fwd", [proj], 2 * width, cos_t, sin_a, sin_b, 1.0)
    cs_up = shards[0].shape[1]
    (o_a, lse_a), (w_up_a, w_up_b, w_out, w_q, w_kv, w_o, w_fd) = _da_fwd(
        qk_a, proj, 2 * nb, bias, batch, s,
        ride=(shards[:6] + shards[8:], True, (cs_up, cs_up, 0, 0, 0, cs_up, 0)))
    (o_b, tot_b), (w_fg, w_fu) = _sb_fwd(proj, 3 * nb, 4 * nb, 5 * nb, batch, s, ride=(shards[6:8], True, (0, 0)))
    w_out = w_out.reshape(d, d)
    w_q = w_q.reshape(d, -1)
    w_kv = w_kv.reshape(d, -1)
    w_fd = w_fd.reshape(-1, d)
    w_fg = w_fg.reshape(-1, d)
    w_fu = w_fu.reshape(-1, d)
    ua, ub, mixed = _mixer_fwd(o_a, o_b, w_up_a, w_up_b, proj, 6 * nb)
    h1, n2 = _mm_res_norm("mix_out", mixed, w_out, x, g_mem_q)
    mem_n = _rms_fwd("norm_mem_kv", mem, g_mem_kv)
    q_m = _mm_w("mem_q", n2, w_q, BF16)
    kv_m = _mm_w("mem_kv", mem_n, w_kv, BF16)
    o_m = _mem_fwd(q_m, kv_m, batch, s, n_mem)
    h2, n3 = _mm_res_norm("mem_out", o_m, w_o, h1, g_ffn)
    hg, hu, act = _ffn_up(n3, w_fg, w_fu)
    h3 = _mm_w("ffn_down", act, w_fd, F32, res=h2, tm=ROW_TILE)
    loss_part, dh3, dh3_b, dg_final = _loss_head(h3, tgt, g_final.reshape(1, d))

    dhg, dhu, dn3 = _ffn_bwd(dh3_b, w_fd, w_fg, w_fu, hg, hu)
    gw_fd = _wgrad("gw_ffn_down", act, dh3_b)
    gw_fg = _wgrad("gw_ffn_gate", dhg, n3)
    gw_fu = _wgrad("gw_ffn_up", dhu, n3)
    dh2, dh2_b, dg_ffn = _rms_bwd("norm_ffn_bwd", dn3, h2, g_ffn, dh3, ("f32", "bf16"))

    do_m = _mm_w("mem_out_bwd", dh2_b, w_o, BF16, dims=NT)
    gw_o = _wgrad("gw_mem_o", o_m, dh2_b)
    dq_m, dkv_m = _mem_bwd(q_m, kv_m, do_m, batch, s, n_mem)
    gw_q = _wgrad("gw_mem_q", n2, dq_m)
    gw_kv = _wgrad("gw_mem_kv", mem_n, dkv_m)
    dn2 = _mm_w("mem_q_bwd", dq_m, w_q, F32, dims=NT)
    dmem_n = _mm_w("mem_kv_bwd", dkv_m, w_kv, F32, dims=NT)
    (dg_mem_kv,) = _rms_bwd("norm_mem_kv_bwd", dmem_n, mem, g_mem_kv, None, ())
    dh1, dh1_b, dg_mem_q = _rms_bwd("norm_mem_q_bwd", dn2, h1, g_mem_q, dh2, ("f32", "bf16"))

    gw_out = _wgrad("gw_out", mixed, dh1_b)
    dua, dub, dgates = _mixer_bwd(dh1_b, w_out, ua, ub, proj, 6 * nb)
    do_a = _mm_w("up_a_bwd", dua, w_up_a, BF16, dims=NT)
    do_b = _mm_w("up_b_bwd", dub, w_up_b, BF16, dims=NT)
    gw_ua = _wgrad("gw_up_a", o_a, dua)
    gw_ub = _wgrad("gw_up_b", o_b, dub)
    (dq_ar, dk_ar, dv_a), (p_fg, p_fd) = _da_bwd(
        qk_a, proj, 2 * nb, bias, o_a, lse_a, do_a, batch, s,
        ride=([gw_fg.reshape(n_sh, -1, d), gw_fd.reshape(n_sh, -1, d)], False, (0, 0)))
    dqk_a = _rope_apply("rope_bwd", [dq_ar, dk_ar], width, cos_t, sin_a, sin_b, -1.0)
    mid = [gw_ua, gw_ub, gw_out.reshape(n_sh, -1, d), gw_q.reshape(n_sh, -1, gw_q.shape[1]),
           gw_kv.reshape(n_sh, -1, gw_kv.shape[1]), gw_o, gw_fu.reshape(n_sh, -1, d)]
    (dq_b, dk_b, dv_b), (*p_mid, p_fu) = _sb_bwd(proj, 3 * nb, 4 * nb, 5 * nb, tot_b, do_b, batch, s,
                                                 ride=(mid, False, (cs_up, cs_up, 0, 0, 0, cs_up, 0)))
    p_ffn = [p_fg, p_fu, p_fd]
    dproj = jnp.concatenate([dqk_a, dv_a, dq_b, dk_b, dv_b, dgates], axis=1)
    dn1 = _mm_w("proj_in_bwd", dproj, w_in, F32, dims=NT, tm=ROW_TILE)
    p_in = _gw_in_scatter(n1, dproj)
    grad_x, dg_mix = _rms_bwd("norm_mix_bwd", dn1, x, g_mix, dh1, ("f32",))
    return loss_part, grad_x, [p_in] + list(p_mid) + p_ffn, (dg_mix, dg_mem_q, dg_mem_kv, dg_ffn, dg_final)


WEIGHTS =("w_in", "w_up_a", "w_up_b", "w_out", "w_q_mem", "w_kv_mem", "w_o_mem", "w_ffn_gate", "w_ffn_up", "w_ffn_down")
GAINS = ("g_mix", "g_mem_q", "g_mem_kv", "g_ffn", "g_final")
ORDER = ("g_mix", "w_in", "w_up_a", "w_up_b", "w_out", "g_mem_q", "g_mem_kv", "w_q_mem", "w_kv_mem", "w_o_mem", "g_ffn",
         "w_ffn_gate", "w_ffn_up", "w_ffn_down", "g_final")


def kernel(x, mem, positions, g_mix, w_in, w_up_a, w_up_b, w_out, g_mem_q, g_mem_kv, w_q_mem, w_kv_mem, w_o_mem, g_ffn, w_ffn_gate, w_ffn_up, w_ffn_down, g_final, loss_target, m_g_mix, m_w_in, m_w_up_a, m_w_up_b, m_w_out, m_g_mem_q, m_g_mem_kv, m_w_q_mem, m_w_kv_mem, m_w_o_mem, m_g_ffn, m_w_ffn_gate, m_w_ffn_up, m_w_ffn_down, m_g_final, v_g_mix, v_w_in, v_w_up_a, v_w_up_b, v_w_out, v_g_mem_q, v_g_mem_kv, v_w_q_mem, v_w_kv_mem, v_w_o_mem, v_g_ffn, v_w_ffn_gate, v_w_ffn_up, v_w_ffn_down, v_g_final):
    given = dict(locals())
    batch, s, d = x.shape
    t = batch * s
    flipped = ("w_ffn_gate", "w_ffn_up")

    def view(a, n):
        a = a.reshape(a.shape[-2:])
        return a.T if n in flipped else a

    def unview(a, n):
        return (a.T if n in flipped else a).reshape(given[n].shape)

    shard = {n: view(given[n], n) for n in WEIGHTS}
    gains = [given[n].reshape(1, d) for n in GAINS]

    pad = (-shard["w_ffn_down"].shape[0]) % LANES
    cast = _cast_weights([shard[n] for n in WEIGHTS], [pad if n in flipped + ("w_ffn_down",) else 0 for n in WEIGHTS])
    loss_part, grad_x, pieces, dgains = _local_step(
        x.reshape(t, d), mem.reshape(-1, d), positions.reshape(t, 1), loss_target.reshape(t, d), gains, cast[0],
        cast[1:], batch)

    grad, delta, new_m, new_v = {}, {}, {}, {}
    for n, p in zip(WEIGHTS, pieces):
        outs = _adam("adam_" + n, p, shard[n], view(given["m_" + n], n), view(given["v_" + n], n))
        grad[n], delta[n], new_m[n], new_v[n] = [unview(o, n) for o in outs]

    g_all = _allreduce_small(list(dgains), loss_part)
    small = _adam_small(g_all, gains, [given["m_" + n].reshape(1, d) for n in GAINS],
                        [given["v_" + n].reshape(1, d) for n in GAINS])
    for out, vals in zip((grad, delta, new_m, new_v), small):
        for n, val in zip(GAINS, vals):
            out[n] = val.reshape(given[n].shape)

    loss = g_all[SMALL_ROWS - 1, 0]
    return (loss, grad_x.reshape(x.shape), *[grad[n] for n in ORDER], *[delta[n] for n in ORDER],
            *[new_m[n] for n in ORDER], *[new_v[n] for n in ORDER])
```

```python
import functools
import math

import jax
import jax.numpy as jnp
import numpy as np
from jax import lax
from jax.experimental import pallas as pl
from jax.experimental.pallas import tpu as pltpu

F32 = jnp.float32
BF16 = jnp.bfloat16

N_DEV = 8
HEAD_DIM = 64
MEM_HEAD_DIM = 128
N_HEADS_MEM = 4
BLOCK = 128
DIL_PATTERNS = ((128, 1), (512, 4), (2048, 16))
ROPE_THETA = 500000.0
ROPE_HALF = 8
RMS_EPS = 1e-6
ADAM_LR, ADAM_B1, ADAM_B2, ADAM_EPS, ADAM_WD, ADAM_STEP = 0.001, 0.9, 0.999, 1e-08, 0.01, 10
NEG = -1e30
ROW_TILE = 512
LANES = 128

ANY = pl.BlockSpec(memory_space=pl.ANY)
VMEM = pl.BlockSpec(memory_space=pltpu.VMEM)
NN = (((1,), (0,)), ((), ()))
NT = (((1,), (1,)), ((), ()))
TN = (((0,), (0,)), ((), ()))


def _params(sem):
    return pltpu.CompilerParams(dimension_semantics=sem)


def _mm(name, a, b, *, grid, a_spec, b_spec, o_shape, o_spec, dims, out_dtype, nk=1):
    def body(*refs):
        a_ref, b_ref, o_ref = refs[0], refs[1], refs[2]
        p = lax.dot_general(a_ref[...], b_ref[...], dims, preferred_element_type=F32)
        if nk == 1:
            o_ref[...] = p.astype(out_dtype)
            return
        acc_ref = refs[-1]
        k = pl.program_id(len(grid) - 1)

        @pl.when(k == 0)
        def _():
            acc_ref[...] = p

        @pl.when(k > 0)
        def _():
            acc_ref[...] += p

        @pl.when(k == nk - 1)
        def _():
            o_ref[...] = acc_ref[...].astype(out_dtype)

    o_block = tuple(d for d in o_spec.block_shape if d is not None)
    sem = ("parallel",) * (len(grid) - 1) + (("arbitrary",) if nk > 1 else ("parallel",))
    return pl.pallas_call(
        body, name=name, grid=grid, in_specs=[a_spec, b_spec],
        out_specs=o_spec, out_shape=jax.ShapeDtypeStruct(o_shape, out_dtype),
        scratch_shapes=[pltpu.VMEM(o_block, F32)] if nk > 1 else [],
        compiler_params=_params(sem),
    )(a, b)


def _rms_fwd(name, x, g):
    t, d = x.shape
    tm = min(ROW_TILE, t)

    def body(x_ref, g_ref, o_ref):
        xf = x_ref[...]
        r = lax.rsqrt(jnp.mean(xf * xf, axis=-1, keepdims=True) + RMS_EPS)
        o_ref[...] = (xf * r * g_ref[...]).astype(BF16)

    return pl.pallas_call(
        body, name=name, grid=(t // tm,),
        in_specs=[pl.BlockSpec((tm, d), lambda i: (i, 0)), pl.BlockSpec((1, d), lambda i: (0, 0))],
        out_specs=pl.BlockSpec((tm, d), lambda i: (i, 0)), out_shape=jax.ShapeDtypeStruct((t, d), BF16),
        compiler_params=_params(("parallel",)),
    )(x, g)


def _rms_bwd_rows(dnf, xf, gv, res):
    r = lax.rsqrt(jnp.mean(xf * xf, axis=-1, keepdims=True) + RMS_EPS)
    xh = xf * r
    dxh = dnf * gv
    dx = r * (dxh - xh * jnp.mean(dxh * xh, axis=-1, keepdims=True))
    if res is not None:
        dx = dx + res
    return dx, jnp.sum(dnf * xh, axis=0, keepdims=True)


def _rms_bwd(name, dn, x, g, dres, want):
    t, d = x.shape
    tm = min(ROW_TILE, t)
    has_res = dres is not None
    lhs = list(dn) if isinstance(dn, tuple) else [dn]
    n_lhs = len(lhs[:2])

    def body(*refs):
        x_ref, g_ref = refs[n_lhs], refs[n_lhs + 1]
        r_ref = refs[n_lhs + 2] if has_res else None
        dx_refs, dg_ref = refs[-1 - len(want):-1], refs[-1]
        if n_lhs == 2:
            dnf = lax.dot_general(refs[0][...], refs[1][...], lhs[2], preferred_element_type=F32)
        else:
            dnf = refs[0][...].astype(F32)
        dx, dg = _rms_bwd_rows(dnf, x_ref[...], g_ref[...], r_ref[...] if has_res else None)
        for kind, dx_ref in zip(want, dx_refs):
            dx_ref[...] = dx.astype(F32 if kind == "f32" else BF16)

        @pl.when(pl.program_id(0) == 0)
        def _():
            dg_ref[...] = jnp.zeros_like(dg_ref)

        dg_ref[...] += dg

    row = pl.BlockSpec((tm, d), lambda i: (i, 0))
    vec = pl.BlockSpec((1, d), lambda i: (0, 0))
    if n_lhs == 2:
        first = [pl.BlockSpec((tm, lhs[0].shape[1]), lambda i: (i, 0)), pl.BlockSpec(lhs[1].shape, lambda i: (0, 0))]
    else:
        first = [row]
    return pl.pallas_call(
        body, name=name, grid=(t // tm,),
        in_specs=first + [row, vec] + ([row] if has_res else []),
        out_specs=[row] * len(want) + [vec],
        out_shape=[jax.ShapeDtypeStruct((t, d), F32 if kind == "f32" else BF16) for kind in want]
        + [jax.ShapeDtypeStruct((1, d), F32)],
        compiler_params=_params(("arbitrary",)),
    )(*(lhs[:2] + [x, g] + ([dres] if has_res else [])))


def _loss_head(a, w, res, tgt, g):
    t, d = res.shape
    k = a.shape[1]
    tm = min(ROW_TILE, t)

    def body(a_ref, w_ref, r_ref, t_ref, g_ref, loss_ref, dh_ref, dhb_ref, dg_ref):
        xf = lax.dot_general(a_ref[...], w_ref[...], NN, preferred_element_type=F32) + r_ref[...]
        gv = g_ref[...]
        r = lax.rsqrt(jnp.mean(xf * xf, axis=-1, keepdims=True) + RMS_EPS)
        xh = xf * r
        e = xh * gv - t_ref[...]
        dy = e * (1.0 / d)
        dxh = dy * gv
        dh = r * (dxh - xh * jnp.mean(dxh * xh, axis=-1, keepdims=True))
        dh_ref[...] = dh
        dhb_ref[...] = dh.astype(BF16)

        @pl.when(pl.program_id(0) == 0)
        def _():
            dg_ref[...] = jnp.zeros_like(dg_ref)
            loss_ref[...] = jnp.zeros_like(loss_ref)

        dg_ref[...] += jnp.sum(dy * xh, axis=0, keepdims=True)
        part = jnp.sum(jnp.sum(e * e, axis=1, keepdims=True), axis=0, keepdims=True) * (0.5 / d)
        loss_ref[...] += jnp.broadcast_to(part, loss_ref.shape)

    row = pl.BlockSpec((tm, d), lambda i: (i, 0))
    vec = pl.BlockSpec((1, d), lambda i: (0, 0))
    return pl.pallas_call(
        body, name="loss_head", grid=(t // tm,),
        in_specs=[pl.BlockSpec((tm, k), lambda i: (i, 0)), pl.BlockSpec((k, d), lambda i: (0, 0)), row, row, vec],
        out_specs=[pl.BlockSpec((8, LANES), lambda i: (0, 0)), row, row, vec],
        out_shape=[jax.ShapeDtypeStruct((8, LANES), F32), jax.ShapeDtypeStruct((t, d), F32),
                   jax.ShapeDtypeStruct((t, d), BF16), jax.ShapeDtypeStruct((1, d), F32)],
        compiler_params=_params(("arbitrary",)),
    )(a, w, res, tgt, g)


def _rope_tables(pos, inv_freq, sel_lo, sel_hi):
    t = pos.shape[0]
    tm = min(ROW_TILE, t)

    def body(p_ref, f_ref, lo_ref, hi_ref, c_ref, sa_ref, sb_ref):
        ang = p_ref[...].astype(F32) * f_ref[...]
        rot = lo_ref[...] + hi_ref[...]
        cs, sn = jnp.cos(ang), jnp.sin(ang)
        c_ref[...] = cs * rot + (1.0 - rot)
        sa_ref[...] = -sn * lo_ref[...]
        sb_ref[...] = sn * hi_ref[...]

    vec = pl.BlockSpec((1, LANES), lambda i: (0, 0))
    row = pl.BlockSpec((tm, LANES), lambda i: (i, 0))
    return pl.pallas_call(
        body, name="rope_tables", grid=(t // tm,),
        in_specs=[pl.BlockSpec((tm, 1), lambda i: (i, 0)), vec, vec, vec],
        out_specs=[row, row, row], out_shape=[jax.ShapeDtypeStruct((t, LANES), F32)] * 3,
        compiler_params=_params(("parallel",)),
    )(pos, inv_freq, sel_lo, sel_hi)


def _rope_apply(name, srcs, width, cos_t, sin_a, sin_b, sign):
    t = srcs[0].shape[0]
    tm = min(ROW_TILE, t)
    n_cols = width // LANES

    def body(*refs):
        x_refs, (c_ref, sa_ref, sb_ref, o_ref) = refs[:len(srcs)], refs[len(srcs):]
        cs, sa, sb = c_ref[...], sign * sa_ref[...], sign * sb_ref[...]
        for a, x_ref in enumerate(x_refs):
            for c in range(n_cols):
                xf = x_ref[:, c * LANES:(c + 1) * LANES].astype(F32)
                up = pltpu.roll(xf, LANES - ROPE_HALF, 1)
                dn = pltpu.roll(xf, ROPE_HALF, 1)
                o_ref[:, a * width + c * LANES:a * width + (c + 1) * LANES] = (xf * cs + up * sa + dn * sb).astype(BF16)

    wide = len(srcs) * width
    tab = pl.BlockSpec((tm, LANES), lambda i: (i, 0))
    return pl.pallas_call(
        body, name=name, grid=(t // tm,),
        in_specs=[pl.BlockSpec((tm, width), lambda i: (i, 0))] * len(srcs) + [tab, tab, tab],
        out_specs=pl.BlockSpec((tm, wide), lambda i: (i, 0)),
        out_shape=jax.ShapeDtypeStruct((t, wide), BF16),
        compiler_params=_params(("parallel",)),
    )(*srcs, cos_t, sin_a, sin_b)


DA_T = 256
FWD_STREAMS = 4
BWD_STREAMS = 2


def _lane_lo():
    return lax.broadcasted_iota(jnp.int32, (BLOCK, LANES), 1) < HEAD_DIM


def _dilated_bias_tiles(s):
    n = s // DA_T
    dist = (np.arange(n)[:, None, None] * DA_T + np.arange(DA_T)[None, :, None] - np.arange(DA_T)[None, None, :])
    cnt = np.zeros(dist.shape, np.float32)
    for window, dil in DIL_PATTERNS:
        cnt += ((dist >= 0) & (dist % dil == 0) & (dist <= window)).astype(np.float32)
    return jnp.asarray(np.where(cnt > 0, np.log(np.maximum(cnt, 1.0)), NEG).astype(np.float32))


def _stack_heads(x, lo):
    zero = jnp.zeros_like(x)
    return jnp.concatenate([jnp.where(lo, x, zero), jnp.where(lo, zero, x)], axis=0)


def _da_fwd(qk, proj, v_col0, bias, batch, s, ride=None, streams=FWD_STREAMS):
    t = qk.shape[0]
    nq = s // DA_T
    n_pairs = 4
    ns = streams
    wide = ns * LANES
    scale = HEAD_DIM ** -0.5

    def body(q_ref, k_ref, v_ref, b_ref, o_ref, lse_ref, acc_ref, m_ref, l_ref):
        i = pl.program_id(2)
        lo = lax.broadcasted_iota(jnp.int32, (DA_T, LANES), 1) < HEAD_DIM
        ones = jnp.ones((DA_T, LANES), BF16)
        acc_ref[...] = jnp.zeros_like(acc_ref)
        m_ref[...] = jnp.full(m_ref.shape, NEG, F32)
        l_ref[...] = jnp.zeros_like(l_ref)
        qqs = [_stack_heads(q_ref[:, st * LANES:(st + 1) * LANES] * scale, lo) for st in range(ns)]

        def scores(st, rows, bias2):
            k = k_ref[rows, st * LANES:(st + 1) * LANES]
            return lax.dot_general(qqs[st], k, NT, preferred_element_type=F32) + bias2

        def softmax(st, sc):
            m_old = m_ref[st]
            m_new = jnp.maximum(m_old, jnp.max(sc, axis=1, keepdims=True))
            m_ref[st] = m_new
            return jnp.exp(sc - m_new).astype(BF16), jnp.exp(m_old - m_new)

        def values(st, rows, p, alpha):
            v = v_ref[rows, st * LANES:(st + 1) * LANES]
            vz = jnp.zeros_like(v)
            l_ref[st] = alpha * l_ref[st] + lax.dot_general(p, ones, NN, preferred_element_type=F32)
            pv = (lax.dot_general(p[:DA_T], jnp.where(lo, v, vz), NN, preferred_element_type=F32)
                  + lax.dot_general(p[DA_T:], jnp.where(lo, vz, v), NN, preferred_element_type=F32))
            acc_ref[st] = acc_ref[st] * jnp.where(lo, alpha[:DA_T], alpha[DA_T:]) + pv

        def trip(dlt, carry):
            rows = pl.ds(pl.multiple_of((i - dlt) * DA_T, DA_T), DA_T)
            bias_t = b_ref[dlt]
            bias2 = jnp.concatenate([bias_t, bias_t], axis=0)
            scs = [scores(st, rows, bias2) for st in range(ns)]
            pas = [softmax(st, scs[st]) for st in range(ns)]
            for st in range(ns):
                values(st, rows, *pas[st])
            return carry

        lax.fori_loop(0, i + 1, trip, 0)
        for st in range(ns):
            cols = slice(st * LANES, (st + 1) * LANES)
            l_t = l_ref[st]
            o_ref[:, cols] = (acc_ref[st] / jnp.where(lo, l_t[:DA_T], l_t[DA_T:])).astype(BF16)
            lse = m_ref[st] + jnp.log(l_t)
            lse_ref[:, cols] = jnp.where(lo, lse[:DA_T], lse[DA_T:])

    blk = pl.BlockSpec((DA_T, wide), lambda b, h, i: (b * nq + i, h))
    return _call(
        body, name="attn_a_fwd", grid=(batch, n_pairs // ns, nq),
        in_specs=[blk,
                  pl.BlockSpec((s, wide), lambda b, h, i: (b, n_pairs // ns + h)),
                  pl.BlockSpec((s, wide), lambda b, h, i: (b, v_col0 // ns + h)),
                  pl.BlockSpec((nq, DA_T, DA_T), lambda b, h, i: (0, 0, 0))],
        out_specs=[blk, blk],
        out_shape=[jax.ShapeDtypeStruct((t, n_pairs * LANES), BF16), jax.ShapeDtypeStruct((t, n_pairs * LANES), F32)],
        scratch=[pltpu.VMEM((ns, DA_T, LANES), F32), pltpu.VMEM((ns, 2 * DA_T, 1), F32),
                 pltpu.VMEM((ns, 2 * DA_T, LANES), F32)],
        sem=("parallel", "parallel", "arbitrary"), args=(qk, qk, proj, bias), ride=ride)


def _da_bwd(qk, proj, v_col0, bias, o, lse, do, batch, s, ride=None, streams=BWD_STREAMS):
    t = qk.shape[0]
    nq = s // DA_T
    n_pairs = 4
    ns = streams
    wide = ns * LANES
    scale = HEAD_DIM ** -0.5

    def body(q_ref, k_ref, v_ref, b_ref, o_ref, lse_ref, do_ref, dq_ref, dk_ref, dv_ref, dk_acc, dv_acc, dq_acc):
        i = pl.program_id(2)
        lo = lax.broadcasted_iota(jnp.int32, (DA_T, LANES), 1) < HEAD_DIM

        @pl.when(i == 0)
        def _():
            dk_acc[...] = jnp.zeros_like(dk_acc)
            dv_acc[...] = jnp.zeros_like(dv_acc)

        dq_acc[...] = jnp.zeros_like(dq_acc)
        qqs, dds, deltas, lses = [], [], [], []
        for st in range(ns):
            cols = slice(st * LANES, (st + 1) * LANES)
            do_ = do_ref[:, cols]
            qqs.append(_stack_heads(q_ref[:, cols] * scale, lo))
            dds.append(_stack_heads(do_, lo))
            prod = do_.astype(F32) * o_ref[:, cols].astype(F32)
            fz = jnp.zeros_like(prod)
            deltas.append(jnp.concatenate([jnp.sum(jnp.where(lo, prod, fz), axis=1, keepdims=True),
                                           jnp.sum(jnp.where(lo, fz, prod), axis=1, keepdims=True)], axis=0))
            lse_t = lse_ref[:, cols]
            lses.append(jnp.concatenate([lse_t[:, 0:1], lse_t[:, HEAD_DIM:HEAD_DIM + 1]], axis=0))

        def products(st, rows, bias2):
            cols = slice(st * LANES, (st + 1) * LANES)
            sc = lax.dot_general(qqs[st], k_ref[rows, cols], NT, preferred_element_type=F32) + bias2
            return sc, lax.dot_general(dds[st], v_ref[rows, cols], NT, preferred_element_type=F32)

        def weights(st, sc, dp):
            p = jnp.exp(sc - lses[st])
            return (p * (dp - deltas[st])).astype(BF16), p.astype(BF16)

        def gradients(st, rows, ds, p):
            cols = slice(st * LANES, (st + 1) * LANES)
            k = k_ref[rows, cols]
            kz = jnp.zeros_like(k)
            dq_acc[st] += (lax.dot_general(ds[:DA_T], jnp.where(lo, k, kz), NN, preferred_element_type=F32)
                           + lax.dot_general(ds[DA_T:], jnp.where(lo, kz, k), NN, preferred_element_type=F32))
            dk_acc[rows, cols] += lax.dot_general(ds, qqs[st], TN, preferred_element_type=F32)
            dv_acc[rows, cols] += lax.dot_general(p, dds[st], TN, preferred_element_type=F32)

        def trip(dlt, carry):
            rows = pl.ds(pl.multiple_of((i - dlt) * DA_T, DA_T), DA_T)
            bias_t = b_ref[dlt]
            bias2 = jnp.concatenate([bias_t, bias_t], axis=0)
            prods = [products(st, rows, bias2) for st in range(ns)]
            wts = [weights(st, *prods[st]) for st in range(ns)]
            for st in range(ns):
                gradients(st, rows, *wts[st])
            return carry

        lax.fori_loop(0, i + 1, trip, 0)
        for st in range(ns):
            dq_ref[:, st * LANES:(st + 1) * LANES] = (dq_acc[st] * scale).astype(BF16)

        @pl.when(i == nq - 1)
        def _():
            dk_ref[...] = dk_acc[...].astype(BF16)
            dv_ref[...] = dv_acc[...].astype(BF16)

    blk = pl.BlockSpec((DA_T, wide), lambda b, h, i: (b * nq + i, h))
    seq = pl.BlockSpec((s, wide), lambda b, h, i: (b, h))
    out = jax.ShapeDtypeStruct((t, n_pairs * LANES), BF16)
    return _call(
        body, name="attn_a_bwd", grid=(batch, n_pairs // ns, nq),
        in_specs=[blk,
                  pl.BlockSpec((s, wide), lambda b, h, i: (b, n_pairs // ns + h)),
                  pl.BlockSpec((s, wide), lambda b, h, i: (b, v_col0 // ns + h)),
                  pl.BlockSpec((nq, DA_T, DA_T), lambda b, h, i: (0, 0, 0)),
                  blk, blk, blk],
        out_specs=[blk, seq, seq], out_shape=[out, out, out],
        scratch=[pltpu.VMEM((s, wide), F32), pltpu.VMEM((s, wide), F32), pltpu.VMEM((ns, DA_T, LANES), F32)],
        sem=("parallel", "parallel", "arbitrary"), args=(qk, qk, proj, bias, o, lse, do), ride=ride)


SB_Q = 256


def _sb_consts(after):
    r = lax.broadcasted_iota(jnp.int32, (2 * BLOCK, 2 * BLOCK), 0) % BLOCK
    c = lax.broadcasted_iota(jnp.int32, (2 * BLOCK, 2 * BLOCK), 1)
    tri = (r > c) if after else (r < c)
    return jnp.logical_or(c >= BLOCK, tri).astype(BF16)


def _split(x):
    hi = x.astype(BF16)
    lo = (x - hi.astype(F32)).astype(BF16)
    return jnp.concatenate([hi, lo], axis=1)


def _sb_fwd(proj, q_col0, k_col0, v_col0, batch, s, ride=None, streams=FWD_STREAMS):
    t = proj.shape[0]
    nq = s // SB_Q
    n_pairs = 4
    ns = streams
    wide = ns * LANES
    scale = HEAD_DIM ** -0.5

    def body(q_ref, k_ref, v_ref, o_ref, tot_ref, acc_ref, run_ref):
        i = pl.program_id(2)
        lo_q = lax.broadcasted_iota(jnp.int32, (SB_Q, LANES), 1) < HEAD_DIM
        lo_k = _lane_lo()
        mat = _sb_consts(True)
        row = lax.broadcasted_iota(jnp.int32, (2 * SB_Q, LANES), 0) % SB_Q
        ahead = row - lax.broadcasted_iota(jnp.int32, (2 * SB_Q, LANES), 1)
        acc_ref[...] = jnp.zeros_like(acc_ref)
        run_ref[...] = jnp.zeros_like(run_ref)
        qqs = [_stack_heads(q_ref[:, st * LANES:(st + 1) * LANES] * scale, lo_q) for st in range(ns)]

        def units(todo):
            def rows(j):
                return pl.ds(pl.multiple_of(j * BLOCK, BLOCK), BLOCK)

            zs = [lax.dot_general(qqs[st], k_ref[rows(j), st * LANES:(st + 1) * LANES], NT, preferred_element_type=F32)
                  for st, j, _ in todo]
            logs = []
            for z, (_, _, off) in zip(zs, todo):
                lsig = jnp.minimum(z, 0.0) - jnp.log(1.0 + jnp.exp(-jnp.abs(z)))
                lneg = lsig - z
                if off is not None:
                    lneg = jnp.where(ahead > off, lneg, 0.0)
                logs.append((lsig, _split(lneg)))
            sums = [lax.dot_general(cat, mat, NN, preferred_element_type=F32) for _, cat in logs]
            probs = []
            for (lsig, _), sm, (st, _, off) in zip(logs, sums, todo):
                run = run_ref[st]
                a = jnp.exp(lsig + run + sm[:, :BLOCK])
                if off is not None:
                    a = jnp.where(ahead > off, a, 0.0)
                run_ref[st] = run + sm[:, BLOCK:]
                probs.append(a.astype(BF16))
            for ab, (st, j, _) in zip(probs, todo):
                v = v_ref[rows(j), st * LANES:(st + 1) * LANES]
                vz = jnp.zeros_like(v)
                acc_ref[st] += (lax.dot_general(ab[:SB_Q], jnp.where(lo_k, v, vz), NN, preferred_element_type=F32)
                                + lax.dot_general(ab[SB_Q:], jnp.where(lo_k, vz, v), NN, preferred_element_type=F32))

        units([(st, 2 * i + 1, BLOCK) for st in range(ns)] + [(st, 2 * i, 0) for st in range(ns)])

        def pair(p, carry):
            jp = i - 1 - p
            units([(st, 2 * jp + 1, None) for st in range(ns)] + [(st, 2 * jp, None) for st in range(ns)])
            return carry

        lax.fori_loop(0, i, pair, 0)
        for st in range(ns):
            cols = slice(st * LANES, (st + 1) * LANES)
            o_ref[:, cols] = acc_ref[st].astype(BF16)
            tot_ref[:, cols] = jnp.where(lo_q, run_ref[st, 0:SB_Q, :], run_ref[st, SB_Q:2 * SB_Q, :])

    def seq(col0):
        return pl.BlockSpec((s, wide), lambda b, h, i: (b, col0 // ns + h))

    blk = pl.BlockSpec((SB_Q, wide), lambda b, h, i: (b * nq + i, h))
    return _call(
        body, name="attn_b_fwd", grid=(batch, n_pairs // ns, nq),
        in_specs=[pl.BlockSpec((SB_Q, wide), lambda b, h, i: (b * nq + i, q_col0 // ns + h)), seq(k_col0), seq(v_col0)],
        out_specs=[blk, blk],
        out_shape=[jax.ShapeDtypeStruct((t, n_pairs * LANES), BF16), jax.ShapeDtypeStruct((t, n_pairs * LANES), F32)],
        scratch=[pltpu.VMEM((ns, SB_Q, LANES), F32), pltpu.VMEM((ns, 2 * SB_Q, LANES), F32)],
        sem=("parallel", "parallel", "arbitrary"), args=(proj, proj, proj), ride=ride)


def _sb_bwd(proj, q_col0, k_col0, v_col0, tot, do, batch, s, ride=None, streams=BWD_STREAMS):
    t = proj.shape[0]
    nq = s // SB_Q
    n_pairs = 4
    ns = streams
    wide = ns * LANES
    scale = HEAD_DIM ** -0.5

    def body(q_ref, k_ref, v_ref, tot_ref, do_ref, dq_ref, dk_ref, dv_ref, dk_acc, dv_acc, dq_acc, seen_ref, gsum_ref):
        i = pl.program_id(2)
        lo_q = lax.broadcasted_iota(jnp.int32, (SB_Q, LANES), 1) < HEAD_DIM
        lo_k = _lane_lo()

        @pl.when(i == 0)
        def _():
            dk_acc[...] = jnp.zeros_like(dk_acc)
            dv_acc[...] = jnp.zeros_like(dv_acc)

        mat_after = _sb_consts(True)
        mat_before = _sb_consts(False)
        row = lax.broadcasted_iota(jnp.int32, (2 * SB_Q, LANES), 0) % SB_Q
        ahead = row - lax.broadcasted_iota(jnp.int32, (2 * SB_Q, LANES), 1)
        dq_acc[...] = jnp.zeros_like(dq_acc)
        seen_ref[...] = jnp.zeros_like(seen_ref)
        gsum_ref[...] = jnp.zeros_like(gsum_ref)
        qqs, dds, totals = [], [], []
        for st in range(ns):
            cols = slice(st * LANES, (st + 1) * LANES)
            qqs.append(_stack_heads(q_ref[:, cols] * scale, lo_q))
            dds.append(_stack_heads(do_ref[:, cols], lo_q))
            tot_t = tot_ref[:, cols]
            totals.append(jnp.concatenate([jnp.broadcast_to(tot_t[:, 0:1], (SB_Q, LANES)),
                                           jnp.broadcast_to(tot_t[:, HEAD_DIM:HEAD_DIM + 1], (SB_Q, LANES))], axis=0))

        def units(todo):
            def rows(j):
                return pl.ds(pl.multiple_of(j * BLOCK, BLOCK), BLOCK)

            def cols(st):
                return slice(st * LANES, (st + 1) * LANES)

            prods = [(lax.dot_general(qqs[st], k_ref[rows(j), cols(st)], NT, preferred_element_type=F32),
                      lax.dot_general(dds[st], v_ref[rows(j), cols(st)], NT, preferred_element_type=F32))
                     for st, j, _ in todo]
            logs = []
            for (z, _), (_, _, off) in zip(prods, todo):
                lsig = jnp.minimum(z, 0.0) - jnp.log(1.0 + jnp.exp(-jnp.abs(z)))
                lneg = lsig - z
                if off is not None:
                    lneg = jnp.where(ahead > off, lneg, 0.0)
                logs.append((lsig, _split(lneg)))
            sums = [lax.dot_general(cat, mat_after, NN, preferred_element_type=F32) for _, cat in logs]
            gates = []
            for (lsig, _), sm, (_, da), (st, _, off) in zip(logs, sums, prods, todo):
                seen = seen_ref[st]
                a = jnp.exp(lsig + (totals[st] - seen - sm[:, BLOCK:]) + sm[:, :BLOCK])
                if off is not None:
                    a = jnp.where(ahead > off, a, 0.0)
                seen_ref[st] = seen + sm[:, BLOCK:]
                g = a * da
                gates.append((a.astype(BF16), g, _split(g)))
            gsums = [lax.dot_general(cat, mat_before, NN, preferred_element_type=F32) for _, _, cat in gates]
            outs = []
            for (lsig, _), (ab, g, _), gs, (st, _, off) in zip(logs, gates, gsums, todo):
                gsum = gsum_ref[st]
                dz = g - jnp.exp(lsig) * (g + gsum + gs[:, :BLOCK])
                if off is not None:
                    dz = jnp.where(ahead > off, dz, 0.0)
                gsum_ref[st] = gsum + gs[:, BLOCK:]
                outs.append((dz.astype(BF16), ab))
            for (dzb, ab), (st, j, _) in zip(outs, todo):
                k = k_ref[rows(j), cols(st)]
                kz = jnp.zeros_like(k)
                dq_acc[st] += (lax.dot_general(dzb[:SB_Q], jnp.where(lo_k, k, kz), NN, preferred_element_type=F32)
                               + lax.dot_general(dzb[SB_Q:], jnp.where(lo_k, kz, k), NN, preferred_element_type=F32))
                dk_acc[rows(j), cols(st)] += lax.dot_general(dzb, qqs[st], TN, preferred_element_type=F32)
                dv_acc[rows(j), cols(st)] += lax.dot_general(ab, dds[st], TN, preferred_element_type=F32)

        def pair(p, carry):
            units([(st, 2 * p, None) for st in range(ns)] + [(st, 2 * p + 1, None) for st in range(ns)])
            return carry

        lax.fori_loop(0, i, pair, 0)
        units([(st, 2 * i, 0) for st in range(ns)] + [(st, 2 * i + 1, BLOCK) for st in range(ns)])
        for st in range(ns):
            dq_ref[:, st * LANES:(st + 1) * LANES] = (dq_acc[st] * scale).astype(BF16)

        @pl.when(i == nq - 1)
        def _():
            dk_ref[...] = dk_acc[...].astype(BF16)
            dv_ref[...] = dv_acc[...].astype(BF16)

    def seq_in(col0):
        return pl.BlockSpec((s, wide), lambda b, h, i: (b, col0 // ns + h))

    blk = pl.BlockSpec((SB_Q, wide), lambda b, h, i: (b * nq + i, h))
    seq = pl.BlockSpec((s, wide), lambda b, h, i: (b, h))
    out = jax.ShapeDtypeStruct((t, n_pairs * LANES), BF16)
    return _call(
        body, name="attn_b_bwd", grid=(batch, n_pairs // ns, nq),
        in_specs=[pl.BlockSpec((SB_Q, wide), lambda b, h, i: (b * nq + i, q_col0 // ns + h)), seq_in(k_col0),
                  seq_in(v_col0), blk, blk],
        out_specs=[blk, seq, seq], out_shape=[out, out, out],
        scratch=[pltpu.VMEM((s, wide), F32), pltpu.VMEM((s, wide), F32), pltpu.VMEM((ns, SB_Q, LANES), F32),
                 pltpu.VMEM((ns, 2 * SB_Q, LANES), F32), pltpu.VMEM((ns, 2 * SB_Q, LANES), F32)],
        sem=("parallel", "parallel", "arbitrary"), args=(proj, proj, proj, tot, do), ride=ride)


MEM_Q_TILE = 512


def _mem_fwd(q, kv, batch, s, n_mem):
    t, width = q.shape
    tq = min(MEM_Q_TILE, s)
    nq = s // tq
    scale = MEM_HEAD_DIM ** -0.5

    def body(q_ref, kv_ref, o_ref):
        for h in range(N_HEADS_MEM):
            cols = slice(h * MEM_HEAD_DIM, (h + 1) * MEM_HEAD_DIM)
            k = kv_ref[:, cols]
            v = kv_ref[:, width + h * MEM_HEAD_DIM: width + (h + 1) * MEM_HEAD_DIM]
            sc = lax.dot_general(q_ref[:, cols], k, NT, preferred_element_type=F32) * scale
            p = jnp.exp(sc - jnp.max(sc, axis=1, keepdims=True))
            p = p / jnp.sum(p, axis=1, keepdims=True)
            o_ref[:, cols] = lax.dot_general(p.astype(BF16), v, NN, preferred_element_type=F32).astype(BF16)

    return pl.pallas_call(
        body, name="mem_attn_fwd", grid=(batch, nq),
        in_specs=[pl.BlockSpec((tq, width), lambda b, i: (b * nq + i, 0)),
                  pl.BlockSpec((n_mem, 2 * width), lambda b, i: (b, 0))],
        out_specs=pl.BlockSpec((tq, width), lambda b, i: (b * nq + i, 0)),
        out_shape=jax.ShapeDtypeStruct((t, width), BF16),
        compiler_params=_params(("parallel", "parallel")),
    )(q, kv)


def _mem_bwd(q, kv, do, batch, s, n_mem):
    t, width = q.shape
    tq = min(MEM_Q_TILE, s)
    nq = s // tq
    scale = MEM_HEAD_DIM ** -0.5

    def body(q_ref, kv_ref, do_ref, dq_ref, dkv_ref, acc):
        i = pl.program_id(1)

        @pl.when(i == 0)
        def _():
            acc[...] = jnp.zeros_like(acc)

        for h in range(N_HEADS_MEM):
            cols = slice(h * MEM_HEAD_DIM, (h + 1) * MEM_HEAD_DIM)
            vcols = slice(width + h * MEM_HEAD_DIM, width + (h + 1) * MEM_HEAD_DIM)
            qh, k, v, doh = q_ref[:, cols], kv_ref[:, cols], kv_ref[:, vcols], do_ref[:, cols]
            sc = lax.dot_general(qh, k, NT, preferred_element_type=F32) * scale
            p = jnp.exp(sc - jnp.max(sc, axis=1, keepdims=True))
            p = p / jnp.sum(p, axis=1, keepdims=True)
            dp = lax.dot_general(doh, v, NT, preferred_element_type=F32)
            ds = (p * (dp - jnp.sum(p * dp, axis=1, keepdims=True)) * scale).astype(BF16)
            dq_ref[:, cols] = lax.dot_general(ds, k, NN, preferred_element_type=F32).astype(BF16)
            acc[:, cols] += lax.dot_general(ds, qh, TN, preferred_element_type=F32)
            acc[:, vcols] += lax.dot_general(p.astype(BF16), doh, TN, preferred_element_type=F32)

        @pl.when(i == nq - 1)
        def _():
            dkv_ref[...] = acc[...].astype(BF16)

    row = pl.BlockSpec((tq, width), lambda b, i: (b * nq + i, 0))
    kvs = pl.BlockSpec((n_mem, 2 * width), lambda b, i: (b, 0))
    return pl.pallas_call(
        body, name="mem_attn_bwd", grid=(batch, nq),
        in_specs=[row, kvs, row], out_specs=[row, kvs],
        out_shape=[jax.ShapeDtypeStruct((t, width), BF16), jax.ShapeDtypeStruct((batch * n_mem, 2 * width), BF16)],
        scratch_shapes=[pltpu.VMEM((n_mem, 2 * width), F32)],
        compiler_params=_params(("parallel", "arbitrary")),
    )(q, kv, do)


def _mixer_fwd(o_a, o_b, w_a, w_b, proj, gate_col0):
    t, width = o_a.shape
    d = w_a.shape[1]
    tm = min(ROW_TILE, t)
    gb0 = gate_col0 * LANES // d

    def body(oa_ref, ob_ref, wa_ref, wb_ref, ga_ref, gb_ref, ua_ref, ub_ref, mix_ref):
        ua = lax.dot_general(oa_ref[...], wa_ref[...], NN, preferred_element_type=F32)
        ub = lax.dot_general(ob_ref[...], wb_ref[...], NN, preferred_element_type=F32)
        ua_ref[...] = ua.astype(BF16)
        ub_ref[...] = ub.astype(BF16)
        mix_ref[...] = (jax.nn.sigmoid(ga_ref[...].astype(F32)) * ua
                        + jax.nn.sigmoid(gb_ref[...].astype(F32)) * ub).astype(BF16)

    row = pl.BlockSpec((tm, width), lambda i: (i, 0))
    wsp = pl.BlockSpec((width, d), lambda i: (0, 0))
    out = pl.BlockSpec((tm, d), lambda i: (i, 0))
    osh = jax.ShapeDtypeStruct((t, d), BF16)
    return pl.pallas_call(
        body, name="mixer_fwd", grid=(t // tm,),
        in_specs=[row, row, wsp, wsp,
                  pl.BlockSpec((tm, d), lambda i: (i, gb0)), pl.BlockSpec((tm, d), lambda i: (i, gb0 + 1))],
        out_specs=[out, out, out], out_shape=[osh, osh, osh],
        compiler_params=_params(("parallel",)),
    )(o_a, o_b, w_a, w_b, proj, proj)


def _mixer_bwd(dh, w_out, ua, ub, proj, gate_col0):
    t, d = dh.shape
    tm = min(ROW_TILE, t)
    nc = d // LANES

    def body(dh_ref, w_ref, ua_ref, ub_ref, ga_ref, gb_ref, dua_ref, dub_ref, dg_ref):
        dm = lax.dot_general(dh_ref[...], w_ref[...], NT, preferred_element_type=F32)
        sa = jax.nn.sigmoid(ga_ref[...].astype(F32))
        sb = jax.nn.sigmoid(gb_ref[...].astype(F32))
        dua_ref[...] = (dm * sa).astype(BF16)
        dub_ref[...] = (dm * sb).astype(BF16)
        dg_ref[:, 0:d] = (dm * ua_ref[...].astype(F32) * sa * (1.0 - sa)).astype(BF16)
        dg_ref[:, d:2 * d] = (dm * ub_ref[...].astype(F32) * sb * (1.0 - sb)).astype(BF16)

    row = pl.BlockSpec((tm, d), lambda i: (i, 0))
    return pl.pallas_call(
        body, name="mixer_bwd", grid=(t // tm,),
        in_specs=[row, pl.BlockSpec((d, d), lambda i: (0, 0)), row, row,
                  pl.BlockSpec((tm, d), lambda i: (i, gate_col0 // nc)),
                  pl.BlockSpec((tm, d), lambda i: (i, gate_col0 // nc + 1))],
        out_specs=[row, row, pl.BlockSpec((tm, 2 * d), lambda i: (i, 0))],
        out_shape=[jax.ShapeDtypeStruct((t, d), BF16), jax.ShapeDtypeStruct((t, d), BF16),
                   jax.ShapeDtypeStruct((t, 2 * d), BF16)],
        compiler_params=_params(("parallel",)),
    )(dh, w_out, ua, ub, proj, proj)


FFN_COLS = 1024


def _ffn_up(n, w_gate, w_up):
    t, d = n.shape
    hidden = w_gate.shape[0]
    tm = min(ROW_TILE, t)
    tn = min(FFN_COLS, hidden)

    def body(n_ref, wg_ref, wu_ref, hg_ref, hu_ref, act_ref):
        hg = lax.dot_general(n_ref[...], wg_ref[...], NT, preferred_element_type=F32)
        hu = lax.dot_general(n_ref[...], wu_ref[...], NT, preferred_element_type=F32)
        hg_ref[...] = hg.astype(BF16)
        hu_ref[...] = hu.astype(BF16)
        act_ref[...] = (hg * jax.nn.sigmoid(hg) * hu).astype(BF16)

    wsp = pl.BlockSpec((tn, d), lambda j, i: (j, 0))
    out = pl.BlockSpec((tm, tn), lambda j, i: (i, j))
    osh = jax.ShapeDtypeStruct((t, hidden), BF16)
    return pl.pallas_call(
        body, name="ffn_up", grid=(hidden // tn, t // tm),
        in_specs=[pl.BlockSpec((tm, d), lambda j, i: (i, 0)), wsp, wsp],
        out_specs=[out, out, out], out_shape=[osh, osh, osh],
        compiler_params=_params(("parallel", "parallel")),
    )(n, w_gate, w_up)


def _ffn_bwd(dh, w_down, w_gate, w_up, hg, hu, x, g, dres):
    t, d = dh.shape
    hidden = w_down.shape[0]
    tm = min(ROW_TILE, t)
    tn = min(FFN_COLS, hidden)
    nj = hidden // tn

    def body(dh_ref, wd_ref, wg_ref, wu_ref, hg_ref, hu_ref, x_ref, g_ref, r_ref, dhg_ref, dhu_ref, dx_ref, dxb_ref,
             dg_ref, acc):
        j, i = pl.program_id(0), pl.program_id(1)
        dact = lax.dot_general(dh_ref[...], wd_ref[...], NT, preferred_element_type=F32)
        hg = hg_ref[...].astype(F32)
        sg = jax.nn.sigmoid(hg)
        dhu = (dact * hg * sg).astype(BF16)
        dhg = (dact * hu_ref[...].astype(F32) * sg * (1.0 + hg * (1.0 - sg))).astype(BF16)
        dhu_ref[...] = dhu
        dhg_ref[...] = dhg
        part = (lax.dot_general(dhg, wg_ref[...], NN, preferred_element_type=F32)
                + lax.dot_general(dhu, wu_ref[...], NN, preferred_element_type=F32))

        @pl.when(j == 0)
        def _():
            acc[i] = part

        @pl.when(j > 0)
        def _():
            acc[i] += part

        @pl.when(jnp.logical_and(j == 0, i == 0))
        def _():
            dg_ref[...] = jnp.zeros_like(dg_ref)

        @pl.when(j == nj - 1)
        def _():
            dx, dg = _rms_bwd_rows(acc[i], x_ref[...], g_ref[...], r_ref[...])
            dx_ref[...] = dx
            dxb_ref[...] = dx.astype(BF16)
            dg_ref[...] += dg

    hid = pl.BlockSpec((tm, tn), lambda j, i: (i, j))
    wsp = pl.BlockSpec((tn, d), lambda j, i: (j, 0), pipeline_mode=pl.Buffered(1))
    late = pl.BlockSpec((tm, d), lambda j, i: (jnp.where(j == nj - 1, i, 0), 0))
    vec = pl.BlockSpec((1, d), lambda j, i: (0, 0))
    osh = jax.ShapeDtypeStruct((t, hidden), BF16)
    return pl.pallas_call(
        body, name="ffn_bwd", grid=(nj, t // tm),
        in_specs=[pl.BlockSpec((tm, d), lambda j, i: (i, 0)), wsp, wsp, wsp, hid, hid, late, vec, late],
        out_specs=[hid, hid, late, late, vec],
        out_shape=[osh, osh, jax.ShapeDtypeStruct((t, d), F32), jax.ShapeDtypeStruct((t, d), BF16),
                   jax.ShapeDtypeStruct((1, d), F32)],
        scratch_shapes=[pltpu.VMEM((t // tm, tm, d), F32)],
        compiler_params=_params(("arbitrary", "arbitrary")),
    )(dh, w_down, w_gate, w_up, hg, hu, x, g, dres)


MM_ROWS = 1024


def _mm_w(name, a, w, out_dtype, dims=NN):
    t, k = a.shape
    n = w.shape[1] if dims == NN else w.shape[0]
    tm, tn = min(MM_ROWS, t), min(1024, n)
    o_spec = pl.BlockSpec((tm, tn), lambda j, i: (i, j))
    b_spec = pl.BlockSpec((k, tn), lambda j, i: (0, j)) if dims == NN else pl.BlockSpec((tn, k), lambda j, i: (j, 0))
    return _mm(name, a, w, grid=(n // tn, t // tm), a_spec=pl.BlockSpec((tm, k), lambda j, i: (i, 0)), b_spec=b_spec,
               o_shape=(t, n), o_spec=o_spec, dims=dims, out_dtype=out_dtype)


def _mm_res_norm(name, a, w, res, g):
    t, k = a.shape
    d = w.shape[1]
    tm = min(ROW_TILE, t)

    def body(a_ref, w_ref, r_ref, g_ref, h_ref, n_ref):
        h = lax.dot_general(a_ref[...], w_ref[...], NN, preferred_element_type=F32) + r_ref[...]
        h_ref[...] = h
        r = lax.rsqrt(jnp.mean(h * h, axis=-1, keepdims=True) + RMS_EPS)
        n_ref[...] = (h * r * g_ref[...]).astype(BF16)

    row = pl.BlockSpec((tm, d), lambda i: (i, 0))
    return pl.pallas_call(
        body, name=name, grid=(t // tm,),
        in_specs=[pl.BlockSpec((tm, k), lambda i: (i, 0)), pl.BlockSpec((k, d), lambda i: (0, 0)), row,
                  pl.BlockSpec((1, d), lambda i: (0, 0))],
        out_specs=[row, row], out_shape=[jax.ShapeDtypeStruct((t, d), F32), jax.ShapeDtypeStruct((t, d), BF16)],
        compiler_params=_params(("parallel",)),
    )(a, w, res, g)


def _wgrad(name, a, g, tk=1024, tn=1024):
    t, k = a.shape
    n = g.shape[1]
    tm, tk, tn = min(2 * MM_ROWS, t), min(tk, k), min(tn, n)
    return _mm(name, a, g, grid=(k // tk, n // tn, t // tm),
               a_spec=pl.BlockSpec((tm, tk), lambda p, q, r: (r, p)), b_spec=pl.BlockSpec((tm, tn), lambda p, q, r: (r, q)),
               o_shape=(k, n), o_spec=pl.BlockSpec((tk, tn), lambda p, q, r: (p, q)), dims=TN, out_dtype=BF16, nk=t // tm)


def _peers():
    x, y, c = lax.axis_index("x"), lax.axis_index("y"), lax.axis_index("c")
    me = 4 * x + 2 * y + c
    out = []
    for k in range(1, N_DEV):
        kx, ky, kc = (k >> 2) & 1, (k >> 1) & 1, k & 1
        px = 1 - x if kx else x
        py = 1 - y if ky else y
        pc = 1 - c if kc else c
        out.append(((px, py, pc), 4 * px + 2 * py + pc))
    return me, out


def _cast_weights(ws, pad_rows):
    def body(*refs):
        n = len(refs) // 2
        for i_ref, o_ref, pr in zip(refs[:n], refs[n:], pad_rows):
            r, c = i_ref.shape
            o_ref[0:r, :] = i_ref[...].astype(BF16)
            if pr:
                o_ref[r:r + pr, :] = jnp.zeros((pr, c), BF16)

    return pl.pallas_call(
        body, name="cast_weights", in_specs=[VMEM] * len(ws), out_specs=[VMEM] * len(ws),
        out_shape=[jax.ShapeDtypeStruct((w.shape[0] + pr, w.shape[1]), BF16) for w, pr in zip(ws, pad_rows)],
    )(*ws)


def _window(ref, j, c):
    return ref.at[:, pl.ds(pl.multiple_of(j * c, LANES), c)]


def _scatter_copies(ins, outs, sems, cols, landed):
    send_sems, recv_sems, loc_sems = sems
    n_peer = N_DEV - 1
    me, peers = _peers()

    def src(w, j):
        return _window(ins[w], j, cols[w]) if cols[w] else ins[w].at[j]

    local = [pltpu.make_async_copy(src(w, me), outs[w].at[me], loc_sems.at[w]) for w in range(len(ins))]
    remote = [pltpu.make_async_remote_copy(
        src_ref=src(w, idx), dst_ref=outs[w].at[idx if landed else me],
        send_sem=send_sems.at[w * n_peer + k], recv_sem=recv_sems.at[w * n_peer + k],
        device_id=dev, device_id_type=pl.DeviceIdType.MESH)
        for k, (dev, idx) in reversed(list(enumerate(peers))) for w in range(len(ins))]
    return local, remote


OTHER_CHIPS = (2, 4, 6)


def _gather_copies(ins, outs, sems, cols):
    send_sems, recv_sems, loc_sems = sems
    x, y, c = lax.axis_index("x"), lax.axis_index("y"), lax.axis_index("c")
    me = 4 * x + 2 * y + c
    n_pair = N_DEV - 1

    def dev(mask):
        return (1 - x if mask & 4 else x, 1 - y if mask & 2 else y, 1 - c if mask & 1 else c)

    def slot(w, mask):
        j = jnp.bitwise_xor(me, mask)
        return _window(outs[w], j, cols[w]) if cols[w] else outs[w].at[j]

    def remote(w, pair, src, to_slot, target):
        return pltpu.make_async_remote_copy(src_ref=src, dst_ref=slot(w, to_slot), send_sem=send_sems.at[w * n_pair + pair],
                                            recv_sem=recv_sems.at[w * n_pair + pair], device_id=dev(target),
                                            device_id_type=pl.DeviceIdType.MESH)

    ws = range(len(ins))
    return dict(
        local=[pltpu.make_async_copy(ins[w], slot(w, 0), loc_sems.at[w]) for w in ws],
        to_chips=[remote(w, 1 + t, ins[w], 0, m) for t, m in enumerate(OTHER_CHIPS) for w in ws],
        to_core=[remote(w, 0, ins[w], 0, 1) for w in ws],
        from_chips=[remote(w, 1 + t, ins[w], m, 0) for t, m in enumerate(OTHER_CHIPS) for w in ws],
        pass_on=[remote(w, 4 + t, slot(w, m), m, 1) for t, m in enumerate(OTHER_CHIPS) for w in ws],
        from_core=[remote(w, 0, ins[w], 1, 0) for w in ws]
        + [remote(w, 4 + t, ins[w], m + 1, 0) for t, m in enumerate(OTHER_CHIPS) for w in ws])


def _exchange_start(ins, outs, sems, gather, cols):
    if gather:
        cps = _gather_copies(ins, outs, sems, cols)
        for cp in cps["local"] + cps["to_chips"] + cps["to_core"]:
            cp.start()
    else:
        local, remote = _scatter_copies(ins, outs, sems, cols, False)
        for cp in local + remote:
            cp.start()


def _exchange_pass_on(ins, outs, sems, gather, cols, chips):
    if gather:
        cps = _gather_copies(ins, outs, sems, cols)
        n = len(ins)
        for t in chips:
            for arrived, onward in zip(cps["from_chips"][t * n:(t + 1) * n], cps["pass_on"][t * n:(t + 1) * n]):
                arrived.wait_recv()
                onward.start()


def _exchange_wait(ins, outs, sems, gather, cols):
    if gather:
        cps = _gather_copies(ins, outs, sems, cols)
        for cp in cps["local"]:
            cp.wait()
        for cp in cps["to_chips"] + cps["to_core"] + cps["pass_on"]:
            cp.wait_send()
        for cp in cps["from_core"]:
            cp.wait_recv()
    else:
        local, remote = _scatter_copies(ins, outs, sems, cols, True)
        for cp in local:
            cp.wait()
        for cp in remote:
            cp.wait_send()
            cp.wait_recv()


def _exchange_shapes(arrs, gather, cols):
    n = len(arrs)
    out_shape = []
    for a, c in zip(arrs, cols):
        if gather:
            shape = (a.shape[0], N_DEV * c) if c else (N_DEV,) + a.shape
        else:
            shape = (N_DEV, a.shape[0], c) if c else a.shape
        out_shape.append(jax.ShapeDtypeStruct(shape, a.dtype))
    sems = [pltpu.SemaphoreType.DMA((n * (N_DEV - 1),)), pltpu.SemaphoreType.DMA((n * (N_DEV - 1),)),
            pltpu.SemaphoreType.DMA((n,))]
    return out_shape, sems


def _call(body, *, name, grid, in_specs, out_specs, out_shape, scratch, sem, args, ride=None):
    if ride is None:
        outs = pl.pallas_call(body, name=name, grid=grid, in_specs=in_specs, out_specs=out_specs, out_shape=out_shape,
                              scratch_shapes=scratch, compiler_params=_params(sem))(*args)
        return outs, None
    arrs, gather, cols = ride
    n, n_in, n_out, n_scr = len(arrs), len(in_specs), len(out_specs), len(scratch)
    x_shape, x_sems = _exchange_shapes(arrs, gather, cols)

    def riding(*refs):
        ins, x_ins = refs[:n_in], refs[n_in:n_in + n]
        outs = refs[n_in + n:n_in + n + n_out]
        x_outs = refs[n_in + n + n_out:n_in + 2 * n + n_out]
        scr = refs[n_in + 2 * n + n_out:n_in + 2 * n + n_out + n_scr]
        sems = refs[n_in + 2 * n + n_out + n_scr:]
        def at(step):
            return functools.reduce(jnp.logical_and, [pl.program_id(a) == v for a, v in enumerate(step)])

        @pl.when(at((0,) * len(grid)))
        def _():
            _exchange_start(x_ins, x_outs, sems, gather, cols)

        @pl.when(at((grid[0] // 2,) + (0,) * (len(grid) - 1)))
        def _():
            _exchange_pass_on(x_ins, x_outs, sems, gather, cols, (0, 1))

        @pl.when(at((grid[0] // 2,) + (0,) * (len(grid) - 2) + (5 * grid[-1] // 8,)))
        def _():
            _exchange_pass_on(x_ins, x_outs, sems, gather, cols, (2,))

        body(*ins, *outs, *scr)

        @pl.when(at(tuple(g - 1 for g in grid)))
        def _():
            _exchange_wait(x_ins, x_outs, sems, gather, cols)

    res = pl.pallas_call(
        riding, name=name, grid=grid, in_specs=list(in_specs) + [ANY] * n, out_specs=list(out_specs) + [ANY] * n,
        out_shape=list(out_shape) + x_shape, scratch_shapes=list(scratch) + x_sems,
        compiler_params=_params(("arbitrary",) * len(grid)))(*args, *arrs)
    return res[:n_out], res[n_out:]


def _my_block():
    return (4 * lax.axis_index("x") + 2 * lax.axis_index("y") + lax.axis_index("c")).astype(jnp.int32).reshape(1)


def _proj_in_gather(n, w_shard):
    t, k = n.shape
    cs = w_shard.shape[1]
    tm = min(MM_ROWS, t)
    ni = t // tm
    arrival = (0, 1) + OTHER_CHIPS + tuple(m + 1 for m in OTHER_CHIPS)

    def mask_at(s):
        return jnp.where(s < 2, s, jnp.where(s < 5, 2 * (s - 1), 2 * (s - 4) + 1))

    def body(me_ref, n_ref, w_hbm, o_ref, all_hbm, w_vmem, send_sems, recv_sems, loc_sems, load_sems):
        s, i = pl.program_id(0), pl.program_id(1)
        cps = _gather_copies([w_hbm], [all_hbm], (send_sems, recv_sems, loc_sems), (cs,))
        arrived = cps["local"] + cps["from_core"][:1] + cps["from_chips"] + cps["from_core"][1:]

        def load(step):
            src = w_hbm if step == 0 else _window(all_hbm, jnp.bitwise_xor(me_ref[0], arrival[step]), cs)
            return pltpu.make_async_copy(src, w_vmem.at[step % 2], load_sems.at[step % 2])

        @pl.when(jnp.logical_and(s == 0, i == 0))
        def _():
            for cp in cps["local"] + cps["to_chips"] + cps["to_core"]:
                cp.start()
            load(0).start()

        for step, mask in enumerate(arrival):
            @pl.when(jnp.logical_and(s == step, i == 0))
            def _(step=step):
                load(step).wait()

            if step + 1 < N_DEV:
                @pl.when(jnp.logical_and(s == step, i == min(1, ni - 1)))
                def _(step=step):
                    arrived[step + 1].wait_recv()
                    if arrival[step + 1] in OTHER_CHIPS:
                        cps["pass_on"][OTHER_CHIPS.index(arrival[step + 1])].start()
                    load(step + 1).start()

        o_ref[...] = lax.dot_general(n_ref[...], w_vmem[s % 2], NN, preferred_element_type=F32).astype(BF16)

        @pl.when(jnp.logical_and(s == N_DEV - 1, i == ni - 1))
        def _():
            cps["local"][0].wait()
            for cp in cps["to_chips"] + cps["to_core"] + cps["pass_on"]:
                cp.wait_send()

    return pl.pallas_call(
        body, name="proj_in",
        grid_spec=pltpu.PrefetchScalarGridSpec(
            num_scalar_prefetch=1, grid=(N_DEV, ni),
            in_specs=[pl.BlockSpec((tm, k), lambda s, i, me: (i, 0)), ANY],
            out_specs=[pl.BlockSpec((tm, cs), lambda s, i, me: (i, jnp.bitwise_xor(me[0], mask_at(s)))), ANY],
            scratch_shapes=[pltpu.VMEM((2, k, cs), BF16), pltpu.SemaphoreType.DMA((N_DEV - 1,)),
                            pltpu.SemaphoreType.DMA((N_DEV - 1,)), pltpu.SemaphoreType.DMA((1,)),
                            pltpu.SemaphoreType.DMA((2,))]),
        out_shape=[jax.ShapeDtypeStruct((t, N_DEV * cs), BF16), jax.ShapeDtypeStruct((k, N_DEV * cs), BF16)],
        compiler_params=_params(("arbitrary", "arbitrary")),
    )(_my_block(), n, w_shard)


def _gw_in_scatter(a, g):
    t, k = a.shape
    cs = g.shape[1] // N_DEV
    tm = min(MM_ROWS, t)
    nr = t // tm
    n_chip = N_DEV // 2
    chips = (6, 4, 2, 0)

    def body(me_ref, a_ref, g_ref, out_hbm, acc, stage, other, core_send, core_recv, chip_send, chip_recv, loc_sem):
        s, r = pl.program_id(0), pl.program_id(1)
        x, y, c = lax.axis_index("x"), lax.axis_index("y"), lax.axis_index("c")
        my_chip = 2 * x + y
        part = lax.dot_general(a_ref[...], g_ref[...], TN, preferred_element_type=F32)

        def to_core(m):
            return pltpu.make_async_remote_copy(src_ref=stage.at[0], dst_ref=other.at[m], send_sem=core_send.at[m],
                                                recv_sem=core_recv.at[m], device_id=(x, y, 1 - c),
                                                device_id_type=pl.DeviceIdType.MESH)

        def to_chip(m, landed):
            mask = chips[m]
            there = (1 - x if mask & 4 else x, 1 - y if mask & 2 else y, c)
            slot = (2 * there[0] + there[1]) if landed else my_chip
            return pltpu.make_async_remote_copy(src_ref=stage.at[1], dst_ref=out_hbm.at[slot], send_sem=chip_send.at[m],
                                                recv_sem=chip_recv.at[m], device_id=there,
                                                device_id_type=pl.DeviceIdType.MESH)

        local = pltpu.make_async_copy(stage.at[1], out_hbm.at[my_chip], loc_sem)

        @pl.when(r == 0)
        def _():
            acc[...] = part

        @pl.when(r > 0)
        def _():
            acc[...] += part

        for step in range(N_DEV):
            m = step // 2

            @pl.when(jnp.logical_and(s == step, r == nr - 1))
            def _(step=step, m=m):
                if step % 2 == 0:
                    if m > 0:
                        to_core(m - 1).wait_send()
                    stage[0] = acc[...].astype(BF16)
                    to_core(m).start()
                else:
                    if m > 0:
                        to_chip(m - 1, False).wait_send()
                    to_core(m).wait_recv()
                    stage[1] = (acc[...] + other[m].astype(F32)).astype(BF16)
                    if m < n_chip - 1:
                        to_chip(m, False).start()
                    else:
                        local.start()
                        to_core(m).wait_send()
                        local.wait()
                        for mm in range(n_chip - 1):
                            to_chip(mm, True).wait_recv()

    return pl.pallas_call(
        body, name="gw_in",
        grid_spec=pltpu.PrefetchScalarGridSpec(
            num_scalar_prefetch=1, grid=(N_DEV, nr),
            in_specs=[pl.BlockSpec((tm, k), lambda s, r, me: (r, 0)),
                      pl.BlockSpec((tm, cs), lambda s, r, me: (r, jnp.bitwise_xor(me[0], N_DEV - 1 - s)))],
            out_specs=ANY,
            scratch_shapes=[pltpu.VMEM((k, cs), F32), pltpu.VMEM((2, k, cs), BF16), pltpu.VMEM((n_chip, k, cs), BF16),
                            pltpu.SemaphoreType.DMA((n_chip,)), pltpu.SemaphoreType.DMA((n_chip,)),
                            pltpu.SemaphoreType.DMA((n_chip - 1,)), pltpu.SemaphoreType.DMA((n_chip - 1,)),
                            pltpu.SemaphoreType.DMA]),
        out_shape=jax.ShapeDtypeStruct((n_chip, k, cs), BF16),
        compiler_params=_params(("arbitrary", "arbitrary")),
    )(_my_block(), a, g)


SMALL_ROWS = 8


def _allreduce_small(parts, loss_part):
    n, d = len(parts), parts[0].shape[1]

    def body(*refs):
        part_refs, loss_ref, o_ref = refs[:n], refs[n], refs[n + 1]
        mine_ref, all_ref, send_sems, recv_sems = refs[n + 2:]
        me, peers = _peers()
        mine_ref[...] = jnp.zeros_like(mine_ref)
        for i, p_ref in enumerate(part_refs):
            mine_ref[i:i + 1, :] = p_ref[...]
        mine_ref[SMALL_ROWS - 1:SMALL_ROWS, 0:LANES] = loss_ref[0:1, :]
        all_ref[me] = mine_ref[...]
        for k, (dev, idx) in enumerate(peers):
            pltpu.make_async_remote_copy(src_ref=mine_ref, dst_ref=all_ref.at[me], send_sem=send_sems.at[k],
                                         recv_sem=recv_sems.at[k], device_id=dev,
                                         device_id_type=pl.DeviceIdType.MESH).start()
        for k, (dev, idx) in enumerate(peers):
            cp = pltpu.make_async_remote_copy(src_ref=mine_ref, dst_ref=all_ref.at[idx], send_sem=send_sems.at[k],
                                              recv_sem=recv_sems.at[k], device_id=dev,
                                              device_id_type=pl.DeviceIdType.MESH)
            cp.wait_send()
            cp.wait_recv()
        tot = all_ref[0]
        for dvc in range(1, N_DEV):
            tot = tot + all_ref[dvc]
        o_ref[...] = tot

    return pl.pallas_call(
        body, name="allreduce_small", in_specs=[VMEM] * (n + 1), out_specs=VMEM,
        out_shape=jax.ShapeDtypeStruct((SMALL_ROWS, d), F32),
        scratch_shapes=[pltpu.VMEM((SMALL_ROWS, d), F32), pltpu.VMEM((N_DEV, SMALL_ROWS, d), F32),
                        pltpu.SemaphoreType.DMA((N_DEV - 1,)), pltpu.SemaphoreType.DMA((N_DEV - 1,))],
    )(*parts, loss_part)


def _adam_math(g, w, m, v):
    m_new = ADAM_B1 * m + (1.0 - ADAM_B1) * g
    v_new = ADAM_B2 * v + (1.0 - ADAM_B2) * (g * g)
    m_hat = m_new / (1.0 - ADAM_B1 ** ADAM_STEP)
    v_hat = v_new / (1.0 - ADAM_B2 ** ADAM_STEP)
    delta = -ADAM_LR * (m_hat / (jnp.sqrt(v_hat) + ADAM_EPS) + ADAM_WD * w)
    return delta, m_new, v_new


def _adam(name, pieces, w, m, v):
    r, c = w.shape
    n_piece, _, cp = pieces.shape
    tr = r
    for cand in (256, 176, 128, 64):
        if r % cand == 0 and r > cand:
            tr = cand
            break

    def body(p_ref, w_ref, m_ref, v_ref, g_ref, d_ref, mo_ref, vo_ref):
        g = p_ref[0, :, 0:c].astype(F32)
        for j in range(1, n_piece):
            g = g + p_ref[j, :, 0:c].astype(F32)
        delta, m_new, v_new = _adam_math(g, w_ref[...], m_ref[...], v_ref[...])
        g_ref[...] = g
        d_ref[...] = delta
        mo_ref[...] = m_new
        vo_ref[...] = v_new

    blk = pl.BlockSpec((tr, c), lambda i: (i, 0))
    osh = jax.ShapeDtypeStruct((r, c), F32)
    return pl.pallas_call(
        body, name=name, grid=(r // tr,),
        in_specs=[pl.BlockSpec((n_piece, tr, cp), lambda i: (0, i, 0)), blk, blk, blk],
        out_specs=[blk, blk, blk, blk], out_shape=[osh, osh, osh, osh],
        compiler_params=_params(("parallel",)),
    )(pieces, w, m, v)


def _adam_small(g_all, ws, ms, vs):
    n = len(ws)

    def body(*refs):
        g_ref, ins, outs = refs[0], refs[1:1 + 3 * n], refs[1 + 3 * n:]
        for i in range(n):
            g = g_ref[i:i + 1, :]
            delta, m_new, v_new = _adam_math(g, ins[i][...], ins[n + i][...], ins[2 * n + i][...])
            for kind, val in enumerate((g, delta, m_new, v_new)):
                outs[kind * n + i][...] = val

    osh = jax.ShapeDtypeStruct(ws[0].shape, F32)
    res = pl.pallas_call(body, name="adam_small", in_specs=[VMEM] * (1 + 3 * n), out_specs=[VMEM] * (4 * n),
                         out_shape=[osh] * (4 * n))(g_all, *ws, *ms, *vs)
    return res[:n], res[n:2 * n], res[2 * n:3 * n], res[3 * n:]


def _local_step(x, mem, pos, tgt, gains, w_in_shard, shards, batch):
    g_mix, g_mem_q, g_mem_kv, g_ffn, g_final = gains
    t, d = x.shape
    s = t // batch
    n_mem = mem.shape[0] // batch
    n_sh = N_DEV
    width = shards[0].shape[0]
    nb = width // LANES

    lane = np.arange(LANES) % HEAD_DIM
    sel_lo = (lane < ROPE_HALF).astype(np.float32)[None, :]
    sel_hi = ((lane >= ROPE_HALF) & (lane < 2 * ROPE_HALF)).astype(np.float32)[None, :]
    freqs = np.float32(ROPE_THETA) ** (-np.arange(ROPE_HALF, dtype=np.float32) / np.float32(ROPE_HALF))
    inv_freq = np.where(lane < 2 * ROPE_HALF, freqs[lane % ROPE_HALF], 0.0).astype(np.float32)[None, :]
    cos_t, sin_a, sin_b = _rope_tables(pos, jnp.asarray(inv_freq), jnp.asarray(sel_lo), jnp.asarray(sel_hi))
    bias = _dilated_bias_tiles(s)

    n1 = _rms_fwd("norm_mix", x, g_mix)
    proj, w_in = _proj_in_gather(n1, w_in_shard)
    qk_a = _rope_apply("rope_fwd", [proj], 2 * width, cos_t, sin_a, sin_b, 1.0)
    cs_up = shards[0].shape[1]
    (o_a, lse_a), (w_up_a, w_up_b, w_out, w_q, w_kv, w_o, w_fd) = _da_fwd(
        qk_a, proj, 2 * nb, bias, batch, s,
        ride=(shards[:6] + shards[8:], True, (cs_up, cs_up, 0, 0, 0, cs_up, 0)))
    (o_b, tot_b), (w_fg, w_fu) = _sb_fwd(proj, 3 * nb, 4 * nb, 5 * nb, batch, s, ride=(shards[6:8], True, (0, 0)))
    w_out = w_out.reshape(d, d)
    w_q = w_q.reshape(d, -1)
    w_kv = w_kv.reshape(d, -1)
    w_fd = w_fd.reshape(-1, d)
    w_fg = w_fg.reshape(-1, d)
    w_fu = w_fu.reshape(-1, d)
    ua, ub, mixed = _mixer_fwd(o_a, o_b, w_up_a, w_up_b, proj, 6 * nb)
    h1, n2 = _mm_res_norm("mix_out", mixed, w_out, x, g_mem_q)
    mem_n = _rms_fwd("norm_mem_kv", mem, g_mem_kv)
    q_m = _mm_w("mem_q", n2, w_q, BF16)
    kv_m = _mm_w("mem_kv", mem_n, w_kv, BF16)
    o_m = _mem_fwd(q_m, kv_m, batch, s, n_mem)
    h2, n3 = _mm_res_norm("mem_out", o_m, w_o, h1, g_ffn)
    hg, hu, act = _ffn_up(n3, w_fg, w_fu)
    loss_part, dh3, dh3_b, dg_final = _loss_head(act, w_fd, h2, tgt, g_final.reshape(1, d))

    dhg, dhu, dh2, dh2_b, dg_ffn = _ffn_bwd(dh3_b, w_fd, w_fg, w_fu, hg, hu, h2, g_ffn, dh3)
    gw_fd = _wgrad("gw_ffn_down", act, dh3_b)
    gw_fg = _wgrad("gw_ffn_gate", dhg, n3)
    gw_fu = _wgrad("gw_ffn_up", dhu, n3)

    do_m = _mm_w("mem_out_bwd", dh2_b, w_o, BF16, dims=NT)
    gw_o = _wgrad("gw_mem_o", o_m, dh2_b)
    dq_m, dkv_m = _mem_bwd(q_m, kv_m, do_m, batch, s, n_mem)
    gw_q = _wgrad("gw_mem_q", n2, dq_m)
    gw_kv = _wgrad("gw_mem_kv", mem_n, dkv_m)
    (dg_mem_kv,) = _rms_bwd("norm_mem_kv_bwd", (dkv_m, w_kv, NT), mem, g_mem_kv, None, ())
    dh1, dh1_b, dg_mem_q = _rms_bwd("norm_mem_q_bwd", (dq_m, w_q, NT), h1, g_mem_q, dh2, ("f32", "bf16"))

    gw_out = _wgrad("gw_out", mixed, dh1_b)
    dua, dub, dgates = _mixer_bwd(dh1_b, w_out, ua, ub, proj, 6 * nb)
    do_a = _mm_w("up_a_bwd", dua, w_up_a, BF16, dims=NT)
    do_b = _mm_w("up_b_bwd", dub, w_up_b, BF16, dims=NT)
    gw_ua = _wgrad("gw_up_a", o_a, dua)
    gw_ub = _wgrad("gw_up_b", o_b, dub)
    (dq_ar, dk_ar, dv_a), (p_fg, p_fd) = _da_bwd(
        qk_a, proj, 2 * nb, bias, o_a, lse_a, do_a, batch, s,
        ride=([gw_fg.reshape(n_sh, -1, d), gw_fd.reshape(n_sh, -1, d)], False, (0, 0)))
    dqk_a = _rope_apply("rope_bwd", [dq_ar, dk_ar], width, cos_t, sin_a, sin_b, -1.0)
    mid = [gw_ua, gw_ub, gw_out.reshape(n_sh, -1, d), gw_q.reshape(n_sh, -1, gw_q.shape[1]),
           gw_kv.reshape(n_sh, -1, gw_kv.shape[1]), gw_o, gw_fu.reshape(n_sh, -1, d)]
    (dq_b, dk_b, dv_b), (*p_mid, p_fu) = _sb_bwd(proj, 3 * nb, 4 * nb, 5 * nb, tot_b, do_b, batch, s,
                                                 ride=(mid, False, (cs_up, cs_up, 0, 0, 0, cs_up, 0)))
    p_ffn = [p_fg, p_fu, p_fd]
    dproj = jnp.concatenate([dqk_a, dv_a, dq_b, dk_b, dv_b, dgates], axis=1)
    grad_x, dg_mix = _rms_bwd("proj_in_bwd", (dproj, w_in, NT), x, g_mix, dh1, ("f32",))
    p_in = _gw_in_scatter(n1, dproj)
    return loss_part, grad_x, [p_in] + list(p_mid) + p_ffn, (dg_mix, dg_mem_q, dg_mem_kv, dg_ffn, dg_final)


WEIGHTS =("w_in", "w_up_a", "w_up_b", "w_out", "w_q_mem", "w_kv_mem", "w_o_mem", "w_ffn_gate", "w_ffn_up", "w_ffn_down")
GAINS = ("g_mix", "g_mem_q", "g_mem_kv", "g_ffn", "g_final")
ORDER = ("g_mix", "w_in", "w_up_a", "w_up_b", "w_out", "g_mem_q", "g_mem_kv", "w_q_mem", "w_kv_mem", "w_o_mem", "g_ffn",
         "w_ffn_gate", "w_ffn_up", "w_ffn_down", "g_final")


def kernel(x, mem, positions, g_mix, w_in, w_up_a, w_up_b, w_out, g_mem_q, g_mem_kv, w_q_mem, w_kv_mem, w_o_mem, g_ffn, w_ffn_gate, w_ffn_up, w_ffn_down, g_final, loss_target, m_g_mix, m_w_in, m_w_up_a, m_w_up_b, m_w_out, m_g_mem_q, m_g_mem_kv, m_w_q_mem, m_w_kv_mem, m_w_o_mem, m_g_ffn, m_w_ffn_gate, m_w_ffn_up, m_w_ffn_down, m_g_final, v_g_mix, v_w_in, v_w_up_a, v_w_up_b, v_w_out, v_g_mem_q, v_g_mem_kv, v_w_q_mem, v_w_kv_mem, v_w_o_mem, v_g_ffn, v_w_ffn_gate, v_w_ffn_up, v_w_ffn_down, v_g_final):
    given = dict(locals())
    batch, s, d = x.shape
    t = batch * s
    flipped = ("w_ffn_gate", "w_ffn_up")

    def view(a, n):
        a = a.reshape(a.shape[-2:])
        return a.T if n in flipped else a

    def unview(a, n):
        return (a.T if n in flipped else a).reshape(given[n].shape)

    shard = {n: view(given[n], n) for n in WEIGHTS}
    gains = [given[n].reshape(1, d) for n in GAINS]

    pad = (-shard["w_ffn_down"].shape[0]) % LANES
    cast = _cast_weights([shard[n] for n in WEIGHTS], [pad if n in flipped + ("w_ffn_down",) else 0 for n in WEIGHTS])
    loss_part, grad_x, pieces, dgains = _local_step(
        x.reshape(t, d), mem.reshape(-1, d), positions.reshape(t, 1), loss_target.reshape(t, d), gains, cast[0],
        cast[1:], batch)

    grad, delta, new_m, new_v = {}, {}, {}, {}
    for n, p in zip(WEIGHTS, pieces):
        outs = _adam("adam_" + n, p, shard[n], view(given["m_" + n], n), view(given["v_" + n], n))
        grad[n], delta[n], new_m[n], new_v[n] = [unview(o, n) for o in outs]

    g_all = _allreduce_small(list(dgains), loss_part)
    small = _adam_small(g_all, gains, [given["m_" + n].reshape(1, d) for n in GAINS],
                        [given["v_" + n].reshape(1, d) for n in GAINS])
    for out, vals in zip((grad, delta, new_m, new_v), small):
        for n, val in zip(GAINS, vals):
            out[n] = val.reshape(given[n].shape)

    loss = g_all[SMALL_ROWS - 1, 0]
    return (loss, grad_x.reshape(x.shape), *[grad[n] for n in ORDER], *[delta[n] for n in ORDER],
            *[new_m[n] for n in ORDER], *[new_v[n] for n in ORDER])
```

```python
import functools
import math

import jax
import jax.numpy as jnp
import numpy as np
from jax import lax
from jax.experimental import pallas as pl
from jax.experimental.pallas import tpu as pltpu

F32 = jnp.float32
BF16 = jnp.bfloat16

N_DEV = 8
HEAD_DIM = 64
MEM_HEAD_DIM = 128
N_HEADS_MEM = 4
BLOCK = 128
DIL_PATTERNS = ((128, 1), (512, 4), (2048, 16))
ROPE_THETA = 500000.0
ROPE_HALF = 8
RMS_EPS = 1e-6
ADAM_LR, ADAM_B1, ADAM_B2, ADAM_EPS, ADAM_WD, ADAM_STEP = 0.001, 0.9, 0.999, 1e-08, 0.01, 10
NEG = -1e30
ROW_TILE = 512
LANES = 128

ANY = pl.BlockSpec(memory_space=pl.ANY)
VMEM = pl.BlockSpec(memory_space=pltpu.VMEM)
NN = (((1,), (0,)), ((), ()))
NT = (((1,), (1,)), ((), ()))
TN = (((0,), (0,)), ((), ()))


def _params(sem):
    return pltpu.CompilerParams(dimension_semantics=sem)


def _mm(name, a, b, *, grid, a_spec, b_spec, o_shape, o_spec, dims, out_dtype, nk=1):
    def body(*refs):
        a_ref, b_ref, o_ref = refs[0], refs[1], refs[2]
        p = lax.dot_general(a_ref[...], b_ref[...], dims, preferred_element_type=F32)
        if nk == 1:
            o_ref[...] = p.astype(out_dtype)
            return
        acc_ref = refs[-1]
        k = pl.program_id(len(grid) - 1)

        @pl.when(k == 0)
        def _():
            acc_ref[...] = p

        @pl.when(k > 0)
        def _():
            acc_ref[...] += p

        @pl.when(k == nk - 1)
        def _():
            o_ref[...] = acc_ref[...].astype(out_dtype)

    o_block = tuple(d for d in o_spec.block_shape if d is not None)
    sem = ("parallel",) * (len(grid) - 1) + (("arbitrary",) if nk > 1 else ("parallel",))
    return pl.pallas_call(
        body, name=name, grid=grid, in_specs=[a_spec, b_spec],
        out_specs=o_spec, out_shape=jax.ShapeDtypeStruct(o_shape, out_dtype),
        scratch_shapes=[pltpu.VMEM(o_block, F32)] if nk > 1 else [],
        compiler_params=_params(sem),
    )(a, b)


def _rms_fwd(name, x, g):
    t, d = x.shape
    tm = min(ROW_TILE, t)

    def body(x_ref, g_ref, o_ref):
        xf = x_ref[...]
        r = lax.rsqrt(jnp.mean(xf * xf, axis=-1, keepdims=True) + RMS_EPS)
        o_ref[...] = (xf * r * g_ref[...]).astype(BF16)

    return pl.pallas_call(
        body, name=name, grid=(t // tm,),
        in_specs=[pl.BlockSpec((tm, d), lambda i: (i, 0)), pl.BlockSpec((1, d), lambda i: (0, 0))],
        out_specs=pl.BlockSpec((tm, d), lambda i: (i, 0)), out_shape=jax.ShapeDtypeStruct((t, d), BF16),
        compiler_params=_params(("parallel",)),
    )(x, g)


def _rms_bwd_rows(dnf, xf, gv, res):
    r = lax.rsqrt(jnp.mean(xf * xf, axis=-1, keepdims=True) + RMS_EPS)
    xh = xf * r
    dxh = dnf * gv
    dx = r * (dxh - xh * jnp.mean(dxh * xh, axis=-1, keepdims=True))
    if res is not None:
        dx = dx + res
    return dx, jnp.sum(dnf * xh, axis=0, keepdims=True)


def _rms_bwd(name, dn, x, g, dres, want):
    t, d = x.shape
    tm = min(ROW_TILE, t)
    has_res = dres is not None
    lhs = list(dn) if isinstance(dn, tuple) else [dn]
    n_lhs = len(lhs[:2])

    def body(*refs):
        x_ref, g_ref = refs[n_lhs], refs[n_lhs + 1]
        r_ref = refs[n_lhs + 2] if has_res else None
        dx_refs, dg_ref = refs[-1 - len(want):-1], refs[-1]
        if n_lhs == 2:
            dnf = lax.dot_general(refs[0][...], refs[1][...], lhs[2], preferred_element_type=F32)
        else:
            dnf = refs[0][...].astype(F32)
        dx, dg = _rms_bwd_rows(dnf, x_ref[...], g_ref[...], r_ref[...] if has_res else None)
        for kind, dx_ref in zip(want, dx_refs):
            dx_ref[...] = dx.astype(F32 if kind == "f32" else BF16)

        @pl.when(pl.program_id(0) == 0)
        def _():
            dg_ref[...] = jnp.zeros_like(dg_ref)

        dg_ref[...] += dg

    row = pl.BlockSpec((tm, d), lambda i: (i, 0))
    vec = pl.BlockSpec((1, d), lambda i: (0, 0))
    if n_lhs == 2:
        first = [pl.BlockSpec((tm, lhs[0].shape[1]), lambda i: (i, 0)), pl.BlockSpec(lhs[1].shape, lambda i: (0, 0))]
    else:
        first = [row]
    return pl.pallas_call(
        body, name=name, grid=(t // tm,),
        in_specs=first + [row, vec] + ([row] if has_res else []),
        out_specs=[row] * len(want) + [vec],
        out_shape=[jax.ShapeDtypeStruct((t, d), F32 if kind == "f32" else BF16) for kind in want]
        + [jax.ShapeDtypeStruct((1, d), F32)],
        compiler_params=_params(("arbitrary",)),
    )(*(lhs[:2] + [x, g] + ([dres] if has_res else [])))


def _loss_head(a, w, res, tgt, g):
    t, d = res.shape
    k = a.shape[1]
    tm = min(ROW_TILE, t)

    def body(a_ref, w_ref, r_ref, t_ref, g_ref, loss_ref, dh_ref, dhb_ref, dg_ref):
        xf = lax.dot_general(a_ref[...], w_ref[...], NN, preferred_element_type=F32) + r_ref[...]
        gv = g_ref[...]
        r = lax.rsqrt(jnp.mean(xf * xf, axis=-1, keepdims=True) + RMS_EPS)
        xh = xf * r
        e = xh * gv - t_ref[...]
        dy = e * (1.0 / d)
        dxh = dy * gv
        dh = r * (dxh - xh * jnp.mean(dxh * xh, axis=-1, keepdims=True))
        dh_ref[...] = dh
        dhb_ref[...] = dh.astype(BF16)

        @pl.when(pl.program_id(0) == 0)
        def _():
            dg_ref[...] = jnp.zeros_like(dg_ref)
            loss_ref[...] = jnp.zeros_like(loss_ref)

        dg_ref[...] += jnp.sum(dy * xh, axis=0, keepdims=True)
        part = jnp.sum(jnp.sum(e * e, axis=1, keepdims=True), axis=0, keepdims=True) * (0.5 / d)
        loss_ref[...] += jnp.broadcast_to(part, loss_ref.shape)

    row = pl.BlockSpec((tm, d), lambda i: (i, 0))
    vec = pl.BlockSpec((1, d), lambda i: (0, 0))
    return pl.pallas_call(
        body, name="loss_head", grid=(t // tm,),
        in_specs=[pl.BlockSpec((tm, k), lambda i: (i, 0)), pl.BlockSpec((k, d), lambda i: (0, 0)), row, row, vec],
        out_specs=[pl.BlockSpec((8, LANES), lambda i: (0, 0)), row, row, vec],
        out_shape=[jax.ShapeDtypeStruct((8, LANES), F32), jax.ShapeDtypeStruct((t, d), F32),
                   jax.ShapeDtypeStruct((t, d), BF16), jax.ShapeDtypeStruct((1, d), F32)],
        compiler_params=_params(("arbitrary",)),
    )(a, w, res, tgt, g)


def _rope_tables(pos, inv_freq, sel_lo, sel_hi):
    t = pos.shape[0]
    tm = min(ROW_TILE, t)

    def body(p_ref, f_ref, lo_ref, hi_ref, c_ref, sa_ref, sb_ref):
        ang = p_ref[...].astype(F32) * f_ref[...]
        rot = lo_ref[...] + hi_ref[...]
        cs, sn = jnp.cos(ang), jnp.sin(ang)
        c_ref[...] = cs * rot + (1.0 - rot)
        sa_ref[...] = -sn * lo_ref[...]
        sb_ref[...] = sn * hi_ref[...]

    vec = pl.BlockSpec((1, LANES), lambda i: (0, 0))
    row = pl.BlockSpec((tm, LANES), lambda i: (i, 0))
    return pl.pallas_call(
        body, name="rope_tables", grid=(t // tm,),
        in_specs=[pl.BlockSpec((tm, 1), lambda i: (i, 0)), vec, vec, vec],
        out_specs=[row, row, row], out_shape=[jax.ShapeDtypeStruct((t, LANES), F32)] * 3,
        compiler_params=_params(("parallel",)),
    )(pos, inv_freq, sel_lo, sel_hi)


def _rope_apply(name, srcs, width, cos_t, sin_a, sin_b, sign):
    t = srcs[0].shape[0]
    tm = min(ROW_TILE, t)
    n_cols = width // LANES

    def body(*refs):
        x_refs, (c_ref, sa_ref, sb_ref, o_ref) = refs[:len(srcs)], refs[len(srcs):]
        cs, sa, sb = c_ref[...], sign * sa_ref[...], sign * sb_ref[...]
        for a, x_ref in enumerate(x_refs):
            for c in range(n_cols):
                xf = x_ref[:, c * LANES:(c + 1) * LANES].astype(F32)
                up = pltpu.roll(xf, LANES - ROPE_HALF, 1)
                dn = pltpu.roll(xf, ROPE_HALF, 1)
                o_ref[:, a * width + c * LANES:a * width + (c + 1) * LANES] = (xf * cs + up * sa + dn * sb).astype(BF16)

    wide = len(srcs) * width
    tab = pl.BlockSpec((tm, LANES), lambda i: (i, 0))
    return pl.pallas_call(
        body, name=name, grid=(t // tm,),
        in_specs=[pl.BlockSpec((tm, width), lambda i: (i, 0))] * len(srcs) + [tab, tab, tab],
        out_specs=pl.BlockSpec((tm, wide), lambda i: (i, 0)),
        out_shape=jax.ShapeDtypeStruct((t, wide), BF16),
        compiler_params=_params(("parallel",)),
    )(*srcs, cos_t, sin_a, sin_b)


DA_T = 256
MIX_STREAMS = 4
SB_BWD_STREAMS = 2


def _lane_lo():
    return lax.broadcasted_iota(jnp.int32, (BLOCK, LANES), 1) < HEAD_DIM


def _dilated_bias_tiles(s):
    n = s // DA_T
    dist = (np.arange(n)[:, None, None] * DA_T + np.arange(DA_T)[None, :, None] - np.arange(DA_T)[None, None, :])
    cnt = np.zeros(dist.shape, np.float32)
    for window, dil in DIL_PATTERNS:
        cnt += ((dist >= 0) & (dist % dil == 0) & (dist <= window)).astype(np.float32)
    return jnp.asarray(np.where(cnt > 0, np.log(np.maximum(cnt, 1.0)), NEG).astype(np.float32))


def _stack_heads(x, lo):
    zero = jnp.zeros_like(x)
    return jnp.concatenate([jnp.where(lo, x, zero), jnp.where(lo, zero, x)], axis=0)


def _da_fwd(qk, proj, v_col0, bias, batch, s, ride=None, streams=MIX_STREAMS):
    t = qk.shape[0]
    nq = s // DA_T
    n_pairs = 4
    ns = streams
    wide = ns * LANES
    scale = HEAD_DIM ** -0.5

    def body(q_ref, k_ref, v_ref, b_ref, o_ref, lse_ref, acc_ref, m_ref, l_ref):
        i = pl.program_id(2)
        lo = lax.broadcasted_iota(jnp.int32, (DA_T, LANES), 1) < HEAD_DIM
        ones = jnp.ones((DA_T, LANES), BF16)
        acc_ref[...] = jnp.zeros_like(acc_ref)
        m_ref[...] = jnp.full(m_ref.shape, NEG, F32)
        l_ref[...] = jnp.zeros_like(l_ref)
        qqs = [_stack_heads(q_ref[:, st * LANES:(st + 1) * LANES] * scale, lo) for st in range(ns)]

        def scores(st, rows, bias2):
            k = k_ref[rows, st * LANES:(st + 1) * LANES]
            return lax.dot_general(qqs[st], k, NT, preferred_element_type=F32) + bias2

        def softmax(st, sc):
            m_old = m_ref[st]
            m_new = jnp.maximum(m_old, jnp.max(sc, axis=1, keepdims=True))
            m_ref[st] = m_new
            return jnp.exp(sc - m_new).astype(BF16), jnp.exp(m_old - m_new)

        def values(st, rows, p, alpha):
            v = v_ref[rows, st * LANES:(st + 1) * LANES]
            vz = jnp.zeros_like(v)
            l_ref[st] = alpha * l_ref[st] + lax.dot_general(p, ones, NN, preferred_element_type=F32)
            pv = (lax.dot_general(p[:DA_T], jnp.where(lo, v, vz), NN, preferred_element_type=F32)
                  + lax.dot_general(p[DA_T:], jnp.where(lo, vz, v), NN, preferred_element_type=F32))
            acc_ref[st] = acc_ref[st] * jnp.where(lo, alpha[:DA_T], alpha[DA_T:]) + pv

        def trip(dlt, carry):
            rows = pl.ds(pl.multiple_of((i - dlt) * DA_T, DA_T), DA_T)
            bias_t = b_ref[dlt]
            bias2 = jnp.concatenate([bias_t, bias_t], axis=0)
            scs = [scores(st, rows, bias2) for st in range(ns)]
            pas = [softmax(st, scs[st]) for st in range(ns)]
            for st in range(ns):
                values(st, rows, *pas[st])
            return carry

        lax.fori_loop(0, i + 1, trip, 0)
        for st in range(ns):
            cols = slice(st * LANES, (st + 1) * LANES)
            l_t = l_ref[st]
            o_ref[:, cols] = (acc_ref[st] / jnp.where(lo, l_t[:DA_T], l_t[DA_T:])).astype(BF16)
            lse = m_ref[st] + jnp.log(l_t)
            lse_ref[:, cols] = jnp.where(lo, lse[:DA_T], lse[DA_T:])

    blk = pl.BlockSpec((DA_T, wide), lambda b, h, i: (b * nq + i, h))
    return _call(
        body, name="attn_a_fwd", grid=(batch, n_pairs // ns, nq),
        in_specs=[blk,
                  pl.BlockSpec((s, wide), lambda b, h, i: (b, n_pairs // ns + h)),
                  pl.BlockSpec((s, wide), lambda b, h, i: (b, v_col0 // ns + h)),
                  pl.BlockSpec((nq, DA_T, DA_T), lambda b, h, i: (0, 0, 0))],
        out_specs=[blk, blk],
        out_shape=[jax.ShapeDtypeStruct((t, n_pairs * LANES), BF16), jax.ShapeDtypeStruct((t, n_pairs * LANES), F32)],
        scratch=[pltpu.VMEM((ns, DA_T, LANES), F32), pltpu.VMEM((ns, 2 * DA_T, 1), F32),
                 pltpu.VMEM((ns, 2 * DA_T, LANES), F32)],
        sem=("parallel", "parallel", "arbitrary"), args=(qk, qk, proj, bias), ride=ride)


def _da_bwd(qk, proj, v_col0, bias, o, lse, do, batch, s, ride=None, streams=MIX_STREAMS):
    t = qk.shape[0]
    nq = s // DA_T
    n_pairs = 4
    ns = streams
    wide = ns * LANES
    scale = HEAD_DIM ** -0.5

    def body(q_ref, k_ref, v_ref, b_ref, o_ref, lse_ref, do_ref, dq_ref, dk_ref, dv_ref, dk_acc, dv_acc, dq_acc):
        i = pl.program_id(2)
        lo = lax.broadcasted_iota(jnp.int32, (DA_T, LANES), 1) < HEAD_DIM

        @pl.when(i == 0)
        def _():
            dk_acc[...] = jnp.zeros_like(dk_acc)
            dv_acc[...] = jnp.zeros_like(dv_acc)

        dq_acc[...] = jnp.zeros_like(dq_acc)
        qqs, dds, deltas, lses = [], [], [], []
        for st in range(ns):
            cols = slice(st * LANES, (st + 1) * LANES)
            do_ = do_ref[:, cols]
            qqs.append(_stack_heads(q_ref[:, cols] * scale, lo))
            dds.append(_stack_heads(do_, lo))
            prod = do_.astype(F32) * o_ref[:, cols].astype(F32)
            fz = jnp.zeros_like(prod)
            deltas.append(jnp.concatenate([jnp.sum(jnp.where(lo, prod, fz), axis=1, keepdims=True),
                                           jnp.sum(jnp.where(lo, fz, prod), axis=1, keepdims=True)], axis=0))
            lse_t = lse_ref[:, cols]
            lses.append(jnp.concatenate([lse_t[:, 0:1], lse_t[:, HEAD_DIM:HEAD_DIM + 1]], axis=0))

        def products(st, rows, bias2):
            cols = slice(st * LANES, (st + 1) * LANES)
            sc = lax.dot_general(qqs[st], k_ref[rows, cols], NT, preferred_element_type=F32) + bias2
            return sc, lax.dot_general(dds[st], v_ref[rows, cols], NT, preferred_element_type=F32)

        def weights(st, sc, dp):
            p = jnp.exp(sc - lses[st])
            return (p * (dp - deltas[st])).astype(BF16), p.astype(BF16)

        def gradients(st, rows, ds, p):
            cols = slice(st * LANES, (st + 1) * LANES)
            k = k_ref[rows, cols]
            kz = jnp.zeros_like(k)
            dq_acc[st] += (lax.dot_general(ds[:DA_T], jnp.where(lo, k, kz), NN, preferred_element_type=F32)
                           + lax.dot_general(ds[DA_T:], jnp.where(lo, kz, k), NN, preferred_element_type=F32))
            dk_acc[rows, cols] += lax.dot_general(ds, qqs[st], TN, preferred_element_type=F32)
            dv_acc[rows, cols] += lax.dot_general(p, dds[st], TN, preferred_element_type=F32)

        def trip(dlt, carry):
            rows = pl.ds(pl.multiple_of((i - dlt) * DA_T, DA_T), DA_T)
            bias_t = b_ref[dlt]
            bias2 = jnp.concatenate([bias_t, bias_t], axis=0)
            prods = [products(st, rows, bias2) for st in range(ns)]
            wts = [weights(st, *prods[st]) for st in range(ns)]
            for st in range(ns):
                gradients(st, rows, *wts[st])
            return carry

        lax.fori_loop(0, i + 1, trip, 0)
        for st in range(ns):
            dq_ref[:, st * LANES:(st + 1) * LANES] = (dq_acc[st] * scale).astype(BF16)

        @pl.when(i == nq - 1)
        def _():
            dk_ref[...] = dk_acc[...].astype(BF16)
            dv_ref[...] = dv_acc[...].astype(BF16)

    blk = pl.BlockSpec((DA_T, wide), lambda b, h, i: (b * nq + i, h))
    seq = pl.BlockSpec((s, wide), lambda b, h, i: (b, h), pipeline_mode=pl.Buffered(1))
    one = pl.Buffered(1)
    out = jax.ShapeDtypeStruct((t, n_pairs * LANES), BF16)
    return _call(
        body, name="attn_a_bwd", grid=(batch, n_pairs // ns, nq),
        in_specs=[blk,
                  pl.BlockSpec((s, wide), lambda b, h, i: (b, n_pairs // ns + h), pipeline_mode=one),
                  pl.BlockSpec((s, wide), lambda b, h, i: (b, v_col0 // ns + h), pipeline_mode=one),
                  pl.BlockSpec((nq, DA_T, DA_T), lambda b, h, i: (0, 0, 0), pipeline_mode=one),
                  blk, blk, blk],
        out_specs=[blk, seq, seq], out_shape=[out, out, out],
        scratch=[pltpu.VMEM((s, wide), F32), pltpu.VMEM((s, wide), F32), pltpu.VMEM((ns, DA_T, LANES), F32)],
        sem=("parallel", "parallel", "arbitrary"), args=(qk, qk, proj, bias, o, lse, do), ride=ride)


SB_Q = 256


def _sb_consts(after):
    r = lax.broadcasted_iota(jnp.int32, (2 * BLOCK, 2 * BLOCK), 0) % BLOCK
    c = lax.broadcasted_iota(jnp.int32, (2 * BLOCK, 2 * BLOCK), 1)
    tri = (r > c) if after else (r < c)
    return jnp.logical_or(c >= BLOCK, tri).astype(BF16)


def _split(x):
    hi = x.astype(BF16)
    lo = (x - hi.astype(F32)).astype(BF16)
    return jnp.concatenate([hi, lo], axis=1)


def _sb_fwd(proj, q_col0, k_col0, v_col0, batch, s, ride=None, streams=MIX_STREAMS):
    t = proj.shape[0]
    nq = s // SB_Q
    n_pairs = 4
    ns = streams
    wide = ns * LANES
    scale = HEAD_DIM ** -0.5

    def body(q_ref, k_ref, v_ref, o_ref, tot_ref, acc_ref, run_ref):
        i = pl.program_id(2)
        lo_q = lax.broadcasted_iota(jnp.int32, (SB_Q, LANES), 1) < HEAD_DIM
        lo_k = _lane_lo()
        mat = _sb_consts(True)
        row = lax.broadcasted_iota(jnp.int32, (2 * SB_Q, LANES), 0) % SB_Q
        ahead = row - lax.broadcasted_iota(jnp.int32, (2 * SB_Q, LANES), 1)
        acc_ref[...] = jnp.zeros_like(acc_ref)
        run_ref[...] = jnp.zeros_like(run_ref)
        qqs = [_stack_heads(q_ref[:, st * LANES:(st + 1) * LANES] * scale, lo_q) for st in range(ns)]

        def units(todo):
            def rows(j):
                return pl.ds(pl.multiple_of(j * BLOCK, BLOCK), BLOCK)

            zs = [lax.dot_general(qqs[st], k_ref[rows(j), st * LANES:(st + 1) * LANES], NT, preferred_element_type=F32)
                  for st, j, _ in todo]
            logs = []
            for z, (_, _, off) in zip(zs, todo):
                lsig = jnp.minimum(z, 0.0) - jnp.log(1.0 + jnp.exp(-jnp.abs(z)))
                lneg = lsig - z
                if off is not None:
                    lneg = jnp.where(ahead > off, lneg, 0.0)
                logs.append((lsig, _split(lneg)))
            sums = [lax.dot_general(cat, mat, NN, preferred_element_type=F32) for _, cat in logs]
            probs = []
            for (lsig, _), sm, (st, _, off) in zip(logs, sums, todo):
                run = run_ref[st]
                a = jnp.exp(lsig + run + sm[:, :BLOCK])
                if off is not None:
                    a = jnp.where(ahead > off, a, 0.0)
                run_ref[st] = run + sm[:, BLOCK:]
                probs.append(a.astype(BF16))
            for ab, (st, j, _) in zip(probs, todo):
                v = v_ref[rows(j), st * LANES:(st + 1) * LANES]
                vz = jnp.zeros_like(v)
                acc_ref[st] += (lax.dot_general(ab[:SB_Q], jnp.where(lo_k, v, vz), NN, preferred_element_type=F32)
                                + lax.dot_general(ab[SB_Q:], jnp.where(lo_k, vz, v), NN, preferred_element_type=F32))

        units([(st, 2 * i + 1, BLOCK) for st in range(ns)] + [(st, 2 * i, 0) for st in range(ns)])

        def pair(p, carry):
            jp = i - 1 - p
            units([(st, 2 * jp + 1, None) for st in range(ns)] + [(st, 2 * jp, None) for st in range(ns)])
            return carry

        lax.fori_loop(0, i, pair, 0)
        for st in range(ns):
            cols = slice(st * LANES, (st + 1) * LANES)
            o_ref[:, cols] = acc_ref[st].astype(BF16)
            tot_ref[:, cols] = jnp.where(lo_q, run_ref[st, 0:SB_Q, :], run_ref[st, SB_Q:2 * SB_Q, :])

    def seq(col0):
        return pl.BlockSpec((s, wide), lambda b, h, i: (b, col0 // ns + h))

    blk = pl.BlockSpec((SB_Q, wide), lambda b, h, i: (b * nq + i, h))
    return _call(
        body, name="attn_b_fwd", grid=(batch, n_pairs // ns, nq),
        in_specs=[pl.BlockSpec((SB_Q, wide), lambda b, h, i: (b * nq + i, q_col0 // ns + h)), seq(k_col0), seq(v_col0)],
        out_specs=[blk, blk],
        out_shape=[jax.ShapeDtypeStruct((t, n_pairs * LANES), BF16), jax.ShapeDtypeStruct((t, n_pairs * LANES), F32)],
        scratch=[pltpu.VMEM((ns, SB_Q, LANES), F32), pltpu.VMEM((ns, 2 * SB_Q, LANES), F32)],
        sem=("parallel", "parallel", "arbitrary"), args=(proj, proj, proj), ride=ride)


def _sb_bwd(proj, q_col0, k_col0, v_col0, tot, do, batch, s, ride=None, streams=SB_BWD_STREAMS):
    t = proj.shape[0]
    nq = s // SB_Q
    n_pairs = 4
    ns = streams
    wide = ns * LANES
    scale = HEAD_DIM ** -0.5

    def body(q_ref, k_ref, v_ref, tot_ref, do_ref, dq_ref, dk_ref, dv_ref, dk_acc, dv_acc, dq_acc, seen_ref, gsum_ref):
        i = pl.program_id(2)
        lo_q = lax.broadcasted_iota(jnp.int32, (SB_Q, LANES), 1) < HEAD_DIM
        lo_k = _lane_lo()

        @pl.when(i == 0)
        def _():
            dk_acc[...] = jnp.zeros_like(dk_acc)
            dv_acc[...] = jnp.zeros_like(dv_acc)

        mat_after = _sb_consts(True)
        mat_before = _sb_consts(False)
        row = lax.broadcasted_iota(jnp.int32, (2 * SB_Q, LANES), 0) % SB_Q
        ahead = row - lax.broadcasted_iota(jnp.int32, (2 * SB_Q, LANES), 1)
        dq_acc[...] = jnp.zeros_like(dq_acc)
        seen_ref[...] = jnp.zeros_like(seen_ref)
        gsum_ref[...] = jnp.zeros_like(gsum_ref)
        qqs, dds, totals = [], [], []
        for st in range(ns):
            cols = slice(st * LANES, (st + 1) * LANES)
            qqs.append(_stack_heads(q_ref[:, cols] * scale, lo_q))
            dds.append(_stack_heads(do_ref[:, cols], lo_q))
            tot_t = tot_ref[:, cols]
            totals.append(jnp.concatenate([jnp.broadcast_to(tot_t[:, 0:1], (SB_Q, LANES)),
                                           jnp.broadcast_to(tot_t[:, HEAD_DIM:HEAD_DIM + 1], (SB_Q, LANES))], axis=0))

        def units(todo):
            def rows(j):
                return pl.ds(pl.multiple_of(j * BLOCK, BLOCK), BLOCK)

            def cols(st):
                return slice(st * LANES, (st + 1) * LANES)

            prods = [(lax.dot_general(qqs[st], k_ref[rows(j), cols(st)], NT, preferred_element_type=F32),
                      lax.dot_general(dds[st], v_ref[rows(j), cols(st)], NT, preferred_element_type=F32))
                     for st, j, _ in todo]
            logs = []
            for (z, _), (_, _, off) in zip(prods, todo):
                lsig = jnp.minimum(z, 0.0) - jnp.log(1.0 + jnp.exp(-jnp.abs(z)))
                lneg = lsig - z
                if off is not None:
                    lneg = jnp.where(ahead > off, lneg, 0.0)
                logs.append((lsig, _split(lneg)))
            sums = [lax.dot_general(cat, mat_after, NN, preferred_element_type=F32) for _, cat in logs]
            gates = []
            for (lsig, _), sm, (_, da), (st, _, off) in zip(logs, sums, prods, todo):
                seen = seen_ref[st]
                a = jnp.exp(lsig + (totals[st] - seen - sm[:, BLOCK:]) + sm[:, :BLOCK])
                if off is not None:
                    a = jnp.where(ahead > off, a, 0.0)
                seen_ref[st] = seen + sm[:, BLOCK:]
                g = a * da
                gates.append((a.astype(BF16), g, _split(g)))
            gsums = [lax.dot_general(cat, mat_before, NN, preferred_element_type=F32) for _, _, cat in gates]
            outs = []
            for (lsig, _), (ab, g, _), gs, (st, _, off) in zip(logs, gates, gsums, todo):
                gsum = gsum_ref[st]
                dz = g - jnp.exp(lsig) * (g + gsum + gs[:, :BLOCK])
                if off is not None:
                    dz = jnp.where(ahead > off, dz, 0.0)
                gsum_ref[st] = gsum + gs[:, BLOCK:]
                outs.append((dz.astype(BF16), ab))
            for (dzb, ab), (st, j, _) in zip(outs, todo):
                k = k_ref[rows(j), cols(st)]
                kz = jnp.zeros_like(k)
                dq_acc[st] += (lax.dot_general(dzb[:SB_Q], jnp.where(lo_k, k, kz), NN, preferred_element_type=F32)
                               + lax.dot_general(dzb[SB_Q:], jnp.where(lo_k, kz, k), NN, preferred_element_type=F32))
                dk_acc[rows(j), cols(st)] += lax.dot_general(dzb, qqs[st], TN, preferred_element_type=F32)
                dv_acc[rows(j), cols(st)] += lax.dot_general(ab, dds[st], TN, preferred_element_type=F32)

        def pair(p, carry):
            units([(st, 2 * p, None) for st in range(ns)] + [(st, 2 * p + 1, None) for st in range(ns)])
            return carry

        lax.fori_loop(0, i, pair, 0)
        units([(st, 2 * i, 0) for st in range(ns)] + [(st, 2 * i + 1, BLOCK) for st in range(ns)])
        for st in range(ns):
            dq_ref[:, st * LANES:(st + 1) * LANES] = (dq_acc[st] * scale).astype(BF16)

        @pl.when(i == nq - 1)
        def _():
            dk_ref[...] = dk_acc[...].astype(BF16)
            dv_ref[...] = dv_acc[...].astype(BF16)

    def seq_in(col0):
        return pl.BlockSpec((s, wide), lambda b, h, i: (b, col0 // ns + h), pipeline_mode=pl.Buffered(1))

    blk = pl.BlockSpec((SB_Q, wide), lambda b, h, i: (b * nq + i, h))
    seq = pl.BlockSpec((s, wide), lambda b, h, i: (b, h), pipeline_mode=pl.Buffered(1))
    out = jax.ShapeDtypeStruct((t, n_pairs * LANES), BF16)
    return _call(
        body, name="attn_b_bwd", grid=(batch, n_pairs // ns, nq),
        in_specs=[pl.BlockSpec((SB_Q, wide), lambda b, h, i: (b * nq + i, q_col0 // ns + h)), seq_in(k_col0),
                  seq_in(v_col0), blk, blk],
        out_specs=[blk, seq, seq], out_shape=[out, out, out],
        scratch=[pltpu.VMEM((s, wide), F32), pltpu.VMEM((s, wide), F32), pltpu.VMEM((ns, SB_Q, LANES), F32),
                 pltpu.VMEM((ns, 2 * SB_Q, LANES), F32), pltpu.VMEM((ns, 2 * SB_Q, LANES), F32)],
        sem=("parallel", "parallel", "arbitrary"), args=(proj, proj, proj, tot, do), ride=ride)


MEM_Q_TILE = 512


def _mem_fwd(q, kv, batch, s, n_mem):
    t, width = q.shape
    tq = min(MEM_Q_TILE, s)
    nq = s // tq
    scale = MEM_HEAD_DIM ** -0.5

    def body(q_ref, kv_ref, o_ref):
        for h in range(N_HEADS_MEM):
            cols = slice(h * MEM_HEAD_DIM, (h + 1) * MEM_HEAD_DIM)
            k = kv_ref[:, cols]
            v = kv_ref[:, width + h * MEM_HEAD_DIM: width + (h + 1) * MEM_HEAD_DIM]
            sc = lax.dot_general(q_ref[:, cols], k, NT, preferred_element_type=F32) * scale
            p = jnp.exp(sc - jnp.max(sc, axis=1, keepdims=True))
            p = p / jnp.sum(p, axis=1, keepdims=True)
            o_ref[:, cols] = lax.dot_general(p.astype(BF16), v, NN, preferred_element_type=F32).astype(BF16)

    return pl.pallas_call(
        body, name="mem_attn_fwd", grid=(batch, nq),
        in_specs=[pl.BlockSpec((tq, width), lambda b, i: (b * nq + i, 0)),
                  pl.BlockSpec((n_mem, 2 * width), lambda b, i: (b, 0))],
        out_specs=pl.BlockSpec((tq, width), lambda b, i: (b * nq + i, 0)),
        out_shape=jax.ShapeDtypeStruct((t, width), BF16),
        compiler_params=_params(("parallel", "parallel")),
    )(q, kv)


def _mem_bwd(q, kv, do, batch, s, n_mem):
    t, width = q.shape
    tq = min(MEM_Q_TILE, s)
    nq = s // tq
    scale = MEM_HEAD_DIM ** -0.5

    def body(q_ref, kv_ref, do_ref, dq_ref, dkv_ref, acc):
        i = pl.program_id(1)

        @pl.when(i == 0)
        def _():
            acc[...] = jnp.zeros_like(acc)

        for h in range(N_HEADS_MEM):
            cols = slice(h * MEM_HEAD_DIM, (h + 1) * MEM_HEAD_DIM)
            vcols = slice(width + h * MEM_HEAD_DIM, width + (h + 1) * MEM_HEAD_DIM)
            qh, k, v, doh = q_ref[:, cols], kv_ref[:, cols], kv_ref[:, vcols], do_ref[:, cols]
            sc = lax.dot_general(qh, k, NT, preferred_element_type=F32) * scale
            p = jnp.exp(sc - jnp.max(sc, axis=1, keepdims=True))
            p = p / jnp.sum(p, axis=1, keepdims=True)
            dp = lax.dot_general(doh, v, NT, preferred_element_type=F32)
            ds = (p * (dp - jnp.sum(p * dp, axis=1, keepdims=True)) * scale).astype(BF16)
            dq_ref[:, cols] = lax.dot_general(ds, k, NN, preferred_element_type=F32).astype(BF16)
            acc[:, cols] += lax.dot_general(ds, qh, TN, preferred_element_type=F32)
            acc[:, vcols] += lax.dot_general(p.astype(BF16), doh, TN, preferred_element_type=F32)

        @pl.when(i == nq - 1)
        def _():
            dkv_ref[...] = acc[...].astype(BF16)

    row = pl.BlockSpec((tq, width), lambda b, i: (b * nq + i, 0))
    kvs = pl.BlockSpec((n_mem, 2 * width), lambda b, i: (b, 0))
    return pl.pallas_call(
        body, name="mem_attn_bwd", grid=(batch, nq),
        in_specs=[row, kvs, row], out_specs=[row, kvs],
        out_shape=[jax.ShapeDtypeStruct((t, width), BF16), jax.ShapeDtypeStruct((batch * n_mem, 2 * width), BF16)],
        scratch_shapes=[pltpu.VMEM((n_mem, 2 * width), F32)],
        compiler_params=_params(("parallel", "arbitrary")),
    )(q, kv, do)


def _mixer_fwd(o_a, o_b, w_a, w_b, proj, gate_col0):
    t, width = o_a.shape
    d = w_a.shape[1]
    tm = min(ROW_TILE, t)
    gb0 = gate_col0 * LANES // d

    def body(oa_ref, ob_ref, wa_ref, wb_ref, ga_ref, gb_ref, ua_ref, ub_ref, mix_ref):
        ua = lax.dot_general(oa_ref[...], wa_ref[...], NN, preferred_element_type=F32)
        ub = lax.dot_general(ob_ref[...], wb_ref[...], NN, preferred_element_type=F32)
        ua_ref[...] = ua.astype(BF16)
        ub_ref[...] = ub.astype(BF16)
        mix_ref[...] = (jax.nn.sigmoid(ga_ref[...].astype(F32)) * ua
                        + jax.nn.sigmoid(gb_ref[...].astype(F32)) * ub).astype(BF16)

    row = pl.BlockSpec((tm, width), lambda i: (i, 0))
    wsp = pl.BlockSpec((width, d), lambda i: (0, 0))
    out = pl.BlockSpec((tm, d), lambda i: (i, 0))
    osh = jax.ShapeDtypeStruct((t, d), BF16)
    return pl.pallas_call(
        body, name="mixer_fwd", grid=(t // tm,),
        in_specs=[row, row, wsp, wsp,
                  pl.BlockSpec((tm, d), lambda i: (i, gb0)), pl.BlockSpec((tm, d), lambda i: (i, gb0 + 1))],
        out_specs=[out, out, out], out_shape=[osh, osh, osh],
        compiler_params=_params(("parallel",)),
    )(o_a, o_b, w_a, w_b, proj, proj)


def _mixer_bwd(dh, w_out, ua, ub, proj, gate_col0):
    t, d = dh.shape
    tm = min(ROW_TILE, t)
    nc = d // LANES

    def body(dh_ref, w_ref, ua_ref, ub_ref, ga_ref, gb_ref, dua_ref, dub_ref, dg_ref):
        dm = lax.dot_general(dh_ref[...], w_ref[...], NT, preferred_element_type=F32)
        sa = jax.nn.sigmoid(ga_ref[...].astype(F32))
        sb = jax.nn.sigmoid(gb_ref[...].astype(F32))
        dua_ref[...] = (dm * sa).astype(BF16)
        dub_ref[...] = (dm * sb).astype(BF16)
        dg_ref[:, 0:d] = (dm * ua_ref[...].astype(F32) * sa * (1.0 - sa)).astype(BF16)
        dg_ref[:, d:2 * d] = (dm * ub_ref[...].astype(F32) * sb * (1.0 - sb)).astype(BF16)

    row = pl.BlockSpec((tm, d), lambda i: (i, 0))
    return pl.pallas_call(
        body, name="mixer_bwd", grid=(t // tm,),
        in_specs=[row, pl.BlockSpec((d, d), lambda i: (0, 0)), row, row,
                  pl.BlockSpec((tm, d), lambda i: (i, gate_col0 // nc)),
                  pl.BlockSpec((tm, d), lambda i: (i, gate_col0 // nc + 1))],
        out_specs=[row, row, pl.BlockSpec((tm, 2 * d), lambda i: (i, 0))],
        out_shape=[jax.ShapeDtypeStruct((t, d), BF16), jax.ShapeDtypeStruct((t, d), BF16),
                   jax.ShapeDtypeStruct((t, 2 * d), BF16)],
        compiler_params=_params(("parallel",)),
    )(dh, w_out, ua, ub, proj, proj)


FFN_COLS = 1024


def _ffn_up(n, w_gate, w_up):
    t, d = n.shape
    hidden = w_gate.shape[0]
    tm = min(ROW_TILE, t)
    tn = min(FFN_COLS, hidden)

    def body(n_ref, wg_ref, wu_ref, hg_ref, hu_ref, act_ref):
        hg = lax.dot_general(n_ref[...], wg_ref[...], NT, preferred_element_type=F32)
        hu = lax.dot_general(n_ref[...], wu_ref[...], NT, preferred_element_type=F32)
        hg_ref[...] = hg.astype(BF16)
        hu_ref[...] = hu.astype(BF16)
        act_ref[...] = (hg * jax.nn.sigmoid(hg) * hu).astype(BF16)

    wsp = pl.BlockSpec((tn, d), lambda j, i: (j, 0))
    out = pl.BlockSpec((tm, tn), lambda j, i: (i, j))
    osh = jax.ShapeDtypeStruct((t, hidden), BF16)
    return pl.pallas_call(
        body, name="ffn_up", grid=(hidden // tn, t // tm),
        in_specs=[pl.BlockSpec((tm, d), lambda j, i: (i, 0)), wsp, wsp],
        out_specs=[out, out, out], out_shape=[osh, osh, osh],
        compiler_params=_params(("parallel", "parallel")),
    )(n, w_gate, w_up)


def _ffn_bwd(dh, w_down, w_gate, w_up, hg, hu, x, g, dres):
    t, d = dh.shape
    hidden = w_down.shape[0]
    tm = min(ROW_TILE, t)
    tn = min(FFN_COLS, hidden)
    nj = hidden // tn

    def body(dh_ref, wd_ref, wg_ref, wu_ref, hg_ref, hu_ref, x_ref, g_ref, r_ref, dhg_ref, dhu_ref, dx_ref, dxb_ref,
             dg_ref, acc):
        j, i = pl.program_id(0), pl.program_id(1)
        dact = lax.dot_general(dh_ref[...], wd_ref[...], NT, preferred_element_type=F32)
        hg = hg_ref[...].astype(F32)
        sg = jax.nn.sigmoid(hg)
        dhu = (dact * hg * sg).astype(BF16)
        dhg = (dact * hu_ref[...].astype(F32) * sg * (1.0 + hg * (1.0 - sg))).astype(BF16)
        dhu_ref[...] = dhu
        dhg_ref[...] = dhg
        part = (lax.dot_general(dhg, wg_ref[...], NN, preferred_element_type=F32)
                + lax.dot_general(dhu, wu_ref[...], NN, preferred_element_type=F32))

        @pl.when(j == 0)
        def _():
            acc[i] = part

        @pl.when(j > 0)
        def _():
            acc[i] += part

        @pl.when(jnp.logical_and(j == 0, i == 0))
        def _():
            dg_ref[...] = jnp.zeros_like(dg_ref)

        @pl.when(j == nj - 1)
        def _():
            dx, dg = _rms_bwd_rows(acc[i], x_ref[...], g_ref[...], r_ref[...])
            dx_ref[...] = dx
            dxb_ref[...] = dx.astype(BF16)
            dg_ref[...] += dg

    hid = pl.BlockSpec((tm, tn), lambda j, i: (i, j))
    wsp = pl.BlockSpec((tn, d), lambda j, i: (j, 0), pipeline_mode=pl.Buffered(1))
    late = pl.BlockSpec((tm, d), lambda j, i: (jnp.where(j == nj - 1, i, 0), 0))
    vec = pl.BlockSpec((1, d), lambda j, i: (0, 0))
    osh = jax.ShapeDtypeStruct((t, hidden), BF16)
    return pl.pallas_call(
        body, name="ffn_bwd", grid=(nj, t // tm),
        in_specs=[pl.BlockSpec((tm, d), lambda j, i: (i, 0)), wsp, wsp, wsp, hid, hid, late, vec, late],
        out_specs=[hid, hid, late, late, vec],
        out_shape=[osh, osh, jax.ShapeDtypeStruct((t, d), F32), jax.ShapeDtypeStruct((t, d), BF16),
                   jax.ShapeDtypeStruct((1, d), F32)],
        scratch_shapes=[pltpu.VMEM((t // tm, tm, d), F32)],
        compiler_params=_params(("arbitrary", "arbitrary")),
    )(dh, w_down, w_gate, w_up, hg, hu, x, g, dres)


MM_ROWS = 1024


def _mm_w(name, a, w, out_dtype, dims=NN):
    t, k = a.shape
    n = w.shape[1] if dims == NN else w.shape[0]
    tm, tn = min(MM_ROWS, t), min(1024, n)
    o_spec = pl.BlockSpec((tm, tn), lambda j, i: (i, j))
    b_spec = pl.BlockSpec((k, tn), lambda j, i: (0, j)) if dims == NN else pl.BlockSpec((tn, k), lambda j, i: (j, 0))
    return _mm(name, a, w, grid=(n // tn, t // tm), a_spec=pl.BlockSpec((tm, k), lambda j, i: (i, 0)), b_spec=b_spec,
               o_shape=(t, n), o_spec=o_spec, dims=dims, out_dtype=out_dtype)


def _mm_res_norm(name, a, w, res, g):
    t, k = a.shape
    d = w.shape[1]
    tm = min(ROW_TILE, t)

    def body(a_ref, w_ref, r_ref, g_ref, h_ref, n_ref):
        h = lax.dot_general(a_ref[...], w_ref[...], NN, preferred_element_type=F32) + r_ref[...]
        h_ref[...] = h
        r = lax.rsqrt(jnp.mean(h * h, axis=-1, keepdims=True) + RMS_EPS)
        n_ref[...] = (h * r * g_ref[...]).astype(BF16)

    row = pl.BlockSpec((tm, d), lambda i: (i, 0))
    return pl.pallas_call(
        body, name=name, grid=(t // tm,),
        in_specs=[pl.BlockSpec((tm, k), lambda i: (i, 0)), pl.BlockSpec((k, d), lambda i: (0, 0)), row,
                  pl.BlockSpec((1, d), lambda i: (0, 0))],
        out_specs=[row, row], out_shape=[jax.ShapeDtypeStruct((t, d), F32), jax.ShapeDtypeStruct((t, d), BF16)],
        compiler_params=_params(("parallel",)),
    )(a, w, res, g)


def _wgrad(name, a, g, tk=1024, tn=1024):
    t, k = a.shape
    n = g.shape[1]
    tm, tk, tn = min(2 * MM_ROWS, t), min(tk, k), min(tn, n)
    return _mm(name, a, g, grid=(k // tk, n // tn, t // tm),
               a_spec=pl.BlockSpec((tm, tk), lambda p, q, r: (r, p)), b_spec=pl.BlockSpec((tm, tn), lambda p, q, r: (r, q)),
               o_shape=(k, n), o_spec=pl.BlockSpec((tk, tn), lambda p, q, r: (p, q)), dims=TN, out_dtype=BF16, nk=t // tm)


def _peers():
    x, y, c = lax.axis_index("x"), lax.axis_index("y"), lax.axis_index("c")
    me = 4 * x + 2 * y + c
    out = []
    for k in range(1, N_DEV):
        kx, ky, kc = (k >> 2) & 1, (k >> 1) & 1, k & 1
        px = 1 - x if kx else x
        py = 1 - y if ky else y
        pc = 1 - c if kc else c
        out.append(((px, py, pc), 4 * px + 2 * py + pc))
    return me, out


def _cast_weights(ws, pad_rows):
    def body(*refs):
        n = len(refs) // 2
        for i_ref, o_ref, pr in zip(refs[:n], refs[n:], pad_rows):
            r, c = i_ref.shape
            o_ref[0:r, :] = i_ref[...].astype(BF16)
            if pr:
                o_ref[r:r + pr, :] = jnp.zeros((pr, c), BF16)

    return pl.pallas_call(
        body, name="cast_weights", in_specs=[VMEM] * len(ws), out_specs=[VMEM] * len(ws),
        out_shape=[jax.ShapeDtypeStruct((w.shape[0] + pr, w.shape[1]), BF16) for w, pr in zip(ws, pad_rows)],
    )(*ws)


def _window(ref, j, c):
    return ref.at[:, pl.ds(pl.multiple_of(j * c, LANES), c)]


def _scatter_copies(ins, outs, sems, cols, landed):
    send_sems, recv_sems, loc_sems = sems
    n_peer = N_DEV - 1
    me, peers = _peers()

    def src(w, j):
        return _window(ins[w], j, cols[w]) if cols[w] else ins[w].at[j]

    local = [pltpu.make_async_copy(src(w, me), outs[w].at[me], loc_sems.at[w]) for w in range(len(ins))]
    remote = [pltpu.make_async_remote_copy(
        src_ref=src(w, idx), dst_ref=outs[w].at[idx if landed else me],
        send_sem=send_sems.at[w * n_peer + k], recv_sem=recv_sems.at[w * n_peer + k],
        device_id=dev, device_id_type=pl.DeviceIdType.MESH)
        for k, (dev, idx) in reversed(list(enumerate(peers))) for w in range(len(ins))]
    return local, remote


OTHER_CHIPS = (2, 4, 6)


def _gather_copies(ins, outs, sems, cols):
    send_sems, recv_sems, loc_sems = sems
    x, y, c = lax.axis_index("x"), lax.axis_index("y"), lax.axis_index("c")
    me = 4 * x + 2 * y + c
    n_pair = N_DEV - 1

    def dev(mask):
        return (1 - x if mask & 4 else x, 1 - y if mask & 2 else y, 1 - c if mask & 1 else c)

    def slot(w, mask):
        j = jnp.bitwise_xor(me, mask)
        return _window(outs[w], j, cols[w]) if cols[w] else outs[w].at[j]

    def remote(w, pair, src, to_slot, target):
        return pltpu.make_async_remote_copy(src_ref=src, dst_ref=slot(w, to_slot), send_sem=send_sems.at[w * n_pair + pair],
                                            recv_sem=recv_sems.at[w * n_pair + pair], device_id=dev(target),
                                            device_id_type=pl.DeviceIdType.MESH)

    ws = range(len(ins))
    return dict(
        local=[pltpu.make_async_copy(ins[w], slot(w, 0), loc_sems.at[w]) for w in ws],
        to_chips=[remote(w, 1 + t, ins[w], 0, m) for t, m in enumerate(OTHER_CHIPS) for w in ws],
        to_core=[remote(w, 0, ins[w], 0, 1) for w in ws],
        from_chips=[remote(w, 1 + t, ins[w], m, 0) for t, m in enumerate(OTHER_CHIPS) for w in ws],
        pass_on=[remote(w, 4 + t, slot(w, m), m, 1) for t, m in enumerate(OTHER_CHIPS) for w in ws],
        from_core=[remote(w, 0, ins[w], 1, 0) for w in ws]
        + [remote(w, 4 + t, ins[w], m + 1, 0) for t, m in enumerate(OTHER_CHIPS) for w in ws])


def _exchange_start(ins, outs, sems, gather, cols):
    if gather:
        cps = _gather_copies(ins, outs, sems, cols)
        for cp in cps["local"] + cps["to_chips"] + cps["to_core"]:
            cp.start()
    else:
        local, remote = _scatter_copies(ins, outs, sems, cols, False)
        for cp in local + remote:
            cp.start()


def _exchange_pass_on(ins, outs, sems, gather, cols, chips):
    if gather:
        cps = _gather_copies(ins, outs, sems, cols)
        n = len(ins)
        for t in chips:
            for arrived, onward in zip(cps["from_chips"][t * n:(t + 1) * n], cps["pass_on"][t * n:(t + 1) * n]):
                arrived.wait_recv()
                onward.start()


def _exchange_wait(ins, outs, sems, gather, cols):
    if gather:
        cps = _gather_copies(ins, outs, sems, cols)
        for cp in cps["local"]:
            cp.wait()
        for cp in cps["to_chips"] + cps["to_core"] + cps["pass_on"]:
            cp.wait_send()
        for cp in cps["from_core"]:
            cp.wait_recv()
    else:
        local, remote = _scatter_copies(ins, outs, sems, cols, True)
        for cp in local:
            cp.wait()
        for cp in remote:
            cp.wait_send()
            cp.wait_recv()


def _exchange_shapes(arrs, gather, cols):
    n = len(arrs)
    out_shape = []
    for a, c in zip(arrs, cols):
        if gather:
            shape = (a.shape[0], N_DEV * c) if c else (N_DEV,) + a.shape
        else:
            shape = (N_DEV, a.shape[0], c) if c else a.shape
        out_shape.append(jax.ShapeDtypeStruct(shape, a.dtype))
    sems = [pltpu.SemaphoreType.DMA((n * (N_DEV - 1),)), pltpu.SemaphoreType.DMA((n * (N_DEV - 1),)),
            pltpu.SemaphoreType.DMA((n,))]
    return out_shape, sems


def _call(body, *, name, grid, in_specs, out_specs, out_shape, scratch, sem, args, ride=None):
    if ride is None:
        outs = pl.pallas_call(body, name=name, grid=grid, in_specs=in_specs, out_specs=out_specs, out_shape=out_shape,
                              scratch_shapes=scratch, compiler_params=_params(sem))(*args)
        return outs, None
    arrs, gather, cols = ride
    n, n_in, n_out, n_scr = len(arrs), len(in_specs), len(out_specs), len(scratch)
    x_shape, x_sems = _exchange_shapes(arrs, gather, cols)

    def riding(*refs):
        ins, x_ins = refs[:n_in], refs[n_in:n_in + n]
        outs = refs[n_in + n:n_in + n + n_out]
        x_outs = refs[n_in + n + n_out:n_in + 2 * n + n_out]
        scr = refs[n_in + 2 * n + n_out:n_in + 2 * n + n_out + n_scr]
        sems = refs[n_in + 2 * n + n_out + n_scr:]
        def at(step):
            return functools.reduce(jnp.logical_and, [pl.program_id(a) == v for a, v in enumerate(step)])

        @pl.when(at((0,) * len(grid)))
        def _():
            _exchange_start(x_ins, x_outs, sems, gather, cols)

        @pl.when(at((grid[0] // 2,) + (0,) * (len(grid) - 1)))
        def _():
            _exchange_pass_on(x_ins, x_outs, sems, gather, cols, (0, 1))

        @pl.when(at((grid[0] // 2,) + (0,) * (len(grid) - 2) + (5 * grid[-1] // 8,)))
        def _():
            _exchange_pass_on(x_ins, x_outs, sems, gather, cols, (2,))

        body(*ins, *outs, *scr)

        @pl.when(at(tuple(g - 1 for g in grid)))
        def _():
            _exchange_wait(x_ins, x_outs, sems, gather, cols)

    res = pl.pallas_call(
        riding, name=name, grid=grid, in_specs=list(in_specs) + [ANY] * n, out_specs=list(out_specs) + [ANY] * n,
        out_shape=list(out_shape) + x_shape, scratch_shapes=list(scratch) + x_sems,
        compiler_params=_params(("arbitrary",) * len(grid)))(*args, *arrs)
    return res[:n_out], res[n_out:]


def _my_block():
    return (4 * lax.axis_index("x") + 2 * lax.axis_index("y") + lax.axis_index("c")).astype(jnp.int32).reshape(1)


def _proj_in_gather(n, w_shard):
    t, k = n.shape
    cs = w_shard.shape[1]
    tm = min(MM_ROWS, t)
    ni = t // tm
    arrival = (0, 1) + OTHER_CHIPS + tuple(m + 1 for m in OTHER_CHIPS)

    def mask_at(s):
        return jnp.where(s < 2, s, jnp.where(s < 5, 2 * (s - 1), 2 * (s - 4) + 1))

    def body(me_ref, n_ref, w_hbm, o_ref, all_hbm, w_vmem, send_sems, recv_sems, loc_sems, load_sems):
        s, i = pl.program_id(0), pl.program_id(1)
        cps = _gather_copies([w_hbm], [all_hbm], (send_sems, recv_sems, loc_sems), (cs,))
        arrived = cps["local"] + cps["from_core"][:1] + cps["from_chips"] + cps["from_core"][1:]

        def load(step):
            src = w_hbm if step == 0 else _window(all_hbm, jnp.bitwise_xor(me_ref[0], arrival[step]), cs)
            return pltpu.make_async_copy(src, w_vmem.at[step % 2], load_sems.at[step % 2])

        @pl.when(jnp.logical_and(s == 0, i == 0))
        def _():
            for cp in cps["local"] + cps["to_chips"] + cps["to_core"]:
                cp.start()
            load(0).start()

        for step, mask in enumerate(arrival):
            @pl.when(jnp.logical_and(s == step, i == 0))
            def _(step=step):
                load(step).wait()

            if step + 1 < N_DEV:
                @pl.when(jnp.logical_and(s == step, i == min(1, ni - 1)))
                def _(step=step):
                    arrived[step + 1].wait_recv()
                    if arrival[step + 1] in OTHER_CHIPS:
                        cps["pass_on"][OTHER_CHIPS.index(arrival[step + 1])].start()
                    load(step + 1).start()

        o_ref[...] = lax.dot_general(n_ref[...], w_vmem[s % 2], NN, preferred_element_type=F32).astype(BF16)

        @pl.when(jnp.logical_and(s == N_DEV - 1, i == ni - 1))
        def _():
            cps["local"][0].wait()
            for cp in cps["to_chips"] + cps["to_core"] + cps["pass_on"]:
                cp.wait_send()

    return pl.pallas_call(
        body, name="proj_in",
        grid_spec=pltpu.PrefetchScalarGridSpec(
            num_scalar_prefetch=1, grid=(N_DEV, ni),
            in_specs=[pl.BlockSpec((tm, k), lambda s, i, me: (i, 0)), ANY],
            out_specs=[pl.BlockSpec((tm, cs), lambda s, i, me: (i, jnp.bitwise_xor(me[0], mask_at(s)))), ANY],
            scratch_shapes=[pltpu.VMEM((2, k, cs), BF16), pltpu.SemaphoreType.DMA((N_DEV - 1,)),
                            pltpu.SemaphoreType.DMA((N_DEV - 1,)), pltpu.SemaphoreType.DMA((1,)),
                            pltpu.SemaphoreType.DMA((2,))]),
        out_shape=[jax.ShapeDtypeStruct((t, N_DEV * cs), BF16), jax.ShapeDtypeStruct((k, N_DEV * cs), BF16)],
        compiler_params=_params(("arbitrary", "arbitrary")),
    )(_my_block(), n, w_shard)


def _gw_in_scatter(a, g):
    t, k = a.shape
    cs = g.shape[1] // N_DEV
    tm = min(MM_ROWS, t)
    nr = t // tm
    n_chip = N_DEV // 2
    chips = (6, 4, 2, 0)

    def body(me_ref, a_ref, g_ref, out_hbm, acc, stage, other, core_send, core_recv, chip_send, chip_recv, loc_sem):
        s, r = pl.program_id(0), pl.program_id(1)
        x, y, c = lax.axis_index("x"), lax.axis_index("y"), lax.axis_index("c")
        my_chip = 2 * x + y
        part = lax.dot_general(a_ref[...], g_ref[...], TN, preferred_element_type=F32)

        def to_core(m):
            return pltpu.make_async_remote_copy(src_ref=stage.at[0], dst_ref=other.at[m], send_sem=core_send.at[m],
                                                recv_sem=core_recv.at[m], device_id=(x, y, 1 - c),
                                                device_id_type=pl.DeviceIdType.MESH)

        def to_chip(m, landed):
            mask = chips[m]
            there = (1 - x if mask & 4 else x, 1 - y if mask & 2 else y, c)
            slot = (2 * there[0] + there[1]) if landed else my_chip
            return pltpu.make_async_remote_copy(src_ref=stage.at[1], dst_ref=out_hbm.at[slot], send_sem=chip_send.at[m],
                                                recv_sem=chip_recv.at[m], device_id=there,
                                                device_id_type=pl.DeviceIdType.MESH)

        local = pltpu.make_async_copy(stage.at[1], out_hbm.at[my_chip], loc_sem)

        @pl.when(r == 0)
        def _():
            acc[...] = part

        @pl.when(r > 0)
        def _():
            acc[...] += part

        for step in range(N_DEV):
            m = step // 2

            @pl.when(jnp.logical_and(s == step, r == nr - 1))
            def _(step=step, m=m):
                if step % 2 == 0:
                    if m > 0:
                        to_core(m - 1).wait_send()
                    stage[0] = acc[...].astype(BF16)
                    to_core(m).start()
                else:
                    if m > 0:
                        to_chip(m - 1, False).wait_send()
                    to_core(m).wait_recv()
                    stage[1] = (acc[...] + other[m].astype(F32)).astype(BF16)
                    if m < n_chip - 1:
                        to_chip(m, False).start()
                    else:
                        local.start()
                        to_core(m).wait_send()
                        local.wait()
                        for mm in range(n_chip - 1):
                            to_chip(mm, True).wait_recv()

    return pl.pallas_call(
        body, name="gw_in",
        grid_spec=pltpu.PrefetchScalarGridSpec(
            num_scalar_prefetch=1, grid=(N_DEV, nr),
            in_specs=[pl.BlockSpec((tm, k), lambda s, r, me: (r, 0)),
                      pl.BlockSpec((tm, cs), lambda s, r, me: (r, jnp.bitwise_xor(me[0], N_DEV - 1 - s)))],
            out_specs=ANY,
            scratch_shapes=[pltpu.VMEM((k, cs), F32), pltpu.VMEM((2, k, cs), BF16), pltpu.VMEM((n_chip, k, cs), BF16),
                            pltpu.SemaphoreType.DMA((n_chip,)), pltpu.SemaphoreType.DMA((n_chip,)),
                            pltpu.SemaphoreType.DMA((n_chip - 1,)), pltpu.SemaphoreType.DMA((n_chip - 1,)),
                            pltpu.SemaphoreType.DMA]),
        out_shape=jax.ShapeDtypeStruct((n_chip, k, cs), BF16),
        compiler_params=_params(("arbitrary", "arbitrary")),
    )(_my_block(), a, g)


SMALL_ROWS = 8


def _allreduce_small(parts, loss_part):
    n, d = len(parts), parts[0].shape[1]

    def body(*refs):
        part_refs, loss_ref, o_ref = refs[:n], refs[n], refs[n + 1]
        mine_ref, all_ref, send_sems, recv_sems = refs[n + 2:]
        me, peers = _peers()
        mine_ref[...] = jnp.zeros_like(mine_ref)
        for i, p_ref in enumerate(part_refs):
            mine_ref[i:i + 1, :] = p_ref[...]
        mine_ref[SMALL_ROWS - 1:SMALL_ROWS, 0:LANES] = loss_ref[0:1, :]
        all_ref[me] = mine_ref[...]
        for k, (dev, idx) in enumerate(peers):
            pltpu.make_async_remote_copy(src_ref=mine_ref, dst_ref=all_ref.at[me], send_sem=send_sems.at[k],
                                         recv_sem=recv_sems.at[k], device_id=dev,
                                         device_id_type=pl.DeviceIdType.MESH).start()
        for k, (dev, idx) in enumerate(peers):
            cp = pltpu.make_async_remote_copy(src_ref=mine_ref, dst_ref=all_ref.at[idx], send_sem=send_sems.at[k],
                                              recv_sem=recv_sems.at[k], device_id=dev,
                                              device_id_type=pl.DeviceIdType.MESH)
            cp.wait_send()
            cp.wait_recv()
        tot = all_ref[0]
        for dvc in range(1, N_DEV):
            tot = tot + all_ref[dvc]
        o_ref[...] = tot

    return pl.pallas_call(
        body, name="allreduce_small", in_specs=[VMEM] * (n + 1), out_specs=VMEM,
        out_shape=jax.ShapeDtypeStruct((SMALL_ROWS, d), F32),
        scratch_shapes=[pltpu.VMEM((SMALL_ROWS, d), F32), pltpu.VMEM((N_DEV, SMALL_ROWS, d), F32),
                        pltpu.SemaphoreType.DMA((N_DEV - 1,)), pltpu.SemaphoreType.DMA((N_DEV - 1,))],
    )(*parts, loss_part)


def _adam_math(g, w, m, v):
    m_new = ADAM_B1 * m + (1.0 - ADAM_B1) * g
    v_new = ADAM_B2 * v + (1.0 - ADAM_B2) * (g * g)
    m_hat = m_new / (1.0 - ADAM_B1 ** ADAM_STEP)
    v_hat = v_new / (1.0 - ADAM_B2 ** ADAM_STEP)
    delta = -ADAM_LR * (m_hat / (jnp.sqrt(v_hat) + ADAM_EPS) + ADAM_WD * w)
    return delta, m_new, v_new


def _adam(name, pieces, w, m, v):
    r, c = w.shape
    n_piece, _, cp = pieces.shape
    tr = r
    for cand in (256, 176, 128, 64):
        if r % cand == 0 and r > cand:
            tr = cand
            break

    def body(p_ref, w_ref, m_ref, v_ref, g_ref, d_ref, mo_ref, vo_ref):
        g = p_ref[0, :, 0:c].astype(F32)
        for j in range(1, n_piece):
            g = g + p_ref[j, :, 0:c].astype(F32)
        delta, m_new, v_new = _adam_math(g, w_ref[...], m_ref[...], v_ref[...])
        g_ref[...] = g
        d_ref[...] = delta
        mo_ref[...] = m_new
        vo_ref[...] = v_new

    blk = pl.BlockSpec((tr, c), lambda i: (i, 0))
    osh = jax.ShapeDtypeStruct((r, c), F32)
    return pl.pallas_call(
        body, name=name, grid=(r // tr,),
        in_specs=[pl.BlockSpec((n_piece, tr, cp), lambda i: (0, i, 0)), blk, blk, blk],
        out_specs=[blk, blk, blk, blk], out_shape=[osh, osh, osh, osh],
        compiler_params=_params(("parallel",)),
    )(pieces, w, m, v)


def _adam_small(g_all, ws, ms, vs):
    n = len(ws)

    def body(*refs):
        g_ref, ins, outs = refs[0], refs[1:1 + 3 * n], refs[1 + 3 * n:]
        for i in range(n):
            g = g_ref[i:i + 1, :]
            delta, m_new, v_new = _adam_math(g, ins[i][...], ins[n + i][...], ins[2 * n + i][...])
            for kind, val in enumerate((g, delta, m_new, v_new)):
                outs[kind * n + i][...] = val

    osh = jax.ShapeDtypeStruct(ws[0].shape, F32)
    res = pl.pallas_call(body, name="adam_small", in_specs=[VMEM] * (1 + 3 * n), out_specs=[VMEM] * (4 * n),
                         out_shape=[osh] * (4 * n))(g_all, *ws, *ms, *vs)
    return res[:n], res[n:2 * n], res[2 * n:3 * n], res[3 * n:]


def _local_step(x, mem, pos, tgt, gains, w_in_shard, shards, batch):
    g_mix, g_mem_q, g_mem_kv, g_ffn, g_final = gains
    t, d = x.shape
    s = t // batch
    n_mem = mem.shape[0] // batch
    n_sh = N_DEV
    width = shards[0].shape[0]
    nb = width // LANES

    lane = np.arange(LANES) % HEAD_DIM
    sel_lo = (lane < ROPE_HALF).astype(np.float32)[None, :]
    sel_hi = ((lane >= ROPE_HALF) & (lane < 2 * ROPE_HALF)).astype(np.float32)[None, :]
    freqs = np.float32(ROPE_THETA) ** (-np.arange(ROPE_HALF, dtype=np.float32) / np.float32(ROPE_HALF))
    inv_freq = np.where(lane < 2 * ROPE_HALF, freqs[lane % ROPE_HALF], 0.0).astype(np.float32)[None, :]
    cos_t, sin_a, sin_b = _rope_tables(pos, jnp.asarray(inv_freq), jnp.asarray(sel_lo), jnp.asarray(sel_hi))
    bias = _dilated_bias_tiles(s)

    n1 = _rms_fwd("norm_mix", x, g_mix)
    proj, w_in = _proj_in_gather(n1, w_in_shard)
    qk_a = _rope_apply("rope_fwd", [proj], 2 * width, cos_t, sin_a, sin_b, 1.0)
    cs_up = shards[0].shape[1]
    (o_a, lse_a), (w_up_a, w_up_b, w_out, w_q, w_kv, w_o, w_fd) = _da_fwd(
        qk_a, proj, 2 * nb, bias, batch, s,
        ride=(shards[:6] + shards[8:], True, (cs_up, cs_up, 0, 0, 0, cs_up, 0)))
    (o_b, tot_b), (w_fg, w_fu) = _sb_fwd(proj, 3 * nb, 4 * nb, 5 * nb, batch, s, ride=(shards[6:8], True, (0, 0)))
    w_out = w_out.reshape(d, d)
    w_q = w_q.reshape(d, -1)
    w_kv = w_kv.reshape(d, -1)
    w_fd = w_fd.reshape(-1, d)
    w_fg = w_fg.reshape(-1, d)
    w_fu = w_fu.reshape(-1, d)
    ua, ub, mixed = _mixer_fwd(o_a, o_b, w_up_a, w_up_b, proj, 6 * nb)
    h1, n2 = _mm_res_norm("mix_out", mixed, w_out, x, g_mem_q)
    mem_n = _rms_fwd("norm_mem_kv", mem, g_mem_kv)
    q_m = _mm_w("mem_q", n2, w_q, BF16)
    kv_m = _mm_w("mem_kv", mem_n, w_kv, BF16)
    o_m = _mem_fwd(q_m, kv_m, batch, s, n_mem)
    h2, n3 = _mm_res_norm("mem_out", o_m, w_o, h1, g_ffn)
    hg, hu, act = _ffn_up(n3, w_fg, w_fu)
    loss_part, dh3, dh3_b, dg_final = _loss_head(act, w_fd, h2, tgt, g_final.reshape(1, d))

    dhg, dhu, dh2, dh2_b, dg_ffn = _ffn_bwd(dh3_b, w_fd, w_fg, w_fu, hg, hu, h2, g_ffn, dh3)
    gw_fd = _wgrad("gw_ffn_down", act, dh3_b)
    gw_fg = _wgrad("gw_ffn_gate", dhg, n3)
    gw_fu = _wgrad("gw_ffn_up", dhu, n3)

    do_m = _mm_w("mem_out_bwd", dh2_b, w_o, BF16, dims=NT)
    gw_o = _wgrad("gw_mem_o", o_m, dh2_b)
    dq_m, dkv_m = _mem_bwd(q_m, kv_m, do_m, batch, s, n_mem)
    gw_q = _wgrad("gw_mem_q", n2, dq_m)
    gw_kv = _wgrad("gw_mem_kv", mem_n, dkv_m)
    (dg_mem_kv,) = _rms_bwd("norm_mem_kv_bwd", (dkv_m, w_kv, NT), mem, g_mem_kv, None, ())
    dh1, dh1_b, dg_mem_q = _rms_bwd("norm_mem_q_bwd", (dq_m, w_q, NT), h1, g_mem_q, dh2, ("f32", "bf16"))

    gw_out = _wgrad("gw_out", mixed, dh1_b)
    dua, dub, dgates = _mixer_bwd(dh1_b, w_out, ua, ub, proj, 6 * nb)
    do_a = _mm_w("up_a_bwd", dua, w_up_a, BF16, dims=NT)
    do_b = _mm_w("up_b_bwd", dub, w_up_b, BF16, dims=NT)
    gw_ua = _wgrad("gw_up_a", o_a, dua)
    gw_ub = _wgrad("gw_up_b", o_b, dub)
    (dq_ar, dk_ar, dv_a), (p_fg, p_fd) = _da_bwd(
        qk_a, proj, 2 * nb, bias, o_a, lse_a, do_a, batch, s,
        ride=([gw_fg.reshape(n_sh, -1, d), gw_fd.reshape(n_sh, -1, d)], False, (0, 0)))
    dqk_a = _rope_apply("rope_bwd", [dq_ar, dk_ar], width, cos_t, sin_a, sin_b, -1.0)
    mid = [gw_ua, gw_ub, gw_out.reshape(n_sh, -1, d), gw_q.reshape(n_sh, -1, gw_q.shape[1]),
           gw_kv.reshape(n_sh, -1, gw_kv.shape[1]), gw_o, gw_fu.reshape(n_sh, -1, d)]
    (dq_b, dk_b, dv_b), (*p_mid, p_fu) = _sb_bwd(proj, 3 * nb, 4 * nb, 5 * nb, tot_b, do_b, batch, s,
                                                 ride=(mid, False, (cs_up, cs_up, 0, 0, 0, cs_up, 0)))
    p_ffn = [p_fg, p_fu, p_fd]
    dproj = jnp.concatenate([dqk_a, dv_a, dq_b, dk_b, dv_b, dgates], axis=1)
    grad_x, dg_mix = _rms_bwd("proj_in_bwd", (dproj, w_in, NT), x, g_mix, dh1, ("f32",))
    p_in = _gw_in_scatter(n1, dproj)
    return loss_part, grad_x, [p_in] + list(p_mid) + p_ffn, (dg_mix, dg_mem_q, dg_mem_kv, dg_ffn, dg_final)


WEIGHTS =("w_in", "w_up_a", "w_up_b", "w_out", "w_q_mem", "w_kv_mem", "w_o_mem", "w_ffn_gate", "w_ffn_up", "w_ffn_down")
GAINS = ("g_mix", "g_mem_q", "g_mem_kv", "g_ffn", "g_final")
ORDER = ("g_mix", "w_in", "w_up_a", "w_up_b", "w_out", "g_mem_q", "g_mem_kv", "w_q_mem", "w_kv_mem", "w_o_mem", "g_ffn",
         "w_ffn_gate", "w_ffn_up", "w_ffn_down", "g_final")


def kernel(x, mem, positions, g_mix, w_in, w_up_a, w_up_b, w_out, g_mem_q, g_mem_kv, w_q_mem, w_kv_mem, w_o_mem, g_ffn, w_ffn_gate, w_ffn_up, w_ffn_down, g_final, loss_target, m_g_mix, m_w_in, m_w_up_a, m_w_up_b, m_w_out, m_g_mem_q, m_g_mem_kv, m_w_q_mem, m_w_kv_mem, m_w_o_mem, m_g_ffn, m_w_ffn_gate, m_w_ffn_up, m_w_ffn_down, m_g_final, v_g_mix, v_w_in, v_w_up_a, v_w_up_b, v_w_out, v_g_mem_q, v_g_mem_kv, v_w_q_mem, v_w_kv_mem, v_w_o_mem, v_g_ffn, v_w_ffn_gate, v_w_ffn_up, v_w_ffn_down, v_g_final):
    given = dict(locals())
    batch, s, d = x.shape
    t = batch * s
    flipped = ("w_ffn_gate", "w_ffn_up")

    def view(a, n):
        a = a.reshape(a.shape[-2:])
        return a.T if n in flipped else a

    def unview(a, n):
        return (a.T if n in flipped else a).reshape(given[n].shape)

    shard = {n: view(given[n], n) for n in WEIGHTS}
    gains = [given[n].reshape(1, d) for n in GAINS]

    pad = (-shard["w_ffn_down"].shape[0]) % LANES
    cast = _cast_weights([shard[n] for n in WEIGHTS], [pad if n in flipped + ("w_ffn_down",) else 0 for n in WEIGHTS])
    loss_part, grad_x, pieces, dgains = _local_step(
        x.reshape(t, d), mem.reshape(-1, d), positions.reshape(t, 1), loss_target.reshape(t, d), gains, cast[0],
        cast[1:], batch)

    grad, delta, new_m, new_v = {}, {}, {}, {}
    for n, p in zip(WEIGHTS, pieces):
        outs = _adam("adam_" + n, p, shard[n], view(given["m_" + n], n), view(given["v_" + n], n))
        grad[n], delta[n], new_m[n], new_v[n] = [unview(o, n) for o in outs]

    g_all = _allreduce_small(list(dgains), loss_part)
    small = _adam_small(g_all, gains, [given["m_" + n].reshape(1, d) for n in GAINS],
                        [given["v_" + n].reshape(1, d) for n in GAINS])
    for out, vals in zip((grad, delta, new_m, new_v), small):
        for n, val in zip(GAINS, vals):
            out[n] = val.reshape(given[n].shape)

    loss = g_all[SMALL_ROWS - 1, 0]
    return (loss, grad_x.reshape(x.shape), *[grad[n] for n in ORDER], *[delta[n] for n in ORDER],
            *[new_m[n] for n in ORDER], *[new_v[n] for n in ORDER])
```

```python
import functools
import math

import jax
import jax.numpy as jnp
import numpy as np
from jax import lax
from jax.experimental import pallas as pl
from jax.experimental.pallas import tpu as pltpu

F32 = jnp.float32
BF16 = jnp.bfloat16

N_DEV = 8
HEAD_DIM = 64
MEM_HEAD_DIM = 128
N_HEADS_MEM = 4
BLOCK = 128
DIL_PATTERNS = ((128, 1), (512, 4), (2048, 16))
ROPE_THETA = 500000.0
ROPE_HALF = 8
RMS_EPS = 1e-6
ADAM_LR, ADAM_B1, ADAM_B2, ADAM_EPS, ADAM_WD, ADAM_STEP = 0.001, 0.9, 0.999, 1e-08, 0.01, 10
NEG = -1e30
ROW_TILE = 512
LANES = 128

ANY = pl.BlockSpec(memory_space=pl.ANY)
VMEM = pl.BlockSpec(memory_space=pltpu.VMEM)
NN = (((1,), (0,)), ((), ()))
NT = (((1,), (1,)), ((), ()))
TN = (((0,), (0,)), ((), ()))


def _params(sem):
    return pltpu.CompilerParams(dimension_semantics=sem)


def _mm(name, a, b, *, grid, a_spec, b_spec, o_shape, o_spec, dims, out_dtype, nk=1):
    def body(*refs):
        a_ref, b_ref, o_ref = refs[0], refs[1], refs[2]
        p = lax.dot_general(a_ref[...], b_ref[...], dims, preferred_element_type=F32)
        if nk == 1:
            o_ref[...] = p.astype(out_dtype)
            return
        acc_ref = refs[-1]
        k = pl.program_id(len(grid) - 1)

        @pl.when(k == 0)
        def _():
            acc_ref[...] = p

        @pl.when(k > 0)
        def _():
            acc_ref[...] += p

        @pl.when(k == nk - 1)
        def _():
            o_ref[...] = acc_ref[...].astype(out_dtype)

    o_block = tuple(d for d in o_spec.block_shape if d is not None)
    sem = ("parallel",) * (len(grid) - 1) + (("arbitrary",) if nk > 1 else ("parallel",))
    return pl.pallas_call(
        body, name=name, grid=grid, in_specs=[a_spec, b_spec],
        out_specs=o_spec, out_shape=jax.ShapeDtypeStruct(o_shape, out_dtype),
        scratch_shapes=[pltpu.VMEM(o_block, F32)] if nk > 1 else [],
        compiler_params=_params(sem),
    )(a, b)


def _rms_fwd(name, x, g):
    t, d = x.shape
    tm = min(ROW_TILE, t)

    def body(x_ref, g_ref, o_ref):
        xf = x_ref[...]
        r = lax.rsqrt(jnp.mean(xf * xf, axis=-1, keepdims=True) + RMS_EPS)
        o_ref[...] = (xf * r * g_ref[...]).astype(BF16)

    return pl.pallas_call(
        body, name=name, grid=(t // tm,),
        in_specs=[pl.BlockSpec((tm, d), lambda i: (i, 0)), pl.BlockSpec((1, d), lambda i: (0, 0))],
        out_specs=pl.BlockSpec((tm, d), lambda i: (i, 0)), out_shape=jax.ShapeDtypeStruct((t, d), BF16),
        compiler_params=_params(("parallel",)),
    )(x, g)


def _rms_bwd_rows(dnf, xf, gv, res):
    r = lax.rsqrt(jnp.mean(xf * xf, axis=-1, keepdims=True) + RMS_EPS)
    xh = xf * r
    dxh = dnf * gv
    dx = r * (dxh - xh * jnp.mean(dxh * xh, axis=-1, keepdims=True))
    if res is not None:
        dx = dx + res
    return dx, jnp.sum(dnf * xh, axis=0, keepdims=True)


def _rms_bwd(name, dn, x, g, dres, want):
    t, d = x.shape
    tm = min(ROW_TILE, t)
    has_res = dres is not None
    lhs = list(dn) if isinstance(dn, tuple) else [dn]
    n_lhs = len(lhs[:2])

    def body(*refs):
        x_ref, g_ref = refs[n_lhs], refs[n_lhs + 1]
        r_ref = refs[n_lhs + 2] if has_res else None
        dx_refs, dg_ref = refs[-1 - len(want):-1], refs[-1]
        if n_lhs == 2:
            dnf = lax.dot_general(refs[0][...], refs[1][...], lhs[2], preferred_element_type=F32)
        else:
            dnf = refs[0][...].astype(F32)
        dx, dg = _rms_bwd_rows(dnf, x_ref[...], g_ref[...], r_ref[...] if has_res else None)
        for kind, dx_ref in zip(want, dx_refs):
            dx_ref[...] = dx.astype(F32 if kind == "f32" else BF16)

        @pl.when(pl.program_id(0) == 0)
        def _():
            dg_ref[...] = jnp.zeros_like(dg_ref)

        dg_ref[...] += dg

    row = pl.BlockSpec((tm, d), lambda i: (i, 0))
    vec = pl.BlockSpec((1, d), lambda i: (0, 0))
    if n_lhs == 2:
        first = [pl.BlockSpec((tm, lhs[0].shape[1]), lambda i: (i, 0)), pl.BlockSpec(lhs[1].shape, lambda i: (0, 0))]
    else:
        first = [row]
    return pl.pallas_call(
        body, name=name, grid=(t // tm,),
        in_specs=first + [row, vec] + ([row] if has_res else []),
        out_specs=[row] * len(want) + [vec],
        out_shape=[jax.ShapeDtypeStruct((t, d), F32 if kind == "f32" else BF16) for kind in want]
        + [jax.ShapeDtypeStruct((1, d), F32)],
        compiler_params=_params(("arbitrary",)),
    )(*(lhs[:2] + [x, g] + ([dres] if has_res else [])))


def _loss_head(a, w, res, tgt, g):
    t, d = res.shape
    k = a.shape[1]
    tm = min(ROW_TILE, t)

    def body(a_ref, w_ref, r_ref, t_ref, g_ref, loss_ref, dh_ref, dhb_ref, dg_ref):
        xf = lax.dot_general(a_ref[...], w_ref[...], NN, preferred_element_type=F32) + r_ref[...]
        gv = g_ref[...]
        r = lax.rsqrt(jnp.mean(xf * xf, axis=-1, keepdims=True) + RMS_EPS)
        xh = xf * r
        e = xh * gv - t_ref[...]
        dy = e * (1.0 / d)
        dxh = dy * gv
        dh = r * (dxh - xh * jnp.mean(dxh * xh, axis=-1, keepdims=True))
        dh_ref[...] = dh
        dhb_ref[...] = dh.astype(BF16)

        @pl.when(pl.program_id(0) == 0)
        def _():
            dg_ref[...] = jnp.zeros_like(dg_ref)
            loss_ref[...] = jnp.zeros_like(loss_ref)

        dg_ref[...] += jnp.sum(dy * xh, axis=0, keepdims=True)
        part = jnp.sum(jnp.sum(e * e, axis=1, keepdims=True), axis=0, keepdims=True) * (0.5 / d)
        loss_ref[...] += jnp.broadcast_to(part, loss_ref.shape)

    row = pl.BlockSpec((tm, d), lambda i: (i, 0))
    vec = pl.BlockSpec((1, d), lambda i: (0, 0))
    return pl.pallas_call(
        body, name="loss_head", grid=(t // tm,),
        in_specs=[pl.BlockSpec((tm, k), lambda i: (i, 0)), pl.BlockSpec((k, d), lambda i: (0, 0)), row, row, vec],
        out_specs=[pl.BlockSpec((8, LANES), lambda i: (0, 0)), row, row, vec],
        out_shape=[jax.ShapeDtypeStruct((8, LANES), F32), jax.ShapeDtypeStruct((t, d), F32),
                   jax.ShapeDtypeStruct((t, d), BF16), jax.ShapeDtypeStruct((1, d), F32)],
        compiler_params=_params(("arbitrary",)),
    )(a, w, res, tgt, g)


def _rope_tables(pos, inv_freq, sel_lo, sel_hi):
    t = pos.shape[0]
    tm = min(ROW_TILE, t)

    def body(p_ref, f_ref, lo_ref, hi_ref, c_ref, sa_ref, sb_ref):
        ang = p_ref[...].astype(F32) * f_ref[...]
        rot = lo_ref[...] + hi_ref[...]
        cs, sn = jnp.cos(ang), jnp.sin(ang)
        c_ref[...] = cs * rot + (1.0 - rot)
        sa_ref[...] = -sn * lo_ref[...]
        sb_ref[...] = sn * hi_ref[...]

    vec = pl.BlockSpec((1, LANES), lambda i: (0, 0))
    row = pl.BlockSpec((tm, LANES), lambda i: (i, 0))
    return pl.pallas_call(
        body, name="rope_tables", grid=(t // tm,),
        in_specs=[pl.BlockSpec((tm, 1), lambda i: (i, 0)), vec, vec, vec],
        out_specs=[row, row, row], out_shape=[jax.ShapeDtypeStruct((t, LANES), F32)] * 3,
        compiler_params=_params(("parallel",)),
    )(pos, inv_freq, sel_lo, sel_hi)


def _rope_apply(name, srcs, width, cos_t, sin_a, sin_b, sign):
    t = srcs[0].shape[0]
    tm = min(ROW_TILE, t)
    n_cols = width // LANES

    def body(*refs):
        x_refs, (c_ref, sa_ref, sb_ref, o_ref) = refs[:len(srcs)], refs[len(srcs):]
        cs, sa, sb = c_ref[...], sign * sa_ref[...], sign * sb_ref[...]
        for a, x_ref in enumerate(x_refs):
            for c in range(n_cols):
                xf = x_ref[:, c * LANES:(c + 1) * LANES].astype(F32)
                up = pltpu.roll(xf, LANES - ROPE_HALF, 1)
                dn = pltpu.roll(xf, ROPE_HALF, 1)
                o_ref[:, a * width + c * LANES:a * width + (c + 1) * LANES] = (xf * cs + up * sa + dn * sb).astype(BF16)

    wide = len(srcs) * width
    tab = pl.BlockSpec((tm, LANES), lambda i: (i, 0))
    return pl.pallas_call(
        body, name=name, grid=(t // tm,),
        in_specs=[pl.BlockSpec((tm, width), lambda i: (i, 0))] * len(srcs) + [tab, tab, tab],
        out_specs=pl.BlockSpec((tm, wide), lambda i: (i, 0)),
        out_shape=jax.ShapeDtypeStruct((t, wide), BF16),
        compiler_params=_params(("parallel",)),
    )(*srcs, cos_t, sin_a, sin_b)


DA_T = 256
MIX_STREAMS = 4
SB_BWD_STREAMS = 2


def _lane_lo():
    return lax.broadcasted_iota(jnp.int32, (BLOCK, LANES), 1) < HEAD_DIM


def _dilated_bias_tiles(s):
    n = s // DA_T
    dist = (np.arange(n)[:, None, None] * DA_T + np.arange(DA_T)[None, :, None] - np.arange(DA_T)[None, None, :])
    cnt = np.zeros(dist.shape, np.float32)
    for window, dil in DIL_PATTERNS:
        cnt += ((dist >= 0) & (dist % dil == 0) & (dist <= window)).astype(np.float32)
    return jnp.asarray(np.where(cnt > 0, np.log(np.maximum(cnt, 1.0)), NEG).astype(np.float32))


def _stack_heads(x, lo):
    zero = jnp.zeros_like(x)
    return jnp.concatenate([jnp.where(lo, x, zero), jnp.where(lo, zero, x)], axis=0)


def _da_fwd(qk, proj, v_col0, bias, batch, s, ride=None, streams=MIX_STREAMS):
    t = qk.shape[0]
    nq = s // DA_T
    n_pairs = 4
    ns = streams
    wide = ns * LANES
    scale = HEAD_DIM ** -0.5

    def body(q_ref, k_ref, v_ref, b_ref, o_ref, lse_ref, acc_ref, m_ref, l_ref):
        i = pl.program_id(2)
        lo = lax.broadcasted_iota(jnp.int32, (DA_T, LANES), 1) < HEAD_DIM
        ones = jnp.ones((DA_T, LANES), BF16)
        acc_ref[...] = jnp.zeros_like(acc_ref)
        m_ref[...] = jnp.full(m_ref.shape, NEG, F32)
        l_ref[...] = jnp.zeros_like(l_ref)
        qqs = [_stack_heads(q_ref[:, st * LANES:(st + 1) * LANES] * scale, lo) for st in range(ns)]

        def scores(st, rows, bias2):
            k = k_ref[rows, st * LANES:(st + 1) * LANES]
            return lax.dot_general(qqs[st], k, NT, preferred_element_type=F32) + bias2

        def softmax(st, sc):
            m_old = m_ref[st]
            m_new = jnp.maximum(m_old, jnp.max(sc, axis=1, keepdims=True))
            m_ref[st] = m_new
            return jnp.exp(sc - m_new).astype(BF16), jnp.exp(m_old - m_new)

        def values(st, rows, p, alpha):
            v = v_ref[rows, st * LANES:(st + 1) * LANES]
            vz = jnp.zeros_like(v)
            l_ref[st] = alpha * l_ref[st] + lax.dot_general(p, ones, NN, preferred_element_type=F32)
            pv = (lax.dot_general(p[:DA_T], jnp.where(lo, v, vz), NN, preferred_element_type=F32)
                  + lax.dot_general(p[DA_T:], jnp.where(lo, vz, v), NN, preferred_element_type=F32))
            acc_ref[st] = acc_ref[st] * jnp.where(lo, alpha[:DA_T], alpha[DA_T:]) + pv

        def trip(dlt, carry):
            rows = pl.ds(pl.multiple_of((i - dlt) * DA_T, DA_T), DA_T)
            bias_t = b_ref[dlt]
            bias2 = jnp.concatenate([bias_t, bias_t], axis=0)
            scs = [scores(st, rows, bias2) for st in range(ns)]
            pas = [softmax(st, scs[st]) for st in range(ns)]
            for st in range(ns):
                values(st, rows, *pas[st])
            return carry

        lax.fori_loop(0, i + 1, trip, 0)
        for st in range(ns):
            cols = slice(st * LANES, (st + 1) * LANES)
            l_t = l_ref[st]
            o_ref[:, cols] = (acc_ref[st] / jnp.where(lo, l_t[:DA_T], l_t[DA_T:])).astype(BF16)
            lse = m_ref[st] + jnp.log(l_t)
            lse_ref[:, cols] = jnp.where(lo, lse[:DA_T], lse[DA_T:])

    blk = pl.BlockSpec((DA_T, wide), lambda b, h, i: (b * nq + i, h))
    return _call(
        body, name="attn_a_fwd", grid=(batch, n_pairs // ns, nq),
        in_specs=[blk,
                  pl.BlockSpec((s, wide), lambda b, h, i: (b, n_pairs // ns + h)),
                  pl.BlockSpec((s, wide), lambda b, h, i: (b, v_col0 // ns + h)),
                  pl.BlockSpec((nq, DA_T, DA_T), lambda b, h, i: (0, 0, 0))],
        out_specs=[blk, blk],
        out_shape=[jax.ShapeDtypeStruct((t, n_pairs * LANES), BF16), jax.ShapeDtypeStruct((t, n_pairs * LANES), F32)],
        scratch=[pltpu.VMEM((ns, DA_T, LANES), F32), pltpu.VMEM((ns, 2 * DA_T, 1), F32),
                 pltpu.VMEM((ns, 2 * DA_T, LANES), F32)],
        sem=("parallel", "parallel", "arbitrary"), args=(qk, qk, proj, bias), ride=ride)


def _da_bwd(qk, proj, v_col0, bias, o, lse, do, batch, s, ride=None, streams=MIX_STREAMS):
    t = qk.shape[0]
    nq = s // DA_T
    n_pairs = 4
    ns = streams
    wide = ns * LANES
    scale = HEAD_DIM ** -0.5

    def body(q_ref, k_ref, v_ref, b_ref, o_ref, lse_ref, do_ref, dq_ref, dk_ref, dv_ref, dk_acc, dv_acc, dq_acc):
        i = pl.program_id(2)
        lo = lax.broadcasted_iota(jnp.int32, (DA_T, LANES), 1) < HEAD_DIM

        @pl.when(i == 0)
        def _():
            dk_acc[...] = jnp.zeros_like(dk_acc)
            dv_acc[...] = jnp.zeros_like(dv_acc)

        dq_acc[...] = jnp.zeros_like(dq_acc)
        qqs, dds, deltas, lses = [], [], [], []
        for st in range(ns):
            cols = slice(st * LANES, (st + 1) * LANES)
            do_ = do_ref[:, cols]
            qqs.append(_stack_heads(q_ref[:, cols] * scale, lo))
            dds.append(_stack_heads(do_, lo))
            prod = do_.astype(F32) * o_ref[:, cols].astype(F32)
            fz = jnp.zeros_like(prod)
            deltas.append(jnp.concatenate([jnp.sum(jnp.where(lo, prod, fz), axis=1, keepdims=True),
                                           jnp.sum(jnp.where(lo, fz, prod), axis=1, keepdims=True)], axis=0))
            lse_t = lse_ref[:, cols]
            lses.append(jnp.concatenate([lse_t[:, 0:1], lse_t[:, HEAD_DIM:HEAD_DIM + 1]], axis=0))

        def products(st, rows, bias2):
            cols = slice(st * LANES, (st + 1) * LANES)
            sc = lax.dot_general(qqs[st], k_ref[rows, cols], NT, preferred_element_type=F32) + bias2
            return sc, lax.dot_general(dds[st], v_ref[rows, cols], NT, preferred_element_type=F32)

        def weights(st, sc, dp):
            p = jnp.exp(sc - lses[st])
            return (p * (dp - deltas[st])).astype(BF16), p.astype(BF16)

        def gradients(st, rows, ds, p):
            cols = slice(st * LANES, (st + 1) * LANES)
            k = k_ref[rows, cols]
            kz = jnp.zeros_like(k)
            dq_acc[st] += (lax.dot_general(ds[:DA_T], jnp.where(lo, k, kz), NN, preferred_element_type=F32)
                           + lax.dot_general(ds[DA_T:], jnp.where(lo, kz, k), NN, preferred_element_type=F32))
            dk_acc[rows, cols] += lax.dot_general(ds, qqs[st], TN, preferred_element_type=F32)
            dv_acc[rows, cols] += lax.dot_general(p, dds[st], TN, preferred_element_type=F32)

        def trip(dlt, carry):
            rows = pl.ds(pl.multiple_of((i - dlt) * DA_T, DA_T), DA_T)
            bias_t = b_ref[dlt]
            bias2 = jnp.concatenate([bias_t, bias_t], axis=0)
            prods = [products(st, rows, bias2) for st in range(ns)]
            wts = [weights(st, *prods[st]) for st in range(ns)]
            for st in range(ns):
                gradients(st, rows, *wts[st])
            return carry

        lax.fori_loop(0, i + 1, trip, 0)
        for st in range(ns):
            dq_ref[:, st * LANES:(st + 1) * LANES] = (dq_acc[st] * scale).astype(BF16)

        @pl.when(i == nq - 1)
        def _():
            dk_ref[...] = dk_acc[...].astype(BF16)
            dv_ref[...] = dv_acc[...].astype(BF16)

    blk = pl.BlockSpec((DA_T, wide), lambda b, h, i: (b * nq + i, h))
    seq = pl.BlockSpec((s, wide), lambda b, h, i: (b, h), pipeline_mode=pl.Buffered(1))
    one = pl.Buffered(1)
    out = jax.ShapeDtypeStruct((t, n_pairs * LANES), BF16)
    return _call(
        body, name="attn_a_bwd", grid=(batch, n_pairs // ns, nq),
        in_specs=[blk,
                  pl.BlockSpec((s, wide), lambda b, h, i: (b, n_pairs // ns + h), pipeline_mode=one),
                  pl.BlockSpec((s, wide), lambda b, h, i: (b, v_col0 // ns + h), pipeline_mode=one),
                  pl.BlockSpec((nq, DA_T, DA_T), lambda b, h, i: (0, 0, 0), pipeline_mode=one),
                  blk, blk, blk],
        out_specs=[blk, seq, seq], out_shape=[out, out, out],
        scratch=[pltpu.VMEM((s, wide), F32), pltpu.VMEM((s, wide), F32), pltpu.VMEM((ns, DA_T, LANES), F32)],
        sem=("parallel", "parallel", "arbitrary"), args=(qk, qk, proj, bias, o, lse, do), ride=ride)


SB_Q = 256


def _sb_consts(after):
    r = lax.broadcasted_iota(jnp.int32, (2 * BLOCK, 2 * BLOCK), 0) % BLOCK
    c = lax.broadcasted_iota(jnp.int32, (2 * BLOCK, 2 * BLOCK), 1)
    tri = (r > c) if after else (r < c)
    return jnp.logical_or(c >= BLOCK, tri).astype(BF16)


def _split(x):
    hi = x.astype(BF16)
    lo = (x - hi.astype(F32)).astype(BF16)
    return jnp.concatenate([hi, lo], axis=1)


def _sb_fwd(proj, q_col0, k_col0, v_col0, batch, s, ride=None, streams=MIX_STREAMS):
    t = proj.shape[0]
    nq = s // SB_Q
    n_pairs = 4
    ns = streams
    wide = ns * LANES
    scale = HEAD_DIM ** -0.5

    def body(q_ref, k_ref, v_ref, o_ref, tot_ref, acc_ref, run_ref):
        i = pl.program_id(2)
        lo_q = lax.broadcasted_iota(jnp.int32, (SB_Q, LANES), 1) < HEAD_DIM
        lo_k = _lane_lo()
        mat = _sb_consts(True)
        row = lax.broadcasted_iota(jnp.int32, (2 * SB_Q, LANES), 0) % SB_Q
        ahead = row - lax.broadcasted_iota(jnp.int32, (2 * SB_Q, LANES), 1)
        acc_ref[...] = jnp.zeros_like(acc_ref)
        run_ref[...] = jnp.zeros_like(run_ref)
        qqs = [_stack_heads(q_ref[:, st * LANES:(st + 1) * LANES] * scale, lo_q) for st in range(ns)]

        def units(todo):
            def rows(j):
                return pl.ds(pl.multiple_of(j * BLOCK, BLOCK), BLOCK)

            zs = [lax.dot_general(qqs[st], k_ref[rows(j), st * LANES:(st + 1) * LANES], NT, preferred_element_type=F32)
                  for st, j, _ in todo]
            logs = []
            for z, (_, _, off) in zip(zs, todo):
                lsig = jnp.minimum(z, 0.0) - jnp.log(1.0 + jnp.exp(-jnp.abs(z)))
                lneg = lsig - z
                if off is not None:
                    lneg = jnp.where(ahead > off, lneg, 0.0)
                logs.append((lsig, _split(lneg)))
            sums = [lax.dot_general(cat, mat, NN, preferred_element_type=F32) for _, cat in logs]
            probs = []
            for (lsig, _), sm, (st, _, off) in zip(logs, sums, todo):
                run = run_ref[st]
                a = jnp.exp(lsig + run + sm[:, :BLOCK])
                if off is not None:
                    a = jnp.where(ahead > off, a, 0.0)
                run_ref[st] = run + sm[:, BLOCK:]
                probs.append(a.astype(BF16))
            for ab, (st, j, _) in zip(probs, todo):
                v = v_ref[rows(j), st * LANES:(st + 1) * LANES]
                vz = jnp.zeros_like(v)
                acc_ref[st] += (lax.dot_general(ab[:SB_Q], jnp.where(lo_k, v, vz), NN, preferred_element_type=F32)
                                + lax.dot_general(ab[SB_Q:], jnp.where(lo_k, vz, v), NN, preferred_element_type=F32))

        units([(st, 2 * i + 1, BLOCK) for st in range(ns)] + [(st, 2 * i, 0) for st in range(ns)])

        def pair(p, carry):
            jp = i - 1 - p
            units([(st, 2 * jp + 1, None) for st in range(ns)] + [(st, 2 * jp, None) for st in range(ns)])
            return carry

        lax.fori_loop(0, i, pair, 0)
        for st in range(ns):
            cols = slice(st * LANES, (st + 1) * LANES)
            o_ref[:, cols] = acc_ref[st].astype(BF16)
            tot_ref[:, cols] = jnp.where(lo_q, run_ref[st, 0:SB_Q, :], run_ref[st, SB_Q:2 * SB_Q, :])

    def seq(col0):
        return pl.BlockSpec((s, wide), lambda b, h, i: (b, col0 // ns + h))

    blk = pl.BlockSpec((SB_Q, wide), lambda b, h, i: (b * nq + i, h))
    return _call(
        body, name="attn_b_fwd", grid=(batch, n_pairs // ns, nq),
        in_specs=[pl.BlockSpec((SB_Q, wide), lambda b, h, i: (b * nq + i, q_col0 // ns + h)), seq(k_col0), seq(v_col0)],
        out_specs=[blk, blk],
        out_shape=[jax.ShapeDtypeStruct((t, n_pairs * LANES), BF16), jax.ShapeDtypeStruct((t, n_pairs * LANES), F32)],
        scratch=[pltpu.VMEM((ns, SB_Q, LANES), F32), pltpu.VMEM((ns, 2 * SB_Q, LANES), F32)],
        sem=("parallel", "parallel", "arbitrary"), args=(proj, proj, proj), ride=ride)


def _sb_bwd(proj, q_col0, k_col0, v_col0, tot, do, batch, s, ride=None, streams=SB_BWD_STREAMS):
    t = proj.shape[0]
    nq = s // SB_Q
    n_pairs = 4
    ns = streams
    wide = ns * LANES
    scale = HEAD_DIM ** -0.5

    def body(q_ref, k_ref, v_ref, tot_ref, do_ref, dq_ref, dk_ref, dv_ref, dk_acc, dv_acc, dq_acc, seen_ref, gsum_ref):
        i = pl.program_id(2)
        lo_q = lax.broadcasted_iota(jnp.int32, (SB_Q, LANES), 1) < HEAD_DIM
        lo_k = _lane_lo()

        @pl.when(i == 0)
        def _():
            dk_acc[...] = jnp.zeros_like(dk_acc)
            dv_acc[...] = jnp.zeros_like(dv_acc)

        mat_after = _sb_consts(True)
        mat_before = _sb_consts(False)
        row = lax.broadcasted_iota(jnp.int32, (2 * SB_Q, LANES), 0) % SB_Q
        ahead = row - lax.broadcasted_iota(jnp.int32, (2 * SB_Q, LANES), 1)
        dq_acc[...] = jnp.zeros_like(dq_acc)
        seen_ref[...] = jnp.zeros_like(seen_ref)
        gsum_ref[...] = jnp.zeros_like(gsum_ref)
        qqs, dds, totals = [], [], []
        for st in range(ns):
            cols = slice(st * LANES, (st + 1) * LANES)
            qqs.append(_stack_heads(q_ref[:, cols] * scale, lo_q))
            dds.append(_stack_heads(do_ref[:, cols], lo_q))
            tot_t = tot_ref[:, cols]
            totals.append(jnp.concatenate([jnp.broadcast_to(tot_t[:, 0:1], (SB_Q, LANES)),
                                           jnp.broadcast_to(tot_t[:, HEAD_DIM:HEAD_DIM + 1], (SB_Q, LANES))], axis=0))

        def units(todo):
            def rows(j):
                return pl.ds(pl.multiple_of(j * BLOCK, BLOCK), BLOCK)

            def cols(st):
                return slice(st * LANES, (st + 1) * LANES)

            prods = [(lax.dot_general(qqs[st], k_ref[rows(j), cols(st)], NT, preferred_element_type=F32),
                      lax.dot_general(dds[st], v_ref[rows(j), cols(st)], NT, preferred_element_type=F32))
                     for st, j, _ in todo]
            logs = []
            for (z, _), (_, _, off) in zip(prods, todo):
                lsig = jnp.minimum(z, 0.0) - jnp.log(1.0 + jnp.exp(-jnp.abs(z)))
                lneg = lsig - z
                if off is not None:
                    lneg = jnp.where(ahead > off, lneg, 0.0)
                logs.append((lsig, _split(lneg)))
            sums = [lax.dot_general(cat, mat_after, NN, preferred_element_type=F32) for _, cat in logs]
            gates = []
            for (lsig, _), sm, (_, da), (st, _, off) in zip(logs, sums, prods, todo):
                seen = seen_ref[st]
                a = jnp.exp(lsig + (totals[st] - seen - sm[:, BLOCK:]) + sm[:, :BLOCK])
                if off is not None:
                    a = jnp.where(ahead > off, a, 0.0)
                seen_ref[st] = seen + sm[:, BLOCK:]
                g = a * da
                gates.append((a.astype(BF16), g, _split(g)))
            gsums = [lax.dot_general(cat, mat_before, NN, preferred_element_type=F32) for _, _, cat in gates]
            outs = []
            for (lsig, _), (ab, g, _), gs, (st, _, off) in zip(logs, gates, gsums, todo):
                gsum = gsum_ref[st]
                dz = g - jnp.exp(lsig) * (g + gsum + gs[:, :BLOCK])
                if off is not None:
                    dz = jnp.where(ahead > off, dz, 0.0)
                gsum_ref[st] = gsum + gs[:, BLOCK:]
                outs.append((dz.astype(BF16), ab))
            for (dzb, ab), (st, j, _) in zip(outs, todo):
                k = k_ref[rows(j), cols(st)]
                kz = jnp.zeros_like(k)
                dq_acc[st] += (lax.dot_general(dzb[:SB_Q], jnp.where(lo_k, k, kz), NN, preferred_element_type=F32)
                               + lax.dot_general(dzb[SB_Q:], jnp.where(lo_k, kz, k), NN, preferred_element_type=F32))
                dk_acc[rows(j), cols(st)] += lax.dot_general(dzb, qqs[st], TN, preferred_element_type=F32)
                dv_acc[rows(j), cols(st)] += lax.dot_general(ab, dds[st], TN, preferred_element_type=F32)

        def pair(p, carry):
            units([(st, 2 * p, None) for st in range(ns)] + [(st, 2 * p + 1, None) for st in range(ns)])
            return carry

        lax.fori_loop(0, i, pair, 0)
        units([(st, 2 * i, 0) for st in range(ns)] + [(st, 2 * i + 1, BLOCK) for st in range(ns)])
        for st in range(ns):
            dq_ref[:, st * LANES:(st + 1) * LANES] = (dq_acc[st] * scale).astype(BF16)

        @pl.when(i == nq - 1)
        def _():
            dk_ref[...] = dk_acc[...].astype(BF16)
            dv_ref[...] = dv_acc[...].astype(BF16)

    def seq_in(col0):
        return pl.BlockSpec((s, wide), lambda b, h, i: (b, col0 // ns + h))

    blk = pl.BlockSpec((SB_Q, wide), lambda b, h, i: (b * nq + i, h))
    seq = pl.BlockSpec((s, wide), lambda b, h, i: (b, h))
    out = jax.ShapeDtypeStruct((t, n_pairs * LANES), BF16)
    return _call(
        body, name="attn_b_bwd", grid=(batch, n_pairs // ns, nq),
        in_specs=[pl.BlockSpec((SB_Q, wide), lambda b, h, i: (b * nq + i, q_col0 // ns + h)), seq_in(k_col0),
                  seq_in(v_col0), blk, blk],
        out_specs=[blk, seq, seq], out_shape=[out, out, out],
        scratch=[pltpu.VMEM((s, wide), F32), pltpu.VMEM((s, wide), F32), pltpu.VMEM((ns, SB_Q, LANES), F32),
                 pltpu.VMEM((ns, 2 * SB_Q, LANES), F32), pltpu.VMEM((ns, 2 * SB_Q, LANES), F32)],
        sem=("parallel", "parallel", "arbitrary"), args=(proj, proj, proj, tot, do), ride=ride)


MEM_Q_TILE = 512


def _mem_fwd(q, kv, batch, s, n_mem):
    t, width = q.shape
    tq = min(MEM_Q_TILE, s)
    nq = s // tq
    scale = MEM_HEAD_DIM ** -0.5

    def body(q_ref, kv_ref, o_ref):
        for h in range(N_HEADS_MEM):
            cols = slice(h * MEM_HEAD_DIM, (h + 1) * MEM_HEAD_DIM)
            k = kv_ref[:, cols]
            v = kv_ref[:, width + h * MEM_HEAD_DIM: width + (h + 1) * MEM_HEAD_DIM]
            sc = lax.dot_general(q_ref[:, cols], k, NT, preferred_element_type=F32) * scale
            p = jnp.exp(sc - jnp.max(sc, axis=1, keepdims=True))
            p = p / jnp.sum(p, axis=1, keepdims=True)
            o_ref[:, cols] = lax.dot_general(p.astype(BF16), v, NN, preferred_element_type=F32).astype(BF16)

    return pl.pallas_call(
        body, name="mem_attn_fwd", grid=(batch, nq),
        in_specs=[pl.BlockSpec((tq, width), lambda b, i: (b * nq + i, 0)),
                  pl.BlockSpec((n_mem, 2 * width), lambda b, i: (b, 0))],
        out_specs=pl.BlockSpec((tq, width), lambda b, i: (b * nq + i, 0)),
        out_shape=jax.ShapeDtypeStruct((t, width), BF16),
        compiler_params=_params(("parallel", "parallel")),
    )(q, kv)


def _mem_bwd(q, kv, do, batch, s, n_mem):
    t, width = q.shape
    tq = min(MEM_Q_TILE, s)
    nq = s // tq
    scale = MEM_HEAD_DIM ** -0.5

    def body(q_ref, kv_ref, do_ref, dq_ref, dkv_ref, acc):
        i = pl.program_id(1)

        @pl.when(i == 0)
        def _():
            acc[...] = jnp.zeros_like(acc)

        for h in range(N_HEADS_MEM):
            cols = slice(h * MEM_HEAD_DIM, (h + 1) * MEM_HEAD_DIM)
            vcols = slice(width + h * MEM_HEAD_DIM, width + (h + 1) * MEM_HEAD_DIM)
            qh, k, v, doh = q_ref[:, cols], kv_ref[:, cols], kv_ref[:, vcols], do_ref[:, cols]
            sc = lax.dot_general(qh, k, NT, preferred_element_type=F32) * scale
            p = jnp.exp(sc - jnp.max(sc, axis=1, keepdims=True))
            p = p / jnp.sum(p, axis=1, keepdims=True)
            dp = lax.dot_general(doh, v, NT, preferred_element_type=F32)
            ds = (p * (dp - jnp.sum(p * dp, axis=1, keepdims=True)) * scale).astype(BF16)
            dq_ref[:, cols] = lax.dot_general(ds, k, NN, preferred_element_type=F32).astype(BF16)
            acc[:, cols] += lax.dot_general(ds, qh, TN, preferred_element_type=F32)
            acc[:, vcols] += lax.dot_general(p.astype(BF16), doh, TN, preferred_element_type=F32)

        @pl.when(i == nq - 1)
        def _():
            dkv_ref[...] = acc[...].astype(BF16)

    row = pl.BlockSpec((tq, width), lambda b, i: (b * nq + i, 0))
    kvs = pl.BlockSpec((n_mem, 2 * width), lambda b, i: (b, 0))
    return pl.pallas_call(
        body, name="mem_attn_bwd", grid=(batch, nq),
        in_specs=[row, kvs, row], out_specs=[row, kvs],
        out_shape=[jax.ShapeDtypeStruct((t, width), BF16), jax.ShapeDtypeStruct((batch * n_mem, 2 * width), BF16)],
        scratch_shapes=[pltpu.VMEM((n_mem, 2 * width), F32)],
        compiler_params=_params(("parallel", "arbitrary")),
    )(q, kv, do)


def _mixer_fwd(o_a, o_b, w_a, w_b, proj, gate_col0, w_out, x, g):
    t, width = o_a.shape
    d = w_a.shape[1]
    tm = min(ROW_TILE, t)
    gb0 = gate_col0 * LANES // d

    def body(oa_ref, ob_ref, wa_ref, wb_ref, ga_ref, gb_ref, wo_ref, x_ref, g_ref, ua_ref, ub_ref, mix_ref, n_ref,
             h_ref):
        ua = lax.dot_general(oa_ref[...], wa_ref[...], NN, preferred_element_type=F32)
        ub = lax.dot_general(ob_ref[...], wb_ref[...], NN, preferred_element_type=F32)
        ua_ref[...] = ua.astype(BF16)
        ub_ref[...] = ub.astype(BF16)
        mixed = (jax.nn.sigmoid(ga_ref[...].astype(F32)) * ua + jax.nn.sigmoid(gb_ref[...].astype(F32)) * ub).astype(BF16)
        mix_ref[...] = mixed
        h = lax.dot_general(mixed, wo_ref[...], NN, preferred_element_type=F32) + x_ref[...]
        h_ref[...] = h
        r = lax.rsqrt(jnp.mean(h * h, axis=-1, keepdims=True) + RMS_EPS)
        n_ref[...] = (h * r * g_ref[...]).astype(BF16)

    row = pl.BlockSpec((tm, width), lambda i: (i, 0))
    wsp = pl.BlockSpec((width, d), lambda i: (0, 0))
    out = pl.BlockSpec((tm, d), lambda i: (i, 0))
    osh = jax.ShapeDtypeStruct((t, d), BF16)
    return pl.pallas_call(
        body, name="mixer_fwd", grid=(t // tm,),
        in_specs=[row, row, wsp, wsp,
                  pl.BlockSpec((tm, d), lambda i: (i, gb0)), pl.BlockSpec((tm, d), lambda i: (i, gb0 + 1)),
                  pl.BlockSpec((d, d), lambda i: (0, 0)), out, pl.BlockSpec((1, d), lambda i: (0, 0))],
        out_specs=[out, out, out, out, out], out_shape=[osh, osh, osh, osh, jax.ShapeDtypeStruct((t, d), F32)],
        compiler_params=_params(("parallel",)),
    )(o_a, o_b, w_a, w_b, proj, proj, w_out, x, g)


def _mixer_bwd(dh, w_out, ua, ub, proj, gate_col0):
    t, d = dh.shape
    tm = min(ROW_TILE, t)
    nc = d // LANES

    def body(dh_ref, w_ref, ua_ref, ub_ref, ga_ref, gb_ref, dua_ref, dub_ref, dg_ref):
        dm = lax.dot_general(dh_ref[...], w_ref[...], NT, preferred_element_type=F32)
        sa = jax.nn.sigmoid(ga_ref[...].astype(F32))
        sb = jax.nn.sigmoid(gb_ref[...].astype(F32))
        dua_ref[...] = (dm * sa).astype(BF16)
        dub_ref[...] = (dm * sb).astype(BF16)
        dg_ref[:, 0:d] = (dm * ua_ref[...].astype(F32) * sa * (1.0 - sa)).astype(BF16)
        dg_ref[:, d:2 * d] = (dm * ub_ref[...].astype(F32) * sb * (1.0 - sb)).astype(BF16)

    row = pl.BlockSpec((tm, d), lambda i: (i, 0))
    return pl.pallas_call(
        body, name="mixer_bwd", grid=(t // tm,),
        in_specs=[row, pl.BlockSpec((d, d), lambda i: (0, 0)), row, row,
                  pl.BlockSpec((tm, d), lambda i: (i, gate_col0 // nc)),
                  pl.BlockSpec((tm, d), lambda i: (i, gate_col0 // nc + 1))],
        out_specs=[row, row, pl.BlockSpec((tm, 2 * d), lambda i: (i, 0))],
        out_shape=[jax.ShapeDtypeStruct((t, d), BF16), jax.ShapeDtypeStruct((t, d), BF16),
                   jax.ShapeDtypeStruct((t, 2 * d), BF16)],
        compiler_params=_params(("parallel",)),
    )(dh, w_out, ua, ub, proj, proj)


FFN_COLS = 1024


def _ffn_up(n, w_gate, w_up):
    t, d = n.shape
    hidden = w_gate.shape[0]
    tm = min(ROW_TILE, t)
    tn = min(FFN_COLS, hidden)

    def body(n_ref, wg_ref, wu_ref, hg_ref, hu_ref, act_ref):
        hg = lax.dot_general(n_ref[...], wg_ref[...], NT, preferred_element_type=F32)
        hu = lax.dot_general(n_ref[...], wu_ref[...], NT, preferred_element_type=F32)
        hg_ref[...] = hg.astype(BF16)
        hu_ref[...] = hu.astype(BF16)
        act_ref[...] = (hg * jax.nn.sigmoid(hg) * hu).astype(BF16)

    wsp = pl.BlockSpec((tn, d), lambda j, i: (j, 0))
    out = pl.BlockSpec((tm, tn), lambda j, i: (i, j))
    osh = jax.ShapeDtypeStruct((t, hidden), BF16)
    return pl.pallas_call(
        body, name="ffn_up", grid=(hidden // tn, t // tm),
        in_specs=[pl.BlockSpec((tm, d), lambda j, i: (i, 0)), wsp, wsp],
        out_specs=[out, out, out], out_shape=[osh, osh, osh],
        compiler_params=_params(("parallel", "parallel")),
    )(n, w_gate, w_up)


def _ffn_bwd(dh, w_down, w_gate, w_up, hg, hu, x, g, dres):
    t, d = dh.shape
    hidden = w_down.shape[0]
    tm = min(ROW_TILE, t)
    tn = min(FFN_COLS, hidden)
    nj = hidden // tn

    def body(dh_ref, wd_ref, wg_ref, wu_ref, hg_ref, hu_ref, x_ref, g_ref, r_ref, dhg_ref, dhu_ref, dx_ref, dxb_ref,
             dg_ref, acc):
        j, i = pl.program_id(0), pl.program_id(1)
        dact = lax.dot_general(dh_ref[...], wd_ref[...], NT, preferred_element_type=F32)
        hg = hg_ref[...].astype(F32)
        sg = jax.nn.sigmoid(hg)
        dhu = (dact * hg * sg).astype(BF16)
        dhg = (dact * hu_ref[...].astype(F32) * sg * (1.0 + hg * (1.0 - sg))).astype(BF16)
        dhu_ref[...] = dhu
        dhg_ref[...] = dhg
        part = (lax.dot_general(dhg, wg_ref[...], NN, preferred_element_type=F32)
                + lax.dot_general(dhu, wu_ref[...], NN, preferred_element_type=F32))

        @pl.when(j == 0)
        def _():
            acc[i] = part

        @pl.when(j > 0)
        def _():
            acc[i] += part

        @pl.when(jnp.logical_and(j == 0, i == 0))
        def _():
            dg_ref[...] = jnp.zeros_like(dg_ref)

        @pl.when(j == nj - 1)
        def _():
            dx, dg = _rms_bwd_rows(acc[i], x_ref[...], g_ref[...], r_ref[...])
            dx_ref[...] = dx
            dxb_ref[...] = dx.astype(BF16)
            dg_ref[...] += dg

    hid = pl.BlockSpec((tm, tn), lambda j, i: (i, j))
    wsp = pl.BlockSpec((tn, d), lambda j, i: (j, 0), pipeline_mode=pl.Buffered(1))
    late = pl.BlockSpec((tm, d), lambda j, i: (jnp.where(j == nj - 1, i, 0), 0))
    vec = pl.BlockSpec((1, d), lambda j, i: (0, 0))
    osh = jax.ShapeDtypeStruct((t, hidden), BF16)
    return pl.pallas_call(
        body, name="ffn_bwd", grid=(nj, t // tm),
        in_specs=[pl.BlockSpec((tm, d), lambda j, i: (i, 0)), wsp, wsp, wsp, hid, hid, late, vec, late],
        out_specs=[hid, hid, late, late, vec],
        out_shape=[osh, osh, jax.ShapeDtypeStruct((t, d), F32), jax.ShapeDtypeStruct((t, d), BF16),
                   jax.ShapeDtypeStruct((1, d), F32)],
        scratch_shapes=[pltpu.VMEM((t // tm, tm, d), F32)],
        compiler_params=_params(("arbitrary", "arbitrary")),
    )(dh, w_down, w_gate, w_up, hg, hu, x, g, dres)


MM_ROWS = 1024


def _mm_w(name, a, w, out_dtype, dims=NN):
    t, k = a.shape
    n = w.shape[1] if dims == NN else w.shape[0]
    tm, tn = min(MM_ROWS, t), min(1024, n)
    o_spec = pl.BlockSpec((tm, tn), lambda j, i: (i, j))
    b_spec = pl.BlockSpec((k, tn), lambda j, i: (0, j)) if dims == NN else pl.BlockSpec((tn, k), lambda j, i: (j, 0))
    return _mm(name, a, w, grid=(n // tn, t // tm), a_spec=pl.BlockSpec((tm, k), lambda j, i: (i, 0)), b_spec=b_spec,
               o_shape=(t, n), o_spec=o_spec, dims=dims, out_dtype=out_dtype)


def _mm_res_norm(name, a, w, res, g):
    t, k = a.shape
    d = w.shape[1]
    tm = min(ROW_TILE, t)

    def body(a_ref, w_ref, r_ref, g_ref, h_ref, n_ref):
        h = lax.dot_general(a_ref[...], w_ref[...], NN, preferred_element_type=F32) + r_ref[...]
        h_ref[...] = h
        r = lax.rsqrt(jnp.mean(h * h, axis=-1, keepdims=True) + RMS_EPS)
        n_ref[...] = (h * r * g_ref[...]).astype(BF16)

    row = pl.BlockSpec((tm, d), lambda i: (i, 0))
    return pl.pallas_call(
        body, name=name, grid=(t // tm,),
        in_specs=[pl.BlockSpec((tm, k), lambda i: (i, 0)), pl.BlockSpec((k, d), lambda i: (0, 0)), row,
                  pl.BlockSpec((1, d), lambda i: (0, 0))],
        out_specs=[row, row], out_shape=[jax.ShapeDtypeStruct((t, d), F32), jax.ShapeDtypeStruct((t, d), BF16)],
        compiler_params=_params(("parallel",)),
    )(a, w, res, g)


def _wgrad(name, a, g, tk=1024, tn=1024):
    t, k = a.shape
    n = g.shape[1]
    tm, tk, tn = min(2 * MM_ROWS, t), min(tk, k), min(tn, n)
    return _mm(name, a, g, grid=(k // tk, n // tn, t // tm),
               a_spec=pl.BlockSpec((tm, tk), lambda p, q, r: (r, p)), b_spec=pl.BlockSpec((tm, tn), lambda p, q, r: (r, q)),
               o_shape=(k, n), o_spec=pl.BlockSpec((tk, tn), lambda p, q, r: (p, q)), dims=TN, out_dtype=BF16, nk=t // tm)


def _peers():
    x, y, c = lax.axis_index("x"), lax.axis_index("y"), lax.axis_index("c")
    me = 4 * x + 2 * y + c
    out = []
    for k in range(1, N_DEV):
        kx, ky, kc = (k >> 2) & 1, (k >> 1) & 1, k & 1
        px = 1 - x if kx else x
        py = 1 - y if ky else y
        pc = 1 - c if kc else c
        out.append(((px, py, pc), 4 * px + 2 * py + pc))
    return me, out


def _cast_weights(ws, pad_rows):
    def body(*refs):
        n = len(refs) // 2
        for i_ref, o_ref, pr in zip(refs[:n], refs[n:], pad_rows):
            r, c = i_ref.shape
            o_ref[0:r, :] = i_ref[...].astype(BF16)
            if pr:
                o_ref[r:r + pr, :] = jnp.zeros((pr, c), BF16)

    return pl.pallas_call(
        body, name="cast_weights", in_specs=[VMEM] * len(ws), out_specs=[VMEM] * len(ws),
        out_shape=[jax.ShapeDtypeStruct((w.shape[0] + pr, w.shape[1]), BF16) for w, pr in zip(ws, pad_rows)],
    )(*ws)


def _window(ref, j, c):
    return ref.at[:, pl.ds(pl.multiple_of(j * c, LANES), c)]


def _scatter_copies(ins, outs, sems, cols, landed):
    send_sems, recv_sems, loc_sems = sems
    n_peer = N_DEV - 1
    me, peers = _peers()

    def src(w, j):
        return _window(ins[w], j, cols[w]) if cols[w] else ins[w].at[j]

    local = [pltpu.make_async_copy(src(w, me), outs[w].at[me], loc_sems.at[w]) for w in range(len(ins))]
    remote = [pltpu.make_async_remote_copy(
        src_ref=src(w, idx), dst_ref=outs[w].at[idx if landed else me],
        send_sem=send_sems.at[w * n_peer + k], recv_sem=recv_sems.at[w * n_peer + k],
        device_id=dev, device_id_type=pl.DeviceIdType.MESH)
        for k, (dev, idx) in reversed(list(enumerate(peers))) for w in range(len(ins))]
    return local, remote


OTHER_CHIPS = (2, 4, 6)


def _gather_copies(ins, outs, sems, cols):
    send_sems, recv_sems, loc_sems = sems
    x, y, c = lax.axis_index("x"), lax.axis_index("y"), lax.axis_index("c")
    me = 4 * x + 2 * y + c
    n_pair = N_DEV - 1

    def dev(mask):
        return (1 - x if mask & 4 else x, 1 - y if mask & 2 else y, 1 - c if mask & 1 else c)

    def slot(w, mask):
        j = jnp.bitwise_xor(me, mask)
        return _window(outs[w], j, cols[w]) if cols[w] else outs[w].at[j]

    def remote(w, pair, src, to_slot, target):
        return pltpu.make_async_remote_copy(src_ref=src, dst_ref=slot(w, to_slot), send_sem=send_sems.at[w * n_pair + pair],
                                            recv_sem=recv_sems.at[w * n_pair + pair], device_id=dev(target),
                                            device_id_type=pl.DeviceIdType.MESH)

    ws = range(len(ins))
    return dict(
        local=[pltpu.make_async_copy(ins[w], slot(w, 0), loc_sems.at[w]) for w in ws],
        to_chips=[remote(w, 1 + t, ins[w], 0, m) for t, m in enumerate(OTHER_CHIPS) for w in ws],
        to_core=[remote(w, 0, ins[w], 0, 1) for w in ws],
        from_chips=[remote(w, 1 + t, ins[w], m, 0) for t, m in enumerate(OTHER_CHIPS) for w in ws],
        pass_on=[remote(w, 4 + t, slot(w, m), m, 1) for t, m in enumerate(OTHER_CHIPS) for w in ws],
        from_core=[remote(w, 0, ins[w], 1, 0) for w in ws]
        + [remote(w, 4 + t, ins[w], m + 1, 0) for t, m in enumerate(OTHER_CHIPS) for w in ws])


def _exchange_start(ins, outs, sems, gather, cols):
    if gather:
        cps = _gather_copies(ins, outs, sems, cols)
        for cp in cps["local"] + cps["to_chips"] + cps["to_core"]:
            cp.start()
    else:
        local, remote = _scatter_copies(ins, outs, sems, cols, False)
        for cp in local + remote:
            cp.start()


def _exchange_pass_on(ins, outs, sems, gather, cols, chips):
    if gather:
        cps = _gather_copies(ins, outs, sems, cols)
        n = len(ins)
        for t in chips:
            for arrived, onward in zip(cps["from_chips"][t * n:(t + 1) * n], cps["pass_on"][t * n:(t + 1) * n]):
                arrived.wait_recv()
                onward.start()


def _exchange_wait(ins, outs, sems, gather, cols):
    if gather:
        cps = _gather_copies(ins, outs, sems, cols)
        for cp in cps["local"]:
            cp.wait()
        for cp in cps["to_chips"] + cps["to_core"] + cps["pass_on"]:
            cp.wait_send()
        for cp in cps["from_core"]:
            cp.wait_recv()
    else:
        local, remote = _scatter_copies(ins, outs, sems, cols, True)
        for cp in local:
            cp.wait()
        for cp in remote:
            cp.wait_send()
            cp.wait_recv()


def _exchange_shapes(arrs, gather, cols):
    n = len(arrs)
    out_shape = []
    for a, c in zip(arrs, cols):
        if gather:
            shape = (a.shape[0], N_DEV * c) if c else (N_DEV,) + a.shape
        else:
            shape = (N_DEV, a.shape[0], c) if c else a.shape
        out_shape.append(jax.ShapeDtypeStruct(shape, a.dtype))
    sems = [pltpu.SemaphoreType.DMA((n * (N_DEV - 1),)), pltpu.SemaphoreType.DMA((n * (N_DEV - 1),)),
            pltpu.SemaphoreType.DMA((n,))]
    return out_shape, sems


def _call(body, *, name, grid, in_specs, out_specs, out_shape, scratch, sem, args, ride=None):
    if ride is None:
        outs = pl.pallas_call(body, name=name, grid=grid, in_specs=in_specs, out_specs=out_specs, out_shape=out_shape,
                              scratch_shapes=scratch, compiler_params=_params(sem))(*args)
        return outs, None
    arrs, gather, cols = ride
    n, n_in, n_out, n_scr = len(arrs), len(in_specs), len(out_specs), len(scratch)
    x_shape, x_sems = _exchange_shapes(arrs, gather, cols)

    def riding(*refs):
        ins, x_ins = refs[:n_in], refs[n_in:n_in + n]
        outs = refs[n_in + n:n_in + n + n_out]
        x_outs = refs[n_in + n + n_out:n_in + 2 * n + n_out]
        scr = refs[n_in + 2 * n + n_out:n_in + 2 * n + n_out + n_scr]
        sems = refs[n_in + 2 * n + n_out + n_scr:]
        def at(step):
            return functools.reduce(jnp.logical_and, [pl.program_id(a) == v for a, v in enumerate(step)])

        @pl.when(at((0,) * len(grid)))
        def _():
            _exchange_start(x_ins, x_outs, sems, gather, cols)

        @pl.when(at((grid[0] // 2,) + (0,) * (len(grid) - 2) + (grid[-1] // 2,)))
        def _():
            _exchange_pass_on(x_ins, x_outs, sems, gather, cols, (0, 1))

        @pl.when(at((grid[0] // 2,) + (0,) * (len(grid) - 2) + (3 * grid[-1] // 4,)))
        def _():
            _exchange_pass_on(x_ins, x_outs, sems, gather, cols, (2,))

        body(*ins, *outs, *scr)

        @pl.when(at(tuple(g - 1 for g in grid)))
        def _():
            _exchange_wait(x_ins, x_outs, sems, gather, cols)

    res = pl.pallas_call(
        riding, name=name, grid=grid, in_specs=list(in_specs) + [ANY] * n, out_specs=list(out_specs) + [ANY] * n,
        out_shape=list(out_shape) + x_shape, scratch_shapes=list(scratch) + x_sems,
        compiler_params=_params(("arbitrary",) * len(grid)))(*args, *arrs)
    return res[:n_out], res[n_out:]


def _my_block():
    return (4 * lax.axis_index("x") + 2 * lax.axis_index("y") + lax.axis_index("c")).astype(jnp.int32).reshape(1)


def _proj_in_gather(n, w_shard):
    t, k = n.shape
    cs = w_shard.shape[1]
    tm = min(MM_ROWS, t)
    ni = t // tm
    arrival = (0, 1) + OTHER_CHIPS + tuple(m + 1 for m in OTHER_CHIPS)

    def mask_at(s):
        return jnp.where(s < 2, s, jnp.where(s < 5, 2 * (s - 1), 2 * (s - 4) + 1))

    def body(me_ref, n_ref, w_hbm, o_ref, all_hbm, w_vmem, send_sems, recv_sems, loc_sems, load_sems):
        s, i = pl.program_id(0), pl.program_id(1)
        cps = _gather_copies([w_hbm], [all_hbm], (send_sems, recv_sems, loc_sems), (cs,))
        arrived = cps["local"] + cps["from_core"][:1] + cps["from_chips"] + cps["from_core"][1:]

        def load(step):
            src = w_hbm if step == 0 else _window(all_hbm, jnp.bitwise_xor(me_ref[0], arrival[step]), cs)
            return pltpu.make_async_copy(src, w_vmem.at[step % 2], load_sems.at[step % 2])

        @pl.when(jnp.logical_and(s == 0, i == 0))
        def _():
            for cp in cps["local"] + cps["to_chips"] + cps["to_core"]:
                cp.start()
            load(0).start()

        for step, mask in enumerate(arrival):
            @pl.when(jnp.logical_and(s == step, i == 0))
            def _(step=step):
                load(step).wait()

            if step + 1 < N_DEV:
                @pl.when(jnp.logical_and(s == step, i == min(1, ni - 1)))
                def _(step=step):
                    arrived[step + 1].wait_recv()
                    if arrival[step + 1] in OTHER_CHIPS:
                        cps["pass_on"][OTHER_CHIPS.index(arrival[step + 1])].start()
                    load(step + 1).start()

        o_ref[...] = lax.dot_general(n_ref[...], w_vmem[s % 2], NN, preferred_element_type=F32).astype(BF16)

        @pl.when(jnp.logical_and(s == N_DEV - 1, i == ni - 1))
        def _():
            cps["local"][0].wait()
            for cp in cps["to_chips"] + cps["to_core"] + cps["pass_on"]:
                cp.wait_send()

    return pl.pallas_call(
        body, name="proj_in",
        grid_spec=pltpu.PrefetchScalarGridSpec(
            num_scalar_prefetch=1, grid=(N_DEV, ni),
            in_specs=[pl.BlockSpec((tm, k), lambda s, i, me: (i, 0)), ANY],
            out_specs=[pl.BlockSpec((tm, cs), lambda s, i, me: (i, jnp.bitwise_xor(me[0], mask_at(s)))), ANY],
            scratch_shapes=[pltpu.VMEM((2, k, cs), BF16), pltpu.SemaphoreType.DMA((N_DEV - 1,)),
                            pltpu.SemaphoreType.DMA((N_DEV - 1,)), pltpu.SemaphoreType.DMA((1,)),
                            pltpu.SemaphoreType.DMA((2,))]),
        out_shape=[jax.ShapeDtypeStruct((t, N_DEV * cs), BF16), jax.ShapeDtypeStruct((k, N_DEV * cs), BF16)],
        compiler_params=_params(("arbitrary", "arbitrary")),
    )(_my_block(), n, w_shard)


def _gw_in_scatter(a, g):
    t, k = a.shape
    cs = g.shape[1] // N_DEV
    tm = min(MM_ROWS, t)
    nr = t // tm
    n_chip = N_DEV // 2
    chips = (6, 4, 2, 0)

    def body(me_ref, a_ref, g_ref, out_hbm, acc, stage, other, core_send, core_recv, chip_send, chip_recv, loc_sem):
        s, r = pl.program_id(0), pl.program_id(1)
        x, y, c = lax.axis_index("x"), lax.axis_index("y"), lax.axis_index("c")
        my_chip = 2 * x + y
        part = lax.dot_general(a_ref[...], g_ref[...], TN, preferred_element_type=F32)

        def to_core(m):
            return pltpu.make_async_remote_copy(src_ref=stage.at[0], dst_ref=other.at[m], send_sem=core_send.at[m],
                                                recv_sem=core_recv.at[m], device_id=(x, y, 1 - c),
                                                device_id_type=pl.DeviceIdType.MESH)

        def to_chip(m, landed):
            mask = chips[m]
            there = (1 - x if mask & 4 else x, 1 - y if mask & 2 else y, c)
            slot = (2 * there[0] + there[1]) if landed else my_chip
            return pltpu.make_async_remote_copy(src_ref=stage.at[1], dst_ref=out_hbm.at[slot], send_sem=chip_send.at[m],
                                                recv_sem=chip_recv.at[m], device_id=there,
                                                device_id_type=pl.DeviceIdType.MESH)

        local = pltpu.make_async_copy(stage.at[1], out_hbm.at[my_chip], loc_sem)

        @pl.when(r == 0)
        def _():
            acc[...] = part

        @pl.when(r > 0)
        def _():
            acc[...] += part

        for step in range(N_DEV):
            m = step // 2

            @pl.when(jnp.logical_and(s == step, r == nr - 1))
            def _(step=step, m=m):
                if step % 2 == 0:
                    if m > 0:
                        to_core(m - 1).wait_send()
                    stage[0] = acc[...].astype(BF16)
                    to_core(m).start()
                else:
                    if m > 0:
                        to_chip(m - 1, False).wait_send()
                    to_core(m).wait_recv()
                    stage[1] = (acc[...] + other[m].astype(F32)).astype(BF16)
                    if m < n_chip - 1:
                        to_chip(m, False).start()
                    else:
                        local.start()
                        to_core(m).wait_send()
                        local.wait()
                        for mm in range(n_chip - 1):
                            to_chip(mm, True).wait_recv()

    return pl.pallas_call(
        body, name="gw_in",
        grid_spec=pltpu.PrefetchScalarGridSpec(
            num_scalar_prefetch=1, grid=(N_DEV, nr),
            in_specs=[pl.BlockSpec((tm, k), lambda s, r, me: (r, 0)),
                      pl.BlockSpec((tm, cs), lambda s, r, me: (r, jnp.bitwise_xor(me[0], N_DEV - 1 - s)))],
            out_specs=ANY,
            scratch_shapes=[pltpu.VMEM((k, cs), F32), pltpu.VMEM((2, k, cs), BF16), pltpu.VMEM((n_chip, k, cs), BF16),
                            pltpu.SemaphoreType.DMA((n_chip,)), pltpu.SemaphoreType.DMA((n_chip,)),
                            pltpu.SemaphoreType.DMA((n_chip - 1,)), pltpu.SemaphoreType.DMA((n_chip - 1,)),
                            pltpu.SemaphoreType.DMA]),
        out_shape=jax.ShapeDtypeStruct((n_chip, k, cs), BF16),
        compiler_params=_params(("arbitrary", "arbitrary")),
    )(_my_block(), a, g)


SMALL_ROWS = 8


def _allreduce_small(parts, loss_part):
    n, d = len(parts), parts[0].shape[1]

    def body(*refs):
        part_refs, loss_ref, o_ref = refs[:n], refs[n], refs[n + 1]
        mine_ref, all_ref, send_sems, recv_sems = refs[n + 2:]
        me, peers = _peers()
        mine_ref[...] = jnp.zeros_like(mine_ref)
        for i, p_ref in enumerate(part_refs):
            mine_ref[i:i + 1, :] = p_ref[...]
        mine_ref[SMALL_ROWS - 1:SMALL_ROWS, 0:LANES] = loss_ref[0:1, :]
        all_ref[me] = mine_ref[...]
        for k, (dev, idx) in enumerate(peers):
            pltpu.make_async_remote_copy(src_ref=mine_ref, dst_ref=all_ref.at[me], send_sem=send_sems.at[k],
                                         recv_sem=recv_sems.at[k], device_id=dev,
                                         device_id_type=pl.DeviceIdType.MESH).start()
        for k, (dev, idx) in enumerate(peers):
            cp = pltpu.make_async_remote_copy(src_ref=mine_ref, dst_ref=all_ref.at[idx], send_sem=send_sems.at[k],
                                              recv_sem=recv_sems.at[k], device_id=dev,
                                              device_id_type=pl.DeviceIdType.MESH)
            cp.wait_send()
            cp.wait_recv()
        tot = all_ref[0]
        for dvc in range(1, N_DEV):
            tot = tot + all_ref[dvc]
        o_ref[...] = tot

    return pl.pallas_call(
        body, name="allreduce_small", in_specs=[VMEM] * (n + 1), out_specs=VMEM,
        out_shape=jax.ShapeDtypeStruct((SMALL_ROWS, d), F32),
        scratch_shapes=[pltpu.VMEM((SMALL_ROWS, d), F32), pltpu.VMEM((N_DEV, SMALL_ROWS, d), F32),
                        pltpu.SemaphoreType.DMA((N_DEV - 1,)), pltpu.SemaphoreType.DMA((N_DEV - 1,))],
    )(*parts, loss_part)


def _adam_math(g, w, m, v):
    m_new = ADAM_B1 * m + (1.0 - ADAM_B1) * g
    v_new = ADAM_B2 * v + (1.0 - ADAM_B2) * (g * g)
    m_hat = m_new / (1.0 - ADAM_B1 ** ADAM_STEP)
    v_hat = v_new / (1.0 - ADAM_B2 ** ADAM_STEP)
    delta = -ADAM_LR * (m_hat / (jnp.sqrt(v_hat) + ADAM_EPS) + ADAM_WD * w)
    return delta, m_new, v_new


def _adam(name, pieces, w, m, v):
    r, c = w.shape
    n_piece, _, cp = pieces.shape
    tr = r
    for cand in (256, 176, 128, 64):
        if r % cand == 0 and r > cand:
            tr = cand
            break

    def body(p_ref, w_ref, m_ref, v_ref, g_ref, d_ref, mo_ref, vo_ref):
        g = p_ref[0, :, 0:c].astype(F32)
        for j in range(1, n_piece):
            g = g + p_ref[j, :, 0:c].astype(F32)
        delta, m_new, v_new = _adam_math(g, w_ref[...], m_ref[...], v_ref[...])
        g_ref[...] = g
        d_ref[...] = delta
        mo_ref[...] = m_new
        vo_ref[...] = v_new

    blk = pl.BlockSpec((tr, c), lambda i: (i, 0))
    osh = jax.ShapeDtypeStruct((r, c), F32)
    return pl.pallas_call(
        body, name=name, grid=(r // tr,),
        in_specs=[pl.BlockSpec((n_piece, tr, cp), lambda i: (0, i, 0)), blk, blk, blk],
        out_specs=[blk, blk, blk, blk], out_shape=[osh, osh, osh, osh],
        compiler_params=_params(("parallel",)),
    )(pieces, w, m, v)


def _adam_small(g_all, ws, ms, vs):
    n = len(ws)

    def body(*refs):
        g_ref, ins, outs = refs[0], refs[1:1 + 3 * n], refs[1 + 3 * n:]
        for i in range(n):
            g = g_ref[i:i + 1, :]
            delta, m_new, v_new = _adam_math(g, ins[i][...], ins[n + i][...], ins[2 * n + i][...])
            for kind, val in enumerate((g, delta, m_new, v_new)):
                outs[kind * n + i][...] = val

    osh = jax.ShapeDtypeStruct(ws[0].shape, F32)
    res = pl.pallas_call(body, name="adam_small", in_specs=[VMEM] * (1 + 3 * n), out_specs=[VMEM] * (4 * n),
                         out_shape=[osh] * (4 * n))(g_all, *ws, *ms, *vs)
    return res[:n], res[n:2 * n], res[2 * n:3 * n], res[3 * n:]


def _local_step(x, mem, pos, tgt, gains, w_in_shard, shards, batch):
    g_mix, g_mem_q, g_mem_kv, g_ffn, g_final = gains
    t, d = x.shape
    s = t // batch
    n_mem = mem.shape[0] // batch
    n_sh = N_DEV
    width = shards[0].shape[0]
    nb = width // LANES

    lane = np.arange(LANES) % HEAD_DIM
    sel_lo = (lane < ROPE_HALF).astype(np.float32)[None, :]
    sel_hi = ((lane >= ROPE_HALF) & (lane < 2 * ROPE_HALF)).astype(np.float32)[None, :]
    freqs = np.float32(ROPE_THETA) ** (-np.arange(ROPE_HALF, dtype=np.float32) / np.float32(ROPE_HALF))
    inv_freq = np.where(lane < 2 * ROPE_HALF, freqs[lane % ROPE_HALF], 0.0).astype(np.float32)[None, :]
    cos_t, sin_a, sin_b = _rope_tables(pos, jnp.asarray(inv_freq), jnp.asarray(sel_lo), jnp.asarray(sel_hi))
    bias = _dilated_bias_tiles(s)

    n1 = _rms_fwd("norm_mix", x, g_mix)
    proj, w_in = _proj_in_gather(n1, w_in_shard)
    qk_a = _rope_apply("rope_fwd", [proj], 2 * width, cos_t, sin_a, sin_b, 1.0)
    cs_up = shards[0].shape[1]
    (o_a, lse_a), (w_up_a, w_up_b, w_out, w_q, w_kv, w_o, w_fd) = _da_fwd(
        qk_a, proj, 2 * nb, bias, batch, s,
        ride=(shards[:6] + shards[8:], True, (cs_up, cs_up, 0, 0, 0, cs_up, 0)))
    (o_b, tot_b), (w_fg, w_fu) = _sb_fwd(proj, 3 * nb, 4 * nb, 5 * nb, batch, s, ride=(shards[6:8], True, (0, 0)))
    w_out = w_out.reshape(d, d)
    w_q = w_q.reshape(d, -1)
    w_kv = w_kv.reshape(d, -1)
    w_fd = w_fd.reshape(-1, d)
    w_fg = w_fg.reshape(-1, d)
    w_fu = w_fu.reshape(-1, d)
    ua, ub, mixed, n2, h1 = _mixer_fwd(o_a, o_b, w_up_a, w_up_b, proj, 6 * nb, w_out, x, g_mem_q)
    mem_n = _rms_fwd("norm_mem_kv", mem, g_mem_kv)
    q_m = _mm_w("mem_q", n2, w_q, BF16)
    kv_m = _mm_w("mem_kv", mem_n, w_kv, BF16)
    o_m = _mem_fwd(q_m, kv_m, batch, s, n_mem)
    h2, n3 = _mm_res_norm("mem_out", o_m, w_o, h1, g_ffn)
    hg, hu, act = _ffn_up(n3, w_fg, w_fu)
    loss_part, dh3, dh3_b, dg_final = _loss_head(act, w_fd, h2, tgt, g_final.reshape(1, d))

    dhg, dhu, dh2, dh2_b, dg_ffn = _ffn_bwd(dh3_b, w_fd, w_fg, w_fu, hg, hu, h2, g_ffn, dh3)
    gw_fd = _wgrad("gw_ffn_down", act, dh3_b)
    gw_fg = _wgrad("gw_ffn_gate", dhg, n3)
    gw_fu = _wgrad("gw_ffn_up", dhu, n3)

    do_m = _mm_w("mem_out_bwd", dh2_b, w_o, BF16, dims=NT)
    gw_o = _wgrad("gw_mem_o", o_m, dh2_b)
    dq_m, dkv_m = _mem_bwd(q_m, kv_m, do_m, batch, s, n_mem)
    gw_q = _wgrad("gw_mem_q", n2, dq_m)
    gw_kv = _wgrad("gw_mem_kv", mem_n, dkv_m)
    (dg_mem_kv,) = _rms_bwd("norm_mem_kv_bwd", (dkv_m, w_kv, NT), mem, g_mem_kv, None, ())
    dh1, dh1_b, dg_mem_q = _rms_bwd("norm_mem_q_bwd", (dq_m, w_q, NT), h1, g_mem_q, dh2, ("f32", "bf16"))

    gw_out = _wgrad("gw_out", mixed, dh1_b)
    dua, dub, dgates = _mixer_bwd(dh1_b, w_out, ua, ub, proj, 6 * nb)
    do_a = _mm_w("up_a_bwd", dua, w_up_a, BF16, dims=NT)
    do_b = _mm_w("up_b_bwd", dub, w_up_b, BF16, dims=NT)
    gw_ua = _wgrad("gw_up_a", o_a, dua)
    gw_ub = _wgrad("gw_up_b", o_b, dub)
    (dq_ar, dk_ar, dv_a), (p_fg, p_fd) = _da_bwd(
        qk_a, proj, 2 * nb, bias, o_a, lse_a, do_a, batch, s,
        ride=([gw_fg.reshape(n_sh, -1, d), gw_fd.reshape(n_sh, -1, d)], False, (0, 0)))
    dqk_a = _rope_apply("rope_bwd", [dq_ar, dk_ar], width, cos_t, sin_a, sin_b, -1.0)
    mid = [gw_ua, gw_ub, gw_out.reshape(n_sh, -1, d), gw_q.reshape(n_sh, -1, gw_q.shape[1]),
           gw_kv.reshape(n_sh, -1, gw_kv.shape[1]), gw_o, gw_fu.reshape(n_sh, -1, d)]
    (dq_b, dk_b, dv_b), (*p_mid, p_fu) = _sb_bwd(proj, 3 * nb, 4 * nb, 5 * nb, tot_b, do_b, batch, s,
                                                 ride=(mid, False, (cs_up, cs_up, 0, 0, 0, cs_up, 0)))
    p_ffn = [p_fg, p_fu, p_fd]
    dproj = jnp.concatenate([dqk_a, dv_a, dq_b, dk_b, dv_b, dgates], axis=1)
    grad_x, dg_mix = _rms_bwd("proj_in_bwd", (dproj, w_in, NT), x, g_mix, dh1, ("f32",))
    p_in = _gw_in_scatter(n1, dproj)
    return loss_part, grad_x, [p_in] + list(p_mid) + p_ffn, (dg_mix, dg_mem_q, dg_mem_kv, dg_ffn, dg_final)


WEIGHTS =("w_in", "w_up_a", "w_up_b", "w_out", "w_q_mem", "w_kv_mem", "w_o_mem", "w_ffn_gate", "w_ffn_up", "w_ffn_down")
GAINS = ("g_mix", "g_mem_q", "g_mem_kv", "g_ffn", "g_final")
ORDER = ("g_mix", "w_in", "w_up_a", "w_up_b", "w_out", "g_mem_q", "g_mem_kv", "w_q_mem", "w_kv_mem", "w_o_mem", "g_ffn",
         "w_ffn_gate", "w_ffn_up", "w_ffn_down", "g_final")


def kernel(x, mem, positions, g_mix, w_in, w_up_a, w_up_b, w_out, g_mem_q, g_mem_kv, w_q_mem, w_kv_mem, w_o_mem, g_ffn, w_ffn_gate, w_ffn_up, w_ffn_down, g_final, loss_target, m_g_mix, m_w_in, m_w_up_a, m_w_up_b, m_w_out, m_g_mem_q, m_g_mem_kv, m_w_q_mem, m_w_kv_mem, m_w_o_mem, m_g_ffn, m_w_ffn_gate, m_w_ffn_up, m_w_ffn_down, m_g_final, v_g_mix, v_w_in, v_w_up_a, v_w_up_b, v_w_out, v_g_mem_q, v_g_mem_kv, v_w_q_mem, v_w_kv_mem, v_w_o_mem, v_g_ffn, v_w_ffn_gate, v_w_ffn_up, v_w_ffn_down, v_g_final):
    given = dict(locals())
    batch, s, d = x.shape
    t = batch * s
    flipped = ("w_ffn_gate", "w_ffn_up")

    def view(a, n):
        a = a.reshape(a.shape[-2:])
        return a.T if n in flipped else a

    def unview(a, n):
        return (a.T if n in flipped else a).reshape(given[n].shape)

    shard = {n: view(given[n], n) for n in WEIGHTS}
    gains = [given[n].reshape(1, d) for n in GAINS]

    pad = (-shard["w_ffn_down"].shape[0]) % LANES
    cast = _cast_weights([shard[n] for n in WEIGHTS], [pad if n in flipped + ("w_ffn_down",) else 0 for n in WEIGHTS])
    loss_part, grad_x, pieces, dgains = _local_step(
        x.reshape(t, d), mem.reshape(-1, d), positions.reshape(t, 1), loss_target.reshape(t, d), gains, cast[0],
        cast[1:], batch)

    grad, delta, new_m, new_v = {}, {}, {}, {}
    for n, p in zip(WEIGHTS, pieces):
        outs = _adam("adam_" + n, p, shard[n], view(given["m_" + n], n), view(given["v_" + n], n))
        grad[n], delta[n], new_m[n], new_v[n] = [unview(o, n) for o in outs]

    g_all = _allreduce_small(list(dgains), loss_part)
    small = _adam_small(g_all, gains, [given["m_" + n].reshape(1, d) for n in GAINS],
                        [given["v_" + n].reshape(1, d) for n in GAINS])
    for out, vals in zip((grad, delta, new_m, new_v), small):
        for n, val in zip(GAINS, vals):
            out[n] = val.reshape(given[n].shape)

    loss = g_all[SMALL_ROWS - 1, 0]
    return (loss, grad_x.reshape(x.shape), *[grad[n] for n in ORDER], *[delta[n] for n in ORDER],
            *[new_m[n] for n in ORDER], *[new_v[n] for n in ORDER])
```

```python
import functools
import math

import jax
import jax.numpy as jnp
import numpy as np
from jax import lax
from jax.experimental import pallas as pl
from jax.experimental.pallas import tpu as pltpu

F32 = jnp.float32
BF16 = jnp.bfloat16

N_DEV = 8
HEAD_DIM = 64
MEM_HEAD_DIM = 128
N_HEADS_MEM = 4
BLOCK = 128
DIL_PATTERNS = ((128, 1), (512, 4), (2048, 16))
ROPE_THETA = 500000.0
ROPE_HALF = 8
RMS_EPS = 1e-6
ADAM_LR, ADAM_B1, ADAM_B2, ADAM_EPS, ADAM_WD, ADAM_STEP = 0.001, 0.9, 0.999, 1e-08, 0.01, 10
NEG = -1e30
ROW_TILE = 512
LANES = 128

ANY = pl.BlockSpec(memory_space=pl.ANY)
VMEM = pl.BlockSpec(memory_space=pltpu.VMEM)
NN = (((1,), (0,)), ((), ()))
NT = (((1,), (1,)), ((), ()))
TN = (((0,), (0,)), ((), ()))


def _params(sem):
    return pltpu.CompilerParams(dimension_semantics=sem)


def _mm(name, a, b, *, grid, a_spec, b_spec, o_shape, o_spec, dims, out_dtype, nk=1):
    def body(*refs):
        a_ref, b_ref, o_ref = refs[0], refs[1], refs[2]
        p = lax.dot_general(a_ref[...], b_ref[...], dims, preferred_element_type=F32)
        if nk == 1:
            o_ref[...] = p.astype(out_dtype)
            return
        acc_ref = refs[-1]
        k = pl.program_id(len(grid) - 1)

        @pl.when(k == 0)
        def _():
            acc_ref[...] = p

        @pl.when(k > 0)
        def _():
            acc_ref[...] += p

        @pl.when(k == nk - 1)
        def _():
            o_ref[...] = acc_ref[...].astype(out_dtype)

    o_block = tuple(d for d in o_spec.block_shape if d is not None)
    sem = ("parallel",) * (len(grid) - 1) + (("arbitrary",) if nk > 1 else ("parallel",))
    return pl.pallas_call(
        body, name=name, grid=grid, in_specs=[a_spec, b_spec],
        out_specs=o_spec, out_shape=jax.ShapeDtypeStruct(o_shape, out_dtype),
        scratch_shapes=[pltpu.VMEM(o_block, F32)] if nk > 1 else [],
        compiler_params=_params(sem),
    )(a, b)


def _rms_fwd(name, x, g):
    t, d = x.shape
    tm = min(ROW_TILE, t)

    def body(x_ref, g_ref, o_ref):
        xf = x_ref[...]
        r = lax.rsqrt(jnp.mean(xf * xf, axis=-1, keepdims=True) + RMS_EPS)
        o_ref[...] = (xf * r * g_ref[...]).astype(BF16)

    return pl.pallas_call(
        body, name=name, grid=(t // tm,),
        in_specs=[pl.BlockSpec((tm, d), lambda i: (i, 0)), pl.BlockSpec((1, d), lambda i: (0, 0))],
        out_specs=pl.BlockSpec((tm, d), lambda i: (i, 0)), out_shape=jax.ShapeDtypeStruct((t, d), BF16),
        compiler_params=_params(("parallel",)),
    )(x, g)


def _rms_bwd_rows(dnf, xf, gv, res):
    r = lax.rsqrt(jnp.mean(xf * xf, axis=-1, keepdims=True) + RMS_EPS)
    xh = xf * r
    dxh = dnf * gv
    dx = r * (dxh - xh * jnp.mean(dxh * xh, axis=-1, keepdims=True))
    if res is not None:
        dx = dx + res
    return dx, jnp.sum(dnf * xh, axis=0, keepdims=True)


def _rms_bwd(name, dn, x, g, dres, want):
    t, d = x.shape
    tm = min(ROW_TILE, t)
    has_res = dres is not None
    lhs = list(dn) if isinstance(dn, tuple) else [dn]
    n_lhs = len(lhs[:2])

    def body(*refs):
        x_ref, g_ref = refs[n_lhs], refs[n_lhs + 1]
        r_ref = refs[n_lhs + 2] if has_res else None
        dx_refs, dg_ref = refs[-1 - len(want):-1], refs[-1]
        if n_lhs == 2:
            dnf = lax.dot_general(refs[0][...], refs[1][...], lhs[2], preferred_element_type=F32)
        else:
            dnf = refs[0][...].astype(F32)
        dx, dg = _rms_bwd_rows(dnf, x_ref[...], g_ref[...], r_ref[...] if has_res else None)
        for kind, dx_ref in zip(want, dx_refs):
            dx_ref[...] = dx.astype(F32 if kind == "f32" else BF16)

        @pl.when(pl.program_id(0) == 0)
        def _():
            dg_ref[...] = jnp.zeros_like(dg_ref)

        dg_ref[...] += dg

    row = pl.BlockSpec((tm, d), lambda i: (i, 0))
    vec = pl.BlockSpec((1, d), lambda i: (0, 0))
    if n_lhs == 2:
        first = [pl.BlockSpec((tm, lhs[0].shape[1]), lambda i: (i, 0)), pl.BlockSpec(lhs[1].shape, lambda i: (0, 0))]
    else:
        first = [row]
    return pl.pallas_call(
        body, name=name, grid=(t // tm,),
        in_specs=first + [row, vec] + ([row] if has_res else []),
        out_specs=[row] * len(want) + [vec],
        out_shape=[jax.ShapeDtypeStruct((t, d), F32 if kind == "f32" else BF16) for kind in want]
        + [jax.ShapeDtypeStruct((1, d), F32)],
        compiler_params=_params(("arbitrary",)),
    )(*(lhs[:2] + [x, g] + ([dres] if has_res else [])))


def _loss_head(a, w, res, tgt, g):
    t, d = res.shape
    k = a.shape[1]
    tm = min(ROW_TILE, t)

    def body(a_ref, w_ref, r_ref, t_ref, g_ref, loss_ref, dh_ref, dhb_ref, dg_ref):
        xf = lax.dot_general(a_ref[...], w_ref[...], NN, preferred_element_type=F32) + r_ref[...]
        gv = g_ref[...]
        r = lax.rsqrt(jnp.mean(xf * xf, axis=-1, keepdims=True) + RMS_EPS)
        xh = xf * r
        e = xh * gv - t_ref[...]
        dy = e * (1.0 / d)
        dxh = dy * gv
        dh = r * (dxh - xh * jnp.mean(dxh * xh, axis=-1, keepdims=True))
        dh_ref[...] = dh
        dhb_ref[...] = dh.astype(BF16)

        @pl.when(pl.program_id(0) == 0)
        def _():
            dg_ref[...] = jnp.zeros_like(dg_ref)
            loss_ref[...] = jnp.zeros_like(loss_ref)

        dg_ref[...] += jnp.sum(dy * xh, axis=0, keepdims=True)
        part = jnp.sum(jnp.sum(e * e, axis=1, keepdims=True), axis=0, keepdims=True) * (0.5 / d)
        loss_ref[...] += jnp.broadcast_to(part, loss_ref.shape)

    row = pl.BlockSpec((tm, d), lambda i: (i, 0))
    vec = pl.BlockSpec((1, d), lambda i: (0, 0))
    return pl.pallas_call(
        body, name="loss_head", grid=(t // tm,),
        in_specs=[pl.BlockSpec((tm, k), lambda i: (i, 0)), pl.BlockSpec((k, d), lambda i: (0, 0)), row, row, vec],
        out_specs=[pl.BlockSpec((8, LANES), lambda i: (0, 0)), row, row, vec],
        out_shape=[jax.ShapeDtypeStruct((8, LANES), F32), jax.ShapeDtypeStruct((t, d), F32),
                   jax.ShapeDtypeStruct((t, d), BF16), jax.ShapeDtypeStruct((1, d), F32)],
        compiler_params=_params(("arbitrary",)),
    )(a, w, res, tgt, g)


def _rope_tables(pos, inv_freq, sel_lo, sel_hi):
    t = pos.shape[0]
    tm = min(ROW_TILE, t)

    def body(p_ref, f_ref, lo_ref, hi_ref, c_ref, sa_ref, sb_ref):
        ang = p_ref[...].astype(F32) * f_ref[...]
        rot = lo_ref[...] + hi_ref[...]
        cs, sn = jnp.cos(ang), jnp.sin(ang)
        c_ref[...] = cs * rot + (1.0 - rot)
        sa_ref[...] = -sn * lo_ref[...]
        sb_ref[...] = sn * hi_ref[...]

    vec = pl.BlockSpec((1, LANES), lambda i: (0, 0))
    row = pl.BlockSpec((tm, LANES), lambda i: (i, 0))
    return pl.pallas_call(
        body, name="rope_tables", grid=(t // tm,),
        in_specs=[pl.BlockSpec((tm, 1), lambda i: (i, 0)), vec, vec, vec],
        out_specs=[row, row, row], out_shape=[jax.ShapeDtypeStruct((t, LANES), F32)] * 3,
        compiler_params=_params(("parallel",)),
    )(pos, inv_freq, sel_lo, sel_hi)


def _rope_apply(name, srcs, width, cos_t, sin_a, sin_b, sign):
    t = srcs[0].shape[0]
    tm = min(ROW_TILE, t)
    n_cols = width // LANES

    def body(*refs):
        x_refs, (c_ref, sa_ref, sb_ref, o_ref) = refs[:len(srcs)], refs[len(srcs):]
        cs, sa, sb = c_ref[...], sign * sa_ref[...], sign * sb_ref[...]
        for a, x_ref in enumerate(x_refs):
            for c in range(n_cols):
                xf = x_ref[:, c * LANES:(c + 1) * LANES].astype(F32)
                up = pltpu.roll(xf, LANES - ROPE_HALF, 1)
                dn = pltpu.roll(xf, ROPE_HALF, 1)
                o_ref[:, a * width + c * LANES:a * width + (c + 1) * LANES] = (xf * cs + up * sa + dn * sb).astype(BF16)

    wide = len(srcs) * width
    tab = pl.BlockSpec((tm, LANES), lambda i: (i, 0))
    return pl.pallas_call(
        body, name=name, grid=(t // tm,),
        in_specs=[pl.BlockSpec((tm, width), lambda i: (i, 0))] * len(srcs) + [tab, tab, tab],
        out_specs=pl.BlockSpec((tm, wide), lambda i: (i, 0)),
        out_shape=jax.ShapeDtypeStruct((t, wide), BF16),
        compiler_params=_params(("parallel",)),
    )(*srcs, cos_t, sin_a, sin_b)


DA_T = 256
MIX_STREAMS = 4
SB_BWD_STREAMS = 2


def _lane_lo():
    return lax.broadcasted_iota(jnp.int32, (BLOCK, LANES), 1) < HEAD_DIM


def _dilated_bias_tiles(s):
    n = s // DA_T
    dist = (np.arange(n)[:, None, None] * DA_T + np.arange(DA_T)[None, :, None] - np.arange(DA_T)[None, None, :])
    cnt = np.zeros(dist.shape, np.float32)
    for window, dil in DIL_PATTERNS:
        cnt += ((dist >= 0) & (dist % dil == 0) & (dist <= window)).astype(np.float32)
    return jnp.asarray(np.where(cnt > 0, np.log(np.maximum(cnt, 1.0)), NEG).astype(np.float32))


def _stack_heads(x, lo):
    zero = jnp.zeros_like(x)
    return jnp.concatenate([jnp.where(lo, x, zero), jnp.where(lo, zero, x)], axis=0)


def _da_fwd(qk, proj, v_col0, bias, batch, s, ride=None, streams=MIX_STREAMS):
    t = qk.shape[0]
    nq = s // DA_T
    n_pairs = 4
    ns = streams
    wide = ns * LANES
    scale = HEAD_DIM ** -0.5

    def body(q_ref, k_ref, v_ref, b_ref, o_ref, lse_ref, acc_ref, m_ref, l_ref):
        i = pl.program_id(2)
        lo = lax.broadcasted_iota(jnp.int32, (DA_T, LANES), 1) < HEAD_DIM
        ones = jnp.ones((DA_T, LANES), BF16)
        acc_ref[...] = jnp.zeros_like(acc_ref)
        m_ref[...] = jnp.full(m_ref.shape, NEG, F32)
        l_ref[...] = jnp.zeros_like(l_ref)
        qqs = [_stack_heads(q_ref[:, st * LANES:(st + 1) * LANES] * scale, lo) for st in range(ns)]

        def scores(st, rows, bias2):
            k = k_ref[rows, st * LANES:(st + 1) * LANES]
            return lax.dot_general(qqs[st], k, NT, preferred_element_type=F32) + bias2

        def softmax(st, sc):
            m_old = m_ref[st]
            m_new = jnp.maximum(m_old, jnp.max(sc, axis=1, keepdims=True))
            m_ref[st] = m_new
            return jnp.exp(sc - m_new).astype(BF16), jnp.exp(m_old - m_new)

        def values(st, rows, p, alpha):
            v = v_ref[rows, st * LANES:(st + 1) * LANES]
            vz = jnp.zeros_like(v)
            l_ref[st] = alpha * l_ref[st] + lax.dot_general(p, ones, NN, preferred_element_type=F32)
            pv = (lax.dot_general(p[:DA_T], jnp.where(lo, v, vz), NN, preferred_element_type=F32)
                  + lax.dot_general(p[DA_T:], jnp.where(lo, vz, v), NN, preferred_element_type=F32))
            acc_ref[st] = acc_ref[st] * jnp.where(lo, alpha[:DA_T], alpha[DA_T:]) + pv

        def trip(dlt, carry):
            rows = pl.ds(pl.multiple_of((i - dlt) * DA_T, DA_T), DA_T)
            bias_t = b_ref[dlt]
            bias2 = jnp.concatenate([bias_t, bias_t], axis=0)
            scs = [scores(st, rows, bias2) for st in range(ns)]
            pas = [softmax(st, scs[st]) for st in range(ns)]
            for st in range(ns):
                values(st, rows, *pas[st])
            return carry

        lax.fori_loop(0, i + 1, trip, 0)
        for st in range(ns):
            cols = slice(st * LANES, (st + 1) * LANES)
            l_t = l_ref[st]
            o_ref[:, cols] = (acc_ref[st] / jnp.where(lo, l_t[:DA_T], l_t[DA_T:])).astype(BF16)
            lse = m_ref[st] + jnp.log(l_t)
            lse_ref[:, cols] = jnp.where(lo, lse[:DA_T], lse[DA_T:])

    blk = pl.BlockSpec((DA_T, wide), lambda b, h, i: (b * nq + i, h))
    return _call(
        body, name="attn_a_fwd", grid=(batch, n_pairs // ns, nq),
        in_specs=[blk,
                  pl.BlockSpec((s, wide), lambda b, h, i: (b, n_pairs // ns + h)),
                  pl.BlockSpec((s, wide), lambda b, h, i: (b, v_col0 // ns + h)),
                  pl.BlockSpec((nq, DA_T, DA_T), lambda b, h, i: (0, 0, 0))],
        out_specs=[blk, blk],
        out_shape=[jax.ShapeDtypeStruct((t, n_pairs * LANES), BF16), jax.ShapeDtypeStruct((t, n_pairs * LANES), F32)],
        scratch=[pltpu.VMEM((ns, DA_T, LANES), F32), pltpu.VMEM((ns, 2 * DA_T, 1), F32),
                 pltpu.VMEM((ns, 2 * DA_T, LANES), F32)],
        sem=("parallel", "parallel", "arbitrary"), args=(qk, qk, proj, bias), ride=ride)


def _da_bwd(qk, proj, v_col0, bias, o, lse, do, batch, s, ride=None, streams=MIX_STREAMS):
    t = qk.shape[0]
    nq = s // DA_T
    n_pairs = 4
    ns = streams
    wide = ns * LANES
    scale = HEAD_DIM ** -0.5

    def body(q_ref, k_ref, v_ref, b_ref, o_ref, lse_ref, do_ref, dq_ref, dk_ref, dv_ref, dk_acc, dv_acc, dq_acc):
        i = pl.program_id(2)
        lo = lax.broadcasted_iota(jnp.int32, (DA_T, LANES), 1) < HEAD_DIM

        @pl.when(i == 0)
        def _():
            dk_acc[...] = jnp.zeros_like(dk_acc)
            dv_acc[...] = jnp.zeros_like(dv_acc)

        dq_acc[...] = jnp.zeros_like(dq_acc)
        qqs, dds, deltas, lses = [], [], [], []
        for st in range(ns):
            cols = slice(st * LANES, (st + 1) * LANES)
            do_ = do_ref[:, cols]
            qqs.append(_stack_heads(q_ref[:, cols] * scale, lo))
            dds.append(_stack_heads(do_, lo))
            prod = do_.astype(F32) * o_ref[:, cols].astype(F32)
            fz = jnp.zeros_like(prod)
            deltas.append(jnp.concatenate([jnp.sum(jnp.where(lo, prod, fz), axis=1, keepdims=True),
                                           jnp.sum(jnp.where(lo, fz, prod), axis=1, keepdims=True)], axis=0))
            lse_t = lse_ref[:, cols]
            lses.append(jnp.concatenate([lse_t[:, 0:1], lse_t[:, HEAD_DIM:HEAD_DIM + 1]], axis=0))

        def products(st, rows, bias2):
            cols = slice(st * LANES, (st + 1) * LANES)
            sc = lax.dot_general(qqs[st], k_ref[rows, cols], NT, preferred_element_type=F32) + bias2
            return sc, lax.dot_general(dds[st], v_ref[rows, cols], NT, preferred_element_type=F32)

        def weights(st, sc, dp):
            p = jnp.exp(sc - lses[st])
            return (p * (dp - deltas[st])).astype(BF16), p.astype(BF16)

        def gradients(st, rows, ds, p):
            cols = slice(st * LANES, (st + 1) * LANES)
            k = k_ref[rows, cols]
            kz = jnp.zeros_like(k)
            dq_acc[st] += (lax.dot_general(ds[:DA_T], jnp.where(lo, k, kz), NN, preferred_element_type=F32)
                           + lax.dot_general(ds[DA_T:], jnp.where(lo, kz, k), NN, preferred_element_type=F32))
            dk_acc[rows, cols] += lax.dot_general(ds, qqs[st], TN, preferred_element_type=F32)
            dv_acc[rows, cols] += lax.dot_general(p, dds[st], TN, preferred_element_type=F32)

        def trip(dlt, carry):
            rows = pl.ds(pl.multiple_of((i - dlt) * DA_T, DA_T), DA_T)
            bias_t = b_ref[dlt]
            bias2 = jnp.concatenate([bias_t, bias_t], axis=0)
            prods = [products(st, rows, bias2) for st in range(ns)]
            wts = [weights(st, *prods[st]) for st in range(ns)]
            for st in range(ns):
                gradients(st, rows, *wts[st])
            return carry

        lax.fori_loop(0, i + 1, trip, 0)
        for st in range(ns):
            dq_ref[:, st * LANES:(st + 1) * LANES] = (dq_acc[st] * scale).astype(BF16)

        @pl.when(i == nq - 1)
        def _():
            dk_ref[...] = dk_acc[...].astype(BF16)
            dv_ref[...] = dv_acc[...].astype(BF16)

    blk = pl.BlockSpec((DA_T, wide), lambda b, h, i: (b * nq + i, h))
    seq = pl.BlockSpec((s, wide), lambda b, h, i: (b, h), pipeline_mode=pl.Buffered(1))
    one = pl.Buffered(1)
    out = jax.ShapeDtypeStruct((t, n_pairs * LANES), BF16)
    return _call(
        body, name="attn_a_bwd", grid=(batch, n_pairs // ns, nq),
        in_specs=[blk,
                  pl.BlockSpec((s, wide), lambda b, h, i: (b, n_pairs // ns + h), pipeline_mode=one),
                  pl.BlockSpec((s, wide), lambda b, h, i: (b, v_col0 // ns + h), pipeline_mode=one),
                  pl.BlockSpec((nq, DA_T, DA_T), lambda b, h, i: (0, 0, 0), pipeline_mode=one),
                  blk, blk, blk],
        out_specs=[blk, seq, seq], out_shape=[out, out, out],
        scratch=[pltpu.VMEM((s, wide), F32), pltpu.VMEM((s, wide), F32), pltpu.VMEM((ns, DA_T, LANES), F32)],
        sem=("parallel", "parallel", "arbitrary"), args=(qk, qk, proj, bias, o, lse, do), ride=ride)


SB_Q = 256


def _sb_consts(after):
    r = lax.broadcasted_iota(jnp.int32, (2 * BLOCK, 2 * BLOCK), 0) % BLOCK
    c = lax.broadcasted_iota(jnp.int32, (2 * BLOCK, 2 * BLOCK), 1)
    tri = (r > c) if after else (r < c)
    return jnp.logical_or(c >= BLOCK, tri).astype(BF16)


def _split(x):
    hi = x.astype(BF16)
    lo = (x - hi.astype(F32)).astype(BF16)
    return jnp.concatenate([hi, lo], axis=1)


def _sb_fwd(proj, q_col0, k_col0, v_col0, batch, s, ride=None, streams=MIX_STREAMS):
    t = proj.shape[0]
    nq = s // SB_Q
    n_pairs = 4
    ns = streams
    wide = ns * LANES
    scale = HEAD_DIM ** -0.5

    def body(q_ref, k_ref, v_ref, o_ref, tot_ref, acc_ref, run_ref):
        i = pl.program_id(2)
        lo_q = lax.broadcasted_iota(jnp.int32, (SB_Q, LANES), 1) < HEAD_DIM
        lo_k = _lane_lo()
        mat = _sb_consts(True)
        row = lax.broadcasted_iota(jnp.int32, (2 * SB_Q, LANES), 0) % SB_Q
        ahead = row - lax.broadcasted_iota(jnp.int32, (2 * SB_Q, LANES), 1)
        acc_ref[...] = jnp.zeros_like(acc_ref)
        run_ref[...] = jnp.zeros_like(run_ref)
        qqs = [_stack_heads(q_ref[:, st * LANES:(st + 1) * LANES] * scale, lo_q) for st in range(ns)]

        def units(todo):
            def rows(j):
                return pl.ds(pl.multiple_of(j * BLOCK, BLOCK), BLOCK)

            zs = [lax.dot_general(qqs[st], k_ref[rows(j), st * LANES:(st + 1) * LANES], NT, preferred_element_type=F32)
                  for st, j, _ in todo]
            logs = []
            for z, (_, _, off) in zip(zs, todo):
                lsig = jnp.minimum(z, 0.0) - jnp.log(1.0 + jnp.exp(-jnp.abs(z)))
                lneg = lsig - z
                if off is not None:
                    lneg = jnp.where(ahead > off, lneg, 0.0)
                logs.append((lsig, _split(lneg)))
            sums = [lax.dot_general(cat, mat, NN, preferred_element_type=F32) for _, cat in logs]
            probs = []
            for (lsig, _), sm, (st, _, off) in zip(logs, sums, todo):
                run = run_ref[st]
                a = jnp.exp(lsig + run + sm[:, :BLOCK])
                if off is not None:
                    a = jnp.where(ahead > off, a, 0.0)
                run_ref[st] = run + sm[:, BLOCK:]
                probs.append(a.astype(BF16))
            for ab, (st, j, _) in zip(probs, todo):
                v = v_ref[rows(j), st * LANES:(st + 1) * LANES]
                vz = jnp.zeros_like(v)
                acc_ref[st] += (lax.dot_general(ab[:SB_Q], jnp.where(lo_k, v, vz), NN, preferred_element_type=F32)
                                + lax.dot_general(ab[SB_Q:], jnp.where(lo_k, vz, v), NN, preferred_element_type=F32))

        units([(st, 2 * i + 1, BLOCK) for st in range(ns)] + [(st, 2 * i, 0) for st in range(ns)])

        def pair(p, carry):
            jp = i - 1 - p
            units([(st, 2 * jp + 1, None) for st in range(ns)] + [(st, 2 * jp, None) for st in range(ns)])
            return carry

        lax.fori_loop(0, i, pair, 0)
        for st in range(ns):
            cols = slice(st * LANES, (st + 1) * LANES)
            o_ref[:, cols] = acc_ref[st].astype(BF16)
            tot_ref[:, cols] = jnp.where(lo_q, run_ref[st, 0:SB_Q, :], run_ref[st, SB_Q:2 * SB_Q, :])

    def seq(col0):
        return pl.BlockSpec((s, wide), lambda b, h, i: (b, col0 // ns + h))

    blk = pl.BlockSpec((SB_Q, wide), lambda b, h, i: (b * nq + i, h))
    return _call(
        body, name="attn_b_fwd", grid=(batch, n_pairs // ns, nq),
        in_specs=[pl.BlockSpec((SB_Q, wide), lambda b, h, i: (b * nq + i, q_col0 // ns + h)), seq(k_col0), seq(v_col0)],
        out_specs=[blk, blk],
        out_shape=[jax.ShapeDtypeStruct((t, n_pairs * LANES), BF16), jax.ShapeDtypeStruct((t, n_pairs * LANES), F32)],
        scratch=[pltpu.VMEM((ns, SB_Q, LANES), F32), pltpu.VMEM((ns, 2 * SB_Q, LANES), F32)],
        sem=("parallel", "parallel", "arbitrary"), args=(proj, proj, proj), ride=ride)


def _sb_bwd(proj, q_col0, k_col0, v_col0, tot, do, batch, s, ride=None, streams=SB_BWD_STREAMS):
    t = proj.shape[0]
    nq = s // SB_Q
    n_pairs = 4
    ns = streams
    wide = ns * LANES
    scale = HEAD_DIM ** -0.5

    def body(q_ref, k_ref, v_ref, tot_ref, do_ref, dq_ref, dk_ref, dv_ref, dk_acc, dv_acc, dq_acc, seen_ref, gsum_ref):
        i = pl.program_id(2)
        lo_q = lax.broadcasted_iota(jnp.int32, (SB_Q, LANES), 1) < HEAD_DIM
        lo_k = _lane_lo()

        @pl.when(i == 0)
        def _():
            dk_acc[...] = jnp.zeros_like(dk_acc)
            dv_acc[...] = jnp.zeros_like(dv_acc)

        mat_after = _sb_consts(True)
        mat_before = _sb_consts(False)
        row = lax.broadcasted_iota(jnp.int32, (2 * SB_Q, LANES), 0) % SB_Q
        ahead = row - lax.broadcasted_iota(jnp.int32, (2 * SB_Q, LANES), 1)
        dq_acc[...] = jnp.zeros_like(dq_acc)
        seen_ref[...] = jnp.zeros_like(seen_ref)
        gsum_ref[...] = jnp.zeros_like(gsum_ref)
        qqs, dds, totals = [], [], []
        for st in range(ns):
            cols = slice(st * LANES, (st + 1) * LANES)
            qqs.append(_stack_heads(q_ref[:, cols] * scale, lo_q))
            dds.append(_stack_heads(do_ref[:, cols], lo_q))
            tot_t = tot_ref[:, cols]
            totals.append(jnp.concatenate([jnp.broadcast_to(tot_t[:, 0:1], (SB_Q, LANES)),
                                           jnp.broadcast_to(tot_t[:, HEAD_DIM:HEAD_DIM + 1], (SB_Q, LANES))], axis=0))

        def units(todo):
            def rows(j):
                return pl.ds(pl.multiple_of(j * BLOCK, BLOCK), BLOCK)

            def cols(st):
                return slice(st * LANES, (st + 1) * LANES)

            prods = [(lax.dot_general(qqs[st], k_ref[rows(j), cols(st)], NT, preferred_element_type=F32),
                      lax.dot_general(dds[st], v_ref[rows(j), cols(st)], NT, preferred_element_type=F32))
                     for st, j, _ in todo]
            logs = []
            for (z, _), (_, _, off) in zip(prods, todo):
                lsig = jnp.minimum(z, 0.0) - jnp.log(1.0 + jnp.exp(-jnp.abs(z)))
                lneg = lsig - z
                if off is not None:
                    lneg = jnp.where(ahead > off, lneg, 0.0)
                logs.append((lsig, _split(lneg)))
            sums = [lax.dot_general(cat, mat_after, NN, preferred_element_type=F32) for _, cat in logs]
            gates = []
            for (lsig, _), sm, (_, da), (st, _, off) in zip(logs, sums, prods, todo):
                seen = seen_ref[st]
                a = jnp.exp(lsig + (totals[st] - seen - sm[:, BLOCK:]) + sm[:, :BLOCK])
                if off is not None:
                    a = jnp.where(ahead > off, a, 0.0)
                seen_ref[st] = seen + sm[:, BLOCK:]
                g = a * da
                gates.append((a.astype(BF16), g, _split(g)))
            gsums = [lax.dot_general(cat, mat_before, NN, preferred_element_type=F32) for _, _, cat in gates]
            outs = []
            for (lsig, _), (ab, g, _), gs, (st, _, off) in zip(logs, gates, gsums, todo):
                gsum = gsum_ref[st]
                dz = g - jnp.exp(lsig) * (g + gsum + gs[:, :BLOCK])
                if off is not None:
                    dz = jnp.where(ahead > off, dz, 0.0)
                gsum_ref[st] = gsum + gs[:, BLOCK:]
                outs.append((dz.astype(BF16), ab))
            for (dzb, ab), (st, j, _) in zip(outs, todo):
                k = k_ref[rows(j), cols(st)]
                kz = jnp.zeros_like(k)
                dq_acc[st] += (lax.dot_general(dzb[:SB_Q], jnp.where(lo_k, k, kz), NN, preferred_element_type=F32)
                               + lax.dot_general(dzb[SB_Q:], jnp.where(lo_k, kz, k), NN, preferred_element_type=F32))
                dk_acc[rows(j), cols(st)] += lax.dot_general(dzb, qqs[st], TN, preferred_element_type=F32)
                dv_acc[rows(j), cols(st)] += lax.dot_general(ab, dds[st], TN, preferred_element_type=F32)

        def pair(p, carry):
            units([(st, 2 * p, None) for st in range(ns)] + [(st, 2 * p + 1, None) for st in range(ns)])
            return carry

        lax.fori_loop(0, i, pair, 0)
        units([(st, 2 * i, 0) for st in range(ns)] + [(st, 2 * i + 1, BLOCK) for st in range(ns)])
        for st in range(ns):
            dq_ref[:, st * LANES:(st + 1) * LANES] = (dq_acc[st] * scale).astype(BF16)

        @pl.when(i == nq - 1)
        def _():
            dk_ref[...] = dk_acc[...].astype(BF16)
            dv_ref[...] = dv_acc[...].astype(BF16)

    def seq_in(col0):
        return pl.BlockSpec((s, wide), lambda b, h, i: (b, col0 // ns + h))

    blk = pl.BlockSpec((SB_Q, wide), lambda b, h, i: (b * nq + i, h))
    seq = pl.BlockSpec((s, wide), lambda b, h, i: (b, h))
    out = jax.ShapeDtypeStruct((t, n_pairs * LANES), BF16)
    return _call(
        body, name="attn_b_bwd", grid=(batch, n_pairs // ns, nq),
        in_specs=[pl.BlockSpec((SB_Q, wide), lambda b, h, i: (b * nq + i, q_col0 // ns + h)), seq_in(k_col0),
                  seq_in(v_col0), blk, blk],
        out_specs=[blk, seq, seq], out_shape=[out, out, out],
        scratch=[pltpu.VMEM((s, wide), F32), pltpu.VMEM((s, wide), F32), pltpu.VMEM((ns, SB_Q, LANES), F32),
                 pltpu.VMEM((ns, 2 * SB_Q, LANES), F32), pltpu.VMEM((ns, 2 * SB_Q, LANES), F32)],
        sem=("parallel", "parallel", "arbitrary"), args=(proj, proj, proj, tot, do), ride=ride)


MEM_Q_TILE = 512


def _mem_fwd(q, kv, batch, s, n_mem):
    t, width = q.shape
    tq = min(MEM_Q_TILE, s)
    nq = s // tq
    scale = MEM_HEAD_DIM ** -0.5

    def body(q_ref, kv_ref, o_ref):
        for h in range(N_HEADS_MEM):
            cols = slice(h * MEM_HEAD_DIM, (h + 1) * MEM_HEAD_DIM)
            k = kv_ref[:, cols]
            v = kv_ref[:, width + h * MEM_HEAD_DIM: width + (h + 1) * MEM_HEAD_DIM]
            sc = lax.dot_general(q_ref[:, cols], k, NT, preferred_element_type=F32) * scale
            p = jnp.exp(sc - jnp.max(sc, axis=1, keepdims=True))
            p = p / jnp.sum(p, axis=1, keepdims=True)
            o_ref[:, cols] = lax.dot_general(p.astype(BF16), v, NN, preferred_element_type=F32).astype(BF16)

    return pl.pallas_call(
        body, name="mem_attn_fwd", grid=(batch, nq),
        in_specs=[pl.BlockSpec((tq, width), lambda b, i: (b * nq + i, 0)),
                  pl.BlockSpec((n_mem, 2 * width), lambda b, i: (b, 0))],
        out_specs=pl.BlockSpec((tq, width), lambda b, i: (b * nq + i, 0)),
        out_shape=jax.ShapeDtypeStruct((t, width), BF16),
        compiler_params=_params(("parallel", "parallel")),
    )(q, kv)


def _mem_bwd(q, kv, do, batch, s, n_mem):
    t, width = q.shape
    tq = min(MEM_Q_TILE, s)
    nq = s // tq
    scale = MEM_HEAD_DIM ** -0.5

    def body(q_ref, kv_ref, do_ref, dq_ref, dkv_ref, acc):
        i = pl.program_id(1)

        @pl.when(i == 0)
        def _():
            acc[...] = jnp.zeros_like(acc)

        for h in range(N_HEADS_MEM):
            cols = slice(h * MEM_HEAD_DIM, (h + 1) * MEM_HEAD_DIM)
            vcols = slice(width + h * MEM_HEAD_DIM, width + (h + 1) * MEM_HEAD_DIM)
            qh, k, v, doh = q_ref[:, cols], kv_ref[:, cols], kv_ref[:, vcols], do_ref[:, cols]
            sc = lax.dot_general(qh, k, NT, preferred_element_type=F32) * scale
            p = jnp.exp(sc - jnp.max(sc, axis=1, keepdims=True))
            p = p / jnp.sum(p, axis=1, keepdims=True)
            dp = lax.dot_general(doh, v, NT, preferred_element_type=F32)
            ds = (p * (dp - jnp.sum(p * dp, axis=1, keepdims=True)) * scale).astype(BF16)
            dq_ref[:, cols] = lax.dot_general(ds, k, NN, preferred_element_type=F32).astype(BF16)
            acc[:, cols] += lax.dot_general(ds, qh, TN, preferred_element_type=F32)
            acc[:, vcols] += lax.dot_general(p.astype(BF16), doh, TN, preferred_element_type=F32)

        @pl.when(i == nq - 1)
        def _():
            dkv_ref[...] = acc[...].astype(BF16)

    row = pl.BlockSpec((tq, width), lambda b, i: (b * nq + i, 0))
    kvs = pl.BlockSpec((n_mem, 2 * width), lambda b, i: (b, 0))
    return pl.pallas_call(
        body, name="mem_attn_bwd", grid=(batch, nq),
        in_specs=[row, kvs, row], out_specs=[row, kvs],
        out_shape=[jax.ShapeDtypeStruct((t, width), BF16), jax.ShapeDtypeStruct((batch * n_mem, 2 * width), BF16)],
        scratch_shapes=[pltpu.VMEM((n_mem, 2 * width), F32)],
        compiler_params=_params(("parallel", "arbitrary")),
    )(q, kv, do)


def _mixer_fwd(o_a, o_b, w_a, w_b, proj, gate_col0, w_out, x, g):
    t, width = o_a.shape
    d = w_a.shape[1]
    tm = min(ROW_TILE, t)
    gb0 = gate_col0 * LANES // d

    def body(oa_ref, ob_ref, wa_ref, wb_ref, ga_ref, gb_ref, wo_ref, x_ref, g_ref, ua_ref, ub_ref, mix_ref, n_ref,
             h_ref):
        ua = lax.dot_general(oa_ref[...], wa_ref[...], NN, preferred_element_type=F32)
        ub = lax.dot_general(ob_ref[...], wb_ref[...], NN, preferred_element_type=F32)
        ua_ref[...] = ua.astype(BF16)
        ub_ref[...] = ub.astype(BF16)
        mixed = (jax.nn.sigmoid(ga_ref[...].astype(F32)) * ua + jax.nn.sigmoid(gb_ref[...].astype(F32)) * ub).astype(BF16)
        mix_ref[...] = mixed
        h = lax.dot_general(mixed, wo_ref[...], NN, preferred_element_type=F32) + x_ref[...]
        h_ref[...] = h
        r = lax.rsqrt(jnp.mean(h * h, axis=-1, keepdims=True) + RMS_EPS)
        n_ref[...] = (h * r * g_ref[...]).astype(BF16)

    row = pl.BlockSpec((tm, width), lambda i: (i, 0))
    wsp = pl.BlockSpec((width, d), lambda i: (0, 0))
    out = pl.BlockSpec((tm, d), lambda i: (i, 0))
    osh = jax.ShapeDtypeStruct((t, d), BF16)
    return pl.pallas_call(
        body, name="mixer_fwd", grid=(t // tm,),
        in_specs=[row, row, wsp, wsp,
                  pl.BlockSpec((tm, d), lambda i: (i, gb0)), pl.BlockSpec((tm, d), lambda i: (i, gb0 + 1)),
                  pl.BlockSpec((d, d), lambda i: (0, 0)), out, pl.BlockSpec((1, d), lambda i: (0, 0))],
        out_specs=[out, out, out, out, out], out_shape=[osh, osh, osh, osh, jax.ShapeDtypeStruct((t, d), F32)],
        compiler_params=_params(("parallel",)),
    )(o_a, o_b, w_a, w_b, proj, proj, w_out, x, g)


def _mixer_bwd(dh, w_out, ua, ub, proj, gate_col0, w_a, w_b):
    t, d = dh.shape
    width = w_a.shape[0]
    tm = min(ROW_TILE, t)
    nc = d // LANES

    def body(dh_ref, w_ref, ua_ref, ub_ref, ga_ref, gb_ref, wa_ref, wb_ref, dua_ref, dub_ref, dg_ref, doa_ref, dob_ref):
        dm = lax.dot_general(dh_ref[...], w_ref[...], NT, preferred_element_type=F32)
        sa = jax.nn.sigmoid(ga_ref[...].astype(F32))
        sb = jax.nn.sigmoid(gb_ref[...].astype(F32))
        dua = (dm * sa).astype(BF16)
        dub = (dm * sb).astype(BF16)
        dua_ref[...] = dua
        dub_ref[...] = dub
        dg_ref[:, 0:d] = (dm * ua_ref[...].astype(F32) * sa * (1.0 - sa)).astype(BF16)
        dg_ref[:, d:2 * d] = (dm * ub_ref[...].astype(F32) * sb * (1.0 - sb)).astype(BF16)
        doa_ref[...] = lax.dot_general(dua, wa_ref[...], NT, preferred_element_type=F32).astype(BF16)
        dob_ref[...] = lax.dot_general(dub, wb_ref[...], NT, preferred_element_type=F32).astype(BF16)

    row = pl.BlockSpec((tm, d), lambda i: (i, 0))
    wsp = pl.BlockSpec((width, d), lambda i: (0, 0))
    osp = pl.BlockSpec((tm, width), lambda i: (i, 0))
    return pl.pallas_call(
        body, name="mixer_bwd", grid=(t // tm,),
        in_specs=[row, pl.BlockSpec((d, d), lambda i: (0, 0)), row, row,
                  pl.BlockSpec((tm, d), lambda i: (i, gate_col0 // nc)),
                  pl.BlockSpec((tm, d), lambda i: (i, gate_col0 // nc + 1)), wsp, wsp],
        out_specs=[row, row, pl.BlockSpec((tm, 2 * d), lambda i: (i, 0)), osp, osp],
        out_shape=[jax.ShapeDtypeStruct((t, d), BF16), jax.ShapeDtypeStruct((t, d), BF16),
                   jax.ShapeDtypeStruct((t, 2 * d), BF16), jax.ShapeDtypeStruct((t, width), BF16),
                   jax.ShapeDtypeStruct((t, width), BF16)],
        compiler_params=_params(("parallel",)),
    )(dh, w_out, ua, ub, proj, proj, w_a, w_b)


FFN_COLS = 1024


def _ffn_up(n, w_gate, w_up):
    t, d = n.shape
    hidden = w_gate.shape[0]
    tm = min(ROW_TILE, t)
    tn = min(FFN_COLS, hidden)

    def body(n_ref, wg_ref, wu_ref, hg_ref, hu_ref, act_ref):
        hg = lax.dot_general(n_ref[...], wg_ref[...], NT, preferred_element_type=F32)
        hu = lax.dot_general(n_ref[...], wu_ref[...], NT, preferred_element_type=F32)
        hg_ref[...] = hg.astype(BF16)
        hu_ref[...] = hu.astype(BF16)
        act_ref[...] = (hg * jax.nn.sigmoid(hg) * hu).astype(BF16)

    wsp = pl.BlockSpec((tn, d), lambda j, i: (j, 0))
    out = pl.BlockSpec((tm, tn), lambda j, i: (i, j))
    osh = jax.ShapeDtypeStruct((t, hidden), BF16)
    return pl.pallas_call(
        body, name="ffn_up", grid=(hidden // tn, t // tm),
        in_specs=[pl.BlockSpec((tm, d), lambda j, i: (i, 0)), wsp, wsp],
        out_specs=[out, out, out], out_shape=[osh, osh, osh],
        compiler_params=_params(("parallel", "parallel")),
    )(n, w_gate, w_up)


def _ffn_bwd(dh, w_down, w_gate, w_up, hg, hu, x, g, dres):
    t, d = dh.shape
    hidden = w_down.shape[0]
    tm = min(ROW_TILE, t)
    tn = min(FFN_COLS, hidden)
    nj = hidden // tn

    def body(dh_ref, wd_ref, wg_ref, wu_ref, hg_ref, hu_ref, x_ref, g_ref, r_ref, dhg_ref, dhu_ref, dx_ref, dxb_ref,
             dg_ref, acc):
        j, i = pl.program_id(0), pl.program_id(1)
        dact = lax.dot_general(dh_ref[...], wd_ref[...], NT, preferred_element_type=F32)
        hg = hg_ref[...].astype(F32)
        sg = jax.nn.sigmoid(hg)
        dhu = (dact * hg * sg).astype(BF16)
        dhg = (dact * hu_ref[...].astype(F32) * sg * (1.0 + hg * (1.0 - sg))).astype(BF16)
        dhu_ref[...] = dhu
        dhg_ref[...] = dhg
        part = (lax.dot_general(dhg, wg_ref[...], NN, preferred_element_type=F32)
                + lax.dot_general(dhu, wu_ref[...], NN, preferred_element_type=F32))

        @pl.when(j == 0)
        def _():
            acc[i] = part

        @pl.when(j > 0)
        def _():
            acc[i] += part

        @pl.when(jnp.logical_and(j == 0, i == 0))
        def _():
            dg_ref[...] = jnp.zeros_like(dg_ref)

        @pl.when(j == nj - 1)
        def _():
            dx, dg = _rms_bwd_rows(acc[i], x_ref[...], g_ref[...], r_ref[...])
            dx_ref[...] = dx
            dxb_ref[...] = dx.astype(BF16)
            dg_ref[...] += dg

    hid = pl.BlockSpec((tm, tn), lambda j, i: (i, j))
    wsp = pl.BlockSpec((tn, d), lambda j, i: (j, 0), pipeline_mode=pl.Buffered(1))
    late = pl.BlockSpec((tm, d), lambda j, i: (jnp.where(j == nj - 1, i, 0), 0))
    vec = pl.BlockSpec((1, d), lambda j, i: (0, 0))
    osh = jax.ShapeDtypeStruct((t, hidden), BF16)
    return pl.pallas_call(
        body, name="ffn_bwd", grid=(nj, t // tm),
        in_specs=[pl.BlockSpec((tm, d), lambda j, i: (i, 0)), wsp, wsp, wsp, hid, hid, late, vec, late],
        out_specs=[hid, hid, late, late, vec],
        out_shape=[osh, osh, jax.ShapeDtypeStruct((t, d), F32), jax.ShapeDtypeStruct((t, d), BF16),
                   jax.ShapeDtypeStruct((1, d), F32)],
        scratch_shapes=[pltpu.VMEM((t // tm, tm, d), F32)],
        compiler_params=_params(("arbitrary", "arbitrary")),
    )(dh, w_down, w_gate, w_up, hg, hu, x, g, dres)


MM_ROWS = 1024


def _mm_w(name, a, w, out_dtype, dims=NN):
    t, k = a.shape
    n = w.shape[1] if dims == NN else w.shape[0]
    tm, tn = min(MM_ROWS, t), min(1024, n)
    o_spec = pl.BlockSpec((tm, tn), lambda j, i: (i, j))
    b_spec = pl.BlockSpec((k, tn), lambda j, i: (0, j)) if dims == NN else pl.BlockSpec((tn, k), lambda j, i: (j, 0))
    return _mm(name, a, w, grid=(n // tn, t // tm), a_spec=pl.BlockSpec((tm, k), lambda j, i: (i, 0)), b_spec=b_spec,
               o_shape=(t, n), o_spec=o_spec, dims=dims, out_dtype=out_dtype)


def _mm_res_norm(name, a, w, res, g):
    t, k = a.shape
    d = w.shape[1]
    tm = min(ROW_TILE, t)

    def body(a_ref, w_ref, r_ref, g_ref, h_ref, n_ref):
        h = lax.dot_general(a_ref[...], w_ref[...], NN, preferred_element_type=F32) + r_ref[...]
        h_ref[...] = h
        r = lax.rsqrt(jnp.mean(h * h, axis=-1, keepdims=True) + RMS_EPS)
        n_ref[...] = (h * r * g_ref[...]).astype(BF16)

    row = pl.BlockSpec((tm, d), lambda i: (i, 0))
    return pl.pallas_call(
        body, name=name, grid=(t // tm,),
        in_specs=[pl.BlockSpec((tm, k), lambda i: (i, 0)), pl.BlockSpec((k, d), lambda i: (0, 0)), row,
                  pl.BlockSpec((1, d), lambda i: (0, 0))],
        out_specs=[row, row], out_shape=[jax.ShapeDtypeStruct((t, d), F32), jax.ShapeDtypeStruct((t, d), BF16)],
        compiler_params=_params(("parallel",)),
    )(a, w, res, g)


def _wgrad(name, a, g, tk=1024, tn=1024):
    t, k = a.shape
    n = g.shape[1]
    tm, tk, tn = min(2 * MM_ROWS, t), min(tk, k), min(tn, n)
    return _mm(name, a, g, grid=(k // tk, n // tn, t // tm),
               a_spec=pl.BlockSpec((tm, tk), lambda p, q, r: (r, p)), b_spec=pl.BlockSpec((tm, tn), lambda p, q, r: (r, q)),
               o_shape=(k, n), o_spec=pl.BlockSpec((tk, tn), lambda p, q, r: (p, q)), dims=TN, out_dtype=BF16, nk=t // tm)


def _peers():
    x, y, c = lax.axis_index("x"), lax.axis_index("y"), lax.axis_index("c")
    me = 4 * x + 2 * y + c
    out = []
    for k in range(1, N_DEV):
        kx, ky, kc = (k >> 2) & 1, (k >> 1) & 1, k & 1
        px = 1 - x if kx else x
        py = 1 - y if ky else y
        pc = 1 - c if kc else c
        out.append(((px, py, pc), 4 * px + 2 * py + pc))
    return me, out


def _cast_weights(ws, pad_rows):
    def body(*refs):
        n = len(refs) // 2
        for i_ref, o_ref, pr in zip(refs[:n], refs[n:], pad_rows):
            r, c = i_ref.shape
            o_ref[0:r, :] = i_ref[...].astype(BF16)
            if pr:
                o_ref[r:r + pr, :] = jnp.zeros((pr, c), BF16)

    return pl.pallas_call(
        body, name="cast_weights", in_specs=[VMEM] * len(ws), out_specs=[VMEM] * len(ws),
        out_shape=[jax.ShapeDtypeStruct((w.shape[0] + pr, w.shape[1]), BF16) for w, pr in zip(ws, pad_rows)],
    )(*ws)


def _window(ref, j, c):
    return ref.at[:, pl.ds(pl.multiple_of(j * c, LANES), c)]


def _scatter_copies(ins, outs, sems, cols, landed):
    send_sems, recv_sems, loc_sems = sems
    n_peer = N_DEV - 1
    me, peers = _peers()

    def src(w, j):
        return _window(ins[w], j, cols[w]) if cols[w] else ins[w].at[j]

    local = [pltpu.make_async_copy(src(w, me), outs[w].at[me], loc_sems.at[w]) for w in range(len(ins))]
    remote = [pltpu.make_async_remote_copy(
        src_ref=src(w, idx), dst_ref=outs[w].at[idx if landed else me],
        send_sem=send_sems.at[w * n_peer + k], recv_sem=recv_sems.at[w * n_peer + k],
        device_id=dev, device_id_type=pl.DeviceIdType.MESH)
        for k, (dev, idx) in reversed(list(enumerate(peers))) for w in range(len(ins))]
    return local, remote


OTHER_CHIPS = (2, 4, 6)


def _gather_copies(ins, outs, sems, cols):
    send_sems, recv_sems, loc_sems = sems
    x, y, c = lax.axis_index("x"), lax.axis_index("y"), lax.axis_index("c")
    me = 4 * x + 2 * y + c
    n_pair = N_DEV - 1

    def dev(mask):
        return (1 - x if mask & 4 else x, 1 - y if mask & 2 else y, 1 - c if mask & 1 else c)

    def slot(w, mask):
        j = jnp.bitwise_xor(me, mask)
        return _window(outs[w], j, cols[w]) if cols[w] else outs[w].at[j]

    def remote(w, pair, src, to_slot, target):
        return pltpu.make_async_remote_copy(src_ref=src, dst_ref=slot(w, to_slot), send_sem=send_sems.at[w * n_pair + pair],
                                            recv_sem=recv_sems.at[w * n_pair + pair], device_id=dev(target),
                                            device_id_type=pl.DeviceIdType.MESH)

    ws = range(len(ins))
    return dict(
        local=[pltpu.make_async_copy(ins[w], slot(w, 0), loc_sems.at[w]) for w in ws],
        to_chips=[remote(w, 1 + t, ins[w], 0, m) for t, m in enumerate(OTHER_CHIPS) for w in ws],
        to_core=[remote(w, 0, ins[w], 0, 1) for w in ws],
        from_chips=[remote(w, 1 + t, ins[w], m, 0) for t, m in enumerate(OTHER_CHIPS) for w in ws],
        pass_on=[remote(w, 4 + t, slot(w, m), m, 1) for t, m in enumerate(OTHER_CHIPS) for w in ws],
        from_core=[remote(w, 0, ins[w], 1, 0) for w in ws]
        + [remote(w, 4 + t, ins[w], m + 1, 0) for t, m in enumerate(OTHER_CHIPS) for w in ws])


def _exchange_start(ins, outs, sems, gather, cols):
    if gather:
        cps = _gather_copies(ins, outs, sems, cols)
        for cp in cps["local"] + cps["to_chips"] + cps["to_core"]:
            cp.start()
    else:
        local, remote = _scatter_copies(ins, outs, sems, cols, False)
        for cp in local + remote:
            cp.start()


def _exchange_pass_on(ins, outs, sems, gather, cols, chips):
    if gather:
        cps = _gather_copies(ins, outs, sems, cols)
        n = len(ins)
        for t in chips:
            for arrived, onward in zip(cps["from_chips"][t * n:(t + 1) * n], cps["pass_on"][t * n:(t + 1) * n]):
                arrived.wait_recv()
                onward.start()


def _exchange_wait(ins, outs, sems, gather, cols):
    if gather:
        cps = _gather_copies(ins, outs, sems, cols)
        for cp in cps["local"]:
            cp.wait()
        for cp in cps["to_chips"] + cps["to_core"] + cps["pass_on"]:
            cp.wait_send()
        for cp in cps["from_core"]:
            cp.wait_recv()
    else:
        local, remote = _scatter_copies(ins, outs, sems, cols, True)
        for cp in local:
            cp.wait()
        for cp in remote:
            cp.wait_send()
            cp.wait_recv()


def _exchange_shapes(arrs, gather, cols):
    n = len(arrs)
    out_shape = []
    for a, c in zip(arrs, cols):
        if gather:
            shape = (a.shape[0], N_DEV * c) if c else (N_DEV,) + a.shape
        else:
            shape = (N_DEV, a.shape[0], c) if c else a.shape
        out_shape.append(jax.ShapeDtypeStruct(shape, a.dtype))
    sems = [pltpu.SemaphoreType.DMA((n * (N_DEV - 1),)), pltpu.SemaphoreType.DMA((n * (N_DEV - 1),)),
            pltpu.SemaphoreType.DMA((n,))]
    return out_shape, sems


def _call(body, *, name, grid, in_specs, out_specs, out_shape, scratch, sem, args, ride=None):
    if ride is None:
        outs = pl.pallas_call(body, name=name, grid=grid, in_specs=in_specs, out_specs=out_specs, out_shape=out_shape,
                              scratch_shapes=scratch, compiler_params=_params(sem))(*args)
        return outs, None
    arrs, gather, cols = ride
    n, n_in, n_out, n_scr = len(arrs), len(in_specs), len(out_specs), len(scratch)
    x_shape, x_sems = _exchange_shapes(arrs, gather, cols)

    def riding(*refs):
        ins, x_ins = refs[:n_in], refs[n_in:n_in + n]
        outs = refs[n_in + n:n_in + n + n_out]
        x_outs = refs[n_in + n + n_out:n_in + 2 * n + n_out]
        scr = refs[n_in + 2 * n + n_out:n_in + 2 * n + n_out + n_scr]
        sems = refs[n_in + 2 * n + n_out + n_scr:]
        def at(step):
            return functools.reduce(jnp.logical_and, [pl.program_id(a) == v for a, v in enumerate(step)])

        @pl.when(at((0,) * len(grid)))
        def _():
            _exchange_start(x_ins, x_outs, sems, gather, cols)

        @pl.when(at((grid[0] // 2,) + (0,) * (len(grid) - 2) + (grid[-1] // 2,)))
        def _():
            _exchange_pass_on(x_ins, x_outs, sems, gather, cols, (0, 1))

        @pl.when(at((grid[0] // 2,) + (0,) * (len(grid) - 2) + (3 * grid[-1] // 4,)))
        def _():
            _exchange_pass_on(x_ins, x_outs, sems, gather, cols, (2,))

        body(*ins, *outs, *scr)

        @pl.when(at(tuple(g - 1 for g in grid)))
        def _():
            _exchange_wait(x_ins, x_outs, sems, gather, cols)

    res = pl.pallas_call(
        riding, name=name, grid=grid, in_specs=list(in_specs) + [ANY] * n, out_specs=list(out_specs) + [ANY] * n,
        out_shape=list(out_shape) + x_shape, scratch_shapes=list(scratch) + x_sems,
        compiler_params=_params(("arbitrary",) * len(grid)))(*args, *arrs)
    return res[:n_out], res[n_out:]


def _my_block():
    return (4 * lax.axis_index("x") + 2 * lax.axis_index("y") + lax.axis_index("c")).astype(jnp.int32).reshape(1)


def _proj_in_gather(x, g, w_shard):
    t, k = x.shape
    cs = w_shard.shape[1]
    tm = min(MM_ROWS, t)
    ni = t // tm
    arrival = (0, 1) + OTHER_CHIPS + tuple(m + 1 for m in OTHER_CHIPS)

    def mask_at(s):
        return jnp.where(s < 2, s, jnp.where(s < 5, 2 * (s - 1), 2 * (s - 4) + 1))

    def body(me_ref, x_ref, g_ref, w_hbm, o_ref, all_hbm, n_hbm, w_vmem, n_vmem, send_sems, recv_sems, loc_sems,
             load_sems, n_sem):
        s, i = pl.program_id(0), pl.program_id(1)
        cps = _gather_copies([w_hbm], [all_hbm], (send_sems, recv_sems, loc_sems), (cs,))
        arrived = cps["local"] + cps["from_core"][:1] + cps["from_chips"] + cps["from_core"][1:]

        def load(step):
            src = w_hbm if step == 0 else _window(all_hbm, jnp.bitwise_xor(me_ref[0], arrival[step]), cs)
            return pltpu.make_async_copy(src, w_vmem.at[step % 2], load_sems.at[step % 2])

        @pl.when(jnp.logical_and(s == 0, i == 0))
        def _():
            for cp in cps["local"] + cps["to_chips"] + cps["to_core"]:
                cp.start()
            load(0).start()

        for step, mask in enumerate(arrival):
            @pl.when(jnp.logical_and(s == step, i == 0))
            def _(step=step):
                load(step).wait()

            if step + 1 < N_DEV:
                @pl.when(jnp.logical_and(s == step, i == min(1, ni - 1)))
                def _(step=step):
                    arrived[step + 1].wait_recv()
                    if arrival[step + 1] in OTHER_CHIPS:
                        cps["pass_on"][OTHER_CHIPS.index(arrival[step + 1])].start()
                    load(step + 1).start()

        @pl.when(s == 0)
        def _():
            xf = x_ref[...]
            r = lax.rsqrt(jnp.mean(xf * xf, axis=-1, keepdims=True) + RMS_EPS)
            n_vmem[i] = (xf * r * g_ref[...]).astype(BF16)
            keep = pltpu.make_async_copy(n_vmem.at[i], n_hbm.at[pl.ds(pl.multiple_of(i * tm, tm), tm), :], n_sem)
            keep.start()
            keep.wait()

        o_ref[...] = lax.dot_general(n_vmem[i], w_vmem[s % 2], NN, preferred_element_type=F32).astype(BF16)

        @pl.when(jnp.logical_and(s == N_DEV - 1, i == ni - 1))
        def _():
            cps["local"][0].wait()
            for cp in cps["to_chips"] + cps["to_core"] + cps["pass_on"]:
                cp.wait_send()

    return pl.pallas_call(
        body, name="proj_in",
        grid_spec=pltpu.PrefetchScalarGridSpec(
            num_scalar_prefetch=1, grid=(N_DEV, ni),
            in_specs=[pl.BlockSpec((tm, k), lambda s, i, me: (jnp.where(s == 0, i, 0), 0)),
                      pl.BlockSpec((1, k), lambda s, i, me: (0, 0)), ANY],
            out_specs=[pl.BlockSpec((tm, cs), lambda s, i, me: (i, jnp.bitwise_xor(me[0], mask_at(s)))), ANY, ANY],
            scratch_shapes=[pltpu.VMEM((2, k, cs), BF16), pltpu.VMEM((ni, tm, k), BF16),
                            pltpu.SemaphoreType.DMA((N_DEV - 1,)), pltpu.SemaphoreType.DMA((N_DEV - 1,)),
                            pltpu.SemaphoreType.DMA((1,)), pltpu.SemaphoreType.DMA((2,)), pltpu.SemaphoreType.DMA]),
        out_shape=[jax.ShapeDtypeStruct((t, N_DEV * cs), BF16), jax.ShapeDtypeStruct((k, N_DEV * cs), BF16),
                   jax.ShapeDtypeStruct((t, k), BF16)],
        compiler_params=_params(("arbitrary", "arbitrary")),
    )(_my_block(), x, g, w_shard)


def _gw_in_scatter(a, g):
    t, k = a.shape
    cs = g.shape[1] // N_DEV
    tm = min(MM_ROWS, t)
    nr = t // tm
    n_chip = N_DEV // 2
    chips = (6, 4, 2, 0)

    def body(me_ref, a_ref, g_ref, out_hbm, acc, stage, other, core_send, core_recv, chip_send, chip_recv, loc_sem):
        s, r = pl.program_id(0), pl.program_id(1)
        x, y, c = lax.axis_index("x"), lax.axis_index("y"), lax.axis_index("c")
        my_chip = 2 * x + y
        part = lax.dot_general(a_ref[...], g_ref[...], TN, preferred_element_type=F32)

        def to_core(m):
            return pltpu.make_async_remote_copy(src_ref=stage.at[0], dst_ref=other.at[m], send_sem=core_send.at[m],
                                                recv_sem=core_recv.at[m], device_id=(x, y, 1 - c),
                                                device_id_type=pl.DeviceIdType.MESH)

        def to_chip(m, landed):
            mask = chips[m]
            there = (1 - x if mask & 4 else x, 1 - y if mask & 2 else y, c)
            slot = (2 * there[0] + there[1]) if landed else my_chip
            return pltpu.make_async_remote_copy(src_ref=stage.at[1], dst_ref=out_hbm.at[slot], send_sem=chip_send.at[m],
                                                recv_sem=chip_recv.at[m], device_id=there,
                                                device_id_type=pl.DeviceIdType.MESH)

        local = pltpu.make_async_copy(stage.at[1], out_hbm.at[my_chip], loc_sem)

        @pl.when(r == 0)
        def _():
            acc[...] = part

        @pl.when(r > 0)
        def _():
            acc[...] += part

        for step in range(N_DEV):
            m = step // 2

            @pl.when(jnp.logical_and(s == step, r == nr - 1))
            def _(step=step, m=m):
                if step % 2 == 0:
                    if m > 0:
                        to_core(m - 1).wait_send()
                    stage[0] = acc[...].astype(BF16)
                    to_core(m).start()
                else:
                    if m > 0:
                        to_chip(m - 1, False).wait_send()
                    to_core(m).wait_recv()
                    stage[1] = (acc[...] + other[m].astype(F32)).astype(BF16)
                    if m < n_chip - 1:
                        to_chip(m, False).start()
                    else:
                        local.start()
                        to_core(m).wait_send()
                        local.wait()
                        for mm in range(n_chip - 1):
                            to_chip(mm, True).wait_recv()

    return pl.pallas_call(
        body, name="gw_in",
        grid_spec=pltpu.PrefetchScalarGridSpec(
            num_scalar_prefetch=1, grid=(N_DEV, nr),
            in_specs=[pl.BlockSpec((tm, k), lambda s, r, me: (r, 0)),
                      pl.BlockSpec((tm, cs), lambda s, r, me: (r, jnp.bitwise_xor(me[0], N_DEV - 1 - s)))],
            out_specs=ANY,
            scratch_shapes=[pltpu.VMEM((k, cs), F32), pltpu.VMEM((2, k, cs), BF16), pltpu.VMEM((n_chip, k, cs), BF16),
                            pltpu.SemaphoreType.DMA((n_chip,)), pltpu.SemaphoreType.DMA((n_chip,)),
                            pltpu.SemaphoreType.DMA((n_chip - 1,)), pltpu.SemaphoreType.DMA((n_chip - 1,)),
                            pltpu.SemaphoreType.DMA]),
        out_shape=jax.ShapeDtypeStruct((n_chip, k, cs), BF16),
        compiler_params=_params(("arbitrary", "arbitrary")),
    )(_my_block(), a, g)


SMALL_ROWS = 8


def _allreduce_small(parts, loss_part):
    n, d = len(parts), parts[0].shape[1]

    def body(*refs):
        part_refs, loss_ref, o_ref = refs[:n], refs[n], refs[n + 1]
        mine_ref, all_ref, send_sems, recv_sems = refs[n + 2:]
        me, peers = _peers()
        mine_ref[...] = jnp.zeros_like(mine_ref)
        for i, p_ref in enumerate(part_refs):
            mine_ref[i:i + 1, :] = p_ref[...]
        mine_ref[SMALL_ROWS - 1:SMALL_ROWS, 0:LANES] = loss_ref[0:1, :]
        all_ref[me] = mine_ref[...]
        for k, (dev, idx) in enumerate(peers):
            pltpu.make_async_remote_copy(src_ref=mine_ref, dst_ref=all_ref.at[me], send_sem=send_sems.at[k],
                                         recv_sem=recv_sems.at[k], device_id=dev,
                                         device_id_type=pl.DeviceIdType.MESH).start()
        for k, (dev, idx) in enumerate(peers):
            cp = pltpu.make_async_remote_copy(src_ref=mine_ref, dst_ref=all_ref.at[idx], send_sem=send_sems.at[k],
                                              recv_sem=recv_sems.at[k], device_id=dev,
                                              device_id_type=pl.DeviceIdType.MESH)
            cp.wait_send()
            cp.wait_recv()
        tot = all_ref[0]
        for dvc in range(1, N_DEV):
            tot = tot + all_ref[dvc]
        o_ref[...] = tot

    return pl.pallas_call(
        body, name="allreduce_small", in_specs=[VMEM] * (n + 1), out_specs=VMEM,
        out_shape=jax.ShapeDtypeStruct((SMALL_ROWS, d), F32),
        scratch_shapes=[pltpu.VMEM((SMALL_ROWS, d), F32), pltpu.VMEM((N_DEV, SMALL_ROWS, d), F32),
                        pltpu.SemaphoreType.DMA((N_DEV - 1,)), pltpu.SemaphoreType.DMA((N_DEV - 1,))],
    )(*parts, loss_part)


def _adam_math(g, w, m, v):
    m_new = ADAM_B1 * m + (1.0 - ADAM_B1) * g
    v_new = ADAM_B2 * v + (1.0 - ADAM_B2) * (g * g)
    m_hat = m_new / (1.0 - ADAM_B1 ** ADAM_STEP)
    v_hat = v_new / (1.0 - ADAM_B2 ** ADAM_STEP)
    delta = -ADAM_LR * (m_hat / (jnp.sqrt(v_hat) + ADAM_EPS) + ADAM_WD * w)
    return delta, m_new, v_new


def _adam(name, pieces, w, m, v):
    r, c = w.shape
    n_piece, _, cp = pieces.shape
    tr = r
    for cand in (256, 176, 128, 64):
        if r % cand == 0 and r > cand:
            tr = cand
            break

    def body(p_ref, w_ref, m_ref, v_ref, g_ref, d_ref, mo_ref, vo_ref):
        g = p_ref[0, :, 0:c].astype(F32)
        for j in range(1, n_piece):
            g = g + p_ref[j, :, 0:c].astype(F32)
        delta, m_new, v_new = _adam_math(g, w_ref[...], m_ref[...], v_ref[...])
        g_ref[...] = g
        d_ref[...] = delta
        mo_ref[...] = m_new
        vo_ref[...] = v_new

    blk = pl.BlockSpec((tr, c), lambda i: (i, 0))
    osh = jax.ShapeDtypeStruct((r, c), F32)
    return pl.pallas_call(
        body, name=name, grid=(r // tr,),
        in_specs=[pl.BlockSpec((n_piece, tr, cp), lambda i: (0, i, 0)), blk, blk, blk],
        out_specs=[blk, blk, blk, blk], out_shape=[osh, osh, osh, osh],
        compiler_params=_params(("parallel",)),
    )(pieces, w, m, v)


def _adam_small(g_all, ws, ms, vs):
    n = len(ws)

    def body(*refs):
        g_ref, ins, outs = refs[0], refs[1:1 + 3 * n], refs[1 + 3 * n:]
        for i in range(n):
            g = g_ref[i:i + 1, :]
            delta, m_new, v_new = _adam_math(g, ins[i][...], ins[n + i][...], ins[2 * n + i][...])
            for kind, val in enumerate((g, delta, m_new, v_new)):
                outs[kind * n + i][...] = val

    osh = jax.ShapeDtypeStruct(ws[0].shape, F32)
    res = pl.pallas_call(body, name="adam_small", in_specs=[VMEM] * (1 + 3 * n), out_specs=[VMEM] * (4 * n),
                         out_shape=[osh] * (4 * n))(g_all, *ws, *ms, *vs)
    return res[:n], res[n:2 * n], res[2 * n:3 * n], res[3 * n:]


def _local_step(x, mem, pos, tgt, gains, w_in_shard, shards, batch):
    g_mix, g_mem_q, g_mem_kv, g_ffn, g_final = gains
    t, d = x.shape
    s = t // batch
    n_mem = mem.shape[0] // batch
    n_sh = N_DEV
    width = shards[0].shape[0]
    nb = width // LANES

    lane = np.arange(LANES) % HEAD_DIM
    sel_lo = (lane < ROPE_HALF).astype(np.float32)[None, :]
    sel_hi = ((lane >= ROPE_HALF) & (lane < 2 * ROPE_HALF)).astype(np.float32)[None, :]
    freqs = np.float32(ROPE_THETA) ** (-np.arange(ROPE_HALF, dtype=np.float32) / np.float32(ROPE_HALF))
    inv_freq = np.where(lane < 2 * ROPE_HALF, freqs[lane % ROPE_HALF], 0.0).astype(np.float32)[None, :]
    cos_t, sin_a, sin_b = _rope_tables(pos, jnp.asarray(inv_freq), jnp.asarray(sel_lo), jnp.asarray(sel_hi))
    bias = _dilated_bias_tiles(s)

    proj, w_in, n1 = _proj_in_gather(x, g_mix, w_in_shard)
    qk_a = _rope_apply("rope_fwd", [proj], 2 * width, cos_t, sin_a, sin_b, 1.0)
    cs_up = shards[0].shape[1]
    (o_a, lse_a), (w_up_a, w_up_b, w_out, w_q, w_kv, w_o, w_fd) = _da_fwd(
        qk_a, proj, 2 * nb, bias, batch, s,
        ride=(shards[:6] + shards[8:], True, (cs_up, cs_up, 0, 0, 0, cs_up, 0)))
    (o_b, tot_b), (w_fg, w_fu) = _sb_fwd(proj, 3 * nb, 4 * nb, 5 * nb, batch, s, ride=(shards[6:8], True, (0, 0)))
    w_out = w_out.reshape(d, d)
    w_q = w_q.reshape(d, -1)
    w_kv = w_kv.reshape(d, -1)
    w_fd = w_fd.reshape(-1, d)
    w_fg = w_fg.reshape(-1, d)
    w_fu = w_fu.reshape(-1, d)
    ua, ub, mixed, n2, h1 = _mixer_fwd(o_a, o_b, w_up_a, w_up_b, proj, 6 * nb, w_out, x, g_mem_q)
    mem_n = _rms_fwd("norm_mem_kv", mem, g_mem_kv)
    q_m = _mm_w("mem_q", n2, w_q, BF16)
    kv_m = _mm_w("mem_kv", mem_n, w_kv, BF16)
    o_m = _mem_fwd(q_m, kv_m, batch, s, n_mem)
    h2, n3 = _mm_res_norm("mem_out", o_m, w_o, h1, g_ffn)
    hg, hu, act = _ffn_up(n3, w_fg, w_fu)
    loss_part, dh3, dh3_b, dg_final = _loss_head(act, w_fd, h2, tgt, g_final.reshape(1, d))

    dhg, dhu, dh2, dh2_b, dg_ffn = _ffn_bwd(dh3_b, w_fd, w_fg, w_fu, hg, hu, h2, g_ffn, dh3)
    gw_fd = _wgrad("gw_ffn_down", act, dh3_b)
    gw_fg = _wgrad("gw_ffn_gate", dhg, n3)
    gw_fu = _wgrad("gw_ffn_up", dhu, n3)

    do_m = _mm_w("mem_out_bwd", dh2_b, w_o, BF16, dims=NT)
    gw_o = _wgrad("gw_mem_o", o_m, dh2_b)
    dq_m, dkv_m = _mem_bwd(q_m, kv_m, do_m, batch, s, n_mem)
    gw_q = _wgrad("gw_mem_q", n2, dq_m)
    gw_kv = _wgrad("gw_mem_kv", mem_n, dkv_m)
    (dg_mem_kv,) = _rms_bwd("norm_mem_kv_bwd", (dkv_m, w_kv, NT), mem, g_mem_kv, None, ())
    dh1, dh1_b, dg_mem_q = _rms_bwd("norm_mem_q_bwd", (dq_m, w_q, NT), h1, g_mem_q, dh2, ("f32", "bf16"))

    gw_out = _wgrad("gw_out", mixed, dh1_b)
    dua, dub, dgates, do_a, do_b = _mixer_bwd(dh1_b, w_out, ua, ub, proj, 6 * nb, w_up_a, w_up_b)
    gw_ua = _wgrad("gw_up_a", o_a, dua)
    gw_ub = _wgrad("gw_up_b", o_b, dub)
    (dq_ar, dk_ar, dv_a), (p_fg, p_fd) = _da_bwd(
        qk_a, proj, 2 * nb, bias, o_a, lse_a, do_a, batch, s,
        ride=([gw_fg.reshape(n_sh, -1, d), gw_fd.reshape(n_sh, -1, d)], False, (0, 0)))
    dqk_a = _rope_apply("rope_bwd", [dq_ar, dk_ar], width, cos_t, sin_a, sin_b, -1.0)
    mid = [gw_ua, gw_ub, gw_out.reshape(n_sh, -1, d), gw_q.reshape(n_sh, -1, gw_q.shape[1]),
           gw_kv.reshape(n_sh, -1, gw_kv.shape[1]), gw_o, gw_fu.reshape(n_sh, -1, d)]
    (dq_b, dk_b, dv_b), (*p_mid, p_fu) = _sb_bwd(proj, 3 * nb, 4 * nb, 5 * nb, tot_b, do_b, batch, s,
                                                 ride=(mid, False, (cs_up, cs_up, 0, 0, 0, cs_up, 0)))
    p_ffn = [p_fg, p_fu, p_fd]
    dproj = jnp.concatenate([dqk_a, dv_a, dq_b, dk_b, dv_b, dgates], axis=1)
    grad_x, dg_mix = _rms_bwd("proj_in_bwd", (dproj, w_in, NT), x, g_mix, dh1, ("f32",))
    p_in = _gw_in_scatter(n1, dproj)
    return loss_part, grad_x, [p_in] + list(p_mid) + p_ffn, (dg_mix, dg_mem_q, dg_mem_kv, dg_ffn, dg_final)


WEIGHTS =("w_in", "w_up_a", "w_up_b", "w_out", "w_q_mem", "w_kv_mem", "w_o_mem", "w_ffn_gate", "w_ffn_up", "w_ffn_down")
GAINS = ("g_mix", "g_mem_q", "g_mem_kv", "g_ffn", "g_final")
ORDER = ("g_mix", "w_in", "w_up_a", "w_up_b", "w_out", "g_mem_q", "g_mem_kv", "w_q_mem", "w_kv_mem", "w_o_mem", "g_ffn",
         "w_ffn_gate", "w_ffn_up", "w_ffn_down", "g_final")


def kernel(x, mem, positions, g_mix, w_in, w_up_a, w_up_b, w_out, g_mem_q, g_mem_kv, w_q_mem, w_kv_mem, w_o_mem, g_ffn, w_ffn_gate, w_ffn_up, w_ffn_down, g_final, loss_target, m_g_mix, m_w_in, m_w_up_a, m_w_up_b, m_w_out, m_g_mem_q, m_g_mem_kv, m_w_q_mem, m_w_kv_mem, m_w_o_mem, m_g_ffn, m_w_ffn_gate, m_w_ffn_up, m_w_ffn_down, m_g_final, v_g_mix, v_w_in, v_w_up_a, v_w_up_b, v_w_out, v_g_mem_q, v_g_mem_kv, v_w_q_mem, v_w_kv_mem, v_w_o_mem, v_g_ffn, v_w_ffn_gate, v_w_ffn_up, v_w_ffn_down, v_g_final):
    given = dict(locals())
    batch, s, d = x.shape
    t = batch * s
    flipped = ("w_ffn_gate", "w_ffn_up")

    def view(a, n):
        a = a.reshape(a.shape[-2:])
        return a.T if n in flipped else a

    def unview(a, n):
        return (a.T if n in flipped else a).reshape(given[n].shape)

    shard = {n: view(given[n], n) for n in WEIGHTS}
    gains = [given[n].reshape(1, d) for n in GAINS]

    pad = (-shard["w_ffn_down"].shape[0]) % LANES
    cast = _cast_weights([shard[n] for n in WEIGHTS], [pad if n in flipped + ("w_ffn_down",) else 0 for n in WEIGHTS])
    loss_part, grad_x, pieces, dgains = _local_step(
        x.reshape(t, d), mem.reshape(-1, d), positions.reshape(t, 1), loss_target.reshape(t, d), gains, cast[0],
        cast[1:], batch)

    grad, delta, new_m, new_v = {}, {}, {}, {}
    for n, p in zip(WEIGHTS, pieces):
        outs = _adam("adam_" + n, p, shard[n], view(given["m_" + n], n), view(given["v_" + n], n))
        grad[n], delta[n], new_m[n], new_v[n] = [unview(o, n) for o in outs]

    g_all = _allreduce_small(list(dgains), loss_part)
    small = _adam_small(g_all, gains, [given["m_" + n].reshape(1, d) for n in GAINS],
                        [given["v_" + n].reshape(1, d) for n in GAINS])
    for out, vals in zip((grad, delta, new_m, new_v), small):
        for n, val in zip(GAINS, vals):
            out[n] = val.reshape(given[n].shape)

    loss = g_all[SMALL_ROWS - 1, 0]
    return (loss, grad_x.reshape(x.shape), *[grad[n] for n in ORDER], *[delta[n] for n in ORDER],
            *[new_m[n] for n in ORDER], *[new_v[n] for n in ORDER])
```

```python
import functools
import math

import jax
import jax.numpy as jnp
import numpy as np
from jax import lax
from jax.experimental import pallas as pl
from jax.experimental.pallas import tpu as pltpu

F32 = jnp.float32
BF16 = jnp.bfloat16

N_DEV = 8
HEAD_DIM = 64
MEM_HEAD_DIM = 128
N_HEADS_MEM = 4
BLOCK = 128
DIL_PATTERNS = ((128, 1), (512, 4), (2048, 16))
ROPE_THETA = 500000.0
ROPE_HALF = 8
RMS_EPS = 1e-6
ADAM_LR, ADAM_B1, ADAM_B2, ADAM_EPS, ADAM_WD, ADAM_STEP = 0.001, 0.9, 0.999, 1e-08, 0.01, 10
NEG = -1e30
ROW_TILE = 512
LANES = 128

ANY = pl.BlockSpec(memory_space=pl.ANY)
VMEM = pl.BlockSpec(memory_space=pltpu.VMEM)
NN = (((1,), (0,)), ((), ()))
NT = (((1,), (1,)), ((), ()))
TN = (((0,), (0,)), ((), ()))


def _params(sem):
    return pltpu.CompilerParams(dimension_semantics=sem)


def _mm(name, a, b, *, grid, a_spec, b_spec, o_shape, o_spec, dims, out_dtype, nk=1):
    def body(*refs):
        a_ref, b_ref, o_ref = refs[0], refs[1], refs[2]
        p = lax.dot_general(a_ref[...], b_ref[...], dims, preferred_element_type=F32)
        if nk == 1:
            o_ref[...] = p.astype(out_dtype)
            return
        acc_ref = refs[-1]
        k = pl.program_id(len(grid) - 1)

        @pl.when(k == 0)
        def _():
            acc_ref[...] = p

        @pl.when(k > 0)
        def _():
            acc_ref[...] += p

        @pl.when(k == nk - 1)
        def _():
            o_ref[...] = acc_ref[...].astype(out_dtype)

    o_block = tuple(d for d in o_spec.block_shape if d is not None)
    sem = ("parallel",) * (len(grid) - 1) + (("arbitrary",) if nk > 1 else ("parallel",))
    return pl.pallas_call(
        body, name=name, grid=grid, in_specs=[a_spec, b_spec],
        out_specs=o_spec, out_shape=jax.ShapeDtypeStruct(o_shape, out_dtype),
        scratch_shapes=[pltpu.VMEM(o_block, F32)] if nk > 1 else [],
        compiler_params=_params(sem),
    )(a, b)


def _rms_fwd(name, x, g):
    t, d = x.shape
    tm = min(ROW_TILE, t)

    def body(x_ref, g_ref, o_ref):
        xf = x_ref[...]
        r = lax.rsqrt(jnp.mean(xf * xf, axis=-1, keepdims=True) + RMS_EPS)
        o_ref[...] = (xf * r * g_ref[...]).astype(BF16)

    return pl.pallas_call(
        body, name=name, grid=(t // tm,),
        in_specs=[pl.BlockSpec((tm, d), lambda i: (i, 0)), pl.BlockSpec((1, d), lambda i: (0, 0))],
        out_specs=pl.BlockSpec((tm, d), lambda i: (i, 0)), out_shape=jax.ShapeDtypeStruct((t, d), BF16),
        compiler_params=_params(("parallel",)),
    )(x, g)


def _rms_bwd_rows(dnf, xf, gv, res):
    r = lax.rsqrt(jnp.mean(xf * xf, axis=-1, keepdims=True) + RMS_EPS)
    xh = xf * r
    dxh = dnf * gv
    dx = r * (dxh - xh * jnp.mean(dxh * xh, axis=-1, keepdims=True))
    if res is not None:
        dx = dx + res
    return dx, jnp.sum(dnf * xh, axis=0, keepdims=True)


def _rms_bwd(name, dn, x, g, dres, want):
    t, d = x.shape
    tm = min(ROW_TILE, t)
    has_res = dres is not None
    lhs = list(dn) if isinstance(dn, tuple) else [dn]
    n_lhs = len(lhs[:2])

    def body(*refs):
        x_ref, g_ref = refs[n_lhs], refs[n_lhs + 1]
        r_ref = refs[n_lhs + 2] if has_res else None
        dx_refs, dg_ref = refs[-1 - len(want):-1], refs[-1]
        if n_lhs == 2:
            dnf = lax.dot_general(refs[0][...], refs[1][...], lhs[2], preferred_element_type=F32)
        else:
            dnf = refs[0][...].astype(F32)
        dx, dg = _rms_bwd_rows(dnf, x_ref[...], g_ref[...], r_ref[...] if has_res else None)
        for kind, dx_ref in zip(want, dx_refs):
            dx_ref[...] = dx.astype(F32 if kind == "f32" else BF16)

        @pl.when(pl.program_id(0) == 0)
        def _():
            dg_ref[...] = jnp.zeros_like(dg_ref)

        dg_ref[...] += dg

    row = pl.BlockSpec((tm, d), lambda i: (i, 0))
    vec = pl.BlockSpec((1, d), lambda i: (0, 0))
    if n_lhs == 2:
        first = [pl.BlockSpec((tm, lhs[0].shape[1]), lambda i: (i, 0)), pl.BlockSpec(lhs[1].shape, lambda i: (0, 0))]
    else:
        first = [row]
    return pl.pallas_call(
        body, name=name, grid=(t // tm,),
        in_specs=first + [row, vec] + ([row] if has_res else []),
        out_specs=[row] * len(want) + [vec],
        out_shape=[jax.ShapeDtypeStruct((t, d), F32 if kind == "f32" else BF16) for kind in want]
        + [jax.ShapeDtypeStruct((1, d), F32)],
        compiler_params=_params(("arbitrary",)),
    )(*(lhs[:2] + [x, g] + ([dres] if has_res else [])))


def _loss_head(a, w, res, tgt, g):
    t, d = res.shape
    k = a.shape[1]
    tm = min(ROW_TILE, t)

    def body(a_ref, w_ref, r_ref, t_ref, g_ref, loss_ref, dh_ref, dhb_ref, dg_ref):
        xf = lax.dot_general(a_ref[...], w_ref[...], NN, preferred_element_type=F32) + r_ref[...]
        gv = g_ref[...]
        r = lax.rsqrt(jnp.mean(xf * xf, axis=-1, keepdims=True) + RMS_EPS)
        xh = xf * r
        e = xh * gv - t_ref[...]
        dy = e * (1.0 / d)
        dxh = dy * gv
        dh = r * (dxh - xh * jnp.mean(dxh * xh, axis=-1, keepdims=True))
        dh_ref[...] = dh
        dhb_ref[...] = dh.astype(BF16)

        @pl.when(pl.program_id(0) == 0)
        def _():
            dg_ref[...] = jnp.zeros_like(dg_ref)
            loss_ref[...] = jnp.zeros_like(loss_ref)

        dg_ref[...] += jnp.sum(dy * xh, axis=0, keepdims=True)
        part = jnp.sum(jnp.sum(e * e, axis=1, keepdims=True), axis=0, keepdims=True) * (0.5 / d)
        loss_ref[...] += jnp.broadcast_to(part, loss_ref.shape)

    row = pl.BlockSpec((tm, d), lambda i: (i, 0))
    vec = pl.BlockSpec((1, d), lambda i: (0, 0))
    return pl.pallas_call(
        body, name="loss_head", grid=(t // tm,),
        in_specs=[pl.BlockSpec((tm, k), lambda i: (i, 0)), pl.BlockSpec((k, d), lambda i: (0, 0)), row, row, vec],
        out_specs=[pl.BlockSpec((8, LANES), lambda i: (0, 0)), row, row, vec],
        out_shape=[jax.ShapeDtypeStruct((8, LANES), F32), jax.ShapeDtypeStruct((t, d), F32),
                   jax.ShapeDtypeStruct((t, d), BF16), jax.ShapeDtypeStruct((1, d), F32)],
        compiler_params=_params(("arbitrary",)),
    )(a, w, res, tgt, g)


def _rope_tables(pos, inv_freq, sel_lo, sel_hi):
    t = pos.shape[0]
    tm = min(ROW_TILE, t)

    def body(p_ref, f_ref, lo_ref, hi_ref, c_ref, sa_ref, sb_ref):
        ang = p_ref[...].astype(F32) * f_ref[...]
        rot = lo_ref[...] + hi_ref[...]
        cs, sn = jnp.cos(ang), jnp.sin(ang)
        c_ref[...] = cs * rot + (1.0 - rot)
        sa_ref[...] = -sn * lo_ref[...]
        sb_ref[...] = sn * hi_ref[...]

    vec = pl.BlockSpec((1, LANES), lambda i: (0, 0))
    row = pl.BlockSpec((tm, LANES), lambda i: (i, 0))
    return pl.pallas_call(
        body, name="rope_tables", grid=(t // tm,),
        in_specs=[pl.BlockSpec((tm, 1), lambda i: (i, 0)), vec, vec, vec],
        out_specs=[row, row, row], out_shape=[jax.ShapeDtypeStruct((t, LANES), F32)] * 3,
        compiler_params=_params(("parallel",)),
    )(pos, inv_freq, sel_lo, sel_hi)


def _rope_apply(name, srcs, width, cos_t, sin_a, sin_b, sign):
    t = srcs[0].shape[0]
    tm = min(ROW_TILE, t)
    n_cols = width // LANES

    def body(*refs):
        x_refs, (c_ref, sa_ref, sb_ref, o_ref) = refs[:len(srcs)], refs[len(srcs):]
        cs, sa, sb = c_ref[...], sign * sa_ref[...], sign * sb_ref[...]
        for a, x_ref in enumerate(x_refs):
            for c in range(n_cols):
                xf = x_ref[:, c * LANES:(c + 1) * LANES].astype(F32)
                up = pltpu.roll(xf, LANES - ROPE_HALF, 1)
                dn = pltpu.roll(xf, ROPE_HALF, 1)
                o_ref[:, a * width + c * LANES:a * width + (c + 1) * LANES] = (xf * cs + up * sa + dn * sb).astype(BF16)

    wide = len(srcs) * width
    tab = pl.BlockSpec((tm, LANES), lambda i: (i, 0))
    return pl.pallas_call(
        body, name=name, grid=(t // tm,),
        in_specs=[pl.BlockSpec((tm, width), lambda i: (i, 0))] * len(srcs) + [tab, tab, tab],
        out_specs=pl.BlockSpec((tm, wide), lambda i: (i, 0)),
        out_shape=jax.ShapeDtypeStruct((t, wide), BF16),
        compiler_params=_params(("parallel",)),
    )(*srcs, cos_t, sin_a, sin_b)


DA_T = 256
MIX_STREAMS = 4
SB_BWD_STREAMS = 2


def _lane_lo():
    return lax.broadcasted_iota(jnp.int32, (BLOCK, LANES), 1) < HEAD_DIM


def _dilated_bias_tiles(s):
    n = s // DA_T
    dist = (np.arange(n)[:, None, None] * DA_T + np.arange(DA_T)[None, :, None] - np.arange(DA_T)[None, None, :])
    cnt = np.zeros(dist.shape, np.float32)
    for window, dil in DIL_PATTERNS:
        cnt += ((dist >= 0) & (dist % dil == 0) & (dist <= window)).astype(np.float32)
    return jnp.asarray(np.where(cnt > 0, np.log(np.maximum(cnt, 1.0)), NEG).astype(np.float32))


def _stack_heads(x, lo):
    zero = jnp.zeros_like(x)
    return jnp.concatenate([jnp.where(lo, x, zero), jnp.where(lo, zero, x)], axis=0)


def _da_fwd(qk, proj, v_col0, bias, batch, s, ride=None, streams=MIX_STREAMS):
    t = qk.shape[0]
    nq = s // DA_T
    n_pairs = 4
    ns = streams
    wide = ns * LANES
    scale = HEAD_DIM ** -0.5

    def body(q_ref, k_ref, v_ref, b_ref, o_ref, lse_ref, acc_ref, m_ref, l_ref):
        i = pl.program_id(2)
        lo = lax.broadcasted_iota(jnp.int32, (DA_T, LANES), 1) < HEAD_DIM
        ones = jnp.ones((DA_T, LANES), BF16)
        acc_ref[...] = jnp.zeros_like(acc_ref)
        m_ref[...] = jnp.full(m_ref.shape, NEG, F32)
        l_ref[...] = jnp.zeros_like(l_ref)
        qqs = [_stack_heads(q_ref[:, st * LANES:(st + 1) * LANES] * scale, lo) for st in range(ns)]

        def scores(st, rows, bias2):
            k = k_ref[rows, st * LANES:(st + 1) * LANES]
            return lax.dot_general(qqs[st], k, NT, preferred_element_type=F32) + bias2

        def softmax(st, sc):
            m_old = m_ref[st]
            m_new = jnp.maximum(m_old, jnp.max(sc, axis=1, keepdims=True))
            m_ref[st] = m_new
            return jnp.exp(sc - m_new).astype(BF16), jnp.exp(m_old - m_new)

        def values(st, rows, p, alpha):
            v = v_ref[rows, st * LANES:(st + 1) * LANES]
            vz = jnp.zeros_like(v)
            l_ref[st] = alpha * l_ref[st] + lax.dot_general(p, ones, NN, preferred_element_type=F32)
            pv = (lax.dot_general(p[:DA_T], jnp.where(lo, v, vz), NN, preferred_element_type=F32)
                  + lax.dot_general(p[DA_T:], jnp.where(lo, vz, v), NN, preferred_element_type=F32))
            acc_ref[st] = acc_ref[st] * jnp.where(lo, alpha[:DA_T], alpha[DA_T:]) + pv

        def trip(dlt, carry):
            rows = pl.ds(pl.multiple_of((i - dlt) * DA_T, DA_T), DA_T)
            bias_t = b_ref[dlt]
            bias2 = jnp.concatenate([bias_t, bias_t], axis=0)
            scs = [scores(st, rows, bias2) for st in range(ns)]
            pas = [softmax(st, scs[st]) for st in range(ns)]
            for st in range(ns):
                values(st, rows, *pas[st])
            return carry

        lax.fori_loop(0, i + 1, trip, 0)
        for st in range(ns):
            cols = slice(st * LANES, (st + 1) * LANES)
            l_t = l_ref[st]
            o_ref[:, cols] = (acc_ref[st] / jnp.where(lo, l_t[:DA_T], l_t[DA_T:])).astype(BF16)
            lse = m_ref[st] + jnp.log(l_t)
            lse_ref[:, cols] = jnp.where(lo, lse[:DA_T], lse[DA_T:])

    blk = pl.BlockSpec((DA_T, wide), lambda b, h, i: (b * nq + i, h))
    return _call(
        body, name="attn_a_fwd", grid=(batch, n_pairs // ns, nq),
        in_specs=[blk,
                  pl.BlockSpec((s, wide), lambda b, h, i: (b, n_pairs // ns + h)),
                  pl.BlockSpec((s, wide), lambda b, h, i: (b, v_col0 // ns + h)),
                  pl.BlockSpec((nq, DA_T, DA_T), lambda b, h, i: (0, 0, 0))],
        out_specs=[blk, blk],
        out_shape=[jax.ShapeDtypeStruct((t, n_pairs * LANES), BF16), jax.ShapeDtypeStruct((t, n_pairs * LANES), F32)],
        scratch=[pltpu.VMEM((ns, DA_T, LANES), F32), pltpu.VMEM((ns, 2 * DA_T, 1), F32),
                 pltpu.VMEM((ns, 2 * DA_T, LANES), F32)],
        sem=("parallel", "parallel", "arbitrary"), args=(qk, qk, proj, bias), ride=ride)


def _da_bwd(qk, proj, v_col0, bias, o, lse, do, batch, s, ride=None, streams=MIX_STREAMS):
    t = qk.shape[0]
    nq = s // DA_T
    n_pairs = 4
    ns = streams
    wide = ns * LANES
    scale = HEAD_DIM ** -0.5

    def body(q_ref, k_ref, v_ref, b_ref, o_ref, lse_ref, do_ref, dq_ref, dk_ref, dv_ref, dk_acc, dv_acc, dq_acc):
        i = pl.program_id(2)
        lo = lax.broadcasted_iota(jnp.int32, (DA_T, LANES), 1) < HEAD_DIM

        @pl.when(i == 0)
        def _():
            dk_acc[...] = jnp.zeros_like(dk_acc)
            dv_acc[...] = jnp.zeros_like(dv_acc)

        dq_acc[...] = jnp.zeros_like(dq_acc)
        qqs, dds, deltas, lses = [], [], [], []
        for st in range(ns):
            cols = slice(st * LANES, (st + 1) * LANES)
            do_ = do_ref[:, cols]
            qqs.append(_stack_heads(q_ref[:, cols] * scale, lo))
            dds.append(_stack_heads(do_, lo))
            prod = do_.astype(F32) * o_ref[:, cols].astype(F32)
            fz = jnp.zeros_like(prod)
            deltas.append(jnp.concatenate([jnp.sum(jnp.where(lo, prod, fz), axis=1, keepdims=True),
                                           jnp.sum(jnp.where(lo, fz, prod), axis=1, keepdims=True)], axis=0))
            lse_t = lse_ref[:, cols]
            lses.append(jnp.concatenate([lse_t[:, 0:1], lse_t[:, HEAD_DIM:HEAD_DIM + 1]], axis=0))

        def products(st, rows, bias2):
            cols = slice(st * LANES, (st + 1) * LANES)
            sc = lax.dot_general(qqs[st], k_ref[rows, cols], NT, preferred_element_type=F32) + bias2
            return sc, lax.dot_general(dds[st], v_ref[rows, cols], NT, preferred_element_type=F32)

        def weights(st, sc, dp):
            p = jnp.exp(sc - lses[st])
            return (p * (dp - deltas[st])).astype(BF16), p.astype(BF16)

        def gradients(st, rows, ds, p):
            cols = slice(st * LANES, (st + 1) * LANES)
            k = k_ref[rows, cols]
            kz = jnp.zeros_like(k)
            dq_acc[st] += (lax.dot_general(ds[:DA_T], jnp.where(lo, k, kz), NN, preferred_element_type=F32)
                           + lax.dot_general(ds[DA_T:], jnp.where(lo, kz, k), NN, preferred_element_type=F32))
            dk_acc[rows, cols] += lax.dot_general(ds, qqs[st], TN, preferred_element_type=F32)
            dv_acc[rows, cols] += lax.dot_general(p, dds[st], TN, preferred_element_type=F32)

        def trip(dlt, carry):
            rows = pl.ds(pl.multiple_of((i - dlt) * DA_T, DA_T), DA_T)
            bias_t = b_ref[dlt]
            bias2 = jnp.concatenate([bias_t, bias_t], axis=0)
            prods = [products(st, rows, bias2) for st in range(ns)]
            wts = [weights(st, *prods[st]) for st in range(ns)]
            for st in range(ns):
                gradients(st, rows, *wts[st])
            return carry

        lax.fori_loop(0, i + 1, trip, 0)
        for st in range(ns):
            dq_ref[:, st * LANES:(st + 1) * LANES] = (dq_acc[st] * scale).astype(BF16)

        @pl.when(i == nq - 1)
        def _():
            dk_ref[...] = dk_acc[...].astype(BF16)
            dv_ref[...] = dv_acc[...].astype(BF16)

    blk = pl.BlockSpec((DA_T, wide), lambda b, h, i: (b * nq + i, h))
    seq = pl.BlockSpec((s, wide), lambda b, h, i: (b, h), pipeline_mode=pl.Buffered(1))
    one = pl.Buffered(1)
    out = jax.ShapeDtypeStruct((t, n_pairs * LANES), BF16)
    return _call(
        body, name="attn_a_bwd", grid=(batch, n_pairs // ns, nq),
        in_specs=[blk,
                  pl.BlockSpec((s, wide), lambda b, h, i: (b, n_pairs // ns + h), pipeline_mode=one),
                  pl.BlockSpec((s, wide), lambda b, h, i: (b, v_col0 // ns + h), pipeline_mode=one),
                  pl.BlockSpec((nq, DA_T, DA_T), lambda b, h, i: (0, 0, 0), pipeline_mode=one),
                  blk, blk, blk],
        out_specs=[blk, seq, seq], out_shape=[out, out, out],
        scratch=[pltpu.VMEM((s, wide), F32), pltpu.VMEM((s, wide), F32), pltpu.VMEM((ns, DA_T, LANES), F32)],
        sem=("parallel", "parallel", "arbitrary"), args=(qk, qk, proj, bias, o, lse, do), ride=ride)


SB_Q = 256


def _sb_consts(after):
    r = lax.broadcasted_iota(jnp.int32, (2 * BLOCK, 2 * BLOCK), 0) % BLOCK
    c = lax.broadcasted_iota(jnp.int32, (2 * BLOCK, 2 * BLOCK), 1)
    tri = (r > c) if after else (r < c)
    return jnp.logical_or(c >= BLOCK, tri).astype(BF16)


def _split(x):
    hi = x.astype(BF16)
    lo = (x - hi.astype(F32)).astype(BF16)
    return jnp.concatenate([hi, lo], axis=1)


def _sb_fwd(proj, q_col0, k_col0, v_col0, batch, s, ride=None, streams=MIX_STREAMS):
    t = proj.shape[0]
    nq = s // SB_Q
    n_pairs = 4
    ns = streams
    wide = ns * LANES
    scale = HEAD_DIM ** -0.5

    def body(q_ref, k_ref, v_ref, o_ref, tot_ref, acc_ref, run_ref):
        i = pl.program_id(2)
        lo_q = lax.broadcasted_iota(jnp.int32, (SB_Q, LANES), 1) < HEAD_DIM
        lo_k = _lane_lo()
        mat = _sb_consts(True)
        row = lax.broadcasted_iota(jnp.int32, (2 * SB_Q, LANES), 0) % SB_Q
        ahead = row - lax.broadcasted_iota(jnp.int32, (2 * SB_Q, LANES), 1)
        acc_ref[...] = jnp.zeros_like(acc_ref)
        run_ref[...] = jnp.zeros_like(run_ref)
        qqs = [_stack_heads(q_ref[:, st * LANES:(st + 1) * LANES] * scale, lo_q) for st in range(ns)]

        def units(todo):
            def rows(j):
                return pl.ds(pl.multiple_of(j * BLOCK, BLOCK), BLOCK)

            zs = [lax.dot_general(qqs[st], k_ref[rows(j), st * LANES:(st + 1) * LANES], NT, preferred_element_type=F32)
                  for st, j, _ in todo]
            logs = []
            for z, (_, _, off) in zip(zs, todo):
                lsig = jnp.minimum(z, 0.0) - jnp.log(1.0 + jnp.exp(-jnp.abs(z)))
                lneg = lsig - z
                if off is not None:
                    lneg = jnp.where(ahead > off, lneg, 0.0)
                logs.append((lsig, _split(lneg)))
            sums = [lax.dot_general(cat, mat, NN, preferred_element_type=F32) for _, cat in logs]
            probs = []
            for (lsig, _), sm, (st, _, off) in zip(logs, sums, todo):
                run = run_ref[st]
                a = jnp.exp(lsig + run + sm[:, :BLOCK])
                if off is not None:
                    a = jnp.where(ahead > off, a, 0.0)
                run_ref[st] = run + sm[:, BLOCK:]
                probs.append(a.astype(BF16))
            for ab, (st, j, _) in zip(probs, todo):
                v = v_ref[rows(j), st * LANES:(st + 1) * LANES]
                vz = jnp.zeros_like(v)
                acc_ref[st] += (lax.dot_general(ab[:SB_Q], jnp.where(lo_k, v, vz), NN, preferred_element_type=F32)
                                + lax.dot_general(ab[SB_Q:], jnp.where(lo_k, vz, v), NN, preferred_element_type=F32))

        units([(st, 2 * i + 1, BLOCK) for st in range(ns)] + [(st, 2 * i, 0) for st in range(ns)])

        def pair(p, carry):
            jp = i - 1 - p
            units([(st, 2 * jp + 1, None) for st in range(ns)] + [(st, 2 * jp, None) for st in range(ns)])
            return carry

        lax.fori_loop(0, i, pair, 0)
        for st in range(ns):
            cols = slice(st * LANES, (st + 1) * LANES)
            o_ref[:, cols] = acc_ref[st].astype(BF16)
            tot_ref[:, cols] = jnp.where(lo_q, run_ref[st, 0:SB_Q, :], run_ref[st, SB_Q:2 * SB_Q, :])

    def seq(col0):
        return pl.BlockSpec((s, wide), lambda b, h, i: (b, col0 // ns + h))

    blk = pl.BlockSpec((SB_Q, wide), lambda b, h, i: (b * nq + i, h))
    return _call(
        body, name="attn_b_fwd", grid=(batch, n_pairs // ns, nq),
        in_specs=[pl.BlockSpec((SB_Q, wide), lambda b, h, i: (b * nq + i, q_col0 // ns + h)), seq(k_col0), seq(v_col0)],
        out_specs=[blk, blk],
        out_shape=[jax.ShapeDtypeStruct((t, n_pairs * LANES), BF16), jax.ShapeDtypeStruct((t, n_pairs * LANES), F32)],
        scratch=[pltpu.VMEM((ns, SB_Q, LANES), F32), pltpu.VMEM((ns, 2 * SB_Q, LANES), F32)],
        sem=("parallel", "parallel", "arbitrary"), args=(proj, proj, proj), ride=ride)


def _sb_bwd(proj, q_col0, k_col0, v_col0, tot, do, batch, s, ride=None, streams=SB_BWD_STREAMS):
    t = proj.shape[0]
    nq = s // SB_Q
    n_pairs = 4
    ns = streams
    wide = ns * LANES
    scale = HEAD_DIM ** -0.5

    def body(q_ref, k_ref, v_ref, tot_ref, do_ref, dq_ref, dk_ref, dv_ref, dk_acc, dv_acc, dq_acc, seen_ref, gsum_ref):
        i = pl.program_id(2)
        lo_q = lax.broadcasted_iota(jnp.int32, (SB_Q, LANES), 1) < HEAD_DIM
        lo_k = _lane_lo()

        @pl.when(i == 0)
        def _():
            dk_acc[...] = jnp.zeros_like(dk_acc)
            dv_acc[...] = jnp.zeros_like(dv_acc)

        mat_after = _sb_consts(True)
        mat_before = _sb_consts(False)[:BLOCK]
        row = lax.broadcasted_iota(jnp.int32, (2 * SB_Q, LANES), 0) % SB_Q
        ahead = row - lax.broadcasted_iota(jnp.int32, (2 * SB_Q, LANES), 1)
        dq_acc[...] = jnp.zeros_like(dq_acc)
        seen_ref[...] = jnp.zeros_like(seen_ref)
        gsum_ref[...] = jnp.zeros_like(gsum_ref)
        qqs, dds, totals = [], [], []
        for st in range(ns):
            cols = slice(st * LANES, (st + 1) * LANES)
            qqs.append(_stack_heads(q_ref[:, cols] * scale, lo_q))
            dds.append(_stack_heads(do_ref[:, cols], lo_q))
            tot_t = tot_ref[:, cols]
            totals.append(jnp.concatenate([jnp.broadcast_to(tot_t[:, 0:1], (SB_Q, LANES)),
                                           jnp.broadcast_to(tot_t[:, HEAD_DIM:HEAD_DIM + 1], (SB_Q, LANES))], axis=0))

        def units(todo):
            def rows(j):
                return pl.ds(pl.multiple_of(j * BLOCK, BLOCK), BLOCK)

            def cols(st):
                return slice(st * LANES, (st + 1) * LANES)

            prods = [(lax.dot_general(qqs[st], k_ref[rows(j), cols(st)], NT, preferred_element_type=F32),
                      lax.dot_general(dds[st], v_ref[rows(j), cols(st)], NT, preferred_element_type=F32))
                     for st, j, _ in todo]
            logs = []
            for (z, _), (_, _, off) in zip(prods, todo):
                lsig = jnp.minimum(z, 0.0) - jnp.log(1.0 + jnp.exp(-jnp.abs(z)))
                lneg = lsig - z
                if off is not None:
                    lneg = jnp.where(ahead > off, lneg, 0.0)
                logs.append((lsig, _split(lneg)))
            sums = [lax.dot_general(cat, mat_after, NN, preferred_element_type=F32) for _, cat in logs]
            gates = []
            for (lsig, _), sm, (_, da), (st, _, off) in zip(logs, sums, prods, todo):
                seen = seen_ref[st]
                a = jnp.exp(lsig + (totals[st] - seen - sm[:, BLOCK:]) + sm[:, :BLOCK])
                if off is not None:
                    a = jnp.where(ahead > off, a, 0.0)
                seen_ref[st] = seen + sm[:, BLOCK:]
                g = a * da
                gates.append((a.astype(BF16), g, g.astype(BF16)))
            gsums = [lax.dot_general(cat, mat_before, NN, preferred_element_type=F32) for _, _, cat in gates]
            outs = []
            for (lsig, _), (ab, g, _), gs, (st, _, off) in zip(logs, gates, gsums, todo):
                gsum = gsum_ref[st]
                dz = g - jnp.exp(lsig) * (g + gsum + gs[:, :BLOCK])
                if off is not None:
                    dz = jnp.where(ahead > off, dz, 0.0)
                gsum_ref[st] = gsum + gs[:, BLOCK:]
                outs.append((dz.astype(BF16), ab))
            for (dzb, ab), (st, j, _) in zip(outs, todo):
                k = k_ref[rows(j), cols(st)]
                kz = jnp.zeros_like(k)
                dq_acc[st] += (lax.dot_general(dzb[:SB_Q], jnp.where(lo_k, k, kz), NN, preferred_element_type=F32)
                               + lax.dot_general(dzb[SB_Q:], jnp.where(lo_k, kz, k), NN, preferred_element_type=F32))
                dk_acc[rows(j), cols(st)] += lax.dot_general(dzb, qqs[st], TN, preferred_element_type=F32)
                dv_acc[rows(j), cols(st)] += lax.dot_general(ab, dds[st], TN, preferred_element_type=F32)

        def pair(p, carry):
            units([(st, 2 * p, None) for st in range(ns)] + [(st, 2 * p + 1, None) for st in range(ns)])
            return carry

        lax.fori_loop(0, i, pair, 0)
        units([(st, 2 * i, 0) for st in range(ns)] + [(st, 2 * i + 1, BLOCK) for st in range(ns)])
        for st in range(ns):
            dq_ref[:, st * LANES:(st + 1) * LANES] = (dq_acc[st] * scale).astype(BF16)

        @pl.when(i == nq - 1)
        def _():
            dk_ref[...] = dk_acc[...].astype(BF16)
            dv_ref[...] = dv_acc[...].astype(BF16)

    def seq_in(col0):
        return pl.BlockSpec((s, wide), lambda b, h, i: (b, col0 // ns + h))

    blk = pl.BlockSpec((SB_Q, wide), lambda b, h, i: (b * nq + i, h))
    seq = pl.BlockSpec((s, wide), lambda b, h, i: (b, h))
    out = jax.ShapeDtypeStruct((t, n_pairs * LANES), BF16)
    return _call(
        body, name="attn_b_bwd", grid=(batch, n_pairs // ns, nq),
        in_specs=[pl.BlockSpec((SB_Q, wide), lambda b, h, i: (b * nq + i, q_col0 // ns + h)), seq_in(k_col0),
                  seq_in(v_col0), blk, blk],
        out_specs=[blk, seq, seq], out_shape=[out, out, out],
        scratch=[pltpu.VMEM((s, wide), F32), pltpu.VMEM((s, wide), F32), pltpu.VMEM((ns, SB_Q, LANES), F32),
                 pltpu.VMEM((ns, 2 * SB_Q, LANES), F32), pltpu.VMEM((ns, 2 * SB_Q, LANES), F32)],
        sem=("parallel", "parallel", "arbitrary"), args=(proj, proj, proj, tot, do), ride=ride)


MEM_Q_TILE = 512


def _mem_fwd(q, kv, batch, s, n_mem):
    t, width = q.shape
    tq = min(MEM_Q_TILE, s)
    nq = s // tq
    scale = MEM_HEAD_DIM ** -0.5

    def body(q_ref, kv_ref, o_ref):
        for h in range(N_HEADS_MEM):
            cols = slice(h * MEM_HEAD_DIM, (h + 1) * MEM_HEAD_DIM)
            k = kv_ref[:, cols]
            v = kv_ref[:, width + h * MEM_HEAD_DIM: width + (h + 1) * MEM_HEAD_DIM]
            sc = lax.dot_general(q_ref[:, cols], k, NT, preferred_element_type=F32) * scale
            p = jnp.exp(sc - jnp.max(sc, axis=1, keepdims=True))
            p = p / jnp.sum(p, axis=1, keepdims=True)
            o_ref[:, cols] = lax.dot_general(p.astype(BF16), v, NN, preferred_element_type=F32).astype(BF16)

    return pl.pallas_call(
        body, name="mem_attn_fwd", grid=(batch, nq),
        in_specs=[pl.BlockSpec((tq, width), lambda b, i: (b * nq + i, 0)),
                  pl.BlockSpec((n_mem, 2 * width), lambda b, i: (b, 0))],
        out_specs=pl.BlockSpec((tq, width), lambda b, i: (b * nq + i, 0)),
        out_shape=jax.ShapeDtypeStruct((t, width), BF16),
        compiler_params=_params(("parallel", "parallel")),
    )(q, kv)


def _mem_bwd(q, kv, do, batch, s, n_mem):
    t, width = q.shape
    tq = min(MEM_Q_TILE, s)
    nq = s // tq
    scale = MEM_HEAD_DIM ** -0.5

    def body(q_ref, kv_ref, do_ref, dq_ref, dkv_ref, acc):
        i = pl.program_id(1)

        @pl.when(i == 0)
        def _():
            acc[...] = jnp.zeros_like(acc)

        for h in range(N_HEADS_MEM):
            cols = slice(h * MEM_HEAD_DIM, (h + 1) * MEM_HEAD_DIM)
            vcols = slice(width + h * MEM_HEAD_DIM, width + (h + 1) * MEM_HEAD_DIM)
            qh, k, v, doh = q_ref[:, cols], kv_ref[:, cols], kv_ref[:, vcols], do_ref[:, cols]
            sc = lax.dot_general(qh, k, NT, preferred_element_type=F32) * scale
            p = jnp.exp(sc - jnp.max(sc, axis=1, keepdims=True))
            p = p / jnp.sum(p, axis=1, keepdims=True)
            dp = lax.dot_general(doh, v, NT, preferred_element_type=F32)
            ds = (p * (dp - jnp.sum(p * dp, axis=1, keepdims=True)) * scale).astype(BF16)
            dq_ref[:, cols] = lax.dot_general(ds, k, NN, preferred_element_type=F32).astype(BF16)
            acc[:, cols] += lax.dot_general(ds, qh, TN, preferred_element_type=F32)
            acc[:, vcols] += lax.dot_general(p.astype(BF16), doh, TN, preferred_element_type=F32)

        @pl.when(i == nq - 1)
        def _():
            dkv_ref[...] = acc[...].astype(BF16)

    row = pl.BlockSpec((tq, width), lambda b, i: (b * nq + i, 0))
    kvs = pl.BlockSpec((n_mem, 2 * width), lambda b, i: (b, 0))
    return pl.pallas_call(
        body, name="mem_attn_bwd", grid=(batch, nq),
        in_specs=[row, kvs, row], out_specs=[row, kvs],
        out_shape=[jax.ShapeDtypeStruct((t, width), BF16), jax.ShapeDtypeStruct((batch * n_mem, 2 * width), BF16)],
        scratch_shapes=[pltpu.VMEM((n_mem, 2 * width), F32)],
        compiler_params=_params(("parallel", "arbitrary")),
    )(q, kv, do)


def _mixer_fwd(o_a, o_b, w_a, w_b, proj, gate_col0, w_out, x, g):
    t, width = o_a.shape
    d = w_a.shape[1]
    tm = min(ROW_TILE, t)
    gb0 = gate_col0 * LANES // d

    def body(oa_ref, ob_ref, wa_ref, wb_ref, ga_ref, gb_ref, wo_ref, x_ref, g_ref, ua_ref, ub_ref, mix_ref, n_ref,
             h_ref):
        ua = lax.dot_general(oa_ref[...], wa_ref[...], NN, preferred_element_type=F32)
        ub = lax.dot_general(ob_ref[...], wb_ref[...], NN, preferred_element_type=F32)
        ua_ref[...] = ua.astype(BF16)
        ub_ref[...] = ub.astype(BF16)
        mixed = (jax.nn.sigmoid(ga_ref[...].astype(F32)) * ua + jax.nn.sigmoid(gb_ref[...].astype(F32)) * ub).astype(BF16)
        mix_ref[...] = mixed
        h = lax.dot_general(mixed, wo_ref[...], NN, preferred_element_type=F32) + x_ref[...]
        h_ref[...] = h
        r = lax.rsqrt(jnp.mean(h * h, axis=-1, keepdims=True) + RMS_EPS)
        n_ref[...] = (h * r * g_ref[...]).astype(BF16)

    row = pl.BlockSpec((tm, width), lambda i: (i, 0))
    wsp = pl.BlockSpec((width, d), lambda i: (0, 0))
    out = pl.BlockSpec((tm, d), lambda i: (i, 0))
    osh = jax.ShapeDtypeStruct((t, d), BF16)
    return pl.pallas_call(
        body, name="mixer_fwd", grid=(t // tm,),
        in_specs=[row, row, wsp, wsp,
                  pl.BlockSpec((tm, d), lambda i: (i, gb0)), pl.BlockSpec((tm, d), lambda i: (i, gb0 + 1)),
                  pl.BlockSpec((d, d), lambda i: (0, 0)), out, pl.BlockSpec((1, d), lambda i: (0, 0))],
        out_specs=[out, out, out, out, out], out_shape=[osh, osh, osh, osh, jax.ShapeDtypeStruct((t, d), F32)],
        compiler_params=_params(("parallel",)),
    )(o_a, o_b, w_a, w_b, proj, proj, w_out, x, g)


def _mixer_bwd(dh, w_out, ua, ub, proj, gate_col0, w_a, w_b):
    t, d = dh.shape
    width = w_a.shape[0]
    tm = min(ROW_TILE, t)
    nc = d // LANES

    def body(dh_ref, w_ref, ua_ref, ub_ref, ga_ref, gb_ref, wa_ref, wb_ref, dua_ref, dub_ref, dg_ref, doa_ref, dob_ref):
        dm = lax.dot_general(dh_ref[...], w_ref[...], NT, preferred_element_type=F32)
        sa = jax.nn.sigmoid(ga_ref[...].astype(F32))
        sb = jax.nn.sigmoid(gb_ref[...].astype(F32))
        dua = (dm * sa).astype(BF16)
        dub = (dm * sb).astype(BF16)
        dua_ref[...] = dua
        dub_ref[...] = dub
        dg_ref[:, 0:d] = (dm * ua_ref[...].astype(F32) * sa * (1.0 - sa)).astype(BF16)
        dg_ref[:, d:2 * d] = (dm * ub_ref[...].astype(F32) * sb * (1.0 - sb)).astype(BF16)
        doa_ref[...] = lax.dot_general(dua, wa_ref[...], NT, preferred_element_type=F32).astype(BF16)
        dob_ref[...] = lax.dot_general(dub, wb_ref[...], NT, preferred_element_type=F32).astype(BF16)

    row = pl.BlockSpec((tm, d), lambda i: (i, 0))
    wsp = pl.BlockSpec((width, d), lambda i: (0, 0))
    osp = pl.BlockSpec((tm, width), lambda i: (i, 0))
    return pl.pallas_call(
        body, name="mixer_bwd", grid=(t // tm,),
        in_specs=[row, pl.BlockSpec((d, d), lambda i: (0, 0)), row, row,
                  pl.BlockSpec((tm, d), lambda i: (i, gate_col0 // nc)),
                  pl.BlockSpec((tm, d), lambda i: (i, gate_col0 // nc + 1)), wsp, wsp],
        out_specs=[row, row, pl.BlockSpec((tm, 2 * d), lambda i: (i, 0)), osp, osp],
        out_shape=[jax.ShapeDtypeStruct((t, d), BF16), jax.ShapeDtypeStruct((t, d), BF16),
                   jax.ShapeDtypeStruct((t, 2 * d), BF16), jax.ShapeDtypeStruct((t, width), BF16),
                   jax.ShapeDtypeStruct((t, width), BF16)],
        compiler_params=_params(("parallel",)),
    )(dh, w_out, ua, ub, proj, proj, w_a, w_b)


FFN_COLS = 1024


def _ffn_up(n, w_gate, w_up):
    t, d = n.shape
    hidden = w_gate.shape[0]
    tm = min(ROW_TILE, t)
    tn = min(FFN_COLS, hidden)

    def body(n_ref, wg_ref, wu_ref, hg_ref, hu_ref, act_ref):
        hg = lax.dot_general(n_ref[...], wg_ref[...], NT, preferred_element_type=F32)
        hu = lax.dot_general(n_ref[...], wu_ref[...], NT, preferred_element_type=F32)
        hg_ref[...] = hg.astype(BF16)
        hu_ref[...] = hu.astype(BF16)
        act_ref[...] = (hg * jax.nn.sigmoid(hg) * hu).astype(BF16)

    wsp = pl.BlockSpec((tn, d), lambda j, i: (j, 0))
    out = pl.BlockSpec((tm, tn), lambda j, i: (i, j))
    osh = jax.ShapeDtypeStruct((t, hidden), BF16)
    return pl.pallas_call(
        body, name="ffn_up", grid=(hidden // tn, t // tm),
        in_specs=[pl.BlockSpec((tm, d), lambda j, i: (i, 0)), wsp, wsp],
        out_specs=[out, out, out], out_shape=[osh, osh, osh],
        compiler_params=_params(("parallel", "parallel")),
    )(n, w_gate, w_up)


def _ffn_bwd(dh, w_down, w_gate, w_up, hg, hu, x, g, dres):
    t, d = dh.shape
    hidden = w_down.shape[0]
    tm = min(ROW_TILE, t)
    tn = min(FFN_COLS, hidden)
    nj = hidden // tn

    def body(dh_ref, wd_ref, wg_ref, wu_ref, hg_ref, hu_ref, x_ref, g_ref, r_ref, dhg_ref, dhu_ref, dx_ref, dxb_ref,
             dg_ref, acc):
        j, i = pl.program_id(0), pl.program_id(1)
        dact = lax.dot_general(dh_ref[...], wd_ref[...], NT, preferred_element_type=F32)
        hg = hg_ref[...].astype(F32)
        sg = jax.nn.sigmoid(hg)
        dhu = (dact * hg * sg).astype(BF16)
        dhg = (dact * hu_ref[...].astype(F32) * sg * (1.0 + hg * (1.0 - sg))).astype(BF16)
        dhu_ref[...] = dhu
        dhg_ref[...] = dhg
        part = (lax.dot_general(dhg, wg_ref[...], NN, preferred_element_type=F32)
                + lax.dot_general(dhu, wu_ref[...], NN, preferred_element_type=F32))

        @pl.when(j == 0)
        def _():
            acc[i] = part

        @pl.when(j > 0)
        def _():
            acc[i] += part

        @pl.when(jnp.logical_and(j == 0, i == 0))
        def _():
            dg_ref[...] = jnp.zeros_like(dg_ref)

        @pl.when(j == nj - 1)
        def _():
            dx, dg = _rms_bwd_rows(acc[i], x_ref[...], g_ref[...], r_ref[...])
            dx_ref[...] = dx
            dxb_ref[...] = dx.astype(BF16)
            dg_ref[...] += dg

    hid = pl.BlockSpec((tm, tn), lambda j, i: (i, j))
    wsp = pl.BlockSpec((tn, d), lambda j, i: (j, 0), pipeline_mode=pl.Buffered(1))
    late = pl.BlockSpec((tm, d), lambda j, i: (jnp.where(j == nj - 1, i, 0), 0))
    vec = pl.BlockSpec((1, d), lambda j, i: (0, 0))
    osh = jax.ShapeDtypeStruct((t, hidden), BF16)
    return pl.pallas_call(
        body, name="ffn_bwd", grid=(nj, t // tm),
        in_specs=[pl.BlockSpec((tm, d), lambda j, i: (i, 0)), wsp, wsp, wsp, hid, hid, late, vec, late],
        out_specs=[hid, hid, late, late, vec],
        out_shape=[osh, osh, jax.ShapeDtypeStruct((t, d), F32), jax.ShapeDtypeStruct((t, d), BF16),
                   jax.ShapeDtypeStruct((1, d), F32)],
        scratch_shapes=[pltpu.VMEM((t // tm, tm, d), F32)],
        compiler_params=_params(("arbitrary", "arbitrary")),
    )(dh, w_down, w_gate, w_up, hg, hu, x, g, dres)


MM_ROWS = 1024


def _mm_w(name, a, w, out_dtype, dims=NN):
    t, k = a.shape
    n = w.shape[1] if dims == NN else w.shape[0]
    tm, tn = min(MM_ROWS, t), min(1024, n)
    o_spec = pl.BlockSpec((tm, tn), lambda j, i: (i, j))
    b_spec = pl.BlockSpec((k, tn), lambda j, i: (0, j)) if dims == NN else pl.BlockSpec((tn, k), lambda j, i: (j, 0))
    return _mm(name, a, w, grid=(n // tn, t // tm), a_spec=pl.BlockSpec((tm, k), lambda j, i: (i, 0)), b_spec=b_spec,
               o_shape=(t, n), o_spec=o_spec, dims=dims, out_dtype=out_dtype)


def _mm_res_norm(name, a, w, res, g):
    t, k = a.shape
    d = w.shape[1]
    tm = min(ROW_TILE, t)

    def body(a_ref, w_ref, r_ref, g_ref, h_ref, n_ref):
        h = lax.dot_general(a_ref[...], w_ref[...], NN, preferred_element_type=F32) + r_ref[...]
        h_ref[...] = h
        r = lax.rsqrt(jnp.mean(h * h, axis=-1, keepdims=True) + RMS_EPS)
        n_ref[...] = (h * r * g_ref[...]).astype(BF16)

    row = pl.BlockSpec((tm, d), lambda i: (i, 0))
    return pl.pallas_call(
        body, name=name, grid=(t // tm,),
        in_specs=[pl.BlockSpec((tm, k), lambda i: (i, 0)), pl.BlockSpec((k, d), lambda i: (0, 0)), row,
                  pl.BlockSpec((1, d), lambda i: (0, 0))],
        out_specs=[row, row], out_shape=[jax.ShapeDtypeStruct((t, d), F32), jax.ShapeDtypeStruct((t, d), BF16)],
        compiler_params=_params(("parallel",)),
    )(a, w, res, g)


def _wgrad(name, a, g, tk=1024, tn=1024):
    t, k = a.shape
    n = g.shape[1]
    tm, tk, tn = min(2 * MM_ROWS, t), min(tk, k), min(tn, n)
    return _mm(name, a, g, grid=(k // tk, n // tn, t // tm),
               a_spec=pl.BlockSpec((tm, tk), lambda p, q, r: (r, p)), b_spec=pl.BlockSpec((tm, tn), lambda p, q, r: (r, q)),
               o_shape=(k, n), o_spec=pl.BlockSpec((tk, tn), lambda p, q, r: (p, q)), dims=TN, out_dtype=BF16, nk=t // tm)


def _peers():
    x, y, c = lax.axis_index("x"), lax.axis_index("y"), lax.axis_index("c")
    me = 4 * x + 2 * y + c
    out = []
    for k in range(1, N_DEV):
        kx, ky, kc = (k >> 2) & 1, (k >> 1) & 1, k & 1
        px = 1 - x if kx else x
        py = 1 - y if ky else y
        pc = 1 - c if kc else c
        out.append(((px, py, pc), 4 * px + 2 * py + pc))
    return me, out


def _cast_weights(ws, pad_rows):
    def body(*refs):
        n = len(refs) // 2
        for i_ref, o_ref, pr in zip(refs[:n], refs[n:], pad_rows):
            r, c = i_ref.shape
            o_ref[0:r, :] = i_ref[...].astype(BF16)
            if pr:
                o_ref[r:r + pr, :] = jnp.zeros((pr, c), BF16)

    return pl.pallas_call(
        body, name="cast_weights", in_specs=[VMEM] * len(ws), out_specs=[VMEM] * len(ws),
        out_shape=[jax.ShapeDtypeStruct((w.shape[0] + pr, w.shape[1]), BF16) for w, pr in zip(ws, pad_rows)],
    )(*ws)


def _window(ref, j, c):
    return ref.at[:, pl.ds(pl.multiple_of(j * c, LANES), c)]


def _scatter_copies(ins, outs, sems, cols, landed):
    send_sems, recv_sems, loc_sems = sems
    n_peer = N_DEV - 1
    me, peers = _peers()

    def src(w, j):
        return _window(ins[w], j, cols[w]) if cols[w] else ins[w].at[j]

    local = [pltpu.make_async_copy(src(w, me), outs[w].at[me], loc_sems.at[w]) for w in range(len(ins))]
    remote = [pltpu.make_async_remote_copy(
        src_ref=src(w, idx), dst_ref=outs[w].at[idx if landed else me],
        send_sem=send_sems.at[w * n_peer + k], recv_sem=recv_sems.at[w * n_peer + k],
        device_id=dev, device_id_type=pl.DeviceIdType.MESH)
        for k, (dev, idx) in reversed(list(enumerate(peers))) for w in range(len(ins))]
    return local, remote


OTHER_CHIPS = (2, 4, 6)


def _gather_copies(ins, outs, sems, cols):
    send_sems, recv_sems, loc_sems = sems
    x, y, c = lax.axis_index("x"), lax.axis_index("y"), lax.axis_index("c")
    me = 4 * x + 2 * y + c
    n_pair = N_DEV - 1

    def dev(mask):
        return (1 - x if mask & 4 else x, 1 - y if mask & 2 else y, 1 - c if mask & 1 else c)

    def slot(w, mask):
        j = jnp.bitwise_xor(me, mask)
        return _window(outs[w], j, cols[w]) if cols[w] else outs[w].at[j]

    def remote(w, pair, src, to_slot, target):
        return pltpu.make_async_remote_copy(src_ref=src, dst_ref=slot(w, to_slot), send_sem=send_sems.at[w * n_pair + pair],
                                            recv_sem=recv_sems.at[w * n_pair + pair], device_id=dev(target),
                                            device_id_type=pl.DeviceIdType.MESH)

    ws = range(len(ins))
    return dict(
        local=[pltpu.make_async_copy(ins[w], slot(w, 0), loc_sems.at[w]) for w in ws],
        to_chips=[remote(w, 1 + t, ins[w], 0, m) for t, m in enumerate(OTHER_CHIPS) for w in ws],
        to_core=[remote(w, 0, ins[w], 0, 1) for w in ws],
        from_chips=[remote(w, 1 + t, ins[w], m, 0) for t, m in enumerate(OTHER_CHIPS) for w in ws],
        pass_on=[remote(w, 4 + t, slot(w, m), m, 1) for t, m in enumerate(OTHER_CHIPS) for w in ws],
        from_core=[remote(w, 0, ins[w], 1, 0) for w in ws]
        + [remote(w, 4 + t, ins[w], m + 1, 0) for t, m in enumerate(OTHER_CHIPS) for w in ws])


def _exchange_start(ins, outs, sems, gather, cols):
    if gather:
        cps = _gather_copies(ins, outs, sems, cols)
        for cp in cps["local"] + cps["to_chips"] + cps["to_core"]:
            cp.start()
    else:
        local, remote = _scatter_copies(ins, outs, sems, cols, False)
        for cp in local + remote:
            cp.start()


def _exchange_pass_on(ins, outs, sems, gather, cols, chips):
    if gather:
        cps = _gather_copies(ins, outs, sems, cols)
        n = len(ins)
        for t in chips:
            for arrived, onward in zip(cps["from_chips"][t * n:(t + 1) * n], cps["pass_on"][t * n:(t + 1) * n]):
                arrived.wait_recv()
                onward.start()


def _exchange_wait(ins, outs, sems, gather, cols):
    if gather:
        cps = _gather_copies(ins, outs, sems, cols)
        for cp in cps["local"]:
            cp.wait()
        for cp in cps["to_chips"] + cps["to_core"] + cps["pass_on"]:
            cp.wait_send()
        for cp in cps["from_core"]:
            cp.wait_recv()
    else:
        local, remote = _scatter_copies(ins, outs, sems, cols, True)
        for cp in local:
            cp.wait()
        for cp in remote:
            cp.wait_send()
            cp.wait_recv()


def _exchange_shapes(arrs, gather, cols):
    n = len(arrs)
    out_shape = []
    for a, c in zip(arrs, cols):
        if gather:
            shape = (a.shape[0], N_DEV * c) if c else (N_DEV,) + a.shape
        else:
            shape = (N_DEV, a.shape[0], c) if c else a.shape
        out_shape.append(jax.ShapeDtypeStruct(shape, a.dtype))
    sems = [pltpu.SemaphoreType.DMA((n * (N_DEV - 1),)), pltpu.SemaphoreType.DMA((n * (N_DEV - 1),)),
            pltpu.SemaphoreType.DMA((n,))]
    return out_shape, sems


def _call(body, *, name, grid, in_specs, out_specs, out_shape, scratch, sem, args, ride=None):
    if ride is None:
        outs = pl.pallas_call(body, name=name, grid=grid, in_specs=in_specs, out_specs=out_specs, out_shape=out_shape,
                              scratch_shapes=scratch, compiler_params=_params(sem))(*args)
        return outs, None
    arrs, gather, cols = ride
    n, n_in, n_out, n_scr = len(arrs), len(in_specs), len(out_specs), len(scratch)
    x_shape, x_sems = _exchange_shapes(arrs, gather, cols)

    def riding(*refs):
        ins, x_ins = refs[:n_in], refs[n_in:n_in + n]
        outs = refs[n_in + n:n_in + n + n_out]
        x_outs = refs[n_in + n + n_out:n_in + 2 * n + n_out]
        scr = refs[n_in + 2 * n + n_out:n_in + 2 * n + n_out + n_scr]
        sems = refs[n_in + 2 * n + n_out + n_scr:]
        def at(step):
            return functools.reduce(jnp.logical_and, [pl.program_id(a) == v for a, v in enumerate(step)])

        @pl.when(at((0,) * len(grid)))
        def _():
            _exchange_start(x_ins, x_outs, sems, gather, cols)

        @pl.when(at((grid[0] // 2,) + (0,) * (len(grid) - 2) + (grid[-1] // 2,)))
        def _():
            _exchange_pass_on(x_ins, x_outs, sems, gather, cols, (0, 1))

        @pl.when(at((grid[0] // 2,) + (0,) * (len(grid) - 2) + (3 * grid[-1] // 4,)))
        def _():
            _exchange_pass_on(x_ins, x_outs, sems, gather, cols, (2,))

        body(*ins, *outs, *scr)

        @pl.when(at(tuple(g - 1 for g in grid)))
        def _():
            _exchange_wait(x_ins, x_outs, sems, gather, cols)

    res = pl.pallas_call(
        riding, name=name, grid=grid, in_specs=list(in_specs) + [ANY] * n, out_specs=list(out_specs) + [ANY] * n,
        out_shape=list(out_shape) + x_shape, scratch_shapes=list(scratch) + x_sems,
        compiler_params=_params(("arbitrary",) * len(grid)))(*args, *arrs)
    return res[:n_out], res[n_out:]


def _my_block():
    return (4 * lax.axis_index("x") + 2 * lax.axis_index("y") + lax.axis_index("c")).astype(jnp.int32).reshape(1)


def _proj_in_gather(x, g, w_shard):
    t, k = x.shape
    cs = w_shard.shape[1]
    tm = min(MM_ROWS, t)
    ni = t // tm
    arrival = (0, 1, 2, 4, 3, 5, 6, 7)

    def mask_at(s):
        return jnp.where(s == 3, 4, jnp.where(s == 4, 3, s))

    def body(me_ref, x_ref, g_ref, w_hbm, o_ref, all_hbm, n_hbm, w_vmem, n_vmem, send_sems, recv_sems, loc_sems,
             load_sems, n_sem):
        s, i = pl.program_id(0), pl.program_id(1)
        cps = _gather_copies([w_hbm], [all_hbm], (send_sems, recv_sems, loc_sems), (cs,))
        by_mask = {0: cps["local"][0], 1: cps["from_core"][0]}
        for t_chip, m in enumerate(OTHER_CHIPS):
            by_mask[m] = cps["from_chips"][t_chip]
            by_mask[m + 1] = cps["from_core"][1 + t_chip]
        arrived = [by_mask[m] for m in arrival]

        def load(step):
            src = w_hbm if step == 0 else _window(all_hbm, jnp.bitwise_xor(me_ref[0], arrival[step]), cs)
            return pltpu.make_async_copy(src, w_vmem.at[step % 2], load_sems.at[step % 2])

        @pl.when(jnp.logical_and(s == 0, i == 0))
        def _():
            for cp in cps["local"] + cps["to_chips"] + cps["to_core"]:
                cp.start()
            load(0).start()

        for step, mask in enumerate(arrival):
            @pl.when(jnp.logical_and(s == step, i == 0))
            def _(step=step):
                load(step).wait()

            if step + 1 < N_DEV:
                @pl.when(jnp.logical_and(s == step, i == min(1, ni - 1)))
                def _(step=step):
                    arrived[step + 1].wait_recv()
                    if arrival[step + 1] in OTHER_CHIPS:
                        cps["pass_on"][OTHER_CHIPS.index(arrival[step + 1])].start()
                    load(step + 1).start()

        @pl.when(s == 0)
        def _():
            xf = x_ref[...]
            r = lax.rsqrt(jnp.mean(xf * xf, axis=-1, keepdims=True) + RMS_EPS)
            n_vmem[i] = (xf * r * g_ref[...]).astype(BF16)
            keep = pltpu.make_async_copy(n_vmem.at[i], n_hbm.at[pl.ds(pl.multiple_of(i * tm, tm), tm), :], n_sem)
            keep.start()
            keep.wait()

        o_ref[...] = lax.dot_general(n_vmem[i], w_vmem[s % 2], NN, preferred_element_type=F32).astype(BF16)

        @pl.when(jnp.logical_and(s == N_DEV - 1, i == ni - 1))
        def _():
            cps["local"][0].wait()
            for cp in cps["to_chips"] + cps["to_core"] + cps["pass_on"]:
                cp.wait_send()

    return pl.pallas_call(
        body, name="proj_in",
        grid_spec=pltpu.PrefetchScalarGridSpec(
            num_scalar_prefetch=1, grid=(N_DEV, ni),
            in_specs=[pl.BlockSpec((tm, k), lambda s, i, me: (jnp.where(s == 0, i, 0), 0)),
                      pl.BlockSpec((1, k), lambda s, i, me: (0, 0)), ANY],
            out_specs=[pl.BlockSpec((tm, cs), lambda s, i, me: (i, jnp.bitwise_xor(me[0], mask_at(s)))), ANY, ANY],
            scratch_shapes=[pltpu.VMEM((2, k, cs), BF16), pltpu.VMEM((ni, tm, k), BF16),
                            pltpu.SemaphoreType.DMA((N_DEV - 1,)), pltpu.SemaphoreType.DMA((N_DEV - 1,)),
                            pltpu.SemaphoreType.DMA((1,)), pltpu.SemaphoreType.DMA((2,)), pltpu.SemaphoreType.DMA]),
        out_shape=[jax.ShapeDtypeStruct((t, N_DEV * cs), BF16), jax.ShapeDtypeStruct((k, N_DEV * cs), BF16),
                   jax.ShapeDtypeStruct((t, k), BF16)],
        compiler_params=_params(("arbitrary", "arbitrary")),
    )(_my_block(), x, g, w_shard)


def _gw_in_scatter(a, g):
    t, k = a.shape
    cs = g.shape[1] // N_DEV
    tm = min(MM_ROWS, t)
    nr = t // tm
    n_chip = N_DEV // 2
    chips = (6, 4, 2, 0)

    def body(me_ref, a_ref, g_ref, out_hbm, acc, stage, other, core_send, core_recv, chip_send, chip_recv, loc_sem):
        s, r = pl.program_id(0), pl.program_id(1)
        x, y, c = lax.axis_index("x"), lax.axis_index("y"), lax.axis_index("c")
        my_chip = 2 * x + y
        part = lax.dot_general(a_ref[...], g_ref[...], TN, preferred_element_type=F32)

        def to_core(m):
            return pltpu.make_async_remote_copy(src_ref=stage.at[0], dst_ref=other.at[m], send_sem=core_send.at[m],
                                                recv_sem=core_recv.at[m], device_id=(x, y, 1 - c),
                                                device_id_type=pl.DeviceIdType.MESH)

        def to_chip(m, landed):
            mask = chips[m]
            there = (1 - x if mask & 4 else x, 1 - y if mask & 2 else y, c)
            slot = (2 * there[0] + there[1]) if landed else my_chip
            return pltpu.make_async_remote_copy(src_ref=stage.at[1], dst_ref=out_hbm.at[slot], send_sem=chip_send.at[m],
                                                recv_sem=chip_recv.at[m], device_id=there,
                                                device_id_type=pl.DeviceIdType.MESH)

        local = pltpu.make_async_copy(stage.at[1], out_hbm.at[my_chip], loc_sem)

        @pl.when(r == 0)
        def _():
            acc[...] = part

        @pl.when(r > 0)
        def _():
            acc[...] += part

        for step in range(N_DEV):
            m = step // 2

            @pl.when(jnp.logical_and(s == step, r == nr - 1))
            def _(step=step, m=m):
                if step % 2 == 0:
                    if m > 0:
                        to_core(m - 1).wait_send()
                    stage[0] = acc[...].astype(BF16)
                    to_core(m).start()
                else:
                    if m > 0:
                        to_chip(m - 1, False).wait_send()
                    to_core(m).wait_recv()
                    stage[1] = (acc[...] + other[m].astype(F32)).astype(BF16)
                    if m < n_chip - 1:
                        to_chip(m, False).start()
                    else:
                        local.start()
                        to_core(m).wait_send()
                        local.wait()
                        for mm in range(n_chip - 1):
                            to_chip(mm, True).wait_recv()

    return pl.pallas_call(
        body, name="gw_in",
        grid_spec=pltpu.PrefetchScalarGridSpec(
            num_scalar_prefetch=1, grid=(N_DEV, nr),
            in_specs=[pl.BlockSpec((tm, k), lambda s, r, me: (r, 0)),
                      pl.BlockSpec((tm, cs), lambda s, r, me: (r, jnp.bitwise_xor(me[0], N_DEV - 1 - s)))],
            out_specs=ANY,
            scratch_shapes=[pltpu.VMEM((k, cs), F32), pltpu.VMEM((2, k, cs), BF16), pltpu.VMEM((n_chip, k, cs), BF16),
                            pltpu.SemaphoreType.DMA((n_chip,)), pltpu.SemaphoreType.DMA((n_chip,)),
                            pltpu.SemaphoreType.DMA((n_chip - 1,)), pltpu.SemaphoreType.DMA((n_chip - 1,)),
                            pltpu.SemaphoreType.DMA]),
        out_shape=jax.ShapeDtypeStruct((n_chip, k, cs), BF16),
        compiler_params=_params(("arbitrary", "arbitrary")),
    )(_my_block(), a, g)


SMALL_ROWS = 8


def _allreduce_small(parts, loss_part):
    n, d = len(parts), parts[0].shape[1]

    def body(*refs):
        part_refs, loss_ref, o_ref = refs[:n], refs[n], refs[n + 1]
        mine_ref, all_ref, send_sems, recv_sems = refs[n + 2:]
        me, peers = _peers()
        mine_ref[...] = jnp.zeros_like(mine_ref)
        for i, p_ref in enumerate(part_refs):
            mine_ref[i:i + 1, :] = p_ref[...]
        mine_ref[SMALL_ROWS - 1:SMALL_ROWS, 0:LANES] = loss_ref[0:1, :]
        all_ref[me] = mine_ref[...]
        for k, (dev, idx) in enumerate(peers):
            pltpu.make_async_remote_copy(src_ref=mine_ref, dst_ref=all_ref.at[me], send_sem=send_sems.at[k],
                                         recv_sem=recv_sems.at[k], device_id=dev,
                                         device_id_type=pl.DeviceIdType.MESH).start()
        for k, (dev, idx) in enumerate(peers):
            cp = pltpu.make_async_remote_copy(src_ref=mine_ref, dst_ref=all_ref.at[idx], send_sem=send_sems.at[k],
                                              recv_sem=recv_sems.at[k], device_id=dev,
                                              device_id_type=pl.DeviceIdType.MESH)
            cp.wait_send()
            cp.wait_recv()
        tot = all_ref[0]
        for dvc in range(1, N_DEV):
            tot = tot + all_ref[dvc]
        o_ref[...] = tot

    return pl.pallas_call(
        body, name="allreduce_small", in_specs=[VMEM] * (n + 1), out_specs=VMEM,
        out_shape=jax.ShapeDtypeStruct((SMALL_ROWS, d), F32),
        scratch_shapes=[pltpu.VMEM((SMALL_ROWS, d), F32), pltpu.VMEM((N_DEV, SMALL_ROWS, d), F32),
                        pltpu.SemaphoreType.DMA((N_DEV - 1,)), pltpu.SemaphoreType.DMA((N_DEV - 1,))],
    )(*parts, loss_part)


def _adam_math(g, w, m, v):
    m_new = ADAM_B1 * m + (1.0 - ADAM_B1) * g
    v_new = ADAM_B2 * v + (1.0 - ADAM_B2) * (g * g)
    m_hat = m_new / (1.0 - ADAM_B1 ** ADAM_STEP)
    v_hat = v_new / (1.0 - ADAM_B2 ** ADAM_STEP)
    delta = -ADAM_LR * (m_hat / (jnp.sqrt(v_hat) + ADAM_EPS) + ADAM_WD * w)
    return delta, m_new, v_new


def _adam(name, pieces, w, m, v):
    r, c = w.shape
    n_piece, _, cp = pieces.shape
    tr = r
    for cand in (256, 176, 128, 64):
        if r % cand == 0 and r > cand:
            tr = cand
            break

    def body(p_ref, w_ref, m_ref, v_ref, g_ref, d_ref, mo_ref, vo_ref):
        g = p_ref[0, :, 0:c].astype(F32)
        for j in range(1, n_piece):
            g = g + p_ref[j, :, 0:c].astype(F32)
        delta, m_new, v_new = _adam_math(g, w_ref[...], m_ref[...], v_ref[...])
        g_ref[...] = g
        d_ref[...] = delta
        mo_ref[...] = m_new
        vo_ref[...] = v_new

    blk = pl.BlockSpec((tr, c), lambda i: (i, 0))
    osh = jax.ShapeDtypeStruct((r, c), F32)
    return pl.pallas_call(
        body, name=name, grid=(r // tr,),
        in_specs=[pl.BlockSpec((n_piece, tr, cp), lambda i: (0, i, 0)), blk, blk, blk],
        out_specs=[blk, blk, blk, blk], out_shape=[osh, osh, osh, osh],
        compiler_params=_params(("parallel",)),
    )(pieces, w, m, v)


def _adam_small(g_all, ws, ms, vs):
    n = len(ws)

    def body(*refs):
        g_ref, ins, outs = refs[0], refs[1:1 + 3 * n], refs[1 + 3 * n:]
        for i in range(n):
            g = g_ref[i:i + 1, :]
            delta, m_new, v_new = _adam_math(g, ins[i][...], ins[n + i][...], ins[2 * n + i][...])
            for kind, val in enumerate((g, delta, m_new, v_new)):
                outs[kind * n + i][...] = val

    osh = jax.ShapeDtypeStruct(ws[0].shape, F32)
    res = pl.pallas_call(body, name="adam_small", in_specs=[VMEM] * (1 + 3 * n), out_specs=[VMEM] * (4 * n),
                         out_shape=[osh] * (4 * n))(g_all, *ws, *ms, *vs)
    return res[:n], res[n:2 * n], res[2 * n:3 * n], res[3 * n:]


def _local_step(x, mem, pos, tgt, gains, w_in_shard, shards, batch):
    g_mix, g_mem_q, g_mem_kv, g_ffn, g_final = gains
    t, d = x.shape
    s = t // batch
    n_mem = mem.shape[0] // batch
    n_sh = N_DEV
    width = shards[0].shape[0]
    nb = width // LANES

    lane = np.arange(LANES) % HEAD_DIM
    sel_lo = (lane < ROPE_HALF).astype(np.float32)[None, :]
    sel_hi = ((lane >= ROPE_HALF) & (lane < 2 * ROPE_HALF)).astype(np.float32)[None, :]
    freqs = np.float32(ROPE_THETA) ** (-np.arange(ROPE_HALF, dtype=np.float32) / np.float32(ROPE_HALF))
    inv_freq = np.where(lane < 2 * ROPE_HALF, freqs[lane % ROPE_HALF], 0.0).astype(np.float32)[None, :]
    cos_t, sin_a, sin_b = _rope_tables(pos, jnp.asarray(inv_freq), jnp.asarray(sel_lo), jnp.asarray(sel_hi))
    bias = _dilated_bias_tiles(s)

    proj, w_in, n1 = _proj_in_gather(x, g_mix, w_in_shard)
    qk_a = _rope_apply("rope_fwd", [proj], 2 * width, cos_t, sin_a, sin_b, 1.0)
    cs_up = shards[0].shape[1]
    (o_a, lse_a), (w_up_a, w_up_b, w_out, w_q, w_kv, w_o, w_fd) = _da_fwd(
        qk_a, proj, 2 * nb, bias, batch, s,
        ride=(shards[:6] + shards[8:], True, (cs_up, cs_up, 0, 0, 0, cs_up, 0)))
    (o_b, tot_b), (w_fg, w_fu) = _sb_fwd(proj, 3 * nb, 4 * nb, 5 * nb, batch, s, ride=(shards[6:8], True, (0, 0)))
    w_out = w_out.reshape(d, d)
    w_q = w_q.reshape(d, -1)
    w_kv = w_kv.reshape(d, -1)
    w_fd = w_fd.reshape(-1, d)
    w_fg = w_fg.reshape(-1, d)
    w_fu = w_fu.reshape(-1, d)
    ua, ub, mixed, n2, h1 = _mixer_fwd(o_a, o_b, w_up_a, w_up_b, proj, 6 * nb, w_out, x, g_mem_q)
    mem_n = _rms_fwd("norm_mem_kv", mem, g_mem_kv)
    q_m = _mm_w("mem_q", n2, w_q, BF16)
    kv_m = _mm_w("mem_kv", mem_n, w_kv, BF16)
    o_m = _mem_fwd(q_m, kv_m, batch, s, n_mem)
    h2, n3 = _mm_res_norm("mem_out", o_m, w_o, h1, g_ffn)
    hg, hu, act = _ffn_up(n3, w_fg, w_fu)
    loss_part, dh3, dh3_b, dg_final = _loss_head(act, w_fd, h2, tgt, g_final.reshape(1, d))

    dhg, dhu, dh2, dh2_b, dg_ffn = _ffn_bwd(dh3_b, w_fd, w_fg, w_fu, hg, hu, h2, g_ffn, dh3)
    gw_fd = _wgrad("gw_ffn_down", act, dh3_b)
    gw_fg = _wgrad("gw_ffn_gate", dhg, n3)
    gw_fu = _wgrad("gw_ffn_up", dhu, n3)

    do_m = _mm_w("mem_out_bwd", dh2_b, w_o, BF16, dims=NT)
    gw_o = _wgrad("gw_mem_o", o_m, dh2_b)
    dq_m, dkv_m = _mem_bwd(q_m, kv_m, do_m, batch, s, n_mem)
    gw_q = _wgrad("gw_mem_q", n2, dq_m)
    gw_kv = _wgrad("gw_mem_kv", mem_n, dkv_m)
    (dg_mem_kv,) = _rms_bwd("norm_mem_kv_bwd", (dkv_m, w_kv, NT), mem, g_mem_kv, None, ())
    dh1, dh1_b, dg_mem_q = _rms_bwd("norm_mem_q_bwd", (dq_m, w_q, NT), h1, g_mem_q, dh2, ("f32", "bf16"))

    gw_out = _wgrad("gw_out", mixed, dh1_b)
    dua, dub, dgates, do_a, do_b = _mixer_bwd(dh1_b, w_out, ua, ub, proj, 6 * nb, w_up_a, w_up_b)
    gw_ua = _wgrad("gw_up_a", o_a, dua)
    gw_ub = _wgrad("gw_up_b", o_b, dub)
    (dq_ar, dk_ar, dv_a), (p_fg, p_fd) = _da_bwd(
        qk_a, proj, 2 * nb, bias, o_a, lse_a, do_a, batch, s,
        ride=([gw_fg.reshape(n_sh, -1, d), gw_fd.reshape(n_sh, -1, d)], False, (0, 0)))
    dqk_a = _rope_apply("rope_bwd", [dq_ar, dk_ar], width, cos_t, sin_a, sin_b, -1.0)
    mid = [gw_ua, gw_ub, gw_out.reshape(n_sh, -1, d), gw_q.reshape(n_sh, -1, gw_q.shape[1]),
           gw_kv.reshape(n_sh, -1, gw_kv.shape[1]), gw_o, gw_fu.reshape(n_sh, -1, d)]
    (dq_b, dk_b, dv_b), (*p_mid, p_fu) = _sb_bwd(proj, 3 * nb, 4 * nb, 5 * nb, tot_b, do_b, batch, s,
                                                 ride=(mid, False, (cs_up, cs_up, 0, 0, 0, cs_up, 0)))
    p_ffn = [p_fg, p_fu, p_fd]
    dproj = jnp.concatenate([dqk_a, dv_a, dq_b, dk_b, dv_b, dgates], axis=1)
    grad_x, dg_mix = _rms_bwd("proj_in_bwd", (dproj, w_in, NT), x, g_mix, dh1, ("f32",))
    p_in = _gw_in_scatter(n1, dproj)
    return loss_part, grad_x, [p_in] + list(p_mid) + p_ffn, (dg_mix, dg_mem_q, dg_mem_kv, dg_ffn, dg_final)


WEIGHTS =("w_in", "w_up_a", "w_up_b", "w_out", "w_q_mem", "w_kv_mem", "w_o_mem", "w_ffn_gate", "w_ffn_up", "w_ffn_down")
GAINS = ("g_mix", "g_mem_q", "g_mem_kv", "g_ffn", "g_final")
ORDER = ("g_mix", "w_in", "w_up_a", "w_up_b", "w_out", "g_mem_q", "g_mem_kv", "w_q_mem", "w_kv_mem", "w_o_mem", "g_ffn",
         "w_ffn_gate", "w_ffn_up", "w_ffn_down", "g_final")


def kernel(x, mem, positions, g_mix, w_in, w_up_a, w_up_b, w_out, g_mem_q, g_mem_kv, w_q_mem, w_kv_mem, w_o_mem, g_ffn, w_ffn_gate, w_ffn_up, w_ffn_down, g_final, loss_target, m_g_mix, m_w_in, m_w_up_a, m_w_up_b, m_w_out, m_g_mem_q, m_g_mem_kv, m_w_q_mem, m_w_kv_mem, m_w_o_mem, m_g_ffn, m_w_ffn_gate, m_w_ffn_up, m_w_ffn_down, m_g_final, v_g_mix, v_w_in, v_w_up_a, v_w_up_b, v_w_out, v_g_mem_q, v_g_mem_kv, v_w_q_mem, v_w_kv_mem, v_w_o_mem, v_g_ffn, v_w_ffn_gate, v_w_ffn_up, v_w_ffn_down, v_g_final):
    given = dict(locals())
    batch, s, d = x.shape
    t = batch * s
    flipped = ("w_ffn_gate", "w_ffn_up")

    def view(a, n):
        a = a.reshape(a.shape[-2:])
        return a.T if n in flipped else a

    def unview(a, n):
        return (a.T if n in flipped else a).reshape(given[n].shape)

    shard = {n: view(given[n], n) for n in WEIGHTS}
    gains = [given[n].reshape(1, d) for n in GAINS]

    pad = (-shard["w_ffn_down"].shape[0]) % LANES
    cast = _cast_weights([shard[n] for n in WEIGHTS], [pad if n in flipped + ("w_ffn_down",) else 0 for n in WEIGHTS])
    loss_part, grad_x, pieces, dgains = _local_step(
        x.reshape(t, d), mem.reshape(-1, d), positions.reshape(t, 1), loss_target.reshape(t, d), gains, cast[0],
        cast[1:], batch)

    grad, delta, new_m, new_v = {}, {}, {}, {}
    for n, p in zip(WEIGHTS, pieces):
        outs = _adam("adam_" + n, p, shard[n], view(given["m_" + n], n), view(given["v_" + n], n))
        grad[n], delta[n], new_m[n], new_v[n] = [unview(o, n) for o in outs]

    g_all = _allreduce_small(list(dgains), loss_part)
    small = _adam_small(g_all, gains, [given["m_" + n].reshape(1, d) for n in GAINS],
                        [given["v_" + n].reshape(1, d) for n in GAINS])
    for out, vals in zip((grad, delta, new_m, new_v), small):
        for n, val in zip(GAINS, vals):
            out[n] = val.reshape(given[n].shape)

    loss = g_all[SMALL_ROWS - 1, 0]
    return (loss, grad_x.reshape(x.shape), *[grad[n] for n in ORDER], *[delta[n] for n in ORDER],
            *[new_m[n] for n in ORDER], *[new_v[n] for n in ORDER])
```

```python
import functools
import math

import jax
import jax.numpy as jnp
import numpy as np
from jax import lax
from jax.experimental import pallas as pl
from jax.experimental.pallas import tpu as pltpu

F32 = jnp.float32
BF16 = jnp.bfloat16

N_DEV = 8
HEAD_DIM = 64
MEM_HEAD_DIM = 128
N_HEADS_MEM = 4
BLOCK = 128
DIL_PATTERNS = ((128, 1), (512, 4), (2048, 16))
ROPE_THETA = 500000.0
ROPE_HALF = 8
RMS_EPS = 1e-6
ADAM_LR, ADAM_B1, ADAM_B2, ADAM_EPS, ADAM_WD, ADAM_STEP = 0.001, 0.9, 0.999, 1e-08, 0.01, 10
NEG = -1e30
ROW_TILE = 512
LANES = 128

ANY = pl.BlockSpec(memory_space=pl.ANY)
VMEM = pl.BlockSpec(memory_space=pltpu.VMEM)
NN = (((1,), (0,)), ((), ()))
NT = (((1,), (1,)), ((), ()))
TN = (((0,), (0,)), ((), ()))


def _params(sem):
    return pltpu.CompilerParams(dimension_semantics=sem)


def _mm(name, a, b, *, grid, a_spec, b_spec, o_shape, o_spec, dims, out_dtype, nk=1):
    def body(*refs):
        a_ref, b_ref, o_ref = refs[0], refs[1], refs[2]
        p = lax.dot_general(a_ref[...], b_ref[...], dims, preferred_element_type=F32)
        if nk == 1:
            o_ref[...] = p.astype(out_dtype)
            return
        acc_ref = refs[-1]
        k = pl.program_id(len(grid) - 1)

        @pl.when(k == 0)
        def _():
            acc_ref[...] = p

        @pl.when(k > 0)
        def _():
            acc_ref[...] += p

        @pl.when(k == nk - 1)
        def _():
            o_ref[...] = acc_ref[...].astype(out_dtype)

    o_block = tuple(d for d in o_spec.block_shape if d is not None)
    sem = ("parallel",) * (len(grid) - 1) + (("arbitrary",) if nk > 1 else ("parallel",))
    return pl.pallas_call(
        body, name=name, grid=grid, in_specs=[a_spec, b_spec],
        out_specs=o_spec, out_shape=jax.ShapeDtypeStruct(o_shape, out_dtype),
        scratch_shapes=[pltpu.VMEM(o_block, F32)] if nk > 1 else [],
        compiler_params=_params(sem),
    )(a, b)


def _rms_fwd(name, x, g):
    t, d = x.shape
    tm = min(ROW_TILE, t)

    def body(x_ref, g_ref, o_ref):
        xf = x_ref[...]
        r = lax.rsqrt(jnp.mean(xf * xf, axis=-1, keepdims=True) + RMS_EPS)
        o_ref[...] = (xf * r * g_ref[...]).astype(BF16)

    return pl.pallas_call(
        body, name=name, grid=(t // tm,),
        in_specs=[pl.BlockSpec((tm, d), lambda i: (i, 0)), pl.BlockSpec((1, d), lambda i: (0, 0))],
        out_specs=pl.BlockSpec((tm, d), lambda i: (i, 0)), out_shape=jax.ShapeDtypeStruct((t, d), BF16),
        compiler_params=_params(("parallel",)),
    )(x, g)


def _rms_bwd_rows(dnf, xf, gv, res):
    r = lax.rsqrt(jnp.mean(xf * xf, axis=-1, keepdims=True) + RMS_EPS)
    xh = xf * r
    dxh = dnf * gv
    dx = r * (dxh - xh * jnp.mean(dxh * xh, axis=-1, keepdims=True))
    if res is not None:
        dx = dx + res
    return dx, jnp.sum(dnf * xh, axis=0, keepdims=True)


def _rms_bwd(name, dn, x, g, dres, want):
    t, d = x.shape
    tm = min(ROW_TILE, t)
    has_res = dres is not None
    lhs = list(dn) if isinstance(dn, tuple) else [dn]
    n_lhs = len(lhs[:2])

    def body(*refs):
        x_ref, g_ref = refs[n_lhs], refs[n_lhs + 1]
        r_ref = refs[n_lhs + 2] if has_res else None
        dx_refs, dg_ref = refs[-1 - len(want):-1], refs[-1]
        if n_lhs == 2:
            dnf = lax.dot_general(refs[0][...], refs[1][...], lhs[2], preferred_element_type=F32)
        else:
            dnf = refs[0][...].astype(F32)
        dx, dg = _rms_bwd_rows(dnf, x_ref[...], g_ref[...], r_ref[...] if has_res else None)
        for kind, dx_ref in zip(want, dx_refs):
            dx_ref[...] = dx.astype(F32 if kind == "f32" else BF16)

        @pl.when(pl.program_id(0) == 0)
        def _():
            dg_ref[...] = jnp.zeros_like(dg_ref)

        dg_ref[...] += dg

    row = pl.BlockSpec((tm, d), lambda i: (i, 0))
    vec = pl.BlockSpec((1, d), lambda i: (0, 0))
    if n_lhs == 2:
        first = [pl.BlockSpec((tm, lhs[0].shape[1]), lambda i: (i, 0)), pl.BlockSpec(lhs[1].shape, lambda i: (0, 0))]
    else:
        first = [row]
    return pl.pallas_call(
        body, name=name, grid=(t // tm,),
        in_specs=first + [row, vec] + ([row] if has_res else []),
        out_specs=[row] * len(want) + [vec],
        out_shape=[jax.ShapeDtypeStruct((t, d), F32 if kind == "f32" else BF16) for kind in want]
        + [jax.ShapeDtypeStruct((1, d), F32)],
        compiler_params=_params(("arbitrary",)),
    )(*(lhs[:2] + [x, g] + ([dres] if has_res else [])))


def _loss_head(a, w, res, tgt, g):
    t, d = res.shape
    k = a.shape[1]
    tm = min(ROW_TILE, t)

    def body(a_ref, w_ref, r_ref, t_ref, g_ref, loss_ref, dh_ref, dhb_ref, dg_ref):
        xf = lax.dot_general(a_ref[...], w_ref[...], NN, preferred_element_type=F32) + r_ref[...]
        gv = g_ref[...]
        r = lax.rsqrt(jnp.mean(xf * xf, axis=-1, keepdims=True) + RMS_EPS)
        xh = xf * r
        e = xh * gv - t_ref[...]
        dy = e * (1.0 / d)
        dxh = dy * gv
        dh = r * (dxh - xh * jnp.mean(dxh * xh, axis=-1, keepdims=True))
        dh_ref[...] = dh
        dhb_ref[...] = dh.astype(BF16)

        @pl.when(pl.program_id(0) == 0)
        def _():
            dg_ref[...] = jnp.zeros_like(dg_ref)
            loss_ref[...] = jnp.zeros_like(loss_ref)

        dg_ref[...] += jnp.sum(dy * xh, axis=0, keepdims=True)
        part = jnp.sum(jnp.sum(e * e, axis=1, keepdims=True), axis=0, keepdims=True) * (0.5 / d)
        loss_ref[...] += jnp.broadcast_to(part, loss_ref.shape)

    row = pl.BlockSpec((tm, d), lambda i: (i, 0))
    vec = pl.BlockSpec((1, d), lambda i: (0, 0))
    return pl.pallas_call(
        body, name="loss_head", grid=(t // tm,),
        in_specs=[pl.BlockSpec((tm, k), lambda i: (i, 0)), pl.BlockSpec((k, d), lambda i: (0, 0)), row, row, vec],
        out_specs=[pl.BlockSpec((8, LANES), lambda i: (0, 0)), row, row, vec],
        out_shape=[jax.ShapeDtypeStruct((8, LANES), F32), jax.ShapeDtypeStruct((t, d), F32),
                   jax.ShapeDtypeStruct((t, d), BF16), jax.ShapeDtypeStruct((1, d), F32)],
        compiler_params=_params(("arbitrary",)),
    )(a, w, res, tgt, g)


def _rope_tables(pos, inv_freq, sel_lo, sel_hi):
    t = pos.shape[0]
    tm = min(ROW_TILE, t)

    def body(p_ref, f_ref, lo_ref, hi_ref, c_ref, sa_ref, sb_ref):
        ang = p_ref[...].astype(F32) * f_ref[...]
        rot = lo_ref[...] + hi_ref[...]
        cs, sn = jnp.cos(ang), jnp.sin(ang)
        c_ref[...] = cs * rot + (1.0 - rot)
        sa_ref[...] = -sn * lo_ref[...]
        sb_ref[...] = sn * hi_ref[...]

    vec = pl.BlockSpec((1, LANES), lambda i: (0, 0))
    row = pl.BlockSpec((tm, LANES), lambda i: (i, 0))
    return pl.pallas_call(
        body, name="rope_tables", grid=(t // tm,),
        in_specs=[pl.BlockSpec((tm, 1), lambda i: (i, 0)), vec, vec, vec],
        out_specs=[row, row, row], out_shape=[jax.ShapeDtypeStruct((t, LANES), F32)] * 3,
        compiler_params=_params(("parallel",)),
    )(pos, inv_freq, sel_lo, sel_hi)


def _rope_apply(name, srcs, width, cos_t, sin_a, sin_b, sign):
    t = srcs[0].shape[0]
    tm = min(ROW_TILE, t)
    n_cols = width // LANES

    def body(*refs):
        x_refs, (c_ref, sa_ref, sb_ref, o_ref) = refs[:len(srcs)], refs[len(srcs):]
        cs, sa, sb = c_ref[...], sign * sa_ref[...], sign * sb_ref[...]
        for a, x_ref in enumerate(x_refs):
            for c in range(n_cols):
                xf = x_ref[:, c * LANES:(c + 1) * LANES].astype(F32)
                up = pltpu.roll(xf, LANES - ROPE_HALF, 1)
                dn = pltpu.roll(xf, ROPE_HALF, 1)
                o_ref[:, a * width + c * LANES:a * width + (c + 1) * LANES] = (xf * cs + up * sa + dn * sb).astype(BF16)

    wide = len(srcs) * width
    tab = pl.BlockSpec((tm, LANES), lambda i: (i, 0))
    return pl.pallas_call(
        body, name=name, grid=(t // tm,),
        in_specs=[pl.BlockSpec((tm, width), lambda i: (i, 0))] * len(srcs) + [tab, tab, tab],
        out_specs=pl.BlockSpec((tm, wide), lambda i: (i, 0)),
        out_shape=jax.ShapeDtypeStruct((t, wide), BF16),
        compiler_params=_params(("parallel",)),
    )(*srcs, cos_t, sin_a, sin_b)


DA_T = 256
MIX_STREAMS = 4
SB_BWD_STREAMS = 2


def _lane_lo():
    return lax.broadcasted_iota(jnp.int32, (BLOCK, LANES), 1) < HEAD_DIM


def _dilated_bias_tiles(s):
    n = s // DA_T
    dist = (np.arange(n)[:, None, None] * DA_T + np.arange(DA_T)[None, :, None] - np.arange(DA_T)[None, None, :])
    cnt = np.zeros(dist.shape, np.float32)
    for window, dil in DIL_PATTERNS:
        cnt += ((dist >= 0) & (dist % dil == 0) & (dist <= window)).astype(np.float32)
    return jnp.asarray(np.where(cnt > 0, np.log(np.maximum(cnt, 1.0)), NEG).astype(np.float32))


def _stack_heads(x, lo):
    zero = jnp.zeros_like(x)
    return jnp.concatenate([jnp.where(lo, x, zero), jnp.where(lo, zero, x)], axis=0)


def _da_fwd(qk, proj, v_col0, bias, batch, s, ride=None, streams=MIX_STREAMS):
    t = qk.shape[0]
    nq = s // DA_T
    n_pairs = 4
    ns = streams
    wide = ns * LANES
    scale = HEAD_DIM ** -0.5

    def body(q_ref, k_ref, v_ref, b_ref, o_ref, lse_ref, acc_ref, m_ref, l_ref):
        i = pl.program_id(2)
        lo = lax.broadcasted_iota(jnp.int32, (DA_T, LANES), 1) < HEAD_DIM
        ones = jnp.ones((DA_T, LANES), BF16)
        acc_ref[...] = jnp.zeros_like(acc_ref)
        m_ref[...] = jnp.full(m_ref.shape, NEG, F32)
        l_ref[...] = jnp.zeros_like(l_ref)
        qqs = [_stack_heads(q_ref[:, st * LANES:(st + 1) * LANES] * scale, lo) for st in range(ns)]

        def scores(st, rows, bias2):
            k = k_ref[rows, st * LANES:(st + 1) * LANES]
            return lax.dot_general(qqs[st], k, NT, preferred_element_type=F32) + bias2

        def softmax(st, sc):
            m_old = m_ref[st]
            m_new = jnp.maximum(m_old, jnp.max(sc, axis=1, keepdims=True))
            m_ref[st] = m_new
            return jnp.exp(sc - m_new).astype(BF16), jnp.exp(m_old - m_new)

        def values(st, rows, p, alpha):
            v = v_ref[rows, st * LANES:(st + 1) * LANES]
            vz = jnp.zeros_like(v)
            l_ref[st] = alpha * l_ref[st] + lax.dot_general(p, ones, NN, preferred_element_type=F32)
            pv = (lax.dot_general(p[:DA_T], jnp.where(lo, v, vz), NN, preferred_element_type=F32)
                  + lax.dot_general(p[DA_T:], jnp.where(lo, vz, v), NN, preferred_element_type=F32))
            acc_ref[st] = acc_ref[st] * jnp.where(lo, alpha[:DA_T], alpha[DA_T:]) + pv

        def trip(dlt, carry):
            rows = pl.ds(pl.multiple_of((i - dlt) * DA_T, DA_T), DA_T)
            bias_t = b_ref[dlt]
            bias2 = jnp.concatenate([bias_t, bias_t], axis=0)
            scs = [scores(st, rows, bias2) for st in range(ns)]
            pas = [softmax(st, scs[st]) for st in range(ns)]
            for st in range(ns):
                values(st, rows, *pas[st])
            return carry

        lax.fori_loop(0, i + 1, trip, 0)
        for st in range(ns):
            cols = slice(st * LANES, (st + 1) * LANES)
            l_t = l_ref[st]
            o_ref[:, cols] = (acc_ref[st] / jnp.where(lo, l_t[:DA_T], l_t[DA_T:])).astype(BF16)
            lse = m_ref[st] + jnp.log(l_t)
            lse_ref[:, cols] = jnp.where(lo, lse[:DA_T], lse[DA_T:])

    blk = pl.BlockSpec((DA_T, wide), lambda b, h, i: (b * nq + i, h))
    return _call(
        body, name="attn_a_fwd", grid=(batch, n_pairs // ns, nq),
        in_specs=[blk,
                  pl.BlockSpec((s, wide), lambda b, h, i: (b, n_pairs // ns + h)),
                  pl.BlockSpec((s, wide), lambda b, h, i: (b, v_col0 // ns + h)),
                  pl.BlockSpec((nq, DA_T, DA_T), lambda b, h, i: (0, 0, 0))],
        out_specs=[blk, blk],
        out_shape=[jax.ShapeDtypeStruct((t, n_pairs * LANES), BF16), jax.ShapeDtypeStruct((t, n_pairs * LANES), F32)],
        scratch=[pltpu.VMEM((ns, DA_T, LANES), F32), pltpu.VMEM((ns, 2 * DA_T, 1), F32),
                 pltpu.VMEM((ns, 2 * DA_T, LANES), F32)],
        sem=("parallel", "parallel", "arbitrary"), args=(qk, qk, proj, bias), ride=ride)


def _da_bwd(qk, proj, v_col0, bias, o, lse, do, batch, s, ride=None, streams=MIX_STREAMS):
    t = qk.shape[0]
    nq = s // DA_T
    n_pairs = 4
    ns = streams
    wide = ns * LANES
    scale = HEAD_DIM ** -0.5

    def body(q_ref, k_ref, v_ref, b_ref, o_ref, lse_ref, do_ref, dq_ref, dk_ref, dv_ref, dk_acc, dv_acc, dq_acc):
        i = pl.program_id(2)
        lo = lax.broadcasted_iota(jnp.int32, (DA_T, LANES), 1) < HEAD_DIM

        @pl.when(i == 0)
        def _():
            dk_acc[...] = jnp.zeros_like(dk_acc)
            dv_acc[...] = jnp.zeros_like(dv_acc)

        dq_acc[...] = jnp.zeros_like(dq_acc)
        qqs, dds, deltas, lses = [], [], [], []
        for st in range(ns):
            cols = slice(st * LANES, (st + 1) * LANES)
            do_ = do_ref[:, cols]
            qqs.append(_stack_heads(q_ref[:, cols] * scale, lo))
            dds.append(_stack_heads(do_, lo))
            prod = do_.astype(F32) * o_ref[:, cols].astype(F32)
            fz = jnp.zeros_like(prod)
            deltas.append(jnp.concatenate([jnp.sum(jnp.where(lo, prod, fz), axis=1, keepdims=True),
                                           jnp.sum(jnp.where(lo, fz, prod), axis=1, keepdims=True)], axis=0))
            lse_t = lse_ref[:, cols]
            lses.append(jnp.concatenate([lse_t[:, 0:1], lse_t[:, HEAD_DIM:HEAD_DIM + 1]], axis=0))

        def products(st, rows, bias2):
            cols = slice(st * LANES, (st + 1) * LANES)
            sc = lax.dot_general(qqs[st], k_ref[rows, cols], NT, preferred_element_type=F32) + bias2
            return sc, lax.dot_general(dds[st], v_ref[rows, cols], NT, preferred_element_type=F32)

        def weights(st, sc, dp):
            p = jnp.exp(sc - lses[st])
            return (p * (dp - deltas[st])).astype(BF16), p.astype(BF16)

        def gradients(st, rows, ds, p):
            cols = slice(st * LANES, (st + 1) * LANES)
            k = k_ref[rows, cols]
            kz = jnp.zeros_like(k)
            dq_acc[st] += (lax.dot_general(ds[:DA_T], jnp.where(lo, k, kz), NN, preferred_element_type=F32)
                           + lax.dot_general(ds[DA_T:], jnp.where(lo, kz, k), NN, preferred_element_type=F32))
            dk_acc[rows, cols] += lax.dot_general(ds, qqs[st], TN, preferred_element_type=F32)
            dv_acc[rows, cols] += lax.dot_general(p, dds[st], TN, preferred_element_type=F32)

        def trip(dlt, carry):
            rows = pl.ds(pl.multiple_of((i - dlt) * DA_T, DA_T), DA_T)
            bias_t = b_ref[dlt]
            bias2 = jnp.concatenate([bias_t, bias_t], axis=0)
            prods = [products(st, rows, bias2) for st in range(ns)]
            wts = [weights(st, *prods[st]) for st in range(ns)]
            for st in range(ns):
                gradients(st, rows, *wts[st])
            return carry

        lax.fori_loop(0, i + 1, trip, 0)
        for st in range(ns):
            dq_ref[:, st * LANES:(st + 1) * LANES] = (dq_acc[st] * scale).astype(BF16)

        @pl.when(i == nq - 1)
        def _():
            dk_ref[...] = dk_acc[...].astype(BF16)
            dv_ref[...] = dv_acc[...].astype(BF16)

    blk = pl.BlockSpec((DA_T, wide), lambda b, h, i: (b * nq + i, h))
    seq = pl.BlockSpec((s, wide), lambda b, h, i: (b, h), pipeline_mode=pl.Buffered(1))
    one = pl.Buffered(1)
    out = jax.ShapeDtypeStruct((t, n_pairs * LANES), BF16)
    return _call(
        body, name="attn_a_bwd", grid=(batch, n_pairs // ns, nq),
        in_specs=[blk,
                  pl.BlockSpec((s, wide), lambda b, h, i: (b, n_pairs // ns + h), pipeline_mode=one),
                  pl.BlockSpec((s, wide), lambda b, h, i: (b, v_col0 // ns + h), pipeline_mode=one),
                  pl.BlockSpec((nq, DA_T, DA_T), lambda b, h, i: (0, 0, 0), pipeline_mode=one),
                  blk, blk, blk],
        out_specs=[blk, seq, seq], out_shape=[out, out, out],
        scratch=[pltpu.VMEM((s, wide), F32), pltpu.VMEM((s, wide), F32), pltpu.VMEM((ns, DA_T, LANES), F32)],
        sem=("parallel", "parallel", "arbitrary"), args=(qk, qk, proj, bias, o, lse, do), ride=ride)


SB_Q = 256


def _sb_consts(after):
    r = lax.broadcasted_iota(jnp.int32, (2 * BLOCK, 2 * BLOCK), 0) % BLOCK
    c = lax.broadcasted_iota(jnp.int32, (2 * BLOCK, 2 * BLOCK), 1)
    tri = (r > c) if after else (r < c)
    return jnp.logical_or(c >= BLOCK, tri).astype(BF16)


def _split(x):
    hi = x.astype(BF16)
    lo = (x - hi.astype(F32)).astype(BF16)
    return jnp.concatenate([hi, lo], axis=1)


def _sb_fwd(proj, q_col0, k_col0, v_col0, batch, s, ride=None, streams=MIX_STREAMS):
    t = proj.shape[0]
    nq = s // SB_Q
    n_pairs = 4
    ns = streams
    wide = ns * LANES
    scale = HEAD_DIM ** -0.5

    def body(q_ref, k_ref, v_ref, o_ref, tot_ref, acc_ref, run_ref):
        i = pl.program_id(2)
        lo_q = lax.broadcasted_iota(jnp.int32, (SB_Q, LANES), 1) < HEAD_DIM
        lo_k = _lane_lo()
        mat = _sb_consts(True)
        row = lax.broadcasted_iota(jnp.int32, (2 * SB_Q, LANES), 0) % SB_Q
        ahead = row - lax.broadcasted_iota(jnp.int32, (2 * SB_Q, LANES), 1)
        acc_ref[...] = jnp.zeros_like(acc_ref)
        run_ref[...] = jnp.zeros_like(run_ref)
        qqs = [_stack_heads(q_ref[:, st * LANES:(st + 1) * LANES] * scale, lo_q) for st in range(ns)]

        def units(todo):
            def rows(j):
                return pl.ds(pl.multiple_of(j * BLOCK, BLOCK), BLOCK)

            zs = [lax.dot_general(qqs[st], k_ref[rows(j), st * LANES:(st + 1) * LANES], NT, preferred_element_type=F32)
                  for st, j, _ in todo]
            logs = []
            for z, (_, _, off) in zip(zs, todo):
                lsig = jnp.minimum(z, 0.0) - jnp.log(1.0 + jnp.exp(-jnp.abs(z)))
                lneg = lsig - z
                if off is not None:
                    lneg = jnp.where(ahead > off, lneg, 0.0)
                logs.append((lsig, _split(lneg)))
            sums = [lax.dot_general(cat, mat, NN, preferred_element_type=F32) for _, cat in logs]
            probs = []
            for (lsig, _), sm, (st, _, off) in zip(logs, sums, todo):
                run = run_ref[st]
                a = jnp.exp(lsig + run + sm[:, :BLOCK])
                if off is not None:
                    a = jnp.where(ahead > off, a, 0.0)
                run_ref[st] = run + sm[:, BLOCK:]
                probs.append(a.astype(BF16))
            for ab, (st, j, _) in zip(probs, todo):
                v = v_ref[rows(j), st * LANES:(st + 1) * LANES]
                vz = jnp.zeros_like(v)
                acc_ref[st] += (lax.dot_general(ab[:SB_Q], jnp.where(lo_k, v, vz), NN, preferred_element_type=F32)
                                + lax.dot_general(ab[SB_Q:], jnp.where(lo_k, vz, v), NN, preferred_element_type=F32))

        units([(st, 2 * i + 1, BLOCK) for st in range(ns)] + [(st, 2 * i, 0) for st in range(ns)])

        def pair(p, carry):
            jp = i - 1 - p
            units([(st, 2 * jp + 1, None) for st in range(ns)] + [(st, 2 * jp, None) for st in range(ns)])
            return carry

        lax.fori_loop(0, i, pair, 0)
        for st in range(ns):
            cols = slice(st * LANES, (st + 1) * LANES)
            o_ref[:, cols] = acc_ref[st].astype(BF16)
            tot_ref[:, cols] = jnp.where(lo_q, run_ref[st, 0:SB_Q, :], run_ref[st, SB_Q:2 * SB_Q, :])

    def seq(col0):
        return pl.BlockSpec((s, wide), lambda b, h, i: (b, col0 // ns + h))

    blk = pl.BlockSpec((SB_Q, wide), lambda b, h, i: (b * nq + i, h))
    return _call(
        body, name="attn_b_fwd", grid=(batch, n_pairs // ns, nq),
        in_specs=[pl.BlockSpec((SB_Q, wide), lambda b, h, i: (b * nq + i, q_col0 // ns + h)), seq(k_col0), seq(v_col0)],
        out_specs=[blk, blk],
        out_shape=[jax.ShapeDtypeStruct((t, n_pairs * LANES), BF16), jax.ShapeDtypeStruct((t, n_pairs * LANES), F32)],
        scratch=[pltpu.VMEM((ns, SB_Q, LANES), F32), pltpu.VMEM((ns, 2 * SB_Q, LANES), F32)],
        sem=("parallel", "parallel", "arbitrary"), args=(proj, proj, proj), ride=ride)


def _sb_bwd(proj, q_col0, k_col0, v_col0, tot, do, batch, s, ride=None, streams=SB_BWD_STREAMS):
    t = proj.shape[0]
    nq = s // SB_Q
    n_pairs = 4
    ns = streams
    wide = ns * LANES
    scale = HEAD_DIM ** -0.5

    def body(q_ref, k_ref, v_ref, tot_ref, do_ref, dq_ref, dk_ref, dv_ref, dk_acc, dv_acc, dq_acc, seen_ref, gsum_ref):
        i = pl.program_id(2)
        lo_q = lax.broadcasted_iota(jnp.int32, (SB_Q, LANES), 1) < HEAD_DIM
        lo_k = _lane_lo()

        @pl.when(i == 0)
        def _():
            dk_acc[...] = jnp.zeros_like(dk_acc)
            dv_acc[...] = jnp.zeros_like(dv_acc)

        mat_after = _sb_consts(True)
        mat_before = _sb_consts(False)[:BLOCK]
        row = lax.broadcasted_iota(jnp.int32, (2 * SB_Q, LANES), 0) % SB_Q
        ahead = row - lax.broadcasted_iota(jnp.int32, (2 * SB_Q, LANES), 1)
        dq_acc[...] = jnp.zeros_like(dq_acc)
        seen_ref[...] = jnp.zeros_like(seen_ref)
        gsum_ref[...] = jnp.zeros_like(gsum_ref)
        qqs, dds, totals = [], [], []
        for st in range(ns):
            cols = slice(st * LANES, (st + 1) * LANES)
            qqs.append(_stack_heads(q_ref[:, cols] * scale, lo_q))
            dds.append(_stack_heads(do_ref[:, cols], lo_q))
            tot_t = tot_ref[:, cols]
            totals.append(jnp.concatenate([jnp.broadcast_to(tot_t[:, 0:1], (SB_Q, LANES)),
                                           jnp.broadcast_to(tot_t[:, HEAD_DIM:HEAD_DIM + 1], (SB_Q, LANES))], axis=0))

        def units(todo):
            def rows(j):
                return pl.ds(pl.multiple_of(j * BLOCK, BLOCK), BLOCK)

            def cols(st):
                return slice(st * LANES, (st + 1) * LANES)

            prods = [(lax.dot_general(qqs[st], k_ref[rows(j), cols(st)], NT, preferred_element_type=F32),
                      lax.dot_general(dds[st], v_ref[rows(j), cols(st)], NT, preferred_element_type=F32))
                     for st, j, _ in todo]
            logs = []
            for (z, _), (_, _, off) in zip(prods, todo):
                lsig = jnp.minimum(z, 0.0) - jnp.log(1.0 + jnp.exp(-jnp.abs(z)))
                lneg = lsig - z
                if off is not None:
                    lneg = jnp.where(ahead > off, lneg, 0.0)
                logs.append((lsig, _split(lneg)))
            sums = [lax.dot_general(cat, mat_after, NN, preferred_element_type=F32) for _, cat in logs]
            gates = []
            for (lsig, _), sm, (_, da), (st, _, off) in zip(logs, sums, prods, todo):
                seen = seen_ref[st]
                a = jnp.exp(lsig + (totals[st] - seen - sm[:, BLOCK:]) + sm[:, :BLOCK])
                if off is not None:
                    a = jnp.where(ahead > off, a, 0.0)
                seen_ref[st] = seen + sm[:, BLOCK:]
                g = a * da
                gates.append((a.astype(BF16), g, g.astype(BF16)))
            gsums = [lax.dot_general(cat, mat_before, NN, preferred_element_type=F32) for _, _, cat in gates]
            outs = []
            for (lsig, _), (ab, g, _), gs, (st, _, off) in zip(logs, gates, gsums, todo):
                gsum = gsum_ref[st]
                dz = g - jnp.exp(lsig) * (g + gsum + gs[:, :BLOCK])
                if off is not None:
                    dz = jnp.where(ahead > off, dz, 0.0)
                gsum_ref[st] = gsum + gs[:, BLOCK:]
                outs.append((dz.astype(BF16), ab))
            for (dzb, ab), (st, j, _) in zip(outs, todo):
                k = k_ref[rows(j), cols(st)]
                kz = jnp.zeros_like(k)
                dq_acc[st] += (lax.dot_general(dzb[:SB_Q], jnp.where(lo_k, k, kz), NN, preferred_element_type=F32)
                               + lax.dot_general(dzb[SB_Q:], jnp.where(lo_k, kz, k), NN, preferred_element_type=F32))
                dk_acc[rows(j), cols(st)] += lax.dot_general(dzb, qqs[st], TN, preferred_element_type=F32)
                dv_acc[rows(j), cols(st)] += lax.dot_general(ab, dds[st], TN, preferred_element_type=F32)

        def pair(p, carry):
            units([(st, 2 * p, None) for st in range(ns)] + [(st, 2 * p + 1, None) for st in range(ns)])
            return carry

        lax.fori_loop(0, i, pair, 0)
        units([(st, 2 * i, 0) for st in range(ns)] + [(st, 2 * i + 1, BLOCK) for st in range(ns)])
        for st in range(ns):
            dq_ref[:, st * LANES:(st + 1) * LANES] = (dq_acc[st] * scale).astype(BF16)

        @pl.when(i == nq - 1)
        def _():
            dk_ref[...] = dk_acc[...].astype(BF16)
            dv_ref[...] = dv_acc[...].astype(BF16)

    def seq_in(col0):
        return pl.BlockSpec((s, wide), lambda b, h, i: (b, col0 // ns + h))

    blk = pl.BlockSpec((SB_Q, wide), lambda b, h, i: (b * nq + i, h))
    seq = pl.BlockSpec((s, wide), lambda b, h, i: (b, h))
    out = jax.ShapeDtypeStruct((t, n_pairs * LANES), BF16)
    return _call(
        body, name="attn_b_bwd", grid=(batch, n_pairs // ns, nq),
        in_specs=[pl.BlockSpec((SB_Q, wide), lambda b, h, i: (b * nq + i, q_col0 // ns + h)), seq_in(k_col0),
                  seq_in(v_col0), blk, blk],
        out_specs=[blk, seq, seq], out_shape=[out, out, out],
        scratch=[pltpu.VMEM((s, wide), F32), pltpu.VMEM((s, wide), F32), pltpu.VMEM((ns, SB_Q, LANES), F32),
                 pltpu.VMEM((ns, 2 * SB_Q, LANES), F32), pltpu.VMEM((ns, 2 * SB_Q, LANES), F32)],
        sem=("parallel", "parallel", "arbitrary"), args=(proj, proj, proj, tot, do), ride=ride)


MEM_Q_TILE = 512


def _mem_fwd(q, kv, batch, s, n_mem):
    t, width = q.shape
    tq = min(MEM_Q_TILE, s)
    nq = s // tq
    scale = MEM_HEAD_DIM ** -0.5

    def body(q_ref, kv_ref, o_ref):
        for h in range(N_HEADS_MEM):
            cols = slice(h * MEM_HEAD_DIM, (h + 1) * MEM_HEAD_DIM)
            k = kv_ref[:, cols]
            v = kv_ref[:, width + h * MEM_HEAD_DIM: width + (h + 1) * MEM_HEAD_DIM]
            sc = lax.dot_general(q_ref[:, cols], k, NT, preferred_element_type=F32) * scale
            p = jnp.exp(sc - jnp.max(sc, axis=1, keepdims=True))
            p = p / jnp.sum(p, axis=1, keepdims=True)
            o_ref[:, cols] = lax.dot_general(p.astype(BF16), v, NN, preferred_element_type=F32).astype(BF16)

    return pl.pallas_call(
        body, name="mem_attn_fwd", grid=(batch, nq),
        in_specs=[pl.BlockSpec((tq, width), lambda b, i: (b * nq + i, 0)),
                  pl.BlockSpec((n_mem, 2 * width), lambda b, i: (b, 0))],
        out_specs=pl.BlockSpec((tq, width), lambda b, i: (b * nq + i, 0)),
        out_shape=jax.ShapeDtypeStruct((t, width), BF16),
        compiler_params=_params(("parallel", "parallel")),
    )(q, kv)


def _mem_bwd(q, kv, do, batch, s, n_mem):
    t, width = q.shape
    tq = min(MEM_Q_TILE, s)
    nq = s // tq
    scale = MEM_HEAD_DIM ** -0.5

    def body(q_ref, kv_ref, do_ref, dq_ref, dkv_ref, acc):
        i = pl.program_id(1)

        @pl.when(i == 0)
        def _():
            acc[...] = jnp.zeros_like(acc)

        for h in range(N_HEADS_MEM):
            cols = slice(h * MEM_HEAD_DIM, (h + 1) * MEM_HEAD_DIM)
            vcols = slice(width + h * MEM_HEAD_DIM, width + (h + 1) * MEM_HEAD_DIM)
            qh, k, v, doh = q_ref[:, cols], kv_ref[:, cols], kv_ref[:, vcols], do_ref[:, cols]
            sc = lax.dot_general(qh, k, NT, preferred_element_type=F32) * scale
            p = jnp.exp(sc - jnp.max(sc, axis=1, keepdims=True))
            p = p / jnp.sum(p, axis=1, keepdims=True)
            dp = lax.dot_general(doh, v, NT, preferred_element_type=F32)
            ds = (p * (dp - jnp.sum(p * dp, axis=1, keepdims=True)) * scale).astype(BF16)
            dq_ref[:, cols] = lax.dot_general(ds, k, NN, preferred_element_type=F32).astype(BF16)
            acc[:, cols] += lax.dot_general(ds, qh, TN, preferred_element_type=F32)
            acc[:, vcols] += lax.dot_general(p.astype(BF16), doh, TN, preferred_element_type=F32)

        @pl.when(i == nq - 1)
        def _():
            dkv_ref[...] = acc[...].astype(BF16)

    row = pl.BlockSpec((tq, width), lambda b, i: (b * nq + i, 0))
    kvs = pl.BlockSpec((n_mem, 2 * width), lambda b, i: (b, 0))
    return pl.pallas_call(
        body, name="mem_attn_bwd", grid=(batch, nq),
        in_specs=[row, kvs, row], out_specs=[row, kvs],
        out_shape=[jax.ShapeDtypeStruct((t, width), BF16), jax.ShapeDtypeStruct((batch * n_mem, 2 * width), BF16)],
        scratch_shapes=[pltpu.VMEM((n_mem, 2 * width), F32)],
        compiler_params=_params(("parallel", "arbitrary")),
    )(q, kv, do)


def _mixer_fwd(o_a, o_b, w_a, w_b, proj, gate_col0, w_out, x, g, w_q):
    t, width = o_a.shape
    d = w_a.shape[1]
    nq_cols = w_q.shape[1]
    tm = min(ROW_TILE, t)
    gb0 = gate_col0 * LANES // d

    def body(oa_ref, ob_ref, wa_ref, wb_ref, ga_ref, gb_ref, wo_ref, x_ref, g_ref, wq_ref, ua_ref, ub_ref, mix_ref,
             n_ref, h_ref, q_ref):
        ua = lax.dot_general(oa_ref[...], wa_ref[...], NN, preferred_element_type=F32)
        ub = lax.dot_general(ob_ref[...], wb_ref[...], NN, preferred_element_type=F32)
        ua_ref[...] = ua.astype(BF16)
        ub_ref[...] = ub.astype(BF16)
        mixed = (jax.nn.sigmoid(ga_ref[...].astype(F32)) * ua + jax.nn.sigmoid(gb_ref[...].astype(F32)) * ub).astype(BF16)
        mix_ref[...] = mixed
        h = lax.dot_general(mixed, wo_ref[...], NN, preferred_element_type=F32) + x_ref[...]
        h_ref[...] = h
        r = lax.rsqrt(jnp.mean(h * h, axis=-1, keepdims=True) + RMS_EPS)
        n = (h * r * g_ref[...]).astype(BF16)
        n_ref[...] = n
        q_ref[...] = lax.dot_general(n, wq_ref[...], NN, preferred_element_type=F32).astype(BF16)

    row = pl.BlockSpec((tm, width), lambda i: (i, 0))
    wsp = pl.BlockSpec((width, d), lambda i: (0, 0))
    out = pl.BlockSpec((tm, d), lambda i: (i, 0))
    osh = jax.ShapeDtypeStruct((t, d), BF16)
    return pl.pallas_call(
        body, name="mixer_fwd", grid=(t // tm,),
        in_specs=[row, row, wsp, wsp,
                  pl.BlockSpec((tm, d), lambda i: (i, gb0)), pl.BlockSpec((tm, d), lambda i: (i, gb0 + 1)),
                  pl.BlockSpec((d, d), lambda i: (0, 0)), out, pl.BlockSpec((1, d), lambda i: (0, 0)),
                  pl.BlockSpec((d, nq_cols), lambda i: (0, 0))],
        out_specs=[out, out, out, out, out, pl.BlockSpec((tm, nq_cols), lambda i: (i, 0))],
        out_shape=[osh, osh, osh, osh, jax.ShapeDtypeStruct((t, d), F32), jax.ShapeDtypeStruct((t, nq_cols), BF16)],
        compiler_params=_params(("parallel",)),
    )(o_a, o_b, w_a, w_b, proj, proj, w_out, x, g, w_q)


def _mixer_bwd(dh, w_out, ua, ub, proj, gate_col0, w_a, w_b):
    t, d = dh.shape
    width = w_a.shape[0]
    tm = min(ROW_TILE, t)
    nc = d // LANES

    def body(dh_ref, w_ref, ua_ref, ub_ref, ga_ref, gb_ref, wa_ref, wb_ref, dua_ref, dub_ref, dg_ref, doa_ref, dob_ref):
        dm = lax.dot_general(dh_ref[...], w_ref[...], NT, preferred_element_type=F32)
        sa = jax.nn.sigmoid(ga_ref[...].astype(F32))
        sb = jax.nn.sigmoid(gb_ref[...].astype(F32))
        dua = (dm * sa).astype(BF16)
        dub = (dm * sb).astype(BF16)
        dua_ref[...] = dua
        dub_ref[...] = dub
        dg_ref[:, 0:d] = (dm * ua_ref[...].astype(F32) * sa * (1.0 - sa)).astype(BF16)
        dg_ref[:, d:2 * d] = (dm * ub_ref[...].astype(F32) * sb * (1.0 - sb)).astype(BF16)
        doa_ref[...] = lax.dot_general(dua, wa_ref[...], NT, preferred_element_type=F32).astype(BF16)
        dob_ref[...] = lax.dot_general(dub, wb_ref[...], NT, preferred_element_type=F32).astype(BF16)

    row = pl.BlockSpec((tm, d), lambda i: (i, 0))
    wsp = pl.BlockSpec((width, d), lambda i: (0, 0))
    osp = pl.BlockSpec((tm, width), lambda i: (i, 0))
    return pl.pallas_call(
        body, name="mixer_bwd", grid=(t // tm,),
        in_specs=[row, pl.BlockSpec((d, d), lambda i: (0, 0)), row, row,
                  pl.BlockSpec((tm, d), lambda i: (i, gate_col0 // nc)),
                  pl.BlockSpec((tm, d), lambda i: (i, gate_col0 // nc + 1)), wsp, wsp],
        out_specs=[row, row, pl.BlockSpec((tm, 2 * d), lambda i: (i, 0)), osp, osp],
        out_shape=[jax.ShapeDtypeStruct((t, d), BF16), jax.ShapeDtypeStruct((t, d), BF16),
                   jax.ShapeDtypeStruct((t, 2 * d), BF16), jax.ShapeDtypeStruct((t, width), BF16),
                   jax.ShapeDtypeStruct((t, width), BF16)],
        compiler_params=_params(("parallel",)),
    )(dh, w_out, ua, ub, proj, proj, w_a, w_b)


FFN_COLS = 1024


def _ffn_up(n, w_gate, w_up):
    t, d = n.shape
    hidden = w_gate.shape[0]
    tm = min(ROW_TILE, t)
    tn = min(FFN_COLS, hidden)

    def body(n_ref, wg_ref, wu_ref, hg_ref, hu_ref, act_ref):
        hg = lax.dot_general(n_ref[...], wg_ref[...], NT, preferred_element_type=F32)
        hu = lax.dot_general(n_ref[...], wu_ref[...], NT, preferred_element_type=F32)
        hg_ref[...] = hg.astype(BF16)
        hu_ref[...] = hu.astype(BF16)
        act_ref[...] = (hg * jax.nn.sigmoid(hg) * hu).astype(BF16)

    wsp = pl.BlockSpec((tn, d), lambda j, i: (j, 0))
    out = pl.BlockSpec((tm, tn), lambda j, i: (i, j))
    osh = jax.ShapeDtypeStruct((t, hidden), BF16)
    return pl.pallas_call(
        body, name="ffn_up", grid=(hidden // tn, t // tm),
        in_specs=[pl.BlockSpec((tm, d), lambda j, i: (i, 0)), wsp, wsp],
        out_specs=[out, out, out], out_shape=[osh, osh, osh],
        compiler_params=_params(("parallel", "parallel")),
    )(n, w_gate, w_up)


def _ffn_bwd(dh, w_down, w_gate, w_up, hg, hu, x, g, dres, w_prev):
    t, d = dh.shape
    hidden = w_down.shape[0]
    q = w_prev.shape[0]
    tm = min(ROW_TILE, t)
    tn = min(FFN_COLS, hidden)
    nj = hidden // tn

    def body(dh_ref, wd_ref, wg_ref, wu_ref, hg_ref, hu_ref, x_ref, g_ref, r_ref, wp_ref, dhg_ref, dhu_ref, dx_ref,
             dxb_ref, dg_ref, do_ref, acc):
        j, i = pl.program_id(0), pl.program_id(1)
        dact = lax.dot_general(dh_ref[...], wd_ref[...], NT, preferred_element_type=F32)
        hg = hg_ref[...].astype(F32)
        sg = jax.nn.sigmoid(hg)
        dhu = (dact * hg * sg).astype(BF16)
        dhg = (dact * hu_ref[...].astype(F32) * sg * (1.0 + hg * (1.0 - sg))).astype(BF16)
        dhu_ref[...] = dhu
        dhg_ref[...] = dhg
        part = (lax.dot_general(dhg, wg_ref[...], NN, preferred_element_type=F32)
                + lax.dot_general(dhu, wu_ref[...], NN, preferred_element_type=F32))

        @pl.when(j == 0)
        def _():
            acc[i] = part

        @pl.when(j > 0)
        def _():
            acc[i] += part

        @pl.when(jnp.logical_and(j == 0, i == 0))
        def _():
            dg_ref[...] = jnp.zeros_like(dg_ref)

        @pl.when(j == nj - 1)
        def _():
            dx, dg = _rms_bwd_rows(acc[i], x_ref[...], g_ref[...], r_ref[...])
            dx_ref[...] = dx
            dxb = dx.astype(BF16)
            dxb_ref[...] = dxb
            dg_ref[...] += dg
            do_ref[...] = lax.dot_general(dxb, wp_ref[...], NT, preferred_element_type=F32).astype(BF16)

    hid = pl.BlockSpec((tm, tn), lambda j, i: (i, j))
    wsp = pl.BlockSpec((tn, d), lambda j, i: (j, 0), pipeline_mode=pl.Buffered(1))
    late = pl.BlockSpec((tm, d), lambda j, i: (jnp.where(j == nj - 1, i, 0), 0))
    late_q = pl.BlockSpec((tm, q), lambda j, i: (jnp.where(j == nj - 1, i, 0), 0))
    vec = pl.BlockSpec((1, d), lambda j, i: (0, 0))
    osh = jax.ShapeDtypeStruct((t, hidden), BF16)
    return pl.pallas_call(
        body, name="ffn_bwd", grid=(nj, t // tm),
        in_specs=[pl.BlockSpec((tm, d), lambda j, i: (i, 0)), wsp, wsp, wsp, hid, hid, late, vec, late,
                  pl.BlockSpec((q, d), lambda j, i: (0, 0), pipeline_mode=pl.Buffered(1))],
        out_specs=[hid, hid, late, late, vec, late_q],
        out_shape=[osh, osh, jax.ShapeDtypeStruct((t, d), F32), jax.ShapeDtypeStruct((t, d), BF16),
                   jax.ShapeDtypeStruct((1, d), F32), jax.ShapeDtypeStruct((t, q), BF16)],
        scratch_shapes=[pltpu.VMEM((t // tm, tm, d), F32)],
        compiler_params=_params(("arbitrary", "arbitrary")),
    )(dh, w_down, w_gate, w_up, hg, hu, x, g, dres, w_prev)


MM_ROWS = 1024


def _mm_w(name, a, w, out_dtype, dims=NN):
    t, k = a.shape
    n = w.shape[1] if dims == NN else w.shape[0]
    tm, tn = min(MM_ROWS, t), min(1024, n)
    o_spec = pl.BlockSpec((tm, tn), lambda j, i: (i, j))
    b_spec = pl.BlockSpec((k, tn), lambda j, i: (0, j)) if dims == NN else pl.BlockSpec((tn, k), lambda j, i: (j, 0))
    return _mm(name, a, w, grid=(n // tn, t // tm), a_spec=pl.BlockSpec((tm, k), lambda j, i: (i, 0)), b_spec=b_spec,
               o_shape=(t, n), o_spec=o_spec, dims=dims, out_dtype=out_dtype)


def _mm_res_norm(name, a, w, res, g):
    t, k = a.shape
    d = w.shape[1]
    tm = min(ROW_TILE, t)

    def body(a_ref, w_ref, r_ref, g_ref, h_ref, n_ref):
        h = lax.dot_general(a_ref[...], w_ref[...], NN, preferred_element_type=F32) + r_ref[...]
        h_ref[...] = h
        r = lax.rsqrt(jnp.mean(h * h, axis=-1, keepdims=True) + RMS_EPS)
        n_ref[...] = (h * r * g_ref[...]).astype(BF16)

    row = pl.BlockSpec((tm, d), lambda i: (i, 0))
    return pl.pallas_call(
        body, name=name, grid=(t // tm,),
        in_specs=[pl.BlockSpec((tm, k), lambda i: (i, 0)), pl.BlockSpec((k, d), lambda i: (0, 0)), row,
                  pl.BlockSpec((1, d), lambda i: (0, 0))],
        out_specs=[row, row], out_shape=[jax.ShapeDtypeStruct((t, d), F32), jax.ShapeDtypeStruct((t, d), BF16)],
        compiler_params=_params(("parallel",)),
    )(a, w, res, g)


def _wgrad(name, a, g, tk=1024, tn=1024):
    t, k = a.shape
    n = g.shape[1]
    tm, tk, tn = min(2 * MM_ROWS, t), min(tk, k), min(tn, n)
    return _mm(name, a, g, grid=(k // tk, n // tn, t // tm),
               a_spec=pl.BlockSpec((tm, tk), lambda p, q, r: (r, p)), b_spec=pl.BlockSpec((tm, tn), lambda p, q, r: (r, q)),
               o_shape=(k, n), o_spec=pl.BlockSpec((tk, tn), lambda p, q, r: (p, q)), dims=TN, out_dtype=BF16, nk=t // tm)


def _peers():
    x, y, c = lax.axis_index("x"), lax.axis_index("y"), lax.axis_index("c")
    me = 4 * x + 2 * y + c
    out = []
    for k in range(1, N_DEV):
        kx, ky, kc = (k >> 2) & 1, (k >> 1) & 1, k & 1
        px = 1 - x if kx else x
        py = 1 - y if ky else y
        pc = 1 - c if kc else c
        out.append(((px, py, pc), 4 * px + 2 * py + pc))
    return me, out


def _cast_weights(ws, pad_rows):
    def body(*refs):
        n = len(refs) // 2
        for i_ref, o_ref, pr in zip(refs[:n], refs[n:], pad_rows):
            r, c = i_ref.shape
            o_ref[0:r, :] = i_ref[...].astype(BF16)
            if pr:
                o_ref[r:r + pr, :] = jnp.zeros((pr, c), BF16)

    return pl.pallas_call(
        body, name="cast_weights", in_specs=[VMEM] * len(ws), out_specs=[VMEM] * len(ws),
        out_shape=[jax.ShapeDtypeStruct((w.shape[0] + pr, w.shape[1]), BF16) for w, pr in zip(ws, pad_rows)],
    )(*ws)


def _window(ref, j, c):
    return ref.at[:, pl.ds(pl.multiple_of(j * c, LANES), c)]


def _scatter_copies(ins, outs, sems, cols, landed):
    send_sems, recv_sems, loc_sems = sems
    n_peer = N_DEV - 1
    me, peers = _peers()

    def src(w, j):
        return _window(ins[w], j, cols[w]) if cols[w] else ins[w].at[j]

    local = [pltpu.make_async_copy(src(w, me), outs[w].at[me], loc_sems.at[w]) for w in range(len(ins))]
    remote = [pltpu.make_async_remote_copy(
        src_ref=src(w, idx), dst_ref=outs[w].at[idx if landed else me],
        send_sem=send_sems.at[w * n_peer + k], recv_sem=recv_sems.at[w * n_peer + k],
        device_id=dev, device_id_type=pl.DeviceIdType.MESH)
        for k, (dev, idx) in reversed(list(enumerate(peers))) for w in range(len(ins))]
    return local, remote


OTHER_CHIPS = (2, 4, 6)


def _gather_copies(ins, outs, sems, cols):
    send_sems, recv_sems, loc_sems = sems
    x, y, c = lax.axis_index("x"), lax.axis_index("y"), lax.axis_index("c")
    me = 4 * x + 2 * y + c
    n_pair = N_DEV - 1

    def dev(mask):
        return (1 - x if mask & 4 else x, 1 - y if mask & 2 else y, 1 - c if mask & 1 else c)

    def slot(w, mask):
        j = jnp.bitwise_xor(me, mask)
        return _window(outs[w], j, cols[w]) if cols[w] else outs[w].at[j]

    def remote(w, pair, src, to_slot, target):
        return pltpu.make_async_remote_copy(src_ref=src, dst_ref=slot(w, to_slot), send_sem=send_sems.at[w * n_pair + pair],
                                            recv_sem=recv_sems.at[w * n_pair + pair], device_id=dev(target),
                                            device_id_type=pl.DeviceIdType.MESH)

    ws = range(len(ins))
    return dict(
        local=[pltpu.make_async_copy(ins[w], slot(w, 0), loc_sems.at[w]) for w in ws],
        to_chips=[remote(w, 1 + t, ins[w], 0, m) for t, m in enumerate(OTHER_CHIPS) for w in ws],
        to_core=[remote(w, 0, ins[w], 0, 1) for w in ws],
        from_chips=[remote(w, 1 + t, ins[w], m, 0) for t, m in enumerate(OTHER_CHIPS) for w in ws],
        pass_on=[remote(w, 4 + t, slot(w, m), m, 1) for t, m in enumerate(OTHER_CHIPS) for w in ws],
        from_core=[remote(w, 0, ins[w], 1, 0) for w in ws]
        + [remote(w, 4 + t, ins[w], m + 1, 0) for t, m in enumerate(OTHER_CHIPS) for w in ws])


def _exchange_start(ins, outs, sems, gather, cols):
    if gather:
        cps = _gather_copies(ins, outs, sems, cols)
        for cp in cps["local"] + cps["to_chips"] + cps["to_core"]:
            cp.start()
    else:
        local, remote = _scatter_copies(ins, outs, sems, cols, False)
        for cp in local + remote:
            cp.start()


def _exchange_pass_on(ins, outs, sems, gather, cols, chips):
    if gather:
        cps = _gather_copies(ins, outs, sems, cols)
        n = len(ins)
        for t in chips:
            for arrived, onward in zip(cps["from_chips"][t * n:(t + 1) * n], cps["pass_on"][t * n:(t + 1) * n]):
                arrived.wait_recv()
                onward.start()


def _exchange_wait(ins, outs, sems, gather, cols):
    if gather:
        cps = _gather_copies(ins, outs, sems, cols)
        for cp in cps["local"]:
            cp.wait()
        for cp in cps["to_chips"] + cps["to_core"] + cps["pass_on"]:
            cp.wait_send()
        for cp in cps["from_core"]:
            cp.wait_recv()
    else:
        local, remote = _scatter_copies(ins, outs, sems, cols, True)
        for cp in local:
            cp.wait()
        for cp in remote:
            cp.wait_send()
            cp.wait_recv()


def _exchange_shapes(arrs, gather, cols):
    n = len(arrs)
    out_shape = []
    for a, c in zip(arrs, cols):
        if gather:
            shape = (a.shape[0], N_DEV * c) if c else (N_DEV,) + a.shape
        else:
            shape = (N_DEV, a.shape[0], c) if c else a.shape
        out_shape.append(jax.ShapeDtypeStruct(shape, a.dtype))
    sems = [pltpu.SemaphoreType.DMA((n * (N_DEV - 1),)), pltpu.SemaphoreType.DMA((n * (N_DEV - 1),)),
            pltpu.SemaphoreType.DMA((n,))]
    return out_shape, sems


def _call(body, *, name, grid, in_specs, out_specs, out_shape, scratch, sem, args, ride=None):
    if ride is None:
        outs = pl.pallas_call(body, name=name, grid=grid, in_specs=in_specs, out_specs=out_specs, out_shape=out_shape,
                              scratch_shapes=scratch, compiler_params=_params(sem))(*args)
        return outs, None
    arrs, gather, cols = ride
    n, n_in, n_out, n_scr = len(arrs), len(in_specs), len(out_specs), len(scratch)
    x_shape, x_sems = _exchange_shapes(arrs, gather, cols)

    def riding(*refs):
        ins, x_ins = refs[:n_in], refs[n_in:n_in + n]
        outs = refs[n_in + n:n_in + n + n_out]
        x_outs = refs[n_in + n + n_out:n_in + 2 * n + n_out]
        scr = refs[n_in + 2 * n + n_out:n_in + 2 * n + n_out + n_scr]
        sems = refs[n_in + 2 * n + n_out + n_scr:]
        def at(step):
            return functools.reduce(jnp.logical_and, [pl.program_id(a) == v for a, v in enumerate(step)])

        @pl.when(at((0,) * len(grid)))
        def _():
            _exchange_start(x_ins, x_outs, sems, gather, cols)

        @pl.when(at((grid[0] // 2,) + (0,) * (len(grid) - 2) + (grid[-1] // 2,)))
        def _():
            _exchange_pass_on(x_ins, x_outs, sems, gather, cols, (0, 1))

        @pl.when(at((grid[0] // 2,) + (0,) * (len(grid) - 2) + (3 * grid[-1] // 4,)))
        def _():
            _exchange_pass_on(x_ins, x_outs, sems, gather, cols, (2,))

        body(*ins, *outs, *scr)

        @pl.when(at(tuple(g - 1 for g in grid)))
        def _():
            _exchange_wait(x_ins, x_outs, sems, gather, cols)

    res = pl.pallas_call(
        riding, name=name, grid=grid, in_specs=list(in_specs) + [ANY] * n, out_specs=list(out_specs) + [ANY] * n,
        out_shape=list(out_shape) + x_shape, scratch_shapes=list(scratch) + x_sems,
        compiler_params=_params(("arbitrary",) * len(grid)))(*args, *arrs)
    return res[:n_out], res[n_out:]


def _my_block():
    return (4 * lax.axis_index("x") + 2 * lax.axis_index("y") + lax.axis_index("c")).astype(jnp.int32).reshape(1)


def _proj_in_gather(x, g, w_shard):
    t, k = x.shape
    cs = w_shard.shape[1]
    tm = min(MM_ROWS, t)
    ni = t // tm
    arrival = (0, 1, 2, 4, 3, 5, 6, 7)

    def mask_at(s):
        return jnp.where(s == 3, 4, jnp.where(s == 4, 3, s))

    def body(me_ref, x_ref, g_ref, w_hbm, o_ref, all_hbm, n_hbm, w_vmem, n_vmem, send_sems, recv_sems, loc_sems,
             load_sems, n_sem):
        s, i = pl.program_id(0), pl.program_id(1)
        cps = _gather_copies([w_hbm], [all_hbm], (send_sems, recv_sems, loc_sems), (cs,))
        by_mask = {0: cps["local"][0], 1: cps["from_core"][0]}
        for t_chip, m in enumerate(OTHER_CHIPS):
            by_mask[m] = cps["from_chips"][t_chip]
            by_mask[m + 1] = cps["from_core"][1 + t_chip]
        arrived = [by_mask[m] for m in arrival]

        def load(step):
            src = w_hbm if step == 0 else _window(all_hbm, jnp.bitwise_xor(me_ref[0], arrival[step]), cs)
            return pltpu.make_async_copy(src, w_vmem.at[step % 2], load_sems.at[step % 2])

        @pl.when(jnp.logical_and(s == 0, i == 0))
        def _():
            for cp in cps["local"] + cps["to_chips"] + cps["to_core"]:
                cp.start()
            load(0).start()

        for step, mask in enumerate(arrival):
            @pl.when(jnp.logical_and(s == step, i == 0))
            def _(step=step):
                load(step).wait()

            if step + 1 < N_DEV:
                @pl.when(jnp.logical_and(s == step, i == min(1, ni - 1)))
                def _(step=step):
                    arrived[step + 1].wait_recv()
                    if arrival[step + 1] in OTHER_CHIPS:
                        cps["pass_on"][OTHER_CHIPS.index(arrival[step + 1])].start()
                    load(step + 1).start()

        @pl.when(s == 0)
        def _():
            xf = x_ref[...]
            r = lax.rsqrt(jnp.mean(xf * xf, axis=-1, keepdims=True) + RMS_EPS)
            n_vmem[i] = (xf * r * g_ref[...]).astype(BF16)
            keep = pltpu.make_async_copy(n_vmem.at[i], n_hbm.at[pl.ds(pl.multiple_of(i * tm, tm), tm), :], n_sem)
            keep.start()
            keep.wait()

        o_ref[...] = lax.dot_general(n_vmem[i], w_vmem[s % 2], NN, preferred_element_type=F32).astype(BF16)

        @pl.when(jnp.logical_and(s == N_DEV - 1, i == ni - 1))
        def _():
            cps["local"][0].wait()
            for cp in cps["to_chips"] + cps["to_core"] + cps["pass_on"]:
                cp.wait_send()

    return pl.pallas_call(
        body, name="proj_in",
        grid_spec=pltpu.PrefetchScalarGridSpec(
            num_scalar_prefetch=1, grid=(N_DEV, ni),
            in_specs=[pl.BlockSpec((tm, k), lambda s, i, me: (jnp.where(s == 0, i, 0), 0)),
                      pl.BlockSpec((1, k), lambda s, i, me: (0, 0)), ANY],
            out_specs=[pl.BlockSpec((tm, cs), lambda s, i, me: (i, jnp.bitwise_xor(me[0], mask_at(s)))), ANY, ANY],
            scratch_shapes=[pltpu.VMEM((2, k, cs), BF16), pltpu.VMEM((ni, tm, k), BF16),
                            pltpu.SemaphoreType.DMA((N_DEV - 1,)), pltpu.SemaphoreType.DMA((N_DEV - 1,)),
                            pltpu.SemaphoreType.DMA((1,)), pltpu.SemaphoreType.DMA((2,)), pltpu.SemaphoreType.DMA]),
        out_shape=[jax.ShapeDtypeStruct((t, N_DEV * cs), BF16), jax.ShapeDtypeStruct((k, N_DEV * cs), BF16),
                   jax.ShapeDtypeStruct((t, k), BF16)],
        compiler_params=_params(("arbitrary", "arbitrary")),
    )(_my_block(), x, g, w_shard)


def _gw_in_scatter(a, g):
    t, k = a.shape
    cs = g.shape[1] // N_DEV
    tm = min(MM_ROWS, t)
    nr = t // tm
    n_chip = N_DEV // 2
    chips = (6, 4, 2, 0)

    def body(me_ref, a_ref, g_ref, out_hbm, acc, stage, other, core_send, core_recv, chip_send, chip_recv, loc_sem):
        s, r = pl.program_id(0), pl.program_id(1)
        x, y, c = lax.axis_index("x"), lax.axis_index("y"), lax.axis_index("c")
        my_chip = 2 * x + y
        part = lax.dot_general(a_ref[...], g_ref[...], TN, preferred_element_type=F32)

        def to_core(m):
            return pltpu.make_async_remote_copy(src_ref=stage.at[0], dst_ref=other.at[m], send_sem=core_send.at[m],
                                                recv_sem=core_recv.at[m], device_id=(x, y, 1 - c),
                                                device_id_type=pl.DeviceIdType.MESH)

        def to_chip(m, landed):
            mask = chips[m]
            there = (1 - x if mask & 4 else x, 1 - y if mask & 2 else y, c)
            slot = (2 * there[0] + there[1]) if landed else my_chip
            return pltpu.make_async_remote_copy(src_ref=stage.at[1], dst_ref=out_hbm.at[slot], send_sem=chip_send.at[m],
                                                recv_sem=chip_recv.at[m], device_id=there,
                                                device_id_type=pl.DeviceIdType.MESH)

        local = pltpu.make_async_copy(stage.at[1], out_hbm.at[my_chip], loc_sem)

        @pl.when(r == 0)
        def _():
            acc[...] = part

        @pl.when(r > 0)
        def _():
            acc[...] += part

        for step in range(N_DEV):
            m = step // 2

            @pl.when(jnp.logical_and(s == step, r == nr - 1))
            def _(step=step, m=m):
                if step % 2 == 0:
                    if m > 0:
                        to_core(m - 1).wait_send()
                    stage[0] = acc[...].astype(BF16)
                    to_core(m).start()
                else:
                    if m > 0:
                        to_chip(m - 1, False).wait_send()
                    to_core(m).wait_recv()
                    stage[1] = (acc[...] + other[m].astype(F32)).astype(BF16)
                    if m < n_chip - 1:
                        to_chip(m, False).start()
                    else:
                        local.start()
                        to_core(m).wait_send()
                        local.wait()
                        for mm in range(n_chip - 1):
                            to_chip(mm, True).wait_recv()

    return pl.pallas_call(
        body, name="gw_in",
        grid_spec=pltpu.PrefetchScalarGridSpec(
            num_scalar_prefetch=1, grid=(N_DEV, nr),
            in_specs=[pl.BlockSpec((tm, k), lambda s, r, me: (r, 0)),
                      pl.BlockSpec((tm, cs), lambda s, r, me: (r, jnp.bitwise_xor(me[0], N_DEV - 1 - s)))],
            out_specs=ANY,
            scratch_shapes=[pltpu.VMEM((k, cs), F32), pltpu.VMEM((2, k, cs), BF16), pltpu.VMEM((n_chip, k, cs), BF16),
                            pltpu.SemaphoreType.DMA((n_chip,)), pltpu.SemaphoreType.DMA((n_chip,)),
                            pltpu.SemaphoreType.DMA((n_chip - 1,)), pltpu.SemaphoreType.DMA((n_chip - 1,)),
                            pltpu.SemaphoreType.DMA]),
        out_shape=jax.ShapeDtypeStruct((n_chip, k, cs), BF16),
        compiler_params=_params(("arbitrary", "arbitrary")),
    )(_my_block(), a, g)


SMALL_ROWS = 8


def _allreduce_small(parts, loss_part):
    n, d = len(parts), parts[0].shape[1]

    def body(*refs):
        part_refs, loss_ref, o_ref = refs[:n], refs[n], refs[n + 1]
        mine_ref, all_ref, send_sems, recv_sems = refs[n + 2:]
        me, peers = _peers()
        mine_ref[...] = jnp.zeros_like(mine_ref)
        for i, p_ref in enumerate(part_refs):
            mine_ref[i:i + 1, :] = p_ref[...]
        mine_ref[SMALL_ROWS - 1:SMALL_ROWS, 0:LANES] = loss_ref[0:1, :]
        all_ref[me] = mine_ref[...]
        for k, (dev, idx) in enumerate(peers):
            pltpu.make_async_remote_copy(src_ref=mine_ref, dst_ref=all_ref.at[me], send_sem=send_sems.at[k],
                                         recv_sem=recv_sems.at[k], device_id=dev,
                                         device_id_type=pl.DeviceIdType.MESH).start()
        for k, (dev, idx) in enumerate(peers):
            cp = pltpu.make_async_remote_copy(src_ref=mine_ref, dst_ref=all_ref.at[idx], send_sem=send_sems.at[k],
                                              recv_sem=recv_sems.at[k], device_id=dev,
                                              device_id_type=pl.DeviceIdType.MESH)
            cp.wait_send()
            cp.wait_recv()
        tot = all_ref[0]
        for dvc in range(1, N_DEV):
            tot = tot + all_ref[dvc]
        o_ref[...] = tot

    return pl.pallas_call(
        body, name="allreduce_small", in_specs=[VMEM] * (n + 1), out_specs=VMEM,
        out_shape=jax.ShapeDtypeStruct((SMALL_ROWS, d), F32),
        scratch_shapes=[pltpu.VMEM((SMALL_ROWS, d), F32), pltpu.VMEM((N_DEV, SMALL_ROWS, d), F32),
                        pltpu.SemaphoreType.DMA((N_DEV - 1,)), pltpu.SemaphoreType.DMA((N_DEV - 1,))],
    )(*parts, loss_part)


def _adam_math(g, w, m, v):
    m_new = ADAM_B1 * m + (1.0 - ADAM_B1) * g
    v_new = ADAM_B2 * v + (1.0 - ADAM_B2) * (g * g)
    m_hat = m_new / (1.0 - ADAM_B1 ** ADAM_STEP)
    v_hat = v_new / (1.0 - ADAM_B2 ** ADAM_STEP)
    delta = -ADAM_LR * (m_hat / (jnp.sqrt(v_hat) + ADAM_EPS) + ADAM_WD * w)
    return delta, m_new, v_new


def _adam(name, pieces, w, m, v):
    r, c = w.shape
    n_piece, _, cp = pieces.shape
    tr = r
    for cand in (256, 176, 128, 64):
        if r % cand == 0 and r > cand:
            tr = cand
            break

    def body(p_ref, w_ref, m_ref, v_ref, g_ref, d_ref, mo_ref, vo_ref):
        g = p_ref[0, :, 0:c].astype(F32)
        for j in range(1, n_piece):
            g = g + p_ref[j, :, 0:c].astype(F32)
        delta, m_new, v_new = _adam_math(g, w_ref[...], m_ref[...], v_ref[...])
        g_ref[...] = g
        d_ref[...] = delta
        mo_ref[...] = m_new
        vo_ref[...] = v_new

    blk = pl.BlockSpec((tr, c), lambda i: (i, 0))
    osh = jax.ShapeDtypeStruct((r, c), F32)
    return pl.pallas_call(
        body, name=name, grid=(r // tr,),
        in_specs=[pl.BlockSpec((n_piece, tr, cp), lambda i: (0, i, 0)), blk, blk, blk],
        out_specs=[blk, blk, blk, blk], out_shape=[osh, osh, osh, osh],
        compiler_params=_params(("parallel",)),
    )(pieces, w, m, v)


def _adam_small(g_all, ws, ms, vs):
    n = len(ws)

    def body(*refs):
        g_ref, ins, outs = refs[0], refs[1:1 + 3 * n], refs[1 + 3 * n:]
        for i in range(n):
            g = g_ref[i:i + 1, :]
            delta, m_new, v_new = _adam_math(g, ins[i][...], ins[n + i][...], ins[2 * n + i][...])
            for kind, val in enumerate((g, delta, m_new, v_new)):
                outs[kind * n + i][...] = val

    osh = jax.ShapeDtypeStruct(ws[0].shape, F32)
    res = pl.pallas_call(body, name="adam_small", in_specs=[VMEM] * (1 + 3 * n), out_specs=[VMEM] * (4 * n),
                         out_shape=[osh] * (4 * n))(g_all, *ws, *ms, *vs)
    return res[:n], res[n:2 * n], res[2 * n:3 * n], res[3 * n:]


def _local_step(x, mem, pos, tgt, gains, w_in_shard, shards, batch):
    g_mix, g_mem_q, g_mem_kv, g_ffn, g_final = gains
    t, d = x.shape
    s = t // batch
    n_mem = mem.shape[0] // batch
    n_sh = N_DEV
    width = shards[0].shape[0]
    nb = width // LANES

    lane = np.arange(LANES) % HEAD_DIM
    sel_lo = (lane < ROPE_HALF).astype(np.float32)[None, :]
    sel_hi = ((lane >= ROPE_HALF) & (lane < 2 * ROPE_HALF)).astype(np.float32)[None, :]
    freqs = np.float32(ROPE_THETA) ** (-np.arange(ROPE_HALF, dtype=np.float32) / np.float32(ROPE_HALF))
    inv_freq = np.where(lane < 2 * ROPE_HALF, freqs[lane % ROPE_HALF], 0.0).astype(np.float32)[None, :]
    cos_t, sin_a, sin_b = _rope_tables(pos, jnp.asarray(inv_freq), jnp.asarray(sel_lo), jnp.asarray(sel_hi))
    bias = _dilated_bias_tiles(s)

    proj, w_in, n1 = _proj_in_gather(x, g_mix, w_in_shard)
    qk_a = _rope_apply("rope_fwd", [proj], 2 * width, cos_t, sin_a, sin_b, 1.0)
    cs_up = shards[0].shape[1]
    (o_a, lse_a), (w_up_a, w_up_b, w_out, w_q, w_kv, w_o, w_fd) = _da_fwd(
        qk_a, proj, 2 * nb, bias, batch, s,
        ride=(shards[:6] + shards[8:], True, (cs_up, cs_up, 0, 0, 0, cs_up, 0)))
    (o_b, tot_b), (w_fg, w_fu) = _sb_fwd(proj, 3 * nb, 4 * nb, 5 * nb, batch, s, ride=(shards[6:8], True, (0, 0)))
    w_out = w_out.reshape(d, d)
    w_q = w_q.reshape(d, -1)
    w_kv = w_kv.reshape(d, -1)
    w_fd = w_fd.reshape(-1, d)
    w_fg = w_fg.reshape(-1, d)
    w_fu = w_fu.reshape(-1, d)
    ua, ub, mixed, n2, h1, q_m = _mixer_fwd(o_a, o_b, w_up_a, w_up_b, proj, 6 * nb, w_out, x, g_mem_q, w_q)
    mem_n = _rms_fwd("norm_mem_kv", mem, g_mem_kv)
    kv_m = _mm_w("mem_kv", mem_n, w_kv, BF16)
    o_m = _mem_fwd(q_m, kv_m, batch, s, n_mem)
    h2, n3 = _mm_res_norm("mem_out", o_m, w_o, h1, g_ffn)
    hg, hu, act = _ffn_up(n3, w_fg, w_fu)
    loss_part, dh3, dh3_b, dg_final = _loss_head(act, w_fd, h2, tgt, g_final.reshape(1, d))

    dhg, dhu, dh2, dh2_b, dg_ffn, do_m = _ffn_bwd(dh3_b, w_fd, w_fg, w_fu, hg, hu, h2, g_ffn, dh3, w_o)
    gw_fd = _wgrad("gw_ffn_down", act, dh3_b)
    gw_fg = _wgrad("gw_ffn_gate", dhg, n3)
    gw_fu = _wgrad("gw_ffn_up", dhu, n3)

    gw_o = _wgrad("gw_mem_o", o_m, dh2_b)
    dq_m, dkv_m = _mem_bwd(q_m, kv_m, do_m, batch, s, n_mem)
    gw_q = _wgrad("gw_mem_q", n2, dq_m)
    gw_kv = _wgrad("gw_mem_kv", mem_n, dkv_m)
    (dg_mem_kv,) = _rms_bwd("norm_mem_kv_bwd", (dkv_m, w_kv, NT), mem, g_mem_kv, None, ())
    dh1, dh1_b, dg_mem_q = _rms_bwd("norm_mem_q_bwd", (dq_m, w_q, NT), h1, g_mem_q, dh2, ("f32", "bf16"))

    gw_out = _wgrad("gw_out", mixed, dh1_b)
    dua, dub, dgates, do_a, do_b = _mixer_bwd(dh1_b, w_out, ua, ub, proj, 6 * nb, w_up_a, w_up_b)
    gw_ua = _wgrad("gw_up_a", o_a, dua)
    gw_ub = _wgrad("gw_up_b", o_b, dub)
    (dq_ar, dk_ar, dv_a), (p_fg, p_fd) = _da_bwd(
        qk_a, proj, 2 * nb, bias, o_a, lse_a, do_a, batch, s,
        ride=([gw_fg.reshape(n_sh, -1, d), gw_fd.reshape(n_sh, -1, d)], False, (0, 0)))
    dqk_a = _rope_apply("rope_bwd", [dq_ar, dk_ar], width, cos_t, sin_a, sin_b, -1.0)
    mid = [gw_ua, gw_ub, gw_out.reshape(n_sh, -1, d), gw_q.reshape(n_sh, -1, gw_q.shape[1]),
           gw_kv.reshape(n_sh, -1, gw_kv.shape[1]), gw_o, gw_fu.reshape(n_sh, -1, d)]
    (dq_b, dk_b, dv_b), (*p_mid, p_fu) = _sb_bwd(proj, 3 * nb, 4 * nb, 5 * nb, tot_b, do_b, batch, s,
                                                 ride=(mid, False, (cs_up, cs_up, 0, 0, 0, cs_up, 0)))
    p_ffn = [p_fg, p_fu, p_fd]
    dproj = jnp.concatenate([dqk_a, dv_a, dq_b, dk_b, dv_b, dgates], axis=1)
    grad_x, dg_mix = _rms_bwd("proj_in_bwd", (dproj, w_in, NT), x, g_mix, dh1, ("f32",))
    p_in = _gw_in_scatter(n1, dproj)
    return loss_part, grad_x, [p_in] + list(p_mid) + p_ffn, (dg_mix, dg_mem_q, dg_mem_kv, dg_ffn, dg_final)


WEIGHTS =("w_in", "w_up_a", "w_up_b", "w_out", "w_q_mem", "w_kv_mem", "w_o_mem", "w_ffn_gate", "w_ffn_up", "w_ffn_down")
GAINS = ("g_mix", "g_mem_q", "g_mem_kv", "g_ffn", "g_final")
ORDER = ("g_mix", "w_in", "w_up_a", "w_up_b", "w_out", "g_mem_q", "g_mem_kv", "w_q_mem", "w_kv_mem", "w_o_mem", "g_ffn",
         "w_ffn_gate", "w_ffn_up", "w_ffn_down", "g_final")


def kernel(x, mem, positions, g_mix, w_in, w_up_a, w_up_b, w_out, g_mem_q, g_mem_kv, w_q_mem, w_kv_mem, w_o_mem, g_ffn, w_ffn_gate, w_ffn_up, w_ffn_down, g_final, loss_target, m_g_mix, m_w_in, m_w_up_a, m_w_up_b, m_w_out, m_g_mem_q, m_g_mem_kv, m_w_q_mem, m_w_kv_mem, m_w_o_mem, m_g_ffn, m_w_ffn_gate, m_w_ffn_up, m_w_ffn_down, m_g_final, v_g_mix, v_w_in, v_w_up_a, v_w_up_b, v_w_out, v_g_mem_q, v_g_mem_kv, v_w_q_mem, v_w_kv_mem, v_w_o_mem, v_g_ffn, v_w_ffn_gate, v_w_ffn_up, v_w_ffn_down, v_g_final):
    given = dict(locals())
    batch, s, d = x.shape
    t = batch * s
    flipped = ("w_ffn_gate", "w_ffn_up")

    def view(a, n):
        a = a.reshape(a.shape[-2:])
        return a.T if n in flipped else a

    def unview(a, n):
        return (a.T if n in flipped else a).reshape(given[n].shape)

    shard = {n: view(given[n], n) for n in WEIGHTS}
    gains = [given[n].reshape(1, d) for n in GAINS]

    pad = (-shard["w_ffn_down"].shape[0]) % LANES
    cast = _cast_weights([shard[n] for n in WEIGHTS], [pad if n in flipped + ("w_ffn_down",) else 0 for n in WEIGHTS])
    loss_part, grad_x, pieces, dgains = _local_step(
        x.reshape(t, d), mem.reshape(-1, d), positions.reshape(t, 1), loss_target.reshape(t, d), gains, cast[0],
        cast[1:], batch)

    grad, delta, new_m, new_v = {}, {}, {}, {}
    for n, p in zip(WEIGHTS, pieces):
        outs = _adam("adam_" + n, p, shard[n], view(given["m_" + n], n), view(given["v_" + n], n))
        grad[n], delta[n], new_m[n], new_v[n] = [unview(o, n) for o in outs]

    g_all = _allreduce_small(list(dgains), loss_part)
    small = _adam_small(g_all, gains, [given["m_" + n].reshape(1, d) for n in GAINS],
                        [given["v_" + n].reshape(1, d) for n in GAINS])
    for out, vals in zip((grad, delta, new_m, new_v), small):
        for n, val in zip(GAINS, vals):
            out[n] = val.reshape(given[n].shape)

    loss = g_all[SMALL_ROWS - 1, 0]
    return (loss, grad_x.reshape(x.shape), *[grad[n] for n in ORDER], *[delta[n] for n in ORDER],
            *[new_m[n] for n in ORDER], *[new_v[n] for n in ORDER])
```

```python
import functools
import math

import jax
import jax.numpy as jnp
import numpy as np
from jax import lax
from jax.experimental import pallas as pl
from jax.experimental.pallas import tpu as pltpu

F32 = jnp.float32
BF16 = jnp.bfloat16

N_DEV = 8
HEAD_DIM = 64
MEM_HEAD_DIM = 128
N_HEADS_MEM = 4
BLOCK = 128
DIL_PATTERNS = ((128, 1), (512, 4), (2048, 16))
ROPE_THETA = 500000.0
ROPE_HALF = 8
RMS_EPS = 1e-6
ADAM_LR, ADAM_B1, ADAM_B2, ADAM_EPS, ADAM_WD, ADAM_STEP = 0.001, 0.9, 0.999, 1e-08, 0.01, 10
NEG = -1e30
ROW_TILE = 512
LANES = 128

ANY = pl.BlockSpec(memory_space=pl.ANY)
VMEM = pl.BlockSpec(memory_space=pltpu.VMEM)
NN = (((1,), (0,)), ((), ()))
NT = (((1,), (1,)), ((), ()))
TN = (((0,), (0,)), ((), ()))


def _params(sem):
    return pltpu.CompilerParams(dimension_semantics=sem)


def _mm(name, a, b, *, grid, a_spec, b_spec, o_shape, o_spec, dims, out_dtype, nk=1):
    def body(*refs):
        a_ref, b_ref, o_ref = refs[0], refs[1], refs[2]
        p = lax.dot_general(a_ref[...], b_ref[...], dims, preferred_element_type=F32)
        if nk == 1:
            o_ref[...] = p.astype(out_dtype)
            return
        acc_ref = refs[-1]
        k = pl.program_id(len(grid) - 1)

        @pl.when(k == 0)
        def _():
            acc_ref[...] = p

        @pl.when(k > 0)
        def _():
            acc_ref[...] += p

        @pl.when(k == nk - 1)
        def _():
            o_ref[...] = acc_ref[...].astype(out_dtype)

    o_block = tuple(d for d in o_spec.block_shape if d is not None)
    sem = ("parallel",) * (len(grid) - 1) + (("arbitrary",) if nk > 1 else ("parallel",))
    return pl.pallas_call(
        body, name=name, grid=grid, in_specs=[a_spec, b_spec],
        out_specs=o_spec, out_shape=jax.ShapeDtypeStruct(o_shape, out_dtype),
        scratch_shapes=[pltpu.VMEM(o_block, F32)] if nk > 1 else [],
        compiler_params=_params(sem),
    )(a, b)


def _rms_fwd(name, x, g):
    t, d = x.shape
    tm = min(ROW_TILE, t)

    def body(x_ref, g_ref, o_ref):
        xf = x_ref[...]
        r = lax.rsqrt(jnp.mean(xf * xf, axis=-1, keepdims=True) + RMS_EPS)
        o_ref[...] = (xf * r * g_ref[...]).astype(BF16)

    return pl.pallas_call(
        body, name=name, grid=(t // tm,),
        in_specs=[pl.BlockSpec((tm, d), lambda i: (i, 0)), pl.BlockSpec((1, d), lambda i: (0, 0))],
        out_specs=pl.BlockSpec((tm, d), lambda i: (i, 0)), out_shape=jax.ShapeDtypeStruct((t, d), BF16),
        compiler_params=_params(("parallel",)),
    )(x, g)


def _rms_bwd_rows(dnf, xf, gv, res):
    r = lax.rsqrt(jnp.mean(xf * xf, axis=-1, keepdims=True) + RMS_EPS)
    xh = xf * r
    dxh = dnf * gv
    dx = r * (dxh - xh * jnp.mean(dxh * xh, axis=-1, keepdims=True))
    if res is not None:
        dx = dx + res
    return dx, jnp.sum(dnf * xh, axis=0, keepdims=True)


def _rms_bwd(name, dn, x, g, dres, want):
    t, d = x.shape
    tm = min(ROW_TILE, t)
    has_res = dres is not None
    lhs = list(dn) if isinstance(dn, tuple) else [dn]
    n_lhs = len(lhs[:2])

    def body(*refs):
        x_ref, g_ref = refs[n_lhs], refs[n_lhs + 1]
        r_ref = refs[n_lhs + 2] if has_res else None
        dx_refs, dg_ref = refs[-1 - len(want):-1], refs[-1]
        if n_lhs == 2:
            dnf = lax.dot_general(refs[0][...], refs[1][...], lhs[2], preferred_element_type=F32)
        else:
            dnf = refs[0][...].astype(F32)
        dx, dg = _rms_bwd_rows(dnf, x_ref[...], g_ref[...], r_ref[...] if has_res else None)
        for kind, dx_ref in zip(want, dx_refs):
            dx_ref[...] = dx.astype(F32 if kind == "f32" else BF16)

        @pl.when(pl.program_id(0) == 0)
        def _():
            dg_ref[...] = jnp.zeros_like(dg_ref)

        dg_ref[...] += dg

    row = pl.BlockSpec((tm, d), lambda i: (i, 0))
    vec = pl.BlockSpec((1, d), lambda i: (0, 0))
    if n_lhs == 2:
        first = [pl.BlockSpec((tm, lhs[0].shape[1]), lambda i: (i, 0)), pl.BlockSpec(lhs[1].shape, lambda i: (0, 0))]
    else:
        first = [row]
    return pl.pallas_call(
        body, name=name, grid=(t // tm,),
        in_specs=first + [row, vec] + ([row] if has_res else []),
        out_specs=[row] * len(want) + [vec],
        out_shape=[jax.ShapeDtypeStruct((t, d), F32 if kind == "f32" else BF16) for kind in want]
        + [jax.ShapeDtypeStruct((1, d), F32)],
        compiler_params=_params(("arbitrary",)),
    )(*(lhs[:2] + [x, g] + ([dres] if has_res else [])))


def _loss_head(a, w, res, tgt, g):
    t, d = res.shape
    k = a.shape[1]
    tm = min(ROW_TILE, t)

    def body(a_ref, w_ref, r_ref, t_ref, g_ref, loss_ref, dh_ref, dhb_ref, dg_ref):
        xf = lax.dot_general(a_ref[...], w_ref[...], NN, preferred_element_type=F32) + r_ref[...]
        gv = g_ref[...]
        r = lax.rsqrt(jnp.mean(xf * xf, axis=-1, keepdims=True) + RMS_EPS)
        xh = xf * r
        e = xh * gv - t_ref[...]
        dy = e * (1.0 / d)
        dxh = dy * gv
        dh = r * (dxh - xh * jnp.mean(dxh * xh, axis=-1, keepdims=True))
        dh_ref[...] = dh
        dhb_ref[...] = dh.astype(BF16)

        @pl.when(pl.program_id(0) == 0)
        def _():
            dg_ref[...] = jnp.zeros_like(dg_ref)
            loss_ref[...] = jnp.zeros_like(loss_ref)

        dg_ref[...] += jnp.sum(dy * xh, axis=0, keepdims=True)
        part = jnp.sum(jnp.sum(e * e, axis=1, keepdims=True), axis=0, keepdims=True) * (0.5 / d)
        loss_ref[...] += jnp.broadcast_to(part, loss_ref.shape)

    row = pl.BlockSpec((tm, d), lambda i: (i, 0))
    vec = pl.BlockSpec((1, d), lambda i: (0, 0))
    return pl.pallas_call(
        body, name="loss_head", grid=(t // tm,),
        in_specs=[pl.BlockSpec((tm, k), lambda i: (i, 0)), pl.BlockSpec((k, d), lambda i: (0, 0)), row, row, vec],
        out_specs=[pl.BlockSpec((8, LANES), lambda i: (0, 0)), row, row, vec],
        out_shape=[jax.ShapeDtypeStruct((8, LANES), F32), jax.ShapeDtypeStruct((t, d), F32),
                   jax.ShapeDtypeStruct((t, d), BF16), jax.ShapeDtypeStruct((1, d), F32)],
        compiler_params=_params(("arbitrary",)),
    )(a, w, res, tgt, g)


def _rope_tables(pos, inv_freq, sel_lo, sel_hi):
    t = pos.shape[0]
    tm = min(ROW_TILE, t)

    def body(p_ref, f_ref, lo_ref, hi_ref, c_ref, sa_ref, sb_ref):
        ang = p_ref[...].astype(F32) * f_ref[...]
        rot = lo_ref[...] + hi_ref[...]
        cs, sn = jnp.cos(ang), jnp.sin(ang)
        c_ref[...] = cs * rot + (1.0 - rot)
        sa_ref[...] = -sn * lo_ref[...]
        sb_ref[...] = sn * hi_ref[...]

    vec = pl.BlockSpec((1, LANES), lambda i: (0, 0))
    row = pl.BlockSpec((tm, LANES), lambda i: (i, 0))
    return pl.pallas_call(
        body, name="rope_tables", grid=(t // tm,),
        in_specs=[pl.BlockSpec((tm, 1), lambda i: (i, 0)), vec, vec, vec],
        out_specs=[row, row, row], out_shape=[jax.ShapeDtypeStruct((t, LANES), F32)] * 3,
        compiler_params=_params(("parallel",)),
    )(pos, inv_freq, sel_lo, sel_hi)


def _rope_apply(name, srcs, width, cos_t, sin_a, sin_b, sign, tail=()):
    t = srcs[0].shape[0]
    tm = min(ROW_TILE, t)
    n_cols = width // LANES
    n_src = len(srcs)

    def body(*refs):
        x_refs, tail_refs = refs[:n_src], refs[n_src:n_src + len(tail)]
        c_ref, sa_ref, sb_ref, o_ref = refs[n_src + len(tail):]
        cs, sa, sb = c_ref[...], sign * sa_ref[...], sign * sb_ref[...]
        for a, x_ref in enumerate(x_refs):
            for c in range(n_cols):
                xf = x_ref[:, c * LANES:(c + 1) * LANES].astype(F32)
                up = pltpu.roll(xf, LANES - ROPE_HALF, 1)
                dn = pltpu.roll(xf, ROPE_HALF, 1)
                o_ref[:, a * width + c * LANES:a * width + (c + 1) * LANES] = (xf * cs + up * sa + dn * sb).astype(BF16)
        col = n_src * width
        for t_ref in tail_refs:
            o_ref[:, col:col + t_ref.shape[1]] = t_ref[...]
            col += t_ref.shape[1]

    wide = n_src * width + sum(a.shape[1] for a in tail)
    tab = pl.BlockSpec((tm, LANES), lambda i: (i, 0))
    return pl.pallas_call(
        body, name=name, grid=(t // tm,),
        in_specs=[pl.BlockSpec((tm, width), lambda i: (i, 0))] * n_src
        + [pl.BlockSpec((tm, a.shape[1]), lambda i: (i, 0)) for a in tail] + [tab, tab, tab],
        out_specs=pl.BlockSpec((tm, wide), lambda i: (i, 0)),
        out_shape=jax.ShapeDtypeStruct((t, wide), BF16),
        compiler_params=_params(("parallel",)),
    )(*srcs, *tail, cos_t, sin_a, sin_b)


DA_T = 256
MIX_STREAMS = 4
SB_BWD_STREAMS = 2


def _lane_lo():
    return lax.broadcasted_iota(jnp.int32, (BLOCK, LANES), 1) < HEAD_DIM


def _dilated_bias_tiles(s):
    n = s // DA_T
    dist = (np.arange(n)[:, None, None] * DA_T + np.arange(DA_T)[None, :, None] - np.arange(DA_T)[None, None, :])
    cnt = np.zeros(dist.shape, np.float32)
    for window, dil in DIL_PATTERNS:
        cnt += ((dist >= 0) & (dist % dil == 0) & (dist <= window)).astype(np.float32)
    return jnp.asarray(np.where(cnt > 0, np.log(np.maximum(cnt, 1.0)), NEG).astype(np.float32))


def _stack_heads(x, lo):
    zero = jnp.zeros_like(x)
    return jnp.concatenate([jnp.where(lo, x, zero), jnp.where(lo, zero, x)], axis=0)


def _da_fwd(qk, proj, v_col0, bias, batch, s, ride=None, streams=MIX_STREAMS):
    t = qk.shape[0]
    nq = s // DA_T
    n_pairs = 4
    ns = streams
    wide = ns * LANES
    scale = HEAD_DIM ** -0.5

    def body(q_ref, k_ref, v_ref, b_ref, o_ref, lse_ref, acc_ref, m_ref, l_ref):
        i = pl.program_id(2)
        lo = lax.broadcasted_iota(jnp.int32, (DA_T, LANES), 1) < HEAD_DIM
        ones = jnp.ones((DA_T, LANES), BF16)
        acc_ref[...] = jnp.zeros_like(acc_ref)
        m_ref[...] = jnp.full(m_ref.shape, NEG, F32)
        l_ref[...] = jnp.zeros_like(l_ref)
        qqs = [_stack_heads(q_ref[:, st * LANES:(st + 1) * LANES] * scale, lo) for st in range(ns)]

        def scores(st, rows, bias2):
            k = k_ref[rows, st * LANES:(st + 1) * LANES]
            return lax.dot_general(qqs[st], k, NT, preferred_element_type=F32) + bias2

        def softmax(st, sc):
            m_old = m_ref[st]
            m_new = jnp.maximum(m_old, jnp.max(sc, axis=1, keepdims=True))
            m_ref[st] = m_new
            return jnp.exp(sc - m_new).astype(BF16), jnp.exp(m_old - m_new)

        def values(st, rows, p, alpha):
            v = v_ref[rows, st * LANES:(st + 1) * LANES]
            vz = jnp.zeros_like(v)
            l_ref[st] = alpha * l_ref[st] + lax.dot_general(p, ones, NN, preferred_element_type=F32)
            pv = (lax.dot_general(p[:DA_T], jnp.where(lo, v, vz), NN, preferred_element_type=F32)
                  + lax.dot_general(p[DA_T:], jnp.where(lo, vz, v), NN, preferred_element_type=F32))
            acc_ref[st] = acc_ref[st] * jnp.where(lo, alpha[:DA_T], alpha[DA_T:]) + pv

        def trip(dlt, carry):
            rows = pl.ds(pl.multiple_of((i - dlt) * DA_T, DA_T), DA_T)
            bias_t = b_ref[dlt]
            bias2 = jnp.concatenate([bias_t, bias_t], axis=0)
            scs = [scores(st, rows, bias2) for st in range(ns)]
            pas = [softmax(st, scs[st]) for st in range(ns)]
            for st in range(ns):
                values(st, rows, *pas[st])
            return carry

        lax.fori_loop(0, i + 1, trip, 0)
        for st in range(ns):
            cols = slice(st * LANES, (st + 1) * LANES)
            l_t = l_ref[st]
            o_ref[:, cols] = (acc_ref[st] / jnp.where(lo, l_t[:DA_T], l_t[DA_T:])).astype(BF16)
            lse = m_ref[st] + jnp.log(l_t)
            lse_ref[:, cols] = jnp.where(lo, lse[:DA_T], lse[DA_T:])

    blk = pl.BlockSpec((DA_T, wide), lambda b, h, i: (b * nq + i, h))
    return _call(
        body, name="attn_a_fwd", grid=(batch, n_pairs // ns, nq),
        in_specs=[blk,
                  pl.BlockSpec((s, wide), lambda b, h, i: (b, n_pairs // ns + h)),
                  pl.BlockSpec((s, wide), lambda b, h, i: (b, v_col0 // ns + h)),
                  pl.BlockSpec((nq, DA_T, DA_T), lambda b, h, i: (0, 0, 0))],
        out_specs=[blk, blk],
        out_shape=[jax.ShapeDtypeStruct((t, n_pairs * LANES), BF16), jax.ShapeDtypeStruct((t, n_pairs * LANES), F32)],
        scratch=[pltpu.VMEM((ns, DA_T, LANES), F32), pltpu.VMEM((ns, 2 * DA_T, 1), F32),
                 pltpu.VMEM((ns, 2 * DA_T, LANES), F32)],
        sem=("parallel", "parallel", "arbitrary"), args=(qk, qk, proj, bias), ride=ride)


def _da_bwd(qk, proj, v_col0, bias, o, lse, do, batch, s, ride=None, streams=MIX_STREAMS):
    t = qk.shape[0]
    nq = s // DA_T
    n_pairs = 4
    ns = streams
    wide = ns * LANES
    scale = HEAD_DIM ** -0.5

    def body(q_ref, k_ref, v_ref, b_ref, o_ref, lse_ref, do_ref, dq_ref, dk_ref, dv_ref, dk_acc, dv_acc, dq_acc):
        i = pl.program_id(2)
        lo = lax.broadcasted_iota(jnp.int32, (DA_T, LANES), 1) < HEAD_DIM

        @pl.when(i == 0)
        def _():
            dk_acc[...] = jnp.zeros_like(dk_acc)
            dv_acc[...] = jnp.zeros_like(dv_acc)

        dq_acc[...] = jnp.zeros_like(dq_acc)
        qqs, dds, deltas, lses = [], [], [], []
        for st in range(ns):
            cols = slice(st * LANES, (st + 1) * LANES)
            do_ = do_ref[:, cols]
            qqs.append(_stack_heads(q_ref[:, cols] * scale, lo))
            dds.append(_stack_heads(do_, lo))
            prod = do_.astype(F32) * o_ref[:, cols].astype(F32)
            fz = jnp.zeros_like(prod)
            deltas.append(jnp.concatenate([jnp.sum(jnp.where(lo, prod, fz), axis=1, keepdims=True),
                                           jnp.sum(jnp.where(lo, fz, prod), axis=1, keepdims=True)], axis=0))
            lse_t = lse_ref[:, cols]
            lses.append(jnp.concatenate([lse_t[:, 0:1], lse_t[:, HEAD_DIM:HEAD_DIM + 1]], axis=0))

        def products(st, rows, bias2):
            cols = slice(st * LANES, (st + 1) * LANES)
            sc = lax.dot_general(qqs[st], k_ref[rows, cols], NT, preferred_element_type=F32) + bias2
            return sc, lax.dot_general(dds[st], v_ref[rows, cols], NT, preferred_element_type=F32)

        def weights(st, sc, dp):
            p = jnp.exp(sc - lses[st])
            return (p * (dp - deltas[st])).astype(BF16), p.astype(BF16)

        def gradients(st, rows, ds, p):
            cols = slice(st * LANES, (st + 1) * LANES)
            k = k_ref[rows, cols]
            kz = jnp.zeros_like(k)
            dq_acc[st] += (lax.dot_general(ds[:DA_T], jnp.where(lo, k, kz), NN, preferred_element_type=F32)
                           + lax.dot_general(ds[DA_T:], jnp.where(lo, kz, k), NN, preferred_element_type=F32))
            dk_acc[rows, cols] += lax.dot_general(ds, qqs[st], TN, preferred_element_type=F32)
            dv_acc[rows, cols] += lax.dot_general(p, dds[st], TN, preferred_element_type=F32)

        def trip(dlt, carry):
            rows = pl.ds(pl.multiple_of((i - dlt) * DA_T, DA_T), DA_T)
            bias_t = b_ref[dlt]
            bias2 = jnp.concatenate([bias_t, bias_t], axis=0)
            prods = [products(st, rows, bias2) for st in range(ns)]
            wts = [weights(st, *prods[st]) for st in range(ns)]
            for st in range(ns):
                gradients(st, rows, *wts[st])
            return carry

        lax.fori_loop(0, i + 1, trip, 0)
        for st in range(ns):
            dq_ref[:, st * LANES:(st + 1) * LANES] = (dq_acc[st] * scale).astype(BF16)

        @pl.when(i == nq - 1)
        def _():
            dk_ref[...] = dk_acc[...].astype(BF16)
            dv_ref[...] = dv_acc[...].astype(BF16)

    blk = pl.BlockSpec((DA_T, wide), lambda b, h, i: (b * nq + i, h))
    seq = pl.BlockSpec((s, wide), lambda b, h, i: (b, h), pipeline_mode=pl.Buffered(1))
    one = pl.Buffered(1)
    out = jax.ShapeDtypeStruct((t, n_pairs * LANES), BF16)
    return _call(
        body, name="attn_a_bwd", grid=(batch, n_pairs // ns, nq),
        in_specs=[blk,
                  pl.BlockSpec((s, wide), lambda b, h, i: (b, n_pairs // ns + h), pipeline_mode=one),
                  pl.BlockSpec((s, wide), lambda b, h, i: (b, v_col0 // ns + h), pipeline_mode=one),
                  pl.BlockSpec((nq, DA_T, DA_T), lambda b, h, i: (0, 0, 0), pipeline_mode=one),
                  blk, blk, blk],
        out_specs=[blk, seq, seq], out_shape=[out, out, out],
        scratch=[pltpu.VMEM((s, wide), F32), pltpu.VMEM((s, wide), F32), pltpu.VMEM((ns, DA_T, LANES), F32)],
        sem=("parallel", "parallel", "arbitrary"), args=(qk, qk, proj, bias, o, lse, do), ride=ride)


SB_Q = 256


def _sb_consts(after):
    r = lax.broadcasted_iota(jnp.int32, (2 * BLOCK, 2 * BLOCK), 0) % BLOCK
    c = lax.broadcasted_iota(jnp.int32, (2 * BLOCK, 2 * BLOCK), 1)
    tri = (r > c) if after else (r < c)
    return jnp.logical_or(c >= BLOCK, tri).astype(BF16)


def _split(x):
    hi = x.astype(BF16)
    lo = (x - hi.astype(F32)).astype(BF16)
    return jnp.concatenate([hi, lo], axis=1)


def _sb_fwd(proj, q_col0, k_col0, v_col0, batch, s, ride=None, streams=MIX_STREAMS):
    t = proj.shape[0]
    nq = s // SB_Q
    n_pairs = 4
    ns = streams
    wide = ns * LANES
    scale = HEAD_DIM ** -0.5

    def body(q_ref, k_ref, v_ref, o_ref, tot_ref, acc_ref, run_ref):
        i = pl.program_id(2)
        lo_q = lax.broadcasted_iota(jnp.int32, (SB_Q, LANES), 1) < HEAD_DIM
        lo_k = _lane_lo()
        mat = _sb_consts(True)
        row = lax.broadcasted_iota(jnp.int32, (2 * SB_Q, LANES), 0) % SB_Q
        ahead = row - lax.broadcasted_iota(jnp.int32, (2 * SB_Q, LANES), 1)
        acc_ref[...] = jnp.zeros_like(acc_ref)
        run_ref[...] = jnp.zeros_like(run_ref)
        qqs = [_stack_heads(q_ref[:, st * LANES:(st + 1) * LANES] * scale, lo_q) for st in range(ns)]

        def units(todo):
            def rows(j):
                return pl.ds(pl.multiple_of(j * BLOCK, BLOCK), BLOCK)

            zs = [lax.dot_general(qqs[st], k_ref[rows(j), st * LANES:(st + 1) * LANES], NT, preferred_element_type=F32)
                  for st, j, _ in todo]
            logs = []
            for z, (_, _, off) in zip(zs, todo):
                lsig = jnp.minimum(z, 0.0) - jnp.log(1.0 + jnp.exp(-jnp.abs(z)))
                lneg = lsig - z
                if off is not None:
                    lneg = jnp.where(ahead > off, lneg, 0.0)
                logs.append((lsig, _split(lneg)))
            sums = [lax.dot_general(cat, mat, NN, preferred_element_type=F32) for _, cat in logs]
            probs = []
            for (lsig, _), sm, (st, _, off) in zip(logs, sums, todo):
                run = run_ref[st]
                a = jnp.exp(lsig + run + sm[:, :BLOCK])
                if off is not None:
                    a = jnp.where(ahead > off, a, 0.0)
                run_ref[st] = run + sm[:, BLOCK:]
                probs.append(a.astype(BF16))
            for ab, (st, j, _) in zip(probs, todo):
                v = v_ref[rows(j), st * LANES:(st + 1) * LANES]
                vz = jnp.zeros_like(v)
                acc_ref[st] += (lax.dot_general(ab[:SB_Q], jnp.where(lo_k, v, vz), NN, preferred_element_type=F32)
                                + lax.dot_general(ab[SB_Q:], jnp.where(lo_k, vz, v), NN, preferred_element_type=F32))

        units([(st, 2 * i + 1, BLOCK) for st in range(ns)] + [(st, 2 * i, 0) for st in range(ns)])

        def pair(p, carry):
            jp = i - 1 - p
            units([(st, 2 * jp + 1, None) for st in range(ns)] + [(st, 2 * jp, None) for st in range(ns)])
            return carry

        lax.fori_loop(0, i, pair, 0)
        for st in range(ns):
            cols = slice(st * LANES, (st + 1) * LANES)
            o_ref[:, cols] = acc_ref[st].astype(BF16)
            tot_ref[:, cols] = jnp.where(lo_q, run_ref[st, 0:SB_Q, :], run_ref[st, SB_Q:2 * SB_Q, :])

    def seq(col0):
        return pl.BlockSpec((s, wide), lambda b, h, i: (b, col0 // ns + h))

    blk = pl.BlockSpec((SB_Q, wide), lambda b, h, i: (b * nq + i, h))
    return _call(
        body, name="attn_b_fwd", grid=(batch, n_pairs // ns, nq),
        in_specs=[pl.BlockSpec((SB_Q, wide), lambda b, h, i: (b * nq + i, q_col0 // ns + h)), seq(k_col0), seq(v_col0)],
        out_specs=[blk, blk],
        out_shape=[jax.ShapeDtypeStruct((t, n_pairs * LANES), BF16), jax.ShapeDtypeStruct((t, n_pairs * LANES), F32)],
        scratch=[pltpu.VMEM((ns, SB_Q, LANES), F32), pltpu.VMEM((ns, 2 * SB_Q, LANES), F32)],
        sem=("parallel", "parallel", "arbitrary"), args=(proj, proj, proj), ride=ride)


def _sb_bwd(proj, q_col0, k_col0, v_col0, tot, do, batch, s, ride=None, streams=SB_BWD_STREAMS):
    t = proj.shape[0]
    nq = s // SB_Q
    n_pairs = 4
    ns = streams
    wide = ns * LANES
    scale = HEAD_DIM ** -0.5

    def body(q_ref, k_ref, v_ref, tot_ref, do_ref, dq_ref, dk_ref, dv_ref, dk_acc, dv_acc, dq_acc, seen_ref, gsum_ref):
        i = pl.program_id(2)
        lo_q = lax.broadcasted_iota(jnp.int32, (SB_Q, LANES), 1) < HEAD_DIM
        lo_k = _lane_lo()

        @pl.when(i == 0)
        def _():
            dk_acc[...] = jnp.zeros_like(dk_acc)
            dv_acc[...] = jnp.zeros_like(dv_acc)

        mat_after = _sb_consts(True)
        mat_before = _sb_consts(False)[:BLOCK]
        row = lax.broadcasted_iota(jnp.int32, (2 * SB_Q, LANES), 0) % SB_Q
        ahead = row - lax.broadcasted_iota(jnp.int32, (2 * SB_Q, LANES), 1)
        dq_acc[...] = jnp.zeros_like(dq_acc)
        seen_ref[...] = jnp.zeros_like(seen_ref)
        gsum_ref[...] = jnp.zeros_like(gsum_ref)
        qqs, dds, totals = [], [], []
        for st in range(ns):
            cols = slice(st * LANES, (st + 1) * LANES)
            qqs.append(_stack_heads(q_ref[:, cols] * scale, lo_q))
            dds.append(_stack_heads(do_ref[:, cols], lo_q))
            tot_t = tot_ref[:, cols]
            totals.append(jnp.concatenate([jnp.broadcast_to(tot_t[:, 0:1], (SB_Q, LANES)),
                                           jnp.broadcast_to(tot_t[:, HEAD_DIM:HEAD_DIM + 1], (SB_Q, LANES))], axis=0))

        def units(todo):
            def rows(j):
                return pl.ds(pl.multiple_of(j * BLOCK, BLOCK), BLOCK)

            def cols(st):
                return slice(st * LANES, (st + 1) * LANES)

            prods = [(lax.dot_general(qqs[st], k_ref[rows(j), cols(st)], NT, preferred_element_type=F32),
                      lax.dot_general(dds[st], v_ref[rows(j), cols(st)], NT, preferred_element_type=F32))
                     for st, j, _ in todo]
            logs = []
            for (z, _), (_, _, off) in zip(prods, todo):
                lsig = jnp.minimum(z, 0.0) - jnp.log(1.0 + jnp.exp(-jnp.abs(z)))
                lneg = lsig - z
                if off is not None:
                    lneg = jnp.where(ahead > off, lneg, 0.0)
                logs.append((lsig, _split(lneg)))
            sums = [lax.dot_general(cat, mat_after, NN, preferred_element_type=F32) for _, cat in logs]
            gates = []
            for (lsig, _), sm, (_, da), (st, _, off) in zip(logs, sums, prods, todo):
                seen = seen_ref[st]
                a = jnp.exp(lsig + (totals[st] - seen - sm[:, BLOCK:]) + sm[:, :BLOCK])
                if off is not None:
                    a = jnp.where(ahead > off, a, 0.0)
                seen_ref[st] = seen + sm[:, BLOCK:]
                g = a * da
                gates.append((a.astype(BF16), g, g.astype(BF16)))
            gsums = [lax.dot_general(cat, mat_before, NN, preferred_element_type=F32) for _, _, cat in gates]
            outs = []
            for (lsig, _), (ab, g, _), gs, (st, _, off) in zip(logs, gates, gsums, todo):
                gsum = gsum_ref[st]
                dz = g - jnp.exp(lsig) * (g + gsum + gs[:, :BLOCK])
                if off is not None:
                    dz = jnp.where(ahead > off, dz, 0.0)
                gsum_ref[st] = gsum + gs[:, BLOCK:]
                outs.append((dz.astype(BF16), ab))
            for (dzb, ab), (st, j, _) in zip(outs, todo):
                k = k_ref[rows(j), cols(st)]
                kz = jnp.zeros_like(k)
                dq_acc[st] += (lax.dot_general(dzb[:SB_Q], jnp.where(lo_k, k, kz), NN, preferred_element_type=F32)
                               + lax.dot_general(dzb[SB_Q:], jnp.where(lo_k, kz, k), NN, preferred_element_type=F32))
                dk_acc[rows(j), cols(st)] += lax.dot_general(dzb, qqs[st], TN, preferred_element_type=F32)
                dv_acc[rows(j), cols(st)] += lax.dot_general(ab, dds[st], TN, preferred_element_type=F32)

        def pair(p, carry):
            units([(st, 2 * p, None) for st in range(ns)] + [(st, 2 * p + 1, None) for st in range(ns)])
            return carry

        lax.fori_loop(0, i, pair, 0)
        units([(st, 2 * i, 0) for st in range(ns)] + [(st, 2 * i + 1, BLOCK) for st in range(ns)])
        for st in range(ns):
            dq_ref[:, st * LANES:(st + 1) * LANES] = (dq_acc[st] * scale).astype(BF16)

        @pl.when(i == nq - 1)
        def _():
            dk_ref[...] = dk_acc[...].astype(BF16)
            dv_ref[...] = dv_acc[...].astype(BF16)

    def seq_in(col0):
        return pl.BlockSpec((s, wide), lambda b, h, i: (b, col0 // ns + h))

    blk = pl.BlockSpec((SB_Q, wide), lambda b, h, i: (b * nq + i, h))
    seq = pl.BlockSpec((s, wide), lambda b, h, i: (b, h))
    out = jax.ShapeDtypeStruct((t, n_pairs * LANES), BF16)
    return _call(
        body, name="attn_b_bwd", grid=(batch, n_pairs // ns, nq),
        in_specs=[pl.BlockSpec((SB_Q, wide), lambda b, h, i: (b * nq + i, q_col0 // ns + h)), seq_in(k_col0),
                  seq_in(v_col0), blk, blk],
        out_specs=[blk, seq, seq], out_shape=[out, out, out],
        scratch=[pltpu.VMEM((s, wide), F32), pltpu.VMEM((s, wide), F32), pltpu.VMEM((ns, SB_Q, LANES), F32),
                 pltpu.VMEM((ns, 2 * SB_Q, LANES), F32), pltpu.VMEM((ns, 2 * SB_Q, LANES), F32)],
        sem=("parallel", "parallel", "arbitrary"), args=(proj, proj, proj, tot, do), ride=ride)


MEM_Q_TILE = 512


def _mem_fwd(q, kv, batch, s, n_mem):
    t, width = q.shape
    tq = min(MEM_Q_TILE, s)
    nq = s // tq
    scale = MEM_HEAD_DIM ** -0.5

    def body(q_ref, kv_ref, o_ref):
        for h in range(N_HEADS_MEM):
            cols = slice(h * MEM_HEAD_DIM, (h + 1) * MEM_HEAD_DIM)
            k = kv_ref[:, cols]
            v = kv_ref[:, width + h * MEM_HEAD_DIM: width + (h + 1) * MEM_HEAD_DIM]
            sc = lax.dot_general(q_ref[:, cols], k, NT, preferred_element_type=F32) * scale
            p = jnp.exp(sc - jnp.max(sc, axis=1, keepdims=True))
            p = p / jnp.sum(p, axis=1, keepdims=True)
            o_ref[:, cols] = lax.dot_general(p.astype(BF16), v, NN, preferred_element_type=F32).astype(BF16)

    return pl.pallas_call(
        body, name="mem_attn_fwd", grid=(batch, nq),
        in_specs=[pl.BlockSpec((tq, width), lambda b, i: (b * nq + i, 0)),
                  pl.BlockSpec((n_mem, 2 * width), lambda b, i: (b, 0))],
        out_specs=pl.BlockSpec((tq, width), lambda b, i: (b * nq + i, 0)),
        out_shape=jax.ShapeDtypeStruct((t, width), BF16),
        compiler_params=_params(("parallel", "parallel")),
    )(q, kv)


def _mem_bwd(q, kv, do, batch, s, n_mem):
    t, width = q.shape
    tq = min(MEM_Q_TILE, s)
    nq = s // tq
    scale = MEM_HEAD_DIM ** -0.5

    def body(q_ref, kv_ref, do_ref, dq_ref, dkv_ref, acc):
        i = pl.program_id(1)

        @pl.when(i == 0)
        def _():
            acc[...] = jnp.zeros_like(acc)

        for h in range(N_HEADS_MEM):
            cols = slice(h * MEM_HEAD_DIM, (h + 1) * MEM_HEAD_DIM)
            vcols = slice(width + h * MEM_HEAD_DIM, width + (h + 1) * MEM_HEAD_DIM)
            qh, k, v, doh = q_ref[:, cols], kv_ref[:, cols], kv_ref[:, vcols], do_ref[:, cols]
            sc = lax.dot_general(qh, k, NT, preferred_element_type=F32) * scale
            p = jnp.exp(sc - jnp.max(sc, axis=1, keepdims=True))
            p = p / jnp.sum(p, axis=1, keepdims=True)
            dp = lax.dot_general(doh, v, NT, preferred_element_type=F32)
            ds = (p * (dp - jnp.sum(p * dp, axis=1, keepdims=True)) * scale).astype(BF16)
            dq_ref[:, cols] = lax.dot_general(ds, k, NN, preferred_element_type=F32).astype(BF16)
            acc[:, cols] += lax.dot_general(ds, qh, TN, preferred_element_type=F32)
            acc[:, vcols] += lax.dot_general(p.astype(BF16), doh, TN, preferred_element_type=F32)

        @pl.when(i == nq - 1)
        def _():
            dkv_ref[...] = acc[...].astype(BF16)

    row = pl.BlockSpec((tq, width), lambda b, i: (b * nq + i, 0))
    kvs = pl.BlockSpec((n_mem, 2 * width), lambda b, i: (b, 0))
    return pl.pallas_call(
        body, name="mem_attn_bwd", grid=(batch, nq),
        in_specs=[row, kvs, row], out_specs=[row, kvs],
        out_shape=[jax.ShapeDtypeStruct((t, width), BF16), jax.ShapeDtypeStruct((batch * n_mem, 2 * width), BF16)],
        scratch_shapes=[pltpu.VMEM((n_mem, 2 * width), F32)],
        compiler_params=_params(("parallel", "arbitrary")),
    )(q, kv, do)


def _mixer_fwd(o_a, o_b, w_a, w_b, proj, gate_col0, w_out, x, g, w_q):
    t, width = o_a.shape
    d = w_a.shape[1]
    nq_cols = w_q.shape[1]
    tm = min(ROW_TILE, t)
    gb0 = gate_col0 * LANES // d

    def body(oa_ref, ob_ref, wa_ref, wb_ref, ga_ref, gb_ref, wo_ref, x_ref, g_ref, wq_ref, ua_ref, ub_ref, mix_ref,
             n_ref, h_ref, q_ref):
        ua = lax.dot_general(oa_ref[...], wa_ref[...], NN, preferred_element_type=F32)
        ub = lax.dot_general(ob_ref[...], wb_ref[...], NN, preferred_element_type=F32)
        ua_ref[...] = ua.astype(BF16)
        ub_ref[...] = ub.astype(BF16)
        mixed = (jax.nn.sigmoid(ga_ref[...].astype(F32)) * ua + jax.nn.sigmoid(gb_ref[...].astype(F32)) * ub).astype(BF16)
        mix_ref[...] = mixed
        h = lax.dot_general(mixed, wo_ref[...], NN, preferred_element_type=F32) + x_ref[...]
        h_ref[...] = h
        r = lax.rsqrt(jnp.mean(h * h, axis=-1, keepdims=True) + RMS_EPS)
        n = (h * r * g_ref[...]).astype(BF16)
        n_ref[...] = n
        q_ref[...] = lax.dot_general(n, wq_ref[...], NN, preferred_element_type=F32).astype(BF16)

    row = pl.BlockSpec((tm, width), lambda i: (i, 0))
    wsp = pl.BlockSpec((width, d), lambda i: (0, 0))
    out = pl.BlockSpec((tm, d), lambda i: (i, 0))
    osh = jax.ShapeDtypeStruct((t, d), BF16)
    return pl.pallas_call(
        body, name="mixer_fwd", grid=(t // tm,),
        in_specs=[row, row, wsp, wsp,
                  pl.BlockSpec((tm, d), lambda i: (i, gb0)), pl.BlockSpec((tm, d), lambda i: (i, gb0 + 1)),
                  pl.BlockSpec((d, d), lambda i: (0, 0)), out, pl.BlockSpec((1, d), lambda i: (0, 0)),
                  pl.BlockSpec((d, nq_cols), lambda i: (0, 0))],
        out_specs=[out, out, out, out, out, pl.BlockSpec((tm, nq_cols), lambda i: (i, 0))],
        out_shape=[osh, osh, osh, osh, jax.ShapeDtypeStruct((t, d), F32), jax.ShapeDtypeStruct((t, nq_cols), BF16)],
        compiler_params=_params(("parallel",)),
    )(o_a, o_b, w_a, w_b, proj, proj, w_out, x, g, w_q)


def _mixer_bwd(dh, w_out, ua, ub, proj, gate_col0, w_a, w_b):
    t, d = dh.shape
    width = w_a.shape[0]
    tm = min(ROW_TILE, t)
    nc = d // LANES

    def body(dh_ref, w_ref, ua_ref, ub_ref, ga_ref, gb_ref, wa_ref, wb_ref, dua_ref, dub_ref, dg_ref, doa_ref, dob_ref):
        dm = lax.dot_general(dh_ref[...], w_ref[...], NT, preferred_element_type=F32)
        sa = jax.nn.sigmoid(ga_ref[...].astype(F32))
        sb = jax.nn.sigmoid(gb_ref[...].astype(F32))
        dua = (dm * sa).astype(BF16)
        dub = (dm * sb).astype(BF16)
        dua_ref[...] = dua
        dub_ref[...] = dub
        dg_ref[:, 0:d] = (dm * ua_ref[...].astype(F32) * sa * (1.0 - sa)).astype(BF16)
        dg_ref[:, d:2 * d] = (dm * ub_ref[...].astype(F32) * sb * (1.0 - sb)).astype(BF16)
        doa_ref[...] = lax.dot_general(dua, wa_ref[...], NT, preferred_element_type=F32).astype(BF16)
        dob_ref[...] = lax.dot_general(dub, wb_ref[...], NT, preferred_element_type=F32).astype(BF16)

    row = pl.BlockSpec((tm, d), lambda i: (i, 0))
    wsp = pl.BlockSpec((width, d), lambda i: (0, 0))
    osp = pl.BlockSpec((tm, width), lambda i: (i, 0))
    return pl.pallas_call(
        body, name="mixer_bwd", grid=(t // tm,),
        in_specs=[row, pl.BlockSpec((d, d), lambda i: (0, 0)), row, row,
                  pl.BlockSpec((tm, d), lambda i: (i, gate_col0 // nc)),
                  pl.BlockSpec((tm, d), lambda i: (i, gate_col0 // nc + 1)), wsp, wsp],
        out_specs=[row, row, pl.BlockSpec((tm, 2 * d), lambda i: (i, 0)), osp, osp],
        out_shape=[jax.ShapeDtypeStruct((t, d), BF16), jax.ShapeDtypeStruct((t, d), BF16),
                   jax.ShapeDtypeStruct((t, 2 * d), BF16), jax.ShapeDtypeStruct((t, width), BF16),
                   jax.ShapeDtypeStruct((t, width), BF16)],
        compiler_params=_params(("parallel",)),
    )(dh, w_out, ua, ub, proj, proj, w_a, w_b)


FFN_COLS = 1024


def _ffn_up(n, w_gate, w_up):
    t, d = n.shape
    hidden = w_gate.shape[0]
    tm = min(ROW_TILE, t)
    tn = min(FFN_COLS, hidden)

    def body(n_ref, wg_ref, wu_ref, hg_ref, hu_ref, act_ref):
        hg = lax.dot_general(n_ref[...], wg_ref[...], NT, preferred_element_type=F32)
        hu = lax.dot_general(n_ref[...], wu_ref[...], NT, preferred_element_type=F32)
        hg_ref[...] = hg.astype(BF16)
        hu_ref[...] = hu.astype(BF16)
        act_ref[...] = (hg * jax.nn.sigmoid(hg) * hu).astype(BF16)

    wsp = pl.BlockSpec((tn, d), lambda j, i: (j, 0))
    out = pl.BlockSpec((tm, tn), lambda j, i: (i, j))
    osh = jax.ShapeDtypeStruct((t, hidden), BF16)
    return pl.pallas_call(
        body, name="ffn_up", grid=(hidden // tn, t // tm),
        in_specs=[pl.BlockSpec((tm, d), lambda j, i: (i, 0)), wsp, wsp],
        out_specs=[out, out, out], out_shape=[osh, osh, osh],
        compiler_params=_params(("parallel", "parallel")),
    )(n, w_gate, w_up)


def _ffn_bwd(dh, w_down, w_gate, w_up, hg, hu, x, g, dres, w_prev):
    t, d = dh.shape
    hidden = w_down.shape[0]
    q = w_prev.shape[0]
    tm = min(ROW_TILE, t)
    tn = min(FFN_COLS, hidden)
    nj = hidden // tn

    def body(dh_ref, wd_ref, wg_ref, wu_ref, hg_ref, hu_ref, x_ref, g_ref, r_ref, wp_ref, dhg_ref, dhu_ref, dx_ref,
             dxb_ref, dg_ref, do_ref, acc):
        j, i = pl.program_id(0), pl.program_id(1)
        dact = lax.dot_general(dh_ref[...], wd_ref[...], NT, preferred_element_type=F32)
        hg = hg_ref[...].astype(F32)
        sg = jax.nn.sigmoid(hg)
        dhu = (dact * hg * sg).astype(BF16)
        dhg = (dact * hu_ref[...].astype(F32) * sg * (1.0 + hg * (1.0 - sg))).astype(BF16)
        dhu_ref[...] = dhu
        dhg_ref[...] = dhg
        part = (lax.dot_general(dhg, wg_ref[...], NN, preferred_element_type=F32)
                + lax.dot_general(dhu, wu_ref[...], NN, preferred_element_type=F32))

        @pl.when(j == 0)
        def _():
            acc[i] = part

        @pl.when(j > 0)
        def _():
            acc[i] += part

        @pl.when(jnp.logical_and(j == 0, i == 0))
        def _():
            dg_ref[...] = jnp.zeros_like(dg_ref)

        @pl.when(j == nj - 1)
        def _():
            dx, dg = _rms_bwd_rows(acc[i], x_ref[...], g_ref[...], r_ref[...])
            dx_ref[...] = dx
            dxb = dx.astype(BF16)
            dxb_ref[...] = dxb
            dg_ref[...] += dg
            do_ref[...] = lax.dot_general(dxb, wp_ref[...], NT, preferred_element_type=F32).astype(BF16)

    hid = pl.BlockSpec((tm, tn), lambda j, i: (i, j))
    wsp = pl.BlockSpec((tn, d), lambda j, i: (j, 0), pipeline_mode=pl.Buffered(1))
    late = pl.BlockSpec((tm, d), lambda j, i: (jnp.where(j == nj - 1, i, 0), 0))
    late_q = pl.BlockSpec((tm, q), lambda j, i: (jnp.where(j == nj - 1, i, 0), 0))
    vec = pl.BlockSpec((1, d), lambda j, i: (0, 0))
    osh = jax.ShapeDtypeStruct((t, hidden), BF16)
    return pl.pallas_call(
        body, name="ffn_bwd", grid=(nj, t // tm),
        in_specs=[pl.BlockSpec((tm, d), lambda j, i: (i, 0)), wsp, wsp, wsp, hid, hid, late, vec, late,
                  pl.BlockSpec((q, d), lambda j, i: (0, 0), pipeline_mode=pl.Buffered(1))],
        out_specs=[hid, hid, late, late, vec, late_q],
        out_shape=[osh, osh, jax.ShapeDtypeStruct((t, d), F32), jax.ShapeDtypeStruct((t, d), BF16),
                   jax.ShapeDtypeStruct((1, d), F32), jax.ShapeDtypeStruct((t, q), BF16)],
        scratch_shapes=[pltpu.VMEM((t // tm, tm, d), F32)],
        compiler_params=_params(("arbitrary", "arbitrary")),
    )(dh, w_down, w_gate, w_up, hg, hu, x, g, dres, w_prev)


MM_ROWS = 1024


def _mm_w(name, a, w, out_dtype, dims=NN):
    t, k = a.shape
    n = w.shape[1] if dims == NN else w.shape[0]
    tm, tn = min(MM_ROWS, t), min(1024, n)
    o_spec = pl.BlockSpec((tm, tn), lambda j, i: (i, j))
    b_spec = pl.BlockSpec((k, tn), lambda j, i: (0, j)) if dims == NN else pl.BlockSpec((tn, k), lambda j, i: (j, 0))
    return _mm(name, a, w, grid=(n // tn, t // tm), a_spec=pl.BlockSpec((tm, k), lambda j, i: (i, 0)), b_spec=b_spec,
               o_shape=(t, n), o_spec=o_spec, dims=dims, out_dtype=out_dtype)


def _mm_res_norm(name, a, w, res, g):
    t, k = a.shape
    d = w.shape[1]
    tm = min(ROW_TILE, t)

    def body(a_ref, w_ref, r_ref, g_ref, h_ref, n_ref):
        h = lax.dot_general(a_ref[...], w_ref[...], NN, preferred_element_type=F32) + r_ref[...]
        h_ref[...] = h
        r = lax.rsqrt(jnp.mean(h * h, axis=-1, keepdims=True) + RMS_EPS)
        n_ref[...] = (h * r * g_ref[...]).astype(BF16)

    row = pl.BlockSpec((tm, d), lambda i: (i, 0))
    return pl.pallas_call(
        body, name=name, grid=(t // tm,),
        in_specs=[pl.BlockSpec((tm, k), lambda i: (i, 0)), pl.BlockSpec((k, d), lambda i: (0, 0)), row,
                  pl.BlockSpec((1, d), lambda i: (0, 0))],
        out_specs=[row, row], out_shape=[jax.ShapeDtypeStruct((t, d), F32), jax.ShapeDtypeStruct((t, d), BF16)],
        compiler_params=_params(("parallel",)),
    )(a, w, res, g)


def _wgrad(name, a, g, tk=1024, tn=1024):
    t, k = a.shape
    n = g.shape[1]
    tm, tk, tn = min(2 * MM_ROWS, t), min(tk, k), min(tn, n)
    return _mm(name, a, g, grid=(k // tk, n // tn, t // tm),
               a_spec=pl.BlockSpec((tm, tk), lambda p, q, r: (r, p)), b_spec=pl.BlockSpec((tm, tn), lambda p, q, r: (r, q)),
               o_shape=(k, n), o_spec=pl.BlockSpec((tk, tn), lambda p, q, r: (p, q)), dims=TN, out_dtype=BF16, nk=t // tm)


def _peers():
    x, y, c = lax.axis_index("x"), lax.axis_index("y"), lax.axis_index("c")
    me = 4 * x + 2 * y + c
    out = []
    for k in range(1, N_DEV):
        kx, ky, kc = (k >> 2) & 1, (k >> 1) & 1, k & 1
        px = 1 - x if kx else x
        py = 1 - y if ky else y
        pc = 1 - c if kc else c
        out.append(((px, py, pc), 4 * px + 2 * py + pc))
    return me, out


def _cast_weights(ws, pad_rows):
    def body(*refs):
        n = len(refs) // 2
        for i_ref, o_ref, pr in zip(refs[:n], refs[n:], pad_rows):
            r, c = i_ref.shape
            o_ref[0:r, :] = i_ref[...].astype(BF16)
            if pr:
                o_ref[r:r + pr, :] = jnp.zeros((pr, c), BF16)

    return pl.pallas_call(
        body, name="cast_weights", in_specs=[VMEM] * len(ws), out_specs=[VMEM] * len(ws),
        out_shape=[jax.ShapeDtypeStruct((w.shape[0] + pr, w.shape[1]), BF16) for w, pr in zip(ws, pad_rows)],
    )(*ws)


def _window(ref, j, c):
    return ref.at[:, pl.ds(pl.multiple_of(j * c, LANES), c)]


def _scatter_copies(ins, outs, sems, cols, landed):
    send_sems, recv_sems, loc_sems = sems
    n_peer = N_DEV - 1
    me, peers = _peers()

    def src(w, j):
        return _window(ins[w], j, cols[w]) if cols[w] else ins[w].at[j]

    local = [pltpu.make_async_copy(src(w, me), outs[w].at[me], loc_sems.at[w]) for w in range(len(ins))]
    remote = [pltpu.make_async_remote_copy(
        src_ref=src(w, idx), dst_ref=outs[w].at[idx if landed else me],
        send_sem=send_sems.at[w * n_peer + k], recv_sem=recv_sems.at[w * n_peer + k],
        device_id=dev, device_id_type=pl.DeviceIdType.MESH)
        for k, (dev, idx) in reversed(list(enumerate(peers))) for w in range(len(ins))]
    return local, remote


OTHER_CHIPS = (2, 4, 6)


def _gather_copies(ins, outs, sems, cols):
    send_sems, recv_sems, loc_sems = sems
    x, y, c = lax.axis_index("x"), lax.axis_index("y"), lax.axis_index("c")
    me = 4 * x + 2 * y + c
    n_pair = N_DEV - 1

    def dev(mask):
        return (1 - x if mask & 4 else x, 1 - y if mask & 2 else y, 1 - c if mask & 1 else c)

    def slot(w, mask):
        j = jnp.bitwise_xor(me, mask)
        return _window(outs[w], j, cols[w]) if cols[w] else outs[w].at[j]

    def remote(w, pair, src, to_slot, target):
        return pltpu.make_async_remote_copy(src_ref=src, dst_ref=slot(w, to_slot), send_sem=send_sems.at[w * n_pair + pair],
                                            recv_sem=recv_sems.at[w * n_pair + pair], device_id=dev(target),
                                            device_id_type=pl.DeviceIdType.MESH)

    ws = range(len(ins))
    return dict(
        local=[pltpu.make_async_copy(ins[w], slot(w, 0), loc_sems.at[w]) for w in ws],
        to_chips=[remote(w, 1 + t, ins[w], 0, m) for t, m in enumerate(OTHER_CHIPS) for w in ws],
        to_core=[remote(w, 0, ins[w], 0, 1) for w in ws],
        from_chips=[remote(w, 1 + t, ins[w], m, 0) for t, m in enumerate(OTHER_CHIPS) for w in ws],
        pass_on=[remote(w, 4 + t, slot(w, m), m, 1) for t, m in enumerate(OTHER_CHIPS) for w in ws],
        from_core=[remote(w, 0, ins[w], 1, 0) for w in ws]
        + [remote(w, 4 + t, ins[w], m + 1, 0) for t, m in enumerate(OTHER_CHIPS) for w in ws])


def _exchange_start(ins, outs, sems, gather, cols):
    if gather:
        cps = _gather_copies(ins, outs, sems, cols)
        for cp in cps["local"] + cps["to_chips"] + cps["to_core"]:
            cp.start()
    else:
        local, remote = _scatter_copies(ins, outs, sems, cols, False)
        for cp in local + remote:
            cp.start()


def _exchange_pass_on(ins, outs, sems, gather, cols, chips):
    if gather:
        cps = _gather_copies(ins, outs, sems, cols)
        n = len(ins)
        for t in chips:
            for arrived, onward in zip(cps["from_chips"][t * n:(t + 1) * n], cps["pass_on"][t * n:(t + 1) * n]):
                arrived.wait_recv()
                onward.start()


def _exchange_wait(ins, outs, sems, gather, cols):
    if gather:
        cps = _gather_copies(ins, outs, sems, cols)
        for cp in cps["local"]:
            cp.wait()
        for cp in cps["to_chips"] + cps["to_core"] + cps["pass_on"]:
            cp.wait_send()
        for cp in cps["from_core"]:
            cp.wait_recv()
    else:
        local, remote = _scatter_copies(ins, outs, sems, cols, True)
        for cp in local:
            cp.wait()
        for cp in remote:
            cp.wait_send()
            cp.wait_recv()


def _exchange_shapes(arrs, gather, cols):
    n = len(arrs)
    out_shape = []
    for a, c in zip(arrs, cols):
        if gather:
            shape = (a.shape[0], N_DEV * c) if c else (N_DEV,) + a.shape
        else:
            shape = (N_DEV, a.shape[0], c) if c else a.shape
        out_shape.append(jax.ShapeDtypeStruct(shape, a.dtype))
    sems = [pltpu.SemaphoreType.DMA((n * (N_DEV - 1),)), pltpu.SemaphoreType.DMA((n * (N_DEV - 1),)),
            pltpu.SemaphoreType.DMA((n,))]
    return out_shape, sems


def _call(body, *, name, grid, in_specs, out_specs, out_shape, scratch, sem, args, ride=None):
    if ride is None:
        outs = pl.pallas_call(body, name=name, grid=grid, in_specs=in_specs, out_specs=out_specs, out_shape=out_shape,
                              scratch_shapes=scratch, compiler_params=_params(sem))(*args)
        return outs, None
    arrs, gather, cols = ride
    n, n_in, n_out, n_scr = len(arrs), len(in_specs), len(out_specs), len(scratch)
    x_shape, x_sems = _exchange_shapes(arrs, gather, cols)

    def riding(*refs):
        ins, x_ins = refs[:n_in], refs[n_in:n_in + n]
        outs = refs[n_in + n:n_in + n + n_out]
        x_outs = refs[n_in + n + n_out:n_in + 2 * n + n_out]
        scr = refs[n_in + 2 * n + n_out:n_in + 2 * n + n_out + n_scr]
        sems = refs[n_in + 2 * n + n_out + n_scr:]
        def at(step):
            return functools.reduce(jnp.logical_and, [pl.program_id(a) == v for a, v in enumerate(step)])

        @pl.when(at((0,) * len(grid)))
        def _():
            _exchange_start(x_ins, x_outs, sems, gather, cols)

        @pl.when(at((grid[0] // 2,) + (0,) * (len(grid) - 2) + (grid[-1] // 2,)))
        def _():
            _exchange_pass_on(x_ins, x_outs, sems, gather, cols, (0, 1))

        @pl.when(at((grid[0] // 2,) + (0,) * (len(grid) - 2) + (3 * grid[-1] // 4,)))
        def _():
            _exchange_pass_on(x_ins, x_outs, sems, gather, cols, (2,))

        body(*ins, *outs, *scr)

        @pl.when(at(tuple(g - 1 for g in grid)))
        def _():
            _exchange_wait(x_ins, x_outs, sems, gather, cols)

    res = pl.pallas_call(
        riding, name=name, grid=grid, in_specs=list(in_specs) + [ANY] * n, out_specs=list(out_specs) + [ANY] * n,
        out_shape=list(out_shape) + x_shape, scratch_shapes=list(scratch) + x_sems,
        compiler_params=_params(("arbitrary",) * len(grid)))(*args, *arrs)
    return res[:n_out], res[n_out:]


def _my_block():
    return (4 * lax.axis_index("x") + 2 * lax.axis_index("y") + lax.axis_index("c")).astype(jnp.int32).reshape(1)


def _proj_in_gather(x, g, w_shard):
    t, k = x.shape
    cs = w_shard.shape[1]
    tm = min(MM_ROWS, t)
    ni = t // tm
    arrival = (0, 1, 2, 4, 3, 5, 6, 7)

    def mask_at(s):
        return jnp.where(s == 3, 4, jnp.where(s == 4, 3, s))

    def body(me_ref, x_ref, g_ref, w_hbm, o_ref, all_hbm, n_hbm, w_vmem, n_vmem, send_sems, recv_sems, loc_sems,
             load_sems, n_sem):
        s, i = pl.program_id(0), pl.program_id(1)
        cps = _gather_copies([w_hbm], [all_hbm], (send_sems, recv_sems, loc_sems), (cs,))
        by_mask = {0: cps["local"][0], 1: cps["from_core"][0]}
        for t_chip, m in enumerate(OTHER_CHIPS):
            by_mask[m] = cps["from_chips"][t_chip]
            by_mask[m + 1] = cps["from_core"][1 + t_chip]
        arrived = [by_mask[m] for m in arrival]

        def load(step):
            src = w_hbm if step == 0 else _window(all_hbm, jnp.bitwise_xor(me_ref[0], arrival[step]), cs)
            return pltpu.make_async_copy(src, w_vmem.at[step % 2], load_sems.at[step % 2])

        @pl.when(jnp.logical_and(s == 0, i == 0))
        def _():
            for cp in cps["local"] + cps["to_chips"] + cps["to_core"]:
                cp.start()
            load(0).start()

        for step, mask in enumerate(arrival):
            @pl.when(jnp.logical_and(s == step, i == 0))
            def _(step=step):
                load(step).wait()

            if step + 1 < N_DEV:
                @pl.when(jnp.logical_and(s == step, i == min(1, ni - 1)))
                def _(step=step):
                    arrived[step + 1].wait_recv()
                    if arrival[step + 1] in OTHER_CHIPS:
                        cps["pass_on"][OTHER_CHIPS.index(arrival[step + 1])].start()
                    load(step + 1).start()

        @pl.when(s == 0)
        def _():
            xf = x_ref[...]
            r = lax.rsqrt(jnp.mean(xf * xf, axis=-1, keepdims=True) + RMS_EPS)
            n_vmem[i] = (xf * r * g_ref[...]).astype(BF16)
            keep = pltpu.make_async_copy(n_vmem.at[i], n_hbm.at[pl.ds(pl.multiple_of(i * tm, tm), tm), :], n_sem)
            keep.start()
            keep.wait()

        o_ref[...] = lax.dot_general(n_vmem[i], w_vmem[s % 2], NN, preferred_element_type=F32).astype(BF16)

        @pl.when(jnp.logical_and(s == N_DEV - 1, i == ni - 1))
        def _():
            cps["local"][0].wait()
            for cp in cps["to_chips"] + cps["to_core"] + cps["pass_on"]:
                cp.wait_send()

    return pl.pallas_call(
        body, name="proj_in",
        grid_spec=pltpu.PrefetchScalarGridSpec(
            num_scalar_prefetch=1, grid=(N_DEV, ni),
            in_specs=[pl.BlockSpec((tm, k), lambda s, i, me: (jnp.where(s == 0, i, 0), 0)),
                      pl.BlockSpec((1, k), lambda s, i, me: (0, 0)), ANY],
            out_specs=[pl.BlockSpec((tm, cs), lambda s, i, me: (i, jnp.bitwise_xor(me[0], mask_at(s)))), ANY, ANY],
            scratch_shapes=[pltpu.VMEM((2, k, cs), BF16), pltpu.VMEM((ni, tm, k), BF16),
                            pltpu.SemaphoreType.DMA((N_DEV - 1,)), pltpu.SemaphoreType.DMA((N_DEV - 1,)),
                            pltpu.SemaphoreType.DMA((1,)), pltpu.SemaphoreType.DMA((2,)), pltpu.SemaphoreType.DMA]),
        out_shape=[jax.ShapeDtypeStruct((t, N_DEV * cs), BF16), jax.ShapeDtypeStruct((k, N_DEV * cs), BF16),
                   jax.ShapeDtypeStruct((t, k), BF16)],
        compiler_params=_params(("arbitrary", "arbitrary")),
    )(_my_block(), x, g, w_shard)


def _gw_in_scatter(a, g):
    t, k = a.shape
    cs = g.shape[1] // N_DEV
    tm = min(MM_ROWS, t)
    nr = t // tm
    n_chip = N_DEV // 2
    chips = (6, 4, 2, 0)

    def body(me_ref, a_ref, g_ref, out_hbm, acc, stage, other, core_send, core_recv, chip_send, chip_recv, loc_sem):
        s, r = pl.program_id(0), pl.program_id(1)
        x, y, c = lax.axis_index("x"), lax.axis_index("y"), lax.axis_index("c")
        my_chip = 2 * x + y
        part = lax.dot_general(a_ref[...], g_ref[...], TN, preferred_element_type=F32)

        def to_core(m):
            return pltpu.make_async_remote_copy(src_ref=stage.at[0], dst_ref=other.at[m], send_sem=core_send.at[m],
                                                recv_sem=core_recv.at[m], device_id=(x, y, 1 - c),
                                                device_id_type=pl.DeviceIdType.MESH)

        def to_chip(m, landed):
            mask = chips[m]
            there = (1 - x if mask & 4 else x, 1 - y if mask & 2 else y, c)
            slot = (2 * there[0] + there[1]) if landed else my_chip
            return pltpu.make_async_remote_copy(src_ref=stage.at[1], dst_ref=out_hbm.at[slot], send_sem=chip_send.at[m],
                                                recv_sem=chip_recv.at[m], device_id=there,
                                                device_id_type=pl.DeviceIdType.MESH)

        local = pltpu.make_async_copy(stage.at[1], out_hbm.at[my_chip], loc_sem)

        @pl.when(r == 0)
        def _():
            acc[...] = part

        @pl.when(r > 0)
        def _():
            acc[...] += part

        for step in range(N_DEV):
            m = step // 2

            @pl.when(jnp.logical_and(s == step, r == nr - 1))
            def _(step=step, m=m):
                if step % 2 == 0:
                    if m > 0:
                        to_core(m - 1).wait_send()
                    stage[0] = acc[...].astype(BF16)
                    to_core(m).start()
                else:
                    if m > 0:
                        to_chip(m - 1, False).wait_send()
                    to_core(m).wait_recv()
                    stage[1] = (acc[...] + other[m].astype(F32)).astype(BF16)
                    if m < n_chip - 1:
                        to_chip(m, False).start()
                    else:
                        local.start()
                        to_core(m).wait_send()
                        local.wait()
                        for mm in range(n_chip - 1):
                            to_chip(mm, True).wait_recv()

    return pl.pallas_call(
        body, name="gw_in",
        grid_spec=pltpu.PrefetchScalarGridSpec(
            num_scalar_prefetch=1, grid=(N_DEV, nr),
            in_specs=[pl.BlockSpec((tm, k), lambda s, r, me: (r, 0)),
                      pl.BlockSpec((tm, cs), lambda s, r, me: (r, jnp.bitwise_xor(me[0], N_DEV - 1 - s)))],
            out_specs=ANY,
            scratch_shapes=[pltpu.VMEM((k, cs), F32), pltpu.VMEM((2, k, cs), BF16), pltpu.VMEM((n_chip, k, cs), BF16),
                            pltpu.SemaphoreType.DMA((n_chip,)), pltpu.SemaphoreType.DMA((n_chip,)),
                            pltpu.SemaphoreType.DMA((n_chip - 1,)), pltpu.SemaphoreType.DMA((n_chip - 1,)),
                            pltpu.SemaphoreType.DMA]),
        out_shape=jax.ShapeDtypeStruct((n_chip, k, cs), BF16),
        compiler_params=_params(("arbitrary", "arbitrary")),
    )(_my_block(), a, g)


SMALL_ROWS = 8


def _allreduce_small(parts, loss_part):
    n, d = len(parts), parts[0].shape[1]

    def body(*refs):
        part_refs, loss_ref, o_ref = refs[:n], refs[n], refs[n + 1]
        mine_ref, all_ref, send_sems, recv_sems = refs[n + 2:]
        me, peers = _peers()
        mine_ref[...] = jnp.zeros_like(mine_ref)
        for i, p_ref in enumerate(part_refs):
            mine_ref[i:i + 1, :] = p_ref[...]
        mine_ref[SMALL_ROWS - 1:SMALL_ROWS, 0:LANES] = loss_ref[0:1, :]
        all_ref[me] = mine_ref[...]
        for k, (dev, idx) in enumerate(peers):
            pltpu.make_async_remote_copy(src_ref=mine_ref, dst_ref=all_ref.at[me], send_sem=send_sems.at[k],
                                         recv_sem=recv_sems.at[k], device_id=dev,
                                         device_id_type=pl.DeviceIdType.MESH).start()
        for k, (dev, idx) in enumerate(peers):
            cp = pltpu.make_async_remote_copy(src_ref=mine_ref, dst_ref=all_ref.at[idx], send_sem=send_sems.at[k],
                                              recv_sem=recv_sems.at[k], device_id=dev,
                                              device_id_type=pl.DeviceIdType.MESH)
            cp.wait_send()
            cp.wait_recv()
        tot = all_ref[0]
        for dvc in range(1, N_DEV):
            tot = tot + all_ref[dvc]
        o_ref[...] = tot

    return pl.pallas_call(
        body, name="allreduce_small", in_specs=[VMEM] * (n + 1), out_specs=VMEM,
        out_shape=jax.ShapeDtypeStruct((SMALL_ROWS, d), F32),
        scratch_shapes=[pltpu.VMEM((SMALL_ROWS, d), F32), pltpu.VMEM((N_DEV, SMALL_ROWS, d), F32),
                        pltpu.SemaphoreType.DMA((N_DEV - 1,)), pltpu.SemaphoreType.DMA((N_DEV - 1,))],
    )(*parts, loss_part)


def _adam_math(g, w, m, v):
    m_new = ADAM_B1 * m + (1.0 - ADAM_B1) * g
    v_new = ADAM_B2 * v + (1.0 - ADAM_B2) * (g * g)
    m_hat = m_new / (1.0 - ADAM_B1 ** ADAM_STEP)
    v_hat = v_new / (1.0 - ADAM_B2 ** ADAM_STEP)
    delta = -ADAM_LR * (m_hat / (jnp.sqrt(v_hat) + ADAM_EPS) + ADAM_WD * w)
    return delta, m_new, v_new


def _adam(name, pieces, w, m, v):
    r, c = w.shape
    n_piece, _, cp = pieces.shape
    tr = r
    for cand in (256, 176, 128, 64):
        if r % cand == 0 and r > cand:
            tr = cand
            break

    def body(p_ref, w_ref, m_ref, v_ref, g_ref, d_ref, mo_ref, vo_ref):
        g = p_ref[0, :, 0:c].astype(F32)
        for j in range(1, n_piece):
            g = g + p_ref[j, :, 0:c].astype(F32)
        delta, m_new, v_new = _adam_math(g, w_ref[...], m_ref[...], v_ref[...])
        g_ref[...] = g
        d_ref[...] = delta
        mo_ref[...] = m_new
        vo_ref[...] = v_new

    blk = pl.BlockSpec((tr, c), lambda i: (i, 0))
    osh = jax.ShapeDtypeStruct((r, c), F32)
    return pl.pallas_call(
        body, name=name, grid=(r // tr,),
        in_specs=[pl.BlockSpec((n_piece, tr, cp), lambda i: (0, i, 0)), blk, blk, blk],
        out_specs=[blk, blk, blk, blk], out_shape=[osh, osh, osh, osh],
        compiler_params=_params(("parallel",)),
    )(pieces, w, m, v)


def _adam_small(g_all, ws, ms, vs):
    n = len(ws)

    def body(*refs):
        g_ref, ins, outs = refs[0], refs[1:1 + 3 * n], refs[1 + 3 * n:]
        for i in range(n):
            g = g_ref[i:i + 1, :]
            delta, m_new, v_new = _adam_math(g, ins[i][...], ins[n + i][...], ins[2 * n + i][...])
            for kind, val in enumerate((g, delta, m_new, v_new)):
                outs[kind * n + i][...] = val

    osh = jax.ShapeDtypeStruct(ws[0].shape, F32)
    res = pl.pallas_call(body, name="adam_small", in_specs=[VMEM] * (1 + 3 * n), out_specs=[VMEM] * (4 * n),
                         out_shape=[osh] * (4 * n))(g_all, *ws, *ms, *vs)
    return res[:n], res[n:2 * n], res[2 * n:3 * n], res[3 * n:]


def _local_step(x, mem, pos, tgt, gains, w_in_shard, shards, batch):
    g_mix, g_mem_q, g_mem_kv, g_ffn, g_final = gains
    t, d = x.shape
    s = t // batch
    n_mem = mem.shape[0] // batch
    n_sh = N_DEV
    width = shards[0].shape[0]
    nb = width // LANES

    lane = np.arange(LANES) % HEAD_DIM
    sel_lo = (lane < ROPE_HALF).astype(np.float32)[None, :]
    sel_hi = ((lane >= ROPE_HALF) & (lane < 2 * ROPE_HALF)).astype(np.float32)[None, :]
    freqs = np.float32(ROPE_THETA) ** (-np.arange(ROPE_HALF, dtype=np.float32) / np.float32(ROPE_HALF))
    inv_freq = np.where(lane < 2 * ROPE_HALF, freqs[lane % ROPE_HALF], 0.0).astype(np.float32)[None, :]
    cos_t, sin_a, sin_b = _rope_tables(pos, jnp.asarray(inv_freq), jnp.asarray(sel_lo), jnp.asarray(sel_hi))
    bias = _dilated_bias_tiles(s)

    proj, w_in, n1 = _proj_in_gather(x, g_mix, w_in_shard)
    qk_a = _rope_apply("rope_fwd", [proj], 2 * width, cos_t, sin_a, sin_b, 1.0)
    cs_up = shards[0].shape[1]
    (o_a, lse_a), (w_up_a, w_up_b, w_out, w_q, w_kv, w_o, w_fd) = _da_fwd(
        qk_a, proj, 2 * nb, bias, batch, s,
        ride=(shards[:6] + shards[8:], True, (cs_up, cs_up, 0, 0, 0, cs_up, 0)))
    (o_b, tot_b), (w_fg, w_fu) = _sb_fwd(proj, 3 * nb, 4 * nb, 5 * nb, batch, s, ride=(shards[6:8], True, (0, 0)))
    w_out = w_out.reshape(d, d)
    w_q = w_q.reshape(d, -1)
    w_kv = w_kv.reshape(d, -1)
    w_fd = w_fd.reshape(-1, d)
    w_fg = w_fg.reshape(-1, d)
    w_fu = w_fu.reshape(-1, d)
    ua, ub, mixed, n2, h1, q_m = _mixer_fwd(o_a, o_b, w_up_a, w_up_b, proj, 6 * nb, w_out, x, g_mem_q, w_q)
    mem_n = _rms_fwd("norm_mem_kv", mem, g_mem_kv)
    kv_m = _mm_w("mem_kv", mem_n, w_kv, BF16)
    o_m = _mem_fwd(q_m, kv_m, batch, s, n_mem)
    h2, n3 = _mm_res_norm("mem_out", o_m, w_o, h1, g_ffn)
    hg, hu, act = _ffn_up(n3, w_fg, w_fu)
    loss_part, dh3, dh3_b, dg_final = _loss_head(act, w_fd, h2, tgt, g_final.reshape(1, d))

    dhg, dhu, dh2, dh2_b, dg_ffn, do_m = _ffn_bwd(dh3_b, w_fd, w_fg, w_fu, hg, hu, h2, g_ffn, dh3, w_o)
    gw_fd = _wgrad("gw_ffn_down", act, dh3_b)
    gw_fg = _wgrad("gw_ffn_gate", dhg, n3)
    gw_fu = _wgrad("gw_ffn_up", dhu, n3)

    gw_o = _wgrad("gw_mem_o", o_m, dh2_b)
    dq_m, dkv_m = _mem_bwd(q_m, kv_m, do_m, batch, s, n_mem)
    gw_q = _wgrad("gw_mem_q", n2, dq_m)
    gw_kv = _wgrad("gw_mem_kv", mem_n, dkv_m)
    (dg_mem_kv,) = _rms_bwd("norm_mem_kv_bwd", (dkv_m, w_kv, NT), mem, g_mem_kv, None, ())
    dh1, dh1_b, dg_mem_q = _rms_bwd("norm_mem_q_bwd", (dq_m, w_q, NT), h1, g_mem_q, dh2, ("f32", "bf16"))

    gw_out = _wgrad("gw_out", mixed, dh1_b)
    dua, dub, dgates, do_a, do_b = _mixer_bwd(dh1_b, w_out, ua, ub, proj, 6 * nb, w_up_a, w_up_b)
    gw_ua = _wgrad("gw_up_a", o_a, dua)
    gw_ub = _wgrad("gw_up_b", o_b, dub)
    (dq_ar, dk_ar, dv_a), (p_fg, p_fd) = _da_bwd(
        qk_a, proj, 2 * nb, bias, o_a, lse_a, do_a, batch, s,
        ride=([gw_fg.reshape(n_sh, -1, d), gw_fd.reshape(n_sh, -1, d)], False, (0, 0)))
    mid = [gw_ua, gw_ub, gw_out.reshape(n_sh, -1, d), gw_q.reshape(n_sh, -1, gw_q.shape[1]),
           gw_kv.reshape(n_sh, -1, gw_kv.shape[1]), gw_o, gw_fu.reshape(n_sh, -1, d)]
    (dq_b, dk_b, dv_b), (*p_mid, p_fu) = _sb_bwd(proj, 3 * nb, 4 * nb, 5 * nb, tot_b, do_b, batch, s,
                                                 ride=(mid, False, (cs_up, cs_up, 0, 0, 0, cs_up, 0)))
    p_ffn = [p_fg, p_fu, p_fd]
    dproj = _rope_apply("rope_bwd", [dq_ar, dk_ar], width, cos_t, sin_a, sin_b, -1.0,
                        tail=(dv_a, dq_b, dk_b, dv_b, dgates))
    grad_x, dg_mix = _rms_bwd("proj_in_bwd", (dproj, w_in, NT), x, g_mix, dh1, ("f32",))
    p_in = _gw_in_scatter(n1, dproj)
    return loss_part, grad_x, [p_in] + list(p_mid) + p_ffn, (dg_mix, dg_mem_q, dg_mem_kv, dg_ffn, dg_final)


WEIGHTS =("w_in", "w_up_a", "w_up_b", "w_out", "w_q_mem", "w_kv_mem", "w_o_mem", "w_ffn_gate", "w_ffn_up", "w_ffn_down")
GAINS = ("g_mix", "g_mem_q", "g_mem_kv", "g_ffn", "g_final")
ORDER = ("g_mix", "w_in", "w_up_a", "w_up_b", "w_out", "g_mem_q", "g_mem_kv", "w_q_mem", "w_kv_mem", "w_o_mem", "g_ffn",
         "w_ffn_gate", "w_ffn_up", "w_ffn_down", "g_final")


def kernel(x, mem, positions, g_mix, w_in, w_up_a, w_up_b, w_out, g_mem_q, g_mem_kv, w_q_mem, w_kv_mem, w_o_mem, g_ffn, w_ffn_gate, w_ffn_up, w_ffn_down, g_final, loss_target, m_g_mix, m_w_in, m_w_up_a, m_w_up_b, m_w_out, m_g_mem_q, m_g_mem_kv, m_w_q_mem, m_w_kv_mem, m_w_o_mem, m_g_ffn, m_w_ffn_gate, m_w_ffn_up, m_w_ffn_down, m_g_final, v_g_mix, v_w_in, v_w_up_a, v_w_up_b, v_w_out, v_g_mem_q, v_g_mem_kv, v_w_q_mem, v_w_kv_mem, v_w_o_mem, v_g_ffn, v_w_ffn_gate, v_w_ffn_up, v_w_ffn_down, v_g_final):
    given = dict(locals())
    batch, s, d = x.shape
    t = batch * s
    flipped = ("w_ffn_gate", "w_ffn_up")

    def view(a, n):
        a = a.reshape(a.shape[-2:])
        return a.T if n in flipped else a

    def unview(a, n):
        return (a.T if n in flipped else a).reshape(given[n].shape)

    shard = {n: view(given[n], n) for n in WEIGHTS}
    gains = [given[n].reshape(1, d) for n in GAINS]

    pad = (-shard["w_ffn_down"].shape[0]) % LANES
    cast = _cast_weights([shard[n] for n in WEIGHTS], [pad if n in flipped + ("w_ffn_down",) else 0 for n in WEIGHTS])
    loss_part, grad_x, pieces, dgains = _local_step(
        x.reshape(t, d), mem.reshape(-1, d), positions.reshape(t, 1), loss_target.reshape(t, d), gains, cast[0],
        cast[1:], batch)

    grad, delta, new_m, new_v = {}, {}, {}, {}
    for n, p in zip(WEIGHTS, pieces):
        outs = _adam("adam_" + n, p, shard[n], view(given["m_" + n], n), view(given["v_" + n], n))
        grad[n], delta[n], new_m[n], new_v[n] = [unview(o, n) for o in outs]

    g_all = _allreduce_small(list(dgains), loss_part)
    small = _adam_small(g_all, gains, [given["m_" + n].reshape(1, d) for n in GAINS],
                        [given["v_" + n].reshape(1, d) for n in GAINS])
    for out, vals in zip((grad, delta, new_m, new_v), small):
        for n, val in zip(GAINS, vals):
            out[n] = val.reshape(given[n].shape)

    loss = g_all[SMALL_ROWS - 1, 0]
    return (loss, grad_x.reshape(x.shape), *[grad[n] for n in ORDER], *[delta[n] for n in ORDER],
            *[new_m[n] for n in ORDER], *[new_v[n] for n in ORDER])
```

```python
import functools
import math

import jax
import jax.numpy as jnp
import numpy as np
from jax import lax
from jax.experimental import pallas as pl
from jax.experimental.pallas import tpu as pltpu

F32 = jnp.float32
BF16 = jnp.bfloat16

N_DEV = 8
HEAD_DIM = 64
MEM_HEAD_DIM = 128
N_HEADS_MEM = 4
BLOCK = 128
DIL_PATTERNS = ((128, 1), (512, 4), (2048, 16))
ROPE_THETA = 500000.0
ROPE_HALF = 8
RMS_EPS = 1e-6
ADAM_LR, ADAM_B1, ADAM_B2, ADAM_EPS, ADAM_WD, ADAM_STEP = 0.001, 0.9, 0.999, 1e-08, 0.01, 10
NEG = -1e30
ROW_TILE = 512
LANES = 128

ANY = pl.BlockSpec(memory_space=pl.ANY)
VMEM = pl.BlockSpec(memory_space=pltpu.VMEM)
NN = (((1,), (0,)), ((), ()))
NT = (((1,), (1,)), ((), ()))
TN = (((0,), (0,)), ((), ()))


def _params(sem):
    return pltpu.CompilerParams(dimension_semantics=sem)


def _mm(name, a, b, *, grid, a_spec, b_spec, o_shape, o_spec, dims, out_dtype, nk=1):
    def body(*refs):
        a_ref, b_ref, o_ref = refs[0], refs[1], refs[2]
        p = lax.dot_general(a_ref[...], b_ref[...], dims, preferred_element_type=F32)
        if nk == 1:
            o_ref[...] = p.astype(out_dtype)
            return
        acc_ref = refs[-1]
        k = pl.program_id(len(grid) - 1)

        @pl.when(k == 0)
        def _():
            acc_ref[...] = p

        @pl.when(k > 0)
        def _():
            acc_ref[...] += p

        @pl.when(k == nk - 1)
        def _():
            o_ref[...] = acc_ref[...].astype(out_dtype)

    o_block = tuple(d for d in o_spec.block_shape if d is not None)
    sem = ("parallel",) * (len(grid) - 1) + (("arbitrary",) if nk > 1 else ("parallel",))
    return pl.pallas_call(
        body, name=name, grid=grid, in_specs=[a_spec, b_spec],
        out_specs=o_spec, out_shape=jax.ShapeDtypeStruct(o_shape, out_dtype),
        scratch_shapes=[pltpu.VMEM(o_block, F32)] if nk > 1 else [],
        compiler_params=_params(sem),
    )(a, b)


def _rms_fwd(name, x, g):
    t, d = x.shape
    tm = min(ROW_TILE, t)

    def body(x_ref, g_ref, o_ref):
        xf = x_ref[...]
        r = lax.rsqrt(jnp.mean(xf * xf, axis=-1, keepdims=True) + RMS_EPS)
        o_ref[...] = (xf * r * g_ref[...]).astype(BF16)

    return pl.pallas_call(
        body, name=name, grid=(t // tm,),
        in_specs=[pl.BlockSpec((tm, d), lambda i: (i, 0)), pl.BlockSpec((1, d), lambda i: (0, 0))],
        out_specs=pl.BlockSpec((tm, d), lambda i: (i, 0)), out_shape=jax.ShapeDtypeStruct((t, d), BF16),
        compiler_params=_params(("parallel",)),
    )(x, g)


def _rms_bwd_rows(dnf, xf, gv, res):
    r = lax.rsqrt(jnp.mean(xf * xf, axis=-1, keepdims=True) + RMS_EPS)
    xh = xf * r
    dxh = dnf * gv
    dx = r * (dxh - xh * jnp.mean(dxh * xh, axis=-1, keepdims=True))
    if res is not None:
        dx = dx + res
    return dx, jnp.sum(dnf * xh, axis=0, keepdims=True)


def _rms_bwd(name, dn, x, g, dres, want):
    t, d = x.shape
    tm = min(ROW_TILE, t)
    has_res = dres is not None
    lhs = list(dn) if isinstance(dn, tuple) else [dn]
    n_lhs = len(lhs[:2])

    def body(*refs):
        x_ref, g_ref = refs[n_lhs], refs[n_lhs + 1]
        r_ref = refs[n_lhs + 2] if has_res else None
        dx_refs, dg_ref = refs[-1 - len(want):-1], refs[-1]
        if n_lhs == 2:
            dnf = lax.dot_general(refs[0][...], refs[1][...], lhs[2], preferred_element_type=F32)
        else:
            dnf = refs[0][...].astype(F32)
        dx, dg = _rms_bwd_rows(dnf, x_ref[...], g_ref[...], r_ref[...] if has_res else None)
        for kind, dx_ref in zip(want, dx_refs):
            dx_ref[...] = dx.astype(F32 if kind == "f32" else BF16)

        @pl.when(pl.program_id(0) == 0)
        def _():
            dg_ref[...] = jnp.zeros_like(dg_ref)

        dg_ref[...] += dg

    row = pl.BlockSpec((tm, d), lambda i: (i, 0))
    vec = pl.BlockSpec((1, d), lambda i: (0, 0))
    if n_lhs == 2:
        first = [pl.BlockSpec((tm, lhs[0].shape[1]), lambda i: (i, 0)), pl.BlockSpec(lhs[1].shape, lambda i: (0, 0))]
    else:
        first = [row]
    return pl.pallas_call(
        body, name=name, grid=(t // tm,),
        in_specs=first + [row, vec] + ([row] if has_res else []),
        out_specs=[row] * len(want) + [vec],
        out_shape=[jax.ShapeDtypeStruct((t, d), F32 if kind == "f32" else BF16) for kind in want]
        + [jax.ShapeDtypeStruct((1, d), F32)],
        compiler_params=_params(("arbitrary",)),
    )(*(lhs[:2] + [x, g] + ([dres] if has_res else [])))


def _loss_head(a, w, res, tgt, g):
    t, d = res.shape
    k = a.shape[1]
    tm = min(ROW_TILE, t)

    def body(a_ref, w_ref, r_ref, t_ref, g_ref, loss_ref, dh_ref, dhb_ref, dg_ref):
        xf = lax.dot_general(a_ref[...], w_ref[...], NN, preferred_element_type=F32) + r_ref[...]
        gv = g_ref[...]
        r = lax.rsqrt(jnp.mean(xf * xf, axis=-1, keepdims=True) + RMS_EPS)
        xh = xf * r
        e = xh * gv - t_ref[...]
        dy = e * (1.0 / d)
        dxh = dy * gv
        dh = r * (dxh - xh * jnp.mean(dxh * xh, axis=-1, keepdims=True))
        dh_ref[...] = dh
        dhb_ref[...] = dh.astype(BF16)

        @pl.when(pl.program_id(0) == 0)
        def _():
            dg_ref[...] = jnp.zeros_like(dg_ref)
            loss_ref[...] = jnp.zeros_like(loss_ref)

        dg_ref[...] += jnp.sum(dy * xh, axis=0, keepdims=True)
        part = jnp.sum(jnp.sum(e * e, axis=1, keepdims=True), axis=0, keepdims=True) * (0.5 / d)
        loss_ref[...] += jnp.broadcast_to(part, loss_ref.shape)

    row = pl.BlockSpec((tm, d), lambda i: (i, 0))
    vec = pl.BlockSpec((1, d), lambda i: (0, 0))
    return pl.pallas_call(
        body, name="loss_head", grid=(t // tm,),
        in_specs=[pl.BlockSpec((tm, k), lambda i: (i, 0)), pl.BlockSpec((k, d), lambda i: (0, 0)), row, row, vec],
        out_specs=[pl.BlockSpec((8, LANES), lambda i: (0, 0)), row, row, vec],
        out_shape=[jax.ShapeDtypeStruct((8, LANES), F32), jax.ShapeDtypeStruct((t, d), F32),
                   jax.ShapeDtypeStruct((t, d), BF16), jax.ShapeDtypeStruct((1, d), F32)],
        compiler_params=_params(("arbitrary",)),
    )(a, w, res, tgt, g)


def _rope_tables(pos, inv_freq, sel_lo, sel_hi):
    t = pos.shape[0]
    tm = min(ROW_TILE, t)

    def body(p_ref, f_ref, lo_ref, hi_ref, c_ref, sa_ref, sb_ref):
        ang = p_ref[...].astype(F32) * f_ref[...]
        rot = lo_ref[...] + hi_ref[...]
        cs, sn = jnp.cos(ang), jnp.sin(ang)
        c_ref[...] = cs * rot + (1.0 - rot)
        sa_ref[...] = -sn * lo_ref[...]
        sb_ref[...] = sn * hi_ref[...]

    vec = pl.BlockSpec((1, LANES), lambda i: (0, 0))
    row = pl.BlockSpec((tm, LANES), lambda i: (i, 0))
    return pl.pallas_call(
        body, name="rope_tables", grid=(t // tm,),
        in_specs=[pl.BlockSpec((tm, 1), lambda i: (i, 0)), vec, vec, vec],
        out_specs=[row, row, row], out_shape=[jax.ShapeDtypeStruct((t, LANES), F32)] * 3,
        compiler_params=_params(("parallel",)),
    )(pos, inv_freq, sel_lo, sel_hi)


def _rope_apply(name, srcs, width, cos_t, sin_a, sin_b, sign, tail=()):
    t = srcs[0].shape[0]
    tm = min(ROW_TILE, t)
    n_cols = width // LANES
    n_src = len(srcs)

    def body(*refs):
        x_refs, tail_refs = refs[:n_src], refs[n_src:n_src + len(tail)]
        c_ref, sa_ref, sb_ref, o_ref = refs[n_src + len(tail):]
        cs, sa, sb = c_ref[...], sign * sa_ref[...], sign * sb_ref[...]
        for a, x_ref in enumerate(x_refs):
            for c in range(n_cols):
                xf = x_ref[:, c * LANES:(c + 1) * LANES].astype(F32)
                up = pltpu.roll(xf, LANES - ROPE_HALF, 1)
                dn = pltpu.roll(xf, ROPE_HALF, 1)
                o_ref[:, a * width + c * LANES:a * width + (c + 1) * LANES] = (xf * cs + up * sa + dn * sb).astype(BF16)
        col = n_src * width
        for t_ref in tail_refs:
            o_ref[:, col:col + t_ref.shape[1]] = t_ref[...]
            col += t_ref.shape[1]

    wide = n_src * width + sum(a.shape[1] for a in tail)
    tab = pl.BlockSpec((tm, LANES), lambda i: (i, 0))
    return pl.pallas_call(
        body, name=name, grid=(t // tm,),
        in_specs=[pl.BlockSpec((tm, width), lambda i: (i, 0))] * n_src
        + [pl.BlockSpec((tm, a.shape[1]), lambda i: (i, 0)) for a in tail] + [tab, tab, tab],
        out_specs=pl.BlockSpec((tm, wide), lambda i: (i, 0)),
        out_shape=jax.ShapeDtypeStruct((t, wide), BF16),
        compiler_params=_params(("parallel",)),
    )(*srcs, *tail, cos_t, sin_a, sin_b)


DA_T = 256
MIX_STREAMS = 4
SB_BWD_STREAMS = 2


def _lane_lo():
    return lax.broadcasted_iota(jnp.int32, (BLOCK, LANES), 1) < HEAD_DIM


def _dilated_bias_tiles(s):
    n = s // DA_T
    dist = (np.arange(n)[:, None, None] * DA_T + np.arange(DA_T)[None, :, None] - np.arange(DA_T)[None, None, :])
    cnt = np.zeros(dist.shape, np.float32)
    for window, dil in DIL_PATTERNS:
        cnt += ((dist >= 0) & (dist % dil == 0) & (dist <= window)).astype(np.float32)
    return jnp.asarray(np.where(cnt > 0, np.log(np.maximum(cnt, 1.0)), NEG).astype(np.float32))


def _stack_heads(x, lo):
    zero = jnp.zeros_like(x)
    return jnp.concatenate([jnp.where(lo, x, zero), jnp.where(lo, zero, x)], axis=0)


def _da_fwd(qk, proj, v_col0, bias, batch, s, ride=None, streams=MIX_STREAMS):
    t = qk.shape[0]
    nq = s // DA_T
    n_pairs = 4
    ns = streams
    wide = ns * LANES
    scale = HEAD_DIM ** -0.5

    def body(q_ref, k_ref, v_ref, b_ref, o_ref, lse_ref, acc_ref, m_ref, l_ref):
        i = pl.program_id(2)
        lo = lax.broadcasted_iota(jnp.int32, (DA_T, LANES), 1) < HEAD_DIM
        ones = jnp.ones((DA_T, LANES), BF16)
        acc_ref[...] = jnp.zeros_like(acc_ref)
        m_ref[...] = jnp.full(m_ref.shape, NEG, F32)
        l_ref[...] = jnp.zeros_like(l_ref)
        qqs = [_stack_heads(q_ref[:, st * LANES:(st + 1) * LANES] * scale, lo) for st in range(ns)]

        def scores(st, rows, bias2):
            k = k_ref[rows, st * LANES:(st + 1) * LANES]
            return lax.dot_general(qqs[st], k, NT, preferred_element_type=F32) + bias2

        def softmax(st, sc):
            m_old = m_ref[st]
            m_new = jnp.maximum(m_old, jnp.max(sc, axis=1, keepdims=True))
            m_ref[st] = m_new
            return jnp.exp(sc - m_new).astype(BF16), jnp.exp(m_old - m_new)

        def values(st, rows, p, alpha):
            v = v_ref[rows, st * LANES:(st + 1) * LANES]
            vz = jnp.zeros_like(v)
            l_ref[st] = alpha * l_ref[st] + lax.dot_general(p, ones, NN, preferred_element_type=F32)
            pv = (lax.dot_general(p[:DA_T], jnp.where(lo, v, vz), NN, preferred_element_type=F32)
                  + lax.dot_general(p[DA_T:], jnp.where(lo, vz, v), NN, preferred_element_type=F32))
            acc_ref[st] = acc_ref[st] * jnp.where(lo, alpha[:DA_T], alpha[DA_T:]) + pv

        def trip(dlt, carry):
            rows = pl.ds(pl.multiple_of((i - dlt) * DA_T, DA_T), DA_T)
            bias_t = b_ref[dlt]
            bias2 = jnp.concatenate([bias_t, bias_t], axis=0)
            scs = [scores(st, rows, bias2) for st in range(ns)]
            pas = [softmax(st, scs[st]) for st in range(ns)]
            for st in range(ns):
                values(st, rows, *pas[st])
            return carry

        lax.fori_loop(0, i + 1, trip, 0)
        for st in range(ns):
            cols = slice(st * LANES, (st + 1) * LANES)
            l_t = l_ref[st]
            o_ref[:, cols] = (acc_ref[st] / jnp.where(lo, l_t[:DA_T], l_t[DA_T:])).astype(BF16)
            lse = m_ref[st] + jnp.log(l_t)
            lse_ref[:, cols] = jnp.where(lo, lse[:DA_T], lse[DA_T:])

    blk = pl.BlockSpec((DA_T, wide), lambda b, h, i: (b * nq + i, h))
    return _call(
        body, name="attn_a_fwd", grid=(batch, n_pairs // ns, nq),
        in_specs=[blk,
                  pl.BlockSpec((s, wide), lambda b, h, i: (b, n_pairs // ns + h)),
                  pl.BlockSpec((s, wide), lambda b, h, i: (b, v_col0 // ns + h)),
                  pl.BlockSpec((nq, DA_T, DA_T), lambda b, h, i: (0, 0, 0))],
        out_specs=[blk, blk],
        out_shape=[jax.ShapeDtypeStruct((t, n_pairs * LANES), BF16), jax.ShapeDtypeStruct((t, n_pairs * LANES), F32)],
        scratch=[pltpu.VMEM((ns, DA_T, LANES), F32), pltpu.VMEM((ns, 2 * DA_T, 1), F32),
                 pltpu.VMEM((ns, 2 * DA_T, LANES), F32)],
        sem=("parallel", "parallel", "arbitrary"), args=(qk, qk, proj, bias), ride=ride)


def _da_bwd(qk, proj, v_col0, bias, o, lse, do, batch, s, ride=None, streams=MIX_STREAMS):
    t = qk.shape[0]
    nq = s // DA_T
    n_pairs = 4
    ns = streams
    wide = ns * LANES
    scale = HEAD_DIM ** -0.5

    def body(q_ref, k_ref, v_ref, b_ref, o_ref, lse_ref, do_ref, dq_ref, dk_ref, dv_ref, dk_acc, dv_acc, dq_acc):
        i = pl.program_id(2)
        lo = lax.broadcasted_iota(jnp.int32, (DA_T, LANES), 1) < HEAD_DIM

        @pl.when(i == 0)
        def _():
            dk_acc[...] = jnp.zeros_like(dk_acc)
            dv_acc[...] = jnp.zeros_like(dv_acc)

        dq_acc[...] = jnp.zeros_like(dq_acc)
        qqs, dds, deltas, lses = [], [], [], []
        for st in range(ns):
            cols = slice(st * LANES, (st + 1) * LANES)
            do_ = do_ref[:, cols]
            qqs.append(_stack_heads(q_ref[:, cols] * scale, lo))
            dds.append(_stack_heads(do_, lo))
            prod = do_.astype(F32) * o_ref[:, cols].astype(F32)
            fz = jnp.zeros_like(prod)
            deltas.append(jnp.concatenate([jnp.sum(jnp.where(lo, prod, fz), axis=1, keepdims=True),
                                           jnp.sum(jnp.where(lo, fz, prod), axis=1, keepdims=True)], axis=0))
            lse_t = lse_ref[:, cols]
            lses.append(jnp.concatenate([lse_t[:, 0:1], lse_t[:, HEAD_DIM:HEAD_DIM + 1]], axis=0))

        def products(st, rows, bias2):
            cols = slice(st * LANES, (st + 1) * LANES)
            sc = lax.dot_general(qqs[st], k_ref[rows, cols], NT, preferred_element_type=F32) + bias2
            return sc, lax.dot_general(dds[st], v_ref[rows, cols], NT, preferred_element_type=F32)

        def weights(st, sc, dp):
            p = jnp.exp(sc - lses[st])
            return (p * (dp - deltas[st])).astype(BF16), p.astype(BF16)

        def gradients(st, rows, ds, p):
            cols = slice(st * LANES, (st + 1) * LANES)
            k = k_ref[rows, cols]
            kz = jnp.zeros_like(k)
            dq_acc[st] += (lax.dot_general(ds[:DA_T], jnp.where(lo, k, kz), NN, preferred_element_type=F32)
                           + lax.dot_general(ds[DA_T:], jnp.where(lo, kz, k), NN, preferred_element_type=F32))
            dk_acc[rows, cols] += lax.dot_general(ds, qqs[st], TN, preferred_element_type=F32)
            dv_acc[rows, cols] += lax.dot_general(p, dds[st], TN, preferred_element_type=F32)

        def trip(dlt, carry):
            rows = pl.ds(pl.multiple_of((i - dlt) * DA_T, DA_T), DA_T)
            bias_t = b_ref[dlt]
            bias2 = jnp.concatenate([bias_t, bias_t], axis=0)
            prods = [products(st, rows, bias2) for st in range(ns)]
            wts = [weights(st, *prods[st]) for st in range(ns)]
            for st in range(ns):
                gradients(st, rows, *wts[st])
            return carry

        lax.fori_loop(0, i + 1, trip, 0)
        for st in range(ns):
            dq_ref[:, st * LANES:(st + 1) * LANES] = (dq_acc[st] * scale).astype(BF16)

        @pl.when(i == nq - 1)
        def _():
            dk_ref[...] = dk_acc[...].astype(BF16)
            dv_ref[...] = dv_acc[...].astype(BF16)

    blk = pl.BlockSpec((DA_T, wide), lambda b, h, i: (b * nq + i, h))
    seq = pl.BlockSpec((s, wide), lambda b, h, i: (b, h), pipeline_mode=pl.Buffered(1))
    one = pl.Buffered(1)
    out = jax.ShapeDtypeStruct((t, n_pairs * LANES), BF16)
    return _call(
        body, name="attn_a_bwd", grid=(batch, n_pairs // ns, nq),
        in_specs=[blk,
                  pl.BlockSpec((s, wide), lambda b, h, i: (b, n_pairs // ns + h), pipeline_mode=one),
                  pl.BlockSpec((s, wide), lambda b, h, i: (b, v_col0 // ns + h), pipeline_mode=one),
                  pl.BlockSpec((nq, DA_T, DA_T), lambda b, h, i: (0, 0, 0), pipeline_mode=one),
                  blk, blk, blk],
        out_specs=[blk, seq, seq], out_shape=[out, out, out],
        scratch=[pltpu.VMEM((s, wide), F32), pltpu.VMEM((s, wide), F32), pltpu.VMEM((ns, DA_T, LANES), F32)],
        sem=("parallel", "parallel", "arbitrary"), args=(qk, qk, proj, bias, o, lse, do), ride=ride)


SB_Q = 256


def _sb_consts(after):
    r = lax.broadcasted_iota(jnp.int32, (2 * BLOCK, 2 * BLOCK), 0) % BLOCK
    c = lax.broadcasted_iota(jnp.int32, (2 * BLOCK, 2 * BLOCK), 1)
    tri = (r > c) if after else (r < c)
    return jnp.logical_or(c >= BLOCK, tri).astype(BF16)


def _split(x):
    hi = x.astype(BF16)
    lo = (x - hi.astype(F32)).astype(BF16)
    return jnp.concatenate([hi, lo], axis=1)


def _sb_fwd(proj, q_col0, k_col0, v_col0, batch, s, ride=None, streams=MIX_STREAMS):
    t = proj.shape[0]
    nq = s // SB_Q
    n_pairs = 4
    ns = streams
    wide = ns * LANES
    scale = HEAD_DIM ** -0.5

    def body(q_ref, k_ref, v_ref, o_ref, tot_ref, acc_ref, run_ref):
        i = pl.program_id(2)
        lo_q = lax.broadcasted_iota(jnp.int32, (SB_Q, LANES), 1) < HEAD_DIM
        lo_k = _lane_lo()
        mat = _sb_consts(True)
        row = lax.broadcasted_iota(jnp.int32, (2 * SB_Q, LANES), 0) % SB_Q
        ahead = row - lax.broadcasted_iota(jnp.int32, (2 * SB_Q, LANES), 1)
        acc_ref[...] = jnp.zeros_like(acc_ref)
        run_ref[...] = jnp.zeros_like(run_ref)
        qqs = [_stack_heads(q_ref[:, st * LANES:(st + 1) * LANES] * scale, lo_q) for st in range(ns)]

        def units(todo):
            def rows(j):
                return pl.ds(pl.multiple_of(j * BLOCK, BLOCK), BLOCK)

            zs = [lax.dot_general(qqs[st], k_ref[rows(j), st * LANES:(st + 1) * LANES], NT, preferred_element_type=F32)
                  for st, j, _ in todo]
            logs = []
            for z, (_, _, off) in zip(zs, todo):
                lsig = jnp.minimum(z, 0.0) - jnp.log(1.0 + jnp.exp(-jnp.abs(z)))
                lneg = lsig - z
                if off is not None:
                    lneg = jnp.where(ahead > off, lneg, 0.0)
                logs.append((lsig, _split(lneg)))
            sums = [lax.dot_general(cat, mat, NN, preferred_element_type=F32) for _, cat in logs]
            probs = []
            for (lsig, _), sm, (st, _, off) in zip(logs, sums, todo):
                run = run_ref[st]
                a = jnp.exp(lsig + run + sm[:, :BLOCK])
                if off is not None:
                    a = jnp.where(ahead > off, a, 0.0)
                run_ref[st] = run + sm[:, BLOCK:]
                probs.append(a.astype(BF16))
            for ab, (st, j, _) in zip(probs, todo):
                v = v_ref[rows(j), st * LANES:(st + 1) * LANES]
                vz = jnp.zeros_like(v)
                acc_ref[st] += (lax.dot_general(ab[:SB_Q], jnp.where(lo_k, v, vz), NN, preferred_element_type=F32)
                                + lax.dot_general(ab[SB_Q:], jnp.where(lo_k, vz, v), NN, preferred_element_type=F32))

        units([(st, 2 * i + 1, BLOCK) for st in range(ns)] + [(st, 2 * i, 0) for st in range(ns)])

        def pair(p, carry):
            jp = i - 1 - p
            units([(st, 2 * jp + 1, None) for st in range(ns)] + [(st, 2 * jp, None) for st in range(ns)])
            return carry

        lax.fori_loop(0, i, pair, 0)
        for st in range(ns):
            cols = slice(st * LANES, (st + 1) * LANES)
            o_ref[:, cols] = acc_ref[st].astype(BF16)
            tot_ref[:, cols] = jnp.where(lo_q, run_ref[st, 0:SB_Q, :], run_ref[st, SB_Q:2 * SB_Q, :])

    def seq(col0):
        return pl.BlockSpec((s, wide), lambda b, h, i: (b, col0 // ns + h))

    blk = pl.BlockSpec((SB_Q, wide), lambda b, h, i: (b * nq + i, h))
    return _call(
        body, name="attn_b_fwd", grid=(batch, n_pairs // ns, nq),
        in_specs=[pl.BlockSpec((SB_Q, wide), lambda b, h, i: (b * nq + i, q_col0 // ns + h)), seq(k_col0), seq(v_col0)],
        out_specs=[blk, blk],
        out_shape=[jax.ShapeDtypeStruct((t, n_pairs * LANES), BF16), jax.ShapeDtypeStruct((t, n_pairs * LANES), F32)],
        scratch=[pltpu.VMEM((ns, SB_Q, LANES), F32), pltpu.VMEM((ns, 2 * SB_Q, LANES), F32)],
        sem=("parallel", "parallel", "arbitrary"), args=(proj, proj, proj), ride=ride)


def _sb_bwd(proj, q_col0, k_col0, v_col0, tot, do, batch, s, ride=None, streams=SB_BWD_STREAMS):
    t = proj.shape[0]
    nq = s // SB_Q
    n_pairs = 4
    ns = streams
    wide = ns * LANES
    scale = HEAD_DIM ** -0.5

    def body(q_ref, k_ref, v_ref, tot_ref, do_ref, dq_ref, dk_ref, dv_ref, dk_acc, dv_acc, dq_acc, seen_ref, gsum_ref):
        i = pl.program_id(2)
        lo_q = lax.broadcasted_iota(jnp.int32, (SB_Q, LANES), 1) < HEAD_DIM
        lo_k = _lane_lo()

        @pl.when(i == 0)
        def _():
            dk_acc[...] = jnp.zeros_like(dk_acc)
            dv_acc[...] = jnp.zeros_like(dv_acc)

        mat_after = _sb_consts(True)
        mat_before = _sb_consts(False)[:BLOCK]
        row = lax.broadcasted_iota(jnp.int32, (2 * SB_Q, LANES), 0) % SB_Q
        ahead = row - lax.broadcasted_iota(jnp.int32, (2 * SB_Q, LANES), 1)
        dq_acc[...] = jnp.zeros_like(dq_acc)
        seen_ref[...] = jnp.zeros_like(seen_ref)
        gsum_ref[...] = jnp.zeros_like(gsum_ref)
        qqs, dds, totals = [], [], []
        for st in range(ns):
            cols = slice(st * LANES, (st + 1) * LANES)
            qqs.append(_stack_heads(q_ref[:, cols] * scale, lo_q))
            dds.append(_stack_heads(do_ref[:, cols], lo_q))
            tot_t = tot_ref[:, cols]
            totals.append(jnp.concatenate([jnp.broadcast_to(tot_t[:, 0:1], (SB_Q, LANES)),
                                           jnp.broadcast_to(tot_t[:, HEAD_DIM:HEAD_DIM + 1], (SB_Q, LANES))], axis=0))

        def units(todo):
            def rows(j):
                return pl.ds(pl.multiple_of(j * BLOCK, BLOCK), BLOCK)

            def cols(st):
                return slice(st * LANES, (st + 1) * LANES)

            prods = [(lax.dot_general(qqs[st], k_ref[rows(j), cols(st)], NT, preferred_element_type=F32),
                      lax.dot_general(dds[st], v_ref[rows(j), cols(st)], NT, preferred_element_type=F32))
                     for st, j, _ in todo]
            logs = []
            for (z, _), (_, _, off) in zip(prods, todo):
                lsig = jnp.minimum(z, 0.0) - jnp.log(1.0 + jnp.exp(-jnp.abs(z)))
                lneg = lsig - z
                if off is not None:
                    lneg = jnp.where(ahead > off, lneg, 0.0)
                logs.append((lsig, _split(lneg)))
            sums = [lax.dot_general(cat, mat_after, NN, preferred_element_type=F32) for _, cat in logs]
            gates = []
            for (lsig, _), sm, (_, da), (st, _, off) in zip(logs, sums, prods, todo):
                seen = seen_ref[st]
                a = jnp.exp(lsig + (totals[st] - seen - sm[:, BLOCK:]) + sm[:, :BLOCK])
                if off is not None:
                    a = jnp.where(ahead > off, a, 0.0)
                seen_ref[st] = seen + sm[:, BLOCK:]
                g = a * da
                gates.append((a.astype(BF16), g, g.astype(BF16)))
            gsums = [lax.dot_general(cat, mat_before, NN, preferred_element_type=F32) for _, _, cat in gates]
            outs = []
            for (lsig, _), (ab, g, _), gs, (st, _, off) in zip(logs, gates, gsums, todo):
                gsum = gsum_ref[st]
                dz = g - jnp.exp(lsig) * (g + gsum + gs[:, :BLOCK])
                if off is not None:
                    dz = jnp.where(ahead > off, dz, 0.0)
                gsum_ref[st] = gsum + gs[:, BLOCK:]
                outs.append((dz.astype(BF16), ab))
            for (dzb, ab), (st, j, _) in zip(outs, todo):
                k = k_ref[rows(j), cols(st)]
                kz = jnp.zeros_like(k)
                dq_acc[st] += (lax.dot_general(dzb[:SB_Q], jnp.where(lo_k, k, kz), NN, preferred_element_type=F32)
                               + lax.dot_general(dzb[SB_Q:], jnp.where(lo_k, kz, k), NN, preferred_element_type=F32))
                dk_acc[rows(j), cols(st)] += lax.dot_general(dzb, qqs[st], TN, preferred_element_type=F32)
                dv_acc[rows(j), cols(st)] += lax.dot_general(ab, dds[st], TN, preferred_element_type=F32)

        def pair(p, carry):
            units([(st, 2 * p, None) for st in range(ns)] + [(st, 2 * p + 1, None) for st in range(ns)])
            return carry

        lax.fori_loop(0, i, pair, 0)
        units([(st, 2 * i, 0) for st in range(ns)] + [(st, 2 * i + 1, BLOCK) for st in range(ns)])
        for st in range(ns):
            dq_ref[:, st * LANES:(st + 1) * LANES] = (dq_acc[st] * scale).astype(BF16)

        @pl.when(i == nq - 1)
        def _():
            dk_ref[...] = dk_acc[...].astype(BF16)
            dv_ref[...] = dv_acc[...].astype(BF16)

    def seq_in(col0):
        return pl.BlockSpec((s, wide), lambda b, h, i: (b, col0 // ns + h))

    blk = pl.BlockSpec((SB_Q, wide), lambda b, h, i: (b * nq + i, h))
    seq = pl.BlockSpec((s, wide), lambda b, h, i: (b, h))
    out = jax.ShapeDtypeStruct((t, n_pairs * LANES), BF16)
    return _call(
        body, name="attn_b_bwd", grid=(batch, n_pairs // ns, nq),
        in_specs=[pl.BlockSpec((SB_Q, wide), lambda b, h, i: (b * nq + i, q_col0 // ns + h)), seq_in(k_col0),
                  seq_in(v_col0), blk, blk],
        out_specs=[blk, seq, seq], out_shape=[out, out, out],
        scratch=[pltpu.VMEM((s, wide), F32), pltpu.VMEM((s, wide), F32), pltpu.VMEM((ns, SB_Q, LANES), F32),
                 pltpu.VMEM((ns, 2 * SB_Q, LANES), F32), pltpu.VMEM((ns, 2 * SB_Q, LANES), F32)],
        sem=("parallel", "parallel", "arbitrary"), args=(proj, proj, proj, tot, do), ride=ride)


MEM_Q_TILE = 512


def _mem_fwd(q, kv, batch, s, n_mem):
    t, width = q.shape
    tq = min(MEM_Q_TILE, s)
    nq = s // tq
    scale = MEM_HEAD_DIM ** -0.5

    def body(q_ref, kv_ref, o_ref):
        for h in range(N_HEADS_MEM):
            cols = slice(h * MEM_HEAD_DIM, (h + 1) * MEM_HEAD_DIM)
            k = kv_ref[:, cols]
            v = kv_ref[:, width + h * MEM_HEAD_DIM: width + (h + 1) * MEM_HEAD_DIM]
            sc = lax.dot_general(q_ref[:, cols], k, NT, preferred_element_type=F32) * scale
            p = jnp.exp(sc - jnp.max(sc, axis=1, keepdims=True))
            p = p / jnp.sum(p, axis=1, keepdims=True)
            o_ref[:, cols] = lax.dot_general(p.astype(BF16), v, NN, preferred_element_type=F32).astype(BF16)

    return pl.pallas_call(
        body, name="mem_attn_fwd", grid=(batch, nq),
        in_specs=[pl.BlockSpec((tq, width), lambda b, i: (b * nq + i, 0)),
                  pl.BlockSpec((n_mem, 2 * width), lambda b, i: (b, 0))],
        out_specs=pl.BlockSpec((tq, width), lambda b, i: (b * nq + i, 0)),
        out_shape=jax.ShapeDtypeStruct((t, width), BF16),
        compiler_params=_params(("parallel", "parallel")),
    )(q, kv)


def _mem_bwd(q, kv, do, batch, s, n_mem):
    t, width = q.shape
    tq = min(MEM_Q_TILE, s)
    nq = s // tq
    scale = MEM_HEAD_DIM ** -0.5

    def body(q_ref, kv_ref, do_ref, dq_ref, dkv_ref, acc):
        i = pl.program_id(1)

        @pl.when(i == 0)
        def _():
            acc[...] = jnp.zeros_like(acc)

        for h in range(N_HEADS_MEM):
            cols = slice(h * MEM_HEAD_DIM, (h + 1) * MEM_HEAD_DIM)
            vcols = slice(width + h * MEM_HEAD_DIM, width + (h + 1) * MEM_HEAD_DIM)
            qh, k, v, doh = q_ref[:, cols], kv_ref[:, cols], kv_ref[:, vcols], do_ref[:, cols]
            sc = lax.dot_general(qh, k, NT, preferred_element_type=F32) * scale
            p = jnp.exp(sc - jnp.max(sc, axis=1, keepdims=True))
            p = p / jnp.sum(p, axis=1, keepdims=True)
            dp = lax.dot_general(doh, v, NT, preferred_element_type=F32)
            ds = (p * (dp - jnp.sum(p * dp, axis=1, keepdims=True)) * scale).astype(BF16)
            dq_ref[:, cols] = lax.dot_general(ds, k, NN, preferred_element_type=F32).astype(BF16)
            acc[:, cols] += lax.dot_general(ds, qh, TN, preferred_element_type=F32)
            acc[:, vcols] += lax.dot_general(p.astype(BF16), doh, TN, preferred_element_type=F32)

        @pl.when(i == nq - 1)
        def _():
            dkv_ref[...] = acc[...].astype(BF16)

    row = pl.BlockSpec((tq, width), lambda b, i: (b * nq + i, 0))
    kvs = pl.BlockSpec((n_mem, 2 * width), lambda b, i: (b, 0))
    return pl.pallas_call(
        body, name="mem_attn_bwd", grid=(batch, nq),
        in_specs=[row, kvs, row], out_specs=[row, kvs],
        out_shape=[jax.ShapeDtypeStruct((t, width), BF16), jax.ShapeDtypeStruct((batch * n_mem, 2 * width), BF16)],
        scratch_shapes=[pltpu.VMEM((n_mem, 2 * width), F32)],
        compiler_params=_params(("parallel", "arbitrary")),
    )(q, kv, do)


def _mixer_fwd(o_a, o_b, w_a, w_b, proj, gate_col0, w_out, x, g, w_q):
    t, width = o_a.shape
    d = w_a.shape[1]
    nq_cols = w_q.shape[1]
    tm = min(ROW_TILE, t)
    gb0 = gate_col0 * LANES // d

    def body(oa_ref, ob_ref, wa_ref, wb_ref, ga_ref, gb_ref, wo_ref, x_ref, g_ref, wq_ref, ua_ref, ub_ref, mix_ref,
             n_ref, h_ref, q_ref):
        ua = lax.dot_general(oa_ref[...], wa_ref[...], NN, preferred_element_type=F32)
        ub = lax.dot_general(ob_ref[...], wb_ref[...], NN, preferred_element_type=F32)
        ua_ref[...] = ua.astype(BF16)
        ub_ref[...] = ub.astype(BF16)
        mixed = (jax.nn.sigmoid(ga_ref[...].astype(F32)) * ua + jax.nn.sigmoid(gb_ref[...].astype(F32)) * ub).astype(BF16)
        mix_ref[...] = mixed
        h = lax.dot_general(mixed, wo_ref[...], NN, preferred_element_type=F32) + x_ref[...]
        h_ref[...] = h
        r = lax.rsqrt(jnp.mean(h * h, axis=-1, keepdims=True) + RMS_EPS)
        n = (h * r * g_ref[...]).astype(BF16)
        n_ref[...] = n
        q_ref[...] = lax.dot_general(n, wq_ref[...], NN, preferred_element_type=F32).astype(BF16)

    row = pl.BlockSpec((tm, width), lambda i: (i, 0))
    wsp = pl.BlockSpec((width, d), lambda i: (0, 0))
    out = pl.BlockSpec((tm, d), lambda i: (i, 0))
    osh = jax.ShapeDtypeStruct((t, d), BF16)
    return pl.pallas_call(
        body, name="mixer_fwd", grid=(t // tm,),
        in_specs=[row, row, wsp, wsp,
                  pl.BlockSpec((tm, d), lambda i: (i, gb0)), pl.BlockSpec((tm, d), lambda i: (i, gb0 + 1)),
                  pl.BlockSpec((d, d), lambda i: (0, 0)), out, pl.BlockSpec((1, d), lambda i: (0, 0)),
                  pl.BlockSpec((d, nq_cols), lambda i: (0, 0))],
        out_specs=[out, out, out, out, out, pl.BlockSpec((tm, nq_cols), lambda i: (i, 0))],
        out_shape=[osh, osh, osh, osh, jax.ShapeDtypeStruct((t, d), F32), jax.ShapeDtypeStruct((t, nq_cols), BF16)],
        compiler_params=_params(("parallel",)),
    )(o_a, o_b, w_a, w_b, proj, proj, w_out, x, g, w_q)


def _mixer_bwd(dh, w_out, ua, ub, proj, gate_col0, w_a, w_b):
    t, d = dh.shape
    width = w_a.shape[0]
    tm = min(ROW_TILE, t)
    nc = d // LANES

    def body(dh_ref, w_ref, ua_ref, ub_ref, ga_ref, gb_ref, wa_ref, wb_ref, dua_ref, dub_ref, dg_ref, doa_ref, dob_ref):
        dm = lax.dot_general(dh_ref[...], w_ref[...], NT, preferred_element_type=F32)
        sa = jax.nn.sigmoid(ga_ref[...].astype(F32))
        sb = jax.nn.sigmoid(gb_ref[...].astype(F32))
        dua = (dm * sa).astype(BF16)
        dub = (dm * sb).astype(BF16)
        dua_ref[...] = dua
        dub_ref[...] = dub
        dg_ref[:, 0:d] = (dm * ua_ref[...].astype(F32) * sa * (1.0 - sa)).astype(BF16)
        dg_ref[:, d:2 * d] = (dm * ub_ref[...].astype(F32) * sb * (1.0 - sb)).astype(BF16)
        doa_ref[...] = lax.dot_general(dua, wa_ref[...], NT, preferred_element_type=F32).astype(BF16)
        dob_ref[...] = lax.dot_general(dub, wb_ref[...], NT, preferred_element_type=F32).astype(BF16)

    row = pl.BlockSpec((tm, d), lambda i: (i, 0))
    wsp = pl.BlockSpec((width, d), lambda i: (0, 0))
    osp = pl.BlockSpec((tm, width), lambda i: (i, 0))
    return pl.pallas_call(
        body, name="mixer_bwd", grid=(t // tm,),
        in_specs=[row, pl.BlockSpec((d, d), lambda i: (0, 0)), row, row,
                  pl.BlockSpec((tm, d), lambda i: (i, gate_col0 // nc)),
                  pl.BlockSpec((tm, d), lambda i: (i, gate_col0 // nc + 1)), wsp, wsp],
        out_specs=[row, row, pl.BlockSpec((tm, 2 * d), lambda i: (i, 0)), osp, osp],
        out_shape=[jax.ShapeDtypeStruct((t, d), BF16), jax.ShapeDtypeStruct((t, d), BF16),
                   jax.ShapeDtypeStruct((t, 2 * d), BF16), jax.ShapeDtypeStruct((t, width), BF16),
                   jax.ShapeDtypeStruct((t, width), BF16)],
        compiler_params=_params(("parallel",)),
    )(dh, w_out, ua, ub, proj, proj, w_a, w_b)


FFN_COLS = 1024


def _ffn_up(n, w_gate, w_up):
    t, d = n.shape
    hidden = w_gate.shape[0]
    tm = min(ROW_TILE, t)
    tn = min(FFN_COLS, hidden)

    def body(n_ref, wg_ref, wu_ref, hg_ref, hu_ref, act_ref):
        hg = lax.dot_general(n_ref[...], wg_ref[...], NT, preferred_element_type=F32)
        hu = lax.dot_general(n_ref[...], wu_ref[...], NT, preferred_element_type=F32)
        hg_ref[...] = hg.astype(BF16)
        hu_ref[...] = hu.astype(BF16)
        act_ref[...] = (hg * jax.nn.sigmoid(hg) * hu).astype(BF16)

    wsp = pl.BlockSpec((tn, d), lambda j, i: (j, 0))
    out = pl.BlockSpec((tm, tn), lambda j, i: (i, j))
    osh = jax.ShapeDtypeStruct((t, hidden), BF16)
    return pl.pallas_call(
        body, name="ffn_up", grid=(hidden // tn, t // tm),
        in_specs=[pl.BlockSpec((tm, d), lambda j, i: (i, 0)), wsp, wsp],
        out_specs=[out, out, out], out_shape=[osh, osh, osh],
        compiler_params=_params(("parallel", "parallel")),
    )(n, w_gate, w_up)


def _ffn_bwd(dh, w_down, w_gate, w_up, hg, hu, x, g, dres, w_prev):
    t, d = dh.shape
    hidden = w_down.shape[0]
    q = w_prev.shape[0]
    tm = min(ROW_TILE, t)
    tn = min(FFN_COLS, hidden)
    nj = hidden // tn

    def body(dh_ref, wd_ref, wg_ref, wu_ref, hg_ref, hu_ref, x_ref, g_ref, r_ref, wp_ref, dhg_ref, dhu_ref, dx_ref,
             dxb_ref, dg_ref, do_ref, acc):
        j, i = pl.program_id(0), pl.program_id(1)
        dact = lax.dot_general(dh_ref[...], wd_ref[...], NT, preferred_element_type=F32)
        hg = hg_ref[...].astype(F32)
        sg = jax.nn.sigmoid(hg)
        dhu = (dact * hg * sg).astype(BF16)
        dhg = (dact * hu_ref[...].astype(F32) * sg * (1.0 + hg * (1.0 - sg))).astype(BF16)
        dhu_ref[...] = dhu
        dhg_ref[...] = dhg
        part = (lax.dot_general(dhg, wg_ref[...], NN, preferred_element_type=F32)
                + lax.dot_general(dhu, wu_ref[...], NN, preferred_element_type=F32))

        @pl.when(j == 0)
        def _():
            acc[i] = part

        @pl.when(j > 0)
        def _():
            acc[i] += part

        @pl.when(jnp.logical_and(j == 0, i == 0))
        def _():
            dg_ref[...] = jnp.zeros_like(dg_ref)

        @pl.when(j == nj - 1)
        def _():
            dx, dg = _rms_bwd_rows(acc[i], x_ref[...], g_ref[...], r_ref[...])
            dx_ref[...] = dx
            dxb = dx.astype(BF16)
            dxb_ref[...] = dxb
            dg_ref[...] += dg
            do_ref[...] = lax.dot_general(dxb, wp_ref[...], NT, preferred_element_type=F32).astype(BF16)

    hid = pl.BlockSpec((tm, tn), lambda j, i: (i, j))
    wsp = pl.BlockSpec((tn, d), lambda j, i: (j, 0), pipeline_mode=pl.Buffered(1))
    late = pl.BlockSpec((tm, d), lambda j, i: (jnp.where(j == nj - 1, i, 0), 0))
    late_q = pl.BlockSpec((tm, q), lambda j, i: (jnp.where(j == nj - 1, i, 0), 0))
    vec = pl.BlockSpec((1, d), lambda j, i: (0, 0))
    osh = jax.ShapeDtypeStruct((t, hidden), BF16)
    return pl.pallas_call(
        body, name="ffn_bwd", grid=(nj, t // tm),
        in_specs=[pl.BlockSpec((tm, d), lambda j, i: (i, 0)), wsp, wsp, wsp, hid, hid, late, vec, late,
                  pl.BlockSpec((q, d), lambda j, i: (0, 0), pipeline_mode=pl.Buffered(1))],
        out_specs=[hid, hid, late, late, vec, late_q],
        out_shape=[osh, osh, jax.ShapeDtypeStruct((t, d), F32), jax.ShapeDtypeStruct((t, d), BF16),
                   jax.ShapeDtypeStruct((1, d), F32), jax.ShapeDtypeStruct((t, q), BF16)],
        scratch_shapes=[pltpu.VMEM((t // tm, tm, d), F32)],
        compiler_params=_params(("arbitrary", "arbitrary")),
    )(dh, w_down, w_gate, w_up, hg, hu, x, g, dres, w_prev)


MM_ROWS = 1024


def _mm_w(name, a, w, out_dtype, dims=NN):
    t, k = a.shape
    n = w.shape[1] if dims == NN else w.shape[0]
    tm, tn = min(MM_ROWS, t), min(1024, n)
    o_spec = pl.BlockSpec((tm, tn), lambda j, i: (i, j))
    b_spec = pl.BlockSpec((k, tn), lambda j, i: (0, j)) if dims == NN else pl.BlockSpec((tn, k), lambda j, i: (j, 0))
    return _mm(name, a, w, grid=(n // tn, t // tm), a_spec=pl.BlockSpec((tm, k), lambda j, i: (i, 0)), b_spec=b_spec,
               o_shape=(t, n), o_spec=o_spec, dims=dims, out_dtype=out_dtype)


def _mm_res_norm(name, a, w, res, g):
    t, k = a.shape
    d = w.shape[1]
    tm = min(ROW_TILE, t)

    def body(a_ref, w_ref, r_ref, g_ref, h_ref, n_ref):
        h = lax.dot_general(a_ref[...], w_ref[...], NN, preferred_element_type=F32) + r_ref[...]
        h_ref[...] = h
        r = lax.rsqrt(jnp.mean(h * h, axis=-1, keepdims=True) + RMS_EPS)
        n_ref[...] = (h * r * g_ref[...]).astype(BF16)

    row = pl.BlockSpec((tm, d), lambda i: (i, 0))
    return pl.pallas_call(
        body, name=name, grid=(t // tm,),
        in_specs=[pl.BlockSpec((tm, k), lambda i: (i, 0)), pl.BlockSpec((k, d), lambda i: (0, 0)), row,
                  pl.BlockSpec((1, d), lambda i: (0, 0))],
        out_specs=[row, row], out_shape=[jax.ShapeDtypeStruct((t, d), F32), jax.ShapeDtypeStruct((t, d), BF16)],
        compiler_params=_params(("parallel",)),
    )(a, w, res, g)


def _wgrad(name, a, g, tk=1024, tn=1024):
    t, k = a.shape
    n = g.shape[1]
    tm, tk, tn = min(2 * MM_ROWS, t), min(tk, k), min(tn, n)
    return _mm(name, a, g, grid=(k // tk, n // tn, t // tm),
               a_spec=pl.BlockSpec((tm, tk), lambda p, q, r: (r, p)), b_spec=pl.BlockSpec((tm, tn), lambda p, q, r: (r, q)),
               o_shape=(k, n), o_spec=pl.BlockSpec((tk, tn), lambda p, q, r: (p, q)), dims=TN, out_dtype=BF16, nk=t // tm)


def _peers():
    x, y, c = lax.axis_index("x"), lax.axis_index("y"), lax.axis_index("c")
    me = 4 * x + 2 * y + c
    out = []
    for k in range(1, N_DEV):
        kx, ky, kc = (k >> 2) & 1, (k >> 1) & 1, k & 1
        px = 1 - x if kx else x
        py = 1 - y if ky else y
        pc = 1 - c if kc else c
        out.append(((px, py, pc), 4 * px + 2 * py + pc))
    return me, out


def _cast_weights(ws, pad_rows):
    def body(*refs):
        n = len(refs) // 2
        for i_ref, o_ref, pr in zip(refs[:n], refs[n:], pad_rows):
            r, c = i_ref.shape
            o_ref[0:r, :] = i_ref[...].astype(BF16)
            if pr:
                o_ref[r:r + pr, :] = jnp.zeros((pr, c), BF16)

    return pl.pallas_call(
        body, name="cast_weights", in_specs=[VMEM] * len(ws), out_specs=[VMEM] * len(ws),
        out_shape=[jax.ShapeDtypeStruct((w.shape[0] + pr, w.shape[1]), BF16) for w, pr in zip(ws, pad_rows)],
    )(*ws)


def _window(ref, j, c):
    return ref.at[:, pl.ds(pl.multiple_of(j * c, LANES), c)]


def _scatter_copies(ins, outs, sems, cols, landed):
    send_sems, recv_sems, loc_sems = sems
    n_peer = N_DEV - 1
    me, peers = _peers()

    def src(w, j):
        return _window(ins[w], j, cols[w]) if cols[w] else ins[w].at[j]

    local = [pltpu.make_async_copy(src(w, me), outs[w].at[me], loc_sems.at[w]) for w in range(len(ins))]
    remote = [pltpu.make_async_remote_copy(
        src_ref=src(w, idx), dst_ref=outs[w].at[idx if landed else me],
        send_sem=send_sems.at[w * n_peer + k], recv_sem=recv_sems.at[w * n_peer + k],
        device_id=dev, device_id_type=pl.DeviceIdType.MESH)
        for k, (dev, idx) in reversed(list(enumerate(peers))) for w in range(len(ins))]
    return local, remote


OTHER_CHIPS = (2, 4, 6)


def _gather_copies(ins, outs, sems, cols):
    send_sems, recv_sems, loc_sems = sems
    x, y, c = lax.axis_index("x"), lax.axis_index("y"), lax.axis_index("c")
    me = 4 * x + 2 * y + c
    n_pair = N_DEV - 1

    def dev(mask):
        return (1 - x if mask & 4 else x, 1 - y if mask & 2 else y, 1 - c if mask & 1 else c)

    def slot(w, mask):
        j = jnp.bitwise_xor(me, mask)
        return _window(outs[w], j, cols[w]) if cols[w] else outs[w].at[j]

    def remote(w, pair, src, to_slot, target):
        return pltpu.make_async_remote_copy(src_ref=src, dst_ref=slot(w, to_slot), send_sem=send_sems.at[w * n_pair + pair],
                                            recv_sem=recv_sems.at[w * n_pair + pair], device_id=dev(target),
                                            device_id_type=pl.DeviceIdType.MESH)

    ws = range(len(ins))
    return dict(
        local=[pltpu.make_async_copy(ins[w], slot(w, 0), loc_sems.at[w]) for w in ws],
        to_chips=[remote(w, 1 + t, ins[w], 0, m) for t, m in enumerate(OTHER_CHIPS) for w in ws],
        to_core=[remote(w, 0, ins[w], 0, 1) for w in ws],
        from_chips=[remote(w, 1 + t, ins[w], m, 0) for t, m in enumerate(OTHER_CHIPS) for w in ws],
        pass_on=[remote(w, 4 + t, slot(w, m), m, 1) for t, m in enumerate(OTHER_CHIPS) for w in ws],
        from_core=[remote(w, 0, ins[w], 1, 0) for w in ws]
        + [remote(w, 4 + t, ins[w], m + 1, 0) for t, m in enumerate(OTHER_CHIPS) for w in ws])


def _exchange_start(ins, outs, sems, gather, cols):
    if gather:
        cps = _gather_copies(ins, outs, sems, cols)
        for cp in cps["local"] + cps["to_chips"] + cps["to_core"]:
            cp.start()
    else:
        local, remote = _scatter_copies(ins, outs, sems, cols, False)
        for cp in local + remote:
            cp.start()


def _exchange_pass_on(ins, outs, sems, gather, cols, chips):
    if gather:
        cps = _gather_copies(ins, outs, sems, cols)
        n = len(ins)
        for t in chips:
            for arrived, onward in zip(cps["from_chips"][t * n:(t + 1) * n], cps["pass_on"][t * n:(t + 1) * n]):
                arrived.wait_recv()
                onward.start()


def _exchange_wait(ins, outs, sems, gather, cols):
    if gather:
        cps = _gather_copies(ins, outs, sems, cols)
        for cp in cps["local"]:
            cp.wait()
        for cp in cps["to_chips"] + cps["to_core"] + cps["pass_on"]:
            cp.wait_send()
        for cp in cps["from_core"]:
            cp.wait_recv()
    else:
        local, remote = _scatter_copies(ins, outs, sems, cols, True)
        for cp in local:
            cp.wait()
        for cp in remote:
            cp.wait_send()
            cp.wait_recv()


def _exchange_shapes(arrs, gather, cols):
    n = len(arrs)
    out_shape = []
    for a, c in zip(arrs, cols):
        if gather:
            shape = (a.shape[0], N_DEV * c) if c else (N_DEV,) + a.shape
        else:
            shape = (N_DEV, a.shape[0], c) if c else a.shape
        out_shape.append(jax.ShapeDtypeStruct(shape, a.dtype))
    sems = [pltpu.SemaphoreType.DMA((n * (N_DEV - 1),)), pltpu.SemaphoreType.DMA((n * (N_DEV - 1),)),
            pltpu.SemaphoreType.DMA((n,))]
    return out_shape, sems


def _call(body, *, name, grid, in_specs, out_specs, out_shape, scratch, sem, args, ride=None):
    if ride is None:
        outs = pl.pallas_call(body, name=name, grid=grid, in_specs=in_specs, out_specs=out_specs, out_shape=out_shape,
                              scratch_shapes=scratch, compiler_params=_params(sem))(*args)
        return outs, None
    arrs, gather, cols = ride
    n, n_in, n_out, n_scr = len(arrs), len(in_specs), len(out_specs), len(scratch)
    x_shape, x_sems = _exchange_shapes(arrs, gather, cols)

    def riding(*refs):
        ins, x_ins = refs[:n_in], refs[n_in:n_in + n]
        outs = refs[n_in + n:n_in + n + n_out]
        x_outs = refs[n_in + n + n_out:n_in + 2 * n + n_out]
        scr = refs[n_in + 2 * n + n_out:n_in + 2 * n + n_out + n_scr]
        sems = refs[n_in + 2 * n + n_out + n_scr:]
        def at(step):
            return functools.reduce(jnp.logical_and, [pl.program_id(a) == v for a, v in enumerate(step)])

        @pl.when(at((0,) * len(grid)))
        def _():
            _exchange_start(x_ins, x_outs, sems, gather, cols)

        @pl.when(at((grid[0] // 2,) + (0,) * (len(grid) - 2) + (grid[-1] // 2,)))
        def _():
            _exchange_pass_on(x_ins, x_outs, sems, gather, cols, (0, 1))

        @pl.when(at((grid[0] // 2,) + (0,) * (len(grid) - 2) + (3 * grid[-1] // 4,)))
        def _():
            _exchange_pass_on(x_ins, x_outs, sems, gather, cols, (2,))

        body(*ins, *outs, *scr)

        @pl.when(at(tuple(g - 1 for g in grid)))
        def _():
            _exchange_wait(x_ins, x_outs, sems, gather, cols)

    res = pl.pallas_call(
        riding, name=name, grid=grid, in_specs=list(in_specs) + [ANY] * n, out_specs=list(out_specs) + [ANY] * n,
        out_shape=list(out_shape) + x_shape, scratch_shapes=list(scratch) + x_sems,
        compiler_params=_params(("arbitrary",) * len(grid)))(*args, *arrs)
    return res[:n_out], res[n_out:]


def _my_block():
    return (4 * lax.axis_index("x") + 2 * lax.axis_index("y") + lax.axis_index("c")).astype(jnp.int32).reshape(1)


def _proj_in_gather(x, g, w_shard):
    t, k = x.shape
    cs = w_shard.shape[1]
    tm = min(MM_ROWS, t)
    ni = t // tm
    arrival = (0, 1, 2, 4, 3, 5, 6, 7)

    def mask_at(s):
        return jnp.where(s == 3, 4, jnp.where(s == 4, 3, s))

    def body(me_ref, x_ref, g_ref, w_hbm, o_ref, all_hbm, n_hbm, w_vmem, n_vmem, send_sems, recv_sems, loc_sems,
             load_sems, n_sem):
        s, i = pl.program_id(0), pl.program_id(1)
        cps = _gather_copies([w_hbm], [all_hbm], (send_sems, recv_sems, loc_sems), (cs,))
        by_mask = {0: cps["local"][0], 1: cps["from_core"][0]}
        for t_chip, m in enumerate(OTHER_CHIPS):
            by_mask[m] = cps["from_chips"][t_chip]
            by_mask[m + 1] = cps["from_core"][1 + t_chip]
        arrived = [by_mask[m] for m in arrival]

        def load(step):
            src = w_hbm if step == 0 else _window(all_hbm, jnp.bitwise_xor(me_ref[0], arrival[step]), cs)
            return pltpu.make_async_copy(src, w_vmem.at[step % 2], load_sems.at[step % 2])

        @pl.when(jnp.logical_and(s == 0, i == 0))
        def _():
            for cp in cps["local"] + cps["to_chips"] + cps["to_core"]:
                cp.start()
            load(0).start()

        for step, mask in enumerate(arrival):
            @pl.when(jnp.logical_and(s == step, i == 0))
            def _(step=step):
                load(step).wait()

            if step + 1 < N_DEV:
                @pl.when(jnp.logical_and(s == step, i == min(1, ni - 1)))
                def _(step=step):
                    arrived[step + 1].wait_recv()
                    if arrival[step + 1] in OTHER_CHIPS:
                        cps["pass_on"][OTHER_CHIPS.index(arrival[step + 1])].start()
                    load(step + 1).start()

        @pl.when(s == 0)
        def _():
            xf = x_ref[...]
            r = lax.rsqrt(jnp.mean(xf * xf, axis=-1, keepdims=True) + RMS_EPS)
            n_vmem[i] = (xf * r * g_ref[...]).astype(BF16)
            keep = pltpu.make_async_copy(n_vmem.at[i], n_hbm.at[pl.ds(pl.multiple_of(i * tm, tm), tm), :], n_sem)
            keep.start()
            keep.wait()

        o_ref[...] = lax.dot_general(n_vmem[i], w_vmem[s % 2], NN, preferred_element_type=F32).astype(BF16)

        @pl.when(jnp.logical_and(s == N_DEV - 1, i == ni - 1))
        def _():
            cps["local"][0].wait()
            for cp in cps["to_chips"] + cps["to_core"] + cps["pass_on"]:
                cp.wait_send()

    return pl.pallas_call(
        body, name="proj_in",
        grid_spec=pltpu.PrefetchScalarGridSpec(
            num_scalar_prefetch=1, grid=(N_DEV, ni),
            in_specs=[pl.BlockSpec((tm, k), lambda s, i, me: (jnp.where(s == 0, i, 0), 0)),
                      pl.BlockSpec((1, k), lambda s, i, me: (0, 0)), ANY],
            out_specs=[pl.BlockSpec((tm, cs), lambda s, i, me: (i, jnp.bitwise_xor(me[0], mask_at(s)))), ANY, ANY],
            scratch_shapes=[pltpu.VMEM((2, k, cs), BF16), pltpu.VMEM((ni, tm, k), BF16),
                            pltpu.SemaphoreType.DMA((N_DEV - 1,)), pltpu.SemaphoreType.DMA((N_DEV - 1,)),
                            pltpu.SemaphoreType.DMA((1,)), pltpu.SemaphoreType.DMA((2,)), pltpu.SemaphoreType.DMA]),
        out_shape=[jax.ShapeDtypeStruct((t, N_DEV * cs), BF16), jax.ShapeDtypeStruct((k, N_DEV * cs), BF16),
                   jax.ShapeDtypeStruct((t, k), BF16)],
        compiler_params=_params(("arbitrary", "arbitrary")),
    )(_my_block(), x, g, w_shard)


def _gw_in_scatter(a, g):
    t, k = a.shape
    cs = g.shape[1] // N_DEV
    tm = min(MM_ROWS, t)
    nr = t // tm
    n_chip = N_DEV // 2
    chips = (6, 4, 2, 0)

    def body(me_ref, a_ref, g_ref, out_hbm, acc, stage, other, core_send, core_recv, chip_send, chip_recv, loc_sem):
        s, r = pl.program_id(0), pl.program_id(1)
        x, y, c = lax.axis_index("x"), lax.axis_index("y"), lax.axis_index("c")
        my_chip = 2 * x + y
        part = lax.dot_general(a_ref[...], g_ref[...], TN, preferred_element_type=F32)

        def to_core(m):
            return pltpu.make_async_remote_copy(src_ref=stage.at[0], dst_ref=other.at[m], send_sem=core_send.at[m],
                                                recv_sem=core_recv.at[m], device_id=(x, y, 1 - c),
                                                device_id_type=pl.DeviceIdType.MESH)

        def to_chip(m, landed):
            mask = chips[m]
            there = (1 - x if mask & 4 else x, 1 - y if mask & 2 else y, c)
            slot = (2 * there[0] + there[1]) if landed else my_chip
            return pltpu.make_async_remote_copy(src_ref=stage.at[1], dst_ref=out_hbm.at[slot], send_sem=chip_send.at[m],
                                                recv_sem=chip_recv.at[m], device_id=there,
                                                device_id_type=pl.DeviceIdType.MESH)

        local = pltpu.make_async_copy(stage.at[1], out_hbm.at[my_chip], loc_sem)

        @pl.when(r == 0)
        def _():
            acc[...] = part

        @pl.when(r > 0)
        def _():
            acc[...] += part

        for step in range(N_DEV):
            m = step // 2

            @pl.when(jnp.logical_and(s == step, r == nr - 1))
            def _(step=step, m=m):
                if step % 2 == 0:
                    if m > 0:
                        to_core(m - 1).wait_send()
                    stage[0] = acc[...].astype(BF16)
                    to_core(m).start()
                else:
                    if m > 0:
                        to_chip(m - 1, False).wait_send()
                    to_core(m).wait_recv()
                    stage[1] = (acc[...] + other[m].astype(F32)).astype(BF16)
                    if m < n_chip - 1:
                        to_chip(m, False).start()
                    else:
                        local.start()
                        to_core(m).wait_send()
                        local.wait()
                        for mm in range(n_chip - 1):
                            to_chip(mm, True).wait_recv()

    return pl.pallas_call(
        body, name="gw_in",
        grid_spec=pltpu.PrefetchScalarGridSpec(
            num_scalar_prefetch=1, grid=(N_DEV, nr),
            in_specs=[pl.BlockSpec((tm, k), lambda s, r, me: (r, 0)),
                      pl.BlockSpec((tm, cs), lambda s, r, me: (r, jnp.bitwise_xor(me[0], N_DEV - 1 - s)))],
            out_specs=ANY,
            scratch_shapes=[pltpu.VMEM((k, cs), F32), pltpu.VMEM((2, k, cs), BF16), pltpu.VMEM((n_chip, k, cs), BF16),
                            pltpu.SemaphoreType.DMA((n_chip,)), pltpu.SemaphoreType.DMA((n_chip,)),
                            pltpu.SemaphoreType.DMA((n_chip - 1,)), pltpu.SemaphoreType.DMA((n_chip - 1,)),
                            pltpu.SemaphoreType.DMA]),
        out_shape=jax.ShapeDtypeStruct((n_chip, k, cs), BF16),
        compiler_params=_params(("arbitrary", "arbitrary")),
    )(_my_block(), a, g)


SMALL_ROWS = 8


def _allreduce_small(parts, loss_part):
    n, d = len(parts), parts[0].shape[1]

    def body(*refs):
        part_refs, loss_ref, o_ref = refs[:n], refs[n], refs[n + 1]
        mine_ref, all_ref, send_sems, recv_sems = refs[n + 2:]
        me, peers = _peers()
        mine_ref[...] = jnp.zeros_like(mine_ref)
        for i, p_ref in enumerate(part_refs):
            mine_ref[i:i + 1, :] = p_ref[...]
        mine_ref[SMALL_ROWS - 1:SMALL_ROWS, 0:LANES] = loss_ref[0:1, :]
        all_ref[me] = mine_ref[...]
        for k, (dev, idx) in enumerate(peers):
            pltpu.make_async_remote_copy(src_ref=mine_ref, dst_ref=all_ref.at[me], send_sem=send_sems.at[k],
                                         recv_sem=recv_sems.at[k], device_id=dev,
                                         device_id_type=pl.DeviceIdType.MESH).start()
        for k, (dev, idx) in enumerate(peers):
            cp = pltpu.make_async_remote_copy(src_ref=mine_ref, dst_ref=all_ref.at[idx], send_sem=send_sems.at[k],
                                              recv_sem=recv_sems.at[k], device_id=dev,
                                              device_id_type=pl.DeviceIdType.MESH)
            cp.wait_send()
            cp.wait_recv()
        tot = all_ref[0]
        for dvc in range(1, N_DEV):
            tot = tot + all_ref[dvc]
        o_ref[...] = tot

    return pl.pallas_call(
        body, name="allreduce_small", in_specs=[VMEM] * (n + 1), out_specs=VMEM,
        out_shape=jax.ShapeDtypeStruct((SMALL_ROWS, d), F32),
        scratch_shapes=[pltpu.VMEM((SMALL_ROWS, d), F32), pltpu.VMEM((N_DEV, SMALL_ROWS, d), F32),
                        pltpu.SemaphoreType.DMA((N_DEV - 1,)), pltpu.SemaphoreType.DMA((N_DEV - 1,))],
    )(*parts, loss_part)


def _adam_math(g, w, m, v):
    m_new = ADAM_B1 * m + (1.0 - ADAM_B1) * g
    v_new = ADAM_B2 * v + (1.0 - ADAM_B2) * (g * g)
    m_hat = m_new / (1.0 - ADAM_B1 ** ADAM_STEP)
    v_hat = v_new / (1.0 - ADAM_B2 ** ADAM_STEP)
    delta = -ADAM_LR * (m_hat / (jnp.sqrt(v_hat) + ADAM_EPS) + ADAM_WD * w)
    return delta, m_new, v_new


def _adam(name, pieces, w, m, v):
    r, c = w.shape
    n_piece, _, cp = pieces.shape
    tr = r
    for cand in (256, 176, 128, 64):
        if r % cand == 0 and r > cand:
            tr = cand
            break

    def body(p_ref, w_ref, m_ref, v_ref, g_ref, d_ref, mo_ref, vo_ref):
        g = p_ref[0, :, 0:c].astype(F32)
        for j in range(1, n_piece):
            g = g + p_ref[j, :, 0:c].astype(F32)
        delta, m_new, v_new = _adam_math(g, w_ref[...], m_ref[...], v_ref[...])
        g_ref[...] = g
        d_ref[...] = delta
        mo_ref[...] = m_new
        vo_ref[...] = v_new

    blk = pl.BlockSpec((tr, c), lambda i: (i, 0))
    osh = jax.ShapeDtypeStruct((r, c), F32)
    return pl.pallas_call(
        body, name=name, grid=(r // tr,),
        in_specs=[pl.BlockSpec((n_piece, tr, cp), lambda i: (0, i, 0)), blk, blk, blk],
        out_specs=[blk, blk, blk, blk], out_shape=[osh, osh, osh, osh],
        compiler_params=_params(("parallel",)),
    )(pieces, w, m, v)


def _adam_small(g_all, ws, ms, vs):
    n = len(ws)

    def body(*refs):
        g_ref, ins, outs = refs[0], refs[1:1 + 3 * n], refs[1 + 3 * n:]
        for i in range(n):
            g = g_ref[i:i + 1, :]
            delta, m_new, v_new = _adam_math(g, ins[i][...], ins[n + i][...], ins[2 * n + i][...])
            for kind, val in enumerate((g, delta, m_new, v_new)):
                outs[kind * n + i][...] = val

    osh = jax.ShapeDtypeStruct(ws[0].shape, F32)
    res = pl.pallas_call(body, name="adam_small", in_specs=[VMEM] * (1 + 3 * n), out_specs=[VMEM] * (4 * n),
                         out_shape=[osh] * (4 * n))(g_all, *ws, *ms, *vs)
    return res[:n], res[n:2 * n], res[2 * n:3 * n], res[3 * n:]


def _local_step(x, mem, pos, tgt, gains, w_in_shard, shards, batch):
    g_mix, g_mem_q, g_mem_kv, g_ffn, g_final = gains
    t, d = x.shape
    s = t // batch
    n_mem = mem.shape[0] // batch
    n_sh = N_DEV
    width = shards[0].shape[0]
    nb = width // LANES

    lane = np.arange(LANES) % HEAD_DIM
    sel_lo = (lane < ROPE_HALF).astype(np.float32)[None, :]
    sel_hi = ((lane >= ROPE_HALF) & (lane < 2 * ROPE_HALF)).astype(np.float32)[None, :]
    freqs = np.float32(ROPE_THETA) ** (-np.arange(ROPE_HALF, dtype=np.float32) / np.float32(ROPE_HALF))
    inv_freq = np.where(lane < 2 * ROPE_HALF, freqs[lane % ROPE_HALF], 0.0).astype(np.float32)[None, :]
    cos_t, sin_a, sin_b = _rope_tables(pos, jnp.asarray(inv_freq), jnp.asarray(sel_lo), jnp.asarray(sel_hi))
    bias = _dilated_bias_tiles(s)

    proj, w_in, n1 = _proj_in_gather(x, g_mix, w_in_shard)
    qk_a = _rope_apply("rope_fwd", [proj], 2 * width, cos_t, sin_a, sin_b, 1.0)
    cs_up = shards[0].shape[1]
    (o_a, lse_a), (w_up_a, w_up_b, w_out, w_q, w_kv, w_o, w_fg, w_fu, w_fd) = _da_fwd(
        qk_a, proj, 2 * nb, bias, batch, s, ride=(shards, True, (cs_up, cs_up, 0, 0, 0, cs_up, 0, 0, 0)))
    (o_b, tot_b), _ = _sb_fwd(proj, 3 * nb, 4 * nb, 5 * nb, batch, s)
    w_out = w_out.reshape(d, d)
    w_q = w_q.reshape(d, -1)
    w_kv = w_kv.reshape(d, -1)
    w_fd = w_fd.reshape(-1, d)
    w_fg = w_fg.reshape(-1, d)
    w_fu = w_fu.reshape(-1, d)
    ua, ub, mixed, n2, h1, q_m = _mixer_fwd(o_a, o_b, w_up_a, w_up_b, proj, 6 * nb, w_out, x, g_mem_q, w_q)
    mem_n = _rms_fwd("norm_mem_kv", mem, g_mem_kv)
    kv_m = _mm_w("mem_kv", mem_n, w_kv, BF16)
    o_m = _mem_fwd(q_m, kv_m, batch, s, n_mem)
    h2, n3 = _mm_res_norm("mem_out", o_m, w_o, h1, g_ffn)
    hg, hu, act = _ffn_up(n3, w_fg, w_fu)
    loss_part, dh3, dh3_b, dg_final = _loss_head(act, w_fd, h2, tgt, g_final.reshape(1, d))

    dhg, dhu, dh2, dh2_b, dg_ffn, do_m = _ffn_bwd(dh3_b, w_fd, w_fg, w_fu, hg, hu, h2, g_ffn, dh3, w_o)
    gw_fd = _wgrad("gw_ffn_down", act, dh3_b)
    gw_fg = _wgrad("gw_ffn_gate", dhg, n3)
    gw_fu = _wgrad("gw_ffn_up", dhu, n3)

    gw_o = _wgrad("gw_mem_o", o_m, dh2_b)
    dq_m, dkv_m = _mem_bwd(q_m, kv_m, do_m, batch, s, n_mem)
    gw_q = _wgrad("gw_mem_q", n2, dq_m)
    gw_kv = _wgrad("gw_mem_kv", mem_n, dkv_m)
    (dg_mem_kv,) = _rms_bwd("norm_mem_kv_bwd", (dkv_m, w_kv, NT), mem, g_mem_kv, None, ())
    dh1, dh1_b, dg_mem_q = _rms_bwd("norm_mem_q_bwd", (dq_m, w_q, NT), h1, g_mem_q, dh2, ("f32", "bf16"))

    gw_out = _wgrad("gw_out", mixed, dh1_b)
    dua, dub, dgates, do_a, do_b = _mixer_bwd(dh1_b, w_out, ua, ub, proj, 6 * nb, w_up_a, w_up_b)
    gw_ua = _wgrad("gw_up_a", o_a, dua)
    gw_ub = _wgrad("gw_up_b", o_b, dub)
    (dq_ar, dk_ar, dv_a), (p_fg, p_fd) = _da_bwd(
        qk_a, proj, 2 * nb, bias, o_a, lse_a, do_a, batch, s,
        ride=([gw_fg.reshape(n_sh, -1, d), gw_fd.reshape(n_sh, -1, d)], False, (0, 0)))
    mid = [gw_ua, gw_ub, gw_out.reshape(n_sh, -1, d), gw_q.reshape(n_sh, -1, gw_q.shape[1]),
           gw_kv.reshape(n_sh, -1, gw_kv.shape[1]), gw_o, gw_fu.reshape(n_sh, -1, d)]
    (dq_b, dk_b, dv_b), (*p_mid, p_fu) = _sb_bwd(proj, 3 * nb, 4 * nb, 5 * nb, tot_b, do_b, batch, s,
                                                 ride=(mid, False, (cs_up, cs_up, 0, 0, 0, cs_up, 0)))
    p_ffn = [p_fg, p_fu, p_fd]
    dproj = _rope_apply("rope_bwd", [dq_ar, dk_ar], width, cos_t, sin_a, sin_b, -1.0,
                        tail=(dv_a, dq_b, dk_b, dv_b, dgates))
    grad_x, dg_mix = _rms_bwd("proj_in_bwd", (dproj, w_in, NT), x, g_mix, dh1, ("f32",))
    p_in = _gw_in_scatter(n1, dproj)
    return loss_part, grad_x, [p_in] + list(p_mid) + p_ffn, (dg_mix, dg_mem_q, dg_mem_kv, dg_ffn, dg_final)


WEIGHTS =("w_in", "w_up_a", "w_up_b", "w_out", "w_q_mem", "w_kv_mem", "w_o_mem", "w_ffn_gate", "w_ffn_up", "w_ffn_down")
GAINS = ("g_mix", "g_mem_q", "g_mem_kv", "g_ffn", "g_final")
ORDER = ("g_mix", "w_in", "w_up_a", "w_up_b", "w_out", "g_mem_q", "g_mem_kv", "w_q_mem", "w_kv_mem", "w_o_mem", "g_ffn",
         "w_ffn_gate", "w_ffn_up", "w_ffn_down", "g_final")


def kernel(x, mem, positions, g_mix, w_in, w_up_a, w_up_b, w_out, g_mem_q, g_mem_kv, w_q_mem, w_kv_mem, w_o_mem, g_ffn, w_ffn_gate, w_ffn_up, w_ffn_down, g_final, loss_target, m_g_mix, m_w_in, m_w_up_a, m_w_up_b, m_w_out, m_g_mem_q, m_g_mem_kv, m_w_q_mem, m_w_kv_mem, m_w_o_mem, m_g_ffn, m_w_ffn_gate, m_w_ffn_up, m_w_ffn_down, m_g_final, v_g_mix, v_w_in, v_w_up_a, v_w_up_b, v_w_out, v_g_mem_q, v_g_mem_kv, v_w_q_mem, v_w_kv_mem, v_w_o_mem, v_g_ffn, v_w_ffn_gate, v_w_ffn_up, v_w_ffn_down, v_g_final):
    given = dict(locals())
    batch, s, d = x.shape
    t = batch * s
    flipped = ("w_ffn_gate", "w_ffn_up")

    def view(a, n):
        a = a.reshape(a.shape[-2:])
        return a.T if n in flipped else a

    def unview(a, n):
        return (a.T if n in flipped else a).reshape(given[n].shape)

    shard = {n: view(given[n], n) for n in WEIGHTS}
    gains = [given[n].reshape(1, d) for n in GAINS]

    pad = (-shard["w_ffn_down"].shape[0]) % LANES
    cast = _cast_weights([shard[n] for n in WEIGHTS], [pad if n in flipped + ("w_ffn_down",) else 0 for n in WEIGHTS])
    loss_part, grad_x, pieces, dgains = _local_step(
        x.reshape(t, d), mem.reshape(-1, d), positions.reshape(t, 1), loss_target.reshape(t, d), gains, cast[0],
        cast[1:], batch)

    grad, delta, new_m, new_v = {}, {}, {}, {}
    for n, p in zip(WEIGHTS, pieces):
        outs = _adam("adam_" + n, p, shard[n], view(given["m_" + n], n), view(given["v_" + n], n))
        grad[n], delta[n], new_m[n], new_v[n] = [unview(o, n) for o in outs]

    g_all = _allreduce_small(list(dgains), loss_part)
    small = _adam_small(g_all, gains, [given["m_" + n].reshape(1, d) for n in GAINS],
                        [given["v_" + n].reshape(1, d) for n in GAINS])
    for out, vals in zip((grad, delta, new_m, new_v), small):
        for n, val in zip(GAINS, vals):
            out[n] = val.reshape(given[n].shape)

    loss = g_all[SMALL_ROWS - 1, 0]
    return (loss, grad_x.reshape(x.shape), *[grad[n] for n in ORDER], *[delta[n] for n in ORDER],
            *[new_m[n] for n in ORDER], *[new_v[n] for n in ORDER])
```

```python
import functools
import math

import jax
import jax.numpy as jnp
import numpy as np
from jax import lax
from jax.experimental import pallas as pl
from jax.experimental.pallas import tpu as pltpu

F32 = jnp.float32
BF16 = jnp.bfloat16

N_DEV = 8
HEAD_DIM = 64
MEM_HEAD_DIM = 128
N_HEADS_MEM = 4
BLOCK = 128
DIL_PATTERNS = ((128, 1), (512, 4), (2048, 16))
ROPE_THETA = 500000.0
ROPE_HALF = 8
RMS_EPS = 1e-6
ADAM_LR, ADAM_B1, ADAM_B2, ADAM_EPS, ADAM_WD, ADAM_STEP = 0.001, 0.9, 0.999, 1e-08, 0.01, 10
NEG = -1e30
ROW_TILE = 512
LANES = 128

ANY = pl.BlockSpec(memory_space=pl.ANY)
VMEM = pl.BlockSpec(memory_space=pltpu.VMEM)
NN = (((1,), (0,)), ((), ()))
NT = (((1,), (1,)), ((), ()))
TN = (((0,), (0,)), ((), ()))


def _params(sem):
    return pltpu.CompilerParams(dimension_semantics=sem)


def _mm(name, a, b, *, grid, a_spec, b_spec, o_shape, o_spec, dims, out_dtype, nk=1):
    def body(*refs):
        a_ref, b_ref, o_ref = refs[0], refs[1], refs[2]
        p = lax.dot_general(a_ref[...], b_ref[...], dims, preferred_element_type=F32)
        if nk == 1:
            o_ref[...] = p.astype(out_dtype)
            return
        acc_ref = refs[-1]
        k = pl.program_id(len(grid) - 1)

        @pl.when(k == 0)
        def _():
            acc_ref[...] = p

        @pl.when(k > 0)
        def _():
            acc_ref[...] += p

        @pl.when(k == nk - 1)
        def _():
            o_ref[...] = acc_ref[...].astype(out_dtype)

    o_block = tuple(d for d in o_spec.block_shape if d is not None)
    sem = ("parallel",) * (len(grid) - 1) + (("arbitrary",) if nk > 1 else ("parallel",))
    return pl.pallas_call(
        body, name=name, grid=grid, in_specs=[a_spec, b_spec],
        out_specs=o_spec, out_shape=jax.ShapeDtypeStruct(o_shape, out_dtype),
        scratch_shapes=[pltpu.VMEM(o_block, F32)] if nk > 1 else [],
        compiler_params=_params(sem),
    )(a, b)


def _rms_fwd(name, x, g):
    t, d = x.shape
    tm = min(ROW_TILE, t)

    def body(x_ref, g_ref, o_ref):
        xf = x_ref[...]
        r = lax.rsqrt(jnp.mean(xf * xf, axis=-1, keepdims=True) + RMS_EPS)
        o_ref[...] = (xf * r * g_ref[...]).astype(BF16)

    return pl.pallas_call(
        body, name=name, grid=(t // tm,),
        in_specs=[pl.BlockSpec((tm, d), lambda i: (i, 0)), pl.BlockSpec((1, d), lambda i: (0, 0))],
        out_specs=pl.BlockSpec((tm, d), lambda i: (i, 0)), out_shape=jax.ShapeDtypeStruct((t, d), BF16),
        compiler_params=_params(("parallel",)),
    )(x, g)


def _rms_bwd_rows(dnf, xf, gv, res):
    r = lax.rsqrt(jnp.mean(xf * xf, axis=-1, keepdims=True) + RMS_EPS)
    xh = xf * r
    dxh = dnf * gv
    dx = r * (dxh - xh * jnp.mean(dxh * xh, axis=-1, keepdims=True))
    if res is not None:
        dx = dx + res
    return dx, jnp.sum(dnf * xh, axis=0, keepdims=True)


def _rms_bwd(name, dn, x, g, dres, want):
    t, d = x.shape
    tm = min(ROW_TILE, t)
    has_res = dres is not None
    lhs = list(dn) if isinstance(dn, tuple) else [dn]
    n_lhs = len(lhs[:2])

    def body(*refs):
        x_ref, g_ref = refs[n_lhs], refs[n_lhs + 1]
        r_ref = refs[n_lhs + 2] if has_res else None
        dx_refs, dg_ref = refs[-1 - len(want):-1], refs[-1]
        if n_lhs == 2:
            dnf = lax.dot_general(refs[0][...], refs[1][...], lhs[2], preferred_element_type=F32)
        else:
            dnf = refs[0][...].astype(F32)
        dx, dg = _rms_bwd_rows(dnf, x_ref[...], g_ref[...], r_ref[...] if has_res else None)
        for kind, dx_ref in zip(want, dx_refs):
            dx_ref[...] = dx.astype(F32 if kind == "f32" else BF16)

        @pl.when(pl.program_id(0) == 0)
        def _():
            dg_ref[...] = jnp.zeros_like(dg_ref)

        dg_ref[...] += dg

    row = pl.BlockSpec((tm, d), lambda i: (i, 0))
    vec = pl.BlockSpec((1, d), lambda i: (0, 0))
    if n_lhs == 2:
        first = [pl.BlockSpec((tm, lhs[0].shape[1]), lambda i: (i, 0)), pl.BlockSpec(lhs[1].shape, lambda i: (0, 0))]
    else:
        first = [row]
    return pl.pallas_call(
        body, name=name, grid=(t // tm,),
        in_specs=first + [row, vec] + ([row] if has_res else []),
        out_specs=[row] * len(want) + [vec],
        out_shape=[jax.ShapeDtypeStruct((t, d), F32 if kind == "f32" else BF16) for kind in want]
        + [jax.ShapeDtypeStruct((1, d), F32)],
        compiler_params=_params(("arbitrary",)),
    )(*(lhs[:2] + [x, g] + ([dres] if has_res else [])))


def _loss_head(a, w, res, tgt, g):
    t, d = res.shape
    k = a.shape[1]
    tm = min(ROW_TILE, t)

    def body(a_ref, w_ref, r_ref, t_ref, g_ref, loss_ref, dh_ref, dhb_ref, dg_ref):
        xf = lax.dot_general(a_ref[...], w_ref[...], NN, preferred_element_type=F32) + r_ref[...]
        gv = g_ref[...]
        r = lax.rsqrt(jnp.mean(xf * xf, axis=-1, keepdims=True) + RMS_EPS)
        xh = xf * r
        e = xh * gv - t_ref[...]
        dy = e * (1.0 / d)
        dxh = dy * gv
        dh = r * (dxh - xh * jnp.mean(dxh * xh, axis=-1, keepdims=True))
        dh_ref[...] = dh
        dhb_ref[...] = dh.astype(BF16)

        @pl.when(pl.program_id(0) == 0)
        def _():
            dg_ref[...] = jnp.zeros_like(dg_ref)
            loss_ref[...] = jnp.zeros_like(loss_ref)

        dg_ref[...] += jnp.sum(dy * xh, axis=0, keepdims=True)
        part = jnp.sum(jnp.sum(e * e, axis=1, keepdims=True), axis=0, keepdims=True) * (0.5 / d)
        loss_ref[...] += jnp.broadcast_to(part, loss_ref.shape)

    row = pl.BlockSpec((tm, d), lambda i: (i, 0))
    vec = pl.BlockSpec((1, d), lambda i: (0, 0))
    return pl.pallas_call(
        body, name="loss_head", grid=(t // tm,),
        in_specs=[pl.BlockSpec((tm, k), lambda i: (i, 0)), pl.BlockSpec((k, d), lambda i: (0, 0)), row, row, vec],
        out_specs=[pl.BlockSpec((8, LANES), lambda i: (0, 0)), row, row, vec],
        out_shape=[jax.ShapeDtypeStruct((8, LANES), F32), jax.ShapeDtypeStruct((t, d), F32),
                   jax.ShapeDtypeStruct((t, d), BF16), jax.ShapeDtypeStruct((1, d), F32)],
        compiler_params=_params(("arbitrary",)),
    )(a, w, res, tgt, g)


def _rope_tables(pos, inv_freq, sel_lo, sel_hi):
    t = pos.shape[0]
    tm = min(ROW_TILE, t)

    def body(p_ref, f_ref, lo_ref, hi_ref, c_ref, sa_ref, sb_ref):
        ang = p_ref[...].astype(F32) * f_ref[...]
        rot = lo_ref[...] + hi_ref[...]
        cs, sn = jnp.cos(ang), jnp.sin(ang)
        c_ref[...] = cs * rot + (1.0 - rot)
        sa_ref[...] = -sn * lo_ref[...]
        sb_ref[...] = sn * hi_ref[...]

    vec = pl.BlockSpec((1, LANES), lambda i: (0, 0))
    row = pl.BlockSpec((tm, LANES), lambda i: (i, 0))
    return pl.pallas_call(
        body, name="rope_tables", grid=(t // tm,),
        in_specs=[pl.BlockSpec((tm, 1), lambda i: (i, 0)), vec, vec, vec],
        out_specs=[row, row, row], out_shape=[jax.ShapeDtypeStruct((t, LANES), F32)] * 3,
        compiler_params=_params(("parallel",)),
    )(pos, inv_freq, sel_lo, sel_hi)


def _rope_apply(name, srcs, width, cos_t, sin_a, sin_b, sign, tail=()):
    t = srcs[0].shape[0]
    tm = min(ROW_TILE, t)
    n_cols = width // LANES
    n_src = len(srcs)

    def body(*refs):
        x_refs, tail_refs = refs[:n_src], refs[n_src:n_src + len(tail)]
        c_ref, sa_ref, sb_ref, o_ref = refs[n_src + len(tail):]
        cs, sa, sb = c_ref[...], sign * sa_ref[...], sign * sb_ref[...]
        for a, x_ref in enumerate(x_refs):
            for c in range(n_cols):
                xf = x_ref[:, c * LANES:(c + 1) * LANES].astype(F32)
                up = pltpu.roll(xf, LANES - ROPE_HALF, 1)
                dn = pltpu.roll(xf, ROPE_HALF, 1)
                o_ref[:, a * width + c * LANES:a * width + (c + 1) * LANES] = (xf * cs + up * sa + dn * sb).astype(BF16)
        col = n_src * width
        for t_ref in tail_refs:
            o_ref[:, col:col + t_ref.shape[1]] = t_ref[...]
            col += t_ref.shape[1]

    wide = n_src * width + sum(a.shape[1] for a in tail)
    tab = pl.BlockSpec((tm, LANES), lambda i: (i, 0))
    return pl.pallas_call(
        body, name=name, grid=(t // tm,),
        in_specs=[pl.BlockSpec((tm, width), lambda i: (i, 0))] * n_src
        + [pl.BlockSpec((tm, a.shape[1]), lambda i: (i, 0)) for a in tail] + [tab, tab, tab],
        out_specs=pl.BlockSpec((tm, wide), lambda i: (i, 0)),
        out_shape=jax.ShapeDtypeStruct((t, wide), BF16),
        compiler_params=_params(("parallel",)),
    )(*srcs, *tail, cos_t, sin_a, sin_b)


DA_T = 256
MIX_STREAMS = 4
SB_BWD_STREAMS = 2


def _lane_lo():
    return lax.broadcasted_iota(jnp.int32, (BLOCK, LANES), 1) < HEAD_DIM


def _dilated_bias_tiles(s):
    n = s // DA_T
    dist = (np.arange(n)[:, None, None] * DA_T + np.arange(DA_T)[None, :, None] - np.arange(DA_T)[None, None, :])
    cnt = np.zeros(dist.shape, np.float32)
    for window, dil in DIL_PATTERNS:
        cnt += ((dist >= 0) & (dist % dil == 0) & (dist <= window)).astype(np.float32)
    return jnp.asarray(np.where(cnt > 0, np.log(np.maximum(cnt, 1.0)), NEG).astype(np.float32))


def _stack_heads(x, lo):
    zero = jnp.zeros_like(x)
    return jnp.concatenate([jnp.where(lo, x, zero), jnp.where(lo, zero, x)], axis=0)


def _da_fwd(qk, proj, v_col0, bias, batch, s, ride=None, streams=MIX_STREAMS):
    t = qk.shape[0]
    nq = s // DA_T
    n_pairs = 4
    ns = streams
    wide = ns * LANES
    scale = HEAD_DIM ** -0.5

    def body(q_ref, k_ref, v_ref, b_ref, o_ref, lse_ref, acc_ref, m_ref, l_ref):
        i = pl.program_id(2)
        lo = lax.broadcasted_iota(jnp.int32, (DA_T, LANES), 1) < HEAD_DIM
        ones = jnp.ones((DA_T, LANES), BF16)
        acc_ref[...] = jnp.zeros_like(acc_ref)
        m_ref[...] = jnp.full(m_ref.shape, NEG, F32)
        l_ref[...] = jnp.zeros_like(l_ref)
        qqs = [_stack_heads(q_ref[:, st * LANES:(st + 1) * LANES] * scale, lo) for st in range(ns)]

        def scores(st, rows, bias2):
            k = k_ref[rows, st * LANES:(st + 1) * LANES]
            return lax.dot_general(qqs[st], k, NT, preferred_element_type=F32) + bias2

        def softmax(st, sc):
            m_old = m_ref[st]
            m_new = jnp.maximum(m_old, jnp.broadcast_to(jnp.max(sc, axis=1, keepdims=True), m_old.shape))
            m_ref[st] = m_new
            return jnp.exp(sc - jnp.concatenate([m_new, m_new], axis=1)).astype(BF16), jnp.exp(m_old - m_new)

        def values(st, rows, p, alpha):
            v = v_ref[rows, st * LANES:(st + 1) * LANES]
            vz = jnp.zeros_like(v)
            l_ref[st] = alpha * l_ref[st] + lax.dot_general(p, ones, NN, preferred_element_type=F32)
            pv = (lax.dot_general(p[:DA_T], jnp.where(lo, v, vz), NN, preferred_element_type=F32)
                  + lax.dot_general(p[DA_T:], jnp.where(lo, vz, v), NN, preferred_element_type=F32))
            acc_ref[st] = acc_ref[st] * jnp.where(lo, alpha[:DA_T], alpha[DA_T:]) + pv

        def trip(dlt, carry):
            rows = pl.ds(pl.multiple_of((i - dlt) * DA_T, DA_T), DA_T)
            bias_t = b_ref[dlt]
            bias2 = jnp.concatenate([bias_t, bias_t], axis=0)
            scs = [scores(st, rows, bias2) for st in range(ns)]
            pas = [softmax(st, scs[st]) for st in range(ns)]
            for st in range(ns):
                values(st, rows, *pas[st])
            return carry

        lax.fori_loop(0, i + 1, trip, 0)
        for st in range(ns):
            cols = slice(st * LANES, (st + 1) * LANES)
            l_t = l_ref[st]
            o_ref[:, cols] = (acc_ref[st] / jnp.where(lo, l_t[:DA_T], l_t[DA_T:])).astype(BF16)
            lse = m_ref[st] + jnp.log(l_t)
            lse_ref[:, cols] = jnp.where(lo, lse[:DA_T], lse[DA_T:])

    blk = pl.BlockSpec((DA_T, wide), lambda b, h, i: (b * nq + i, h))
    return _call(
        body, name="attn_a_fwd", grid=(batch, n_pairs // ns, nq),
        in_specs=[blk,
                  pl.BlockSpec((s, wide), lambda b, h, i: (b, n_pairs // ns + h)),
                  pl.BlockSpec((s, wide), lambda b, h, i: (b, v_col0 // ns + h)),
                  pl.BlockSpec((nq, DA_T, DA_T), lambda b, h, i: (0, 0, 0))],
        out_specs=[blk, blk],
        out_shape=[jax.ShapeDtypeStruct((t, n_pairs * LANES), BF16), jax.ShapeDtypeStruct((t, n_pairs * LANES), F32)],
        scratch=[pltpu.VMEM((ns, DA_T, LANES), F32), pltpu.VMEM((ns, 2 * DA_T, LANES), F32),
                 pltpu.VMEM((ns, 2 * DA_T, LANES), F32)],
        sem=("parallel", "parallel", "arbitrary"), args=(qk, qk, proj, bias), ride=ride)


def _da_bwd(qk, proj, v_col0, bias, o, lse, do, batch, s, ride=None, streams=MIX_STREAMS):
    t = qk.shape[0]
    nq = s // DA_T
    n_pairs = 4
    ns = streams
    wide = ns * LANES
    scale = HEAD_DIM ** -0.5

    def body(q_ref, k_ref, v_ref, b_ref, o_ref, lse_ref, do_ref, dq_ref, dk_ref, dv_ref, dk_acc, dv_acc, dq_acc):
        i = pl.program_id(2)
        lo = lax.broadcasted_iota(jnp.int32, (DA_T, LANES), 1) < HEAD_DIM

        @pl.when(i == 0)
        def _():
            dk_acc[...] = jnp.zeros_like(dk_acc)
            dv_acc[...] = jnp.zeros_like(dv_acc)

        dq_acc[...] = jnp.zeros_like(dq_acc)
        qqs, dds, deltas, lses = [], [], [], []
        for st in range(ns):
            cols = slice(st * LANES, (st + 1) * LANES)
            do_ = do_ref[:, cols]
            qqs.append(_stack_heads(q_ref[:, cols] * scale, lo))
            dds.append(_stack_heads(do_, lo))
            prod = do_.astype(F32) * o_ref[:, cols].astype(F32)
            fz = jnp.zeros_like(prod)
            deltas.append(jnp.concatenate([jnp.sum(jnp.where(lo, prod, fz), axis=1, keepdims=True),
                                           jnp.sum(jnp.where(lo, fz, prod), axis=1, keepdims=True)], axis=0))
            lse_t = lse_ref[:, cols]
            lses.append(jnp.concatenate([lse_t[:, 0:1], lse_t[:, HEAD_DIM:HEAD_DIM + 1]], axis=0))

        def products(st, rows, bias2):
            cols = slice(st * LANES, (st + 1) * LANES)
            sc = lax.dot_general(qqs[st], k_ref[rows, cols], NT, preferred_element_type=F32) + bias2
            return sc, lax.dot_general(dds[st], v_ref[rows, cols], NT, preferred_element_type=F32)

        def weights(st, sc, dp):
            p = jnp.exp(sc - lses[st])
            return (p * (dp - deltas[st])).astype(BF16), p.astype(BF16)

        def gradients(st, rows, ds, p):
            cols = slice(st * LANES, (st + 1) * LANES)
            k = k_ref[rows, cols]
            kz = jnp.zeros_like(k)
            dq_acc[st] += (lax.dot_general(ds[:DA_T], jnp.where(lo, k, kz), NN, preferred_element_type=F32)
                           + lax.dot_general(ds[DA_T:], jnp.where(lo, kz, k), NN, preferred_element_type=F32))
            dk_acc[rows, cols] += lax.dot_general(ds, qqs[st], TN, preferred_element_type=F32)
            dv_acc[rows, cols] += lax.dot_general(p, dds[st], TN, preferred_element_type=F32)

        def trip(dlt, carry):
            rows = pl.ds(pl.multiple_of((i - dlt) * DA_T, DA_T), DA_T)
            bias_t = b_ref[dlt]
            bias2 = jnp.concatenate([bias_t, bias_t], axis=0)
            prods = [products(st, rows, bias2) for st in range(ns)]
            wts = [weights(st, *prods[st]) for st in range(ns)]
            for st in range(ns):
                gradients(st, rows, *wts[st])
            return carry

        lax.fori_loop(0, i + 1, trip, 0)
        for st in range(ns):
            dq_ref[:, st * LANES:(st + 1) * LANES] = (dq_acc[st] * scale).astype(BF16)

        @pl.when(i == nq - 1)
        def _():
            dk_ref[...] = dk_acc[...].astype(BF16)
            dv_ref[...] = dv_acc[...].astype(BF16)

    blk = pl.BlockSpec((DA_T, wide), lambda b, h, i: (b * nq + i, h))
    seq = pl.BlockSpec((s, wide), lambda b, h, i: (b, h), pipeline_mode=pl.Buffered(1))
    one = pl.Buffered(1)
    out = jax.ShapeDtypeStruct((t, n_pairs * LANES), BF16)
    return _call(
        body, name="attn_a_bwd", grid=(batch, n_pairs // ns, nq),
        in_specs=[blk,
                  pl.BlockSpec((s, wide), lambda b, h, i: (b, n_pairs // ns + h), pipeline_mode=one),
                  pl.BlockSpec((s, wide), lambda b, h, i: (b, v_col0 // ns + h), pipeline_mode=one),
                  pl.BlockSpec((nq, DA_T, DA_T), lambda b, h, i: (0, 0, 0), pipeline_mode=one),
                  blk, blk, blk],
        out_specs=[blk, seq, seq], out_shape=[out, out, out],
        scratch=[pltpu.VMEM((s, wide), F32), pltpu.VMEM((s, wide), F32), pltpu.VMEM((ns, DA_T, LANES), F32)],
        sem=("parallel", "parallel", "arbitrary"), args=(qk, qk, proj, bias, o, lse, do), ride=ride)


SB_Q = 256


def _sb_consts(after):
    r = lax.broadcasted_iota(jnp.int32, (2 * BLOCK, 2 * BLOCK), 0) % BLOCK
    c = lax.broadcasted_iota(jnp.int32, (2 * BLOCK, 2 * BLOCK), 1)
    tri = (r > c) if after else (r < c)
    return jnp.logical_or(c >= BLOCK, tri).astype(BF16)


def _split(x):
    hi = x.astype(BF16)
    lo = (x - hi.astype(F32)).astype(BF16)
    return jnp.concatenate([hi, lo], axis=1)


def _sb_fwd(proj, q_col0, k_col0, v_col0, batch, s, ride=None, streams=MIX_STREAMS):
    t = proj.shape[0]
    nq = s // SB_Q
    n_pairs = 4
    ns = streams
    wide = ns * LANES
    scale = HEAD_DIM ** -0.5

    def body(q_ref, k_ref, v_ref, o_ref, tot_ref, acc_ref, run_ref):
        i = pl.program_id(2)
        lo_q = lax.broadcasted_iota(jnp.int32, (SB_Q, LANES), 1) < HEAD_DIM
        lo_k = _lane_lo()
        mat = _sb_consts(True)
        row = lax.broadcasted_iota(jnp.int32, (2 * SB_Q, LANES), 0) % SB_Q
        ahead = row - lax.broadcasted_iota(jnp.int32, (2 * SB_Q, LANES), 1)
        acc_ref[...] = jnp.zeros_like(acc_ref)
        run_ref[...] = jnp.zeros_like(run_ref)
        qqs = [_stack_heads(q_ref[:, st * LANES:(st + 1) * LANES] * scale, lo_q) for st in range(ns)]

        def units(todo):
            def rows(j):
                return pl.ds(pl.multiple_of(j * BLOCK, BLOCK), BLOCK)

            zs = [lax.dot_general(qqs[st], k_ref[rows(j), st * LANES:(st + 1) * LANES], NT, preferred_element_type=F32)
                  for st, j, _ in todo]
            logs = []
            for z, (_, _, off) in zip(zs, todo):
                lsig = jnp.minimum(z, 0.0) - jnp.log(1.0 + jnp.exp(-jnp.abs(z)))
                lneg = lsig - z
                if off is not None:
                    lneg = jnp.where(ahead > off, lneg, 0.0)
                logs.append((lsig, _split(lneg)))
            sums = [lax.dot_general(cat, mat, NN, preferred_element_type=F32) for _, cat in logs]
            probs = []
            for (lsig, _), sm, (st, _, off) in zip(logs, sums, todo):
                run = run_ref[st]
                a = jnp.exp(lsig + run + sm[:, :BLOCK])
                if off is not None:
                    a = jnp.where(ahead > off, a, 0.0)
                run_ref[st] = run + sm[:, BLOCK:]
                probs.append(a.astype(BF16))
            for ab, (st, j, _) in zip(probs, todo):
                v = v_ref[rows(j), st * LANES:(st + 1) * LANES]
                vz = jnp.zeros_like(v)
                acc_ref[st] += (lax.dot_general(ab[:SB_Q], jnp.where(lo_k, v, vz), NN, preferred_element_type=F32)
                                + lax.dot_general(ab[SB_Q:], jnp.where(lo_k, vz, v), NN, preferred_element_type=F32))

        units([(st, 2 * i + 1, BLOCK) for st in range(ns)] + [(st, 2 * i, 0) for st in range(ns)])

        def pair(p, carry):
            jp = i - 1 - p
            units([(st, 2 * jp + 1, None) for st in range(ns)] + [(st, 2 * jp, None) for st in range(ns)])
            return carry

        lax.fori_loop(0, i, pair, 0)
        for st in range(ns):
            cols = slice(st * LANES, (st + 1) * LANES)
            o_ref[:, cols] = acc_ref[st].astype(BF16)
            tot_ref[:, cols] = jnp.where(lo_q, run_ref[st, 0:SB_Q, :], run_ref[st, SB_Q:2 * SB_Q, :])

    def seq(col0):
        return pl.BlockSpec((s, wide), lambda b, h, i: (b, col0 // ns + h))

    blk = pl.BlockSpec((SB_Q, wide), lambda b, h, i: (b * nq + i, h))
    return _call(
        body, name="attn_b_fwd", grid=(batch, n_pairs // ns, nq),
        in_specs=[pl.BlockSpec((SB_Q, wide), lambda b, h, i: (b * nq + i, q_col0 // ns + h)), seq(k_col0), seq(v_col0)],
        out_specs=[blk, blk],
        out_shape=[jax.ShapeDtypeStruct((t, n_pairs * LANES), BF16), jax.ShapeDtypeStruct((t, n_pairs * LANES), F32)],
        scratch=[pltpu.VMEM((ns, SB_Q, LANES), F32), pltpu.VMEM((ns, 2 * SB_Q, LANES), F32)],
        sem=("parallel", "parallel", "arbitrary"), args=(proj, proj, proj), ride=ride)


def _sb_bwd(proj, q_col0, k_col0, v_col0, tot, do, batch, s, ride=None, streams=SB_BWD_STREAMS):
    t = proj.shape[0]
    nq = s // SB_Q
    n_pairs = 4
    ns = streams
    wide = ns * LANES
    scale = HEAD_DIM ** -0.5

    def body(q_ref, k_ref, v_ref, tot_ref, do_ref, dq_ref, dk_ref, dv_ref, dk_acc, dv_acc, dq_acc, seen_ref, gsum_ref):
        i = pl.program_id(2)
        lo_q = lax.broadcasted_iota(jnp.int32, (SB_Q, LANES), 1) < HEAD_DIM
        lo_k = _lane_lo()

        @pl.when(i == 0)
        def _():
            dk_acc[...] = jnp.zeros_like(dk_acc)
            dv_acc[...] = jnp.zeros_like(dv_acc)

        mat_after = _sb_consts(True)
        mat_before = _sb_consts(False)[:BLOCK]
        row = lax.broadcasted_iota(jnp.int32, (2 * SB_Q, LANES), 0) % SB_Q
        ahead = row - lax.broadcasted_iota(jnp.int32, (2 * SB_Q, LANES), 1)
        dq_acc[...] = jnp.zeros_like(dq_acc)
        seen_ref[...] = jnp.zeros_like(seen_ref)
        gsum_ref[...] = jnp.zeros_like(gsum_ref)
        qqs, dds, totals = [], [], []
        for st in range(ns):
            cols = slice(st * LANES, (st + 1) * LANES)
            qqs.append(_stack_heads(q_ref[:, cols] * scale, lo_q))
            dds.append(_stack_heads(do_ref[:, cols], lo_q))
            tot_t = tot_ref[:, cols]
            totals.append(jnp.concatenate([jnp.broadcast_to(tot_t[:, 0:1], (SB_Q, LANES)),
                                           jnp.broadcast_to(tot_t[:, HEAD_DIM:HEAD_DIM + 1], (SB_Q, LANES))], axis=0))

        def units(todo):
            def rows(j):
                return pl.ds(pl.multiple_of(j * BLOCK, BLOCK), BLOCK)

            def cols(st):
                return slice(st * LANES, (st + 1) * LANES)

            prods = [(lax.dot_general(qqs[st], k_ref[rows(j), cols(st)], NT, preferred_element_type=F32),
                      lax.dot_general(dds[st], v_ref[rows(j), cols(st)], NT, preferred_element_type=F32))
                     for st, j, _ in todo]
            logs = []
            for (z, _), (_, _, off) in zip(prods, todo):
                lsig = jnp.minimum(z, 0.0) - jnp.log(1.0 + jnp.exp(-jnp.abs(z)))
                lneg = lsig - z
                if off is not None:
                    lneg = jnp.where(ahead > off, lneg, 0.0)
                logs.append((lsig, _split(lneg)))
            sums = [lax.dot_general(cat, mat_after, NN, preferred_element_type=F32) for _, cat in logs]
            gates = []
            for (lsig, _), sm, (_, da), (st, _, off) in zip(logs, sums, prods, todo):
                seen = seen_ref[st]
                a = jnp.exp(lsig + (totals[st] - seen - sm[:, BLOCK:]) + sm[:, :BLOCK])
                if off is not None:
                    a = jnp.where(ahead > off, a, 0.0)
                seen_ref[st] = seen + sm[:, BLOCK:]
                g = a * da
                gates.append((a.astype(BF16), g, g.astype(BF16)))
            gsums = [lax.dot_general(cat, mat_before, NN, preferred_element_type=F32) for _, _, cat in gates]
            outs = []
            for (lsig, _), (ab, g, _), gs, (st, _, off) in zip(logs, gates, gsums, todo):
                gsum = gsum_ref[st]
                dz = g - jnp.exp(lsig) * (g + gsum + gs[:, :BLOCK])
                if off is not None:
                    dz = jnp.where(ahead > off, dz, 0.0)
                gsum_ref[st] = gsum + gs[:, BLOCK:]
                outs.append((dz.astype(BF16), ab))
            for (dzb, ab), (st, j, _) in zip(outs, todo):
                k = k_ref[rows(j), cols(st)]
                kz = jnp.zeros_like(k)
                dq_acc[st] += (lax.dot_general(dzb[:SB_Q], jnp.where(lo_k, k, kz), NN, preferred_element_type=F32)
                               + lax.dot_general(dzb[SB_Q:], jnp.where(lo_k, kz, k), NN, preferred_element_type=F32))
                dk_acc[rows(j), cols(st)] += lax.dot_general(dzb, qqs[st], TN, preferred_element_type=F32)
                dv_acc[rows(j), cols(st)] += lax.dot_general(ab, dds[st], TN, preferred_element_type=F32)

        def pair(p, carry):
            units([(st, 2 * p, None) for st in range(ns)] + [(st, 2 * p + 1, None) for st in range(ns)])
            return carry

        lax.fori_loop(0, i, pair, 0)
        units([(st, 2 * i, 0) for st in range(ns)] + [(st, 2 * i + 1, BLOCK) for st in range(ns)])
        for st in range(ns):
            dq_ref[:, st * LANES:(st + 1) * LANES] = (dq_acc[st] * scale).astype(BF16)

        @pl.when(i == nq - 1)
        def _():
            dk_ref[...] = dk_acc[...].astype(BF16)
            dv_ref[...] = dv_acc[...].astype(BF16)

    def seq_in(col0):
        return pl.BlockSpec((s, wide), lambda b, h, i: (b, col0 // ns + h))

    blk = pl.BlockSpec((SB_Q, wide), lambda b, h, i: (b * nq + i, h))
    seq = pl.BlockSpec((s, wide), lambda b, h, i: (b, h))
    out = jax.ShapeDtypeStruct((t, n_pairs * LANES), BF16)
    return _call(
        body, name="attn_b_bwd", grid=(batch, n_pairs // ns, nq),
        in_specs=[pl.BlockSpec((SB_Q, wide), lambda b, h, i: (b * nq + i, q_col0 // ns + h)), seq_in(k_col0),
                  seq_in(v_col0), blk, blk],
        out_specs=[blk, seq, seq], out_shape=[out, out, out],
        scratch=[pltpu.VMEM((s, wide), F32), pltpu.VMEM((s, wide), F32), pltpu.VMEM((ns, SB_Q, LANES), F32),
                 pltpu.VMEM((ns, 2 * SB_Q, LANES), F32), pltpu.VMEM((ns, 2 * SB_Q, LANES), F32)],
        sem=("parallel", "parallel", "arbitrary"), args=(proj, proj, proj, tot, do), ride=ride)


MEM_Q_TILE = 512


def _mem_fwd(q, kv, batch, s, n_mem):
    t, width = q.shape
    tq = min(MEM_Q_TILE, s)
    nq = s // tq
    scale = MEM_HEAD_DIM ** -0.5

    def body(q_ref, kv_ref, o_ref):
        for h in range(N_HEADS_MEM):
            cols = slice(h * MEM_HEAD_DIM, (h + 1) * MEM_HEAD_DIM)
            k = kv_ref[:, cols]
            v = kv_ref[:, width + h * MEM_HEAD_DIM: width + (h + 1) * MEM_HEAD_DIM]
            sc = lax.dot_general(q_ref[:, cols], k, NT, preferred_element_type=F32) * scale
            p = jnp.exp(sc - jnp.max(sc, axis=1, keepdims=True))
            p = p / jnp.sum(p, axis=1, keepdims=True)
            o_ref[:, cols] = lax.dot_general(p.astype(BF16), v, NN, preferred_element_type=F32).astype(BF16)

    return pl.pallas_call(
        body, name="mem_attn_fwd", grid=(batch, nq),
        in_specs=[pl.BlockSpec((tq, width), lambda b, i: (b * nq + i, 0)),
                  pl.BlockSpec((n_mem, 2 * width), lambda b, i: (b, 0))],
        out_specs=pl.BlockSpec((tq, width), lambda b, i: (b * nq + i, 0)),
        out_shape=jax.ShapeDtypeStruct((t, width), BF16),
        compiler_params=_params(("parallel", "parallel")),
    )(q, kv)


def _mem_bwd(q, kv, do, batch, s, n_mem):
    t, width = q.shape
    tq = min(MEM_Q_TILE, s)
    nq = s // tq
    scale = MEM_HEAD_DIM ** -0.5

    def body(q_ref, kv_ref, do_ref, dq_ref, dkv_ref, acc):
        i = pl.program_id(1)

        @pl.when(i == 0)
        def _():
            acc[...] = jnp.zeros_like(acc)

        for h in range(N_HEADS_MEM):
            cols = slice(h * MEM_HEAD_DIM, (h + 1) * MEM_HEAD_DIM)
            vcols = slice(width + h * MEM_HEAD_DIM, width + (h + 1) * MEM_HEAD_DIM)
            qh, k, v, doh = q_ref[:, cols], kv_ref[:, cols], kv_ref[:, vcols], do_ref[:, cols]
            sc = lax.dot_general(qh, k, NT, preferred_element_type=F32) * scale
            p = jnp.exp(sc - jnp.max(sc, axis=1, keepdims=True))
            p = p / jnp.sum(p, axis=1, keepdims=True)
            dp = lax.dot_general(doh, v, NT, preferred_element_type=F32)
            ds = (p * (dp - jnp.sum(p * dp, axis=1, keepdims=True)) * scale).astype(BF16)
            dq_ref[:, cols] = lax.dot_general(ds, k, NN, preferred_element_type=F32).astype(BF16)
            acc[:, cols] += lax.dot_general(ds, qh, TN, preferred_element_type=F32)
            acc[:, vcols] += lax.dot_general(p.astype(BF16), doh, TN, preferred_element_type=F32)

        @pl.when(i == nq - 1)
        def _():
            dkv_ref[...] = acc[...].astype(BF16)

    row = pl.BlockSpec((tq, width), lambda b, i: (b * nq + i, 0))
    kvs = pl.BlockSpec((n_mem, 2 * width), lambda b, i: (b, 0))
    return pl.pallas_call(
        body, name="mem_attn_bwd", grid=(batch, nq),
        in_specs=[row, kvs, row], out_specs=[row, kvs],
        out_shape=[jax.ShapeDtypeStruct((t, width), BF16), jax.ShapeDtypeStruct((batch * n_mem, 2 * width), BF16)],
        scratch_shapes=[pltpu.VMEM((n_mem, 2 * width), F32)],
        compiler_params=_params(("parallel", "arbitrary")),
    )(q, kv, do)


def _mixer_fwd(o_a, o_b, w_a, w_b, proj, gate_col0, w_out, x, g, w_q):
    t, width = o_a.shape
    d = w_a.shape[1]
    nq_cols = w_q.shape[1]
    tm = min(ROW_TILE, t)
    gb0 = gate_col0 * LANES // d

    def body(oa_ref, ob_ref, wa_ref, wb_ref, ga_ref, gb_ref, wo_ref, x_ref, g_ref, wq_ref, ua_ref, ub_ref, mix_ref,
             n_ref, h_ref, q_ref):
        ua = lax.dot_general(oa_ref[...], wa_ref[...], NN, preferred_element_type=F32)
        ub = lax.dot_general(ob_ref[...], wb_ref[...], NN, preferred_element_type=F32)
        ua_ref[...] = ua.astype(BF16)
        ub_ref[...] = ub.astype(BF16)
        mixed = (jax.nn.sigmoid(ga_ref[...].astype(F32)) * ua + jax.nn.sigmoid(gb_ref[...].astype(F32)) * ub).astype(BF16)
        mix_ref[...] = mixed
        h = lax.dot_general(mixed, wo_ref[...], NN, preferred_element_type=F32) + x_ref[...]
        h_ref[...] = h
        r = lax.rsqrt(jnp.mean(h * h, axis=-1, keepdims=True) + RMS_EPS)
        n = (h * r * g_ref[...]).astype(BF16)
        n_ref[...] = n
        q_ref[...] = lax.dot_general(n, wq_ref[...], NN, preferred_element_type=F32).astype(BF16)

    row = pl.BlockSpec((tm, width), lambda i: (i, 0))
    wsp = pl.BlockSpec((width, d), lambda i: (0, 0))
    out = pl.BlockSpec((tm, d), lambda i: (i, 0))
    osh = jax.ShapeDtypeStruct((t, d), BF16)
    return pl.pallas_call(
        body, name="mixer_fwd", grid=(t // tm,),
        in_specs=[row, row, wsp, wsp,
                  pl.BlockSpec((tm, d), lambda i: (i, gb0)), pl.BlockSpec((tm, d), lambda i: (i, gb0 + 1)),
                  pl.BlockSpec((d, d), lambda i: (0, 0)), out, pl.BlockSpec((1, d), lambda i: (0, 0)),
                  pl.BlockSpec((d, nq_cols), lambda i: (0, 0))],
        out_specs=[out, out, out, out, out, pl.BlockSpec((tm, nq_cols), lambda i: (i, 0))],
        out_shape=[osh, osh, osh, osh, jax.ShapeDtypeStruct((t, d), F32), jax.ShapeDtypeStruct((t, nq_cols), BF16)],
        compiler_params=_params(("parallel",)),
    )(o_a, o_b, w_a, w_b, proj, proj, w_out, x, g, w_q)


def _mixer_bwd(dh, w_out, ua, ub, proj, gate_col0, w_a, w_b):
    t, d = dh.shape
    width = w_a.shape[0]
    tm = min(ROW_TILE, t)
    nc = d // LANES

    def body(dh_ref, w_ref, ua_ref, ub_ref, ga_ref, gb_ref, wa_ref, wb_ref, dua_ref, dub_ref, dg_ref, doa_ref, dob_ref):
        dm = lax.dot_general(dh_ref[...], w_ref[...], NT, preferred_element_type=F32)
        sa = jax.nn.sigmoid(ga_ref[...].astype(F32))
        sb = jax.nn.sigmoid(gb_ref[...].astype(F32))
        dua = (dm * sa).astype(BF16)
        dub = (dm * sb).astype(BF16)
        dua_ref[...] = dua
        dub_ref[...] = dub
        dg_ref[:, 0:d] = (dm * ua_ref[...].astype(F32) * sa * (1.0 - sa)).astype(BF16)
        dg_ref[:, d:2 * d] = (dm * ub_ref[...].astype(F32) * sb * (1.0 - sb)).astype(BF16)
        doa_ref[...] = lax.dot_general(dua, wa_ref[...], NT, preferred_element_type=F32).astype(BF16)
        dob_ref[...] = lax.dot_general(dub, wb_ref[...], NT, preferred_element_type=F32).astype(BF16)

    row = pl.BlockSpec((tm, d), lambda i: (i, 0))
    wsp = pl.BlockSpec((width, d), lambda i: (0, 0))
    osp = pl.BlockSpec((tm, width), lambda i: (i, 0))
    return pl.pallas_call(
        body, name="mixer_bwd", grid=(t // tm,),
        in_specs=[row, pl.BlockSpec((d, d), lambda i: (0, 0)), row, row,
                  pl.BlockSpec((tm, d), lambda i: (i, gate_col0 // nc)),
                  pl.BlockSpec((tm, d), lambda i: (i, gate_col0 // nc + 1)), wsp, wsp],
        out_specs=[row, row, pl.BlockSpec((tm, 2 * d), lambda i: (i, 0)), osp, osp],
        out_shape=[jax.ShapeDtypeStruct((t, d), BF16), jax.ShapeDtypeStruct((t, d), BF16),
                   jax.ShapeDtypeStruct((t, 2 * d), BF16), jax.ShapeDtypeStruct((t, width), BF16),
                   jax.ShapeDtypeStruct((t, width), BF16)],
        compiler_params=_params(("parallel",)),
    )(dh, w_out, ua, ub, proj, proj, w_a, w_b)


FFN_COLS = 1024


def _ffn_up(n, w_gate, w_up):
    t, d = n.shape
    hidden = w_gate.shape[0]
    tm = min(ROW_TILE, t)
    tn = min(FFN_COLS, hidden)

    def body(n_ref, wg_ref, wu_ref, hg_ref, hu_ref, act_ref):
        hg = lax.dot_general(n_ref[...], wg_ref[...], NT, preferred_element_type=F32)
        hu = lax.dot_general(n_ref[...], wu_ref[...], NT, preferred_element_type=F32)
        hg_ref[...] = hg.astype(BF16)
        hu_ref[...] = hu.astype(BF16)
        act_ref[...] = (hg * jax.nn.sigmoid(hg) * hu).astype(BF16)

    wsp = pl.BlockSpec((tn, d), lambda j, i: (j, 0))
    out = pl.BlockSpec((tm, tn), lambda j, i: (i, j))
    osh = jax.ShapeDtypeStruct((t, hidden), BF16)
    return pl.pallas_call(
        body, name="ffn_up", grid=(hidden // tn, t // tm),
        in_specs=[pl.BlockSpec((tm, d), lambda j, i: (i, 0)), wsp, wsp],
        out_specs=[out, out, out], out_shape=[osh, osh, osh],
        compiler_params=_params(("parallel", "parallel")),
    )(n, w_gate, w_up)


def _ffn_bwd(dh, w_down, w_gate, w_up, hg, hu, x, g, dres, w_prev):
    t, d = dh.shape
    hidden = w_down.shape[0]
    q = w_prev.shape[0]
    tm = min(ROW_TILE, t)
    tn = min(FFN_COLS, hidden)
    nj = hidden // tn

    def body(dh_ref, wd_ref, wg_ref, wu_ref, hg_ref, hu_ref, x_ref, g_ref, r_ref, wp_ref, dhg_ref, dhu_ref, dx_ref,
             dxb_ref, dg_ref, do_ref, acc):
        j, i = pl.program_id(0), pl.program_id(1)
        dact = lax.dot_general(dh_ref[...], wd_ref[...], NT, preferred_element_type=F32)
        hg = hg_ref[...].astype(F32)
        sg = jax.nn.sigmoid(hg)
        dhu = (dact * hg * sg).astype(BF16)
        dhg = (dact * hu_ref[...].astype(F32) * sg * (1.0 + hg * (1.0 - sg))).astype(BF16)
        dhu_ref[...] = dhu
        dhg_ref[...] = dhg
        part = (lax.dot_general(dhg, wg_ref[...], NN, preferred_element_type=F32)
                + lax.dot_general(dhu, wu_ref[...], NN, preferred_element_type=F32))

        @pl.when(j == 0)
        def _():
            acc[i] = part

        @pl.when(j > 0)
        def _():
            acc[i] += part

        @pl.when(jnp.logical_and(j == 0, i == 0))
        def _():
            dg_ref[...] = jnp.zeros_like(dg_ref)

        @pl.when(j == nj - 1)
        def _():
            dx, dg = _rms_bwd_rows(acc[i], x_ref[...], g_ref[...], r_ref[...])
            dx_ref[...] = dx
            dxb = dx.astype(BF16)
            dxb_ref[...] = dxb
            dg_ref[...] += dg
            do_ref[...] = lax.dot_general(dxb, wp_ref[...], NT, preferred_element_type=F32).astype(BF16)

    hid = pl.BlockSpec((tm, tn), lambda j, i: (i, j))
    wsp = pl.BlockSpec((tn, d), lambda j, i: (j, 0), pipeline_mode=pl.Buffered(1))
    late = pl.BlockSpec((tm, d), lambda j, i: (jnp.where(j == nj - 1, i, 0), 0))
    late_q = pl.BlockSpec((tm, q), lambda j, i: (jnp.where(j == nj - 1, i, 0), 0))
    vec = pl.BlockSpec((1, d), lambda j, i: (0, 0))
    osh = jax.ShapeDtypeStruct((t, hidden), BF16)
    return pl.pallas_call(
        body, name="ffn_bwd", grid=(nj, t // tm),
        in_specs=[pl.BlockSpec((tm, d), lambda j, i: (i, 0)), wsp, wsp, wsp, hid, hid, late, vec, late,
                  pl.BlockSpec((q, d), lambda j, i: (0, 0), pipeline_mode=pl.Buffered(1))],
        out_specs=[hid, hid, late, late, vec, late_q],
        out_shape=[osh, osh, jax.ShapeDtypeStruct((t, d), F32), jax.ShapeDtypeStruct((t, d), BF16),
                   jax.ShapeDtypeStruct((1, d), F32), jax.ShapeDtypeStruct((t, q), BF16)],
        scratch_shapes=[pltpu.VMEM((t // tm, tm, d), F32)],
        compiler_params=_params(("arbitrary", "arbitrary")),
    )(dh, w_down, w_gate, w_up, hg, hu, x, g, dres, w_prev)


MM_ROWS = 1024


def _mm_w(name, a, w, out_dtype, dims=NN):
    t, k = a.shape
    n = w.shape[1] if dims == NN else w.shape[0]
    tm, tn = min(MM_ROWS, t), min(1024, n)
    o_spec = pl.BlockSpec((tm, tn), lambda j, i: (i, j))
    b_spec = pl.BlockSpec((k, tn), lambda j, i: (0, j)) if dims == NN else pl.BlockSpec((tn, k), lambda j, i: (j, 0))
    return _mm(name, a, w, grid=(n // tn, t // tm), a_spec=pl.BlockSpec((tm, k), lambda j, i: (i, 0)), b_spec=b_spec,
               o_shape=(t, n), o_spec=o_spec, dims=dims, out_dtype=out_dtype)


def _mm_res_norm(name, a, w, res, g):
    t, k = a.shape
    d = w.shape[1]
    tm = min(ROW_TILE, t)

    def body(a_ref, w_ref, r_ref, g_ref, h_ref, n_ref):
        h = lax.dot_general(a_ref[...], w_ref[...], NN, preferred_element_type=F32) + r_ref[...]
        h_ref[...] = h
        r = lax.rsqrt(jnp.mean(h * h, axis=-1, keepdims=True) + RMS_EPS)
        n_ref[...] = (h * r * g_ref[...]).astype(BF16)

    row = pl.BlockSpec((tm, d), lambda i: (i, 0))
    return pl.pallas_call(
        body, name=name, grid=(t // tm,),
        in_specs=[pl.BlockSpec((tm, k), lambda i: (i, 0)), pl.BlockSpec((k, d), lambda i: (0, 0)), row,
                  pl.BlockSpec((1, d), lambda i: (0, 0))],
        out_specs=[row, row], out_shape=[jax.ShapeDtypeStruct((t, d), F32), jax.ShapeDtypeStruct((t, d), BF16)],
        compiler_params=_params(("parallel",)),
    )(a, w, res, g)


def _wgrad(name, a, g, tk=1024, tn=1024):
    t, k = a.shape
    n = g.shape[1]
    tm, tk, tn = min(2 * MM_ROWS, t), min(tk, k), min(tn, n)
    return _mm(name, a, g, grid=(k // tk, n // tn, t // tm),
               a_spec=pl.BlockSpec((tm, tk), lambda p, q, r: (r, p)), b_spec=pl.BlockSpec((tm, tn), lambda p, q, r: (r, q)),
               o_shape=(k, n), o_spec=pl.BlockSpec((tk, tn), lambda p, q, r: (p, q)), dims=TN, out_dtype=BF16, nk=t // tm)


def _peers():
    x, y, c = lax.axis_index("x"), lax.axis_index("y"), lax.axis_index("c")
    me = 4 * x + 2 * y + c
    out = []
    for k in range(1, N_DEV):
        kx, ky, kc = (k >> 2) & 1, (k >> 1) & 1, k & 1
        px = 1 - x if kx else x
        py = 1 - y if ky else y
        pc = 1 - c if kc else c
        out.append(((px, py, pc), 4 * px + 2 * py + pc))
    return me, out


def _cast_weights(ws, pad_rows):
    def body(*refs):
        n = len(refs) // 2
        for i_ref, o_ref, pr in zip(refs[:n], refs[n:], pad_rows):
            r, c = i_ref.shape
            o_ref[0:r, :] = i_ref[...].astype(BF16)
            if pr:
                o_ref[r:r + pr, :] = jnp.zeros((pr, c), BF16)

    return pl.pallas_call(
        body, name="cast_weights", in_specs=[VMEM] * len(ws), out_specs=[VMEM] * len(ws),
        out_shape=[jax.ShapeDtypeStruct((w.shape[0] + pr, w.shape[1]), BF16) for w, pr in zip(ws, pad_rows)],
    )(*ws)


def _window(ref, j, c):
    return ref.at[:, pl.ds(pl.multiple_of(j * c, LANES), c)]


def _scatter_copies(ins, outs, sems, cols, landed):
    send_sems, recv_sems, loc_sems = sems
    n_peer = N_DEV - 1
    me, peers = _peers()

    def src(w, j):
        return _window(ins[w], j, cols[w]) if cols[w] else ins[w].at[j]

    local = [pltpu.make_async_copy(src(w, me), outs[w].at[me], loc_sems.at[w]) for w in range(len(ins))]
    remote = [pltpu.make_async_remote_copy(
        src_ref=src(w, idx), dst_ref=outs[w].at[idx if landed else me],
        send_sem=send_sems.at[w * n_peer + k], recv_sem=recv_sems.at[w * n_peer + k],
        device_id=dev, device_id_type=pl.DeviceIdType.MESH)
        for k, (dev, idx) in reversed(list(enumerate(peers))) for w in range(len(ins))]
    return local, remote


OTHER_CHIPS = (2, 4, 6)


def _gather_copies(ins, outs, sems, cols):
    send_sems, recv_sems, loc_sems = sems
    x, y, c = lax.axis_index("x"), lax.axis_index("y"), lax.axis_index("c")
    me = 4 * x + 2 * y + c
    n_pair = N_DEV - 1

    def dev(mask):
        return (1 - x if mask & 4 else x, 1 - y if mask & 2 else y, 1 - c if mask & 1 else c)

    def slot(w, mask):
        j = jnp.bitwise_xor(me, mask)
        return _window(outs[w], j, cols[w]) if cols[w] else outs[w].at[j]

    def remote(w, pair, src, to_slot, target):
        return pltpu.make_async_remote_copy(src_ref=src, dst_ref=slot(w, to_slot), send_sem=send_sems.at[w * n_pair + pair],
                                            recv_sem=recv_sems.at[w * n_pair + pair], device_id=dev(target),
                                            device_id_type=pl.DeviceIdType.MESH)

    ws = range(len(ins))
    return dict(
        local=[pltpu.make_async_copy(ins[w], slot(w, 0), loc_sems.at[w]) for w in ws],
        to_chips=[remote(w, 1 + t, ins[w], 0, m) for t, m in enumerate(OTHER_CHIPS) for w in ws],
        to_core=[remote(w, 0, ins[w], 0, 1) for w in ws],
        from_chips=[remote(w, 1 + t, ins[w], m, 0) for t, m in enumerate(OTHER_CHIPS) for w in ws],
        pass_on=[remote(w, 4 + t, slot(w, m), m, 1) for t, m in enumerate(OTHER_CHIPS) for w in ws],
        from_core=[remote(w, 0, ins[w], 1, 0) for w in ws]
        + [remote(w, 4 + t, ins[w], m + 1, 0) for t, m in enumerate(OTHER_CHIPS) for w in ws])


def _exchange_start(ins, outs, sems, gather, cols):
    if gather:
        cps = _gather_copies(ins, outs, sems, cols)
        for cp in cps["local"] + cps["to_chips"] + cps["to_core"]:
            cp.start()
    else:
        local, remote = _scatter_copies(ins, outs, sems, cols, False)
        for cp in local + remote:
            cp.start()


def _exchange_pass_on(ins, outs, sems, gather, cols, chips):
    if gather:
        cps = _gather_copies(ins, outs, sems, cols)
        n = len(ins)
        for t in chips:
            for arrived, onward in zip(cps["from_chips"][t * n:(t + 1) * n], cps["pass_on"][t * n:(t + 1) * n]):
                arrived.wait_recv()
                onward.start()


def _exchange_wait(ins, outs, sems, gather, cols):
    if gather:
        cps = _gather_copies(ins, outs, sems, cols)
        for cp in cps["local"]:
            cp.wait()
        for cp in cps["to_chips"] + cps["to_core"] + cps["pass_on"]:
            cp.wait_send()
        for cp in cps["from_core"]:
            cp.wait_recv()
    else:
        local, remote = _scatter_copies(ins, outs, sems, cols, True)
        for cp in local:
            cp.wait()
        for cp in remote:
            cp.wait_send()
            cp.wait_recv()


def _exchange_shapes(arrs, gather, cols):
    n = len(arrs)
    out_shape = []
    for a, c in zip(arrs, cols):
        if gather:
            shape = (a.shape[0], N_DEV * c) if c else (N_DEV,) + a.shape
        else:
            shape = (N_DEV, a.shape[0], c) if c else a.shape
        out_shape.append(jax.ShapeDtypeStruct(shape, a.dtype))
    sems = [pltpu.SemaphoreType.DMA((n * (N_DEV - 1),)), pltpu.SemaphoreType.DMA((n * (N_DEV - 1),)),
            pltpu.SemaphoreType.DMA((n,))]
    return out_shape, sems


def _call(body, *, name, grid, in_specs, out_specs, out_shape, scratch, sem, args, ride=None):
    if ride is None:
        outs = pl.pallas_call(body, name=name, grid=grid, in_specs=in_specs, out_specs=out_specs, out_shape=out_shape,
                              scratch_shapes=scratch, compiler_params=_params(sem))(*args)
        return outs, None
    arrs, gather, cols = ride
    n, n_in, n_out, n_scr = len(arrs), len(in_specs), len(out_specs), len(scratch)
    x_shape, x_sems = _exchange_shapes(arrs, gather, cols)

    def riding(*refs):
        ins, x_ins = refs[:n_in], refs[n_in:n_in + n]
        outs = refs[n_in + n:n_in + n + n_out]
        x_outs = refs[n_in + n + n_out:n_in + 2 * n + n_out]
        scr = refs[n_in + 2 * n + n_out:n_in + 2 * n + n_out + n_scr]
        sems = refs[n_in + 2 * n + n_out + n_scr:]
        def at(step):
            return functools.reduce(jnp.logical_and, [pl.program_id(a) == v for a, v in enumerate(step)])

        @pl.when(at((0,) * len(grid)))
        def _():
            _exchange_start(x_ins, x_outs, sems, gather, cols)

        @pl.when(at((grid[0] // 2,) + (0,) * (len(grid) - 2) + (grid[-1] // 2,)))
        def _():
            _exchange_pass_on(x_ins, x_outs, sems, gather, cols, (0, 1))

        @pl.when(at((grid[0] // 2,) + (0,) * (len(grid) - 2) + (3 * grid[-1] // 4,)))
        def _():
            _exchange_pass_on(x_ins, x_outs, sems, gather, cols, (2,))

        body(*ins, *outs, *scr)

        @pl.when(at(tuple(g - 1 for g in grid)))
        def _():
            _exchange_wait(x_ins, x_outs, sems, gather, cols)

    res = pl.pallas_call(
        riding, name=name, grid=grid, in_specs=list(in_specs) + [ANY] * n, out_specs=list(out_specs) + [ANY] * n,
        out_shape=list(out_shape) + x_shape, scratch_shapes=list(scratch) + x_sems,
        compiler_params=_params(("arbitrary",) * len(grid)))(*args, *arrs)
    return res[:n_out], res[n_out:]


def _my_block():
    return (4 * lax.axis_index("x") + 2 * lax.axis_index("y") + lax.axis_index("c")).astype(jnp.int32).reshape(1)


def _proj_in_gather(x, g, w_shard):
    t, k = x.shape
    cs = w_shard.shape[1]
    tm = min(MM_ROWS, t)
    ni = t // tm
    arrival = (0, 1, 2, 4, 3, 5, 6, 7)

    def mask_at(s):
        return jnp.where(s == 3, 4, jnp.where(s == 4, 3, s))

    def body(me_ref, x_ref, g_ref, w_hbm, o_ref, all_hbm, n_hbm, w_vmem, n_vmem, send_sems, recv_sems, loc_sems,
             load_sems, n_sem):
        s, i = pl.program_id(0), pl.program_id(1)
        cps = _gather_copies([w_hbm], [all_hbm], (send_sems, recv_sems, loc_sems), (cs,))
        by_mask = {0: cps["local"][0], 1: cps["from_core"][0]}
        for t_chip, m in enumerate(OTHER_CHIPS):
            by_mask[m] = cps["from_chips"][t_chip]
            by_mask[m + 1] = cps["from_core"][1 + t_chip]
        arrived = [by_mask[m] for m in arrival]

        def load(step):
            src = w_hbm if step == 0 else _window(all_hbm, jnp.bitwise_xor(me_ref[0], arrival[step]), cs)
            return pltpu.make_async_copy(src, w_vmem.at[step % 2], load_sems.at[step % 2])

        @pl.when(jnp.logical_and(s == 0, i == 0))
        def _():
            for cp in cps["local"] + cps["to_chips"] + cps["to_core"]:
                cp.start()
            load(0).start()

        for step, mask in enumerate(arrival):
            @pl.when(jnp.logical_and(s == step, i == 0))
            def _(step=step):
                load(step).wait()

            if step + 1 < N_DEV:
                @pl.when(jnp.logical_and(s == step, i == min(1, ni - 1)))
                def _(step=step):
                    arrived[step + 1].wait_recv()
                    if arrival[step + 1] in OTHER_CHIPS:
                        cps["pass_on"][OTHER_CHIPS.index(arrival[step + 1])].start()
                    load(step + 1).start()

        @pl.when(s == 0)
        def _():
            xf = x_ref[...]
            r = lax.rsqrt(jnp.mean(xf * xf, axis=-1, keepdims=True) + RMS_EPS)
            n_vmem[i] = (xf * r * g_ref[...]).astype(BF16)
            keep = pltpu.make_async_copy(n_vmem.at[i], n_hbm.at[pl.ds(pl.multiple_of(i * tm, tm), tm), :], n_sem)
            keep.start()
            keep.wait()

        o_ref[...] = lax.dot_general(n_vmem[i], w_vmem[s % 2], NN, preferred_element_type=F32).astype(BF16)

        @pl.when(jnp.logical_and(s == N_DEV - 1, i == ni - 1))
        def _():
            cps["local"][0].wait()
            for cp in cps["to_chips"] + cps["to_core"] + cps["pass_on"]:
                cp.wait_send()

    return pl.pallas_call(
        body, name="proj_in",
        grid_spec=pltpu.PrefetchScalarGridSpec(
            num_scalar_prefetch=1, grid=(N_DEV, ni),
            in_specs=[pl.BlockSpec((tm, k), lambda s, i, me: (jnp.where(s == 0, i, 0), 0)),
                      pl.BlockSpec((1, k), lambda s, i, me: (0, 0)), ANY],
            out_specs=[pl.BlockSpec((tm, cs), lambda s, i, me: (i, jnp.bitwise_xor(me[0], mask_at(s)))), ANY, ANY],
            scratch_shapes=[pltpu.VMEM((2, k, cs), BF16), pltpu.VMEM((ni, tm, k), BF16),
                            pltpu.SemaphoreType.DMA((N_DEV - 1,)), pltpu.SemaphoreType.DMA((N_DEV - 1,)),
                            pltpu.SemaphoreType.DMA((1,)), pltpu.SemaphoreType.DMA((2,)), pltpu.SemaphoreType.DMA]),
        out_shape=[jax.ShapeDtypeStruct((t, N_DEV * cs), BF16), jax.ShapeDtypeStruct((k, N_DEV * cs), BF16),
                   jax.ShapeDtypeStruct((t, k), BF16)],
        compiler_params=_params(("arbitrary", "arbitrary")),
    )(_my_block(), x, g, w_shard)


def _gw_in_scatter(a, g):
    t, k = a.shape
    cs = g.shape[1] // N_DEV
    tm = min(MM_ROWS, t)
    nr = t // tm
    n_chip = N_DEV // 2
    chips = (6, 4, 2, 0)

    def body(me_ref, a_ref, g_ref, out_hbm, acc, stage, other, core_send, core_recv, chip_send, chip_recv, loc_sem):
        s, r = pl.program_id(0), pl.program_id(1)
        x, y, c = lax.axis_index("x"), lax.axis_index("y"), lax.axis_index("c")
        my_chip = 2 * x + y
        part = lax.dot_general(a_ref[...], g_ref[...], TN, preferred_element_type=F32)

        def to_core(m):
            return pltpu.make_async_remote_copy(src_ref=stage.at[0], dst_ref=other.at[m], send_sem=core_send.at[m],
                                                recv_sem=core_recv.at[m], device_id=(x, y, 1 - c),
                                                device_id_type=pl.DeviceIdType.MESH)

        def to_chip(m, landed):
            mask = chips[m]
            there = (1 - x if mask & 4 else x, 1 - y if mask & 2 else y, c)
            slot = (2 * there[0] + there[1]) if landed else my_chip
            return pltpu.make_async_remote_copy(src_ref=stage.at[1], dst_ref=out_hbm.at[slot], send_sem=chip_send.at[m],
                                                recv_sem=chip_recv.at[m], device_id=there,
                                                device_id_type=pl.DeviceIdType.MESH)

        local = pltpu.make_async_copy(stage.at[1], out_hbm.at[my_chip], loc_sem)

        @pl.when(r == 0)
        def _():
            acc[...] = part

        @pl.when(r > 0)
        def _():
            acc[...] += part

        for step in range(N_DEV):
            m = step // 2

            @pl.when(jnp.logical_and(s == step, r == nr - 1))
            def _(step=step, m=m):
                if step % 2 == 0:
                    if m > 0:
                        to_core(m - 1).wait_send()
                    stage[0] = acc[...].astype(BF16)
                    to_core(m).start()
                else:
                    if m > 0:
                        to_chip(m - 1, False).wait_send()
                    to_core(m).wait_recv()
                    stage[1] = (acc[...] + other[m].astype(F32)).astype(BF16)
                    if m < n_chip - 1:
                        to_chip(m, False).start()
                    else:
                        local.start()
                        to_core(m).wait_send()
                        local.wait()
                        for mm in range(n_chip - 1):
                            to_chip(mm, True).wait_recv()

    return pl.pallas_call(
        body, name="gw_in",
        grid_spec=pltpu.PrefetchScalarGridSpec(
            num_scalar_prefetch=1, grid=(N_DEV, nr),
            in_specs=[pl.BlockSpec((tm, k), lambda s, r, me: (r, 0)),
                      pl.BlockSpec((tm, cs), lambda s, r, me: (r, jnp.bitwise_xor(me[0], N_DEV - 1 - s)))],
            out_specs=ANY,
            scratch_shapes=[pltpu.VMEM((k, cs), F32), pltpu.VMEM((2, k, cs), BF16), pltpu.VMEM((n_chip, k, cs), BF16),
                            pltpu.SemaphoreType.DMA((n_chip,)), pltpu.SemaphoreType.DMA((n_chip,)),
                            pltpu.SemaphoreType.DMA((n_chip - 1,)), pltpu.SemaphoreType.DMA((n_chip - 1,)),
                            pltpu.SemaphoreType.DMA]),
        out_shape=jax.ShapeDtypeStruct((n_chip, k, cs), BF16),
        compiler_params=_params(("arbitrary", "arbitrary")),
    )(_my_block(), a, g)


SMALL_ROWS = 8


def _allreduce_small(parts, loss_part):
    n, d = len(parts), parts[0].shape[1]

    def body(*refs):
        part_refs, loss_ref, o_ref = refs[:n], refs[n], refs[n + 1]
        mine_ref, all_ref, send_sems, recv_sems = refs[n + 2:]
        me, peers = _peers()
        mine_ref[...] = jnp.zeros_like(mine_ref)
        for i, p_ref in enumerate(part_refs):
            mine_ref[i:i + 1, :] = p_ref[...]
        mine_ref[SMALL_ROWS - 1:SMALL_ROWS, 0:LANES] = loss_ref[0:1, :]
        all_ref[me] = mine_ref[...]
        for k, (dev, idx) in enumerate(peers):
            pltpu.make_async_remote_copy(src_ref=mine_ref, dst_ref=all_ref.at[me], send_sem=send_sems.at[k],
                                         recv_sem=recv_sems.at[k], device_id=dev,
                                         device_id_type=pl.DeviceIdType.MESH).start()
        for k, (dev, idx) in enumerate(peers):
            cp = pltpu.make_async_remote_copy(src_ref=mine_ref, dst_ref=all_ref.at[idx], send_sem=send_sems.at[k],
                                              recv_sem=recv_sems.at[k], device_id=dev,
                                              device_id_type=pl.DeviceIdType.MESH)
            cp.wait_send()
            cp.wait_recv()
        tot = all_ref[0]
        for dvc in range(1, N_DEV):
            tot = tot + all_ref[dvc]
        o_ref[...] = tot

    return pl.pallas_call(
        body, name="allreduce_small", in_specs=[VMEM] * (n + 1), out_specs=VMEM,
        out_shape=jax.ShapeDtypeStruct((SMALL_ROWS, d), F32),
        scratch_shapes=[pltpu.VMEM((SMALL_ROWS, d), F32), pltpu.VMEM((N_DEV, SMALL_ROWS, d), F32),
                        pltpu.SemaphoreType.DMA((N_DEV - 1,)), pltpu.SemaphoreType.DMA((N_DEV - 1,))],
    )(*parts, loss_part)


def _adam_math(g, w, m, v):
    m_new = ADAM_B1 * m + (1.0 - ADAM_B1) * g
    v_new = ADAM_B2 * v + (1.0 - ADAM_B2) * (g * g)
    m_hat = m_new / (1.0 - ADAM_B1 ** ADAM_STEP)
    v_hat = v_new / (1.0 - ADAM_B2 ** ADAM_STEP)
    delta = -ADAM_LR * (m_hat / (jnp.sqrt(v_hat) + ADAM_EPS) + ADAM_WD * w)
    return delta, m_new, v_new


def _adam(name, pieces, w, m, v):
    r, c = w.shape
    n_piece, _, cp = pieces.shape
    tr = r
    for cand in (256, 176, 128, 64):
        if r % cand == 0 and r > cand:
            tr = cand
            break

    def body(p_ref, w_ref, m_ref, v_ref, g_ref, d_ref, mo_ref, vo_ref):
        g = p_ref[0, :, 0:c].astype(F32)
        for j in range(1, n_piece):
            g = g + p_ref[j, :, 0:c].astype(F32)
        delta, m_new, v_new = _adam_math(g, w_ref[...], m_ref[...], v_ref[...])
        g_ref[...] = g
        d_ref[...] = delta
        mo_ref[...] = m_new
        vo_ref[...] = v_new

    blk = pl.BlockSpec((tr, c), lambda i: (i, 0))
    osh = jax.ShapeDtypeStruct((r, c), F32)
    return pl.pallas_call(
        body, name=name, grid=(r // tr,),
        in_specs=[pl.BlockSpec((n_piece, tr, cp), lambda i: (0, i, 0)), blk, blk, blk],
        out_specs=[blk, blk, blk, blk], out_shape=[osh, osh, osh, osh],
        compiler_params=_params(("parallel",)),
    )(pieces, w, m, v)


def _adam_small(g_all, ws, ms, vs):
    n = len(ws)

    def body(*refs):
        g_ref, ins, outs = refs[0], refs[1:1 + 3 * n], refs[1 + 3 * n:]
        for i in range(n):
            g = g_ref[i:i + 1, :]
            delta, m_new, v_new = _adam_math(g, ins[i][...], ins[n + i][...], ins[2 * n + i][...])
            for kind, val in enumerate((g, delta, m_new, v_new)):
                outs[kind * n + i][...] = val

    osh = jax.ShapeDtypeStruct(ws[0].shape, F32)
    res = pl.pallas_call(body, name="adam_small", in_specs=[VMEM] * (1 + 3 * n), out_specs=[VMEM] * (4 * n),
                         out_shape=[osh] * (4 * n))(g_all, *ws, *ms, *vs)
    return res[:n], res[n:2 * n], res[2 * n:3 * n], res[3 * n:]


def _local_step(x, mem, pos, tgt, gains, w_in_shard, shards, batch):
    g_mix, g_mem_q, g_mem_kv, g_ffn, g_final = gains
    t, d = x.shape
    s = t // batch
    n_mem = mem.shape[0] // batch
    n_sh = N_DEV
    width = shards[0].shape[0]
    nb = width // LANES

    lane = np.arange(LANES) % HEAD_DIM
    sel_lo = (lane < ROPE_HALF).astype(np.float32)[None, :]
    sel_hi = ((lane >= ROPE_HALF) & (lane < 2 * ROPE_HALF)).astype(np.float32)[None, :]
    freqs = np.float32(ROPE_THETA) ** (-np.arange(ROPE_HALF, dtype=np.float32) / np.float32(ROPE_HALF))
    inv_freq = np.where(lane < 2 * ROPE_HALF, freqs[lane % ROPE_HALF], 0.0).astype(np.float32)[None, :]
    cos_t, sin_a, sin_b = _rope_tables(pos, jnp.asarray(inv_freq), jnp.asarray(sel_lo), jnp.asarray(sel_hi))
    bias = _dilated_bias_tiles(s)

    proj, w_in, n1 = _proj_in_gather(x, g_mix, w_in_shard)
    qk_a = _rope_apply("rope_fwd", [proj], 2 * width, cos_t, sin_a, sin_b, 1.0)
    cs_up = shards[0].shape[1]
    (o_a, lse_a), (w_up_a, w_up_b, w_out, w_q, w_kv, w_o, w_fd) = _da_fwd(
        qk_a, proj, 2 * nb, bias, batch, s,
        ride=(shards[:6] + shards[8:], True, (cs_up, cs_up, 0, 0, 0, cs_up, 0)))
    (o_b, tot_b), (w_fg, w_fu) = _sb_fwd(proj, 3 * nb, 4 * nb, 5 * nb, batch, s, ride=(shards[6:8], True, (0, 0)))
    w_out = w_out.reshape(d, d)
    w_q = w_q.reshape(d, -1)
    w_kv = w_kv.reshape(d, -1)
    w_fd = w_fd.reshape(-1, d)
    w_fg = w_fg.reshape(-1, d)
    w_fu = w_fu.reshape(-1, d)
    ua, ub, mixed, n2, h1, q_m = _mixer_fwd(o_a, o_b, w_up_a, w_up_b, proj, 6 * nb, w_out, x, g_mem_q, w_q)
    mem_n = _rms_fwd("norm_mem_kv", mem, g_mem_kv)
    kv_m = _mm_w("mem_kv", mem_n, w_kv, BF16)
    o_m = _mem_fwd(q_m, kv_m, batch, s, n_mem)
    h2, n3 = _mm_res_norm("mem_out", o_m, w_o, h1, g_ffn)
    hg, hu, act = _ffn_up(n3, w_fg, w_fu)
    loss_part, dh3, dh3_b, dg_final = _loss_head(act, w_fd, h2, tgt, g_final.reshape(1, d))

    dhg, dhu, dh2, dh2_b, dg_ffn, do_m = _ffn_bwd(dh3_b, w_fd, w_fg, w_fu, hg, hu, h2, g_ffn, dh3, w_o)
    gw_fd = _wgrad("gw_ffn_down", act, dh3_b)
    gw_fg = _wgrad("gw_ffn_gate", dhg, n3)
    gw_fu = _wgrad("gw_ffn_up", dhu, n3)

    gw_o = _wgrad("gw_mem_o", o_m, dh2_b)
    dq_m, dkv_m = _mem_bwd(q_m, kv_m, do_m, batch, s, n_mem)
    gw_q = _wgrad("gw_mem_q", n2, dq_m)
    gw_kv = _wgrad("gw_mem_kv", mem_n, dkv_m)
    (dg_mem_kv,) = _rms_bwd("norm_mem_kv_bwd", (dkv_m, w_kv, NT), mem, g_mem_kv, None, ())
    dh1, dh1_b, dg_mem_q = _rms_bwd("norm_mem_q_bwd", (dq_m, w_q, NT), h1, g_mem_q, dh2, ("f32", "bf16"))

    gw_out = _wgrad("gw_out", mixed, dh1_b)
    dua, dub, dgates, do_a, do_b = _mixer_bwd(dh1_b, w_out, ua, ub, proj, 6 * nb, w_up_a, w_up_b)
    gw_ua = _wgrad("gw_up_a", o_a, dua)
    gw_ub = _wgrad("gw_up_b", o_b, dub)
    (dq_ar, dk_ar, dv_a), (p_fg, p_fd) = _da_bwd(
        qk_a, proj, 2 * nb, bias, o_a, lse_a, do_a, batch, s,
        ride=([gw_fg.reshape(n_sh, -1, d), gw_fd.reshape(n_sh, -1, d)], False, (0, 0)))
    mid = [gw_ua, gw_ub, gw_out.reshape(n_sh, -1, d), gw_q.reshape(n_sh, -1, gw_q.shape[1]),
           gw_kv.reshape(n_sh, -1, gw_kv.shape[1]), gw_o, gw_fu.reshape(n_sh, -1, d)]
    (dq_b, dk_b, dv_b), (*p_mid, p_fu) = _sb_bwd(proj, 3 * nb, 4 * nb, 5 * nb, tot_b, do_b, batch, s,
                                                 ride=(mid, False, (cs_up, cs_up, 0, 0, 0, cs_up, 0)))
    p_ffn = [p_fg, p_fu, p_fd]
    dproj = _rope_apply("rope_bwd", [dq_ar, dk_ar], width, cos_t, sin_a, sin_b, -1.0,
                        tail=(dv_a, dq_b, dk_b, dv_b, dgates))
    grad_x, dg_mix = _rms_bwd("proj_in_bwd", (dproj, w_in, NT), x, g_mix, dh1, ("f32",))
    p_in = _gw_in_scatter(n1, dproj)
    return loss_part, grad_x, [p_in] + list(p_mid) + p_ffn, (dg_mix, dg_mem_q, dg_mem_kv, dg_ffn, dg_final)


WEIGHTS =("w_in", "w_up_a", "w_up_b", "w_out", "w_q_mem", "w_kv_mem", "w_o_mem", "w_ffn_gate", "w_ffn_up", "w_ffn_down")
GAINS = ("g_mix", "g_mem_q", "g_mem_kv", "g_ffn", "g_final")
ORDER = ("g_mix", "w_in", "w_up_a", "w_up_b", "w_out", "g_mem_q", "g_mem_kv", "w_q_mem", "w_kv_mem", "w_o_mem", "g_ffn",
         "w_ffn_gate", "w_ffn_up", "w_ffn_down", "g_final")


def kernel(x, mem, positions, g_mix, w_in, w_up_a, w_up_b, w_out, g_mem_q, g_mem_kv, w_q_mem, w_kv_mem, w_o_mem, g_ffn, w_ffn_gate, w_ffn_up, w_ffn_down, g_final, loss_target, m_g_mix, m_w_in, m_w_up_a, m_w_up_b, m_w_out, m_g_mem_q, m_g_mem_kv, m_w_q_mem, m_w_kv_mem, m_w_o_mem, m_g_ffn, m_w_ffn_gate, m_w_ffn_up, m_w_ffn_down, m_g_final, v_g_mix, v_w_in, v_w_up_a, v_w_up_b, v_w_out, v_g_mem_q, v_g_mem_kv, v_w_q_mem, v_w_kv_mem, v_w_o_mem, v_g_ffn, v_w_ffn_gate, v_w_ffn_up, v_w_ffn_down, v_g_final):
    given = dict(locals())
    batch, s, d = x.shape
    t = batch * s
    flipped = ("w_ffn_gate", "w_ffn_up")

    def view(a, n):
        a = a.reshape(a.shape[-2:])
        return a.T if n in flipped else a

    def unview(a, n):
        return (a.T if n in flipped else a).reshape(given[n].shape)

    shard = {n: view(given[n], n) for n in WEIGHTS}
    gains = [given[n].reshape(1, d) for n in GAINS]

    pad = (-shard["w_ffn_down"].shape[0]) % LANES
    cast = _cast_weights([shard[n] for n in WEIGHTS], [pad if n in flipped + ("w_ffn_down",) else 0 for n in WEIGHTS])
    loss_part, grad_x, pieces, dgains = _local_step(
        x.reshape(t, d), mem.reshape(-1, d), positions.reshape(t, 1), loss_target.reshape(t, d), gains, cast[0],
        cast[1:], batch)

    grad, delta, new_m, new_v = {}, {}, {}, {}
    for n, p in zip(WEIGHTS, pieces):
        outs = _adam("adam_" + n, p, shard[n], view(given["m_" + n], n), view(given["v_" + n], n))
        grad[n], delta[n], new_m[n], new_v[n] = [unview(o, n) for o in outs]

    g_all = _allreduce_small(list(dgains), loss_part)
    small = _adam_small(g_all, gains, [given["m_" + n].reshape(1, d) for n in GAINS],
                        [given["v_" + n].reshape(1, d) for n in GAINS])
    for out, vals in zip((grad, delta, new_m, new_v), small):
        for n, val in zip(GAINS, vals):
            out[n] = val.reshape(given[n].shape)

    loss = g_all[SMALL_ROWS - 1, 0]
    return (loss, grad_x.reshape(x.shape), *[grad[n] for n in ORDER], *[delta[n] for n in ORDER],
            *[new_m[n] for n in ORDER], *[new_v[n] for n in ORDER])
```

```python
import functools
import math

import jax
import jax.numpy as jnp
import numpy as np
from jax import lax
from jax.experimental import pallas as pl
from jax.experimental.pallas import tpu as pltpu

F32 = jnp.float32
BF16 = jnp.bfloat16

N_DEV = 8
HEAD_DIM = 64
MEM_HEAD_DIM = 128
N_HEADS_MEM = 4
BLOCK = 128
DIL_PATTERNS = ((128, 1), (512, 4), (2048, 16))
ROPE_THETA = 500000.0
ROPE_HALF = 8
RMS_EPS = 1e-6
ADAM_LR, ADAM_B1, ADAM_B2, ADAM_EPS, ADAM_WD, ADAM_STEP = 0.001, 0.9, 0.999, 1e-08, 0.01, 10
NEG = -1e30
ROW_TILE = 512
LANES = 128

ANY = pl.BlockSpec(memory_space=pl.ANY)
VMEM = pl.BlockSpec(memory_space=pltpu.VMEM)
NN = (((1,), (0,)), ((), ()))
NT = (((1,), (1,)), ((), ()))
TN = (((0,), (0,)), ((), ()))


def _params(sem):
    return pltpu.CompilerParams(dimension_semantics=sem)


def _mm(name, a, b, *, grid, a_spec, b_spec, o_shape, o_spec, dims, out_dtype, nk=1):
    def body(*refs):
        a_ref, b_ref, o_ref = refs[0], refs[1], refs[2]
        p = lax.dot_general(a_ref[...], b_ref[...], dims, preferred_element_type=F32)
        if nk == 1:
            o_ref[...] = p.astype(out_dtype)
            return
        acc_ref = refs[-1]
        k = pl.program_id(len(grid) - 1)

        @pl.when(k == 0)
        def _():
            acc_ref[...] = p

        @pl.when(k > 0)
        def _():
            acc_ref[...] += p

        @pl.when(k == nk - 1)
        def _():
            o_ref[...] = acc_ref[...].astype(out_dtype)

    o_block = tuple(d for d in o_spec.block_shape if d is not None)
    sem = ("parallel",) * (len(grid) - 1) + (("arbitrary",) if nk > 1 else ("parallel",))
    return pl.pallas_call(
        body, name=name, grid=grid, in_specs=[a_spec, b_spec],
        out_specs=o_spec, out_shape=jax.ShapeDtypeStruct(o_shape, out_dtype),
        scratch_shapes=[pltpu.VMEM(o_block, F32)] if nk > 1 else [],
        compiler_params=_params(sem),
    )(a, b)


def _rms_fwd(name, x, g):
    t, d = x.shape
    tm = min(ROW_TILE, t)

    def body(x_ref, g_ref, o_ref):
        xf = x_ref[...]
        r = lax.rsqrt(jnp.mean(xf * xf, axis=-1, keepdims=True) + RMS_EPS)
        o_ref[...] = (xf * r * g_ref[...]).astype(BF16)

    return pl.pallas_call(
        body, name=name, grid=(t // tm,),
        in_specs=[pl.BlockSpec((tm, d), lambda i: (i, 0)), pl.BlockSpec((1, d), lambda i: (0, 0))],
        out_specs=pl.BlockSpec((tm, d), lambda i: (i, 0)), out_shape=jax.ShapeDtypeStruct((t, d), BF16),
        compiler_params=_params(("parallel",)),
    )(x, g)


def _rms_bwd_rows(dnf, xf, gv, res):
    r = lax.rsqrt(jnp.mean(xf * xf, axis=-1, keepdims=True) + RMS_EPS)
    xh = xf * r
    dxh = dnf * gv
    dx = r * (dxh - xh * jnp.mean(dxh * xh, axis=-1, keepdims=True))
    if res is not None:
        dx = dx + res
    return dx, jnp.sum(dnf * xh, axis=0, keepdims=True)


def _rms_bwd(name, dn, x, g, dres, want):
    t, d = x.shape
    tm = min(ROW_TILE, t)
    has_res = dres is not None
    lhs = list(dn) if isinstance(dn, tuple) else [dn]
    n_lhs = len(lhs[:2])

    def body(*refs):
        x_ref, g_ref = refs[n_lhs], refs[n_lhs + 1]
        r_ref = refs[n_lhs + 2] if has_res else None
        dx_refs, dg_ref = refs[-1 - len(want):-1], refs[-1]
        if n_lhs == 2:
            dnf = lax.dot_general(refs[0][...], refs[1][...], lhs[2], preferred_element_type=F32)
        else:
            dnf = refs[0][...].astype(F32)
        dx, dg = _rms_bwd_rows(dnf, x_ref[...], g_ref[...], r_ref[...] if has_res else None)
        for kind, dx_ref in zip(want, dx_refs):
            dx_ref[...] = dx.astype(F32 if kind == "f32" else BF16)

        @pl.when(pl.program_id(0) == 0)
        def _():
            dg_ref[...] = jnp.zeros_like(dg_ref)

        dg_ref[...] += dg

    row = pl.BlockSpec((tm, d), lambda i: (i, 0))
    vec = pl.BlockSpec((1, d), lambda i: (0, 0))
    if n_lhs == 2:
        first = [pl.BlockSpec((tm, lhs[0].shape[1]), lambda i: (i, 0)), pl.BlockSpec(lhs[1].shape, lambda i: (0, 0))]
    else:
        first = [row]
    return pl.pallas_call(
        body, name=name, grid=(t // tm,),
        in_specs=first + [row, vec] + ([row] if has_res else []),
        out_specs=[row] * len(want) + [vec],
        out_shape=[jax.ShapeDtypeStruct((t, d), F32 if kind == "f32" else BF16) for kind in want]
        + [jax.ShapeDtypeStruct((1, d), F32)],
        compiler_params=_params(("arbitrary",)),
    )(*(lhs[:2] + [x, g] + ([dres] if has_res else [])))


def _loss_head(a, w, res, tgt, g):
    t, d = res.shape
    k = a.shape[1]
    tm = min(ROW_TILE, t)

    def body(a_ref, w_ref, r_ref, t_ref, g_ref, loss_ref, dh_ref, dhb_ref, dg_ref):
        xf = lax.dot_general(a_ref[...], w_ref[...], NN, preferred_element_type=F32) + r_ref[...]
        gv = g_ref[...]
        r = lax.rsqrt(jnp.mean(xf * xf, axis=-1, keepdims=True) + RMS_EPS)
        xh = xf * r
        e = xh * gv - t_ref[...]
        dy = e * (1.0 / d)
        dxh = dy * gv
        dh = r * (dxh - xh * jnp.mean(dxh * xh, axis=-1, keepdims=True))
        dh_ref[...] = dh
        dhb_ref[...] = dh.astype(BF16)

        @pl.when(pl.program_id(0) == 0)
        def _():
            dg_ref[...] = jnp.zeros_like(dg_ref)
            loss_ref[...] = jnp.zeros_like(loss_ref)

        dg_ref[...] += jnp.sum(dy * xh, axis=0, keepdims=True)
        part = jnp.sum(jnp.sum(e * e, axis=1, keepdims=True), axis=0, keepdims=True) * (0.5 / d)
        loss_ref[...] += jnp.broadcast_to(part, loss_ref.shape)

    row = pl.BlockSpec((tm, d), lambda i: (i, 0))
    vec = pl.BlockSpec((1, d), lambda i: (0, 0))
    return pl.pallas_call(
        body, name="loss_head", grid=(t // tm,),
        in_specs=[pl.BlockSpec((tm, k), lambda i: (i, 0)), pl.BlockSpec((k, d), lambda i: (0, 0)), row, row, vec],
        out_specs=[pl.BlockSpec((8, LANES), lambda i: (0, 0)), row, row, vec],
        out_shape=[jax.ShapeDtypeStruct((8, LANES), F32), jax.ShapeDtypeStruct((t, d), F32),
                   jax.ShapeDtypeStruct((t, d), BF16), jax.ShapeDtypeStruct((1, d), F32)],
        compiler_params=_params(("arbitrary",)),
    )(a, w, res, tgt, g)


def _rope_tables(pos, inv_freq, sel_lo, sel_hi):
    t = pos.shape[0]
    tm = min(ROW_TILE, t)

    def body(p_ref, f_ref, lo_ref, hi_ref, c_ref, sa_ref, sb_ref):
        ang = p_ref[...].astype(F32) * f_ref[...]
        rot = lo_ref[...] + hi_ref[...]
        cs, sn = jnp.cos(ang), jnp.sin(ang)
        c_ref[...] = cs * rot + (1.0 - rot)
        sa_ref[...] = -sn * lo_ref[...]
        sb_ref[...] = sn * hi_ref[...]

    vec = pl.BlockSpec((1, LANES), lambda i: (0, 0))
    row = pl.BlockSpec((tm, LANES), lambda i: (i, 0))
    return pl.pallas_call(
        body, name="rope_tables", grid=(t // tm,),
        in_specs=[pl.BlockSpec((tm, 1), lambda i: (i, 0)), vec, vec, vec],
        out_specs=[row, row, row], out_shape=[jax.ShapeDtypeStruct((t, LANES), F32)] * 3,
        compiler_params=_params(("parallel",)),
    )(pos, inv_freq, sel_lo, sel_hi)


def _rope_apply(name, srcs, width, cos_t, sin_a, sin_b, sign, tail=()):
    t = srcs[0].shape[0]
    tm = min(ROW_TILE, t)
    n_cols = width // LANES
    n_src = len(srcs)

    def body(*refs):
        x_refs, tail_refs = refs[:n_src], refs[n_src:n_src + len(tail)]
        c_ref, sa_ref, sb_ref, o_ref = refs[n_src + len(tail):]
        cs, sa, sb = c_ref[...], sign * sa_ref[...], sign * sb_ref[...]
        for a, x_ref in enumerate(x_refs):
            for c in range(n_cols):
                xf = x_ref[:, c * LANES:(c + 1) * LANES].astype(F32)
                up = pltpu.roll(xf, LANES - ROPE_HALF, 1)
                dn = pltpu.roll(xf, ROPE_HALF, 1)
                o_ref[:, a * width + c * LANES:a * width + (c + 1) * LANES] = (xf * cs + up * sa + dn * sb).astype(BF16)
        col = n_src * width
        for t_ref in tail_refs:
            o_ref[:, col:col + t_ref.shape[1]] = t_ref[...]
            col += t_ref.shape[1]

    wide = n_src * width + sum(a.shape[1] for a in tail)
    tab = pl.BlockSpec((tm, LANES), lambda i: (i, 0))
    return pl.pallas_call(
        body, name=name, grid=(t // tm,),
        in_specs=[pl.BlockSpec((tm, width), lambda i: (i, 0))] * n_src
        + [pl.BlockSpec((tm, a.shape[1]), lambda i: (i, 0)) for a in tail] + [tab, tab, tab],
        out_specs=pl.BlockSpec((tm, wide), lambda i: (i, 0)),
        out_shape=jax.ShapeDtypeStruct((t, wide), BF16),
        compiler_params=_params(("parallel",)),
    )(*srcs, *tail, cos_t, sin_a, sin_b)


DA_T = 256
MIX_STREAMS = 4
SB_BWD_STREAMS = 2


def _lane_lo():
    return lax.broadcasted_iota(jnp.int32, (BLOCK, LANES), 1) < HEAD_DIM


def _dilated_bias_tiles(s):
    n = s // DA_T
    dist = (np.arange(n)[:, None, None] * DA_T + np.arange(DA_T)[None, :, None] - np.arange(DA_T)[None, None, :])
    cnt = np.zeros(dist.shape, np.float32)
    for window, dil in DIL_PATTERNS:
        cnt += ((dist >= 0) & (dist % dil == 0) & (dist <= window)).astype(np.float32)
    return jnp.asarray(np.where(cnt > 0, np.log(np.maximum(cnt, 1.0)), NEG).astype(np.float32))


def _stack_heads(x, lo):
    zero = jnp.zeros_like(x)
    return jnp.concatenate([jnp.where(lo, x, zero), jnp.where(lo, zero, x)], axis=0)


def _da_fwd(qk, proj, v_col0, bias, batch, s, ride=None, streams=MIX_STREAMS):
    t = qk.shape[0]
    nq = s // DA_T
    n_pairs = 4
    ns = streams
    wide = ns * LANES
    scale = HEAD_DIM ** -0.5

    def body(q_ref, k_ref, v_ref, b_ref, o_ref, lse_ref, acc_ref, m_ref, l_ref):
        i = pl.program_id(2)
        lo = lax.broadcasted_iota(jnp.int32, (DA_T, LANES), 1) < HEAD_DIM
        ones = jnp.ones((DA_T, LANES), BF16)
        acc_ref[...] = jnp.zeros_like(acc_ref)
        m_ref[...] = jnp.full(m_ref.shape, NEG, F32)
        l_ref[...] = jnp.zeros_like(l_ref)
        qqs = [_stack_heads(q_ref[:, st * LANES:(st + 1) * LANES] * scale, lo) for st in range(ns)]

        def scores(st, rows, bias2):
            k = k_ref[rows, st * LANES:(st + 1) * LANES]
            return lax.dot_general(qqs[st], k, NT, preferred_element_type=F32) + bias2

        def softmax(st, sc):
            m_old = m_ref[st]
            m_new = jnp.maximum(m_old, jnp.broadcast_to(jnp.max(sc, axis=1, keepdims=True), m_old.shape))
            m_ref[st] = m_new
            return jnp.exp(sc - jnp.concatenate([m_new, m_new], axis=1)).astype(BF16), jnp.exp(m_old - m_new)

        def values(st, rows, p, alpha):
            v = v_ref[rows, st * LANES:(st + 1) * LANES]
            vz = jnp.zeros_like(v)
            l_ref[st] = alpha * l_ref[st] + lax.dot_general(p, ones, NN, preferred_element_type=F32)
            pv = (lax.dot_general(p[:DA_T], jnp.where(lo, v, vz), NN, preferred_element_type=F32)
                  + lax.dot_general(p[DA_T:], jnp.where(lo, vz, v), NN, preferred_element_type=F32))
            acc_ref[st] = acc_ref[st] * jnp.where(lo, alpha[:DA_T], alpha[DA_T:]) + pv

        def trip(dlt, carry):
            rows = pl.ds(pl.multiple_of((i - dlt) * DA_T, DA_T), DA_T)
            bias_t = b_ref[dlt]
            bias2 = jnp.concatenate([bias_t, bias_t], axis=0)
            scs = [scores(st, rows, bias2) for st in range(ns)]
            pas = [softmax(st, scs[st]) for st in range(ns)]
            for st in range(ns):
                values(st, rows, *pas[st])
            return carry

        lax.fori_loop(0, i + 1, trip, 0)
        for st in range(ns):
            cols = slice(st * LANES, (st + 1) * LANES)
            l_t = l_ref[st]
            o_ref[:, cols] = (acc_ref[st] / jnp.where(lo, l_t[:DA_T], l_t[DA_T:])).astype(BF16)
            lse = m_ref[st] + jnp.log(l_t)
            lse_ref[:, cols] = jnp.where(lo, lse[:DA_T], lse[DA_T:])

    blk = pl.BlockSpec((DA_T, wide), lambda b, h, i: (b * nq + i, h))
    return _call(
        body, name="attn_a_fwd", grid=(batch, n_pairs // ns, nq),
        in_specs=[blk,
                  pl.BlockSpec((s, wide), lambda b, h, i: (b, n_pairs // ns + h)),
                  pl.BlockSpec((s, wide), lambda b, h, i: (b, v_col0 // ns + h)),
                  pl.BlockSpec((nq, DA_T, DA_T), lambda b, h, i: (0, 0, 0))],
        out_specs=[blk, blk],
        out_shape=[jax.ShapeDtypeStruct((t, n_pairs * LANES), BF16), jax.ShapeDtypeStruct((t, n_pairs * LANES), F32)],
        scratch=[pltpu.VMEM((ns, DA_T, LANES), F32), pltpu.VMEM((ns, 2 * DA_T, LANES), F32),
                 pltpu.VMEM((ns, 2 * DA_T, LANES), F32)],
        sem=("parallel", "parallel", "arbitrary"), args=(qk, qk, proj, bias), ride=ride)


def _da_bwd(qk, proj, v_col0, bias, o, lse, do, batch, s, ride=None, streams=MIX_STREAMS):
    t = qk.shape[0]
    nq = s // DA_T
    n_pairs = 4
    ns = streams
    wide = ns * LANES
    scale = HEAD_DIM ** -0.5

    def body(q_ref, k_ref, v_ref, b_ref, o_ref, lse_ref, do_ref, dq_ref, dk_ref, dv_ref, dk_acc, dv_acc, dq_acc):
        i = pl.program_id(2)
        lo = lax.broadcasted_iota(jnp.int32, (DA_T, LANES), 1) < HEAD_DIM

        @pl.when(i == 0)
        def _():
            dk_acc[...] = jnp.zeros_like(dk_acc)
            dv_acc[...] = jnp.zeros_like(dv_acc)

        dq_acc[...] = jnp.zeros_like(dq_acc)
        qqs, dds, deltas, lses = [], [], [], []
        for st in range(ns):
            cols = slice(st * LANES, (st + 1) * LANES)
            do_ = do_ref[:, cols]
            qqs.append(_stack_heads(q_ref[:, cols] * scale, lo))
            dds.append(_stack_heads(do_, lo))
            prod = do_.astype(F32) * o_ref[:, cols].astype(F32)
            fz = jnp.zeros_like(prod)
            deltas.append(jnp.concatenate([jnp.sum(jnp.where(lo, prod, fz), axis=1, keepdims=True),
                                           jnp.sum(jnp.where(lo, fz, prod), axis=1, keepdims=True)], axis=0))
            lse_t = lse_ref[:, cols]
            lses.append(jnp.concatenate([lse_t[:, 0:1], lse_t[:, HEAD_DIM:HEAD_DIM + 1]], axis=0))

        def products(st, rows, bias2):
            cols = slice(st * LANES, (st + 1) * LANES)
            sc = lax.dot_general(qqs[st], k_ref[rows, cols], NT, preferred_element_type=F32) + bias2
            return sc, lax.dot_general(dds[st], v_ref[rows, cols], NT, preferred_element_type=F32)

        def weights(st, sc, dp):
            p = jnp.exp(sc - lses[st])
            return (p * (dp - deltas[st])).astype(BF16), p.astype(BF16)

        def gradients(st, rows, ds, p):
            cols = slice(st * LANES, (st + 1) * LANES)
            k = k_ref[rows, cols]
            kz = jnp.zeros_like(k)
            dq_acc[st] += (lax.dot_general(ds[:DA_T], jnp.where(lo, k, kz), NN, preferred_element_type=F32)
                           + lax.dot_general(ds[DA_T:], jnp.where(lo, kz, k), NN, preferred_element_type=F32))
            dk_acc[rows, cols] += lax.dot_general(ds, qqs[st], TN, preferred_element_type=F32)
            dv_acc[rows, cols] += lax.dot_general(p, dds[st], TN, preferred_element_type=F32)

        def trip(dlt, carry):
            rows = pl.ds(pl.multiple_of((i - dlt) * DA_T, DA_T), DA_T)
            bias_t = b_ref[dlt]
            bias2 = jnp.concatenate([bias_t, bias_t], axis=0)
            prods = [products(st, rows, bias2) for st in range(ns)]
            wts = [weights(st, *prods[st]) for st in range(ns)]
            for st in range(ns):
                gradients(st, rows, *wts[st])
            return carry

        lax.fori_loop(0, i + 1, trip, 0)
        for st in range(ns):
            dq_ref[:, st * LANES:(st + 1) * LANES] = (dq_acc[st] * scale).astype(BF16)

        @pl.when(i == nq - 1)
        def _():
            dk_ref[...] = dk_acc[...].astype(BF16)
            dv_ref[...] = dv_acc[...].astype(BF16)

    blk = pl.BlockSpec((DA_T, wide), lambda b, h, i: (b * nq + i, h))
    seq = pl.BlockSpec((s, wide), lambda b, h, i: (b, h), pipeline_mode=pl.Buffered(1))
    one = pl.Buffered(1)
    out = jax.ShapeDtypeStruct((t, n_pairs * LANES), BF16)
    return _call(
        body, name="attn_a_bwd", grid=(batch, n_pairs // ns, nq),
        in_specs=[blk,
                  pl.BlockSpec((s, wide), lambda b, h, i: (b, n_pairs // ns + h), pipeline_mode=one),
                  pl.BlockSpec((s, wide), lambda b, h, i: (b, v_col0 // ns + h), pipeline_mode=one),
                  pl.BlockSpec((nq, DA_T, DA_T), lambda b, h, i: (0, 0, 0), pipeline_mode=one),
                  blk, blk, blk],
        out_specs=[blk, seq, seq], out_shape=[out, out, out],
        scratch=[pltpu.VMEM((s, wide), F32), pltpu.VMEM((s, wide), F32), pltpu.VMEM((ns, DA_T, LANES), F32)],
        sem=("parallel", "parallel", "arbitrary"), args=(qk, qk, proj, bias, o, lse, do), ride=ride)


SB_Q = 256


def _sb_consts(after):
    r = lax.broadcasted_iota(jnp.int32, (2 * BLOCK, 2 * BLOCK), 0) % BLOCK
    c = lax.broadcasted_iota(jnp.int32, (2 * BLOCK, 2 * BLOCK), 1)
    tri = (r > c) if after else (r < c)
    return jnp.logical_or(c >= BLOCK, tri).astype(BF16)


def _split(x):
    hi = x.astype(BF16)
    lo = (x - hi.astype(F32)).astype(BF16)
    return jnp.concatenate([hi, lo], axis=1)


def _sb_fwd(proj, q_col0, k_col0, v_col0, batch, s, ride=None, streams=MIX_STREAMS):
    t = proj.shape[0]
    nq = s // SB_Q
    n_pairs = 4
    ns = streams
    wide = ns * LANES
    scale = HEAD_DIM ** -0.5

    def body(q_ref, k_ref, v_ref, o_ref, tot_ref, acc_ref, run_ref):
        i = pl.program_id(2)
        lo_q = lax.broadcasted_iota(jnp.int32, (SB_Q, LANES), 1) < HEAD_DIM
        lo_k = _lane_lo()
        mat = _sb_consts(True)
        row = lax.broadcasted_iota(jnp.int32, (2 * SB_Q, LANES), 0) % SB_Q
        ahead = row - lax.broadcasted_iota(jnp.int32, (2 * SB_Q, LANES), 1)
        acc_ref[...] = jnp.zeros_like(acc_ref)
        run_ref[...] = jnp.zeros_like(run_ref)
        qqs = [_stack_heads(q_ref[:, st * LANES:(st + 1) * LANES] * scale, lo_q) for st in range(ns)]

        def units(todo):
            def rows(j):
                return pl.ds(pl.multiple_of(j * BLOCK, BLOCK), BLOCK)

            zs = [lax.dot_general(qqs[st], k_ref[rows(j), st * LANES:(st + 1) * LANES], NT, preferred_element_type=F32)
                  for st, j, _ in todo]
            logs = []
            for z, (_, _, off) in zip(zs, todo):
                lsig = jnp.minimum(z, 0.0) - jnp.log(1.0 + jnp.exp(-jnp.abs(z)))
                lneg = lsig - z
                if off is not None:
                    lneg = jnp.where(ahead > off, lneg, 0.0)
                logs.append((lsig, _split(lneg)))
            sums = [lax.dot_general(cat, mat, NN, preferred_element_type=F32) for _, cat in logs]
            probs = []
            for (lsig, _), sm, (st, _, off) in zip(logs, sums, todo):
                run = run_ref[st]
                a = jnp.exp(lsig + run + sm[:, :BLOCK])
                if off is not None:
                    a = jnp.where(ahead > off, a, 0.0)
                run_ref[st] = run + sm[:, BLOCK:]
                probs.append(a.astype(BF16))
            for ab, (st, j, _) in zip(probs, todo):
                v = v_ref[rows(j), st * LANES:(st + 1) * LANES]
                vz = jnp.zeros_like(v)
                acc_ref[st] += (lax.dot_general(ab[:SB_Q], jnp.where(lo_k, v, vz), NN, preferred_element_type=F32)
                                + lax.dot_general(ab[SB_Q:], jnp.where(lo_k, vz, v), NN, preferred_element_type=F32))

        units([(st, 2 * i + 1, BLOCK) for st in range(ns)] + [(st, 2 * i, 0) for st in range(ns)])

        def pair(p, carry):
            jp = i - 1 - p
            units([(st, 2 * jp + 1, None) for st in range(ns)] + [(st, 2 * jp, None) for st in range(ns)])
            return carry

        lax.fori_loop(0, i, pair, 0)
        for st in range(ns):
            cols = slice(st * LANES, (st + 1) * LANES)
            o_ref[:, cols] = acc_ref[st].astype(BF16)
            tot_ref[:, cols] = jnp.where(lo_q, run_ref[st, 0:SB_Q, :], run_ref[st, SB_Q:2 * SB_Q, :])

    def seq(col0):
        return pl.BlockSpec((s, wide), lambda b, h, i: (b, col0 // ns + h))

    blk = pl.BlockSpec((SB_Q, wide), lambda b, h, i: (b * nq + i, h))
    return _call(
        body, name="attn_b_fwd", grid=(batch, n_pairs // ns, nq),
        in_specs=[pl.BlockSpec((SB_Q, wide), lambda b, h, i: (b * nq + i, q_col0 // ns + h)), seq(k_col0), seq(v_col0)],
        out_specs=[blk, blk],
        out_shape=[jax.ShapeDtypeStruct((t, n_pairs * LANES), BF16), jax.ShapeDtypeStruct((t, n_pairs * LANES), F32)],
        scratch=[pltpu.VMEM((ns, SB_Q, LANES), F32), pltpu.VMEM((ns, 2 * SB_Q, LANES), F32)],
        sem=("parallel", "parallel", "arbitrary"), args=(proj, proj, proj), ride=ride)


def _sb_bwd(proj, q_col0, k_col0, v_col0, tot, do, batch, s, ride=None, streams=SB_BWD_STREAMS):
    t = proj.shape[0]
    nq = s // SB_Q
    n_pairs = 4
    ns = streams
    wide = ns * LANES
    scale = HEAD_DIM ** -0.5

    def body(q_ref, k_ref, v_ref, tot_ref, do_ref, dq_ref, dk_ref, dv_ref, dk_acc, dv_acc, dq_acc, seen_ref, gsum_ref):
        i = pl.program_id(2)
        lo_q = lax.broadcasted_iota(jnp.int32, (SB_Q, LANES), 1) < HEAD_DIM
        lo_k = _lane_lo()

        @pl.when(i == 0)
        def _():
            dk_acc[...] = jnp.zeros_like(dk_acc)
            dv_acc[...] = jnp.zeros_like(dv_acc)

        mat_after = _sb_consts(True)
        mat_before = _sb_consts(False)[:BLOCK]
        row = lax.broadcasted_iota(jnp.int32, (2 * SB_Q, LANES), 0) % SB_Q
        ahead = row - lax.broadcasted_iota(jnp.int32, (2 * SB_Q, LANES), 1)
        dq_acc[...] = jnp.zeros_like(dq_acc)
        seen_ref[...] = jnp.zeros_like(seen_ref)
        gsum_ref[...] = jnp.zeros_like(gsum_ref)
        qqs, dds, totals = [], [], []
        for st in range(ns):
            cols = slice(st * LANES, (st + 1) * LANES)
            qqs.append(_stack_heads(q_ref[:, cols] * scale, lo_q))
            dds.append(_stack_heads(do_ref[:, cols], lo_q))
            tot_t = tot_ref[:, cols]
            totals.append(jnp.concatenate([jnp.broadcast_to(tot_t[:, 0:1], (SB_Q, LANES)),
                                           jnp.broadcast_to(tot_t[:, HEAD_DIM:HEAD_DIM + 1], (SB_Q, LANES))], axis=0))

        def units(todo):
            def rows(j):
                return pl.ds(pl.multiple_of(j * BLOCK, BLOCK), BLOCK)

            def cols(st):
                return slice(st * LANES, (st + 1) * LANES)

            prods = [(lax.dot_general(qqs[st], k_ref[rows(j), cols(st)], NT, preferred_element_type=F32),
                      lax.dot_general(dds[st], v_ref[rows(j), cols(st)], NT, preferred_element_type=F32))
                     for st, j, _ in todo]
            logs = []
            for (z, _), (_, _, off) in zip(prods, todo):
                lsig = jnp.minimum(z, 0.0) - jnp.log(1.0 + jnp.exp(-jnp.abs(z)))
                lneg = lsig - z
                if off is not None:
                    lneg = jnp.where(ahead > off, lneg, 0.0)
                logs.append((lsig, _split(lneg)))
            sums = [lax.dot_general(cat, mat_after, NN, preferred_element_type=F32) for _, cat in logs]
            gates = []
            for (lsig, _), sm, (_, da), (st, _, off) in zip(logs, sums, prods, todo):
                seen = seen_ref[st]
                a = jnp.exp(lsig + (totals[st] - seen - sm[:, BLOCK:]) + sm[:, :BLOCK])
                if off is not None:
                    a = jnp.where(ahead > off, a, 0.0)
                seen_ref[st] = seen + sm[:, BLOCK:]
                g = a * da
                gates.append((a.astype(BF16), g, g.astype(BF16)))
            gsums = [lax.dot_general(cat, mat_before, NN, preferred_element_type=F32) for _, _, cat in gates]
            outs = []
            for (lsig, _), (ab, g, _), gs, (st, _, off) in zip(logs, gates, gsums, todo):
                gsum = gsum_ref[st]
                dz = g - jnp.exp(lsig) * (g + gsum + gs[:, :BLOCK])
                if off is not None:
                    dz = jnp.where(ahead > off, dz, 0.0)
                gsum_ref[st] = gsum + gs[:, BLOCK:]
                outs.append((dz.astype(BF16), ab))
            for (dzb, ab), (st, j, _) in zip(outs, todo):
                k = k_ref[rows(j), cols(st)]
                kz = jnp.zeros_like(k)
                dq_acc[st] += (lax.dot_general(dzb[:SB_Q], jnp.where(lo_k, k, kz), NN, preferred_element_type=F32)
                               + lax.dot_general(dzb[SB_Q:], jnp.where(lo_k, kz, k), NN, preferred_element_type=F32))
                dk_acc[rows(j), cols(st)] += lax.dot_general(dzb, qqs[st], TN, preferred_element_type=F32)
                dv_acc[rows(j), cols(st)] += lax.dot_general(ab, dds[st], TN, preferred_element_type=F32)

        def pair(p, carry):
            units([(st, 2 * p, None) for st in range(ns)] + [(st, 2 * p + 1, None) for st in range(ns)])
            return carry

        lax.fori_loop(0, i, pair, 0)
        units([(st, 2 * i, 0) for st in range(ns)] + [(st, 2 * i + 1, BLOCK) for st in range(ns)])
        for st in range(ns):
            dq_ref[:, st * LANES:(st + 1) * LANES] = (dq_acc[st] * scale).astype(BF16)

        @pl.when(i == nq - 1)
        def _():
            dk_ref[...] = dk_acc[...].astype(BF16)
            dv_ref[...] = dv_acc[...].astype(BF16)

    def seq_in(col0):
        return pl.BlockSpec((s, wide), lambda b, h, i: (b, col0 // ns + h))

    blk = pl.BlockSpec((SB_Q, wide), lambda b, h, i: (b * nq + i, h))
    seq = pl.BlockSpec((s, wide), lambda b, h, i: (b, h))
    out = jax.ShapeDtypeStruct((t, n_pairs * LANES), BF16)
    return _call(
        body, name="attn_b_bwd", grid=(batch, n_pairs // ns, nq),
        in_specs=[pl.BlockSpec((SB_Q, wide), lambda b, h, i: (b * nq + i, q_col0 // ns + h)), seq_in(k_col0),
                  seq_in(v_col0), blk, blk],
        out_specs=[blk, seq, seq], out_shape=[out, out, out],
        scratch=[pltpu.VMEM((s, wide), F32), pltpu.VMEM((s, wide), F32), pltpu.VMEM((ns, SB_Q, LANES), F32),
                 pltpu.VMEM((ns, 2 * SB_Q, LANES), F32), pltpu.VMEM((ns, 2 * SB_Q, LANES), F32)],
        sem=("parallel", "parallel", "arbitrary"), args=(proj, proj, proj, tot, do), ride=ride)


MEM_Q_TILE = 512


def _mem_fwd(q, kv, batch, s, n_mem):
    t, width = q.shape
    tq = min(MEM_Q_TILE, s)
    nq = s // tq
    scale = MEM_HEAD_DIM ** -0.5

    def body(q_ref, kv_ref, o_ref):
        for h in range(N_HEADS_MEM):
            cols = slice(h * MEM_HEAD_DIM, (h + 1) * MEM_HEAD_DIM)
            k = kv_ref[:, cols]
            v = kv_ref[:, width + h * MEM_HEAD_DIM: width + (h + 1) * MEM_HEAD_DIM]
            sc = lax.dot_general(q_ref[:, cols], k, NT, preferred_element_type=F32) * scale
            p = jnp.exp(sc - jnp.max(sc, axis=1, keepdims=True))
            p = p / jnp.sum(p, axis=1, keepdims=True)
            o_ref[:, cols] = lax.dot_general(p.astype(BF16), v, NN, preferred_element_type=F32).astype(BF16)

    return pl.pallas_call(
        body, name="mem_attn_fwd", grid=(batch, nq),
        in_specs=[pl.BlockSpec((tq, width), lambda b, i: (b * nq + i, 0)),
                  pl.BlockSpec((n_mem, 2 * width), lambda b, i: (b, 0))],
        out_specs=pl.BlockSpec((tq, width), lambda b, i: (b * nq + i, 0)),
        out_shape=jax.ShapeDtypeStruct((t, width), BF16),
        compiler_params=_params(("parallel", "parallel")),
    )(q, kv)


def _mem_bwd(q, kv, do, batch, s, n_mem):
    t, width = q.shape
    tq = min(MEM_Q_TILE, s)
    nq = s // tq
    scale = MEM_HEAD_DIM ** -0.5

    def body(q_ref, kv_ref, do_ref, dq_ref, dkv_ref, acc):
        i = pl.program_id(1)

        @pl.when(i == 0)
        def _():
            acc[...] = jnp.zeros_like(acc)

        for h in range(N_HEADS_MEM):
            cols = slice(h * MEM_HEAD_DIM, (h + 1) * MEM_HEAD_DIM)
            vcols = slice(width + h * MEM_HEAD_DIM, width + (h + 1) * MEM_HEAD_DIM)
            qh, k, v, doh = q_ref[:, cols], kv_ref[:, cols], kv_ref[:, vcols], do_ref[:, cols]
            sc = lax.dot_general(qh, k, NT, preferred_element_type=F32) * scale
            p = jnp.exp(sc - jnp.max(sc, axis=1, keepdims=True))
            p = p / jnp.sum(p, axis=1, keepdims=True)
            dp = lax.dot_general(doh, v, NT, preferred_element_type=F32)
            ds = (p * (dp - jnp.sum(p * dp, axis=1, keepdims=True)) * scale).astype(BF16)
            dq_ref[:, cols] = lax.dot_general(ds, k, NN, preferred_element_type=F32).astype(BF16)
            acc[:, cols] += lax.dot_general(ds, qh, TN, preferred_element_type=F32)
            acc[:, vcols] += lax.dot_general(p.astype(BF16), doh, TN, preferred_element_type=F32)

        @pl.when(i == nq - 1)
        def _():
            dkv_ref[...] = acc[...].astype(BF16)

    row = pl.BlockSpec((tq, width), lambda b, i: (b * nq + i, 0))
    kvs = pl.BlockSpec((n_mem, 2 * width), lambda b, i: (b, 0))
    return pl.pallas_call(
        body, name="mem_attn_bwd", grid=(batch, nq),
        in_specs=[row, kvs, row], out_specs=[row, kvs],
        out_shape=[jax.ShapeDtypeStruct((t, width), BF16), jax.ShapeDtypeStruct((batch * n_mem, 2 * width), BF16)],
        scratch_shapes=[pltpu.VMEM((n_mem, 2 * width), F32)],
        compiler_params=_params(("parallel", "arbitrary")),
    )(q, kv, do)


def _mixer_fwd(o_a, o_b, w_a, w_b, proj, gate_col0, w_out, x, g, w_q):
    t, width = o_a.shape
    d = w_a.shape[1]
    nq_cols = w_q.shape[1]
    tm = min(ROW_TILE, t)
    gb0 = gate_col0 * LANES // d

    def body(oa_ref, ob_ref, wa_ref, wb_ref, ga_ref, gb_ref, wo_ref, x_ref, g_ref, wq_ref, ua_ref, ub_ref, mix_ref,
             n_ref, h_ref, q_ref):
        ua = lax.dot_general(oa_ref[...], wa_ref[...], NN, preferred_element_type=F32)
        ub = lax.dot_general(ob_ref[...], wb_ref[...], NN, preferred_element_type=F32)
        ua_ref[...] = ua.astype(BF16)
        ub_ref[...] = ub.astype(BF16)
        mixed = (jax.nn.sigmoid(ga_ref[...].astype(F32)) * ua + jax.nn.sigmoid(gb_ref[...].astype(F32)) * ub).astype(BF16)
        mix_ref[...] = mixed
        h = lax.dot_general(mixed, wo_ref[...], NN, preferred_element_type=F32) + x_ref[...]
        h_ref[...] = h
        r = lax.rsqrt(jnp.mean(h * h, axis=-1, keepdims=True) + RMS_EPS)
        n = (h * r * g_ref[...]).astype(BF16)
        n_ref[...] = n
        q_ref[...] = lax.dot_general(n, wq_ref[...], NN, preferred_element_type=F32).astype(BF16)

    row = pl.BlockSpec((tm, width), lambda i: (i, 0))
    wsp = pl.BlockSpec((width, d), lambda i: (0, 0))
    out = pl.BlockSpec((tm, d), lambda i: (i, 0))
    osh = jax.ShapeDtypeStruct((t, d), BF16)
    return pl.pallas_call(
        body, name="mixer_fwd", grid=(t // tm,),
        in_specs=[row, row, wsp, wsp,
                  pl.BlockSpec((tm, d), lambda i: (i, gb0)), pl.BlockSpec((tm, d), lambda i: (i, gb0 + 1)),
                  pl.BlockSpec((d, d), lambda i: (0, 0)), out, pl.BlockSpec((1, d), lambda i: (0, 0)),
                  pl.BlockSpec((d, nq_cols), lambda i: (0, 0))],
        out_specs=[out, out, out, out, out, pl.BlockSpec((tm, nq_cols), lambda i: (i, 0))],
        out_shape=[osh, osh, osh, osh, jax.ShapeDtypeStruct((t, d), F32), jax.ShapeDtypeStruct((t, nq_cols), BF16)],
        compiler_params=_params(("parallel",)),
    )(o_a, o_b, w_a, w_b, proj, proj, w_out, x, g, w_q)


def _mixer_bwd(dh, w_out, ua, ub, proj, gate_col0, w_a, w_b):
    t, d = dh.shape
    width = w_a.shape[0]
    tm = min(ROW_TILE, t)
    nc = d // LANES

    def body(dh_ref, w_ref, ua_ref, ub_ref, ga_ref, gb_ref, wa_ref, wb_ref, dua_ref, dub_ref, dg_ref, doa_ref, dob_ref):
        dm = lax.dot_general(dh_ref[...], w_ref[...], NT, preferred_element_type=F32)
        sa = jax.nn.sigmoid(ga_ref[...].astype(F32))
        sb = jax.nn.sigmoid(gb_ref[...].astype(F32))
        dua = (dm * sa).astype(BF16)
        dub = (dm * sb).astype(BF16)
        dua_ref[...] = dua
        dub_ref[...] = dub
        dg_ref[:, 0:d] = (dm * ua_ref[...].astype(F32) * sa * (1.0 - sa)).astype(BF16)
        dg_ref[:, d:2 * d] = (dm * ub_ref[...].astype(F32) * sb * (1.0 - sb)).astype(BF16)
        doa_ref[...] = lax.dot_general(dua, wa_ref[...], NT, preferred_element_type=F32).astype(BF16)
        dob_ref[...] = lax.dot_general(dub, wb_ref[...], NT, preferred_element_type=F32).astype(BF16)

    row = pl.BlockSpec((tm, d), lambda i: (i, 0))
    wsp = pl.BlockSpec((width, d), lambda i: (0, 0))
    osp = pl.BlockSpec((tm, width), lambda i: (i, 0))
    return pl.pallas_call(
        body, name="mixer_bwd", grid=(t // tm,),
        in_specs=[row, pl.BlockSpec((d, d), lambda i: (0, 0)), row, row,
                  pl.BlockSpec((tm, d), lambda i: (i, gate_col0 // nc)),
                  pl.BlockSpec((tm, d), lambda i: (i, gate_col0 // nc + 1)), wsp, wsp],
        out_specs=[row, row, pl.BlockSpec((tm, 2 * d), lambda i: (i, 0)), osp, osp],
        out_shape=[jax.ShapeDtypeStruct((t, d), BF16), jax.ShapeDtypeStruct((t, d), BF16),
                   jax.ShapeDtypeStruct((t, 2 * d), BF16), jax.ShapeDtypeStruct((t, width), BF16),
                   jax.ShapeDtypeStruct((t, width), BF16)],
        compiler_params=_params(("parallel",)),
    )(dh, w_out, ua, ub, proj, proj, w_a, w_b)


FFN_COLS = 1024


def _ffn_up(n, w_gate, w_up):
    t, d = n.shape
    hidden = w_gate.shape[0]
    tm = min(ROW_TILE, t)
    tn = min(FFN_COLS, hidden)

    def body(n_ref, wg_ref, wu_ref, hg_ref, hu_ref, act_ref):
        hg = lax.dot_general(n_ref[...], wg_ref[...], NT, preferred_element_type=F32)
        hu = lax.dot_general(n_ref[...], wu_ref[...], NT, preferred_element_type=F32)
        hg_ref[...] = hg.astype(BF16)
        hu_ref[...] = hu.astype(BF16)
        act_ref[...] = (hg * jax.nn.sigmoid(hg) * hu).astype(BF16)

    wsp = pl.BlockSpec((tn, d), lambda j, i: (j, 0))
    out = pl.BlockSpec((tm, tn), lambda j, i: (i, j))
    osh = jax.ShapeDtypeStruct((t, hidden), BF16)
    return pl.pallas_call(
        body, name="ffn_up", grid=(hidden // tn, t // tm),
        in_specs=[pl.BlockSpec((tm, d), lambda j, i: (i, 0)), wsp, wsp],
        out_specs=[out, out, out], out_shape=[osh, osh, osh],
        compiler_params=_params(("parallel", "parallel")),
    )(n, w_gate, w_up)


def _ffn_bwd(dh, w_down, w_gate, w_up, hg, hu, x, g, dres, w_prev):
    t, d = dh.shape
    hidden = w_down.shape[0]
    q = w_prev.shape[0]
    tm = min(ROW_TILE, t)
    tn = min(FFN_COLS, hidden)
    nj = hidden // tn

    def body(dh_ref, wd_ref, wg_ref, wu_ref, hg_ref, hu_ref, x_ref, g_ref, r_ref, wp_ref, dhg_ref, dhu_ref, dx_ref,
             dxb_ref, dg_ref, do_ref, acc):
        j, i = pl.program_id(0), pl.program_id(1)
        dact = lax.dot_general(dh_ref[...], wd_ref[...], NT, preferred_element_type=F32)
        hg = hg_ref[...].astype(F32)
        sg = jax.nn.sigmoid(hg)
        dhu = (dact * hg * sg).astype(BF16)
        dhg = (dact * hu_ref[...].astype(F32) * sg * (1.0 + hg * (1.0 - sg))).astype(BF16)
        dhu_ref[...] = dhu
        dhg_ref[...] = dhg
        part = (lax.dot_general(dhg, wg_ref[...], NN, preferred_element_type=F32)
                + lax.dot_general(dhu, wu_ref[...], NN, preferred_element_type=F32))

        @pl.when(j == 0)
        def _():
            acc[i] = part

        @pl.when(j > 0)
        def _():
            acc[i] += part

        @pl.when(jnp.logical_and(j == 0, i == 0))
        def _():
            dg_ref[...] = jnp.zeros_like(dg_ref)

        @pl.when(j == nj - 1)
        def _():
            dx, dg = _rms_bwd_rows(acc[i], x_ref[...], g_ref[...], r_ref[...])
            dx_ref[...] = dx
            dxb = dx.astype(BF16)
            dxb_ref[...] = dxb
            dg_ref[...] += dg
            do_ref[...] = lax.dot_general(dxb, wp_ref[...], NT, preferred_element_type=F32).astype(BF16)

    hid = pl.BlockSpec((tm, tn), lambda j, i: (i, j))
    wsp = pl.BlockSpec((tn, d), lambda j, i: (j, 0), pipeline_mode=pl.Buffered(1))
    late = pl.BlockSpec((tm, d), lambda j, i: (jnp.where(j == nj - 1, i, 0), 0))
    late_q = pl.BlockSpec((tm, q), lambda j, i: (jnp.where(j == nj - 1, i, 0), 0))
    vec = pl.BlockSpec((1, d), lambda j, i: (0, 0))
    osh = jax.ShapeDtypeStruct((t, hidden), BF16)
    return pl.pallas_call(
        body, name="ffn_bwd", grid=(nj, t // tm),
        in_specs=[pl.BlockSpec((tm, d), lambda j, i: (i, 0)), wsp, wsp, wsp, hid, hid, late, vec, late,
                  pl.BlockSpec((q, d), lambda j, i: (0, 0), pipeline_mode=pl.Buffered(1))],
        out_specs=[hid, hid, late, late, vec, late_q],
        out_shape=[osh, osh, jax.ShapeDtypeStruct((t, d), F32), jax.ShapeDtypeStruct((t, d), BF16),
                   jax.ShapeDtypeStruct((1, d), F32), jax.ShapeDtypeStruct((t, q), BF16)],
        scratch_shapes=[pltpu.VMEM((t // tm, tm, d), F32)],
        compiler_params=_params(("arbitrary", "arbitrary")),
    )(dh, w_down, w_gate, w_up, hg, hu, x, g, dres, w_prev)


MM_ROWS = 1024


def _mm_w(name, a, w, out_dtype, dims=NN):
    t, k = a.shape
    n = w.shape[1] if dims == NN else w.shape[0]
    tm, tn = min(MM_ROWS, t), min(1024, n)
    o_spec = pl.BlockSpec((tm, tn), lambda j, i: (i, j))
    b_spec = pl.BlockSpec((k, tn), lambda j, i: (0, j)) if dims == NN else pl.BlockSpec((tn, k), lambda j, i: (j, 0))
    return _mm(name, a, w, grid=(n // tn, t // tm), a_spec=pl.BlockSpec((tm, k), lambda j, i: (i, 0)), b_spec=b_spec,
               o_shape=(t, n), o_spec=o_spec, dims=dims, out_dtype=out_dtype)


def _mm_res_norm(name, a, w, res, g):
    t, k = a.shape
    d = w.shape[1]
    tm = min(ROW_TILE, t)

    def body(a_ref, w_ref, r_ref, g_ref, h_ref, n_ref):
        h = lax.dot_general(a_ref[...], w_ref[...], NN, preferred_element_type=F32) + r_ref[...]
        h_ref[...] = h
        r = lax.rsqrt(jnp.mean(h * h, axis=-1, keepdims=True) + RMS_EPS)
        n_ref[...] = (h * r * g_ref[...]).astype(BF16)

    row = pl.BlockSpec((tm, d), lambda i: (i, 0))
    return pl.pallas_call(
        body, name=name, grid=(t // tm,),
        in_specs=[pl.BlockSpec((tm, k), lambda i: (i, 0)), pl.BlockSpec((k, d), lambda i: (0, 0)), row,
                  pl.BlockSpec((1, d), lambda i: (0, 0))],
        out_specs=[row, row], out_shape=[jax.ShapeDtypeStruct((t, d), F32), jax.ShapeDtypeStruct((t, d), BF16)],
        compiler_params=_params(("parallel",)),
    )(a, w, res, g)


def _wgrad(name, a, g, tk=1024, tn=1024):
    t, k = a.shape
    n = g.shape[1]
    tm, tk, tn = min(2 * MM_ROWS, t), min(tk, k), min(tn, n)
    return _mm(name, a, g, grid=(k // tk, n // tn, t // tm),
               a_spec=pl.BlockSpec((tm, tk), lambda p, q, r: (r, p)), b_spec=pl.BlockSpec((tm, tn), lambda p, q, r: (r, q)),
               o_shape=(k, n), o_spec=pl.BlockSpec((tk, tn), lambda p, q, r: (p, q)), dims=TN, out_dtype=BF16, nk=t // tm)


def _peers():
    x, y, c = lax.axis_index("x"), lax.axis_index("y"), lax.axis_index("c")
    me = 4 * x + 2 * y + c
    out = []
    for k in range(1, N_DEV):
        kx, ky, kc = (k >> 2) & 1, (k >> 1) & 1, k & 1
        px = 1 - x if kx else x
        py = 1 - y if ky else y
        pc = 1 - c if kc else c
        out.append(((px, py, pc), 4 * px + 2 * py + pc))
    return me, out


def _cast_weights(ws, pad_rows):
    def body(*refs):
        n = len(refs) // 2
        for i_ref, o_ref, pr in zip(refs[:n], refs[n:], pad_rows):
            r, c = i_ref.shape
            o_ref[0:r, :] = i_ref[...].astype(BF16)
            if pr:
                o_ref[r:r + pr, :] = jnp.zeros((pr, c), BF16)

    return pl.pallas_call(
        body, name="cast_weights", in_specs=[VMEM] * len(ws), out_specs=[VMEM] * len(ws),
        out_shape=[jax.ShapeDtypeStruct((w.shape[0] + pr, w.shape[1]), BF16) for w, pr in zip(ws, pad_rows)],
    )(*ws)


def _window(ref, j, c):
    return ref.at[:, pl.ds(pl.multiple_of(j * c, LANES), c)]


def _direct_copies(ins, outs, sems, gather, cols, landed):
    send_sems, recv_sems, loc_sems = sems
    n_peer = N_DEV - 1
    me, peers = _peers()

    def src(w, j):
        if gather:
            return ins[w]
        return _window(ins[w], j, cols[w]) if cols[w] else ins[w].at[j]

    def dst(w, j):
        return _window(outs[w], j, cols[w]) if gather and cols[w] else outs[w].at[j]

    local = [pltpu.make_async_copy(src(w, me), dst(w, me), loc_sems.at[w]) for w in range(len(ins))]
    remote = [pltpu.make_async_remote_copy(
        src_ref=src(w, idx), dst_ref=dst(w, idx if landed else me),
        send_sem=send_sems.at[w * n_peer + k], recv_sem=recv_sems.at[w * n_peer + k],
        device_id=dev, device_id_type=pl.DeviceIdType.MESH)
        for k, (dev, idx) in reversed(list(enumerate(peers))) for w in range(len(ins))]
    return local, remote


OTHER_CHIPS = (2, 4, 6)


def _gather_copies(ins, outs, sems, cols):
    send_sems, recv_sems, loc_sems = sems
    x, y, c = lax.axis_index("x"), lax.axis_index("y"), lax.axis_index("c")
    me = 4 * x + 2 * y + c
    n_pair = N_DEV - 1

    def dev(mask):
        return (1 - x if mask & 4 else x, 1 - y if mask & 2 else y, 1 - c if mask & 1 else c)

    def slot(w, mask):
        j = jnp.bitwise_xor(me, mask)
        return _window(outs[w], j, cols[w]) if cols[w] else outs[w].at[j]

    def remote(w, pair, src, to_slot, target):
        return pltpu.make_async_remote_copy(src_ref=src, dst_ref=slot(w, to_slot), send_sem=send_sems.at[w * n_pair + pair],
                                            recv_sem=recv_sems.at[w * n_pair + pair], device_id=dev(target),
                                            device_id_type=pl.DeviceIdType.MESH)

    ws = range(len(ins))
    return dict(
        local=[pltpu.make_async_copy(ins[w], slot(w, 0), loc_sems.at[w]) for w in ws],
        to_chips=[remote(w, 1 + t, ins[w], 0, m) for t, m in enumerate(OTHER_CHIPS) for w in ws],
        to_core=[remote(w, 0, ins[w], 0, 1) for w in ws],
        from_chips=[remote(w, 1 + t, ins[w], m, 0) for t, m in enumerate(OTHER_CHIPS) for w in ws],
        pass_on=[remote(w, 4 + t, slot(w, m), m, 1) for t, m in enumerate(OTHER_CHIPS) for w in ws],
        from_core=[remote(w, 0, ins[w], 1, 0) for w in ws]
        + [remote(w, 4 + t, ins[w], m + 1, 0) for t, m in enumerate(OTHER_CHIPS) for w in ws])


TWO_LEVEL = "gather in two levels"


def _exchange_start(ins, outs, sems, gather, cols):
    if gather == TWO_LEVEL:
        cps = _gather_copies(ins, outs, sems, cols)
        for cp in cps["local"] + cps["to_chips"] + cps["to_core"]:
            cp.start()
    else:
        local, remote = _direct_copies(ins, outs, sems, gather, cols, False)
        for cp in local + remote:
            cp.start()


def _exchange_pass_on(ins, outs, sems, gather, cols, chips):
    if gather == TWO_LEVEL:
        cps = _gather_copies(ins, outs, sems, cols)
        n = len(ins)
        for t in chips:
            for arrived, onward in zip(cps["from_chips"][t * n:(t + 1) * n], cps["pass_on"][t * n:(t + 1) * n]):
                arrived.wait_recv()
                onward.start()


def _exchange_wait(ins, outs, sems, gather, cols):
    if gather == TWO_LEVEL:
        cps = _gather_copies(ins, outs, sems, cols)
        for cp in cps["local"]:
            cp.wait()
        for cp in cps["to_chips"] + cps["to_core"] + cps["pass_on"]:
            cp.wait_send()
        for cp in cps["from_core"]:
            cp.wait_recv()
    else:
        local, remote = _direct_copies(ins, outs, sems, gather, cols, True)
        for cp in local:
            cp.wait()
        for cp in remote:
            cp.wait_send()
            cp.wait_recv()


def _exchange_shapes(arrs, gather, cols):
    n = len(arrs)
    out_shape = []
    for a, c in zip(arrs, cols):
        if gather:
            shape = (a.shape[0], N_DEV * c) if c else (N_DEV,) + a.shape
        else:
            shape = (N_DEV, a.shape[0], c) if c else a.shape
        out_shape.append(jax.ShapeDtypeStruct(shape, a.dtype))
    sems = [pltpu.SemaphoreType.DMA((n * (N_DEV - 1),)), pltpu.SemaphoreType.DMA((n * (N_DEV - 1),)),
            pltpu.SemaphoreType.DMA((n,))]
    return out_shape, sems


def _call(body, *, name, grid, in_specs, out_specs, out_shape, scratch, sem, args, ride=None):
    if ride is None:
        outs = pl.pallas_call(body, name=name, grid=grid, in_specs=in_specs, out_specs=out_specs, out_shape=out_shape,
                              scratch_shapes=scratch, compiler_params=_params(sem))(*args)
        return outs, None
    arrs, gather, cols = ride
    n, n_in, n_out, n_scr = len(arrs), len(in_specs), len(out_specs), len(scratch)
    x_shape, x_sems = _exchange_shapes(arrs, gather, cols)

    def riding(*refs):
        ins, x_ins = refs[:n_in], refs[n_in:n_in + n]
        outs = refs[n_in + n:n_in + n + n_out]
        x_outs = refs[n_in + n + n_out:n_in + 2 * n + n_out]
        scr = refs[n_in + 2 * n + n_out:n_in + 2 * n + n_out + n_scr]
        sems = refs[n_in + 2 * n + n_out + n_scr:]
        def at(step):
            return functools.reduce(jnp.logical_and, [pl.program_id(a) == v for a, v in enumerate(step)])

        @pl.when(at((0,) * len(grid)))
        def _():
            _exchange_start(x_ins, x_outs, sems, gather, cols)

        @pl.when(at((grid[0] // 2,) + (0,) * (len(grid) - 2) + (grid[-1] // 2,)))
        def _():
            _exchange_pass_on(x_ins, x_outs, sems, gather, cols, (0, 1))

        @pl.when(at((grid[0] // 2,) + (0,) * (len(grid) - 2) + (3 * grid[-1] // 4,)))
        def _():
            _exchange_pass_on(x_ins, x_outs, sems, gather, cols, (2,))

        body(*ins, *outs, *scr)

        @pl.when(at(tuple(g - 1 for g in grid)))
        def _():
            _exchange_wait(x_ins, x_outs, sems, gather, cols)

    res = pl.pallas_call(
        riding, name=name, grid=grid, in_specs=list(in_specs) + [ANY] * n, out_specs=list(out_specs) + [ANY] * n,
        out_shape=list(out_shape) + x_shape, scratch_shapes=list(scratch) + x_sems,
        compiler_params=_params(("arbitrary",) * len(grid)))(*args, *arrs)
    return res[:n_out], res[n_out:]


def _my_block():
    return (4 * lax.axis_index("x") + 2 * lax.axis_index("y") + lax.axis_index("c")).astype(jnp.int32).reshape(1)


def _proj_in_gather(x, g, w_shard):
    t, k = x.shape
    cs = w_shard.shape[1]
    tm = min(MM_ROWS, t)
    ni = t // tm
    arrival = (0, 1, 2, 4, 3, 5, 6, 7)

    def mask_at(s):
        return jnp.where(s == 3, 4, jnp.where(s == 4, 3, s))

    def body(me_ref, x_ref, g_ref, w_hbm, o_ref, all_hbm, n_hbm, w_vmem, n_vmem, send_sems, recv_sems, loc_sems,
             load_sems, n_sem):
        s, i = pl.program_id(0), pl.program_id(1)
        cps = _gather_copies([w_hbm], [all_hbm], (send_sems, recv_sems, loc_sems), (cs,))
        by_mask = {0: cps["local"][0], 1: cps["from_core"][0]}
        for t_chip, m in enumerate(OTHER_CHIPS):
            by_mask[m] = cps["from_chips"][t_chip]
            by_mask[m + 1] = cps["from_core"][1 + t_chip]
        arrived = [by_mask[m] for m in arrival]

        def load(step):
            src = w_hbm if step == 0 else _window(all_hbm, jnp.bitwise_xor(me_ref[0], arrival[step]), cs)
            return pltpu.make_async_copy(src, w_vmem.at[step % 2], load_sems.at[step % 2])

        @pl.when(jnp.logical_and(s == 0, i == 0))
        def _():
            for cp in cps["local"] + cps["to_chips"] + cps["to_core"]:
                cp.start()
            load(0).start()

        for step, mask in enumerate(arrival):
            @pl.when(jnp.logical_and(s == step, i == 0))
            def _(step=step):
                load(step).wait()

            if step + 1 < N_DEV:
                @pl.when(jnp.logical_and(s == step, i == min(1, ni - 1)))
                def _(step=step):
                    arrived[step + 1].wait_recv()
                    if arrival[step + 1] in OTHER_CHIPS:
                        cps["pass_on"][OTHER_CHIPS.index(arrival[step + 1])].start()
                    load(step + 1).start()

        @pl.when(s == 0)
        def _():
            xf = x_ref[...]
            r = lax.rsqrt(jnp.mean(xf * xf, axis=-1, keepdims=True) + RMS_EPS)
            n_vmem[i] = (xf * r * g_ref[...]).astype(BF16)
            keep = pltpu.make_async_copy(n_vmem.at[i], n_hbm.at[pl.ds(pl.multiple_of(i * tm, tm), tm), :], n_sem)
            keep.start()
            keep.wait()

        o_ref[...] = lax.dot_general(n_vmem[i], w_vmem[s % 2], NN, preferred_element_type=F32).astype(BF16)

        @pl.when(jnp.logical_and(s == N_DEV - 1, i == ni - 1))
        def _():
            cps["local"][0].wait()
            for cp in cps["to_chips"] + cps["to_core"] + cps["pass_on"]:
                cp.wait_send()

    return pl.pallas_call(
        body, name="proj_in",
        grid_spec=pltpu.PrefetchScalarGridSpec(
            num_scalar_prefetch=1, grid=(N_DEV, ni),
            in_specs=[pl.BlockSpec((tm, k), lambda s, i, me: (jnp.where(s == 0, i, 0), 0)),
                      pl.BlockSpec((1, k), lambda s, i, me: (0, 0)), ANY],
            out_specs=[pl.BlockSpec((tm, cs), lambda s, i, me: (i, jnp.bitwise_xor(me[0], mask_at(s)))), ANY, ANY],
            scratch_shapes=[pltpu.VMEM((2, k, cs), BF16), pltpu.VMEM((ni, tm, k), BF16),
                            pltpu.SemaphoreType.DMA((N_DEV - 1,)), pltpu.SemaphoreType.DMA((N_DEV - 1,)),
                            pltpu.SemaphoreType.DMA((1,)), pltpu.SemaphoreType.DMA((2,)), pltpu.SemaphoreType.DMA]),
        out_shape=[jax.ShapeDtypeStruct((t, N_DEV * cs), BF16), jax.ShapeDtypeStruct((k, N_DEV * cs), BF16),
                   jax.ShapeDtypeStruct((t, k), BF16)],
        compiler_params=_params(("arbitrary", "arbitrary")),
    )(_my_block(), x, g, w_shard)


def _gw_in_scatter(a, g):
    t, k = a.shape
    cs = g.shape[1] // N_DEV
    tm = min(MM_ROWS, t)
    nr = t // tm
    n_chip = N_DEV // 2
    chips = (6, 4, 2, 0)

    def body(me_ref, a_ref, g_ref, out_hbm, acc, stage, other, core_send, core_recv, chip_send, chip_recv, loc_sem):
        s, r = pl.program_id(0), pl.program_id(1)
        x, y, c = lax.axis_index("x"), lax.axis_index("y"), lax.axis_index("c")
        my_chip = 2 * x + y
        part = lax.dot_general(a_ref[...], g_ref[...], TN, preferred_element_type=F32)

        def to_core(m):
            return pltpu.make_async_remote_copy(src_ref=stage.at[0], dst_ref=other.at[m], send_sem=core_send.at[m],
                                                recv_sem=core_recv.at[m], device_id=(x, y, 1 - c),
                                                device_id_type=pl.DeviceIdType.MESH)

        def to_chip(m, landed):
            mask = chips[m]
            there = (1 - x if mask & 4 else x, 1 - y if mask & 2 else y, c)
            slot = (2 * there[0] + there[1]) if landed else my_chip
            return pltpu.make_async_remote_copy(src_ref=stage.at[1], dst_ref=out_hbm.at[slot], send_sem=chip_send.at[m],
                                                recv_sem=chip_recv.at[m], device_id=there,
                                                device_id_type=pl.DeviceIdType.MESH)

        local = pltpu.make_async_copy(stage.at[1], out_hbm.at[my_chip], loc_sem)

        @pl.when(r == 0)
        def _():
            acc[...] = part

        @pl.when(r > 0)
        def _():
            acc[...] += part

        for step in range(N_DEV):
            m = step // 2

            @pl.when(jnp.logical_and(s == step, r == nr - 1))
            def _(step=step, m=m):
                if step % 2 == 0:
                    if m > 0:
                        to_core(m - 1).wait_send()
                    stage[0] = acc[...].astype(BF16)
                    to_core(m).start()
                else:
                    if m > 0:
                        to_chip(m - 1, False).wait_send()
                    to_core(m).wait_recv()
                    stage[1] = (acc[...] + other[m].astype(F32)).astype(BF16)
                    if m < n_chip - 1:
                        to_chip(m, False).start()
                    else:
                        local.start()
                        to_core(m).wait_send()
                        local.wait()
                        for mm in range(n_chip - 1):
                            to_chip(mm, True).wait_recv()

    return pl.pallas_call(
        body, name="gw_in",
        grid_spec=pltpu.PrefetchScalarGridSpec(
            num_scalar_prefetch=1, grid=(N_DEV, nr),
            in_specs=[pl.BlockSpec((tm, k), lambda s, r, me: (r, 0)),
                      pl.BlockSpec((tm, cs), lambda s, r, me: (r, jnp.bitwise_xor(me[0], N_DEV - 1 - s)))],
            out_specs=ANY,
            scratch_shapes=[pltpu.VMEM((k, cs), F32), pltpu.VMEM((2, k, cs), BF16), pltpu.VMEM((n_chip, k, cs), BF16),
                            pltpu.SemaphoreType.DMA((n_chip,)), pltpu.SemaphoreType.DMA((n_chip,)),
                            pltpu.SemaphoreType.DMA((n_chip - 1,)), pltpu.SemaphoreType.DMA((n_chip - 1,)),
                            pltpu.SemaphoreType.DMA]),
        out_shape=jax.ShapeDtypeStruct((n_chip, k, cs), BF16),
        compiler_params=_params(("arbitrary", "arbitrary")),
    )(_my_block(), a, g)


SMALL_ROWS = 8


def _allreduce_small(parts, loss_part):
    n, d = len(parts), parts[0].shape[1]

    def body(*refs):
        part_refs, loss_ref, o_ref = refs[:n], refs[n], refs[n + 1]
        mine_ref, all_ref, send_sems, recv_sems = refs[n + 2:]
        me, peers = _peers()
        mine_ref[...] = jnp.zeros_like(mine_ref)
        for i, p_ref in enumerate(part_refs):
            mine_ref[i:i + 1, :] = p_ref[...]
        mine_ref[SMALL_ROWS - 1:SMALL_ROWS, 0:LANES] = loss_ref[0:1, :]
        all_ref[me] = mine_ref[...]
        for k, (dev, idx) in enumerate(peers):
            pltpu.make_async_remote_copy(src_ref=mine_ref, dst_ref=all_ref.at[me], send_sem=send_sems.at[k],
                                         recv_sem=recv_sems.at[k], device_id=dev,
                                         device_id_type=pl.DeviceIdType.MESH).start()
        for k, (dev, idx) in enumerate(peers):
            cp = pltpu.make_async_remote_copy(src_ref=mine_ref, dst_ref=all_ref.at[idx], send_sem=send_sems.at[k],
                                              recv_sem=recv_sems.at[k], device_id=dev,
                                              device_id_type=pl.DeviceIdType.MESH)
            cp.wait_send()
            cp.wait_recv()
        tot = all_ref[0]
        for dvc in range(1, N_DEV):
            tot = tot + all_ref[dvc]
        o_ref[...] = tot

    return pl.pallas_call(
        body, name="allreduce_small", in_specs=[VMEM] * (n + 1), out_specs=VMEM,
        out_shape=jax.ShapeDtypeStruct((SMALL_ROWS, d), F32),
        scratch_shapes=[pltpu.VMEM((SMALL_ROWS, d), F32), pltpu.VMEM((N_DEV, SMALL_ROWS, d), F32),
                        pltpu.SemaphoreType.DMA((N_DEV - 1,)), pltpu.SemaphoreType.DMA((N_DEV - 1,))],
    )(*parts, loss_part)


def _adam_math(g, w, m, v):
    m_new = ADAM_B1 * m + (1.0 - ADAM_B1) * g
    v_new = ADAM_B2 * v + (1.0 - ADAM_B2) * (g * g)
    m_hat = m_new / (1.0 - ADAM_B1 ** ADAM_STEP)
    v_hat = v_new / (1.0 - ADAM_B2 ** ADAM_STEP)
    delta = -ADAM_LR * (m_hat / (jnp.sqrt(v_hat) + ADAM_EPS) + ADAM_WD * w)
    return delta, m_new, v_new


def _adam(name, pieces, w, m, v):
    r, c = w.shape
    n_piece, _, cp = pieces.shape
    tr = r
    for cand in (256, 176, 128, 64):
        if r % cand == 0 and r > cand:
            tr = cand
            break

    def body(p_ref, w_ref, m_ref, v_ref, g_ref, d_ref, mo_ref, vo_ref):
        g = p_ref[0, :, 0:c].astype(F32)
        for j in range(1, n_piece):
            g = g + p_ref[j, :, 0:c].astype(F32)
        delta, m_new, v_new = _adam_math(g, w_ref[...], m_ref[...], v_ref[...])
        g_ref[...] = g
        d_ref[...] = delta
        mo_ref[...] = m_new
        vo_ref[...] = v_new

    blk = pl.BlockSpec((tr, c), lambda i: (i, 0))
    osh = jax.ShapeDtypeStruct((r, c), F32)
    return pl.pallas_call(
        body, name=name, grid=(r // tr,),
        in_specs=[pl.BlockSpec((n_piece, tr, cp), lambda i: (0, i, 0)), blk, blk, blk],
        out_specs=[blk, blk, blk, blk], out_shape=[osh, osh, osh, osh],
        compiler_params=_params(("parallel",)),
    )(pieces, w, m, v)


def _adam_small(g_all, ws, ms, vs):
    n = len(ws)

    def body(*refs):
        g_ref, ins, outs = refs[0], refs[1:1 + 3 * n], refs[1 + 3 * n:]
        for i in range(n):
            g = g_ref[i:i + 1, :]
            delta, m_new, v_new = _adam_math(g, ins[i][...], ins[n + i][...], ins[2 * n + i][...])
            for kind, val in enumerate((g, delta, m_new, v_new)):
                outs[kind * n + i][...] = val

    osh = jax.ShapeDtypeStruct(ws[0].shape, F32)
    res = pl.pallas_call(body, name="adam_small", in_specs=[VMEM] * (1 + 3 * n), out_specs=[VMEM] * (4 * n),
                         out_shape=[osh] * (4 * n))(g_all, *ws, *ms, *vs)
    return res[:n], res[n:2 * n], res[2 * n:3 * n], res[3 * n:]


def _local_step(x, mem, pos, tgt, gains, w_in_shard, shards, batch):
    g_mix, g_mem_q, g_mem_kv, g_ffn, g_final = gains
    t, d = x.shape
    s = t // batch
    n_mem = mem.shape[0] // batch
    n_sh = N_DEV
    width = shards[0].shape[0]
    nb = width // LANES

    lane = np.arange(LANES) % HEAD_DIM
    sel_lo = (lane < ROPE_HALF).astype(np.float32)[None, :]
    sel_hi = ((lane >= ROPE_HALF) & (lane < 2 * ROPE_HALF)).astype(np.float32)[None, :]
    freqs = np.float32(ROPE_THETA) ** (-np.arange(ROPE_HALF, dtype=np.float32) / np.float32(ROPE_HALF))
    inv_freq = np.where(lane < 2 * ROPE_HALF, freqs[lane % ROPE_HALF], 0.0).astype(np.float32)[None, :]
    cos_t, sin_a, sin_b = _rope_tables(pos, jnp.asarray(inv_freq), jnp.asarray(sel_lo), jnp.asarray(sel_hi))
    bias = _dilated_bias_tiles(s)

    proj, w_in, n1 = _proj_in_gather(x, g_mix, w_in_shard)
    qk_a = _rope_apply("rope_fwd", [proj], 2 * width, cos_t, sin_a, sin_b, 1.0)
    cs_up = shards[0].shape[1]
    (o_a, lse_a), (w_up_a, w_up_b, w_out, w_q, w_kv, w_o, w_fd) = _da_fwd(
        qk_a, proj, 2 * nb, bias, batch, s,
        ride=(shards[:6] + shards[8:], TWO_LEVEL, (cs_up, cs_up, 0, 0, 0, cs_up, 0)))
    (o_b, tot_b), (w_fg, w_fu) = _sb_fwd(proj, 3 * nb, 4 * nb, 5 * nb, batch, s, ride=(shards[6:8], True, (0, 0)))
    w_out = w_out.reshape(d, d)
    w_q = w_q.reshape(d, -1)
    w_kv = w_kv.reshape(d, -1)
    w_fd = w_fd.reshape(-1, d)
    w_fg = w_fg.reshape(-1, d)
    w_fu = w_fu.reshape(-1, d)
    ua, ub, mixed, n2, h1, q_m = _mixer_fwd(o_a, o_b, w_up_a, w_up_b, proj, 6 * nb, w_out, x, g_mem_q, w_q)
    mem_n = _rms_fwd("norm_mem_kv", mem, g_mem_kv)
    kv_m = _mm_w("mem_kv", mem_n, w_kv, BF16)
    o_m = _mem_fwd(q_m, kv_m, batch, s, n_mem)
    h2, n3 = _mm_res_norm("mem_out", o_m, w_o, h1, g_ffn)
    hg, hu, act = _ffn_up(n3, w_fg, w_fu)
    loss_part, dh3, dh3_b, dg_final = _loss_head(act, w_fd, h2, tgt, g_final.reshape(1, d))

    dhg, dhu, dh2, dh2_b, dg_ffn, do_m = _ffn_bwd(dh3_b, w_fd, w_fg, w_fu, hg, hu, h2, g_ffn, dh3, w_o)
    gw_fd = _wgrad("gw_ffn_down", act, dh3_b)
    gw_fg = _wgrad("gw_ffn_gate", dhg, n3)
    gw_fu = _wgrad("gw_ffn_up", dhu, n3)

    gw_o = _wgrad("gw_mem_o", o_m, dh2_b)
    dq_m, dkv_m = _mem_bwd(q_m, kv_m, do_m, batch, s, n_mem)
    gw_q = _wgrad("gw_mem_q", n2, dq_m)
    gw_kv = _wgrad("gw_mem_kv", mem_n, dkv_m)
    (dg_mem_kv,) = _rms_bwd("norm_mem_kv_bwd", (dkv_m, w_kv, NT), mem, g_mem_kv, None, ())
    dh1, dh1_b, dg_mem_q = _rms_bwd("norm_mem_q_bwd", (dq_m, w_q, NT), h1, g_mem_q, dh2, ("f32", "bf16"))

    gw_out = _wgrad("gw_out", mixed, dh1_b)
    dua, dub, dgates, do_a, do_b = _mixer_bwd(dh1_b, w_out, ua, ub, proj, 6 * nb, w_up_a, w_up_b)
    gw_ua = _wgrad("gw_up_a", o_a, dua)
    gw_ub = _wgrad("gw_up_b", o_b, dub)
    (dq_ar, dk_ar, dv_a), (p_fg, p_fd) = _da_bwd(
        qk_a, proj, 2 * nb, bias, o_a, lse_a, do_a, batch, s,
        ride=([gw_fg.reshape(n_sh, -1, d), gw_fd.reshape(n_sh, -1, d)], False, (0, 0)))
    mid = [gw_ua, gw_ub, gw_out.reshape(n_sh, -1, d), gw_q.reshape(n_sh, -1, gw_q.shape[1]),
           gw_kv.reshape(n_sh, -1, gw_kv.shape[1]), gw_o, gw_fu.reshape(n_sh, -1, d)]
    (dq_b, dk_b, dv_b), (*p_mid, p_fu) = _sb_bwd(proj, 3 * nb, 4 * nb, 5 * nb, tot_b, do_b, batch, s,
                                                 ride=(mid, False, (cs_up, cs_up, 0, 0, 0, cs_up, 0)))
    p_ffn = [p_fg, p_fu, p_fd]
    dproj = _rope_apply("rope_bwd", [dq_ar, dk_ar], width, cos_t, sin_a, sin_b, -1.0,
                        tail=(dv_a, dq_b, dk_b, dv_b, dgates))
    grad_x, dg_mix = _rms_bwd("proj_in_bwd", (dproj, w_in, NT), x, g_mix, dh1, ("f32",))
    p_in = _gw_in_scatter(n1, dproj)
    return loss_part, grad_x, [p_in] + list(p_mid) + p_ffn, (dg_mix, dg_mem_q, dg_mem_kv, dg_ffn, dg_final)


WEIGHTS =("w_in", "w_up_a", "w_up_b", "w_out", "w_q_mem", "w_kv_mem", "w_o_mem", "w_ffn_gate", "w_ffn_up", "w_ffn_down")
GAINS = ("g_mix", "g_mem_q", "g_mem_kv", "g_ffn", "g_final")
ORDER = ("g_mix", "w_in", "w_up_a", "w_up_b", "w_out", "g_mem_q", "g_mem_kv", "w_q_mem", "w_kv_mem", "w_o_mem", "g_ffn",
         "w_ffn_gate", "w_ffn_up", "w_ffn_down", "g_final")


def kernel(x, mem, positions, g_mix, w_in, w_up_a, w_up_b, w_out, g_mem_q, g_mem_kv, w_q_mem, w_kv_mem, w_o_mem, g_ffn, w_ffn_gate, w_ffn_up, w_ffn_down, g_final, loss_target, m_g_mix, m_w_in, m_w_up_a, m_w_up_b, m_w_out, m_g_mem_q, m_g_mem_kv, m_w_q_mem, m_w_kv_mem, m_w_o_mem, m_g_ffn, m_w_ffn_gate, m_w_ffn_up, m_w_ffn_down, m_g_final, v_g_mix, v_w_in, v_w_up_a, v_w_up_b, v_w_out, v_g_mem_q, v_g_mem_kv, v_w_q_mem, v_w_kv_mem, v_w_o_mem, v_g_ffn, v_w_ffn_gate, v_w_ffn_up, v_w_ffn_down, v_g_final):
    given = dict(locals())
    batch, s, d = x.shape
    t = batch * s
    flipped = ("w_ffn_gate", "w_ffn_up")

    def view(a, n):
        a = a.reshape(a.shape[-2:])
        return a.T if n in flipped else a

    def unview(a, n):
        return (a.T if n in flipped else a).reshape(given[n].shape)

    shard = {n: view(given[n], n) for n in WEIGHTS}
    gains = [given[n].reshape(1, d) for n in GAINS]

    pad = (-shard["w_ffn_down"].shape[0]) % LANES
    cast = _cast_weights([shard[n] for n in WEIGHTS], [pad if n in flipped + ("w_ffn_down",) else 0 for n in WEIGHTS])
    loss_part, grad_x, pieces, dgains = _local_step(
        x.reshape(t, d), mem.reshape(-1, d), positions.reshape(t, 1), loss_target.reshape(t, d), gains, cast[0],
        cast[1:], batch)

    grad, delta, new_m, new_v = {}, {}, {}, {}
    for n, p in zip(WEIGHTS, pieces):
        outs = _adam("adam_" + n, p, shard[n], view(given["m_" + n], n), view(given["v_" + n], n))
        grad[n], delta[n], new_m[n], new_v[n] = [unview(o, n) for o in outs]

    g_all = _allreduce_small(list(dgains), loss_part)
    small = _adam_small(g_all, gains, [given["m_" + n].reshape(1, d) for n in GAINS],
                        [given["v_" + n].reshape(1, d) for n in GAINS])
    for out, vals in zip((grad, delta, new_m, new_v), small):
        for n, val in zip(GAINS, vals):
            out[n] = val.reshape(given[n].shape)

    loss = g_all[SMALL_ROWS - 1, 0]
    return (loss, grad_x.reshape(x.shape), *[grad[n] for n in ORDER], *[delta[n] for n in ORDER],
            *[new_m[n] for n in ORDER], *[new_v[n] for n in ORDER])
```

```python
import functools
import math

import jax
import jax.numpy as jnp
import numpy as np
from jax import lax
from jax.experimental import pallas as pl
from jax.experimental.pallas import tpu as pltpu

F32 = jnp.float32
BF16 = jnp.bfloat16

N_DEV = 8
HEAD_DIM = 64
MEM_HEAD_DIM = 128
N_HEADS_MEM = 4
BLOCK = 128
DIL_PATTERNS = ((128, 1), (512, 4), (2048, 16))
ROPE_THETA = 500000.0
ROPE_HALF = 8
RMS_EPS = 1e-6
ADAM_LR, ADAM_B1, ADAM_B2, ADAM_EPS, ADAM_WD, ADAM_STEP = 0.001, 0.9, 0.999, 1e-08, 0.01, 10
NEG = -1e30
ROW_TILE = 512
LANES = 128

ANY = pl.BlockSpec(memory_space=pl.ANY)
VMEM = pl.BlockSpec(memory_space=pltpu.VMEM)
NN = (((1,), (0,)), ((), ()))
NT = (((1,), (1,)), ((), ()))
TN = (((0,), (0,)), ((), ()))


def _params(sem):
    return pltpu.CompilerParams(dimension_semantics=sem)


def _mm(name, a, b, *, grid, a_spec, b_spec, o_shape, o_spec, dims, out_dtype, nk=1):
    def body(*refs):
        a_ref, b_ref, o_ref = refs[0], refs[1], refs[2]
        p = lax.dot_general(a_ref[...], b_ref[...], dims, preferred_element_type=F32)
        if nk == 1:
            o_ref[...] = p.astype(out_dtype)
            return
        acc_ref = refs[-1]
        k = pl.program_id(len(grid) - 1)

        @pl.when(k == 0)
        def _():
            acc_ref[...] = p

        @pl.when(k > 0)
        def _():
            acc_ref[...] += p

        @pl.when(k == nk - 1)
        def _():
            o_ref[...] = acc_ref[...].astype(out_dtype)

    o_block = tuple(d for d in o_spec.block_shape if d is not None)
    sem = ("parallel",) * (len(grid) - 1) + (("arbitrary",) if nk > 1 else ("parallel",))
    return pl.pallas_call(
        body, name=name, grid=grid, in_specs=[a_spec, b_spec],
        out_specs=o_spec, out_shape=jax.ShapeDtypeStruct(o_shape, out_dtype),
        scratch_shapes=[pltpu.VMEM(o_block, F32)] if nk > 1 else [],
        compiler_params=_params(sem),
    )(a, b)


def _rms_fwd(name, x, g):
    t, d = x.shape
    tm = min(ROW_TILE, t)

    def body(x_ref, g_ref, o_ref):
        xf = x_ref[...]
        r = lax.rsqrt(jnp.mean(xf * xf, axis=-1, keepdims=True) + RMS_EPS)
        o_ref[...] = (xf * r * g_ref[...]).astype(BF16)

    return pl.pallas_call(
        body, name=name, grid=(t // tm,),
        in_specs=[pl.BlockSpec((tm, d), lambda i: (i, 0)), pl.BlockSpec((1, d), lambda i: (0, 0))],
        out_specs=pl.BlockSpec((tm, d), lambda i: (i, 0)), out_shape=jax.ShapeDtypeStruct((t, d), BF16),
        compiler_params=_params(("parallel",)),
    )(x, g)


def _rms_bwd_rows(dnf, xf, gv, res):
    r = lax.rsqrt(jnp.mean(xf * xf, axis=-1, keepdims=True) + RMS_EPS)
    xh = xf * r
    dxh = dnf * gv
    dx = r * (dxh - xh * jnp.mean(dxh * xh, axis=-1, keepdims=True))
    if res is not None:
        dx = dx + res
    return dx, jnp.sum(dnf * xh, axis=0, keepdims=True)


def _rms_bwd(name, dn, x, g, dres, want):
    t, d = x.shape
    tm = min(ROW_TILE, t)
    has_res = dres is not None
    lhs = list(dn) if isinstance(dn, tuple) else [dn]
    n_lhs = len(lhs[:2])

    def body(*refs):
        x_ref, g_ref = refs[n_lhs], refs[n_lhs + 1]
        r_ref = refs[n_lhs + 2] if has_res else None
        dx_refs, dg_ref = refs[-1 - len(want):-1], refs[-1]
        if n_lhs == 2:
            dnf = lax.dot_general(refs[0][...], refs[1][...], lhs[2], preferred_element_type=F32)
        else:
            dnf = refs[0][...].astype(F32)
        dx, dg = _rms_bwd_rows(dnf, x_ref[...], g_ref[...], r_ref[...] if has_res else None)
        for kind, dx_ref in zip(want, dx_refs):
            dx_ref[...] = dx.astype(F32 if kind == "f32" else BF16)

        @pl.when(pl.program_id(0) == 0)
        def _():
            dg_ref[...] = jnp.zeros_like(dg_ref)

        dg_ref[...] += dg

    row = pl.BlockSpec((tm, d), lambda i: (i, 0))
    vec = pl.BlockSpec((1, d), lambda i: (0, 0))
    if n_lhs == 2:
        first = [pl.BlockSpec((tm, lhs[0].shape[1]), lambda i: (i, 0)), pl.BlockSpec(lhs[1].shape, lambda i: (0, 0))]
    else:
        first = [row]
    return pl.pallas_call(
        body, name=name, grid=(t // tm,),
        in_specs=first + [row, vec] + ([row] if has_res else []),
        out_specs=[row] * len(want) + [vec],
        out_shape=[jax.ShapeDtypeStruct((t, d), F32 if kind == "f32" else BF16) for kind in want]
        + [jax.ShapeDtypeStruct((1, d), F32)],
        compiler_params=_params(("arbitrary",)),
    )(*(lhs[:2] + [x, g] + ([dres] if has_res else [])))


def _loss_head(a, w, res, tgt, g):
    t, d = res.shape
    k = a.shape[1]
    tm = min(ROW_TILE, t)

    def body(a_ref, w_ref, r_ref, t_ref, g_ref, loss_ref, dh_ref, dhb_ref, dg_ref):
        xf = lax.dot_general(a_ref[...], w_ref[...], NN, preferred_element_type=F32) + r_ref[...]
        gv = g_ref[...]
        r = lax.rsqrt(jnp.mean(xf * xf, axis=-1, keepdims=True) + RMS_EPS)
        xh = xf * r
        e = xh * gv - t_ref[...]
        dy = e * (1.0 / d)
        dxh = dy * gv
        dh = r * (dxh - xh * jnp.mean(dxh * xh, axis=-1, keepdims=True))
        dh_ref[...] = dh
        dhb_ref[...] = dh.astype(BF16)

        @pl.when(pl.program_id(0) == 0)
        def _():
            dg_ref[...] = jnp.zeros_like(dg_ref)
            loss_ref[...] = jnp.zeros_like(loss_ref)

        dg_ref[...] += jnp.sum(dy * xh, axis=0, keepdims=True)
        part = jnp.sum(jnp.sum(e * e, axis=1, keepdims=True), axis=0, keepdims=True) * (0.5 / d)
        loss_ref[...] += jnp.broadcast_to(part, loss_ref.shape)

    row = pl.BlockSpec((tm, d), lambda i: (i, 0))
    vec = pl.BlockSpec((1, d), lambda i: (0, 0))
    return pl.pallas_call(
        body, name="loss_head", grid=(t // tm,),
        in_specs=[pl.BlockSpec((tm, k), lambda i: (i, 0)), pl.BlockSpec((k, d), lambda i: (0, 0)), row, row, vec],
        out_specs=[pl.BlockSpec((8, LANES), lambda i: (0, 0)), row, row, vec],
        out_shape=[jax.ShapeDtypeStruct((8, LANES), F32), jax.ShapeDtypeStruct((t, d), F32),
                   jax.ShapeDtypeStruct((t, d), BF16), jax.ShapeDtypeStruct((1, d), F32)],
        compiler_params=_params(("arbitrary",)),
    )(a, w, res, tgt, g)


def _rope_tables(pos, inv_freq, sel_lo, sel_hi):
    t = pos.shape[0]
    tm = min(ROW_TILE, t)

    def body(p_ref, f_ref, lo_ref, hi_ref, c_ref, sa_ref, sb_ref):
        ang = p_ref[...].astype(F32) * f_ref[...]
        rot = lo_ref[...] + hi_ref[...]
        cs, sn = jnp.cos(ang), jnp.sin(ang)
        c_ref[...] = cs * rot + (1.0 - rot)
        sa_ref[...] = -sn * lo_ref[...]
        sb_ref[...] = sn * hi_ref[...]

    vec = pl.BlockSpec((1, LANES), lambda i: (0, 0))
    row = pl.BlockSpec((tm, LANES), lambda i: (i, 0))
    return pl.pallas_call(
        body, name="rope_tables", grid=(t // tm,),
        in_specs=[pl.BlockSpec((tm, 1), lambda i: (i, 0)), vec, vec, vec],
        out_specs=[row, row, row], out_shape=[jax.ShapeDtypeStruct((t, LANES), F32)] * 3,
        compiler_params=_params(("parallel",)),
    )(pos, inv_freq, sel_lo, sel_hi)


def _rope_apply(name, srcs, width, cos_t, sin_a, sin_b, sign, tail=()):
    t = srcs[0].shape[0]
    tm = min(ROW_TILE, t)
    n_cols = width // LANES
    n_src = len(srcs)

    def body(*refs):
        x_refs, tail_refs = refs[:n_src], refs[n_src:n_src + len(tail)]
        c_ref, sa_ref, sb_ref, o_ref = refs[n_src + len(tail):]
        cs, sa, sb = c_ref[...], sign * sa_ref[...], sign * sb_ref[...]
        for a, x_ref in enumerate(x_refs):
            for c in range(n_cols):
                xf = x_ref[:, c * LANES:(c + 1) * LANES].astype(F32)
                up = pltpu.roll(xf, LANES - ROPE_HALF, 1)
                dn = pltpu.roll(xf, ROPE_HALF, 1)
                o_ref[:, a * width + c * LANES:a * width + (c + 1) * LANES] = (xf * cs + up * sa + dn * sb).astype(BF16)
        col = n_src * width
        for t_ref in tail_refs:
            o_ref[:, col:col + t_ref.shape[1]] = t_ref[...]
            col += t_ref.shape[1]

    wide = n_src * width + sum(a.shape[1] for a in tail)
    tab = pl.BlockSpec((tm, LANES), lambda i: (i, 0))
    return pl.pallas_call(
        body, name=name, grid=(t // tm,),
        in_specs=[pl.BlockSpec((tm, width), lambda i: (i, 0))] * n_src
        + [pl.BlockSpec((tm, a.shape[1]), lambda i: (i, 0)) for a in tail] + [tab, tab, tab],
        out_specs=pl.BlockSpec((tm, wide), lambda i: (i, 0)),
        out_shape=jax.ShapeDtypeStruct((t, wide), BF16),
        compiler_params=_params(("parallel",)),
    )(*srcs, *tail, cos_t, sin_a, sin_b)


DA_T = 256
MIX_STREAMS = 4
SB_BWD_STREAMS = 2


def _lane_lo():
    return lax.broadcasted_iota(jnp.int32, (BLOCK, LANES), 1) < HEAD_DIM


def _dilated_bias_tiles(s):
    n = s // DA_T
    dist = (np.arange(n)[:, None, None] * DA_T + np.arange(DA_T)[None, :, None] - np.arange(DA_T)[None, None, :])
    cnt = np.zeros(dist.shape, np.float32)
    for window, dil in DIL_PATTERNS:
        cnt += ((dist >= 0) & (dist % dil == 0) & (dist <= window)).astype(np.float32)
    return jnp.asarray(np.where(cnt > 0, np.log(np.maximum(cnt, 1.0)), NEG).astype(np.float32))


def _stack_heads(x, lo):
    zero = jnp.zeros_like(x)
    return jnp.concatenate([jnp.where(lo, x, zero), jnp.where(lo, zero, x)], axis=0)


def _da_fwd(qk, proj, v_col0, bias, batch, s, ride=None, streams=MIX_STREAMS):
    t = qk.shape[0]
    nq = s // DA_T
    n_pairs = 4
    ns = streams
    wide = ns * LANES
    scale = HEAD_DIM ** -0.5

    def body(q_ref, k_ref, v_ref, b_ref, o_ref, lse_ref, acc_ref, m_ref, l_ref):
        i = pl.program_id(2)
        lo = lax.broadcasted_iota(jnp.int32, (DA_T, LANES), 1) < HEAD_DIM
        ones = jnp.ones((DA_T, LANES), BF16)
        acc_ref[...] = jnp.zeros_like(acc_ref)
        m_ref[...] = jnp.full(m_ref.shape, NEG, F32)
        l_ref[...] = jnp.zeros_like(l_ref)
        qqs = [_stack_heads(q_ref[:, st * LANES:(st + 1) * LANES] * scale, lo) for st in range(ns)]

        def scores(st, rows, bias2):
            k = k_ref[rows, st * LANES:(st + 1) * LANES]
            return lax.dot_general(qqs[st], k, NT, preferred_element_type=F32) + bias2

        def softmax(st, sc):
            m_old = m_ref[st]
            m_new = jnp.maximum(m_old, jnp.broadcast_to(jnp.max(sc, axis=1, keepdims=True), m_old.shape))
            m_ref[st] = m_new
            return jnp.exp(sc - jnp.concatenate([m_new, m_new], axis=1)).astype(BF16), jnp.exp(m_old - m_new)

        def values(st, rows, p, alpha):
            v = v_ref[rows, st * LANES:(st + 1) * LANES]
            vz = jnp.zeros_like(v)
            l_ref[st] = alpha * l_ref[st] + lax.dot_general(p, ones, NN, preferred_element_type=F32)
            pv = (lax.dot_general(p[:DA_T], jnp.where(lo, v, vz), NN, preferred_element_type=F32)
                  + lax.dot_general(p[DA_T:], jnp.where(lo, vz, v), NN, preferred_element_type=F32))
            acc_ref[st] = acc_ref[st] * jnp.where(lo, alpha[:DA_T], alpha[DA_T:]) + pv

        def trip(dlt, carry):
            rows = pl.ds(pl.multiple_of((i - dlt) * DA_T, DA_T), DA_T)
            bias_t = b_ref[dlt]
            bias2 = jnp.concatenate([bias_t, bias_t], axis=0)
            scs = [scores(st, rows, bias2) for st in range(ns)]
            pas = [softmax(st, scs[st]) for st in range(ns)]
            for st in range(ns):
                values(st, rows, *pas[st])
            return carry

        lax.fori_loop(0, i + 1, trip, 0)
        for st in range(ns):
            cols = slice(st * LANES, (st + 1) * LANES)
            l_t = l_ref[st]
            o_ref[:, cols] = (acc_ref[st] / jnp.where(lo, l_t[:DA_T], l_t[DA_T:])).astype(BF16)
            lse = m_ref[st] + jnp.log(l_t)
            lse_ref[:, cols] = jnp.where(lo, lse[:DA_T], lse[DA_T:])

    blk = pl.BlockSpec((DA_T, wide), lambda b, h, i: (b * nq + i, h))
    return _call(
        body, name="attn_a_fwd", grid=(batch, n_pairs // ns, nq),
        in_specs=[blk,
                  pl.BlockSpec((s, wide), lambda b, h, i: (b, n_pairs // ns + h)),
                  pl.BlockSpec((s, wide), lambda b, h, i: (b, v_col0 // ns + h)),
                  pl.BlockSpec((nq, DA_T, DA_T), lambda b, h, i: (0, 0, 0))],
        out_specs=[blk, blk],
        out_shape=[jax.ShapeDtypeStruct((t, n_pairs * LANES), BF16), jax.ShapeDtypeStruct((t, n_pairs * LANES), F32)],
        scratch=[pltpu.VMEM((ns, DA_T, LANES), F32), pltpu.VMEM((ns, 2 * DA_T, LANES), F32),
                 pltpu.VMEM((ns, 2 * DA_T, LANES), F32)],
        sem=("parallel", "parallel", "arbitrary"), args=(qk, qk, proj, bias), ride=ride)


def _da_bwd(qk, proj, v_col0, bias, o, lse, do, batch, s, ride=None, streams=MIX_STREAMS):
    t = qk.shape[0]
    nq = s // DA_T
    n_pairs = 4
    ns = streams
    wide = ns * LANES
    scale = HEAD_DIM ** -0.5

    def body(q_ref, k_ref, v_ref, b_ref, o_ref, lse_ref, do_ref, dq_ref, dk_ref, dv_ref, dk_acc, dv_acc, dq_acc):
        i = pl.program_id(2)
        lo = lax.broadcasted_iota(jnp.int32, (DA_T, LANES), 1) < HEAD_DIM

        @pl.when(i == 0)
        def _():
            dk_acc[...] = jnp.zeros_like(dk_acc)
            dv_acc[...] = jnp.zeros_like(dv_acc)

        dq_acc[...] = jnp.zeros_like(dq_acc)
        qqs, dds, deltas, lses = [], [], [], []
        for st in range(ns):
            cols = slice(st * LANES, (st + 1) * LANES)
            do_ = do_ref[:, cols]
            qqs.append(_stack_heads(q_ref[:, cols] * scale, lo))
            dds.append(_stack_heads(do_, lo))
            prod = do_.astype(F32) * o_ref[:, cols].astype(F32)
            fz = jnp.zeros_like(prod)
            deltas.append(jnp.concatenate([jnp.sum(jnp.where(lo, prod, fz), axis=1, keepdims=True),
                                           jnp.sum(jnp.where(lo, fz, prod), axis=1, keepdims=True)], axis=0))
            lse_t = lse_ref[:, cols]
            lses.append(jnp.concatenate([lse_t[:, 0:1], lse_t[:, HEAD_DIM:HEAD_DIM + 1]], axis=0))

        def products(st, rows, bias2):
            cols = slice(st * LANES, (st + 1) * LANES)
            sc = lax.dot_general(qqs[st], k_ref[rows, cols], NT, preferred_element_type=F32) + bias2
            return sc, lax.dot_general(dds[st], v_ref[rows, cols], NT, preferred_element_type=F32)

        def weights(st, sc, dp):
            p = jnp.exp(sc - lses[st])
            return (p * (dp - deltas[st])).astype(BF16), p.astype(BF16)

        def gradients(st, rows, ds, p):
            cols = slice(st * LANES, (st + 1) * LANES)
            k = k_ref[rows, cols]
            kz = jnp.zeros_like(k)
            dq_acc[st] += (lax.dot_general(ds[:DA_T], jnp.where(lo, k, kz), NN, preferred_element_type=F32)
                           + lax.dot_general(ds[DA_T:], jnp.where(lo, kz, k), NN, preferred_element_type=F32))
            dk_acc[rows, cols] += lax.dot_general(ds, qqs[st], TN, preferred_element_type=F32)
            dv_acc[rows, cols] += lax.dot_general(p, dds[st], TN, preferred_element_type=F32)

        def trip(dlt, carry):
            rows = pl.ds(pl.multiple_of((i - dlt) * DA_T, DA_T), DA_T)
            bias_t = b_ref[dlt]
            bias2 = jnp.concatenate([bias_t, bias_t], axis=0)
            prods = [products(st, rows, bias2) for st in range(ns)]
            wts = [weights(st, *prods[st]) for st in range(ns)]
            for st in range(ns):
                gradients(st, rows, *wts[st])
            return carry

        lax.fori_loop(0, i + 1, trip, 0)
        for st in range(ns):
            dq_ref[:, st * LANES:(st + 1) * LANES] = (dq_acc[st] * scale).astype(BF16)

        @pl.when(i == nq - 1)
        def _():
            dk_ref[...] = dk_acc[...].astype(BF16)
            dv_ref[...] = dv_acc[...].astype(BF16)

    blk = pl.BlockSpec((DA_T, wide), lambda b, h, i: (b * nq + i, h))
    seq = pl.BlockSpec((s, wide), lambda b, h, i: (b, h), pipeline_mode=pl.Buffered(1))
    one = pl.Buffered(1)
    out = jax.ShapeDtypeStruct((t, n_pairs * LANES), BF16)
    return _call(
        body, name="attn_a_bwd", grid=(batch, n_pairs // ns, nq),
        in_specs=[blk,
                  pl.BlockSpec((s, wide), lambda b, h, i: (b, n_pairs // ns + h), pipeline_mode=one),
                  pl.BlockSpec((s, wide), lambda b, h, i: (b, v_col0 // ns + h), pipeline_mode=one),
                  pl.BlockSpec((nq, DA_T, DA_T), lambda b, h, i: (0, 0, 0), pipeline_mode=one),
                  blk, blk, blk],
        out_specs=[blk, seq, seq], out_shape=[out, out, out],
        scratch=[pltpu.VMEM((s, wide), F32), pltpu.VMEM((s, wide), F32), pltpu.VMEM((ns, DA_T, LANES), F32)],
        sem=("parallel", "parallel", "arbitrary"), args=(qk, qk, proj, bias, o, lse, do), ride=ride)


SB_Q = 256


def _sb_consts(after):
    r = lax.broadcasted_iota(jnp.int32, (2 * BLOCK, 2 * BLOCK), 0) % BLOCK
    c = lax.broadcasted_iota(jnp.int32, (2 * BLOCK, 2 * BLOCK), 1)
    tri = (r > c) if after else (r < c)
    return jnp.logical_or(c >= BLOCK, tri).astype(BF16)


def _split(x):
    hi = x.astype(BF16)
    lo = (x - hi.astype(F32)).astype(BF16)
    return jnp.concatenate([hi, lo], axis=1)


def _sb_fwd(proj, q_col0, k_col0, v_col0, batch, s, ride=None, streams=MIX_STREAMS):
    t = proj.shape[0]
    nq = s // SB_Q
    n_pairs = 4
    ns = streams
    wide = ns * LANES
    scale = HEAD_DIM ** -0.5

    def body(q_ref, k_ref, v_ref, o_ref, tot_ref, acc_ref, run_ref):
        i = pl.program_id(2)
        lo_q = lax.broadcasted_iota(jnp.int32, (SB_Q, LANES), 1) < HEAD_DIM
        lo_k = _lane_lo()
        mat = _sb_consts(True)
        row = lax.broadcasted_iota(jnp.int32, (2 * SB_Q, LANES), 0) % SB_Q
        ahead = row - lax.broadcasted_iota(jnp.int32, (2 * SB_Q, LANES), 1)
        acc_ref[...] = jnp.zeros_like(acc_ref)
        run_ref[...] = jnp.zeros_like(run_ref)
        qqs = [_stack_heads(q_ref[:, st * LANES:(st + 1) * LANES] * scale, lo_q) for st in range(ns)]

        def units(todo):
            def rows(j):
                return pl.ds(pl.multiple_of(j * BLOCK, BLOCK), BLOCK)

            zs = [lax.dot_general(qqs[st], k_ref[rows(j), st * LANES:(st + 1) * LANES], NT, preferred_element_type=F32)
                  for st, j, _ in todo]
            logs = []
            for z, (_, _, off) in zip(zs, todo):
                lsig = jnp.minimum(z, 0.0) - jnp.log(1.0 + jnp.exp(-jnp.abs(z)))
                lneg = lsig - z
                if off is not None:
                    lneg = jnp.where(ahead > off, lneg, 0.0)
                logs.append((lsig, _split(lneg)))
            sums = [lax.dot_general(cat, mat, NN, preferred_element_type=F32) for _, cat in logs]
            probs = []
            for (lsig, _), sm, (st, _, off) in zip(logs, sums, todo):
                run = run_ref[st]
                a = jnp.exp(lsig + run + sm[:, :BLOCK])
                if off is not None:
                    a = jnp.where(ahead > off, a, 0.0)
                run_ref[st] = run + sm[:, BLOCK:]
                probs.append(a.astype(BF16))
            for ab, (st, j, _) in zip(probs, todo):
                v = v_ref[rows(j), st * LANES:(st + 1) * LANES]
                vz = jnp.zeros_like(v)
                acc_ref[st] += (lax.dot_general(ab[:SB_Q], jnp.where(lo_k, v, vz), NN, preferred_element_type=F32)
                                + lax.dot_general(ab[SB_Q:], jnp.where(lo_k, vz, v), NN, preferred_element_type=F32))

        units([(st, 2 * i + 1, BLOCK) for st in range(ns)] + [(st, 2 * i, 0) for st in range(ns)])

        def pair(p, carry):
            jp = i - 1 - p
            units([(st, 2 * jp + 1, None) for st in range(ns)] + [(st, 2 * jp, None) for st in range(ns)])
            return carry

        lax.fori_loop(0, i, pair, 0)
        for st in range(ns):
            cols = slice(st * LANES, (st + 1) * LANES)
            o_ref[:, cols] = acc_ref[st].astype(BF16)
            tot_ref[:, cols] = jnp.where(lo_q, run_ref[st, 0:SB_Q, :], run_ref[st, SB_Q:2 * SB_Q, :])

    def seq(col0):
        return pl.BlockSpec((s, wide), lambda b, h, i: (b, col0 // ns + h))

    blk = pl.BlockSpec((SB_Q, wide), lambda b, h, i: (b * nq + i, h))
    return _call(
        body, name="attn_b_fwd", grid=(batch, n_pairs // ns, nq),
        in_specs=[pl.BlockSpec((SB_Q, wide), lambda b, h, i: (b * nq + i, q_col0 // ns + h)), seq(k_col0), seq(v_col0)],
        out_specs=[blk, blk],
        out_shape=[jax.ShapeDtypeStruct((t, n_pairs * LANES), BF16), jax.ShapeDtypeStruct((t, n_pairs * LANES), F32)],
        scratch=[pltpu.VMEM((ns, SB_Q, LANES), F32), pltpu.VMEM((ns, 2 * SB_Q, LANES), F32)],
        sem=("parallel", "parallel", "arbitrary"), args=(proj, proj, proj), ride=ride)


def _sb_bwd(proj, q_col0, k_col0, v_col0, tot, do, batch, s, ride=None, streams=SB_BWD_STREAMS):
    t = proj.shape[0]
    nq = s // SB_Q
    n_pairs = 4
    ns = streams
    wide = ns * LANES
    scale = HEAD_DIM ** -0.5

    def body(q_ref, k_ref, v_ref, tot_ref, do_ref, dq_ref, dk_ref, dv_ref, dk_acc, dv_acc, dq_acc, seen_ref, gsum_ref):
        i = pl.program_id(2)
        lo_q = lax.broadcasted_iota(jnp.int32, (SB_Q, LANES), 1) < HEAD_DIM
        lo_k = _lane_lo()

        @pl.when(i == 0)
        def _():
            dk_acc[...] = jnp.zeros_like(dk_acc)
            dv_acc[...] = jnp.zeros_like(dv_acc)

        mat_after = _sb_consts(True)
        mat_before = _sb_consts(False)[:BLOCK]
        row = lax.broadcasted_iota(jnp.int32, (2 * SB_Q, LANES), 0) % SB_Q
        ahead = row - lax.broadcasted_iota(jnp.int32, (2 * SB_Q, LANES), 1)
        dq_acc[...] = jnp.zeros_like(dq_acc)
        seen_ref[...] = jnp.zeros_like(seen_ref)
        gsum_ref[...] = jnp.zeros_like(gsum_ref)
        qqs, dds, totals = [], [], []
        for st in range(ns):
            cols = slice(st * LANES, (st + 1) * LANES)
            qqs.append(_stack_heads(q_ref[:, cols] * scale, lo_q))
            dds.append(_stack_heads(do_ref[:, cols], lo_q))
            tot_t = tot_ref[:, cols]
            totals.append(jnp.concatenate([jnp.broadcast_to(tot_t[:, 0:1], (SB_Q, LANES)),
                                           jnp.broadcast_to(tot_t[:, HEAD_DIM:HEAD_DIM + 1], (SB_Q, LANES))], axis=0))

        def units(todo):
            def rows(j):
                return pl.ds(pl.multiple_of(j * BLOCK, BLOCK), BLOCK)

            def cols(st):
                return slice(st * LANES, (st + 1) * LANES)

            prods = [(lax.dot_general(qqs[st], k_ref[rows(j), cols(st)], NT, preferred_element_type=F32),
                      lax.dot_general(dds[st], v_ref[rows(j), cols(st)], NT, preferred_element_type=F32))
                     for st, j, _ in todo]
            logs = []
            for (z, _), (_, _, off) in zip(prods, todo):
                lsig = jnp.minimum(z, 0.0) - jnp.log(1.0 + jnp.exp(-jnp.abs(z)))
                lneg = lsig - z
                if off is not None:
                    lneg = jnp.where(ahead > off, lneg, 0.0)
                logs.append((lsig, _split(lneg)))
            sums = [lax.dot_general(cat, mat_after, NN, preferred_element_type=F32) for _, cat in logs]
            gates = []
            for (lsig, _), sm, (_, da), (st, _, off) in zip(logs, sums, prods, todo):
                seen = seen_ref[st]
                a = jnp.exp(lsig + (totals[st] - seen - sm[:, BLOCK:]) + sm[:, :BLOCK])
                if off is not None:
                    a = jnp.where(ahead > off, a, 0.0)
                seen_ref[st] = seen + sm[:, BLOCK:]
                g = a * da
                gates.append((a.astype(BF16), g, g.astype(BF16)))
            gsums = [lax.dot_general(cat, mat_before, NN, preferred_element_type=F32) for _, _, cat in gates]
            outs = []
            for (lsig, _), (ab, g, _), gs, (st, _, off) in zip(logs, gates, gsums, todo):
                gsum = gsum_ref[st]
                dz = g - jnp.exp(lsig) * (g + gsum + gs[:, :BLOCK])
                if off is not None:
                    dz = jnp.where(ahead > off, dz, 0.0)
                gsum_ref[st] = gsum + gs[:, BLOCK:]
                outs.append((dz.astype(BF16), ab))
            for (dzb, ab), (st, j, _) in zip(outs, todo):
                k = k_ref[rows(j), cols(st)]
                kz = jnp.zeros_like(k)
                dq_acc[st] += (lax.dot_general(dzb[:SB_Q], jnp.where(lo_k, k, kz), NN, preferred_element_type=F32)
                               + lax.dot_general(dzb[SB_Q:], jnp.where(lo_k, kz, k), NN, preferred_element_type=F32))
                dk_acc[rows(j), cols(st)] += lax.dot_general(dzb, qqs[st], TN, preferred_element_type=F32)
                dv_acc[rows(j), cols(st)] += lax.dot_general(ab, dds[st], TN, preferred_element_type=F32)

        def pair(p, carry):
            units([(st, 2 * p, None) for st in range(ns)] + [(st, 2 * p + 1, None) for st in range(ns)])
            return carry

        lax.fori_loop(0, i, pair, 0)
        units([(st, 2 * i, 0) for st in range(ns)] + [(st, 2 * i + 1, BLOCK) for st in range(ns)])
        for st in range(ns):
            dq_ref[:, st * LANES:(st + 1) * LANES] = (dq_acc[st] * scale).astype(BF16)

        @pl.when(i == nq - 1)
        def _():
            dk_ref[...] = dk_acc[...].astype(BF16)
            dv_ref[...] = dv_acc[...].astype(BF16)

    def seq_in(col0):
        return pl.BlockSpec((s, wide), lambda b, h, i: (b, col0 // ns + h))

    blk = pl.BlockSpec((SB_Q, wide), lambda b, h, i: (b * nq + i, h))
    seq = pl.BlockSpec((s, wide), lambda b, h, i: (b, h))
    out = jax.ShapeDtypeStruct((t, n_pairs * LANES), BF16)
    return _call(
        body, name="attn_b_bwd", grid=(batch, n_pairs // ns, nq),
        in_specs=[pl.BlockSpec((SB_Q, wide), lambda b, h, i: (b * nq + i, q_col0 // ns + h)), seq_in(k_col0),
                  seq_in(v_col0), blk, blk],
        out_specs=[blk, seq, seq], out_shape=[out, out, out],
        scratch=[pltpu.VMEM((s, wide), F32), pltpu.VMEM((s, wide), F32), pltpu.VMEM((ns, SB_Q, LANES), F32),
                 pltpu.VMEM((ns, 2 * SB_Q, LANES), F32), pltpu.VMEM((ns, 2 * SB_Q, LANES), F32)],
        sem=("parallel", "parallel", "arbitrary"), args=(proj, proj, proj, tot, do), ride=ride)


MEM_Q_TILE = 512


def _mem_fwd(q, kv, batch, s, n_mem):
    t, width = q.shape
    tq = min(MEM_Q_TILE, s)
    nq = s // tq
    scale = MEM_HEAD_DIM ** -0.5

    def body(q_ref, kv_ref, o_ref):
        for h in range(N_HEADS_MEM):
            cols = slice(h * MEM_HEAD_DIM, (h + 1) * MEM_HEAD_DIM)
            k = kv_ref[:, cols]
            v = kv_ref[:, width + h * MEM_HEAD_DIM: width + (h + 1) * MEM_HEAD_DIM]
            sc = lax.dot_general(q_ref[:, cols], k, NT, preferred_element_type=F32) * scale
            p = jnp.exp(sc - jnp.max(sc, axis=1, keepdims=True))
            p = p / jnp.sum(p, axis=1, keepdims=True)
            o_ref[:, cols] = lax.dot_general(p.astype(BF16), v, NN, preferred_element_type=F32).astype(BF16)

    return pl.pallas_call(
        body, name="mem_attn_fwd", grid=(batch, nq),
        in_specs=[pl.BlockSpec((tq, width), lambda b, i: (b * nq + i, 0)),
                  pl.BlockSpec((n_mem, 2 * width), lambda b, i: (b, 0))],
        out_specs=pl.BlockSpec((tq, width), lambda b, i: (b * nq + i, 0)),
        out_shape=jax.ShapeDtypeStruct((t, width), BF16),
        compiler_params=_params(("parallel", "parallel")),
    )(q, kv)


def _mem_bwd(q, kv, do, batch, s, n_mem):
    t, width = q.shape
    tq = min(MEM_Q_TILE, s)
    nq = s // tq
    scale = MEM_HEAD_DIM ** -0.5

    def body(q_ref, kv_ref, do_ref, dq_ref, dkv_ref, acc):
        i = pl.program_id(1)

        @pl.when(i == 0)
        def _():
            acc[...] = jnp.zeros_like(acc)

        for h in range(N_HEADS_MEM):
            cols = slice(h * MEM_HEAD_DIM, (h + 1) * MEM_HEAD_DIM)
            vcols = slice(width + h * MEM_HEAD_DIM, width + (h + 1) * MEM_HEAD_DIM)
            qh, k, v, doh = q_ref[:, cols], kv_ref[:, cols], kv_ref[:, vcols], do_ref[:, cols]
            sc = lax.dot_general(qh, k, NT, preferred_element_type=F32) * scale
            p = jnp.exp(sc - jnp.max(sc, axis=1, keepdims=True))
            p = p / jnp.sum(p, axis=1, keepdims=True)
            dp = lax.dot_general(doh, v, NT, preferred_element_type=F32)
            ds = (p * (dp - jnp.sum(p * dp, axis=1, keepdims=True)) * scale).astype(BF16)
            dq_ref[:, cols] = lax.dot_general(ds, k, NN, preferred_element_type=F32).astype(BF16)
            acc[:, cols] += lax.dot_general(ds, qh, TN, preferred_element_type=F32)
            acc[:, vcols] += lax.dot_general(p.astype(BF16), doh, TN, preferred_element_type=F32)

        @pl.when(i == nq - 1)
        def _():
            dkv_ref[...] = acc[...].astype(BF16)

    row = pl.BlockSpec((tq, width), lambda b, i: (b * nq + i, 0))
    kvs = pl.BlockSpec((n_mem, 2 * width), lambda b, i: (b, 0))
    return pl.pallas_call(
        body, name="mem_attn_bwd", grid=(batch, nq),
        in_specs=[row, kvs, row], out_specs=[row, kvs],
        out_shape=[jax.ShapeDtypeStruct((t, width), BF16), jax.ShapeDtypeStruct((batch * n_mem, 2 * width), BF16)],
        scratch_shapes=[pltpu.VMEM((n_mem, 2 * width), F32)],
        compiler_params=_params(("parallel", "arbitrary")),
    )(q, kv, do)


def _mixer_fwd(o_a, o_b, w_a, w_b, proj, gate_col0, w_out, x, g, w_q):
    t, width = o_a.shape
    d = w_a.shape[1]
    nq_cols = w_q.shape[1]
    tm = min(ROW_TILE, t)
    gb0 = gate_col0 * LANES // d

    def body(oa_ref, ob_ref, wa_ref, wb_ref, ga_ref, gb_ref, wo_ref, x_ref, g_ref, wq_ref, ua_ref, ub_ref, mix_ref,
             n_ref, h_ref, q_ref):
        ua = lax.dot_general(oa_ref[...], wa_ref[...], NN, preferred_element_type=F32)
        ub = lax.dot_general(ob_ref[...], wb_ref[...], NN, preferred_element_type=F32)
        ua_ref[...] = ua.astype(BF16)
        ub_ref[...] = ub.astype(BF16)
        mixed = (jax.nn.sigmoid(ga_ref[...].astype(F32)) * ua + jax.nn.sigmoid(gb_ref[...].astype(F32)) * ub).astype(BF16)
        mix_ref[...] = mixed
        h = lax.dot_general(mixed, wo_ref[...], NN, preferred_element_type=F32) + x_ref[...]
        h_ref[...] = h
        r = lax.rsqrt(jnp.mean(h * h, axis=-1, keepdims=True) + RMS_EPS)
        n = (h * r * g_ref[...]).astype(BF16)
        n_ref[...] = n
        q_ref[...] = lax.dot_general(n, wq_ref[...], NN, preferred_element_type=F32).astype(BF16)

    row = pl.BlockSpec((tm, width), lambda i: (i, 0))
    wsp = pl.BlockSpec((width, d), lambda i: (0, 0))
    out = pl.BlockSpec((tm, d), lambda i: (i, 0))
    osh = jax.ShapeDtypeStruct((t, d), BF16)
    return pl.pallas_call(
        body, name="mixer_fwd", grid=(t // tm,),
        in_specs=[row, row, wsp, wsp,
                  pl.BlockSpec((tm, d), lambda i: (i, gb0)), pl.BlockSpec((tm, d), lambda i: (i, gb0 + 1)),
                  pl.BlockSpec((d, d), lambda i: (0, 0)), out, pl.BlockSpec((1, d), lambda i: (0, 0)),
                  pl.BlockSpec((d, nq_cols), lambda i: (0, 0))],
        out_specs=[out, out, out, out, out, pl.BlockSpec((tm, nq_cols), lambda i: (i, 0))],
        out_shape=[osh, osh, osh, osh, jax.ShapeDtypeStruct((t, d), F32), jax.ShapeDtypeStruct((t, nq_cols), BF16)],
        compiler_params=_params(("parallel",)),
    )(o_a, o_b, w_a, w_b, proj, proj, w_out, x, g, w_q)


def _mixer_bwd(dh, w_out, ua, ub, proj, gate_col0, w_a, w_b):
    t, d = dh.shape
    width = w_a.shape[0]
    tm = min(ROW_TILE, t)
    nc = d // LANES

    def body(dh_ref, w_ref, ua_ref, ub_ref, ga_ref, gb_ref, wa_ref, wb_ref, dua_ref, dub_ref, dg_ref, doa_ref, dob_ref):
        dm = lax.dot_general(dh_ref[...], w_ref[...], NT, preferred_element_type=F32)
        sa = jax.nn.sigmoid(ga_ref[...].astype(F32))
        sb = jax.nn.sigmoid(gb_ref[...].astype(F32))
        dua = (dm * sa).astype(BF16)
        dub = (dm * sb).astype(BF16)
        dua_ref[...] = dua
        dub_ref[...] = dub
        dg_ref[:, 0:d] = (dm * ua_ref[...].astype(F32) * sa * (1.0 - sa)).astype(BF16)
        dg_ref[:, d:2 * d] = (dm * ub_ref[...].astype(F32) * sb * (1.0 - sb)).astype(BF16)
        doa_ref[...] = lax.dot_general(dua, wa_ref[...], NT, preferred_element_type=F32).astype(BF16)
        dob_ref[...] = lax.dot_general(dub, wb_ref[...], NT, preferred_element_type=F32).astype(BF16)

    row = pl.BlockSpec((tm, d), lambda i: (i, 0))
    wsp = pl.BlockSpec((width, d), lambda i: (0, 0))
    osp = pl.BlockSpec((tm, width), lambda i: (i, 0))
    return pl.pallas_call(
        body, name="mixer_bwd", grid=(t // tm,),
        in_specs=[row, pl.BlockSpec((d, d), lambda i: (0, 0)), row, row,
                  pl.BlockSpec((tm, d), lambda i: (i, gate_col0 // nc)),
                  pl.BlockSpec((tm, d), lambda i: (i, gate_col0 // nc + 1)), wsp, wsp],
        out_specs=[row, row, pl.BlockSpec((tm, 2 * d), lambda i: (i, 0)), osp, osp],
        out_shape=[jax.ShapeDtypeStruct((t, d), BF16), jax.ShapeDtypeStruct((t, d), BF16),
                   jax.ShapeDtypeStruct((t, 2 * d), BF16), jax.ShapeDtypeStruct((t, width), BF16),
                   jax.ShapeDtypeStruct((t, width), BF16)],
        compiler_params=_params(("parallel",)),
    )(dh, w_out, ua, ub, proj, proj, w_a, w_b)


FFN_COLS = 1024


def _ffn_up(n, w_gate, w_up):
    t, d = n.shape
    hidden = w_gate.shape[0]
    tm = min(ROW_TILE, t)
    tn = min(FFN_COLS, hidden)

    def body(n_ref, wg_ref, wu_ref, hg_ref, hu_ref, act_ref):
        hg = lax.dot_general(n_ref[...], wg_ref[...], NT, preferred_element_type=F32)
        hu = lax.dot_general(n_ref[...], wu_ref[...], NT, preferred_element_type=F32)
        hg_ref[...] = hg.astype(BF16)
        hu_ref[...] = hu.astype(BF16)
        act_ref[...] = (hg * jax.nn.sigmoid(hg) * hu).astype(BF16)

    wsp = pl.BlockSpec((tn, d), lambda j, i: (j, 0))
    out = pl.BlockSpec((tm, tn), lambda j, i: (i, j))
    osh = jax.ShapeDtypeStruct((t, hidden), BF16)
    return pl.pallas_call(
        body, name="ffn_up", grid=(hidden // tn, t // tm),
        in_specs=[pl.BlockSpec((tm, d), lambda j, i: (i, 0)), wsp, wsp],
        out_specs=[out, out, out], out_shape=[osh, osh, osh],
        compiler_params=_params(("parallel", "parallel")),
    )(n, w_gate, w_up)


def _ffn_bwd(dh, w_down, w_gate, w_up, hg, hu, x, g, dres, w_prev):
    t, d = dh.shape
    hidden = w_down.shape[0]
    q = w_prev.shape[0]
    tm = min(ROW_TILE, t)
    tn = min(FFN_COLS, hidden)
    nj = hidden // tn

    def body(dh_ref, wd_ref, wg_ref, wu_ref, hg_ref, hu_ref, x_ref, g_ref, r_ref, wp_ref, dhg_ref, dhu_ref, dx_ref,
             dxb_ref, dg_ref, do_ref, acc):
        j, i = pl.program_id(0), pl.program_id(1)
        dact = lax.dot_general(dh_ref[...], wd_ref[...], NT, preferred_element_type=F32)
        hg = hg_ref[...].astype(F32)
        sg = jax.nn.sigmoid(hg)
        dhu = (dact * hg * sg).astype(BF16)
        dhg = (dact * hu_ref[...].astype(F32) * sg * (1.0 + hg * (1.0 - sg))).astype(BF16)
        dhu_ref[...] = dhu
        dhg_ref[...] = dhg
        part = (lax.dot_general(dhg, wg_ref[...], NN, preferred_element_type=F32)
                + lax.dot_general(dhu, wu_ref[...], NN, preferred_element_type=F32))

        @pl.when(j == 0)
        def _():
            acc[i] = part

        @pl.when(j > 0)
        def _():
            acc[i] += part

        @pl.when(jnp.logical_and(j == 0, i == 0))
        def _():
            dg_ref[...] = jnp.zeros_like(dg_ref)

        @pl.when(j == nj - 1)
        def _():
            dx, dg = _rms_bwd_rows(acc[i], x_ref[...], g_ref[...], r_ref[...])
            dx_ref[...] = dx
            dxb = dx.astype(BF16)
            dxb_ref[...] = dxb
            dg_ref[...] += dg
            do_ref[...] = lax.dot_general(dxb, wp_ref[...], NT, preferred_element_type=F32).astype(BF16)

    hid = pl.BlockSpec((tm, tn), lambda j, i: (i, j))
    wsp = pl.BlockSpec((tn, d), lambda j, i: (j, 0), pipeline_mode=pl.Buffered(1))
    late = pl.BlockSpec((tm, d), lambda j, i: (jnp.where(j == nj - 1, i, 0), 0))
    late_q = pl.BlockSpec((tm, q), lambda j, i: (jnp.where(j == nj - 1, i, 0), 0))
    vec = pl.BlockSpec((1, d), lambda j, i: (0, 0))
    osh = jax.ShapeDtypeStruct((t, hidden), BF16)
    return pl.pallas_call(
        body, name="ffn_bwd", grid=(nj, t // tm),
        in_specs=[pl.BlockSpec((tm, d), lambda j, i: (i, 0)), wsp, wsp, wsp, hid, hid, late, vec, late,
                  pl.BlockSpec((q, d), lambda j, i: (0, 0), pipeline_mode=pl.Buffered(1))],
        out_specs=[hid, hid, late, late, vec, late_q],
        out_shape=[osh, osh, jax.ShapeDtypeStruct((t, d), F32), jax.ShapeDtypeStruct((t, d), BF16),
                   jax.ShapeDtypeStruct((1, d), F32), jax.ShapeDtypeStruct((t, q), BF16)],
        scratch_shapes=[pltpu.VMEM((t // tm, tm, d), F32)],
        compiler_params=_params(("arbitrary", "arbitrary")),
    )(dh, w_down, w_gate, w_up, hg, hu, x, g, dres, w_prev)


MM_ROWS = 1024


def _mm_w(name, a, w, out_dtype, dims=NN):
    t, k = a.shape
    n = w.shape[1] if dims == NN else w.shape[0]
    tm, tn = min(MM_ROWS, t), min(1024, n)
    o_spec = pl.BlockSpec((tm, tn), lambda j, i: (i, j))
    b_spec = pl.BlockSpec((k, tn), lambda j, i: (0, j)) if dims == NN else pl.BlockSpec((tn, k), lambda j, i: (j, 0))
    return _mm(name, a, w, grid=(n // tn, t // tm), a_spec=pl.BlockSpec((tm, k), lambda j, i: (i, 0)), b_spec=b_spec,
               o_shape=(t, n), o_spec=o_spec, dims=dims, out_dtype=out_dtype)


def _mm_res_norm(name, a, w, res, g):
    t, k = a.shape
    d = w.shape[1]
    tm = min(ROW_TILE, t)

    def body(a_ref, w_ref, r_ref, g_ref, h_ref, n_ref):
        h = lax.dot_general(a_ref[...], w_ref[...], NN, preferred_element_type=F32) + r_ref[...]
        h_ref[...] = h
        r = lax.rsqrt(jnp.mean(h * h, axis=-1, keepdims=True) + RMS_EPS)
        n_ref[...] = (h * r * g_ref[...]).astype(BF16)

    row = pl.BlockSpec((tm, d), lambda i: (i, 0))
    return pl.pallas_call(
        body, name=name, grid=(t // tm,),
        in_specs=[pl.BlockSpec((tm, k), lambda i: (i, 0)), pl.BlockSpec((k, d), lambda i: (0, 0)), row,
                  pl.BlockSpec((1, d), lambda i: (0, 0))],
        out_specs=[row, row], out_shape=[jax.ShapeDtypeStruct((t, d), F32), jax.ShapeDtypeStruct((t, d), BF16)],
        compiler_params=_params(("parallel",)),
    )(a, w, res, g)


def _wgrad(name, a, g, tk=1024, tn=1024):
    t, k = a.shape
    n = g.shape[1]
    tm, tk, tn = min(2 * MM_ROWS, t), min(tk, k), min(tn, n)
    return _mm(name, a, g, grid=(k // tk, n // tn, t // tm),
               a_spec=pl.BlockSpec((tm, tk), lambda p, q, r: (r, p)), b_spec=pl.BlockSpec((tm, tn), lambda p, q, r: (r, q)),
               o_shape=(k, n), o_spec=pl.BlockSpec((tk, tn), lambda p, q, r: (p, q)), dims=TN, out_dtype=BF16, nk=t // tm)


def _peers():
    x, y, c = lax.axis_index("x"), lax.axis_index("y"), lax.axis_index("c")
    me = 4 * x + 2 * y + c
    out = []
    for k in range(1, N_DEV):
        kx, ky, kc = (k >> 2) & 1, (k >> 1) & 1, k & 1
        px = 1 - x if kx else x
        py = 1 - y if ky else y
        pc = 1 - c if kc else c
        out.append(((px, py, pc), 4 * px + 2 * py + pc))
    return me, out


def _cast_weights(ws, pad_rows):
    def body(*refs):
        n = len(refs) // 2
        for i_ref, o_ref, pr in zip(refs[:n], refs[n:], pad_rows):
            r, c = i_ref.shape
            o_ref[0:r, :] = i_ref[...].astype(BF16)
            if pr:
                o_ref[r:r + pr, :] = jnp.zeros((pr, c), BF16)

    return pl.pallas_call(
        body, name="cast_weights", in_specs=[VMEM] * len(ws), out_specs=[VMEM] * len(ws),
        out_shape=[jax.ShapeDtypeStruct((w.shape[0] + pr, w.shape[1]), BF16) for w, pr in zip(ws, pad_rows)],
    )(*ws)


def _window(ref, j, c):
    return ref.at[:, pl.ds(pl.multiple_of(j * c, LANES), c)]


def _direct_copies(ins, outs, sems, gather, cols, landed):
    send_sems, recv_sems, loc_sems = sems
    n_peer = N_DEV - 1
    me, peers = _peers()

    def src(w, j):
        if gather:
            return ins[w]
        return _window(ins[w], j, cols[w]) if cols[w] else ins[w].at[j]

    def dst(w, j):
        return _window(outs[w], j, cols[w]) if gather and cols[w] else outs[w].at[j]

    local = [pltpu.make_async_copy(src(w, me), dst(w, me), loc_sems.at[w]) for w in range(len(ins))]
    remote = [pltpu.make_async_remote_copy(
        src_ref=src(w, idx), dst_ref=dst(w, idx if landed else me),
        send_sem=send_sems.at[w * n_peer + k], recv_sem=recv_sems.at[w * n_peer + k],
        device_id=dev, device_id_type=pl.DeviceIdType.MESH)
        for k, (dev, idx) in reversed(list(enumerate(peers))) for w in range(len(ins))]
    return local, remote


OTHER_CHIPS = (2, 4, 6)


def _gather_copies(ins, outs, sems, cols):
    send_sems, recv_sems, loc_sems = sems
    x, y, c = lax.axis_index("x"), lax.axis_index("y"), lax.axis_index("c")
    me = 4 * x + 2 * y + c
    n_pair = N_DEV - 1

    def dev(mask):
        return (1 - x if mask & 4 else x, 1 - y if mask & 2 else y, 1 - c if mask & 1 else c)

    def slot(w, mask):
        j = jnp.bitwise_xor(me, mask)
        return _window(outs[w], j, cols[w]) if cols[w] else outs[w].at[j]

    def remote(w, pair, src, to_slot, target):
        return pltpu.make_async_remote_copy(src_ref=src, dst_ref=slot(w, to_slot), send_sem=send_sems.at[w * n_pair + pair],
                                            recv_sem=recv_sems.at[w * n_pair + pair], device_id=dev(target),
                                            device_id_type=pl.DeviceIdType.MESH)

    ws = range(len(ins))
    return dict(
        local=[pltpu.make_async_copy(ins[w], slot(w, 0), loc_sems.at[w]) for w in ws],
        to_chips=[remote(w, 1 + t, ins[w], 0, m) for t, m in enumerate(OTHER_CHIPS) for w in ws],
        to_core=[remote(w, 0, ins[w], 0, 1) for w in ws],
        from_chips=[remote(w, 1 + t, ins[w], m, 0) for t, m in enumerate(OTHER_CHIPS) for w in ws],
        pass_on=[remote(w, 4 + t, slot(w, m), m, 1) for t, m in enumerate(OTHER_CHIPS) for w in ws],
        from_core=[remote(w, 0, ins[w], 1, 0) for w in ws]
        + [remote(w, 4 + t, ins[w], m + 1, 0) for t, m in enumerate(OTHER_CHIPS) for w in ws])


TWO_LEVEL = "gather in two levels"


def _exchange_start(ins, outs, sems, gather, cols):
    if gather == TWO_LEVEL:
        cps = _gather_copies(ins, outs, sems, cols)
        for cp in cps["local"] + cps["to_chips"] + cps["to_core"]:
            cp.start()
    else:
        local, remote = _direct_copies(ins, outs, sems, gather, cols, False)
        for cp in local + remote:
            cp.start()


def _exchange_pass_on(ins, outs, sems, gather, cols, chips):
    if gather == TWO_LEVEL:
        cps = _gather_copies(ins, outs, sems, cols)
        n = len(ins)
        for t in chips:
            for arrived, onward in zip(cps["from_chips"][t * n:(t + 1) * n], cps["pass_on"][t * n:(t + 1) * n]):
                arrived.wait_recv()
                onward.start()


def _exchange_wait(ins, outs, sems, gather, cols):
    if gather == TWO_LEVEL:
        cps = _gather_copies(ins, outs, sems, cols)
        for cp in cps["local"]:
            cp.wait()
        for cp in cps["to_chips"] + cps["to_core"] + cps["pass_on"]:
            cp.wait_send()
        for cp in cps["from_core"]:
            cp.wait_recv()
    else:
        local, remote = _direct_copies(ins, outs, sems, gather, cols, True)
        for cp in local:
            cp.wait()
        for cp in remote:
            cp.wait_send()
            cp.wait_recv()


def _exchange_shapes(arrs, gather, cols):
    n = len(arrs)
    out_shape = []
    for a, c in zip(arrs, cols):
        if gather:
            shape = (a.shape[0], N_DEV * c) if c else (N_DEV,) + a.shape
        else:
            shape = (N_DEV, a.shape[0], c) if c else a.shape
        out_shape.append(jax.ShapeDtypeStruct(shape, a.dtype))
    sems = [pltpu.SemaphoreType.DMA((n * (N_DEV - 1),)), pltpu.SemaphoreType.DMA((n * (N_DEV - 1),)),
            pltpu.SemaphoreType.DMA((n,))]
    return out_shape, sems


def _call(body, *, name, grid, in_specs, out_specs, out_shape, scratch, sem, args, ride=None):
    if ride is None:
        outs = pl.pallas_call(body, name=name, grid=grid, in_specs=in_specs, out_specs=out_specs, out_shape=out_shape,
                              scratch_shapes=scratch, compiler_params=_params(sem))(*args)
        return outs, None
    arrs, gather, cols = ride
    n, n_in, n_out, n_scr = len(arrs), len(in_specs), len(out_specs), len(scratch)
    x_shape, x_sems = _exchange_shapes(arrs, gather, cols)

    def riding(*refs):
        ins, x_ins = refs[:n_in], refs[n_in:n_in + n]
        outs = refs[n_in + n:n_in + n + n_out]
        x_outs = refs[n_in + n + n_out:n_in + 2 * n + n_out]
        scr = refs[n_in + 2 * n + n_out:n_in + 2 * n + n_out + n_scr]
        sems = refs[n_in + 2 * n + n_out + n_scr:]
        def at(step):
            return functools.reduce(jnp.logical_and, [pl.program_id(a) == v for a, v in enumerate(step)])

        @pl.when(at((0,) * len(grid)))
        def _():
            _exchange_start(x_ins, x_outs, sems, gather, cols)

        @pl.when(at((grid[0] // 2,) + (0,) * (len(grid) - 2) + (grid[-1] // 2,)))
        def _():
            _exchange_pass_on(x_ins, x_outs, sems, gather, cols, (0, 1))

        @pl.when(at((grid[0] // 2,) + (0,) * (len(grid) - 2) + (3 * grid[-1] // 4,)))
        def _():
            _exchange_pass_on(x_ins, x_outs, sems, gather, cols, (2,))

        body(*ins, *outs, *scr)

        @pl.when(at(tuple(g - 1 for g in grid)))
        def _():
            _exchange_wait(x_ins, x_outs, sems, gather, cols)

    res = pl.pallas_call(
        riding, name=name, grid=grid, in_specs=list(in_specs) + [ANY] * n, out_specs=list(out_specs) + [ANY] * n,
        out_shape=list(out_shape) + x_shape, scratch_shapes=list(scratch) + x_sems,
        compiler_params=_params(("arbitrary",) * len(grid)))(*args, *arrs)
    return res[:n_out], res[n_out:]


def _my_block():
    return (4 * lax.axis_index("x") + 2 * lax.axis_index("y") + lax.axis_index("c")).astype(jnp.int32).reshape(1)


def _proj_in_gather(x, g, w_shard):
    t, k = x.shape
    cs = w_shard.shape[1]
    tm = min(MM_ROWS, t)
    ni = t // tm
    arrival = (0, 1, 2, 4, 3, 5, 6, 7)

    def mask_at(s):
        return jnp.where(s == 3, 4, jnp.where(s == 4, 3, s))

    def body(me_ref, x_ref, g_ref, w_hbm, o_ref, all_hbm, n_hbm, w_vmem, n_vmem, send_sems, recv_sems, loc_sems,
             load_sems, n_sem):
        s, i = pl.program_id(0), pl.program_id(1)
        cps = _gather_copies([w_hbm], [all_hbm], (send_sems, recv_sems, loc_sems), (cs,))
        by_mask = {0: cps["local"][0], 1: cps["from_core"][0]}
        for t_chip, m in enumerate(OTHER_CHIPS):
            by_mask[m] = cps["from_chips"][t_chip]
            by_mask[m + 1] = cps["from_core"][1 + t_chip]
        arrived = [by_mask[m] for m in arrival]

        def load(step):
            src = w_hbm if step == 0 else _window(all_hbm, jnp.bitwise_xor(me_ref[0], arrival[step]), cs)
            return pltpu.make_async_copy(src, w_vmem.at[step % 2], load_sems.at[step % 2])

        @pl.when(jnp.logical_and(s == 0, i == 0))
        def _():
            for cp in cps["local"] + cps["to_chips"] + cps["to_core"]:
                cp.start()
            load(0).start()

        for step, mask in enumerate(arrival):
            @pl.when(jnp.logical_and(s == step, i == 0))
            def _(step=step):
                load(step).wait()

            if step + 1 < N_DEV:
                @pl.when(jnp.logical_and(s == step, i == min(1, ni - 1)))
                def _(step=step):
                    arrived[step + 1].wait_recv()
                    if arrival[step + 1] in OTHER_CHIPS:
                        cps["pass_on"][OTHER_CHIPS.index(arrival[step + 1])].start()
                    load(step + 1).start()

        @pl.when(s == 0)
        def _():
            xf = x_ref[...]
            r = lax.rsqrt(jnp.mean(xf * xf, axis=-1, keepdims=True) + RMS_EPS)
            n_vmem[i] = (xf * r * g_ref[...]).astype(BF16)
            keep = pltpu.make_async_copy(n_vmem.at[i], n_hbm.at[pl.ds(pl.multiple_of(i * tm, tm), tm), :], n_sem)
            keep.start()
            keep.wait()

        o_ref[...] = lax.dot_general(n_vmem[i], w_vmem[s % 2], NN, preferred_element_type=F32).astype(BF16)

        @pl.when(jnp.logical_and(s == N_DEV - 1, i == ni - 1))
        def _():
            cps["local"][0].wait()
            for cp in cps["to_chips"] + cps["to_core"] + cps["pass_on"]:
                cp.wait_send()

    return pl.pallas_call(
        body, name="proj_in",
        grid_spec=pltpu.PrefetchScalarGridSpec(
            num_scalar_prefetch=1, grid=(N_DEV, ni),
            in_specs=[pl.BlockSpec((tm, k), lambda s, i, me: (jnp.where(s == 0, i, 0), 0)),
                      pl.BlockSpec((1, k), lambda s, i, me: (0, 0)), ANY],
            out_specs=[pl.BlockSpec((tm, cs), lambda s, i, me: (i, jnp.bitwise_xor(me[0], mask_at(s)))), ANY, ANY],
            scratch_shapes=[pltpu.VMEM((2, k, cs), BF16), pltpu.VMEM((ni, tm, k), BF16),
                            pltpu.SemaphoreType.DMA((N_DEV - 1,)), pltpu.SemaphoreType.DMA((N_DEV - 1,)),
                            pltpu.SemaphoreType.DMA((1,)), pltpu.SemaphoreType.DMA((2,)), pltpu.SemaphoreType.DMA]),
        out_shape=[jax.ShapeDtypeStruct((t, N_DEV * cs), BF16), jax.ShapeDtypeStruct((k, N_DEV * cs), BF16),
                   jax.ShapeDtypeStruct((t, k), BF16)],
        compiler_params=_params(("arbitrary", "arbitrary")),
    )(_my_block(), x, g, w_shard)


def _gw_in_scatter(a, g):
    t, k = a.shape
    cs = g.shape[1] // N_DEV
    tm = min(MM_ROWS, t)
    nr = t // tm
    n_chip = N_DEV // 2
    chips = (6, 4, 2, 0)

    def body(me_ref, a_ref, g_ref, out_hbm, acc, stage, other, core_send, core_recv, chip_send, chip_recv, loc_sem):
        s, r = pl.program_id(0), pl.program_id(1)
        x, y, c = lax.axis_index("x"), lax.axis_index("y"), lax.axis_index("c")
        my_chip = 2 * x + y
        part = lax.dot_general(a_ref[...], g_ref[...], TN, preferred_element_type=F32)

        def to_core(m):
            return pltpu.make_async_remote_copy(src_ref=stage.at[0], dst_ref=other.at[m], send_sem=core_send.at[m],
                                                recv_sem=core_recv.at[m], device_id=(x, y, 1 - c),
                                                device_id_type=pl.DeviceIdType.MESH)

        def to_chip(m, landed):
            mask = chips[m]
            there = (1 - x if mask & 4 else x, 1 - y if mask & 2 else y, c)
            slot = (2 * there[0] + there[1]) if landed else my_chip
            return pltpu.make_async_remote_copy(src_ref=stage.at[1], dst_ref=out_hbm.at[slot], send_sem=chip_send.at[m],
                                                recv_sem=chip_recv.at[m], device_id=there,
                                                device_id_type=pl.DeviceIdType.MESH)

        local = pltpu.make_async_copy(stage.at[1], out_hbm.at[my_chip], loc_sem)

        @pl.when(r == 0)
        def _():
            acc[...] = part

        @pl.when(r > 0)
        def _():
            acc[...] += part

        for step in range(N_DEV):
            m = step // 2

            @pl.when(jnp.logical_and(s == step, r == nr - 1))
            def _(step=step, m=m):
                if step % 2 == 0:
                    if m > 0:
                        to_core(m - 1).wait_send()
                    stage[0] = acc[...].astype(BF16)
                    to_core(m).start()
                else:
                    if m > 0:
                        to_chip(m - 1, False).wait_send()
                    to_core(m).wait_recv()
                    stage[1] = (acc[...] + other[m].astype(F32)).astype(BF16)
                    if m < n_chip - 1:
                        to_chip(m, False).start()
                    else:
                        local.start()
                        to_core(m).wait_send()
                        local.wait()
                        for mm in range(n_chip - 1):
                            to_chip(mm, True).wait_recv()

    return pl.pallas_call(
        body, name="gw_in",
        grid_spec=pltpu.PrefetchScalarGridSpec(
            num_scalar_prefetch=1, grid=(N_DEV, nr),
            in_specs=[pl.BlockSpec((tm, k), lambda s, r, me: (r, 0)),
                      pl.BlockSpec((tm, cs), lambda s, r, me: (r, jnp.bitwise_xor(me[0], N_DEV - 1 - s)))],
            out_specs=ANY,
            scratch_shapes=[pltpu.VMEM((k, cs), F32), pltpu.VMEM((2, k, cs), BF16), pltpu.VMEM((n_chip, k, cs), BF16),
                            pltpu.SemaphoreType.DMA((n_chip,)), pltpu.SemaphoreType.DMA((n_chip,)),
                            pltpu.SemaphoreType.DMA((n_chip - 1,)), pltpu.SemaphoreType.DMA((n_chip - 1,)),
                            pltpu.SemaphoreType.DMA]),
        out_shape=jax.ShapeDtypeStruct((n_chip, k, cs), BF16),
        compiler_params=_params(("arbitrary", "arbitrary")),
    )(_my_block(), a, g)


SMALL_ROWS = 8


def _allreduce_small(parts, loss_part):
    n, d = len(parts), parts[0].shape[1]

    def body(*refs):
        part_refs, loss_ref, o_ref = refs[:n], refs[n], refs[n + 1]
        mine_ref, all_ref, send_sems, recv_sems = refs[n + 2:]
        me, peers = _peers()
        mine_ref[...] = jnp.zeros_like(mine_ref)
        for i, p_ref in enumerate(part_refs):
            mine_ref[i:i + 1, :] = p_ref[...]
        mine_ref[SMALL_ROWS - 1:SMALL_ROWS, 0:LANES] = loss_ref[0:1, :]
        all_ref[me] = mine_ref[...]
        for k, (dev, idx) in enumerate(peers):
            pltpu.make_async_remote_copy(src_ref=mine_ref, dst_ref=all_ref.at[me], send_sem=send_sems.at[k],
                                         recv_sem=recv_sems.at[k], device_id=dev,
                                         device_id_type=pl.DeviceIdType.MESH).start()
        for k, (dev, idx) in enumerate(peers):
            cp = pltpu.make_async_remote_copy(src_ref=mine_ref, dst_ref=all_ref.at[idx], send_sem=send_sems.at[k],
                                              recv_sem=recv_sems.at[k], device_id=dev,
                                              device_id_type=pl.DeviceIdType.MESH)
            cp.wait_send()
            cp.wait_recv()
        tot = all_ref[0]
        for dvc in range(1, N_DEV):
            tot = tot + all_ref[dvc]
        o_ref[...] = tot

    return pl.pallas_call(
        body, name="allreduce_small", in_specs=[VMEM] * (n + 1), out_specs=VMEM,
        out_shape=jax.ShapeDtypeStruct((SMALL_ROWS, d), F32),
        scratch_shapes=[pltpu.VMEM((SMALL_ROWS, d), F32), pltpu.VMEM((N_DEV, SMALL_ROWS, d), F32),
                        pltpu.SemaphoreType.DMA((N_DEV - 1,)), pltpu.SemaphoreType.DMA((N_DEV - 1,))],
    )(*parts, loss_part)


def _adam_math(g, w, m, v):
    m_new = ADAM_B1 * m + (1.0 - ADAM_B1) * g
    v_new = ADAM_B2 * v + (1.0 - ADAM_B2) * (g * g)
    m_hat = m_new / (1.0 - ADAM_B1 ** ADAM_STEP)
    v_hat = v_new / (1.0 - ADAM_B2 ** ADAM_STEP)
    delta = -ADAM_LR * (m_hat / (jnp.sqrt(v_hat) + ADAM_EPS) + ADAM_WD * w)
    return delta, m_new, v_new


def _adam(name, pieces, w, m, v):
    r, c = w.shape
    n_piece, _, cp = pieces.shape
    tr = r
    for cand in (256, 176, 128, 64):
        if r % cand == 0 and r > cand:
            tr = cand
            break

    def body(p_ref, w_ref, m_ref, v_ref, g_ref, d_ref, mo_ref, vo_ref):
        g = p_ref[0, :, 0:c].astype(F32)
        for j in range(1, n_piece):
            g = g + p_ref[j, :, 0:c].astype(F32)
        delta, m_new, v_new = _adam_math(g, w_ref[...], m_ref[...], v_ref[...])
        g_ref[...] = g
        d_ref[...] = delta
        mo_ref[...] = m_new
        vo_ref[...] = v_new

    blk = pl.BlockSpec((tr, c), lambda i: (i, 0))
    osh = jax.ShapeDtypeStruct((r, c), F32)
    return pl.pallas_call(
        body, name=name, grid=(r // tr,),
        in_specs=[pl.BlockSpec((n_piece, tr, cp), lambda i: (0, i, 0)), blk, blk, blk],
        out_specs=[blk, blk, blk, blk], out_shape=[osh, osh, osh, osh],
        compiler_params=_params(("parallel",)),
    )(pieces, w, m, v)


def _adam_small(g_all, ws, ms, vs):
    n = len(ws)

    def body(*refs):
        g_ref, ins, outs = refs[0], refs[1:1 + 3 * n], refs[1 + 3 * n:]
        for i in range(n):
            g = g_ref[i:i + 1, :]
            delta, m_new, v_new = _adam_math(g, ins[i][...], ins[n + i][...], ins[2 * n + i][...])
            for kind, val in enumerate((g, delta, m_new, v_new)):
                outs[kind * n + i][...] = val

    osh = jax.ShapeDtypeStruct(ws[0].shape, F32)
    res = pl.pallas_call(body, name="adam_small", in_specs=[VMEM] * (1 + 3 * n), out_specs=[VMEM] * (4 * n),
                         out_shape=[osh] * (4 * n))(g_all, *ws, *ms, *vs)
    return res[:n], res[n:2 * n], res[2 * n:3 * n], res[3 * n:]


def _local_step(x, mem, pos, tgt, gains, w_in_shard, shards, batch):
    g_mix, g_mem_q, g_mem_kv, g_ffn, g_final = gains
    t, d = x.shape
    s = t // batch
    n_mem = mem.shape[0] // batch
    n_sh = N_DEV
    width = shards[0].shape[0]
    nb = width // LANES

    lane = np.arange(LANES) % HEAD_DIM
    sel_lo = (lane < ROPE_HALF).astype(np.float32)[None, :]
    sel_hi = ((lane >= ROPE_HALF) & (lane < 2 * ROPE_HALF)).astype(np.float32)[None, :]
    freqs = np.float32(ROPE_THETA) ** (-np.arange(ROPE_HALF, dtype=np.float32) / np.float32(ROPE_HALF))
    inv_freq = np.where(lane < 2 * ROPE_HALF, freqs[lane % ROPE_HALF], 0.0).astype(np.float32)[None, :]
    cos_t, sin_a, sin_b = _rope_tables(pos, jnp.asarray(inv_freq), jnp.asarray(sel_lo), jnp.asarray(sel_hi))
    bias = _dilated_bias_tiles(s)

    proj, w_in, n1 = _proj_in_gather(x, g_mix, w_in_shard)
    qk_a = _rope_apply("rope_fwd", [proj], 2 * width, cos_t, sin_a, sin_b, 1.0)
    cs_up = shards[0].shape[1]
    (o_a, lse_a), (w_up_a, w_up_b, w_out, w_q, w_kv, w_o) = _da_fwd(
        qk_a, proj, 2 * nb, bias, batch, s, ride=(shards[:6], TWO_LEVEL, (cs_up, cs_up, 0, 0, 0, cs_up)))
    (o_b, tot_b), (w_fg, w_fu, w_fd) = _sb_fwd(proj, 3 * nb, 4 * nb, 5 * nb, batch, s,
                                               ride=(shards[6:], TWO_LEVEL, (0, 0, 0)))
    w_out = w_out.reshape(d, d)
    w_q = w_q.reshape(d, -1)
    w_kv = w_kv.reshape(d, -1)
    w_fd = w_fd.reshape(-1, d)
    w_fg = w_fg.reshape(-1, d)
    w_fu = w_fu.reshape(-1, d)
    ua, ub, mixed, n2, h1, q_m = _mixer_fwd(o_a, o_b, w_up_a, w_up_b, proj, 6 * nb, w_out, x, g_mem_q, w_q)
    mem_n = _rms_fwd("norm_mem_kv", mem, g_mem_kv)
    kv_m = _mm_w("mem_kv", mem_n, w_kv, BF16)
    o_m = _mem_fwd(q_m, kv_m, batch, s, n_mem)
    h2, n3 = _mm_res_norm("mem_out", o_m, w_o, h1, g_ffn)
    hg, hu, act = _ffn_up(n3, w_fg, w_fu)
    loss_part, dh3, dh3_b, dg_final = _loss_head(act, w_fd, h2, tgt, g_final.reshape(1, d))

    dhg, dhu, dh2, dh2_b, dg_ffn, do_m = _ffn_bwd(dh3_b, w_fd, w_fg, w_fu, hg, hu, h2, g_ffn, dh3, w_o)
    gw_fd = _wgrad("gw_ffn_down", act, dh3_b)
    gw_fg = _wgrad("gw_ffn_gate", dhg, n3)
    gw_fu = _wgrad("gw_ffn_up", dhu, n3)

    gw_o = _wgrad("gw_mem_o", o_m, dh2_b)
    dq_m, dkv_m = _mem_bwd(q_m, kv_m, do_m, batch, s, n_mem)
    gw_q = _wgrad("gw_mem_q", n2, dq_m)
    gw_kv = _wgrad("gw_mem_kv", mem_n, dkv_m)
    (dg_mem_kv,) = _rms_bwd("norm_mem_kv_bwd", (dkv_m, w_kv, NT), mem, g_mem_kv, None, ())
    dh1, dh1_b, dg_mem_q = _rms_bwd("norm_mem_q_bwd", (dq_m, w_q, NT), h1, g_mem_q, dh2, ("f32", "bf16"))

    gw_out = _wgrad("gw_out", mixed, dh1_b)
    dua, dub, dgates, do_a, do_b = _mixer_bwd(dh1_b, w_out, ua, ub, proj, 6 * nb, w_up_a, w_up_b)
    gw_ua = _wgrad("gw_up_a", o_a, dua)
    gw_ub = _wgrad("gw_up_b", o_b, dub)
    (dq_ar, dk_ar, dv_a), (p_fg, p_fd) = _da_bwd(
        qk_a, proj, 2 * nb, bias, o_a, lse_a, do_a, batch, s,
        ride=([gw_fg.reshape(n_sh, -1, d), gw_fd.reshape(n_sh, -1, d)], False, (0, 0)))
    mid = [gw_ua, gw_ub, gw_out.reshape(n_sh, -1, d), gw_q.reshape(n_sh, -1, gw_q.shape[1]),
           gw_kv.reshape(n_sh, -1, gw_kv.shape[1]), gw_o, gw_fu.reshape(n_sh, -1, d)]
    (dq_b, dk_b, dv_b), (*p_mid, p_fu) = _sb_bwd(proj, 3 * nb, 4 * nb, 5 * nb, tot_b, do_b, batch, s,
                                                 ride=(mid, False, (cs_up, cs_up, 0, 0, 0, cs_up, 0)))
    p_ffn = [p_fg, p_fu, p_fd]
    dproj = _rope_apply("rope_bwd", [dq_ar, dk_ar], width, cos_t, sin_a, sin_b, -1.0,
                        tail=(dv_a, dq_b, dk_b, dv_b, dgates))
    grad_x, dg_mix = _rms_bwd("proj_in_bwd", (dproj, w_in, NT), x, g_mix, dh1, ("f32",))
    p_in = _gw_in_scatter(n1, dproj)
    return loss_part, grad_x, [p_in] + list(p_mid) + p_ffn, (dg_mix, dg_mem_q, dg_mem_kv, dg_ffn, dg_final)


WEIGHTS =("w_in", "w_up_a", "w_up_b", "w_out", "w_q_mem", "w_kv_mem", "w_o_mem", "w_ffn_gate", "w_ffn_up", "w_ffn_down")
GAINS = ("g_mix", "g_mem_q", "g_mem_kv", "g_ffn", "g_final")
ORDER = ("g_mix", "w_in", "w_up_a", "w_up_b", "w_out", "g_mem_q", "g_mem_kv", "w_q_mem", "w_kv_mem", "w_o_mem", "g_ffn",
         "w_ffn_gate", "w_ffn_up", "w_ffn_down", "g_final")


def kernel(x, mem, positions, g_mix, w_in, w_up_a, w_up_b, w_out, g_mem_q, g_mem_kv, w_q_mem, w_kv_mem, w_o_mem, g_ffn, w_ffn_gate, w_ffn_up, w_ffn_down, g_final, loss_target, m_g_mix, m_w_in, m_w_up_a, m_w_up_b, m_w_out, m_g_mem_q, m_g_mem_kv, m_w_q_mem, m_w_kv_mem, m_w_o_mem, m_g_ffn, m_w_ffn_gate, m_w_ffn_up, m_w_ffn_down, m_g_final, v_g_mix, v_w_in, v_w_up_a, v_w_up_b, v_w_out, v_g_mem_q, v_g_mem_kv, v_w_q_mem, v_w_kv_mem, v_w_o_mem, v_g_ffn, v_w_ffn_gate, v_w_ffn_up, v_w_ffn_down, v_g_final):
    given = dict(locals())
    batch, s, d = x.shape
    t = batch * s
    flipped = ("w_ffn_gate", "w_ffn_up")

    def view(a, n):
        a = a.reshape(a.shape[-2:])
        return a.T if n in flipped else a

    def unview(a, n):
        return (a.T if n in flipped else a).reshape(given[n].shape)

    shard = {n: view(given[n], n) for n in WEIGHTS}
    gains = [given[n].reshape(1, d) for n in GAINS]

    pad = (-shard["w_ffn_down"].shape[0]) % LANES
    cast = _cast_weights([shard[n] for n in WEIGHTS], [pad if n in flipped + ("w_ffn_down",) else 0 for n in WEIGHTS])
    loss_part, grad_x, pieces, dgains = _local_step(
        x.reshape(t, d), mem.reshape(-1, d), positions.reshape(t, 1), loss_target.reshape(t, d), gains, cast[0],
        cast[1:], batch)

    grad, delta, new_m, new_v = {}, {}, {}, {}
    for n, p in zip(WEIGHTS, pieces):
        outs = _adam("adam_" + n, p, shard[n], view(given["m_" + n], n), view(given["v_" + n], n))
        grad[n], delta[n], new_m[n], new_v[n] = [unview(o, n) for o in outs]

    g_all = _allreduce_small(list(dgains), loss_part)
    small = _adam_small(g_all, gains, [given["m_" + n].reshape(1, d) for n in GAINS],
                        [given["v_" + n].reshape(1, d) for n in GAINS])
    for out, vals in zip((grad, delta, new_m, new_v), small):
        for n, val in zip(GAINS, vals):
            out[n] = val.reshape(given[n].shape)

    loss = g_all[SMALL_ROWS - 1, 0]
    return (loss, grad_x.reshape(x.shape), *[grad[n] for n in ORDER], *[delta[n] for n in ORDER],
            *[new_m[n] for n in ORDER], *[new_v[n] for n in ORDER])
```

```python
import functools
import math

import jax
import jax.numpy as jnp
import numpy as np
from jax import lax
from jax.experimental import pallas as pl
from jax.experimental.pallas import tpu as pltpu

F32 = jnp.float32
BF16 = jnp.bfloat16

N_DEV = 8
HEAD_DIM = 64
MEM_HEAD_DIM = 128
N_HEADS_MEM = 4
BLOCK = 128
DIL_PATTERNS = ((128, 1), (512, 4), (2048, 16))
ROPE_THETA = 500000.0
ROPE_HALF = 8
RMS_EPS = 1e-6
ADAM_LR, ADAM_B1, ADAM_B2, ADAM_EPS, ADAM_WD, ADAM_STEP = 0.001, 0.9, 0.999, 1e-08, 0.01, 10
NEG = -1e30
ROW_TILE = 512
LANES = 128

ANY = pl.BlockSpec(memory_space=pl.ANY)
VMEM = pl.BlockSpec(memory_space=pltpu.VMEM)
NN = (((1,), (0,)), ((), ()))
NT = (((1,), (1,)), ((), ()))
TN = (((0,), (0,)), ((), ()))


def _params(sem):
    return pltpu.CompilerParams(dimension_semantics=sem)


def _mm(name, a, b, *, grid, a_spec, b_spec, o_shape, o_spec, dims, out_dtype):
    def body(a_ref, b_ref, o_ref):
        o_ref[...] = lax.dot_general(a_ref[...], b_ref[...], dims, preferred_element_type=F32).astype(out_dtype)

    return pl.pallas_call(
        body, name=name, grid=grid, in_specs=[a_spec, b_spec],
        out_specs=o_spec, out_shape=jax.ShapeDtypeStruct(o_shape, out_dtype),
        compiler_params=_params(("parallel",) * len(grid)),
    )(a, b)


def _rms_fwd(name, x, g):
    t, d = x.shape
    tm = min(ROW_TILE, t)

    def body(x_ref, g_ref, o_ref):
        xf = x_ref[...]
        r = lax.rsqrt(jnp.mean(xf * xf, axis=-1, keepdims=True) + RMS_EPS)
        o_ref[...] = (xf * r * g_ref[...]).astype(BF16)

    return pl.pallas_call(
        body, name=name, grid=(t // tm,),
        in_specs=[pl.BlockSpec((tm, d), lambda i: (i, 0)), pl.BlockSpec((1, d), lambda i: (0, 0))],
        out_specs=pl.BlockSpec((tm, d), lambda i: (i, 0)), out_shape=jax.ShapeDtypeStruct((t, d), BF16),
        compiler_params=_params(("parallel",)),
    )(x, g)


def _rms_bwd_rows(dnf, xf, gv, res):
    r = lax.rsqrt(jnp.mean(xf * xf, axis=-1, keepdims=True) + RMS_EPS)
    xh = xf * r
    dxh = dnf * gv
    dx = r * (dxh - xh * jnp.mean(dxh * xh, axis=-1, keepdims=True))
    if res is not None:
        dx = dx + res
    return dx, jnp.sum(dnf * xh, axis=0, keepdims=True)


def _rms_bwd(name, dn, x, g, dres, want):
    t, d = x.shape
    tm = min(ROW_TILE, t)
    has_res = dres is not None
    lhs = list(dn) if isinstance(dn, tuple) else [dn]
    n_lhs = len(lhs[:2])

    def body(*refs):
        x_ref, g_ref = refs[n_lhs], refs[n_lhs + 1]
        r_ref = refs[n_lhs + 2] if has_res else None
        dx_refs, dg_ref = refs[-1 - len(want):-1], refs[-1]
        if n_lhs == 2:
            dnf = lax.dot_general(refs[0][...], refs[1][...], lhs[2], preferred_element_type=F32)
        else:
            dnf = refs[0][...].astype(F32)
        dx, dg = _rms_bwd_rows(dnf, x_ref[...], g_ref[...], r_ref[...] if has_res else None)
        for kind, dx_ref in zip(want, dx_refs):
            dx_ref[...] = dx.astype(F32 if kind == "f32" else BF16)

        @pl.when(pl.program_id(0) == 0)
        def _():
            dg_ref[...] = jnp.zeros_like(dg_ref)

        dg_ref[...] += dg

    row = pl.BlockSpec((tm, d), lambda i: (i, 0))
    vec = pl.BlockSpec((1, d), lambda i: (0, 0))
    if n_lhs == 2:
        first = [pl.BlockSpec((tm, lhs[0].shape[1]), lambda i: (i, 0)), pl.BlockSpec(lhs[1].shape, lambda i: (0, 0))]
    else:
        first = [row]
    return pl.pallas_call(
        body, name=name, grid=(t // tm,),
        in_specs=first + [row, vec] + ([row] if has_res else []),
        out_specs=[row] * len(want) + [vec],
        out_shape=[jax.ShapeDtypeStruct((t, d), F32 if kind == "f32" else BF16) for kind in want]
        + [jax.ShapeDtypeStruct((1, d), F32)],
        compiler_params=_params(("arbitrary",)),
    )(*(lhs[:2] + [x, g] + ([dres] if has_res else [])))


def _loss_head(a, w, res, tgt, g):
    t, d = res.shape
    k = a.shape[1]
    tm = min(ROW_TILE, t)

    def body(a_ref, w_ref, r_ref, t_ref, g_ref, loss_ref, dh_ref, dhb_ref, dg_ref):
        xf = lax.dot_general(a_ref[...], w_ref[...], NN, preferred_element_type=F32) + r_ref[...]
        gv = g_ref[...]
        r = lax.rsqrt(jnp.mean(xf * xf, axis=-1, keepdims=True) + RMS_EPS)
        xh = xf * r
        e = xh * gv - t_ref[...]
        dy = e * (1.0 / d)
        dxh = dy * gv
        dh = r * (dxh - xh * jnp.mean(dxh * xh, axis=-1, keepdims=True))
        dh_ref[...] = dh
        dhb_ref[...] = dh.astype(BF16)

        @pl.when(pl.program_id(0) == 0)
        def _():
            dg_ref[...] = jnp.zeros_like(dg_ref)
            loss_ref[...] = jnp.zeros_like(loss_ref)

        dg_ref[...] += jnp.sum(dy * xh, axis=0, keepdims=True)
        part = jnp.sum(jnp.sum(e * e, axis=1, keepdims=True), axis=0, keepdims=True) * (0.5 / d)
        loss_ref[...] += jnp.broadcast_to(part, loss_ref.shape)

    row = pl.BlockSpec((tm, d), lambda i: (i, 0))
    vec = pl.BlockSpec((1, d), lambda i: (0, 0))
    return pl.pallas_call(
        body, name="loss_head", grid=(t // tm,),
        in_specs=[pl.BlockSpec((tm, k), lambda i: (i, 0)), pl.BlockSpec((k, d), lambda i: (0, 0)), row, row, vec],
        out_specs=[pl.BlockSpec((8, LANES), lambda i: (0, 0)), row, row, vec],
        out_shape=[jax.ShapeDtypeStruct((8, LANES), F32), jax.ShapeDtypeStruct((t, d), F32),
                   jax.ShapeDtypeStruct((t, d), BF16), jax.ShapeDtypeStruct((1, d), F32)],
        compiler_params=_params(("arbitrary",)),
    )(a, w, res, tgt, g)


def _rope_tables(pos, inv_freq, sel_lo, sel_hi):
    t = pos.shape[0]
    tm = min(ROW_TILE, t)

    def body(p_ref, f_ref, lo_ref, hi_ref, c_ref, sa_ref, sb_ref):
        ang = p_ref[...].astype(F32) * f_ref[...]
        rot = lo_ref[...] + hi_ref[...]
        cs, sn = jnp.cos(ang), jnp.sin(ang)
        c_ref[...] = cs * rot + (1.0 - rot)
        sa_ref[...] = -sn * lo_ref[...]
        sb_ref[...] = sn * hi_ref[...]

    vec = pl.BlockSpec((1, LANES), lambda i: (0, 0))
    row = pl.BlockSpec((tm, LANES), lambda i: (i, 0))
    return pl.pallas_call(
        body, name="rope_tables", grid=(t // tm,),
        in_specs=[pl.BlockSpec((tm, 1), lambda i: (i, 0)), vec, vec, vec],
        out_specs=[row, row, row], out_shape=[jax.ShapeDtypeStruct((t, LANES), F32)] * 3,
        compiler_params=_params(("parallel",)),
    )(pos, inv_freq, sel_lo, sel_hi)


def _rope_apply(name, srcs, width, cos_t, sin_a, sin_b, sign, tail=()):
    t = srcs[0].shape[0]
    tm = min(ROW_TILE, t)
    n_cols = width // LANES
    n_src = len(srcs)

    def body(*refs):
        x_refs, tail_refs = refs[:n_src], refs[n_src:n_src + len(tail)]
        c_ref, sa_ref, sb_ref, o_ref = refs[n_src + len(tail):]
        cs, sa, sb = c_ref[...], sign * sa_ref[...], sign * sb_ref[...]
        for a, x_ref in enumerate(x_refs):
            for c in range(n_cols):
                xf = x_ref[:, c * LANES:(c + 1) * LANES].astype(F32)
                up = pltpu.roll(xf, LANES - ROPE_HALF, 1)
                dn = pltpu.roll(xf, ROPE_HALF, 1)
                o_ref[:, a * width + c * LANES:a * width + (c + 1) * LANES] = (xf * cs + up * sa + dn * sb).astype(BF16)
        col = n_src * width
        for t_ref in tail_refs:
            o_ref[:, col:col + t_ref.shape[1]] = t_ref[...]
            col += t_ref.shape[1]

    wide = n_src * width + sum(a.shape[1] for a in tail)
    tab = pl.BlockSpec((tm, LANES), lambda i: (i, 0))
    return pl.pallas_call(
        body, name=name, grid=(t // tm,),
        in_specs=[pl.BlockSpec((tm, width), lambda i: (i, 0))] * n_src
        + [pl.BlockSpec((tm, a.shape[1]), lambda i: (i, 0)) for a in tail] + [tab, tab, tab],
        out_specs=pl.BlockSpec((tm, wide), lambda i: (i, 0)),
        out_shape=jax.ShapeDtypeStruct((t, wide), BF16),
        compiler_params=_params(("parallel",)),
    )(*srcs, *tail, cos_t, sin_a, sin_b)


DA_T = 256
MIX_STREAMS = 4
SB_BWD_STREAMS = 2


def _lane_lo():
    return lax.broadcasted_iota(jnp.int32, (BLOCK, LANES), 1) < HEAD_DIM


def _dilated_bias_tiles(s):
    n = s // DA_T
    dist = (np.arange(n)[:, None, None] * DA_T + np.arange(DA_T)[None, :, None] - np.arange(DA_T)[None, None, :])
    cnt = np.zeros(dist.shape, np.float32)
    for window, dil in DIL_PATTERNS:
        cnt += ((dist >= 0) & (dist % dil == 0) & (dist <= window)).astype(np.float32)
    return jnp.asarray(np.where(cnt > 0, np.log(np.maximum(cnt, 1.0)), NEG).astype(np.float32))


def _stack_heads(x, lo):
    zero = jnp.zeros_like(x)
    return jnp.concatenate([jnp.where(lo, x, zero), jnp.where(lo, zero, x)], axis=0)


def _da_fwd(qk, proj, v_col0, bias, batch, s, ride=None, streams=MIX_STREAMS):
    t = qk.shape[0]
    nq = s // DA_T
    n_pairs = 4
    ns = streams
    wide = ns * LANES
    scale = HEAD_DIM ** -0.5

    def body(q_ref, k_ref, v_ref, b_ref, o_ref, lse_ref, acc_ref, m_ref, l_ref):
        i = pl.program_id(2)
        lo = lax.broadcasted_iota(jnp.int32, (DA_T, LANES), 1) < HEAD_DIM
        ones = jnp.ones((DA_T, LANES), BF16)
        acc_ref[...] = jnp.zeros_like(acc_ref)
        m_ref[...] = jnp.full(m_ref.shape, NEG, F32)
        l_ref[...] = jnp.zeros_like(l_ref)
        qqs = [_stack_heads(q_ref[:, st * LANES:(st + 1) * LANES] * scale, lo) for st in range(ns)]

        def scores(st, rows, bias2):
            k = k_ref[rows, st * LANES:(st + 1) * LANES]
            return lax.dot_general(qqs[st], k, NT, preferred_element_type=F32) + bias2

        def softmax(st, sc):
            m_old = m_ref[st]
            m_new = jnp.maximum(m_old, jnp.broadcast_to(jnp.max(sc, axis=1, keepdims=True), m_old.shape))
            m_ref[st] = m_new
            return jnp.exp(sc - jnp.concatenate([m_new, m_new], axis=1)).astype(BF16), jnp.exp(m_old - m_new)

        def values(st, rows, p, alpha):
            v = v_ref[rows, st * LANES:(st + 1) * LANES]
            vz = jnp.zeros_like(v)
            l_ref[st] = alpha * l_ref[st] + lax.dot_general(p, ones, NN, preferred_element_type=F32)
            pv = (lax.dot_general(p[:DA_T], jnp.where(lo, v, vz), NN, preferred_element_type=F32)
                  + lax.dot_general(p[DA_T:], jnp.where(lo, vz, v), NN, preferred_element_type=F32))
            acc_ref[st] = acc_ref[st] * jnp.where(lo, alpha[:DA_T], alpha[DA_T:]) + pv

        def trip(dlt, carry):
            rows = pl.ds(pl.multiple_of((i - dlt) * DA_T, DA_T), DA_T)
            bias_t = b_ref[dlt]
            bias2 = jnp.concatenate([bias_t, bias_t], axis=0)
            scs = [scores(st, rows, bias2) for st in range(ns)]
            pas = [softmax(st, scs[st]) for st in range(ns)]
            for st in range(ns):
                values(st, rows, *pas[st])
            return carry

        lax.fori_loop(0, i + 1, trip, 0)
        for st in range(ns):
            cols = slice(st * LANES, (st + 1) * LANES)
            l_t = l_ref[st]
            o_ref[:, cols] = (acc_ref[st] / jnp.where(lo, l_t[:DA_T], l_t[DA_T:])).astype(BF16)
            lse = m_ref[st] + jnp.log(l_t)
            lse_ref[:, cols] = jnp.where(lo, lse[:DA_T], lse[DA_T:])

    blk = pl.BlockSpec((DA_T, wide), lambda b, h, i: (b * nq + i, h))
    return _call(
        body, name="attn_a_fwd", grid=(batch, n_pairs // ns, nq),
        in_specs=[blk,
                  pl.BlockSpec((s, wide), lambda b, h, i: (b, n_pairs // ns + h)),
                  pl.BlockSpec((s, wide), lambda b, h, i: (b, v_col0 // ns + h)),
                  pl.BlockSpec((nq, DA_T, DA_T), lambda b, h, i: (0, 0, 0))],
        out_specs=[blk, blk],
        out_shape=[jax.ShapeDtypeStruct((t, n_pairs * LANES), BF16), jax.ShapeDtypeStruct((t, n_pairs * LANES), F32)],
        scratch=[pltpu.VMEM((ns, DA_T, LANES), F32), pltpu.VMEM((ns, 2 * DA_T, LANES), F32),
                 pltpu.VMEM((ns, 2 * DA_T, LANES), F32)],
        sem=("parallel", "parallel", "arbitrary"), args=(qk, qk, proj, bias), ride=ride)


def _da_bwd(qk, proj, v_col0, bias, o, lse, do, batch, s, ride=None, streams=MIX_STREAMS):
    t = qk.shape[0]
    nq = s // DA_T
    n_pairs = 4
    ns = streams
    wide = ns * LANES
    scale = HEAD_DIM ** -0.5

    def body(q_ref, k_ref, v_ref, b_ref, o_ref, lse_ref, do_ref, dq_ref, dk_ref, dv_ref, dk_acc, dv_acc, dq_acc):
        i = pl.program_id(2)
        lo = lax.broadcasted_iota(jnp.int32, (DA_T, LANES), 1) < HEAD_DIM

        @pl.when(i == 0)
        def _():
            dk_acc[...] = jnp.zeros_like(dk_acc)
            dv_acc[...] = jnp.zeros_like(dv_acc)

        dq_acc[...] = jnp.zeros_like(dq_acc)
        qqs, dds, deltas, lses = [], [], [], []
        for st in range(ns):
            cols = slice(st * LANES, (st + 1) * LANES)
            do_ = do_ref[:, cols]
            qqs.append(_stack_heads(q_ref[:, cols] * scale, lo))
            dds.append(_stack_heads(do_, lo))
            prod = do_.astype(F32) * o_ref[:, cols].astype(F32)
            fz = jnp.zeros_like(prod)
            deltas.append(jnp.concatenate([jnp.sum(jnp.where(lo, prod, fz), axis=1, keepdims=True),
                                           jnp.sum(jnp.where(lo, fz, prod), axis=1, keepdims=True)], axis=0))
            lse_t = lse_ref[:, cols]
            lses.append(jnp.concatenate([lse_t[:, 0:1], lse_t[:, HEAD_DIM:HEAD_DIM + 1]], axis=0))

        def products(st, rows, bias2):
            cols = slice(st * LANES, (st + 1) * LANES)
            sc = lax.dot_general(qqs[st], k_ref[rows, cols], NT, preferred_element_type=F32) + bias2
            return sc, lax.dot_general(dds[st], v_ref[rows, cols], NT, preferred_element_type=F32)

        def weights(st, sc, dp):
            p = jnp.exp(sc - lses[st])
            return (p * (dp - deltas[st])).astype(BF16), p.astype(BF16)

        def gradients(st, rows, ds, p):
            cols = slice(st * LANES, (st + 1) * LANES)
            k = k_ref[rows, cols]
            kz = jnp.zeros_like(k)
            dq_acc[st] += (lax.dot_general(ds[:DA_T], jnp.where(lo, k, kz), NN, preferred_element_type=F32)
                           + lax.dot_general(ds[DA_T:], jnp.where(lo, kz, k), NN, preferred_element_type=F32))
            dk_acc[rows, cols] += lax.dot_general(ds, qqs[st], TN, preferred_element_type=F32)
            dv_acc[rows, cols] += lax.dot_general(p, dds[st], TN, preferred_element_type=F32)

        def trip(dlt, carry):
            rows = pl.ds(pl.multiple_of((i - dlt) * DA_T, DA_T), DA_T)
            bias_t = b_ref[dlt]
            bias2 = jnp.concatenate([bias_t, bias_t], axis=0)
            prods = [products(st, rows, bias2) for st in range(ns)]
            wts = [weights(st, *prods[st]) for st in range(ns)]
            for st in range(ns):
                gradients(st, rows, *wts[st])
            return carry

        lax.fori_loop(0, i + 1, trip, 0)
        for st in range(ns):
            dq_ref[:, st * LANES:(st + 1) * LANES] = (dq_acc[st] * scale).astype(BF16)

        @pl.when(i == nq - 1)
        def _():
            dk_ref[...] = dk_acc[...].astype(BF16)
            dv_ref[...] = dv_acc[...].astype(BF16)

    blk = pl.BlockSpec((DA_T, wide), lambda b, h, i: (b * nq + i, h))
    seq = pl.BlockSpec((s, wide), lambda b, h, i: (b, h), pipeline_mode=pl.Buffered(1))
    one = pl.Buffered(1)
    out = jax.ShapeDtypeStruct((t, n_pairs * LANES), BF16)
    return _call(
        body, name="attn_a_bwd", grid=(batch, n_pairs // ns, nq),
        in_specs=[blk,
                  pl.BlockSpec((s, wide), lambda b, h, i: (b, n_pairs // ns + h), pipeline_mode=one),
                  pl.BlockSpec((s, wide), lambda b, h, i: (b, v_col0 // ns + h), pipeline_mode=one),
                  pl.BlockSpec((nq, DA_T, DA_T), lambda b, h, i: (0, 0, 0), pipeline_mode=one),
                  blk, blk, blk],
        out_specs=[blk, seq, seq], out_shape=[out, out, out],
        scratch=[pltpu.VMEM((s, wide), F32), pltpu.VMEM((s, wide), F32), pltpu.VMEM((ns, DA_T, LANES), F32)],
        sem=("parallel", "parallel", "arbitrary"), args=(qk, qk, proj, bias, o, lse, do), ride=ride)


SB_Q = 256


def _sb_consts(after):
    r = lax.broadcasted_iota(jnp.int32, (2 * BLOCK, 2 * BLOCK), 0) % BLOCK
    c = lax.broadcasted_iota(jnp.int32, (2 * BLOCK, 2 * BLOCK), 1)
    tri = (r > c) if after else (r < c)
    return jnp.logical_or(c >= BLOCK, tri).astype(BF16)


def _split(x):
    hi = x.astype(BF16)
    lo = (x - hi.astype(F32)).astype(BF16)
    return jnp.concatenate([hi, lo], axis=1)


def _sb_fwd(proj, q_col0, k_col0, v_col0, batch, s, ride=None, streams=MIX_STREAMS):
    t = proj.shape[0]
    nq = s // SB_Q
    n_pairs = 4
    ns = streams
    wide = ns * LANES
    scale = HEAD_DIM ** -0.5

    def body(q_ref, k_ref, v_ref, o_ref, tot_ref, acc_ref, run_ref):
        i = pl.program_id(2)
        lo_q = lax.broadcasted_iota(jnp.int32, (SB_Q, LANES), 1) < HEAD_DIM
        lo_k = _lane_lo()
        mat = _sb_consts(True)
        row = lax.broadcasted_iota(jnp.int32, (2 * SB_Q, LANES), 0) % SB_Q
        ahead = row - lax.broadcasted_iota(jnp.int32, (2 * SB_Q, LANES), 1)
        acc_ref[...] = jnp.zeros_like(acc_ref)
        run_ref[...] = jnp.zeros_like(run_ref)
        qqs = [_stack_heads(q_ref[:, st * LANES:(st + 1) * LANES] * scale, lo_q) for st in range(ns)]

        def units(todo):
            def rows(j):
                return pl.ds(pl.multiple_of(j * BLOCK, BLOCK), BLOCK)

            zs = [lax.dot_general(qqs[st], k_ref[rows(j), st * LANES:(st + 1) * LANES], NT, preferred_element_type=F32)
                  for st, j, _ in todo]
            logs = []
            for z, (_, _, off) in zip(zs, todo):
                lsig = jnp.minimum(z, 0.0) - jnp.log(1.0 + jnp.exp(-jnp.abs(z)))
                lneg = lsig - z
                if off is not None:
                    lneg = jnp.where(ahead > off, lneg, 0.0)
                logs.append((lsig, _split(lneg)))
            sums = [lax.dot_general(cat, mat, NN, preferred_element_type=F32) for _, cat in logs]
            probs = []
            for (lsig, _), sm, (st, _, off) in zip(logs, sums, todo):
                run = run_ref[st]
                a = jnp.exp(lsig + run + sm[:, :BLOCK])
                if off is not None:
                    a = jnp.where(ahead > off, a, 0.0)
                run_ref[st] = run + sm[:, BLOCK:]
                probs.append(a.astype(BF16))
            for ab, (st, j, _) in zip(probs, todo):
                v = v_ref[rows(j), st * LANES:(st + 1) * LANES]
                vz = jnp.zeros_like(v)
                acc_ref[st] += (lax.dot_general(ab[:SB_Q], jnp.where(lo_k, v, vz), NN, preferred_element_type=F32)
                                + lax.dot_general(ab[SB_Q:], jnp.where(lo_k, vz, v), NN, preferred_element_type=F32))

        units([(st, 2 * i + 1, BLOCK) for st in range(ns)] + [(st, 2 * i, 0) for st in range(ns)])

        def pair(p, carry):
            jp = i - 1 - p
            units([(st, 2 * jp + 1, None) for st in range(ns)] + [(st, 2 * jp, None) for st in range(ns)])
            return carry

        lax.fori_loop(0, i, pair, 0)
        for st in range(ns):
            cols = slice(st * LANES, (st + 1) * LANES)
            o_ref[:, cols] = acc_ref[st].astype(BF16)
            tot_ref[:, cols] = jnp.where(lo_q, run_ref[st, 0:SB_Q, :], run_ref[st, SB_Q:2 * SB_Q, :])

    def seq(col0):
        return pl.BlockSpec((s, wide), lambda b, h, i: (b, col0 // ns + h))

    blk = pl.BlockSpec((SB_Q, wide), lambda b, h, i: (b * nq + i, h))
    return _call(
        body, name="attn_b_fwd", grid=(batch, n_pairs // ns, nq),
        in_specs=[pl.BlockSpec((SB_Q, wide), lambda b, h, i: (b * nq + i, q_col0 // ns + h)), seq(k_col0), seq(v_col0)],
        out_specs=[blk, blk],
        out_shape=[jax.ShapeDtypeStruct((t, n_pairs * LANES), BF16), jax.ShapeDtypeStruct((t, n_pairs * LANES), F32)],
        scratch=[pltpu.VMEM((ns, SB_Q, LANES), F32), pltpu.VMEM((ns, 2 * SB_Q, LANES), F32)],
        sem=("parallel", "parallel", "arbitrary"), args=(proj, proj, proj), ride=ride)


def _sb_bwd(proj, q_col0, k_col0, v_col0, tot, do, batch, s, ride=None, streams=SB_BWD_STREAMS):
    t = proj.shape[0]
    nq = s // SB_Q
    n_pairs = 4
    ns = streams
    wide = ns * LANES
    scale = HEAD_DIM ** -0.5

    def body(q_ref, k_ref, v_ref, tot_ref, do_ref, dq_ref, dk_ref, dv_ref, dk_acc, dv_acc, dq_acc, seen_ref, gsum_ref):
        i = pl.program_id(2)
        lo_q = lax.broadcasted_iota(jnp.int32, (SB_Q, LANES), 1) < HEAD_DIM
        lo_k = _lane_lo()

        @pl.when(i == 0)
        def _():
            dk_acc[...] = jnp.zeros_like(dk_acc)
            dv_acc[...] = jnp.zeros_like(dv_acc)

        mat_after = _sb_consts(True)
        mat_before = _sb_consts(False)[:BLOCK]
        row = lax.broadcasted_iota(jnp.int32, (2 * SB_Q, LANES), 0) % SB_Q
        ahead = row - lax.broadcasted_iota(jnp.int32, (2 * SB_Q, LANES), 1)
        dq_acc[...] = jnp.zeros_like(dq_acc)
        seen_ref[...] = jnp.zeros_like(seen_ref)
        gsum_ref[...] = jnp.zeros_like(gsum_ref)
        qqs, dds, totals = [], [], []
        for st in range(ns):
            cols = slice(st * LANES, (st + 1) * LANES)
            qqs.append(_stack_heads(q_ref[:, cols] * scale, lo_q))
            dds.append(_stack_heads(do_ref[:, cols], lo_q))
            tot_t = tot_ref[:, cols]
            totals.append(jnp.concatenate([jnp.broadcast_to(tot_t[:, 0:1], (SB_Q, LANES)),
                                           jnp.broadcast_to(tot_t[:, HEAD_DIM:HEAD_DIM + 1], (SB_Q, LANES))], axis=0))

        def units(todo):
            def rows(j):
                return pl.ds(pl.multiple_of(j * BLOCK, BLOCK), BLOCK)

            def cols(st):
                return slice(st * LANES, (st + 1) * LANES)

            prods = [(lax.dot_general(qqs[st], k_ref[rows(j), cols(st)], NT, preferred_element_type=F32),
                      lax.dot_general(dds[st], v_ref[rows(j), cols(st)], NT, preferred_element_type=F32))
                     for st, j, _ in todo]
            logs = []
            for (z, _), (_, _, off) in zip(prods, todo):
                lsig = jnp.minimum(z, 0.0) - jnp.log(1.0 + jnp.exp(-jnp.abs(z)))
                lneg = lsig - z
                if off is not None:
                    lneg = jnp.where(ahead > off, lneg, 0.0)
                logs.append((lsig, _split(lneg)))
            sums = [lax.dot_general(cat, mat_after, NN, preferred_element_type=F32) for _, cat in logs]
            gates = []
            for (lsig, _), sm, (_, da), (st, _, off) in zip(logs, sums, prods, todo):
                seen = seen_ref[st]
                a = jnp.exp(lsig + (totals[st] - seen - sm[:, BLOCK:]) + sm[:, :BLOCK])
                if off is not None:
                    a = jnp.where(ahead > off, a, 0.0)
                seen_ref[st] = seen + sm[:, BLOCK:]
                g = a * da
                gates.append((a.astype(BF16), g, g.astype(BF16)))
            gsums = [lax.dot_general(cat, mat_before, NN, preferred_element_type=F32) for _, _, cat in gates]
            outs = []
            for (lsig, _), (ab, g, _), gs, (st, _, off) in zip(logs, gates, gsums, todo):
                gsum = gsum_ref[st]
                dz = g - jnp.exp(lsig) * (g + gsum + gs[:, :BLOCK])
                if off is not None:
                    dz = jnp.where(ahead > off, dz, 0.0)
                gsum_ref[st] = gsum + gs[:, BLOCK:]
                outs.append((dz.astype(BF16), ab))
            for (dzb, ab), (st, j, _) in zip(outs, todo):
                k = k_ref[rows(j), cols(st)]
                kz = jnp.zeros_like(k)
                dq_acc[st] += (lax.dot_general(dzb[:SB_Q], jnp.where(lo_k, k, kz), NN, preferred_element_type=F32)
                               + lax.dot_general(dzb[SB_Q:], jnp.where(lo_k, kz, k), NN, preferred_element_type=F32))
                dk_acc[rows(j), cols(st)] += lax.dot_general(dzb, qqs[st], TN, preferred_element_type=F32)
                dv_acc[rows(j), cols(st)] += lax.dot_general(ab, dds[st], TN, preferred_element_type=F32)

        def pair(p, carry):
            units([(st, 2 * p, None) for st in range(ns)] + [(st, 2 * p + 1, None) for st in range(ns)])
            return carry

        lax.fori_loop(0, i, pair, 0)
        units([(st, 2 * i, 0) for st in range(ns)] + [(st, 2 * i + 1, BLOCK) for st in range(ns)])
        for st in range(ns):
            dq_ref[:, st * LANES:(st + 1) * LANES] = (dq_acc[st] * scale).astype(BF16)

        @pl.when(i == nq - 1)
        def _():
            dk_ref[...] = dk_acc[...].astype(BF16)
            dv_ref[...] = dv_acc[...].astype(BF16)

    def seq_in(col0):
        return pl.BlockSpec((s, wide), lambda b, h, i: (b, col0 // ns + h))

    blk = pl.BlockSpec((SB_Q, wide), lambda b, h, i: (b * nq + i, h))
    seq = pl.BlockSpec((s, wide), lambda b, h, i: (b, h))
    out = jax.ShapeDtypeStruct((t, n_pairs * LANES), BF16)
    return _call(
        body, name="attn_b_bwd", grid=(batch, n_pairs // ns, nq),
        in_specs=[pl.BlockSpec((SB_Q, wide), lambda b, h, i: (b * nq + i, q_col0 // ns + h)), seq_in(k_col0),
                  seq_in(v_col0), blk, blk],
        out_specs=[blk, seq, seq], out_shape=[out, out, out],
        scratch=[pltpu.VMEM((s, wide), F32), pltpu.VMEM((s, wide), F32), pltpu.VMEM((ns, SB_Q, LANES), F32),
                 pltpu.VMEM((ns, 2 * SB_Q, LANES), F32), pltpu.VMEM((ns, 2 * SB_Q, LANES), F32)],
        sem=("parallel", "parallel", "arbitrary"), args=(proj, proj, proj, tot, do), ride=ride)


MEM_Q_TILE = 512


def _mem_fwd(q, kv, batch, s, n_mem):
    t, width = q.shape
    tq = min(MEM_Q_TILE, s)
    nq = s // tq
    scale = MEM_HEAD_DIM ** -0.5

    def body(q_ref, kv_ref, o_ref):
        for h in range(N_HEADS_MEM):
            cols = slice(h * MEM_HEAD_DIM, (h + 1) * MEM_HEAD_DIM)
            k = kv_ref[:, cols]
            v = kv_ref[:, width + h * MEM_HEAD_DIM: width + (h + 1) * MEM_HEAD_DIM]
            sc = lax.dot_general(q_ref[:, cols], k, NT, preferred_element_type=F32) * scale
            p = jnp.exp(sc - jnp.max(sc, axis=1, keepdims=True))
            p = p / jnp.sum(p, axis=1, keepdims=True)
            o_ref[:, cols] = lax.dot_general(p.astype(BF16), v, NN, preferred_element_type=F32).astype(BF16)

    return pl.pallas_call(
        body, name="mem_attn_fwd", grid=(batch, nq),
        in_specs=[pl.BlockSpec((tq, width), lambda b, i: (b * nq + i, 0)),
                  pl.BlockSpec((n_mem, 2 * width), lambda b, i: (b, 0))],
        out_specs=pl.BlockSpec((tq, width), lambda b, i: (b * nq + i, 0)),
        out_shape=jax.ShapeDtypeStruct((t, width), BF16),
        compiler_params=_params(("parallel", "parallel")),
    )(q, kv)


def _mem_bwd(q, kv, do, batch, s, n_mem):
    t, width = q.shape
    tq = min(MEM_Q_TILE, s)
    nq = s // tq
    scale = MEM_HEAD_DIM ** -0.5

    def body(q_ref, kv_ref, do_ref, dq_ref, dkv_ref, acc):
        i = pl.program_id(1)

        @pl.when(i == 0)
        def _():
            acc[...] = jnp.zeros_like(acc)

        for h in range(N_HEADS_MEM):
            cols = slice(h * MEM_HEAD_DIM, (h + 1) * MEM_HEAD_DIM)
            vcols = slice(width + h * MEM_HEAD_DIM, width + (h + 1) * MEM_HEAD_DIM)
            qh, k, v, doh = q_ref[:, cols], kv_ref[:, cols], kv_ref[:, vcols], do_ref[:, cols]
            sc = lax.dot_general(qh, k, NT, preferred_element_type=F32) * scale
            p = jnp.exp(sc - jnp.max(sc, axis=1, keepdims=True))
            p = p / jnp.sum(p, axis=1, keepdims=True)
            dp = lax.dot_general(doh, v, NT, preferred_element_type=F32)
            ds = (p * (dp - jnp.sum(p * dp, axis=1, keepdims=True)) * scale).astype(BF16)
            dq_ref[:, cols] = lax.dot_general(ds, k, NN, preferred_element_type=F32).astype(BF16)
            acc[:, cols] += lax.dot_general(ds, qh, TN, preferred_element_type=F32)
            acc[:, vcols] += lax.dot_general(p.astype(BF16), doh, TN, preferred_element_type=F32)

        @pl.when(i == nq - 1)
        def _():
            dkv_ref[...] = acc[...].astype(BF16)

    row = pl.BlockSpec((tq, width), lambda b, i: (b * nq + i, 0))
    kvs = pl.BlockSpec((n_mem, 2 * width), lambda b, i: (b, 0))
    return pl.pallas_call(
        body, name="mem_attn_bwd", grid=(batch, nq),
        in_specs=[row, kvs, row], out_specs=[row, kvs],
        out_shape=[jax.ShapeDtypeStruct((t, width), BF16), jax.ShapeDtypeStruct((batch * n_mem, 2 * width), BF16)],
        scratch_shapes=[pltpu.VMEM((n_mem, 2 * width), F32)],
        compiler_params=_params(("parallel", "arbitrary")),
    )(q, kv, do)


def _mixer_fwd(o_a, o_b, w_a, w_b, proj, gate_col0, w_out, x, g, w_q):
    t, width = o_a.shape
    d = w_a.shape[1]
    nq_cols = w_q.shape[1]
    tm = min(ROW_TILE, t)
    gb0 = gate_col0 * LANES // d

    def body(oa_ref, ob_ref, wa_ref, wb_ref, ga_ref, gb_ref, wo_ref, x_ref, g_ref, wq_ref, ua_ref, ub_ref, mix_ref,
             n_ref, h_ref, q_ref):
        ua = lax.dot_general(oa_ref[...], wa_ref[...], NN, preferred_element_type=F32)
        ub = lax.dot_general(ob_ref[...], wb_ref[...], NN, preferred_element_type=F32)
        ua_ref[...] = ua.astype(BF16)
        ub_ref[...] = ub.astype(BF16)
        mixed = (jax.nn.sigmoid(ga_ref[...].astype(F32)) * ua + jax.nn.sigmoid(gb_ref[...].astype(F32)) * ub).astype(BF16)
        mix_ref[...] = mixed
        h = lax.dot_general(mixed, wo_ref[...], NN, preferred_element_type=F32) + x_ref[...]
        h_ref[...] = h
        r = lax.rsqrt(jnp.mean(h * h, axis=-1, keepdims=True) + RMS_EPS)
        n = (h * r * g_ref[...]).astype(BF16)
        n_ref[...] = n
        q_ref[...] = lax.dot_general(n, wq_ref[...], NN, preferred_element_type=F32).astype(BF16)

    row = pl.BlockSpec((tm, width), lambda i: (i, 0))
    wsp = pl.BlockSpec((width, d), lambda i: (0, 0))
    out = pl.BlockSpec((tm, d), lambda i: (i, 0))
    osh = jax.ShapeDtypeStruct((t, d), BF16)
    return pl.pallas_call(
        body, name="mixer_fwd", grid=(t // tm,),
        in_specs=[row, row, wsp, wsp,
                  pl.BlockSpec((tm, d), lambda i: (i, gb0)), pl.BlockSpec((tm, d), lambda i: (i, gb0 + 1)),
                  pl.BlockSpec((d, d), lambda i: (0, 0)), out, pl.BlockSpec((1, d), lambda i: (0, 0)),
                  pl.BlockSpec((d, nq_cols), lambda i: (0, 0))],
        out_specs=[out, out, out, out, out, pl.BlockSpec((tm, nq_cols), lambda i: (i, 0))],
        out_shape=[osh, osh, osh, osh, jax.ShapeDtypeStruct((t, d), F32), jax.ShapeDtypeStruct((t, nq_cols), BF16)],
        compiler_params=_params(("parallel",)),
    )(o_a, o_b, w_a, w_b, proj, proj, w_out, x, g, w_q)


def _mixer_bwd(dh, w_out, ua, ub, proj, gate_col0, w_a, w_b):
    t, d = dh.shape
    width = w_a.shape[0]
    tm = min(ROW_TILE, t)
    nc = d // LANES

    def body(dh_ref, w_ref, ua_ref, ub_ref, ga_ref, gb_ref, wa_ref, wb_ref, dua_ref, dub_ref, dg_ref, doa_ref, dob_ref):
        dm = lax.dot_general(dh_ref[...], w_ref[...], NT, preferred_element_type=F32)
        sa = jax.nn.sigmoid(ga_ref[...].astype(F32))
        sb = jax.nn.sigmoid(gb_ref[...].astype(F32))
        dua = (dm * sa).astype(BF16)
        dub = (dm * sb).astype(BF16)
        dua_ref[...] = dua
        dub_ref[...] = dub
        dg_ref[:, 0:d] = (dm * ua_ref[...].astype(F32) * sa * (1.0 - sa)).astype(BF16)
        dg_ref[:, d:2 * d] = (dm * ub_ref[...].astype(F32) * sb * (1.0 - sb)).astype(BF16)
        doa_ref[...] = lax.dot_general(dua, wa_ref[...], NT, preferred_element_type=F32).astype(BF16)
        dob_ref[...] = lax.dot_general(dub, wb_ref[...], NT, preferred_element_type=F32).astype(BF16)

    row = pl.BlockSpec((tm, d), lambda i: (i, 0))
    wsp = pl.BlockSpec((width, d), lambda i: (0, 0))
    osp = pl.BlockSpec((tm, width), lambda i: (i, 0))
    return pl.pallas_call(
        body, name="mixer_bwd", grid=(t // tm,),
        in_specs=[row, pl.BlockSpec((d, d), lambda i: (0, 0)), row, row,
                  pl.BlockSpec((tm, d), lambda i: (i, gate_col0 // nc)),
                  pl.BlockSpec((tm, d), lambda i: (i, gate_col0 // nc + 1)), wsp, wsp],
        out_specs=[row, row, pl.BlockSpec((tm, 2 * d), lambda i: (i, 0)), osp, osp],
        out_shape=[jax.ShapeDtypeStruct((t, d), BF16), jax.ShapeDtypeStruct((t, d), BF16),
                   jax.ShapeDtypeStruct((t, 2 * d), BF16), jax.ShapeDtypeStruct((t, width), BF16),
                   jax.ShapeDtypeStruct((t, width), BF16)],
        compiler_params=_params(("parallel",)),
    )(dh, w_out, ua, ub, proj, proj, w_a, w_b)


FFN_COLS = 1024


def _ffn_up(n, w_gate, w_up):
    t, d = n.shape
    hidden = w_gate.shape[0]
    tm = min(ROW_TILE, t)
    tn = min(FFN_COLS, hidden)

    def body(n_ref, wg_ref, wu_ref, hg_ref, hu_ref, act_ref):
        hg = lax.dot_general(n_ref[...], wg_ref[...], NT, preferred_element_type=F32)
        hu = lax.dot_general(n_ref[...], wu_ref[...], NT, preferred_element_type=F32)
        hg_ref[...] = hg.astype(BF16)
        hu_ref[...] = hu.astype(BF16)
        act_ref[...] = (hg * jax.nn.sigmoid(hg) * hu).astype(BF16)

    wsp = pl.BlockSpec((tn, d), lambda j, i: (j, 0))
    out = pl.BlockSpec((tm, tn), lambda j, i: (i, j))
    osh = jax.ShapeDtypeStruct((t, hidden), BF16)
    return pl.pallas_call(
        body, name="ffn_up", grid=(hidden // tn, t // tm),
        in_specs=[pl.BlockSpec((tm, d), lambda j, i: (i, 0)), wsp, wsp],
        out_specs=[out, out, out], out_shape=[osh, osh, osh],
        compiler_params=_params(("parallel", "parallel")),
    )(n, w_gate, w_up)


def _ffn_bwd(dh, w_down, w_gate, w_up, hg, hu, x, g, dres, w_prev):
    t, d = dh.shape
    hidden = w_down.shape[0]
    q = w_prev.shape[0]
    tm = min(ROW_TILE, t)
    tn = min(FFN_COLS, hidden)
    nj = hidden // tn

    def body(dh_ref, wd_ref, wg_ref, wu_ref, hg_ref, hu_ref, x_ref, g_ref, r_ref, wp_ref, dhg_ref, dhu_ref, dx_ref,
             dxb_ref, dg_ref, do_ref, acc):
        j, i = pl.program_id(0), pl.program_id(1)
        dact = lax.dot_general(dh_ref[...], wd_ref[...], NT, preferred_element_type=F32)
        hg = hg_ref[...].astype(F32)
        sg = jax.nn.sigmoid(hg)
        dhu = (dact * hg * sg).astype(BF16)
        dhg = (dact * hu_ref[...].astype(F32) * sg * (1.0 + hg * (1.0 - sg))).astype(BF16)
        dhu_ref[...] = dhu
        dhg_ref[...] = dhg
        part = (lax.dot_general(dhg, wg_ref[...], NN, preferred_element_type=F32)
                + lax.dot_general(dhu, wu_ref[...], NN, preferred_element_type=F32))

        @pl.when(j == 0)
        def _():
            acc[i] = part

        @pl.when(j > 0)
        def _():
            acc[i] += part

        @pl.when(jnp.logical_and(j == 0, i == 0))
        def _():
            dg_ref[...] = jnp.zeros_like(dg_ref)

        @pl.when(j == nj - 1)
        def _():
            dx, dg = _rms_bwd_rows(acc[i], x_ref[...], g_ref[...], r_ref[...])
            dx_ref[...] = dx
            dxb = dx.astype(BF16)
            dxb_ref[...] = dxb
            dg_ref[...] += dg
            do_ref[...] = lax.dot_general(dxb, wp_ref[...], NT, preferred_element_type=F32).astype(BF16)

    hid = pl.BlockSpec((tm, tn), lambda j, i: (i, j))
    wsp = pl.BlockSpec((tn, d), lambda j, i: (j, 0), pipeline_mode=pl.Buffered(1))
    late = pl.BlockSpec((tm, d), lambda j, i: (jnp.where(j == nj - 1, i, 0), 0))
    late_q = pl.BlockSpec((tm, q), lambda j, i: (jnp.where(j == nj - 1, i, 0), 0))
    vec = pl.BlockSpec((1, d), lambda j, i: (0, 0))
    osh = jax.ShapeDtypeStruct((t, hidden), BF16)
    return pl.pallas_call(
        body, name="ffn_bwd", grid=(nj, t // tm),
        in_specs=[pl.BlockSpec((tm, d), lambda j, i: (i, 0)), wsp, wsp, wsp, hid, hid, late, vec, late,
                  pl.BlockSpec((q, d), lambda j, i: (0, 0), pipeline_mode=pl.Buffered(1))],
        out_specs=[hid, hid, late, late, vec, late_q],
        out_shape=[osh, osh, jax.ShapeDtypeStruct((t, d), F32), jax.ShapeDtypeStruct((t, d), BF16),
                   jax.ShapeDtypeStruct((1, d), F32), jax.ShapeDtypeStruct((t, q), BF16)],
        scratch_shapes=[pltpu.VMEM((t // tm, tm, d), F32)],
        compiler_params=_params(("arbitrary", "arbitrary")),
    )(dh, w_down, w_gate, w_up, hg, hu, x, g, dres, w_prev)


MM_ROWS = 1024


def _mm_w(name, a, w, out_dtype, dims=NN):
    t, k = a.shape
    n = w.shape[1] if dims == NN else w.shape[0]
    tm, tn = min(MM_ROWS, t), min(1024, n)
    o_spec = pl.BlockSpec((tm, tn), lambda j, i: (i, j))
    b_spec = pl.BlockSpec((k, tn), lambda j, i: (0, j)) if dims == NN else pl.BlockSpec((tn, k), lambda j, i: (j, 0))
    return _mm(name, a, w, grid=(n // tn, t // tm), a_spec=pl.BlockSpec((tm, k), lambda j, i: (i, 0)), b_spec=b_spec,
               o_shape=(t, n), o_spec=o_spec, dims=dims, out_dtype=out_dtype)


def _mm_res_norm(name, a, w, res, g):
    t, k = a.shape
    d = w.shape[1]
    tm = min(ROW_TILE, t)

    def body(a_ref, w_ref, r_ref, g_ref, h_ref, n_ref):
        h = lax.dot_general(a_ref[...], w_ref[...], NN, preferred_element_type=F32) + r_ref[...]
        h_ref[...] = h
        r = lax.rsqrt(jnp.mean(h * h, axis=-1, keepdims=True) + RMS_EPS)
        n_ref[...] = (h * r * g_ref[...]).astype(BF16)

    row = pl.BlockSpec((tm, d), lambda i: (i, 0))
    return pl.pallas_call(
        body, name=name, grid=(t // tm,),
        in_specs=[pl.BlockSpec((tm, k), lambda i: (i, 0)), pl.BlockSpec((k, d), lambda i: (0, 0)), row,
                  pl.BlockSpec((1, d), lambda i: (0, 0))],
        out_specs=[row, row], out_shape=[jax.ShapeDtypeStruct((t, d), F32), jax.ShapeDtypeStruct((t, d), BF16)],
        compiler_params=_params(("parallel",)),
    )(a, w, res, g)


WGRAD_COLS = 256


def _wgrad(name, a, g, tk=1024, tn=1024):
    t, k = a.shape
    n = g.shape[1]
    tm, tk, tn = min(2 * MM_ROWS, t), min(tk, k), min(tn, n)
    nr = t // tm
    tc = min(WGRAD_COLS, tn)

    def body(a_ref, g_ref, o_ref, *acc):
        def run(first, last):
            for c in range(0, tn, tc):
                p = lax.dot_general(a_ref[...], g_ref[:, c:c + tc], TN, preferred_element_type=F32)
                if not first:
                    p += acc[0][:, c:c + tc]
                if last:
                    o_ref[:, c:c + tc] = p.astype(BF16)
                else:
                    acc[0][:, c:c + tc] = p

        if nr == 1:
            run(True, True)
            return
        r = pl.program_id(2)
        pl.when(r == 0)(functools.partial(run, True, False))
        if nr > 2:
            pl.when(jnp.logical_and(r > 0, r < nr - 1))(functools.partial(run, False, False))
        pl.when(r == nr - 1)(functools.partial(run, False, True))

    return pl.pallas_call(
        body, name=name, grid=(k // tk, n // tn, nr),
        in_specs=[pl.BlockSpec((tm, tk), lambda p, q, r: (r, p)), pl.BlockSpec((tm, tn), lambda p, q, r: (r, q))],
        out_specs=pl.BlockSpec((tk, tn), lambda p, q, r: (p, q)), out_shape=jax.ShapeDtypeStruct((k, n), BF16),
        scratch_shapes=[pltpu.VMEM((tk, tn), F32)] if nr > 1 else [],
        compiler_params=_params(("parallel", "parallel", "arbitrary")),
    )(a, g)


def _peers():
    x, y, c = lax.axis_index("x"), lax.axis_index("y"), lax.axis_index("c")
    me = 4 * x + 2 * y + c
    out = []
    for k in range(1, N_DEV):
        kx, ky, kc = (k >> 2) & 1, (k >> 1) & 1, k & 1
        px = 1 - x if kx else x
        py = 1 - y if ky else y
        pc = 1 - c if kc else c
        out.append(((px, py, pc), 4 * px + 2 * py + pc))
    return me, out


def _cast_weights(ws, pad_rows):
    def body(*refs):
        n = len(refs) // 2
        for i_ref, o_ref, pr in zip(refs[:n], refs[n:], pad_rows):
            r, c = i_ref.shape
            o_ref[0:r, :] = i_ref[...].astype(BF16)
            if pr:
                o_ref[r:r + pr, :] = jnp.zeros((pr, c), BF16)

    return pl.pallas_call(
        body, name="cast_weights", in_specs=[VMEM] * len(ws), out_specs=[VMEM] * len(ws),
        out_shape=[jax.ShapeDtypeStruct((w.shape[0] + pr, w.shape[1]), BF16) for w, pr in zip(ws, pad_rows)],
    )(*ws)


def _window(ref, j, c):
    return ref.at[:, pl.ds(pl.multiple_of(j * c, LANES), c)]


def _direct_copies(ins, outs, sems, gather, cols, landed):
    send_sems, recv_sems, loc_sems = sems
    n_peer = N_DEV - 1
    me, peers = _peers()

    def src(w, j):
        if gather:
            return ins[w]
        return _window(ins[w], j, cols[w]) if cols[w] else ins[w].at[j]

    def dst(w, j):
        return _window(outs[w], j, cols[w]) if gather and cols[w] else outs[w].at[j]

    local = [pltpu.make_async_copy(src(w, me), dst(w, me), loc_sems.at[w]) for w in range(len(ins))]
    remote = [pltpu.make_async_remote_copy(
        src_ref=src(w, idx), dst_ref=dst(w, idx if landed else me),
        send_sem=send_sems.at[w * n_peer + k], recv_sem=recv_sems.at[w * n_peer + k],
        device_id=dev, device_id_type=pl.DeviceIdType.MESH)
        for k, (dev, idx) in reversed(list(enumerate(peers))) for w in range(len(ins))]
    return local, remote


OTHER_CHIPS = (2, 4, 6)


def _gather_copies(ins, outs, sems, cols):
    send_sems, recv_sems, loc_sems = sems
    x, y, c = lax.axis_index("x"), lax.axis_index("y"), lax.axis_index("c")
    me = 4 * x + 2 * y + c
    n_pair = N_DEV - 1

    def dev(mask):
        return (1 - x if mask & 4 else x, 1 - y if mask & 2 else y, 1 - c if mask & 1 else c)

    def slot(w, mask):
        j = jnp.bitwise_xor(me, mask)
        return _window(outs[w], j, cols[w]) if cols[w] else outs[w].at[j]

    def remote(w, pair, src, to_slot, target):
        return pltpu.make_async_remote_copy(src_ref=src, dst_ref=slot(w, to_slot), send_sem=send_sems.at[w * n_pair + pair],
                                            recv_sem=recv_sems.at[w * n_pair + pair], device_id=dev(target),
                                            device_id_type=pl.DeviceIdType.MESH)

    ws = range(len(ins))
    return dict(
        local=[pltpu.make_async_copy(ins[w], slot(w, 0), loc_sems.at[w]) for w in ws],
        to_chips=[remote(w, 1 + t, ins[w], 0, m) for t, m in enumerate(OTHER_CHIPS) for w in ws],
        to_core=[remote(w, 0, ins[w], 0, 1) for w in ws],
        from_chips=[remote(w, 1 + t, ins[w], m, 0) for t, m in enumerate(OTHER_CHIPS) for w in ws],
        pass_on=[remote(w, 4 + t, slot(w, m), m, 1) for t, m in enumerate(OTHER_CHIPS) for w in ws],
        from_core=[remote(w, 0, ins[w], 1, 0) for w in ws]
        + [remote(w, 4 + t, ins[w], m + 1, 0) for t, m in enumerate(OTHER_CHIPS) for w in ws])


TWO_LEVEL = "gather in two levels"


def _exchange_start(ins, outs, sems, gather, cols):
    if gather == TWO_LEVEL:
        cps = _gather_copies(ins, outs, sems, cols)
        for cp in cps["local"] + cps["to_chips"] + cps["to_core"]:
            cp.start()
    else:
        local, remote = _direct_copies(ins, outs, sems, gather, cols, False)
        for cp in local + remote:
            cp.start()


def _exchange_pass_on(ins, outs, sems, gather, cols, chips):
    if gather == TWO_LEVEL:
        cps = _gather_copies(ins, outs, sems, cols)
        n = len(ins)
        for t in chips:
            for arrived, onward in zip(cps["from_chips"][t * n:(t + 1) * n], cps["pass_on"][t * n:(t + 1) * n]):
                arrived.wait_recv()
                onward.start()


def _exchange_wait(ins, outs, sems, gather, cols):
    if gather == TWO_LEVEL:
        cps = _gather_copies(ins, outs, sems, cols)
        for cp in cps["local"]:
            cp.wait()
        for cp in cps["to_chips"] + cps["to_core"] + cps["pass_on"]:
            cp.wait_send()
        for cp in cps["from_core"]:
            cp.wait_recv()
    else:
        local, remote = _direct_copies(ins, outs, sems, gather, cols, True)
        for cp in local:
            cp.wait()
        for cp in remote:
            cp.wait_send()
            cp.wait_recv()


def _exchange_shapes(arrs, gather, cols):
    n = len(arrs)
    out_shape = []
    for a, c in zip(arrs, cols):
        if gather:
            shape = (a.shape[0], N_DEV * c) if c else (N_DEV,) + a.shape
        else:
            shape = (N_DEV, a.shape[0], c) if c else a.shape
        out_shape.append(jax.ShapeDtypeStruct(shape, a.dtype))
    sems = [pltpu.SemaphoreType.DMA((n * (N_DEV - 1),)), pltpu.SemaphoreType.DMA((n * (N_DEV - 1),)),
            pltpu.SemaphoreType.DMA((n,))]
    return out_shape, sems


def _call(body, *, name, grid, in_specs, out_specs, out_shape, scratch, sem, args, ride=None):
    if ride is None:
        outs = pl.pallas_call(body, name=name, grid=grid, in_specs=in_specs, out_specs=out_specs, out_shape=out_shape,
                              scratch_shapes=scratch, compiler_params=_params(sem))(*args)
        return outs, None
    arrs, gather, cols = ride
    n, n_in, n_out, n_scr = len(arrs), len(in_specs), len(out_specs), len(scratch)
    x_shape, x_sems = _exchange_shapes(arrs, gather, cols)

    def riding(*refs):
        ins, x_ins = refs[:n_in], refs[n_in:n_in + n]
        outs = refs[n_in + n:n_in + n + n_out]
        x_outs = refs[n_in + n + n_out:n_in + 2 * n + n_out]
        scr = refs[n_in + 2 * n + n_out:n_in + 2 * n + n_out + n_scr]
        sems = refs[n_in + 2 * n + n_out + n_scr:]
        def at(step):
            return functools.reduce(jnp.logical_and, [pl.program_id(a) == v for a, v in enumerate(step)])

        @pl.when(at((0,) * len(grid)))
        def _():
            _exchange_start(x_ins, x_outs, sems, gather, cols)

        @pl.when(at((grid[0] // 2,) + (0,) * (len(grid) - 2) + (grid[-1] // 2,)))
        def _():
            _exchange_pass_on(x_ins, x_outs, sems, gather, cols, (0, 1))

        @pl.when(at((grid[0] // 2,) + (0,) * (len(grid) - 2) + (3 * grid[-1] // 4,)))
        def _():
            _exchange_pass_on(x_ins, x_outs, sems, gather, cols, (2,))

        body(*ins, *outs, *scr)

        @pl.when(at(tuple(g - 1 for g in grid)))
        def _():
            _exchange_wait(x_ins, x_outs, sems, gather, cols)

    res = pl.pallas_call(
        riding, name=name, grid=grid, in_specs=list(in_specs) + [ANY] * n, out_specs=list(out_specs) + [ANY] * n,
        out_shape=list(out_shape) + x_shape, scratch_shapes=list(scratch) + x_sems,
        compiler_params=_params(("arbitrary",) * len(grid)))(*args, *arrs)
    return res[:n_out], res[n_out:]


def _my_block():
    return (4 * lax.axis_index("x") + 2 * lax.axis_index("y") + lax.axis_index("c")).astype(jnp.int32).reshape(1)


def _proj_in_gather(x, g, w_shard):
    t, k = x.shape
    cs = w_shard.shape[1]
    tm = min(MM_ROWS, t)
    ni = t // tm
    arrival = (0, 1, 2, 4, 3, 5, 6, 7)

    def mask_at(s):
        return jnp.where(s == 3, 4, jnp.where(s == 4, 3, s))

    def body(me_ref, x_ref, g_ref, w_hbm, o_ref, all_hbm, n_hbm, w_vmem, n_vmem, send_sems, recv_sems, loc_sems,
             load_sems, n_sem):
        s, i = pl.program_id(0), pl.program_id(1)
        cps = _gather_copies([w_hbm], [all_hbm], (send_sems, recv_sems, loc_sems), (cs,))
        by_mask = {0: cps["local"][0], 1: cps["from_core"][0]}
        for t_chip, m in enumerate(OTHER_CHIPS):
            by_mask[m] = cps["from_chips"][t_chip]
            by_mask[m + 1] = cps["from_core"][1 + t_chip]
        arrived = [by_mask[m] for m in arrival]

        def load(step):
            src = w_hbm if step == 0 else _window(all_hbm, jnp.bitwise_xor(me_ref[0], arrival[step]), cs)
            return pltpu.make_async_copy(src, w_vmem.at[step % 2], load_sems.at[step % 2])

        @pl.when(jnp.logical_and(s == 0, i == 0))
        def _():
            for cp in cps["local"] + cps["to_chips"] + cps["to_core"]:
                cp.start()
            load(0).start()

        for step, mask in enumerate(arrival):
            @pl.when(jnp.logical_and(s == step, i == 0))
            def _(step=step):
                load(step).wait()

            if step + 1 < N_DEV:
                @pl.when(jnp.logical_and(s == step, i == min(1, ni - 1)))
                def _(step=step):
                    arrived[step + 1].wait_recv()
                    if arrival[step + 1] in OTHER_CHIPS:
                        cps["pass_on"][OTHER_CHIPS.index(arrival[step + 1])].start()
                    load(step + 1).start()

        @pl.when(s == 0)
        def _():
            xf = x_ref[...]
            r = lax.rsqrt(jnp.mean(xf * xf, axis=-1, keepdims=True) + RMS_EPS)
            n_vmem[i] = (xf * r * g_ref[...]).astype(BF16)
            keep = pltpu.make_async_copy(n_vmem.at[i], n_hbm.at[pl.ds(pl.multiple_of(i * tm, tm), tm), :], n_sem)
            keep.start()
            keep.wait()

        o_ref[...] = lax.dot_general(n_vmem[i], w_vmem[s % 2], NN, preferred_element_type=F32).astype(BF16)

        @pl.when(jnp.logical_and(s == N_DEV - 1, i == ni - 1))
        def _():
            cps["local"][0].wait()
            for cp in cps["to_chips"] + cps["to_core"] + cps["pass_on"]:
                cp.wait_send()

    return pl.pallas_call(
        body, name="proj_in",
        grid_spec=pltpu.PrefetchScalarGridSpec(
            num_scalar_prefetch=1, grid=(N_DEV, ni),
            in_specs=[pl.BlockSpec((tm, k), lambda s, i, me: (jnp.where(s == 0, i, 0), 0)),
                      pl.BlockSpec((1, k), lambda s, i, me: (0, 0)), ANY],
            out_specs=[pl.BlockSpec((tm, cs), lambda s, i, me: (i, jnp.bitwise_xor(me[0], mask_at(s)))), ANY, ANY],
            scratch_shapes=[pltpu.VMEM((2, k, cs), BF16), pltpu.VMEM((ni, tm, k), BF16),
                            pltpu.SemaphoreType.DMA((N_DEV - 1,)), pltpu.SemaphoreType.DMA((N_DEV - 1,)),
                            pltpu.SemaphoreType.DMA((1,)), pltpu.SemaphoreType.DMA((2,)), pltpu.SemaphoreType.DMA]),
        out_shape=[jax.ShapeDtypeStruct((t, N_DEV * cs), BF16), jax.ShapeDtypeStruct((k, N_DEV * cs), BF16),
                   jax.ShapeDtypeStruct((t, k), BF16)],
        compiler_params=_params(("arbitrary", "arbitrary")),
    )(_my_block(), x, g, w_shard)


def _gw_in_scatter(a, g):
    t, k = a.shape
    cs = g.shape[1] // N_DEV
    tm = min(MM_ROWS, t)
    nr = t // tm
    n_chip = N_DEV // 2
    chips = (6, 4, 2, 0)

    def body(me_ref, a_ref, g_ref, out_hbm, acc, stage, other, core_send, core_recv, chip_send, chip_recv, loc_sem):
        s, r = pl.program_id(0), pl.program_id(1)
        x, y, c = lax.axis_index("x"), lax.axis_index("y"), lax.axis_index("c")
        my_chip = 2 * x + y
        part = lax.dot_general(a_ref[...], g_ref[...], TN, preferred_element_type=F32)

        def to_core(m):
            return pltpu.make_async_remote_copy(src_ref=stage.at[0], dst_ref=other.at[m], send_sem=core_send.at[m],
                                                recv_sem=core_recv.at[m], device_id=(x, y, 1 - c),
                                                device_id_type=pl.DeviceIdType.MESH)

        def to_chip(m, landed):
            mask = chips[m]
            there = (1 - x if mask & 4 else x, 1 - y if mask & 2 else y, c)
            slot = (2 * there[0] + there[1]) if landed else my_chip
            return pltpu.make_async_remote_copy(src_ref=stage.at[1], dst_ref=out_hbm.at[slot], send_sem=chip_send.at[m],
                                                recv_sem=chip_recv.at[m], device_id=there,
                                                device_id_type=pl.DeviceIdType.MESH)

        local = pltpu.make_async_copy(stage.at[1], out_hbm.at[my_chip], loc_sem)

        @pl.when(r == 0)
        def _():
            acc[...] = part

        @pl.when(r > 0)
        def _():
            acc[...] += part

        for step in range(N_DEV):
            m = step // 2

            @pl.when(jnp.logical_and(s == step, r == nr - 1))
            def _(step=step, m=m):
                if step % 2 == 0:
                    if m > 0:
                        to_core(m - 1).wait_send()
                    stage[0] = acc[...].astype(BF16)
                    to_core(m).start()
                else:
                    if m > 0:
                        to_chip(m - 1, False).wait_send()
                    to_core(m).wait_recv()
                    stage[1] = (acc[...] + other[m].astype(F32)).astype(BF16)
                    if m < n_chip - 1:
                        to_chip(m, False).start()
                    else:
                        local.start()
                        to_core(m).wait_send()
                        local.wait()
                        for mm in range(n_chip - 1):
                            to_chip(mm, True).wait_recv()

    return pl.pallas_call(
        body, name="gw_in",
        grid_spec=pltpu.PrefetchScalarGridSpec(
            num_scalar_prefetch=1, grid=(N_DEV, nr),
            in_specs=[pl.BlockSpec((tm, k), lambda s, r, me: (r, 0)),
                      pl.BlockSpec((tm, cs), lambda s, r, me: (r, jnp.bitwise_xor(me[0], N_DEV - 1 - s)))],
            out_specs=ANY,
            scratch_shapes=[pltpu.VMEM((k, cs), F32), pltpu.VMEM((2, k, cs), BF16), pltpu.VMEM((n_chip, k, cs), BF16),
                            pltpu.SemaphoreType.DMA((n_chip,)), pltpu.SemaphoreType.DMA((n_chip,)),
                            pltpu.SemaphoreType.DMA((n_chip - 1,)), pltpu.SemaphoreType.DMA((n_chip - 1,)),
                            pltpu.SemaphoreType.DMA]),
        out_shape=jax.ShapeDtypeStruct((n_chip, k, cs), BF16),
        compiler_params=_params(("arbitrary", "arbitrary")),
    )(_my_block(), a, g)


SMALL_ROWS = 8


def _allreduce_small(parts, loss_part):
    n, d = len(parts), parts[0].shape[1]

    def body(*refs):
        part_refs, loss_ref, o_ref = refs[:n], refs[n], refs[n + 1]
        mine_ref, all_ref, send_sems, recv_sems = refs[n + 2:]
        me, peers = _peers()
        mine_ref[...] = jnp.zeros_like(mine_ref)
        for i, p_ref in enumerate(part_refs):
            mine_ref[i:i + 1, :] = p_ref[...]
        mine_ref[SMALL_ROWS - 1:SMALL_ROWS, 0:LANES] = loss_ref[0:1, :]
        all_ref[me] = mine_ref[...]
        for k, (dev, idx) in enumerate(peers):
            pltpu.make_async_remote_copy(src_ref=mine_ref, dst_ref=all_ref.at[me], send_sem=send_sems.at[k],
                                         recv_sem=recv_sems.at[k], device_id=dev,
                                         device_id_type=pl.DeviceIdType.MESH).start()
        for k, (dev, idx) in enumerate(peers):
            cp = pltpu.make_async_remote_copy(src_ref=mine_ref, dst_ref=all_ref.at[idx], send_sem=send_sems.at[k],
                                              recv_sem=recv_sems.at[k], device_id=dev,
                                              device_id_type=pl.DeviceIdType.MESH)
            cp.wait_send()
            cp.wait_recv()
        tot = all_ref[0]
        for dvc in range(1, N_DEV):
            tot = tot + all_ref[dvc]
        o_ref[...] = tot

    return pl.pallas_call(
        body, name="allreduce_small", in_specs=[VMEM] * (n + 1), out_specs=VMEM,
        out_shape=jax.ShapeDtypeStruct((SMALL_ROWS, d), F32),
        scratch_shapes=[pltpu.VMEM((SMALL_ROWS, d), F32), pltpu.VMEM((N_DEV, SMALL_ROWS, d), F32),
                        pltpu.SemaphoreType.DMA((N_DEV - 1,)), pltpu.SemaphoreType.DMA((N_DEV - 1,))],
    )(*parts, loss_part)


def _adam_math(g, w, m, v):
    m_new = ADAM_B1 * m + (1.0 - ADAM_B1) * g
    v_new = ADAM_B2 * v + (1.0 - ADAM_B2) * (g * g)
    m_hat = m_new / (1.0 - ADAM_B1 ** ADAM_STEP)
    v_hat = v_new / (1.0 - ADAM_B2 ** ADAM_STEP)
    delta = -ADAM_LR * (m_hat / (jnp.sqrt(v_hat) + ADAM_EPS) + ADAM_WD * w)
    return delta, m_new, v_new


def _adam(name, pieces, w, m, v):
    r, c = w.shape
    n_piece, _, cp = pieces.shape
    tr = r
    for cand in (256, 176, 128, 64):
        if r % cand == 0 and r > cand:
            tr = cand
            break

    def body(p_ref, w_ref, m_ref, v_ref, g_ref, d_ref, mo_ref, vo_ref):
        g = p_ref[0, :, 0:c].astype(F32)
        for j in range(1, n_piece):
            g = g + p_ref[j, :, 0:c].astype(F32)
        delta, m_new, v_new = _adam_math(g, w_ref[...], m_ref[...], v_ref[...])
        g_ref[...] = g
        d_ref[...] = delta
        mo_ref[...] = m_new
        vo_ref[...] = v_new

    blk = pl.BlockSpec((tr, c), lambda i: (i, 0))
    osh = jax.ShapeDtypeStruct((r, c), F32)
    return pl.pallas_call(
        body, name=name, grid=(r // tr,),
        in_specs=[pl.BlockSpec((n_piece, tr, cp), lambda i: (0, i, 0)), blk, blk, blk],
        out_specs=[blk, blk, blk, blk], out_shape=[osh, osh, osh, osh],
        compiler_params=_params(("parallel",)),
    )(pieces, w, m, v)


def _adam_small(g_all, ws, ms, vs):
    n = len(ws)

    def body(*refs):
        g_ref, ins, outs = refs[0], refs[1:1 + 3 * n], refs[1 + 3 * n:]
        for i in range(n):
            g = g_ref[i:i + 1, :]
            delta, m_new, v_new = _adam_math(g, ins[i][...], ins[n + i][...], ins[2 * n + i][...])
            for kind, val in enumerate((g, delta, m_new, v_new)):
                outs[kind * n + i][...] = val

    osh = jax.ShapeDtypeStruct(ws[0].shape, F32)
    res = pl.pallas_call(body, name="adam_small", in_specs=[VMEM] * (1 + 3 * n), out_specs=[VMEM] * (4 * n),
                         out_shape=[osh] * (4 * n))(g_all, *ws, *ms, *vs)
    return res[:n], res[n:2 * n], res[2 * n:3 * n], res[3 * n:]


def _local_step(x, mem, pos, tgt, gains, w_in_shard, shards, batch):
    g_mix, g_mem_q, g_mem_kv, g_ffn, g_final = gains
    t, d = x.shape
    s = t // batch
    n_mem = mem.shape[0] // batch
    n_sh = N_DEV
    width = shards[0].shape[0]
    nb = width // LANES

    lane = np.arange(LANES) % HEAD_DIM
    sel_lo = (lane < ROPE_HALF).astype(np.float32)[None, :]
    sel_hi = ((lane >= ROPE_HALF) & (lane < 2 * ROPE_HALF)).astype(np.float32)[None, :]
    freqs = np.float32(ROPE_THETA) ** (-np.arange(ROPE_HALF, dtype=np.float32) / np.float32(ROPE_HALF))
    inv_freq = np.where(lane < 2 * ROPE_HALF, freqs[lane % ROPE_HALF], 0.0).astype(np.float32)[None, :]
    cos_t, sin_a, sin_b = _rope_tables(pos, jnp.asarray(inv_freq), jnp.asarray(sel_lo), jnp.asarray(sel_hi))
    bias = _dilated_bias_tiles(s)

    proj, w_in, n1 = _proj_in_gather(x, g_mix, w_in_shard)
    qk_a = _rope_apply("rope_fwd", [proj], 2 * width, cos_t, sin_a, sin_b, 1.0)
    cs_up = shards[0].shape[1]
    (o_a, lse_a), (w_up_a, w_up_b, w_out, w_q, w_kv, w_o, w_fd) = _da_fwd(
        qk_a, proj, 2 * nb, bias, batch, s,
        ride=(shards[:6] + shards[8:], TWO_LEVEL, (cs_up, cs_up, 0, 0, 0, cs_up, 0)))
    (o_b, tot_b), (w_fg, w_fu) = _sb_fwd(proj, 3 * nb, 4 * nb, 5 * nb, batch, s, ride=(shards[6:8], True, (0, 0)))
    w_out = w_out.reshape(d, d)
    w_q = w_q.reshape(d, -1)
    w_kv = w_kv.reshape(d, -1)
    w_fd = w_fd.reshape(-1, d)
    w_fg = w_fg.reshape(-1, d)
    w_fu = w_fu.reshape(-1, d)
    ua, ub, mixed, n2, h1, q_m = _mixer_fwd(o_a, o_b, w_up_a, w_up_b, proj, 6 * nb, w_out, x, g_mem_q, w_q)
    mem_n = _rms_fwd("norm_mem_kv", mem, g_mem_kv)
    kv_m = _mm_w("mem_kv", mem_n, w_kv, BF16)
    o_m = _mem_fwd(q_m, kv_m, batch, s, n_mem)
    h2, n3 = _mm_res_norm("mem_out", o_m, w_o, h1, g_ffn)
    hg, hu, act = _ffn_up(n3, w_fg, w_fu)
    loss_part, dh3, dh3_b, dg_final = _loss_head(act, w_fd, h2, tgt, g_final.reshape(1, d))

    dhg, dhu, dh2, dh2_b, dg_ffn, do_m = _ffn_bwd(dh3_b, w_fd, w_fg, w_fu, hg, hu, h2, g_ffn, dh3, w_o)
    gw_fd = _wgrad("gw_ffn_down", act, dh3_b)
    gw_fg = _wgrad("gw_ffn_gate", dhg, n3)
    gw_fu = _wgrad("gw_ffn_up", dhu, n3)

    gw_o = _wgrad("gw_mem_o", o_m, dh2_b)
    dq_m, dkv_m = _mem_bwd(q_m, kv_m, do_m, batch, s, n_mem)
    gw_q = _wgrad("gw_mem_q", n2, dq_m)
    gw_kv = _wgrad("gw_mem_kv", mem_n, dkv_m)
    (dg_mem_kv,) = _rms_bwd("norm_mem_kv_bwd", (dkv_m, w_kv, NT), mem, g_mem_kv, None, ())
    dh1, dh1_b, dg_mem_q = _rms_bwd("norm_mem_q_bwd", (dq_m, w_q, NT), h1, g_mem_q, dh2, ("f32", "bf16"))

    gw_out = _wgrad("gw_out", mixed, dh1_b)
    dua, dub, dgates, do_a, do_b = _mixer_bwd(dh1_b, w_out, ua, ub, proj, 6 * nb, w_up_a, w_up_b)
    gw_ua = _wgrad("gw_up_a", o_a, dua)
    gw_ub = _wgrad("gw_up_b", o_b, dub)
    (dq_ar, dk_ar, dv_a), (p_fg, p_fd) = _da_bwd(
        qk_a, proj, 2 * nb, bias, o_a, lse_a, do_a, batch, s,
        ride=([gw_fg.reshape(n_sh, -1, d), gw_fd.reshape(n_sh, -1, d)], False, (0, 0)))
    mid = [gw_ua, gw_ub, gw_out.reshape(n_sh, -1, d), gw_q.reshape(n_sh, -1, gw_q.shape[1]),
           gw_kv.reshape(n_sh, -1, gw_kv.shape[1]), gw_o, gw_fu.reshape(n_sh, -1, d)]
    (dq_b, dk_b, dv_b), (*p_mid, p_fu) = _sb_bwd(proj, 3 * nb, 4 * nb, 5 * nb, tot_b, do_b, batch, s,
                                                 ride=(mid, False, (cs_up, cs_up, 0, 0, 0, cs_up, 0)))
    p_ffn = [p_fg, p_fu, p_fd]
    dproj = _rope_apply("rope_bwd", [dq_ar, dk_ar], width, cos_t, sin_a, sin_b, -1.0,
                        tail=(dv_a, dq_b, dk_b, dv_b, dgates))
    grad_x, dg_mix = _rms_bwd("proj_in_bwd", (dproj, w_in, NT), x, g_mix, dh1, ("f32",))
    p_in = _gw_in_scatter(n1, dproj)
    return loss_part, grad_x, [p_in] + list(p_mid) + p_ffn, (dg_mix, dg_mem_q, dg_mem_kv, dg_ffn, dg_final)


WEIGHTS =("w_in", "w_up_a", "w_up_b", "w_out", "w_q_mem", "w_kv_mem", "w_o_mem", "w_ffn_gate", "w_ffn_up", "w_ffn_down")
GAINS = ("g_mix", "g_mem_q", "g_mem_kv", "g_ffn", "g_final")
ORDER = ("g_mix", "w_in", "w_up_a", "w_up_b", "w_out", "g_mem_q", "g_mem_kv", "w_q_mem", "w_kv_mem", "w_o_mem", "g_ffn",
         "w_ffn_gate", "w_ffn_up", "w_ffn_down", "g_final")


def kernel(x, mem, positions, g_mix, w_in, w_up_a, w_up_b, w_out, g_mem_q, g_mem_kv, w_q_mem, w_kv_mem, w_o_mem, g_ffn, w_ffn_gate, w_ffn_up, w_ffn_down, g_final, loss_target, m_g_mix, m_w_in, m_w_up_a, m_w_up_b, m_w_out, m_g_mem_q, m_g_mem_kv, m_w_q_mem, m_w_kv_mem, m_w_o_mem, m_g_ffn, m_w_ffn_gate, m_w_ffn_up, m_w_ffn_down, m_g_final, v_g_mix, v_w_in, v_w_up_a, v_w_up_b, v_w_out, v_g_mem_q, v_g_mem_kv, v_w_q_mem, v_w_kv_mem, v_w_o_mem, v_g_ffn, v_w_ffn_gate, v_w_ffn_up, v_w_ffn_down, v_g_final):
    given = dict(locals())
    batch, s, d = x.shape
    t = batch * s
    flipped = ("w_ffn_gate", "w_ffn_up")

    def view(a, n):
        a = a.reshape(a.shape[-2:])
        return a.T if n in flipped else a

    def unview(a, n):
        return (a.T if n in flipped else a).reshape(given[n].shape)

    shard = {n: view(given[n], n) for n in WEIGHTS}
    gains = [given[n].reshape(1, d) for n in GAINS]

    pad = (-shard["w_ffn_down"].shape[0]) % LANES
    cast = _cast_weights([shard[n] for n in WEIGHTS], [pad if n in flipped + ("w_ffn_down",) else 0 for n in WEIGHTS])
    loss_part, grad_x, pieces, dgains = _local_step(
        x.reshape(t, d), mem.reshape(-1, d), positions.reshape(t, 1), loss_target.reshape(t, d), gains, cast[0],
        cast[1:], batch)

    grad, delta, new_m, new_v = {}, {}, {}, {}
    for n, p in zip(WEIGHTS, pieces):
        outs = _adam("adam_" + n, p, shard[n], view(given["m_" + n], n), view(given["v_" + n], n))
        grad[n], delta[n], new_m[n], new_v[n] = [unview(o, n) for o in outs]

    g_all = _allreduce_small(list(dgains), loss_part)
    small = _adam_small(g_all, gains, [given["m_" + n].reshape(1, d) for n in GAINS],
                        [given["v_" + n].reshape(1, d) for n in GAINS])
    for out, vals in zip((grad, delta, new_m, new_v), small):
        for n, val in zip(GAINS, vals):
            out[n] = val.reshape(given[n].shape)

    loss = g_all[SMALL_ROWS - 1, 0]
    return (loss, grad_x.reshape(x.shape), *[grad[n] for n in ORDER], *[delta[n] for n in ORDER],
            *[new_m[n] for n in ORDER], *[new_v[n] for n in ORDER])
```

```python
import functools
import math

import jax
import jax.numpy as jnp
import numpy as np
from jax import lax
from jax.experimental import pallas as pl
from jax.experimental.pallas import tpu as pltpu

F32 = jnp.float32
BF16 = jnp.bfloat16

N_DEV = 8
HEAD_DIM = 64
MEM_HEAD_DIM = 128
N_HEADS_MEM = 4
BLOCK = 128
DIL_PATTERNS = ((128, 1), (512, 4), (2048, 16))
ROPE_THETA = 500000.0
ROPE_HALF = 8
RMS_EPS = 1e-6
ADAM_LR, ADAM_B1, ADAM_B2, ADAM_EPS, ADAM_WD, ADAM_STEP = 0.001, 0.9, 0.999, 1e-08, 0.01, 10
NEG = -1e30
ROW_TILE = 512
LANES = 128

ANY = pl.BlockSpec(memory_space=pl.ANY)
VMEM = pl.BlockSpec(memory_space=pltpu.VMEM)
NN = (((1,), (0,)), ((), ()))
NT = (((1,), (1,)), ((), ()))
TN = (((0,), (0,)), ((), ()))


def _params(sem):
    return pltpu.CompilerParams(dimension_semantics=sem)


def _mm(name, a, b, *, grid, a_spec, b_spec, o_shape, o_spec, dims, out_dtype):
    def body(a_ref, b_ref, o_ref):
        o_ref[...] = lax.dot_general(a_ref[...], b_ref[...], dims, preferred_element_type=F32).astype(out_dtype)

    return pl.pallas_call(
        body, name=name, grid=grid, in_specs=[a_spec, b_spec],
        out_specs=o_spec, out_shape=jax.ShapeDtypeStruct(o_shape, out_dtype),
        compiler_params=_params(("parallel",) * len(grid)),
    )(a, b)


def _rms_fwd(name, x, g):
    t, d = x.shape
    tm = min(ROW_TILE, t)

    def body(x_ref, g_ref, o_ref):
        xf = x_ref[...]
        r = lax.rsqrt(jnp.mean(xf * xf, axis=-1, keepdims=True) + RMS_EPS)
        o_ref[...] = (xf * r * g_ref[...]).astype(BF16)

    return pl.pallas_call(
        body, name=name, grid=(t // tm,),
        in_specs=[pl.BlockSpec((tm, d), lambda i: (i, 0)), pl.BlockSpec((1, d), lambda i: (0, 0))],
        out_specs=pl.BlockSpec((tm, d), lambda i: (i, 0)), out_shape=jax.ShapeDtypeStruct((t, d), BF16),
        compiler_params=_params(("parallel",)),
    )(x, g)


def _rms_bwd_rows(dnf, xf, gv, res):
    r = lax.rsqrt(jnp.mean(xf * xf, axis=-1, keepdims=True) + RMS_EPS)
    xh = xf * r
    dxh = dnf * gv
    dx = r * (dxh - xh * jnp.mean(dxh * xh, axis=-1, keepdims=True))
    if res is not None:
        dx = dx + res
    return dx, jnp.sum(dnf * xh, axis=0, keepdims=True)


def _rms_bwd(name, dn, x, g, dres, want):
    t, d = x.shape
    tm = min(ROW_TILE, t)
    has_res = dres is not None
    lhs = list(dn) if isinstance(dn, tuple) else [dn]
    n_lhs = len(lhs[:2])

    def body(*refs):
        x_ref, g_ref = refs[n_lhs], refs[n_lhs + 1]
        r_ref = refs[n_lhs + 2] if has_res else None
        dx_refs, dg_ref = refs[-1 - len(want):-1], refs[-1]
        if n_lhs == 2:
            dnf = lax.dot_general(refs[0][...], refs[1][...], lhs[2], preferred_element_type=F32)
        else:
            dnf = refs[0][...].astype(F32)
        dx, dg = _rms_bwd_rows(dnf, x_ref[...], g_ref[...], r_ref[...] if has_res else None)
        for kind, dx_ref in zip(want, dx_refs):
            dx_ref[...] = dx.astype(F32 if kind == "f32" else BF16)

        @pl.when(pl.program_id(0) == 0)
        def _():
            dg_ref[...] = jnp.zeros_like(dg_ref)

        dg_ref[...] += dg

    row = pl.BlockSpec((tm, d), lambda i: (i, 0))
    vec = pl.BlockSpec((1, d), lambda i: (0, 0))
    if n_lhs == 2:
        first = [pl.BlockSpec((tm, lhs[0].shape[1]), lambda i: (i, 0)), pl.BlockSpec(lhs[1].shape, lambda i: (0, 0))]
    else:
        first = [row]
    return pl.pallas_call(
        body, name=name, grid=(t // tm,),
        in_specs=first + [row, vec] + ([row] if has_res else []),
        out_specs=[row] * len(want) + [vec],
        out_shape=[jax.ShapeDtypeStruct((t, d), F32 if kind == "f32" else BF16) for kind in want]
        + [jax.ShapeDtypeStruct((1, d), F32)],
        compiler_params=_params(("arbitrary",)),
    )(*(lhs[:2] + [x, g] + ([dres] if has_res else [])))


def _loss_head(a, w, res, tgt, g):
    t, d = res.shape
    k = a.shape[1]
    tm = min(ROW_TILE, t)

    def body(a_ref, w_ref, r_ref, t_ref, g_ref, loss_ref, dh_ref, dhb_ref, dg_ref):
        xf = lax.dot_general(a_ref[...], w_ref[...], NN, preferred_element_type=F32) + r_ref[...]
        gv = g_ref[...]
        r = lax.rsqrt(jnp.mean(xf * xf, axis=-1, keepdims=True) + RMS_EPS)
        xh = xf * r
        e = xh * gv - t_ref[...]
        dy = e * (1.0 / d)
        dxh = dy * gv
        dh = r * (dxh - xh * jnp.mean(dxh * xh, axis=-1, keepdims=True))
        dh_ref[...] = dh
        dhb_ref[...] = dh.astype(BF16)

        @pl.when(pl.program_id(0) == 0)
        def _():
            dg_ref[...] = jnp.zeros_like(dg_ref)
            loss_ref[...] = jnp.zeros_like(loss_ref)

        dg_ref[...] += jnp.sum(dy * xh, axis=0, keepdims=True)
        part = jnp.sum(jnp.sum(e * e, axis=1, keepdims=True), axis=0, keepdims=True) * (0.5 / d)
        loss_ref[...] += jnp.broadcast_to(part, loss_ref.shape)

    row = pl.BlockSpec((tm, d), lambda i: (i, 0))
    vec = pl.BlockSpec((1, d), lambda i: (0, 0))
    return pl.pallas_call(
        body, name="loss_head", grid=(t // tm,),
        in_specs=[pl.BlockSpec((tm, k), lambda i: (i, 0)), pl.BlockSpec((k, d), lambda i: (0, 0)), row, row, vec],
        out_specs=[pl.BlockSpec((8, LANES), lambda i: (0, 0)), row, row, vec],
        out_shape=[jax.ShapeDtypeStruct((8, LANES), F32), jax.ShapeDtypeStruct((t, d), F32),
                   jax.ShapeDtypeStruct((t, d), BF16), jax.ShapeDtypeStruct((1, d), F32)],
        compiler_params=_params(("arbitrary",)),
    )(a, w, res, tgt, g)


def _rope_tables(pos, inv_freq, sel_lo, sel_hi):
    t = pos.shape[0]
    tm = min(ROW_TILE, t)

    def body(p_ref, f_ref, lo_ref, hi_ref, c_ref, sa_ref, sb_ref):
        ang = p_ref[...].astype(F32) * f_ref[...]
        rot = lo_ref[...] + hi_ref[...]
        cs, sn = jnp.cos(ang), jnp.sin(ang)
        c_ref[...] = cs * rot + (1.0 - rot)
        sa_ref[...] = -sn * lo_ref[...]
        sb_ref[...] = sn * hi_ref[...]

    vec = pl.BlockSpec((1, LANES), lambda i: (0, 0))
    row = pl.BlockSpec((tm, LANES), lambda i: (i, 0))
    return pl.pallas_call(
        body, name="rope_tables", grid=(t // tm,),
        in_specs=[pl.BlockSpec((tm, 1), lambda i: (i, 0)), vec, vec, vec],
        out_specs=[row, row, row], out_shape=[jax.ShapeDtypeStruct((t, LANES), F32)] * 3,
        compiler_params=_params(("parallel",)),
    )(pos, inv_freq, sel_lo, sel_hi)


def _rope_apply(name, srcs, width, cos_t, sin_a, sin_b, sign, tail=()):
    t = srcs[0].shape[0]
    tm = min(ROW_TILE, t)
    n_cols = width // LANES
    n_src = len(srcs)

    def body(*refs):
        x_refs, tail_refs = refs[:n_src], refs[n_src:n_src + len(tail)]
        c_ref, sa_ref, sb_ref, o_ref = refs[n_src + len(tail):]
        cs, sa, sb = c_ref[...], sign * sa_ref[...], sign * sb_ref[...]
        for a, x_ref in enumerate(x_refs):
            for c in range(n_cols):
                xf = x_ref[:, c * LANES:(c + 1) * LANES].astype(F32)
                up = pltpu.roll(xf, LANES - ROPE_HALF, 1)
                dn = pltpu.roll(xf, ROPE_HALF, 1)
                o_ref[:, a * width + c * LANES:a * width + (c + 1) * LANES] = (xf * cs + up * sa + dn * sb).astype(BF16)
        col = n_src * width
        for t_ref in tail_refs:
            o_ref[:, col:col + t_ref.shape[1]] = t_ref[...]
            col += t_ref.shape[1]

    wide = n_src * width + sum(a.shape[1] for a in tail)
    tab = pl.BlockSpec((tm, LANES), lambda i: (i, 0))
    return pl.pallas_call(
        body, name=name, grid=(t // tm,),
        in_specs=[pl.BlockSpec((tm, width), lambda i: (i, 0))] * n_src
        + [pl.BlockSpec((tm, a.shape[1]), lambda i: (i, 0)) for a in tail] + [tab, tab, tab],
        out_specs=pl.BlockSpec((tm, wide), lambda i: (i, 0)),
        out_shape=jax.ShapeDtypeStruct((t, wide), BF16),
        compiler_params=_params(("parallel",)),
    )(*srcs, *tail, cos_t, sin_a, sin_b)


DA_T = 256
MIX_STREAMS = 4
SB_BWD_STREAMS = 2


def _lane_lo():
    return lax.broadcasted_iota(jnp.int32, (BLOCK, LANES), 1) < HEAD_DIM


def _dilated_bias_tiles(s):
    n = s // DA_T
    dist = (np.arange(n)[:, None, None] * DA_T + np.arange(DA_T)[None, :, None] - np.arange(DA_T)[None, None, :])
    cnt = np.zeros(dist.shape, np.float32)
    for window, dil in DIL_PATTERNS:
        cnt += ((dist >= 0) & (dist % dil == 0) & (dist <= window)).astype(np.float32)
    return jnp.asarray(np.where(cnt > 0, np.log(np.maximum(cnt, 1.0)), NEG).astype(np.float32))


def _stack_heads(x, lo):
    zero = jnp.zeros_like(x)
    return jnp.concatenate([jnp.where(lo, x, zero), jnp.where(lo, zero, x)], axis=0)


def _da_fwd(qk, proj, v_col0, bias, batch, s, ride=None, streams=MIX_STREAMS):
    t = qk.shape[0]
    nq = s // DA_T
    n_pairs = 4
    ns = streams
    wide = ns * LANES
    scale = HEAD_DIM ** -0.5

    def body(q_ref, k_ref, v_ref, b_ref, o_ref, lse_ref, acc_ref, m_ref, l_ref):
        i = pl.program_id(2)
        lo = lax.broadcasted_iota(jnp.int32, (DA_T, LANES), 1) < HEAD_DIM
        ones = jnp.ones((DA_T, LANES), BF16)
        acc_ref[...] = jnp.zeros_like(acc_ref)
        m_ref[...] = jnp.full(m_ref.shape, NEG, F32)
        l_ref[...] = jnp.zeros_like(l_ref)
        qqs = [_stack_heads(q_ref[:, st * LANES:(st + 1) * LANES] * scale, lo) for st in range(ns)]

        def scores(st, rows, bias2):
            k = k_ref[rows, st * LANES:(st + 1) * LANES]
            return lax.dot_general(qqs[st], k, NT, preferred_element_type=F32) + bias2

        def softmax(st, sc):
            m_old = m_ref[st]
            m_new = jnp.maximum(m_old, jnp.broadcast_to(jnp.max(sc, axis=1, keepdims=True), m_old.shape))
            m_ref[st] = m_new
            return jnp.exp(sc - jnp.concatenate([m_new, m_new], axis=1)).astype(BF16), jnp.exp(m_old - m_new)

        def values(st, rows, p, alpha):
            v = v_ref[rows, st * LANES:(st + 1) * LANES]
            vz = jnp.zeros_like(v)
            l_ref[st] = alpha * l_ref[st] + lax.dot_general(p, ones, NN, preferred_element_type=F32)
            pv = (lax.dot_general(p[:DA_T], jnp.where(lo, v, vz), NN, preferred_element_type=F32)
                  + lax.dot_general(p[DA_T:], jnp.where(lo, vz, v), NN, preferred_element_type=F32))
            acc_ref[st] = acc_ref[st] * jnp.where(lo, alpha[:DA_T], alpha[DA_T:]) + pv

        def trip(dlt, carry):
            rows = pl.ds(pl.multiple_of((i - dlt) * DA_T, DA_T), DA_T)
            bias_t = b_ref[dlt]
            bias2 = jnp.concatenate([bias_t, bias_t], axis=0)
            scs = [scores(st, rows, bias2) for st in range(ns)]
            pas = [softmax(st, scs[st]) for st in range(ns)]
            for st in range(ns):
                values(st, rows, *pas[st])
            return carry

        lax.fori_loop(0, i + 1, trip, 0)
        for st in range(ns):
            cols = slice(st * LANES, (st + 1) * LANES)
            l_t = l_ref[st]
            o_ref[:, cols] = (acc_ref[st] / jnp.where(lo, l_t[:DA_T], l_t[DA_T:])).astype(BF16)
            lse = m_ref[st] + jnp.log(l_t)
            lse_ref[:, cols] = jnp.where(lo, lse[:DA_T], lse[DA_T:])

    blk = pl.BlockSpec((DA_T, wide), lambda b, h, i: (b * nq + i, h))
    return _call(
        body, name="attn_a_fwd", grid=(batch, n_pairs // ns, nq),
        in_specs=[blk,
                  pl.BlockSpec((s, wide), lambda b, h, i: (b, n_pairs // ns + h)),
                  pl.BlockSpec((s, wide), lambda b, h, i: (b, v_col0 // ns + h)),
                  pl.BlockSpec((nq, DA_T, DA_T), lambda b, h, i: (0, 0, 0))],
        out_specs=[blk, blk],
        out_shape=[jax.ShapeDtypeStruct((t, n_pairs * LANES), BF16), jax.ShapeDtypeStruct((t, n_pairs * LANES), F32)],
        scratch=[pltpu.VMEM((ns, DA_T, LANES), F32), pltpu.VMEM((ns, 2 * DA_T, LANES), F32),
                 pltpu.VMEM((ns, 2 * DA_T, LANES), F32)],
        sem=("parallel", "parallel", "arbitrary"), args=(qk, qk, proj, bias), ride=ride)


def _da_bwd(qk, proj, v_col0, bias, o, lse, do, batch, s, ride=None, streams=MIX_STREAMS):
    t = qk.shape[0]
    nq = s // DA_T
    n_pairs = 4
    ns = streams
    wide = ns * LANES
    scale = HEAD_DIM ** -0.5

    def body(q_ref, k_ref, v_ref, b_ref, o_ref, lse_ref, do_ref, dq_ref, dk_ref, dv_ref, dk_acc, dv_acc, dq_acc):
        i = pl.program_id(2)
        lo = lax.broadcasted_iota(jnp.int32, (DA_T, LANES), 1) < HEAD_DIM

        @pl.when(i == 0)
        def _():
            dk_acc[...] = jnp.zeros_like(dk_acc)
            dv_acc[...] = jnp.zeros_like(dv_acc)

        dq_acc[...] = jnp.zeros_like(dq_acc)
        qqs, dds, deltas, lses = [], [], [], []
        for st in range(ns):
            cols = slice(st * LANES, (st + 1) * LANES)
            do_ = do_ref[:, cols]
            qqs.append(_stack_heads(q_ref[:, cols] * scale, lo))
            dds.append(_stack_heads(do_, lo))
            prod = do_.astype(F32) * o_ref[:, cols].astype(F32)
            fz = jnp.zeros_like(prod)
            deltas.append(jnp.concatenate([jnp.sum(jnp.where(lo, prod, fz), axis=1, keepdims=True),
                                           jnp.sum(jnp.where(lo, fz, prod), axis=1, keepdims=True)], axis=0))
            lse_t = lse_ref[:, cols]
            lses.append(jnp.concatenate([lse_t[:, 0:1], lse_t[:, HEAD_DIM:HEAD_DIM + 1]], axis=0))

        def products(st, rows, bias2):
            cols = slice(st * LANES, (st + 1) * LANES)
            sc = lax.dot_general(qqs[st], k_ref[rows, cols], NT, preferred_element_type=F32) + bias2
            return sc, lax.dot_general(dds[st], v_ref[rows, cols], NT, preferred_element_type=F32)

        def weights(st, sc, dp):
            p = jnp.exp(sc - lses[st])
            return (p * (dp - deltas[st])).astype(BF16), p.astype(BF16)

        def gradients(st, rows, ds, p):
            cols = slice(st * LANES, (st + 1) * LANES)
            k = k_ref[rows, cols]
            kz = jnp.zeros_like(k)
            dq_acc[st] += (lax.dot_general(ds[:DA_T], jnp.where(lo, k, kz), NN, preferred_element_type=F32)
                           + lax.dot_general(ds[DA_T:], jnp.where(lo, kz, k), NN, preferred_element_type=F32))
            dk_acc[rows, cols] += lax.dot_general(ds, qqs[st], TN, preferred_element_type=F32)
            dv_acc[rows, cols] += lax.dot_general(p, dds[st], TN, preferred_element_type=F32)

        def trip(dlt, carry):
            rows = pl.ds(pl.multiple_of((i - dlt) * DA_T, DA_T), DA_T)
            bias_t = b_ref[dlt]
            bias2 = jnp.concatenate([bias_t, bias_t], axis=0)
            prods = [products(st, rows, bias2) for st in range(ns)]
            wts = [weights(st, *prods[st]) for st in range(ns)]
            for st in range(ns):
                gradients(st, rows, *wts[st])
            return carry

        lax.fori_loop(0, i + 1, trip, 0)
        for st in range(ns):
            dq_ref[:, st * LANES:(st + 1) * LANES] = (dq_acc[st] * scale).astype(BF16)

        @pl.when(i == nq - 1)
        def _():
            dk_ref[...] = dk_acc[...].astype(BF16)
            dv_ref[...] = dv_acc[...].astype(BF16)

    blk = pl.BlockSpec((DA_T, wide), lambda b, h, i: (b * nq + i, h))
    seq = pl.BlockSpec((s, wide), lambda b, h, i: (b, h), pipeline_mode=pl.Buffered(1))
    one = pl.Buffered(1)
    out = jax.ShapeDtypeStruct((t, n_pairs * LANES), BF16)
    return _call(
        body, name="attn_a_bwd", grid=(batch, n_pairs // ns, nq),
        in_specs=[blk,
                  pl.BlockSpec((s, wide), lambda b, h, i: (b, n_pairs // ns + h), pipeline_mode=one),
                  pl.BlockSpec((s, wide), lambda b, h, i: (b, v_col0 // ns + h), pipeline_mode=one),
                  pl.BlockSpec((nq, DA_T, DA_T), lambda b, h, i: (0, 0, 0), pipeline_mode=one),
                  blk, blk, blk],
        out_specs=[blk, seq, seq], out_shape=[out, out, out],
        scratch=[pltpu.VMEM((s, wide), F32), pltpu.VMEM((s, wide), F32), pltpu.VMEM((ns, DA_T, LANES), F32)],
        sem=("parallel", "parallel", "arbitrary"), args=(qk, qk, proj, bias, o, lse, do), ride=ride)


SB_Q = 256


def _sb_consts(after):
    r = lax.broadcasted_iota(jnp.int32, (2 * BLOCK, 2 * BLOCK), 0) % BLOCK
    c = lax.broadcasted_iota(jnp.int32, (2 * BLOCK, 2 * BLOCK), 1)
    tri = (r > c) if after else (r < c)
    return jnp.logical_or(c >= BLOCK, tri).astype(BF16)


def _split(x):
    hi = x.astype(BF16)
    lo = (x - hi.astype(F32)).astype(BF16)
    return jnp.concatenate([hi, lo], axis=1)


def _sb_fwd(proj, q_col0, k_col0, v_col0, batch, s, ride=None, streams=MIX_STREAMS):
    t = proj.shape[0]
    nq = s // SB_Q
    n_pairs = 4
    ns = streams
    wide = ns * LANES
    scale = HEAD_DIM ** -0.5

    def body(q_ref, k_ref, v_ref, o_ref, tot_ref, acc_ref, run_ref):
        i = pl.program_id(2)
        lo_q = lax.broadcasted_iota(jnp.int32, (SB_Q, LANES), 1) < HEAD_DIM
        lo_k = _lane_lo()
        mat = _sb_consts(True)
        row = lax.broadcasted_iota(jnp.int32, (2 * SB_Q, LANES), 0) % SB_Q
        ahead = row - lax.broadcasted_iota(jnp.int32, (2 * SB_Q, LANES), 1)
        acc_ref[...] = jnp.zeros_like(acc_ref)
        run_ref[...] = jnp.zeros_like(run_ref)
        qqs = [_stack_heads(q_ref[:, st * LANES:(st + 1) * LANES] * scale, lo_q) for st in range(ns)]

        def units(todo):
            def rows(j):
                return pl.ds(pl.multiple_of(j * BLOCK, BLOCK), BLOCK)

            zs = [lax.dot_general(qqs[st], k_ref[rows(j), st * LANES:(st + 1) * LANES], NT, preferred_element_type=F32)
                  for st, j, _ in todo]
            logs = []
            for z, (_, _, off) in zip(zs, todo):
                lsig = jnp.minimum(z, 0.0) - jnp.log(1.0 + jnp.exp(-jnp.abs(z)))
                lneg = lsig - z
                if off is not None:
                    lneg = jnp.where(ahead > off, lneg, 0.0)
                logs.append((lsig, _split(lneg)))
            sums = [lax.dot_general(cat, mat, NN, preferred_element_type=F32) for _, cat in logs]
            probs = []
            for (lsig, _), sm, (st, _, off) in zip(logs, sums, todo):
                run = run_ref[st]
                a = jnp.exp(lsig + run + sm[:, :BLOCK])
                if off is not None:
                    a = jnp.where(ahead > off, a, 0.0)
                run_ref[st] = run + sm[:, BLOCK:]
                probs.append(a.astype(BF16))
            for ab, (st, j, _) in zip(probs, todo):
                v = v_ref[rows(j), st * LANES:(st + 1) * LANES]
                vz = jnp.zeros_like(v)
                acc_ref[st] += (lax.dot_general(ab[:SB_Q], jnp.where(lo_k, v, vz), NN, preferred_element_type=F32)
                                + lax.dot_general(ab[SB_Q:], jnp.where(lo_k, vz, v), NN, preferred_element_type=F32))

        units([(st, 2 * i + 1, BLOCK) for st in range(ns)] + [(st, 2 * i, 0) for st in range(ns)])

        def pair(p, carry):
            jp = i - 1 - p
            units([(st, 2 * jp + 1, None) for st in range(ns)] + [(st, 2 * jp, None) for st in range(ns)])
            return carry

        lax.fori_loop(0, i, pair, 0)
        for st in range(ns):
            cols = slice(st * LANES, (st + 1) * LANES)
            o_ref[:, cols] = acc_ref[st].astype(BF16)
            tot_ref[:, cols] = jnp.where(lo_q, run_ref[st, 0:SB_Q, :], run_ref[st, SB_Q:2 * SB_Q, :])

    def seq(col0):
        return pl.BlockSpec((s, wide), lambda b, h, i: (b, col0 // ns + h))

    blk = pl.BlockSpec((SB_Q, wide), lambda b, h, i: (b * nq + i, h))
    return _call(
        body, name="attn_b_fwd", grid=(batch, n_pairs // ns, nq),
        in_specs=[pl.BlockSpec((SB_Q, wide), lambda b, h, i: (b * nq + i, q_col0 // ns + h)), seq(k_col0), seq(v_col0)],
        out_specs=[blk, blk],
        out_shape=[jax.ShapeDtypeStruct((t, n_pairs * LANES), BF16), jax.ShapeDtypeStruct((t, n_pairs * LANES), F32)],
        scratch=[pltpu.VMEM((ns, SB_Q, LANES), F32), pltpu.VMEM((ns, 2 * SB_Q, LANES), F32)],
        sem=("parallel", "parallel", "arbitrary"), args=(proj, proj, proj), ride=ride)


def _sb_bwd(proj, q_col0, k_col0, v_col0, tot, do, batch, s, ride=None, streams=SB_BWD_STREAMS):
    t = proj.shape[0]
    nq = s // SB_Q
    n_pairs = 4
    ns = streams
    wide = ns * LANES
    scale = HEAD_DIM ** -0.5

    def body(q_ref, k_ref, v_ref, tot_ref, do_ref, dq_ref, dk_ref, dv_ref, dk_acc, dv_acc, dq_acc, seen_ref, gsum_ref):
        i = pl.program_id(2)
        lo_q = lax.broadcasted_iota(jnp.int32, (SB_Q, LANES), 1) < HEAD_DIM
        lo_k = _lane_lo()

        @pl.when(i == 0)
        def _():
            dk_acc[...] = jnp.zeros_like(dk_acc)
            dv_acc[...] = jnp.zeros_like(dv_acc)

        mat_after = _sb_consts(True)
        mat_before = _sb_consts(False)[:BLOCK]
        row = lax.broadcasted_iota(jnp.int32, (2 * SB_Q, LANES), 0) % SB_Q
        ahead = row - lax.broadcasted_iota(jnp.int32, (2 * SB_Q, LANES), 1)
        dq_acc[...] = jnp.zeros_like(dq_acc)
        seen_ref[...] = jnp.zeros_like(seen_ref)
        gsum_ref[...] = jnp.zeros_like(gsum_ref)
        qqs, dds, totals = [], [], []
        for st in range(ns):
            cols = slice(st * LANES, (st + 1) * LANES)
            qqs.append(_stack_heads(q_ref[:, cols] * scale, lo_q))
            dds.append(_stack_heads(do_ref[:, cols], lo_q))
            tot_t = tot_ref[:, cols]
            totals.append(jnp.concatenate([jnp.broadcast_to(tot_t[:, 0:1], (SB_Q, LANES)),
                                           jnp.broadcast_to(tot_t[:, HEAD_DIM:HEAD_DIM + 1], (SB_Q, LANES))], axis=0))

        def units(todo):
            def rows(j):
                return pl.ds(pl.multiple_of(j * BLOCK, BLOCK), BLOCK)

            def cols(st):
                return slice(st * LANES, (st + 1) * LANES)

            prods = [(lax.dot_general(qqs[st], k_ref[rows(j), cols(st)], NT, preferred_element_type=F32),
                      lax.dot_general(dds[st], v_ref[rows(j), cols(st)], NT, preferred_element_type=F32))
                     for st, j, _ in todo]
            logs = []
            for (z, _), (_, _, off) in zip(prods, todo):
                lsig = jnp.minimum(z, 0.0) - jnp.log(1.0 + jnp.exp(-jnp.abs(z)))
                lneg = lsig - z
                if off is not None:
                    lneg = jnp.where(ahead > off, lneg, 0.0)
                logs.append((lsig, _split(lneg)))
            sums = [lax.dot_general(cat, mat_after, NN, preferred_element_type=F32) for _, cat in logs]
            gates = []
            for (lsig, _), sm, (_, da), (st, _, off) in zip(logs, sums, prods, todo):
                seen = seen_ref[st]
                a = jnp.exp(lsig + (totals[st] - seen - sm[:, BLOCK:]) + sm[:, :BLOCK])
                if off is not None:
                    a = jnp.where(ahead > off, a, 0.0)
                seen_ref[st] = seen + sm[:, BLOCK:]
                g = a * da
                gates.append((a.astype(BF16), g, g.astype(BF16)))
            gsums = [lax.dot_general(cat, mat_before, NN, preferred_element_type=F32) for _, _, cat in gates]
            outs = []
            for (lsig, _), (ab, g, _), gs, (st, _, off) in zip(logs, gates, gsums, todo):
                gsum = gsum_ref[st]
                dz = g - jnp.exp(lsig) * (g + gsum + gs[:, :BLOCK])
                if off is not None:
                    dz = jnp.where(ahead > off, dz, 0.0)
                gsum_ref[st] = gsum + gs[:, BLOCK:]
                outs.append((dz.astype(BF16), ab))
            for (dzb, ab), (st, j, _) in zip(outs, todo):
                k = k_ref[rows(j), cols(st)]
                kz = jnp.zeros_like(k)
                dq_acc[st] += (lax.dot_general(dzb[:SB_Q], jnp.where(lo_k, k, kz), NN, preferred_element_type=F32)
                               + lax.dot_general(dzb[SB_Q:], jnp.where(lo_k, kz, k), NN, preferred_element_type=F32))
                dk_acc[rows(j), cols(st)] += lax.dot_general(dzb, qqs[st], TN, preferred_element_type=F32)
                dv_acc[rows(j), cols(st)] += lax.dot_general(ab, dds[st], TN, preferred_element_type=F32)

        def pair(p, carry):
            units([(st, 2 * p, None) for st in range(ns)] + [(st, 2 * p + 1, None) for st in range(ns)])
            return carry

        lax.fori_loop(0, i, pair, 0)
        units([(st, 2 * i, 0) for st in range(ns)] + [(st, 2 * i + 1, BLOCK) for st in range(ns)])
        for st in range(ns):
            dq_ref[:, st * LANES:(st + 1) * LANES] = (dq_acc[st] * scale).astype(BF16)

        @pl.when(i == nq - 1)
        def _():
            dk_ref[...] = dk_acc[...].astype(BF16)
            dv_ref[...] = dv_acc[...].astype(BF16)

    def seq_in(col0):
        return pl.BlockSpec((s, wide), lambda b, h, i: (b, col0 // ns + h))

    blk = pl.BlockSpec((SB_Q, wide), lambda b, h, i: (b * nq + i, h))
    seq = pl.BlockSpec((s, wide), lambda b, h, i: (b, h))
    out = jax.ShapeDtypeStruct((t, n_pairs * LANES), BF16)
    return _call(
        body, name="attn_b_bwd", grid=(batch, n_pairs // ns, nq),
        in_specs=[pl.BlockSpec((SB_Q, wide), lambda b, h, i: (b * nq + i, q_col0 // ns + h)), seq_in(k_col0),
                  seq_in(v_col0), blk, blk],
        out_specs=[blk, seq, seq], out_shape=[out, out, out],
        scratch=[pltpu.VMEM((s, wide), F32), pltpu.VMEM((s, wide), F32), pltpu.VMEM((ns, SB_Q, LANES), F32),
                 pltpu.VMEM((ns, 2 * SB_Q, LANES), F32), pltpu.VMEM((ns, 2 * SB_Q, LANES), F32)],
        sem=("parallel", "parallel", "arbitrary"), args=(proj, proj, proj, tot, do), ride=ride)


MEM_Q_TILE = 512


def _mem_fwd(q, kv, batch, s, n_mem):
    t, width = q.shape
    tq = min(MEM_Q_TILE, s)
    nq = s // tq
    scale = MEM_HEAD_DIM ** -0.5

    def body(q_ref, kv_ref, o_ref):
        for h in range(N_HEADS_MEM):
            cols = slice(h * MEM_HEAD_DIM, (h + 1) * MEM_HEAD_DIM)
            k = kv_ref[:, cols]
            v = kv_ref[:, width + h * MEM_HEAD_DIM: width + (h + 1) * MEM_HEAD_DIM]
            sc = lax.dot_general(q_ref[:, cols], k, NT, preferred_element_type=F32) * scale
            p = jnp.exp(sc - jnp.max(sc, axis=1, keepdims=True))
            p = p / jnp.sum(p, axis=1, keepdims=True)
            o_ref[:, cols] = lax.dot_general(p.astype(BF16), v, NN, preferred_element_type=F32).astype(BF16)

    return pl.pallas_call(
        body, name="mem_attn_fwd", grid=(batch, nq),
        in_specs=[pl.BlockSpec((tq, width), lambda b, i: (b * nq + i, 0)),
                  pl.BlockSpec((n_mem, 2 * width), lambda b, i: (b, 0))],
        out_specs=pl.BlockSpec((tq, width), lambda b, i: (b * nq + i, 0)),
        out_shape=jax.ShapeDtypeStruct((t, width), BF16),
        compiler_params=_params(("parallel", "parallel")),
    )(q, kv)


def _mem_bwd(q, kv, do, batch, s, n_mem):
    t, width = q.shape
    tq = min(MEM_Q_TILE, s)
    nq = s // tq
    scale = MEM_HEAD_DIM ** -0.5

    def body(q_ref, kv_ref, do_ref, dq_ref, dkv_ref, acc):
        i = pl.program_id(1)

        @pl.when(i == 0)
        def _():
            acc[...] = jnp.zeros_like(acc)

        for h in range(N_HEADS_MEM):
            cols = slice(h * MEM_HEAD_DIM, (h + 1) * MEM_HEAD_DIM)
            vcols = slice(width + h * MEM_HEAD_DIM, width + (h + 1) * MEM_HEAD_DIM)
            qh, k, v, doh = q_ref[:, cols], kv_ref[:, cols], kv_ref[:, vcols], do_ref[:, cols]
            sc = lax.dot_general(qh, k, NT, preferred_element_type=F32) * scale
            p = jnp.exp(sc - jnp.max(sc, axis=1, keepdims=True))
            p = p / jnp.sum(p, axis=1, keepdims=True)
            dp = lax.dot_general(doh, v, NT, preferred_element_type=F32)
            ds = (p * (dp - jnp.sum(p * dp, axis=1, keepdims=True)) * scale).astype(BF16)
            dq_ref[:, cols] = lax.dot_general(ds, k, NN, preferred_element_type=F32).astype(BF16)
            acc[:, cols] += lax.dot_general(ds, qh, TN, preferred_element_type=F32)
            acc[:, vcols] += lax.dot_general(p.astype(BF16), doh, TN, preferred_element_type=F32)

        @pl.when(i == nq - 1)
        def _():
            dkv_ref[...] = acc[...].astype(BF16)

    row = pl.BlockSpec((tq, width), lambda b, i: (b * nq + i, 0))
    kvs = pl.BlockSpec((n_mem, 2 * width), lambda b, i: (b, 0))
    return pl.pallas_call(
        body, name="mem_attn_bwd", grid=(batch, nq),
        in_specs=[row, kvs, row], out_specs=[row, kvs],
        out_shape=[jax.ShapeDtypeStruct((t, width), BF16), jax.ShapeDtypeStruct((batch * n_mem, 2 * width), BF16)],
        scratch_shapes=[pltpu.VMEM((n_mem, 2 * width), F32)],
        compiler_params=_params(("parallel", "arbitrary")),
    )(q, kv, do)


def _mixer_fwd(o_a, o_b, w_a, w_b, proj, gate_col0, w_out, x, g, w_q):
    t, width = o_a.shape
    d = w_a.shape[1]
    nq_cols = w_q.shape[1]
    tm = min(ROW_TILE, t)
    gb0 = gate_col0 * LANES // d

    def body(oa_ref, ob_ref, wa_ref, wb_ref, ga_ref, gb_ref, wo_ref, x_ref, g_ref, wq_ref, ua_ref, ub_ref, mix_ref,
             n_ref, h_ref, q_ref):
        ua = lax.dot_general(oa_ref[...], wa_ref[...], NN, preferred_element_type=F32)
        ub = lax.dot_general(ob_ref[...], wb_ref[...], NN, preferred_element_type=F32)
        ua_ref[...] = ua.astype(BF16)
        ub_ref[...] = ub.astype(BF16)
        mixed = (jax.nn.sigmoid(ga_ref[...].astype(F32)) * ua + jax.nn.sigmoid(gb_ref[...].astype(F32)) * ub).astype(BF16)
        mix_ref[...] = mixed
        h = lax.dot_general(mixed, wo_ref[...], NN, preferred_element_type=F32) + x_ref[...]
        h_ref[...] = h
        r = lax.rsqrt(jnp.mean(h * h, axis=-1, keepdims=True) + RMS_EPS)
        n = (h * r * g_ref[...]).astype(BF16)
        n_ref[...] = n
        q_ref[...] = lax.dot_general(n, wq_ref[...], NN, preferred_element_type=F32).astype(BF16)

    row = pl.BlockSpec((tm, width), lambda i: (i, 0))
    wsp = pl.BlockSpec((width, d), lambda i: (0, 0))
    out = pl.BlockSpec((tm, d), lambda i: (i, 0))
    osh = jax.ShapeDtypeStruct((t, d), BF16)
    return pl.pallas_call(
        body, name="mixer_fwd", grid=(t // tm,),
        in_specs=[row, row, wsp, wsp,
                  pl.BlockSpec((tm, d), lambda i: (i, gb0)), pl.BlockSpec((tm, d), lambda i: (i, gb0 + 1)),
                  pl.BlockSpec((d, d), lambda i: (0, 0)), out, pl.BlockSpec((1, d), lambda i: (0, 0)),
                  pl.BlockSpec((d, nq_cols), lambda i: (0, 0))],
        out_specs=[out, out, out, out, out, pl.BlockSpec((tm, nq_cols), lambda i: (i, 0))],
        out_shape=[osh, osh, osh, osh, jax.ShapeDtypeStruct((t, d), F32), jax.ShapeDtypeStruct((t, nq_cols), BF16)],
        compiler_params=_params(("parallel",)),
    )(o_a, o_b, w_a, w_b, proj, proj, w_out, x, g, w_q)


def _mixer_bwd(dh, w_out, ua, ub, proj, gate_col0, w_a, w_b):
    t, d = dh.shape
    width = w_a.shape[0]
    tm = min(ROW_TILE, t)
    nc = d // LANES

    def body(dh_ref, w_ref, ua_ref, ub_ref, ga_ref, gb_ref, wa_ref, wb_ref, dua_ref, dub_ref, dg_ref, doa_ref, dob_ref):
        dm = lax.dot_general(dh_ref[...], w_ref[...], NT, preferred_element_type=F32)
        sa = jax.nn.sigmoid(ga_ref[...].astype(F32))
        sb = jax.nn.sigmoid(gb_ref[...].astype(F32))
        dua = (dm * sa).astype(BF16)
        dub = (dm * sb).astype(BF16)
        dua_ref[...] = dua
        dub_ref[...] = dub
        dg_ref[:, 0:d] = (dm * ua_ref[...].astype(F32) * sa * (1.0 - sa)).astype(BF16)
        dg_ref[:, d:2 * d] = (dm * ub_ref[...].astype(F32) * sb * (1.0 - sb)).astype(BF16)
        doa_ref[...] = lax.dot_general(dua, wa_ref[...], NT, preferred_element_type=F32).astype(BF16)
        dob_ref[...] = lax.dot_general(dub, wb_ref[...], NT, preferred_element_type=F32).astype(BF16)

    row = pl.BlockSpec((tm, d), lambda i: (i, 0))
    wsp = pl.BlockSpec((width, d), lambda i: (0, 0))
    osp = pl.BlockSpec((tm, width), lambda i: (i, 0))
    return pl.pallas_call(
        body, name="mixer_bwd", grid=(t // tm,),
        in_specs=[row, pl.BlockSpec((d, d), lambda i: (0, 0)), row, row,
                  pl.BlockSpec((tm, d), lambda i: (i, gate_col0 // nc)),
                  pl.BlockSpec((tm, d), lambda i: (i, gate_col0 // nc + 1)), wsp, wsp],
        out_specs=[row, row, pl.BlockSpec((tm, 2 * d), lambda i: (i, 0)), osp, osp],
        out_shape=[jax.ShapeDtypeStruct((t, d), BF16), jax.ShapeDtypeStruct((t, d), BF16),
                   jax.ShapeDtypeStruct((t, 2 * d), BF16), jax.ShapeDtypeStruct((t, width), BF16),
                   jax.ShapeDtypeStruct((t, width), BF16)],
        compiler_params=_params(("parallel",)),
    )(dh, w_out, ua, ub, proj, proj, w_a, w_b)


FFN_COLS = 1024
FFN_CHUNK = 256


def _ffn_up(n, w_gate, w_up):
    t, d = n.shape
    hidden = w_gate.shape[0]
    tm = min(ROW_TILE, t)
    tn = min(FFN_COLS, hidden)

    def body(n_ref, wg_ref, wu_ref, hg_ref, hu_ref, act_ref):
        hg = lax.dot_general(n_ref[...], wg_ref[...], NT, preferred_element_type=F32)
        hu = lax.dot_general(n_ref[...], wu_ref[...], NT, preferred_element_type=F32)
        hg_ref[...] = hg.astype(BF16)
        hu_ref[...] = hu.astype(BF16)
        act_ref[...] = (hg * jax.nn.sigmoid(hg) * hu).astype(BF16)

    wsp = pl.BlockSpec((tn, d), lambda j, i: (j, 0))
    out = pl.BlockSpec((tm, tn), lambda j, i: (i, j))
    osh = jax.ShapeDtypeStruct((t, hidden), BF16)
    return pl.pallas_call(
        body, name="ffn_up", grid=(hidden // tn, t // tm),
        in_specs=[pl.BlockSpec((tm, d), lambda j, i: (i, 0)), wsp, wsp],
        out_specs=[out, out, out], out_shape=[osh, osh, osh],
        compiler_params=_params(("parallel", "parallel")),
    )(n, w_gate, w_up)


def _ffn_bwd(dh, w_down, w_gate, w_up, hg, hu, x, g, dres, w_prev):
    t, d = dh.shape
    hidden = w_down.shape[0]
    q = w_prev.shape[0]
    tm = min(ROW_TILE, t)
    tn = min(FFN_COLS, hidden)
    tc = min(FFN_CHUNK, tn)
    nj = hidden // tn

    def body(dh_ref, wd_ref, wg_ref, wu_ref, hg_ref, hu_ref, x_ref, g_ref, r_ref, wp_ref, dhg_ref, dhu_ref, dx_ref,
             dxb_ref, dg_ref, do_ref, acc):
        j, i = pl.program_id(0), pl.program_id(1)
        def d_act(c):
            return lax.dot_general(dh_ref[...], wd_ref[c:c + tc, :], NT, preferred_element_type=F32)

        part = None
        dact = d_act(0)
        for c in range(0, tn, tc):
            dact_next = d_act(c + tc) if c + tc < tn else None
            hg = hg_ref[:, c:c + tc].astype(F32)
            sg = jax.nn.sigmoid(hg)
            dhu = (dact * hg * sg).astype(BF16)
            dhg = (dact * hu_ref[:, c:c + tc].astype(F32) * sg * (1.0 + hg * (1.0 - sg))).astype(BF16)
            dhu_ref[:, c:c + tc] = dhu
            dhg_ref[:, c:c + tc] = dhg
            p = (lax.dot_general(dhg, wg_ref[c:c + tc, :], NN, preferred_element_type=F32)
                 + lax.dot_general(dhu, wu_ref[c:c + tc, :], NN, preferred_element_type=F32))
            part = p if part is None else part + p
            dact = dact_next

        @pl.when(j == 0)
        def _():
            acc[i] = part

        @pl.when(j > 0)
        def _():
            acc[i] += part

        @pl.when(jnp.logical_and(j == 0, i == 0))
        def _():
            dg_ref[...] = jnp.zeros_like(dg_ref)

        @pl.when(j == nj - 1)
        def _():
            dx, dg = _rms_bwd_rows(acc[i], x_ref[...], g_ref[...], r_ref[...])
            dx_ref[...] = dx
            dxb = dx.astype(BF16)
            dxb_ref[...] = dxb
            dg_ref[...] += dg
            do_ref[...] = lax.dot_general(dxb, wp_ref[...], NT, preferred_element_type=F32).astype(BF16)

    hid = pl.BlockSpec((tm, tn), lambda j, i: (i, j))
    wsp = pl.BlockSpec((tn, d), lambda j, i: (j, 0), pipeline_mode=pl.Buffered(1))
    late = pl.BlockSpec((tm, d), lambda j, i: (jnp.where(j == nj - 1, i, 0), 0))
    late_q = pl.BlockSpec((tm, q), lambda j, i: (jnp.where(j == nj - 1, i, 0), 0))
    vec = pl.BlockSpec((1, d), lambda j, i: (0, 0))
    osh = jax.ShapeDtypeStruct((t, hidden), BF16)
    return pl.pallas_call(
        body, name="ffn_bwd", grid=(nj, t // tm),
        in_specs=[pl.BlockSpec((tm, d), lambda j, i: (i, 0)), wsp, wsp, wsp, hid, hid, late, vec, late,
                  pl.BlockSpec((q, d), lambda j, i: (0, 0), pipeline_mode=pl.Buffered(1))],
        out_specs=[hid, hid, late, late, vec, late_q],
        out_shape=[osh, osh, jax.ShapeDtypeStruct((t, d), F32), jax.ShapeDtypeStruct((t, d), BF16),
                   jax.ShapeDtypeStruct((1, d), F32), jax.ShapeDtypeStruct((t, q), BF16)],
        scratch_shapes=[pltpu.VMEM((t // tm, tm, d), F32)],
        compiler_params=_params(("arbitrary", "arbitrary")),
    )(dh, w_down, w_gate, w_up, hg, hu, x, g, dres, w_prev)


MM_ROWS = 1024


def _mm_w(name, a, w, out_dtype, dims=NN):
    t, k = a.shape
    n = w.shape[1] if dims == NN else w.shape[0]
    tm, tn = min(MM_ROWS, t), min(1024, n)
    o_spec = pl.BlockSpec((tm, tn), lambda j, i: (i, j))
    b_spec = pl.BlockSpec((k, tn), lambda j, i: (0, j)) if dims == NN else pl.BlockSpec((tn, k), lambda j, i: (j, 0))
    return _mm(name, a, w, grid=(n // tn, t // tm), a_spec=pl.BlockSpec((tm, k), lambda j, i: (i, 0)), b_spec=b_spec,
               o_shape=(t, n), o_spec=o_spec, dims=dims, out_dtype=out_dtype)


def _mm_res_norm(name, a, w, res, g):
    t, k = a.shape
    d = w.shape[1]
    tm = min(ROW_TILE, t)

    def body(a_ref, w_ref, r_ref, g_ref, h_ref, n_ref):
        h = lax.dot_general(a_ref[...], w_ref[...], NN, preferred_element_type=F32) + r_ref[...]
        h_ref[...] = h
        r = lax.rsqrt(jnp.mean(h * h, axis=-1, keepdims=True) + RMS_EPS)
        n_ref[...] = (h * r * g_ref[...]).astype(BF16)

    row = pl.BlockSpec((tm, d), lambda i: (i, 0))
    return pl.pallas_call(
        body, name=name, grid=(t // tm,),
        in_specs=[pl.BlockSpec((tm, k), lambda i: (i, 0)), pl.BlockSpec((k, d), lambda i: (0, 0)), row,
                  pl.BlockSpec((1, d), lambda i: (0, 0))],
        out_specs=[row, row], out_shape=[jax.ShapeDtypeStruct((t, d), F32), jax.ShapeDtypeStruct((t, d), BF16)],
        compiler_params=_params(("parallel",)),
    )(a, w, res, g)


WGRAD_COLS = 256


def _wgrad(name, a, g, tk=1024, tn=1024):
    t, k = a.shape
    n = g.shape[1]
    tm, tk, tn = min(2 * MM_ROWS, t), min(tk, k), min(tn, n)
    nr = t // tm
    tc = min(WGRAD_COLS, tn)

    def body(a_ref, g_ref, o_ref, *acc):
        def run(first, last):
            for c in range(0, tn, tc):
                p = lax.dot_general(a_ref[...], g_ref[:, c:c + tc], TN, preferred_element_type=F32)
                if not first:
                    p += acc[0][:, c:c + tc]
                if last:
                    o_ref[:, c:c + tc] = p.astype(BF16)
                else:
                    acc[0][:, c:c + tc] = p

        if nr == 1:
            run(True, True)
            return
        r = pl.program_id(2)
        pl.when(r == 0)(functools.partial(run, True, False))
        if nr > 2:
            pl.when(jnp.logical_and(r > 0, r < nr - 1))(functools.partial(run, False, False))
        pl.when(r == nr - 1)(functools.partial(run, False, True))

    return pl.pallas_call(
        body, name=name, grid=(k // tk, n // tn, nr),
        in_specs=[pl.BlockSpec((tm, tk), lambda p, q, r: (r, p)), pl.BlockSpec((tm, tn), lambda p, q, r: (r, q))],
        out_specs=pl.BlockSpec((tk, tn), lambda p, q, r: (p, q)), out_shape=jax.ShapeDtypeStruct((k, n), BF16),
        scratch_shapes=[pltpu.VMEM((tk, tn), F32)] if nr > 1 else [],
        compiler_params=_params(("parallel", "parallel", "arbitrary")),
    )(a, g)


def _peers():
    x, y, c = lax.axis_index("x"), lax.axis_index("y"), lax.axis_index("c")
    me = 4 * x + 2 * y + c
    out = []
    for k in range(1, N_DEV):
        kx, ky, kc = (k >> 2) & 1, (k >> 1) & 1, k & 1
        px = 1 - x if kx else x
        py = 1 - y if ky else y
        pc = 1 - c if kc else c
        out.append(((px, py, pc), 4 * px + 2 * py + pc))
    return me, out


def _cast_weights(ws, pad_rows):
    def body(*refs):
        n = len(refs) // 2
        for i_ref, o_ref, pr in zip(refs[:n], refs[n:], pad_rows):
            r, c = i_ref.shape
            o_ref[0:r, :] = i_ref[...].astype(BF16)
            if pr:
                o_ref[r:r + pr, :] = jnp.zeros((pr, c), BF16)

    return pl.pallas_call(
        body, name="cast_weights", in_specs=[VMEM] * len(ws), out_specs=[VMEM] * len(ws),
        out_shape=[jax.ShapeDtypeStruct((w.shape[0] + pr, w.shape[1]), BF16) for w, pr in zip(ws, pad_rows)],
    )(*ws)


def _window(ref, j, c):
    return ref.at[:, pl.ds(pl.multiple_of(j * c, LANES), c)]


def _direct_copies(ins, outs, sems, gather, cols, landed):
    send_sems, recv_sems, loc_sems = sems
    n_peer = N_DEV - 1
    me, peers = _peers()

    def src(w, j):
        if gather:
            return ins[w]
        return _window(ins[w], j, cols[w]) if cols[w] else ins[w].at[j]

    def dst(w, j):
        return _window(outs[w], j, cols[w]) if gather and cols[w] else outs[w].at[j]

    local = [pltpu.make_async_copy(src(w, me), dst(w, me), loc_sems.at[w]) for w in range(len(ins))]
    remote = [pltpu.make_async_remote_copy(
        src_ref=src(w, idx), dst_ref=dst(w, idx if landed else me),
        send_sem=send_sems.at[w * n_peer + k], recv_sem=recv_sems.at[w * n_peer + k],
        device_id=dev, device_id_type=pl.DeviceIdType.MESH)
        for k, (dev, idx) in reversed(list(enumerate(peers))) for w in range(len(ins))]
    return local, remote


OTHER_CHIPS = (2, 4, 6)


def _gather_copies(ins, outs, sems, cols):
    send_sems, recv_sems, loc_sems = sems
    x, y, c = lax.axis_index("x"), lax.axis_index("y"), lax.axis_index("c")
    me = 4 * x + 2 * y + c
    n_pair = N_DEV - 1

    def dev(mask):
        return (1 - x if mask & 4 else x, 1 - y if mask & 2 else y, 1 - c if mask & 1 else c)

    def slot(w, mask):
        j = jnp.bitwise_xor(me, mask)
        return _window(outs[w], j, cols[w]) if cols[w] else outs[w].at[j]

    def remote(w, pair, src, to_slot, target):
        return pltpu.make_async_remote_copy(src_ref=src, dst_ref=slot(w, to_slot), send_sem=send_sems.at[w * n_pair + pair],
                                            recv_sem=recv_sems.at[w * n_pair + pair], device_id=dev(target),
                                            device_id_type=pl.DeviceIdType.MESH)

    ws = range(len(ins))
    return dict(
        local=[pltpu.make_async_copy(ins[w], slot(w, 0), loc_sems.at[w]) for w in ws],
        to_chips=[remote(w, 1 + t, ins[w], 0, m) for t, m in enumerate(OTHER_CHIPS) for w in ws],
        to_core=[remote(w, 0, ins[w], 0, 1) for w in ws],
        from_chips=[remote(w, 1 + t, ins[w], m, 0) for t, m in enumerate(OTHER_CHIPS) for w in ws],
        pass_on=[remote(w, 4 + t, slot(w, m), m, 1) for t, m in enumerate(OTHER_CHIPS) for w in ws],
        from_core=[remote(w, 0, ins[w], 1, 0) for w in ws]
        + [remote(w, 4 + t, ins[w], m + 1, 0) for t, m in enumerate(OTHER_CHIPS) for w in ws])


TWO_LEVEL = "gather in two levels"


def _exchange_start(ins, outs, sems, gather, cols):
    if gather == TWO_LEVEL:
        cps = _gather_copies(ins, outs, sems, cols)
        for cp in cps["local"] + cps["to_chips"] + cps["to_core"]:
            cp.start()
    else:
        local, remote = _direct_copies(ins, outs, sems, gather, cols, False)
        for cp in local + remote:
            cp.start()


def _exchange_pass_on(ins, outs, sems, gather, cols, chips):
    if gather == TWO_LEVEL:
        cps = _gather_copies(ins, outs, sems, cols)
        n = len(ins)
        for t in chips:
            for arrived, onward in zip(cps["from_chips"][t * n:(t + 1) * n], cps["pass_on"][t * n:(t + 1) * n]):
                arrived.wait_recv()
                onward.start()


def _exchange_wait(ins, outs, sems, gather, cols):
    if gather == TWO_LEVEL:
        cps = _gather_copies(ins, outs, sems, cols)
        for cp in cps["local"]:
            cp.wait()
        for cp in cps["to_chips"] + cps["to_core"] + cps["pass_on"]:
            cp.wait_send()
        for cp in cps["from_core"]:
            cp.wait_recv()
    else:
        local, remote = _direct_copies(ins, outs, sems, gather, cols, True)
        for cp in local:
            cp.wait()
        for cp in remote:
            cp.wait_send()
            cp.wait_recv()


def _exchange_shapes(arrs, gather, cols):
    n = len(arrs)
    out_shape = []
    for a, c in zip(arrs, cols):
        if gather:
            shape = (a.shape[0], N_DEV * c) if c else (N_DEV,) + a.shape
        else:
            shape = (N_DEV, a.shape[0], c) if c else a.shape
        out_shape.append(jax.ShapeDtypeStruct(shape, a.dtype))
    sems = [pltpu.SemaphoreType.DMA((n * (N_DEV - 1),)), pltpu.SemaphoreType.DMA((n * (N_DEV - 1),)),
            pltpu.SemaphoreType.DMA((n,))]
    return out_shape, sems


def _call(body, *, name, grid, in_specs, out_specs, out_shape, scratch, sem, args, ride=None):
    if ride is None:
        outs = pl.pallas_call(body, name=name, grid=grid, in_specs=in_specs, out_specs=out_specs, out_shape=out_shape,
                              scratch_shapes=scratch, compiler_params=_params(sem))(*args)
        return outs, None
    arrs, gather, cols = ride
    n, n_in, n_out, n_scr = len(arrs), len(in_specs), len(out_specs), len(scratch)
    x_shape, x_sems = _exchange_shapes(arrs, gather, cols)

    def riding(*refs):
        ins, x_ins = refs[:n_in], refs[n_in:n_in + n]
        outs = refs[n_in + n:n_in + n + n_out]
        x_outs = refs[n_in + n + n_out:n_in + 2 * n + n_out]
        scr = refs[n_in + 2 * n + n_out:n_in + 2 * n + n_out + n_scr]
        sems = refs[n_in + 2 * n + n_out + n_scr:]
        def at(step):
            return functools.reduce(jnp.logical_and, [pl.program_id(a) == v for a, v in enumerate(step)])

        @pl.when(at((0,) * len(grid)))
        def _():
            _exchange_start(x_ins, x_outs, sems, gather, cols)

        @pl.when(at((grid[0] // 2,) + (0,) * (len(grid) - 2) + (grid[-1] // 2,)))
        def _():
            _exchange_pass_on(x_ins, x_outs, sems, gather, cols, (0, 1))

        @pl.when(at((grid[0] // 2,) + (0,) * (len(grid) - 2) + (3 * grid[-1] // 4,)))
        def _():
            _exchange_pass_on(x_ins, x_outs, sems, gather, cols, (2,))

        body(*ins, *outs, *scr)

        @pl.when(at(tuple(g - 1 for g in grid)))
        def _():
            _exchange_wait(x_ins, x_outs, sems, gather, cols)

    res = pl.pallas_call(
        riding, name=name, grid=grid, in_specs=list(in_specs) + [ANY] * n, out_specs=list(out_specs) + [ANY] * n,
        out_shape=list(out_shape) + x_shape, scratch_shapes=list(scratch) + x_sems,
        compiler_params=_params(("arbitrary",) * len(grid)))(*args, *arrs)
    return res[:n_out], res[n_out:]


def _my_block():
    return (4 * lax.axis_index("x") + 2 * lax.axis_index("y") + lax.axis_index("c")).astype(jnp.int32).reshape(1)


def _proj_in_gather(x, g, w_shard):
    t, k = x.shape
    cs = w_shard.shape[1]
    tm = min(MM_ROWS, t)
    ni = t // tm
    arrival = (0, 1, 2, 4, 3, 5, 6, 7)

    def mask_at(s):
        return jnp.where(s == 3, 4, jnp.where(s == 4, 3, s))

    def body(me_ref, x_ref, g_ref, w_hbm, o_ref, all_hbm, n_hbm, w_vmem, n_vmem, send_sems, recv_sems, loc_sems,
             load_sems, n_sem):
        s, i = pl.program_id(0), pl.program_id(1)
        cps = _gather_copies([w_hbm], [all_hbm], (send_sems, recv_sems, loc_sems), (cs,))
        by_mask = {0: cps["local"][0], 1: cps["from_core"][0]}
        for t_chip, m in enumerate(OTHER_CHIPS):
            by_mask[m] = cps["from_chips"][t_chip]
            by_mask[m + 1] = cps["from_core"][1 + t_chip]
        arrived = [by_mask[m] for m in arrival]

        def load(step):
            src = w_hbm if step == 0 else _window(all_hbm, jnp.bitwise_xor(me_ref[0], arrival[step]), cs)
            return pltpu.make_async_copy(src, w_vmem.at[step % 2], load_sems.at[step % 2])

        @pl.when(jnp.logical_and(s == 0, i == 0))
        def _():
            for cp in cps["local"] + cps["to_chips"] + cps["to_core"]:
                cp.start()
            load(0).start()

        for step, mask in enumerate(arrival):
            @pl.when(jnp.logical_and(s == step, i == 0))
            def _(step=step):
                load(step).wait()

            if step + 1 < N_DEV:
                @pl.when(jnp.logical_and(s == step, i == min(1, ni - 1)))
                def _(step=step):
                    arrived[step + 1].wait_recv()
                    if arrival[step + 1] in OTHER_CHIPS:
                        cps["pass_on"][OTHER_CHIPS.index(arrival[step + 1])].start()
                    load(step + 1).start()

        @pl.when(s == 0)
        def _():
            xf = x_ref[...]
            r = lax.rsqrt(jnp.mean(xf * xf, axis=-1, keepdims=True) + RMS_EPS)
            n_vmem[i] = (xf * r * g_ref[...]).astype(BF16)
            keep = pltpu.make_async_copy(n_vmem.at[i], n_hbm.at[pl.ds(pl.multiple_of(i * tm, tm), tm), :], n_sem)
            keep.start()
            keep.wait()

        o_ref[...] = lax.dot_general(n_vmem[i], w_vmem[s % 2], NN, preferred_element_type=F32).astype(BF16)

        @pl.when(jnp.logical_and(s == N_DEV - 1, i == ni - 1))
        def _():
            cps["local"][0].wait()
            for cp in cps["to_chips"] + cps["to_core"] + cps["pass_on"]:
                cp.wait_send()

    return pl.pallas_call(
        body, name="proj_in",
        grid_spec=pltpu.PrefetchScalarGridSpec(
            num_scalar_prefetch=1, grid=(N_DEV, ni),
            in_specs=[pl.BlockSpec((tm, k), lambda s, i, me: (jnp.where(s == 0, i, 0), 0)),
                      pl.BlockSpec((1, k), lambda s, i, me: (0, 0)), ANY],
            out_specs=[pl.BlockSpec((tm, cs), lambda s, i, me: (i, jnp.bitwise_xor(me[0], mask_at(s)))), ANY, ANY],
            scratch_shapes=[pltpu.VMEM((2, k, cs), BF16), pltpu.VMEM((ni, tm, k), BF16),
                            pltpu.SemaphoreType.DMA((N_DEV - 1,)), pltpu.SemaphoreType.DMA((N_DEV - 1,)),
                            pltpu.SemaphoreType.DMA((1,)), pltpu.SemaphoreType.DMA((2,)), pltpu.SemaphoreType.DMA]),
        out_shape=[jax.ShapeDtypeStruct((t, N_DEV * cs), BF16), jax.ShapeDtypeStruct((k, N_DEV * cs), BF16),
                   jax.ShapeDtypeStruct((t, k), BF16)],
        compiler_params=_params(("arbitrary", "arbitrary")),
    )(_my_block(), x, g, w_shard)


def _gw_in_scatter(a, g):
    t, k = a.shape
    cs = g.shape[1] // N_DEV
    tm = min(MM_ROWS, t)
    nr = t // tm
    n_chip = N_DEV // 2
    chips = (6, 4, 2, 0)

    def body(me_ref, a_ref, g_ref, out_hbm, acc, stage, other, core_send, core_recv, chip_send, chip_recv, loc_sem):
        s, r = pl.program_id(0), pl.program_id(1)
        x, y, c = lax.axis_index("x"), lax.axis_index("y"), lax.axis_index("c")
        my_chip = 2 * x + y
        part = lax.dot_general(a_ref[...], g_ref[...], TN, preferred_element_type=F32)

        def to_core(m):
            return pltpu.make_async_remote_copy(src_ref=stage.at[0], dst_ref=other.at[m], send_sem=core_send.at[m],
                                                recv_sem=core_recv.at[m], device_id=(x, y, 1 - c),
                                                device_id_type=pl.DeviceIdType.MESH)

        def to_chip(m, landed):
            mask = chips[m]
            there = (1 - x if mask & 4 else x, 1 - y if mask & 2 else y, c)
            slot = (2 * there[0] + there[1]) if landed else my_chip
            return pltpu.make_async_remote_copy(src_ref=stage.at[1], dst_ref=out_hbm.at[slot], send_sem=chip_send.at[m],
                                                recv_sem=chip_recv.at[m], device_id=there,
                                                device_id_type=pl.DeviceIdType.MESH)

        local = pltpu.make_async_copy(stage.at[1], out_hbm.at[my_chip], loc_sem)

        @pl.when(r == 0)
        def _():
            acc[...] = part

        @pl.when(r > 0)
        def _():
            acc[...] += part

        for step in range(N_DEV):
            m = step // 2

            @pl.when(jnp.logical_and(s == step, r == nr - 1))
            def _(step=step, m=m):
                if step % 2 == 0:
                    if m > 0:
                        to_core(m - 1).wait_send()
                    stage[0] = acc[...].astype(BF16)
                    to_core(m).start()
                else:
                    if m > 0:
                        to_chip(m - 1, False).wait_send()
                    to_core(m).wait_recv()
                    stage[1] = (acc[...] + other[m].astype(F32)).astype(BF16)
                    if m < n_chip - 1:
                        to_chip(m, False).start()
                    else:
                        local.start()
                        to_core(m).wait_send()
                        local.wait()
                        for mm in range(n_chip - 1):
                            to_chip(mm, True).wait_recv()

    return pl.pallas_call(
        body, name="gw_in",
        grid_spec=pltpu.PrefetchScalarGridSpec(
            num_scalar_prefetch=1, grid=(N_DEV, nr),
            in_specs=[pl.BlockSpec((tm, k), lambda s, r, me: (r, 0)),
                      pl.BlockSpec((tm, cs), lambda s, r, me: (r, jnp.bitwise_xor(me[0], N_DEV - 1 - s)))],
            out_specs=ANY,
            scratch_shapes=[pltpu.VMEM((k, cs), F32), pltpu.VMEM((2, k, cs), BF16), pltpu.VMEM((n_chip, k, cs), BF16),
                            pltpu.SemaphoreType.DMA((n_chip,)), pltpu.SemaphoreType.DMA((n_chip,)),
                            pltpu.SemaphoreType.DMA((n_chip - 1,)), pltpu.SemaphoreType.DMA((n_chip - 1,)),
                            pltpu.SemaphoreType.DMA]),
        out_shape=jax.ShapeDtypeStruct((n_chip, k, cs), BF16),
        compiler_params=_params(("arbitrary", "arbitrary")),
    )(_my_block(), a, g)


SMALL_ROWS = 8


def _allreduce_small(parts, loss_part):
    n, d = len(parts), parts[0].shape[1]

    def body(*refs):
        part_refs, loss_ref, o_ref = refs[:n], refs[n], refs[n + 1]
        mine_ref, all_ref, send_sems, recv_sems = refs[n + 2:]
        me, peers = _peers()
        mine_ref[...] = jnp.zeros_like(mine_ref)
        for i, p_ref in enumerate(part_refs):
            mine_ref[i:i + 1, :] = p_ref[...]
        mine_ref[SMALL_ROWS - 1:SMALL_ROWS, 0:LANES] = loss_ref[0:1, :]
        all_ref[me] = mine_ref[...]
        for k, (dev, idx) in enumerate(peers):
            pltpu.make_async_remote_copy(src_ref=mine_ref, dst_ref=all_ref.at[me], send_sem=send_sems.at[k],
                                         recv_sem=recv_sems.at[k], device_id=dev,
                                         device_id_type=pl.DeviceIdType.MESH).start()
        for k, (dev, idx) in enumerate(peers):
            cp = pltpu.make_async_remote_copy(src_ref=mine_ref, dst_ref=all_ref.at[idx], send_sem=send_sems.at[k],
                                              recv_sem=recv_sems.at[k], device_id=dev,
                                              device_id_type=pl.DeviceIdType.MESH)
            cp.wait_send()
            cp.wait_recv()
        tot = all_ref[0]
        for dvc in range(1, N_DEV):
            tot = tot + all_ref[dvc]
        o_ref[...] = tot

    return pl.pallas_call(
        body, name="allreduce_small", in_specs=[VMEM] * (n + 1), out_specs=VMEM,
        out_shape=jax.ShapeDtypeStruct((SMALL_ROWS, d), F32),
        scratch_shapes=[pltpu.VMEM((SMALL_ROWS, d), F32), pltpu.VMEM((N_DEV, SMALL_ROWS, d), F32),
                        pltpu.SemaphoreType.DMA((N_DEV - 1,)), pltpu.SemaphoreType.DMA((N_DEV - 1,))],
    )(*parts, loss_part)


def _adam_math(g, w, m, v):
    m_new = ADAM_B1 * m + (1.0 - ADAM_B1) * g
    v_new = ADAM_B2 * v + (1.0 - ADAM_B2) * (g * g)
    m_hat = m_new / (1.0 - ADAM_B1 ** ADAM_STEP)
    v_hat = v_new / (1.0 - ADAM_B2 ** ADAM_STEP)
    delta = -ADAM_LR * (m_hat / (jnp.sqrt(v_hat) + ADAM_EPS) + ADAM_WD * w)
    return delta, m_new, v_new


def _adam(name, pieces, w, m, v):
    r, c = w.shape
    n_piece, _, cp = pieces.shape
    tr = r
    for cand in (256, 176, 128, 64):
        if r % cand == 0 and r > cand:
            tr = cand
            break

    def body(p_ref, w_ref, m_ref, v_ref, g_ref, d_ref, mo_ref, vo_ref):
        g = p_ref[0, :, 0:c].astype(F32)
        for j in range(1, n_piece):
            g = g + p_ref[j, :, 0:c].astype(F32)
        delta, m_new, v_new = _adam_math(g, w_ref[...], m_ref[...], v_ref[...])
        g_ref[...] = g
        d_ref[...] = delta
        mo_ref[...] = m_new
        vo_ref[...] = v_new

    blk = pl.BlockSpec((tr, c), lambda i: (i, 0))
    osh = jax.ShapeDtypeStruct((r, c), F32)
    return pl.pallas_call(
        body, name=name, grid=(r // tr,),
        in_specs=[pl.BlockSpec((n_piece, tr, cp), lambda i: (0, i, 0)), blk, blk, blk],
        out_specs=[blk, blk, blk, blk], out_shape=[osh, osh, osh, osh],
        compiler_params=_params(("parallel",)),
    )(pieces, w, m, v)


def _adam_small(g_all, ws, ms, vs):
    n = len(ws)

    def body(*refs):
        g_ref, ins, outs = refs[0], refs[1:1 + 3 * n], refs[1 + 3 * n:]
        for i in range(n):
            g = g_ref[i:i + 1, :]
            delta, m_new, v_new = _adam_math(g, ins[i][...], ins[n + i][...], ins[2 * n + i][...])
            for kind, val in enumerate((g, delta, m_new, v_new)):
                outs[kind * n + i][...] = val

    osh = jax.ShapeDtypeStruct(ws[0].shape, F32)
    res = pl.pallas_call(body, name="adam_small", in_specs=[VMEM] * (1 + 3 * n), out_specs=[VMEM] * (4 * n),
                         out_shape=[osh] * (4 * n))(g_all, *ws, *ms, *vs)
    return res[:n], res[n:2 * n], res[2 * n:3 * n], res[3 * n:]


def _local_step(x, mem, pos, tgt, gains, w_in_shard, shards, batch):
    g_mix, g_mem_q, g_mem_kv, g_ffn, g_final = gains
    t, d = x.shape
    s = t // batch
    n_mem = mem.shape[0] // batch
    n_sh = N_DEV
    width = shards[0].shape[0]
    nb = width // LANES

    lane = np.arange(LANES) % HEAD_DIM
    sel_lo = (lane < ROPE_HALF).astype(np.float32)[None, :]
    sel_hi = ((lane >= ROPE_HALF) & (lane < 2 * ROPE_HALF)).astype(np.float32)[None, :]
    freqs = np.float32(ROPE_THETA) ** (-np.arange(ROPE_HALF, dtype=np.float32) / np.float32(ROPE_HALF))
    inv_freq = np.where(lane < 2 * ROPE_HALF, freqs[lane % ROPE_HALF], 0.0).astype(np.float32)[None, :]
    cos_t, sin_a, sin_b = _rope_tables(pos, jnp.asarray(inv_freq), jnp.asarray(sel_lo), jnp.asarray(sel_hi))
    bias = _dilated_bias_tiles(s)

    proj, w_in, n1 = _proj_in_gather(x, g_mix, w_in_shard)
    qk_a = _rope_apply("rope_fwd", [proj], 2 * width, cos_t, sin_a, sin_b, 1.0)
    cs_up = shards[0].shape[1]
    (o_a, lse_a), (w_up_a, w_up_b, w_out, w_q, w_kv, w_o, w_fd) = _da_fwd(
        qk_a, proj, 2 * nb, bias, batch, s,
        ride=(shards[:6] + shards[8:], TWO_LEVEL, (cs_up, cs_up, 0, 0, 0, cs_up, 0)))
    (o_b, tot_b), (w_fg, w_fu) = _sb_fwd(proj, 3 * nb, 4 * nb, 5 * nb, batch, s, ride=(shards[6:8], True, (0, 0)))
    w_out = w_out.reshape(d, d)
    w_q = w_q.reshape(d, -1)
    w_kv = w_kv.reshape(d, -1)
    w_fd = w_fd.reshape(-1, d)
    w_fg = w_fg.reshape(-1, d)
    w_fu = w_fu.reshape(-1, d)
    ua, ub, mixed, n2, h1, q_m = _mixer_fwd(o_a, o_b, w_up_a, w_up_b, proj, 6 * nb, w_out, x, g_mem_q, w_q)
    mem_n = _rms_fwd("norm_mem_kv", mem, g_mem_kv)
    kv_m = _mm_w("mem_kv", mem_n, w_kv, BF16)
    o_m = _mem_fwd(q_m, kv_m, batch, s, n_mem)
    h2, n3 = _mm_res_norm("mem_out", o_m, w_o, h1, g_ffn)
    hg, hu, act = _ffn_up(n3, w_fg, w_fu)
    loss_part, dh3, dh3_b, dg_final = _loss_head(act, w_fd, h2, tgt, g_final.reshape(1, d))

    dhg, dhu, dh2, dh2_b, dg_ffn, do_m = _ffn_bwd(dh3_b, w_fd, w_fg, w_fu, hg, hu, h2, g_ffn, dh3, w_o)
    gw_fd = _wgrad("gw_ffn_down", act, dh3_b)
    gw_fg = _wgrad("gw_ffn_gate", dhg, n3)
    gw_fu = _wgrad("gw_ffn_up", dhu, n3)

    gw_o = _wgrad("gw_mem_o", o_m, dh2_b)
    dq_m, dkv_m = _mem_bwd(q_m, kv_m, do_m, batch, s, n_mem)
    gw_q = _wgrad("gw_mem_q", n2, dq_m)
    gw_kv = _wgrad("gw_mem_kv", mem_n, dkv_m)
    (dg_mem_kv,) = _rms_bwd("norm_mem_kv_bwd", (dkv_m, w_kv, NT), mem, g_mem_kv, None, ())
    dh1, dh1_b, dg_mem_q = _rms_bwd("norm_mem_q_bwd", (dq_m, w_q, NT), h1, g_mem_q, dh2, ("f32", "bf16"))

    gw_out = _wgrad("gw_out", mixed, dh1_b)
    dua, dub, dgates, do_a, do_b = _mixer_bwd(dh1_b, w_out, ua, ub, proj, 6 * nb, w_up_a, w_up_b)
    gw_ua = _wgrad("gw_up_a", o_a, dua)
    gw_ub = _wgrad("gw_up_b", o_b, dub)
    (dq_ar, dk_ar, dv_a), (p_fg, p_fd) = _da_bwd(
        qk_a, proj, 2 * nb, bias, o_a, lse_a, do_a, batch, s,
        ride=([gw_fg.reshape(n_sh, -1, d), gw_fd.reshape(n_sh, -1, d)], False, (0, 0)))
    mid = [gw_ua, gw_ub, gw_out.reshape(n_sh, -1, d), gw_q.reshape(n_sh, -1, gw_q.shape[1]),
           gw_kv.reshape(n_sh, -1, gw_kv.shape[1]), gw_o, gw_fu.reshape(n_sh, -1, d)]
    (dq_b, dk_b, dv_b), (*p_mid, p_fu) = _sb_bwd(proj, 3 * nb, 4 * nb, 5 * nb, tot_b, do_b, batch, s,
                                                 ride=(mid, False, (cs_up, cs_up, 0, 0, 0, cs_up, 0)))
    p_ffn = [p_fg, p_fu, p_fd]
    dproj = _rope_apply("rope_bwd", [dq_ar, dk_ar], width, cos_t, sin_a, sin_b, -1.0,
                        tail=(dv_a, dq_b, dk_b, dv_b, dgates))
    grad_x, dg_mix = _rms_bwd("proj_in_bwd", (dproj, w_in, NT), x, g_mix, dh1, ("f32",))
    p_in = _gw_in_scatter(n1, dproj)
    return loss_part, grad_x, [p_in] + list(p_mid) + p_ffn, (dg_mix, dg_mem_q, dg_mem_kv, dg_ffn, dg_final)


WEIGHTS =("w_in", "w_up_a", "w_up_b", "w_out", "w_q_mem", "w_kv_mem", "w_o_mem", "w_ffn_gate", "w_ffn_up", "w_ffn_down")
GAINS = ("g_mix", "g_mem_q", "g_mem_kv", "g_ffn", "g_final")
ORDER = ("g_mix", "w_in", "w_up_a", "w_up_b", "w_out", "g_mem_q", "g_mem_kv", "w_q_mem", "w_kv_mem", "w_o_mem", "g_ffn",
         "w_ffn_gate", "w_ffn_up", "w_ffn_down", "g_final")


def kernel(x, mem, positions, g_mix, w_in, w_up_a, w_up_b, w_out, g_mem_q, g_mem_kv, w_q_mem, w_kv_mem, w_o_mem, g_ffn, w_ffn_gate, w_ffn_up, w_ffn_down, g_final, loss_target, m_g_mix, m_w_in, m_w_up_a, m_w_up_b, m_w_out, m_g_mem_q, m_g_mem_kv, m_w_q_mem, m_w_kv_mem, m_w_o_mem, m_g_ffn, m_w_ffn_gate, m_w_ffn_up, m_w_ffn_down, m_g_final, v_g_mix, v_w_in, v_w_up_a, v_w_up_b, v_w_out, v_g_mem_q, v_g_mem_kv, v_w_q_mem, v_w_kv_mem, v_w_o_mem, v_g_ffn, v_w_ffn_gate, v_w_ffn_up, v_w_ffn_down, v_g_final):
    given = dict(locals())
    batch, s, d = x.shape
    t = batch * s
    flipped = ("w_ffn_gate", "w_ffn_up")

    def view(a, n):
        a = a.reshape(a.shape[-2:])
        return a.T if n in flipped else a

    def unview(a, n):
        return (a.T if n in flipped else a).reshape(given[n].shape)

    shard = {n: view(given[n], n) for n in WEIGHTS}
    gains = [given[n].reshape(1, d) for n in GAINS]

    pad = (-shard["w_ffn_down"].shape[0]) % LANES
    cast = _cast_weights([shard[n] for n in WEIGHTS], [pad if n in flipped + ("w_ffn_down",) else 0 for n in WEIGHTS])
    loss_part, grad_x, pieces, dgains = _local_step(
        x.reshape(t, d), mem.reshape(-1, d), positions.reshape(t, 1), loss_target.reshape(t, d), gains, cast[0],
        cast[1:], batch)

    grad, delta, new_m, new_v = {}, {}, {}, {}
    for n, p in zip(WEIGHTS, pieces):
        outs = _adam("adam_" + n, p, shard[n], view(given["m_" + n], n), view(given["v_" + n], n))
        grad[n], delta[n], new_m[n], new_v[n] = [unview(o, n) for o in outs]

    g_all = _allreduce_small(list(dgains), loss_part)
    small = _adam_small(g_all, gains, [given["m_" + n].reshape(1, d) for n in GAINS],
                        [given["v_" + n].reshape(1, d) for n in GAINS])
    for out, vals in zip((grad, delta, new_m, new_v), small):
        for n, val in zip(GAINS, vals):
            out[n] = val.reshape(given[n].shape)

    loss = g_all[SMALL_ROWS - 1, 0]
    return (loss, grad_x.reshape(x.shape), *[grad[n] for n in ORDER], *[delta[n] for n in ORDER],
            *[new_m[n] for n in ORDER], *[new_v[n] for n in ORDER])
```

```python
import functools
import math

import jax
import jax.numpy as jnp
import numpy as np
from jax import lax
from jax.experimental import pallas as pl
from jax.experimental.pallas import tpu as pltpu

F32 = jnp.float32
BF16 = jnp.bfloat16

N_DEV = 8
HEAD_DIM = 64
MEM_HEAD_DIM = 128
N_HEADS_MEM = 4
BLOCK = 128
DIL_PATTERNS = ((128, 1), (512, 4), (2048, 16))
ROPE_THETA = 500000.0
ROPE_HALF = 8
RMS_EPS = 1e-6
ADAM_LR, ADAM_B1, ADAM_B2, ADAM_EPS, ADAM_WD, ADAM_STEP = 0.001, 0.9, 0.999, 1e-08, 0.01, 10
NEG = -1e30
ROW_TILE = 512
LANES = 128

ANY = pl.BlockSpec(memory_space=pl.ANY)
VMEM = pl.BlockSpec(memory_space=pltpu.VMEM)
NN = (((1,), (0,)), ((), ()))
NT = (((1,), (1,)), ((), ()))
TN = (((0,), (0,)), ((), ()))


def _params(sem):
    return pltpu.CompilerParams(dimension_semantics=sem)


def _mm(name, a, b, *, grid, a_spec, b_spec, o_shape, o_spec, dims, out_dtype):
    def body(a_ref, b_ref, o_ref):
        o_ref[...] = lax.dot_general(a_ref[...], b_ref[...], dims, preferred_element_type=F32).astype(out_dtype)

    return pl.pallas_call(
        body, name=name, grid=grid, in_specs=[a_spec, b_spec],
        out_specs=o_spec, out_shape=jax.ShapeDtypeStruct(o_shape, out_dtype),
        compiler_params=_params(("parallel",) * len(grid)),
    )(a, b)


def _rms_fwd(name, x, g):
    t, d = x.shape
    tm = min(ROW_TILE, t)

    def body(x_ref, g_ref, o_ref):
        xf = x_ref[...]
        r = lax.rsqrt(jnp.mean(xf * xf, axis=-1, keepdims=True) + RMS_EPS)
        o_ref[...] = (xf * r * g_ref[...]).astype(BF16)

    return pl.pallas_call(
        body, name=name, grid=(t // tm,),
        in_specs=[pl.BlockSpec((tm, d), lambda i: (i, 0)), pl.BlockSpec((1, d), lambda i: (0, 0))],
        out_specs=pl.BlockSpec((tm, d), lambda i: (i, 0)), out_shape=jax.ShapeDtypeStruct((t, d), BF16),
        compiler_params=_params(("parallel",)),
    )(x, g)


def _rms_bwd_rows(dnf, xf, gv, res):
    r = lax.rsqrt(jnp.mean(xf * xf, axis=-1, keepdims=True) + RMS_EPS)
    xh = xf * r
    dxh = dnf * gv
    dx = r * (dxh - xh * jnp.mean(dxh * xh, axis=-1, keepdims=True))
    if res is not None:
        dx = dx + res
    return dx, jnp.sum(dnf * xh, axis=0, keepdims=True)


def _rms_bwd(name, dn, x, g, dres, want):
    t, d = x.shape
    tm = min(ROW_TILE, t)
    has_res = dres is not None
    lhs = list(dn) if isinstance(dn, tuple) else [dn]
    n_lhs = len(lhs[:2])

    def body(*refs):
        x_ref, g_ref = refs[n_lhs], refs[n_lhs + 1]
        r_ref = refs[n_lhs + 2] if has_res else None
        dx_refs, dg_ref = refs[-1 - len(want):-1], refs[-1]
        if n_lhs == 2:
            dnf = lax.dot_general(refs[0][...], refs[1][...], lhs[2], preferred_element_type=F32)
        else:
            dnf = refs[0][...].astype(F32)
        dx, dg = _rms_bwd_rows(dnf, x_ref[...], g_ref[...], r_ref[...] if has_res else None)
        for kind, dx_ref in zip(want, dx_refs):
            dx_ref[...] = dx.astype(F32 if kind == "f32" else BF16)

        @pl.when(pl.program_id(0) == 0)
        def _():
            dg_ref[...] = jnp.zeros_like(dg_ref)

        dg_ref[...] += dg

    row = pl.BlockSpec((tm, d), lambda i: (i, 0))
    vec = pl.BlockSpec((1, d), lambda i: (0, 0))
    if n_lhs == 2:
        first = [pl.BlockSpec((tm, lhs[0].shape[1]), lambda i: (i, 0)), pl.BlockSpec(lhs[1].shape, lambda i: (0, 0))]
    else:
        first = [row]
    return pl.pallas_call(
        body, name=name, grid=(t // tm,),
        in_specs=first + [row, vec] + ([row] if has_res else []),
        out_specs=[row] * len(want) + [vec],
        out_shape=[jax.ShapeDtypeStruct((t, d), F32 if kind == "f32" else BF16) for kind in want]
        + [jax.ShapeDtypeStruct((1, d), F32)],
        compiler_params=_params(("arbitrary",)),
    )(*(lhs[:2] + [x, g] + ([dres] if has_res else [])))


def _loss_head(a, w, res, tgt, g):
    t, d = res.shape
    k = a.shape[1]
    tm = min(ROW_TILE, t)

    def body(a_ref, w_ref, r_ref, t_ref, g_ref, loss_ref, dh_ref, dhb_ref, dg_ref):
        xf = lax.dot_general(a_ref[...], w_ref[...], NN, preferred_element_type=F32) + r_ref[...]
        gv = g_ref[...]
        r = lax.rsqrt(jnp.mean(xf * xf, axis=-1, keepdims=True) + RMS_EPS)
        xh = xf * r
        e = xh * gv - t_ref[...]
        dy = e * (1.0 / d)
        dxh = dy * gv
        dh = r * (dxh - xh * jnp.mean(dxh * xh, axis=-1, keepdims=True))
        dh_ref[...] = dh
        dhb_ref[...] = dh.astype(BF16)

        @pl.when(pl.program_id(0) == 0)
        def _():
            dg_ref[...] = jnp.zeros_like(dg_ref)
            loss_ref[...] = jnp.zeros_like(loss_ref)

        dg_ref[...] += jnp.sum(dy * xh, axis=0, keepdims=True)
        part = jnp.sum(jnp.sum(e * e, axis=1, keepdims=True), axis=0, keepdims=True) * (0.5 / d)
        loss_ref[...] += jnp.broadcast_to(part, loss_ref.shape)

    row = pl.BlockSpec((tm, d), lambda i: (i, 0))
    vec = pl.BlockSpec((1, d), lambda i: (0, 0))
    return pl.pallas_call(
        body, name="loss_head", grid=(t // tm,),
        in_specs=[pl.BlockSpec((tm, k), lambda i: (i, 0)), pl.BlockSpec((k, d), lambda i: (0, 0)), row, row, vec],
        out_specs=[pl.BlockSpec((8, LANES), lambda i: (0, 0)), row, row, vec],
        out_shape=[jax.ShapeDtypeStruct((8, LANES), F32), jax.ShapeDtypeStruct((t, d), F32),
                   jax.ShapeDtypeStruct((t, d), BF16), jax.ShapeDtypeStruct((1, d), F32)],
        compiler_params=_params(("arbitrary",)),
    )(a, w, res, tgt, g)


def _rope_tables(pos, inv_freq, sel_lo, sel_hi):
    t = pos.shape[0]
    tm = min(ROW_TILE, t)

    def body(p_ref, f_ref, lo_ref, hi_ref, c_ref, sa_ref, sb_ref):
        ang = p_ref[...].astype(F32) * f_ref[...]
        rot = lo_ref[...] + hi_ref[...]
        cs, sn = jnp.cos(ang), jnp.sin(ang)
        c_ref[...] = cs * rot + (1.0 - rot)
        sa_ref[...] = -sn * lo_ref[...]
        sb_ref[...] = sn * hi_ref[...]

    vec = pl.BlockSpec((1, LANES), lambda i: (0, 0))
    row = pl.BlockSpec((tm, LANES), lambda i: (i, 0))
    return pl.pallas_call(
        body, name="rope_tables", grid=(t // tm,),
        in_specs=[pl.BlockSpec((tm, 1), lambda i: (i, 0)), vec, vec, vec],
        out_specs=[row, row, row], out_shape=[jax.ShapeDtypeStruct((t, LANES), F32)] * 3,
        compiler_params=_params(("parallel",)),
    )(pos, inv_freq, sel_lo, sel_hi)


def _rope_apply(name, srcs, width, cos_t, sin_a, sin_b, sign, tail=()):
    t = srcs[0].shape[0]
    tm = min(ROW_TILE, t)
    n_cols = width // LANES
    n_src = len(srcs)

    def body(*refs):
        x_refs, tail_refs = refs[:n_src], refs[n_src:n_src + len(tail)]
        c_ref, sa_ref, sb_ref, o_ref = refs[n_src + len(tail):]
        cs, sa, sb = c_ref[...], sign * sa_ref[...], sign * sb_ref[...]
        for a, x_ref in enumerate(x_refs):
            for c in range(n_cols):
                xf = x_ref[:, c * LANES:(c + 1) * LANES].astype(F32)
                up = pltpu.roll(xf, LANES - ROPE_HALF, 1)
                dn = pltpu.roll(xf, ROPE_HALF, 1)
                o_ref[:, a * width + c * LANES:a * width + (c + 1) * LANES] = (xf * cs + up * sa + dn * sb).astype(BF16)
        col = n_src * width
        for t_ref in tail_refs:
            o_ref[:, col:col + t_ref.shape[1]] = t_ref[...]
            col += t_ref.shape[1]

    wide = n_src * width + sum(a.shape[1] for a in tail)
    tab = pl.BlockSpec((tm, LANES), lambda i: (i, 0))
    return pl.pallas_call(
        body, name=name, grid=(t // tm,),
        in_specs=[pl.BlockSpec((tm, width), lambda i: (i, 0))] * n_src
        + [pl.BlockSpec((tm, a.shape[1]), lambda i: (i, 0)) for a in tail] + [tab, tab, tab],
        out_specs=pl.BlockSpec((tm, wide), lambda i: (i, 0)),
        out_shape=jax.ShapeDtypeStruct((t, wide), BF16),
        compiler_params=_params(("parallel",)),
    )(*srcs, *tail, cos_t, sin_a, sin_b)


DA_T = 256
MIX_STREAMS = 4
SB_BWD_STREAMS = 2


def _lane_lo():
    return lax.broadcasted_iota(jnp.int32, (BLOCK, LANES), 1) < HEAD_DIM


def _dilated_bias_tiles(s):
    n = s // DA_T
    dist = (np.arange(n)[:, None, None] * DA_T + np.arange(DA_T)[None, :, None] - np.arange(DA_T)[None, None, :])
    cnt = np.zeros(dist.shape, np.float32)
    for window, dil in DIL_PATTERNS:
        cnt += ((dist >= 0) & (dist % dil == 0) & (dist <= window)).astype(np.float32)
    return jnp.asarray(np.where(cnt > 0, np.log(np.maximum(cnt, 1.0)), NEG).astype(np.float32))


def _stack_heads(x, lo):
    zero = jnp.zeros_like(x)
    return jnp.concatenate([jnp.where(lo, x, zero), jnp.where(lo, zero, x)], axis=0)


def _da_fwd(qk, proj, v_col0, bias, batch, s, ride=None, streams=MIX_STREAMS):
    t = qk.shape[0]
    nq = s // DA_T
    n_pairs = 4
    ns = streams
    wide = ns * LANES
    scale = HEAD_DIM ** -0.5

    def body(q_ref, k_ref, v_ref, b_ref, o_ref, lse_ref, acc_ref, m_ref, l_ref):
        i = pl.program_id(2)
        lo = lax.broadcasted_iota(jnp.int32, (DA_T, LANES), 1) < HEAD_DIM
        ones = jnp.ones((DA_T, LANES), BF16)
        acc_ref[...] = jnp.zeros_like(acc_ref)
        m_ref[...] = jnp.full(m_ref.shape, NEG, F32)
        l_ref[...] = jnp.zeros_like(l_ref)
        qqs = [_stack_heads(q_ref[:, st * LANES:(st + 1) * LANES] * scale, lo) for st in range(ns)]

        def scores(st, rows, bias2):
            k = k_ref[rows, st * LANES:(st + 1) * LANES]
            return lax.dot_general(qqs[st], k, NT, preferred_element_type=F32) + bias2

        def softmax(st, sc):
            m_old = m_ref[st]
            m_new = jnp.maximum(m_old, jnp.broadcast_to(jnp.max(sc, axis=1, keepdims=True), m_old.shape))
            m_ref[st] = m_new
            return jnp.exp(sc - jnp.concatenate([m_new, m_new], axis=1)).astype(BF16), jnp.exp(m_old - m_new)

        def values(st, rows, p, alpha):
            v = v_ref[rows, st * LANES:(st + 1) * LANES]
            vz = jnp.zeros_like(v)
            l_ref[st] = alpha * l_ref[st] + lax.dot_general(p, ones, NN, preferred_element_type=F32)
            pv = (lax.dot_general(p[:DA_T], jnp.where(lo, v, vz), NN, preferred_element_type=F32)
                  + lax.dot_general(p[DA_T:], jnp.where(lo, vz, v), NN, preferred_element_type=F32))
            acc_ref[st] = acc_ref[st] * jnp.where(lo, alpha[:DA_T], alpha[DA_T:]) + pv

        def trip(dlt, carry):
            rows = pl.ds(pl.multiple_of((i - dlt) * DA_T, DA_T), DA_T)
            bias_t = b_ref[dlt]
            bias2 = jnp.concatenate([bias_t, bias_t], axis=0)
            scs = [scores(st, rows, bias2) for st in range(ns)]
            pas = [softmax(st, scs[st]) for st in range(ns)]
            for st in range(ns):
                values(st, rows, *pas[st])
            return carry

        lax.fori_loop(0, i + 1, trip, 0)
        for st in range(ns):
            cols = slice(st * LANES, (st + 1) * LANES)
            l_t = l_ref[st]
            o_ref[:, cols] = (acc_ref[st] / jnp.where(lo, l_t[:DA_T], l_t[DA_T:])).astype(BF16)
            lse = m_ref[st] + jnp.log(l_t)
            lse_ref[:, cols] = jnp.where(lo, lse[:DA_T], lse[DA_T:])

    blk = pl.BlockSpec((DA_T, wide), lambda b, h, i: (b * nq + i, h))
    return _call(
        body, name="attn_a_fwd", grid=(batch, n_pairs // ns, nq),
        in_specs=[blk,
                  pl.BlockSpec((s, wide), lambda b, h, i: (b, n_pairs // ns + h)),
                  pl.BlockSpec((s, wide), lambda b, h, i: (b, v_col0 // ns + h)),
                  pl.BlockSpec((nq, DA_T, DA_T), lambda b, h, i: (0, 0, 0))],
        out_specs=[blk, blk],
        out_shape=[jax.ShapeDtypeStruct((t, n_pairs * LANES), BF16), jax.ShapeDtypeStruct((t, n_pairs * LANES), F32)],
        scratch=[pltpu.VMEM((ns, DA_T, LANES), F32), pltpu.VMEM((ns, 2 * DA_T, LANES), F32),
                 pltpu.VMEM((ns, 2 * DA_T, LANES), F32)],
        sem=("parallel", "parallel", "arbitrary"), args=(qk, qk, proj, bias), ride=ride)


def _da_bwd(qk, proj, v_col0, bias, o, lse, do, batch, s, ride=None, streams=MIX_STREAMS):
    t = qk.shape[0]
    nq = s // DA_T
    n_pairs = 4
    ns = streams
    wide = ns * LANES
    scale = HEAD_DIM ** -0.5

    def body(q_ref, k_ref, v_ref, b_ref, o_ref, lse_ref, do_ref, dq_ref, dk_ref, dv_ref, dk_acc, dv_acc, dq_acc):
        i = pl.program_id(2)
        lo = lax.broadcasted_iota(jnp.int32, (DA_T, LANES), 1) < HEAD_DIM

        @pl.when(i == 0)
        def _():
            dk_acc[...] = jnp.zeros_like(dk_acc)
            dv_acc[...] = jnp.zeros_like(dv_acc)

        dq_acc[...] = jnp.zeros_like(dq_acc)
        qqs, dds, deltas, lses = [], [], [], []
        for st in range(ns):
            cols = slice(st * LANES, (st + 1) * LANES)
            do_ = do_ref[:, cols]
            qqs.append(_stack_heads(q_ref[:, cols] * scale, lo))
            dds.append(_stack_heads(do_, lo))
            prod = do_.astype(F32) * o_ref[:, cols].astype(F32)
            fz = jnp.zeros_like(prod)
            deltas.append(jnp.concatenate([jnp.sum(jnp.where(lo, prod, fz), axis=1, keepdims=True),
                                           jnp.sum(jnp.where(lo, fz, prod), axis=1, keepdims=True)], axis=0))
            lse_t = lse_ref[:, cols]
            lses.append(jnp.concatenate([lse_t[:, 0:1], lse_t[:, HEAD_DIM:HEAD_DIM + 1]], axis=0))

        def products(st, rows, bias2):
            cols = slice(st * LANES, (st + 1) * LANES)
            sc = lax.dot_general(qqs[st], k_ref[rows, cols], NT, preferred_element_type=F32) + bias2
            return sc, lax.dot_general(dds[st], v_ref[rows, cols], NT, preferred_element_type=F32)

        def weights(st, sc, dp):
            p = jnp.exp(sc - lses[st])
            return (p * (dp - deltas[st])).astype(BF16), p.astype(BF16)

        def gradients(st, rows, ds, p):
            cols = slice(st * LANES, (st + 1) * LANES)
            k = k_ref[rows, cols]
            kz = jnp.zeros_like(k)
            dq_acc[st] += (lax.dot_general(ds[:DA_T], jnp.where(lo, k, kz), NN, preferred_element_type=F32)
                           + lax.dot_general(ds[DA_T:], jnp.where(lo, kz, k), NN, preferred_element_type=F32))
            dk_acc[rows, cols] += lax.dot_general(ds, qqs[st], TN, preferred_element_type=F32)
            dv_acc[rows, cols] += lax.dot_general(p, dds[st], TN, preferred_element_type=F32)

        def trip(dlt, carry):
            rows = pl.ds(pl.multiple_of((i - dlt) * DA_T, DA_T), DA_T)
            bias_t = b_ref[dlt]
            bias2 = jnp.concatenate([bias_t, bias_t], axis=0)
            prods = [products(st, rows, bias2) for st in range(ns)]
            wts = [weights(st, *prods[st]) for st in range(ns)]
            for st in range(ns):
                gradients(st, rows, *wts[st])
            return carry

        lax.fori_loop(0, i + 1, trip, 0)
        for st in range(ns):
            dq_ref[:, st * LANES:(st + 1) * LANES] = (dq_acc[st] * scale).astype(BF16)

        @pl.when(i == nq - 1)
        def _():
            dk_ref[...] = dk_acc[...].astype(BF16)
            dv_ref[...] = dv_acc[...].astype(BF16)

    blk = pl.BlockSpec((DA_T, wide), lambda b, h, i: (b * nq + i, h))
    seq = pl.BlockSpec((s, wide), lambda b, h, i: (b, h), pipeline_mode=pl.Buffered(1))
    one = pl.Buffered(1)
    out = jax.ShapeDtypeStruct((t, n_pairs * LANES), BF16)
    return _call(
        body, name="attn_a_bwd", grid=(batch, n_pairs // ns, nq),
        in_specs=[blk,
                  pl.BlockSpec((s, wide), lambda b, h, i: (b, n_pairs // ns + h), pipeline_mode=one),
                  pl.BlockSpec((s, wide), lambda b, h, i: (b, v_col0 // ns + h), pipeline_mode=one),
                  pl.BlockSpec((nq, DA_T, DA_T), lambda b, h, i: (0, 0, 0), pipeline_mode=one),
                  blk, blk, blk],
        out_specs=[blk, seq, seq], out_shape=[out, out, out],
        scratch=[pltpu.VMEM((s, wide), F32), pltpu.VMEM((s, wide), F32), pltpu.VMEM((ns, DA_T, LANES), F32)],
        sem=("parallel", "parallel", "arbitrary"), args=(qk, qk, proj, bias, o, lse, do), ride=ride)


SB_Q = 256


def _sb_consts(after):
    r = lax.broadcasted_iota(jnp.int32, (2 * BLOCK, 2 * BLOCK), 0) % BLOCK
    c = lax.broadcasted_iota(jnp.int32, (2 * BLOCK, 2 * BLOCK), 1)
    tri = (r > c) if after else (r < c)
    return jnp.logical_or(c >= BLOCK, tri).astype(BF16)


def _split(x):
    hi = x.astype(BF16)
    lo = (x - hi.astype(F32)).astype(BF16)
    return jnp.concatenate([hi, lo], axis=1)


def _sb_fwd(proj, q_col0, k_col0, v_col0, batch, s, ride=None, streams=MIX_STREAMS):
    t = proj.shape[0]
    nq = s // SB_Q
    n_pairs = 4
    ns = streams
    wide = ns * LANES
    scale = HEAD_DIM ** -0.5

    def body(q_ref, k_ref, v_ref, o_ref, tot_ref, acc_ref, run_ref):
        i = pl.program_id(2)
        lo_q = lax.broadcasted_iota(jnp.int32, (SB_Q, LANES), 1) < HEAD_DIM
        lo_k = _lane_lo()
        mat = _sb_consts(True)
        row = lax.broadcasted_iota(jnp.int32, (2 * SB_Q, LANES), 0) % SB_Q
        ahead = row - lax.broadcasted_iota(jnp.int32, (2 * SB_Q, LANES), 1)
        acc_ref[...] = jnp.zeros_like(acc_ref)
        run_ref[...] = jnp.zeros_like(run_ref)
        qqs = [_stack_heads(q_ref[:, st * LANES:(st + 1) * LANES] * scale, lo_q) for st in range(ns)]

        def units(todo):
            def rows(j):
                return pl.ds(pl.multiple_of(j * BLOCK, BLOCK), BLOCK)

            zs = [lax.dot_general(qqs[st], k_ref[rows(j), st * LANES:(st + 1) * LANES], NT, preferred_element_type=F32)
                  for st, j, _ in todo]
            logs = []
            for z, (_, _, off) in zip(zs, todo):
                lsig = jnp.minimum(z, 0.0) - jnp.log(1.0 + jnp.exp(-jnp.abs(z)))
                lneg = lsig - z
                if off is not None:
                    lneg = jnp.where(ahead > off, lneg, 0.0)
                logs.append((lsig, _split(lneg)))
            sums = [lax.dot_general(cat, mat, NN, preferred_element_type=F32) for _, cat in logs]
            probs = []
            for (lsig, _), sm, (st, _, off) in zip(logs, sums, todo):
                run = run_ref[st]
                a = jnp.exp(lsig + run + sm[:, :BLOCK])
                if off is not None:
                    a = jnp.where(ahead > off, a, 0.0)
                run_ref[st] = run + sm[:, BLOCK:]
                probs.append(a.astype(BF16))
            for ab, (st, j, _) in zip(probs, todo):
                v = v_ref[rows(j), st * LANES:(st + 1) * LANES]
                vz = jnp.zeros_like(v)
                acc_ref[st] += (lax.dot_general(ab[:SB_Q], jnp.where(lo_k, v, vz), NN, preferred_element_type=F32)
                                + lax.dot_general(ab[SB_Q:], jnp.where(lo_k, vz, v), NN, preferred_element_type=F32))

        units([(st, 2 * i + 1, BLOCK) for st in range(ns)] + [(st, 2 * i, 0) for st in range(ns)])

        def pair(p, carry):
            jp = i - 1 - p
            units([(st, 2 * jp + 1, None) for st in range(ns)] + [(st, 2 * jp, None) for st in range(ns)])
            return carry

        lax.fori_loop(0, i, pair, 0)
        for st in range(ns):
            cols = slice(st * LANES, (st + 1) * LANES)
            o_ref[:, cols] = acc_ref[st].astype(BF16)
            tot_ref[:, cols] = jnp.where(lo_q, run_ref[st, 0:SB_Q, :], run_ref[st, SB_Q:2 * SB_Q, :])

    def seq(col0):
        return pl.BlockSpec((s, wide), lambda b, h, i: (b, col0 // ns + h))

    blk = pl.BlockSpec((SB_Q, wide), lambda b, h, i: (b * nq + i, h))
    return _call(
        body, name="attn_b_fwd", grid=(batch, n_pairs // ns, nq),
        in_specs=[pl.BlockSpec((SB_Q, wide), lambda b, h, i: (b * nq + i, q_col0 // ns + h)), seq(k_col0), seq(v_col0)],
        out_specs=[blk, blk],
        out_shape=[jax.ShapeDtypeStruct((t, n_pairs * LANES), BF16), jax.ShapeDtypeStruct((t, n_pairs * LANES), F32)],
        scratch=[pltpu.VMEM((ns, SB_Q, LANES), F32), pltpu.VMEM((ns, 2 * SB_Q, LANES), F32)],
        sem=("parallel", "parallel", "arbitrary"), args=(proj, proj, proj), ride=ride)


def _sb_bwd(proj, q_col0, k_col0, v_col0, tot, do, batch, s, ride=None, streams=SB_BWD_STREAMS):
    t = proj.shape[0]
    nq = s // SB_Q
    n_pairs = 4
    ns = streams
    wide = ns * LANES
    scale = HEAD_DIM ** -0.5

    def body(q_ref, k_ref, v_ref, tot_ref, do_ref, dq_ref, dk_ref, dv_ref, dk_acc, dv_acc, dq_acc, seen_ref, gsum_ref):
        i = pl.program_id(2)
        lo_q = lax.broadcasted_iota(jnp.int32, (SB_Q, LANES), 1) < HEAD_DIM
        lo_k = _lane_lo()

        @pl.when(i == 0)
        def _():
            dk_acc[...] = jnp.zeros_like(dk_acc)
            dv_acc[...] = jnp.zeros_like(dv_acc)

        mat_after = _sb_consts(True)
        mat_before = _sb_consts(False)[:BLOCK]
        row = lax.broadcasted_iota(jnp.int32, (2 * SB_Q, LANES), 0) % SB_Q
        ahead = row - lax.broadcasted_iota(jnp.int32, (2 * SB_Q, LANES), 1)
        dq_acc[...] = jnp.zeros_like(dq_acc)
        seen_ref[...] = jnp.zeros_like(seen_ref)
        gsum_ref[...] = jnp.zeros_like(gsum_ref)
        qqs, dds, totals = [], [], []
        for st in range(ns):
            cols = slice(st * LANES, (st + 1) * LANES)
            qqs.append(_stack_heads(q_ref[:, cols] * scale, lo_q))
            dds.append(_stack_heads(do_ref[:, cols], lo_q))
            tot_t = tot_ref[:, cols]
            totals.append(jnp.concatenate([jnp.broadcast_to(tot_t[:, 0:1], (SB_Q, LANES)),
                                           jnp.broadcast_to(tot_t[:, HEAD_DIM:HEAD_DIM + 1], (SB_Q, LANES))], axis=0))

        def units(todo):
            def rows(j):
                return pl.ds(pl.multiple_of(j * BLOCK, BLOCK), BLOCK)

            def cols(st):
                return slice(st * LANES, (st + 1) * LANES)

            prods = [(lax.dot_general(qqs[st], k_ref[rows(j), cols(st)], NT, preferred_element_type=F32),
                      lax.dot_general(dds[st], v_ref[rows(j), cols(st)], NT, preferred_element_type=F32))
                     for st, j, _ in todo]
            logs = []
            for (z, _), (_, _, off) in zip(prods, todo):
                lsig = jnp.minimum(z, 0.0) - jnp.log(1.0 + jnp.exp(-jnp.abs(z)))
                lneg = lsig - z
                if off is not None:
                    lneg = jnp.where(ahead > off, lneg, 0.0)
                logs.append((lsig, _split(lneg)))
            sums = [lax.dot_general(cat, mat_after, NN, preferred_element_type=F32) for _, cat in logs]
            gates = []
            for (lsig, _), sm, (_, da), (st, _, off) in zip(logs, sums, prods, todo):
                seen = seen_ref[st]
                a = jnp.exp(lsig + (totals[st] - seen - sm[:, BLOCK:]) + sm[:, :BLOCK])
                if off is not None:
                    a = jnp.where(ahead > off, a, 0.0)
                seen_ref[st] = seen + sm[:, BLOCK:]
                g = a * da
                gates.append((a.astype(BF16), g, g.astype(BF16)))
            gsums = [lax.dot_general(cat, mat_before, NN, preferred_element_type=F32) for _, _, cat in gates]
            outs = []
            for (lsig, _), (ab, g, _), gs, (st, _, off) in zip(logs, gates, gsums, todo):
                gsum = gsum_ref[st]
                dz = g - jnp.exp(lsig) * (g + gsum + gs[:, :BLOCK])
                if off is not None:
                    dz = jnp.where(ahead > off, dz, 0.0)
                gsum_ref[st] = gsum + gs[:, BLOCK:]
                outs.append((dz.astype(BF16), ab))
            for (dzb, ab), (st, j, _) in zip(outs, todo):
                k = k_ref[rows(j), cols(st)]
                kz = jnp.zeros_like(k)
                dq_acc[st] += (lax.dot_general(dzb[:SB_Q], jnp.where(lo_k, k, kz), NN, preferred_element_type=F32)
                               + lax.dot_general(dzb[SB_Q:], jnp.where(lo_k, kz, k), NN, preferred_element_type=F32))
                dk_acc[rows(j), cols(st)] += lax.dot_general(dzb, qqs[st], TN, preferred_element_type=F32)
                dv_acc[rows(j), cols(st)] += lax.dot_general(ab, dds[st], TN, preferred_element_type=F32)

        def pair(p, carry):
            units([(st, 2 * p, None) for st in range(ns)] + [(st, 2 * p + 1, None) for st in range(ns)])
            return carry

        lax.fori_loop(0, i, pair, 0)
        units([(st, 2 * i, 0) for st in range(ns)] + [(st, 2 * i + 1, BLOCK) for st in range(ns)])
        for st in range(ns):
            dq_ref[:, st * LANES:(st + 1) * LANES] = (dq_acc[st] * scale).astype(BF16)

        @pl.when(i == nq - 1)
        def _():
            dk_ref[...] = dk_acc[...].astype(BF16)
            dv_ref[...] = dv_acc[...].astype(BF16)

    def seq_in(col0):
        return pl.BlockSpec((s, wide), lambda b, h, i: (b, col0 // ns + h))

    blk = pl.BlockSpec((SB_Q, wide), lambda b, h, i: (b * nq + i, h))
    seq = pl.BlockSpec((s, wide), lambda b, h, i: (b, h))
    out = jax.ShapeDtypeStruct((t, n_pairs * LANES), BF16)
    return _call(
        body, name="attn_b_bwd", grid=(batch, n_pairs // ns, nq),
        in_specs=[pl.BlockSpec((SB_Q, wide), lambda b, h, i: (b * nq + i, q_col0 // ns + h)), seq_in(k_col0),
                  seq_in(v_col0), blk, blk],
        out_specs=[blk, seq, seq], out_shape=[out, out, out],
        scratch=[pltpu.VMEM((s, wide), F32), pltpu.VMEM((s, wide), F32), pltpu.VMEM((ns, SB_Q, LANES), F32),
                 pltpu.VMEM((ns, 2 * SB_Q, LANES), F32), pltpu.VMEM((ns, 2 * SB_Q, LANES), F32)],
        sem=("parallel", "parallel", "arbitrary"), args=(proj, proj, proj, tot, do), ride=ride)


MEM_Q_TILE = 512


def _mem_fwd(q, kv, batch, s, n_mem):
    t, width = q.shape
    tq = min(MEM_Q_TILE, s)
    nq = s // tq
    scale = MEM_HEAD_DIM ** -0.5

    def body(q_ref, kv_ref, o_ref):
        for h in range(N_HEADS_MEM):
            cols = slice(h * MEM_HEAD_DIM, (h + 1) * MEM_HEAD_DIM)
            k = kv_ref[:, cols]
            v = kv_ref[:, width + h * MEM_HEAD_DIM: width + (h + 1) * MEM_HEAD_DIM]
            sc = lax.dot_general(q_ref[:, cols], k, NT, preferred_element_type=F32) * scale
            p = jnp.exp(sc - jnp.max(sc, axis=1, keepdims=True))
            p = p / jnp.sum(p, axis=1, keepdims=True)
            o_ref[:, cols] = lax.dot_general(p.astype(BF16), v, NN, preferred_element_type=F32).astype(BF16)

    return pl.pallas_call(
        body, name="mem_attn_fwd", grid=(batch, nq),
        in_specs=[pl.BlockSpec((tq, width), lambda b, i: (b * nq + i, 0)),
                  pl.BlockSpec((n_mem, 2 * width), lambda b, i: (b, 0))],
        out_specs=pl.BlockSpec((tq, width), lambda b, i: (b * nq + i, 0)),
        out_shape=jax.ShapeDtypeStruct((t, width), BF16),
        compiler_params=_params(("parallel", "parallel")),
    )(q, kv)


def _mem_bwd(q, kv, do, batch, s, n_mem):
    t, width = q.shape
    tq = min(MEM_Q_TILE, s)
    nq = s // tq
    scale = MEM_HEAD_DIM ** -0.5

    def body(q_ref, kv_ref, do_ref, dq_ref, dkv_ref, acc):
        i = pl.program_id(1)

        @pl.when(i == 0)
        def _():
            acc[...] = jnp.zeros_like(acc)

        for h in range(N_HEADS_MEM):
            cols = slice(h * MEM_HEAD_DIM, (h + 1) * MEM_HEAD_DIM)
            vcols = slice(width + h * MEM_HEAD_DIM, width + (h + 1) * MEM_HEAD_DIM)
            qh, k, v, doh = q_ref[:, cols], kv_ref[:, cols], kv_ref[:, vcols], do_ref[:, cols]
            sc = lax.dot_general(qh, k, NT, preferred_element_type=F32) * scale
            p = jnp.exp(sc - jnp.max(sc, axis=1, keepdims=True))
            p = p / jnp.sum(p, axis=1, keepdims=True)
            dp = lax.dot_general(doh, v, NT, preferred_element_type=F32)
            ds = (p * (dp - jnp.sum(p * dp, axis=1, keepdims=True)) * scale).astype(BF16)
            dq_ref[:, cols] = lax.dot_general(ds, k, NN, preferred_element_type=F32).astype(BF16)
            acc[:, cols] += lax.dot_general(ds, qh, TN, preferred_element_type=F32)
            acc[:, vcols] += lax.dot_general(p.astype(BF16), doh, TN, preferred_element_type=F32)

        @pl.when(i == nq - 1)
        def _():
            dkv_ref[...] = acc[...].astype(BF16)

    row = pl.BlockSpec((tq, width), lambda b, i: (b * nq + i, 0))
    kvs = pl.BlockSpec((n_mem, 2 * width), lambda b, i: (b, 0))
    return pl.pallas_call(
        body, name="mem_attn_bwd", grid=(batch, nq),
        in_specs=[row, kvs, row], out_specs=[row, kvs],
        out_shape=[jax.ShapeDtypeStruct((t, width), BF16), jax.ShapeDtypeStruct((batch * n_mem, 2 * width), BF16)],
        scratch_shapes=[pltpu.VMEM((n_mem, 2 * width), F32)],
        compiler_params=_params(("parallel", "arbitrary")),
    )(q, kv, do)


def _mixer_fwd(o_a, o_b, w_a, w_b, proj, gate_col0, w_out, x, g, w_q):
    t, width = o_a.shape
    d = w_a.shape[1]
    nq_cols = w_q.shape[1]
    tm = min(ROW_TILE, t)
    gb0 = gate_col0 * LANES // d

    def body(oa_ref, ob_ref, wa_ref, wb_ref, ga_ref, gb_ref, wo_ref, x_ref, g_ref, wq_ref, ua_ref, ub_ref, mix_ref,
             n_ref, h_ref, q_ref):
        ua = lax.dot_general(oa_ref[...], wa_ref[...], NN, preferred_element_type=F32)
        ub = lax.dot_general(ob_ref[...], wb_ref[...], NN, preferred_element_type=F32)
        ua_ref[...] = ua.astype(BF16)
        ub_ref[...] = ub.astype(BF16)
        mixed = (jax.nn.sigmoid(ga_ref[...].astype(F32)) * ua + jax.nn.sigmoid(gb_ref[...].astype(F32)) * ub).astype(BF16)
        mix_ref[...] = mixed
        h = lax.dot_general(mixed, wo_ref[...], NN, preferred_element_type=F32) + x_ref[...]
        h_ref[...] = h
        r = lax.rsqrt(jnp.mean(h * h, axis=-1, keepdims=True) + RMS_EPS)
        n = (h * r * g_ref[...]).astype(BF16)
        n_ref[...] = n
        q_ref[...] = lax.dot_general(n, wq_ref[...], NN, preferred_element_type=F32).astype(BF16)

    row = pl.BlockSpec((tm, width), lambda i: (i, 0))
    wsp = pl.BlockSpec((width, d), lambda i: (0, 0))
    out = pl.BlockSpec((tm, d), lambda i: (i, 0))
    osh = jax.ShapeDtypeStruct((t, d), BF16)
    return pl.pallas_call(
        body, name="mixer_fwd", grid=(t // tm,),
        in_specs=[row, row, wsp, wsp,
                  pl.BlockSpec((tm, d), lambda i: (i, gb0)), pl.BlockSpec((tm, d), lambda i: (i, gb0 + 1)),
                  pl.BlockSpec((d, d), lambda i: (0, 0)), out, pl.BlockSpec((1, d), lambda i: (0, 0)),
                  pl.BlockSpec((d, nq_cols), lambda i: (0, 0))],
        out_specs=[out, out, out, out, out, pl.BlockSpec((tm, nq_cols), lambda i: (i, 0))],
        out_shape=[osh, osh, osh, osh, jax.ShapeDtypeStruct((t, d), F32), jax.ShapeDtypeStruct((t, nq_cols), BF16)],
        compiler_params=_params(("parallel",)),
    )(o_a, o_b, w_a, w_b, proj, proj, w_out, x, g, w_q)


def _mixer_bwd(dh, w_out, ua, ub, proj, gate_col0, w_a, w_b):
    t, d = dh.shape
    width = w_a.shape[0]
    tm = min(ROW_TILE, t)
    nc = d // LANES

    def body(dh_ref, w_ref, ua_ref, ub_ref, ga_ref, gb_ref, wa_ref, wb_ref, dua_ref, dub_ref, dg_ref, doa_ref, dob_ref):
        dm = lax.dot_general(dh_ref[...], w_ref[...], NT, preferred_element_type=F32)
        sa = jax.nn.sigmoid(ga_ref[...].astype(F32))
        sb = jax.nn.sigmoid(gb_ref[...].astype(F32))
        dua = (dm * sa).astype(BF16)
        dub = (dm * sb).astype(BF16)
        dua_ref[...] = dua
        dub_ref[...] = dub
        dg_ref[:, 0:d] = (dm * ua_ref[...].astype(F32) * sa * (1.0 - sa)).astype(BF16)
        dg_ref[:, d:2 * d] = (dm * ub_ref[...].astype(F32) * sb * (1.0 - sb)).astype(BF16)
        doa_ref[...] = lax.dot_general(dua, wa_ref[...], NT, preferred_element_type=F32).astype(BF16)
        dob_ref[...] = lax.dot_general(dub, wb_ref[...], NT, preferred_element_type=F32).astype(BF16)

    row = pl.BlockSpec((tm, d), lambda i: (i, 0))
    wsp = pl.BlockSpec((width, d), lambda i: (0, 0))
    osp = pl.BlockSpec((tm, width), lambda i: (i, 0))
    return pl.pallas_call(
        body, name="mixer_bwd", grid=(t // tm,),
        in_specs=[row, pl.BlockSpec((d, d), lambda i: (0, 0)), row, row,
                  pl.BlockSpec((tm, d), lambda i: (i, gate_col0 // nc)),
                  pl.BlockSpec((tm, d), lambda i: (i, gate_col0 // nc + 1)), wsp, wsp],
        out_specs=[row, row, pl.BlockSpec((tm, 2 * d), lambda i: (i, 0)), osp, osp],
        out_shape=[jax.ShapeDtypeStruct((t, d), BF16), jax.ShapeDtypeStruct((t, d), BF16),
                   jax.ShapeDtypeStruct((t, 2 * d), BF16), jax.ShapeDtypeStruct((t, width), BF16),
                   jax.ShapeDtypeStruct((t, width), BF16)],
        compiler_params=_params(("parallel",)),
    )(dh, w_out, ua, ub, proj, proj, w_a, w_b)


FFN_COLS = 1024


def _ffn_up(n, w_gate, w_up):
    t, d = n.shape
    hidden = w_gate.shape[0]
    tm = min(ROW_TILE, t)
    tn = min(FFN_COLS, hidden)

    def body(n_ref, wg_ref, wu_ref, hg_ref, hu_ref, act_ref):
        hg = lax.dot_general(n_ref[...], wg_ref[...], NT, preferred_element_type=F32)
        hu = lax.dot_general(n_ref[...], wu_ref[...], NT, preferred_element_type=F32)
        hg_ref[...] = hg.astype(BF16)
        hu_ref[...] = hu.astype(BF16)
        act_ref[...] = (hg * jax.nn.sigmoid(hg) * hu).astype(BF16)

    wsp = pl.BlockSpec((tn, d), lambda j, i: (j, 0))
    out = pl.BlockSpec((tm, tn), lambda j, i: (i, j))
    osh = jax.ShapeDtypeStruct((t, hidden), BF16)
    return pl.pallas_call(
        body, name="ffn_up", grid=(hidden // tn, t // tm),
        in_specs=[pl.BlockSpec((tm, d), lambda j, i: (i, 0)), wsp, wsp],
        out_specs=[out, out, out], out_shape=[osh, osh, osh],
        compiler_params=_params(("parallel", "parallel")),
    )(n, w_gate, w_up)


def _ffn_bwd(dh, w_down, w_gate, w_up, hg, hu, x, g, dres, w_prev):
    t, d = dh.shape
    hidden = w_down.shape[0]
    q = w_prev.shape[0]
    tm = min(ROW_TILE, t)
    tn = min(FFN_COLS, hidden)
    nj = hidden // tn

    def body(dh_ref, wd_ref, wg_ref, wu_ref, hg_ref, hu_ref, x_ref, g_ref, r_ref, wp_ref, dhg_ref, dhu_ref, dx_ref,
             dxb_ref, dg_ref, do_ref, acc):
        j, i = pl.program_id(0), pl.program_id(1)
        dact = lax.dot_general(dh_ref[...], wd_ref[...], NT, preferred_element_type=F32)
        hg = hg_ref[...].astype(F32)
        sg = jax.nn.sigmoid(hg)
        dhu = (dact * hg * sg).astype(BF16)
        dhg = (dact * hu_ref[...].astype(F32) * sg * (1.0 + hg * (1.0 - sg))).astype(BF16)
        dhu_ref[...] = dhu
        dhg_ref[...] = dhg
        part = (lax.dot_general(dhg, wg_ref[...], NN, preferred_element_type=F32)
                + lax.dot_general(dhu, wu_ref[...], NN, preferred_element_type=F32))

        @pl.when(j == 0)
        def _():
            acc[i] = part

        @pl.when(j > 0)
        def _():
            acc[i] += part

        @pl.when(jnp.logical_and(j == 0, i == 0))
        def _():
            dg_ref[...] = jnp.zeros_like(dg_ref)

        @pl.when(j == nj - 1)
        def _():
            dx, dg = _rms_bwd_rows(acc[i], x_ref[...], g_ref[...], r_ref[...])
            dx_ref[...] = dx
            dxb = dx.astype(BF16)
            dxb_ref[...] = dxb
            dg_ref[...] += dg
            do_ref[...] = lax.dot_general(dxb, wp_ref[...], NT, preferred_element_type=F32).astype(BF16)

    hid = pl.BlockSpec((tm, tn), lambda j, i: (i, j))
    wsp = pl.BlockSpec((tn, d), lambda j, i: (j, 0), pipeline_mode=pl.Buffered(1))
    late = pl.BlockSpec((tm, d), lambda j, i: (jnp.where(j == nj - 1, i, 0), 0))
    late_q = pl.BlockSpec((tm, q), lambda j, i: (jnp.where(j == nj - 1, i, 0), 0))
    vec = pl.BlockSpec((1, d), lambda j, i: (0, 0))
    osh = jax.ShapeDtypeStruct((t, hidden), BF16)
    return pl.pallas_call(
        body, name="ffn_bwd", grid=(nj, t // tm),
        in_specs=[pl.BlockSpec((tm, d), lambda j, i: (i, 0)), wsp, wsp, wsp, hid, hid, late, vec, late,
                  pl.BlockSpec((q, d), lambda j, i: (0, 0), pipeline_mode=pl.Buffered(1))],
        out_specs=[hid, hid, late, late, vec, late_q],
        out_shape=[osh, osh, jax.ShapeDtypeStruct((t, d), F32), jax.ShapeDtypeStruct((t, d), BF16),
                   jax.ShapeDtypeStruct((1, d), F32), jax.ShapeDtypeStruct((t, q), BF16)],
        scratch_shapes=[pltpu.VMEM((t // tm, tm, d), F32)],
        compiler_params=_params(("arbitrary", "arbitrary")),
    )(dh, w_down, w_gate, w_up, hg, hu, x, g, dres, w_prev)


MM_ROWS = 1024


def _mm_w(name, a, w, out_dtype, dims=NN):
    t, k = a.shape
    n = w.shape[1] if dims == NN else w.shape[0]
    tm, tn = min(MM_ROWS, t), min(1024, n)
    o_spec = pl.BlockSpec((tm, tn), lambda j, i: (i, j))
    b_spec = pl.BlockSpec((k, tn), lambda j, i: (0, j)) if dims == NN else pl.BlockSpec((tn, k), lambda j, i: (j, 0))
    return _mm(name, a, w, grid=(n // tn, t // tm), a_spec=pl.BlockSpec((tm, k), lambda j, i: (i, 0)), b_spec=b_spec,
               o_shape=(t, n), o_spec=o_spec, dims=dims, out_dtype=out_dtype)


def _mm_res_norm(name, a, w, res, g):
    t, k = a.shape
    d = w.shape[1]
    tm = min(ROW_TILE, t)

    def body(a_ref, w_ref, r_ref, g_ref, h_ref, n_ref):
        h = lax.dot_general(a_ref[...], w_ref[...], NN, preferred_element_type=F32) + r_ref[...]
        h_ref[...] = h
        r = lax.rsqrt(jnp.mean(h * h, axis=-1, keepdims=True) + RMS_EPS)
        n_ref[...] = (h * r * g_ref[...]).astype(BF16)

    row = pl.BlockSpec((tm, d), lambda i: (i, 0))
    return pl.pallas_call(
        body, name=name, grid=(t // tm,),
        in_specs=[pl.BlockSpec((tm, k), lambda i: (i, 0)), pl.BlockSpec((k, d), lambda i: (0, 0)), row,
                  pl.BlockSpec((1, d), lambda i: (0, 0))],
        out_specs=[row, row], out_shape=[jax.ShapeDtypeStruct((t, d), F32), jax.ShapeDtypeStruct((t, d), BF16)],
        compiler_params=_params(("parallel",)),
    )(a, w, res, g)


WGRAD_COLS = 256


def _wgrad(name, a, g, tk=1024, tn=1024):
    t, k = a.shape
    n = g.shape[1]
    tk, tn = min(tk, k), min(tn, n)
    np_ = k // tk
    tm = min(2 * MM_ROWS if np_ == 1 else MM_ROWS, t)
    nr = t // tm
    tc = min(WGRAD_COLS, tn)

    def body(a_ref, g_ref, o_ref, *acc):
        def run(first, last):
            mine = acc[0].at[pl.program_id(2)] if acc else None
            for c in range(0, tn, tc):
                p = lax.dot_general(a_ref[...], g_ref[:, c:c + tc], TN, preferred_element_type=F32)
                if not first:
                    p += mine[:, c:c + tc]
                if last:
                    o_ref[:, c:c + tc] = p.astype(BF16)
                else:
                    mine[:, c:c + tc] = p

        if nr == 1:
            run(True, True)
            return
        r = pl.program_id(1)
        pl.when(r == 0)(functools.partial(run, True, False))
        if nr > 2:
            pl.when(jnp.logical_and(r > 0, r < nr - 1))(functools.partial(run, False, False))
        pl.when(r == nr - 1)(functools.partial(run, False, True))

    return pl.pallas_call(
        body, name=name, grid=(n // tn, nr, np_),
        in_specs=[pl.BlockSpec((tm, tk), lambda q, r, p: (r, p)), pl.BlockSpec((tm, tn), lambda q, r, p: (r, q))],
        out_specs=pl.BlockSpec((tk, tn), lambda q, r, p: (jnp.where(r == nr - 1, p, 0), q)),
        out_shape=jax.ShapeDtypeStruct((k, n), BF16),
        scratch_shapes=[pltpu.VMEM((np_, tk, tn), F32)] if nr > 1 else [],
        compiler_params=_params(("parallel", "arbitrary", "arbitrary")),
    )(a, g)


def _peers():
    x, y, c = lax.axis_index("x"), lax.axis_index("y"), lax.axis_index("c")
    me = 4 * x + 2 * y + c
    out = []
    for k in range(1, N_DEV):
        kx, ky, kc = (k >> 2) & 1, (k >> 1) & 1, k & 1
        px = 1 - x if kx else x
        py = 1 - y if ky else y
        pc = 1 - c if kc else c
        out.append(((px, py, pc), 4 * px + 2 * py + pc))
    return me, out


def _cast_weights(ws, pad_rows):
    def body(*refs):
        n = len(refs) // 2
        for i_ref, o_ref, pr in zip(refs[:n], refs[n:], pad_rows):
            r, c = i_ref.shape
            o_ref[0:r, :] = i_ref[...].astype(BF16)
            if pr:
                o_ref[r:r + pr, :] = jnp.zeros((pr, c), BF16)

    return pl.pallas_call(
        body, name="cast_weights", in_specs=[VMEM] * len(ws), out_specs=[VMEM] * len(ws),
        out_shape=[jax.ShapeDtypeStruct((w.shape[0] + pr, w.shape[1]), BF16) for w, pr in zip(ws, pad_rows)],
    )(*ws)


def _window(ref, j, c):
    return ref.at[:, pl.ds(pl.multiple_of(j * c, LANES), c)]


def _direct_copies(ins, outs, sems, gather, cols, landed):
    send_sems, recv_sems, loc_sems = sems
    n_peer = N_DEV - 1
    me, peers = _peers()

    def src(w, j):
        if gather:
            return ins[w]
        return _window(ins[w], j, cols[w]) if cols[w] else ins[w].at[j]

    def dst(w, j):
        return _window(outs[w], j, cols[w]) if gather and cols[w] else outs[w].at[j]

    local = [pltpu.make_async_copy(src(w, me), dst(w, me), loc_sems.at[w]) for w in range(len(ins))]
    remote = [pltpu.make_async_remote_copy(
        src_ref=src(w, idx), dst_ref=dst(w, idx if landed else me),
        send_sem=send_sems.at[w * n_peer + k], recv_sem=recv_sems.at[w * n_peer + k],
        device_id=dev, device_id_type=pl.DeviceIdType.MESH)
        for k, (dev, idx) in reversed(list(enumerate(peers))) for w in range(len(ins))]
    return local, remote


OTHER_CHIPS = (2, 4, 6)


def _gather_copies(ins, outs, sems, cols):
    send_sems, recv_sems, loc_sems = sems
    x, y, c = lax.axis_index("x"), lax.axis_index("y"), lax.axis_index("c")
    me = 4 * x + 2 * y + c
    n_pair = N_DEV - 1

    def dev(mask):
        return (1 - x if mask & 4 else x, 1 - y if mask & 2 else y, 1 - c if mask & 1 else c)

    def slot(w, mask):
        j = jnp.bitwise_xor(me, mask)
        return _window(outs[w], j, cols[w]) if cols[w] else outs[w].at[j]

    def remote(w, pair, src, to_slot, target):
        return pltpu.make_async_remote_copy(src_ref=src, dst_ref=slot(w, to_slot), send_sem=send_sems.at[w * n_pair + pair],
                                            recv_sem=recv_sems.at[w * n_pair + pair], device_id=dev(target),
                                            device_id_type=pl.DeviceIdType.MESH)

    ws = range(len(ins))
    return dict(
        local=[pltpu.make_async_copy(ins[w], slot(w, 0), loc_sems.at[w]) for w in ws],
        to_chips=[remote(w, 1 + t, ins[w], 0, m) for t, m in enumerate(OTHER_CHIPS) for w in ws],
        to_core=[remote(w, 0, ins[w], 0, 1) for w in ws],
        from_chips=[remote(w, 1 + t, ins[w], m, 0) for t, m in enumerate(OTHER_CHIPS) for w in ws],
        pass_on=[remote(w, 4 + t, slot(w, m), m, 1) for t, m in enumerate(OTHER_CHIPS) for w in ws],
        from_core=[remote(w, 0, ins[w], 1, 0) for w in ws]
        + [remote(w, 4 + t, ins[w], m + 1, 0) for t, m in enumerate(OTHER_CHIPS) for w in ws])


TWO_LEVEL = "gather in two levels"


def _exchange_start(ins, outs, sems, gather, cols):
    if gather == TWO_LEVEL:
        cps = _gather_copies(ins, outs, sems, cols)
        for cp in cps["local"] + cps["to_chips"] + cps["to_core"]:
            cp.start()
    else:
        local, remote = _direct_copies(ins, outs, sems, gather, cols, False)
        for cp in local + remote:
            cp.start()


def _exchange_pass_on(ins, outs, sems, gather, cols, chips):
    if gather == TWO_LEVEL:
        cps = _gather_copies(ins, outs, sems, cols)
        n = len(ins)
        for t in chips:
            for arrived, onward in zip(cps["from_chips"][t * n:(t + 1) * n], cps["pass_on"][t * n:(t + 1) * n]):
                arrived.wait_recv()
                onward.start()


def _exchange_wait(ins, outs, sems, gather, cols):
    if gather == TWO_LEVEL:
        cps = _gather_copies(ins, outs, sems, cols)
        for cp in cps["local"]:
            cp.wait()
        for cp in cps["to_chips"] + cps["to_core"] + cps["pass_on"]:
            cp.wait_send()
        for cp in cps["from_core"]:
            cp.wait_recv()
    else:
        local, remote = _direct_copies(ins, outs, sems, gather, cols, True)
        for cp in local:
            cp.wait()
        for cp in remote:
            cp.wait_send()
            cp.wait_recv()


def _exchange_shapes(arrs, gather, cols):
    n = len(arrs)
    out_shape = []
    for a, c in zip(arrs, cols):
        if gather:
            shape = (a.shape[0], N_DEV * c) if c else (N_DEV,) + a.shape
        else:
            shape = (N_DEV, a.shape[0], c) if c else a.shape
        out_shape.append(jax.ShapeDtypeStruct(shape, a.dtype))
    sems = [pltpu.SemaphoreType.DMA((n * (N_DEV - 1),)), pltpu.SemaphoreType.DMA((n * (N_DEV - 1),)),
            pltpu.SemaphoreType.DMA((n,))]
    return out_shape, sems


def _call(body, *, name, grid, in_specs, out_specs, out_shape, scratch, sem, args, ride=None):
    if ride is None:
        outs = pl.pallas_call(body, name=name, grid=grid, in_specs=in_specs, out_specs=out_specs, out_shape=out_shape,
                              scratch_shapes=scratch, compiler_params=_params(sem))(*args)
        return outs, None
    arrs, gather, cols = ride
    n, n_in, n_out, n_scr = len(arrs), len(in_specs), len(out_specs), len(scratch)
    x_shape, x_sems = _exchange_shapes(arrs, gather, cols)

    def riding(*refs):
        ins, x_ins = refs[:n_in], refs[n_in:n_in + n]
        outs = refs[n_in + n:n_in + n + n_out]
        x_outs = refs[n_in + n + n_out:n_in + 2 * n + n_out]
        scr = refs[n_in + 2 * n + n_out:n_in + 2 * n + n_out + n_scr]
        sems = refs[n_in + 2 * n + n_out + n_scr:]
        def at(step):
            return functools.reduce(jnp.logical_and, [pl.program_id(a) == v for a, v in enumerate(step)])

        @pl.when(at((0,) * len(grid)))
        def _():
            _exchange_start(x_ins, x_outs, sems, gather, cols)

        @pl.when(at((grid[0] // 2,) + (0,) * (len(grid) - 2) + (grid[-1] // 2,)))
        def _():
            _exchange_pass_on(x_ins, x_outs, sems, gather, cols, (0, 1))

        @pl.when(at((grid[0] // 2,) + (0,) * (len(grid) - 2) + (3 * grid[-1] // 4,)))
        def _():
            _exchange_pass_on(x_ins, x_outs, sems, gather, cols, (2,))

        body(*ins, *outs, *scr)

        @pl.when(at(tuple(g - 1 for g in grid)))
        def _():
            _exchange_wait(x_ins, x_outs, sems, gather, cols)

    res = pl.pallas_call(
        riding, name=name, grid=grid, in_specs=list(in_specs) + [ANY] * n, out_specs=list(out_specs) + [ANY] * n,
        out_shape=list(out_shape) + x_shape, scratch_shapes=list(scratch) + x_sems,
        compiler_params=_params(("arbitrary",) * len(grid)))(*args, *arrs)
    return res[:n_out], res[n_out:]


def _my_block():
    return (4 * lax.axis_index("x") + 2 * lax.axis_index("y") + lax.axis_index("c")).astype(jnp.int32).reshape(1)


def _proj_in_gather(x, g, w_shard):
    t, k = x.shape
    cs = w_shard.shape[1]
    tm = min(MM_ROWS, t)
    ni = t // tm
    arrival = (0, 1, 2, 4, 3, 5, 6, 7)

    def mask_at(s):
        return jnp.where(s == 3, 4, jnp.where(s == 4, 3, s))

    def body(me_ref, x_ref, g_ref, w_hbm, o_ref, all_hbm, n_hbm, w_vmem, n_vmem, send_sems, recv_sems, loc_sems,
             load_sems, n_sem):
        s, i = pl.program_id(0), pl.program_id(1)
        cps = _gather_copies([w_hbm], [all_hbm], (send_sems, recv_sems, loc_sems), (cs,))
        by_mask = {0: cps["local"][0], 1: cps["from_core"][0]}
        for t_chip, m in enumerate(OTHER_CHIPS):
            by_mask[m] = cps["from_chips"][t_chip]
            by_mask[m + 1] = cps["from_core"][1 + t_chip]
        arrived = [by_mask[m] for m in arrival]

        def load(step):
            src = w_hbm if step == 0 else _window(all_hbm, jnp.bitwise_xor(me_ref[0], arrival[step]), cs)
            return pltpu.make_async_copy(src, w_vmem.at[step % 2], load_sems.at[step % 2])

        @pl.when(jnp.logical_and(s == 0, i == 0))
        def _():
            for cp in cps["local"] + cps["to_chips"] + cps["to_core"]:
                cp.start()
            load(0).start()

        for step, mask in enumerate(arrival):
            @pl.when(jnp.logical_and(s == step, i == 0))
            def _(step=step):
                load(step).wait()

            if step + 1 < N_DEV:
                @pl.when(jnp.logical_and(s == step, i == min(1, ni - 1)))
                def _(step=step):
                    arrived[step + 1].wait_recv()
                    if arrival[step + 1] in OTHER_CHIPS:
                        cps["pass_on"][OTHER_CHIPS.index(arrival[step + 1])].start()
                    load(step + 1).start()

        @pl.when(s == 0)
        def _():
            xf = x_ref[...]
            r = lax.rsqrt(jnp.mean(xf * xf, axis=-1, keepdims=True) + RMS_EPS)
            n_vmem[i] = (xf * r * g_ref[...]).astype(BF16)
            keep = pltpu.make_async_copy(n_vmem.at[i], n_hbm.at[pl.ds(pl.multiple_of(i * tm, tm), tm), :], n_sem)
            keep.start()
            keep.wait()

        o_ref[...] = lax.dot_general(n_vmem[i], w_vmem[s % 2], NN, preferred_element_type=F32).astype(BF16)

        @pl.when(jnp.logical_and(s == N_DEV - 1, i == ni - 1))
        def _():
            cps["local"][0].wait()
            for cp in cps["to_chips"] + cps["to_core"] + cps["pass_on"]:
                cp.wait_send()

    return pl.pallas_call(
        body, name="proj_in",
        grid_spec=pltpu.PrefetchScalarGridSpec(
            num_scalar_prefetch=1, grid=(N_DEV, ni),
            in_specs=[pl.BlockSpec((tm, k), lambda s, i, me: (jnp.where(s == 0, i, 0), 0)),
                      pl.BlockSpec((1, k), lambda s, i, me: (0, 0)), ANY],
            out_specs=[pl.BlockSpec((tm, cs), lambda s, i, me: (i, jnp.bitwise_xor(me[0], mask_at(s)))), ANY, ANY],
            scratch_shapes=[pltpu.VMEM((2, k, cs), BF16), pltpu.VMEM((ni, tm, k), BF16),
                            pltpu.SemaphoreType.DMA((N_DEV - 1,)), pltpu.SemaphoreType.DMA((N_DEV - 1,)),
                            pltpu.SemaphoreType.DMA((1,)), pltpu.SemaphoreType.DMA((2,)), pltpu.SemaphoreType.DMA]),
        out_shape=[jax.ShapeDtypeStruct((t, N_DEV * cs), BF16), jax.ShapeDtypeStruct((k, N_DEV * cs), BF16),
                   jax.ShapeDtypeStruct((t, k), BF16)],
        compiler_params=_params(("arbitrary", "arbitrary")),
    )(_my_block(), x, g, w_shard)


def _gw_in_scatter(a, g):
    t, k = a.shape
    cs = g.shape[1] // N_DEV
    tm = min(MM_ROWS, t)
    nr = t // tm
    n_chip = N_DEV // 2
    chips = (6, 4, 2, 0)

    def body(me_ref, a_ref, g_ref, out_hbm, acc, stage, other, core_send, core_recv, chip_send, chip_recv, loc_sem):
        s, r = pl.program_id(0), pl.program_id(1)
        x, y, c = lax.axis_index("x"), lax.axis_index("y"), lax.axis_index("c")
        my_chip = 2 * x + y
        part = lax.dot_general(a_ref[...], g_ref[...], TN, preferred_element_type=F32)

        def to_core(m):
            return pltpu.make_async_remote_copy(src_ref=stage.at[0], dst_ref=other.at[m], send_sem=core_send.at[m],
                                                recv_sem=core_recv.at[m], device_id=(x, y, 1 - c),
                                                device_id_type=pl.DeviceIdType.MESH)

        def to_chip(m, landed):
            mask = chips[m]
            there = (1 - x if mask & 4 else x, 1 - y if mask & 2 else y, c)
            slot = (2 * there[0] + there[1]) if landed else my_chip
            return pltpu.make_async_remote_copy(src_ref=stage.at[1], dst_ref=out_hbm.at[slot], send_sem=chip_send.at[m],
                                                recv_sem=chip_recv.at[m], device_id=there,
                                                device_id_type=pl.DeviceIdType.MESH)

        local = pltpu.make_async_copy(stage.at[1], out_hbm.at[my_chip], loc_sem)

        @pl.when(r == 0)
        def _():
            acc[...] = part

        @pl.when(r > 0)
        def _():
            acc[...] += part

        for step in range(N_DEV):
            m = step // 2

            @pl.when(jnp.logical_and(s == step, r == nr - 1))
            def _(step=step, m=m):
                if step % 2 == 0:
                    if m > 0:
                        to_core(m - 1).wait_send()
                    stage[0] = acc[...].astype(BF16)
                    to_core(m).start()
                else:
                    if m > 0:
                        to_chip(m - 1, False).wait_send()
                    to_core(m).wait_recv()
                    stage[1] = (acc[...] + other[m].astype(F32)).astype(BF16)
                    if m < n_chip - 1:
                        to_chip(m, False).start()
                    else:
                        local.start()
                        to_core(m).wait_send()
                        local.wait()
                        for mm in range(n_chip - 1):
                            to_chip(mm, True).wait_recv()

    return pl.pallas_call(
        body, name="gw_in",
        grid_spec=pltpu.PrefetchScalarGridSpec(
            num_scalar_prefetch=1, grid=(N_DEV, nr),
            in_specs=[pl.BlockSpec((tm, k), lambda s, r, me: (r, 0)),
                      pl.BlockSpec((tm, cs), lambda s, r, me: (r, jnp.bitwise_xor(me[0], N_DEV - 1 - s)))],
            out_specs=ANY,
            scratch_shapes=[pltpu.VMEM((k, cs), F32), pltpu.VMEM((2, k, cs), BF16), pltpu.VMEM((n_chip, k, cs), BF16),
                            pltpu.SemaphoreType.DMA((n_chip,)), pltpu.SemaphoreType.DMA((n_chip,)),
                            pltpu.SemaphoreType.DMA((n_chip - 1,)), pltpu.SemaphoreType.DMA((n_chip - 1,)),
                            pltpu.SemaphoreType.DMA]),
        out_shape=jax.ShapeDtypeStruct((n_chip, k, cs), BF16),
        compiler_params=_params(("arbitrary", "arbitrary")),
    )(_my_block(), a, g)


SMALL_ROWS = 8


def _allreduce_small(parts, loss_part):
    n, d = len(parts), parts[0].shape[1]

    def body(*refs):
        part_refs, loss_ref, o_ref = refs[:n], refs[n], refs[n + 1]
        mine_ref, all_ref, send_sems, recv_sems = refs[n + 2:]
        me, peers = _peers()
        mine_ref[...] = jnp.zeros_like(mine_ref)
        for i, p_ref in enumerate(part_refs):
            mine_ref[i:i + 1, :] = p_ref[...]
        mine_ref[SMALL_ROWS - 1:SMALL_ROWS, 0:LANES] = loss_ref[0:1, :]
        all_ref[me] = mine_ref[...]
        for k, (dev, idx) in enumerate(peers):
            pltpu.make_async_remote_copy(src_ref=mine_ref, dst_ref=all_ref.at[me], send_sem=send_sems.at[k],
                                         recv_sem=recv_sems.at[k], device_id=dev,
                                         device_id_type=pl.DeviceIdType.MESH).start()
        for k, (dev, idx) in enumerate(peers):
            cp = pltpu.make_async_remote_copy(src_ref=mine_ref, dst_ref=all_ref.at[idx], send_sem=send_sems.at[k],
                                              recv_sem=recv_sems.at[k], device_id=dev,
                                              device_id_type=pl.DeviceIdType.MESH)
            cp.wait_send()
            cp.wait_recv()
        tot = all_ref[0]
        for dvc in range(1, N_DEV):
            tot = tot + all_ref[dvc]
        o_ref[...] = tot

    return pl.pallas_call(
        body, name="allreduce_small", in_specs=[VMEM] * (n + 1), out_specs=VMEM,
        out_shape=jax.ShapeDtypeStruct((SMALL_ROWS, d), F32),
        scratch_shapes=[pltpu.VMEM((SMALL_ROWS, d), F32), pltpu.VMEM((N_DEV, SMALL_ROWS, d), F32),
                        pltpu.SemaphoreType.DMA((N_DEV - 1,)), pltpu.SemaphoreType.DMA((N_DEV - 1,))],
    )(*parts, loss_part)


def _adam_math(g, w, m, v):
    m_new = ADAM_B1 * m + (1.0 - ADAM_B1) * g
    v_new = ADAM_B2 * v + (1.0 - ADAM_B2) * (g * g)
    m_hat = m_new / (1.0 - ADAM_B1 ** ADAM_STEP)
    v_hat = v_new / (1.0 - ADAM_B2 ** ADAM_STEP)
    delta = -ADAM_LR * (m_hat / (jnp.sqrt(v_hat) + ADAM_EPS) + ADAM_WD * w)
    return delta, m_new, v_new


def _adam(name, pieces, w, m, v):
    r, c = w.shape
    n_piece, _, cp = pieces.shape
    tr = r
    for cand in (256, 176, 128, 64):
        if r % cand == 0 and r > cand:
            tr = cand
            break

    def body(p_ref, w_ref, m_ref, v_ref, g_ref, d_ref, mo_ref, vo_ref):
        g = p_ref[0, :, 0:c].astype(F32)
        for j in range(1, n_piece):
            g = g + p_ref[j, :, 0:c].astype(F32)
        delta, m_new, v_new = _adam_math(g, w_ref[...], m_ref[...], v_ref[...])
        g_ref[...] = g
        d_ref[...] = delta
        mo_ref[...] = m_new
        vo_ref[...] = v_new

    blk = pl.BlockSpec((tr, c), lambda i: (i, 0))
    osh = jax.ShapeDtypeStruct((r, c), F32)
    return pl.pallas_call(
        body, name=name, grid=(r // tr,),
        in_specs=[pl.BlockSpec((n_piece, tr, cp), lambda i: (0, i, 0)), blk, blk, blk],
        out_specs=[blk, blk, blk, blk], out_shape=[osh, osh, osh, osh],
        compiler_params=_params(("parallel",)),
    )(pieces, w, m, v)


def _adam_small(g_all, ws, ms, vs):
    n = len(ws)

    def body(*refs):
        g_ref, ins, outs = refs[0], refs[1:1 + 3 * n], refs[1 + 3 * n:]
        for i in range(n):
            g = g_ref[i:i + 1, :]
            delta, m_new, v_new = _adam_math(g, ins[i][...], ins[n + i][...], ins[2 * n + i][...])
            for kind, val in enumerate((g, delta, m_new, v_new)):
                outs[kind * n + i][...] = val

    osh = jax.ShapeDtypeStruct(ws[0].shape, F32)
    res = pl.pallas_call(body, name="adam_small", in_specs=[VMEM] * (1 + 3 * n), out_specs=[VMEM] * (4 * n),
                         out_shape=[osh] * (4 * n))(g_all, *ws, *ms, *vs)
    return res[:n], res[n:2 * n], res[2 * n:3 * n], res[3 * n:]


def _local_step(x, mem, pos, tgt, gains, w_in_shard, shards, batch):
    g_mix, g_mem_q, g_mem_kv, g_ffn, g_final = gains
    t, d = x.shape
    s = t // batch
    n_mem = mem.shape[0] // batch
    n_sh = N_DEV
    width = shards[0].shape[0]
    nb = width // LANES

    lane = np.arange(LANES) % HEAD_DIM
    sel_lo = (lane < ROPE_HALF).astype(np.float32)[None, :]
    sel_hi = ((lane >= ROPE_HALF) & (lane < 2 * ROPE_HALF)).astype(np.float32)[None, :]
    freqs = np.float32(ROPE_THETA) ** (-np.arange(ROPE_HALF, dtype=np.float32) / np.float32(ROPE_HALF))
    inv_freq = np.where(lane < 2 * ROPE_HALF, freqs[lane % ROPE_HALF], 0.0).astype(np.float32)[None, :]
    cos_t, sin_a, sin_b = _rope_tables(pos, jnp.asarray(inv_freq), jnp.asarray(sel_lo), jnp.asarray(sel_hi))
    bias = _dilated_bias_tiles(s)

    proj, w_in, n1 = _proj_in_gather(x, g_mix, w_in_shard)
    qk_a = _rope_apply("rope_fwd", [proj], 2 * width, cos_t, sin_a, sin_b, 1.0)
    cs_up = shards[0].shape[1]
    (o_a, lse_a), (w_up_a, w_up_b, w_out, w_q, w_kv, w_o, w_fd) = _da_fwd(
        qk_a, proj, 2 * nb, bias, batch, s,
        ride=(shards[:6] + shards[8:], TWO_LEVEL, (cs_up, cs_up, 0, 0, 0, cs_up, 0)))
    (o_b, tot_b), (w_fg, w_fu) = _sb_fwd(proj, 3 * nb, 4 * nb, 5 * nb, batch, s, ride=(shards[6:8], True, (0, 0)))
    w_out = w_out.reshape(d, d)
    w_q = w_q.reshape(d, -1)
    w_kv = w_kv.reshape(d, -1)
    w_fd = w_fd.reshape(-1, d)
    w_fg = w_fg.reshape(-1, d)
    w_fu = w_fu.reshape(-1, d)
    ua, ub, mixed, n2, h1, q_m = _mixer_fwd(o_a, o_b, w_up_a, w_up_b, proj, 6 * nb, w_out, x, g_mem_q, w_q)
    mem_n = _rms_fwd("norm_mem_kv", mem, g_mem_kv)
    kv_m = _mm_w("mem_kv", mem_n, w_kv, BF16)
    o_m = _mem_fwd(q_m, kv_m, batch, s, n_mem)
    h2, n3 = _mm_res_norm("mem_out", o_m, w_o, h1, g_ffn)
    hg, hu, act = _ffn_up(n3, w_fg, w_fu)
    loss_part, dh3, dh3_b, dg_final = _loss_head(act, w_fd, h2, tgt, g_final.reshape(1, d))

    dhg, dhu, dh2, dh2_b, dg_ffn, do_m = _ffn_bwd(dh3_b, w_fd, w_fg, w_fu, hg, hu, h2, g_ffn, dh3, w_o)
    gw_fd = _wgrad("gw_ffn_down", act, dh3_b)
    gw_fg = _wgrad("gw_ffn_gate", dhg, n3)
    gw_fu = _wgrad("gw_ffn_up", dhu, n3)

    gw_o = _wgrad("gw_mem_o", o_m, dh2_b)
    dq_m, dkv_m = _mem_bwd(q_m, kv_m, do_m, batch, s, n_mem)
    gw_q = _wgrad("gw_mem_q", n2, dq_m)
    gw_kv = _wgrad("gw_mem_kv", mem_n, dkv_m)
    (dg_mem_kv,) = _rms_bwd("norm_mem_kv_bwd", (dkv_m, w_kv, NT), mem, g_mem_kv, None, ())
    dh1, dh1_b, dg_mem_q = _rms_bwd("norm_mem_q_bwd", (dq_m, w_q, NT), h1, g_mem_q, dh2, ("f32", "bf16"))

    gw_out = _wgrad("gw_out", mixed, dh1_b)
    dua, dub, dgates, do_a, do_b = _mixer_bwd(dh1_b, w_out, ua, ub, proj, 6 * nb, w_up_a, w_up_b)
    gw_ua = _wgrad("gw_up_a", o_a, dua)
    gw_ub = _wgrad("gw_up_b", o_b, dub)
    (dq_ar, dk_ar, dv_a), (p_fg, p_fd) = _da_bwd(
        qk_a, proj, 2 * nb, bias, o_a, lse_a, do_a, batch, s,
        ride=([gw_fg.reshape(n_sh, -1, d), gw_fd.reshape(n_sh, -1, d)], False, (0, 0)))
    mid = [gw_ua, gw_ub, gw_out.reshape(n_sh, -1, d), gw_q.reshape(n_sh, -1, gw_q.shape[1]),
           gw_kv.reshape(n_sh, -1, gw_kv.shape[1]), gw_o, gw_fu.reshape(n_sh, -1, d)]
    (dq_b, dk_b, dv_b), (*p_mid, p_fu) = _sb_bwd(proj, 3 * nb, 4 * nb, 5 * nb, tot_b, do_b, batch, s,
                                                 ride=(mid, False, (cs_up, cs_up, 0, 0, 0, cs_up, 0)))
    p_ffn = [p_fg, p_fu, p_fd]
    dproj = _rope_apply("rope_bwd", [dq_ar, dk_ar], width, cos_t, sin_a, sin_b, -1.0,
                        tail=(dv_a, dq_b, dk_b, dv_b, dgates))
    grad_x, dg_mix = _rms_bwd("proj_in_bwd", (dproj, w_in, NT), x, g_mix, dh1, ("f32",))
    p_in = _gw_in_scatter(n1, dproj)
    return loss_part, grad_x, [p_in] + list(p_mid) + p_ffn, (dg_mix, dg_mem_q, dg_mem_kv, dg_ffn, dg_final)


WEIGHTS =("w_in", "w_up_a", "w_up_b", "w_out", "w_q_mem", "w_kv_mem", "w_o_mem", "w_ffn_gate", "w_ffn_up", "w_ffn_down")
GAINS = ("g_mix", "g_mem_q", "g_mem_kv", "g_ffn", "g_final")
ORDER = ("g_mix", "w_in", "w_up_a", "w_up_b", "w_out", "g_mem_q", "g_mem_kv", "w_q_mem", "w_kv_mem", "w_o_mem", "g_ffn",
         "w_ffn_gate", "w_ffn_up", "w_ffn_down", "g_final")


def kernel(x, mem, positions, g_mix, w_in, w_up_a, w_up_b, w_out, g_mem_q, g_mem_kv, w_q_mem, w_kv_mem, w_o_mem, g_ffn, w_ffn_gate, w_ffn_up, w_ffn_down, g_final, loss_target, m_g_mix, m_w_in, m_w_up_a, m_w_up_b, m_w_out, m_g_mem_q, m_g_mem_kv, m_w_q_mem, m_w_kv_mem, m_w_o_mem, m_g_ffn, m_w_ffn_gate, m_w_ffn_up, m_w_ffn_down, m_g_final, v_g_mix, v_w_in, v_w_up_a, v_w_up_b, v_w_out, v_g_mem_q, v_g_mem_kv, v_w_q_mem, v_w_kv_mem, v_w_o_mem, v_g_ffn, v_w_ffn_gate, v_w_ffn_up, v_w_ffn_down, v_g_final):
    given = dict(locals())
    batch, s, d = x.shape
    t = batch * s
    flipped = ("w_ffn_gate", "w_ffn_up")

    def view(a, n):
        a = a.reshape(a.shape[-2:])
        return a.T if n in flipped else a

    def unview(a, n):
        return (a.T if n in flipped else a).reshape(given[n].shape)

    shard = {n: view(given[n], n) for n in WEIGHTS}
    gains = [given[n].reshape(1, d) for n in GAINS]

    pad = (-shard["w_ffn_down"].shape[0]) % LANES
    cast = _cast_weights([shard[n] for n in WEIGHTS], [pad if n in flipped + ("w_ffn_down",) else 0 for n in WEIGHTS])
    loss_part, grad_x, pieces, dgains = _local_step(
        x.reshape(t, d), mem.reshape(-1, d), positions.reshape(t, 1), loss_target.reshape(t, d), gains, cast[0],
        cast[1:], batch)

    grad, delta, new_m, new_v = {}, {}, {}, {}
    for n, p in zip(WEIGHTS, pieces):
        outs = _adam("adam_" + n, p, shard[n], view(given["m_" + n], n), view(given["v_" + n], n))
        grad[n], delta[n], new_m[n], new_v[n] = [unview(o, n) for o in outs]

    g_all = _allreduce_small(list(dgains), loss_part)
    small = _adam_small(g_all, gains, [given["m_" + n].reshape(1, d) for n in GAINS],
                        [given["v_" + n].reshape(1, d) for n in GAINS])
    for out, vals in zip((grad, delta, new_m, new_v), small):
        for n, val in zip(GAINS, vals):
            out[n] = val.reshape(given[n].shape)

    loss = g_all[SMALL_ROWS - 1, 0]
    return (loss, grad_x.reshape(x.shape), *[grad[n] for n in ORDER], *[delta[n] for n in ORDER],
            *[new_m[n] for n in ORDER], *[new_v[n] for n in ORDER])
```

```python
import functools
import math

import jax
import jax.numpy as jnp
import numpy as np
from jax import lax
from jax.experimental import pallas as pl
from jax.experimental.pallas import tpu as pltpu

F32 = jnp.float32
BF16 = jnp.bfloat16

N_DEV = 8
HEAD_DIM = 64
MEM_HEAD_DIM = 128
N_HEADS_MEM = 4
BLOCK = 128
DIL_PATTERNS = ((128, 1), (512, 4), (2048, 16))
ROPE_THETA = 500000.0
ROPE_HALF = 8
RMS_EPS = 1e-6
ADAM_LR, ADAM_B1, ADAM_B2, ADAM_EPS, ADAM_WD, ADAM_STEP = 0.001, 0.9, 0.999, 1e-08, 0.01, 10
NEG = -1e30
ROW_TILE = 512
LANES = 128

ANY = pl.BlockSpec(memory_space=pl.ANY)
VMEM = pl.BlockSpec(memory_space=pltpu.VMEM)
NN = (((1,), (0,)), ((), ()))
NT = (((1,), (1,)), ((), ()))
TN = (((0,), (0,)), ((), ()))


def _params(sem):
    return pltpu.CompilerParams(dimension_semantics=sem)


def _mm(name, a, b, *, grid, a_spec, b_spec, o_shape, o_spec, dims, out_dtype):
    def body(a_ref, b_ref, o_ref):
        o_ref[...] = lax.dot_general(a_ref[...], b_ref[...], dims, preferred_element_type=F32).astype(out_dtype)

    return pl.pallas_call(
        body, name=name, grid=grid, in_specs=[a_spec, b_spec],
        out_specs=o_spec, out_shape=jax.ShapeDtypeStruct(o_shape, out_dtype),
        compiler_params=_params(("parallel",) * len(grid)),
    )(a, b)


def _rms_fwd(name, x, g):
    t, d = x.shape
    tm = min(ROW_TILE, t)

    def body(x_ref, g_ref, o_ref):
        xf = x_ref[...]
        r = lax.rsqrt(jnp.mean(xf * xf, axis=-1, keepdims=True) + RMS_EPS)
        o_ref[...] = (xf * r * g_ref[...]).astype(BF16)

    return pl.pallas_call(
        body, name=name, grid=(t // tm,),
        in_specs=[pl.BlockSpec((tm, d), lambda i: (i, 0)), pl.BlockSpec((1, d), lambda i: (0, 0))],
        out_specs=pl.BlockSpec((tm, d), lambda i: (i, 0)), out_shape=jax.ShapeDtypeStruct((t, d), BF16),
        compiler_params=_params(("parallel",)),
    )(x, g)


def _rms_bwd_rows(dnf, xf, gv, res):
    r = lax.rsqrt(jnp.mean(xf * xf, axis=-1, keepdims=True) + RMS_EPS)
    xh = xf * r
    dxh = dnf * gv
    dx = r * (dxh - xh * jnp.mean(dxh * xh, axis=-1, keepdims=True))
    if res is not None:
        dx = dx + res
    return dx, jnp.sum(dnf * xh, axis=0, keepdims=True)


def _rms_bwd(name, dn, x, g, dres, want):
    t, d = x.shape
    tm = min(ROW_TILE, t)
    has_res = dres is not None
    lhs = list(dn) if isinstance(dn, tuple) else [dn]
    n_lhs = len(lhs[:2])

    def body(*refs):
        x_ref, g_ref = refs[n_lhs], refs[n_lhs + 1]
        r_ref = refs[n_lhs + 2] if has_res else None
        dx_refs, dg_ref = refs[-1 - len(want):-1], refs[-1]
        if n_lhs == 2:
            dnf = lax.dot_general(refs[0][...], refs[1][...], lhs[2], preferred_element_type=F32)
        else:
            dnf = refs[0][...].astype(F32)
        dx, dg = _rms_bwd_rows(dnf, x_ref[...], g_ref[...], r_ref[...] if has_res else None)
        for kind, dx_ref in zip(want, dx_refs):
            dx_ref[...] = dx.astype(F32 if kind == "f32" else BF16)

        @pl.when(pl.program_id(0) == 0)
        def _():
            dg_ref[...] = jnp.zeros_like(dg_ref)

        dg_ref[...] += dg

    row = pl.BlockSpec((tm, d), lambda i: (i, 0))
    vec = pl.BlockSpec((1, d), lambda i: (0, 0))
    if n_lhs == 2:
        first = [pl.BlockSpec((tm, lhs[0].shape[1]), lambda i: (i, 0)), pl.BlockSpec(lhs[1].shape, lambda i: (0, 0))]
    else:
        first = [row]
    return pl.pallas_call(
        body, name=name, grid=(t // tm,),
        in_specs=first + [row, vec] + ([row] if has_res else []),
        out_specs=[row] * len(want) + [vec],
        out_shape=[jax.ShapeDtypeStruct((t, d), F32 if kind == "f32" else BF16) for kind in want]
        + [jax.ShapeDtypeStruct((1, d), F32)],
        compiler_params=_params(("arbitrary",)),
    )(*(lhs[:2] + [x, g] + ([dres] if has_res else [])))


def _loss_head(a, w, res, tgt, g):
    t, d = res.shape
    k = a.shape[1]
    tm = min(ROW_TILE, t)

    def body(a_ref, w_ref, r_ref, t_ref, g_ref, loss_ref, dh_ref, dhb_ref, dg_ref):
        xf = lax.dot_general(a_ref[...], w_ref[...], NN, preferred_element_type=F32) + r_ref[...]
        gv = g_ref[...]
        r = lax.rsqrt(jnp.mean(xf * xf, axis=-1, keepdims=True) + RMS_EPS)
        xh = xf * r
        e = xh * gv - t_ref[...]
        dy = e * (1.0 / d)
        dxh = dy * gv
        dh = r * (dxh - xh * jnp.mean(dxh * xh, axis=-1, keepdims=True))
        dh_ref[...] = dh
        dhb_ref[...] = dh.astype(BF16)

        @pl.when(pl.program_id(0) == 0)
        def _():
            dg_ref[...] = jnp.zeros_like(dg_ref)
            loss_ref[...] = jnp.zeros_like(loss_ref)

        dg_ref[...] += jnp.sum(dy * xh, axis=0, keepdims=True)
        part = jnp.sum(jnp.sum(e * e, axis=1, keepdims=True), axis=0, keepdims=True) * (0.5 / d)
        loss_ref[...] += jnp.broadcast_to(part, loss_ref.shape)

    row = pl.BlockSpec((tm, d), lambda i: (i, 0))
    vec = pl.BlockSpec((1, d), lambda i: (0, 0))
    return pl.pallas_call(
        body, name="loss_head", grid=(t // tm,),
        in_specs=[pl.BlockSpec((tm, k), lambda i: (i, 0)), pl.BlockSpec((k, d), lambda i: (0, 0)), row, row, vec],
        out_specs=[pl.BlockSpec((8, LANES), lambda i: (0, 0)), row, row, vec],
        out_shape=[jax.ShapeDtypeStruct((8, LANES), F32), jax.ShapeDtypeStruct((t, d), F32),
                   jax.ShapeDtypeStruct((t, d), BF16), jax.ShapeDtypeStruct((1, d), F32)],
        compiler_params=_params(("arbitrary",)),
    )(a, w, res, tgt, g)


def _rope_tables(pos, inv_freq, sel_lo, sel_hi):
    t = pos.shape[0]
    tm = min(ROW_TILE, t)

    def body(p_ref, f_ref, lo_ref, hi_ref, c_ref, sa_ref, sb_ref):
        ang = p_ref[...].astype(F32) * f_ref[...]
        rot = lo_ref[...] + hi_ref[...]
        cs, sn = jnp.cos(ang), jnp.sin(ang)
        c_ref[...] = cs * rot + (1.0 - rot)
        sa_ref[...] = -sn * lo_ref[...]
        sb_ref[...] = sn * hi_ref[...]

    vec = pl.BlockSpec((1, LANES), lambda i: (0, 0))
    row = pl.BlockSpec((tm, LANES), lambda i: (i, 0))
    return pl.pallas_call(
        body, name="rope_tables", grid=(t // tm,),
        in_specs=[pl.BlockSpec((tm, 1), lambda i: (i, 0)), vec, vec, vec],
        out_specs=[row, row, row], out_shape=[jax.ShapeDtypeStruct((t, LANES), F32)] * 3,
        compiler_params=_params(("parallel",)),
    )(pos, inv_freq, sel_lo, sel_hi)


def _rope_apply(name, srcs, width, cos_t, sin_a, sin_b, sign, tail=()):
    t = srcs[0].shape[0]
    tm = min(ROW_TILE, t)
    n_cols = width // LANES
    n_src = len(srcs)

    def body(*refs):
        x_refs, tail_refs = refs[:n_src], refs[n_src:n_src + len(tail)]
        c_ref, sa_ref, sb_ref, o_ref = refs[n_src + len(tail):]
        cs, sa, sb = c_ref[...], sign * sa_ref[...], sign * sb_ref[...]
        for a, x_ref in enumerate(x_refs):
            for c in range(n_cols):
                xf = x_ref[:, c * LANES:(c + 1) * LANES].astype(F32)
                up = pltpu.roll(xf, LANES - ROPE_HALF, 1)
                dn = pltpu.roll(xf, ROPE_HALF, 1)
                o_ref[:, a * width + c * LANES:a * width + (c + 1) * LANES] = (xf * cs + up * sa + dn * sb).astype(BF16)
        col = n_src * width
        for t_ref in tail_refs:
            o_ref[:, col:col + t_ref.shape[1]] = t_ref[...]
            col += t_ref.shape[1]

    wide = n_src * width + sum(a.shape[1] for a in tail)
    tab = pl.BlockSpec((tm, LANES), lambda i: (i, 0))
    return pl.pallas_call(
        body, name=name, grid=(t // tm,),
        in_specs=[pl.BlockSpec((tm, width), lambda i: (i, 0))] * n_src
        + [pl.BlockSpec((tm, a.shape[1]), lambda i: (i, 0)) for a in tail] + [tab, tab, tab],
        out_specs=pl.BlockSpec((tm, wide), lambda i: (i, 0)),
        out_shape=jax.ShapeDtypeStruct((t, wide), BF16),
        compiler_params=_params(("parallel",)),
    )(*srcs, *tail, cos_t, sin_a, sin_b)


DA_T = 256
MIX_STREAMS = 4
SB_BWD_STREAMS = 2


def _lane_lo():
    return lax.broadcasted_iota(jnp.int32, (BLOCK, LANES), 1) < HEAD_DIM


def _dilated_bias_tiles(s):
    n = s // DA_T
    dist = (np.arange(n)[:, None, None] * DA_T + np.arange(DA_T)[None, :, None] - np.arange(DA_T)[None, None, :])
    cnt = np.zeros(dist.shape, np.float32)
    for window, dil in DIL_PATTERNS:
        cnt += ((dist >= 0) & (dist % dil == 0) & (dist <= window)).astype(np.float32)
    return jnp.asarray(np.where(cnt > 0, np.log(np.maximum(cnt, 1.0)), NEG).astype(np.float32))


def _stack_heads(x, lo):
    zero = jnp.zeros_like(x)
    return jnp.concatenate([jnp.where(lo, x, zero), jnp.where(lo, zero, x)], axis=0)


def _da_fwd(qk, proj, v_col0, bias, batch, s, ride=None, streams=MIX_STREAMS):
    t = qk.shape[0]
    nq = s // DA_T
    n_pairs = 4
    ns = streams
    wide = ns * LANES
    scale = HEAD_DIM ** -0.5

    def body(q_ref, k_ref, v_ref, b_ref, o_ref, lse_ref, acc_ref, m_ref, l_ref):
        i = pl.program_id(2)
        lo = lax.broadcasted_iota(jnp.int32, (DA_T, LANES), 1) < HEAD_DIM
        ones = jnp.ones((DA_T, LANES), BF16)
        acc_ref[...] = jnp.zeros_like(acc_ref)
        m_ref[...] = jnp.full(m_ref.shape, NEG, F32)
        l_ref[...] = jnp.zeros_like(l_ref)
        qqs = [_stack_heads(q_ref[:, st * LANES:(st + 1) * LANES] * scale, lo) for st in range(ns)]

        def scores(st, rows, bias2):
            k = k_ref[rows, st * LANES:(st + 1) * LANES]
            return lax.dot_general(qqs[st], k, NT, preferred_element_type=F32) + bias2

        def softmax(st, sc):
            m_old = m_ref[st]
            m_new = jnp.maximum(m_old, jnp.broadcast_to(jnp.max(sc, axis=1, keepdims=True), m_old.shape))
            m_ref[st] = m_new
            return jnp.exp(sc - jnp.concatenate([m_new, m_new], axis=1)).astype(BF16), jnp.exp(m_old - m_new)

        def values(st, rows, p, alpha):
            v = v_ref[rows, st * LANES:(st + 1) * LANES]
            vz = jnp.zeros_like(v)
            l_ref[st] = alpha * l_ref[st] + lax.dot_general(p, ones, NN, preferred_element_type=F32)
            pv = (lax.dot_general(p[:DA_T], jnp.where(lo, v, vz), NN, preferred_element_type=F32)
                  + lax.dot_general(p[DA_T:], jnp.where(lo, vz, v), NN, preferred_element_type=F32))
            acc_ref[st] = acc_ref[st] * jnp.where(lo, alpha[:DA_T], alpha[DA_T:]) + pv

        def trip(dlt, carry):
            rows = pl.ds(pl.multiple_of((i - dlt) * DA_T, DA_T), DA_T)
            bias_t = b_ref[dlt]
            bias2 = jnp.concatenate([bias_t, bias_t], axis=0)
            scs = [scores(st, rows, bias2) for st in range(ns)]
            pas = [softmax(st, scs[st]) for st in range(ns)]
            for st in range(ns):
                values(st, rows, *pas[st])
            return carry

        lax.fori_loop(0, i + 1, trip, 0)
        for st in range(ns):
            cols = slice(st * LANES, (st + 1) * LANES)
            l_t = l_ref[st]
            o_ref[:, cols] = (acc_ref[st] / jnp.where(lo, l_t[:DA_T], l_t[DA_T:])).astype(BF16)
            lse = m_ref[st] + jnp.log(l_t)
            lse_ref[:, cols] = jnp.where(lo, lse[:DA_T], lse[DA_T:])

    blk = pl.BlockSpec((DA_T, wide), lambda b, h, i: (b * nq + i, h))
    return _call(
        body, name="attn_a_fwd", grid=(batch, n_pairs // ns, nq),
        in_specs=[blk,
                  pl.BlockSpec((s, wide), lambda b, h, i: (b, n_pairs // ns + h)),
                  pl.BlockSpec((s, wide), lambda b, h, i: (b, v_col0 // ns + h)),
                  pl.BlockSpec((nq, DA_T, DA_T), lambda b, h, i: (0, 0, 0))],
        out_specs=[blk, blk],
        out_shape=[jax.ShapeDtypeStruct((t, n_pairs * LANES), BF16), jax.ShapeDtypeStruct((t, n_pairs * LANES), F32)],
        scratch=[pltpu.VMEM((ns, DA_T, LANES), F32), pltpu.VMEM((ns, 2 * DA_T, LANES), F32),
                 pltpu.VMEM((ns, 2 * DA_T, LANES), F32)],
        sem=("parallel", "parallel", "arbitrary"), args=(qk, qk, proj, bias), ride=ride)


def _da_bwd(qk, proj, v_col0, bias, o, lse, do, batch, s, ride=None, streams=MIX_STREAMS):
    t = qk.shape[0]
    nq = s // DA_T
    n_pairs = 4
    ns = streams
    wide = ns * LANES
    scale = HEAD_DIM ** -0.5

    def body(q_ref, k_ref, v_ref, b_ref, o_ref, lse_ref, do_ref, dq_ref, dk_ref, dv_ref, dk_acc, dv_acc, dq_acc):
        i = pl.program_id(2)
        lo = lax.broadcasted_iota(jnp.int32, (DA_T, LANES), 1) < HEAD_DIM

        @pl.when(i == 0)
        def _():
            dk_acc[...] = jnp.zeros_like(dk_acc)
            dv_acc[...] = jnp.zeros_like(dv_acc)

        dq_acc[...] = jnp.zeros_like(dq_acc)
        qqs, dds, deltas, lses = [], [], [], []
        for st in range(ns):
            cols = slice(st * LANES, (st + 1) * LANES)
            do_ = do_ref[:, cols]
            qqs.append(_stack_heads(q_ref[:, cols] * scale, lo))
            dds.append(_stack_heads(do_, lo))
            prod = do_.astype(F32) * o_ref[:, cols].astype(F32)
            fz = jnp.zeros_like(prod)
            deltas.append(jnp.concatenate([jnp.sum(jnp.where(lo, prod, fz), axis=1, keepdims=True),
                                           jnp.sum(jnp.where(lo, fz, prod), axis=1, keepdims=True)], axis=0))
            lse_t = lse_ref[:, cols]
            lses.append(jnp.concatenate([lse_t[:, 0:1], lse_t[:, HEAD_DIM:HEAD_DIM + 1]], axis=0))

        def products(st, rows, bias2):
            cols = slice(st * LANES, (st + 1) * LANES)
            sc = lax.dot_general(qqs[st], k_ref[rows, cols], NT, preferred_element_type=F32) + bias2
            return sc, lax.dot_general(dds[st], v_ref[rows, cols], NT, preferred_element_type=F32)

        def weights(st, sc, dp):
            p = jnp.exp(sc - lses[st])
            return (p * (dp - deltas[st])).astype(BF16), p.astype(BF16)

        def gradients(st, rows, ds, p):
            cols = slice(st * LANES, (st + 1) * LANES)
            k = k_ref[rows, cols]
            kz = jnp.zeros_like(k)
            dq_acc[st] += (lax.dot_general(ds[:DA_T], jnp.where(lo, k, kz), NN, preferred_element_type=F32)
                           + lax.dot_general(ds[DA_T:], jnp.where(lo, kz, k), NN, preferred_element_type=F32))
            dk_acc[rows, cols] += lax.dot_general(ds, qqs[st], TN, preferred_element_type=F32)
            dv_acc[rows, cols] += lax.dot_general(p, dds[st], TN, preferred_element_type=F32)

        def trip(dlt, carry):
            rows = pl.ds(pl.multiple_of((i - dlt) * DA_T, DA_T), DA_T)
            bias_t = b_ref[dlt]
            bias2 = jnp.concatenate([bias_t, bias_t], axis=0)
            prods = [products(st, rows, bias2) for st in range(ns)]
            wts = [weights(st, *prods[st]) for st in range(ns)]
            for st in range(ns):
                gradients(st, rows, *wts[st])
            return carry

        lax.fori_loop(0, i + 1, trip, 0)
        for st in range(ns):
            dq_ref[:, st * LANES:(st + 1) * LANES] = (dq_acc[st] * scale).astype(BF16)

        @pl.when(i == nq - 1)
        def _():
            dk_ref[...] = dk_acc[...].astype(BF16)
            dv_ref[...] = dv_acc[...].astype(BF16)

    blk = pl.BlockSpec((DA_T, wide), lambda b, h, i: (b * nq + i, h))
    seq = pl.BlockSpec((s, wide), lambda b, h, i: (b, h), pipeline_mode=pl.Buffered(1))
    one = pl.Buffered(1)
    out = jax.ShapeDtypeStruct((t, n_pairs * LANES), BF16)
    return _call(
        body, name="attn_a_bwd", grid=(batch, n_pairs // ns, nq),
        in_specs=[blk,
                  pl.BlockSpec((s, wide), lambda b, h, i: (b, n_pairs // ns + h), pipeline_mode=one),
                  pl.BlockSpec((s, wide), lambda b, h, i: (b, v_col0 // ns + h), pipeline_mode=one),
                  pl.BlockSpec((nq, DA_T, DA_T), lambda b, h, i: (0, 0, 0), pipeline_mode=one),
                  blk, blk, blk],
        out_specs=[blk, seq, seq], out_shape=[out, out, out],
        scratch=[pltpu.VMEM((s, wide), F32), pltpu.VMEM((s, wide), F32), pltpu.VMEM((ns, DA_T, LANES), F32)],
        sem=("parallel", "parallel", "arbitrary"), args=(qk, qk, proj, bias, o, lse, do), ride=ride)


SB_Q = 256


def _sb_consts(after):
    r = lax.broadcasted_iota(jnp.int32, (2 * BLOCK, 2 * BLOCK), 0) % BLOCK
    c = lax.broadcasted_iota(jnp.int32, (2 * BLOCK, 2 * BLOCK), 1)
    tri = (r > c) if after else (r < c)
    return jnp.logical_or(c >= BLOCK, tri).astype(BF16)


def _split(x):
    hi = x.astype(BF16)
    lo = (x - hi.astype(F32)).astype(BF16)
    return jnp.concatenate([hi, lo], axis=1)


def _sb_fwd(proj, q_col0, k_col0, v_col0, batch, s, ride=None, streams=MIX_STREAMS):
    t = proj.shape[0]
    nq = s // SB_Q
    n_pairs = 4
    ns = streams
    wide = ns * LANES
    scale = HEAD_DIM ** -0.5

    def body(q_ref, k_ref, v_ref, o_ref, tot_ref, acc_ref, run_ref):
        i = pl.program_id(2)
        lo_q = lax.broadcasted_iota(jnp.int32, (SB_Q, LANES), 1) < HEAD_DIM
        lo_k = _lane_lo()
        mat = _sb_consts(True)
        row = lax.broadcasted_iota(jnp.int32, (2 * SB_Q, LANES), 0) % SB_Q
        ahead = row - lax.broadcasted_iota(jnp.int32, (2 * SB_Q, LANES), 1)
        acc_ref[...] = jnp.zeros_like(acc_ref)
        run_ref[...] = jnp.zeros_like(run_ref)
        qqs = [_stack_heads(q_ref[:, st * LANES:(st + 1) * LANES] * scale, lo_q) for st in range(ns)]

        def units(todo):
            def rows(j):
                return pl.ds(pl.multiple_of(j * BLOCK, BLOCK), BLOCK)

            zs = [lax.dot_general(qqs[st], k_ref[rows(j), st * LANES:(st + 1) * LANES], NT, preferred_element_type=F32)
                  for st, j, _ in todo]
            logs = []
            for z, (_, _, off) in zip(zs, todo):
                lsig = jnp.minimum(z, 0.0) - jnp.log(1.0 + jnp.exp(-jnp.abs(z)))
                lneg = lsig - z
                if off is not None:
                    lneg = jnp.where(ahead > off, lneg, 0.0)
                logs.append((lsig, _split(lneg)))
            sums = [lax.dot_general(cat, mat, NN, preferred_element_type=F32) for _, cat in logs]
            probs = []
            for (lsig, _), sm, (st, _, off) in zip(logs, sums, todo):
                run = run_ref[st]
                a = jnp.exp(lsig + run + sm[:, :BLOCK])
                if off is not None:
                    a = jnp.where(ahead > off, a, 0.0)
                run_ref[st] = run + sm[:, BLOCK:]
                probs.append(a.astype(BF16))
            for ab, (st, j, _) in zip(probs, todo):
                v = v_ref[rows(j), st * LANES:(st + 1) * LANES]
                vz = jnp.zeros_like(v)
                acc_ref[st] += (lax.dot_general(ab[:SB_Q], jnp.where(lo_k, v, vz), NN, preferred_element_type=F32)
                                + lax.dot_general(ab[SB_Q:], jnp.where(lo_k, vz, v), NN, preferred_element_type=F32))

        units([(st, 2 * i + 1, BLOCK) for st in range(ns)] + [(st, 2 * i, 0) for st in range(ns)])

        def pair(p, carry):
            jp = i - 1 - p
            units([(st, 2 * jp + 1, None) for st in range(ns)] + [(st, 2 * jp, None) for st in range(ns)])
            return carry

        lax.fori_loop(0, i, pair, 0)
        for st in range(ns):
            cols = slice(st * LANES, (st + 1) * LANES)
            o_ref[:, cols] = acc_ref[st].astype(BF16)
            tot_ref[:, cols] = jnp.where(lo_q, run_ref[st, 0:SB_Q, :], run_ref[st, SB_Q:2 * SB_Q, :])

    def seq(col0):
        return pl.BlockSpec((s, wide), lambda b, h, i: (b, col0 // ns + h))

    blk = pl.BlockSpec((SB_Q, wide), lambda b, h, i: (b * nq + i, h))
    return _call(
        body, name="attn_b_fwd", grid=(batch, n_pairs // ns, nq),
        in_specs=[pl.BlockSpec((SB_Q, wide), lambda b, h, i: (b * nq + i, q_col0 // ns + h)), seq(k_col0), seq(v_col0)],
        out_specs=[blk, blk],
        out_shape=[jax.ShapeDtypeStruct((t, n_pairs * LANES), BF16), jax.ShapeDtypeStruct((t, n_pairs * LANES), F32)],
        scratch=[pltpu.VMEM((ns, SB_Q, LANES), F32), pltpu.VMEM((ns, 2 * SB_Q, LANES), F32)],
        sem=("parallel", "parallel", "arbitrary"), args=(proj, proj, proj), ride=ride)


def _sb_bwd(proj, q_col0, k_col0, v_col0, tot, do, batch, s, ride=None, streams=SB_BWD_STREAMS):
    t = proj.shape[0]
    nq = s // SB_Q
    n_pairs = 4
    ns = streams
    wide = ns * LANES
    scale = HEAD_DIM ** -0.5

    def body(q_ref, k_ref, v_ref, tot_ref, do_ref, dq_ref, dk_ref, dv_ref, dk_acc, dv_acc, dq_acc, seen_ref, gsum_ref):
        i = pl.program_id(2)
        lo_q = lax.broadcasted_iota(jnp.int32, (SB_Q, LANES), 1) < HEAD_DIM
        lo_k = _lane_lo()

        @pl.when(i == 0)
        def _():
            dk_acc[...] = jnp.zeros_like(dk_acc)
            dv_acc[...] = jnp.zeros_like(dv_acc)

        mat_after = _sb_consts(True)
        mat_before = _sb_consts(False)[:BLOCK]
        row = lax.broadcasted_iota(jnp.int32, (2 * SB_Q, LANES), 0) % SB_Q
        ahead = row - lax.broadcasted_iota(jnp.int32, (2 * SB_Q, LANES), 1)
        dq_acc[...] = jnp.zeros_like(dq_acc)
        seen_ref[...] = jnp.zeros_like(seen_ref)
        gsum_ref[...] = jnp.zeros_like(gsum_ref)
        qqs, dds, totals = [], [], []
        for st in range(ns):
            cols = slice(st * LANES, (st + 1) * LANES)
            qqs.append(_stack_heads(q_ref[:, cols] * scale, lo_q))
            dds.append(_stack_heads(do_ref[:, cols], lo_q))
            tot_t = tot_ref[:, cols]
            totals.append(jnp.concatenate([jnp.broadcast_to(tot_t[:, 0:1], (SB_Q, LANES)),
                                           jnp.broadcast_to(tot_t[:, HEAD_DIM:HEAD_DIM + 1], (SB_Q, LANES))], axis=0))

        def units(todo):
            def rows(j):
                return pl.ds(pl.multiple_of(j * BLOCK, BLOCK), BLOCK)

            def cols(st):
                return slice(st * LANES, (st + 1) * LANES)

            prods = [(lax.dot_general(qqs[st], k_ref[rows(j), cols(st)], NT, preferred_element_type=F32),
                      lax.dot_general(dds[st], v_ref[rows(j), cols(st)], NT, preferred_element_type=F32))
                     for st, j, _ in todo]
            logs = []
            for (z, _), (_, _, off) in zip(prods, todo):
                lsig = jnp.minimum(z, 0.0) - jnp.log(1.0 + jnp.exp(-jnp.abs(z)))
                lneg = lsig - z
                if off is not None:
                    lneg = jnp.where(ahead > off, lneg, 0.0)
                logs.append((lsig, _split(lneg)))
            sums = [lax.dot_general(cat, mat_after, NN, preferred_element_type=F32) for _, cat in logs]
            gates = []
            for (lsig, _), sm, (_, da), (st, _, off) in zip(logs, sums, prods, todo):
                seen = seen_ref[st]
                a = jnp.exp(lsig + (totals[st] - seen - sm[:, BLOCK:]) + sm[:, :BLOCK])
                if off is not None:
                    a = jnp.where(ahead > off, a, 0.0)
                seen_ref[st] = seen + sm[:, BLOCK:]
                g = a * da
                gates.append((a.astype(BF16), g, g.astype(BF16)))
            gsums = [lax.dot_general(cat, mat_before, NN, preferred_element_type=F32) for _, _, cat in gates]
            outs = []
            for (lsig, _), (ab, g, _), gs, (st, _, off) in zip(logs, gates, gsums, todo):
                gsum = gsum_ref[st]
                dz = g - jnp.exp(lsig) * (g + gsum + gs[:, :BLOCK])
                if off is not None:
                    dz = jnp.where(ahead > off, dz, 0.0)
                gsum_ref[st] = gsum + gs[:, BLOCK:]
                outs.append((dz.astype(BF16), ab))
            for (dzb, ab), (st, j, _) in zip(outs, todo):
                k = k_ref[rows(j), cols(st)]
                kz = jnp.zeros_like(k)
                dq_acc[st] += (lax.dot_general(dzb[:SB_Q], jnp.where(lo_k, k, kz), NN, preferred_element_type=F32)
                               + lax.dot_general(dzb[SB_Q:], jnp.where(lo_k, kz, k), NN, preferred_element_type=F32))
                dk_acc[rows(j), cols(st)] += lax.dot_general(dzb, qqs[st], TN, preferred_element_type=F32)
                dv_acc[rows(j), cols(st)] += lax.dot_general(ab, dds[st], TN, preferred_element_type=F32)

        def pair(p, carry):
            units([(st, 2 * p, None) for st in range(ns)] + [(st, 2 * p + 1, None) for st in range(ns)])
            return carry

        lax.fori_loop(0, i, pair, 0)
        units([(st, 2 * i, 0) for st in range(ns)] + [(st, 2 * i + 1, BLOCK) for st in range(ns)])
        for st in range(ns):
            dq_ref[:, st * LANES:(st + 1) * LANES] = (dq_acc[st] * scale).astype(BF16)

        @pl.when(i == nq - 1)
        def _():
            dk_ref[...] = dk_acc[...].astype(BF16)
            dv_ref[...] = dv_acc[...].astype(BF16)

    def seq_in(col0):
        return pl.BlockSpec((s, wide), lambda b, h, i: (b, col0 // ns + h))

    blk = pl.BlockSpec((SB_Q, wide), lambda b, h, i: (b * nq + i, h))
    seq = pl.BlockSpec((s, wide), lambda b, h, i: (b, h))
    out = jax.ShapeDtypeStruct((t, n_pairs * LANES), BF16)
    return _call(
        body, name="attn_b_bwd", grid=(batch, n_pairs // ns, nq),
        in_specs=[pl.BlockSpec((SB_Q, wide), lambda b, h, i: (b * nq + i, q_col0 // ns + h)), seq_in(k_col0),
                  seq_in(v_col0), blk, blk],
        out_specs=[blk, seq, seq], out_shape=[out, out, out],
        scratch=[pltpu.VMEM((s, wide), F32), pltpu.VMEM((s, wide), F32), pltpu.VMEM((ns, SB_Q, LANES), F32),
                 pltpu.VMEM((ns, 2 * SB_Q, LANES), F32), pltpu.VMEM((ns, 2 * SB_Q, LANES), F32)],
        sem=("parallel", "parallel", "arbitrary"), args=(proj, proj, proj, tot, do), ride=ride)


MEM_Q_TILE = 512


def _mem_fwd(q, kv, batch, s, n_mem):
    t, width = q.shape
    tq = min(MEM_Q_TILE, s)
    nq = s // tq
    scale = MEM_HEAD_DIM ** -0.5

    def body(q_ref, kv_ref, o_ref):
        for h in range(N_HEADS_MEM):
            cols = slice(h * MEM_HEAD_DIM, (h + 1) * MEM_HEAD_DIM)
            k = kv_ref[:, cols]
            v = kv_ref[:, width + h * MEM_HEAD_DIM: width + (h + 1) * MEM_HEAD_DIM]
            sc = lax.dot_general(q_ref[:, cols], k, NT, preferred_element_type=F32) * scale
            p = jnp.exp(sc - jnp.max(sc, axis=1, keepdims=True))
            p = p / jnp.sum(p, axis=1, keepdims=True)
            o_ref[:, cols] = lax.dot_general(p.astype(BF16), v, NN, preferred_element_type=F32).astype(BF16)

    return pl.pallas_call(
        body, name="mem_attn_fwd", grid=(batch, nq),
        in_specs=[pl.BlockSpec((tq, width), lambda b, i: (b * nq + i, 0)),
                  pl.BlockSpec((n_mem, 2 * width), lambda b, i: (b, 0))],
        out_specs=pl.BlockSpec((tq, width), lambda b, i: (b * nq + i, 0)),
        out_shape=jax.ShapeDtypeStruct((t, width), BF16),
        compiler_params=_params(("parallel", "parallel")),
    )(q, kv)


def _mem_bwd(q, kv, do, batch, s, n_mem):
    t, width = q.shape
    tq = min(MEM_Q_TILE, s)
    nq = s // tq
    scale = MEM_HEAD_DIM ** -0.5

    def body(q_ref, kv_ref, do_ref, dq_ref, dkv_ref, acc):
        i = pl.program_id(1)

        @pl.when(i == 0)
        def _():
            acc[...] = jnp.zeros_like(acc)

        for h in range(N_HEADS_MEM):
            cols = slice(h * MEM_HEAD_DIM, (h + 1) * MEM_HEAD_DIM)
            vcols = slice(width + h * MEM_HEAD_DIM, width + (h + 1) * MEM_HEAD_DIM)
            qh, k, v, doh = q_ref[:, cols], kv_ref[:, cols], kv_ref[:, vcols], do_ref[:, cols]
            sc = lax.dot_general(qh, k, NT, preferred_element_type=F32) * scale
            p = jnp.exp(sc - jnp.max(sc, axis=1, keepdims=True))
            p = p / jnp.sum(p, axis=1, keepdims=True)
            dp = lax.dot_general(doh, v, NT, preferred_element_type=F32)
            ds = (p * (dp - jnp.sum(p * dp, axis=1, keepdims=True)) * scale).astype(BF16)
            dq_ref[:, cols] = lax.dot_general(ds, k, NN, preferred_element_type=F32).astype(BF16)
            acc[:, cols] += lax.dot_general(ds, qh, TN, preferred_element_type=F32)
            acc[:, vcols] += lax.dot_general(p.astype(BF16), doh, TN, preferred_element_type=F32)

        @pl.when(i == nq - 1)
        def _():
            dkv_ref[...] = acc[...].astype(BF16)

    row = pl.BlockSpec((tq, width), lambda b, i: (b * nq + i, 0))
    kvs = pl.BlockSpec((n_mem, 2 * width), lambda b, i: (b, 0))
    return pl.pallas_call(
        body, name="mem_attn_bwd", grid=(batch, nq),
        in_specs=[row, kvs, row], out_specs=[row, kvs],
        out_shape=[jax.ShapeDtypeStruct((t, width), BF16), jax.ShapeDtypeStruct((batch * n_mem, 2 * width), BF16)],
        scratch_shapes=[pltpu.VMEM((n_mem, 2 * width), F32)],
        compiler_params=_params(("parallel", "arbitrary")),
    )(q, kv, do)


def _mixer_fwd(o_a, o_b, w_a, w_b, proj, gate_col0, w_out, x, g, w_q):
    t, width = o_a.shape
    d = w_a.shape[1]
    nq_cols = w_q.shape[1]
    tm = min(ROW_TILE, t)
    gb0 = gate_col0 * LANES // d

    def body(oa_ref, ob_ref, wa_ref, wb_ref, ga_ref, gb_ref, wo_ref, x_ref, g_ref, wq_ref, ua_ref, ub_ref, mix_ref,
             n_ref, h_ref, q_ref):
        ua = lax.dot_general(oa_ref[...], wa_ref[...], NN, preferred_element_type=F32)
        ub = lax.dot_general(ob_ref[...], wb_ref[...], NN, preferred_element_type=F32)
        ua_ref[...] = ua.astype(BF16)
        ub_ref[...] = ub.astype(BF16)
        mixed = (jax.nn.sigmoid(ga_ref[...].astype(F32)) * ua + jax.nn.sigmoid(gb_ref[...].astype(F32)) * ub).astype(BF16)
        mix_ref[...] = mixed
        h = lax.dot_general(mixed, wo_ref[...], NN, preferred_element_type=F32) + x_ref[...]
        h_ref[...] = h
        r = lax.rsqrt(jnp.mean(h * h, axis=-1, keepdims=True) + RMS_EPS)
        n = (h * r * g_ref[...]).astype(BF16)
        n_ref[...] = n
        q_ref[...] = lax.dot_general(n, wq_ref[...], NN, preferred_element_type=F32).astype(BF16)

    row = pl.BlockSpec((tm, width), lambda i: (i, 0))
    wsp = pl.BlockSpec((width, d), lambda i: (0, 0))
    out = pl.BlockSpec((tm, d), lambda i: (i, 0))
    osh = jax.ShapeDtypeStruct((t, d), BF16)
    return pl.pallas_call(
        body, name="mixer_fwd", grid=(t // tm,),
        in_specs=[row, row, wsp, wsp,
                  pl.BlockSpec((tm, d), lambda i: (i, gb0)), pl.BlockSpec((tm, d), lambda i: (i, gb0 + 1)),
                  pl.BlockSpec((d, d), lambda i: (0, 0)), out, pl.BlockSpec((1, d), lambda i: (0, 0)),
                  pl.BlockSpec((d, nq_cols), lambda i: (0, 0))],
        out_specs=[out, out, out, out, out, pl.BlockSpec((tm, nq_cols), lambda i: (i, 0))],
        out_shape=[osh, osh, osh, osh, jax.ShapeDtypeStruct((t, d), F32), jax.ShapeDtypeStruct((t, nq_cols), BF16)],
        compiler_params=_params(("parallel",)),
    )(o_a, o_b, w_a, w_b, proj, proj, w_out, x, g, w_q)


def _mixer_bwd(dh, w_out, ua, ub, proj, gate_col0, w_a, w_b):
    t, d = dh.shape
    width = w_a.shape[0]
    tm = min(ROW_TILE, t)
    nc = d // LANES

    def body(dh_ref, w_ref, ua_ref, ub_ref, ga_ref, gb_ref, wa_ref, wb_ref, dua_ref, dub_ref, dg_ref, doa_ref, dob_ref):
        dm = lax.dot_general(dh_ref[...], w_ref[...], NT, preferred_element_type=F32)
        sa = jax.nn.sigmoid(ga_ref[...].astype(F32))
        sb = jax.nn.sigmoid(gb_ref[...].astype(F32))
        dua = (dm * sa).astype(BF16)
        dub = (dm * sb).astype(BF16)
        dua_ref[...] = dua
        dub_ref[...] = dub
        dg_ref[:, 0:d] = (dm * ua_ref[...].astype(F32) * sa * (1.0 - sa)).astype(BF16)
        dg_ref[:, d:2 * d] = (dm * ub_ref[...].astype(F32) * sb * (1.0 - sb)).astype(BF16)
        doa_ref[...] = lax.dot_general(dua, wa_ref[...], NT, preferred_element_type=F32).astype(BF16)
        dob_ref[...] = lax.dot_general(dub, wb_ref[...], NT, preferred_element_type=F32).astype(BF16)

    row = pl.BlockSpec((tm, d), lambda i: (i, 0))
    wsp = pl.BlockSpec((width, d), lambda i: (0, 0))
    osp = pl.BlockSpec((tm, width), lambda i: (i, 0))
    return pl.pallas_call(
        body, name="mixer_bwd", grid=(t // tm,),
        in_specs=[row, pl.BlockSpec((d, d), lambda i: (0, 0)), row, row,
                  pl.BlockSpec((tm, d), lambda i: (i, gate_col0 // nc)),
                  pl.BlockSpec((tm, d), lambda i: (i, gate_col0 // nc + 1)), wsp, wsp],
        out_specs=[row, row, pl.BlockSpec((tm, 2 * d), lambda i: (i, 0)), osp, osp],
        out_shape=[jax.ShapeDtypeStruct((t, d), BF16), jax.ShapeDtypeStruct((t, d), BF16),
                   jax.ShapeDtypeStruct((t, 2 * d), BF16), jax.ShapeDtypeStruct((t, width), BF16),
                   jax.ShapeDtypeStruct((t, width), BF16)],
        compiler_params=_params(("parallel",)),
    )(dh, w_out, ua, ub, proj, proj, w_a, w_b)


FFN_COLS = 1024
FFN_CHUNK = 256


def _ffn_up(n, w_gate, w_up):
    t, d = n.shape
    hidden = w_gate.shape[0]
    tm = min(2 * ROW_TILE, t)
    tn = min(FFN_COLS, hidden)
    tc = min(FFN_CHUNK, tn)

    def body(n_ref, wg_ref, wu_ref, hg_ref, hu_ref, act_ref):
        for c in range(0, tn, tc):
            hg = lax.dot_general(n_ref[...], wg_ref[c:c + tc, :], NT, preferred_element_type=F32)
            hu = lax.dot_general(n_ref[...], wu_ref[c:c + tc, :], NT, preferred_element_type=F32)
            hg_ref[:, c:c + tc] = hg.astype(BF16)
            hu_ref[:, c:c + tc] = hu.astype(BF16)
            act_ref[:, c:c + tc] = (hg * jax.nn.sigmoid(hg) * hu).astype(BF16)

    wsp = pl.BlockSpec((tn, d), lambda j, i: (j, 0))
    out = pl.BlockSpec((tm, tn), lambda j, i: (i, j))
    osh = jax.ShapeDtypeStruct((t, hidden), BF16)
    return pl.pallas_call(
        body, name="ffn_up", grid=(hidden // tn, t // tm),
        in_specs=[pl.BlockSpec((tm, d), lambda j, i: (i, 0)), wsp, wsp],
        out_specs=[out, out, out], out_shape=[osh, osh, osh],
        compiler_params=_params(("parallel", "parallel")),
    )(n, w_gate, w_up)


def _ffn_bwd(dh, w_down, w_gate, w_up, hg, hu, x, g, dres, w_prev):
    t, d = dh.shape
    hidden = w_down.shape[0]
    q = w_prev.shape[0]
    tm = min(ROW_TILE, t)
    tn = min(FFN_COLS, hidden)
    nj = hidden // tn

    def body(dh_ref, wd_ref, wg_ref, wu_ref, hg_ref, hu_ref, x_ref, g_ref, r_ref, wp_ref, dhg_ref, dhu_ref, dx_ref,
             dxb_ref, dg_ref, do_ref, acc):
        j, i = pl.program_id(0), pl.program_id(1)
        dact = lax.dot_general(dh_ref[...], wd_ref[...], NT, preferred_element_type=F32)
        hg = hg_ref[...].astype(F32)
        sg = jax.nn.sigmoid(hg)
        dhu = (dact * hg * sg).astype(BF16)
        dhg = (dact * hu_ref[...].astype(F32) * sg * (1.0 + hg * (1.0 - sg))).astype(BF16)
        dhu_ref[...] = dhu
        dhg_ref[...] = dhg
        part = (lax.dot_general(dhg, wg_ref[...], NN, preferred_element_type=F32)
                + lax.dot_general(dhu, wu_ref[...], NN, preferred_element_type=F32))

        @pl.when(j == 0)
        def _():
            acc[i] = part

        @pl.when(j > 0)
        def _():
            acc[i] += part

        @pl.when(jnp.logical_and(j == 0, i == 0))
        def _():
            dg_ref[...] = jnp.zeros_like(dg_ref)

        @pl.when(j == nj - 1)
        def _():
            dx, dg = _rms_bwd_rows(acc[i], x_ref[...], g_ref[...], r_ref[...])
            dx_ref[...] = dx
            dxb = dx.astype(BF16)
            dxb_ref[...] = dxb
            dg_ref[...] += dg
            do_ref[...] = lax.dot_general(dxb, wp_ref[...], NT, preferred_element_type=F32).astype(BF16)

    hid = pl.BlockSpec((tm, tn), lambda j, i: (i, j))
    wsp = pl.BlockSpec((tn, d), lambda j, i: (j, 0), pipeline_mode=pl.Buffered(1))
    late = pl.BlockSpec((tm, d), lambda j, i: (jnp.where(j == nj - 1, i, 0), 0))
    late_q = pl.BlockSpec((tm, q), lambda j, i: (jnp.where(j == nj - 1, i, 0), 0))
    vec = pl.BlockSpec((1, d), lambda j, i: (0, 0))
    osh = jax.ShapeDtypeStruct((t, hidden), BF16)
    return pl.pallas_call(
        body, name="ffn_bwd", grid=(nj, t // tm),
        in_specs=[pl.BlockSpec((tm, d), lambda j, i: (i, 0)), wsp, wsp, wsp, hid, hid, late, vec, late,
                  pl.BlockSpec((q, d), lambda j, i: (0, 0), pipeline_mode=pl.Buffered(1))],
        out_specs=[hid, hid, late, late, vec, late_q],
        out_shape=[osh, osh, jax.ShapeDtypeStruct((t, d), F32), jax.ShapeDtypeStruct((t, d), BF16),
                   jax.ShapeDtypeStruct((1, d), F32), jax.ShapeDtypeStruct((t, q), BF16)],
        scratch_shapes=[pltpu.VMEM((t // tm, tm, d), F32)],
        compiler_params=_params(("arbitrary", "arbitrary")),
    )(dh, w_down, w_gate, w_up, hg, hu, x, g, dres, w_prev)


MM_ROWS = 1024


def _mm_w(name, a, w, out_dtype, dims=NN):
    t, k = a.shape
    n = w.shape[1] if dims == NN else w.shape[0]
    tm, tn = min(MM_ROWS, t), min(1024, n)
    o_spec = pl.BlockSpec((tm, tn), lambda j, i: (i, j))
    b_spec = pl.BlockSpec((k, tn), lambda j, i: (0, j)) if dims == NN else pl.BlockSpec((tn, k), lambda j, i: (j, 0))
    return _mm(name, a, w, grid=(n // tn, t // tm), a_spec=pl.BlockSpec((tm, k), lambda j, i: (i, 0)), b_spec=b_spec,
               o_shape=(t, n), o_spec=o_spec, dims=dims, out_dtype=out_dtype)


def _mm_res_norm(name, a, w, res, g):
    t, k = a.shape
    d = w.shape[1]
    tm = min(ROW_TILE, t)

    def body(a_ref, w_ref, r_ref, g_ref, h_ref, n_ref):
        h = lax.dot_general(a_ref[...], w_ref[...], NN, preferred_element_type=F32) + r_ref[...]
        h_ref[...] = h
        r = lax.rsqrt(jnp.mean(h * h, axis=-1, keepdims=True) + RMS_EPS)
        n_ref[...] = (h * r * g_ref[...]).astype(BF16)

    row = pl.BlockSpec((tm, d), lambda i: (i, 0))
    return pl.pallas_call(
        body, name=name, grid=(t // tm,),
        in_specs=[pl.BlockSpec((tm, k), lambda i: (i, 0)), pl.BlockSpec((k, d), lambda i: (0, 0)), row,
                  pl.BlockSpec((1, d), lambda i: (0, 0))],
        out_specs=[row, row], out_shape=[jax.ShapeDtypeStruct((t, d), F32), jax.ShapeDtypeStruct((t, d), BF16)],
        compiler_params=_params(("parallel",)),
    )(a, w, res, g)


WGRAD_COLS = 256


def _wgrad(name, a, g, tk=1024, tn=1024):
    t, k = a.shape
    n = g.shape[1]
    tm, tk, tn = min(2 * MM_ROWS, t), min(tk, k), min(tn, n)
    nr = t // tm
    tc = min(WGRAD_COLS, tn)

    def body(a_ref, g_ref, o_ref, *acc):
        def run(first, last):
            for c in range(0, tn, tc):
                p = lax.dot_general(a_ref[...], g_ref[:, c:c + tc], TN, preferred_element_type=F32)
                if not first:
                    p += acc[0][:, c:c + tc]
                if last:
                    o_ref[:, c:c + tc] = p.astype(BF16)
                else:
                    acc[0][:, c:c + tc] = p

        if nr == 1:
            run(True, True)
            return
        r = pl.program_id(2)
        pl.when(r == 0)(functools.partial(run, True, False))
        if nr > 2:
            pl.when(jnp.logical_and(r > 0, r < nr - 1))(functools.partial(run, False, False))
        pl.when(r == nr - 1)(functools.partial(run, False, True))

    return pl.pallas_call(
        body, name=name, grid=(k // tk, n // tn, nr),
        in_specs=[pl.BlockSpec((tm, tk), lambda p, q, r: (r, p)), pl.BlockSpec((tm, tn), lambda p, q, r: (r, q))],
        out_specs=pl.BlockSpec((tk, tn), lambda p, q, r: (p, q)), out_shape=jax.ShapeDtypeStruct((k, n), BF16),
        scratch_shapes=[pltpu.VMEM((tk, tn), F32)] if nr > 1 else [],
        compiler_params=_params(("parallel", "parallel", "arbitrary")),
    )(a, g)


def _peers():
    x, y, c = lax.axis_index("x"), lax.axis_index("y"), lax.axis_index("c")
    me = 4 * x + 2 * y + c
    out = []
    for k in range(1, N_DEV):
        kx, ky, kc = (k >> 2) & 1, (k >> 1) & 1, k & 1
        px = 1 - x if kx else x
        py = 1 - y if ky else y
        pc = 1 - c if kc else c
        out.append(((px, py, pc), 4 * px + 2 * py + pc))
    return me, out


def _cast_weights(ws, pad_rows):
    def body(*refs):
        n = len(refs) // 2
        for i_ref, o_ref, pr in zip(refs[:n], refs[n:], pad_rows):
            r, c = i_ref.shape
            o_ref[0:r, :] = i_ref[...].astype(BF16)
            if pr:
                o_ref[r:r + pr, :] = jnp.zeros((pr, c), BF16)

    return pl.pallas_call(
        body, name="cast_weights", in_specs=[VMEM] * len(ws), out_specs=[VMEM] * len(ws),
        out_shape=[jax.ShapeDtypeStruct((w.shape[0] + pr, w.shape[1]), BF16) for w, pr in zip(ws, pad_rows)],
    )(*ws)


def _window(ref, j, c):
    return ref.at[:, pl.ds(pl.multiple_of(j * c, LANES), c)]


def _direct_copies(ins, outs, sems, gather, cols, landed):
    send_sems, recv_sems, loc_sems = sems
    n_peer = N_DEV - 1
    me, peers = _peers()

    def src(w, j):
        if gather:
            return ins[w]
        return _window(ins[w], j, cols[w]) if cols[w] else ins[w].at[j]

    def dst(w, j):
        return _window(outs[w], j, cols[w]) if gather and cols[w] else outs[w].at[j]

    local = [pltpu.make_async_copy(src(w, me), dst(w, me), loc_sems.at[w]) for w in range(len(ins))]
    remote = [pltpu.make_async_remote_copy(
        src_ref=src(w, idx), dst_ref=dst(w, idx if landed else me),
        send_sem=send_sems.at[w * n_peer + k], recv_sem=recv_sems.at[w * n_peer + k],
        device_id=dev, device_id_type=pl.DeviceIdType.MESH)
        for k, (dev, idx) in reversed(list(enumerate(peers))) for w in range(len(ins))]
    return local, remote


OTHER_CHIPS = (2, 4, 6)


def _gather_copies(ins, outs, sems, cols):
    send_sems, recv_sems, loc_sems = sems
    x, y, c = lax.axis_index("x"), lax.axis_index("y"), lax.axis_index("c")
    me = 4 * x + 2 * y + c
    n_pair = N_DEV - 1

    def dev(mask):
        return (1 - x if mask & 4 else x, 1 - y if mask & 2 else y, 1 - c if mask & 1 else c)

    def slot(w, mask):
        j = jnp.bitwise_xor(me, mask)
        return _window(outs[w], j, cols[w]) if cols[w] else outs[w].at[j]

    def remote(w, pair, src, to_slot, target):
        return pltpu.make_async_remote_copy(src_ref=src, dst_ref=slot(w, to_slot), send_sem=send_sems.at[w * n_pair + pair],
                                            recv_sem=recv_sems.at[w * n_pair + pair], device_id=dev(target),
                                            device_id_type=pl.DeviceIdType.MESH)

    ws = range(len(ins))
    return dict(
        local=[pltpu.make_async_copy(ins[w], slot(w, 0), loc_sems.at[w]) for w in ws],
        to_chips=[remote(w, 1 + t, ins[w], 0, m) for t, m in enumerate(OTHER_CHIPS) for w in ws],
        to_core=[remote(w, 0, ins[w], 0, 1) for w in ws],
        from_chips=[remote(w, 1 + t, ins[w], m, 0) for t, m in enumerate(OTHER_CHIPS) for w in ws],
        pass_on=[remote(w, 4 + t, slot(w, m), m, 1) for t, m in enumerate(OTHER_CHIPS) for w in ws],
        from_core=[remote(w, 0, ins[w], 1, 0) for w in ws]
        + [remote(w, 4 + t, ins[w], m + 1, 0) for t, m in enumerate(OTHER_CHIPS) for w in ws])


TWO_LEVEL = "gather in two levels"


def _exchange_start(ins, outs, sems, gather, cols):
    if gather == TWO_LEVEL:
        cps = _gather_copies(ins, outs, sems, cols)
        for cp in cps["local"] + cps["to_chips"] + cps["to_core"]:
            cp.start()
    else:
        local, remote = _direct_copies(ins, outs, sems, gather, cols, False)
        for cp in local + remote:
            cp.start()


def _exchange_pass_on(ins, outs, sems, gather, cols, chips):
    if gather == TWO_LEVEL:
        cps = _gather_copies(ins, outs, sems, cols)
        n = len(ins)
        for t in chips:
            for arrived, onward in zip(cps["from_chips"][t * n:(t + 1) * n], cps["pass_on"][t * n:(t + 1) * n]):
                arrived.wait_recv()
                onward.start()


def _exchange_wait(ins, outs, sems, gather, cols):
    if gather == TWO_LEVEL:
        cps = _gather_copies(ins, outs, sems, cols)
        for cp in cps["local"]:
            cp.wait()
        for cp in cps["to_chips"] + cps["to_core"] + cps["pass_on"]:
            cp.wait_send()
        for cp in cps["from_core"]:
            cp.wait_recv()
    else:
        local, remote = _direct_copies(ins, outs, sems, gather, cols, True)
        for cp in local:
            cp.wait()
        for cp in remote:
            cp.wait_send()
            cp.wait_recv()


def _exchange_shapes(arrs, gather, cols):
    n = len(arrs)
    out_shape = []
    for a, c in zip(arrs, cols):
        if gather:
            shape = (a.shape[0], N_DEV * c) if c else (N_DEV,) + a.shape
        else:
            shape = (N_DEV, a.shape[0], c) if c else a.shape
        out_shape.append(jax.ShapeDtypeStruct(shape, a.dtype))
    sems = [pltpu.SemaphoreType.DMA((n * (N_DEV - 1),)), pltpu.SemaphoreType.DMA((n * (N_DEV - 1),)),
            pltpu.SemaphoreType.DMA((n,))]
    return out_shape, sems


def _call(body, *, name, grid, in_specs, out_specs, out_shape, scratch, sem, args, ride=None):
    if ride is None:
        outs = pl.pallas_call(body, name=name, grid=grid, in_specs=in_specs, out_specs=out_specs, out_shape=out_shape,
                              scratch_shapes=scratch, compiler_params=_params(sem))(*args)
        return outs, None
    arrs, gather, cols = ride
    n, n_in, n_out, n_scr = len(arrs), len(in_specs), len(out_specs), len(scratch)
    x_shape, x_sems = _exchange_shapes(arrs, gather, cols)

    def riding(*refs):
        ins, x_ins = refs[:n_in], refs[n_in:n_in + n]
        outs = refs[n_in + n:n_in + n + n_out]
        x_outs = refs[n_in + n + n_out:n_in + 2 * n + n_out]
        scr = refs[n_in + 2 * n + n_out:n_in + 2 * n + n_out + n_scr]
        sems = refs[n_in + 2 * n + n_out + n_scr:]
        def at(step):
            return functools.reduce(jnp.logical_and, [pl.program_id(a) == v for a, v in enumerate(step)])

        @pl.when(at((0,) * len(grid)))
        def _():
            _exchange_start(x_ins, x_outs, sems, gather, cols)

        @pl.when(at((grid[0] // 2,) + (0,) * (len(grid) - 2) + (grid[-1] // 2,)))
        def _():
            _exchange_pass_on(x_ins, x_outs, sems, gather, cols, (0, 1))

        @pl.when(at((grid[0] // 2,) + (0,) * (len(grid) - 2) + (3 * grid[-1] // 4,)))
        def _():
            _exchange_pass_on(x_ins, x_outs, sems, gather, cols, (2,))

        body(*ins, *outs, *scr)

        @pl.when(at(tuple(g - 1 for g in grid)))
        def _():
            _exchange_wait(x_ins, x_outs, sems, gather, cols)

    res = pl.pallas_call(
        riding, name=name, grid=grid, in_specs=list(in_specs) + [ANY] * n, out_specs=list(out_specs) + [ANY] * n,
        out_shape=list(out_shape) + x_shape, scratch_shapes=list(scratch) + x_sems,
        compiler_params=_params(("arbitrary",) * len(grid)))(*args, *arrs)
    return res[:n_out], res[n_out:]


def _my_block():
    return (4 * lax.axis_index("x") + 2 * lax.axis_index("y") + lax.axis_index("c")).astype(jnp.int32).reshape(1)


def _proj_in_gather(x, g, w_shard):
    t, k = x.shape
    cs = w_shard.shape[1]
    tm = min(MM_ROWS, t)
    ni = t // tm
    arrival = (0, 1, 2, 4, 3, 5, 6, 7)

    def mask_at(s):
        return jnp.where(s == 3, 4, jnp.where(s == 4, 3, s))

    def body(me_ref, x_ref, g_ref, w_hbm, o_ref, all_hbm, n_hbm, w_vmem, n_vmem, send_sems, recv_sems, loc_sems,
             load_sems, n_sem):
        s, i = pl.program_id(0), pl.program_id(1)
        cps = _gather_copies([w_hbm], [all_hbm], (send_sems, recv_sems, loc_sems), (cs,))
        by_mask = {0: cps["local"][0], 1: cps["from_core"][0]}
        for t_chip, m in enumerate(OTHER_CHIPS):
            by_mask[m] = cps["from_chips"][t_chip]
            by_mask[m + 1] = cps["from_core"][1 + t_chip]
        arrived = [by_mask[m] for m in arrival]

        def load(step):
            src = w_hbm if step == 0 else _window(all_hbm, jnp.bitwise_xor(me_ref[0], arrival[step]), cs)
            return pltpu.make_async_copy(src, w_vmem.at[step % 2], load_sems.at[step % 2])

        @pl.when(jnp.logical_and(s == 0, i == 0))
        def _():
            for cp in cps["local"] + cps["to_chips"] + cps["to_core"]:
                cp.start()
            load(0).start()

        for step, mask in enumerate(arrival):
            @pl.when(jnp.logical_and(s == step, i == 0))
            def _(step=step):
                load(step).wait()

            if step + 1 < N_DEV:
                @pl.when(jnp.logical_and(s == step, i == min(1, ni - 1)))
                def _(step=step):
                    arrived[step + 1].wait_recv()
                    if arrival[step + 1] in OTHER_CHIPS:
                        cps["pass_on"][OTHER_CHIPS.index(arrival[step + 1])].start()
                    load(step + 1).start()

        @pl.when(s == 0)
        def _():
            xf = x_ref[...]
            r = lax.rsqrt(jnp.mean(xf * xf, axis=-1, keepdims=True) + RMS_EPS)
            n_vmem[i] = (xf * r * g_ref[...]).astype(BF16)
            keep = pltpu.make_async_copy(n_vmem.at[i], n_hbm.at[pl.ds(pl.multiple_of(i * tm, tm), tm), :], n_sem)
            keep.start()
            keep.wait()

        o_ref[...] = lax.dot_general(n_vmem[i], w_vmem[s % 2], NN, preferred_element_type=F32).astype(BF16)

        @pl.when(jnp.logical_and(s == N_DEV - 1, i == ni - 1))
        def _():
            cps["local"][0].wait()
            for cp in cps["to_chips"] + cps["to_core"] + cps["pass_on"]:
                cp.wait_send()

    return pl.pallas_call(
        body, name="proj_in",
        grid_spec=pltpu.PrefetchScalarGridSpec(
            num_scalar_prefetch=1, grid=(N_DEV, ni),
            in_specs=[pl.BlockSpec((tm, k), lambda s, i, me: (jnp.where(s == 0, i, 0), 0)),
                      pl.BlockSpec((1, k), lambda s, i, me: (0, 0)), ANY],
            out_specs=[pl.BlockSpec((tm, cs), lambda s, i, me: (i, jnp.bitwise_xor(me[0], mask_at(s)))), ANY, ANY],
            scratch_shapes=[pltpu.VMEM((2, k, cs), BF16), pltpu.VMEM((ni, tm, k), BF16),
                            pltpu.SemaphoreType.DMA((N_DEV - 1,)), pltpu.SemaphoreType.DMA((N_DEV - 1,)),
                            pltpu.SemaphoreType.DMA((1,)), pltpu.SemaphoreType.DMA((2,)), pltpu.SemaphoreType.DMA]),
        out_shape=[jax.ShapeDtypeStruct((t, N_DEV * cs), BF16), jax.ShapeDtypeStruct((k, N_DEV * cs), BF16),
                   jax.ShapeDtypeStruct((t, k), BF16)],
        compiler_params=_params(("arbitrary", "arbitrary")),
    )(_my_block(), x, g, w_shard)


def _gw_in_scatter(a, g):
    t, k = a.shape
    cs = g.shape[1] // N_DEV
    tm = min(MM_ROWS, t)
    nr = t // tm
    n_chip = N_DEV // 2
    chips = (6, 4, 2, 0)

    def body(me_ref, a_ref, g_ref, out_hbm, acc, stage, other, core_send, core_recv, chip_send, chip_recv, loc_sem):
        s, r = pl.program_id(0), pl.program_id(1)
        x, y, c = lax.axis_index("x"), lax.axis_index("y"), lax.axis_index("c")
        my_chip = 2 * x + y
        part = lax.dot_general(a_ref[...], g_ref[...], TN, preferred_element_type=F32)

        def to_core(m):
            return pltpu.make_async_remote_copy(src_ref=stage.at[0], dst_ref=other.at[m], send_sem=core_send.at[m],
                                                recv_sem=core_recv.at[m], device_id=(x, y, 1 - c),
                                                device_id_type=pl.DeviceIdType.MESH)

        def to_chip(m, landed):
            mask = chips[m]
            there = (1 - x if mask & 4 else x, 1 - y if mask & 2 else y, c)
            slot = (2 * there[0] + there[1]) if landed else my_chip
            return pltpu.make_async_remote_copy(src_ref=stage.at[1], dst_ref=out_hbm.at[slot], send_sem=chip_send.at[m],
                                                recv_sem=chip_recv.at[m], device_id=there,
                                                device_id_type=pl.DeviceIdType.MESH)

        local = pltpu.make_async_copy(stage.at[1], out_hbm.at[my_chip], loc_sem)

        @pl.when(r == 0)
        def _():
            acc[...] = part

        @pl.when(r > 0)
        def _():
            acc[...] += part

        for step in range(N_DEV):
            m = step // 2

            @pl.when(jnp.logical_and(s == step, r == nr - 1))
            def _(step=step, m=m):
                if step % 2 == 0:
                    if m > 0:
                        to_core(m - 1).wait_send()
                    stage[0] = acc[...].astype(BF16)
                    to_core(m).start()
                else:
                    if m > 0:
                        to_chip(m - 1, False).wait_send()
                    to_core(m).wait_recv()
                    stage[1] = (acc[...] + other[m].astype(F32)).astype(BF16)
                    if m < n_chip - 1:
                        to_chip(m, False).start()
                    else:
                        local.start()
                        to_core(m).wait_send()
                        local.wait()
                        for mm in range(n_chip - 1):
                            to_chip(mm, True).wait_recv()

    return pl.pallas_call(
        body, name="gw_in",
        grid_spec=pltpu.PrefetchScalarGridSpec(
            num_scalar_prefetch=1, grid=(N_DEV, nr),
            in_specs=[pl.BlockSpec((tm, k), lambda s, r, me: (r, 0)),
                      pl.BlockSpec((tm, cs), lambda s, r, me: (r, jnp.bitwise_xor(me[0], N_DEV - 1 - s)))],
            out_specs=ANY,
            scratch_shapes=[pltpu.VMEM((k, cs), F32), pltpu.VMEM((2, k, cs), BF16), pltpu.VMEM((n_chip, k, cs), BF16),
                            pltpu.SemaphoreType.DMA((n_chip,)), pltpu.SemaphoreType.DMA((n_chip,)),
                            pltpu.SemaphoreType.DMA((n_chip - 1,)), pltpu.SemaphoreType.DMA((n_chip - 1,)),
                            pltpu.SemaphoreType.DMA]),
        out_shape=jax.ShapeDtypeStruct((n_chip, k, cs), BF16),
        compiler_params=_params(("arbitrary", "arbitrary")),
    )(_my_block(), a, g)


SMALL_ROWS = 8


def _allreduce_small(parts, loss_part):
    n, d = len(parts), parts[0].shape[1]

    def body(*refs):
        part_refs, loss_ref, o_ref = refs[:n], refs[n], refs[n + 1]
        mine_ref, all_ref, send_sems, recv_sems = refs[n + 2:]
        me, peers = _peers()
        mine_ref[...] = jnp.zeros_like(mine_ref)
        for i, p_ref in enumerate(part_refs):
            mine_ref[i:i + 1, :] = p_ref[...]
        mine_ref[SMALL_ROWS - 1:SMALL_ROWS, 0:LANES] = loss_ref[0:1, :]
        all_ref[me] = mine_ref[...]
        for k, (dev, idx) in enumerate(peers):
            pltpu.make_async_remote_copy(src_ref=mine_ref, dst_ref=all_ref.at[me], send_sem=send_sems.at[k],
                                         recv_sem=recv_sems.at[k], device_id=dev,
                                         device_id_type=pl.DeviceIdType.MESH).start()
        for k, (dev, idx) in enumerate(peers):
            cp = pltpu.make_async_remote_copy(src_ref=mine_ref, dst_ref=all_ref.at[idx], send_sem=send_sems.at[k],
                                              recv_sem=recv_sems.at[k], device_id=dev,
                                              device_id_type=pl.DeviceIdType.MESH)
            cp.wait_send()
            cp.wait_recv()
        tot = all_ref[0]
        for dvc in range(1, N_DEV):
            tot = tot + all_ref[dvc]
        o_ref[...] = tot

    return pl.pallas_call(
        body, name="allreduce_small", in_specs=[VMEM] * (n + 1), out_specs=VMEM,
        out_shape=jax.ShapeDtypeStruct((SMALL_ROWS, d), F32),
        scratch_shapes=[pltpu.VMEM((SMALL_ROWS, d), F32), pltpu.VMEM((N_DEV, SMALL_ROWS, d), F32),
                        pltpu.SemaphoreType.DMA((N_DEV - 1,)), pltpu.SemaphoreType.DMA((N_DEV - 1,))],
    )(*parts, loss_part)


def _adam_math(g, w, m, v):
    m_new = ADAM_B1 * m + (1.0 - ADAM_B1) * g
    v_new = ADAM_B2 * v + (1.0 - ADAM_B2) * (g * g)
    m_hat = m_new / (1.0 - ADAM_B1 ** ADAM_STEP)
    v_hat = v_new / (1.0 - ADAM_B2 ** ADAM_STEP)
    delta = -ADAM_LR * (m_hat / (jnp.sqrt(v_hat) + ADAM_EPS) + ADAM_WD * w)
    return delta, m_new, v_new


def _adam(name, pieces, w, m, v):
    r, c = w.shape
    n_piece, _, cp = pieces.shape
    tr = r
    for cand in (256, 176, 128, 64):
        if r % cand == 0 and r > cand:
            tr = cand
            break

    def body(p_ref, w_ref, m_ref, v_ref, g_ref, d_ref, mo_ref, vo_ref):
        g = p_ref[0, :, 0:c].astype(F32)
        for j in range(1, n_piece):
            g = g + p_ref[j, :, 0:c].astype(F32)
        delta, m_new, v_new = _adam_math(g, w_ref[...], m_ref[...], v_ref[...])
        g_ref[...] = g
        d_ref[...] = delta
        mo_ref[...] = m_new
        vo_ref[...] = v_new

    blk = pl.BlockSpec((tr, c), lambda i: (i, 0))
    osh = jax.ShapeDtypeStruct((r, c), F32)
    return pl.pallas_call(
        body, name=name, grid=(r // tr,),
        in_specs=[pl.BlockSpec((n_piece, tr, cp), lambda i: (0, i, 0)), blk, blk, blk],
        out_specs=[blk, blk, blk, blk], out_shape=[osh, osh, osh, osh],
        compiler_params=_params(("parallel",)),
    )(pieces, w, m, v)


def _adam_small(g_all, ws, ms, vs):
    n = len(ws)

    def body(*refs):
        g_ref, ins, outs = refs[0], refs[1:1 + 3 * n], refs[1 + 3 * n:]
        for i in range(n):
            g = g_ref[i:i + 1, :]
            delta, m_new, v_new = _adam_math(g, ins[i][...], ins[n + i][...], ins[2 * n + i][...])
            for kind, val in enumerate((g, delta, m_new, v_new)):
                outs[kind * n + i][...] = val

    osh = jax.ShapeDtypeStruct(ws[0].shape, F32)
    res = pl.pallas_call(body, name="adam_small", in_specs=[VMEM] * (1 + 3 * n), out_specs=[VMEM] * (4 * n),
                         out_shape=[osh] * (4 * n))(g_all, *ws, *ms, *vs)
    return res[:n], res[n:2 * n], res[2 * n:3 * n], res[3 * n:]


def _local_step(x, mem, pos, tgt, gains, w_in_shard, shards, batch):
    g_mix, g_mem_q, g_mem_kv, g_ffn, g_final = gains
    t, d = x.shape
    s = t // batch
    n_mem = mem.shape[0] // batch
    n_sh = N_DEV
    width = shards[0].shape[0]
    nb = width // LANES

    lane = np.arange(LANES) % HEAD_DIM
    sel_lo = (lane < ROPE_HALF).astype(np.float32)[None, :]
    sel_hi = ((lane >= ROPE_HALF) & (lane < 2 * ROPE_HALF)).astype(np.float32)[None, :]
    freqs = np.float32(ROPE_THETA) ** (-np.arange(ROPE_HALF, dtype=np.float32) / np.float32(ROPE_HALF))
    inv_freq = np.where(lane < 2 * ROPE_HALF, freqs[lane % ROPE_HALF], 0.0).astype(np.float32)[None, :]
    cos_t, sin_a, sin_b = _rope_tables(pos, jnp.asarray(inv_freq), jnp.asarray(sel_lo), jnp.asarray(sel_hi))
    bias = _dilated_bias_tiles(s)

    proj, w_in, n1 = _proj_in_gather(x, g_mix, w_in_shard)
    qk_a = _rope_apply("rope_fwd", [proj], 2 * width, cos_t, sin_a, sin_b, 1.0)
    cs_up = shards[0].shape[1]
    (o_a, lse_a), (w_up_a, w_up_b, w_out, w_q, w_kv, w_o, w_fd) = _da_fwd(
        qk_a, proj, 2 * nb, bias, batch, s,
        ride=(shards[:6] + shards[8:], TWO_LEVEL, (cs_up, cs_up, 0, 0, 0, cs_up, 0)))
    (o_b, tot_b), (w_fg, w_fu) = _sb_fwd(proj, 3 * nb, 4 * nb, 5 * nb, batch, s, ride=(shards[6:8], True, (0, 0)))
    w_out = w_out.reshape(d, d)
    w_q = w_q.reshape(d, -1)
    w_kv = w_kv.reshape(d, -1)
    w_fd = w_fd.reshape(-1, d)
    w_fg = w_fg.reshape(-1, d)
    w_fu = w_fu.reshape(-1, d)
    ua, ub, mixed, n2, h1, q_m = _mixer_fwd(o_a, o_b, w_up_a, w_up_b, proj, 6 * nb, w_out, x, g_mem_q, w_q)
    mem_n = _rms_fwd("norm_mem_kv", mem, g_mem_kv)
    kv_m = _mm_w("mem_kv", mem_n, w_kv, BF16)
    o_m = _mem_fwd(q_m, kv_m, batch, s, n_mem)
    h2, n3 = _mm_res_norm("mem_out", o_m, w_o, h1, g_ffn)
    hg, hu, act = _ffn_up(n3, w_fg, w_fu)
    loss_part, dh3, dh3_b, dg_final = _loss_head(act, w_fd, h2, tgt, g_final.reshape(1, d))

    dhg, dhu, dh2, dh2_b, dg_ffn, do_m = _ffn_bwd(dh3_b, w_fd, w_fg, w_fu, hg, hu, h2, g_ffn, dh3, w_o)
    gw_fd = _wgrad("gw_ffn_down", act, dh3_b)
    gw_fg = _wgrad("gw_ffn_gate", dhg, n3)
    gw_fu = _wgrad("gw_ffn_up", dhu, n3)

    gw_o = _wgrad("gw_mem_o", o_m, dh2_b)
    dq_m, dkv_m = _mem_bwd(q_m, kv_m, do_m, batch, s, n_mem)
    gw_q = _wgrad("gw_mem_q", n2, dq_m)
    gw_kv = _wgrad("gw_mem_kv", mem_n, dkv_m)
    (dg_mem_kv,) = _rms_bwd("norm_mem_kv_bwd", (dkv_m, w_kv, NT), mem, g_mem_kv, None, ())
    dh1, dh1_b, dg_mem_q = _rms_bwd("norm_mem_q_bwd", (dq_m, w_q, NT), h1, g_mem_q, dh2, ("f32", "bf16"))

    gw_out = _wgrad("gw_out", mixed, dh1_b)
    dua, dub, dgates, do_a, do_b = _mixer_bwd(dh1_b, w_out, ua, ub, proj, 6 * nb, w_up_a, w_up_b)
    gw_ua = _wgrad("gw_up_a", o_a, dua)
    gw_ub = _wgrad("gw_up_b", o_b, dub)
    (dq_ar, dk_ar, dv_a), (p_fg, p_fd) = _da_bwd(
        qk_a, proj, 2 * nb, bias, o_a, lse_a, do_a, batch, s,
        ride=([gw_fg.reshape(n_sh, -1, d), gw_fd.reshape(n_sh, -1, d)], False, (0, 0)))
    mid = [gw_ua, gw_ub, gw_out.reshape(n_sh, -1, d), gw_q.reshape(n_sh, -1, gw_q.shape[1]),
           gw_kv.reshape(n_sh, -1, gw_kv.shape[1]), gw_o, gw_fu.reshape(n_sh, -1, d)]
    (dq_b, dk_b, dv_b), (*p_mid, p_fu) = _sb_bwd(proj, 3 * nb, 4 * nb, 5 * nb, tot_b, do_b, batch, s,
                                                 ride=(mid, False, (cs_up, cs_up, 0, 0, 0, cs_up, 0)))
    p_ffn = [p_fg, p_fu, p_fd]
    dproj = _rope_apply("rope_bwd", [dq_ar, dk_ar], width, cos_t, sin_a, sin_b, -1.0,
                        tail=(dv_a, dq_b, dk_b, dv_b, dgates))
    grad_x, dg_mix = _rms_bwd("proj_in_bwd", (dproj, w_in, NT), x, g_mix, dh1, ("f32",))
    p_in = _gw_in_scatter(n1, dproj)
    return loss_part, grad_x, [p_in] + list(p_mid) + p_ffn, (dg_mix, dg_mem_q, dg_mem_kv, dg_ffn, dg_final)


WEIGHTS =("w_in", "w_up_a", "w_up_b", "w_out", "w_q_mem", "w_kv_mem", "w_o_mem", "w_ffn_gate", "w_ffn_up", "w_ffn_down")
GAINS = ("g_mix", "g_mem_q", "g_mem_kv", "g_ffn", "g_final")
ORDER = ("g_mix", "w_in", "w_up_a", "w_up_b", "w_out", "g_mem_q", "g_mem_kv", "w_q_mem", "w_kv_mem", "w_o_mem", "g_ffn",
         "w_ffn_gate", "w_ffn_up", "w_ffn_down", "g_final")


def kernel(x, mem, positions, g_mix, w_in, w_up_a, w_up_b, w_out, g_mem_q, g_mem_kv, w_q_mem, w_kv_mem, w_o_mem, g_ffn, w_ffn_gate, w_ffn_up, w_ffn_down, g_final, loss_target, m_g_mix, m_w_in, m_w_up_a, m_w_up_b, m_w_out, m_g_mem_q, m_g_mem_kv, m_w_q_mem, m_w_kv_mem, m_w_o_mem, m_g_ffn, m_w_ffn_gate, m_w_ffn_up, m_w_ffn_down, m_g_final, v_g_mix, v_w_in, v_w_up_a, v_w_up_b, v_w_out, v_g_mem_q, v_g_mem_kv, v_w_q_mem, v_w_kv_mem, v_w_o_mem, v_g_ffn, v_w_ffn_gate, v_w_ffn_up, v_w_ffn_down, v_g_final):
    given = dict(locals())
    batch, s, d = x.shape
    t = batch * s
    flipped = ("w_ffn_gate", "w_ffn_up")

    def view(a, n):
        a = a.reshape(a.shape[-2:])
        return a.T if n in flipped else a

    def unview(a, n):
        return (a.T if n in flipped else a).reshape(given[n].shape)

    shard = {n: view(given[n], n) for n in WEIGHTS}
    gains = [given[n].reshape(1, d) for n in GAINS]

    pad = (-shard["w_ffn_down"].shape[0]) % LANES
    cast = _cast_weights([shard[n] for n in WEIGHTS], [pad if n in flipped + ("w_ffn_down",) else 0 for n in WEIGHTS])
    loss_part, grad_x, pieces, dgains = _local_step(
        x.reshape(t, d), mem.reshape(-1, d), positions.reshape(t, 1), loss_target.reshape(t, d), gains, cast[0],
        cast[1:], batch)

    grad, delta, new_m, new_v = {}, {}, {}, {}
    for n, p in zip(WEIGHTS, pieces):
        outs = _adam("adam_" + n, p, shard[n], view(given["m_" + n], n), view(given["v_" + n], n))
        grad[n], delta[n], new_m[n], new_v[n] = [unview(o, n) for o in outs]

    g_all = _allreduce_small(list(dgains), loss_part)
    small = _adam_small(g_all, gains, [given["m_" + n].reshape(1, d) for n in GAINS],
                        [given["v_" + n].reshape(1, d) for n in GAINS])
    for out, vals in zip((grad, delta, new_m, new_v), small):
        for n, val in zip(GAINS, vals):
            out[n] = val.reshape(given[n].shape)

    loss = g_all[SMALL_ROWS - 1, 0]
    return (loss, grad_x.reshape(x.shape), *[grad[n] for n in ORDER], *[delta[n] for n in ORDER],
            *[new_m[n] for n in ORDER], *[new_v[n] for n in ORDER])
```

```python
import functools
import math

import jax
import jax.numpy as jnp
import numpy as np
from jax import lax
from jax.experimental import pallas as pl
from jax.experimental.pallas import tpu as pltpu

F32 = jnp.float32
BF16 = jnp.bfloat16

N_DEV = 8
HEAD_DIM = 64
MEM_HEAD_DIM = 128
N_HEADS_MEM = 4
BLOCK = 128
DIL_PATTERNS = ((128, 1), (512, 4), (2048, 16))
ROPE_THETA = 500000.0
ROPE_HALF = 8
RMS_EPS = 1e-6
ADAM_LR, ADAM_B1, ADAM_B2, ADAM_EPS, ADAM_WD, ADAM_STEP = 0.001, 0.9, 0.999, 1e-08, 0.01, 10
NEG = -1e30
ROW_TILE = 512
LANES = 128

ANY = pl.BlockSpec(memory_space=pl.ANY)
VMEM = pl.BlockSpec(memory_space=pltpu.VMEM)
NN = (((1,), (0,)), ((), ()))
NT = (((1,), (1,)), ((), ()))
TN = (((0,), (0,)), ((), ()))


def _params(sem):
    return pltpu.CompilerParams(dimension_semantics=sem)


def _mm(name, a, b, *, grid, a_spec, b_spec, o_shape, o_spec, dims, out_dtype):
    def body(a_ref, b_ref, o_ref):
        o_ref[...] = lax.dot_general(a_ref[...], b_ref[...], dims, preferred_element_type=F32).astype(out_dtype)

    return pl.pallas_call(
        body, name=name, grid=grid, in_specs=[a_spec, b_spec],
        out_specs=o_spec, out_shape=jax.ShapeDtypeStruct(o_shape, out_dtype),
        compiler_params=_params(("parallel",) * len(grid)),
    )(a, b)


def _rms_fwd(name, x, g):
    t, d = x.shape
    tm = min(ROW_TILE, t)

    def body(x_ref, g_ref, o_ref):
        xf = x_ref[...]
        r = lax.rsqrt(jnp.mean(xf * xf, axis=-1, keepdims=True) + RMS_EPS)
        o_ref[...] = (xf * r * g_ref[...]).astype(BF16)

    return pl.pallas_call(
        body, name=name, grid=(t // tm,),
        in_specs=[pl.BlockSpec((tm, d), lambda i: (i, 0)), pl.BlockSpec((1, d), lambda i: (0, 0))],
        out_specs=pl.BlockSpec((tm, d), lambda i: (i, 0)), out_shape=jax.ShapeDtypeStruct((t, d), BF16),
        compiler_params=_params(("parallel",)),
    )(x, g)


def _rms_bwd_rows(dnf, xf, gv, res):
    r = lax.rsqrt(jnp.mean(xf * xf, axis=-1, keepdims=True) + RMS_EPS)
    xh = xf * r
    dxh = dnf * gv
    dx = r * (dxh - xh * jnp.mean(dxh * xh, axis=-1, keepdims=True))
    if res is not None:
        dx = dx + res
    return dx, jnp.sum(dnf * xh, axis=0, keepdims=True)


def _rms_bwd(name, dn, x, g, dres, want):
    t, d = x.shape
    has_res = dres is not None
    lhs = list(dn) if isinstance(dn, tuple) else [dn]
    n_lhs = len(lhs[:2])
    tm = min(MM_ROWS if lhs[0].shape[1] <= d else ROW_TILE, t)

    def body(*refs):
        x_ref, g_ref = refs[n_lhs], refs[n_lhs + 1]
        r_ref = refs[n_lhs + 2] if has_res else None
        dx_refs, dg_ref = refs[-1 - len(want):-1], refs[-1]
        if n_lhs == 2:
            dnf = lax.dot_general(refs[0][...], refs[1][...], lhs[2], preferred_element_type=F32)
        else:
            dnf = refs[0][...].astype(F32)
        dx, dg = _rms_bwd_rows(dnf, x_ref[...], g_ref[...], r_ref[...] if has_res else None)
        for kind, dx_ref in zip(want, dx_refs):
            dx_ref[...] = dx.astype(F32 if kind == "f32" else BF16)

        @pl.when(pl.program_id(0) == 0)
        def _():
            dg_ref[...] = jnp.zeros_like(dg_ref)

        dg_ref[...] += dg

    row = pl.BlockSpec((tm, d), lambda i: (i, 0))
    vec = pl.BlockSpec((1, d), lambda i: (0, 0))
    if n_lhs == 2:
        first = [pl.BlockSpec((tm, lhs[0].shape[1]), lambda i: (i, 0)), pl.BlockSpec(lhs[1].shape, lambda i: (0, 0))]
    else:
        first = [row]
    return pl.pallas_call(
        body, name=name, grid=(t // tm,),
        in_specs=first + [row, vec] + ([row] if has_res else []),
        out_specs=[row] * len(want) + [vec],
        out_shape=[jax.ShapeDtypeStruct((t, d), F32 if kind == "f32" else BF16) for kind in want]
        + [jax.ShapeDtypeStruct((1, d), F32)],
        compiler_params=_params(("arbitrary",)),
    )(*(lhs[:2] + [x, g] + ([dres] if has_res else [])))


def _loss_head(a, w, res, tgt, g):
    t, d = res.shape
    k = a.shape[1]
    tm = min(ROW_TILE, t)

    def body(a_ref, w_ref, r_ref, t_ref, g_ref, loss_ref, dh_ref, dhb_ref, dg_ref):
        xf = lax.dot_general(a_ref[...], w_ref[...], NN, preferred_element_type=F32) + r_ref[...]
        gv = g_ref[...]
        r = lax.rsqrt(jnp.mean(xf * xf, axis=-1, keepdims=True) + RMS_EPS)
        xh = xf * r
        e = xh * gv - t_ref[...]
        dy = e * (1.0 / d)
        dxh = dy * gv
        dh = r * (dxh - xh * jnp.mean(dxh * xh, axis=-1, keepdims=True))
        dh_ref[...] = dh
        dhb_ref[...] = dh.astype(BF16)

        @pl.when(pl.program_id(0) == 0)
        def _():
            dg_ref[...] = jnp.zeros_like(dg_ref)
            loss_ref[...] = jnp.zeros_like(loss_ref)

        dg_ref[...] += jnp.sum(dy * xh, axis=0, keepdims=True)
        part = jnp.sum(jnp.sum(e * e, axis=1, keepdims=True), axis=0, keepdims=True) * (0.5 / d)
        loss_ref[...] += jnp.broadcast_to(part, loss_ref.shape)

    row = pl.BlockSpec((tm, d), lambda i: (i, 0))
    vec = pl.BlockSpec((1, d), lambda i: (0, 0))
    return pl.pallas_call(
        body, name="loss_head", grid=(t // tm,),
        in_specs=[pl.BlockSpec((tm, k), lambda i: (i, 0)), pl.BlockSpec((k, d), lambda i: (0, 0)), row, row, vec],
        out_specs=[pl.BlockSpec((8, LANES), lambda i: (0, 0)), row, row, vec],
        out_shape=[jax.ShapeDtypeStruct((8, LANES), F32), jax.ShapeDtypeStruct((t, d), F32),
                   jax.ShapeDtypeStruct((t, d), BF16), jax.ShapeDtypeStruct((1, d), F32)],
        compiler_params=_params(("arbitrary",)),
    )(a, w, res, tgt, g)


def _rope_tables(pos, inv_freq, sel_lo, sel_hi):
    t = pos.shape[0]
    tm = min(ROW_TILE, t)

    def body(p_ref, f_ref, lo_ref, hi_ref, c_ref, sa_ref, sb_ref):
        ang = p_ref[...].astype(F32) * f_ref[...]
        rot = lo_ref[...] + hi_ref[...]
        cs, sn = jnp.cos(ang), jnp.sin(ang)
        c_ref[...] = cs * rot + (1.0 - rot)
        sa_ref[...] = -sn * lo_ref[...]
        sb_ref[...] = sn * hi_ref[...]

    vec = pl.BlockSpec((1, LANES), lambda i: (0, 0))
    row = pl.BlockSpec((tm, LANES), lambda i: (i, 0))
    return pl.pallas_call(
        body, name="rope_tables", grid=(t // tm,),
        in_specs=[pl.BlockSpec((tm, 1), lambda i: (i, 0)), vec, vec, vec],
        out_specs=[row, row, row], out_shape=[jax.ShapeDtypeStruct((t, LANES), F32)] * 3,
        compiler_params=_params(("parallel",)),
    )(pos, inv_freq, sel_lo, sel_hi)


def _rope_apply(name, srcs, width, cos_t, sin_a, sin_b, sign, tail=()):
    t = srcs[0].shape[0]
    tm = min(MM_ROWS, t)
    n_cols = width // LANES
    n_src = len(srcs)

    def body(*refs):
        x_refs, tail_refs = refs[:n_src], refs[n_src:n_src + len(tail)]
        c_ref, sa_ref, sb_ref, o_ref = refs[n_src + len(tail):]
        cs, sa, sb = c_ref[...], sign * sa_ref[...], sign * sb_ref[...]
        for a, x_ref in enumerate(x_refs):
            for c in range(n_cols):
                xf = x_ref[:, c * LANES:(c + 1) * LANES].astype(F32)
                up = pltpu.roll(xf, LANES - ROPE_HALF, 1)
                dn = pltpu.roll(xf, ROPE_HALF, 1)
                o_ref[:, a * width + c * LANES:a * width + (c + 1) * LANES] = (xf * cs + up * sa + dn * sb).astype(BF16)
        col = n_src * width
        for t_ref in tail_refs:
            o_ref[:, col:col + t_ref.shape[1]] = t_ref[...]
            col += t_ref.shape[1]

    wide = n_src * width + sum(a.shape[1] for a in tail)
    tab = pl.BlockSpec((tm, LANES), lambda i: (i, 0))
    return pl.pallas_call(
        body, name=name, grid=(t // tm,),
        in_specs=[pl.BlockSpec((tm, width), lambda i: (i, 0))] * n_src
        + [pl.BlockSpec((tm, a.shape[1]), lambda i: (i, 0)) for a in tail] + [tab, tab, tab],
        out_specs=pl.BlockSpec((tm, wide), lambda i: (i, 0)),
        out_shape=jax.ShapeDtypeStruct((t, wide), BF16),
        compiler_params=_params(("parallel",)),
    )(*srcs, *tail, cos_t, sin_a, sin_b)


DA_T = 256
MIX_STREAMS = 4
SB_BWD_STREAMS = 2


def _lane_lo():
    return lax.broadcasted_iota(jnp.int32, (BLOCK, LANES), 1) < HEAD_DIM


def _dilated_bias_tiles(s):
    n = s // DA_T
    dist = (np.arange(n)[:, None, None] * DA_T + np.arange(DA_T)[None, :, None] - np.arange(DA_T)[None, None, :])
    cnt = np.zeros(dist.shape, np.float32)
    for window, dil in DIL_PATTERNS:
        cnt += ((dist >= 0) & (dist % dil == 0) & (dist <= window)).astype(np.float32)
    return jnp.asarray(np.where(cnt > 0, np.log(np.maximum(cnt, 1.0)), NEG).astype(np.float32))


def _stack_heads(x, lo):
    zero = jnp.zeros_like(x)
    return jnp.concatenate([jnp.where(lo, x, zero), jnp.where(lo, zero, x)], axis=0)


def _da_fwd(qk, proj, v_col0, bias, batch, s, ride=None, streams=MIX_STREAMS):
    t = qk.shape[0]
    nq = s // DA_T
    n_pairs = 4
    ns = streams
    wide = ns * LANES
    scale = HEAD_DIM ** -0.5

    def body(q_ref, k_ref, v_ref, b_ref, o_ref, lse_ref, acc_ref, m_ref, l_ref):
        i = pl.program_id(2)
        lo = lax.broadcasted_iota(jnp.int32, (DA_T, LANES), 1) < HEAD_DIM
        ones = jnp.ones((DA_T, LANES), BF16)
        acc_ref[...] = jnp.zeros_like(acc_ref)
        m_ref[...] = jnp.full(m_ref.shape, NEG, F32)
        l_ref[...] = jnp.zeros_like(l_ref)
        qqs = [_stack_heads(q_ref[:, st * LANES:(st + 1) * LANES] * scale, lo) for st in range(ns)]

        def scores(st, rows, bias2):
            k = k_ref[rows, st * LANES:(st + 1) * LANES]
            return lax.dot_general(qqs[st], k, NT, preferred_element_type=F32) + bias2

        def softmax(st, sc):
            m_old = m_ref[st]
            m_new = jnp.maximum(m_old, jnp.broadcast_to(jnp.max(sc, axis=1, keepdims=True), m_old.shape))
            m_ref[st] = m_new
            return jnp.exp(sc - jnp.concatenate([m_new, m_new], axis=1)).astype(BF16), jnp.exp(m_old - m_new)

        def values(st, rows, p, alpha):
            v = v_ref[rows, st * LANES:(st + 1) * LANES]
            vz = jnp.zeros_like(v)
            l_ref[st] = alpha * l_ref[st] + lax.dot_general(p, ones, NN, preferred_element_type=F32)
            pv = (lax.dot_general(p[:DA_T], jnp.where(lo, v, vz), NN, preferred_element_type=F32)
                  + lax.dot_general(p[DA_T:], jnp.where(lo, vz, v), NN, preferred_element_type=F32))
            acc_ref[st] = acc_ref[st] * jnp.where(lo, alpha[:DA_T], alpha[DA_T:]) + pv

        def trip(dlt, carry):
            rows = pl.ds(pl.multiple_of((i - dlt) * DA_T, DA_T), DA_T)
            bias_t = b_ref[dlt]
            bias2 = jnp.concatenate([bias_t, bias_t], axis=0)
            scs = [scores(st, rows, bias2) for st in range(ns)]
            pas = [softmax(st, scs[st]) for st in range(ns)]
            for st in range(ns):
                values(st, rows, *pas[st])
            return carry

        lax.fori_loop(0, i + 1, trip, 0)
        for st in range(ns):
            cols = slice(st * LANES, (st + 1) * LANES)
            l_t = l_ref[st]
            o_ref[:, cols] = (acc_ref[st] / jnp.where(lo, l_t[:DA_T], l_t[DA_T:])).astype(BF16)
            lse = m_ref[st] + jnp.log(l_t)
            lse_ref[:, cols] = jnp.where(lo, lse[:DA_T], lse[DA_T:])

    blk = pl.BlockSpec((DA_T, wide), lambda b, h, i: (b * nq + i, h))
    return _call(
        body, name="attn_a_fwd", grid=(batch, n_pairs // ns, nq),
        in_specs=[blk,
                  pl.BlockSpec((s, wide), lambda b, h, i: (b, n_pairs // ns + h)),
                  pl.BlockSpec((s, wide), lambda b, h, i: (b, v_col0 // ns + h)),
                  pl.BlockSpec((nq, DA_T, DA_T), lambda b, h, i: (0, 0, 0))],
        out_specs=[blk, blk],
        out_shape=[jax.ShapeDtypeStruct((t, n_pairs * LANES), BF16), jax.ShapeDtypeStruct((t, n_pairs * LANES), F32)],
        scratch=[pltpu.VMEM((ns, DA_T, LANES), F32), pltpu.VMEM((ns, 2 * DA_T, LANES), F32),
                 pltpu.VMEM((ns, 2 * DA_T, LANES), F32)],
        sem=("parallel", "parallel", "arbitrary"), args=(qk, qk, proj, bias), ride=ride)


def _da_bwd(qk, proj, v_col0, bias, o, lse, do, batch, s, ride=None, streams=MIX_STREAMS):
    t = qk.shape[0]
    nq = s // DA_T
    n_pairs = 4
    ns = streams
    wide = ns * LANES
    scale = HEAD_DIM ** -0.5

    def body(q_ref, k_ref, v_ref, b_ref, o_ref, lse_ref, do_ref, dq_ref, dk_ref, dv_ref, dk_acc, dv_acc, dq_acc):
        i = pl.program_id(2)
        lo = lax.broadcasted_iota(jnp.int32, (DA_T, LANES), 1) < HEAD_DIM

        @pl.when(i == 0)
        def _():
            dk_acc[...] = jnp.zeros_like(dk_acc)
            dv_acc[...] = jnp.zeros_like(dv_acc)

        dq_acc[...] = jnp.zeros_like(dq_acc)
        qqs, dds, deltas, lses = [], [], [], []
        for st in range(ns):
            cols = slice(st * LANES, (st + 1) * LANES)
            do_ = do_ref[:, cols]
            qqs.append(_stack_heads(q_ref[:, cols] * scale, lo))
            dds.append(_stack_heads(do_, lo))
            prod = do_.astype(F32) * o_ref[:, cols].astype(F32)
            fz = jnp.zeros_like(prod)
            deltas.append(jnp.concatenate([jnp.sum(jnp.where(lo, prod, fz), axis=1, keepdims=True),
                                           jnp.sum(jnp.where(lo, fz, prod), axis=1, keepdims=True)], axis=0))
            lse_t = lse_ref[:, cols]
            lses.append(jnp.concatenate([lse_t[:, 0:1], lse_t[:, HEAD_DIM:HEAD_DIM + 1]], axis=0))

        def products(st, rows, bias2):
            cols = slice(st * LANES, (st + 1) * LANES)
            sc = lax.dot_general(qqs[st], k_ref[rows, cols], NT, preferred_element_type=F32) + bias2
            return sc, lax.dot_general(dds[st], v_ref[rows, cols], NT, preferred_element_type=F32)

        def weights(st, sc, dp):
            p = jnp.exp(sc - lses[st])
            return (p * (dp - deltas[st])).astype(BF16), p.astype(BF16)

        def gradients(st, rows, ds, p):
            cols = slice(st * LANES, (st + 1) * LANES)
            k = k_ref[rows, cols]
            kz = jnp.zeros_like(k)
            dq_acc[st] += (lax.dot_general(ds[:DA_T], jnp.where(lo, k, kz), NN, preferred_element_type=F32)
                           + lax.dot_general(ds[DA_T:], jnp.where(lo, kz, k), NN, preferred_element_type=F32))
            dk_acc[rows, cols] += lax.dot_general(ds, qqs[st], TN, preferred_element_type=F32)
            dv_acc[rows, cols] += lax.dot_general(p, dds[st], TN, preferred_element_type=F32)

        def trip(dlt, carry):
            rows = pl.ds(pl.multiple_of((i - dlt) * DA_T, DA_T), DA_T)
            bias_t = b_ref[dlt]
            bias2 = jnp.concatenate([bias_t, bias_t], axis=0)
            prods = [products(st, rows, bias2) for st in range(ns)]
            wts = [weights(st, *prods[st]) for st in range(ns)]
            for st in range(ns):
                gradients(st, rows, *wts[st])
            return carry

        lax.fori_loop(0, i + 1, trip, 0)
        for st in range(ns):
            dq_ref[:, st * LANES:(st + 1) * LANES] = (dq_acc[st] * scale).astype(BF16)

        @pl.when(i == nq - 1)
        def _():
            dk_ref[...] = dk_acc[...].astype(BF16)
            dv_ref[...] = dv_acc[...].astype(BF16)

    blk = pl.BlockSpec((DA_T, wide), lambda b, h, i: (b * nq + i, h))
    seq = pl.BlockSpec((s, wide), lambda b, h, i: (b, h), pipeline_mode=pl.Buffered(1))
    one = pl.Buffered(1)
    out = jax.ShapeDtypeStruct((t, n_pairs * LANES), BF16)
    return _call(
        body, name="attn_a_bwd", grid=(batch, n_pairs // ns, nq),
        in_specs=[blk,
                  pl.BlockSpec((s, wide), lambda b, h, i: (b, n_pairs // ns + h), pipeline_mode=one),
                  pl.BlockSpec((s, wide), lambda b, h, i: (b, v_col0 // ns + h), pipeline_mode=one),
                  pl.BlockSpec((nq, DA_T, DA_T), lambda b, h, i: (0, 0, 0), pipeline_mode=one),
                  blk, blk, blk],
        out_specs=[blk, seq, seq], out_shape=[out, out, out],
        scratch=[pltpu.VMEM((s, wide), F32), pltpu.VMEM((s, wide), F32), pltpu.VMEM((ns, DA_T, LANES), F32)],
        sem=("parallel", "parallel", "arbitrary"), args=(qk, qk, proj, bias, o, lse, do), ride=ride)


SB_Q = 256


def _sb_consts(after):
    r = lax.broadcasted_iota(jnp.int32, (2 * BLOCK, 2 * BLOCK), 0) % BLOCK
    c = lax.broadcasted_iota(jnp.int32, (2 * BLOCK, 2 * BLOCK), 1)
    tri = (r > c) if after else (r < c)
    return jnp.logical_or(c >= BLOCK, tri).astype(BF16)


def _split(x):
    hi = x.astype(BF16)
    lo = (x - hi.astype(F32)).astype(BF16)
    return jnp.concatenate([hi, lo], axis=1)


def _sb_fwd(proj, q_col0, k_col0, v_col0, batch, s, ride=None, streams=MIX_STREAMS):
    t = proj.shape[0]
    nq = s // SB_Q
    n_pairs = 4
    ns = streams
    wide = ns * LANES
    scale = HEAD_DIM ** -0.5

    def body(q_ref, k_ref, v_ref, o_ref, tot_ref, acc_ref, run_ref):
        i = pl.program_id(2)
        lo_q = lax.broadcasted_iota(jnp.int32, (SB_Q, LANES), 1) < HEAD_DIM
        lo_k = _lane_lo()
        mat = _sb_consts(True)
        row = lax.broadcasted_iota(jnp.int32, (2 * SB_Q, LANES), 0) % SB_Q
        ahead = row - lax.broadcasted_iota(jnp.int32, (2 * SB_Q, LANES), 1)
        acc_ref[...] = jnp.zeros_like(acc_ref)
        run_ref[...] = jnp.zeros_like(run_ref)
        qqs = [_stack_heads(q_ref[:, st * LANES:(st + 1) * LANES] * scale, lo_q) for st in range(ns)]

        def units(todo):
            def rows(j):
                return pl.ds(pl.multiple_of(j * BLOCK, BLOCK), BLOCK)

            zs = [lax.dot_general(qqs[st], k_ref[rows(j), st * LANES:(st + 1) * LANES], NT, preferred_element_type=F32)
                  for st, j, _ in todo]
            logs = []
            for z, (_, _, off) in zip(zs, todo):
                lsig = jnp.minimum(z, 0.0) - jnp.log(1.0 + jnp.exp(-jnp.abs(z)))
                lneg = lsig - z
                if off is not None:
                    lneg = jnp.where(ahead > off, lneg, 0.0)
                logs.append((lsig, _split(lneg)))
            sums = [lax.dot_general(cat, mat, NN, preferred_element_type=F32) for _, cat in logs]
            probs = []
            for (lsig, _), sm, (st, _, off) in zip(logs, sums, todo):
                run = run_ref[st]
                a = jnp.exp(lsig + run + sm[:, :BLOCK])
                if off is not None:
                    a = jnp.where(ahead > off, a, 0.0)
                run_ref[st] = run + sm[:, BLOCK:]
                probs.append(a.astype(BF16))
            for ab, (st, j, _) in zip(probs, todo):
                v = v_ref[rows(j), st * LANES:(st + 1) * LANES]
                vz = jnp.zeros_like(v)
                acc_ref[st] += (lax.dot_general(ab[:SB_Q], jnp.where(lo_k, v, vz), NN, preferred_element_type=F32)
                                + lax.dot_general(ab[SB_Q:], jnp.where(lo_k, vz, v), NN, preferred_element_type=F32))

        units([(st, 2 * i + 1, BLOCK) for st in range(ns)] + [(st, 2 * i, 0) for st in range(ns)])

        def pair(p, carry):
            jp = i - 1 - p
            units([(st, 2 * jp + 1, None) for st in range(ns)] + [(st, 2 * jp, None) for st in range(ns)])
            return carry

        lax.fori_loop(0, i, pair, 0)
        for st in range(ns):
            cols = slice(st * LANES, (st + 1) * LANES)
            o_ref[:, cols] = acc_ref[st].astype(BF16)
            tot_ref[:, cols] = jnp.where(lo_q, run_ref[st, 0:SB_Q, :], run_ref[st, SB_Q:2 * SB_Q, :])

    def seq(col0):
        return pl.BlockSpec((s, wide), lambda b, h, i: (b, col0 // ns + h))

    blk = pl.BlockSpec((SB_Q, wide), lambda b, h, i: (b * nq + i, h))
    return _call(
        body, name="attn_b_fwd", grid=(batch, n_pairs // ns, nq),
        in_specs=[pl.BlockSpec((SB_Q, wide), lambda b, h, i: (b * nq + i, q_col0 // ns + h)), seq(k_col0), seq(v_col0)],
        out_specs=[blk, blk],
        out_shape=[jax.ShapeDtypeStruct((t, n_pairs * LANES), BF16), jax.ShapeDtypeStruct((t, n_pairs * LANES), F32)],
        scratch=[pltpu.VMEM((ns, SB_Q, LANES), F32), pltpu.VMEM((ns, 2 * SB_Q, LANES), F32)],
        sem=("parallel", "parallel", "arbitrary"), args=(proj, proj, proj), ride=ride)


def _sb_bwd(proj, q_col0, k_col0, v_col0, tot, do, batch, s, ride=None, streams=SB_BWD_STREAMS):
    t = proj.shape[0]
    nq = s // SB_Q
    n_pairs = 4
    ns = streams
    wide = ns * LANES
    scale = HEAD_DIM ** -0.5

    def body(q_ref, k_ref, v_ref, tot_ref, do_ref, dq_ref, dk_ref, dv_ref, dk_acc, dv_acc, dq_acc, seen_ref, gsum_ref):
        i = pl.program_id(2)
        lo_q = lax.broadcasted_iota(jnp.int32, (SB_Q, LANES), 1) < HEAD_DIM
        lo_k = _lane_lo()

        @pl.when(i == 0)
        def _():
            dk_acc[...] = jnp.zeros_like(dk_acc)
            dv_acc[...] = jnp.zeros_like(dv_acc)

        mat_after = _sb_consts(True)
        mat_before = _sb_consts(False)[:BLOCK]
        row = lax.broadcasted_iota(jnp.int32, (2 * SB_Q, LANES), 0) % SB_Q
        ahead = row - lax.broadcasted_iota(jnp.int32, (2 * SB_Q, LANES), 1)
        dq_acc[...] = jnp.zeros_like(dq_acc)
        seen_ref[...] = jnp.zeros_like(seen_ref)
        gsum_ref[...] = jnp.zeros_like(gsum_ref)
        qqs, dds, totals = [], [], []
        for st in range(ns):
            cols = slice(st * LANES, (st + 1) * LANES)
            qqs.append(_stack_heads(q_ref[:, cols] * scale, lo_q))
            dds.append(_stack_heads(do_ref[:, cols], lo_q))
            tot_t = tot_ref[:, cols]
            totals.append(jnp.concatenate([jnp.broadcast_to(tot_t[:, 0:1], (SB_Q, LANES)),
                                           jnp.broadcast_to(tot_t[:, HEAD_DIM:HEAD_DIM + 1], (SB_Q, LANES))], axis=0))

        def units(todo):
            def rows(j):
                return pl.ds(pl.multiple_of(j * BLOCK, BLOCK), BLOCK)

            def cols(st):
                return slice(st * LANES, (st + 1) * LANES)

            prods = [(lax.dot_general(qqs[st], k_ref[rows(j), cols(st)], NT, preferred_element_type=F32),
                      lax.dot_general(dds[st], v_ref[rows(j), cols(st)], NT, preferred_element_type=F32))
                     for st, j, _ in todo]
            logs = []
            for (z, _), (_, _, off) in zip(prods, todo):
                lsig = jnp.minimum(z, 0.0) - jnp.log(1.0 + jnp.exp(-jnp.abs(z)))
                lneg = lsig - z
                if off is not None:
                    lneg = jnp.where(ahead > off, lneg, 0.0)
                logs.append((lsig, _split(lneg)))
            sums = [lax.dot_general(cat, mat_after, NN, preferred_element_type=F32) for _, cat in logs]
            gates = []
            for (lsig, _), sm, (_, da), (st, _, off) in zip(logs, sums, prods, todo):
                seen = seen_ref[st]
                a = jnp.exp(lsig + (totals[st] - seen - sm[:, BLOCK:]) + sm[:, :BLOCK])
                if off is not None:
                    a = jnp.where(ahead > off, a, 0.0)
                seen_ref[st] = seen + sm[:, BLOCK:]
                g = a * da
                gates.append((a.astype(BF16), g, g.astype(BF16)))
            gsums = [lax.dot_general(cat, mat_before, NN, preferred_element_type=F32) for _, _, cat in gates]
            outs = []
            for (lsig, _), (ab, g, _), gs, (st, _, off) in zip(logs, gates, gsums, todo):
                gsum = gsum_ref[st]
                dz = g - jnp.exp(lsig) * (g + gsum + gs[:, :BLOCK])
                if off is not None:
                    dz = jnp.where(ahead > off, dz, 0.0)
                gsum_ref[st] = gsum + gs[:, BLOCK:]
                outs.append((dz.astype(BF16), ab))
            for (dzb, ab), (st, j, _) in zip(outs, todo):
                k = k_ref[rows(j), cols(st)]
                kz = jnp.zeros_like(k)
                dq_acc[st] += (lax.dot_general(dzb[:SB_Q], jnp.where(lo_k, k, kz), NN, preferred_element_type=F32)
                               + lax.dot_general(dzb[SB_Q:], jnp.where(lo_k, kz, k), NN, preferred_element_type=F32))
                dk_acc[rows(j), cols(st)] += lax.dot_general(dzb, qqs[st], TN, preferred_element_type=F32)
                dv_acc[rows(j), cols(st)] += lax.dot_general(ab, dds[st], TN, preferred_element_type=F32)

        def pair(p, carry):
            units([(st, 2 * p, None) for st in range(ns)] + [(st, 2 * p + 1, None) for st in range(ns)])
            return carry

        lax.fori_loop(0, i, pair, 0)
        units([(st, 2 * i, 0) for st in range(ns)] + [(st, 2 * i + 1, BLOCK) for st in range(ns)])
        for st in range(ns):
            dq_ref[:, st * LANES:(st + 1) * LANES] = (dq_acc[st] * scale).astype(BF16)

        @pl.when(i == nq - 1)
        def _():
            dk_ref[...] = dk_acc[...].astype(BF16)
            dv_ref[...] = dv_acc[...].astype(BF16)

    def seq_in(col0):
        return pl.BlockSpec((s, wide), lambda b, h, i: (b, col0 // ns + h))

    blk = pl.BlockSpec((SB_Q, wide), lambda b, h, i: (b * nq + i, h))
    seq = pl.BlockSpec((s, wide), lambda b, h, i: (b, h))
    out = jax.ShapeDtypeStruct((t, n_pairs * LANES), BF16)
    return _call(
        body, name="attn_b_bwd", grid=(batch, n_pairs // ns, nq),
        in_specs=[pl.BlockSpec((SB_Q, wide), lambda b, h, i: (b * nq + i, q_col0 // ns + h)), seq_in(k_col0),
                  seq_in(v_col0), blk, blk],
        out_specs=[blk, seq, seq], out_shape=[out, out, out],
        scratch=[pltpu.VMEM((s, wide), F32), pltpu.VMEM((s, wide), F32), pltpu.VMEM((ns, SB_Q, LANES), F32),
                 pltpu.VMEM((ns, 2 * SB_Q, LANES), F32), pltpu.VMEM((ns, 2 * SB_Q, LANES), F32)],
        sem=("parallel", "parallel", "arbitrary"), args=(proj, proj, proj, tot, do), ride=ride)


MEM_Q_TILE = 512


def _mem_fwd(q, kv, batch, s, n_mem):
    t, width = q.shape
    tq = min(MEM_Q_TILE, s)
    nq = s // tq
    scale = MEM_HEAD_DIM ** -0.5

    def body(q_ref, kv_ref, o_ref):
        for h in range(N_HEADS_MEM):
            cols = slice(h * MEM_HEAD_DIM, (h + 1) * MEM_HEAD_DIM)
            k = kv_ref[:, cols]
            v = kv_ref[:, width + h * MEM_HEAD_DIM: width + (h + 1) * MEM_HEAD_DIM]
            sc = lax.dot_general(q_ref[:, cols], k, NT, preferred_element_type=F32) * scale
            p = jnp.exp(sc - jnp.max(sc, axis=1, keepdims=True))
            p = p / jnp.sum(p, axis=1, keepdims=True)
            o_ref[:, cols] = lax.dot_general(p.astype(BF16), v, NN, preferred_element_type=F32).astype(BF16)

    return pl.pallas_call(
        body, name="mem_attn_fwd", grid=(batch, nq),
        in_specs=[pl.BlockSpec((tq, width), lambda b, i: (b * nq + i, 0)),
                  pl.BlockSpec((n_mem, 2 * width), lambda b, i: (b, 0))],
        out_specs=pl.BlockSpec((tq, width), lambda b, i: (b * nq + i, 0)),
        out_shape=jax.ShapeDtypeStruct((t, width), BF16),
        compiler_params=_params(("parallel", "parallel")),
    )(q, kv)


def _mem_bwd(q, kv, do, batch, s, n_mem):
    t, width = q.shape
    tq = min(MEM_Q_TILE, s)
    nq = s // tq
    scale = MEM_HEAD_DIM ** -0.5

    def body(q_ref, kv_ref, do_ref, dq_ref, dkv_ref, acc):
        i = pl.program_id(1)

        @pl.when(i == 0)
        def _():
            acc[...] = jnp.zeros_like(acc)

        for h in range(N_HEADS_MEM):
            cols = slice(h * MEM_HEAD_DIM, (h + 1) * MEM_HEAD_DIM)
            vcols = slice(width + h * MEM_HEAD_DIM, width + (h + 1) * MEM_HEAD_DIM)
            qh, k, v, doh = q_ref[:, cols], kv_ref[:, cols], kv_ref[:, vcols], do_ref[:, cols]
            sc = lax.dot_general(qh, k, NT, preferred_element_type=F32) * scale
            p = jnp.exp(sc - jnp.max(sc, axis=1, keepdims=True))
            p = p / jnp.sum(p, axis=1, keepdims=True)
            dp = lax.dot_general(doh, v, NT, preferred_element_type=F32)
            ds = (p * (dp - jnp.sum(p * dp, axis=1, keepdims=True)) * scale).astype(BF16)
            dq_ref[:, cols] = lax.dot_general(ds, k, NN, preferred_element_type=F32).astype(BF16)
            acc[:, cols] += lax.dot_general(ds, qh, TN, preferred_element_type=F32)
            acc[:, vcols] += lax.dot_general(p.astype(BF16), doh, TN, preferred_element_type=F32)

        @pl.when(i == nq - 1)
        def _():
            dkv_ref[...] = acc[...].astype(BF16)

    row = pl.BlockSpec((tq, width), lambda b, i: (b * nq + i, 0))
    kvs = pl.BlockSpec((n_mem, 2 * width), lambda b, i: (b, 0))
    return pl.pallas_call(
        body, name="mem_attn_bwd", grid=(batch, nq),
        in_specs=[row, kvs, row], out_specs=[row, kvs],
        out_shape=[jax.ShapeDtypeStruct((t, width), BF16), jax.ShapeDtypeStruct((batch * n_mem, 2 * width), BF16)],
        scratch_shapes=[pltpu.VMEM((n_mem, 2 * width), F32)],
        compiler_params=_params(("parallel", "arbitrary")),
    )(q, kv, do)


def _mixer_fwd(o_a, o_b, w_a, w_b, proj, gate_col0, w_out, x, g, w_q):
    t, width = o_a.shape
    d = w_a.shape[1]
    nq_cols = w_q.shape[1]
    tm = min(ROW_TILE, t)
    gb0 = gate_col0 * LANES // d

    def body(oa_ref, ob_ref, wa_ref, wb_ref, ga_ref, gb_ref, wo_ref, x_ref, g_ref, wq_ref, ua_ref, ub_ref, mix_ref,
             n_ref, h_ref, q_ref):
        ua = lax.dot_general(oa_ref[...], wa_ref[...], NN, preferred_element_type=F32)
        ub = lax.dot_general(ob_ref[...], wb_ref[...], NN, preferred_element_type=F32)
        ua_ref[...] = ua.astype(BF16)
        ub_ref[...] = ub.astype(BF16)
        mixed = (jax.nn.sigmoid(ga_ref[...].astype(F32)) * ua + jax.nn.sigmoid(gb_ref[...].astype(F32)) * ub).astype(BF16)
        mix_ref[...] = mixed
        h = lax.dot_general(mixed, wo_ref[...], NN, preferred_element_type=F32) + x_ref[...]
        h_ref[...] = h
        r = lax.rsqrt(jnp.mean(h * h, axis=-1, keepdims=True) + RMS_EPS)
        n = (h * r * g_ref[...]).astype(BF16)
        n_ref[...] = n
        q_ref[...] = lax.dot_general(n, wq_ref[...], NN, preferred_element_type=F32).astype(BF16)

    row = pl.BlockSpec((tm, width), lambda i: (i, 0))
    wsp = pl.BlockSpec((width, d), lambda i: (0, 0))
    out = pl.BlockSpec((tm, d), lambda i: (i, 0))
    osh = jax.ShapeDtypeStruct((t, d), BF16)
    return pl.pallas_call(
        body, name="mixer_fwd", grid=(t // tm,),
        in_specs=[row, row, wsp, wsp,
                  pl.BlockSpec((tm, d), lambda i: (i, gb0)), pl.BlockSpec((tm, d), lambda i: (i, gb0 + 1)),
                  pl.BlockSpec((d, d), lambda i: (0, 0)), out, pl.BlockSpec((1, d), lambda i: (0, 0)),
                  pl.BlockSpec((d, nq_cols), lambda i: (0, 0))],
        out_specs=[out, out, out, out, out, pl.BlockSpec((tm, nq_cols), lambda i: (i, 0))],
        out_shape=[osh, osh, osh, osh, jax.ShapeDtypeStruct((t, d), F32), jax.ShapeDtypeStruct((t, nq_cols), BF16)],
        compiler_params=_params(("parallel",)),
    )(o_a, o_b, w_a, w_b, proj, proj, w_out, x, g, w_q)


def _mixer_bwd(dh, w_out, ua, ub, proj, gate_col0, w_a, w_b):
    t, d = dh.shape
    width = w_a.shape[0]
    tm = min(ROW_TILE, t)
    nc = d // LANES

    def body(dh_ref, w_ref, ua_ref, ub_ref, ga_ref, gb_ref, wa_ref, wb_ref, dua_ref, dub_ref, dg_ref, doa_ref, dob_ref):
        dm = lax.dot_general(dh_ref[...], w_ref[...], NT, preferred_element_type=F32)
        sa = jax.nn.sigmoid(ga_ref[...].astype(F32))
        sb = jax.nn.sigmoid(gb_ref[...].astype(F32))
        dua = (dm * sa).astype(BF16)
        dub = (dm * sb).astype(BF16)
        dua_ref[...] = dua
        dub_ref[...] = dub
        dg_ref[:, 0:d] = (dm * ua_ref[...].astype(F32) * sa * (1.0 - sa)).astype(BF16)
        dg_ref[:, d:2 * d] = (dm * ub_ref[...].astype(F32) * sb * (1.0 - sb)).astype(BF16)
        doa_ref[...] = lax.dot_general(dua, wa_ref[...], NT, preferred_element_type=F32).astype(BF16)
        dob_ref[...] = lax.dot_general(dub, wb_ref[...], NT, preferred_element_type=F32).astype(BF16)

    row = pl.BlockSpec((tm, d), lambda i: (i, 0))
    wsp = pl.BlockSpec((width, d), lambda i: (0, 0))
    osp = pl.BlockSpec((tm, width), lambda i: (i, 0))
    return pl.pallas_call(
        body, name="mixer_bwd", grid=(t // tm,),
        in_specs=[row, pl.BlockSpec((d, d), lambda i: (0, 0)), row, row,
                  pl.BlockSpec((tm, d), lambda i: (i, gate_col0 // nc)),
                  pl.BlockSpec((tm, d), lambda i: (i, gate_col0 // nc + 1)), wsp, wsp],
        out_specs=[row, row, pl.BlockSpec((tm, 2 * d), lambda i: (i, 0)), osp, osp],
        out_shape=[jax.ShapeDtypeStruct((t, d), BF16), jax.ShapeDtypeStruct((t, d), BF16),
                   jax.ShapeDtypeStruct((t, 2 * d), BF16), jax.ShapeDtypeStruct((t, width), BF16),
                   jax.ShapeDtypeStruct((t, width), BF16)],
        compiler_params=_params(("parallel",)),
    )(dh, w_out, ua, ub, proj, proj, w_a, w_b)


FFN_COLS = 1024
FFN_CHUNK = 256


def _ffn_up(n, w_gate, w_up):
    t, d = n.shape
    hidden = w_gate.shape[0]
    tm = min(2 * ROW_TILE, t)
    tn = min(FFN_COLS, hidden)
    tc = min(FFN_CHUNK, tn)

    def body(n_ref, wg_ref, wu_ref, hg_ref, hu_ref, act_ref):
        for c in range(0, tn, tc):
            hg = lax.dot_general(n_ref[...], wg_ref[c:c + tc, :], NT, preferred_element_type=F32)
            hu = lax.dot_general(n_ref[...], wu_ref[c:c + tc, :], NT, preferred_element_type=F32)
            hg_ref[:, c:c + tc] = hg.astype(BF16)
            hu_ref[:, c:c + tc] = hu.astype(BF16)
            act_ref[:, c:c + tc] = (hg * jax.nn.sigmoid(hg) * hu).astype(BF16)

    wsp = pl.BlockSpec((tn, d), lambda j, i: (j, 0))
    out = pl.BlockSpec((tm, tn), lambda j, i: (i, j))
    osh = jax.ShapeDtypeStruct((t, hidden), BF16)
    return pl.pallas_call(
        body, name="ffn_up", grid=(hidden // tn, t // tm),
        in_specs=[pl.BlockSpec((tm, d), lambda j, i: (i, 0)), wsp, wsp],
        out_specs=[out, out, out], out_shape=[osh, osh, osh],
        compiler_params=_params(("parallel", "parallel")),
    )(n, w_gate, w_up)


def _ffn_bwd(dh, w_down, w_gate, w_up, hg, hu, x, g, dres, w_prev):
    t, d = dh.shape
    hidden = w_down.shape[0]
    q = w_prev.shape[0]
    tm = min(ROW_TILE, t)
    tn = min(FFN_COLS, hidden)
    nj = hidden // tn

    def body(dh_ref, wd_ref, wg_ref, wu_ref, hg_ref, hu_ref, x_ref, g_ref, r_ref, wp_ref, dhg_ref, dhu_ref, dx_ref,
             dxb_ref, dg_ref, do_ref, acc):
        j, i = pl.program_id(0), pl.program_id(1)
        dact = lax.dot_general(dh_ref[...], wd_ref[...], NT, preferred_element_type=F32)
        hg = hg_ref[...].astype(F32)
        sg = jax.nn.sigmoid(hg)
        dhu = (dact * hg * sg).astype(BF16)
        dhg = (dact * hu_ref[...].astype(F32) * sg * (1.0 + hg * (1.0 - sg))).astype(BF16)
        dhu_ref[...] = dhu
        dhg_ref[...] = dhg
        part = (lax.dot_general(dhg, wg_ref[...], NN, preferred_element_type=F32)
                + lax.dot_general(dhu, wu_ref[...], NN, preferred_element_type=F32))

        @pl.when(j == 0)
        def _():
            acc[i] = part

        @pl.when(j > 0)
        def _():
            acc[i] += part

        @pl.when(jnp.logical_and(j == 0, i == 0))
        def _():
            dg_ref[...] = jnp.zeros_like(dg_ref)

        @pl.when(j == nj - 1)
        def _():
            dx, dg = _rms_bwd_rows(acc[i], x_ref[...], g_ref[...], r_ref[...])
            dx_ref[...] = dx
            dxb = dx.astype(BF16)
            dxb_ref[...] = dxb
            dg_ref[...] += dg
            do_ref[...] = lax.dot_general(dxb, wp_ref[...], NT, preferred_element_type=F32).astype(BF16)

    hid = pl.BlockSpec((tm, tn), lambda j, i: (i, j))
    wsp = pl.BlockSpec((tn, d), lambda j, i: (j, 0), pipeline_mode=pl.Buffered(1))
    late = pl.BlockSpec((tm, d), lambda j, i: (jnp.where(j == nj - 1, i, 0), 0))
    late_q = pl.BlockSpec((tm, q), lambda j, i: (jnp.where(j == nj - 1, i, 0), 0))
    vec = pl.BlockSpec((1, d), lambda j, i: (0, 0))
    osh = jax.ShapeDtypeStruct((t, hidden), BF16)
    return pl.pallas_call(
        body, name="ffn_bwd", grid=(nj, t // tm),
        in_specs=[pl.BlockSpec((tm, d), lambda j, i: (i, 0)), wsp, wsp, wsp, hid, hid, late, vec, late,
                  pl.BlockSpec((q, d), lambda j, i: (0, 0), pipeline_mode=pl.Buffered(1))],
        out_specs=[hid, hid, late, late, vec, late_q],
        out_shape=[osh, osh, jax.ShapeDtypeStruct((t, d), F32), jax.ShapeDtypeStruct((t, d), BF16),
                   jax.ShapeDtypeStruct((1, d), F32), jax.ShapeDtypeStruct((t, q), BF16)],
        scratch_shapes=[pltpu.VMEM((t // tm, tm, d), F32)],
        compiler_params=_params(("arbitrary", "arbitrary")),
    )(dh, w_down, w_gate, w_up, hg, hu, x, g, dres, w_prev)


MM_ROWS = 1024


def _mm_w(name, a, w, out_dtype, dims=NN):
    t, k = a.shape
    n = w.shape[1] if dims == NN else w.shape[0]
    tm, tn = min(MM_ROWS, t), min(1024, n)
    o_spec = pl.BlockSpec((tm, tn), lambda j, i: (i, j))
    b_spec = pl.BlockSpec((k, tn), lambda j, i: (0, j)) if dims == NN else pl.BlockSpec((tn, k), lambda j, i: (j, 0))
    return _mm(name, a, w, grid=(n // tn, t // tm), a_spec=pl.BlockSpec((tm, k), lambda j, i: (i, 0)), b_spec=b_spec,
               o_shape=(t, n), o_spec=o_spec, dims=dims, out_dtype=out_dtype)


def _mm_res_norm(name, a, w, res, g):
    t, k = a.shape
    d = w.shape[1]
    tm = min(MM_ROWS, t)

    def body(a_ref, w_ref, r_ref, g_ref, h_ref, n_ref):
        h = lax.dot_general(a_ref[...], w_ref[...], NN, preferred_element_type=F32) + r_ref[...]
        h_ref[...] = h
        r = lax.rsqrt(jnp.mean(h * h, axis=-1, keepdims=True) + RMS_EPS)
        n_ref[...] = (h * r * g_ref[...]).astype(BF16)

    row = pl.BlockSpec((tm, d), lambda i: (i, 0))
    return pl.pallas_call(
        body, name=name, grid=(t // tm,),
        in_specs=[pl.BlockSpec((tm, k), lambda i: (i, 0)), pl.BlockSpec((k, d), lambda i: (0, 0)), row,
                  pl.BlockSpec((1, d), lambda i: (0, 0))],
        out_specs=[row, row], out_shape=[jax.ShapeDtypeStruct((t, d), F32), jax.ShapeDtypeStruct((t, d), BF16)],
        compiler_params=_params(("parallel",)),
    )(a, w, res, g)


WGRAD_COLS = 256


def _wgrad(name, a, g, tk=1024, tn=1024):
    t, k = a.shape
    n = g.shape[1]
    tm, tk, tn = min(2 * MM_ROWS, t), min(tk, k), min(tn, n)
    nr = t // tm
    tc = min(WGRAD_COLS, tn)

    def body(a_ref, g_ref, o_ref, *acc):
        def run(first, last):
            for c in range(0, tn, tc):
                p = lax.dot_general(a_ref[...], g_ref[:, c:c + tc], TN, preferred_element_type=F32)
                if not first:
                    p += acc[0][:, c:c + tc]
                if last:
                    o_ref[:, c:c + tc] = p.astype(BF16)
                else:
                    acc[0][:, c:c + tc] = p

        if nr == 1:
            run(True, True)
            return
        r = pl.program_id(2)
        pl.when(r == 0)(functools.partial(run, True, False))
        if nr > 2:
            pl.when(jnp.logical_and(r > 0, r < nr - 1))(functools.partial(run, False, False))
        pl.when(r == nr - 1)(functools.partial(run, False, True))

    return pl.pallas_call(
        body, name=name, grid=(k // tk, n // tn, nr),
        in_specs=[pl.BlockSpec((tm, tk), lambda p, q, r: (r, p)), pl.BlockSpec((tm, tn), lambda p, q, r: (r, q))],
        out_specs=pl.BlockSpec((tk, tn), lambda p, q, r: (p, q)), out_shape=jax.ShapeDtypeStruct((k, n), BF16),
        scratch_shapes=[pltpu.VMEM((tk, tn), F32)] if nr > 1 else [],
        compiler_params=_params(("parallel", "parallel", "arbitrary")),
    )(a, g)


def _peers():
    x, y, c = lax.axis_index("x"), lax.axis_index("y"), lax.axis_index("c")
    me = 4 * x + 2 * y + c
    out = []
    for k in range(1, N_DEV):
        kx, ky, kc = (k >> 2) & 1, (k >> 1) & 1, k & 1
        px = 1 - x if kx else x
        py = 1 - y if ky else y
        pc = 1 - c if kc else c
        out.append(((px, py, pc), 4 * px + 2 * py + pc))
    return me, out


def _cast_weights(ws, pad_rows):
    def body(*refs):
        n = len(refs) // 2
        for i_ref, o_ref, pr in zip(refs[:n], refs[n:], pad_rows):
            r, c = i_ref.shape
            o_ref[0:r, :] = i_ref[...].astype(BF16)
            if pr:
                o_ref[r:r + pr, :] = jnp.zeros((pr, c), BF16)

    return pl.pallas_call(
        body, name="cast_weights", in_specs=[VMEM] * len(ws), out_specs=[VMEM] * len(ws),
        out_shape=[jax.ShapeDtypeStruct((w.shape[0] + pr, w.shape[1]), BF16) for w, pr in zip(ws, pad_rows)],
    )(*ws)


def _window(ref, j, c):
    return ref.at[:, pl.ds(pl.multiple_of(j * c, LANES), c)]


def _direct_copies(ins, outs, sems, gather, cols, landed):
    send_sems, recv_sems, loc_sems = sems
    n_peer = N_DEV - 1
    me, peers = _peers()

    def src(w, j):
        if gather:
            return ins[w]
        return _window(ins[w], j, cols[w]) if cols[w] else ins[w].at[j]

    def dst(w, j):
        return _window(outs[w], j, cols[w]) if gather and cols[w] else outs[w].at[j]

    local = [pltpu.make_async_copy(src(w, me), dst(w, me), loc_sems.at[w]) for w in range(len(ins))]
    remote = [pltpu.make_async_remote_copy(
        src_ref=src(w, idx), dst_ref=dst(w, idx if landed else me),
        send_sem=send_sems.at[w * n_peer + k], recv_sem=recv_sems.at[w * n_peer + k],
        device_id=dev, device_id_type=pl.DeviceIdType.MESH)
        for k, (dev, idx) in reversed(list(enumerate(peers))) for w in range(len(ins))]
    return local, remote


OTHER_CHIPS = (2, 4, 6)


def _gather_copies(ins, outs, sems, cols):
    send_sems, recv_sems, loc_sems = sems
    x, y, c = lax.axis_index("x"), lax.axis_index("y"), lax.axis_index("c")
    me = 4 * x + 2 * y + c
    n_pair = N_DEV - 1

    def dev(mask):
        return (1 - x if mask & 4 else x, 1 - y if mask & 2 else y, 1 - c if mask & 1 else c)

    def slot(w, mask):
        j = jnp.bitwise_xor(me, mask)
        return _window(outs[w], j, cols[w]) if cols[w] else outs[w].at[j]

    def remote(w, pair, src, to_slot, target):
        return pltpu.make_async_remote_copy(src_ref=src, dst_ref=slot(w, to_slot), send_sem=send_sems.at[w * n_pair + pair],
                                            recv_sem=recv_sems.at[w * n_pair + pair], device_id=dev(target),
                                            device_id_type=pl.DeviceIdType.MESH)

    ws = range(len(ins))
    return dict(
        local=[pltpu.make_async_copy(ins[w], slot(w, 0), loc_sems.at[w]) for w in ws],
        to_chips=[remote(w, 1 + t, ins[w], 0, m) for t, m in enumerate(OTHER_CHIPS) for w in ws],
        to_core=[remote(w, 0, ins[w], 0, 1) for w in ws],
        from_chips=[remote(w, 1 + t, ins[w], m, 0) for t, m in enumerate(OTHER_CHIPS) for w in ws],
        pass_on=[remote(w, 4 + t, slot(w, m), m, 1) for t, m in enumerate(OTHER_CHIPS) for w in ws],
        from_core=[remote(w, 0, ins[w], 1, 0) for w in ws]
        + [remote(w, 4 + t, ins[w], m + 1, 0) for t, m in enumerate(OTHER_CHIPS) for w in ws])


TWO_LEVEL = "gather in two levels"


def _exchange_start(ins, outs, sems, gather, cols):
    if gather == TWO_LEVEL:
        cps = _gather_copies(ins, outs, sems, cols)
        for cp in cps["local"] + cps["to_chips"] + cps["to_core"]:
            cp.start()
    else:
        local, remote = _direct_copies(ins, outs, sems, gather, cols, False)
        for cp in local + remote:
            cp.start()


def _exchange_pass_on(ins, outs, sems, gather, cols, chips):
    if gather == TWO_LEVEL:
        cps = _gather_copies(ins, outs, sems, cols)
        n = len(ins)
        for t in chips:
            for arrived, onward in zip(cps["from_chips"][t * n:(t + 1) * n], cps["pass_on"][t * n:(t + 1) * n]):
                arrived.wait_recv()
                onward.start()


def _exchange_wait(ins, outs, sems, gather, cols):
    if gather == TWO_LEVEL:
        cps = _gather_copies(ins, outs, sems, cols)
        for cp in cps["local"]:
            cp.wait()
        for cp in cps["to_chips"] + cps["to_core"] + cps["pass_on"]:
            cp.wait_send()
        for cp in cps["from_core"]:
            cp.wait_recv()
    else:
        local, remote = _direct_copies(ins, outs, sems, gather, cols, True)
        for cp in local:
            cp.wait()
        for cp in remote:
            cp.wait_send()
            cp.wait_recv()


def _exchange_shapes(arrs, gather, cols):
    n = len(arrs)
    out_shape = []
    for a, c in zip(arrs, cols):
        if gather:
            shape = (a.shape[0], N_DEV * c) if c else (N_DEV,) + a.shape
        else:
            shape = (N_DEV, a.shape[0], c) if c else a.shape
        out_shape.append(jax.ShapeDtypeStruct(shape, a.dtype))
    sems = [pltpu.SemaphoreType.DMA((n * (N_DEV - 1),)), pltpu.SemaphoreType.DMA((n * (N_DEV - 1),)),
            pltpu.SemaphoreType.DMA((n,))]
    return out_shape, sems


def _call(body, *, name, grid, in_specs, out_specs, out_shape, scratch, sem, args, ride=None):
    if ride is None:
        outs = pl.pallas_call(body, name=name, grid=grid, in_specs=in_specs, out_specs=out_specs, out_shape=out_shape,
                              scratch_shapes=scratch, compiler_params=_params(sem))(*args)
        return outs, None
    arrs, gather, cols = ride
    n, n_in, n_out, n_scr = len(arrs), len(in_specs), len(out_specs), len(scratch)
    x_shape, x_sems = _exchange_shapes(arrs, gather, cols)

    def riding(*refs):
        ins, x_ins = refs[:n_in], refs[n_in:n_in + n]
        outs = refs[n_in + n:n_in + n + n_out]
        x_outs = refs[n_in + n + n_out:n_in + 2 * n + n_out]
        scr = refs[n_in + 2 * n + n_out:n_in + 2 * n + n_out + n_scr]
        sems = refs[n_in + 2 * n + n_out + n_scr:]
        def at(step):
            return functools.reduce(jnp.logical_and, [pl.program_id(a) == v for a, v in enumerate(step)])

        @pl.when(at((0,) * len(grid)))
        def _():
            _exchange_start(x_ins, x_outs, sems, gather, cols)

        @pl.when(at((grid[0] // 2,) + (0,) * (len(grid) - 2) + (grid[-1] // 2,)))
        def _():
            _exchange_pass_on(x_ins, x_outs, sems, gather, cols, (0, 1))

        @pl.when(at((grid[0] // 2,) + (0,) * (len(grid) - 2) + (3 * grid[-1] // 4,)))
        def _():
            _exchange_pass_on(x_ins, x_outs, sems, gather, cols, (2,))

        body(*ins, *outs, *scr)

        @pl.when(at(tuple(g - 1 for g in grid)))
        def _():
            _exchange_wait(x_ins, x_outs, sems, gather, cols)

    res = pl.pallas_call(
        riding, name=name, grid=grid, in_specs=list(in_specs) + [ANY] * n, out_specs=list(out_specs) + [ANY] * n,
        out_shape=list(out_shape) + x_shape, scratch_shapes=list(scratch) + x_sems,
        compiler_params=_params(("arbitrary",) * len(grid)))(*args, *arrs)
    return res[:n_out], res[n_out:]


def _my_block():
    return (4 * lax.axis_index("x") + 2 * lax.axis_index("y") + lax.axis_index("c")).astype(jnp.int32).reshape(1)


def _proj_in_gather(x, g, w_shard):
    t, k = x.shape
    cs = w_shard.shape[1]
    tm = min(MM_ROWS, t)
    ni = t // tm
    arrival = (0, 1, 2, 4, 3, 5, 6, 7)

    def mask_at(s):
        return jnp.where(s == 3, 4, jnp.where(s == 4, 3, s))

    def body(me_ref, x_ref, g_ref, w_hbm, o_ref, all_hbm, n_hbm, w_vmem, n_vmem, send_sems, recv_sems, loc_sems,
             load_sems, n_sem):
        s, i = pl.program_id(0), pl.program_id(1)
        cps = _gather_copies([w_hbm], [all_hbm], (send_sems, recv_sems, loc_sems), (cs,))
        by_mask = {0: cps["local"][0], 1: cps["from_core"][0]}
        for t_chip, m in enumerate(OTHER_CHIPS):
            by_mask[m] = cps["from_chips"][t_chip]
            by_mask[m + 1] = cps["from_core"][1 + t_chip]
        arrived = [by_mask[m] for m in arrival]

        def load(step):
            src = w_hbm if step == 0 else _window(all_hbm, jnp.bitwise_xor(me_ref[0], arrival[step]), cs)
            return pltpu.make_async_copy(src, w_vmem.at[step % 2], load_sems.at[step % 2])

        @pl.when(jnp.logical_and(s == 0, i == 0))
        def _():
            for cp in cps["local"] + cps["to_chips"] + cps["to_core"]:
                cp.start()
            load(0).start()

        for step, mask in enumerate(arrival):
            @pl.when(jnp.logical_and(s == step, i == 0))
            def _(step=step):
                load(step).wait()

            if step + 1 < N_DEV:
                @pl.when(jnp.logical_and(s == step, i == min(1, ni - 1)))
                def _(step=step):
                    arrived[step + 1].wait_recv()
                    if arrival[step + 1] in OTHER_CHIPS:
                        cps["pass_on"][OTHER_CHIPS.index(arrival[step + 1])].start()
                    load(step + 1).start()

        @pl.when(s == 0)
        def _():
            xf = x_ref[...]
            r = lax.rsqrt(jnp.mean(xf * xf, axis=-1, keepdims=True) + RMS_EPS)
            n_vmem[i] = (xf * r * g_ref[...]).astype(BF16)
            keep = pltpu.make_async_copy(n_vmem.at[i], n_hbm.at[pl.ds(pl.multiple_of(i * tm, tm), tm), :], n_sem)
            keep.start()
            keep.wait()

        o_ref[...] = lax.dot_general(n_vmem[i], w_vmem[s % 2], NN, preferred_element_type=F32).astype(BF16)

        @pl.when(jnp.logical_and(s == N_DEV - 1, i == ni - 1))
        def _():
            cps["local"][0].wait()
            for cp in cps["to_chips"] + cps["to_core"] + cps["pass_on"]:
                cp.wait_send()

    return pl.pallas_call(
        body, name="proj_in",
        grid_spec=pltpu.PrefetchScalarGridSpec(
            num_scalar_prefetch=1, grid=(N_DEV, ni),
            in_specs=[pl.BlockSpec((tm, k), lambda s, i, me: (jnp.where(s == 0, i, 0), 0)),
                      pl.BlockSpec((1, k), lambda s, i, me: (0, 0)), ANY],
            out_specs=[pl.BlockSpec((tm, cs), lambda s, i, me: (i, jnp.bitwise_xor(me[0], mask_at(s)))), ANY, ANY],
            scratch_shapes=[pltpu.VMEM((2, k, cs), BF16), pltpu.VMEM((ni, tm, k), BF16),
                            pltpu.SemaphoreType.DMA((N_DEV - 1,)), pltpu.SemaphoreType.DMA((N_DEV - 1,)),
                            pltpu.SemaphoreType.DMA((1,)), pltpu.SemaphoreType.DMA((2,)), pltpu.SemaphoreType.DMA]),
        out_shape=[jax.ShapeDtypeStruct((t, N_DEV * cs), BF16), jax.ShapeDtypeStruct((k, N_DEV * cs), BF16),
                   jax.ShapeDtypeStruct((t, k), BF16)],
        compiler_params=_params(("arbitrary", "arbitrary")),
    )(_my_block(), x, g, w_shard)


def _gw_in_scatter(a, g):
    t, k = a.shape
    cs = g.shape[1] // N_DEV
    tm = min(MM_ROWS, t)
    nr = t // tm
    n_chip = N_DEV // 2
    chips = (6, 4, 2, 0)

    def body(me_ref, a_ref, g_ref, out_hbm, acc, stage, other, core_send, core_recv, chip_send, chip_recv, loc_sem):
        s, r = pl.program_id(0), pl.program_id(1)
        x, y, c = lax.axis_index("x"), lax.axis_index("y"), lax.axis_index("c")
        my_chip = 2 * x + y
        part = lax.dot_general(a_ref[...], g_ref[...], TN, preferred_element_type=F32)

        def to_core(m):
            return pltpu.make_async_remote_copy(src_ref=stage.at[0], dst_ref=other.at[m], send_sem=core_send.at[m],
                                                recv_sem=core_recv.at[m], device_id=(x, y, 1 - c),
                                                device_id_type=pl.DeviceIdType.MESH)

        def to_chip(m, landed):
            mask = chips[m]
            there = (1 - x if mask & 4 else x, 1 - y if mask & 2 else y, c)
            slot = (2 * there[0] + there[1]) if landed else my_chip
            return pltpu.make_async_remote_copy(src_ref=stage.at[1], dst_ref=out_hbm.at[slot], send_sem=chip_send.at[m],
                                                recv_sem=chip_recv.at[m], device_id=there,
                                                device_id_type=pl.DeviceIdType.MESH)

        local = pltpu.make_async_copy(stage.at[1], out_hbm.at[my_chip], loc_sem)

        @pl.when(r == 0)
        def _():
            acc[...] = part

        @pl.when(r > 0)
        def _():
            acc[...] += part

        for step in range(N_DEV):
            m = step // 2

            @pl.when(jnp.logical_and(s == step, r == nr - 1))
            def _(step=step, m=m):
                if step % 2 == 0:
                    if m > 0:
                        to_core(m - 1).wait_send()
                    stage[0] = acc[...].astype(BF16)
                    to_core(m).start()
                else:
                    if m > 0:
                        to_chip(m - 1, False).wait_send()
                    to_core(m).wait_recv()
                    stage[1] = (acc[...] + other[m].astype(F32)).astype(BF16)
                    if m < n_chip - 1:
                        to_chip(m, False).start()
                    else:
                        local.start()
                        to_core(m).wait_send()
                        local.wait()
                        for mm in range(n_chip - 1):
                            to_chip(mm, True).wait_recv()

    return pl.pallas_call(
        body, name="gw_in",
        grid_spec=pltpu.PrefetchScalarGridSpec(
            num_scalar_prefetch=1, grid=(N_DEV, nr),
            in_specs=[pl.BlockSpec((tm, k), lambda s, r, me: (r, 0)),
                      pl.BlockSpec((tm, cs), lambda s, r, me: (r, jnp.bitwise_xor(me[0], N_DEV - 1 - s)))],
            out_specs=ANY,
            scratch_shapes=[pltpu.VMEM((k, cs), F32), pltpu.VMEM((2, k, cs), BF16), pltpu.VMEM((n_chip, k, cs), BF16),
                            pltpu.SemaphoreType.DMA((n_chip,)), pltpu.SemaphoreType.DMA((n_chip,)),
                            pltpu.SemaphoreType.DMA((n_chip - 1,)), pltpu.SemaphoreType.DMA((n_chip - 1,)),
                            pltpu.SemaphoreType.DMA]),
        out_shape=jax.ShapeDtypeStruct((n_chip, k, cs), BF16),
        compiler_params=_params(("arbitrary", "arbitrary")),
    )(_my_block(), a, g)


SMALL_ROWS = 8


def _allreduce_small(parts, loss_part):
    n, d = len(parts), parts[0].shape[1]

    def body(*refs):
        part_refs, loss_ref, o_ref = refs[:n], refs[n], refs[n + 1]
        mine_ref, all_ref, send_sems, recv_sems = refs[n + 2:]
        me, peers = _peers()
        mine_ref[...] = jnp.zeros_like(mine_ref)
        for i, p_ref in enumerate(part_refs):
            mine_ref[i:i + 1, :] = p_ref[...]
        mine_ref[SMALL_ROWS - 1:SMALL_ROWS, 0:LANES] = loss_ref[0:1, :]
        all_ref[me] = mine_ref[...]
        for k, (dev, idx) in enumerate(peers):
            pltpu.make_async_remote_copy(src_ref=mine_ref, dst_ref=all_ref.at[me], send_sem=send_sems.at[k],
                                         recv_sem=recv_sems.at[k], device_id=dev,
                                         device_id_type=pl.DeviceIdType.MESH).start()
        for k, (dev, idx) in enumerate(peers):
            cp = pltpu.make_async_remote_copy(src_ref=mine_ref, dst_ref=all_ref.at[idx], send_sem=send_sems.at[k],
                                              recv_sem=recv_sems.at[k], device_id=dev,
                                              device_id_type=pl.DeviceIdType.MESH)
            cp.wait_send()
            cp.wait_recv()
        tot = all_ref[0]
        for dvc in range(1, N_DEV):
            tot = tot + all_ref[dvc]
        o_ref[...] = tot

    return pl.pallas_call(
        body, name="allreduce_small", in_specs=[VMEM] * (n + 1), out_specs=VMEM,
        out_shape=jax.ShapeDtypeStruct((SMALL_ROWS, d), F32),
        scratch_shapes=[pltpu.VMEM((SMALL_ROWS, d), F32), pltpu.VMEM((N_DEV, SMALL_ROWS, d), F32),
                        pltpu.SemaphoreType.DMA((N_DEV - 1,)), pltpu.SemaphoreType.DMA((N_DEV - 1,))],
    )(*parts, loss_part)


def _adam_math(g, w, m, v):
    m_new = ADAM_B1 * m + (1.0 - ADAM_B1) * g
    v_new = ADAM_B2 * v + (1.0 - ADAM_B2) * (g * g)
    m_hat = m_new / (1.0 - ADAM_B1 ** ADAM_STEP)
    v_hat = v_new / (1.0 - ADAM_B2 ** ADAM_STEP)
    delta = -ADAM_LR * (m_hat / (jnp.sqrt(v_hat) + ADAM_EPS) + ADAM_WD * w)
    return delta, m_new, v_new


def _adam(name, pieces, w, m, v):
    r, c = w.shape
    n_piece, _, cp = pieces.shape
    tr = r
    for cand in (256, 176, 128, 64):
        if r % cand == 0 and r > cand:
            tr = cand
            break

    def body(p_ref, w_ref, m_ref, v_ref, g_ref, d_ref, mo_ref, vo_ref):
        g = p_ref[0, :, 0:c].astype(F32)
        for j in range(1, n_piece):
            g = g + p_ref[j, :, 0:c].astype(F32)
        delta, m_new, v_new = _adam_math(g, w_ref[...], m_ref[...], v_ref[...])
        g_ref[...] = g
        d_ref[...] = delta
        mo_ref[...] = m_new
        vo_ref[...] = v_new

    blk = pl.BlockSpec((tr, c), lambda i: (i, 0))
    osh = jax.ShapeDtypeStruct((r, c), F32)
    return pl.pallas_call(
        body, name=name, grid=(r // tr,),
        in_specs=[pl.BlockSpec((n_piece, tr, cp), lambda i: (0, i, 0)), blk, blk, blk],
        out_specs=[blk, blk, blk, blk], out_shape=[osh, osh, osh, osh],
        compiler_params=_params(("parallel",)),
    )(pieces, w, m, v)


def _adam_small(g_all, ws, ms, vs):
    n = len(ws)

    def body(*refs):
        g_ref, ins, outs = refs[0], refs[1:1 + 3 * n], refs[1 + 3 * n:]
        for i in range(n):
            g = g_ref[i:i + 1, :]
            delta, m_new, v_new = _adam_math(g, ins[i][...], ins[n + i][...], ins[2 * n + i][...])
            for kind, val in enumerate((g, delta, m_new, v_new)):
                outs[kind * n + i][...] = val

    osh = jax.ShapeDtypeStruct(ws[0].shape, F32)
    res = pl.pallas_call(body, name="adam_small", in_specs=[VMEM] * (1 + 3 * n), out_specs=[VMEM] * (4 * n),
                         out_shape=[osh] * (4 * n))(g_all, *ws, *ms, *vs)
    return res[:n], res[n:2 * n], res[2 * n:3 * n], res[3 * n:]


def _local_step(x, mem, pos, tgt, gains, w_in_shard, shards, batch):
    g_mix, g_mem_q, g_mem_kv, g_ffn, g_final = gains
    t, d = x.shape
    s = t // batch
    n_mem = mem.shape[0] // batch
    n_sh = N_DEV
    width = shards[0].shape[0]
    nb = width // LANES

    lane = np.arange(LANES) % HEAD_DIM
    sel_lo = (lane < ROPE_HALF).astype(np.float32)[None, :]
    sel_hi = ((lane >= ROPE_HALF) & (lane < 2 * ROPE_HALF)).astype(np.float32)[None, :]
    freqs = np.float32(ROPE_THETA) ** (-np.arange(ROPE_HALF, dtype=np.float32) / np.float32(ROPE_HALF))
    inv_freq = np.where(lane < 2 * ROPE_HALF, freqs[lane % ROPE_HALF], 0.0).astype(np.float32)[None, :]
    cos_t, sin_a, sin_b = _rope_tables(pos, jnp.asarray(inv_freq), jnp.asarray(sel_lo), jnp.asarray(sel_hi))
    bias = _dilated_bias_tiles(s)

    proj, w_in, n1 = _proj_in_gather(x, g_mix, w_in_shard)
    qk_a = _rope_apply("rope_fwd", [proj], 2 * width, cos_t, sin_a, sin_b, 1.0)
    cs_up = shards[0].shape[1]
    (o_a, lse_a), (w_up_a, w_up_b, w_out, w_q, w_kv, w_o, w_fd) = _da_fwd(
        qk_a, proj, 2 * nb, bias, batch, s,
        ride=(shards[:6] + shards[8:], TWO_LEVEL, (cs_up, cs_up, 0, 0, 0, cs_up, 0)))
    (o_b, tot_b), (w_fg, w_fu) = _sb_fwd(proj, 3 * nb, 4 * nb, 5 * nb, batch, s, ride=(shards[6:8], True, (0, 0)))
    w_out = w_out.reshape(d, d)
    w_q = w_q.reshape(d, -1)
    w_kv = w_kv.reshape(d, -1)
    w_fd = w_fd.reshape(-1, d)
    w_fg = w_fg.reshape(-1, d)
    w_fu = w_fu.reshape(-1, d)
    ua, ub, mixed, n2, h1, q_m = _mixer_fwd(o_a, o_b, w_up_a, w_up_b, proj, 6 * nb, w_out, x, g_mem_q, w_q)
    mem_n = _rms_fwd("norm_mem_kv", mem, g_mem_kv)
    kv_m = _mm_w("mem_kv", mem_n, w_kv, BF16)
    o_m = _mem_fwd(q_m, kv_m, batch, s, n_mem)
    h2, n3 = _mm_res_norm("mem_out", o_m, w_o, h1, g_ffn)
    hg, hu, act = _ffn_up(n3, w_fg, w_fu)
    loss_part, dh3, dh3_b, dg_final = _loss_head(act, w_fd, h2, tgt, g_final.reshape(1, d))

    dhg, dhu, dh2, dh2_b, dg_ffn, do_m = _ffn_bwd(dh3_b, w_fd, w_fg, w_fu, hg, hu, h2, g_ffn, dh3, w_o)
    gw_fd = _wgrad("gw_ffn_down", act, dh3_b)
    gw_fg = _wgrad("gw_ffn_gate", dhg, n3)
    gw_fu = _wgrad("gw_ffn_up", dhu, n3)

    gw_o = _wgrad("gw_mem_o", o_m, dh2_b)
    dq_m, dkv_m = _mem_bwd(q_m, kv_m, do_m, batch, s, n_mem)
    gw_q = _wgrad("gw_mem_q", n2, dq_m)
    gw_kv = _wgrad("gw_mem_kv", mem_n, dkv_m)
    (dg_mem_kv,) = _rms_bwd("norm_mem_kv_bwd", (dkv_m, w_kv, NT), mem, g_mem_kv, None, ())
    dh1, dh1_b, dg_mem_q = _rms_bwd("norm_mem_q_bwd", (dq_m, w_q, NT), h1, g_mem_q, dh2, ("f32", "bf16"))

    gw_out = _wgrad("gw_out", mixed, dh1_b)
    dua, dub, dgates, do_a, do_b = _mixer_bwd(dh1_b, w_out, ua, ub, proj, 6 * nb, w_up_a, w_up_b)
    gw_ua = _wgrad("gw_up_a", o_a, dua)
    gw_ub = _wgrad("gw_up_b", o_b, dub)
    (dq_ar, dk_ar, dv_a), (p_fg, p_fd) = _da_bwd(
        qk_a, proj, 2 * nb, bias, o_a, lse_a, do_a, batch, s,
        ride=([gw_fg.reshape(n_sh, -1, d), gw_fd.reshape(n_sh, -1, d)], False, (0, 0)))
    mid = [gw_ua, gw_ub, gw_out.reshape(n_sh, -1, d), gw_q.reshape(n_sh, -1, gw_q.shape[1]),
           gw_kv.reshape(n_sh, -1, gw_kv.shape[1]), gw_o, gw_fu.reshape(n_sh, -1, d)]
    (dq_b, dk_b, dv_b), (*p_mid, p_fu) = _sb_bwd(proj, 3 * nb, 4 * nb, 5 * nb, tot_b, do_b, batch, s,
                                                 ride=(mid, False, (cs_up, cs_up, 0, 0, 0, cs_up, 0)))
    p_ffn = [p_fg, p_fu, p_fd]
    dproj = _rope_apply("rope_bwd", [dq_ar, dk_ar], width, cos_t, sin_a, sin_b, -1.0,
                        tail=(dv_a, dq_b, dk_b, dv_b, dgates))
    grad_x, dg_mix = _rms_bwd("proj_in_bwd", (dproj, w_in, NT), x, g_mix, dh1, ("f32",))
    p_in = _gw_in_scatter(n1, dproj)
    return loss_part, grad_x, [p_in] + list(p_mid) + p_ffn, (dg_mix, dg_mem_q, dg_mem_kv, dg_ffn, dg_final)


WEIGHTS =("w_in", "w_up_a", "w_up_b", "w_out", "w_q_mem", "w_kv_mem", "w_o_mem", "w_ffn_gate", "w_ffn_up", "w_ffn_down")
GAINS = ("g_mix", "g_mem_q", "g_mem_kv", "g_ffn", "g_final")
ORDER = ("g_mix", "w_in", "w_up_a", "w_up_b", "w_out", "g_mem_q", "g_mem_kv", "w_q_mem", "w_kv_mem", "w_o_mem", "g_ffn",
         "w_ffn_gate", "w_ffn_up", "w_ffn_down", "g_final")


def kernel(x, mem, positions, g_mix, w_in, w_up_a, w_up_b, w_out, g_mem_q, g_mem_kv, w_q_mem, w_kv_mem, w_o_mem, g_ffn, w_ffn_gate, w_ffn_up, w_ffn_down, g_final, loss_target, m_g_mix, m_w_in, m_w_up_a, m_w_up_b, m_w_out, m_g_mem_q, m_g_mem_kv, m_w_q_mem, m_w_kv_mem, m_w_o_mem, m_g_ffn, m_w_ffn_gate, m_w_ffn_up, m_w_ffn_down, m_g_final, v_g_mix, v_w_in, v_w_up_a, v_w_up_b, v_w_out, v_g_mem_q, v_g_mem_kv, v_w_q_mem, v_w_kv_mem, v_w_o_mem, v_g_ffn, v_w_ffn_gate, v_w_ffn_up, v_w_ffn_down, v_g_final):
    given = dict(locals())
    batch, s, d = x.shape
    t = batch * s
    flipped = ("w_ffn_gate", "w_ffn_up")

    def view(a, n):
        a = a.reshape(a.shape[-2:])
        return a.T if n in flipped else a

    def unview(a, n):
        return (a.T if n in flipped else a).reshape(given[n].shape)

    shard = {n: view(given[n], n) for n in WEIGHTS}
    gains = [given[n].reshape(1, d) for n in GAINS]

    pad = (-shard["w_ffn_down"].shape[0]) % LANES
    cast = _cast_weights([shard[n] for n in WEIGHTS], [pad if n in flipped + ("w_ffn_down",) else 0 for n in WEIGHTS])
    loss_part, grad_x, pieces, dgains = _local_step(
        x.reshape(t, d), mem.reshape(-1, d), positions.reshape(t, 1), loss_target.reshape(t, d), gains, cast[0],
        cast[1:], batch)

    grad, delta, new_m, new_v = {}, {}, {}, {}
    for n, p in zip(WEIGHTS, pieces):
        outs = _adam("adam_" + n, p, shard[n], view(given["m_" + n], n), view(given["v_" + n], n))
        grad[n], delta[n], new_m[n], new_v[n] = [unview(o, n) for o in outs]

    g_all = _allreduce_small(list(dgains), loss_part)
    small = _adam_small(g_all, gains, [given["m_" + n].reshape(1, d) for n in GAINS],
                        [given["v_" + n].reshape(1, d) for n in GAINS])
    for out, vals in zip((grad, delta, new_m, new_v), small):
        for n, val in zip(GAINS, vals):
            out[n] = val.reshape(given[n].shape)

    loss = g_all[SMALL_ROWS - 1, 0]
    return (loss, grad_x.reshape(x.shape), *[grad[n] for n in ORDER], *[delta[n] for n in ORDER],
            *[new_m[n] for n in ORDER], *[new_v[n] for n in ORDER])
```

```python
import functools
import math

import jax
import jax.numpy as jnp
import numpy as np
from jax import lax
from jax.experimental import pallas as pl
from jax.experimental.pallas import tpu as pltpu

F32 = jnp.float32
BF16 = jnp.bfloat16

N_DEV = 8
HEAD_DIM = 64
MEM_HEAD_DIM = 128
N_HEADS_MEM = 4
BLOCK = 128
DIL_PATTERNS = ((128, 1), (512, 4), (2048, 16))
ROPE_THETA = 500000.0
ROPE_HALF = 8
RMS_EPS = 1e-6
ADAM_LR, ADAM_B1, ADAM_B2, ADAM_EPS, ADAM_WD, ADAM_STEP = 0.001, 0.9, 0.999, 1e-08, 0.01, 10
NEG = -1e30
ROW_TILE = 512
LANES = 128

ANY = pl.BlockSpec(memory_space=pl.ANY)
VMEM = pl.BlockSpec(memory_space=pltpu.VMEM)
NN = (((1,), (0,)), ((), ()))
NT = (((1,), (1,)), ((), ()))
TN = (((0,), (0,)), ((), ()))


def _params(sem):
    return pltpu.CompilerParams(dimension_semantics=sem)


def _mm(name, a, b, *, grid, a_spec, b_spec, o_shape, o_spec, dims, out_dtype):
    def body(a_ref, b_ref, o_ref):
        o_ref[...] = lax.dot_general(a_ref[...], b_ref[...], dims, preferred_element_type=F32).astype(out_dtype)

    return pl.pallas_call(
        body, name=name, grid=grid, in_specs=[a_spec, b_spec],
        out_specs=o_spec, out_shape=jax.ShapeDtypeStruct(o_shape, out_dtype),
        compiler_params=_params(("parallel",) * len(grid)),
    )(a, b)


def _rms_fwd(name, x, g):
    t, d = x.shape
    tm = min(ROW_TILE, t)

    def body(x_ref, g_ref, o_ref):
        xf = x_ref[...]
        r = lax.rsqrt(jnp.mean(xf * xf, axis=-1, keepdims=True) + RMS_EPS)
        o_ref[...] = (xf * r * g_ref[...]).astype(BF16)

    return pl.pallas_call(
        body, name=name, grid=(t // tm,),
        in_specs=[pl.BlockSpec((tm, d), lambda i: (i, 0)), pl.BlockSpec((1, d), lambda i: (0, 0))],
        out_specs=pl.BlockSpec((tm, d), lambda i: (i, 0)), out_shape=jax.ShapeDtypeStruct((t, d), BF16),
        compiler_params=_params(("parallel",)),
    )(x, g)


def _rms_bwd_rows(dnf, xf, gv, res):
    r = lax.rsqrt(jnp.mean(xf * xf, axis=-1, keepdims=True) + RMS_EPS)
    xh = xf * r
    dxh = dnf * gv
    dx = r * (dxh - xh * jnp.mean(dxh * xh, axis=-1, keepdims=True))
    if res is not None:
        dx = dx + res
    return dx, jnp.sum(dnf * xh, axis=0, keepdims=True)


def _rms_bwd(name, dn, x, g, dres, want):
    t, d = x.shape
    has_res = dres is not None
    lhs = list(dn) if isinstance(dn, tuple) else [dn]
    n_lhs = len(lhs[:2])
    tm = min(MM_ROWS if lhs[0].shape[1] <= d else ROW_TILE, t)

    def body(*refs):
        x_ref, g_ref = refs[n_lhs], refs[n_lhs + 1]
        r_ref = refs[n_lhs + 2] if has_res else None
        dx_refs, dg_ref = refs[-1 - len(want):-1], refs[-1]
        if n_lhs == 2:
            dnf = lax.dot_general(refs[0][...], refs[1][...], lhs[2], preferred_element_type=F32)
        else:
            dnf = refs[0][...].astype(F32)
        dx, dg = _rms_bwd_rows(dnf, x_ref[...], g_ref[...], r_ref[...] if has_res else None)
        for kind, dx_ref in zip(want, dx_refs):
            dx_ref[...] = dx.astype(F32 if kind == "f32" else BF16)

        @pl.when(pl.program_id(0) == 0)
        def _():
            dg_ref[...] = jnp.zeros_like(dg_ref)

        dg_ref[...] += dg

    row = pl.BlockSpec((tm, d), lambda i: (i, 0))
    vec = pl.BlockSpec((1, d), lambda i: (0, 0))
    if n_lhs == 2:
        first = [pl.BlockSpec((tm, lhs[0].shape[1]), lambda i: (i, 0)), pl.BlockSpec(lhs[1].shape, lambda i: (0, 0))]
    else:
        first = [row]
    return pl.pallas_call(
        body, name=name, grid=(t // tm,),
        in_specs=first + [row, vec] + ([row] if has_res else []),
        out_specs=[row] * len(want) + [vec],
        out_shape=[jax.ShapeDtypeStruct((t, d), F32 if kind == "f32" else BF16) for kind in want]
        + [jax.ShapeDtypeStruct((1, d), F32)],
        compiler_params=_params(("arbitrary",)),
    )(*(lhs[:2] + [x, g] + ([dres] if has_res else [])))


def _loss_head(a, w, res, tgt, g):
    t, d = res.shape
    k = a.shape[1]
    tm = min(ROW_TILE, t)

    def body(a_ref, w_ref, r_ref, t_ref, g_ref, loss_ref, dh_ref, dhb_ref, dg_ref):
        xf = lax.dot_general(a_ref[...], w_ref[...], NN, preferred_element_type=F32) + r_ref[...]
        gv = g_ref[...]
        r = lax.rsqrt(jnp.mean(xf * xf, axis=-1, keepdims=True) + RMS_EPS)
        xh = xf * r
        e = xh * gv - t_ref[...]
        dy = e * (1.0 / d)
        dxh = dy * gv
        dh = r * (dxh - xh * jnp.mean(dxh * xh, axis=-1, keepdims=True))
        dh_ref[...] = dh
        dhb_ref[...] = dh.astype(BF16)

        @pl.when(pl.program_id(0) == 0)
        def _():
            dg_ref[...] = jnp.zeros_like(dg_ref)
            loss_ref[...] = jnp.zeros_like(loss_ref)

        dg_ref[...] += jnp.sum(dy * xh, axis=0, keepdims=True)
        part = jnp.sum(jnp.sum(e * e, axis=1, keepdims=True), axis=0, keepdims=True) * (0.5 / d)
        loss_ref[...] += jnp.broadcast_to(part, loss_ref.shape)

    row = pl.BlockSpec((tm, d), lambda i: (i, 0))
    vec = pl.BlockSpec((1, d), lambda i: (0, 0))
    return pl.pallas_call(
        body, name="loss_head", grid=(t // tm,),
        in_specs=[pl.BlockSpec((tm, k), lambda i: (i, 0)), pl.BlockSpec((k, d), lambda i: (0, 0)), row, row, vec],
        out_specs=[pl.BlockSpec((8, LANES), lambda i: (0, 0)), row, row, vec],
        out_shape=[jax.ShapeDtypeStruct((8, LANES), F32), jax.ShapeDtypeStruct((t, d), F32),
                   jax.ShapeDtypeStruct((t, d), BF16), jax.ShapeDtypeStruct((1, d), F32)],
        compiler_params=_params(("arbitrary",)),
    )(a, w, res, tgt, g)


def _rope_tables(pos, inv_freq, sel_lo, sel_hi):
    t = pos.shape[0]
    tm = min(MM_ROWS, t)

    def body(p_ref, f_ref, lo_ref, hi_ref, c_ref, sa_ref, sb_ref):
        ang = p_ref[...].astype(F32) * f_ref[...]
        rot = lo_ref[...] + hi_ref[...]
        cs, sn = jnp.cos(ang), jnp.sin(ang)
        c_ref[...] = cs * rot + (1.0 - rot)
        sa_ref[...] = -sn * lo_ref[...]
        sb_ref[...] = sn * hi_ref[...]

    vec = pl.BlockSpec((1, LANES), lambda i: (0, 0))
    row = pl.BlockSpec((tm, LANES), lambda i: (i, 0))
    return pl.pallas_call(
        body, name="rope_tables", grid=(t // tm,),
        in_specs=[pl.BlockSpec((tm, 1), lambda i: (i, 0)), vec, vec, vec],
        out_specs=[row, row, row], out_shape=[jax.ShapeDtypeStruct((t, LANES), F32)] * 3,
        compiler_params=_params(("parallel",)),
    )(pos, inv_freq, sel_lo, sel_hi)


def _rope_apply(name, srcs, width, cos_t, sin_a, sin_b, sign, tail=()):
    t = srcs[0].shape[0]
    tm = min(MM_ROWS, t)
    n_cols = width // LANES
    n_src = len(srcs)

    def body(*refs):
        x_refs, tail_refs = refs[:n_src], refs[n_src:n_src + len(tail)]
        c_ref, sa_ref, sb_ref, o_ref = refs[n_src + len(tail):]
        cs, sa, sb = c_ref[...], sign * sa_ref[...], sign * sb_ref[...]
        for a, x_ref in enumerate(x_refs):
            for c in range(n_cols):
                xf = x_ref[:, c * LANES:(c + 1) * LANES].astype(F32)
                up = pltpu.roll(xf, LANES - ROPE_HALF, 1)
                dn = pltpu.roll(xf, ROPE_HALF, 1)
                o_ref[:, a * width + c * LANES:a * width + (c + 1) * LANES] = (xf * cs + up * sa + dn * sb).astype(BF16)
        col = n_src * width
        for t_ref in tail_refs:
            o_ref[:, col:col + t_ref.shape[1]] = t_ref[...]
            col += t_ref.shape[1]

    wide = n_src * width + sum(a.shape[1] for a in tail)
    tab = pl.BlockSpec((tm, LANES), lambda i: (i, 0))
    return pl.pallas_call(
        body, name=name, grid=(t // tm,),
        in_specs=[pl.BlockSpec((tm, width), lambda i: (i, 0))] * n_src
        + [pl.BlockSpec((tm, a.shape[1]), lambda i: (i, 0)) for a in tail] + [tab, tab, tab],
        out_specs=pl.BlockSpec((tm, wide), lambda i: (i, 0)),
        out_shape=jax.ShapeDtypeStruct((t, wide), BF16),
        compiler_params=_params(("parallel",)),
    )(*srcs, *tail, cos_t, sin_a, sin_b)


DA_T = 256
MIX_STREAMS = 4
SB_BWD_STREAMS = 2


def _lane_lo():
    return lax.broadcasted_iota(jnp.int32, (BLOCK, LANES), 1) < HEAD_DIM


def _dilated_bias_tiles(s):
    n = s // DA_T
    dist = (np.arange(n)[:, None, None] * DA_T + np.arange(DA_T)[None, :, None] - np.arange(DA_T)[None, None, :])
    cnt = np.zeros(dist.shape, np.float32)
    for window, dil in DIL_PATTERNS:
        cnt += ((dist >= 0) & (dist % dil == 0) & (dist <= window)).astype(np.float32)
    return jnp.asarray(np.where(cnt > 0, np.log(np.maximum(cnt, 1.0)), NEG).astype(np.float32))


def _stack_heads(x, lo):
    zero = jnp.zeros_like(x)
    return jnp.concatenate([jnp.where(lo, x, zero), jnp.where(lo, zero, x)], axis=0)


def _da_fwd(qk, proj, v_col0, bias, batch, s, ride=None, streams=MIX_STREAMS):
    t = qk.shape[0]
    nq = s // DA_T
    n_pairs = 4
    ns = streams
    wide = ns * LANES
    scale = HEAD_DIM ** -0.5

    def body(q_ref, k_ref, v_ref, b_ref, o_ref, lse_ref, acc_ref, m_ref, l_ref):
        i = pl.program_id(2)
        lo = lax.broadcasted_iota(jnp.int32, (DA_T, LANES), 1) < HEAD_DIM
        ones = jnp.ones((DA_T, LANES), BF16)
        acc_ref[...] = jnp.zeros_like(acc_ref)
        m_ref[...] = jnp.full(m_ref.shape, NEG, F32)
        l_ref[...] = jnp.zeros_like(l_ref)
        qqs = [_stack_heads(q_ref[:, st * LANES:(st + 1) * LANES] * scale, lo) for st in range(ns)]

        def scores(st, rows, bias2):
            k = k_ref[rows, st * LANES:(st + 1) * LANES]
            return lax.dot_general(qqs[st], k, NT, preferred_element_type=F32) + bias2

        def softmax(st, sc):
            m_old = m_ref[st]
            m_new = jnp.maximum(m_old, jnp.broadcast_to(jnp.max(sc, axis=1, keepdims=True), m_old.shape))
            m_ref[st] = m_new
            return jnp.exp(sc - jnp.concatenate([m_new, m_new], axis=1)).astype(BF16), jnp.exp(m_old - m_new)

        def values(st, rows, p, alpha):
            v = v_ref[rows, st * LANES:(st + 1) * LANES]
            vz = jnp.zeros_like(v)
            l_ref[st] = alpha * l_ref[st] + lax.dot_general(p, ones, NN, preferred_element_type=F32)
            pv = (lax.dot_general(p[:DA_T], jnp.where(lo, v, vz), NN, preferred_element_type=F32)
                  + lax.dot_general(p[DA_T:], jnp.where(lo, vz, v), NN, preferred_element_type=F32))
            acc_ref[st] = acc_ref[st] * jnp.where(lo, alpha[:DA_T], alpha[DA_T:]) + pv

        def trip(dlt, carry):
            rows = pl.ds(pl.multiple_of((i - dlt) * DA_T, DA_T), DA_T)
            bias_t = b_ref[dlt]
            bias2 = jnp.concatenate([bias_t, bias_t], axis=0)
            scs = [scores(st, rows, bias2) for st in range(ns)]
            pas = [softmax(st, scs[st]) for st in range(ns)]
            for st in range(ns):
                values(st, rows, *pas[st])
            return carry

        lax.fori_loop(0, i + 1, trip, 0)
        for st in range(ns):
            cols = slice(st * LANES, (st + 1) * LANES)
            l_t = l_ref[st]
            o_ref[:, cols] = (acc_ref[st] / jnp.where(lo, l_t[:DA_T], l_t[DA_T:])).astype(BF16)
            lse = m_ref[st] + jnp.log(l_t)
            lse_ref[:, cols] = jnp.where(lo, lse[:DA_T], lse[DA_T:])

    blk = pl.BlockSpec((DA_T, wide), lambda b, h, i: (b * nq + i, h))
    return _call(
        body, name="attn_a_fwd", grid=(batch, n_pairs // ns, nq),
        in_specs=[blk,
                  pl.BlockSpec((s, wide), lambda b, h, i: (b, n_pairs // ns + h)),
                  pl.BlockSpec((s, wide), lambda b, h, i: (b, v_col0 // ns + h)),
                  pl.BlockSpec((nq, DA_T, DA_T), lambda b, h, i: (0, 0, 0))],
        out_specs=[blk, blk],
        out_shape=[jax.ShapeDtypeStruct((t, n_pairs * LANES), BF16), jax.ShapeDtypeStruct((t, n_pairs * LANES), F32)],
        scratch=[pltpu.VMEM((ns, DA_T, LANES), F32), pltpu.VMEM((ns, 2 * DA_T, LANES), F32),
                 pltpu.VMEM((ns, 2 * DA_T, LANES), F32)],
        sem=("parallel", "parallel", "arbitrary"), args=(qk, qk, proj, bias), ride=ride)


def _da_bwd(qk, proj, v_col0, bias, o, lse, do, batch, s, ride=None, streams=MIX_STREAMS):
    t = qk.shape[0]
    nq = s // DA_T
    n_pairs = 4
    ns = streams
    wide = ns * LANES
    scale = HEAD_DIM ** -0.5

    def body(q_ref, k_ref, v_ref, b_ref, o_ref, lse_ref, do_ref, dq_ref, dk_ref, dv_ref, dk_acc, dv_acc, dq_acc):
        i = pl.program_id(2)
        lo = lax.broadcasted_iota(jnp.int32, (DA_T, LANES), 1) < HEAD_DIM

        @pl.when(i == 0)
        def _():
            dk_acc[...] = jnp.zeros_like(dk_acc)
            dv_acc[...] = jnp.zeros_like(dv_acc)

        dq_acc[...] = jnp.zeros_like(dq_acc)
        qqs, dds, deltas, lses = [], [], [], []
        for st in range(ns):
            cols = slice(st * LANES, (st + 1) * LANES)
            do_ = do_ref[:, cols]
            qqs.append(_stack_heads(q_ref[:, cols] * scale, lo))
            dds.append(_stack_heads(do_, lo))
            prod = do_.astype(F32) * o_ref[:, cols].astype(F32)
            fz = jnp.zeros_like(prod)
            deltas.append(jnp.concatenate([jnp.sum(jnp.where(lo, prod, fz), axis=1, keepdims=True),
                                           jnp.sum(jnp.where(lo, fz, prod), axis=1, keepdims=True)], axis=0))
            lse_t = lse_ref[:, cols]
            lses.append(jnp.concatenate([lse_t[:, 0:1], lse_t[:, HEAD_DIM:HEAD_DIM + 1]], axis=0))

        def products(st, rows, bias2):
            cols = slice(st * LANES, (st + 1) * LANES)
            sc = lax.dot_general(qqs[st], k_ref[rows, cols], NT, preferred_element_type=F32) + bias2
            return sc, lax.dot_general(dds[st], v_ref[rows, cols], NT, preferred_element_type=F32)

        def weights(st, sc, dp):
            p = jnp.exp(sc - lses[st])
            return (p * (dp - deltas[st])).astype(BF16), p.astype(BF16)

        def gradients(st, rows, ds, p):
            cols = slice(st * LANES, (st + 1) * LANES)
            k = k_ref[rows, cols]
            kz = jnp.zeros_like(k)
            dq_acc[st] += (lax.dot_general(ds[:DA_T], jnp.where(lo, k, kz), NN, preferred_element_type=F32)
                           + lax.dot_general(ds[DA_T:], jnp.where(lo, kz, k), NN, preferred_element_type=F32))
            dk_acc[rows, cols] += lax.dot_general(ds, qqs[st], TN, preferred_element_type=F32)
            dv_acc[rows, cols] += lax.dot_general(p, dds[st], TN, preferred_element_type=F32)

        def trip(dlt, carry):
            rows = pl.ds(pl.multiple_of((i - dlt) * DA_T, DA_T), DA_T)
            bias_t = b_ref[dlt]
            bias2 = jnp.concatenate([bias_t, bias_t], axis=0)
            prods = [products(st, rows, bias2) for st in range(ns)]
            wts = [weights(st, *prods[st]) for st in range(ns)]
            for st in range(ns):
                gradients(st, rows, *wts[st])
            return carry

        lax.fori_loop(0, i + 1, trip, 0)
        for st in range(ns):
            dq_ref[:, st * LANES:(st + 1) * LANES] = (dq_acc[st] * scale).astype(BF16)

        @pl.when(i == nq - 1)
        def _():
            dk_ref[...] = dk_acc[...].astype(BF16)
            dv_ref[...] = dv_acc[...].astype(BF16)

    blk = pl.BlockSpec((DA_T, wide), lambda b, h, i: (b * nq + i, h))
    seq = pl.BlockSpec((s, wide), lambda b, h, i: (b, h), pipeline_mode=pl.Buffered(1))
    one = pl.Buffered(1)
    out = jax.ShapeDtypeStruct((t, n_pairs * LANES), BF16)
    return _call(
        body, name="attn_a_bwd", grid=(batch, n_pairs // ns, nq),
        in_specs=[blk,
                  pl.BlockSpec((s, wide), lambda b, h, i: (b, n_pairs // ns + h), pipeline_mode=one),
                  pl.BlockSpec((s, wide), lambda b, h, i: (b, v_col0 // ns + h), pipeline_mode=one),
                  pl.BlockSpec((nq, DA_T, DA_T), lambda b, h, i: (0, 0, 0), pipeline_mode=one),
                  blk, blk, blk],
        out_specs=[blk, seq, seq], out_shape=[out, out, out],
        scratch=[pltpu.VMEM((s, wide), F32), pltpu.VMEM((s, wide), F32), pltpu.VMEM((ns, DA_T, LANES), F32)],
        sem=("parallel", "parallel", "arbitrary"), args=(qk, qk, proj, bias, o, lse, do), ride=ride)


SB_Q = 256


def _sb_consts(after):
    r = lax.broadcasted_iota(jnp.int32, (2 * BLOCK, 2 * BLOCK), 0) % BLOCK
    c = lax.broadcasted_iota(jnp.int32, (2 * BLOCK, 2 * BLOCK), 1)
    tri = (r > c) if after else (r < c)
    return jnp.logical_or(c >= BLOCK, tri).astype(BF16)


def _split(x):
    hi = x.astype(BF16)
    lo = (x - hi.astype(F32)).astype(BF16)
    return jnp.concatenate([hi, lo], axis=1)


def _sb_fwd(proj, q_col0, k_col0, v_col0, batch, s, ride=None, streams=MIX_STREAMS):
    t = proj.shape[0]
    nq = s // SB_Q
    n_pairs = 4
    ns = streams
    wide = ns * LANES
    scale = HEAD_DIM ** -0.5

    def body(q_ref, k_ref, v_ref, o_ref, tot_ref, acc_ref, run_ref):
        i = pl.program_id(2)
        lo_q = lax.broadcasted_iota(jnp.int32, (SB_Q, LANES), 1) < HEAD_DIM
        lo_k = _lane_lo()
        mat = _sb_consts(True)
        row = lax.broadcasted_iota(jnp.int32, (2 * SB_Q, LANES), 0) % SB_Q
        ahead = row - lax.broadcasted_iota(jnp.int32, (2 * SB_Q, LANES), 1)
        acc_ref[...] = jnp.zeros_like(acc_ref)
        run_ref[...] = jnp.zeros_like(run_ref)
        qqs = [_stack_heads(q_ref[:, st * LANES:(st + 1) * LANES] * scale, lo_q) for st in range(ns)]

        def units(todo):
            def rows(j):
                return pl.ds(pl.multiple_of(j * BLOCK, BLOCK), BLOCK)

            zs = [lax.dot_general(qqs[st], k_ref[rows(j), st * LANES:(st + 1) * LANES], NT, preferred_element_type=F32)
                  for st, j, _ in todo]
            logs = []
            for z, (_, _, off) in zip(zs, todo):
                lsig = jnp.minimum(z, 0.0) - jnp.log(1.0 + jnp.exp(-jnp.abs(z)))
                lneg = lsig - z
                if off is not None:
                    lneg = jnp.where(ahead > off, lneg, 0.0)
                logs.append((lsig, _split(lneg)))
            sums = [lax.dot_general(cat, mat, NN, preferred_element_type=F32) for _, cat in logs]
            probs = []
            for (lsig, _), sm, (st, _, off) in zip(logs, sums, todo):
                run = run_ref[st]
                a = jnp.exp(lsig + run + sm[:, :BLOCK])
                if off is not None:
                    a = jnp.where(ahead > off, a, 0.0)
                run_ref[st] = run + sm[:, BLOCK:]
                probs.append(a.astype(BF16))
            for ab, (st, j, _) in zip(probs, todo):
                v = v_ref[rows(j), st * LANES:(st + 1) * LANES]
                vz = jnp.zeros_like(v)
                acc_ref[st] += (lax.dot_general(ab[:SB_Q], jnp.where(lo_k, v, vz), NN, preferred_element_type=F32)
                                + lax.dot_general(ab[SB_Q:], jnp.where(lo_k, vz, v), NN, preferred_element_type=F32))

        units([(st, 2 * i + 1, BLOCK) for st in range(ns)] + [(st, 2 * i, 0) for st in range(ns)])

        def pair(p, carry):
            jp = i - 1 - p
            units([(st, 2 * jp + 1, None) for st in range(ns)] + [(st, 2 * jp, None) for st in range(ns)])
            return carry

        lax.fori_loop(0, i, pair, 0)
        for st in range(ns):
            cols = slice(st * LANES, (st + 1) * LANES)
            o_ref[:, cols] = acc_ref[st].astype(BF16)
            tot_ref[:, cols] = jnp.where(lo_q, run_ref[st, 0:SB_Q, :], run_ref[st, SB_Q:2 * SB_Q, :])

    def seq(col0):
        return pl.BlockSpec((s, wide), lambda b, h, i: (b, col0 // ns + h))

    blk = pl.BlockSpec((SB_Q, wide), lambda b, h, i: (b * nq + i, h))
    return _call(
        body, name="attn_b_fwd", grid=(batch, n_pairs // ns, nq),
        in_specs=[pl.BlockSpec((SB_Q, wide), lambda b, h, i: (b * nq + i, q_col0 // ns + h)), seq(k_col0), seq(v_col0)],
        out_specs=[blk, blk],
        out_shape=[jax.ShapeDtypeStruct((t, n_pairs * LANES), BF16), jax.ShapeDtypeStruct((t, n_pairs * LANES), F32)],
        scratch=[pltpu.VMEM((ns, SB_Q, LANES), F32), pltpu.VMEM((ns, 2 * SB_Q, LANES), F32)],
        sem=("parallel", "parallel", "arbitrary"), args=(proj, proj, proj), ride=ride)


def _sb_bwd(proj, q_col0, k_col0, v_col0, tot, do, batch, s, ride=None, streams=SB_BWD_STREAMS):
    t = proj.shape[0]
    nq = s // SB_Q
    n_pairs = 4
    ns = streams
    wide = ns * LANES
    scale = HEAD_DIM ** -0.5

    def body(q_ref, k_ref, v_ref, tot_ref, do_ref, dq_ref, dk_ref, dv_ref, dk_acc, dv_acc, dq_acc, seen_ref, gsum_ref):
        i = pl.program_id(2)
        lo_q = lax.broadcasted_iota(jnp.int32, (SB_Q, LANES), 1) < HEAD_DIM
        lo_k = _lane_lo()

        @pl.when(i == 0)
        def _():
            dk_acc[...] = jnp.zeros_like(dk_acc)
            dv_acc[...] = jnp.zeros_like(dv_acc)

        mat_after = _sb_consts(True)
        mat_before = _sb_consts(False)[:BLOCK]
        row = lax.broadcasted_iota(jnp.int32, (2 * SB_Q, LANES), 0) % SB_Q
        ahead = row - lax.broadcasted_iota(jnp.int32, (2 * SB_Q, LANES), 1)
        dq_acc[...] = jnp.zeros_like(dq_acc)
        seen_ref[...] = jnp.zeros_like(seen_ref)
        gsum_ref[...] = jnp.zeros_like(gsum_ref)
        qqs, dds, totals = [], [], []
        for st in range(ns):
            cols = slice(st * LANES, (st + 1) * LANES)
            qqs.append(_stack_heads(q_ref[:, cols] * scale, lo_q))
            dds.append(_stack_heads(do_ref[:, cols], lo_q))
            tot_t = tot_ref[:, cols]
            totals.append(jnp.concatenate([jnp.broadcast_to(tot_t[:, 0:1], (SB_Q, LANES)),
                                           jnp.broadcast_to(tot_t[:, HEAD_DIM:HEAD_DIM + 1], (SB_Q, LANES))], axis=0))

        def units(todo):
            def rows(j):
                return pl.ds(pl.multiple_of(j * BLOCK, BLOCK), BLOCK)

            def cols(st):
                return slice(st * LANES, (st + 1) * LANES)

            prods = [(lax.dot_general(qqs[st], k_ref[rows(j), cols(st)], NT, preferred_element_type=F32),
                      lax.dot_general(dds[st], v_ref[rows(j), cols(st)], NT, preferred_element_type=F32))
                     for st, j, _ in todo]
            logs = []
            for (z, _), (_, _, off) in zip(prods, todo):
                lsig = jnp.minimum(z, 0.0) - jnp.log(1.0 + jnp.exp(-jnp.abs(z)))
                lneg = lsig - z
                if off is not None:
                    lneg = jnp.where(ahead > off, lneg, 0.0)
                logs.append((lsig, _split(lneg)))
            sums = [lax.dot_general(cat, mat_after, NN, preferred_element_type=F32) for _, cat in logs]
            gates = []
            for (lsig, _), sm, (_, da), (st, _, off) in zip(logs, sums, prods, todo):
                seen = seen_ref[st]
                a = jnp.exp(lsig + (totals[st] - seen - sm[:, BLOCK:]) + sm[:, :BLOCK])
                if off is not None:
                    a = jnp.where(ahead > off, a, 0.0)
                seen_ref[st] = seen + sm[:, BLOCK:]
                g = a * da
                gates.append((a.astype(BF16), g, g.astype(BF16)))
            gsums = [lax.dot_general(cat, mat_before, NN, preferred_element_type=F32) for _, _, cat in gates]
            outs = []
            for (lsig, _), (ab, g, _), gs, (st, _, off) in zip(logs, gates, gsums, todo):
                gsum = gsum_ref[st]
                dz = g - jnp.exp(lsig) * (g + gsum + gs[:, :BLOCK])
                if off is not None:
                    dz = jnp.where(ahead > off, dz, 0.0)
                gsum_ref[st] = gsum + gs[:, BLOCK:]
                outs.append((dz.astype(BF16), ab))
            for (dzb, ab), (st, j, _) in zip(outs, todo):
                k = k_ref[rows(j), cols(st)]
                kz = jnp.zeros_like(k)
                dq_acc[st] += (lax.dot_general(dzb[:SB_Q], jnp.where(lo_k, k, kz), NN, preferred_element_type=F32)
                               + lax.dot_general(dzb[SB_Q:], jnp.where(lo_k, kz, k), NN, preferred_element_type=F32))
                dk_acc[rows(j), cols(st)] += lax.dot_general(dzb, qqs[st], TN, preferred_element_type=F32)
                dv_acc[rows(j), cols(st)] += lax.dot_general(ab, dds[st], TN, preferred_element_type=F32)

        def pair(p, carry):
            units([(st, 2 * p, None) for st in range(ns)] + [(st, 2 * p + 1, None) for st in range(ns)])
            return carry

        lax.fori_loop(0, i, pair, 0)
        units([(st, 2 * i, 0) for st in range(ns)] + [(st, 2 * i + 1, BLOCK) for st in range(ns)])
        for st in range(ns):
            dq_ref[:, st * LANES:(st + 1) * LANES] = (dq_acc[st] * scale).astype(BF16)

        @pl.when(i == nq - 1)
        def _():
            dk_ref[...] = dk_acc[...].astype(BF16)
            dv_ref[...] = dv_acc[...].astype(BF16)

    def seq_in(col0):
        return pl.BlockSpec((s, wide), lambda b, h, i: (b, col0 // ns + h))

    blk = pl.BlockSpec((SB_Q, wide), lambda b, h, i: (b * nq + i, h))
    seq = pl.BlockSpec((s, wide), lambda b, h, i: (b, h))
    out = jax.ShapeDtypeStruct((t, n_pairs * LANES), BF16)
    return _call(
        body, name="attn_b_bwd", grid=(batch, n_pairs // ns, nq),
        in_specs=[pl.BlockSpec((SB_Q, wide), lambda b, h, i: (b * nq + i, q_col0 // ns + h)), seq_in(k_col0),
                  seq_in(v_col0), blk, blk],
        out_specs=[blk, seq, seq], out_shape=[out, out, out],
        scratch=[pltpu.VMEM((s, wide), F32), pltpu.VMEM((s, wide), F32), pltpu.VMEM((ns, SB_Q, LANES), F32),
                 pltpu.VMEM((ns, 2 * SB_Q, LANES), F32), pltpu.VMEM((ns, 2 * SB_Q, LANES), F32)],
        sem=("parallel", "parallel", "arbitrary"), args=(proj, proj, proj, tot, do), ride=ride)


MEM_Q_TILE = 1024


def _mem_fwd(q, kv, batch, s, n_mem):
    t, width = q.shape
    tq = min(MEM_Q_TILE, s)
    nq = s // tq
    scale = MEM_HEAD_DIM ** -0.5

    def body(q_ref, kv_ref, o_ref):
        for h in range(N_HEADS_MEM):
            cols = slice(h * MEM_HEAD_DIM, (h + 1) * MEM_HEAD_DIM)
            k = kv_ref[:, cols]
            v = kv_ref[:, width + h * MEM_HEAD_DIM: width + (h + 1) * MEM_HEAD_DIM]
            sc = lax.dot_general(q_ref[:, cols], k, NT, preferred_element_type=F32) * scale
            p = jnp.exp(sc - jnp.max(sc, axis=1, keepdims=True))
            p = p / jnp.sum(p, axis=1, keepdims=True)
            o_ref[:, cols] = lax.dot_general(p.astype(BF16), v, NN, preferred_element_type=F32).astype(BF16)

    return pl.pallas_call(
        body, name="mem_attn_fwd", grid=(batch, nq),
        in_specs=[pl.BlockSpec((tq, width), lambda b, i: (b * nq + i, 0)),
                  pl.BlockSpec((n_mem, 2 * width), lambda b, i: (b, 0))],
        out_specs=pl.BlockSpec((tq, width), lambda b, i: (b * nq + i, 0)),
        out_shape=jax.ShapeDtypeStruct((t, width), BF16),
        compiler_params=_params(("parallel", "parallel")),
    )(q, kv)


def _mem_bwd(q, kv, do, batch, s, n_mem):
    t, width = q.shape
    tq = min(MEM_Q_TILE, s)
    nq = s // tq
    scale = MEM_HEAD_DIM ** -0.5

    def body(q_ref, kv_ref, do_ref, dq_ref, dkv_ref, acc):
        i = pl.program_id(1)

        @pl.when(i == 0)
        def _():
            acc[...] = jnp.zeros_like(acc)

        for h in range(N_HEADS_MEM):
            cols = slice(h * MEM_HEAD_DIM, (h + 1) * MEM_HEAD_DIM)
            vcols = slice(width + h * MEM_HEAD_DIM, width + (h + 1) * MEM_HEAD_DIM)
            qh, k, v, doh = q_ref[:, cols], kv_ref[:, cols], kv_ref[:, vcols], do_ref[:, cols]
            sc = lax.dot_general(qh, k, NT, preferred_element_type=F32) * scale
            p = jnp.exp(sc - jnp.max(sc, axis=1, keepdims=True))
            p = p / jnp.sum(p, axis=1, keepdims=True)
            dp = lax.dot_general(doh, v, NT, preferred_element_type=F32)
            ds = (p * (dp - jnp.sum(p * dp, axis=1, keepdims=True)) * scale).astype(BF16)
            dq_ref[:, cols] = lax.dot_general(ds, k, NN, preferred_element_type=F32).astype(BF16)
            acc[:, cols] += lax.dot_general(ds, qh, TN, preferred_element_type=F32)
            acc[:, vcols] += lax.dot_general(p.astype(BF16), doh, TN, preferred_element_type=F32)

        @pl.when(i == nq - 1)
        def _():
            dkv_ref[...] = acc[...].astype(BF16)

    row = pl.BlockSpec((tq, width), lambda b, i: (b * nq + i, 0))
    kvs = pl.BlockSpec((n_mem, 2 * width), lambda b, i: (b, 0))
    return pl.pallas_call(
        body, name="mem_attn_bwd", grid=(batch, nq),
        in_specs=[row, kvs, row], out_specs=[row, kvs],
        out_shape=[jax.ShapeDtypeStruct((t, width), BF16), jax.ShapeDtypeStruct((batch * n_mem, 2 * width), BF16)],
        scratch_shapes=[pltpu.VMEM((n_mem, 2 * width), F32)],
        compiler_params=_params(("parallel", "arbitrary")),
    )(q, kv, do)


def _mixer_fwd(o_a, o_b, w_a, w_b, proj, gate_col0, w_out, x, g, w_q):
    t, width = o_a.shape
    d = w_a.shape[1]
    nq_cols = w_q.shape[1]
    tm = min(ROW_TILE, t)
    gb0 = gate_col0 * LANES // d

    def body(oa_ref, ob_ref, wa_ref, wb_ref, ga_ref, gb_ref, wo_ref, x_ref, g_ref, wq_ref, ua_ref, ub_ref, mix_ref,
             n_ref, h_ref, q_ref):
        ua = lax.dot_general(oa_ref[...], wa_ref[...], NN, preferred_element_type=F32)
        ub = lax.dot_general(ob_ref[...], wb_ref[...], NN, preferred_element_type=F32)
        ua_ref[...] = ua.astype(BF16)
        ub_ref[...] = ub.astype(BF16)
        mixed = (jax.nn.sigmoid(ga_ref[...].astype(F32)) * ua + jax.nn.sigmoid(gb_ref[...].astype(F32)) * ub).astype(BF16)
        mix_ref[...] = mixed
        h = lax.dot_general(mixed, wo_ref[...], NN, preferred_element_type=F32) + x_ref[...]
        h_ref[...] = h
        r = lax.rsqrt(jnp.mean(h * h, axis=-1, keepdims=True) + RMS_EPS)
        n = (h * r * g_ref[...]).astype(BF16)
        n_ref[...] = n
        q_ref[...] = lax.dot_general(n, wq_ref[...], NN, preferred_element_type=F32).astype(BF16)

    row = pl.BlockSpec((tm, width), lambda i: (i, 0))
    wsp = pl.BlockSpec((width, d), lambda i: (0, 0))
    out = pl.BlockSpec((tm, d), lambda i: (i, 0))
    osh = jax.ShapeDtypeStruct((t, d), BF16)
    return pl.pallas_call(
        body, name="mixer_fwd", grid=(t // tm,),
        in_specs=[row, row, wsp, wsp,
                  pl.BlockSpec((tm, d), lambda i: (i, gb0)), pl.BlockSpec((tm, d), lambda i: (i, gb0 + 1)),
                  pl.BlockSpec((d, d), lambda i: (0, 0)), out, pl.BlockSpec((1, d), lambda i: (0, 0)),
                  pl.BlockSpec((d, nq_cols), lambda i: (0, 0))],
        out_specs=[out, out, out, out, out, pl.BlockSpec((tm, nq_cols), lambda i: (i, 0))],
        out_shape=[osh, osh, osh, osh, jax.ShapeDtypeStruct((t, d), F32), jax.ShapeDtypeStruct((t, nq_cols), BF16)],
        compiler_params=_params(("parallel",)),
    )(o_a, o_b, w_a, w_b, proj, proj, w_out, x, g, w_q)


def _mixer_bwd(dh, w_out, ua, ub, proj, gate_col0, w_a, w_b):
    t, d = dh.shape
    width = w_a.shape[0]
    tm = min(ROW_TILE, t)
    nc = d // LANES

    def body(dh_ref, w_ref, ua_ref, ub_ref, ga_ref, gb_ref, wa_ref, wb_ref, dua_ref, dub_ref, dg_ref, doa_ref, dob_ref):
        dm = lax.dot_general(dh_ref[...], w_ref[...], NT, preferred_element_type=F32)
        sa = jax.nn.sigmoid(ga_ref[...].astype(F32))
        sb = jax.nn.sigmoid(gb_ref[...].astype(F32))
        dua = (dm * sa).astype(BF16)
        dub = (dm * sb).astype(BF16)
        dua_ref[...] = dua
        dub_ref[...] = dub
        dg_ref[:, 0:d] = (dm * ua_ref[...].astype(F32) * sa * (1.0 - sa)).astype(BF16)
        dg_ref[:, d:2 * d] = (dm * ub_ref[...].astype(F32) * sb * (1.0 - sb)).astype(BF16)
        doa_ref[...] = lax.dot_general(dua, wa_ref[...], NT, preferred_element_type=F32).astype(BF16)
        dob_ref[...] = lax.dot_general(dub, wb_ref[...], NT, preferred_element_type=F32).astype(BF16)

    row = pl.BlockSpec((tm, d), lambda i: (i, 0))
    wsp = pl.BlockSpec((width, d), lambda i: (0, 0))
    osp = pl.BlockSpec((tm, width), lambda i: (i, 0))
    return pl.pallas_call(
        body, name="mixer_bwd", grid=(t // tm,),
        in_specs=[row, pl.BlockSpec((d, d), lambda i: (0, 0)), row, row,
                  pl.BlockSpec((tm, d), lambda i: (i, gate_col0 // nc)),
                  pl.BlockSpec((tm, d), lambda i: (i, gate_col0 // nc + 1)), wsp, wsp],
        out_specs=[row, row, pl.BlockSpec((tm, 2 * d), lambda i: (i, 0)), osp, osp],
        out_shape=[jax.ShapeDtypeStruct((t, d), BF16), jax.ShapeDtypeStruct((t, d), BF16),
                   jax.ShapeDtypeStruct((t, 2 * d), BF16), jax.ShapeDtypeStruct((t, width), BF16),
                   jax.ShapeDtypeStruct((t, width), BF16)],
        compiler_params=_params(("parallel",)),
    )(dh, w_out, ua, ub, proj, proj, w_a, w_b)


FFN_COLS = 1024
FFN_CHUNK = 256


def _ffn_up(n, w_gate, w_up):
    t, d = n.shape
    hidden = w_gate.shape[0]
    tm = min(2 * ROW_TILE, t)
    tn = min(FFN_COLS, hidden)
    tc = min(FFN_CHUNK, tn)

    def body(n_ref, wg_ref, wu_ref, hg_ref, hu_ref, act_ref):
        for c in range(0, tn, tc):
            hg = lax.dot_general(n_ref[...], wg_ref[c:c + tc, :], NT, preferred_element_type=F32)
            hu = lax.dot_general(n_ref[...], wu_ref[c:c + tc, :], NT, preferred_element_type=F32)
            hg_ref[:, c:c + tc] = hg.astype(BF16)
            hu_ref[:, c:c + tc] = hu.astype(BF16)
            act_ref[:, c:c + tc] = (hg * jax.nn.sigmoid(hg) * hu).astype(BF16)

    wsp = pl.BlockSpec((tn, d), lambda j, i: (j, 0))
    out = pl.BlockSpec((tm, tn), lambda j, i: (i, j))
    osh = jax.ShapeDtypeStruct((t, hidden), BF16)
    return pl.pallas_call(
        body, name="ffn_up", grid=(hidden // tn, t // tm),
        in_specs=[pl.BlockSpec((tm, d), lambda j, i: (i, 0)), wsp, wsp],
        out_specs=[out, out, out], out_shape=[osh, osh, osh],
        compiler_params=_params(("parallel", "parallel")),
    )(n, w_gate, w_up)


def _ffn_bwd(dh, w_down, w_gate, w_up, hg, hu, x, g, dres, w_prev):
    t, d = dh.shape
    hidden = w_down.shape[0]
    q = w_prev.shape[0]
    tm = min(ROW_TILE, t)
    tn = min(FFN_COLS, hidden)
    nj = hidden // tn

    def body(dh_ref, wd_ref, wg_ref, wu_ref, hg_ref, hu_ref, x_ref, g_ref, r_ref, wp_ref, dhg_ref, dhu_ref, dx_ref,
             dxb_ref, dg_ref, do_ref, acc):
        j, i = pl.program_id(0), pl.program_id(1)
        dact = lax.dot_general(dh_ref[...], wd_ref[...], NT, preferred_element_type=F32)
        hg = hg_ref[...].astype(F32)
        sg = jax.nn.sigmoid(hg)
        dhu = (dact * hg * sg).astype(BF16)
        dhg = (dact * hu_ref[...].astype(F32) * sg * (1.0 + hg * (1.0 - sg))).astype(BF16)
        dhu_ref[...] = dhu
        dhg_ref[...] = dhg
        part = (lax.dot_general(dhg, wg_ref[...], NN, preferred_element_type=F32)
                + lax.dot_general(dhu, wu_ref[...], NN, preferred_element_type=F32))

        @pl.when(j == 0)
        def _():
            acc[i] = part

        @pl.when(j > 0)
        def _():
            acc[i] += part

        @pl.when(jnp.logical_and(j == 0, i == 0))
        def _():
            dg_ref[...] = jnp.zeros_like(dg_ref)

        @pl.when(j == nj - 1)
        def _():
            dx, dg = _rms_bwd_rows(acc[i], x_ref[...], g_ref[...], r_ref[...])
            dx_ref[...] = dx
            dxb = dx.astype(BF16)
            dxb_ref[...] = dxb
            dg_ref[...] += dg
            do_ref[...] = lax.dot_general(dxb, wp_ref[...], NT, preferred_element_type=F32).astype(BF16)

    hid = pl.BlockSpec((tm, tn), lambda j, i: (i, j))
    wsp = pl.BlockSpec((tn, d), lambda j, i: (j, 0), pipeline_mode=pl.Buffered(1))
    late = pl.BlockSpec((tm, d), lambda j, i: (jnp.where(j == nj - 1, i, 0), 0))
    late_q = pl.BlockSpec((tm, q), lambda j, i: (jnp.where(j == nj - 1, i, 0), 0))
    vec = pl.BlockSpec((1, d), lambda j, i: (0, 0))
    osh = jax.ShapeDtypeStruct((t, hidden), BF16)
    return pl.pallas_call(
        body, name="ffn_bwd", grid=(nj, t // tm),
        in_specs=[pl.BlockSpec((tm, d), lambda j, i: (i, 0)), wsp, wsp, wsp, hid, hid, late, vec, late,
                  pl.BlockSpec((q, d), lambda j, i: (0, 0), pipeline_mode=pl.Buffered(1))],
        out_specs=[hid, hid, late, late, vec, late_q],
        out_shape=[osh, osh, jax.ShapeDtypeStruct((t, d), F32), jax.ShapeDtypeStruct((t, d), BF16),
                   jax.ShapeDtypeStruct((1, d), F32), jax.ShapeDtypeStruct((t, q), BF16)],
        scratch_shapes=[pltpu.VMEM((t // tm, tm, d), F32)],
        compiler_params=_params(("arbitrary", "arbitrary")),
    )(dh, w_down, w_gate, w_up, hg, hu, x, g, dres, w_prev)


MM_ROWS = 1024


def _mm_w(name, a, w, out_dtype, dims=NN):
    t, k = a.shape
    n = w.shape[1] if dims == NN else w.shape[0]
    tm, tn = min(MM_ROWS, t), min(1024, n)
    o_spec = pl.BlockSpec((tm, tn), lambda j, i: (i, j))
    b_spec = pl.BlockSpec((k, tn), lambda j, i: (0, j)) if dims == NN else pl.BlockSpec((tn, k), lambda j, i: (j, 0))
    return _mm(name, a, w, grid=(n // tn, t // tm), a_spec=pl.BlockSpec((tm, k), lambda j, i: (i, 0)), b_spec=b_spec,
               o_shape=(t, n), o_spec=o_spec, dims=dims, out_dtype=out_dtype)


def _mm_res_norm(name, a, w, res, g):
    t, k = a.shape
    d = w.shape[1]
    tm = min(MM_ROWS, t)

    def body(a_ref, w_ref, r_ref, g_ref, h_ref, n_ref):
        h = lax.dot_general(a_ref[...], w_ref[...], NN, preferred_element_type=F32) + r_ref[...]
        h_ref[...] = h
        r = lax.rsqrt(jnp.mean(h * h, axis=-1, keepdims=True) + RMS_EPS)
        n_ref[...] = (h * r * g_ref[...]).astype(BF16)

    row = pl.BlockSpec((tm, d), lambda i: (i, 0))
    return pl.pallas_call(
        body, name=name, grid=(t // tm,),
        in_specs=[pl.BlockSpec((tm, k), lambda i: (i, 0)), pl.BlockSpec((k, d), lambda i: (0, 0)), row,
                  pl.BlockSpec((1, d), lambda i: (0, 0))],
        out_specs=[row, row], out_shape=[jax.ShapeDtypeStruct((t, d), F32), jax.ShapeDtypeStruct((t, d), BF16)],
        compiler_params=_params(("parallel",)),
    )(a, w, res, g)


WGRAD_COLS = 256


def _wgrad(name, a, g, tk=1024, tn=1024):
    t, k = a.shape
    n = g.shape[1]
    tm, tk, tn = min(2 * MM_ROWS, t), min(tk, k), min(tn, n)
    nr = t // tm
    tc = min(WGRAD_COLS, tn)

    def body(a_ref, g_ref, o_ref, *acc):
        def run(first, last):
            for c in range(0, tn, tc):
                p = lax.dot_general(a_ref[...], g_ref[:, c:c + tc], TN, preferred_element_type=F32)
                if not first:
                    p += acc[0][:, c:c + tc]
                if last:
                    o_ref[:, c:c + tc] = p.astype(BF16)
                else:
                    acc[0][:, c:c + tc] = p

        if nr == 1:
            run(True, True)
            return
        r = pl.program_id(2)
        pl.when(r == 0)(functools.partial(run, True, False))
        if nr > 2:
            pl.when(jnp.logical_and(r > 0, r < nr - 1))(functools.partial(run, False, False))
        pl.when(r == nr - 1)(functools.partial(run, False, True))

    return pl.pallas_call(
        body, name=name, grid=(k // tk, n // tn, nr),
        in_specs=[pl.BlockSpec((tm, tk), lambda p, q, r: (r, p)), pl.BlockSpec((tm, tn), lambda p, q, r: (r, q))],
        out_specs=pl.BlockSpec((tk, tn), lambda p, q, r: (p, q)), out_shape=jax.ShapeDtypeStruct((k, n), BF16),
        scratch_shapes=[pltpu.VMEM((tk, tn), F32)] if nr > 1 else [],
        compiler_params=_params(("parallel", "parallel", "arbitrary")),
    )(a, g)


def _peers():
    x, y, c = lax.axis_index("x"), lax.axis_index("y"), lax.axis_index("c")
    me = 4 * x + 2 * y + c
    out = []
    for k in range(1, N_DEV):
        kx, ky, kc = (k >> 2) & 1, (k >> 1) & 1, k & 1
        px = 1 - x if kx else x
        py = 1 - y if ky else y
        pc = 1 - c if kc else c
        out.append(((px, py, pc), 4 * px + 2 * py + pc))
    return me, out


def _cast_weights(ws, pad_rows):
    def body(*refs):
        n = len(refs) // 2
        for i_ref, o_ref, pr in zip(refs[:n], refs[n:], pad_rows):
            r, c = i_ref.shape
            o_ref[0:r, :] = i_ref[...].astype(BF16)
            if pr:
                o_ref[r:r + pr, :] = jnp.zeros((pr, c), BF16)

    return pl.pallas_call(
        body, name="cast_weights", in_specs=[VMEM] * len(ws), out_specs=[VMEM] * len(ws),
        out_shape=[jax.ShapeDtypeStruct((w.shape[0] + pr, w.shape[1]), BF16) for w, pr in zip(ws, pad_rows)],
    )(*ws)


def _window(ref, j, c):
    return ref.at[:, pl.ds(pl.multiple_of(j * c, LANES), c)]


def _direct_copies(ins, outs, sems, gather, cols, landed):
    send_sems, recv_sems, loc_sems = sems
    n_peer = N_DEV - 1
    me, peers = _peers()

    def src(w, j):
        if gather:
            return ins[w]
        return _window(ins[w], j, cols[w]) if cols[w] else ins[w].at[j]

    def dst(w, j):
        return _window(outs[w], j, cols[w]) if gather and cols[w] else outs[w].at[j]

    local = [pltpu.make_async_copy(src(w, me), dst(w, me), loc_sems.at[w]) for w in range(len(ins))]
    remote = [pltpu.make_async_remote_copy(
        src_ref=src(w, idx), dst_ref=dst(w, idx if landed else me),
        send_sem=send_sems.at[w * n_peer + k], recv_sem=recv_sems.at[w * n_peer + k],
        device_id=dev, device_id_type=pl.DeviceIdType.MESH)
        for k, (dev, idx) in reversed(list(enumerate(peers))) for w in range(len(ins))]
    return local, remote


OTHER_CHIPS = (2, 4, 6)


def _gather_copies(ins, outs, sems, cols):
    send_sems, recv_sems, loc_sems = sems
    x, y, c = lax.axis_index("x"), lax.axis_index("y"), lax.axis_index("c")
    me = 4 * x + 2 * y + c
    n_pair = N_DEV - 1

    def dev(mask):
        return (1 - x if mask & 4 else x, 1 - y if mask & 2 else y, 1 - c if mask & 1 else c)

    def slot(w, mask):
        j = jnp.bitwise_xor(me, mask)
        return _window(outs[w], j, cols[w]) if cols[w] else outs[w].at[j]

    def remote(w, pair, src, to_slot, target):
        return pltpu.make_async_remote_copy(src_ref=src, dst_ref=slot(w, to_slot), send_sem=send_sems.at[w * n_pair + pair],
                                            recv_sem=recv_sems.at[w * n_pair + pair], device_id=dev(target),
                                            device_id_type=pl.DeviceIdType.MESH)

    ws = range(len(ins))
    return dict(
        local=[pltpu.make_async_copy(ins[w], slot(w, 0), loc_sems.at[w]) for w in ws],
        to_chips=[remote(w, 1 + t, ins[w], 0, m) for t, m in enumerate(OTHER_CHIPS) for w in ws],
        to_core=[remote(w, 0, ins[w], 0, 1) for w in ws],
        from_chips=[remote(w, 1 + t, ins[w], m, 0) for t, m in enumerate(OTHER_CHIPS) for w in ws],
        pass_on=[remote(w, 4 + t, slot(w, m), m, 1) for t, m in enumerate(OTHER_CHIPS) for w in ws],
        from_core=[remote(w, 0, ins[w], 1, 0) for w in ws]
        + [remote(w, 4 + t, ins[w], m + 1, 0) for t, m in enumerate(OTHER_CHIPS) for w in ws])


TWO_LEVEL = "gather in two levels"


def _exchange_start(ins, outs, sems, gather, cols):
    if gather == TWO_LEVEL:
        cps = _gather_copies(ins, outs, sems, cols)
        for cp in cps["local"] + cps["to_chips"] + cps["to_core"]:
            cp.start()
    else:
        local, remote = _direct_copies(ins, outs, sems, gather, cols, False)
        for cp in local + remote:
            cp.start()


def _exchange_pass_on(ins, outs, sems, gather, cols, chips):
    if gather == TWO_LEVEL:
        cps = _gather_copies(ins, outs, sems, cols)
        n = len(ins)
        for t in chips:
            for arrived, onward in zip(cps["from_chips"][t * n:(t + 1) * n], cps["pass_on"][t * n:(t + 1) * n]):
                arrived.wait_recv()
                onward.start()


def _exchange_wait(ins, outs, sems, gather, cols):
    if gather == TWO_LEVEL:
        cps = _gather_copies(ins, outs, sems, cols)
        for cp in cps["local"]:
            cp.wait()
        for cp in cps["to_chips"] + cps["to_core"] + cps["pass_on"]:
            cp.wait_send()
        for cp in cps["from_core"]:
            cp.wait_recv()
    else:
        local, remote = _direct_copies(ins, outs, sems, gather, cols, True)
        for cp in local:
            cp.wait()
        for cp in remote:
            cp.wait_send()
            cp.wait_recv()


def _exchange_shapes(arrs, gather, cols):
    n = len(arrs)
    out_shape = []
    for a, c in zip(arrs, cols):
        if gather:
            shape = (a.shape[0], N_DEV * c) if c else (N_DEV,) + a.shape
        else:
            shape = (N_DEV, a.shape[0], c) if c else a.shape
        out_shape.append(jax.ShapeDtypeStruct(shape, a.dtype))
    sems = [pltpu.SemaphoreType.DMA((n * (N_DEV - 1),)), pltpu.SemaphoreType.DMA((n * (N_DEV - 1),)),
            pltpu.SemaphoreType.DMA((n,))]
    return out_shape, sems


def _call(body, *, name, grid, in_specs, out_specs, out_shape, scratch, sem, args, ride=None):
    if ride is None:
        outs = pl.pallas_call(body, name=name, grid=grid, in_specs=in_specs, out_specs=out_specs, out_shape=out_shape,
                              scratch_shapes=scratch, compiler_params=_params(sem))(*args)
        return outs, None
    arrs, gather, cols = ride
    n, n_in, n_out, n_scr = len(arrs), len(in_specs), len(out_specs), len(scratch)
    x_shape, x_sems = _exchange_shapes(arrs, gather, cols)

    def riding(*refs):
        ins, x_ins = refs[:n_in], refs[n_in:n_in + n]
        outs = refs[n_in + n:n_in + n + n_out]
        x_outs = refs[n_in + n + n_out:n_in + 2 * n + n_out]
        scr = refs[n_in + 2 * n + n_out:n_in + 2 * n + n_out + n_scr]
        sems = refs[n_in + 2 * n + n_out + n_scr:]
        def at(step):
            return functools.reduce(jnp.logical_and, [pl.program_id(a) == v for a, v in enumerate(step)])

        @pl.when(at((0,) * len(grid)))
        def _():
            _exchange_start(x_ins, x_outs, sems, gather, cols)

        @pl.when(at((grid[0] // 2,) + (0,) * (len(grid) - 2) + (grid[-1] // 2,)))
        def _():
            _exchange_pass_on(x_ins, x_outs, sems, gather, cols, (0, 1))

        @pl.when(at((grid[0] // 2,) + (0,) * (len(grid) - 2) + (3 * grid[-1] // 4,)))
        def _():
            _exchange_pass_on(x_ins, x_outs, sems, gather, cols, (2,))

        body(*ins, *outs, *scr)

        @pl.when(at(tuple(g - 1 for g in grid)))
        def _():
            _exchange_wait(x_ins, x_outs, sems, gather, cols)

    res = pl.pallas_call(
        riding, name=name, grid=grid, in_specs=list(in_specs) + [ANY] * n, out_specs=list(out_specs) + [ANY] * n,
        out_shape=list(out_shape) + x_shape, scratch_shapes=list(scratch) + x_sems,
        compiler_params=_params(("arbitrary",) * len(grid)))(*args, *arrs)
    return res[:n_out], res[n_out:]


def _my_block():
    return (4 * lax.axis_index("x") + 2 * lax.axis_index("y") + lax.axis_index("c")).astype(jnp.int32).reshape(1)


def _proj_in_gather(x, g, w_shard):
    t, k = x.shape
    cs = w_shard.shape[1]
    tm = min(MM_ROWS, t)
    ni = t // tm
    arrival = (0, 1, 2, 4, 3, 5, 6, 7)

    def mask_at(s):
        return jnp.where(s == 3, 4, jnp.where(s == 4, 3, s))

    def body(me_ref, x_ref, g_ref, w_hbm, o_ref, all_hbm, n_hbm, w_vmem, n_vmem, send_sems, recv_sems, loc_sems,
             load_sems, n_sem):
        s, i = pl.program_id(0), pl.program_id(1)
        cps = _gather_copies([w_hbm], [all_hbm], (send_sems, recv_sems, loc_sems), (cs,))
        by_mask = {0: cps["local"][0], 1: cps["from_core"][0]}
        for t_chip, m in enumerate(OTHER_CHIPS):
            by_mask[m] = cps["from_chips"][t_chip]
            by_mask[m + 1] = cps["from_core"][1 + t_chip]
        arrived = [by_mask[m] for m in arrival]

        def load(step):
            src = w_hbm if step == 0 else _window(all_hbm, jnp.bitwise_xor(me_ref[0], arrival[step]), cs)
            return pltpu.make_async_copy(src, w_vmem.at[step % 2], load_sems.at[step % 2])

        @pl.when(jnp.logical_and(s == 0, i == 0))
        def _():
            for cp in cps["local"] + cps["to_chips"] + cps["to_core"]:
                cp.start()
            load(0).start()

        for step, mask in enumerate(arrival):
            @pl.when(jnp.logical_and(s == step, i == 0))
            def _(step=step):
                load(step).wait()

            if step + 1 < N_DEV:
                @pl.when(jnp.logical_and(s == step, i == min(1, ni - 1)))
                def _(step=step):
                    arrived[step + 1].wait_recv()
                    if arrival[step + 1] in OTHER_CHIPS:
                        cps["pass_on"][OTHER_CHIPS.index(arrival[step + 1])].start()
                    load(step + 1).start()

        @pl.when(s == 0)
        def _():
            xf = x_ref[...]
            r = lax.rsqrt(jnp.mean(xf * xf, axis=-1, keepdims=True) + RMS_EPS)
            n_vmem[i] = (xf * r * g_ref[...]).astype(BF16)
            keep = pltpu.make_async_copy(n_vmem.at[i], n_hbm.at[pl.ds(pl.multiple_of(i * tm, tm), tm), :], n_sem)
            keep.start()
            keep.wait()

        o_ref[...] = lax.dot_general(n_vmem[i], w_vmem[s % 2], NN, preferred_element_type=F32).astype(BF16)

        @pl.when(jnp.logical_and(s == N_DEV - 1, i == ni - 1))
        def _():
            cps["local"][0].wait()
            for cp in cps["to_chips"] + cps["to_core"] + cps["pass_on"]:
                cp.wait_send()

    return pl.pallas_call(
        body, name="proj_in",
        grid_spec=pltpu.PrefetchScalarGridSpec(
            num_scalar_prefetch=1, grid=(N_DEV, ni),
            in_specs=[pl.BlockSpec((tm, k), lambda s, i, me: (jnp.where(s == 0, i, 0), 0)),
                      pl.BlockSpec((1, k), lambda s, i, me: (0, 0)), ANY],
            out_specs=[pl.BlockSpec((tm, cs), lambda s, i, me: (i, jnp.bitwise_xor(me[0], mask_at(s)))), ANY, ANY],
            scratch_shapes=[pltpu.VMEM((2, k, cs), BF16), pltpu.VMEM((ni, tm, k), BF16),
                            pltpu.SemaphoreType.DMA((N_DEV - 1,)), pltpu.SemaphoreType.DMA((N_DEV - 1,)),
                            pltpu.SemaphoreType.DMA((1,)), pltpu.SemaphoreType.DMA((2,)), pltpu.SemaphoreType.DMA]),
        out_shape=[jax.ShapeDtypeStruct((t, N_DEV * cs), BF16), jax.ShapeDtypeStruct((k, N_DEV * cs), BF16),
                   jax.ShapeDtypeStruct((t, k), BF16)],
        compiler_params=_params(("arbitrary", "arbitrary")),
    )(_my_block(), x, g, w_shard)


def _gw_in_scatter(a, g):
    t, k = a.shape
    cs = g.shape[1] // N_DEV
    tm = min(MM_ROWS, t)
    nr = t // tm
    n_chip = N_DEV // 2
    chips = (6, 4, 2, 0)

    def body(me_ref, a_ref, g_ref, out_hbm, acc, stage, other, core_send, core_recv, chip_send, chip_recv, loc_sem):
        s, r = pl.program_id(0), pl.program_id(1)
        x, y, c = lax.axis_index("x"), lax.axis_index("y"), lax.axis_index("c")
        my_chip = 2 * x + y
        part = lax.dot_general(a_ref[...], g_ref[...], TN, preferred_element_type=F32)

        def to_core(m):
            return pltpu.make_async_remote_copy(src_ref=stage.at[0], dst_ref=other.at[m], send_sem=core_send.at[m],
                                                recv_sem=core_recv.at[m], device_id=(x, y, 1 - c),
                                                device_id_type=pl.DeviceIdType.MESH)

        def to_chip(m, landed):
            mask = chips[m]
            there = (1 - x if mask & 4 else x, 1 - y if mask & 2 else y, c)
            slot = (2 * there[0] + there[1]) if landed else my_chip
            return pltpu.make_async_remote_copy(src_ref=stage.at[1], dst_ref=out_hbm.at[slot], send_sem=chip_send.at[m],
                                                recv_sem=chip_recv.at[m], device_id=there,
                                                device_id_type=pl.DeviceIdType.MESH)

        local = pltpu.make_async_copy(stage.at[1], out_hbm.at[my_chip], loc_sem)

        @pl.when(r == 0)
        def _():
            acc[...] = part

        @pl.when(r > 0)
        def _():
            acc[...] += part

        for step in range(N_DEV):
            m = step // 2

            @pl.when(jnp.logical_and(s == step, r == nr - 1))
            def _(step=step, m=m):
                if step % 2 == 0:
                    if m > 0:
                        to_core(m - 1).wait_send()
                    stage[0] = acc[...].astype(BF16)
                    to_core(m).start()
                else:
                    if m > 0:
                        to_chip(m - 1, False).wait_send()
                    to_core(m).wait_recv()
                    stage[1] = (acc[...] + other[m].astype(F32)).astype(BF16)
                    if m < n_chip - 1:
                        to_chip(m, False).start()
                    else:
                        local.start()
                        to_core(m).wait_send()
                        local.wait()
                        for mm in range(n_chip - 1):
                            to_chip(mm, True).wait_recv()

    return pl.pallas_call(
        body, name="gw_in",
        grid_spec=pltpu.PrefetchScalarGridSpec(
            num_scalar_prefetch=1, grid=(N_DEV, nr),
            in_specs=[pl.BlockSpec((tm, k), lambda s, r, me: (r, 0)),
                      pl.BlockSpec((tm, cs), lambda s, r, me: (r, jnp.bitwise_xor(me[0], N_DEV - 1 - s)))],
            out_specs=ANY,
            scratch_shapes=[pltpu.VMEM((k, cs), F32), pltpu.VMEM((2, k, cs), BF16), pltpu.VMEM((n_chip, k, cs), BF16),
                            pltpu.SemaphoreType.DMA((n_chip,)), pltpu.SemaphoreType.DMA((n_chip,)),
                            pltpu.SemaphoreType.DMA((n_chip - 1,)), pltpu.SemaphoreType.DMA((n_chip - 1,)),
                            pltpu.SemaphoreType.DMA]),
        out_shape=jax.ShapeDtypeStruct((n_chip, k, cs), BF16),
        compiler_params=_params(("arbitrary", "arbitrary")),
    )(_my_block(), a, g)


SMALL_ROWS = 8


def _allreduce_small(parts, loss_part):
    n, d = len(parts), parts[0].shape[1]

    def body(*refs):
        part_refs, loss_ref, o_ref = refs[:n], refs[n], refs[n + 1]
        mine_ref, all_ref, send_sems, recv_sems = refs[n + 2:]
        me, peers = _peers()
        mine_ref[...] = jnp.zeros_like(mine_ref)
        for i, p_ref in enumerate(part_refs):
            mine_ref[i:i + 1, :] = p_ref[...]
        mine_ref[SMALL_ROWS - 1:SMALL_ROWS, 0:LANES] = loss_ref[0:1, :]
        all_ref[me] = mine_ref[...]
        for k, (dev, idx) in enumerate(peers):
            pltpu.make_async_remote_copy(src_ref=mine_ref, dst_ref=all_ref.at[me], send_sem=send_sems.at[k],
                                         recv_sem=recv_sems.at[k], device_id=dev,
                                         device_id_type=pl.DeviceIdType.MESH).start()
        for k, (dev, idx) in enumerate(peers):
            cp = pltpu.make_async_remote_copy(src_ref=mine_ref, dst_ref=all_ref.at[idx], send_sem=send_sems.at[k],
                                              recv_sem=recv_sems.at[k], device_id=dev,
                                              device_id_type=pl.DeviceIdType.MESH)
            cp.wait_send()
            cp.wait_recv()
        tot = all_ref[0]
        for dvc in range(1, N_DEV):
            tot = tot + all_ref[dvc]
        o_ref[...] = tot

    return pl.pallas_call(
        body, name="allreduce_small", in_specs=[VMEM] * (n + 1), out_specs=VMEM,
        out_shape=jax.ShapeDtypeStruct((SMALL_ROWS, d), F32),
        scratch_shapes=[pltpu.VMEM((SMALL_ROWS, d), F32), pltpu.VMEM((N_DEV, SMALL_ROWS, d), F32),
                        pltpu.SemaphoreType.DMA((N_DEV - 1,)), pltpu.SemaphoreType.DMA((N_DEV - 1,))],
    )(*parts, loss_part)


def _adam_math(g, w, m, v):
    m_new = ADAM_B1 * m + (1.0 - ADAM_B1) * g
    v_new = ADAM_B2 * v + (1.0 - ADAM_B2) * (g * g)
    m_hat = m_new / (1.0 - ADAM_B1 ** ADAM_STEP)
    v_hat = v_new / (1.0 - ADAM_B2 ** ADAM_STEP)
    delta = -ADAM_LR * (m_hat / (jnp.sqrt(v_hat) + ADAM_EPS) + ADAM_WD * w)
    return delta, m_new, v_new


def _adam(name, pieces, w, m, v):
    r, c = w.shape
    n_piece, _, cp = pieces.shape
    tr = r
    for cand in (256, 176, 128, 64):
        if r % cand == 0 and r > cand:
            tr = cand
            break

    def body(p_ref, w_ref, m_ref, v_ref, g_ref, d_ref, mo_ref, vo_ref):
        g = p_ref[0, :, 0:c].astype(F32)
        for j in range(1, n_piece):
            g = g + p_ref[j, :, 0:c].astype(F32)
        delta, m_new, v_new = _adam_math(g, w_ref[...], m_ref[...], v_ref[...])
        g_ref[...] = g
        d_ref[...] = delta
        mo_ref[...] = m_new
        vo_ref[...] = v_new

    blk = pl.BlockSpec((tr, c), lambda i: (i, 0))
    osh = jax.ShapeDtypeStruct((r, c), F32)
    return pl.pallas_call(
        body, name=name, grid=(r // tr,),
        in_specs=[pl.BlockSpec((n_piece, tr, cp), lambda i: (0, i, 0)), blk, blk, blk],
        out_specs=[blk, blk, blk, blk], out_shape=[osh, osh, osh, osh],
        compiler_params=_params(("parallel",)),
    )(pieces, w, m, v)


def _adam_small(g_all, ws, ms, vs):
    n = len(ws)

    def body(*refs):
        g_ref, ins, outs = refs[0], refs[1:1 + 3 * n], refs[1 + 3 * n:]
        for i in range(n):
            g = g_ref[i:i + 1, :]
            delta, m_new, v_new = _adam_math(g, ins[i][...], ins[n + i][...], ins[2 * n + i][...])
            for kind, val in enumerate((g, delta, m_new, v_new)):
                outs[kind * n + i][...] = val

    osh = jax.ShapeDtypeStruct(ws[0].shape, F32)
    res = pl.pallas_call(body, name="adam_small", in_specs=[VMEM] * (1 + 3 * n), out_specs=[VMEM] * (4 * n),
                         out_shape=[osh] * (4 * n))(g_all, *ws, *ms, *vs)
    return res[:n], res[n:2 * n], res[2 * n:3 * n], res[3 * n:]


def _local_step(x, mem, pos, tgt, gains, w_in_shard, shards, batch):
    g_mix, g_mem_q, g_mem_kv, g_ffn, g_final = gains
    t, d = x.shape
    s = t // batch
    n_mem = mem.shape[0] // batch
    n_sh = N_DEV
    width = shards[0].shape[0]
    nb = width // LANES

    lane = np.arange(LANES) % HEAD_DIM
    sel_lo = (lane < ROPE_HALF).astype(np.float32)[None, :]
    sel_hi = ((lane >= ROPE_HALF) & (lane < 2 * ROPE_HALF)).astype(np.float32)[None, :]
    freqs = np.float32(ROPE_THETA) ** (-np.arange(ROPE_HALF, dtype=np.float32) / np.float32(ROPE_HALF))
    inv_freq = np.where(lane < 2 * ROPE_HALF, freqs[lane % ROPE_HALF], 0.0).astype(np.float32)[None, :]
    cos_t, sin_a, sin_b = _rope_tables(pos, jnp.asarray(inv_freq), jnp.asarray(sel_lo), jnp.asarray(sel_hi))
    bias = _dilated_bias_tiles(s)

    proj, w_in, n1 = _proj_in_gather(x, g_mix, w_in_shard)
    qk_a = _rope_apply("rope_fwd", [proj], 2 * width, cos_t, sin_a, sin_b, 1.0)
    cs_up = shards[0].shape[1]
    (o_a, lse_a), (w_up_a, w_up_b, w_out, w_q, w_kv, w_o, w_fd) = _da_fwd(
        qk_a, proj, 2 * nb, bias, batch, s,
        ride=(shards[:6] + shards[8:], TWO_LEVEL, (cs_up, cs_up, 0, 0, 0, cs_up, 0)))
    (o_b, tot_b), (w_fg, w_fu) = _sb_fwd(proj, 3 * nb, 4 * nb, 5 * nb, batch, s, ride=(shards[6:8], True, (0, 0)))
    w_out = w_out.reshape(d, d)
    w_q = w_q.reshape(d, -1)
    w_kv = w_kv.reshape(d, -1)
    w_fd = w_fd.reshape(-1, d)
    w_fg = w_fg.reshape(-1, d)
    w_fu = w_fu.reshape(-1, d)
    ua, ub, mixed, n2, h1, q_m = _mixer_fwd(o_a, o_b, w_up_a, w_up_b, proj, 6 * nb, w_out, x, g_mem_q, w_q)
    mem_n = _rms_fwd("norm_mem_kv", mem, g_mem_kv)
    kv_m = _mm_w("mem_kv", mem_n, w_kv, BF16)
    o_m = _mem_fwd(q_m, kv_m, batch, s, n_mem)
    h2, n3 = _mm_res_norm("mem_out", o_m, w_o, h1, g_ffn)
    hg, hu, act = _ffn_up(n3, w_fg, w_fu)
    loss_part, dh3, dh3_b, dg_final = _loss_head(act, w_fd, h2, tgt, g_final.reshape(1, d))

    dhg, dhu, dh2, dh2_b, dg_ffn, do_m = _ffn_bwd(dh3_b, w_fd, w_fg, w_fu, hg, hu, h2, g_ffn, dh3, w_o)
    gw_fd = _wgrad("gw_ffn_down", act, dh3_b)
    gw_fg = _wgrad("gw_ffn_gate", dhg, n3)
    gw_fu = _wgrad("gw_ffn_up", dhu, n3)

    gw_o = _wgrad("gw_mem_o", o_m, dh2_b)
    dq_m, dkv_m = _mem_bwd(q_m, kv_m, do_m, batch, s, n_mem)
    gw_q = _wgrad("gw_mem_q", n2, dq_m)
    gw_kv = _wgrad("gw_mem_kv", mem_n, dkv_m)
    (dg_mem_kv,) = _rms_bwd("norm_mem_kv_bwd", (dkv_m, w_kv, NT), mem, g_mem_kv, None, ())
    dh1, dh1_b, dg_mem_q = _rms_bwd("norm_mem_q_bwd", (dq_m, w_q, NT), h1, g_mem_q, dh2, ("f32", "bf16"))

    gw_out = _wgrad("gw_out", mixed, dh1_b)
    dua, dub, dgates, do_a, do_b = _mixer_bwd(dh1_b, w_out, ua, ub, proj, 6 * nb, w_up_a, w_up_b)
    gw_ua = _wgrad("gw_up_a", o_a, dua)
    gw_ub = _wgrad("gw_up_b", o_b, dub)
    (dq_ar, dk_ar, dv_a), (p_fg, p_fd) = _da_bwd(
        qk_a, proj, 2 * nb, bias, o_a, lse_a, do_a, batch, s,
        ride=([gw_fg.reshape(n_sh, -1, d), gw_fd.reshape(n_sh, -1, d)], False, (0, 0)))
    mid = [gw_ua, gw_ub, gw_out.reshape(n_sh, -1, d), gw_q.reshape(n_sh, -1, gw_q.shape[1]),
           gw_kv.reshape(n_sh, -1, gw_kv.shape[1]), gw_o, gw_fu.reshape(n_sh, -1, d)]
    (dq_b, dk_b, dv_b), (*p_mid, p_fu) = _sb_bwd(proj, 3 * nb, 4 * nb, 5 * nb, tot_b, do_b, batch, s,
                                                 ride=(mid, False, (cs_up, cs_up, 0, 0, 0, cs_up, 0)))
    p_ffn = [p_fg, p_fu, p_fd]
    dproj = _rope_apply("rope_bwd", [dq_ar, dk_ar], width, cos_t, sin_a, sin_b, -1.0,
                        tail=(dv_a, dq_b, dk_b, dv_b, dgates))
    grad_x, dg_mix = _rms_bwd("proj_in_bwd", (dproj, w_in, NT), x, g_mix, dh1, ("f32",))
    p_in = _gw_in_scatter(n1, dproj)
    return loss_part, grad_x, [p_in] + list(p_mid) + p_ffn, (dg_mix, dg_mem_q, dg_mem_kv, dg_ffn, dg_final)


WEIGHTS =("w_in", "w_up_a", "w_up_b", "w_out", "w_q_mem", "w_kv_mem", "w_o_mem", "w_ffn_gate", "w_ffn_up", "w_ffn_down")
GAINS = ("g_mix", "g_mem_q", "g_mem_kv", "g_ffn", "g_final")
ORDER = ("g_mix", "w_in", "w_up_a", "w_up_b", "w_out", "g_mem_q", "g_mem_kv", "w_q_mem", "w_kv_mem", "w_o_mem", "g_ffn",
         "w_ffn_gate", "w_ffn_up", "w_ffn_down", "g_final")


def kernel(x, mem, positions, g_mix, w_in, w_up_a, w_up_b, w_out, g_mem_q, g_mem_kv, w_q_mem, w_kv_mem, w_o_mem, g_ffn, w_ffn_gate, w_ffn_up, w_ffn_down, g_final, loss_target, m_g_mix, m_w_in, m_w_up_a, m_w_up_b, m_w_out, m_g_mem_q, m_g_mem_kv, m_w_q_mem, m_w_kv_mem, m_w_o_mem, m_g_ffn, m_w_ffn_gate, m_w_ffn_up, m_w_ffn_down, m_g_final, v_g_mix, v_w_in, v_w_up_a, v_w_up_b, v_w_out, v_g_mem_q, v_g_mem_kv, v_w_q_mem, v_w_kv_mem, v_w_o_mem, v_g_ffn, v_w_ffn_gate, v_w_ffn_up, v_w_ffn_down, v_g_final):
    given = dict(locals())
    batch, s, d = x.shape
    t = batch * s
    flipped = ("w_ffn_gate", "w_ffn_up")

    def view(a, n):
        a = a.reshape(a.shape[-2:])
        return a.T if n in flipped else a

    def unview(a, n):
        return (a.T if n in flipped else a).reshape(given[n].shape)

    shard = {n: view(given[n], n) for n in WEIGHTS}
    gains = [given[n].reshape(1, d) for n in GAINS]

    pad = (-shard["w_ffn_down"].shape[0]) % LANES
    cast = _cast_weights([shard[n] for n in WEIGHTS], [pad if n in flipped + ("w_ffn_down",) else 0 for n in WEIGHTS])
    loss_part, grad_x, pieces, dgains = _local_step(
        x.reshape(t, d), mem.reshape(-1, d), positions.reshape(t, 1), loss_target.reshape(t, d), gains, cast[0],
        cast[1:], batch)

    grad, delta, new_m, new_v = {}, {}, {}, {}
    for n, p in zip(WEIGHTS, pieces):
        outs = _adam("adam_" + n, p, shard[n], view(given["m_" + n], n), view(given["v_" + n], n))
        grad[n], delta[n], new_m[n], new_v[n] = [unview(o, n) for o in outs]

    g_all = _allreduce_small(list(dgains), loss_part)
    small = _adam_small(g_all, gains, [given["m_" + n].reshape(1, d) for n in GAINS],
                        [given["v_" + n].reshape(1, d) for n in GAINS])
    for out, vals in zip((grad, delta, new_m, new_v), small):
        for n, val in zip(GAINS, vals):
            out[n] = val.reshape(given[n].shape)

    loss = g_all[SMALL_ROWS - 1, 0]
    return (loss, grad_x.reshape(x.shape), *[grad[n] for n in ORDER], *[delta[n] for n in ORDER],
            *[new_m[n] for n in ORDER], *[new_v[n] for n in ORDER])
```

```python
import functools
import math

import jax
import jax.numpy as jnp
import numpy as np
from jax import lax
from jax.experimental import pallas as pl
from jax.experimental.pallas import tpu as pltpu

F32 = jnp.float32
BF16 = jnp.bfloat16

N_DEV = 8
HEAD_DIM = 64
MEM_HEAD_DIM = 128
N_HEADS_MEM = 4
BLOCK = 128
DIL_PATTERNS = ((128, 1), (512, 4), (2048, 16))
ROPE_THETA = 500000.0
ROPE_HALF = 8
RMS_EPS = 1e-6
ADAM_LR, ADAM_B1, ADAM_B2, ADAM_EPS, ADAM_WD, ADAM_STEP = 0.001, 0.9, 0.999, 1e-08, 0.01, 10
NEG = -1e30
ROW_TILE = 512
LANES = 128

ANY = pl.BlockSpec(memory_space=pl.ANY)
VMEM = pl.BlockSpec(memory_space=pltpu.VMEM)
NN = (((1,), (0,)), ((), ()))
NT = (((1,), (1,)), ((), ()))
TN = (((0,), (0,)), ((), ()))


def _params(sem):
    return pltpu.CompilerParams(dimension_semantics=sem)


def _mm(name, a, b, *, grid, a_spec, b_spec, o_shape, o_spec, dims, out_dtype):
    def body(a_ref, b_ref, o_ref):
        o_ref[...] = lax.dot_general(a_ref[...], b_ref[...], dims, preferred_element_type=F32).astype(out_dtype)

    return pl.pallas_call(
        body, name=name, grid=grid, in_specs=[a_spec, b_spec],
        out_specs=o_spec, out_shape=jax.ShapeDtypeStruct(o_shape, out_dtype),
        compiler_params=_params(("parallel",) * len(grid)),
    )(a, b)


def _rms_fwd(name, x, g):
    t, d = x.shape
    tm = min(ROW_TILE, t)

    def body(x_ref, g_ref, o_ref):
        xf = x_ref[...]
        r = lax.rsqrt(jnp.mean(xf * xf, axis=-1, keepdims=True) + RMS_EPS)
        o_ref[...] = (xf * r * g_ref[...]).astype(BF16)

    return pl.pallas_call(
        body, name=name, grid=(t // tm,),
        in_specs=[pl.BlockSpec((tm, d), lambda i: (i, 0)), pl.BlockSpec((1, d), lambda i: (0, 0))],
        out_specs=pl.BlockSpec((tm, d), lambda i: (i, 0)), out_shape=jax.ShapeDtypeStruct((t, d), BF16),
        compiler_params=_params(("parallel",)),
    )(x, g)


def _rms_bwd_rows(dnf, xf, gv, res):
    r = lax.rsqrt(jnp.mean(xf * xf, axis=-1, keepdims=True) + RMS_EPS)
    xh = xf * r
    dxh = dnf * gv
    dx = r * (dxh - xh * jnp.mean(dxh * xh, axis=-1, keepdims=True))
    if res is not None:
        dx = dx + res
    return dx, jnp.sum(dnf * xh, axis=0, keepdims=True)


def _rms_bwd(name, dn, x, g, dres, want):
    t, d = x.shape
    has_res = dres is not None
    lhs = list(dn) if isinstance(dn, tuple) else [dn]
    n_lhs = len(lhs[:2])
    tm = min(MM_ROWS if lhs[0].shape[1] <= d else ROW_TILE, t)

    def body(*refs):
        x_ref, g_ref = refs[n_lhs], refs[n_lhs + 1]
        r_ref = refs[n_lhs + 2] if has_res else None
        dx_refs, dg_ref = refs[-1 - len(want):-1], refs[-1]
        if n_lhs == 2:
            dnf = lax.dot_general(refs[0][...], refs[1][...], lhs[2], preferred_element_type=F32)
        else:
            dnf = refs[0][...].astype(F32)
        dx, dg = _rms_bwd_rows(dnf, x_ref[...], g_ref[...], r_ref[...] if has_res else None)
        for kind, dx_ref in zip(want, dx_refs):
            dx_ref[...] = dx.astype(F32 if kind == "f32" else BF16)

        @pl.when(pl.program_id(0) == 0)
        def _():
            dg_ref[...] = jnp.zeros_like(dg_ref)

        dg_ref[...] += dg

    row = pl.BlockSpec((tm, d), lambda i: (i, 0))
    vec = pl.BlockSpec((1, d), lambda i: (0, 0))
    if n_lhs == 2:
        first = [pl.BlockSpec((tm, lhs[0].shape[1]), lambda i: (i, 0)), pl.BlockSpec(lhs[1].shape, lambda i: (0, 0))]
    else:
        first = [row]
    return pl.pallas_call(
        body, name=name, grid=(t // tm,),
        in_specs=first + [row, vec] + ([row] if has_res else []),
        out_specs=[row] * len(want) + [vec],
        out_shape=[jax.ShapeDtypeStruct((t, d), F32 if kind == "f32" else BF16) for kind in want]
        + [jax.ShapeDtypeStruct((1, d), F32)],
        compiler_params=_params(("arbitrary",)),
    )(*(lhs[:2] + [x, g] + ([dres] if has_res else [])))


def _loss_head(a, w, res, tgt, g):
    t, d = res.shape
    k = a.shape[1]
    tm = min(ROW_TILE, t)

    def body(a_ref, w_ref, r_ref, t_ref, g_ref, loss_ref, dh_ref, dhb_ref, dg_ref):
        xf = lax.dot_general(a_ref[...], w_ref[...], NN, preferred_element_type=F32) + r_ref[...]
        gv = g_ref[...]
        r = lax.rsqrt(jnp.mean(xf * xf, axis=-1, keepdims=True) + RMS_EPS)
        xh = xf * r
        e = xh * gv - t_ref[...]
        dy = e * (1.0 / d)
        dxh = dy * gv
        dh = r * (dxh - xh * jnp.mean(dxh * xh, axis=-1, keepdims=True))
        dh_ref[...] = dh
        dhb_ref[...] = dh.astype(BF16)

        @pl.when(pl.program_id(0) == 0)
        def _():
            dg_ref[...] = jnp.zeros_like(dg_ref)
            loss_ref[...] = jnp.zeros_like(loss_ref)

        dg_ref[...] += jnp.sum(dy * xh, axis=0, keepdims=True)
        part = jnp.sum(jnp.sum(e * e, axis=1, keepdims=True), axis=0, keepdims=True) * (0.5 / d)
        loss_ref[...] += jnp.broadcast_to(part, loss_ref.shape)

    row = pl.BlockSpec((tm, d), lambda i: (i, 0))
    vec = pl.BlockSpec((1, d), lambda i: (0, 0))
    return pl.pallas_call(
        body, name="loss_head", grid=(t // tm,),
        in_specs=[pl.BlockSpec((tm, k), lambda i: (i, 0)), pl.BlockSpec((k, d), lambda i: (0, 0)), row, row, vec],
        out_specs=[pl.BlockSpec((8, LANES), lambda i: (0, 0)), row, row, vec],
        out_shape=[jax.ShapeDtypeStruct((8, LANES), F32), jax.ShapeDtypeStruct((t, d), F32),
                   jax.ShapeDtypeStruct((t, d), BF16), jax.ShapeDtypeStruct((1, d), F32)],
        compiler_params=_params(("arbitrary",)),
    )(a, w, res, tgt, g)


def _rope_tables(pos, inv_freq, sel_lo, sel_hi):
    t = pos.shape[0]
    tm = min(MM_ROWS, t)

    def body(p_ref, f_ref, lo_ref, hi_ref, c_ref, sa_ref, sb_ref):
        ang = p_ref[...].astype(F32) * f_ref[...]
        rot = lo_ref[...] + hi_ref[...]
        cs, sn = jnp.cos(ang), jnp.sin(ang)
        c_ref[...] = cs * rot + (1.0 - rot)
        sa_ref[...] = -sn * lo_ref[...]
        sb_ref[...] = sn * hi_ref[...]

    vec = pl.BlockSpec((1, LANES), lambda i: (0, 0))
    row = pl.BlockSpec((tm, LANES), lambda i: (i, 0))
    return pl.pallas_call(
        body, name="rope_tables", grid=(t // tm,),
        in_specs=[pl.BlockSpec((tm, 1), lambda i: (i, 0)), vec, vec, vec],
        out_specs=[row, row, row], out_shape=[jax.ShapeDtypeStruct((t, LANES), F32)] * 3,
        compiler_params=_params(("parallel",)),
    )(pos, inv_freq, sel_lo, sel_hi)


def _rope_apply(name, srcs, width, cos_t, sin_a, sin_b, sign, tail=()):
    t = srcs[0].shape[0]
    tm = min(MM_ROWS, t)
    n_cols = width // LANES
    n_src = len(srcs)

    def body(*refs):
        x_refs, tail_refs = refs[:n_src], refs[n_src:n_src + len(tail)]
        c_ref, sa_ref, sb_ref, o_ref = refs[n_src + len(tail):]
        cs, sa, sb = c_ref[...], sign * sa_ref[...], sign * sb_ref[...]
        for a, x_ref in enumerate(x_refs):
            for c in range(n_cols):
                xf = x_ref[:, c * LANES:(c + 1) * LANES].astype(F32)
                up = pltpu.roll(xf, LANES - ROPE_HALF, 1)
                dn = pltpu.roll(xf, ROPE_HALF, 1)
                o_ref[:, a * width + c * LANES:a * width + (c + 1) * LANES] = (xf * cs + up * sa + dn * sb).astype(BF16)
        col = n_src * width
        for t_ref in tail_refs:
            o_ref[:, col:col + t_ref.shape[1]] = t_ref[...]
            col += t_ref.shape[1]

    wide = n_src * width + sum(a.shape[1] for a in tail)
    tab = pl.BlockSpec((tm, LANES), lambda i: (i, 0))
    return pl.pallas_call(
        body, name=name, grid=(t // tm,),
        in_specs=[pl.BlockSpec((tm, width), lambda i: (i, 0))] * n_src
        + [pl.BlockSpec((tm, a.shape[1]), lambda i: (i, 0)) for a in tail] + [tab, tab, tab],
        out_specs=pl.BlockSpec((tm, wide), lambda i: (i, 0)),
        out_shape=jax.ShapeDtypeStruct((t, wide), BF16),
        compiler_params=_params(("parallel",)),
    )(*srcs, *tail, cos_t, sin_a, sin_b)


DA_T = 256
MIX_STREAMS = 4
SB_BWD_STREAMS = 2


def _lane_lo():
    return lax.broadcasted_iota(jnp.int32, (BLOCK, LANES), 1) < HEAD_DIM


def _dilated_bias_tiles(s):
    n = s // DA_T
    dist = (np.arange(n)[:, None, None] * DA_T + np.arange(DA_T)[None, :, None] - np.arange(DA_T)[None, None, :])
    cnt = np.zeros(dist.shape, np.float32)
    for window, dil in DIL_PATTERNS:
        cnt += ((dist >= 0) & (dist % dil == 0) & (dist <= window)).astype(np.float32)
    return jnp.asarray(np.where(cnt > 0, np.log(np.maximum(cnt, 1.0)), NEG).astype(np.float32))


def _stack_heads(x, lo):
    zero = jnp.zeros_like(x)
    return jnp.concatenate([jnp.where(lo, x, zero), jnp.where(lo, zero, x)], axis=0)


def _da_fwd(qk, proj, v_col0, bias, batch, s, ride=None, streams=MIX_STREAMS):
    t = qk.shape[0]
    nq = s // DA_T
    n_pairs = 4
    ns = streams
    wide = ns * LANES
    scale = HEAD_DIM ** -0.5

    def body(q_ref, k_ref, v_ref, b_ref, o_ref, lse_ref, acc_ref, m_ref, l_ref):
        i = pl.program_id(2)
        lo = lax.broadcasted_iota(jnp.int32, (DA_T, LANES), 1) < HEAD_DIM
        ones = jnp.ones((DA_T, LANES), BF16)
        acc_ref[...] = jnp.zeros_like(acc_ref)
        m_ref[...] = jnp.full(m_ref.shape, NEG, F32)
        l_ref[...] = jnp.zeros_like(l_ref)
        qqs = [_stack_heads(q_ref[:, st * LANES:(st + 1) * LANES] * scale, lo) for st in range(ns)]

        def scores(st, rows, bias2):
            k = k_ref[rows, st * LANES:(st + 1) * LANES]
            return lax.dot_general(qqs[st], k, NT, preferred_element_type=F32) + bias2

        def softmax(st, sc):
            m_old = m_ref[st]
            m_new = jnp.maximum(m_old, jnp.broadcast_to(jnp.max(sc, axis=1, keepdims=True), m_old.shape))
            m_ref[st] = m_new
            return jnp.exp(sc - jnp.concatenate([m_new, m_new], axis=1)).astype(BF16), jnp.exp(m_old - m_new)

        def values(st, rows, p, alpha):
            v = v_ref[rows, st * LANES:(st + 1) * LANES]
            vz = jnp.zeros_like(v)
            l_ref[st] = alpha * l_ref[st] + lax.dot_general(p, ones, NN, preferred_element_type=F32)
            pv = (lax.dot_general(p[:DA_T], jnp.where(lo, v, vz), NN, preferred_element_type=F32)
                  + lax.dot_general(p[DA_T:], jnp.where(lo, vz, v), NN, preferred_element_type=F32))
            acc_ref[st] = acc_ref[st] * jnp.where(lo, alpha[:DA_T], alpha[DA_T:]) + pv

        def trip(dlt, carry):
            rows = pl.ds(pl.multiple_of((i - dlt) * DA_T, DA_T), DA_T)
            bias_t = b_ref[dlt]
            bias2 = jnp.concatenate([bias_t, bias_t], axis=0)
            scs = [scores(st, rows, bias2) for st in range(ns)]
            pas = [softmax(st, scs[st]) for st in range(ns)]
            for st in range(ns):
                values(st, rows, *pas[st])
            return carry

        lax.fori_loop(0, i + 1, trip, 0)
        for st in range(ns):
            cols = slice(st * LANES, (st + 1) * LANES)
            l_t = l_ref[st]
            o_ref[:, cols] = (acc_ref[st] / jnp.where(lo, l_t[:DA_T], l_t[DA_T:])).astype(BF16)
            lse = m_ref[st] + jnp.log(l_t)
            lse_ref[:, cols] = jnp.where(lo, lse[:DA_T], lse[DA_T:])

    blk = pl.BlockSpec((DA_T, wide), lambda b, h, i: (b * nq + i, h))
    return _call(
        body, name="attn_a_fwd", grid=(batch, n_pairs // ns, nq),
        in_specs=[blk,
                  pl.BlockSpec((s, wide), lambda b, h, i: (b, n_pairs // ns + h)),
                  pl.BlockSpec((s, wide), lambda b, h, i: (b, v_col0 // ns + h)),
                  pl.BlockSpec((nq, DA_T, DA_T), lambda b, h, i: (0, 0, 0))],
        out_specs=[blk, blk],
        out_shape=[jax.ShapeDtypeStruct((t, n_pairs * LANES), BF16), jax.ShapeDtypeStruct((t, n_pairs * LANES), F32)],
        scratch=[pltpu.VMEM((ns, DA_T, LANES), F32), pltpu.VMEM((ns, 2 * DA_T, LANES), F32),
                 pltpu.VMEM((ns, 2 * DA_T, LANES), F32)],
        sem=("parallel", "parallel", "arbitrary"), args=(qk, qk, proj, bias), ride=ride)


def _da_bwd(qk, proj, v_col0, bias, o, lse, do, batch, s, ride=None, streams=MIX_STREAMS):
    t = qk.shape[0]
    nq = s // DA_T
    n_pairs = 4
    ns = streams
    wide = ns * LANES
    scale = HEAD_DIM ** -0.5

    def body(q_ref, k_ref, v_ref, b_ref, o_ref, lse_ref, do_ref, dq_ref, dk_ref, dv_ref, dk_acc, dv_acc, dq_acc):
        i = pl.program_id(2)
        lo = lax.broadcasted_iota(jnp.int32, (DA_T, LANES), 1) < HEAD_DIM

        @pl.when(i == 0)
        def _():
            dk_acc[...] = jnp.zeros_like(dk_acc)
            dv_acc[...] = jnp.zeros_like(dv_acc)

        dq_acc[...] = jnp.zeros_like(dq_acc)
        qqs, dds, deltas, lses = [], [], [], []
        for st in range(ns):
            cols = slice(st * LANES, (st + 1) * LANES)
            do_ = do_ref[:, cols]
            qqs.append(_stack_heads(q_ref[:, cols] * scale, lo))
            dds.append(_stack_heads(do_, lo))
            prod = do_.astype(F32) * o_ref[:, cols].astype(F32)
            fz = jnp.zeros_like(prod)
            deltas.append(jnp.concatenate([jnp.sum(jnp.where(lo, prod, fz), axis=1, keepdims=True),
                                           jnp.sum(jnp.where(lo, fz, prod), axis=1, keepdims=True)], axis=0))
            lse_t = lse_ref[:, cols]
            lses.append(jnp.concatenate([lse_t[:, 0:1], lse_t[:, HEAD_DIM:HEAD_DIM + 1]], axis=0))

        def products(st, rows, bias2):
            cols = slice(st * LANES, (st + 1) * LANES)
            sc = lax.dot_general(qqs[st], k_ref[rows, cols], NT, preferred_element_type=F32) + bias2
            return sc, lax.dot_general(dds[st], v_ref[rows, cols], NT, preferred_element_type=F32)

        def weights(st, sc, dp):
            p = jnp.exp(sc - lses[st])
            return (p * (dp - deltas[st])).astype(BF16), p.astype(BF16)

        def gradients(st, rows, ds, p):
            cols = slice(st * LANES, (st + 1) * LANES)
            k = k_ref[rows, cols]
            kz = jnp.zeros_like(k)
            dq_acc[st] += (lax.dot_general(ds[:DA_T], jnp.where(lo, k, kz), NN, preferred_element_type=F32)
                           + lax.dot_general(ds[DA_T:], jnp.where(lo, kz, k), NN, preferred_element_type=F32))
            dk_acc[rows, cols] += lax.dot_general(ds, qqs[st], TN, preferred_element_type=F32)
            dv_acc[rows, cols] += lax.dot_general(p, dds[st], TN, preferred_element_type=F32)

        def trip(dlt, carry):
            rows = pl.ds(pl.multiple_of((i - dlt) * DA_T, DA_T), DA_T)
            bias_t = b_ref[dlt]
            bias2 = jnp.concatenate([bias_t, bias_t], axis=0)
            prods = [products(st, rows, bias2) for st in range(ns)]
            wts = [weights(st, *prods[st]) for st in range(ns)]
            for st in range(ns):
                gradients(st, rows, *wts[st])
            return carry

        lax.fori_loop(0, i + 1, trip, 0)
        for st in range(ns):
            dq_ref[:, st * LANES:(st + 1) * LANES] = (dq_acc[st] * scale).astype(BF16)

        @pl.when(i == nq - 1)
        def _():
            dk_ref[...] = dk_acc[...].astype(BF16)
            dv_ref[...] = dv_acc[...].astype(BF16)

    blk = pl.BlockSpec((DA_T, wide), lambda b, h, i: (b * nq + i, h))
    seq = pl.BlockSpec((s, wide), lambda b, h, i: (b, h), pipeline_mode=pl.Buffered(1))
    one = pl.Buffered(1)
    out = jax.ShapeDtypeStruct((t, n_pairs * LANES), BF16)
    return _call(
        body, name="attn_a_bwd", grid=(batch, n_pairs // ns, nq),
        in_specs=[blk,
                  pl.BlockSpec((s, wide), lambda b, h, i: (b, n_pairs // ns + h), pipeline_mode=one),
                  pl.BlockSpec((s, wide), lambda b, h, i: (b, v_col0 // ns + h), pipeline_mode=one),
                  pl.BlockSpec((nq, DA_T, DA_T), lambda b, h, i: (0, 0, 0), pipeline_mode=one),
                  blk, blk, blk],
        out_specs=[blk, seq, seq], out_shape=[out, out, out],
        scratch=[pltpu.VMEM((s, wide), F32), pltpu.VMEM((s, wide), F32), pltpu.VMEM((ns, DA_T, LANES), F32)],
        sem=("parallel", "parallel", "arbitrary"), args=(qk, qk, proj, bias, o, lse, do), ride=ride)


SB_Q = 256


def _sb_consts(after):
    r = lax.broadcasted_iota(jnp.int32, (2 * BLOCK, 2 * BLOCK), 0) % BLOCK
    c = lax.broadcasted_iota(jnp.int32, (2 * BLOCK, 2 * BLOCK), 1)
    tri = (r > c) if after else (r < c)
    return jnp.logical_or(c >= BLOCK, tri).astype(BF16)


def _split(x):
    hi = x.astype(BF16)
    lo = (x - hi.astype(F32)).astype(BF16)
    return jnp.concatenate([hi, lo], axis=1)


def _sb_fwd(proj, q_col0, k_col0, v_col0, batch, s, ride=None, streams=MIX_STREAMS):
    t = proj.shape[0]
    nq = s // SB_Q
    n_pairs = 4
    ns = streams
    wide = ns * LANES
    scale = HEAD_DIM ** -0.5

    def body(q_ref, k_ref, v_ref, o_ref, tot_ref, acc_ref, run_ref):
        i = pl.program_id(2)
        lo_q = lax.broadcasted_iota(jnp.int32, (SB_Q, LANES), 1) < HEAD_DIM
        lo_k = _lane_lo()
        mat = _sb_consts(True)
        row = lax.broadcasted_iota(jnp.int32, (2 * SB_Q, LANES), 0) % SB_Q
        ahead = row - lax.broadcasted_iota(jnp.int32, (2 * SB_Q, LANES), 1)
        acc_ref[...] = jnp.zeros_like(acc_ref)
        run_ref[...] = jnp.zeros_like(run_ref)
        qqs = [_stack_heads(q_ref[:, st * LANES:(st + 1) * LANES] * scale, lo_q) for st in range(ns)]

        def units(todo):
            def rows(j):
                return pl.ds(pl.multiple_of(j * BLOCK, BLOCK), BLOCK)

            zs = [lax.dot_general(qqs[st], k_ref[rows(j), st * LANES:(st + 1) * LANES], NT, preferred_element_type=F32)
                  for st, j, _ in todo]
            logs = []
            for z, (_, _, off) in zip(zs, todo):
                lsig = jnp.minimum(z, 0.0) - jnp.log(1.0 + jnp.exp(-jnp.abs(z)))
                lneg = lsig - z
                if off is not None:
                    lneg = jnp.where(ahead > off, lneg, 0.0)
                logs.append((lsig, _split(lneg)))
            sums = [lax.dot_general(cat, mat, NN, preferred_element_type=F32) for _, cat in logs]
            probs = []
            for (lsig, _), sm, (st, _, off) in zip(logs, sums, todo):
                run = run_ref[st]
                a = jnp.exp(lsig + run + sm[:, :BLOCK])
                if off is not None:
                    a = jnp.where(ahead > off, a, 0.0)
                run_ref[st] = run + sm[:, BLOCK:]
                probs.append(a.astype(BF16))
            for ab, (st, j, _) in zip(probs, todo):
                v = v_ref[rows(j), st * LANES:(st + 1) * LANES]
                vz = jnp.zeros_like(v)
                acc_ref[st] += (lax.dot_general(ab[:SB_Q], jnp.where(lo_k, v, vz), NN, preferred_element_type=F32)
                                + lax.dot_general(ab[SB_Q:], jnp.where(lo_k, vz, v), NN, preferred_element_type=F32))

        units([(st, 2 * i + 1, BLOCK) for st in range(ns)] + [(st, 2 * i, 0) for st in range(ns)])

        def pair(p, carry):
            jp = i - 1 - p
            units([(st, 2 * jp + 1, None) for st in range(ns)] + [(st, 2 * jp, None) for st in range(ns)])
            return carry

        lax.fori_loop(0, i, pair, 0)
        for st in range(ns):
            cols = slice(st * LANES, (st + 1) * LANES)
            o_ref[:, cols] = acc_ref[st].astype(BF16)
            tot_ref[:, cols] = jnp.where(lo_q, run_ref[st, 0:SB_Q, :], run_ref[st, SB_Q:2 * SB_Q, :])

    def seq(col0):
        return pl.BlockSpec((s, wide), lambda b, h, i: (b, col0 // ns + h))

    blk = pl.BlockSpec((SB_Q, wide), lambda b, h, i: (b * nq + i, h))
    return _call(
        body, name="attn_b_fwd", grid=(batch, n_pairs // ns, nq),
        in_specs=[pl.BlockSpec((SB_Q, wide), lambda b, h, i: (b * nq + i, q_col0 // ns + h)), seq(k_col0), seq(v_col0)],
        out_specs=[blk, blk],
        out_shape=[jax.ShapeDtypeStruct((t, n_pairs * LANES), BF16), jax.ShapeDtypeStruct((t, n_pairs * LANES), F32)],
        scratch=[pltpu.VMEM((ns, SB_Q, LANES), F32), pltpu.VMEM((ns, 2 * SB_Q, LANES), F32)],
        sem=("parallel", "parallel", "arbitrary"), args=(proj, proj, proj), ride=ride)


def _sb_bwd(proj, q_col0, k_col0, v_col0, tot, do, batch, s, ride=None, streams=SB_BWD_STREAMS):
    t = proj.shape[0]
    nq = s // SB_Q
    n_pairs = 4
    ns = streams
    wide = ns * LANES
    scale = HEAD_DIM ** -0.5

    def body(q_ref, k_ref, v_ref, tot_ref, do_ref, dq_ref, dk_ref, dv_ref, dk_acc, dv_acc, dq_acc, seen_ref, gsum_ref):
        i = pl.program_id(2)
        lo_q = lax.broadcasted_iota(jnp.int32, (SB_Q, LANES), 1) < HEAD_DIM
        lo_k = _lane_lo()

        @pl.when(i == 0)
        def _():
            dk_acc[...] = jnp.zeros_like(dk_acc)
            dv_acc[...] = jnp.zeros_like(dv_acc)

        mat_after = _sb_consts(True)
        mat_before = _sb_consts(False)[:BLOCK]
        row = lax.broadcasted_iota(jnp.int32, (2 * SB_Q, LANES), 0) % SB_Q
        ahead = row - lax.broadcasted_iota(jnp.int32, (2 * SB_Q, LANES), 1)
        dq_acc[...] = jnp.zeros_like(dq_acc)
        seen_ref[...] = jnp.zeros_like(seen_ref)
        gsum_ref[...] = jnp.zeros_like(gsum_ref)
        qqs, dds, totals = [], [], []
        for st in range(ns):
            cols = slice(st * LANES, (st + 1) * LANES)
            qqs.append(_stack_heads(q_ref[:, cols] * scale, lo_q))
            dds.append(_stack_heads(do_ref[:, cols], lo_q))
            tot_t = tot_ref[:, cols]
            totals.append(jnp.concatenate([jnp.broadcast_to(tot_t[:, 0:1], (SB_Q, LANES)),
                                           jnp.broadcast_to(tot_t[:, HEAD_DIM:HEAD_DIM + 1], (SB_Q, LANES))], axis=0))

        def units(todo):
            def rows(j):
                return pl.ds(pl.multiple_of(j * BLOCK, BLOCK), BLOCK)

            def cols(st):
                return slice(st * LANES, (st + 1) * LANES)

            prods = [(lax.dot_general(qqs[st], k_ref[rows(j), cols(st)], NT, preferred_element_type=F32),
                      lax.dot_general(dds[st], v_ref[rows(j), cols(st)], NT, preferred_element_type=F32))
                     for st, j, _ in todo]
            logs = []
            for (z, _), (_, _, off) in zip(prods, todo):
                lsig = jnp.minimum(z, 0.0) - jnp.log(1.0 + jnp.exp(-jnp.abs(z)))
                lneg = lsig - z
                if off is not None:
                    lneg = jnp.where(ahead > off, lneg, 0.0)
                logs.append((lsig, _split(lneg)))
            sums = [lax.dot_general(cat, mat_after, NN, preferred_element_type=F32) for _, cat in logs]
            gates = []
            for (lsig, _), sm, (_, da), (st, _, off) in zip(logs, sums, prods, todo):
                seen = seen_ref[st]
                a = jnp.exp(lsig + (totals[st] - seen - sm[:, BLOCK:]) + sm[:, :BLOCK])
                if off is not None:
                    a = jnp.where(ahead > off, a, 0.0)
                seen_ref[st] = seen + sm[:, BLOCK:]
                g = a * da
                gates.append((a.astype(BF16), g, g.astype(BF16)))
            gsums = [lax.dot_general(cat, mat_before, NN, preferred_element_type=F32) for _, _, cat in gates]
            outs = []
            for (lsig, _), (ab, g, _), gs, (st, _, off) in zip(logs, gates, gsums, todo):
                gsum = gsum_ref[st]
                dz = g - jnp.exp(lsig) * (g + gsum + gs[:, :BLOCK])
                if off is not None:
                    dz = jnp.where(ahead > off, dz, 0.0)
                gsum_ref[st] = gsum + gs[:, BLOCK:]
                outs.append((dz.astype(BF16), ab))
            for (dzb, ab), (st, j, _) in zip(outs, todo):
                k = k_ref[rows(j), cols(st)]
                kz = jnp.zeros_like(k)
                dq_acc[st] += (lax.dot_general(dzb[:SB_Q], jnp.where(lo_k, k, kz), NN, preferred_element_type=F32)
                               + lax.dot_general(dzb[SB_Q:], jnp.where(lo_k, kz, k), NN, preferred_element_type=F32))
                dk_acc[rows(j), cols(st)] += lax.dot_general(dzb, qqs[st], TN, preferred_element_type=F32)
                dv_acc[rows(j), cols(st)] += lax.dot_general(ab, dds[st], TN, preferred_element_type=F32)

        def pair(p, carry):
            units([(st, 2 * p, None) for st in range(ns)] + [(st, 2 * p + 1, None) for st in range(ns)])
            return carry

        lax.fori_loop(0, i, pair, 0)
        units([(st, 2 * i, 0) for st in range(ns)] + [(st, 2 * i + 1, BLOCK) for st in range(ns)])
        for st in range(ns):
            dq_ref[:, st * LANES:(st + 1) * LANES] = (dq_acc[st] * scale).astype(BF16)

        @pl.when(i == nq - 1)
        def _():
            dk_ref[...] = dk_acc[...].astype(BF16)
            dv_ref[...] = dv_acc[...].astype(BF16)

    def seq_in(col0):
        return pl.BlockSpec((s, wide), lambda b, h, i: (b, col0 // ns + h))

    blk = pl.BlockSpec((SB_Q, wide), lambda b, h, i: (b * nq + i, h))
    seq = pl.BlockSpec((s, wide), lambda b, h, i: (b, h))
    out = jax.ShapeDtypeStruct((t, n_pairs * LANES), BF16)
    return _call(
        body, name="attn_b_bwd", grid=(batch, n_pairs // ns, nq),
        in_specs=[pl.BlockSpec((SB_Q, wide), lambda b, h, i: (b * nq + i, q_col0 // ns + h)), seq_in(k_col0),
                  seq_in(v_col0), blk, blk],
        out_specs=[blk, seq, seq], out_shape=[out, out, out],
        scratch=[pltpu.VMEM((s, wide), F32), pltpu.VMEM((s, wide), F32), pltpu.VMEM((ns, SB_Q, LANES), F32),
                 pltpu.VMEM((ns, 2 * SB_Q, LANES), F32), pltpu.VMEM((ns, 2 * SB_Q, LANES), F32)],
        sem=("parallel", "parallel", "arbitrary"), args=(proj, proj, proj, tot, do), ride=ride)


MEM_Q_TILE = 1024


def _mem_fwd(q, kv, batch, s, n_mem):
    t, width = q.shape
    tq = min(MEM_Q_TILE, s)
    nq = s // tq
    scale = MEM_HEAD_DIM ** -0.5

    def body(q_ref, kv_ref, o_ref):
        for h in range(N_HEADS_MEM):
            cols = slice(h * MEM_HEAD_DIM, (h + 1) * MEM_HEAD_DIM)
            k = kv_ref[:, cols]
            v = kv_ref[:, width + h * MEM_HEAD_DIM: width + (h + 1) * MEM_HEAD_DIM]
            sc = lax.dot_general(q_ref[:, cols], k, NT, preferred_element_type=F32) * scale
            p = jnp.exp(sc - jnp.max(sc, axis=1, keepdims=True))
            p = p / jnp.sum(p, axis=1, keepdims=True)
            o_ref[:, cols] = lax.dot_general(p.astype(BF16), v, NN, preferred_element_type=F32).astype(BF16)

    return pl.pallas_call(
        body, name="mem_attn_fwd", grid=(batch, nq),
        in_specs=[pl.BlockSpec((tq, width), lambda b, i: (b * nq + i, 0)),
                  pl.BlockSpec((n_mem, 2 * width), lambda b, i: (b, 0))],
        out_specs=pl.BlockSpec((tq, width), lambda b, i: (b * nq + i, 0)),
        out_shape=jax.ShapeDtypeStruct((t, width), BF16),
        compiler_params=_params(("parallel", "parallel")),
    )(q, kv)


def _mem_bwd(q, kv, do, batch, s, n_mem):
    t, width = q.shape
    tq = min(MEM_Q_TILE, s)
    nq = s // tq
    scale = MEM_HEAD_DIM ** -0.5

    def body(q_ref, kv_ref, do_ref, dq_ref, dkv_ref, acc):
        i = pl.program_id(1)

        @pl.when(i == 0)
        def _():
            acc[...] = jnp.zeros_like(acc)

        for h in range(N_HEADS_MEM):
            cols = slice(h * MEM_HEAD_DIM, (h + 1) * MEM_HEAD_DIM)
            vcols = slice(width + h * MEM_HEAD_DIM, width + (h + 1) * MEM_HEAD_DIM)
            qh, k, v, doh = q_ref[:, cols], kv_ref[:, cols], kv_ref[:, vcols], do_ref[:, cols]
            sc = lax.dot_general(qh, k, NT, preferred_element_type=F32) * scale
            p = jnp.exp(sc - jnp.max(sc, axis=1, keepdims=True))
            p = p / jnp.sum(p, axis=1, keepdims=True)
            dp = lax.dot_general(doh, v, NT, preferred_element_type=F32)
            ds = (p * (dp - jnp.sum(p * dp, axis=1, keepdims=True)) * scale).astype(BF16)
            dq_ref[:, cols] = lax.dot_general(ds, k, NN, preferred_element_type=F32).astype(BF16)
            acc[:, cols] += lax.dot_general(ds, qh, TN, preferred_element_type=F32)
            acc[:, vcols] += lax.dot_general(p.astype(BF16), doh, TN, preferred_element_type=F32)

        @pl.when(i == nq - 1)
        def _():
            dkv_ref[...] = acc[...].astype(BF16)

    row = pl.BlockSpec((tq, width), lambda b, i: (b * nq + i, 0))
    kvs = pl.BlockSpec((n_mem, 2 * width), lambda b, i: (b, 0))
    return pl.pallas_call(
        body, name="mem_attn_bwd", grid=(batch, nq),
        in_specs=[row, kvs, row], out_specs=[row, kvs],
        out_shape=[jax.ShapeDtypeStruct((t, width), BF16), jax.ShapeDtypeStruct((batch * n_mem, 2 * width), BF16)],
        scratch_shapes=[pltpu.VMEM((n_mem, 2 * width), F32)],
        compiler_params=_params(("parallel", "arbitrary")),
    )(q, kv, do)


def _mixer_fwd(o_a, o_b, w_a, w_b, proj, gate_col0, w_out, x, g, w_q):
    t, width = o_a.shape
    d = w_a.shape[1]
    nq_cols = w_q.shape[1]
    tm = min(ROW_TILE, t)
    gb0 = gate_col0 * LANES // d

    def body(oa_ref, ob_ref, wa_ref, wb_ref, ga_ref, gb_ref, wo_ref, x_ref, g_ref, wq_ref, ua_ref, ub_ref, mix_ref,
             n_ref, h_ref, q_ref):
        ua = lax.dot_general(oa_ref[...], wa_ref[...], NN, preferred_element_type=F32)
        ub = lax.dot_general(ob_ref[...], wb_ref[...], NN, preferred_element_type=F32)
        ua_ref[...] = ua.astype(BF16)
        ub_ref[...] = ub.astype(BF16)
        mixed = (jax.nn.sigmoid(ga_ref[...].astype(F32)) * ua + jax.nn.sigmoid(gb_ref[...].astype(F32)) * ub).astype(BF16)
        mix_ref[...] = mixed
        h = lax.dot_general(mixed, wo_ref[...], NN, preferred_element_type=F32) + x_ref[...]
        h_ref[...] = h
        r = lax.rsqrt(jnp.mean(h * h, axis=-1, keepdims=True) + RMS_EPS)
        n = (h * r * g_ref[...]).astype(BF16)
        n_ref[...] = n
        q_ref[...] = lax.dot_general(n, wq_ref[...], NN, preferred_element_type=F32).astype(BF16)

    row = pl.BlockSpec((tm, width), lambda i: (i, 0))
    wsp = pl.BlockSpec((width, d), lambda i: (0, 0))
    out = pl.BlockSpec((tm, d), lambda i: (i, 0))
    osh = jax.ShapeDtypeStruct((t, d), BF16)
    return pl.pallas_call(
        body, name="mixer_fwd", grid=(t // tm,),
        in_specs=[row, row, wsp, wsp,
                  pl.BlockSpec((tm, d), lambda i: (i, gb0)), pl.BlockSpec((tm, d), lambda i: (i, gb0 + 1)),
                  pl.BlockSpec((d, d), lambda i: (0, 0)), out, pl.BlockSpec((1, d), lambda i: (0, 0)),
                  pl.BlockSpec((d, nq_cols), lambda i: (0, 0))],
        out_specs=[out, out, out, out, out, pl.BlockSpec((tm, nq_cols), lambda i: (i, 0))],
        out_shape=[osh, osh, osh, osh, jax.ShapeDtypeStruct((t, d), F32), jax.ShapeDtypeStruct((t, nq_cols), BF16)],
        compiler_params=_params(("parallel",)),
    )(o_a, o_b, w_a, w_b, proj, proj, w_out, x, g, w_q)


def _mixer_bwd(dh, w_out, ua, ub, proj, gate_col0, w_a, w_b):
    t, d = dh.shape
    width = w_a.shape[0]
    tm = min(ROW_TILE, t)
    nc = d // LANES

    def body(dh_ref, w_ref, ua_ref, ub_ref, ga_ref, gb_ref, wa_ref, wb_ref, dua_ref, dub_ref, dg_ref, doa_ref, dob_ref):
        dm = lax.dot_general(dh_ref[...], w_ref[...], NT, preferred_element_type=F32)
        sa = jax.nn.sigmoid(ga_ref[...].astype(F32))
        sb = jax.nn.sigmoid(gb_ref[...].astype(F32))
        dua = (dm * sa).astype(BF16)
        dub = (dm * sb).astype(BF16)
        dua_ref[...] = dua
        dub_ref[...] = dub
        dg_ref[:, 0:d] = (dm * ua_ref[...].astype(F32) * sa * (1.0 - sa)).astype(BF16)
        dg_ref[:, d:2 * d] = (dm * ub_ref[...].astype(F32) * sb * (1.0 - sb)).astype(BF16)
        doa_ref[...] = lax.dot_general(dua, wa_ref[...], NT, preferred_element_type=F32).astype(BF16)
        dob_ref[...] = lax.dot_general(dub, wb_ref[...], NT, preferred_element_type=F32).astype(BF16)

    row = pl.BlockSpec((tm, d), lambda i: (i, 0))
    wsp = pl.BlockSpec((width, d), lambda i: (0, 0))
    osp = pl.BlockSpec((tm, width), lambda i: (i, 0))
    return pl.pallas_call(
        body, name="mixer_bwd", grid=(t // tm,),
        in_specs=[row, pl.BlockSpec((d, d), lambda i: (0, 0)), row, row,
                  pl.BlockSpec((tm, d), lambda i: (i, gate_col0 // nc)),
                  pl.BlockSpec((tm, d), lambda i: (i, gate_col0 // nc + 1)), wsp, wsp],
        out_specs=[row, row, pl.BlockSpec((tm, 2 * d), lambda i: (i, 0)), osp, osp],
        out_shape=[jax.ShapeDtypeStruct((t, d), BF16), jax.ShapeDtypeStruct((t, d), BF16),
                   jax.ShapeDtypeStruct((t, 2 * d), BF16), jax.ShapeDtypeStruct((t, width), BF16),
                   jax.ShapeDtypeStruct((t, width), BF16)],
        compiler_params=_params(("parallel",)),
    )(dh, w_out, ua, ub, proj, proj, w_a, w_b)


FFN_COLS = 1024
FFN_CHUNK = 256


def _ffn_up(n, w_gate, w_up):
    t, d = n.shape
    hidden = w_gate.shape[0]
    tm = min(2 * ROW_TILE, t)
    tn = min(FFN_COLS, hidden)
    tc = min(FFN_CHUNK, tn)

    def body(n_ref, wg_ref, wu_ref, hg_ref, hu_ref, act_ref):
        for c in range(0, tn, tc):
            hg = lax.dot_general(n_ref[...], wg_ref[c:c + tc, :], NT, preferred_element_type=F32)
            hu = lax.dot_general(n_ref[...], wu_ref[c:c + tc, :], NT, preferred_element_type=F32)
            hg_ref[:, c:c + tc] = hg.astype(BF16)
            hu_ref[:, c:c + tc] = hu.astype(BF16)
            act_ref[:, c:c + tc] = (hg * jax.nn.sigmoid(hg) * hu).astype(BF16)

    wsp = pl.BlockSpec((tn, d), lambda j, i: (j, 0))
    out = pl.BlockSpec((tm, tn), lambda j, i: (i, j))
    osh = jax.ShapeDtypeStruct((t, hidden), BF16)
    return pl.pallas_call(
        body, name="ffn_up", grid=(hidden // tn, t // tm),
        in_specs=[pl.BlockSpec((tm, d), lambda j, i: (i, 0)), wsp, wsp],
        out_specs=[out, out, out], out_shape=[osh, osh, osh],
        compiler_params=_params(("parallel", "parallel")),
    )(n, w_gate, w_up)


def _ffn_bwd(dh, w_down, w_gate, w_up, hg, hu, x, g, dres, w_prev):
    t, d = dh.shape
    hidden = w_down.shape[0]
    q = w_prev.shape[0]
    tm = min(ROW_TILE, t)
    tn = min(FFN_COLS, hidden)
    nj = hidden // tn

    def body(dh_ref, wd_ref, wg_ref, wu_ref, hg_ref, hu_ref, x_ref, g_ref, r_ref, wp_ref, dhg_ref, dhu_ref, dx_ref,
             dxb_ref, dg_ref, do_ref, acc):
        j, i = pl.program_id(0), pl.program_id(1)
        dact = lax.dot_general(dh_ref[...], wd_ref[...], NT, preferred_element_type=F32)
        hg = hg_ref[...].astype(F32)
        sg = jax.nn.sigmoid(hg)
        dhu = (dact * hg * sg).astype(BF16)
        dhg = (dact * hu_ref[...].astype(F32) * sg * (1.0 + hg * (1.0 - sg))).astype(BF16)
        dhu_ref[...] = dhu
        dhg_ref[...] = dhg
        part = (lax.dot_general(dhg, wg_ref[...], NN, preferred_element_type=F32)
                + lax.dot_general(dhu, wu_ref[...], NN, preferred_element_type=F32))

        @pl.when(j == 0)
        def _():
            acc[i] = part

        @pl.when(j > 0)
        def _():
            acc[i] += part

        @pl.when(jnp.logical_and(j == 0, i == 0))
        def _():
            dg_ref[...] = jnp.zeros_like(dg_ref)

        @pl.when(j == nj - 1)
        def _():
            dx, dg = _rms_bwd_rows(acc[i], x_ref[...], g_ref[...], r_ref[...])
            dx_ref[...] = dx
            dxb = dx.astype(BF16)
            dxb_ref[...] = dxb
            dg_ref[...] += dg
            do_ref[...] = lax.dot_general(dxb, wp_ref[...], NT, preferred_element_type=F32).astype(BF16)

    hid = pl.BlockSpec((tm, tn), lambda j, i: (i, j))
    wsp = pl.BlockSpec((tn, d), lambda j, i: (j, 0), pipeline_mode=pl.Buffered(1))
    late = pl.BlockSpec((tm, d), lambda j, i: (jnp.where(j == nj - 1, i, 0), 0))
    late_q = pl.BlockSpec((tm, q), lambda j, i: (jnp.where(j == nj - 1, i, 0), 0))
    vec = pl.BlockSpec((1, d), lambda j, i: (0, 0))
    osh = jax.ShapeDtypeStruct((t, hidden), BF16)
    return pl.pallas_call(
        body, name="ffn_bwd", grid=(nj, t // tm),
        in_specs=[pl.BlockSpec((tm, d), lambda j, i: (i, 0)), wsp, wsp, wsp, hid, hid, late, vec, late,
                  pl.BlockSpec((q, d), lambda j, i: (0, 0), pipeline_mode=pl.Buffered(1))],
        out_specs=[hid, hid, late, late, vec, late_q],
        out_shape=[osh, osh, jax.ShapeDtypeStruct((t, d), F32), jax.ShapeDtypeStruct((t, d), BF16),
                   jax.ShapeDtypeStruct((1, d), F32), jax.ShapeDtypeStruct((t, q), BF16)],
        scratch_shapes=[pltpu.VMEM((t // tm, tm, d), F32)],
        compiler_params=_params(("arbitrary", "arbitrary")),
    )(dh, w_down, w_gate, w_up, hg, hu, x, g, dres, w_prev)


MM_ROWS = 1024


def _mm_w(name, a, w, out_dtype, dims=NN):
    t, k = a.shape
    n = w.shape[1] if dims == NN else w.shape[0]
    tm, tn = min(MM_ROWS, t), min(1024, n)
    o_spec = pl.BlockSpec((tm, tn), lambda j, i: (i, j))
    b_spec = pl.BlockSpec((k, tn), lambda j, i: (0, j)) if dims == NN else pl.BlockSpec((tn, k), lambda j, i: (j, 0))
    return _mm(name, a, w, grid=(n // tn, t // tm), a_spec=pl.BlockSpec((tm, k), lambda j, i: (i, 0)), b_spec=b_spec,
               o_shape=(t, n), o_spec=o_spec, dims=dims, out_dtype=out_dtype)


def _mm_res_norm(name, a, w, res, g):
    t, k = a.shape
    d = w.shape[1]
    tm = min(MM_ROWS, t)

    def body(a_ref, w_ref, r_ref, g_ref, h_ref, n_ref):
        h = lax.dot_general(a_ref[...], w_ref[...], NN, preferred_element_type=F32) + r_ref[...]
        h_ref[...] = h
        r = lax.rsqrt(jnp.mean(h * h, axis=-1, keepdims=True) + RMS_EPS)
        n_ref[...] = (h * r * g_ref[...]).astype(BF16)

    row = pl.BlockSpec((tm, d), lambda i: (i, 0))
    return pl.pallas_call(
        body, name=name, grid=(t // tm,),
        in_specs=[pl.BlockSpec((tm, k), lambda i: (i, 0)), pl.BlockSpec((k, d), lambda i: (0, 0)), row,
                  pl.BlockSpec((1, d), lambda i: (0, 0))],
        out_specs=[row, row], out_shape=[jax.ShapeDtypeStruct((t, d), F32), jax.ShapeDtypeStruct((t, d), BF16)],
        compiler_params=_params(("parallel",)),
    )(a, w, res, g)


WGRAD_COLS = 256


RING = 3


def _wgrad_ring(name, a, g, tm, tk, tc):
    t, k = a.shape
    n = g.shape[1]
    n_steps = 2 * (k // tk)
    slots = min(RING, n_steps)

    def body(a_hbm, g_hbm, o_ref, acc, a_buf, g_buf, sems):
        p, r = pl.program_id(0), pl.program_id(1)
        s = 2 * p + r

        def fetch(step):
            sp, sr, slot = step // 2, step % 2, step % slots
            return (pltpu.make_async_copy(a_hbm.at[sr * tm:(sr + 1) * tm, sp * tk:(sp + 1) * tk], a_buf.at[slot],
                                          sems.at[0, slot]),
                    pltpu.make_async_copy(g_hbm.at[sr * tm:(sr + 1) * tm, :], g_buf.at[slot], sems.at[1, slot]))

        for step in range(n_steps):
            @pl.when(s == step)
            def _(step=step):
                ahead = [0, 1] if step == 0 else []
                for nxt in ahead + [step + 2]:
                    if nxt < n_steps:
                        for cp in fetch(nxt):
                            cp.start()
                for cp in fetch(step):
                    cp.wait()

        slot = lax.rem(s, slots)
        a_blk, g_blk = a_buf.at[slot], g_buf.at[slot]

        def run(first):
            for c in range(0, n, tc):
                part = lax.dot_general(a_blk[...], g_blk[:, c:c + tc], TN, preferred_element_type=F32)
                if first:
                    acc[:, c:c + tc] = part
                else:
                    o_ref[:, c:c + tc] = (part + acc[:, c:c + tc]).astype(BF16)

        pl.when(r == 0)(functools.partial(run, True))
        pl.when(r == 1)(functools.partial(run, False))

    return pl.pallas_call(
        body, name=name, grid=(k // tk, 2), in_specs=[ANY, ANY],
        out_specs=pl.BlockSpec((tk, n), lambda p, r: (p, 0)), out_shape=jax.ShapeDtypeStruct((k, n), BF16),
        scratch_shapes=[pltpu.VMEM((tk, n), F32), pltpu.VMEM((slots, tm, tk), BF16), pltpu.VMEM((slots, tm, n), BF16),
                        pltpu.SemaphoreType.DMA((2, slots))],
        compiler_params=_params(("arbitrary", "arbitrary")),
    )(a, g)


def _wgrad(name, a, g, tk=1024, tn=1024):
    t, k = a.shape
    n = g.shape[1]
    tm, tk, tn = min(2 * MM_ROWS, t), min(tk, k), min(tn, n)
    nr = t // tm
    tc = min(WGRAD_COLS, tn)
    if nr == 2 and n == tn:
        return _wgrad_ring(name, a, g, tm, tk, tc)

    def body(a_ref, g_ref, o_ref, *acc):
        def run(first, last):
            for c in range(0, tn, tc):
                p = lax.dot_general(a_ref[...], g_ref[:, c:c + tc], TN, preferred_element_type=F32)
                if not first:
                    p += acc[0][:, c:c + tc]
                if last:
                    o_ref[:, c:c + tc] = p.astype(BF16)
                else:
                    acc[0][:, c:c + tc] = p

        if nr == 1:
            run(True, True)
            return
        r = pl.program_id(2)
        pl.when(r == 0)(functools.partial(run, True, False))
        if nr > 2:
            pl.when(jnp.logical_and(r > 0, r < nr - 1))(functools.partial(run, False, False))
        pl.when(r == nr - 1)(functools.partial(run, False, True))

    return pl.pallas_call(
        body, name=name, grid=(k // tk, n // tn, nr),
        in_specs=[pl.BlockSpec((tm, tk), lambda p, q, r: (r, p)), pl.BlockSpec((tm, tn), lambda p, q, r: (r, q))],
        out_specs=pl.BlockSpec((tk, tn), lambda p, q, r: (p, q)), out_shape=jax.ShapeDtypeStruct((k, n), BF16),
        scratch_shapes=[pltpu.VMEM((tk, tn), F32)] if nr > 1 else [],
        compiler_params=_params(("parallel", "parallel", "arbitrary")),
    )(a, g)


def _peers():
    x, y, c = lax.axis_index("x"), lax.axis_index("y"), lax.axis_index("c")
    me = 4 * x + 2 * y + c
    out = []
    for k in range(1, N_DEV):
        kx, ky, kc = (k >> 2) & 1, (k >> 1) & 1, k & 1
        px = 1 - x if kx else x
        py = 1 - y if ky else y
        pc = 1 - c if kc else c
        out.append(((px, py, pc), 4 * px + 2 * py + pc))
    return me, out


def _cast_weights(ws, pad_rows):
    def body(*refs):
        n = len(refs) // 2
        for i_ref, o_ref, pr in zip(refs[:n], refs[n:], pad_rows):
            r, c = i_ref.shape
            o_ref[0:r, :] = i_ref[...].astype(BF16)
            if pr:
                o_ref[r:r + pr, :] = jnp.zeros((pr, c), BF16)

    return pl.pallas_call(
        body, name="cast_weights", in_specs=[VMEM] * len(ws), out_specs=[VMEM] * len(ws),
        out_shape=[jax.ShapeDtypeStruct((w.shape[0] + pr, w.shape[1]), BF16) for w, pr in zip(ws, pad_rows)],
    )(*ws)


def _window(ref, j, c):
    return ref.at[:, pl.ds(pl.multiple_of(j * c, LANES), c)]


def _direct_copies(ins, outs, sems, gather, cols, landed):
    send_sems, recv_sems, loc_sems = sems
    n_peer = N_DEV - 1
    me, peers = _peers()

    def src(w, j):
        if gather:
            return ins[w]
        return _window(ins[w], j, cols[w]) if cols[w] else ins[w].at[j]

    def dst(w, j):
        return _window(outs[w], j, cols[w]) if gather and cols[w] else outs[w].at[j]

    local = [pltpu.make_async_copy(src(w, me), dst(w, me), loc_sems.at[w]) for w in range(len(ins))]
    remote = [pltpu.make_async_remote_copy(
        src_ref=src(w, idx), dst_ref=dst(w, idx if landed else me),
        send_sem=send_sems.at[w * n_peer + k], recv_sem=recv_sems.at[w * n_peer + k],
        device_id=dev, device_id_type=pl.DeviceIdType.MESH)
        for k, (dev, idx) in reversed(list(enumerate(peers))) for w in range(len(ins))]
    return local, remote


OTHER_CHIPS = (2, 4, 6)


def _gather_copies(ins, outs, sems, cols):
    send_sems, recv_sems, loc_sems = sems
    x, y, c = lax.axis_index("x"), lax.axis_index("y"), lax.axis_index("c")
    me = 4 * x + 2 * y + c
    n_pair = N_DEV - 1

    def dev(mask):
        return (1 - x if mask & 4 else x, 1 - y if mask & 2 else y, 1 - c if mask & 1 else c)

    def slot(w, mask):
        j = jnp.bitwise_xor(me, mask)
        return _window(outs[w], j, cols[w]) if cols[w] else outs[w].at[j]

    def remote(w, pair, src, to_slot, target):
        return pltpu.make_async_remote_copy(src_ref=src, dst_ref=slot(w, to_slot), send_sem=send_sems.at[w * n_pair + pair],
                                            recv_sem=recv_sems.at[w * n_pair + pair], device_id=dev(target),
                                            device_id_type=pl.DeviceIdType.MESH)

    ws = range(len(ins))
    return dict(
        local=[pltpu.make_async_copy(ins[w], slot(w, 0), loc_sems.at[w]) for w in ws],
        to_chips=[remote(w, 1 + t, ins[w], 0, m) for t, m in enumerate(OTHER_CHIPS) for w in ws],
        to_core=[remote(w, 0, ins[w], 0, 1) for w in ws],
        from_chips=[remote(w, 1 + t, ins[w], m, 0) for t, m in enumerate(OTHER_CHIPS) for w in ws],
        pass_on=[remote(w, 4 + t, slot(w, m), m, 1) for t, m in enumerate(OTHER_CHIPS) for w in ws],
        from_core=[remote(w, 0, ins[w], 1, 0) for w in ws]
        + [remote(w, 4 + t, ins[w], m + 1, 0) for t, m in enumerate(OTHER_CHIPS) for w in ws])


TWO_LEVEL = "gather in two levels"


def _exchange_start(ins, outs, sems, gather, cols):
    if gather == TWO_LEVEL:
        cps = _gather_copies(ins, outs, sems, cols)
        for cp in cps["local"] + cps["to_chips"] + cps["to_core"]:
            cp.start()
    else:
        local, remote = _direct_copies(ins, outs, sems, gather, cols, False)
        for cp in local + remote:
            cp.start()


def _exchange_pass_on(ins, outs, sems, gather, cols, chips):
    if gather == TWO_LEVEL:
        cps = _gather_copies(ins, outs, sems, cols)
        n = len(ins)
        for t in chips:
            for arrived, onward in zip(cps["from_chips"][t * n:(t + 1) * n], cps["pass_on"][t * n:(t + 1) * n]):
                arrived.wait_recv()
                onward.start()


def _exchange_wait(ins, outs, sems, gather, cols):
    if gather == TWO_LEVEL:
        cps = _gather_copies(ins, outs, sems, cols)
        for cp in cps["local"]:
            cp.wait()
        for cp in cps["to_chips"] + cps["to_core"] + cps["pass_on"]:
            cp.wait_send()
        for cp in cps["from_core"]:
            cp.wait_recv()
    else:
        local, remote = _direct_copies(ins, outs, sems, gather, cols, True)
        for cp in local:
            cp.wait()
        for cp in remote:
            cp.wait_send()
            cp.wait_recv()


def _exchange_shapes(arrs, gather, cols):
    n = len(arrs)
    out_shape = []
    for a, c in zip(arrs, cols):
        if gather:
            shape = (a.shape[0], N_DEV * c) if c else (N_DEV,) + a.shape
        else:
            shape = (N_DEV, a.shape[0], c) if c else a.shape
        out_shape.append(jax.ShapeDtypeStruct(shape, a.dtype))
    sems = [pltpu.SemaphoreType.DMA((n * (N_DEV - 1),)), pltpu.SemaphoreType.DMA((n * (N_DEV - 1),)),
            pltpu.SemaphoreType.DMA((n,))]
    return out_shape, sems


def _call(body, *, name, grid, in_specs, out_specs, out_shape, scratch, sem, args, ride=None):
    if ride is None:
        outs = pl.pallas_call(body, name=name, grid=grid, in_specs=in_specs, out_specs=out_specs, out_shape=out_shape,
                              scratch_shapes=scratch, compiler_params=_params(sem))(*args)
        return outs, None
    arrs, gather, cols = ride
    n, n_in, n_out, n_scr = len(arrs), len(in_specs), len(out_specs), len(scratch)
    x_shape, x_sems = _exchange_shapes(arrs, gather, cols)

    def riding(*refs):
        ins, x_ins = refs[:n_in], refs[n_in:n_in + n]
        outs = refs[n_in + n:n_in + n + n_out]
        x_outs = refs[n_in + n + n_out:n_in + 2 * n + n_out]
        scr = refs[n_in + 2 * n + n_out:n_in + 2 * n + n_out + n_scr]
        sems = refs[n_in + 2 * n + n_out + n_scr:]
        def at(step):
            return functools.reduce(jnp.logical_and, [pl.program_id(a) == v for a, v in enumerate(step)])

        @pl.when(at((0,) * len(grid)))
        def _():
            _exchange_start(x_ins, x_outs, sems, gather, cols)

        @pl.when(at((grid[0] // 2,) + (0,) * (len(grid) - 2) + (grid[-1] // 2,)))
        def _():
            _exchange_pass_on(x_ins, x_outs, sems, gather, cols, (0, 1))

        @pl.when(at((grid[0] // 2,) + (0,) * (len(grid) - 2) + (3 * grid[-1] // 4,)))
        def _():
            _exchange_pass_on(x_ins, x_outs, sems, gather, cols, (2,))

        body(*ins, *outs, *scr)

        @pl.when(at(tuple(g - 1 for g in grid)))
        def _():
            _exchange_wait(x_ins, x_outs, sems, gather, cols)

    res = pl.pallas_call(
        riding, name=name, grid=grid, in_specs=list(in_specs) + [ANY] * n, out_specs=list(out_specs) + [ANY] * n,
        out_shape=list(out_shape) + x_shape, scratch_shapes=list(scratch) + x_sems,
        compiler_params=_params(("arbitrary",) * len(grid)))(*args, *arrs)
    return res[:n_out], res[n_out:]


def _my_block():
    return (4 * lax.axis_index("x") + 2 * lax.axis_index("y") + lax.axis_index("c")).astype(jnp.int32).reshape(1)


def _proj_in_gather(x, g, w_shard):
    t, k = x.shape
    cs = w_shard.shape[1]
    tm = min(MM_ROWS, t)
    ni = t // tm
    arrival = (0, 1, 2, 4, 3, 5, 6, 7)

    def mask_at(s):
        return jnp.where(s == 3, 4, jnp.where(s == 4, 3, s))

    def body(me_ref, x_ref, g_ref, w_hbm, o_ref, all_hbm, n_hbm, w_vmem, n_vmem, send_sems, recv_sems, loc_sems,
             load_sems, n_sem):
        s, i = pl.program_id(0), pl.program_id(1)
        cps = _gather_copies([w_hbm], [all_hbm], (send_sems, recv_sems, loc_sems), (cs,))
        by_mask = {0: cps["local"][0], 1: cps["from_core"][0]}
        for t_chip, m in enumerate(OTHER_CHIPS):
            by_mask[m] = cps["from_chips"][t_chip]
            by_mask[m + 1] = cps["from_core"][1 + t_chip]
        arrived = [by_mask[m] for m in arrival]

        def load(step):
            src = w_hbm if step == 0 else _window(all_hbm, jnp.bitwise_xor(me_ref[0], arrival[step]), cs)
            return pltpu.make_async_copy(src, w_vmem.at[step % 2], load_sems.at[step % 2])

        @pl.when(jnp.logical_and(s == 0, i == 0))
        def _():
            for cp in cps["local"] + cps["to_chips"] + cps["to_core"]:
                cp.start()
            load(0).start()

        for step, mask in enumerate(arrival):
            @pl.when(jnp.logical_and(s == step, i == 0))
            def _(step=step):
                load(step).wait()

            if step + 1 < N_DEV:
                @pl.when(jnp.logical_and(s == step, i == min(1, ni - 1)))
                def _(step=step):
                    arrived[step + 1].wait_recv()
                    if arrival[step + 1] in OTHER_CHIPS:
                        cps["pass_on"][OTHER_CHIPS.index(arrival[step + 1])].start()
                    load(step + 1).start()

        @pl.when(s == 0)
        def _():
            xf = x_ref[...]
            r = lax.rsqrt(jnp.mean(xf * xf, axis=-1, keepdims=True) + RMS_EPS)
            n_vmem[i] = (xf * r * g_ref[...]).astype(BF16)
            keep = pltpu.make_async_copy(n_vmem.at[i], n_hbm.at[pl.ds(pl.multiple_of(i * tm, tm), tm), :], n_sem)
            keep.start()
            keep.wait()

        o_ref[...] = lax.dot_general(n_vmem[i], w_vmem[s % 2], NN, preferred_element_type=F32).astype(BF16)

        @pl.when(jnp.logical_and(s == N_DEV - 1, i == ni - 1))
        def _():
            cps["local"][0].wait()
            for cp in cps["to_chips"] + cps["to_core"] + cps["pass_on"]:
                cp.wait_send()

    return pl.pallas_call(
        body, name="proj_in",
        grid_spec=pltpu.PrefetchScalarGridSpec(
            num_scalar_prefetch=1, grid=(N_DEV, ni),
            in_specs=[pl.BlockSpec((tm, k), lambda s, i, me: (jnp.where(s == 0, i, 0), 0)),
                      pl.BlockSpec((1, k), lambda s, i, me: (0, 0)), ANY],
            out_specs=[pl.BlockSpec((tm, cs), lambda s, i, me: (i, jnp.bitwise_xor(me[0], mask_at(s)))), ANY, ANY],
            scratch_shapes=[pltpu.VMEM((2, k, cs), BF16), pltpu.VMEM((ni, tm, k), BF16),
                            pltpu.SemaphoreType.DMA((N_DEV - 1,)), pltpu.SemaphoreType.DMA((N_DEV - 1,)),
                            pltpu.SemaphoreType.DMA((1,)), pltpu.SemaphoreType.DMA((2,)), pltpu.SemaphoreType.DMA]),
        out_shape=[jax.ShapeDtypeStruct((t, N_DEV * cs), BF16), jax.ShapeDtypeStruct((k, N_DEV * cs), BF16),
                   jax.ShapeDtypeStruct((t, k), BF16)],
        compiler_params=_params(("arbitrary", "arbitrary")),
    )(_my_block(), x, g, w_shard)


def _gw_in_scatter(a, g):
    t, k = a.shape
    cs = g.shape[1] // N_DEV
    tm = min(MM_ROWS, t)
    nr = t // tm
    n_chip = N_DEV // 2
    chips = (6, 4, 2, 0)

    def body(me_ref, a_ref, g_ref, out_hbm, acc, stage, other, core_send, core_recv, chip_send, chip_recv, loc_sem):
        s, r = pl.program_id(0), pl.program_id(1)
        x, y, c = lax.axis_index("x"), lax.axis_index("y"), lax.axis_index("c")
        my_chip = 2 * x + y
        part = lax.dot_general(a_ref[...], g_ref[...], TN, preferred_element_type=F32)

        def to_core(m):
            return pltpu.make_async_remote_copy(src_ref=stage.at[0], dst_ref=other.at[m], send_sem=core_send.at[m],
                                                recv_sem=core_recv.at[m], device_id=(x, y, 1 - c),
                                                device_id_type=pl.DeviceIdType.MESH)

        def to_chip(m, landed):
            mask = chips[m]
            there = (1 - x if mask & 4 else x, 1 - y if mask & 2 else y, c)
            slot = (2 * there[0] + there[1]) if landed else my_chip
            return pltpu.make_async_remote_copy(src_ref=stage.at[1], dst_ref=out_hbm.at[slot], send_sem=chip_send.at[m],
                                                recv_sem=chip_recv.at[m], device_id=there,
                                                device_id_type=pl.DeviceIdType.MESH)

        local = pltpu.make_async_copy(stage.at[1], out_hbm.at[my_chip], loc_sem)

        @pl.when(r == 0)
        def _():
            acc[...] = part

        @pl.when(r > 0)
        def _():
            acc[...] += part

        for step in range(N_DEV):
            m = step // 2

            @pl.when(jnp.logical_and(s == step, r == nr - 1))
            def _(step=step, m=m):
                if step % 2 == 0:
                    if m > 0:
                        to_core(m - 1).wait_send()
                    stage[0] = acc[...].astype(BF16)
                    to_core(m).start()
                else:
                    if m > 0:
                        to_chip(m - 1, False).wait_send()
                    to_core(m).wait_recv()
                    stage[1] = (acc[...] + other[m].astype(F32)).astype(BF16)
                    if m < n_chip - 1:
                        to_chip(m, False).start()
                    else:
                        local.start()
                        to_core(m).wait_send()
                        local.wait()
                        for mm in range(n_chip - 1):
                            to_chip(mm, True).wait_recv()

    return pl.pallas_call(
        body, name="gw_in",
        grid_spec=pltpu.PrefetchScalarGridSpec(
            num_scalar_prefetch=1, grid=(N_DEV, nr),
            in_specs=[pl.BlockSpec((tm, k), lambda s, r, me: (r, 0)),
                      pl.BlockSpec((tm, cs), lambda s, r, me: (r, jnp.bitwise_xor(me[0], N_DEV - 1 - s)))],
            out_specs=ANY,
            scratch_shapes=[pltpu.VMEM((k, cs), F32), pltpu.VMEM((2, k, cs), BF16), pltpu.VMEM((n_chip, k, cs), BF16),
                            pltpu.SemaphoreType.DMA((n_chip,)), pltpu.SemaphoreType.DMA((n_chip,)),
                            pltpu.SemaphoreType.DMA((n_chip - 1,)), pltpu.SemaphoreType.DMA((n_chip - 1,)),
                            pltpu.SemaphoreType.DMA]),
        out_shape=jax.ShapeDtypeStruct((n_chip, k, cs), BF16),
        compiler_params=_params(("arbitrary", "arbitrary")),
    )(_my_block(), a, g)


SMALL_ROWS = 8


def _allreduce_small(parts, loss_part):
    n, d = len(parts), parts[0].shape[1]

    def body(*refs):
        part_refs, loss_ref, o_ref = refs[:n], refs[n], refs[n + 1]
        mine_ref, all_ref, send_sems, recv_sems = refs[n + 2:]
        me, peers = _peers()
        mine_ref[...] = jnp.zeros_like(mine_ref)
        for i, p_ref in enumerate(part_refs):
            mine_ref[i:i + 1, :] = p_ref[...]
        mine_ref[SMALL_ROWS - 1:SMALL_ROWS, 0:LANES] = loss_ref[0:1, :]
        all_ref[me] = mine_ref[...]
        for k, (dev, idx) in enumerate(peers):
            pltpu.make_async_remote_copy(src_ref=mine_ref, dst_ref=all_ref.at[me], send_sem=send_sems.at[k],
                                         recv_sem=recv_sems.at[k], device_id=dev,
                                         device_id_type=pl.DeviceIdType.MESH).start()
        for k, (dev, idx) in enumerate(peers):
            cp = pltpu.make_async_remote_copy(src_ref=mine_ref, dst_ref=all_ref.at[idx], send_sem=send_sems.at[k],
                                              recv_sem=recv_sems.at[k], device_id=dev,
                                              device_id_type=pl.DeviceIdType.MESH)
            cp.wait_send()
            cp.wait_recv()
        tot = all_ref[0]
        for dvc in range(1, N_DEV):
            tot = tot + all_ref[dvc]
        o_ref[...] = tot

    return pl.pallas_call(
        body, name="allreduce_small", in_specs=[VMEM] * (n + 1), out_specs=VMEM,
        out_shape=jax.ShapeDtypeStruct((SMALL_ROWS, d), F32),
        scratch_shapes=[pltpu.VMEM((SMALL_ROWS, d), F32), pltpu.VMEM((N_DEV, SMALL_ROWS, d), F32),
                        pltpu.SemaphoreType.DMA((N_DEV - 1,)), pltpu.SemaphoreType.DMA((N_DEV - 1,))],
    )(*parts, loss_part)


def _adam_math(g, w, m, v):
    m_new = ADAM_B1 * m + (1.0 - ADAM_B1) * g
    v_new = ADAM_B2 * v + (1.0 - ADAM_B2) * (g * g)
    m_hat = m_new / (1.0 - ADAM_B1 ** ADAM_STEP)
    v_hat = v_new / (1.0 - ADAM_B2 ** ADAM_STEP)
    delta = -ADAM_LR * (m_hat / (jnp.sqrt(v_hat) + ADAM_EPS) + ADAM_WD * w)
    return delta, m_new, v_new


def _adam(name, pieces, w, m, v):
    r, c = w.shape
    n_piece, _, cp = pieces.shape
    tr = r
    for cand in (256, 176, 128, 64):
        if r % cand == 0 and r > cand:
            tr = cand
            break

    def body(p_ref, w_ref, m_ref, v_ref, g_ref, d_ref, mo_ref, vo_ref):
        g = p_ref[0, :, 0:c].astype(F32)
        for j in range(1, n_piece):
            g = g + p_ref[j, :, 0:c].astype(F32)
        delta, m_new, v_new = _adam_math(g, w_ref[...], m_ref[...], v_ref[...])
        g_ref[...] = g
        d_ref[...] = delta
        mo_ref[...] = m_new
        vo_ref[...] = v_new

    blk = pl.BlockSpec((tr, c), lambda i: (i, 0))
    osh = jax.ShapeDtypeStruct((r, c), F32)
    return pl.pallas_call(
        body, name=name, grid=(r // tr,),
        in_specs=[pl.BlockSpec((n_piece, tr, cp), lambda i: (0, i, 0)), blk, blk, blk],
        out_specs=[blk, blk, blk, blk], out_shape=[osh, osh, osh, osh],
        compiler_params=_params(("parallel",)),
    )(pieces, w, m, v)


def _adam_small(g_all, ws, ms, vs):
    n = len(ws)

    def body(*refs):
        g_ref, ins, outs = refs[0], refs[1:1 + 3 * n], refs[1 + 3 * n:]
        for i in range(n):
            g = g_ref[i:i + 1, :]
            delta, m_new, v_new = _adam_math(g, ins[i][...], ins[n + i][...], ins[2 * n + i][...])
            for kind, val in enumerate((g, delta, m_new, v_new)):
                outs[kind * n + i][...] = val

    osh = jax.ShapeDtypeStruct(ws[0].shape, F32)
    res = pl.pallas_call(body, name="adam_small", in_specs=[VMEM] * (1 + 3 * n), out_specs=[VMEM] * (4 * n),
                         out_shape=[osh] * (4 * n))(g_all, *ws, *ms, *vs)
    return res[:n], res[n:2 * n], res[2 * n:3 * n], res[3 * n:]


def _local_step(x, mem, pos, tgt, gains, w_in_shard, shards, batch):
    g_mix, g_mem_q, g_mem_kv, g_ffn, g_final = gains
    t, d = x.shape
    s = t // batch
    n_mem = mem.shape[0] // batch
    n_sh = N_DEV
    width = shards[0].shape[0]
    nb = width // LANES

    lane = np.arange(LANES) % HEAD_DIM
    sel_lo = (lane < ROPE_HALF).astype(np.float32)[None, :]
    sel_hi = ((lane >= ROPE_HALF) & (lane < 2 * ROPE_HALF)).astype(np.float32)[None, :]
    freqs = np.float32(ROPE_THETA) ** (-np.arange(ROPE_HALF, dtype=np.float32) / np.float32(ROPE_HALF))
    inv_freq = np.where(lane < 2 * ROPE_HALF, freqs[lane % ROPE_HALF], 0.0).astype(np.float32)[None, :]
    cos_t, sin_a, sin_b = _rope_tables(pos, jnp.asarray(inv_freq), jnp.asarray(sel_lo), jnp.asarray(sel_hi))
    bias = _dilated_bias_tiles(s)

    proj, w_in, n1 = _proj_in_gather(x, g_mix, w_in_shard)
    qk_a = _rope_apply("rope_fwd", [proj], 2 * width, cos_t, sin_a, sin_b, 1.0)
    cs_up = shards[0].shape[1]
    (o_a, lse_a), (w_up_a, w_up_b, w_out, w_q, w_kv, w_o, w_fd) = _da_fwd(
        qk_a, proj, 2 * nb, bias, batch, s,
        ride=(shards[:6] + shards[8:], TWO_LEVEL, (cs_up, cs_up, 0, 0, 0, cs_up, 0)))
    (o_b, tot_b), (w_fg, w_fu) = _sb_fwd(proj, 3 * nb, 4 * nb, 5 * nb, batch, s, ride=(shards[6:8], True, (0, 0)))
    w_out = w_out.reshape(d, d)
    w_q = w_q.reshape(d, -1)
    w_kv = w_kv.reshape(d, -1)
    w_fd = w_fd.reshape(-1, d)
    w_fg = w_fg.reshape(-1, d)
    w_fu = w_fu.reshape(-1, d)
    ua, ub, mixed, n2, h1, q_m = _mixer_fwd(o_a, o_b, w_up_a, w_up_b, proj, 6 * nb, w_out, x, g_mem_q, w_q)
    mem_n = _rms_fwd("norm_mem_kv", mem, g_mem_kv)
    kv_m = _mm_w("mem_kv", mem_n, w_kv, BF16)
    o_m = _mem_fwd(q_m, kv_m, batch, s, n_mem)
    h2, n3 = _mm_res_norm("mem_out", o_m, w_o, h1, g_ffn)
    hg, hu, act = _ffn_up(n3, w_fg, w_fu)
    loss_part, dh3, dh3_b, dg_final = _loss_head(act, w_fd, h2, tgt, g_final.reshape(1, d))

    dhg, dhu, dh2, dh2_b, dg_ffn, do_m = _ffn_bwd(dh3_b, w_fd, w_fg, w_fu, hg, hu, h2, g_ffn, dh3, w_o)
    gw_fd = _wgrad("gw_ffn_down", act, dh3_b)
    gw_fg = _wgrad("gw_ffn_gate", dhg, n3)
    gw_fu = _wgrad("gw_ffn_up", dhu, n3)

    gw_o = _wgrad("gw_mem_o", o_m, dh2_b)
    dq_m, dkv_m = _mem_bwd(q_m, kv_m, do_m, batch, s, n_mem)
    gw_q = _wgrad("gw_mem_q", n2, dq_m)
    gw_kv = _wgrad("gw_mem_kv", mem_n, dkv_m)
    (dg_mem_kv,) = _rms_bwd("norm_mem_kv_bwd", (dkv_m, w_kv, NT), mem, g_mem_kv, None, ())
    dh1, dh1_b, dg_mem_q = _rms_bwd("norm_mem_q_bwd", (dq_m, w_q, NT), h1, g_mem_q, dh2, ("f32", "bf16"))

    gw_out = _wgrad("gw_out", mixed, dh1_b)
    dua, dub, dgates, do_a, do_b = _mixer_bwd(dh1_b, w_out, ua, ub, proj, 6 * nb, w_up_a, w_up_b)
    gw_ua = _wgrad("gw_up_a", o_a, dua)
    gw_ub = _wgrad("gw_up_b", o_b, dub)
    (dq_ar, dk_ar, dv_a), (p_fg, p_fd) = _da_bwd(
        qk_a, proj, 2 * nb, bias, o_a, lse_a, do_a, batch, s,
        ride=([gw_fg.reshape(n_sh, -1, d), gw_fd.reshape(n_sh, -1, d)], False, (0, 0)))
    mid = [gw_ua, gw_ub, gw_out.reshape(n_sh, -1, d), gw_q.reshape(n_sh, -1, gw_q.shape[1]),
           gw_kv.reshape(n_sh, -1, gw_kv.shape[1]), gw_o, gw_fu.reshape(n_sh, -1, d)]
    (dq_b, dk_b, dv_b), (*p_mid, p_fu) = _sb_bwd(proj, 3 * nb, 4 * nb, 5 * nb, tot_b, do_b, batch, s,
                                                 ride=(mid, False, (cs_up, cs_up, 0, 0, 0, cs_up, 0)))
    p_ffn = [p_fg, p_fu, p_fd]
    dproj = _rope_apply("rope_bwd", [dq_ar, dk_ar], width, cos_t, sin_a, sin_b, -1.0,
                        tail=(dv_a, dq_b, dk_b, dv_b, dgates))
    grad_x, dg_mix = _rms_bwd("proj_in_bwd", (dproj, w_in, NT), x, g_mix, dh1, ("f32",))
    p_in = _gw_in_scatter(n1, dproj)
    return loss_part, grad_x, [p_in] + list(p_mid) + p_ffn, (dg_mix, dg_mem_q, dg_mem_kv, dg_ffn, dg_final)


WEIGHTS =("w_in", "w_up_a", "w_up_b", "w_out", "w_q_mem", "w_kv_mem", "w_o_mem", "w_ffn_gate", "w_ffn_up", "w_ffn_down")
GAINS = ("g_mix", "g_mem_q", "g_mem_kv", "g_ffn", "g_final")
ORDER = ("g_mix", "w_in", "w_up_a", "w_up_b", "w_out", "g_mem_q", "g_mem_kv", "w_q_mem", "w_kv_mem", "w_o_mem", "g_ffn",
         "w_ffn_gate", "w_ffn_up", "w_ffn_down", "g_final")


def kernel(x, mem, positions, g_mix, w_in, w_up_a, w_up_b, w_out, g_mem_q, g_mem_kv, w_q_mem, w_kv_mem, w_o_mem, g_ffn, w_ffn_gate, w_ffn_up, w_ffn_down, g_final, loss_target, m_g_mix, m_w_in, m_w_up_a, m_w_up_b, m_w_out, m_g_mem_q, m_g_mem_kv, m_w_q_mem, m_w_kv_mem, m_w_o_mem, m_g_ffn, m_w_ffn_gate, m_w_ffn_up, m_w_ffn_down, m_g_final, v_g_mix, v_w_in, v_w_up_a, v_w_up_b, v_w_out, v_g_mem_q, v_g_mem_kv, v_w_q_mem, v_w_kv_mem, v_w_o_mem, v_g_ffn, v_w_ffn_gate, v_w_ffn_up, v_w_ffn_down, v_g_final):
    given = dict(locals())
    batch, s, d = x.shape
    t = batch * s
    flipped = ("w_ffn_gate", "w_ffn_up")

    def view(a, n):
        a = a.reshape(a.shape[-2:])
        return a.T if n in flipped else a

    def unview(a, n):
        return (a.T if n in flipped else a).reshape(given[n].shape)

    shard = {n: view(given[n], n) for n in WEIGHTS}
    gains = [given[n].reshape(1, d) for n in GAINS]

    pad = (-shard["w_ffn_down"].shape[0]) % LANES
    cast = _cast_weights([shard[n] for n in WEIGHTS], [pad if n in flipped + ("w_ffn_down",) else 0 for n in WEIGHTS])
    loss_part, grad_x, pieces, dgains = _local_step(
        x.reshape(t, d), mem.reshape(-1, d), positions.reshape(t, 1), loss_target.reshape(t, d), gains, cast[0],
        cast[1:], batch)

    grad, delta, new_m, new_v = {}, {}, {}, {}
    for n, p in zip(WEIGHTS, pieces):
        outs = _adam("adam_" + n, p, shard[n], view(given["m_" + n], n), view(given["v_" + n], n))
        grad[n], delta[n], new_m[n], new_v[n] = [unview(o, n) for o in outs]

    g_all = _allreduce_small(list(dgains), loss_part)
    small = _adam_small(g_all, gains, [given["m_" + n].reshape(1, d) for n in GAINS],
                        [given["v_" + n].reshape(1, d) for n in GAINS])
    for out, vals in zip((grad, delta, new_m, new_v), small):
        for n, val in zip(GAINS, vals):
            out[n] = val.reshape(given[n].shape)

    loss = g_all[SMALL_ROWS - 1, 0]
    return (loss, grad_x.reshape(x.shape), *[grad[n] for n in ORDER], *[delta[n] for n in ORDER],
            *[new_m[n] for n in ORDER], *[new_v[n] for n in ORDER])
```

```python
import functools
import math

import jax
import jax.numpy as jnp
import numpy as np
from jax import lax
from jax.experimental import pallas as pl
from jax.experimental.pallas import tpu as pltpu

F32 = jnp.float32
BF16 = jnp.bfloat16

N_DEV = 8
HEAD_DIM = 64
MEM_HEAD_DIM = 128
N_HEADS_MEM = 4
BLOCK = 128
DIL_PATTERNS = ((128, 1), (512, 4), (2048, 16))
ROPE_THETA = 500000.0
ROPE_HALF = 8
RMS_EPS = 1e-6
ADAM_LR, ADAM_B1, ADAM_B2, ADAM_EPS, ADAM_WD, ADAM_STEP = 0.001, 0.9, 0.999, 1e-08, 0.01, 10
NEG = -1e30
ROW_TILE = 512
LANES = 128

ANY = pl.BlockSpec(memory_space=pl.ANY)
VMEM = pl.BlockSpec(memory_space=pltpu.VMEM)
NN = (((1,), (0,)), ((), ()))
NT = (((1,), (1,)), ((), ()))
TN = (((0,), (0,)), ((), ()))


def _params(sem):
    return pltpu.CompilerParams(dimension_semantics=sem)


def _mm(name, a, b, *, grid, a_spec, b_spec, o_shape, o_spec, dims, out_dtype):
    def body(a_ref, b_ref, o_ref):
        o_ref[...] = lax.dot_general(a_ref[...], b_ref[...], dims, preferred_element_type=F32).astype(out_dtype)

    return pl.pallas_call(
        body, name=name, grid=grid, in_specs=[a_spec, b_spec],
        out_specs=o_spec, out_shape=jax.ShapeDtypeStruct(o_shape, out_dtype),
        compiler_params=_params(("parallel",) * len(grid)),
    )(a, b)


def _rms_fwd(name, x, g):
    t, d = x.shape
    tm = min(ROW_TILE, t)

    def body(x_ref, g_ref, o_ref):
        xf = x_ref[...]
        r = lax.rsqrt(jnp.mean(xf * xf, axis=-1, keepdims=True) + RMS_EPS)
        o_ref[...] = (xf * r * g_ref[...]).astype(BF16)

    return pl.pallas_call(
        body, name=name, grid=(t // tm,),
        in_specs=[pl.BlockSpec((tm, d), lambda i: (i, 0)), pl.BlockSpec((1, d), lambda i: (0, 0))],
        out_specs=pl.BlockSpec((tm, d), lambda i: (i, 0)), out_shape=jax.ShapeDtypeStruct((t, d), BF16),
        compiler_params=_params(("parallel",)),
    )(x, g)


def _rms_bwd_rows(dnf, xf, gv, res):
    r = lax.rsqrt(jnp.mean(xf * xf, axis=-1, keepdims=True) + RMS_EPS)
    xh = xf * r
    dxh = dnf * gv
    dx = r * (dxh - xh * jnp.mean(dxh * xh, axis=-1, keepdims=True))
    if res is not None:
        dx = dx + res
    return dx, jnp.sum(dnf * xh, axis=0, keepdims=True)


def _rms_bwd(name, dn, x, g, dres, want):
    t, d = x.shape
    has_res = dres is not None
    lhs = list(dn) if isinstance(dn, tuple) else [dn]
    n_lhs = len(lhs[:2])
    tm = min(MM_ROWS if lhs[0].shape[1] <= d else ROW_TILE, t)

    def body(*refs):
        x_ref, g_ref = refs[n_lhs], refs[n_lhs + 1]
        r_ref = refs[n_lhs + 2] if has_res else None
        dx_refs, dg_ref = refs[-1 - len(want):-1], refs[-1]
        if n_lhs == 2:
            dnf = lax.dot_general(refs[0][...], refs[1][...], lhs[2], preferred_element_type=F32)
        else:
            dnf = refs[0][...].astype(F32)
        dx, dg = _rms_bwd_rows(dnf, x_ref[...], g_ref[...], r_ref[...] if has_res else None)
        for kind, dx_ref in zip(want, dx_refs):
            dx_ref[...] = dx.astype(F32 if kind == "f32" else BF16)

        @pl.when(pl.program_id(0) == 0)
        def _():
            dg_ref[...] = jnp.zeros_like(dg_ref)

        dg_ref[...] += dg

    row = pl.BlockSpec((tm, d), lambda i: (i, 0))
    vec = pl.BlockSpec((1, d), lambda i: (0, 0))
    if n_lhs == 2:
        first = [pl.BlockSpec((tm, lhs[0].shape[1]), lambda i: (i, 0)), pl.BlockSpec(lhs[1].shape, lambda i: (0, 0))]
    else:
        first = [row]
    return pl.pallas_call(
        body, name=name, grid=(t // tm,),
        in_specs=first + [row, vec] + ([row] if has_res else []),
        out_specs=[row] * len(want) + [vec],
        out_shape=[jax.ShapeDtypeStruct((t, d), F32 if kind == "f32" else BF16) for kind in want]
        + [jax.ShapeDtypeStruct((1, d), F32)],
        compiler_params=_params(("arbitrary",)),
    )(*(lhs[:2] + [x, g] + ([dres] if has_res else [])))


RING = 3


def _loss_head(a, w, res, tgt, g):
    t, d = res.shape
    k = a.shape[1]
    tm = min(ROW_TILE, t)
    n_steps = t // tm
    slots = min(RING, n_steps)

    def body(a_hbm, w_ref, r_hbm, t_hbm, g_ref, loss_ref, dh_ref, dhb_ref, dg_ref, a_buf, r_buf, t_buf, sems):
        def fetch(step):
            rows, slot = slice(step * tm, (step + 1) * tm), step % slots
            return [pltpu.make_async_copy(src.at[rows, :], buf.at[slot], sems.at[j, slot])
                    for j, (src, buf) in enumerate(((a_hbm, a_buf), (r_hbm, r_buf), (t_hbm, t_buf)))]

        for step in range(n_steps):
            @pl.when(pl.program_id(0) == step)
            def _(step=step):
                for nxt in ([0, 1] if step == 0 else []) + [step + 2]:
                    if nxt < n_steps:
                        for cp in fetch(nxt):
                            cp.start()
                for cp in fetch(step):
                    cp.wait()

        slot = lax.rem(pl.program_id(0), slots)
        a_ref, r_ref, t_ref = a_buf.at[slot], r_buf.at[slot], t_buf.at[slot]
        xf = lax.dot_general(a_ref[...], w_ref[...], NN, preferred_element_type=F32) + r_ref[...]
        gv = g_ref[...]
        r = lax.rsqrt(jnp.mean(xf * xf, axis=-1, keepdims=True) + RMS_EPS)
        xh = xf * r
        e = xh * gv - t_ref[...]
        dy = e * (1.0 / d)
        dxh = dy * gv
        dh = r * (dxh - xh * jnp.mean(dxh * xh, axis=-1, keepdims=True))
        dh_ref[...] = dh
        dhb_ref[...] = dh.astype(BF16)

        @pl.when(pl.program_id(0) == 0)
        def _():
            dg_ref[...] = jnp.zeros_like(dg_ref)
            loss_ref[...] = jnp.zeros_like(loss_ref)

        dg_ref[...] += jnp.sum(dy * xh, axis=0, keepdims=True)
        part = jnp.sum(jnp.sum(e * e, axis=1, keepdims=True), axis=0, keepdims=True) * (0.5 / d)
        loss_ref[...] += jnp.broadcast_to(part, loss_ref.shape)

    row = pl.BlockSpec((tm, d), lambda i: (i, 0))
    vec = pl.BlockSpec((1, d), lambda i: (0, 0))
    return pl.pallas_call(
        body, name="loss_head", grid=(t // tm,),
        in_specs=[ANY, pl.BlockSpec((k, d), lambda i: (0, 0)), ANY, ANY, vec],
        out_specs=[pl.BlockSpec((8, LANES), lambda i: (0, 0)), row, row, vec],
        out_shape=[jax.ShapeDtypeStruct((8, LANES), F32), jax.ShapeDtypeStruct((t, d), F32),
                   jax.ShapeDtypeStruct((t, d), BF16), jax.ShapeDtypeStruct((1, d), F32)],
        scratch_shapes=[pltpu.VMEM((slots, tm, k), BF16), pltpu.VMEM((slots, tm, d), F32),
                        pltpu.VMEM((slots, tm, d), F32), pltpu.SemaphoreType.DMA((3, slots))],
        compiler_params=_params(("arbitrary",)),
    )(a, w, res, tgt, g)


def _rope_tables(pos, inv_freq, sel_lo, sel_hi):
    t = pos.shape[0]
    tm = min(MM_ROWS, t)

    def body(p_ref, f_ref, lo_ref, hi_ref, c_ref, sa_ref, sb_ref):
        ang = p_ref[...].astype(F32) * f_ref[...]
        rot = lo_ref[...] + hi_ref[...]
        cs, sn = jnp.cos(ang), jnp.sin(ang)
        c_ref[...] = cs * rot + (1.0 - rot)
        sa_ref[...] = -sn * lo_ref[...]
        sb_ref[...] = sn * hi_ref[...]

    vec = pl.BlockSpec((1, LANES), lambda i: (0, 0))
    row = pl.BlockSpec((tm, LANES), lambda i: (i, 0))
    return pl.pallas_call(
        body, name="rope_tables", grid=(t // tm,),
        in_specs=[pl.BlockSpec((tm, 1), lambda i: (i, 0)), vec, vec, vec],
        out_specs=[row, row, row], out_shape=[jax.ShapeDtypeStruct((t, LANES), F32)] * 3,
        compiler_params=_params(("parallel",)),
    )(pos, inv_freq, sel_lo, sel_hi)


def _rope_apply(name, srcs, width, cos_t, sin_a, sin_b, sign, tail=()):
    t = srcs[0].shape[0]
    tm = min(MM_ROWS, t)
    n_cols = width // LANES
    n_src = len(srcs)

    def body(*refs):
        x_refs, tail_refs = refs[:n_src], refs[n_src:n_src + len(tail)]
        c_ref, sa_ref, sb_ref, o_ref = refs[n_src + len(tail):]
        cs, sa, sb = c_ref[...], sign * sa_ref[...], sign * sb_ref[...]
        for a, x_ref in enumerate(x_refs):
            for c in range(n_cols):
                xf = x_ref[:, c * LANES:(c + 1) * LANES].astype(F32)
                up = pltpu.roll(xf, LANES - ROPE_HALF, 1)
                dn = pltpu.roll(xf, ROPE_HALF, 1)
                o_ref[:, a * width + c * LANES:a * width + (c + 1) * LANES] = (xf * cs + up * sa + dn * sb).astype(BF16)
        col = n_src * width
        for t_ref in tail_refs:
            o_ref[:, col:col + t_ref.shape[1]] = t_ref[...]
            col += t_ref.shape[1]

    wide = n_src * width + sum(a.shape[1] for a in tail)
    tab = pl.BlockSpec((tm, LANES), lambda i: (i, 0))
    return pl.pallas_call(
        body, name=name, grid=(t // tm,),
        in_specs=[pl.BlockSpec((tm, width), lambda i: (i, 0))] * n_src
        + [pl.BlockSpec((tm, a.shape[1]), lambda i: (i, 0)) for a in tail] + [tab, tab, tab],
        out_specs=pl.BlockSpec((tm, wide), lambda i: (i, 0)),
        out_shape=jax.ShapeDtypeStruct((t, wide), BF16),
        compiler_params=_params(("parallel",)),
    )(*srcs, *tail, cos_t, sin_a, sin_b)


DA_T = 256
MIX_STREAMS = 4
SB_BWD_STREAMS = 2


def _lane_lo():
    return lax.broadcasted_iota(jnp.int32, (BLOCK, LANES), 1) < HEAD_DIM


def _dilated_bias_tiles(s):
    n = s // DA_T
    dist = (np.arange(n)[:, None, None] * DA_T + np.arange(DA_T)[None, :, None] - np.arange(DA_T)[None, None, :])
    cnt = np.zeros(dist.shape, np.float32)
    for window, dil in DIL_PATTERNS:
        cnt += ((dist >= 0) & (dist % dil == 0) & (dist <= window)).astype(np.float32)
    return jnp.asarray(np.where(cnt > 0, np.log(np.maximum(cnt, 1.0)), NEG).astype(np.float32))


def _stack_heads(x, lo):
    zero = jnp.zeros_like(x)
    return jnp.concatenate([jnp.where(lo, x, zero), jnp.where(lo, zero, x)], axis=0)


def _da_fwd(qk, proj, v_col0, bias, batch, s, ride=None, streams=MIX_STREAMS):
    t = qk.shape[0]
    nq = s // DA_T
    n_pairs = 4
    ns = streams
    wide = ns * LANES
    scale = HEAD_DIM ** -0.5

    def body(q_ref, k_ref, v_ref, b_ref, o_ref, lse_ref, acc_ref, m_ref, l_ref):
        i = pl.program_id(2)
        lo = lax.broadcasted_iota(jnp.int32, (DA_T, LANES), 1) < HEAD_DIM
        ones = jnp.ones((DA_T, LANES), BF16)
        acc_ref[...] = jnp.zeros_like(acc_ref)
        m_ref[...] = jnp.full(m_ref.shape, NEG, F32)
        l_ref[...] = jnp.zeros_like(l_ref)
        qqs = [_stack_heads(q_ref[:, st * LANES:(st + 1) * LANES] * scale, lo) for st in range(ns)]

        def scores(st, rows, bias2):
            k = k_ref[rows, st * LANES:(st + 1) * LANES]
            return lax.dot_general(qqs[st], k, NT, preferred_element_type=F32) + bias2

        def softmax(st, sc):
            m_old = m_ref[st]
            m_new = jnp.maximum(m_old, jnp.broadcast_to(jnp.max(sc, axis=1, keepdims=True), m_old.shape))
            m_ref[st] = m_new
            return jnp.exp(sc - jnp.concatenate([m_new, m_new], axis=1)).astype(BF16), jnp.exp(m_old - m_new)

        def values(st, rows, p, alpha):
            v = v_ref[rows, st * LANES:(st + 1) * LANES]
            vz = jnp.zeros_like(v)
            l_ref[st] = alpha * l_ref[st] + lax.dot_general(p, ones, NN, preferred_element_type=F32)
            pv = (lax.dot_general(p[:DA_T], jnp.where(lo, v, vz), NN, preferred_element_type=F32)
                  + lax.dot_general(p[DA_T:], jnp.where(lo, vz, v), NN, preferred_element_type=F32))
            acc_ref[st] = acc_ref[st] * jnp.where(lo, alpha[:DA_T], alpha[DA_T:]) + pv

        def trip(dlt, carry):
            rows = pl.ds(pl.multiple_of((i - dlt) * DA_T, DA_T), DA_T)
            bias_t = b_ref[dlt]
            bias2 = jnp.concatenate([bias_t, bias_t], axis=0)
            scs = [scores(st, rows, bias2) for st in range(ns)]
            pas = [softmax(st, scs[st]) for st in range(ns)]
            for st in range(ns):
                values(st, rows, *pas[st])
            return carry

        lax.fori_loop(0, i + 1, trip, 0)
        for st in range(ns):
            cols = slice(st * LANES, (st + 1) * LANES)
            l_t = l_ref[st]
            o_ref[:, cols] = (acc_ref[st] / jnp.where(lo, l_t[:DA_T], l_t[DA_T:])).astype(BF16)
            lse = m_ref[st] + jnp.log(l_t)
            lse_ref[:, cols] = jnp.where(lo, lse[:DA_T], lse[DA_T:])

    blk = pl.BlockSpec((DA_T, wide), lambda b, h, i: (b * nq + i, h))
    return _call(
        body, name="attn_a_fwd", grid=(batch, n_pairs // ns, nq),
        in_specs=[blk,
                  pl.BlockSpec((s, wide), lambda b, h, i: (b, n_pairs // ns + h)),
                  pl.BlockSpec((s, wide), lambda b, h, i: (b, v_col0 // ns + h)),
                  pl.BlockSpec((nq, DA_T, DA_T), lambda b, h, i: (0, 0, 0))],
        out_specs=[blk, blk],
        out_shape=[jax.ShapeDtypeStruct((t, n_pairs * LANES), BF16), jax.ShapeDtypeStruct((t, n_pairs * LANES), F32)],
        scratch=[pltpu.VMEM((ns, DA_T, LANES), F32), pltpu.VMEM((ns, 2 * DA_T, LANES), F32),
                 pltpu.VMEM((ns, 2 * DA_T, LANES), F32)],
        sem=("parallel", "parallel", "arbitrary"), args=(qk, qk, proj, bias), ride=ride)


def _da_bwd(qk, proj, v_col0, bias, o, lse, do, batch, s, ride=None, streams=MIX_STREAMS):
    t = qk.shape[0]
    nq = s // DA_T
    n_pairs = 4
    ns = streams
    wide = ns * LANES
    scale = HEAD_DIM ** -0.5

    def body(q_ref, k_ref, v_ref, b_ref, o_ref, lse_ref, do_ref, dq_ref, dk_ref, dv_ref, dk_acc, dv_acc, dq_acc):
        i = pl.program_id(2)
        lo = lax.broadcasted_iota(jnp.int32, (DA_T, LANES), 1) < HEAD_DIM

        @pl.when(i == 0)
        def _():
            dk_acc[...] = jnp.zeros_like(dk_acc)
            dv_acc[...] = jnp.zeros_like(dv_acc)

        dq_acc[...] = jnp.zeros_like(dq_acc)
        qqs, dds, deltas, lses = [], [], [], []
        for st in range(ns):
            cols = slice(st * LANES, (st + 1) * LANES)
            do_ = do_ref[:, cols]
            qqs.append(_stack_heads(q_ref[:, cols] * scale, lo))
            dds.append(_stack_heads(do_, lo))
            prod = do_.astype(F32) * o_ref[:, cols].astype(F32)
            fz = jnp.zeros_like(prod)
            deltas.append(jnp.concatenate([jnp.sum(jnp.where(lo, prod, fz), axis=1, keepdims=True),
                                           jnp.sum(jnp.where(lo, fz, prod), axis=1, keepdims=True)], axis=0))
            lse_t = lse_ref[:, cols]
            lses.append(jnp.concatenate([lse_t[:, 0:1], lse_t[:, HEAD_DIM:HEAD_DIM + 1]], axis=0))

        def products(st, rows, bias2):
            cols = slice(st * LANES, (st + 1) * LANES)
            sc = lax.dot_general(qqs[st], k_ref[rows, cols], NT, preferred_element_type=F32) + bias2
            return sc, lax.dot_general(dds[st], v_ref[rows, cols], NT, preferred_element_type=F32)

        def weights(st, sc, dp):
            p = jnp.exp(sc - lses[st])
            return (p * (dp - deltas[st])).astype(BF16), p.astype(BF16)

        def gradients(st, rows, ds, p):
            cols = slice(st * LANES, (st + 1) * LANES)
            k = k_ref[rows, cols]
            kz = jnp.zeros_like(k)
            dq_acc[st] += (lax.dot_general(ds[:DA_T], jnp.where(lo, k, kz), NN, preferred_element_type=F32)
                           + lax.dot_general(ds[DA_T:], jnp.where(lo, kz, k), NN, preferred_element_type=F32))
            dk_acc[rows, cols] += lax.dot_general(ds, qqs[st], TN, preferred_element_type=F32)
            dv_acc[rows, cols] += lax.dot_general(p, dds[st], TN, preferred_element_type=F32)

        def trip(dlt, carry):
            rows = pl.ds(pl.multiple_of((i - dlt) * DA_T, DA_T), DA_T)
            bias_t = b_ref[dlt]
            bias2 = jnp.concatenate([bias_t, bias_t], axis=0)
            prods = [products(st, rows, bias2) for st in range(ns)]
            wts = [weights(st, *prods[st]) for st in range(ns)]
            for st in range(ns):
                gradients(st, rows, *wts[st])
            return carry

        lax.fori_loop(0, i + 1, trip, 0)
        for st in range(ns):
            dq_ref[:, st * LANES:(st + 1) * LANES] = (dq_acc[st] * scale).astype(BF16)

        @pl.when(i == nq - 1)
        def _():
            dk_ref[...] = dk_acc[...].astype(BF16)
            dv_ref[...] = dv_acc[...].astype(BF16)

    blk = pl.BlockSpec((DA_T, wide), lambda b, h, i: (b * nq + i, h))
    seq = pl.BlockSpec((s, wide), lambda b, h, i: (b, h), pipeline_mode=pl.Buffered(1))
    one = pl.Buffered(1)
    out = jax.ShapeDtypeStruct((t, n_pairs * LANES), BF16)
    return _call(
        body, name="attn_a_bwd", grid=(batch, n_pairs // ns, nq),
        in_specs=[blk,
                  pl.BlockSpec((s, wide), lambda b, h, i: (b, n_pairs // ns + h), pipeline_mode=one),
                  pl.BlockSpec((s, wide), lambda b, h, i: (b, v_col0 // ns + h), pipeline_mode=one),
                  pl.BlockSpec((nq, DA_T, DA_T), lambda b, h, i: (0, 0, 0), pipeline_mode=one),
                  blk, blk, blk],
        out_specs=[blk, seq, seq], out_shape=[out, out, out],
        scratch=[pltpu.VMEM((s, wide), F32), pltpu.VMEM((s, wide), F32), pltpu.VMEM((ns, DA_T, LANES), F32)],
        sem=("parallel", "parallel", "arbitrary"), args=(qk, qk, proj, bias, o, lse, do), ride=ride)


SB_Q = 256


def _sb_consts(after):
    r = lax.broadcasted_iota(jnp.int32, (2 * BLOCK, 2 * BLOCK), 0) % BLOCK
    c = lax.broadcasted_iota(jnp.int32, (2 * BLOCK, 2 * BLOCK), 1)
    tri = (r > c) if after else (r < c)
    return jnp.logical_or(c >= BLOCK, tri).astype(BF16)


def _split(x):
    hi = x.astype(BF16)
    lo = (x - hi.astype(F32)).astype(BF16)
    return jnp.concatenate([hi, lo], axis=1)


def _sb_fwd(proj, q_col0, k_col0, v_col0, batch, s, ride=None, streams=MIX_STREAMS):
    t = proj.shape[0]
    nq = s // SB_Q
    n_pairs = 4
    ns = streams
    wide = ns * LANES
    scale = HEAD_DIM ** -0.5

    def body(q_ref, k_ref, v_ref, o_ref, tot_ref, acc_ref, run_ref):
        i = pl.program_id(2)
        lo_q = lax.broadcasted_iota(jnp.int32, (SB_Q, LANES), 1) < HEAD_DIM
        lo_k = _lane_lo()
        mat = _sb_consts(True)
        row = lax.broadcasted_iota(jnp.int32, (2 * SB_Q, LANES), 0) % SB_Q
        ahead = row - lax.broadcasted_iota(jnp.int32, (2 * SB_Q, LANES), 1)
        acc_ref[...] = jnp.zeros_like(acc_ref)
        run_ref[...] = jnp.zeros_like(run_ref)
        qqs = [_stack_heads(q_ref[:, st * LANES:(st + 1) * LANES] * scale, lo_q) for st in range(ns)]

        def units(todo):
            def rows(j):
                return pl.ds(pl.multiple_of(j * BLOCK, BLOCK), BLOCK)

            zs = [lax.dot_general(qqs[st], k_ref[rows(j), st * LANES:(st + 1) * LANES], NT, preferred_element_type=F32)
                  for st, j, _ in todo]
            logs = []
            for z, (_, _, off) in zip(zs, todo):
                lsig = jnp.minimum(z, 0.0) - jnp.log(1.0 + jnp.exp(-jnp.abs(z)))
                lneg = lsig - z
                if off is not None:
                    lneg = jnp.where(ahead > off, lneg, 0.0)
                logs.append((lsig, _split(lneg)))
            sums = [lax.dot_general(cat, mat, NN, preferred_element_type=F32) for _, cat in logs]
            probs = []
            for (lsig, _), sm, (st, _, off) in zip(logs, sums, todo):
                run = run_ref[st]
                a = jnp.exp(lsig + run + sm[:, :BLOCK])
                if off is not None:
                    a = jnp.where(ahead > off, a, 0.0)
                run_ref[st] = run + sm[:, BLOCK:]
                probs.append(a.astype(BF16))
            for ab, (st, j, _) in zip(probs, todo):
                v = v_ref[rows(j), st * LANES:(st + 1) * LANES]
                vz = jnp.zeros_like(v)
                acc_ref[st] += (lax.dot_general(ab[:SB_Q], jnp.where(lo_k, v, vz), NN, preferred_element_type=F32)
                                + lax.dot_general(ab[SB_Q:], jnp.where(lo_k, vz, v), NN, preferred_element_type=F32))

        units([(st, 2 * i + 1, BLOCK) for st in range(ns)] + [(st, 2 * i, 0) for st in range(ns)])

        def pair(p, carry):
            jp = i - 1 - p
            units([(st, 2 * jp + 1, None) for st in range(ns)] + [(st, 2 * jp, None) for st in range(ns)])
            return carry

        lax.fori_loop(0, i, pair, 0)
        for st in range(ns):
            cols = slice(st * LANES, (st + 1) * LANES)
            o_ref[:, cols] = acc_ref[st].astype(BF16)
            tot_ref[:, cols] = jnp.where(lo_q, run_ref[st, 0:SB_Q, :], run_ref[st, SB_Q:2 * SB_Q, :])

    def seq(col0):
        return pl.BlockSpec((s, wide), lambda b, h, i: (b, col0 // ns + h))

    blk = pl.BlockSpec((SB_Q, wide), lambda b, h, i: (b * nq + i, h))
    return _call(
        body, name="attn_b_fwd", grid=(batch, n_pairs // ns, nq),
        in_specs=[pl.BlockSpec((SB_Q, wide), lambda b, h, i: (b * nq + i, q_col0 // ns + h)), seq(k_col0), seq(v_col0)],
        out_specs=[blk, blk],
        out_shape=[jax.ShapeDtypeStruct((t, n_pairs * LANES), BF16), jax.ShapeDtypeStruct((t, n_pairs * LANES), F32)],
        scratch=[pltpu.VMEM((ns, SB_Q, LANES), F32), pltpu.VMEM((ns, 2 * SB_Q, LANES), F32)],
        sem=("parallel", "parallel", "arbitrary"), args=(proj, proj, proj), ride=ride)


def _sb_bwd(proj, q_col0, k_col0, v_col0, tot, do, batch, s, ride=None, streams=SB_BWD_STREAMS):
    t = proj.shape[0]
    nq = s // SB_Q
    n_pairs = 4
    ns = streams
    wide = ns * LANES
    scale = HEAD_DIM ** -0.5

    def body(q_ref, k_ref, v_ref, tot_ref, do_ref, dq_ref, dk_ref, dv_ref, dk_acc, dv_acc, dq_acc, seen_ref, gsum_ref):
        i = pl.program_id(2)
        lo_q = lax.broadcasted_iota(jnp.int32, (SB_Q, LANES), 1) < HEAD_DIM
        lo_k = _lane_lo()

        @pl.when(i == 0)
        def _():
            dk_acc[...] = jnp.zeros_like(dk_acc)
            dv_acc[...] = jnp.zeros_like(dv_acc)

        mat_after = _sb_consts(True)
        mat_before = _sb_consts(False)[:BLOCK]
        row = lax.broadcasted_iota(jnp.int32, (2 * SB_Q, LANES), 0) % SB_Q
        ahead = row - lax.broadcasted_iota(jnp.int32, (2 * SB_Q, LANES), 1)
        dq_acc[...] = jnp.zeros_like(dq_acc)
        seen_ref[...] = jnp.zeros_like(seen_ref)
        gsum_ref[...] = jnp.zeros_like(gsum_ref)
        qqs, dds, totals = [], [], []
        for st in range(ns):
            cols = slice(st * LANES, (st + 1) * LANES)
            qqs.append(_stack_heads(q_ref[:, cols] * scale, lo_q))
            dds.append(_stack_heads(do_ref[:, cols], lo_q))
            tot_t = tot_ref[:, cols]
            totals.append(jnp.concatenate([jnp.broadcast_to(tot_t[:, 0:1], (SB_Q, LANES)),
                                           jnp.broadcast_to(tot_t[:, HEAD_DIM:HEAD_DIM + 1], (SB_Q, LANES))], axis=0))

        def units(todo):
            def rows(j):
                return pl.ds(pl.multiple_of(j * BLOCK, BLOCK), BLOCK)

            def cols(st):
                return slice(st * LANES, (st + 1) * LANES)

            prods = [(lax.dot_general(qqs[st], k_ref[rows(j), cols(st)], NT, preferred_element_type=F32),
                      lax.dot_general(dds[st], v_ref[rows(j), cols(st)], NT, preferred_element_type=F32))
                     for st, j, _ in todo]
            logs = []
            for (z, _), (_, _, off) in zip(prods, todo):
                lsig = jnp.minimum(z, 0.0) - jnp.log(1.0 + jnp.exp(-jnp.abs(z)))
                lneg = lsig - z
                if off is not None:
                    lneg = jnp.where(ahead > off, lneg, 0.0)
                logs.append((lsig, _split(lneg)))
            sums = [lax.dot_general(cat, mat_after, NN, preferred_element_type=F32) for _, cat in logs]
            gates = []
            for (lsig, _), sm, (_, da), (st, _, off) in zip(logs, sums, prods, todo):
                seen = seen_ref[st]
                a = jnp.exp(lsig + (totals[st] - seen - sm[:, BLOCK:]) + sm[:, :BLOCK])
                if off is not None:
                    a = jnp.where(ahead > off, a, 0.0)
                seen_ref[st] = seen + sm[:, BLOCK:]
                g = a * da
                gates.append((a.astype(BF16), g, g.astype(BF16)))
            gsums = [lax.dot_general(cat, mat_before, NN, preferred_element_type=F32) for _, _, cat in gates]
            outs = []
            for (lsig, _), (ab, g, _), gs, (st, _, off) in zip(logs, gates, gsums, todo):
                gsum = gsum_ref[st]
                dz = g - jnp.exp(lsig) * (g + gsum + gs[:, :BLOCK])
                if off is not None:
                    dz = jnp.where(ahead > off, dz, 0.0)
                gsum_ref[st] = gsum + gs[:, BLOCK:]
                outs.append((dz.astype(BF16), ab))
            for (dzb, ab), (st, j, _) in zip(outs, todo):
                k = k_ref[rows(j), cols(st)]
                kz = jnp.zeros_like(k)
                dq_acc[st] += (lax.dot_general(dzb[:SB_Q], jnp.where(lo_k, k, kz), NN, preferred_element_type=F32)
                               + lax.dot_general(dzb[SB_Q:], jnp.where(lo_k, kz, k), NN, preferred_element_type=F32))
                dk_acc[rows(j), cols(st)] += lax.dot_general(dzb, qqs[st], TN, preferred_element_type=F32)
                dv_acc[rows(j), cols(st)] += lax.dot_general(ab, dds[st], TN, preferred_element_type=F32)

        def pair(p, carry):
            units([(st, 2 * p, None) for st in range(ns)] + [(st, 2 * p + 1, None) for st in range(ns)])
            return carry

        lax.fori_loop(0, i, pair, 0)
        units([(st, 2 * i, 0) for st in range(ns)] + [(st, 2 * i + 1, BLOCK) for st in range(ns)])
        for st in range(ns):
            dq_ref[:, st * LANES:(st + 1) * LANES] = (dq_acc[st] * scale).astype(BF16)

        @pl.when(i == nq - 1)
        def _():
            dk_ref[...] = dk_acc[...].astype(BF16)
            dv_ref[...] = dv_acc[...].astype(BF16)

    def seq_in(col0):
        return pl.BlockSpec((s, wide), lambda b, h, i: (b, col0 // ns + h))

    blk = pl.BlockSpec((SB_Q, wide), lambda b, h, i: (b * nq + i, h))
    seq = pl.BlockSpec((s, wide), lambda b, h, i: (b, h))
    out = jax.ShapeDtypeStruct((t, n_pairs * LANES), BF16)
    return _call(
        body, name="attn_b_bwd", grid=(batch, n_pairs // ns, nq),
        in_specs=[pl.BlockSpec((SB_Q, wide), lambda b, h, i: (b * nq + i, q_col0 // ns + h)), seq_in(k_col0),
                  seq_in(v_col0), blk, blk],
        out_specs=[blk, seq, seq], out_shape=[out, out, out],
        scratch=[pltpu.VMEM((s, wide), F32), pltpu.VMEM((s, wide), F32), pltpu.VMEM((ns, SB_Q, LANES), F32),
                 pltpu.VMEM((ns, 2 * SB_Q, LANES), F32), pltpu.VMEM((ns, 2 * SB_Q, LANES), F32)],
        sem=("parallel", "parallel", "arbitrary"), args=(proj, proj, proj, tot, do), ride=ride)


MEM_Q_TILE = 1024


def _mem_fwd(q, kv, batch, s, n_mem):
    t, width = q.shape
    tq = min(MEM_Q_TILE, s)
    nq = s // tq
    scale = MEM_HEAD_DIM ** -0.5

    def body(q_ref, kv_ref, o_ref):
        for h in range(N_HEADS_MEM):
            cols = slice(h * MEM_HEAD_DIM, (h + 1) * MEM_HEAD_DIM)
            k = kv_ref[:, cols]
            v = kv_ref[:, width + h * MEM_HEAD_DIM: width + (h + 1) * MEM_HEAD_DIM]
            sc = lax.dot_general(q_ref[:, cols], k, NT, preferred_element_type=F32) * scale
            p = jnp.exp(sc - jnp.max(sc, axis=1, keepdims=True))
            p = p / jnp.sum(p, axis=1, keepdims=True)
            o_ref[:, cols] = lax.dot_general(p.astype(BF16), v, NN, preferred_element_type=F32).astype(BF16)

    return pl.pallas_call(
        body, name="mem_attn_fwd", grid=(batch, nq),
        in_specs=[pl.BlockSpec((tq, width), lambda b, i: (b * nq + i, 0)),
                  pl.BlockSpec((n_mem, 2 * width), lambda b, i: (b, 0))],
        out_specs=pl.BlockSpec((tq, width), lambda b, i: (b * nq + i, 0)),
        out_shape=jax.ShapeDtypeStruct((t, width), BF16),
        compiler_params=_params(("parallel", "parallel")),
    )(q, kv)


def _mem_bwd(q, kv, do, batch, s, n_mem):
    t, width = q.shape
    tq = min(MEM_Q_TILE, s)
    nq = s // tq
    scale = MEM_HEAD_DIM ** -0.5

    def body(q_ref, kv_ref, do_ref, dq_ref, dkv_ref, acc):
        i = pl.program_id(1)

        @pl.when(i == 0)
        def _():
            acc[...] = jnp.zeros_like(acc)

        for h in range(N_HEADS_MEM):
            cols = slice(h * MEM_HEAD_DIM, (h + 1) * MEM_HEAD_DIM)
            vcols = slice(width + h * MEM_HEAD_DIM, width + (h + 1) * MEM_HEAD_DIM)
            qh, k, v, doh = q_ref[:, cols], kv_ref[:, cols], kv_ref[:, vcols], do_ref[:, cols]
            sc = lax.dot_general(qh, k, NT, preferred_element_type=F32) * scale
            p = jnp.exp(sc - jnp.max(sc, axis=1, keepdims=True))
            p = p / jnp.sum(p, axis=1, keepdims=True)
            dp = lax.dot_general(doh, v, NT, preferred_element_type=F32)
            ds = (p * (dp - jnp.sum(p * dp, axis=1, keepdims=True)) * scale).astype(BF16)
            dq_ref[:, cols] = lax.dot_general(ds, k, NN, preferred_element_type=F32).astype(BF16)
            acc[:, cols] += lax.dot_general(ds, qh, TN, preferred_element_type=F32)
            acc[:, vcols] += lax.dot_general(p.astype(BF16), doh, TN, preferred_element_type=F32)

        @pl.when(i == nq - 1)
        def _():
            dkv_ref[...] = acc[...].astype(BF16)

    row = pl.BlockSpec((tq, width), lambda b, i: (b * nq + i, 0))
    kvs = pl.BlockSpec((n_mem, 2 * width), lambda b, i: (b, 0))
    return pl.pallas_call(
        body, name="mem_attn_bwd", grid=(batch, nq),
        in_specs=[row, kvs, row], out_specs=[row, kvs],
        out_shape=[jax.ShapeDtypeStruct((t, width), BF16), jax.ShapeDtypeStruct((batch * n_mem, 2 * width), BF16)],
        scratch_shapes=[pltpu.VMEM((n_mem, 2 * width), F32)],
        compiler_params=_params(("parallel", "arbitrary")),
    )(q, kv, do)


def _mixer_fwd(o_a, o_b, w_a, w_b, proj, gate_col0, w_out, x, g, w_q):
    t, width = o_a.shape
    d = w_a.shape[1]
    nq_cols = w_q.shape[1]
    tm = min(ROW_TILE, t)
    gb0 = gate_col0 * LANES // d

    def body(oa_ref, ob_ref, wa_ref, wb_ref, ga_ref, gb_ref, wo_ref, x_ref, g_ref, wq_ref, ua_ref, ub_ref, mix_ref,
             n_ref, h_ref, q_ref):
        ua = lax.dot_general(oa_ref[...], wa_ref[...], NN, preferred_element_type=F32)
        ub = lax.dot_general(ob_ref[...], wb_ref[...], NN, preferred_element_type=F32)
        ua_ref[...] = ua.astype(BF16)
        ub_ref[...] = ub.astype(BF16)
        mixed = (jax.nn.sigmoid(ga_ref[...].astype(F32)) * ua + jax.nn.sigmoid(gb_ref[...].astype(F32)) * ub).astype(BF16)
        mix_ref[...] = mixed
        h = lax.dot_general(mixed, wo_ref[...], NN, preferred_element_type=F32) + x_ref[...]
        h_ref[...] = h
        r = lax.rsqrt(jnp.mean(h * h, axis=-1, keepdims=True) + RMS_EPS)
        n = (h * r * g_ref[...]).astype(BF16)
        n_ref[...] = n
        q_ref[...] = lax.dot_general(n, wq_ref[...], NN, preferred_element_type=F32).astype(BF16)

    row = pl.BlockSpec((tm, width), lambda i: (i, 0))
    wsp = pl.BlockSpec((width, d), lambda i: (0, 0))
    out = pl.BlockSpec((tm, d), lambda i: (i, 0))
    osh = jax.ShapeDtypeStruct((t, d), BF16)
    return pl.pallas_call(
        body, name="mixer_fwd", grid=(t // tm,),
        in_specs=[row, row, wsp, wsp,
                  pl.BlockSpec((tm, d), lambda i: (i, gb0)), pl.BlockSpec((tm, d), lambda i: (i, gb0 + 1)),
                  pl.BlockSpec((d, d), lambda i: (0, 0)), out, pl.BlockSpec((1, d), lambda i: (0, 0)),
                  pl.BlockSpec((d, nq_cols), lambda i: (0, 0))],
        out_specs=[out, out, out, out, out, pl.BlockSpec((tm, nq_cols), lambda i: (i, 0))],
        out_shape=[osh, osh, osh, osh, jax.ShapeDtypeStruct((t, d), F32), jax.ShapeDtypeStruct((t, nq_cols), BF16)],
        compiler_params=_params(("parallel",)),
    )(o_a, o_b, w_a, w_b, proj, proj, w_out, x, g, w_q)


def _mixer_bwd(dh, w_out, ua, ub, proj, gate_col0, w_a, w_b):
    t, d = dh.shape
    width = w_a.shape[0]
    tm = min(ROW_TILE, t)
    nc = d // LANES

    def body(dh_ref, w_ref, ua_ref, ub_ref, ga_ref, gb_ref, wa_ref, wb_ref, dua_ref, dub_ref, dg_ref, doa_ref, dob_ref):
        dm = lax.dot_general(dh_ref[...], w_ref[...], NT, preferred_element_type=F32)
        sa = jax.nn.sigmoid(ga_ref[...].astype(F32))
        sb = jax.nn.sigmoid(gb_ref[...].astype(F32))
        dua = (dm * sa).astype(BF16)
        dub = (dm * sb).astype(BF16)
        dua_ref[...] = dua
        dub_ref[...] = dub
        dg_ref[:, 0:d] = (dm * ua_ref[...].astype(F32) * sa * (1.0 - sa)).astype(BF16)
        dg_ref[:, d:2 * d] = (dm * ub_ref[...].astype(F32) * sb * (1.0 - sb)).astype(BF16)
        doa_ref[...] = lax.dot_general(dua, wa_ref[...], NT, preferred_element_type=F32).astype(BF16)
        dob_ref[...] = lax.dot_general(dub, wb_ref[...], NT, preferred_element_type=F32).astype(BF16)

    row = pl.BlockSpec((tm, d), lambda i: (i, 0))
    wsp = pl.BlockSpec((width, d), lambda i: (0, 0))
    osp = pl.BlockSpec((tm, width), lambda i: (i, 0))
    return pl.pallas_call(
        body, name="mixer_bwd", grid=(t // tm,),
        in_specs=[row, pl.BlockSpec((d, d), lambda i: (0, 0)), row, row,
                  pl.BlockSpec((tm, d), lambda i: (i, gate_col0 // nc)),
                  pl.BlockSpec((tm, d), lambda i: (i, gate_col0 // nc + 1)), wsp, wsp],
        out_specs=[row, row, pl.BlockSpec((tm, 2 * d), lambda i: (i, 0)), osp, osp],
        out_shape=[jax.ShapeDtypeStruct((t, d), BF16), jax.ShapeDtypeStruct((t, d), BF16),
                   jax.ShapeDtypeStruct((t, 2 * d), BF16), jax.ShapeDtypeStruct((t, width), BF16),
                   jax.ShapeDtypeStruct((t, width), BF16)],
        compiler_params=_params(("parallel",)),
    )(dh, w_out, ua, ub, proj, proj, w_a, w_b)


FFN_COLS = 1024
FFN_CHUNK = 256


def _ffn_up(n, w_gate, w_up):
    t, d = n.shape
    hidden = w_gate.shape[0]
    tm = min(2 * ROW_TILE, t)
    tn = min(FFN_COLS, hidden)
    tc = min(FFN_CHUNK, tn)

    def body(n_ref, wg_ref, wu_ref, hg_ref, hu_ref, act_ref):
        for c in range(0, tn, tc):
            hg = lax.dot_general(n_ref[...], wg_ref[c:c + tc, :], NT, preferred_element_type=F32)
            hu = lax.dot_general(n_ref[...], wu_ref[c:c + tc, :], NT, preferred_element_type=F32)
            hg_ref[:, c:c + tc] = hg.astype(BF16)
            hu_ref[:, c:c + tc] = hu.astype(BF16)
            act_ref[:, c:c + tc] = (hg * jax.nn.sigmoid(hg) * hu).astype(BF16)

    wsp = pl.BlockSpec((tn, d), lambda j, i: (j, 0))
    out = pl.BlockSpec((tm, tn), lambda j, i: (i, j))
    osh = jax.ShapeDtypeStruct((t, hidden), BF16)
    return pl.pallas_call(
        body, name="ffn_up", grid=(hidden // tn, t // tm),
        in_specs=[pl.BlockSpec((tm, d), lambda j, i: (i, 0)), wsp, wsp],
        out_specs=[out, out, out], out_shape=[osh, osh, osh],
        compiler_params=_params(("parallel", "parallel")),
    )(n, w_gate, w_up)


def _ffn_bwd(dh, w_down, w_gate, w_up, hg, hu, x, g, dres, w_prev):
    t, d = dh.shape
    hidden = w_down.shape[0]
    q = w_prev.shape[0]
    tm = min(ROW_TILE, t)
    tn = min(FFN_COLS, hidden)
    nj = hidden // tn

    def body(dh_ref, wd_ref, wg_ref, wu_ref, hg_ref, hu_ref, x_ref, g_ref, r_ref, wp_ref, dhg_ref, dhu_ref, dx_ref,
             dxb_ref, dg_ref, do_ref, acc):
        j, i = pl.program_id(0), pl.program_id(1)
        dact = lax.dot_general(dh_ref[...], wd_ref[...], NT, preferred_element_type=F32)
        hg = hg_ref[...].astype(F32)
        sg = jax.nn.sigmoid(hg)
        dhu = (dact * hg * sg).astype(BF16)
        dhg = (dact * hu_ref[...].astype(F32) * sg * (1.0 + hg * (1.0 - sg))).astype(BF16)
        dhu_ref[...] = dhu
        dhg_ref[...] = dhg
        part = (lax.dot_general(dhg, wg_ref[...], NN, preferred_element_type=F32)
                + lax.dot_general(dhu, wu_ref[...], NN, preferred_element_type=F32))

        @pl.when(j == 0)
        def _():
            acc[i] = part

        @pl.when(j > 0)
        def _():
            acc[i] += part

        @pl.when(jnp.logical_and(j == 0, i == 0))
        def _():
            dg_ref[...] = jnp.zeros_like(dg_ref)

        @pl.when(j == nj - 1)
        def _():
            dx, dg = _rms_bwd_rows(acc[i], x_ref[...], g_ref[...], r_ref[...])
            dx_ref[...] = dx
            dxb = dx.astype(BF16)
            dxb_ref[...] = dxb
            dg_ref[...] += dg
            do_ref[...] = lax.dot_general(dxb, wp_ref[...], NT, preferred_element_type=F32).astype(BF16)

    hid = pl.BlockSpec((tm, tn), lambda j, i: (i, j))
    wsp = pl.BlockSpec((tn, d), lambda j, i: (j, 0), pipeline_mode=pl.Buffered(1))
    late = pl.BlockSpec((tm, d), lambda j, i: (jnp.where(j == nj - 1, i, 0), 0))
    late_q = pl.BlockSpec((tm, q), lambda j, i: (jnp.where(j == nj - 1, i, 0), 0))
    vec = pl.BlockSpec((1, d), lambda j, i: (0, 0))
    osh = jax.ShapeDtypeStruct((t, hidden), BF16)
    return pl.pallas_call(
        body, name="ffn_bwd", grid=(nj, t // tm),
        in_specs=[pl.BlockSpec((tm, d), lambda j, i: (i, 0)), wsp, wsp, wsp, hid, hid, late, vec, late,
                  pl.BlockSpec((q, d), lambda j, i: (0, 0), pipeline_mode=pl.Buffered(1))],
        out_specs=[hid, hid, late, late, vec, late_q],
        out_shape=[osh, osh, jax.ShapeDtypeStruct((t, d), F32), jax.ShapeDtypeStruct((t, d), BF16),
                   jax.ShapeDtypeStruct((1, d), F32), jax.ShapeDtypeStruct((t, q), BF16)],
        scratch_shapes=[pltpu.VMEM((t // tm, tm, d), F32)],
        compiler_params=_params(("arbitrary", "arbitrary")),
    )(dh, w_down, w_gate, w_up, hg, hu, x, g, dres, w_prev)


MM_ROWS = 1024


def _mm_w(name, a, w, out_dtype, dims=NN):
    t, k = a.shape
    n = w.shape[1] if dims == NN else w.shape[0]
    tm, tn = min(MM_ROWS, t), min(1024, n)
    o_spec = pl.BlockSpec((tm, tn), lambda j, i: (i, j))
    b_spec = pl.BlockSpec((k, tn), lambda j, i: (0, j)) if dims == NN else pl.BlockSpec((tn, k), lambda j, i: (j, 0))
    return _mm(name, a, w, grid=(n // tn, t // tm), a_spec=pl.BlockSpec((tm, k), lambda j, i: (i, 0)), b_spec=b_spec,
               o_shape=(t, n), o_spec=o_spec, dims=dims, out_dtype=out_dtype)


def _mm_res_norm(name, a, w, res, g):
    t, k = a.shape
    d = w.shape[1]
    tm = min(MM_ROWS, t)

    def body(a_ref, w_ref, r_ref, g_ref, h_ref, n_ref):
        h = lax.dot_general(a_ref[...], w_ref[...], NN, preferred_element_type=F32) + r_ref[...]
        h_ref[...] = h
        r = lax.rsqrt(jnp.mean(h * h, axis=-1, keepdims=True) + RMS_EPS)
        n_ref[...] = (h * r * g_ref[...]).astype(BF16)

    row = pl.BlockSpec((tm, d), lambda i: (i, 0))
    return pl.pallas_call(
        body, name=name, grid=(t // tm,),
        in_specs=[pl.BlockSpec((tm, k), lambda i: (i, 0)), pl.BlockSpec((k, d), lambda i: (0, 0)), row,
                  pl.BlockSpec((1, d), lambda i: (0, 0))],
        out_specs=[row, row], out_shape=[jax.ShapeDtypeStruct((t, d), F32), jax.ShapeDtypeStruct((t, d), BF16)],
        compiler_params=_params(("parallel",)),
    )(a, w, res, g)


WGRAD_COLS = 256


def _wgrad(name, a, g, tk=1024, tn=1024):
    t, k = a.shape
    n = g.shape[1]
    tm, tk, tn = min(2 * MM_ROWS, t), min(tk, k), min(tn, n)
    nr = t // tm
    tc = min(WGRAD_COLS, tn)

    def body(a_ref, g_ref, o_ref, *acc):
        def run(first, last):
            for c in range(0, tn, tc):
                p = lax.dot_general(a_ref[...], g_ref[:, c:c + tc], TN, preferred_element_type=F32)
                if not first:
                    p += acc[0][:, c:c + tc]
                if last:
                    o_ref[:, c:c + tc] = p.astype(BF16)
                else:
                    acc[0][:, c:c + tc] = p

        if nr == 1:
            run(True, True)
            return
        r = pl.program_id(2)
        pl.when(r == 0)(functools.partial(run, True, False))
        if nr > 2:
            pl.when(jnp.logical_and(r > 0, r < nr - 1))(functools.partial(run, False, False))
        pl.when(r == nr - 1)(functools.partial(run, False, True))

    return pl.pallas_call(
        body, name=name, grid=(k // tk, n // tn, nr),
        in_specs=[pl.BlockSpec((tm, tk), lambda p, q, r: (r, p)), pl.BlockSpec((tm, tn), lambda p, q, r: (r, q))],
        out_specs=pl.BlockSpec((tk, tn), lambda p, q, r: (p, q)), out_shape=jax.ShapeDtypeStruct((k, n), BF16),
        scratch_shapes=[pltpu.VMEM((tk, tn), F32)] if nr > 1 else [],
        compiler_params=_params(("parallel", "parallel", "arbitrary")),
    )(a, g)


def _peers():
    x, y, c = lax.axis_index("x"), lax.axis_index("y"), lax.axis_index("c")
    me = 4 * x + 2 * y + c
    out = []
    for k in range(1, N_DEV):
        kx, ky, kc = (k >> 2) & 1, (k >> 1) & 1, k & 1
        px = 1 - x if kx else x
        py = 1 - y if ky else y
        pc = 1 - c if kc else c
        out.append(((px, py, pc), 4 * px + 2 * py + pc))
    return me, out


def _cast_weights(ws, pad_rows):
    def body(*refs):
        n = len(refs) // 2
        for i_ref, o_ref, pr in zip(refs[:n], refs[n:], pad_rows):
            r, c = i_ref.shape
            o_ref[0:r, :] = i_ref[...].astype(BF16)
            if pr:
                o_ref[r:r + pr, :] = jnp.zeros((pr, c), BF16)

    return pl.pallas_call(
        body, name="cast_weights", in_specs=[VMEM] * len(ws), out_specs=[VMEM] * len(ws),
        out_shape=[jax.ShapeDtypeStruct((w.shape[0] + pr, w.shape[1]), BF16) for w, pr in zip(ws, pad_rows)],
    )(*ws)


def _window(ref, j, c):
    return ref.at[:, pl.ds(pl.multiple_of(j * c, LANES), c)]


def _direct_copies(ins, outs, sems, gather, cols, landed):
    send_sems, recv_sems, loc_sems = sems
    n_peer = N_DEV - 1
    me, peers = _peers()

    def src(w, j):
        if gather:
            return ins[w]
        return _window(ins[w], j, cols[w]) if cols[w] else ins[w].at[j]

    def dst(w, j):
        return _window(outs[w], j, cols[w]) if gather and cols[w] else outs[w].at[j]

    local = [pltpu.make_async_copy(src(w, me), dst(w, me), loc_sems.at[w]) for w in range(len(ins))]
    remote = [pltpu.make_async_remote_copy(
        src_ref=src(w, idx), dst_ref=dst(w, idx if landed else me),
        send_sem=send_sems.at[w * n_peer + k], recv_sem=recv_sems.at[w * n_peer + k],
        device_id=dev, device_id_type=pl.DeviceIdType.MESH)
        for k, (dev, idx) in reversed(list(enumerate(peers))) for w in range(len(ins))]
    return local, remote


OTHER_CHIPS = (2, 4, 6)


def _gather_copies(ins, outs, sems, cols):
    send_sems, recv_sems, loc_sems = sems
    x, y, c = lax.axis_index("x"), lax.axis_index("y"), lax.axis_index("c")
    me = 4 * x + 2 * y + c
    n_pair = N_DEV - 1

    def dev(mask):
        return (1 - x if mask & 4 else x, 1 - y if mask & 2 else y, 1 - c if mask & 1 else c)

    def slot(w, mask):
        j = jnp.bitwise_xor(me, mask)
        return _window(outs[w], j, cols[w]) if cols[w] else outs[w].at[j]

    def remote(w, pair, src, to_slot, target):
        return pltpu.make_async_remote_copy(src_ref=src, dst_ref=slot(w, to_slot), send_sem=send_sems.at[w * n_pair + pair],
                                            recv_sem=recv_sems.at[w * n_pair + pair], device_id=dev(target),
                                            device_id_type=pl.DeviceIdType.MESH)

    ws = range(len(ins))
    return dict(
        local=[pltpu.make_async_copy(ins[w], slot(w, 0), loc_sems.at[w]) for w in ws],
        to_chips=[remote(w, 1 + t, ins[w], 0, m) for t, m in enumerate(OTHER_CHIPS) for w in ws],
        to_core=[remote(w, 0, ins[w], 0, 1) for w in ws],
        from_chips=[remote(w, 1 + t, ins[w], m, 0) for t, m in enumerate(OTHER_CHIPS) for w in ws],
        pass_on=[remote(w, 4 + t, slot(w, m), m, 1) for t, m in enumerate(OTHER_CHIPS) for w in ws],
        from_core=[remote(w, 0, ins[w], 1, 0) for w in ws]
        + [remote(w, 4 + t, ins[w], m + 1, 0) for t, m in enumerate(OTHER_CHIPS) for w in ws])


TWO_LEVEL = "gather in two levels"


def _exchange_start(ins, outs, sems, gather, cols):
    if gather == TWO_LEVEL:
        cps = _gather_copies(ins, outs, sems, cols)
        for cp in cps["local"] + cps["to_chips"] + cps["to_core"]:
            cp.start()
    else:
        local, remote = _direct_copies(ins, outs, sems, gather, cols, False)
        for cp in local + remote:
            cp.start()


def _exchange_pass_on(ins, outs, sems, gather, cols, chips):
    if gather == TWO_LEVEL:
        cps = _gather_copies(ins, outs, sems, cols)
        n = len(ins)
        for t in chips:
            for arrived, onward in zip(cps["from_chips"][t * n:(t + 1) * n], cps["pass_on"][t * n:(t + 1) * n]):
                arrived.wait_recv()
                onward.start()


def _exchange_wait(ins, outs, sems, gather, cols):
    if gather == TWO_LEVEL:
        cps = _gather_copies(ins, outs, sems, cols)
        for cp in cps["local"]:
            cp.wait()
        for cp in cps["to_chips"] + cps["to_core"] + cps["pass_on"]:
            cp.wait_send()
        for cp in cps["from_core"]:
            cp.wait_recv()
    else:
        local, remote = _direct_copies(ins, outs, sems, gather, cols, True)
        for cp in local:
            cp.wait()
        for cp in remote:
            cp.wait_send()
            cp.wait_recv()


def _exchange_shapes(arrs, gather, cols):
    n = len(arrs)
    out_shape = []
    for a, c in zip(arrs, cols):
        if gather:
            shape = (a.shape[0], N_DEV * c) if c else (N_DEV,) + a.shape
        else:
            shape = (N_DEV, a.shape[0], c) if c else a.shape
        out_shape.append(jax.ShapeDtypeStruct(shape, a.dtype))
    sems = [pltpu.SemaphoreType.DMA((n * (N_DEV - 1),)), pltpu.SemaphoreType.DMA((n * (N_DEV - 1),)),
            pltpu.SemaphoreType.DMA((n,))]
    return out_shape, sems


def _call(body, *, name, grid, in_specs, out_specs, out_shape, scratch, sem, args, ride=None):
    if ride is None:
        outs = pl.pallas_call(body, name=name, grid=grid, in_specs=in_specs, out_specs=out_specs, out_shape=out_shape,
                              scratch_shapes=scratch, compiler_params=_params(sem))(*args)
        return outs, None
    arrs, gather, cols = ride
    n, n_in, n_out, n_scr = len(arrs), len(in_specs), len(out_specs), len(scratch)
    x_shape, x_sems = _exchange_shapes(arrs, gather, cols)

    def riding(*refs):
        ins, x_ins = refs[:n_in], refs[n_in:n_in + n]
        outs = refs[n_in + n:n_in + n + n_out]
        x_outs = refs[n_in + n + n_out:n_in + 2 * n + n_out]
        scr = refs[n_in + 2 * n + n_out:n_in + 2 * n + n_out + n_scr]
        sems = refs[n_in + 2 * n + n_out + n_scr:]
        def at(step):
            return functools.reduce(jnp.logical_and, [pl.program_id(a) == v for a, v in enumerate(step)])

        @pl.when(at((0,) * len(grid)))
        def _():
            _exchange_start(x_ins, x_outs, sems, gather, cols)

        @pl.when(at((grid[0] // 2,) + (0,) * (len(grid) - 2) + (grid[-1] // 2,)))
        def _():
            _exchange_pass_on(x_ins, x_outs, sems, gather, cols, (0, 1))

        @pl.when(at((grid[0] // 2,) + (0,) * (len(grid) - 2) + (3 * grid[-1] // 4,)))
        def _():
            _exchange_pass_on(x_ins, x_outs, sems, gather, cols, (2,))

        body(*ins, *outs, *scr)

        @pl.when(at(tuple(g - 1 for g in grid)))
        def _():
            _exchange_wait(x_ins, x_outs, sems, gather, cols)

    res = pl.pallas_call(
        riding, name=name, grid=grid, in_specs=list(in_specs) + [ANY] * n, out_specs=list(out_specs) + [ANY] * n,
        out_shape=list(out_shape) + x_shape, scratch_shapes=list(scratch) + x_sems,
        compiler_params=_params(("arbitrary",) * len(grid)))(*args, *arrs)
    return res[:n_out], res[n_out:]


def _my_block():
    return (4 * lax.axis_index("x") + 2 * lax.axis_index("y") + lax.axis_index("c")).astype(jnp.int32).reshape(1)


def _proj_in_gather(x, g, w_shard):
    t, k = x.shape
    cs = w_shard.shape[1]
    tm = min(MM_ROWS, t)
    ni = t // tm
    arrival = (0, 1, 2, 4, 3, 5, 6, 7)

    def mask_at(s):
        return jnp.where(s == 3, 4, jnp.where(s == 4, 3, s))

    def body(me_ref, x_ref, g_ref, w_hbm, o_ref, all_hbm, n_hbm, w_vmem, n_vmem, send_sems, recv_sems, loc_sems,
             load_sems, n_sem):
        s, i = pl.program_id(0), pl.program_id(1)
        cps = _gather_copies([w_hbm], [all_hbm], (send_sems, recv_sems, loc_sems), (cs,))
        by_mask = {0: cps["local"][0], 1: cps["from_core"][0]}
        for t_chip, m in enumerate(OTHER_CHIPS):
            by_mask[m] = cps["from_chips"][t_chip]
            by_mask[m + 1] = cps["from_core"][1 + t_chip]
        arrived = [by_mask[m] for m in arrival]

        def load(step):
            src = w_hbm if step == 0 else _window(all_hbm, jnp.bitwise_xor(me_ref[0], arrival[step]), cs)
            return pltpu.make_async_copy(src, w_vmem.at[step % 2], load_sems.at[step % 2])

        @pl.when(jnp.logical_and(s == 0, i == 0))
        def _():
            for cp in cps["local"] + cps["to_chips"] + cps["to_core"]:
                cp.start()
            load(0).start()

        for step, mask in enumerate(arrival):
            @pl.when(jnp.logical_and(s == step, i == 0))
            def _(step=step):
                load(step).wait()

            if step + 1 < N_DEV:
                @pl.when(jnp.logical_and(s == step, i == min(1, ni - 1)))
                def _(step=step):
                    arrived[step + 1].wait_recv()
                    if arrival[step + 1] in OTHER_CHIPS:
                        cps["pass_on"][OTHER_CHIPS.index(arrival[step + 1])].start()
                    load(step + 1).start()

        @pl.when(s == 0)
        def _():
            xf = x_ref[...]
            r = lax.rsqrt(jnp.mean(xf * xf, axis=-1, keepdims=True) + RMS_EPS)
            n_vmem[i] = (xf * r * g_ref[...]).astype(BF16)
            keep = pltpu.make_async_copy(n_vmem.at[i], n_hbm.at[pl.ds(pl.multiple_of(i * tm, tm), tm), :], n_sem)
            keep.start()
            keep.wait()

        o_ref[...] = lax.dot_general(n_vmem[i], w_vmem[s % 2], NN, preferred_element_type=F32).astype(BF16)

        @pl.when(jnp.logical_and(s == N_DEV - 1, i == ni - 1))
        def _():
            cps["local"][0].wait()
            for cp in cps["to_chips"] + cps["to_core"] + cps["pass_on"]:
                cp.wait_send()

    return pl.pallas_call(
        body, name="proj_in",
        grid_spec=pltpu.PrefetchScalarGridSpec(
            num_scalar_prefetch=1, grid=(N_DEV, ni),
            in_specs=[pl.BlockSpec((tm, k), lambda s, i, me: (jnp.where(s == 0, i, 0), 0)),
                      pl.BlockSpec((1, k), lambda s, i, me: (0, 0)), ANY],
            out_specs=[pl.BlockSpec((tm, cs), lambda s, i, me: (i, jnp.bitwise_xor(me[0], mask_at(s)))), ANY, ANY],
            scratch_shapes=[pltpu.VMEM((2, k, cs), BF16), pltpu.VMEM((ni, tm, k), BF16),
                            pltpu.SemaphoreType.DMA((N_DEV - 1,)), pltpu.SemaphoreType.DMA((N_DEV - 1,)),
                            pltpu.SemaphoreType.DMA((1,)), pltpu.SemaphoreType.DMA((2,)), pltpu.SemaphoreType.DMA]),
        out_shape=[jax.ShapeDtypeStruct((t, N_DEV * cs), BF16), jax.ShapeDtypeStruct((k, N_DEV * cs), BF16),
                   jax.ShapeDtypeStruct((t, k), BF16)],
        compiler_params=_params(("arbitrary", "arbitrary")),
    )(_my_block(), x, g, w_shard)


def _gw_in_scatter(a, g):
    t, k = a.shape
    cs = g.shape[1] // N_DEV
    tm = min(MM_ROWS, t)
    nr = t // tm
    n_chip = N_DEV // 2
    chips = (6, 4, 2, 0)

    def body(me_ref, a_ref, g_ref, out_hbm, acc, stage, other, core_send, core_recv, chip_send, chip_recv, loc_sem):
        s, r = pl.program_id(0), pl.program_id(1)
        x, y, c = lax.axis_index("x"), lax.axis_index("y"), lax.axis_index("c")
        my_chip = 2 * x + y
        part = lax.dot_general(a_ref[...], g_ref[...], TN, preferred_element_type=F32)

        def to_core(m):
            return pltpu.make_async_remote_copy(src_ref=stage.at[0], dst_ref=other.at[m], send_sem=core_send.at[m],
                                                recv_sem=core_recv.at[m], device_id=(x, y, 1 - c),
                                                device_id_type=pl.DeviceIdType.MESH)

        def to_chip(m, landed):
            mask = chips[m]
            there = (1 - x if mask & 4 else x, 1 - y if mask & 2 else y, c)
            slot = (2 * there[0] + there[1]) if landed else my_chip
            return pltpu.make_async_remote_copy(src_ref=stage.at[1], dst_ref=out_hbm.at[slot], send_sem=chip_send.at[m],
                                                recv_sem=chip_recv.at[m], device_id=there,
                                                device_id_type=pl.DeviceIdType.MESH)

        local = pltpu.make_async_copy(stage.at[1], out_hbm.at[my_chip], loc_sem)

        @pl.when(r == 0)
        def _():
            acc[...] = part

        @pl.when(r > 0)
        def _():
            acc[...] += part

        for step in range(N_DEV):
            m = step // 2

            @pl.when(jnp.logical_and(s == step, r == nr - 1))
            def _(step=step, m=m):
                if step % 2 == 0:
                    if m > 0:
                        to_core(m - 1).wait_send()
                    stage[0] = acc[...].astype(BF16)
                    to_core(m).start()
                else:
                    if m > 0:
                        to_chip(m - 1, False).wait_send()
                    to_core(m).wait_recv()
                    stage[1] = (acc[...] + other[m].astype(F32)).astype(BF16)
                    if m < n_chip - 1:
                        to_chip(m, False).start()
                    else:
                        local.start()
                        to_core(m).wait_send()
                        local.wait()
                        for mm in range(n_chip - 1):
                            to_chip(mm, True).wait_recv()

    return pl.pallas_call(
        body, name="gw_in",
        grid_spec=pltpu.PrefetchScalarGridSpec(
            num_scalar_prefetch=1, grid=(N_DEV, nr),
            in_specs=[pl.BlockSpec((tm, k), lambda s, r, me: (r, 0)),
                      pl.BlockSpec((tm, cs), lambda s, r, me: (r, jnp.bitwise_xor(me[0], N_DEV - 1 - s)))],
            out_specs=ANY,
            scratch_shapes=[pltpu.VMEM((k, cs), F32), pltpu.VMEM((2, k, cs), BF16), pltpu.VMEM((n_chip, k, cs), BF16),
                            pltpu.SemaphoreType.DMA((n_chip,)), pltpu.SemaphoreType.DMA((n_chip,)),
                            pltpu.SemaphoreType.DMA((n_chip - 1,)), pltpu.SemaphoreType.DMA((n_chip - 1,)),
                            pltpu.SemaphoreType.DMA]),
        out_shape=jax.ShapeDtypeStruct((n_chip, k, cs), BF16),
        compiler_params=_params(("arbitrary", "arbitrary")),
    )(_my_block(), a, g)


SMALL_ROWS = 8


def _allreduce_small(parts, loss_part):
    n, d = len(parts), parts[0].shape[1]

    def body(*refs):
        part_refs, loss_ref, o_ref = refs[:n], refs[n], refs[n + 1]
        mine_ref, all_ref, send_sems, recv_sems = refs[n + 2:]
        me, peers = _peers()
        mine_ref[...] = jnp.zeros_like(mine_ref)
        for i, p_ref in enumerate(part_refs):
            mine_ref[i:i + 1, :] = p_ref[...]
        mine_ref[SMALL_ROWS - 1:SMALL_ROWS, 0:LANES] = loss_ref[0:1, :]
        all_ref[me] = mine_ref[...]
        for k, (dev, idx) in enumerate(peers):
            pltpu.make_async_remote_copy(src_ref=mine_ref, dst_ref=all_ref.at[me], send_sem=send_sems.at[k],
                                         recv_sem=recv_sems.at[k], device_id=dev,
                                         device_id_type=pl.DeviceIdType.MESH).start()
        for k, (dev, idx) in enumerate(peers):
            cp = pltpu.make_async_remote_copy(src_ref=mine_ref, dst_ref=all_ref.at[idx], send_sem=send_sems.at[k],
                                              recv_sem=recv_sems.at[k], device_id=dev,
                                              device_id_type=pl.DeviceIdType.MESH)
            cp.wait_send()
            cp.wait_recv()
        tot = all_ref[0]
        for dvc in range(1, N_DEV):
            tot = tot + all_ref[dvc]
        o_ref[...] = tot

    return pl.pallas_call(
        body, name="allreduce_small", in_specs=[VMEM] * (n + 1), out_specs=VMEM,
        out_shape=jax.ShapeDtypeStruct((SMALL_ROWS, d), F32),
        scratch_shapes=[pltpu.VMEM((SMALL_ROWS, d), F32), pltpu.VMEM((N_DEV, SMALL_ROWS, d), F32),
                        pltpu.SemaphoreType.DMA((N_DEV - 1,)), pltpu.SemaphoreType.DMA((N_DEV - 1,))],
    )(*parts, loss_part)


def _adam_math(g, w, m, v):
    m_new = ADAM_B1 * m + (1.0 - ADAM_B1) * g
    v_new = ADAM_B2 * v + (1.0 - ADAM_B2) * (g * g)
    m_hat = m_new / (1.0 - ADAM_B1 ** ADAM_STEP)
    v_hat = v_new / (1.0 - ADAM_B2 ** ADAM_STEP)
    delta = -ADAM_LR * (m_hat / (jnp.sqrt(v_hat) + ADAM_EPS) + ADAM_WD * w)
    return delta, m_new, v_new


def _adam(name, pieces, w, m, v):
    r, c = w.shape
    n_piece, _, cp = pieces.shape
    tr = r
    for cand in (256, 176, 128, 64):
        if r % cand == 0 and r > cand:
            tr = cand
            break

    def body(p_ref, w_ref, m_ref, v_ref, g_ref, d_ref, mo_ref, vo_ref):
        g = p_ref[0, :, 0:c].astype(F32)
        for j in range(1, n_piece):
            g = g + p_ref[j, :, 0:c].astype(F32)
        delta, m_new, v_new = _adam_math(g, w_ref[...], m_ref[...], v_ref[...])
        g_ref[...] = g
        d_ref[...] = delta
        mo_ref[...] = m_new
        vo_ref[...] = v_new

    blk = pl.BlockSpec((tr, c), lambda i: (i, 0))
    osh = jax.ShapeDtypeStruct((r, c), F32)
    return pl.pallas_call(
        body, name=name, grid=(r // tr,),
        in_specs=[pl.BlockSpec((n_piece, tr, cp), lambda i: (0, i, 0)), blk, blk, blk],
        out_specs=[blk, blk, blk, blk], out_shape=[osh, osh, osh, osh],
        compiler_params=_params(("parallel",)),
    )(pieces, w, m, v)


def _adam_small(g_all, ws, ms, vs):
    n = len(ws)

    def body(*refs):
        g_ref, ins, outs = refs[0], refs[1:1 + 3 * n], refs[1 + 3 * n:]
        for i in range(n):
            g = g_ref[i:i + 1, :]
            delta, m_new, v_new = _adam_math(g, ins[i][...], ins[n + i][...], ins[2 * n + i][...])
            for kind, val in enumerate((g, delta, m_new, v_new)):
                outs[kind * n + i][...] = val

    osh = jax.ShapeDtypeStruct(ws[0].shape, F32)
    res = pl.pallas_call(body, name="adam_small", in_specs=[VMEM] * (1 + 3 * n), out_specs=[VMEM] * (4 * n),
                         out_shape=[osh] * (4 * n))(g_all, *ws, *ms, *vs)
    return res[:n], res[n:2 * n], res[2 * n:3 * n], res[3 * n:]


def _local_step(x, mem, pos, tgt, gains, w_in_shard, shards, batch):
    g_mix, g_mem_q, g_mem_kv, g_ffn, g_final = gains
    t, d = x.shape
    s = t // batch
    n_mem = mem.shape[0] // batch
    n_sh = N_DEV
    width = shards[0].shape[0]
    nb = width // LANES

    lane = np.arange(LANES) % HEAD_DIM
    sel_lo = (lane < ROPE_HALF).astype(np.float32)[None, :]
    sel_hi = ((lane >= ROPE_HALF) & (lane < 2 * ROPE_HALF)).astype(np.float32)[None, :]
    freqs = np.float32(ROPE_THETA) ** (-np.arange(ROPE_HALF, dtype=np.float32) / np.float32(ROPE_HALF))
    inv_freq = np.where(lane < 2 * ROPE_HALF, freqs[lane % ROPE_HALF], 0.0).astype(np.float32)[None, :]
    cos_t, sin_a, sin_b = _rope_tables(pos, jnp.asarray(inv_freq), jnp.asarray(sel_lo), jnp.asarray(sel_hi))
    bias = _dilated_bias_tiles(s)

    proj, w_in, n1 = _proj_in_gather(x, g_mix, w_in_shard)
    qk_a = _rope_apply("rope_fwd", [proj], 2 * width, cos_t, sin_a, sin_b, 1.0)
    cs_up = shards[0].shape[1]
    (o_a, lse_a), (w_up_a, w_up_b, w_out, w_q, w_kv, w_o, w_fd) = _da_fwd(
        qk_a, proj, 2 * nb, bias, batch, s,
        ride=(shards[:6] + shards[8:], TWO_LEVEL, (cs_up, cs_up, 0, 0, 0, cs_up, 0)))
    (o_b, tot_b), (w_fg, w_fu) = _sb_fwd(proj, 3 * nb, 4 * nb, 5 * nb, batch, s, ride=(shards[6:8], True, (0, 0)))
    w_out = w_out.reshape(d, d)
    w_q = w_q.reshape(d, -1)
    w_kv = w_kv.reshape(d, -1)
    w_fd = w_fd.reshape(-1, d)
    w_fg = w_fg.reshape(-1, d)
    w_fu = w_fu.reshape(-1, d)
    ua, ub, mixed, n2, h1, q_m = _mixer_fwd(o_a, o_b, w_up_a, w_up_b, proj, 6 * nb, w_out, x, g_mem_q, w_q)
    mem_n = _rms_fwd("norm_mem_kv", mem, g_mem_kv)
    kv_m = _mm_w("mem_kv", mem_n, w_kv, BF16)
    o_m = _mem_fwd(q_m, kv_m, batch, s, n_mem)
    h2, n3 = _mm_res_norm("mem_out", o_m, w_o, h1, g_ffn)
    hg, hu, act = _ffn_up(n3, w_fg, w_fu)
    loss_part, dh3, dh3_b, dg_final = _loss_head(act, w_fd, h2, tgt, g_final.reshape(1, d))

    dhg, dhu, dh2, dh2_b, dg_ffn, do_m = _ffn_bwd(dh3_b, w_fd, w_fg, w_fu, hg, hu, h2, g_ffn, dh3, w_o)
    gw_fd = _wgrad("gw_ffn_down", act, dh3_b)
    gw_fg = _wgrad("gw_ffn_gate", dhg, n3)
    gw_fu = _wgrad("gw_ffn_up", dhu, n3)

    gw_o = _wgrad("gw_mem_o", o_m, dh2_b)
    dq_m, dkv_m = _mem_bwd(q_m, kv_m, do_m, batch, s, n_mem)
    gw_q = _wgrad("gw_mem_q", n2, dq_m)
    gw_kv = _wgrad("gw_mem_kv", mem_n, dkv_m)
    (dg_mem_kv,) = _rms_bwd("norm_mem_kv_bwd", (dkv_m, w_kv, NT), mem, g_mem_kv, None, ())
    dh1, dh1_b, dg_mem_q = _rms_bwd("norm_mem_q_bwd", (dq_m, w_q, NT), h1, g_mem_q, dh2, ("f32", "bf16"))

    gw_out = _wgrad("gw_out", mixed, dh1_b)
    dua, dub, dgates, do_a, do_b = _mixer_bwd(dh1_b, w_out, ua, ub, proj, 6 * nb, w_up_a, w_up_b)
    gw_ua = _wgrad("gw_up_a", o_a, dua)
    gw_ub = _wgrad("gw_up_b", o_b, dub)
    (dq_ar, dk_ar, dv_a), (p_fg, p_fd) = _da_bwd(
        qk_a, proj, 2 * nb, bias, o_a, lse_a, do_a, batch, s,
        ride=([gw_fg.reshape(n_sh, -1, d), gw_fd.reshape(n_sh, -1, d)], False, (0, 0)))
    mid = [gw_ua, gw_ub, gw_out.reshape(n_sh, -1, d), gw_q.reshape(n_sh, -1, gw_q.shape[1]),
           gw_kv.reshape(n_sh, -1, gw_kv.shape[1]), gw_o, gw_fu.reshape(n_sh, -1, d)]
    (dq_b, dk_b, dv_b), (*p_mid, p_fu) = _sb_bwd(proj, 3 * nb, 4 * nb, 5 * nb, tot_b, do_b, batch, s,
                                                 ride=(mid, False, (cs_up, cs_up, 0, 0, 0, cs_up, 0)))
    p_ffn = [p_fg, p_fu, p_fd]
    dproj = _rope_apply("rope_bwd", [dq_ar, dk_ar], width, cos_t, sin_a, sin_b, -1.0,
                        tail=(dv_a, dq_b, dk_b, dv_b, dgates))
    grad_x, dg_mix = _rms_bwd("proj_in_bwd", (dproj, w_in, NT), x, g_mix, dh1, ("f32",))
    p_in = _gw_in_scatter(n1, dproj)
    return loss_part, grad_x, [p_in] + list(p_mid) + p_ffn, (dg_mix, dg_mem_q, dg_mem_kv, dg_ffn, dg_final)


WEIGHTS =("w_in", "w_up_a", "w_up_b", "w_out", "w_q_mem", "w_kv_mem", "w_o_mem", "w_ffn_gate", "w_ffn_up", "w_ffn_down")
GAINS = ("g_mix", "g_mem_q", "g_mem_kv", "g_ffn", "g_final")
ORDER = ("g_mix", "w_in", "w_up_a", "w_up_b", "w_out", "g_mem_q", "g_mem_kv", "w_q_mem", "w_kv_mem", "w_o_mem", "g_ffn",
         "w_ffn_gate", "w_ffn_up", "w_ffn_down", "g_final")


def kernel(x, mem, positions, g_mix, w_in, w_up_a, w_up_b, w_out, g_mem_q, g_mem_kv, w_q_mem, w_kv_mem, w_o_mem, g_ffn, w_ffn_gate, w_ffn_up, w_ffn_down, g_final, loss_target, m_g_mix, m_w_in, m_w_up_a, m_w_up_b, m_w_out, m_g_mem_q, m_g_mem_kv, m_w_q_mem, m_w_kv_mem, m_w_o_mem, m_g_ffn, m_w_ffn_gate, m_w_ffn_up, m_w_ffn_down, m_g_final, v_g_mix, v_w_in, v_w_up_a, v_w_up_b, v_w_out, v_g_mem_q, v_g_mem_kv, v_w_q_mem, v_w_kv_mem, v_w_o_mem, v_g_ffn, v_w_ffn_gate, v_w_ffn_up, v_w_ffn_down, v_g_final):
    given = dict(locals())
    batch, s, d = x.shape
    t = batch * s
    flipped = ("w_ffn_gate", "w_ffn_up")

    def view(a, n):
        a = a.reshape(a.shape[-2:])
        return a.T if n in flipped else a

    def unview(a, n):
        return (a.T if n in flipped else a).reshape(given[n].shape)

    shard = {n: view(given[n], n) for n in WEIGHTS}
    gains = [given[n].reshape(1, d) for n in GAINS]

    pad = (-shard["w_ffn_down"].shape[0]) % LANES
    cast = _cast_weights([shard[n] for n in WEIGHTS], [pad if n in flipped + ("w_ffn_down",) else 0 for n in WEIGHTS])
    loss_part, grad_x, pieces, dgains = _local_step(
        x.reshape(t, d), mem.reshape(-1, d), positions.reshape(t, 1), loss_target.reshape(t, d), gains, cast[0],
        cast[1:], batch)

    grad, delta, new_m, new_v = {}, {}, {}, {}
    for n, p in zip(WEIGHTS, pieces):
        outs = _adam("adam_" + n, p, shard[n], view(given["m_" + n], n), view(given["v_" + n], n))
        grad[n], delta[n], new_m[n], new_v[n] = [unview(o, n) for o in outs]

    g_all = _allreduce_small(list(dgains), loss_part)
    small = _adam_small(g_all, gains, [given["m_" + n].reshape(1, d) for n in GAINS],
                        [given["v_" + n].reshape(1, d) for n in GAINS])
    for out, vals in zip((grad, delta, new_m, new_v), small):
        for n, val in zip(GAINS, vals):
            out[n] = val.reshape(given[n].shape)

    loss = g_all[SMALL_ROWS - 1, 0]
    return (loss, grad_x.reshape(x.shape), *[grad[n] for n in ORDER], *[delta[n] for n in ORDER],
            *[new_m[n] for n in ORDER], *[new_v[n] for n in ORDER])
```
